```python
import jax, jax.numpy as jnp
from jax import lax
import numpy as np


D_MODEL = 1024
BATCH = 16
SEQ = 2048
DEPTH = 1

CONV_CH = D_MODEL
CONV_WIDTH = 31
N_HEADS = 16
HEAD_DIM = 64
ATTN_DIM = N_HEADS * HEAD_DIM
DILATION_GROUPS = ((128, 1), (512, 4), (2048, 16))
ATTN_BLOCK = 128
D_FF = 2816
FFN_CONV_WIDTH = 3
EPS = 1e-6
SPLITS = (2 * CONV_CH, 2 * CONV_CH + ATTN_DIM, 2 * CONV_CH + 2 * ATTN_DIM, 2 * CONV_CH + 3 * ATTN_DIM)
IN_COLS = 2 * CONV_CH + 3 * ATTN_DIM + 2 * D_MODEL

kernel_name = "hybrid_conformer_conv_dilated_attn_block"


def rms_norm(x, g):
    xf = x.astype(jnp.float32)
    xf = xf * lax.rsqrt(jnp.mean(xf * xf, axis=-1, keepdims=True) + EPS)
    return (xf * g.astype(jnp.float32)).astype(x.dtype)


def causal_depthwise_conv(u, w, b):
    width, ch = w.shape
    out = lax.conv_general_dilated(
        u, w[:, None, :].astype(u.dtype), window_strides=(1,), padding=[(width - 1, 0)],
        dimension_numbers=('NWC', 'WIO', 'NWC'), feature_group_count=ch)
    return out + b.astype(u.dtype)


def alibi_slopes(n_heads):
    return 2.0 ** (-8.0 * jnp.arange(1, n_heads + 1, dtype=jnp.float32) / n_heads)


def dilated_window_attention(q, k, v, slopes, window, dilation):
    B, S, H, Dh = q.shape
    n_back = window // dilation
    L = S // dilation
    nb = -(-L // ATTN_BLOCK)
    Lp = nb * ATTN_BLOCK

    def to_sub(t):
        t = t.reshape(B, L, dilation, H, Dh).transpose(0, 2, 1, 3, 4).reshape(B * dilation, L, H, Dh)
        t = jnp.pad(t, ((0, 0), (0, Lp - L), (0, 0), (0, 0)))
        return t.reshape(B * dilation, nb, ATTN_BLOCK, H, Dh)

    def with_prev(t):
        prev = jnp.pad(t, ((0, 0), (1, 0), (0, 0), (0, 0), (0, 0)))[:, :-1]
        return jnp.concatenate([prev, t], axis=2)

    qb = to_sub(q)
    kc = with_prev(to_sub(k))
    vc = with_prev(to_sub(v))

    scores = jnp.einsum('nbqhd,nbkhd->nbhqk', qb, kc)
    steps = (jnp.arange(ATTN_BLOCK)[:, None] + ATTN_BLOCK) - jnp.arange(2 * ATTN_BLOCK)[None, :]
    valid = (steps >= 0) & (steps <= n_back)
    first_block = (jnp.arange(nb) == 0)[:, None, None]
    prev_cols = (jnp.arange(2 * ATTN_BLOCK) < ATTN_BLOCK)[None, None, :]
    valid = valid[None] & ~(first_block & prev_cols)
    dist = (steps * dilation).astype(jnp.float32)
    scores = scores - slopes[:, None, None] * dist
    scores = jnp.where(valid[None, :, None], scores, -jnp.inf)
    m = jnp.max(scores, axis=-1, keepdims=True)
    p = jnp.exp(scores - m)
    den = jnp.sum(p, axis=-1)
    o = jnp.einsum('nbhqk,nbkhd->nbqhd', p, vc) / jnp.swapaxes(den, 2, 3)[..., None]
    lse = jnp.swapaxes(m[..., 0] + jnp.log(den), 2, 3)

    def from_sub(t):
        rest = t.shape[3:]
        t = t.reshape(B, dilation, Lp, *rest)[:, :, :L]
        return jnp.swapaxes(t, 1, 2).reshape(B, S, *rest)

    return from_sub(o), from_sub(lse)


def _fwd_setup_inputs(seed: int = 0) -> dict:
    key = jax.random.key(seed)
    ks = jax.random.split(key, 18)
    f32 = jnp.float32

    def nrm(k, shape, scale):
        return jax.random.normal(k, shape, f32) * scale

    return {
        'x': nrm(ks[0], (BATCH, SEQ, D_MODEL), 1.0),
        'norm1_g': 1.0 + nrm(ks[1], (DEPTH, D_MODEL), 0.1),
        'w_in': nrm(ks[2], (DEPTH, D_MODEL, IN_COLS), D_MODEL ** -0.5),
        'gate_b': nrm(ks[3], (DEPTH, 2 * D_MODEL), 0.02),
        'conv_w': nrm(ks[4], (DEPTH, CONV_WIDTH, CONV_CH), CONV_WIDTH ** -0.5),
        'conv_b': nrm(ks[5], (DEPTH, CONV_CH), 0.02),
        'conv_norm_g': 1.0 + nrm(ks[6], (DEPTH, CONV_CH), 0.1),
        'w_conv_out': nrm(ks[7], (DEPTH, CONV_CH, D_MODEL), CONV_CH ** -0.5),
        'q_norm_g': 1.0 + nrm(ks[8], (DEPTH, HEAD_DIM), 0.1),
        'k_norm_g': 1.0 + nrm(ks[9], (DEPTH, HEAD_DIM), 0.1),
        'w_attn_out': nrm(ks[10], (DEPTH, ATTN_DIM, D_MODEL), ATTN_DIM ** -0.5),
        'w_out': nrm(ks[11], (DEPTH, D_MODEL, D_MODEL), D_MODEL ** -0.5),
        'norm2_g': 1.0 + nrm(ks[12], (DEPTH, D_MODEL), 0.1),
        'w_up': nrm(ks[13], (DEPTH, D_MODEL, 2 * D_FF), D_MODEL ** -0.5),
        'ffn_conv_w': nrm(ks[14], (DEPTH, FFN_CONV_WIDTH, 2 * D_FF), FFN_CONV_WIDTH ** -0.5),
        'ffn_conv_b': nrm(ks[15], (DEPTH, 2 * D_FF), 0.02),
        'w_down': nrm(ks[16], (DEPTH, D_FF, D_MODEL), D_FF ** -0.5),
    }


def _fwd_reference(x, norm1_g, w_in, gate_b, conv_w, conv_b, conv_norm_g, w_conv_out,
              q_norm_g, k_norm_g, w_attn_out, w_out, norm2_g, w_up, ffn_conv_w,
              ffn_conv_b, w_down):
    B, S, _ = x.shape
    slopes = alibi_slopes(N_HEADS)
    for l in range(DEPTH):
        h = rms_norm(x, norm1_g[l])
        z = h @ w_in[l].astype(h.dtype)
        glu_in, q, k, v, gate_logits = jnp.split(z, SPLITS, axis=-1)

        a_val, a_gate = jnp.split(glu_in, 2, axis=-1)
        a = a_val * jax.nn.sigmoid(a_gate)
        a = causal_depthwise_conv(a, conv_w[l], conv_b[l])
        a = jax.nn.silu(rms_norm(a, conv_norm_g[l]))
        y_a = a @ w_conv_out[l].astype(a.dtype)

        q = rms_norm(q.reshape(B, S, N_HEADS, HEAD_DIM), q_norm_g[l]).astype(jnp.float32) * HEAD_DIM ** -0.5
        k = rms_norm(k.reshape(B, S, N_HEADS, HEAD_DIM), k_norm_g[l]).astype(jnp.float32)
        v = v.reshape(B, S, N_HEADS, HEAD_DIM).astype(jnp.float32)
        outs, lses = [], []
        for window, dilation in DILATION_GROUPS:
            o_g, lse_g = dilated_window_attention(q, k, v, slopes, window, dilation)
            outs.append(o_g)
            lses.append(lse_g)
        mix = jax.nn.softmax(jnp.stack(lses), axis=0)
        o = jnp.einsum('gbsh,gbshd->bshd', mix, jnp.stack(outs))
        o = o.astype(x.dtype).reshape(B, S, ATTN_DIM)
        y_b = o @ w_attn_out[l].astype(o.dtype)

        g = jax.nn.sigmoid(gate_logits + gate_b[l].astype(gate_logits.dtype))
        g_a, g_b = jnp.split(g, 2, axis=-1)
        x = x + (g_a * y_a + g_b * y_b) @ w_out[l].astype(x.dtype)

        h = rms_norm(x, norm2_g[l])
        u = causal_depthwise_conv(h @ w_up[l].astype(h.dtype), ffn_conv_w[l], ffn_conv_b[l])
        u_val, u_gate = jnp.split(u, 2, axis=-1)
        x = x + (jax.nn.silu(u_gate) * u_val) @ w_down[l].astype(x.dtype)
    return x


import jax as _jax
import jax.numpy as _jnp

TWIN_FORMAT = 'train_step'
FWD_PARAMS = ['x', 'norm1_g', 'w_in', 'gate_b', 'conv_w', 'conv_b', 'conv_norm_g', 'w_conv_out', 'q_norm_g', 'k_norm_g', 'w_attn_out', 'w_out', 'norm2_g', 'w_up', 'ffn_conv_w', 'ffn_conv_b', 'w_down']
TWIN_WEIGHTS = ['norm1_g', 'w_in', 'gate_b', 'conv_w', 'conv_b', 'conv_norm_g', 'w_conv_out', 'q_norm_g', 'k_norm_g', 'w_attn_out', 'w_out', 'norm2_g', 'w_up', 'ffn_conv_w', 'ffn_conv_b', 'w_down']
TWIN_DIFF_INPUT = 'x'
TWIN_INPUTS = ['x', 'norm1_g', 'w_in', 'gate_b', 'conv_w', 'conv_b', 'conv_norm_g', 'w_conv_out', 'q_norm_g', 'k_norm_g', 'w_attn_out', 'w_out', 'norm2_g', 'w_up', 'ffn_conv_w', 'ffn_conv_b', 'w_down', 'loss_target', 'm_norm1_g', 'm_w_in', 'm_gate_b', 'm_conv_w', 'm_conv_b', 'm_conv_norm_g', 'm_w_conv_out', 'm_q_norm_g', 'm_k_norm_g', 'm_w_attn_out', 'm_w_out', 'm_norm2_g', 'm_w_up', 'm_ffn_conv_w', 'm_ffn_conv_b', 'm_w_down', 'v_norm1_g', 'v_w_in', 'v_gate_b', 'v_conv_w', 'v_conv_b', 'v_conv_norm_g', 'v_w_conv_out', 'v_q_norm_g', 'v_k_norm_g', 'v_w_attn_out', 'v_w_out', 'v_norm2_g', 'v_w_up', 'v_ffn_conv_w', 'v_ffn_conv_b', 'v_w_down']
TWIN_OUTPUTS = ['loss', 'grad_x', 'grad_norm1_g', 'grad_w_in', 'grad_gate_b', 'grad_conv_w', 'grad_conv_b', 'grad_conv_norm_g', 'grad_w_conv_out', 'grad_q_norm_g', 'grad_k_norm_g', 'grad_w_attn_out', 'grad_w_out', 'grad_norm2_g', 'grad_w_up', 'grad_ffn_conv_w', 'grad_ffn_conv_b', 'grad_w_down', 'delta_norm1_g', 'delta_w_in', 'delta_gate_b', 'delta_conv_w', 'delta_conv_b', 'delta_conv_norm_g', 'delta_w_conv_out', 'delta_q_norm_g', 'delta_k_norm_g', 'delta_w_attn_out', 'delta_w_out', 'delta_norm2_g', 'delta_w_up', 'delta_ffn_conv_w', 'delta_ffn_conv_b', 'delta_w_down', 'new_m_norm1_g', 'new_m_w_in', 'new_m_gate_b', 'new_m_conv_w', 'new_m_conv_b', 'new_m_conv_norm_g', 'new_m_w_conv_out', 'new_m_q_norm_g', 'new_m_k_norm_g', 'new_m_w_attn_out', 'new_m_w_out', 'new_m_norm2_g', 'new_m_w_up', 'new_m_ffn_conv_w', 'new_m_ffn_conv_b', 'new_m_w_down', 'new_v_norm1_g', 'new_v_w_in', 'new_v_gate_b', 'new_v_conv_w', 'new_v_conv_b', 'new_v_conv_norm_g', 'new_v_w_conv_out', 'new_v_q_norm_g', 'new_v_k_norm_g', 'new_v_w_attn_out', 'new_v_w_out', 'new_v_norm2_g', 'new_v_w_up', 'new_v_ffn_conv_w', 'new_v_ffn_conv_b', 'new_v_w_down']
TWIN_LEAF_KINDS = {'loss': 'loss', 'grad_x': 'grad_x', 'grad_norm1_g': 'grad_w', 'grad_w_in': 'grad_w', 'grad_gate_b': 'grad_w', 'grad_conv_w': 'grad_w', 'grad_conv_b': 'grad_w', 'grad_conv_norm_g': 'grad_w', 'grad_w_conv_out': 'grad_w', 'grad_q_norm_g': 'grad_w', 'grad_k_norm_g': 'grad_w', 'grad_w_attn_out': 'grad_w', 'grad_w_out': 'grad_w', 'grad_norm2_g': 'grad_w', 'grad_w_up': 'grad_w', 'grad_ffn_conv_w': 'grad_w', 'grad_ffn_conv_b': 'grad_w', 'grad_w_down': 'grad_w', 'delta_norm1_g': 'delta_w', 'delta_w_in': 'delta_w', 'delta_gate_b': 'delta_w', 'delta_conv_w': 'delta_w', 'delta_conv_b': 'delta_w', 'delta_conv_norm_g': 'delta_w', 'delta_w_conv_out': 'delta_w', 'delta_q_norm_g': 'delta_w', 'delta_k_norm_g': 'delta_w', 'delta_w_attn_out': 'delta_w', 'delta_w_out': 'delta_w', 'delta_norm2_g': 'delta_w', 'delta_w_up': 'delta_w', 'delta_ffn_conv_w': 'delta_w', 'delta_ffn_conv_b': 'delta_w', 'delta_w_down': 'delta_w', 'new_m_norm1_g': 'new_m', 'new_m_w_in': 'new_m', 'new_m_gate_b': 'new_m', 'new_m_conv_w': 'new_m', 'new_m_conv_b': 'new_m', 'new_m_conv_norm_g': 'new_m', 'new_m_w_conv_out': 'new_m', 'new_m_q_norm_g': 'new_m', 'new_m_k_norm_g': 'new_m', 'new_m_w_attn_out': 'new_m', 'new_m_w_out': 'new_m', 'new_m_norm2_g': 'new_m', 'new_m_w_up': 'new_m', 'new_m_ffn_conv_w': 'new_m', 'new_m_ffn_conv_b': 'new_m', 'new_m_w_down': 'new_m', 'new_v_norm1_g': 'new_v', 'new_v_w_in': 'new_v', 'new_v_gate_b': 'new_v', 'new_v_conv_w': 'new_v', 'new_v_conv_b': 'new_v', 'new_v_conv_norm_g': 'new_v', 'new_v_w_conv_out': 'new_v', 'new_v_q_norm_g': 'new_v', 'new_v_k_norm_g': 'new_v', 'new_v_w_attn_out': 'new_v', 'new_v_w_out': 'new_v', 'new_v_norm2_g': 'new_v', 'new_v_w_up': 'new_v', 'new_v_ffn_conv_w': 'new_v', 'new_v_ffn_conv_b': 'new_v', 'new_v_w_down': 'new_v'}


def _forward(args):
    return _fwd_reference(*[args[k] for k in FWD_PARAMS])


def _output_shape():
    out = _jax.eval_shape(lambda: _forward(_fwd_setup_inputs(0)))
    return out.shape, out.dtype

N_MICROBATCH = 1
ADAM_LR = 0.001
ADAM_B1 = 0.9
ADAM_B2 = 0.999
ADAM_EPS = 1e-08
ADAM_WD = 0.01
ADAM_STEP = 10
PER_EXAMPLE_BATCH_AXIS = {'x': 0, 'loss_target': 0}
SHARED_INPUTS = []
_WEIGHT_DTYPES = {'norm1_g': _jnp.float32, 'w_in': _jnp.float32, 'gate_b': _jnp.float32, 'conv_w': _jnp.float32, 'conv_b': _jnp.float32, 'conv_norm_g': _jnp.float32, 'w_conv_out': _jnp.float32, 'q_norm_g': _jnp.float32, 'k_norm_g': _jnp.float32, 'w_attn_out': _jnp.float32, 'w_out': _jnp.float32, 'norm2_g': _jnp.float32, 'w_up': _jnp.float32, 'ffn_conv_w': _jnp.float32, 'ffn_conv_b': _jnp.float32, 'w_down': _jnp.float32}
MOMENT_SCALE = {'norm1_g': 1.594994e+00, 'w_in': 1.019534e-01, 'gate_b': 9.819763e-01, 'conv_w': 3.202315e-01, 'conv_b': 7.156704e+00, 'conv_norm_g': 4.689314e+00, 'w_conv_out': 1.200811e+00, 'q_norm_g': 3.526136e+00, 'k_norm_g': 3.490375e+00, 'w_attn_out': 1.447376e-01, 'w_out': 1.095522e+00, 'norm2_g': 2.616016e+01, 'w_up': 4.288787e-01, 'ffn_conv_w': 3.777185e+00, 'ffn_conv_b': 3.318418e+00, 'w_down': 3.334147e-01}


def _to_microbatches(a, axis):
    t = _jnp.moveaxis(a, axis, 0)
    t = t.reshape((N_MICROBATCH, t.shape[0] // N_MICROBATCH) + t.shape[1:])
    return _jnp.moveaxis(t, 1, axis + 1)


def setup_inputs(seed: int = 0) -> dict:
    inp = _fwd_setup_inputs(seed)
    key = _jax.random.fold_in(_jax.random.key(seed), 7919)
    shape, _ = _output_shape()
    out = dict(inp)
    out["loss_target"] = _jax.random.normal(_jax.random.fold_in(key, 0), shape, _jnp.float32)
    for i, name in enumerate(TWIN_WEIGHTS):
        w = inp[name].astype(_jnp.float32)
        if MOMENT_SCALE is None:
            s = _jnp.sqrt(_jnp.mean(_jnp.square(w)) + 1e-30)
        else:
            s = MOMENT_SCALE[name]
        km, kv = _jax.random.split(_jax.random.fold_in(key, i + 1))
        out[name] = w
        out["m_" + name] = s * _jax.random.normal(km, w.shape, _jnp.float32)
        out["v_" + name] = (s * s) * _jax.random.uniform(kv, w.shape, _jnp.float32, 0.5, 1.5)
    if N_MICROBATCH > 1:
        for name, axis in PER_EXAMPLE_BATCH_AXIS.items():
            out[name] = _to_microbatches(out[name], axis)
    return {'x': out['x'], 'norm1_g': out['norm1_g'], 'w_in': out['w_in'], 'gate_b': out['gate_b'], 'conv_w': out['conv_w'], 'conv_b': out['conv_b'], 'conv_norm_g': out['conv_norm_g'], 'w_conv_out': out['w_conv_out'], 'q_norm_g': out['q_norm_g'], 'k_norm_g': out['k_norm_g'], 'w_attn_out': out['w_attn_out'], 'w_out': out['w_out'], 'norm2_g': out['norm2_g'], 'w_up': out['w_up'], 'ffn_conv_w': out['ffn_conv_w'], 'ffn_conv_b': out['ffn_conv_b'], 'w_down': out['w_down'], 'loss_target': out['loss_target'], 'm_norm1_g': out['m_norm1_g'], 'm_w_in': out['m_w_in'], 'm_gate_b': out['m_gate_b'], 'm_conv_w': out['m_conv_w'], 'm_conv_b': out['m_conv_b'], 'm_conv_norm_g': out['m_conv_norm_g'], 'm_w_conv_out': out['m_w_conv_out'], 'm_q_norm_g': out['m_q_norm_g'], 'm_k_norm_g': out['m_k_norm_g'], 'm_w_attn_out': out['m_w_attn_out'], 'm_w_out': out['m_w_out'], 'm_norm2_g': out['m_norm2_g'], 'm_w_up': out['m_w_up'], 'm_ffn_conv_w': out['m_ffn_conv_w'], 'm_ffn_conv_b': out['m_ffn_conv_b'], 'm_w_down': out['m_w_down'], 'v_norm1_g': out['v_norm1_g'], 'v_w_in': out['v_w_in'], 'v_gate_b': out['v_gate_b'], 'v_conv_w': out['v_conv_w'], 'v_conv_b': out['v_conv_b'], 'v_conv_norm_g': out['v_conv_norm_g'], 'v_w_conv_out': out['v_w_conv_out'], 'v_q_norm_g': out['v_q_norm_g'], 'v_k_norm_g': out['v_k_norm_g'], 'v_w_attn_out': out['v_w_attn_out'], 'v_w_out': out['v_w_out'], 'v_norm2_g': out['v_norm2_g'], 'v_w_up': out['v_w_up'], 'v_ffn_conv_w': out['v_ffn_conv_w'], 'v_ffn_conv_b': out['v_ffn_conv_b'], 'v_w_down': out['v_w_down']}


def _loss(weights, diff, rest, loss_target):
    with _jax.named_scope("forward"):
        args = {**rest, TWIN_DIFF_INPUT: diff, **{k: w.astype(_WEIGHT_DTYPES[k]) for k, w in weights.items()}}
        y = _forward(args)
    with _jax.named_scope("loss_head"):
        err = _jnp.square(y.astype(_jnp.float32) - loss_target)
        return 0.5 * _jnp.sum(_jnp.mean(err, axis=-1)) if err.ndim else 0.5 * err


def _adamw(w, g, m, v):
    m = ADAM_B1 * m + (1.0 - ADAM_B1) * g
    v = ADAM_B2 * v + (1.0 - ADAM_B2) * _jnp.square(g)
    m_hat = m / (1.0 - ADAM_B1 ** ADAM_STEP)
    v_hat = v / (1.0 - ADAM_B2 ** ADAM_STEP)
    delta = -ADAM_LR * (m_hat / (_jnp.sqrt(v_hat) + ADAM_EPS) + ADAM_WD * w)
    return delta, m, v


def reference(x, norm1_g, w_in, gate_b, conv_w, conv_b, conv_norm_g, w_conv_out, q_norm_g, k_norm_g, w_attn_out, w_out, norm2_g, w_up, ffn_conv_w, ffn_conv_b, w_down, loss_target, m_norm1_g, m_w_in, m_gate_b, m_conv_w, m_conv_b, m_conv_norm_g, m_w_conv_out, m_q_norm_g, m_k_norm_g, m_w_attn_out, m_w_out, m_norm2_g, m_w_up, m_ffn_conv_w, m_ffn_conv_b, m_w_down, v_norm1_g, v_w_in, v_gate_b, v_conv_w, v_conv_b, v_conv_norm_g, v_w_conv_out, v_q_norm_g, v_k_norm_g, v_w_attn_out, v_w_out, v_norm2_g, v_w_up, v_ffn_conv_w, v_ffn_conv_b, v_w_down):
    given = dict(x=x, norm1_g=norm1_g, w_in=w_in, gate_b=gate_b, conv_w=conv_w, conv_b=conv_b, conv_norm_g=conv_norm_g, w_conv_out=w_conv_out, q_norm_g=q_norm_g, k_norm_g=k_norm_g, w_attn_out=w_attn_out, w_out=w_out, norm2_g=norm2_g, w_up=w_up, ffn_conv_w=ffn_conv_w, ffn_conv_b=ffn_conv_b, w_down=w_down, loss_target=loss_target, m_norm1_g=m_norm1_g, m_w_in=m_w_in, m_gate_b=m_gate_b, m_conv_w=m_conv_w, m_conv_b=m_conv_b, m_conv_norm_g=m_conv_norm_g, m_w_conv_out=m_w_conv_out, m_q_norm_g=m_q_norm_g, m_k_norm_g=m_k_norm_g, m_w_attn_out=m_w_attn_out, m_w_out=m_w_out, m_norm2_g=m_norm2_g, m_w_up=m_w_up, m_ffn_conv_w=m_ffn_conv_w, m_ffn_conv_b=m_ffn_conv_b, m_w_down=m_w_down, v_norm1_g=v_norm1_g, v_w_in=v_w_in, v_gate_b=v_gate_b, v_conv_w=v_conv_w, v_conv_b=v_conv_b, v_conv_norm_g=v_conv_norm_g, v_w_conv_out=v_w_conv_out, v_q_norm_g=v_q_norm_g, v_k_norm_g=v_k_norm_g, v_w_attn_out=v_w_attn_out, v_w_out=v_w_out, v_norm2_g=v_norm2_g, v_w_up=v_w_up, v_ffn_conv_w=v_ffn_conv_w, v_ffn_conv_b=v_ffn_conv_b, v_w_down=v_w_down)
    weights = {n: given[n] for n in TWIN_WEIGHTS}
    shared = {n: given[n] for n in SHARED_INPUTS}
    per_example = {n: given[n] for n in ['x']}
    grad_fn = _jax.value_and_grad(_loss, argnums=(0, 1))

    def one_microbatch(ex, loss_target):
        ex = dict(ex)
        diff = ex.pop(TWIN_DIFF_INPUT)
        return grad_fn(weights, diff, {**shared, **ex}, loss_target)

    if N_MICROBATCH == 1:
        loss, (grad_w, grad_x) = one_microbatch(per_example, given["loss_target"])
    else:
        def body(carry, xs):
            loss_sum, grad_sum = carry
            l_k, (gw_k, gx_k) = one_microbatch(xs[0], xs[1])
            with _jax.named_scope("update"):
                return (loss_sum + l_k, _jax.tree.map(_jnp.add, grad_sum, gw_k)), gx_k

        init = (_jnp.zeros((), _jnp.float32), _jax.tree.map(_jnp.zeros_like, weights))
        (loss, grad_w), grad_x = _jax.lax.scan(body, init, (per_example, given["loss_target"]))
    with _jax.named_scope("update"):
        delta_w, new_m, new_v = {}, {}, {}
        for n in TWIN_WEIGHTS:
            delta_w[n], new_m[n], new_v[n] = _adamw(weights[n], grad_w[n], given["m_" + n], given["v_" + n])
    return (loss, grad_x, *[grad_w[n] for n in TWIN_WEIGHTS], *[delta_w[n] for n in TWIN_WEIGHTS],
            *[new_m[n] for n in TWIN_WEIGHTS], *[new_v[n] for n in TWIN_WEIGHTS])
```

```python
import functools

import jax
import jax.numpy as jnp
from jax import lax
from jax.experimental import pallas as pl
from jax.experimental.pallas import tpu as pltpu

f32 = jnp.float32
bf16 = jnp.bfloat16

D = 1024
N_HEADS = 16
HEAD_DIM = 64
CONV_WIDTH = 31
D_FF = 2816
GROUPS = ((128, 1), (512, 4), (2048, 16))
ATTN_BLOCK = 128
EPS = 1e-6
N_DEV = 8
MESH = pl.DeviceIdType.MESH

ADAM_LR = 0.001
ADAM_B1 = 0.9
ADAM_B2 = 0.999
ADAM_EPS = 1e-08
ADAM_WD = 0.01
ADAM_STEP = 10

VMEM_LIMIT = 56 * 1024 * 1024
MASK_BIAS = 1e30

Z_AVAL, Z_AGATE, Z_GA, Z_GB, Z_Q, Z_K, Z_V = 0, 1, 2, 3, 4, 5, 6


def _wsec_of_zsec(j):
    return jnp.where(j < 2, j, jnp.where(j < 4, j + 3, j - 2))


def _zsec_of_wsec(w):
    return jnp.where(w < 2, w, jnp.where(w < 5, w + 2, w - 3))


def _sig(x):
    return 1.0 / (1.0 + jnp.exp(-x))


def _colsum8(x):
    return x.reshape(-1, 8, x.shape[-1]).sum(axis=0)


def _cparams(sem):
    return pltpu.CompilerParams(dimension_semantics=sem, vmem_limit_bytes=VMEM_LIMIT)


def _my_pos():
    x, y, c = lax.axis_index("x"), lax.axis_index("y"), lax.axis_index("c")
    return x, y, c, 4 * x + 2 * y + c


_DIMS = {"nn": ((1,), (0,)), "nt": ((1,), (1,)), "tn": ((0,), (0,))}


def _matmul_call(name, a, b, a_spec, b_spec, o_spec, out_shape, grid, mode, nk, tm, tn):
    dims = (_DIMS[mode], ((), ()))

    def body(a_ref, b_ref, o_ref, *scratch):
        part = lax.dot_general(a_ref[...], b_ref[...], dims, preferred_element_type=f32)
        if nk == 1:
            o_ref[...] = part.astype(o_ref.dtype)
        else:
            acc = scratch[0]
            k = pl.program_id(2)

            @pl.when(k == 0)
            def _():
                acc[...] = part

            @pl.when(k > 0)
            def _():
                acc[...] += part

            @pl.when(k == nk - 1)
            def _():
                o_ref[...] = acc[...].astype(o_ref.dtype)

    scratch = [] if nk == 1 else [pltpu.VMEM((tm, tn), f32)]
    return pl.pallas_call(
        body, name=name, grid=grid, in_specs=[a_spec, b_spec], out_specs=o_spec, out_shape=out_shape,
        scratch_shapes=scratch, compiler_params=_cparams(("parallel", "parallel", "arbitrary")),
    )(a, b)


def _matmul(name, a, b, mode, out_dtype, tm=1024, tn=1024, tk=None):
    if mode == "nn":
        (M, K), (_, N) = a.shape, b.shape
    elif mode == "nt":
        (M, K), (N, _) = a.shape, b.shape
    else:
        (K, M), (_, N) = a.shape, b.shape
    tm, tn = min(tm, M), min(tn, N)
    tk = K if tk is None else tk
    nk = K // tk
    assert M % tm == 0 and N % tn == 0 and K % tk == 0
    if mode == "tn":
        a_spec = pl.BlockSpec((tk, tm), lambda i, j, k: (k, i))
    else:
        a_spec = pl.BlockSpec((tm, tk), lambda i, j, k: (i, k))
    if mode == "nt":
        b_spec = pl.BlockSpec((tn, tk), lambda i, j, k: (j, k))
    else:
        b_spec = pl.BlockSpec((tk, tn), lambda i, j, k: (k, j))
    o_spec = pl.BlockSpec((tm, tn), lambda i, j, k: (i, j))
    return _matmul_call(name, a, b, a_spec, b_spec, o_spec, jax.ShapeDtypeStruct((M, N), out_dtype),
                        (M // tm, N // tn, nk), mode, nk, tm, tn)


TT = 512


def _rows(c, cb=0, tt=TT):
    return pl.BlockSpec((tt, c), lambda i: (i, cb))


def _sec(s, tt=TT):
    return pl.BlockSpec((None, tt, D), lambda i: (s, i, 0))


def _const(shape):
    return pl.BlockSpec(shape, lambda i: (0,) * len(shape))


def _acc_spec(c):
    return pl.BlockSpec((8, c), lambda i: (0, 0))


def _rms(x):
    return lax.rsqrt(jnp.mean(x * x, axis=-1, keepdims=True) + EPS)


def _rms_bwd(dy_g, xn, rstd):
    return rstd * (dy_g - xn * jnp.mean(dy_g * xn, axis=-1, keepdims=True))


def _head_sum(x, bd):
    parts = []
    for cb in range(x.shape[-1] // 128):
        xb = x[:, cb * 128:(cb + 1) * 128]
        hi = xb.astype(bf16)
        lo = (xb - hi.astype(f32)).astype(bf16)
        parts.append(jnp.dot(hi, bd, preferred_element_type=f32) + jnp.dot(lo, bd, preferred_element_type=f32))
    return parts[0] if len(parts) == 1 else jnp.concatenate(parts, axis=1)


def _norm1_fwd(x, g):
    T = x.shape[0]

    def body(x_ref, g_ref, h_ref):
        xv = x_ref[...]
        h_ref[...] = (xv * _rms(xv) * g_ref[...]).astype(bf16)

    return pl.pallas_call(
        body, name="norm1_fwd", grid=(T // TT,), in_specs=[_rows(D), _const((1, D))], out_specs=_rows(D),
        out_shape=jax.ShapeDtypeStruct((T, D), bf16), compiler_params=_cparams(("parallel",)))(x, g)


def _convnorm_fwd(c, g):
    T = c.shape[0]

    def body(c_ref, g_ref, s_ref):
        cv = c_ref[...]
        r = cv * _rms(cv) * g_ref[...]
        s_ref[...] = (r * _sig(r)).astype(bf16)

    return pl.pallas_call(
        body, name="convnorm_fwd", grid=(T // TT,), in_specs=[_rows(D), _const((1, D))], out_specs=_rows(D),
        out_shape=jax.ShapeDtypeStruct((T, D), bf16), compiler_params=_cparams(("parallel",)))(c, g)


def _qk_fwd(z8, qg, kg, bd):
    T = z8.shape[1]

    def body(q_ref, k_ref, qg_ref, kg_ref, bd_ref, qn_ref, kn_ref):
        bdv = bd_ref[...]
        q = q_ref[...]
        qn_ref[...] = q * lax.rsqrt(_head_sum(q * q, bdv) * (1.0 / HEAD_DIM) + EPS) * qg_ref[...] * (HEAD_DIM ** -0.5)
        k = k_ref[...]
        kn_ref[...] = k * lax.rsqrt(_head_sum(k * k, bdv) * (1.0 / HEAD_DIM) + EPS) * kg_ref[...]

    return pl.pallas_call(
        body, name="qk_fwd", grid=(T // TT,),
        in_specs=[_sec(Z_Q), _sec(Z_K), _const((1, D)), _const((1, D)), _const((128, 128))],
        out_specs=[_rows(D), _rows(D)],
        out_shape=[jax.ShapeDtypeStruct((T, D), f32)] * 2, compiler_params=_cparams(("parallel",)))(z8, z8, qg, kg, bd)


def _gate_fwd(z8, gate_b, ya, yb):
    T = ya.shape[0]

    def body(ga_ref, gb_ref, b_ref, ya_ref, yb_ref, mixed_ref):
        g_a = _sig(ga_ref[...] + b_ref[:, :D])
        g_b = _sig(gb_ref[...] + b_ref[:, D:])
        mixed_ref[...] = (g_a * ya_ref[...] + g_b * yb_ref[...]).astype(bf16)

    return pl.pallas_call(
        body, name="gate_fwd", grid=(T // TT,),
        in_specs=[_sec(Z_GA), _sec(Z_GB), _const((1, 2 * D)), _rows(D), _rows(D)], out_specs=_rows(D),
        out_shape=jax.ShapeDtypeStruct((T, D), bf16), compiler_params=_cparams(("parallel",)))(z8, z8, gate_b, ya, yb)


def _norm2_fwd(x, t1, g):
    T = x.shape[0]

    def body(x_ref, t_ref, g_ref, x1_ref, h2_ref):
        x1 = x_ref[...] + t_ref[...]
        x1_ref[...] = x1
        h2_ref[...] = (x1 * _rms(x1) * g_ref[...]).astype(bf16)

    return pl.pallas_call(
        body, name="norm2_fwd", grid=(T // TT,), in_specs=[_rows(D), _rows(D), _const((1, D))],
        out_specs=[_rows(D), _rows(D)],
        out_shape=[jax.ShapeDtypeStruct((T, D), f32), jax.ShapeDtypeStruct((T, D), bf16)],
        compiler_params=_cparams(("parallel",)))(x, t1, g)


def _loss_fwd(x1, t2, target):
    T = x1.shape[0]

    def body(x1_ref, t_ref, tg_ref, dy_ref, dyb_ref, acc_ref):
        diff = x1_ref[...] + t_ref[...] - tg_ref[...]
        dy = diff * (1.0 / D)
        dy_ref[...] = dy
        dyb_ref[...] = dy.astype(bf16)

        @pl.when(pl.program_id(0) == 0)
        def _():
            acc_ref[...] = jnp.zeros_like(acc_ref)

        acc_ref[...] += _colsum8(diff * diff)

    return pl.pallas_call(
        body, name="loss_fwd", grid=(T // TT,), in_specs=[_rows(D)] * 3,
        out_specs=[_rows(D), _rows(D), _acc_spec(D)],
        out_shape=[jax.ShapeDtypeStruct((T, D), f32), jax.ShapeDtypeStruct((T, D), bf16),
                   jax.ShapeDtypeStruct((8, D), f32)],
        compiler_params=_cparams(("arbitrary",)))(x1, t2, target)


def _norm2_bwd(x1, dh2, dy, g):
    T = x1.shape[0]

    def body(x1_ref, dh_ref, dy_ref, g_ref, dx1_ref, dx1b_ref, dg_ref):
        x1 = x1_ref[...]
        rstd = _rms(x1)
        xn = x1 * rstd
        dh = dh_ref[...]
        dx1 = dy_ref[...] + _rms_bwd(dh * g_ref[...], xn, rstd)
        dx1_ref[...] = dx1
        dx1b_ref[...] = dx1.astype(bf16)

        @pl.when(pl.program_id(0) == 0)
        def _():
            dg_ref[...] = jnp.zeros_like(dg_ref)

        dg_ref[...] += _colsum8(dh * xn)

    return pl.pallas_call(
        body, name="norm2_bwd", grid=(T // TT,), in_specs=[_rows(D), _rows(D), _rows(D), _const((1, D))],
        out_specs=[_rows(D), _rows(D), _acc_spec(D)],
        out_shape=[jax.ShapeDtypeStruct((T, D), f32), jax.ShapeDtypeStruct((T, D), bf16),
                   jax.ShapeDtypeStruct((8, D), f32)],
        compiler_params=_cparams(("arbitrary",)))(x1, dh2, dy, g)


def _gate_bwd(dmixed, z8, gate_b, ya, yb, dz8):
    T = ya.shape[0]

    def body(dm_ref, ga_ref, gb_ref, b_ref, ya_ref, yb_ref, dz_in, dya_ref, dyb_ref, dz_ref, dgb_ref):
        del dz_in
        dm = dm_ref[...]
        g_a = _sig(ga_ref[...] + b_ref[:, :D])
        g_b = _sig(gb_ref[...] + b_ref[:, D:])
        dya_ref[...] = (dm * g_a).astype(bf16)
        dyb_ref[...] = (dm * g_b).astype(bf16)
        dla = dm * ya_ref[...] * g_a * (1.0 - g_a)
        dlb = dm * yb_ref[...] * g_b * (1.0 - g_b)
        dz_ref[0] = dla.astype(bf16)
        dz_ref[1] = dlb.astype(bf16)

        @pl.when(pl.program_id(0) == 0)
        def _():
            dgb_ref[...] = jnp.zeros_like(dgb_ref)

        dgb_ref[:, :D] += _colsum8(dla)
        dgb_ref[:, D:] += _colsum8(dlb)

    return pl.pallas_call(
        body, name="gate_bwd", grid=(T // TT,),
        in_specs=[_rows(D), _sec(Z_GA), _sec(Z_GB), _const((1, 2 * D)), _rows(D), _rows(D),
                  pl.BlockSpec(memory_space=pl.ANY)],
        out_specs=[_rows(D), _rows(D), pl.BlockSpec((2, TT, D), lambda i: (1, i, 0)), _acc_spec(2 * D)],
        out_shape=[jax.ShapeDtypeStruct((T, D), bf16), jax.ShapeDtypeStruct((T, D), bf16),
                   jax.ShapeDtypeStruct(dz8.shape, bf16), jax.ShapeDtypeStruct((8, 2 * D), f32)],
        input_output_aliases={6: 2},
        compiler_params=_cparams(("arbitrary",)))(dmixed, z8, z8, gate_b, ya, yb, dz8)


def _convnorm_bwd(c, ds, g):
    T = c.shape[0]

    def body(c_ref, ds_ref, g_ref, dc_ref, dg_ref):
        cv = c_ref[...]
        rstd = _rms(cv)
        r0 = cv * rstd
        gv = g_ref[...]
        r = r0 * gv
        sg = _sig(r)
        dr = ds_ref[...] * sg * (1.0 + r * (1.0 - sg))
        dc_ref[...] = _rms_bwd(dr * gv, r0, rstd)

        @pl.when(pl.program_id(0) == 0)
        def _():
            dg_ref[...] = jnp.zeros_like(dg_ref)

        dg_ref[...] += _colsum8(dr * r0)

    return pl.pallas_call(
        body, name="convnorm_bwd", grid=(T // TT,), in_specs=[_rows(D), _rows(D), _const((1, D))],
        out_specs=[_rows(D), _acc_spec(D)],
        out_shape=[jax.ShapeDtypeStruct((T, D), f32), jax.ShapeDtypeStruct((8, D), f32)],
        compiler_params=_cparams(("arbitrary",)))(c, ds, g)


def _qk_bwd(z8, dqn, dkn, dv, qg, kg, bd, dz8):
    T = dqn.shape[0]

    def body(q_ref, k_ref, dqn_ref, dkn_ref, dv_ref, qg_ref, kg_ref, bd_ref, dz_in, dz_ref, dqg_ref, dkg_ref):
        del dz_in
        bdv = bd_ref[...]

        @pl.when(pl.program_id(0) == 0)
        def _():
            dqg_ref[...] = jnp.zeros_like(dqg_ref)
            dkg_ref[...] = jnp.zeros_like(dkg_ref)

        def one(raw, dn_scaled, g, dg_ref, sec):
            rstd = lax.rsqrt(_head_sum(raw * raw, bdv) * (1.0 / HEAD_DIM) + EPS)
            n = raw * rstd
            dg_ref[...] += _colsum8(dn_scaled * n)
            dn = dn_scaled * g
            draw = rstd * (dn - n * (_head_sum(dn * n, bdv) * (1.0 / HEAD_DIM)))
            dz_ref[sec] = draw.astype(bf16)

        one(q_ref[...], dqn_ref[...] * (HEAD_DIM ** -0.5), qg_ref[...], dqg_ref, 0)
        one(k_ref[...], dkn_ref[...], kg_ref[...], dkg_ref, 1)
        dz_ref[2] = dv_ref[...].astype(bf16)
        dz_ref[3] = jnp.zeros((TT, D), bf16)

    return pl.pallas_call(
        body, name="qk_bwd", grid=(T // TT,),
        in_specs=[_sec(Z_Q), _sec(Z_K), _rows(D), _rows(D), _rows(D), _const((1, D)), _const((1, D)),
                  _const((128, 128)), pl.BlockSpec(memory_space=pl.ANY)],
        out_specs=[pl.BlockSpec((4, TT, D), lambda i: (1, i, 0)), _acc_spec(D), _acc_spec(D)],
        out_shape=[jax.ShapeDtypeStruct(dz8.shape, bf16), jax.ShapeDtypeStruct((8, D), f32),
                   jax.ShapeDtypeStruct((8, D), f32)],
        input_output_aliases={8: 0},
        compiler_params=_cparams(("arbitrary",)))(z8, z8, dqn, dkn, dv, qg, kg, bd, dz8)


def _norm1_bwd(x, dh, dx1, g):
    T = x.shape[0]

    def body(x_ref, dh_ref, dx1_ref, g_ref, gx_ref, dg_ref):
        xv = x_ref[...]
        rstd = _rms(xv)
        xn = xv * rstd
        dh = dh_ref[...]
        gx_ref[...] = dx1_ref[...] + _rms_bwd(dh * g_ref[...], xn, rstd)

        @pl.when(pl.program_id(0) == 0)
        def _():
            dg_ref[...] = jnp.zeros_like(dg_ref)

        dg_ref[...] += _colsum8(dh * xn)

    return pl.pallas_call(
        body, name="norm1_bwd", grid=(T // TT,), in_specs=[_rows(D), _rows(D), _rows(D), _const((1, D))],
        out_specs=[_rows(D), _acc_spec(D)],
        out_shape=[jax.ShapeDtypeStruct((T, D), f32), jax.ShapeDtypeStruct((8, D), f32)],
        compiler_params=_cparams(("arbitrary",)))(x, dh, dx1, g)


CCW = 256
CR = 64
HALO = 32


def _conv_fwd(z8, conv_w, conv_b, S):
    T = z8.shape[1]
    nb = T // S
    ncb = D // CCW

    def body(av_ref, ag_ref, w_ref, b_ref, c_ref, pad):
        pad[0:HALO, :] = jnp.zeros((HALO, CCW), f32)

        def fill(i, carry):
            r0 = pl.multiple_of(i * 256, 256)
            pad[pl.ds(HALO + r0, 256), :] = av_ref[pl.ds(r0, 256), :] * _sig(ag_ref[pl.ds(r0, 256), :])
            return carry

        lax.fori_loop(0, S // 256, fill, 0)
        bias = b_ref[...]

        def chunk(i, carry):
            r0 = pl.multiple_of(i * CR, CR)
            win = pad[pl.ds(r0, CR + HALO), :]
            acc = jnp.zeros((CR, CCW), f32) + bias
            for j in range(CONV_WIDTH):
                acc = acc + win[2 + j:2 + j + CR, :] * w_ref[j:j + 1, :]
            c_ref[pl.ds(r0, CR), :] = acc
            return carry

        lax.fori_loop(0, S // CR, chunk, 0)

    zs = lambda s: pl.BlockSpec((None, S, CCW), lambda b, cb: (s, b, cb))
    return pl.pallas_call(
        body, name="conv_fwd", grid=(nb, ncb),
        in_specs=[zs(Z_AVAL), zs(Z_AGATE), pl.BlockSpec((CONV_WIDTH, CCW), lambda b, cb: (0, cb)),
                  pl.BlockSpec((1, CCW), lambda b, cb: (0, cb))],
        out_specs=pl.BlockSpec((S, CCW), lambda b, cb: (b, cb)),
        out_shape=jax.ShapeDtypeStruct((T, D), f32),
        scratch_shapes=[pltpu.VMEM((S + HALO, CCW), f32)],
        compiler_params=_cparams(("parallel", "parallel")))(z8, z8, conv_w, conv_b)


def _conv_bwd(dc, z8, conv_w, dz8, S):
    T = dc.shape[0]
    nb = T // S
    ncb = D // CCW

    def body(dc_ref, av_ref, ag_ref, w_ref, dz_in, dz_ref, dw_ref, apad, dpad):
        del dz_in
        apad[0:HALO, :] = jnp.zeros((HALO, CCW), f32)
        dpad[S:S + HALO, :] = jnp.zeros((HALO, CCW), f32)
        dw_ref[...] = jnp.zeros_like(dw_ref)

        def fill(i, carry):
            r0 = pl.multiple_of(i * 256, 256)
            apad[pl.ds(HALO + r0, 256), :] = av_ref[pl.ds(r0, 256), :] * _sig(ag_ref[pl.ds(r0, 256), :])
            dpad[pl.ds(r0, 256), :] = dc_ref[pl.ds(r0, 256), :]
            return carry

        lax.fori_loop(0, S // 256, fill, 0)

        def chunk(i, carry):
            r0 = pl.multiple_of(i * CR, CR)
            awin = apad[pl.ds(r0, CR + HALO), :]
            dwin = dpad[pl.ds(r0, CR + HALO), :]
            dcc = dwin[0:CR, :]
            da = jnp.zeros((CR, CCW), f32)
            for j in range(CONV_WIDTH):
                da = da + dwin[30 - j:30 - j + CR, :] * w_ref[j:j + 1, :]
                dw_ref[8 * j:8 * j + 8, :] += _colsum8(dcc * awin[2 + j:2 + j + CR, :])
            dw_ref[8 * CONV_WIDTH:8 * CONV_WIDTH + 8, :] += _colsum8(dcc)
            av = av_ref[pl.ds(r0, CR), :]
            sg = _sig(ag_ref[pl.ds(r0, CR), :])
            dz_ref[0, pl.ds(r0, CR), :] = (da * sg).astype(bf16)
            dz_ref[1, pl.ds(r0, CR), :] = (da * av * sg * (1.0 - sg)).astype(bf16)
            return carry

        lax.fori_loop(0, S // CR, chunk, 0)

    zs = lambda s: pl.BlockSpec((None, S, CCW), lambda b, cb: (s, b, cb))
    return pl.pallas_call(
        body, name="conv_bwd", grid=(nb, ncb),
        in_specs=[pl.BlockSpec((S, CCW), lambda b, cb: (b, cb)), zs(Z_AVAL), zs(Z_AGATE),
                  pl.BlockSpec((CONV_WIDTH, CCW), lambda b, cb: (0, cb)), pl.BlockSpec(memory_space=pl.ANY)],
        out_specs=[pl.BlockSpec((2, S, CCW), lambda b, cb: (0, b, cb)),
                   pl.BlockSpec((None, 256, CCW), lambda b, cb: (b, 0, cb))],
        out_shape=[jax.ShapeDtypeStruct(dz8.shape, bf16), jax.ShapeDtypeStruct((nb, 256, D), f32)],
        input_output_aliases={4: 0},
        scratch_shapes=[pltpu.VMEM((S + HALO, CCW), f32), pltpu.VMEM((S + HALO, CCW), f32)],
        compiler_params=_cparams(("parallel", "parallel")))(dc, z8, z8, conv_w, dz8)


FR = 128
NFB = D_FF // CCW


def _ffn_window(ref, i, r0):
    return ref[pl.ds(r0 - 8, FR + 8), :]


def _ffn_u(win, w_ref, b_ref):
    return (win[6:6 + FR, :] * w_ref[0:1, :] + win[7:7 + FR, :] * w_ref[1:2, :]
            + win[8:8 + FR, :] * w_ref[2:3, :] + b_ref[...])


def _ffn_fwd(u3, ffn_w, ffn_b, S):
    T = u3.shape[1]
    nb = T // S

    def body(uv_ref, ug_ref, wv_ref, wg_ref, bv_ref, bg_ref, f_ref):
        def chunk(first, i):
            r0 = 0 if first else pl.multiple_of(i * FR, FR)
            if first:
                z = jnp.zeros((8, CCW), f32)
                wv = jnp.concatenate([z, uv_ref[0:FR, :]], axis=0)
                wg = jnp.concatenate([z, ug_ref[0:FR, :]], axis=0)
            else:
                wv = _ffn_window(uv_ref, i, r0)
                wg = _ffn_window(ug_ref, i, r0)
            u_val = _ffn_u(wv, wv_ref, bv_ref)
            u_gate = _ffn_u(wg, wg_ref, bg_ref)
            f_ref[pl.ds(r0, FR), :] = (u_gate * _sig(u_gate) * u_val).astype(bf16)

        chunk(True, 0)

        def loop(i, carry):
            chunk(False, i)
            return carry

        lax.fori_loop(1, S // FR, loop, 0)

    us = lambda h: pl.BlockSpec((None, S, CCW), lambda b, cb: (h, b, cb))
    ws = lambda h: pl.BlockSpec((3, CCW), lambda b, cb: (0, h * NFB + cb))
    bs = lambda h: pl.BlockSpec((1, CCW), lambda b, cb: (0, h * NFB + cb))
    return pl.pallas_call(
        body, name="ffn_fwd", grid=(nb, NFB),
        in_specs=[us(0), us(1), ws(0), ws(1), bs(0), bs(1)],
        out_specs=pl.BlockSpec((S, CCW), lambda b, cb: (b, cb)),
        out_shape=jax.ShapeDtypeStruct((T, D_FF), bf16),
        compiler_params=_cparams(("parallel", "parallel")))(u3, u3, ffn_w, ffn_w, ffn_b, ffn_b)


def _ffn_bwd(u3, df, ffn_w, ffn_b, S):
    T = u3.shape[1]
    nb = T // S

    def body(uv_ref, ug_ref, df_ref, wv_ref, wg_ref, bv_ref, bg_ref, du_ref, dw_ref, dvpad, dgpad):
        dvpad[S:S + 8, :] = jnp.zeros((8, CCW), f32)
        dgpad[S:S + 8, :] = jnp.zeros((8, CCW), f32)
        dw_ref[...] = jnp.zeros_like(dw_ref)

        def chunk(first, i):
            r0 = 0 if first else pl.multiple_of(i * FR, FR)
            if first:
                z = jnp.zeros((8, CCW), f32)
                wv = jnp.concatenate([z, uv_ref[0:FR, :]], axis=0)
                wg = jnp.concatenate([z, ug_ref[0:FR, :]], axis=0)
            else:
                wv = _ffn_window(uv_ref, i, r0)
                wg = _ffn_window(ug_ref, i, r0)
            u_val = _ffn_u(wv, wv_ref, bv_ref)
            u_gate = _ffn_u(wg, wg_ref, bg_ref)
            dfc = df_ref[pl.ds(r0, FR), :]
            sg = _sig(u_gate)
            d_val = dfc * u_gate * sg
            d_gate = dfc * u_val * sg * (1.0 + u_gate * (1.0 - sg))
            dvpad[pl.ds(r0, FR), :] = d_val
            dgpad[pl.ds(r0, FR), :] = d_gate
            for h, (dd, win) in enumerate(((d_val, wv), (d_gate, wg))):
                for j in range(3):
                    dw_ref[h, 8 * j:8 * j + 8, :] += _colsum8(dd * win[6 + j:6 + j + FR, :])
                dw_ref[h, 24:32, :] += _colsum8(dd)

        chunk(True, 0)

        def loop(i, carry):
            chunk(False, i)
            return carry

        lax.fori_loop(1, S // FR, loop, 0)

        def back(i, carry):
            r0 = pl.multiple_of(i * FR, FR)
            for h, (dpad, w_ref) in enumerate(((dvpad, wv_ref), (dgpad, wg_ref))):
                win = dpad[pl.ds(r0, FR + 8), :]
                du = (win[0:FR, :] * w_ref[2:3, :] + win[1:1 + FR, :] * w_ref[1:2, :]
                      + win[2:2 + FR, :] * w_ref[0:1, :])
                du_ref[h, pl.ds(r0, FR), :] = du.astype(bf16)
            return carry

        lax.fori_loop(0, S // FR, back, 0)

    us = lambda h: pl.BlockSpec((None, S, CCW), lambda b, cb: (h, b, cb))
    ws = lambda h: pl.BlockSpec((3, CCW), lambda b, cb: (0, h * NFB + cb))
    bs = lambda h: pl.BlockSpec((1, CCW), lambda b, cb: (0, h * NFB + cb))
    return pl.pallas_call(
        body, name="ffn_bwd", grid=(nb, NFB),
        in_specs=[us(0), us(1), pl.BlockSpec((S, CCW), lambda b, cb: (b, cb)), ws(0), ws(1), bs(0), bs(1)],
        out_specs=[pl.BlockSpec((2, S, CCW), lambda b, cb: (0, b, cb)),
                   pl.BlockSpec((None, 2, 32, CCW), lambda b, cb: (b, 0, 0, cb))],
        out_shape=[jax.ShapeDtypeStruct((2, T, D_FF), bf16), jax.ShapeDtypeStruct((nb, 2, 32, D_FF), f32)],
        scratch_shapes=[pltpu.VMEM((S + 8, CCW), f32), pltpu.VMEM((S + 8, CCW), f32)],
        compiler_params=_cparams(("parallel", "parallel")))(u3, u3, df, ffn_w, ffn_w, ffn_b, ffn_b)


AB = ATTN_BLOCK


def _attn_bias():
    slopes = 2.0 ** (-8.0 * jnp.arange(1, N_HEADS + 1, dtype=f32) / N_HEADS)
    steps = (jnp.arange(AB)[:, None] + AB) - jnp.arange(2 * AB)[None, :]
    out = []
    for window, dil in GROUPS:
        valid = (steps >= 0) & (steps <= window // dil)
        dist = (steps * dil).astype(f32)
        out.append(jnp.where(valid[None], slopes[:, None, None] * dist[None], MASK_BIAS))
    return jnp.stack(out)


def _head_masks():
    lane = lax.broadcasted_iota(jnp.int32, (1, 128), 1)
    return (lane < HEAD_DIM, lane >= HEAD_DIM)


def _attn_blocks(S, visit):
    for g, (_, d) in enumerate(GROUPS):
        nblk = S // (d * AB)
        for r in range(d):
            visit(g, d, r, True)
            if nblk > 1:
                def loop(i, carry, g=g, d=d, r=r):
                    visit(g, d, r + d * AB * i, False)
                    return carry
                lax.fori_loop(1, nblk, loop, 0)


def _attn_fwd(qn, kn, z8, bias, S):
    T = qn.shape[0]
    nb = T // S

    def body(q_ref, k_ref, v_ref, bias_ref, o_ref, ob_ref, lse_ref, *group_scratch):
        og, lg = group_scratch[:3], group_scratch[3:]
        masks = _head_masks()

        def visit(g, d, base, first):
            nk = AB if first else 2 * AB
            kbase = base if first else base - d * AB
            rows = pl.ds(base, AB, stride=d)
            krows = pl.ds(kbase, nk, stride=d)
            q = q_ref[rows, :].astype(bf16)
            kc = k_ref[krows, :].astype(bf16)
            vc = v_ref[krows, :].astype(bf16)
            o_acc = jnp.zeros((AB, 128), f32)
            l_acc = jnp.zeros((AB, 128), f32)
            for hh in range(2):
                mh = masks[hh]
                s = lax.dot_general(jnp.where(mh, q, 0), kc, (((1,), (1,)), ((), ())), preferred_element_type=f32)
                b = bias_ref[g, hh]
                s = s - (b[:, AB:] if first else b)
                m = jnp.max(s, axis=-1, keepdims=True)
                p = jnp.exp(s - m)
                den = jnp.sum(p, axis=-1, keepdims=True)
                pv = jnp.dot(p.astype(bf16), jnp.where(mh, vc, 0), preferred_element_type=f32)
                o_acc = o_acc + pv / den
                l_acc = jnp.where(mh, m + jnp.log(den), l_acc)
            og[g][rows, :] = o_acc
            lg[g][rows, :] = l_acc

        _attn_blocks(S, visit)

        def combine(i, carry):
            rr = pl.ds(pl.multiple_of(i * 256, 256), 256)
            l0, l1, l2 = lg[0][rr, :], lg[1][rr, :], lg[2][rr, :]
            mx = jnp.maximum(jnp.maximum(l0, l1), l2)
            e0, e1, e2 = jnp.exp(l0 - mx), jnp.exp(l1 - mx), jnp.exp(l2 - mx)
            den = e0 + e1 + e2
            o = (e0 * og[0][rr, :] + e1 * og[1][rr, :] + e2 * og[2][rr, :]) / den
            o_ref[rr, :] = o
            ob_ref[rr, :] = o.astype(bf16)
            lse_ref[rr, :] = mx + jnp.log(den)
            return carry

        lax.fori_loop(0, S // 256, combine, 0)

    blk = pl.BlockSpec((S, 128), lambda b, hp: (b, hp))
    return pl.pallas_call(
        body, name="attn_fwd", grid=(nb, N_HEADS // 2),
        in_specs=[blk, blk, pl.BlockSpec((None, S, 128), lambda b, hp: (Z_V, b, hp)),
                  pl.BlockSpec((3, 2, AB, 2 * AB), lambda b, hp: (0, hp, 0, 0))],
        out_specs=[blk, blk, blk],
        out_shape=[jax.ShapeDtypeStruct((T, D), f32), jax.ShapeDtypeStruct((T, D), bf16),
                   jax.ShapeDtypeStruct((T, D), f32)],
        scratch_shapes=[pltpu.VMEM((S, 128), f32)] * 6,
        compiler_params=_cparams(("parallel", "parallel")))(qn, kn, z8, bias)


def _attn_bwd(qn, kn, z8, do, o, lse, bias, bd, S):
    T = qn.shape[0]
    nb = T // S

    def body(q_ref, k_ref, v_ref, do_ref, o_ref, lse_ref, bias_ref, bd_ref, dq_ref, dk_ref, dv_ref, delta):
        masks = _head_masks()
        bdv = bd_ref[...]
        dq_ref[...] = jnp.zeros_like(dq_ref)
        dk_ref[...] = jnp.zeros_like(dk_ref)
        dv_ref[...] = jnp.zeros_like(dv_ref)

        def prep(i, carry):
            rr = pl.ds(pl.multiple_of(i * 256, 256), 256)
            delta[rr, :] = _head_sum(do_ref[rr, :] * o_ref[rr, :], bdv)
            return carry

        lax.fori_loop(0, S // 256, prep, 0)

        def visit(g, d, base, first):
            nk = AB if first else 2 * AB
            kbase = base if first else base - d * AB
            rows = pl.ds(base, AB, stride=d)
            krows = pl.ds(kbase, nk, stride=d)
            q = q_ref[rows, :].astype(bf16)
            kc = k_ref[krows, :].astype(bf16)
            vc = v_ref[krows, :].astype(bf16)
            dob = do_ref[rows, :].astype(bf16)
            lse_b = lse_ref[rows, :]
            del_b = delta[rows, :]
            dq = jnp.zeros((AB, 128), f32)
            dkc = jnp.zeros((nk, 128), f32)
            dvc = jnp.zeros((nk, 128), f32)
            for hh in range(2):
                mh = masks[hh]
                c0 = hh * HEAD_DIM
                qh = jnp.where(mh, q, 0)
                doh = jnp.where(mh, dob, 0)
                s = lax.dot_general(qh, kc, (((1,), (1,)), ((), ())), preferred_element_type=f32)
                b = bias_ref[g, hh]
                s = s - (b[:, AB:] if first else b)
                p = jnp.exp(s - lse_b[:, c0:c0 + 1])
                dp = lax.dot_general(doh, vc, (((1,), (1,)), ((), ())), preferred_element_type=f32)
                ds = (p * (dp - del_b[:, c0:c0 + 1])).astype(bf16)
                pb = p.astype(bf16)
                dq = dq + jnp.dot(ds, jnp.where(mh, kc, 0), preferred_element_type=f32)
                dkc = dkc + lax.dot_general(ds, qh, (((0,), (0,)), ((), ())), preferred_element_type=f32)
                dvc = dvc + lax.dot_general(pb, doh, (((0,), (0,)), ((), ())), preferred_element_type=f32)
            dq_ref[rows, :] += dq
            dk_ref[krows, :] += dkc
            dv_ref[krows, :] += dvc

        _attn_blocks(S, visit)

    blk = pl.BlockSpec((S, 128), lambda b, hp: (b, hp))
    return pl.pallas_call(
        body, name="attn_bwd", grid=(nb, N_HEADS // 2),
        in_specs=[blk, blk, pl.BlockSpec((None, S, 128), lambda b, hp: (Z_V, b, hp)), blk, blk, blk,
                  pl.BlockSpec((3, 2, AB, 2 * AB), lambda b, hp: (0, hp, 0, 0)),
                  pl.BlockSpec((128, 128), lambda b, hp: (0, 0))],
        out_specs=[blk, blk, blk],
        out_shape=[jax.ShapeDtypeStruct((T, D), f32)] * 3,
        scratch_shapes=[pltpu.VMEM((S, 128), f32)],
        compiler_params=_cparams(("parallel", "parallel")))(qn, kn, z8, do, o, lse, bias, bd)


def _any_spec():
    return pl.BlockSpec(memory_space=pl.ANY)


def _allgather_rows(shards):
    n = len(shards)

    def body(*refs):
        ins, outs = refs[:n], refs[n:2 * n]
        send_sems, recv_sems, local_sems = refs[2 * n:]
        x, y, c, me = _my_pos()
        sibling = (x, y, 1 - c)
        chips = [(1 - x, y), (x, 1 - y), (1 - x, 1 - y)]

        def idx(px, py, pc):
            return 4 * px + 2 * py + pc

        def copy(a, k, blk, to, src=None):
            return pltpu.make_async_remote_copy(
                src_ref=outs[a].at[blk] if src is None else src, dst_ref=outs[a].at[blk],
                send_sem=send_sems.at[a, k], recv_sem=recv_sems.at[a, k], device_id=to, device_id_type=MESH)

        mine = [pltpu.make_async_copy(ins[a], outs[a].at[me], local_sems.at[a]) for a in range(n)]
        for cp in mine:
            cp.start()
        first = []
        for a in range(n):
            first.append(copy(a, 0, me, sibling, src=ins[a]))
            first += [copy(a, 1 + j, me, (*chip, c), src=ins[a]) for j, chip in enumerate(chips)]
        for cp in first:
            cp.start()
        passed = []
        for a in range(n):
            for j, chip in enumerate(chips):
                blk = idx(*chip, c)
                copy(a, 1 + j, blk, (x, y, c)).wait_recv()
                cp = copy(a, 4 + j, blk, sibling)
                cp.start()
                passed.append(cp)
        for a in range(n):
            copy(a, 0, idx(x, y, 1 - c), (x, y, c)).wait_recv()
            for j, chip in enumerate(chips):
                copy(a, 4 + j, idx(*chip, 1 - c), (x, y, c)).wait_recv()
        for cp in first + passed:
            cp.wait_send()
        for cp in mine:
            cp.wait()

    return pl.pallas_call(
        body, name="allgather_weights",
        in_specs=[_any_spec()] * n, out_specs=[_any_spec()] * n,
        out_shape=[jax.ShapeDtypeStruct((N_DEV,) + s.shape, s.dtype) for s in shards],
        scratch_shapes=[pltpu.SemaphoreType.DMA((n, 7)), pltpu.SemaphoreType.DMA((n, 7)),
                        pltpu.SemaphoreType.DMA((n,))],
    )(*shards)


def _peer(x, y, c, k):
    tx = 1 - x if (k >> 2) & 1 else x
    ty = 1 - y if (k >> 1) & 1 else y
    tc = 1 - c if k & 1 else c
    return (tx, ty, tc), 4 * tx + 2 * ty + tc


_PEER_ORDER = (2, 4, 6, 3, 5, 7, 1)


def _scatter_blocks(grads):
    n = len(grads)

    def body(*refs):
        ins, outs = refs[:n], refs[n:2 * n]
        send_sems, recv_sems, local_sems = refs[2 * n:]
        x, y, c, me = _my_pos()
        mine = [pltpu.make_async_copy(ins[a].at[me], outs[a].at[me], local_sems.at[a]) for a in range(n)]
        for cp in mine:
            cp.start()
        copies = []
        for k in _PEER_ORDER:
            tgt, tidx = _peer(x, y, c, k)
            for a in range(n):
                cp = pltpu.make_async_remote_copy(
                    src_ref=ins[a].at[tidx], dst_ref=outs[a].at[me],
                    send_sem=send_sems.at[a, k - 1], recv_sem=recv_sems.at[a, k - 1],
                    device_id=tgt, device_id_type=MESH)
                cp.start()
                copies.append(cp)
        for cp in copies:
            cp.wait()
        for cp in mine:
            cp.wait()

    return pl.pallas_call(
        body, name="scatter_grads",
        in_specs=[_any_spec()] * n, out_specs=[_any_spec()] * n,
        out_shape=[jax.ShapeDtypeStruct(g.shape, g.dtype) for g in grads],
        scratch_shapes=[pltpu.SemaphoreType.DMA((n, 7)), pltpu.SemaphoreType.DMA((n, 7)),
                        pltpu.SemaphoreType.DMA((n,))],
    )(*grads)


SMALL_ROWS = 64


def _allreduce_small(name, sg):
    def body(sg_ref, out_ref, buf, send_sems, recv_sems):
        x, y, c, me = _my_pos()
        buf[me] = sg_ref[...]
        copies = []
        for k in _PEER_ORDER:
            tgt, _ = _peer(x, y, c, k)
            cp = pltpu.make_async_remote_copy(
                src_ref=sg_ref, dst_ref=buf.at[me], send_sem=send_sems.at[k - 1], recv_sem=recv_sems.at[k - 1],
                device_id=tgt, device_id_type=MESH)
            cp.start()
            copies.append(cp)
        for cp in copies:
            cp.wait()
        acc = buf[0]
        for p in range(1, N_DEV):
            acc = acc + buf[p]
        out_ref[...] = acc

    return pl.pallas_call(
        body, name=name,
        in_specs=[pl.BlockSpec(memory_space=pltpu.VMEM)], out_specs=pl.BlockSpec(memory_space=pltpu.VMEM),
        out_shape=jax.ShapeDtypeStruct(sg.shape, f32),
        scratch_shapes=[pltpu.VMEM((N_DEV,) + sg.shape, f32), pltpu.SemaphoreType.DMA((7,)),
                        pltpu.SemaphoreType.DMA((7,))],
    )(sg)


def _adam_math(g, w, m, v):
    m = ADAM_B1 * m + (1.0 - ADAM_B1) * g
    v = ADAM_B2 * v + (1.0 - ADAM_B2) * (g * g)
    m_hat = m / (1.0 - ADAM_B1 ** ADAM_STEP)
    v_hat = v / (1.0 - ADAM_B2 ** ADAM_STEP)
    delta = -ADAM_LR * (m_hat / (jnp.sqrt(v_hat) + ADAM_EPS) + ADAM_WD * w)
    return delta, m, v


def _adam_slots(name, slots, w, m, v, tr):
    rows = w.shape[0]

    def body(s_ref, w_ref, m_ref, v_ref, g_ref, d_ref, nm_ref, nv_ref):
        g = s_ref[0].astype(f32)
        for p in range(1, N_DEV):
            g = g + s_ref[p].astype(f32)
        delta, nm, nv = _adam_math(g, w_ref[...], m_ref[...], v_ref[...])
        g_ref[...] = g
        d_ref[...] = delta
        nm_ref[...] = nm
        nv_ref[...] = nv

    rs = pl.BlockSpec((tr, D), lambda i: (i, 0))
    return pl.pallas_call(
        body, name=name, grid=(rows // tr,),
        in_specs=[pl.BlockSpec((N_DEV, tr, D), lambda i: (0, i, 0)), rs, rs, rs], out_specs=[rs] * 4,
        out_shape=[jax.ShapeDtypeStruct((rows, D), f32)] * 4,
        compiler_params=_cparams(("parallel",)))(slots, w, m, v)


def _adam_small(g, w, m, v):
    def body(g_ref, w_ref, m_ref, v_ref, d_ref, nm_ref, nv_ref):
        delta, nm, nv = _adam_math(g_ref[...], w_ref[...], m_ref[...], v_ref[...])
        d_ref[...] = delta
        nm_ref[...] = nm
        nv_ref[...] = nv

    return pl.pallas_call(body, name="adam_small", out_shape=[jax.ShapeDtypeStruct(g.shape, f32)] * 3)(g, w, m, v)


FFN_PAD = 6 * D


def _pack_small(norm1_g, gate_b, conv_w, conv_b, conv_norm_g, q_norm_g, k_norm_g, norm2_g, ffn_conv_w, ffn_conv_b):
    pad_h = lambda a: jnp.pad(a, ((0, 0), (0, D - HEAD_DIM)))
    pad_f = lambda a: jnp.pad(a, ((0, 0), (0, FFN_PAD - 2 * D_FF))).reshape(-1, D)
    parts = [norm1_g, gate_b.reshape(2, D), conv_w, conv_b, conv_norm_g, pad_h(q_norm_g), pad_h(k_norm_g), norm2_g,
             pad_f(ffn_conv_w), pad_f(ffn_conv_b)]
    out = jnp.concatenate(parts, axis=0)
    return jnp.pad(out, ((0, SMALL_ROWS - out.shape[0]), (0, 0)))


def _unpack_small(p):
    ffn = lambda a: a.reshape(-1, FFN_PAD)[:, :2 * D_FF]
    return dict(
        norm1_g=p[0:1], gate_b=p[1:3].reshape(1, 2 * D), conv_w=p[3:34], conv_b=p[34:35], conv_norm_g=p[35:36],
        q_norm_g=p[36:37, :HEAD_DIM], k_norm_g=p[37:38, :HEAD_DIM], norm2_g=p[38:39],
        ffn_conv_w=ffn(p[39:57]), ffn_conv_b=ffn(p[57:63]))


_ADAM_TILE = {896: 128, 704: 64, 128: 128, 352: 176}


def kernel(x, norm1_g, w_in, gate_b, conv_w, conv_b, conv_norm_g, w_conv_out, q_norm_g, k_norm_g, w_attn_out, w_out, norm2_g, w_up, ffn_conv_w, ffn_conv_b, w_down, loss_target, m_norm1_g, m_w_in, m_gate_b, m_conv_w, m_conv_b, m_conv_norm_g, m_w_conv_out, m_q_norm_g, m_k_norm_g, m_w_attn_out, m_w_out, m_norm2_g, m_w_up, m_ffn_conv_w, m_ffn_conv_b, m_w_down, v_norm1_g, v_w_in, v_gate_b, v_conv_w, v_conv_b, v_conv_norm_g, v_w_conv_out, v_q_norm_g, v_k_norm_g, v_w_attn_out, v_w_out, v_norm2_g, v_w_up, v_ffn_conv_w, v_ffn_conv_b, v_w_down):
    BL, S, _ = x.shape
    T = BL * S
    me = 4 * lax.axis_index("x") + 2 * lax.axis_index("y") + lax.axis_index("c")
    xt = x.reshape(T, D)
    target = loss_target.reshape(T, D)

    big = dict(w_in=(w_in[0].T, m_w_in[0].T, v_w_in[0].T), w_up=(w_up[0].T, m_w_up[0].T, v_w_up[0].T),
               w_conv_out=(w_conv_out[0], m_w_conv_out[0], v_w_conv_out[0]),
               w_attn_out=(w_attn_out[0], m_w_attn_out[0], v_w_attn_out[0]),
               w_out=(w_out[0], m_w_out[0], v_w_out[0]), w_down=(w_down[0], m_w_down[0], v_w_down[0]))
    order = ["w_in", "w_conv_out", "w_attn_out", "w_out", "w_up", "w_down"]
    gathered = _allgather_rows([big[n][0].astype(bf16) for n in order])
    W = {n: g.reshape(-1, D) for n, g in zip(order, gathered)}

    def place_cols(shard, full_cols):
        z = jnp.zeros((shard.shape[0], full_cols), f32)
        return lax.dynamic_update_slice(z, shard, (0, me * shard.shape[1]))

    zr = lambda a: jnp.zeros_like(a)
    conv_local = _pack_small(
        zr(norm1_g), zr(gate_b), place_cols(conv_w[0], D), zr(conv_b), zr(conv_norm_g), zr(q_norm_g), zr(k_norm_g),
        zr(norm2_g), place_cols(ffn_conv_w[0], 2 * D_FF), zr(ffn_conv_b))
    conv_all = _unpack_small(_allreduce_small("gather_conv_weights", conv_local))
    conv_w_full, ffn_w_full = conv_all["conv_w"], conv_all["ffn_conv_w"]

    bd = (jnp.arange(128)[:, None] // HEAD_DIM == jnp.arange(128)[None, :] // HEAD_DIM).astype(bf16)
    bias = _attn_bias()
    qg = jnp.tile(q_norm_g, (1, N_HEADS))
    kg = jnp.tile(k_norm_g, (1, N_HEADS))

    h = _norm1_fwd(xt, norm1_g)
    z8 = _matmul_call(
        "mm_z", h, W["w_in"],
        pl.BlockSpec((1024, D), lambda i, j, k: (i, 0)),
        pl.BlockSpec((1024, D), lambda i, j, k: (_wsec_of_zsec(j), 0)),
        pl.BlockSpec((None, 1024, D), lambda i, j, k: (j, i, 0)),
        jax.ShapeDtypeStruct((8, T, D), f32), (T // 1024, 7, 1), "nt", 1, 1024, 1024)
    c = _conv_fwd(z8, conv_w_full, conv_b, S)
    s = _convnorm_fwd(c, conv_norm_g)
    ya = _matmul("mm_ya", s, W["w_conv_out"], "nn", f32)
    qn, kn = _qk_fwd(z8, qg, kg, bd)
    o, ob, lse = _attn_fwd(qn, kn, z8, bias, S)
    yb = _matmul("mm_yb", ob, W["w_attn_out"], "nn", f32)
    mixed = _gate_fwd(z8, gate_b, ya, yb)
    t1 = _matmul("mm_t1", mixed, W["w_out"], "nn", f32)
    x1, h2 = _norm2_fwd(xt, t1, norm2_g)
    TNU = D_FF // 2
    u3 = _matmul_call(
        "mm_u", h2, W["w_up"],
        pl.BlockSpec((1024, D), lambda i, j, k: (i, 0)),
        pl.BlockSpec((TNU, D), lambda i, j, k: (j, 0)),
        pl.BlockSpec((None, 1024, TNU), lambda i, j, k: (j // 2, i, j % 2)),
        jax.ShapeDtypeStruct((2, T, D_FF), f32), (T // 1024, 4, 1), "nt", 1, 1024, TNU)
    f = _ffn_fwd(u3, ffn_w_full, ffn_conv_b, S)
    t2 = _matmul("mm_t2", f, W["w_down"], "nn", f32, tk=TNU)
    dy, dyb, lacc = _loss_fwd(x1, t2, target)
    loss = lax.psum(0.5 / D * jnp.sum(lacc), ("x", "y", "c"))

    df = _matmul("mm_df", dyb, W["w_down"], "nt", f32, tn=TNU)
    g_w_down = _matmul("mm_dwdn", f, dyb, "tn", bf16, tm=TNU, tk=1024)
    du3, dffn = _ffn_bwd(u3, df, ffn_w_full, ffn_conv_b, S)
    dh2 = _matmul_call(
        "mm_dh2", du3, W["w_up"],
        pl.BlockSpec((None, 1024, TNU), lambda i, j, k: (k // 2, i, k % 2)),
        pl.BlockSpec((TNU, D), lambda i, j, k: (k, 0)),
        pl.BlockSpec((1024, D), lambda i, j, k: (i, 0)),
        jax.ShapeDtypeStruct((T, D), f32), (T // 1024, 1, 4), "nn", 4, 1024, D)
    g_w_up = _matmul_call(
        "mm_dwup", du3, h2,
        pl.BlockSpec((None, 1024, TNU), lambda i, j, k: (i // 2, k, i % 2)),
        pl.BlockSpec((1024, D), lambda i, j, k: (k, 0)),
        pl.BlockSpec((TNU, D), lambda i, j, k: (i, 0)),
        jax.ShapeDtypeStruct((2 * D_FF, D), bf16), (4, 1, T // 1024), "tn", T // 1024, TNU, D)
    dx1, dx1b, dg_norm2 = _norm2_bwd(x1, dh2, dy, norm2_g)
    dmixed = _matmul("mm_dmixed", dx1b, W["w_out"], "nt", f32)
    g_w_out = _matmul("mm_dwo", mixed, dx1b, "tn", bf16, tk=1024)
    dz8 = lax.empty((8, T, D), bf16)
    dya, dyb2, dz8, dg_gate = _gate_bwd(dmixed, z8, gate_b, ya, yb, dz8)
    ds = _matmul("mm_ds", dya, W["w_conv_out"], "nt", f32)
    g_w_conv_out = _matmul("mm_dwco", s, dya, "tn", bf16, tk=1024)
    do = _matmul("mm_do", dyb2, W["w_attn_out"], "nt", f32)
    g_w_attn_out = _matmul("mm_dwao", ob, dyb2, "tn", bf16, tk=1024)
    dc, dg_convnorm = _convnorm_bwd(c, ds, conv_norm_g)
    dz8a, dconv = _conv_bwd(dc, z8, conv_w_full, dz8, S)
    dqn, dkn, dv = _attn_bwd(qn, kn, z8, do, o, lse, bias, bd, S)
    dz8b, dg_q, dg_k = _qk_bwd(z8, dqn, dkn, dv, qg, kg, bd, dz8a)
    dh = _matmul_call(
        "mm_dh", dz8b, W["w_in"],
        pl.BlockSpec((None, 1024, D), lambda i, j, k: (k, i, 0)),
        pl.BlockSpec((1024, D), lambda i, j, k: (_wsec_of_zsec(k), 0)),
        pl.BlockSpec((1024, D), lambda i, j, k: (i, 0)),
        jax.ShapeDtypeStruct((T, D), f32), (T // 1024, 1, 7), "nn", 7, 1024, D)
    g_w_in = _matmul_call(
        "mm_dwin", dz8b, h,
        pl.BlockSpec((None, 1024, D), lambda i, j, k: (_zsec_of_wsec(i), k, 0)),
        pl.BlockSpec((1024, D), lambda i, j, k: (k, 0)),
        pl.BlockSpec((1024, D), lambda i, j, k: (i, 0)),
        jax.ShapeDtypeStruct((7 * D, D), bf16), (7, 1, T // 1024), "tn", T // 1024, D, D)
    grad_x, dg_norm1 = _norm1_bwd(xt, dh, dx1, norm1_g)

    gbig = dict(w_in=g_w_in, w_conv_out=g_w_conv_out, w_attn_out=g_w_attn_out, w_out=g_w_out, w_up=g_w_up,
                w_down=g_w_down)
    slots = _scatter_blocks([gbig[n].reshape(N_DEV, -1, D) for n in order])
    slots = dict(zip(order, slots))

    sum8 = lambda a: a.reshape(-1, 8, a.shape[-1]).sum(axis=1)
    dconv_s = sum8(dconv.sum(axis=0))
    dffn_s = dffn.sum(axis=0).reshape(2, 4, 8, D_FF).sum(axis=2)
    dffn_w = jnp.concatenate([dffn_s[0, :3], dffn_s[1, :3]], axis=1)
    dffn_b = jnp.concatenate([dffn_s[0, 3:4], dffn_s[1, 3:4]], axis=1)
    fold = lambda a: sum8(a).reshape(N_HEADS, HEAD_DIM).sum(axis=0)[None]
    small_g_local = _pack_small(
        sum8(dg_norm1), sum8(dg_gate), dconv_s[:CONV_WIDTH], dconv_s[CONV_WIDTH:], sum8(dg_convnorm),
        fold(dg_q), fold(dg_k), sum8(dg_norm2), dffn_w, dffn_b)
    small_g = _allreduce_small("allreduce_small_grads", small_g_local)

    res = {}
    for n in order:
        w, m, v = big[n]
        outs = _adam_slots("adam_" + n, slots[n], w, m, v, _ADAM_TILE[w.shape[0]])
        if n in ("w_in", "w_up"):
            outs = [a.T for a in outs]
        res[n] = [a[None] for a in outs]

    col = lambda a, width: lax.dynamic_slice(a, (0, me * width), (a.shape[0], width))
    small_w_true = _pack_small(norm1_g, gate_b, conv_w_full, conv_b, conv_norm_g, q_norm_g, k_norm_g, norm2_g,
                               ffn_w_full, ffn_conv_b)
    place_m = lambda a, full: place_cols(a[0], full)
    small_m = _pack_small(m_norm1_g, m_gate_b, place_m(m_conv_w, D), m_conv_b, m_conv_norm_g, m_q_norm_g, m_k_norm_g,
                          m_norm2_g, place_m(m_ffn_conv_w, 2 * D_FF), m_ffn_conv_b)
    small_v = _pack_small(v_norm1_g, v_gate_b, place_m(v_conv_w, D), v_conv_b, v_conv_norm_g, v_q_norm_g, v_k_norm_g,
                          v_norm2_g, place_m(v_ffn_conv_w, 2 * D_FF), v_ffn_conv_b)
    sd, sm, sv = _adam_small(small_g, small_w_true, small_m, small_v)
    for i, packed in enumerate((small_g, sd, sm, sv)):
        u = _unpack_small(packed)
        u["conv_w"] = col(u["conv_w"], D // N_DEV)
        u["ffn_conv_w"] = col(u["ffn_conv_w"], 2 * D_FF // N_DEV)
        for n, a in u.items():
            res.setdefault(n, [None] * 4)[i] = a[None] if n in ("conv_w", "ffn_conv_w") else a

    names = ["norm1_g", "w_in", "gate_b", "conv_w", "conv_b", "conv_norm_g", "w_conv_out", "q_norm_g", "k_norm_g",
             "w_attn_out", "w_out", "norm2_g", "w_up", "ffn_conv_w", "ffn_conv_b", "w_down"]
    out = [loss, grad_x.reshape(BL, S, D)]
    for i in range(4):
        out += [res[n][i] for n in names]
    return tuple(out)
```

```python
import functools

import jax
import jax.numpy as jnp
from jax import lax
from jax.experimental import pallas as pl
from jax.experimental.pallas import tpu as pltpu

f32 = jnp.float32
bf16 = jnp.bfloat16

D = 1024
N_HEADS = 16
HEAD_DIM = 64
CONV_WIDTH = 31
D_FF = 2816
GROUPS = ((128, 1), (512, 4), (2048, 16))
ATTN_BLOCK = 128
EPS = 1e-6
N_DEV = 8
MESH = pl.DeviceIdType.MESH

ADAM_LR = 0.001
ADAM_B1 = 0.9
ADAM_B2 = 0.999
ADAM_EPS = 1e-08
ADAM_WD = 0.01
ADAM_STEP = 10

VMEM_LIMIT = 56 * 1024 * 1024
MASK_BIAS = 1e30

Z_AVAL, Z_AGATE, Z_GA, Z_GB, Z_Q, Z_K, Z_V = 0, 1, 2, 3, 4, 5, 6


def _wsec_of_zsec(j):
    return jnp.where(j < 2, j, jnp.where(j < 4, j + 3, j - 2))


def _zsec_of_wsec(w):
    return jnp.where(w < 2, w, jnp.where(w < 5, w + 2, w - 3))


def _sig(x):
    return 1.0 / (1.0 + jnp.exp(-x))


def _colsum8(x):
    return x.reshape(-1, 8, x.shape[-1]).sum(axis=0)


def _cparams(sem):
    return pltpu.CompilerParams(dimension_semantics=sem, vmem_limit_bytes=VMEM_LIMIT)


def _my_pos():
    x, y, c = lax.axis_index("x"), lax.axis_index("y"), lax.axis_index("c")
    return x, y, c, 4 * x + 2 * y + c


_DIMS = {"nn": ((1,), (0,)), "nt": ((1,), (1,)), "tn": ((0,), (0,))}


def _matmul_call(name, a, b, a_spec, b_spec, o_spec, out_shape, grid, mode, nk, tm, tn):
    dims = (_DIMS[mode], ((), ()))

    def body(a_ref, b_ref, o_ref, *scratch):
        part = lax.dot_general(a_ref[...], b_ref[...], dims, preferred_element_type=f32)
        if nk == 1:
            o_ref[...] = part.astype(o_ref.dtype)
        else:
            acc = scratch[0]
            k = pl.program_id(2)

            @pl.when(k == 0)
            def _():
                acc[...] = part

            @pl.when(k > 0)
            def _():
                acc[...] += part

            @pl.when(k == nk - 1)
            def _():
                o_ref[...] = acc[...].astype(o_ref.dtype)

    scratch = [] if nk == 1 else [pltpu.VMEM((tm, tn), f32)]
    return pl.pallas_call(
        body, name=name, grid=grid, in_specs=[a_spec, b_spec], out_specs=o_spec, out_shape=out_shape,
        scratch_shapes=scratch, compiler_params=_cparams(("parallel", "parallel", "arbitrary")),
    )(a, b)


def _matmul(name, a, b, mode, out_dtype, tm=1024, tn=1024, tk=None):
    if mode == "nn":
        (M, K), (_, N) = a.shape, b.shape
    elif mode == "nt":
        (M, K), (N, _) = a.shape, b.shape
    else:
        (K, M), (_, N) = a.shape, b.shape
    tm, tn = min(tm, M), min(tn, N)
    tk = K if tk is None else tk
    nk = K // tk
    assert M % tm == 0 and N % tn == 0 and K % tk == 0
    if mode == "tn":
        a_spec = pl.BlockSpec((tk, tm), lambda i, j, k: (k, i))
    else:
        a_spec = pl.BlockSpec((tm, tk), lambda i, j, k: (i, k))
    if mode == "nt":
        b_spec = pl.BlockSpec((tn, tk), lambda i, j, k: (j, k))
    else:
        b_spec = pl.BlockSpec((tk, tn), lambda i, j, k: (k, j))
    o_spec = pl.BlockSpec((tm, tn), lambda i, j, k: (i, j))
    return _matmul_call(name, a, b, a_spec, b_spec, o_spec, jax.ShapeDtypeStruct((M, N), out_dtype),
                        (M // tm, N // tn, nk), mode, nk, tm, tn)


TT = 512


def _rows(c, cb=0, tt=TT):
    return pl.BlockSpec((tt, c), lambda i: (i, cb))


def _sec(s, tt=TT):
    return pl.BlockSpec((None, tt, D), lambda i: (s, i, 0))


def _const(shape):
    return pl.BlockSpec(shape, lambda i: (0,) * len(shape))


def _acc_spec(c):
    return pl.BlockSpec((8, c), lambda i: (0, 0))


def _rms(x):
    return lax.rsqrt(jnp.mean(x * x, axis=-1, keepdims=True) + EPS)


def _rms_bwd(dy_g, xn, rstd):
    return rstd * (dy_g - xn * jnp.mean(dy_g * xn, axis=-1, keepdims=True))


def _head_sum(x, bd):
    parts = []
    for cb in range(x.shape[-1] // 128):
        xb = x[:, cb * 128:(cb + 1) * 128]
        hi = xb.astype(bf16)
        lo = (xb - hi.astype(f32)).astype(bf16)
        parts.append(jnp.dot(hi, bd, preferred_element_type=f32) + jnp.dot(lo, bd, preferred_element_type=f32))
    return parts[0] if len(parts) == 1 else jnp.concatenate(parts, axis=1)


def _norm1_fwd(x, g):
    T = x.shape[0]

    def body(x_ref, g_ref, h_ref):
        xv = x_ref[...]
        h_ref[...] = (xv * _rms(xv) * g_ref[...]).astype(bf16)

    return pl.pallas_call(
        body, name="norm1_fwd", grid=(T // TT,), in_specs=[_rows(D), _const((1, D))], out_specs=_rows(D),
        out_shape=jax.ShapeDtypeStruct((T, D), bf16), compiler_params=_cparams(("parallel",)))(x, g)


def _convnorm_fwd(c, g):
    T = c.shape[0]

    def body(c_ref, g_ref, s_ref):
        cv = c_ref[...]
        r = cv * _rms(cv) * g_ref[...]
        s_ref[...] = (r * _sig(r)).astype(bf16)

    return pl.pallas_call(
        body, name="convnorm_fwd", grid=(T // TT,), in_specs=[_rows(D), _const((1, D))], out_specs=_rows(D),
        out_shape=jax.ShapeDtypeStruct((T, D), bf16), compiler_params=_cparams(("parallel",)))(c, g)


def _qk_fwd(z8, qg, kg, bd):
    T = z8.shape[1]

    def body(q_ref, k_ref, qg_ref, kg_ref, bd_ref, qn_ref, kn_ref):
        bdv = bd_ref[...]
        q = q_ref[...]
        qn_ref[...] = q * lax.rsqrt(_head_sum(q * q, bdv) * (1.0 / HEAD_DIM) + EPS) * qg_ref[...] * (HEAD_DIM ** -0.5)
        k = k_ref[...]
        kn_ref[...] = k * lax.rsqrt(_head_sum(k * k, bdv) * (1.0 / HEAD_DIM) + EPS) * kg_ref[...]

    return pl.pallas_call(
        body, name="qk_fwd", grid=(T // TT,),
        in_specs=[_sec(Z_Q), _sec(Z_K), _const((1, D)), _const((1, D)), _const((128, 128))],
        out_specs=[_rows(D), _rows(D)],
        out_shape=[jax.ShapeDtypeStruct((T, D), f32)] * 2, compiler_params=_cparams(("parallel",)))(z8, z8, qg, kg, bd)


def _gate_fwd(z8, gate_b, ya, yb):
    T = ya.shape[0]

    def body(ga_ref, gb_ref, b_ref, ya_ref, yb_ref, mixed_ref):
        g_a = _sig(ga_ref[...] + b_ref[:, :D])
        g_b = _sig(gb_ref[...] + b_ref[:, D:])
        mixed_ref[...] = (g_a * ya_ref[...] + g_b * yb_ref[...]).astype(bf16)

    return pl.pallas_call(
        body, name="gate_fwd", grid=(T // TT,),
        in_specs=[_sec(Z_GA), _sec(Z_GB), _const((1, 2 * D)), _rows(D), _rows(D)], out_specs=_rows(D),
        out_shape=jax.ShapeDtypeStruct((T, D), bf16), compiler_params=_cparams(("parallel",)))(z8, z8, gate_b, ya, yb)


def _norm2_fwd(x, t1, g):
    T = x.shape[0]

    def body(x_ref, t_ref, g_ref, x1_ref, h2_ref):
        x1 = x_ref[...] + t_ref[...]
        x1_ref[...] = x1
        h2_ref[...] = (x1 * _rms(x1) * g_ref[...]).astype(bf16)

    return pl.pallas_call(
        body, name="norm2_fwd", grid=(T // TT,), in_specs=[_rows(D), _rows(D), _const((1, D))],
        out_specs=[_rows(D), _rows(D)],
        out_shape=[jax.ShapeDtypeStruct((T, D), f32), jax.ShapeDtypeStruct((T, D), bf16)],
        compiler_params=_cparams(("parallel",)))(x, t1, g)


def _loss_fwd(x1, t2, target):
    T = x1.shape[0]

    def body(x1_ref, t_ref, tg_ref, dy_ref, dyb_ref, acc_ref):
        diff = x1_ref[...] + t_ref[...] - tg_ref[...]
        dy = diff * (1.0 / D)
        dy_ref[...] = dy
        dyb_ref[...] = dy.astype(bf16)

        @pl.when(pl.program_id(0) == 0)
        def _():
            acc_ref[...] = jnp.zeros_like(acc_ref)

        acc_ref[...] += _colsum8(diff * diff)

    return pl.pallas_call(
        body, name="loss_fwd", grid=(T // TT,), in_specs=[_rows(D)] * 3,
        out_specs=[_rows(D), _rows(D), _acc_spec(D)],
        out_shape=[jax.ShapeDtypeStruct((T, D), f32), jax.ShapeDtypeStruct((T, D), bf16),
                   jax.ShapeDtypeStruct((8, D), f32)],
        compiler_params=_cparams(("arbitrary",)))(x1, t2, target)


def _norm2_bwd(x1, dh2, dy, g):
    T = x1.shape[0]

    def body(x1_ref, dh_ref, dy_ref, g_ref, dx1_ref, dx1b_ref, dg_ref):
        x1 = x1_ref[...]
        rstd = _rms(x1)
        xn = x1 * rstd
        dh = dh_ref[...]
        dx1 = dy_ref[...] + _rms_bwd(dh * g_ref[...], xn, rstd)
        dx1_ref[...] = dx1
        dx1b_ref[...] = dx1.astype(bf16)

        @pl.when(pl.program_id(0) == 0)
        def _():
            dg_ref[...] = jnp.zeros_like(dg_ref)

        dg_ref[...] += _colsum8(dh * xn)

    return pl.pallas_call(
        body, name="norm2_bwd", grid=(T // TT,), in_specs=[_rows(D), _rows(D), _rows(D), _const((1, D))],
        out_specs=[_rows(D), _rows(D), _acc_spec(D)],
        out_shape=[jax.ShapeDtypeStruct((T, D), f32), jax.ShapeDtypeStruct((T, D), bf16),
                   jax.ShapeDtypeStruct((8, D), f32)],
        compiler_params=_cparams(("arbitrary",)))(x1, dh2, dy, g)


def _gate_bwd(dmixed, z8, gate_b, ya, yb, dz8):
    T = ya.shape[0]

    def body(dm_ref, ga_ref, gb_ref, b_ref, ya_ref, yb_ref, dz_in, dya_ref, dyb_ref, dz_ref, dgb_ref):
        del dz_in
        dm = dm_ref[...]
        g_a = _sig(ga_ref[...] + b_ref[:, :D])
        g_b = _sig(gb_ref[...] + b_ref[:, D:])
        dya_ref[...] = (dm * g_a).astype(bf16)
        dyb_ref[...] = (dm * g_b).astype(bf16)
        dla = dm * ya_ref[...] * g_a * (1.0 - g_a)
        dlb = dm * yb_ref[...] * g_b * (1.0 - g_b)
        dz_ref[0] = dla.astype(bf16)
        dz_ref[1] = dlb.astype(bf16)

        @pl.when(pl.program_id(0) == 0)
        def _():
            dgb_ref[...] = jnp.zeros_like(dgb_ref)

        dgb_ref[:, :D] += _colsum8(dla)
        dgb_ref[:, D:] += _colsum8(dlb)

    return pl.pallas_call(
        body, name="gate_bwd", grid=(T // TT,),
        in_specs=[_rows(D), _sec(Z_GA), _sec(Z_GB), _const((1, 2 * D)), _rows(D), _rows(D),
                  pl.BlockSpec(memory_space=pl.ANY)],
        out_specs=[_rows(D), _rows(D), pl.BlockSpec((2, TT, D), lambda i: (1, i, 0)), _acc_spec(2 * D)],
        out_shape=[jax.ShapeDtypeStruct((T, D), bf16), jax.ShapeDtypeStruct((T, D), bf16),
                   jax.ShapeDtypeStruct(dz8.shape, bf16), jax.ShapeDtypeStruct((8, 2 * D), f32)],
        input_output_aliases={6: 2},
        compiler_params=_cparams(("arbitrary",)))(dmixed, z8, z8, gate_b, ya, yb, dz8)


def _convnorm_bwd(c, ds, g):
    T = c.shape[0]

    def body(c_ref, ds_ref, g_ref, dc_ref, dg_ref):
        cv = c_ref[...]
        rstd = _rms(cv)
        r0 = cv * rstd
        gv = g_ref[...]
        r = r0 * gv
        sg = _sig(r)
        dr = ds_ref[...] * sg * (1.0 + r * (1.0 - sg))
        dc_ref[...] = _rms_bwd(dr * gv, r0, rstd)

        @pl.when(pl.program_id(0) == 0)
        def _():
            dg_ref[...] = jnp.zeros_like(dg_ref)

        dg_ref[...] += _colsum8(dr * r0)

    return pl.pallas_call(
        body, name="convnorm_bwd", grid=(T // TT,), in_specs=[_rows(D), _rows(D), _const((1, D))],
        out_specs=[_rows(D), _acc_spec(D)],
        out_shape=[jax.ShapeDtypeStruct((T, D), f32), jax.ShapeDtypeStruct((8, D), f32)],
        compiler_params=_cparams(("arbitrary",)))(c, ds, g)


def _qk_bwd(z8, dqn, dkn, dv, qg, kg, bd, dz8):
    T = dqn.shape[0]

    def body(q_ref, k_ref, dqn_ref, dkn_ref, dv_ref, qg_ref, kg_ref, bd_ref, dz_in, dz_ref, dqg_ref, dkg_ref):
        del dz_in
        bdv = bd_ref[...]

        @pl.when(pl.program_id(0) == 0)
        def _():
            dqg_ref[...] = jnp.zeros_like(dqg_ref)
            dkg_ref[...] = jnp.zeros_like(dkg_ref)

        def one(raw, dn_scaled, g, dg_ref, sec):
            rstd = lax.rsqrt(_head_sum(raw * raw, bdv) * (1.0 / HEAD_DIM) + EPS)
            n = raw * rstd
            dg_ref[...] += _colsum8(dn_scaled * n)
            dn = dn_scaled * g
            draw = rstd * (dn - n * (_head_sum(dn * n, bdv) * (1.0 / HEAD_DIM)))
            dz_ref[sec] = draw.astype(bf16)

        one(q_ref[...], dqn_ref[...] * (HEAD_DIM ** -0.5), qg_ref[...], dqg_ref, 0)
        one(k_ref[...], dkn_ref[...], kg_ref[...], dkg_ref, 1)
        dz_ref[2] = dv_ref[...].astype(bf16)
        dz_ref[3] = jnp.zeros((TT, D), bf16)

    return pl.pallas_call(
        body, name="qk_bwd", grid=(T // TT,),
        in_specs=[_sec(Z_Q), _sec(Z_K), _rows(D), _rows(D), _rows(D), _const((1, D)), _const((1, D)),
                  _const((128, 128)), pl.BlockSpec(memory_space=pl.ANY)],
        out_specs=[pl.BlockSpec((4, TT, D), lambda i: (1, i, 0)), _acc_spec(D), _acc_spec(D)],
        out_shape=[jax.ShapeDtypeStruct(dz8.shape, bf16), jax.ShapeDtypeStruct((8, D), f32),
                   jax.ShapeDtypeStruct((8, D), f32)],
        input_output_aliases={8: 0},
        compiler_params=_cparams(("arbitrary",)))(z8, z8, dqn, dkn, dv, qg, kg, bd, dz8)


def _norm1_bwd(x, dh, dx1, g):
    T = x.shape[0]

    def body(x_ref, dh_ref, dx1_ref, g_ref, gx_ref, dg_ref):
        xv = x_ref[...]
        rstd = _rms(xv)
        xn = xv * rstd
        dh = dh_ref[...]
        gx_ref[...] = dx1_ref[...] + _rms_bwd(dh * g_ref[...], xn, rstd)

        @pl.when(pl.program_id(0) == 0)
        def _():
            dg_ref[...] = jnp.zeros_like(dg_ref)

        dg_ref[...] += _colsum8(dh * xn)

    return pl.pallas_call(
        body, name="norm1_bwd", grid=(T // TT,), in_specs=[_rows(D), _rows(D), _rows(D), _const((1, D))],
        out_specs=[_rows(D), _acc_spec(D)],
        out_shape=[jax.ShapeDtypeStruct((T, D), f32), jax.ShapeDtypeStruct((8, D), f32)],
        compiler_params=_cparams(("arbitrary",)))(x, dh, dx1, g)


CCW = 256
CR = 64
HALO = 32


def _conv_fwd(z8, conv_w, conv_b, S):
    T = z8.shape[1]
    nb = T // S
    ncb = D // CCW

    def body(av_ref, ag_ref, w_ref, b_ref, c_ref, pad):
        pad[0:HALO, :] = jnp.zeros((HALO, CCW), f32)

        def fill(i, carry):
            r0 = pl.multiple_of(i * 256, 256)
            pad[pl.ds(HALO + r0, 256), :] = av_ref[pl.ds(r0, 256), :] * _sig(ag_ref[pl.ds(r0, 256), :])
            return carry

        lax.fori_loop(0, S // 256, fill, 0)
        bias = b_ref[...]

        def chunk(i, carry):
            r0 = pl.multiple_of(i * CR, CR)
            win = pad[pl.ds(r0, CR + HALO), :]
            acc = jnp.zeros((CR, CCW), f32) + bias
            for j in range(CONV_WIDTH):
                acc = acc + win[2 + j:2 + j + CR, :] * w_ref[j:j + 1, :]
            c_ref[pl.ds(r0, CR), :] = acc
            return carry

        lax.fori_loop(0, S // CR, chunk, 0)

    zs = lambda s: pl.BlockSpec((None, S, CCW), lambda b, cb: (s, b, cb))
    return pl.pallas_call(
        body, name="conv_fwd", grid=(nb, ncb),
        in_specs=[zs(Z_AVAL), zs(Z_AGATE), pl.BlockSpec((CONV_WIDTH, CCW), lambda b, cb: (0, cb)),
                  pl.BlockSpec((1, CCW), lambda b, cb: (0, cb))],
        out_specs=pl.BlockSpec((S, CCW), lambda b, cb: (b, cb)),
        out_shape=jax.ShapeDtypeStruct((T, D), f32),
        scratch_shapes=[pltpu.VMEM((S + HALO, CCW), f32)],
        compiler_params=_cparams(("parallel", "parallel")))(z8, z8, conv_w, conv_b)


def _conv_bwd(dc, z8, conv_w, dz8, S):
    T = dc.shape[0]
    nb = T // S
    ncb = D // CCW

    def body(dc_ref, av_ref, ag_ref, w_ref, dz_in, dz_ref, dw_ref, apad, dpad):
        del dz_in
        apad[0:HALO, :] = jnp.zeros((HALO, CCW), f32)
        dpad[S:S + HALO, :] = jnp.zeros((HALO, CCW), f32)
        dw_ref[...] = jnp.zeros_like(dw_ref)

        def fill(i, carry):
            r0 = pl.multiple_of(i * 256, 256)
            apad[pl.ds(HALO + r0, 256), :] = av_ref[pl.ds(r0, 256), :] * _sig(ag_ref[pl.ds(r0, 256), :])
            dpad[pl.ds(r0, 256), :] = dc_ref[pl.ds(r0, 256), :]
            return carry

        lax.fori_loop(0, S // 256, fill, 0)

        def chunk(i, carry):
            r0 = pl.multiple_of(i * CR, CR)
            awin = apad[pl.ds(r0, CR + HALO), :]
            dwin = dpad[pl.ds(r0, CR + HALO), :]
            dcc = dwin[0:CR, :]
            da = jnp.zeros((CR, CCW), f32)
            for j in range(CONV_WIDTH):
                da = da + dwin[30 - j:30 - j + CR, :] * w_ref[j:j + 1, :]
                dw_ref[8 * j:8 * j + 8, :] += _colsum8(dcc * awin[2 + j:2 + j + CR, :])
            dw_ref[8 * CONV_WIDTH:8 * CONV_WIDTH + 8, :] += _colsum8(dcc)
            av = av_ref[pl.ds(r0, CR), :]
            sg = _sig(ag_ref[pl.ds(r0, CR), :])
            dz_ref[0, pl.ds(r0, CR), :] = (da * sg).astype(bf16)
            dz_ref[1, pl.ds(r0, CR), :] = (da * av * sg * (1.0 - sg)).astype(bf16)
            return carry

        lax.fori_loop(0, S // CR, chunk, 0)

    zs = lambda s: pl.BlockSpec((None, S, CCW), lambda b, cb: (s, b, cb))
    return pl.pallas_call(
        body, name="conv_bwd", grid=(nb, ncb),
        in_specs=[pl.BlockSpec((S, CCW), lambda b, cb: (b, cb)), zs(Z_AVAL), zs(Z_AGATE),
                  pl.BlockSpec((CONV_WIDTH, CCW), lambda b, cb: (0, cb)), pl.BlockSpec(memory_space=pl.ANY)],
        out_specs=[pl.BlockSpec((2, S, CCW), lambda b, cb: (0, b, cb)),
                   pl.BlockSpec((None, 256, CCW), lambda b, cb: (b, 0, cb))],
        out_shape=[jax.ShapeDtypeStruct(dz8.shape, bf16), jax.ShapeDtypeStruct((nb, 256, D), f32)],
        input_output_aliases={4: 0},
        scratch_shapes=[pltpu.VMEM((S + HALO, CCW), f32), pltpu.VMEM((S + HALO, CCW), f32)],
        compiler_params=_cparams(("parallel", "parallel")))(dc, z8, z8, conv_w, dz8)


FR = 128
NFB = D_FF // CCW


def _ffn_window(ref, i, r0):
    return ref[pl.ds(r0 - 8, FR + 8), :]


def _ffn_u(win, w_ref, b_ref):
    return (win[6:6 + FR, :] * w_ref[0:1, :] + win[7:7 + FR, :] * w_ref[1:2, :]
            + win[8:8 + FR, :] * w_ref[2:3, :] + b_ref[...])


def _ffn_fwd(u3, ffn_w, ffn_b, S):
    T = u3.shape[1]
    nb = T // S

    def body(uv_ref, ug_ref, wv_ref, wg_ref, bv_ref, bg_ref, f_ref):
        def chunk(first, i):
            r0 = 0 if first else pl.multiple_of(i * FR, FR)
            if first:
                z = jnp.zeros((8, CCW), f32)
                wv = jnp.concatenate([z, uv_ref[0:FR, :]], axis=0)
                wg = jnp.concatenate([z, ug_ref[0:FR, :]], axis=0)
            else:
                wv = _ffn_window(uv_ref, i, r0)
                wg = _ffn_window(ug_ref, i, r0)
            u_val = _ffn_u(wv, wv_ref, bv_ref)
            u_gate = _ffn_u(wg, wg_ref, bg_ref)
            f_ref[pl.ds(r0, FR), :] = (u_gate * _sig(u_gate) * u_val).astype(bf16)

        chunk(True, 0)

        def loop(i, carry):
            chunk(False, i)
            return carry

        lax.fori_loop(1, S // FR, loop, 0)

    us = lambda h: pl.BlockSpec((None, S, CCW), lambda b, cb: (h, b, cb))
    ws = lambda h: pl.BlockSpec((3, CCW), lambda b, cb: (0, h * NFB + cb))
    bs = lambda h: pl.BlockSpec((1, CCW), lambda b, cb: (0, h * NFB + cb))
    return pl.pallas_call(
        body, name="ffn_fwd", grid=(nb, NFB),
        in_specs=[us(0), us(1), ws(0), ws(1), bs(0), bs(1)],
        out_specs=pl.BlockSpec((S, CCW), lambda b, cb: (b, cb)),
        out_shape=jax.ShapeDtypeStruct((T, D_FF), bf16),
        compiler_params=_cparams(("parallel", "parallel")))(u3, u3, ffn_w, ffn_w, ffn_b, ffn_b)


def _ffn_bwd(u3, df, ffn_w, ffn_b, S):
    T = u3.shape[1]
    nb = T // S

    def body(uv_ref, ug_ref, df_ref, wv_ref, wg_ref, bv_ref, bg_ref, du_ref, dw_ref, dvpad, dgpad):
        dvpad[S:S + 8, :] = jnp.zeros((8, CCW), f32)
        dgpad[S:S + 8, :] = jnp.zeros((8, CCW), f32)
        dw_ref[...] = jnp.zeros_like(dw_ref)

        def chunk(first, i):
            r0 = 0 if first else pl.multiple_of(i * FR, FR)
            if first:
                z = jnp.zeros((8, CCW), f32)
                wv = jnp.concatenate([z, uv_ref[0:FR, :]], axis=0)
                wg = jnp.concatenate([z, ug_ref[0:FR, :]], axis=0)
            else:
                wv = _ffn_window(uv_ref, i, r0)
                wg = _ffn_window(ug_ref, i, r0)
            u_val = _ffn_u(wv, wv_ref, bv_ref)
            u_gate = _ffn_u(wg, wg_ref, bg_ref)
            dfc = df_ref[pl.ds(r0, FR), :]
            sg = _sig(u_gate)
            d_val = dfc * u_gate * sg
            d_gate = dfc * u_val * sg * (1.0 + u_gate * (1.0 - sg))
            dvpad[pl.ds(r0, FR), :] = d_val
            dgpad[pl.ds(r0, FR), :] = d_gate
            for h, (dd, win) in enumerate(((d_val, wv), (d_gate, wg))):
                for j in range(3):
                    dw_ref[h, 8 * j:8 * j + 8, :] += _colsum8(dd * win[6 + j:6 + j + FR, :])
                dw_ref[h, 24:32, :] += _colsum8(dd)

        chunk(True, 0)

        def loop(i, carry):
            chunk(False, i)
            return carry

        lax.fori_loop(1, S // FR, loop, 0)

        def back(i, carry):
            r0 = pl.multiple_of(i * FR, FR)
            for h, (dpad, w_ref) in enumerate(((dvpad, wv_ref), (dgpad, wg_ref))):
                win = dpad[pl.ds(r0, FR + 8), :]
                du = (win[0:FR, :] * w_ref[2:3, :] + win[1:1 + FR, :] * w_ref[1:2, :]
                      + win[2:2 + FR, :] * w_ref[0:1, :])
                du_ref[h, pl.ds(r0, FR), :] = du.astype(bf16)
            return carry

        lax.fori_loop(0, S // FR, back, 0)

    us = lambda h: pl.BlockSpec((None, S, CCW), lambda b, cb: (h, b, cb))
    ws = lambda h: pl.BlockSpec((3, CCW), lambda b, cb: (0, h * NFB + cb))
    bs = lambda h: pl.BlockSpec((1, CCW), lambda b, cb: (0, h * NFB + cb))
    return pl.pallas_call(
        body, name="ffn_bwd", grid=(nb, NFB),
        in_specs=[us(0), us(1), pl.BlockSpec((S, CCW), lambda b, cb: (b, cb)), ws(0), ws(1), bs(0), bs(1)],
        out_specs=[pl.BlockSpec((2, S, CCW), lambda b, cb: (0, b, cb)),
                   pl.BlockSpec((None, 2, 32, CCW), lambda b, cb: (b, 0, 0, cb))],
        out_shape=[jax.ShapeDtypeStruct((2, T, D_FF), bf16), jax.ShapeDtypeStruct((nb, 2, 32, D_FF), f32)],
        scratch_shapes=[pltpu.VMEM((S + 8, CCW), f32), pltpu.VMEM((S + 8, CCW), f32)],
        compiler_params=_cparams(("parallel", "parallel")))(u3, u3, df, ffn_w, ffn_w, ffn_b, ffn_b)


AB = ATTN_BLOCK


def _attn_bias():
    slopes = 2.0 ** (-8.0 * jnp.arange(1, N_HEADS + 1, dtype=f32) / N_HEADS)
    steps = (jnp.arange(AB)[:, None] + AB) - jnp.arange(2 * AB)[None, :]
    own = (jnp.arange(2 * AB) >= AB)[None, :]
    out = []
    for window, dil in GROUPS:
        valid = (steps >= 0) & (steps <= window // dil)
        dist = slopes[:, None, None] * (steps * dil).astype(f32)[None]
        kinds = [jnp.where(v[None], dist, MASK_BIAS) for v in (valid, valid & own)]
        out.append(jnp.stack(kinds, axis=1))
    return jnp.stack(out)


def _head_masks():
    lane = lax.broadcasted_iota(jnp.int32, (1, 128), 1)
    return (lane < HEAD_DIM, lane >= HEAD_DIM)


def _perm_chunks(S, d):
    L = S // d
    ch = min(L, 256)
    out = []
    for r in range(d):
        for c in range(L // ch):
            start = r + d * ch * c
            out.append((pl.ds(start, ch, stride=d) if d > 1 else pl.ds(start, ch), r * L + c * ch, ch))
    return out


def _stack_heads(x, masks):
    return jnp.concatenate([jnp.where(masks[0], x, 0), jnp.where(masks[1], x, 0)], axis=0)


_NT = (((1,), (1,)), ((), ()))
_TN = (((0,), (0,)), ((), ()))
SCH = 64


def _attn_fwd(qn, kn, z8, bias, S):
    T = qn.shape[0]
    nb = T // S
    nblk = S // AB

    def body(q_ref, k_ref, v_ref, bias_ref, o_ref, ob_ref, lse_ref, qs, ks, vs, s2, p2, ogp, lgp, *group_scratch):
        og, lg = group_scratch[:3], group_scratch[3:]
        masks = _head_masks()
        ks[0:AB, :] = jnp.zeros((AB, 128), bf16)
        vs[0:AB, :] = jnp.zeros((AB, 128), bf16)

        for g, (_, d) in enumerate(GROUPS):
            nsub = S // (d * AB)
            chunks = _perm_chunks(S, d)
            for src, dst, ch in chunks:
                qs[dst:dst + ch, :] = q_ref[src, :].astype(bf16)
                ks[AB + dst:AB + dst + ch, :] = k_ref[src, :].astype(bf16)
                vs[AB + dst:AB + dst + ch, :] = v_ref[src, :].astype(bf16)
            od, ld = (og[g], lg[g]) if d == 1 else (ogp, lgp)

            def scores(j, carry):
                r0 = pl.multiple_of(j * AB, AB)
                q2 = _stack_heads(qs[pl.ds(r0, AB), :], masks)
                s2[j] = lax.dot_general(q2, ks[pl.ds(r0, 2 * AB), :], _NT, preferred_element_type=f32)
                return carry

            lax.fori_loop(0, nblk, scores, 0, unroll=8)

            def softmax(j, carry, g=g, nsub=nsub, ld=ld):
                r0 = pl.multiple_of(j * AB, AB)
                kind = (j % nsub == 0).astype(jnp.int32)
                for cc in range(AB // SCH):
                    lses = []
                    for hh in range(2):
                        rows = pl.ds(hh * AB + cc * SCH, SCH)
                        sb = s2[j, rows, :] - bias_ref[g, hh, kind, cc * SCH:(cc + 1) * SCH, :]
                        m = jnp.max(sb, axis=-1, keepdims=True)
                        p = jnp.exp(sb - m)
                        den = jnp.sum(p, axis=-1, keepdims=True)
                        p2[j, rows, :] = (p * (1.0 / den)).astype(bf16)
                        lses.append(m + jnp.log(den))
                    ld[pl.ds(r0 + cc * SCH, SCH), :] = jnp.where(masks[0], lses[0], lses[1])
                return carry

            lax.fori_loop(0, nblk, softmax, 0, unroll=2)

            def values(j, carry, od=od):
                r0 = pl.multiple_of(j * AB, AB)
                pv2 = jnp.dot(p2[j], vs[pl.ds(r0, 2 * AB), :], preferred_element_type=f32)
                od[pl.ds(r0, AB), :] = jnp.where(masks[0], pv2[:AB], pv2[AB:])
                return carry

            lax.fori_loop(0, nblk, values, 0, unroll=8)

            if d > 1:
                for src, dst, ch in chunks:
                    og[g][src, :] = ogp[dst:dst + ch, :]
                    lg[g][src, :] = lgp[dst:dst + ch, :]

        def combine(i, carry):
            rr = pl.ds(pl.multiple_of(i * 256, 256), 256)
            l0, l1, l2 = lg[0][rr, :], lg[1][rr, :], lg[2][rr, :]
            mx = jnp.maximum(jnp.maximum(l0, l1), l2)
            e0, e1, e2 = jnp.exp(l0 - mx), jnp.exp(l1 - mx), jnp.exp(l2 - mx)
            den = e0 + e1 + e2
            o = (e0 * og[0][rr, :] + e1 * og[1][rr, :] + e2 * og[2][rr, :]) / den
            o_ref[rr, :] = o
            ob_ref[rr, :] = o.astype(bf16)
            lse_ref[rr, :] = mx + jnp.log(den)
            return carry

        lax.fori_loop(0, S // 256, combine, 0)

    blk = pl.BlockSpec((S, 128), lambda b, hp: (b, hp))
    return pl.pallas_call(
        body, name="attn_fwd", grid=(nb, N_HEADS // 2),
        in_specs=[blk, blk, pl.BlockSpec((None, S, 128), lambda b, hp: (Z_V, b, hp)),
                  pl.BlockSpec((3, 2, 2, AB, 2 * AB), lambda b, hp: (0, hp, 0, 0, 0))],
        out_specs=[blk, blk, blk],
        out_shape=[jax.ShapeDtypeStruct((T, D), f32), jax.ShapeDtypeStruct((T, D), bf16),
                   jax.ShapeDtypeStruct((T, D), f32)],
        scratch_shapes=[pltpu.VMEM((S, 128), bf16), pltpu.VMEM((S + AB, 128), bf16), pltpu.VMEM((S + AB, 128), bf16),
                        pltpu.VMEM((nblk, 2 * AB, 2 * AB), f32), pltpu.VMEM((nblk, 2 * AB, 2 * AB), bf16),
                        pltpu.VMEM((S, 128), f32), pltpu.VMEM((S, 128), f32)] + [pltpu.VMEM((S, 128), f32)] * 6,
        compiler_params=_cparams(("parallel", "parallel")))(qn, kn, z8, bias)


def _attn_bwd(qn, kn, z8, do, o, lse, bias, bd, S):
    T = qn.shape[0]
    nb = T // S

    nblk = S // AB

    def body(q_ref, k_ref, v_ref, do_ref, o_ref, lse_ref, bias_ref, bd_ref, dq_ref, dk_ref, dv_ref,
             delta, qs, ks, vs, dos, lsp, dlp, s2, dp2, p2, ds2, dqp, dkp, dvp):
        masks = _head_masks()
        bdv = bd_ref[...]
        dq_ref[...] = jnp.zeros_like(dq_ref)
        dk_ref[...] = jnp.zeros_like(dk_ref)
        dv_ref[...] = jnp.zeros_like(dv_ref)
        ks[0:AB, :] = jnp.zeros((AB, 128), bf16)
        vs[0:AB, :] = jnp.zeros((AB, 128), bf16)

        def prep(i, carry):
            rr = pl.ds(pl.multiple_of(i * 256, 256), 256)
            delta[rr, :] = _head_sum(do_ref[rr, :] * o_ref[rr, :], bdv)
            return carry

        lax.fori_loop(0, S // 256, prep, 0)

        for g, (_, d) in enumerate(GROUPS):
            nsub = S // (d * AB)
            chunks = _perm_chunks(S, d)
            for src, dst, ch in chunks:
                qs[dst:dst + ch, :] = q_ref[src, :].astype(bf16)
                ks[AB + dst:AB + dst + ch, :] = k_ref[src, :].astype(bf16)
                vs[AB + dst:AB + dst + ch, :] = v_ref[src, :].astype(bf16)
                dos[dst:dst + ch, :] = do_ref[src, :].astype(bf16)
                lsp[dst:dst + ch, :] = lse_ref[src, :]
                dlp[dst:dst + ch, :] = delta[src, :]
            dkp[...] = jnp.zeros_like(dkp)
            dvp[...] = jnp.zeros_like(dvp)

            def scores(j, carry):
                r0 = pl.multiple_of(j * AB, AB)
                q2 = _stack_heads(qs[pl.ds(r0, AB), :], masks)
                do2 = _stack_heads(dos[pl.ds(r0, AB), :], masks)
                s2[j] = lax.dot_general(q2, ks[pl.ds(r0, 2 * AB), :], _NT, preferred_element_type=f32)
                dp2[j] = lax.dot_general(do2, vs[pl.ds(r0, 2 * AB), :], _NT, preferred_element_type=f32)
                return carry

            lax.fori_loop(0, nblk, scores, 0, unroll=8)

            def probs(j, carry, g=g, nsub=nsub):
                r0 = pl.multiple_of(j * AB, AB)
                kind = (j % nsub == 0).astype(jnp.int32)
                for cc in range(AB // SCH):
                    lse_c = lsp[pl.ds(r0 + cc * SCH, SCH), :]
                    del_c = dlp[pl.ds(r0 + cc * SCH, SCH), :]
                    for hh in range(2):
                        c0 = hh * HEAD_DIM
                        rows = pl.ds(hh * AB + cc * SCH, SCH)
                        sb = s2[j, rows, :] - bias_ref[g, hh, kind, cc * SCH:(cc + 1) * SCH, :]
                        p = jnp.exp(sb - lse_c[:, c0:c0 + 1])
                        p2[j, rows, :] = p.astype(bf16)
                        ds2[j, rows, :] = (p * (dp2[j, rows, :] - del_c[:, c0:c0 + 1])).astype(bf16)
                return carry

            lax.fori_loop(0, nblk, probs, 0, unroll=2)

            def grads(j, carry):
                r0 = pl.multiple_of(j * AB, AB)
                q2 = _stack_heads(qs[pl.ds(r0, AB), :], masks)
                do2 = _stack_heads(dos[pl.ds(r0, AB), :], masks)
                dsb = ds2[j]
                t = jnp.dot(dsb, ks[pl.ds(r0, 2 * AB), :], preferred_element_type=f32)
                dqp[pl.ds(r0, AB), :] = jnp.where(masks[0], t[:AB], t[AB:])
                dkp[pl.ds(r0, 2 * AB), :] += lax.dot_general(dsb, q2, _TN, preferred_element_type=f32)
                dvp[pl.ds(r0, 2 * AB), :] += lax.dot_general(p2[j], do2, _TN, preferred_element_type=f32)
                return carry

            lax.fori_loop(0, nblk, grads, 0, unroll=4)

            for src, dst, ch in chunks:
                dq_ref[src, :] += dqp[dst:dst + ch, :]
                dk_ref[src, :] += dkp[AB + dst:AB + dst + ch, :]
                dv_ref[src, :] += dvp[AB + dst:AB + dst + ch, :]

    blk = pl.BlockSpec((S, 128), lambda b, hp: (b, hp))
    row = lambda dt, pad=0: pltpu.VMEM((S + pad, 128), dt)
    blocks = lambda dt: pltpu.VMEM((nblk, 2 * AB, 2 * AB), dt)
    return pl.pallas_call(
        body, name="attn_bwd", grid=(nb, N_HEADS // 2),
        in_specs=[blk, blk, pl.BlockSpec((None, S, 128), lambda b, hp: (Z_V, b, hp)), blk, blk, blk,
                  pl.BlockSpec((3, 2, 2, AB, 2 * AB), lambda b, hp: (0, hp, 0, 0, 0)),
                  pl.BlockSpec((128, 128), lambda b, hp: (0, 0))],
        out_specs=[blk, blk, blk],
        out_shape=[jax.ShapeDtypeStruct((T, D), f32)] * 3,
        scratch_shapes=[row(f32), row(bf16), row(bf16, AB), row(bf16, AB), row(bf16), row(f32), row(f32),
                        blocks(f32), blocks(f32), blocks(bf16), blocks(bf16), row(f32), row(f32, AB), row(f32, AB)],
        compiler_params=_cparams(("parallel", "parallel")))(qn, kn, z8, do, o, lse, bias, bd)


def _any_spec():
    return pl.BlockSpec(memory_space=pl.ANY)


def _allgather_rows(shards):
    n = len(shards)

    def body(*refs):
        ins, outs = refs[:n], refs[n:2 * n]
        send_sems, recv_sems, local_sems = refs[2 * n:]
        x, y, c, me = _my_pos()
        sibling = (x, y, 1 - c)
        chips = [(1 - x, y), (x, 1 - y), (1 - x, 1 - y)]

        def idx(px, py, pc):
            return 4 * px + 2 * py + pc

        def copy(a, k, blk, to, src=None):
            return pltpu.make_async_remote_copy(
                src_ref=outs[a].at[blk] if src is None else src, dst_ref=outs[a].at[blk],
                send_sem=send_sems.at[a, k], recv_sem=recv_sems.at[a, k], device_id=to, device_id_type=MESH)

        mine = [pltpu.make_async_copy(ins[a], outs[a].at[me], local_sems.at[a]) for a in range(n)]
        for cp in mine:
            cp.start()
        first = []
        for a in range(n):
            first.append(copy(a, 0, me, sibling, src=ins[a]))
            first += [copy(a, 1 + j, me, (*chip, c), src=ins[a]) for j, chip in enumerate(chips)]
        for cp in first:
            cp.start()
        passed = []
        for a in range(n):
            for j, chip in enumerate(chips):
                blk = idx(*chip, c)
                copy(a, 1 + j, blk, (x, y, c)).wait_recv()
                cp = copy(a, 4 + j, blk, sibling)
                cp.start()
                passed.append(cp)
        for a in range(n):
            copy(a, 0, idx(x, y, 1 - c), (x, y, c)).wait_recv()
            for j, chip in enumerate(chips):
                copy(a, 4 + j, idx(*chip, 1 - c), (x, y, c)).wait_recv()
        for cp in first + passed:
            cp.wait_send()
        for cp in mine:
            cp.wait()

    return pl.pallas_call(
        body, name="allgather_weights",
        in_specs=[_any_spec()] * n, out_specs=[_any_spec()] * n,
        out_shape=[jax.ShapeDtypeStruct((N_DEV,) + s.shape, s.dtype) for s in shards],
        scratch_shapes=[pltpu.SemaphoreType.DMA((n, 7)), pltpu.SemaphoreType.DMA((n, 7)),
                        pltpu.SemaphoreType.DMA((n,))],
    )(*shards)


def _peer(x, y, c, k):
    tx = 1 - x if (k >> 2) & 1 else x
    ty = 1 - y if (k >> 1) & 1 else y
    tc = 1 - c if k & 1 else c
    return (tx, ty, tc), 4 * tx + 2 * ty + tc


_PEER_ORDER = (2, 4, 6, 3, 5, 7, 1)


def _scatter_blocks(grads):
    n = len(grads)

    def body(*refs):
        ins, outs = refs[:n], refs[n:2 * n]
        send_sems, recv_sems, local_sems = refs[2 * n:]
        x, y, c, me = _my_pos()
        mine = [pltpu.make_async_copy(ins[a].at[me], outs[a].at[me], local_sems.at[a]) for a in range(n)]
        for cp in mine:
            cp.start()
        copies = []
        for k in _PEER_ORDER:
            tgt, tidx = _peer(x, y, c, k)
            for a in range(n):
                cp = pltpu.make_async_remote_copy(
                    src_ref=ins[a].at[tidx], dst_ref=outs[a].at[me],
                    send_sem=send_sems.at[a, k - 1], recv_sem=recv_sems.at[a, k - 1],
                    device_id=tgt, device_id_type=MESH)
                cp.start()
                copies.append(cp)
        for cp in copies:
            cp.wait()
        for cp in mine:
            cp.wait()

    return pl.pallas_call(
        body, name="scatter_grads",
        in_specs=[_any_spec()] * n, out_specs=[_any_spec()] * n,
        out_shape=[jax.ShapeDtypeStruct(g.shape, g.dtype) for g in grads],
        scratch_shapes=[pltpu.SemaphoreType.DMA((n, 7)), pltpu.SemaphoreType.DMA((n, 7)),
                        pltpu.SemaphoreType.DMA((n,))],
    )(*grads)


SMALL_ROWS = 64


def _allreduce_small(name, sg):
    def body(sg_ref, out_ref, buf, send_sems, recv_sems):
        x, y, c, me = _my_pos()
        buf[me] = sg_ref[...]
        copies = []
        for k in _PEER_ORDER:
            tgt, _ = _peer(x, y, c, k)
            cp = pltpu.make_async_remote_copy(
                src_ref=sg_ref, dst_ref=buf.at[me], send_sem=send_sems.at[k - 1], recv_sem=recv_sems.at[k - 1],
                device_id=tgt, device_id_type=MESH)
            cp.start()
            copies.append(cp)
        for cp in copies:
            cp.wait()
        acc = buf[0]
        for p in range(1, N_DEV):
            acc = acc + buf[p]
        out_ref[...] = acc

    return pl.pallas_call(
        body, name=name,
        in_specs=[pl.BlockSpec(memory_space=pltpu.VMEM)], out_specs=pl.BlockSpec(memory_space=pltpu.VMEM),
        out_shape=jax.ShapeDtypeStruct(sg.shape, f32),
        scratch_shapes=[pltpu.VMEM((N_DEV,) + sg.shape, f32), pltpu.SemaphoreType.DMA((7,)),
                        pltpu.SemaphoreType.DMA((7,))],
    )(sg)


def _adam_math(g, w, m, v):
    m = ADAM_B1 * m + (1.0 - ADAM_B1) * g
    v = ADAM_B2 * v + (1.0 - ADAM_B2) * (g * g)
    m_hat = m / (1.0 - ADAM_B1 ** ADAM_STEP)
    v_hat = v / (1.0 - ADAM_B2 ** ADAM_STEP)
    delta = -ADAM_LR * (m_hat / (jnp.sqrt(v_hat) + ADAM_EPS) + ADAM_WD * w)
    return delta, m, v


def _adam_slots(name, slots, w, m, v, tr):
    rows = w.shape[0]

    def body(s_ref, w_ref, m_ref, v_ref, g_ref, d_ref, nm_ref, nv_ref):
        g = s_ref[0].astype(f32)
        for p in range(1, N_DEV):
            g = g + s_ref[p].astype(f32)
        delta, nm, nv = _adam_math(g, w_ref[...], m_ref[...], v_ref[...])
        g_ref[...] = g
        d_ref[...] = delta
        nm_ref[...] = nm
        nv_ref[...] = nv

    rs = pl.BlockSpec((tr, D), lambda i: (i, 0))
    return pl.pallas_call(
        body, name=name, grid=(rows // tr,),
        in_specs=[pl.BlockSpec((N_DEV, tr, D), lambda i: (0, i, 0)), rs, rs, rs], out_specs=[rs] * 4,
        out_shape=[jax.ShapeDtypeStruct((rows, D), f32)] * 4,
        compiler_params=_cparams(("parallel",)))(slots, w, m, v)


def _adam_small(g, w, m, v):
    def body(g_ref, w_ref, m_ref, v_ref, d_ref, nm_ref, nv_ref):
        delta, nm, nv = _adam_math(g_ref[...], w_ref[...], m_ref[...], v_ref[...])
        d_ref[...] = delta
        nm_ref[...] = nm
        nv_ref[...] = nv

    return pl.pallas_call(body, name="adam_small", out_shape=[jax.ShapeDtypeStruct(g.shape, f32)] * 3)(g, w, m, v)


FFN_PAD = 6 * D


def _pack_small(norm1_g, gate_b, conv_w, conv_b, conv_norm_g, q_norm_g, k_norm_g, norm2_g, ffn_conv_w, ffn_conv_b):
    pad_h = lambda a: jnp.pad(a, ((0, 0), (0, D - HEAD_DIM)))
    pad_f = lambda a: jnp.pad(a, ((0, 0), (0, FFN_PAD - 2 * D_FF))).reshape(-1, D)
    parts = [norm1_g, gate_b.reshape(2, D), conv_w, conv_b, conv_norm_g, pad_h(q_norm_g), pad_h(k_norm_g), norm2_g,
             pad_f(ffn_conv_w), pad_f(ffn_conv_b)]
    out = jnp.concatenate(parts, axis=0)
    return jnp.pad(out, ((0, SMALL_ROWS - out.shape[0]), (0, 0)))


def _unpack_small(p):
    ffn = lambda a: a.reshape(-1, FFN_PAD)[:, :2 * D_FF]
    return dict(
        norm1_g=p[0:1], gate_b=p[1:3].reshape(1, 2 * D), conv_w=p[3:34], conv_b=p[34:35], conv_norm_g=p[35:36],
        q_norm_g=p[36:37, :HEAD_DIM], k_norm_g=p[37:38, :HEAD_DIM], norm2_g=p[38:39],
        ffn_conv_w=ffn(p[39:57]), ffn_conv_b=ffn(p[57:63]))


_ADAM_TILE = {896: 128, 704: 64, 128: 128, 352: 176}


def kernel(x, norm1_g, w_in, gate_b, conv_w, conv_b, conv_norm_g, w_conv_out, q_norm_g, k_norm_g, w_attn_out, w_out, norm2_g, w_up, ffn_conv_w, ffn_conv_b, w_down, loss_target, m_norm1_g, m_w_in, m_gate_b, m_conv_w, m_conv_b, m_conv_norm_g, m_w_conv_out, m_q_norm_g, m_k_norm_g, m_w_attn_out, m_w_out, m_norm2_g, m_w_up, m_ffn_conv_w, m_ffn_conv_b, m_w_down, v_norm1_g, v_w_in, v_gate_b, v_conv_w, v_conv_b, v_conv_norm_g, v_w_conv_out, v_q_norm_g, v_k_norm_g, v_w_attn_out, v_w_out, v_norm2_g, v_w_up, v_ffn_conv_w, v_ffn_conv_b, v_w_down):
    BL, S, _ = x.shape
    T = BL * S
    me = 4 * lax.axis_index("x") + 2 * lax.axis_index("y") + lax.axis_index("c")
    xt = x.reshape(T, D)
    target = loss_target.reshape(T, D)

    big = dict(w_in=(w_in[0].T, m_w_in[0].T, v_w_in[0].T), w_up=(w_up[0].T, m_w_up[0].T, v_w_up[0].T),
               w_conv_out=(w_conv_out[0], m_w_conv_out[0], v_w_conv_out[0]),
               w_attn_out=(w_attn_out[0], m_w_attn_out[0], v_w_attn_out[0]),
               w_out=(w_out[0], m_w_out[0], v_w_out[0]), w_down=(w_down[0], m_w_down[0], v_w_down[0]))
    order = ["w_in", "w_conv_out", "w_attn_out", "w_out", "w_up", "w_down"]
    gathered = _allgather_rows([big[n][0].astype(bf16) for n in order])
    W = {n: g.reshape(-1, D) for n, g in zip(order, gathered)}

    def place_cols(shard, full_cols):
        z = jnp.zeros((shard.shape[0], full_cols), f32)
        return lax.dynamic_update_slice(z, shard, (0, me * shard.shape[1]))

    zr = lambda a: jnp.zeros_like(a)
    conv_local = _pack_small(
        zr(norm1_g), zr(gate_b), place_cols(conv_w[0], D), zr(conv_b), zr(conv_norm_g), zr(q_norm_g), zr(k_norm_g),
        zr(norm2_g), place_cols(ffn_conv_w[0], 2 * D_FF), zr(ffn_conv_b))
    conv_all = _unpack_small(_allreduce_small("gather_conv_weights", conv_local))
    conv_w_full, ffn_w_full = conv_all["conv_w"], conv_all["ffn_conv_w"]

    bd = (jnp.arange(128)[:, None] // HEAD_DIM == jnp.arange(128)[None, :] // HEAD_DIM).astype(bf16)
    bias = _attn_bias()
    qg = jnp.tile(q_norm_g, (1, N_HEADS))
    kg = jnp.tile(k_norm_g, (1, N_HEADS))

    h = _norm1_fwd(xt, norm1_g)
    z8 = _matmul_call(
        "mm_z", h, W["w_in"],
        pl.BlockSpec((1024, D), lambda i, j, k: (i, 0)),
        pl.BlockSpec((1024, D), lambda i, j, k: (_wsec_of_zsec(j), 0)),
        pl.BlockSpec((None, 1024, D), lambda i, j, k: (j, i, 0)),
        jax.ShapeDtypeStruct((8, T, D), f32), (T // 1024, 7, 1), "nt", 1, 1024, 1024)
    c = _conv_fwd(z8, conv_w_full, conv_b, S)
    s = _convnorm_fwd(c, conv_norm_g)
    ya = _matmul("mm_ya", s, W["w_conv_out"], "nn", f32)
    qn, kn = _qk_fwd(z8, qg, kg, bd)
    o, ob, lse = _attn_fwd(qn, kn, z8, bias, S)
    yb = _matmul("mm_yb", ob, W["w_attn_out"], "nn", f32)
    mixed = _gate_fwd(z8, gate_b, ya, yb)
    t1 = _matmul("mm_t1", mixed, W["w_out"], "nn", f32)
    x1, h2 = _norm2_fwd(xt, t1, norm2_g)
    TNU = D_FF // 2
    u3 = _matmul_call(
        "mm_u", h2, W["w_up"],
        pl.BlockSpec((1024, D), lambda i, j, k: (i, 0)),
        pl.BlockSpec((TNU, D), lambda i, j, k: (j, 0)),
        pl.BlockSpec((None, 1024, TNU), lambda i, j, k: (j // 2, i, j % 2)),
        jax.ShapeDtypeStruct((2, T, D_FF), f32), (T // 1024, 4, 1), "nt", 1, 1024, TNU)
    f = _ffn_fwd(u3, ffn_w_full, ffn_conv_b, S)
    t2 = _matmul("mm_t2", f, W["w_down"], "nn", f32, tk=TNU)
    dy, dyb, lacc = _loss_fwd(x1, t2, target)
    loss = lax.psum(0.5 / D * jnp.sum(lacc), ("x", "y", "c"))

    df = _matmul("mm_df", dyb, W["w_down"], "nt", f32, tn=TNU)
    g_w_down = _matmul("mm_dwdn", f, dyb, "tn", bf16, tm=TNU, tk=1024)
    du3, dffn = _ffn_bwd(u3, df, ffn_w_full, ffn_conv_b, S)
    dh2 = _matmul_call(
        "mm_dh2", du3, W["w_up"],
        pl.BlockSpec((None, 1024, TNU), lambda i, j, k: (k // 2, i, k % 2)),
        pl.BlockSpec((TNU, D), lambda i, j, k: (k, 0)),
        pl.BlockSpec((1024, D), lambda i, j, k: (i, 0)),
        jax.ShapeDtypeStruct((T, D), f32), (T // 1024, 1, 4), "nn", 4, 1024, D)
    g_w_up = _matmul_call(
        "mm_dwup", du3, h2,
        pl.BlockSpec((None, 1024, TNU), lambda i, j, k: (i // 2, k, i % 2)),
        pl.BlockSpec((1024, D), lambda i, j, k: (k, 0)),
        pl.BlockSpec((TNU, D), lambda i, j, k: (i, 0)),
        jax.ShapeDtypeStruct((2 * D_FF, D), bf16), (4, 1, T // 1024), "tn", T // 1024, TNU, D)
    dx1, dx1b, dg_norm2 = _norm2_bwd(x1, dh2, dy, norm2_g)
    dmixed = _matmul("mm_dmixed", dx1b, W["w_out"], "nt", f32)
    g_w_out = _matmul("mm_dwo", mixed, dx1b, "tn", bf16, tk=1024)
    dz8 = lax.empty((8, T, D), bf16)
    dya, dyb2, dz8, dg_gate = _gate_bwd(dmixed, z8, gate_b, ya, yb, dz8)
    ds = _matmul("mm_ds", dya, W["w_conv_out"], "nt", f32)
    g_w_conv_out = _matmul("mm_dwco", s, dya, "tn", bf16, tk=1024)
    do = _matmul("mm_do", dyb2, W["w_attn_out"], "nt", f32)
    g_w_attn_out = _matmul("mm_dwao", ob, dyb2, "tn", bf16, tk=1024)
    dc, dg_convnorm = _convnorm_bwd(c, ds, conv_norm_g)
    dz8a, dconv = _conv_bwd(dc, z8, conv_w_full, dz8, S)
    dqn, dkn, dv = _attn_bwd(qn, kn, z8, do, o, lse, bias, bd, S)
    dz8b, dg_q, dg_k = _qk_bwd(z8, dqn, dkn, dv, qg, kg, bd, dz8a)
    dh = _matmul_call(
        "mm_dh", dz8b, W["w_in"],
        pl.BlockSpec((None, 1024, D), lambda i, j, k: (k, i, 0)),
        pl.BlockSpec((1024, D), lambda i, j, k: (_wsec_of_zsec(k), 0)),
        pl.BlockSpec((1024, D), lambda i, j, k: (i, 0)),
        jax.ShapeDtypeStruct((T, D), f32), (T // 1024, 1, 7), "nn", 7, 1024, D)
    g_w_in = _matmul_call(
        "mm_dwin", dz8b, h,
        pl.BlockSpec((None, 1024, D), lambda i, j, k: (_zsec_of_wsec(i), k, 0)),
        pl.BlockSpec((1024, D), lambda i, j, k: (k, 0)),
        pl.BlockSpec((1024, D), lambda i, j, k: (i, 0)),
        jax.ShapeDtypeStruct((7 * D, D), bf16), (7, 1, T // 1024), "tn", T // 1024, D, D)
    grad_x, dg_norm1 = _norm1_bwd(xt, dh, dx1, norm1_g)

    gbig = dict(w_in=g_w_in, w_conv_out=g_w_conv_out, w_attn_out=g_w_attn_out, w_out=g_w_out, w_up=g_w_up,
                w_down=g_w_down)
    slots = _scatter_blocks([gbig[n].reshape(N_DEV, -1, D) for n in order])
    slots = dict(zip(order, slots))

    sum8 = lambda a: a.reshape(-1, 8, a.shape[-1]).sum(axis=1)
    dconv_s = sum8(dconv.sum(axis=0))
    dffn_s = dffn.sum(axis=0).reshape(2, 4, 8, D_FF).sum(axis=2)
    dffn_w = jnp.concatenate([dffn_s[0, :3], dffn_s[1, :3]], axis=1)
    dffn_b = jnp.concatenate([dffn_s[0, 3:4], dffn_s[1, 3:4]], axis=1)
    fold = lambda a: sum8(a).reshape(N_HEADS, HEAD_DIM).sum(axis=0)[None]
    small_g_local = _pack_small(
        sum8(dg_norm1), sum8(dg_gate), dconv_s[:CONV_WIDTH], dconv_s[CONV_WIDTH:], sum8(dg_convnorm),
        fold(dg_q), fold(dg_k), sum8(dg_norm2), dffn_w, dffn_b)
    small_g = _allreduce_small("allreduce_small_grads", small_g_local)

    res = {}
    for n in order:
        w, m, v = big[n]
        outs = _adam_slots("adam_" + n, slots[n], w, m, v, _ADAM_TILE[w.shape[0]])
        if n in ("w_in", "w_up"):
            outs = [a.T for a in outs]
        res[n] = [a[None] for a in outs]

    col = lambda a, width: lax.dynamic_slice(a, (0, me * width), (a.shape[0], width))
    small_w_true = _pack_small(norm1_g, gate_b, conv_w_full, conv_b, conv_norm_g, q_norm_g, k_norm_g, norm2_g,
                               ffn_w_full, ffn_conv_b)
    place_m = lambda a, full: place_cols(a[0], full)
    small_m = _pack_small(m_norm1_g, m_gate_b, place_m(m_conv_w, D), m_conv_b, m_conv_norm_g, m_q_norm_g, m_k_norm_g,
                          m_norm2_g, place_m(m_ffn_conv_w, 2 * D_FF), m_ffn_conv_b)
    small_v = _pack_small(v_norm1_g, v_gate_b, place_m(v_conv_w, D), v_conv_b, v_conv_norm_g, v_q_norm_g, v_k_norm_g,
                          v_norm2_g, place_m(v_ffn_conv_w, 2 * D_FF), v_ffn_conv_b)
    sd, sm, sv = _adam_small(small_g, small_w_true, small_m, small_v)
    for i, packed in enumerate((small_g, sd, sm, sv)):
        u = _unpack_small(packed)
        u["conv_w"] = col(u["conv_w"], D // N_DEV)
        u["ffn_conv_w"] = col(u["ffn_conv_w"], 2 * D_FF // N_DEV)
        for n, a in u.items():
            res.setdefault(n, [None] * 4)[i] = a[None] if n in ("conv_w", "ffn_conv_w") else a

    names = ["norm1_g", "w_in", "gate_b", "conv_w", "conv_b", "conv_norm_g", "w_conv_out", "q_norm_g", "k_norm_g",
             "w_attn_out", "w_out", "norm2_g", "w_up", "ffn_conv_w", "ffn_conv_b", "w_down"]
    out = [loss, grad_x.reshape(BL, S, D)]
    for i in range(4):
        out += [res[n][i] for n in names]
    return tuple(out)
```

```python
import functools

import jax
import jax.numpy as jnp
from jax import lax
from jax.experimental import pallas as pl
from jax.experimental.pallas import tpu as pltpu

f32 = jnp.float32
bf16 = jnp.bfloat16

D = 1024
N_HEADS = 16
HEAD_DIM = 64
CONV_WIDTH = 31
D_FF = 2816
GROUPS = ((128, 1), (512, 4), (2048, 16))
ATTN_BLOCK = 128
EPS = 1e-6
N_DEV = 8
MESH = pl.DeviceIdType.MESH

ADAM_LR = 0.001
ADAM_B1 = 0.9
ADAM_B2 = 0.999
ADAM_EPS = 1e-08
ADAM_WD = 0.01
ADAM_STEP = 10

VMEM_LIMIT = 56 * 1024 * 1024
MASK_BIAS = 1e30

Z_AVAL, Z_AGATE, Z_GA, Z_GB, Z_Q, Z_K, Z_V = 0, 1, 2, 3, 4, 5, 6


def _wsec_of_zsec(j):
    return jnp.where(j < 2, j, jnp.where(j < 4, j + 3, j - 2))


def _zsec_of_wsec(w):
    return jnp.where(w < 2, w, jnp.where(w < 5, w + 2, w - 3))


def _sig(x):
    return 1.0 / (1.0 + jnp.exp(-x))


def _colsum8(x):
    return x.reshape(-1, 8, x.shape[-1]).sum(axis=0)


def _cparams(sem):
    return pltpu.CompilerParams(dimension_semantics=sem, vmem_limit_bytes=VMEM_LIMIT)


def _my_pos():
    x, y, c = lax.axis_index("x"), lax.axis_index("y"), lax.axis_index("c")
    return x, y, c, 4 * x + 2 * y + c


_DIMS = {"nn": ((1,), (0,)), "nt": ((1,), (1,)), "tn": ((0,), (0,))}


def _matmul_call(name, a, b, a_spec, b_spec, o_spec, out_shape, grid, mode, nk, tm, tn):
    dims = (_DIMS[mode], ((), ()))

    def body(a_ref, b_ref, o_ref, *scratch):
        part = lax.dot_general(a_ref[...], b_ref[...], dims, preferred_element_type=f32)
        if nk == 1:
            o_ref[...] = part.astype(o_ref.dtype)
        else:
            acc = scratch[0]
            k = pl.program_id(2)

            @pl.when(k == 0)
            def _():
                acc[...] = part

            @pl.when(k > 0)
            def _():
                acc[...] += part

            @pl.when(k == nk - 1)
            def _():
                o_ref[...] = acc[...].astype(o_ref.dtype)

    scratch = [] if nk == 1 else [pltpu.VMEM((tm, tn), f32)]
    return pl.pallas_call(
        body, name=name, grid=grid, in_specs=[a_spec, b_spec], out_specs=o_spec, out_shape=out_shape,
        scratch_shapes=scratch, compiler_params=_cparams(("parallel", "parallel", "arbitrary")),
    )(a, b)


def _matmul(name, a, b, mode, out_dtype, tm=1024, tn=1024, tk=None):
    if mode == "nn":
        (M, K), (_, N) = a.shape, b.shape
    elif mode == "nt":
        (M, K), (N, _) = a.shape, b.shape
    else:
        (K, M), (_, N) = a.shape, b.shape
    tm, tn = min(tm, M), min(tn, N)
    tk = K if tk is None else tk
    nk = K // tk
    assert M % tm == 0 and N % tn == 0 and K % tk == 0
    if mode == "tn":
        a_spec = pl.BlockSpec((tk, tm), lambda i, j, k: (k, i))
    else:
        a_spec = pl.BlockSpec((tm, tk), lambda i, j, k: (i, k))
    if mode == "nt":
        b_spec = pl.BlockSpec((tn, tk), lambda i, j, k: (j, k))
    else:
        b_spec = pl.BlockSpec((tk, tn), lambda i, j, k: (k, j))
    o_spec = pl.BlockSpec((tm, tn), lambda i, j, k: (i, j))
    return _matmul_call(name, a, b, a_spec, b_spec, o_spec, jax.ShapeDtypeStruct((M, N), out_dtype),
                        (M // tm, N // tn, nk), mode, nk, tm, tn)


TT = 512


def _rows(c, cb=0, tt=TT):
    return pl.BlockSpec((tt, c), lambda i: (i, cb))


def _sec(s, tt=TT):
    return pl.BlockSpec((None, tt, D), lambda i: (s, i, 0))


def _const(shape):
    return pl.BlockSpec(shape, lambda i: (0,) * len(shape))


def _acc_spec(c):
    return pl.BlockSpec((8, c), lambda i: (0, 0))


def _rms(x):
    return lax.rsqrt(jnp.mean(x * x, axis=-1, keepdims=True) + EPS)


def _rms_bwd(dy_g, xn, rstd):
    return rstd * (dy_g - xn * jnp.mean(dy_g * xn, axis=-1, keepdims=True))


def _head_sum(x, bd):
    parts = []
    for cb in range(x.shape[-1] // 128):
        xb = x[:, cb * 128:(cb + 1) * 128]
        hi = xb.astype(bf16)
        lo = (xb - hi.astype(f32)).astype(bf16)
        parts.append(jnp.dot(hi, bd, preferred_element_type=f32) + jnp.dot(lo, bd, preferred_element_type=f32))
    return parts[0] if len(parts) == 1 else jnp.concatenate(parts, axis=1)


def _norm1_fwd(x, g):
    T = x.shape[0]

    def body(x_ref, g_ref, h_ref):
        xv = x_ref[...]
        h_ref[...] = (xv * _rms(xv) * g_ref[...]).astype(bf16)

    return pl.pallas_call(
        body, name="norm1_fwd", grid=(T // TT,), in_specs=[_rows(D), _const((1, D))], out_specs=_rows(D),
        out_shape=jax.ShapeDtypeStruct((T, D), bf16), compiler_params=_cparams(("parallel",)))(x, g)


def _convnorm_fwd(c, g):
    T = c.shape[0]

    def body(c_ref, g_ref, s_ref):
        cv = c_ref[...]
        r = cv * _rms(cv) * g_ref[...]
        s_ref[...] = (r * _sig(r)).astype(bf16)

    return pl.pallas_call(
        body, name="convnorm_fwd", grid=(T // TT,), in_specs=[_rows(D), _const((1, D))], out_specs=_rows(D),
        out_shape=jax.ShapeDtypeStruct((T, D), bf16), compiler_params=_cparams(("parallel",)))(c, g)


def _qk_fwd(z8, qg, kg, bd):
    T = z8.shape[1]

    def body(q_ref, k_ref, qg_ref, kg_ref, bd_ref, qn_ref, kn_ref):
        bdv = bd_ref[...]
        q = q_ref[...]
        qn_ref[...] = q * lax.rsqrt(_head_sum(q * q, bdv) * (1.0 / HEAD_DIM) + EPS) * qg_ref[...] * (HEAD_DIM ** -0.5)
        k = k_ref[...]
        kn_ref[...] = k * lax.rsqrt(_head_sum(k * k, bdv) * (1.0 / HEAD_DIM) + EPS) * kg_ref[...]

    return pl.pallas_call(
        body, name="qk_fwd", grid=(T // TT,),
        in_specs=[_sec(Z_Q), _sec(Z_K), _const((1, D)), _const((1, D)), _const((128, 128))],
        out_specs=[_rows(D), _rows(D)],
        out_shape=[jax.ShapeDtypeStruct((T, D), f32)] * 2, compiler_params=_cparams(("parallel",)))(z8, z8, qg, kg, bd)


def _gate_fwd(z8, gate_b, ya, yb):
    T = ya.shape[0]

    def body(ga_ref, gb_ref, b_ref, ya_ref, yb_ref, mixed_ref):
        g_a = _sig(ga_ref[...] + b_ref[:, :D])
        g_b = _sig(gb_ref[...] + b_ref[:, D:])
        mixed_ref[...] = (g_a * ya_ref[...] + g_b * yb_ref[...]).astype(bf16)

    return pl.pallas_call(
        body, name="gate_fwd", grid=(T // TT,),
        in_specs=[_sec(Z_GA), _sec(Z_GB), _const((1, 2 * D)), _rows(D), _rows(D)], out_specs=_rows(D),
        out_shape=jax.ShapeDtypeStruct((T, D), bf16), compiler_params=_cparams(("parallel",)))(z8, z8, gate_b, ya, yb)


def _norm2_fwd(x, t1, g):
    T = x.shape[0]

    def body(x_ref, t_ref, g_ref, x1_ref, h2_ref):
        x1 = x_ref[...] + t_ref[...]
        x1_ref[...] = x1
        h2_ref[...] = (x1 * _rms(x1) * g_ref[...]).astype(bf16)

    return pl.pallas_call(
        body, name="norm2_fwd", grid=(T // TT,), in_specs=[_rows(D), _rows(D), _const((1, D))],
        out_specs=[_rows(D), _rows(D)],
        out_shape=[jax.ShapeDtypeStruct((T, D), f32), jax.ShapeDtypeStruct((T, D), bf16)],
        compiler_params=_cparams(("parallel",)))(x, t1, g)


def _loss_fwd(x1, t2, target):
    T = x1.shape[0]

    def body(x1_ref, t_ref, tg_ref, dy_ref, dyb_ref, acc_ref):
        diff = x1_ref[...] + t_ref[...] - tg_ref[...]
        dy = diff * (1.0 / D)
        dy_ref[...] = dy
        dyb_ref[...] = dy.astype(bf16)

        @pl.when(pl.program_id(0) == 0)
        def _():
            acc_ref[...] = jnp.zeros_like(acc_ref)

        acc_ref[...] += _colsum8(diff * diff)

    return pl.pallas_call(
        body, name="loss_fwd", grid=(T // TT,), in_specs=[_rows(D)] * 3,
        out_specs=[_rows(D), _rows(D), _acc_spec(D)],
        out_shape=[jax.ShapeDtypeStruct((T, D), f32), jax.ShapeDtypeStruct((T, D), bf16),
                   jax.ShapeDtypeStruct((8, D), f32)],
        compiler_params=_cparams(("arbitrary",)))(x1, t2, target)


def _norm2_bwd(x1, dh2, dy, g):
    T = x1.shape[0]

    def body(x1_ref, dh_ref, dy_ref, g_ref, dx1_ref, dx1b_ref, dg_ref):
        x1 = x1_ref[...]
        rstd = _rms(x1)
        xn = x1 * rstd
        dh = dh_ref[...]
        dx1 = dy_ref[...] + _rms_bwd(dh * g_ref[...], xn, rstd)
        dx1_ref[...] = dx1
        dx1b_ref[...] = dx1.astype(bf16)

        @pl.when(pl.program_id(0) == 0)
        def _():
            dg_ref[...] = jnp.zeros_like(dg_ref)

        dg_ref[...] += _colsum8(dh * xn)

    return pl.pallas_call(
        body, name="norm2_bwd", grid=(T // TT,), in_specs=[_rows(D), _rows(D), _rows(D), _const((1, D))],
        out_specs=[_rows(D), _rows(D), _acc_spec(D)],
        out_shape=[jax.ShapeDtypeStruct((T, D), f32), jax.ShapeDtypeStruct((T, D), bf16),
                   jax.ShapeDtypeStruct((8, D), f32)],
        compiler_params=_cparams(("arbitrary",)))(x1, dh2, dy, g)


def _gate_bwd(dmixed, z8, gate_b, ya, yb, dz8):
    T = ya.shape[0]

    def body(dm_ref, ga_ref, gb_ref, b_ref, ya_ref, yb_ref, dz_in, dya_ref, dyb_ref, dz_ref, dgb_ref):
        del dz_in
        dm = dm_ref[...]
        g_a = _sig(ga_ref[...] + b_ref[:, :D])
        g_b = _sig(gb_ref[...] + b_ref[:, D:])
        dya_ref[...] = (dm * g_a).astype(bf16)
        dyb_ref[...] = (dm * g_b).astype(bf16)
        dla = dm * ya_ref[...] * g_a * (1.0 - g_a)
        dlb = dm * yb_ref[...] * g_b * (1.0 - g_b)
        dz_ref[0] = dla.astype(bf16)
        dz_ref[1] = dlb.astype(bf16)

        @pl.when(pl.program_id(0) == 0)
        def _():
            dgb_ref[...] = jnp.zeros_like(dgb_ref)

        dgb_ref[:, :D] += _colsum8(dla)
        dgb_ref[:, D:] += _colsum8(dlb)

    return pl.pallas_call(
        body, name="gate_bwd", grid=(T // TT,),
        in_specs=[_rows(D), _sec(Z_GA), _sec(Z_GB), _const((1, 2 * D)), _rows(D), _rows(D),
                  pl.BlockSpec(memory_space=pl.ANY)],
        out_specs=[_rows(D), _rows(D), pl.BlockSpec((2, TT, D), lambda i: (1, i, 0)), _acc_spec(2 * D)],
        out_shape=[jax.ShapeDtypeStruct((T, D), bf16), jax.ShapeDtypeStruct((T, D), bf16),
                   jax.ShapeDtypeStruct(dz8.shape, bf16), jax.ShapeDtypeStruct((8, 2 * D), f32)],
        input_output_aliases={6: 2},
        compiler_params=_cparams(("arbitrary",)))(dmixed, z8, z8, gate_b, ya, yb, dz8)


def _convnorm_bwd(c, ds, g):
    T = c.shape[0]

    def body(c_ref, ds_ref, g_ref, dc_ref, dg_ref):
        cv = c_ref[...]
        rstd = _rms(cv)
        r0 = cv * rstd
        gv = g_ref[...]
        r = r0 * gv
        sg = _sig(r)
        dr = ds_ref[...] * sg * (1.0 + r * (1.0 - sg))
        dc_ref[...] = _rms_bwd(dr * gv, r0, rstd)

        @pl.when(pl.program_id(0) == 0)
        def _():
            dg_ref[...] = jnp.zeros_like(dg_ref)

        dg_ref[...] += _colsum8(dr * r0)

    return pl.pallas_call(
        body, name="convnorm_bwd", grid=(T // TT,), in_specs=[_rows(D), _rows(D), _const((1, D))],
        out_specs=[_rows(D), _acc_spec(D)],
        out_shape=[jax.ShapeDtypeStruct((T, D), f32), jax.ShapeDtypeStruct((8, D), f32)],
        compiler_params=_cparams(("arbitrary",)))(c, ds, g)


def _qk_bwd(z8, dqn, dkn, dv, qg, kg, bd, dz8):
    T = dqn.shape[0]

    def body(q_ref, k_ref, dqn_ref, dkn_ref, dv_ref, qg_ref, kg_ref, bd_ref, dz_in, dz_ref, dqg_ref, dkg_ref):
        del dz_in
        bdv = bd_ref[...]

        @pl.when(pl.program_id(0) == 0)
        def _():
            dqg_ref[...] = jnp.zeros_like(dqg_ref)
            dkg_ref[...] = jnp.zeros_like(dkg_ref)

        def one(raw, dn_scaled, g, dg_ref, sec):
            rstd = lax.rsqrt(_head_sum(raw * raw, bdv) * (1.0 / HEAD_DIM) + EPS)
            n = raw * rstd
            dg_ref[...] += _colsum8(dn_scaled * n)
            dn = dn_scaled * g
            draw = rstd * (dn - n * (_head_sum(dn * n, bdv) * (1.0 / HEAD_DIM)))
            dz_ref[sec] = draw.astype(bf16)

        one(q_ref[...], dqn_ref[...] * (HEAD_DIM ** -0.5), qg_ref[...], dqg_ref, 0)
        one(k_ref[...], dkn_ref[...], kg_ref[...], dkg_ref, 1)
        dz_ref[2] = dv_ref[...].astype(bf16)
        dz_ref[3] = jnp.zeros((TT, D), bf16)

    return pl.pallas_call(
        body, name="qk_bwd", grid=(T // TT,),
        in_specs=[_sec(Z_Q), _sec(Z_K), _rows(D), _rows(D), _rows(D), _const((1, D)), _const((1, D)),
                  _const((128, 128)), pl.BlockSpec(memory_space=pl.ANY)],
        out_specs=[pl.BlockSpec((4, TT, D), lambda i: (1, i, 0)), _acc_spec(D), _acc_spec(D)],
        out_shape=[jax.ShapeDtypeStruct(dz8.shape, bf16), jax.ShapeDtypeStruct((8, D), f32),
                   jax.ShapeDtypeStruct((8, D), f32)],
        input_output_aliases={8: 0},
        compiler_params=_cparams(("arbitrary",)))(z8, z8, dqn, dkn, dv, qg, kg, bd, dz8)


def _norm1_bwd(x, dh, dx1, g):
    T = x.shape[0]

    def body(x_ref, dh_ref, dx1_ref, g_ref, gx_ref, dg_ref):
        xv = x_ref[...]
        rstd = _rms(xv)
        xn = xv * rstd
        dh = dh_ref[...]
        gx_ref[...] = dx1_ref[...] + _rms_bwd(dh * g_ref[...], xn, rstd)

        @pl.when(pl.program_id(0) == 0)
        def _():
            dg_ref[...] = jnp.zeros_like(dg_ref)

        dg_ref[...] += _colsum8(dh * xn)

    return pl.pallas_call(
        body, name="norm1_bwd", grid=(T // TT,), in_specs=[_rows(D), _rows(D), _rows(D), _const((1, D))],
        out_specs=[_rows(D), _acc_spec(D)],
        out_shape=[jax.ShapeDtypeStruct((T, D), f32), jax.ShapeDtypeStruct((8, D), f32)],
        compiler_params=_cparams(("arbitrary",)))(x, dh, dx1, g)


CCW = 256
CR = 64
HALO = 32


def _conv_fwd(z8, conv_w, conv_b, S):
    T = z8.shape[1]
    nb = T // S
    ncb = D // CCW

    def body(av_ref, ag_ref, w_ref, b_ref, c_ref, pad):
        pad[0:HALO, :] = jnp.zeros((HALO, CCW), f32)

        def fill(i, carry):
            r0 = pl.multiple_of(i * 256, 256)
            pad[pl.ds(HALO + r0, 256), :] = av_ref[pl.ds(r0, 256), :] * _sig(ag_ref[pl.ds(r0, 256), :])
            return carry

        lax.fori_loop(0, S // 256, fill, 0)
        bias = b_ref[...]

        def chunk(i, carry):
            r0 = pl.multiple_of(i * CR, CR)
            win = pad[pl.ds(r0, CR + HALO), :]
            acc = jnp.zeros((CR, CCW), f32) + bias
            for j in range(CONV_WIDTH):
                acc = acc + win[2 + j:2 + j + CR, :] * w_ref[j:j + 1, :]
            c_ref[pl.ds(r0, CR), :] = acc
            return carry

        lax.fori_loop(0, S // CR, chunk, 0)

    zs = lambda s: pl.BlockSpec((None, S, CCW), lambda b, cb: (s, b, cb))
    return pl.pallas_call(
        body, name="conv_fwd", grid=(nb, ncb),
        in_specs=[zs(Z_AVAL), zs(Z_AGATE), pl.BlockSpec((CONV_WIDTH, CCW), lambda b, cb: (0, cb)),
                  pl.BlockSpec((1, CCW), lambda b, cb: (0, cb))],
        out_specs=pl.BlockSpec((S, CCW), lambda b, cb: (b, cb)),
        out_shape=jax.ShapeDtypeStruct((T, D), f32),
        scratch_shapes=[pltpu.VMEM((S + HALO, CCW), f32)],
        compiler_params=_cparams(("parallel", "parallel")))(z8, z8, conv_w, conv_b)


def _conv_bwd(dc, z8, conv_w, dz8, S):
    T = dc.shape[0]
    nb = T // S
    ncb = D // CCW

    def body(dc_ref, av_ref, ag_ref, w_ref, dz_in, dz_ref, dw_ref, apad, dpad):
        del dz_in
        apad[0:HALO, :] = jnp.zeros((HALO, CCW), f32)
        dpad[S:S + HALO, :] = jnp.zeros((HALO, CCW), f32)
        dw_ref[...] = jnp.zeros_like(dw_ref)

        def fill(i, carry):
            r0 = pl.multiple_of(i * 256, 256)
            apad[pl.ds(HALO + r0, 256), :] = av_ref[pl.ds(r0, 256), :] * _sig(ag_ref[pl.ds(r0, 256), :])
            dpad[pl.ds(r0, 256), :] = dc_ref[pl.ds(r0, 256), :]
            return carry

        lax.fori_loop(0, S // 256, fill, 0)

        def chunk(i, carry):
            r0 = pl.multiple_of(i * CR, CR)
            awin = apad[pl.ds(r0, CR + HALO), :]
            dwin = dpad[pl.ds(r0, CR + HALO), :]
            dcc = dwin[0:CR, :]
            da = jnp.zeros((CR, CCW), f32)
            for j in range(CONV_WIDTH):
                da = da + dwin[30 - j:30 - j + CR, :] * w_ref[j:j + 1, :]
                dw_ref[8 * j:8 * j + 8, :] += _colsum8(dcc * awin[2 + j:2 + j + CR, :])
            dw_ref[8 * CONV_WIDTH:8 * CONV_WIDTH + 8, :] += _colsum8(dcc)
            av = av_ref[pl.ds(r0, CR), :]
            sg = _sig(ag_ref[pl.ds(r0, CR), :])
            dz_ref[0, pl.ds(r0, CR), :] = (da * sg).astype(bf16)
            dz_ref[1, pl.ds(r0, CR), :] = (da * av * sg * (1.0 - sg)).astype(bf16)
            return carry

        lax.fori_loop(0, S // CR, chunk, 0)

    zs = lambda s: pl.BlockSpec((None, S, CCW), lambda b, cb: (s, b, cb))
    return pl.pallas_call(
        body, name="conv_bwd", grid=(nb, ncb),
        in_specs=[pl.BlockSpec((S, CCW), lambda b, cb: (b, cb)), zs(Z_AVAL), zs(Z_AGATE),
                  pl.BlockSpec((CONV_WIDTH, CCW), lambda b, cb: (0, cb)), pl.BlockSpec(memory_space=pl.ANY)],
        out_specs=[pl.BlockSpec((2, S, CCW), lambda b, cb: (0, b, cb)),
                   pl.BlockSpec((None, 256, CCW), lambda b, cb: (b, 0, cb))],
        out_shape=[jax.ShapeDtypeStruct(dz8.shape, bf16), jax.ShapeDtypeStruct((nb, 256, D), f32)],
        input_output_aliases={4: 0},
        scratch_shapes=[pltpu.VMEM((S + HALO, CCW), f32), pltpu.VMEM((S + HALO, CCW), f32)],
        compiler_params=_cparams(("parallel", "parallel")))(dc, z8, z8, conv_w, dz8)


FR = 128
NFB = D_FF // CCW


def _ffn_window(ref, i, r0):
    return ref[pl.ds(r0 - 8, FR + 8), :]


def _ffn_u(win, w_ref, b_ref):
    return (win[6:6 + FR, :] * w_ref[0:1, :] + win[7:7 + FR, :] * w_ref[1:2, :]
            + win[8:8 + FR, :] * w_ref[2:3, :] + b_ref[...])


def _ffn_fwd(u3, ffn_w, ffn_b, S):
    T = u3.shape[1]
    nb = T // S

    def body(uv_ref, ug_ref, wv_ref, wg_ref, bv_ref, bg_ref, f_ref):
        def chunk(first, i):
            r0 = 0 if first else pl.multiple_of(i * FR, FR)
            if first:
                z = jnp.zeros((8, CCW), f32)
                wv = jnp.concatenate([z, uv_ref[0:FR, :]], axis=0)
                wg = jnp.concatenate([z, ug_ref[0:FR, :]], axis=0)
            else:
                wv = _ffn_window(uv_ref, i, r0)
                wg = _ffn_window(ug_ref, i, r0)
            u_val = _ffn_u(wv, wv_ref, bv_ref)
            u_gate = _ffn_u(wg, wg_ref, bg_ref)
            f_ref[pl.ds(r0, FR), :] = (u_gate * _sig(u_gate) * u_val).astype(bf16)

        chunk(True, 0)

        def loop(i, carry):
            chunk(False, i)
            return carry

        lax.fori_loop(1, S // FR, loop, 0)

    us = lambda h: pl.BlockSpec((None, S, CCW), lambda b, cb: (h, b, cb))
    ws = lambda h: pl.BlockSpec((3, CCW), lambda b, cb: (0, h * NFB + cb))
    bs = lambda h: pl.BlockSpec((1, CCW), lambda b, cb: (0, h * NFB + cb))
    return pl.pallas_call(
        body, name="ffn_fwd", grid=(nb, NFB),
        in_specs=[us(0), us(1), ws(0), ws(1), bs(0), bs(1)],
        out_specs=pl.BlockSpec((S, CCW), lambda b, cb: (b, cb)),
        out_shape=jax.ShapeDtypeStruct((T, D_FF), bf16),
        compiler_params=_cparams(("parallel", "parallel")))(u3, u3, ffn_w, ffn_w, ffn_b, ffn_b)


def _ffn_bwd(u3, df, ffn_w, ffn_b, S):
    T = u3.shape[1]
    nb = T // S

    def body(uv_ref, ug_ref, df_ref, wv_ref, wg_ref, bv_ref, bg_ref, du_ref, dw_ref, dvpad, dgpad):
        dvpad[S:S + 8, :] = jnp.zeros((8, CCW), f32)
        dgpad[S:S + 8, :] = jnp.zeros((8, CCW), f32)
        dw_ref[...] = jnp.zeros_like(dw_ref)

        def chunk(first, i):
            r0 = 0 if first else pl.multiple_of(i * FR, FR)
            if first:
                z = jnp.zeros((8, CCW), f32)
                wv = jnp.concatenate([z, uv_ref[0:FR, :]], axis=0)
                wg = jnp.concatenate([z, ug_ref[0:FR, :]], axis=0)
            else:
                wv = _ffn_window(uv_ref, i, r0)
                wg = _ffn_window(ug_ref, i, r0)
            u_val = _ffn_u(wv, wv_ref, bv_ref)
            u_gate = _ffn_u(wg, wg_ref, bg_ref)
            dfc = df_ref[pl.ds(r0, FR), :]
            sg = _sig(u_gate)
            d_val = dfc * u_gate * sg
            d_gate = dfc * u_val * sg * (1.0 + u_gate * (1.0 - sg))
            dvpad[pl.ds(r0, FR), :] = d_val
            dgpad[pl.ds(r0, FR), :] = d_gate
            for h, (dd, win) in enumerate(((d_val, wv), (d_gate, wg))):
                for j in range(3):
                    dw_ref[h, 8 * j:8 * j + 8, :] += _colsum8(dd * win[6 + j:6 + j + FR, :])
                dw_ref[h, 24:32, :] += _colsum8(dd)

        chunk(True, 0)

        def loop(i, carry):
            chunk(False, i)
            return carry

        lax.fori_loop(1, S // FR, loop, 0)

        def back(i, carry):
            r0 = pl.multiple_of(i * FR, FR)
            for h, (dpad, w_ref) in enumerate(((dvpad, wv_ref), (dgpad, wg_ref))):
                win = dpad[pl.ds(r0, FR + 8), :]
                du = (win[0:FR, :] * w_ref[2:3, :] + win[1:1 + FR, :] * w_ref[1:2, :]
                      + win[2:2 + FR, :] * w_ref[0:1, :])
                du_ref[h, pl.ds(r0, FR), :] = du.astype(bf16)
            return carry

        lax.fori_loop(0, S // FR, back, 0)

    us = lambda h: pl.BlockSpec((None, S, CCW), lambda b, cb: (h, b, cb))
    ws = lambda h: pl.BlockSpec((3, CCW), lambda b, cb: (0, h * NFB + cb))
    bs = lambda h: pl.BlockSpec((1, CCW), lambda b, cb: (0, h * NFB + cb))
    return pl.pallas_call(
        body, name="ffn_bwd", grid=(nb, NFB),
        in_specs=[us(0), us(1), pl.BlockSpec((S, CCW), lambda b, cb: (b, cb)), ws(0), ws(1), bs(0), bs(1)],
        out_specs=[pl.BlockSpec((2, S, CCW), lambda b, cb: (0, b, cb)),
                   pl.BlockSpec((None, 2, 32, CCW), lambda b, cb: (b, 0, 0, cb))],
        out_shape=[jax.ShapeDtypeStruct((2, T, D_FF), bf16), jax.ShapeDtypeStruct((nb, 2, 32, D_FF), f32)],
        scratch_shapes=[pltpu.VMEM((S + 8, CCW), f32), pltpu.VMEM((S + 8, CCW), f32)],
        compiler_params=_cparams(("parallel", "parallel")))(u3, u3, df, ffn_w, ffn_w, ffn_b, ffn_b)


AB = ATTN_BLOCK


def _attn_bias():
    slopes = 2.0 ** (-8.0 * jnp.arange(1, N_HEADS + 1, dtype=f32) / N_HEADS)
    steps = (jnp.arange(AB)[:, None] + AB) - jnp.arange(2 * AB)[None, :]
    own = (jnp.arange(2 * AB) >= AB)[None, :]
    out = []
    for window, dil in GROUPS:
        valid = (steps >= 0) & (steps <= window // dil)
        dist = slopes[:, None, None] * (steps * dil).astype(f32)[None]
        kinds = [jnp.where(v[None], dist, MASK_BIAS) for v in (valid, valid & own)]
        out.append(jnp.stack(kinds, axis=1))
    return jnp.stack(out)


def _head_masks():
    lane = lax.broadcasted_iota(jnp.int32, (1, 128), 1)
    return (lane < HEAD_DIM, lane >= HEAD_DIM)


def _perm_chunks(S, d):
    L = S // d
    ch = min(L, 256)
    out = []
    for r in range(d):
        for c in range(L // ch):
            start = r + d * ch * c
            out.append((pl.ds(start, ch, stride=d) if d > 1 else pl.ds(start, ch), r * L + c * ch, ch))
    return out


def _stack_heads(x, masks):
    return jnp.concatenate([jnp.where(masks[0], x, 0), jnp.where(masks[1], x, 0)], axis=0)


_NT = (((1,), (1,)), ((), ()))
_TN = (((0,), (0,)), ((), ()))
SCH = 64


def _attn_fwd(qn, kn, z8, bias, S):
    T = qn.shape[0]
    nb = T // S
    nblk = S // AB

    def body(q_ref, k_ref, v_ref, bias_ref, o_ref, ob_ref, lse_ref, qs, ks, vs, s2, p2, ogp, lgp, *group_scratch):
        og, lg = group_scratch[:3], group_scratch[3:]
        masks = _head_masks()
        ks[0:AB, :] = jnp.zeros((AB, 128), bf16)
        vs[0:AB, :] = jnp.zeros((AB, 128), bf16)

        for g, (_, d) in enumerate(GROUPS):
            nsub = S // (d * AB)
            chunks = _perm_chunks(S, d)
            for src, dst, ch in chunks:
                qs[dst:dst + ch, :] = q_ref[src, :].astype(bf16)
                ks[AB + dst:AB + dst + ch, :] = k_ref[src, :].astype(bf16)
                vs[AB + dst:AB + dst + ch, :] = v_ref[src, :].astype(bf16)
            od, ld = (og[g], lg[g]) if d == 1 else (ogp, lgp)

            def scores(j, carry):
                r0 = pl.multiple_of(j * AB, AB)
                q2 = _stack_heads(qs[pl.ds(r0, AB), :], masks)
                s2[j] = lax.dot_general(q2, ks[pl.ds(r0, 2 * AB), :], _NT, preferred_element_type=f32)
                return carry

            lax.fori_loop(0, nblk, scores, 0, unroll=8)

            def softmax(j, carry, g=g, nsub=nsub, ld=ld):
                r0 = pl.multiple_of(j * AB, AB)
                kind = (j % nsub == 0).astype(jnp.int32)
                for cc in range(AB // SCH):
                    lses = []
                    for hh in range(2):
                        rows = pl.ds(hh * AB + cc * SCH, SCH)
                        sb = s2[j, rows, :] - bias_ref[g, hh, kind, cc * SCH:(cc + 1) * SCH, :]
                        m = jnp.max(sb, axis=-1, keepdims=True)
                        p = jnp.exp(sb - m)
                        den = jnp.sum(p, axis=-1, keepdims=True)
                        p2[j, rows, :] = (p * (1.0 / den)).astype(bf16)
                        lses.append(m + jnp.log(den))
                    ld[pl.ds(r0 + cc * SCH, SCH), :] = jnp.where(masks[0], lses[0], lses[1])
                return carry

            lax.fori_loop(0, nblk, softmax, 0, unroll=2)

            def values(j, carry, od=od):
                r0 = pl.multiple_of(j * AB, AB)
                pv2 = jnp.dot(p2[j], vs[pl.ds(r0, 2 * AB), :], preferred_element_type=f32)
                od[pl.ds(r0, AB), :] = jnp.where(masks[0], pv2[:AB], pv2[AB:])
                return carry

            lax.fori_loop(0, nblk, values, 0, unroll=8)

            if d > 1:
                for src, dst, ch in chunks:
                    og[g][src, :] = ogp[dst:dst + ch, :]
                    lg[g][src, :] = lgp[dst:dst + ch, :]

        def combine(i, carry):
            rr = pl.ds(pl.multiple_of(i * 256, 256), 256)
            l0, l1, l2 = lg[0][rr, :], lg[1][rr, :], lg[2][rr, :]
            mx = jnp.maximum(jnp.maximum(l0, l1), l2)
            e0, e1, e2 = jnp.exp(l0 - mx), jnp.exp(l1 - mx), jnp.exp(l2 - mx)
            den = e0 + e1 + e2
            o = (e0 * og[0][rr, :] + e1 * og[1][rr, :] + e2 * og[2][rr, :]) / den
            o_ref[rr, :] = o
            ob_ref[rr, :] = o.astype(bf16)
            lse_ref[rr, :] = mx + jnp.log(den)
            return carry

        lax.fori_loop(0, S // 256, combine, 0)

    blk = pl.BlockSpec((S, 128), lambda b, hp: (b, hp))
    return pl.pallas_call(
        body, name="attn_fwd", grid=(nb, N_HEADS // 2),
        in_specs=[blk, blk, pl.BlockSpec((None, S, 128), lambda b, hp: (Z_V, b, hp)),
                  pl.BlockSpec((3, 2, 2, AB, 2 * AB), lambda b, hp: (0, hp, 0, 0, 0))],
        out_specs=[blk, blk, blk],
        out_shape=[jax.ShapeDtypeStruct((T, D), f32), jax.ShapeDtypeStruct((T, D), bf16),
                   jax.ShapeDtypeStruct((T, D), f32)],
        scratch_shapes=[pltpu.VMEM((S, 128), bf16), pltpu.VMEM((S + AB, 128), bf16), pltpu.VMEM((S + AB, 128), bf16),
                        pltpu.VMEM((nblk, 2 * AB, 2 * AB), f32), pltpu.VMEM((nblk, 2 * AB, 2 * AB), bf16),
                        pltpu.VMEM((S, 128), f32), pltpu.VMEM((S, 128), f32)] + [pltpu.VMEM((S, 128), f32)] * 6,
        compiler_params=_cparams(("parallel", "parallel")))(qn, kn, z8, bias)


def _attn_bwd(qn, kn, z8, do, o, lse, bias, bd, S):
    T = qn.shape[0]
    nb = T // S

    nblk = S // AB

    def body(q_ref, k_ref, v_ref, do_ref, o_ref, lse_ref, bias_ref, bd_ref, dq_ref, dk_ref, dv_ref,
             delta, qs, ks, vs, dos, lsp, dlp, s2, dp2, p2, ds2, dqp, dkp, dvp):
        masks = _head_masks()
        bdv = bd_ref[...]
        dq_ref[...] = jnp.zeros_like(dq_ref)
        dk_ref[...] = jnp.zeros_like(dk_ref)
        dv_ref[...] = jnp.zeros_like(dv_ref)
        ks[0:AB, :] = jnp.zeros((AB, 128), bf16)
        vs[0:AB, :] = jnp.zeros((AB, 128), bf16)

        def prep(i, carry):
            rr = pl.ds(pl.multiple_of(i * 256, 256), 256)
            delta[rr, :] = _head_sum(do_ref[rr, :] * o_ref[rr, :], bdv)
            return carry

        lax.fori_loop(0, S // 256, prep, 0)

        for g, (_, d) in enumerate(GROUPS):
            nsub = S // (d * AB)
            chunks = _perm_chunks(S, d)
            for src, dst, ch in chunks:
                qs[dst:dst + ch, :] = q_ref[src, :].astype(bf16)
                ks[AB + dst:AB + dst + ch, :] = k_ref[src, :].astype(bf16)
                vs[AB + dst:AB + dst + ch, :] = v_ref[src, :].astype(bf16)
                dos[dst:dst + ch, :] = do_ref[src, :].astype(bf16)
                lsp[dst:dst + ch, :] = lse_ref[src, :]
                dlp[dst:dst + ch, :] = delta[src, :]
            dkp[...] = jnp.zeros_like(dkp)
            dvp[...] = jnp.zeros_like(dvp)

            def scores(j, carry):
                r0 = pl.multiple_of(j * AB, AB)
                q2 = _stack_heads(qs[pl.ds(r0, AB), :], masks)
                do2 = _stack_heads(dos[pl.ds(r0, AB), :], masks)
                s2[j] = lax.dot_general(q2, ks[pl.ds(r0, 2 * AB), :], _NT, preferred_element_type=f32)
                dp2[j] = lax.dot_general(do2, vs[pl.ds(r0, 2 * AB), :], _NT, preferred_element_type=f32)
                return carry

            lax.fori_loop(0, nblk, scores, 0, unroll=8)

            def probs(j, carry, g=g, nsub=nsub):
                r0 = pl.multiple_of(j * AB, AB)
                kind = (j % nsub == 0).astype(jnp.int32)
                for cc in range(AB // SCH):
                    lse_c = lsp[pl.ds(r0 + cc * SCH, SCH), :]
                    del_c = dlp[pl.ds(r0 + cc * SCH, SCH), :]
                    for hh in range(2):
                        c0 = hh * HEAD_DIM
                        rows = pl.ds(hh * AB + cc * SCH, SCH)
                        sb = s2[j, rows, :] - bias_ref[g, hh, kind, cc * SCH:(cc + 1) * SCH, :]
                        p = jnp.exp(sb - lse_c[:, c0:c0 + 1])
                        p2[j, rows, :] = p.astype(bf16)
                        ds2[j, rows, :] = (p * (dp2[j, rows, :] - del_c[:, c0:c0 + 1])).astype(bf16)
                return carry

            lax.fori_loop(0, nblk, probs, 0, unroll=2)

            def grads(j, carry):
                r0 = pl.multiple_of(j * AB, AB)
                q2 = _stack_heads(qs[pl.ds(r0, AB), :], masks)
                do2 = _stack_heads(dos[pl.ds(r0, AB), :], masks)
                dsb = ds2[j]
                t = jnp.dot(dsb, ks[pl.ds(r0, 2 * AB), :], preferred_element_type=f32)
                dqp[pl.ds(r0, AB), :] = jnp.where(masks[0], t[:AB], t[AB:])
                dkp[pl.ds(r0, 2 * AB), :] += lax.dot_general(dsb, q2, _TN, preferred_element_type=f32)
                dvp[pl.ds(r0, 2 * AB), :] += lax.dot_general(p2[j], do2, _TN, preferred_element_type=f32)
                return carry

            lax.fori_loop(0, nblk, grads, 0, unroll=4)

            for src, dst, ch in chunks:
                dq_ref[src, :] += dqp[dst:dst + ch, :]
                dk_ref[src, :] += dkp[AB + dst:AB + dst + ch, :]
                dv_ref[src, :] += dvp[AB + dst:AB + dst + ch, :]

    blk = pl.BlockSpec((S, 128), lambda b, hp: (b, hp))
    row = lambda dt, pad=0: pltpu.VMEM((S + pad, 128), dt)
    blocks = lambda dt: pltpu.VMEM((nblk, 2 * AB, 2 * AB), dt)
    return pl.pallas_call(
        body, name="attn_bwd", grid=(nb, N_HEADS // 2),
        in_specs=[blk, blk, pl.BlockSpec((None, S, 128), lambda b, hp: (Z_V, b, hp)), blk, blk, blk,
                  pl.BlockSpec((3, 2, 2, AB, 2 * AB), lambda b, hp: (0, hp, 0, 0, 0)),
                  pl.BlockSpec((128, 128), lambda b, hp: (0, 0))],
        out_specs=[blk, blk, blk],
        out_shape=[jax.ShapeDtypeStruct((T, D), f32)] * 3,
        scratch_shapes=[row(f32), row(bf16), row(bf16, AB), row(bf16, AB), row(bf16), row(f32), row(f32),
                        blocks(f32), blocks(f32), blocks(bf16), blocks(bf16), row(f32), row(f32, AB), row(f32, AB)],
        compiler_params=_cparams(("parallel", "parallel")))(qn, kn, z8, do, o, lse, bias, bd)


def _any_spec():
    return pl.BlockSpec(memory_space=pl.ANY)


def _allgather_rows(shards):
    n = len(shards)

    def body(*refs):
        ins, outs = refs[:n], refs[n:2 * n]
        send_sems, recv_sems, local_sems = refs[2 * n:]
        x, y, c, me = _my_pos()
        sibling = (x, y, 1 - c)
        chips = [(1 - x, y), (x, 1 - y), (1 - x, 1 - y)]

        def idx(px, py, pc):
            return 4 * px + 2 * py + pc

        def copy(a, k, blk, to, src=None):
            return pltpu.make_async_remote_copy(
                src_ref=outs[a].at[blk] if src is None else src, dst_ref=outs[a].at[blk],
                send_sem=send_sems.at[a, k], recv_sem=recv_sems.at[a, k], device_id=to, device_id_type=MESH)

        mine = [pltpu.make_async_copy(ins[a], outs[a].at[me], local_sems.at[a]) for a in range(n)]
        for cp in mine:
            cp.start()
        first = []
        for a in range(n):
            first.append(copy(a, 0, me, sibling, src=ins[a]))
            first += [copy(a, 1 + j, me, (*chip, c), src=ins[a]) for j, chip in enumerate(chips)]
        for cp in first:
            cp.start()
        passed = []
        for a in range(n):
            for j, chip in enumerate(chips):
                blk = idx(*chip, c)
                copy(a, 1 + j, blk, (x, y, c)).wait_recv()
                cp = copy(a, 4 + j, blk, sibling)
                cp.start()
                passed.append(cp)
        for a in range(n):
            copy(a, 0, idx(x, y, 1 - c), (x, y, c)).wait_recv()
            for j, chip in enumerate(chips):
                copy(a, 4 + j, idx(*chip, 1 - c), (x, y, c)).wait_recv()
        for cp in first + passed:
            cp.wait_send()
        for cp in mine:
            cp.wait()

    return pl.pallas_call(
        body, name="allgather_weights",
        in_specs=[_any_spec()] * n, out_specs=[_any_spec()] * n,
        out_shape=[jax.ShapeDtypeStruct((N_DEV,) + s.shape, s.dtype) for s in shards],
        scratch_shapes=[pltpu.SemaphoreType.DMA((n, 7)), pltpu.SemaphoreType.DMA((n, 7)),
                        pltpu.SemaphoreType.DMA((n,))],
    )(*shards)


def _peer(x, y, c, k):
    tx = 1 - x if (k >> 2) & 1 else x
    ty = 1 - y if (k >> 1) & 1 else y
    tc = 1 - c if k & 1 else c
    return (tx, ty, tc), 4 * tx + 2 * ty + tc


_PEER_ORDER = (2, 4, 6, 3, 5, 7, 1)


_HBM = pl.BlockSpec(memory_space=pltpu.HBM)
_SEM = pl.BlockSpec(memory_space=pltpu.SEMAPHORE)
_EFFECT = pltpu.SideEffectType.DATAFLOW_SIDE_EFFECTING


def _scatter_copies(srcs, lands, send_sems, recv_sems):
    x, y, c, me = _my_pos()
    copies = []
    for k in _PEER_ORDER:
        tgt, tidx = _peer(x, y, c, k)
        for a in range(len(srcs)):
            copies.append(pltpu.make_async_remote_copy(
                src_ref=srcs[a].at[tidx], dst_ref=lands[a].at[me],
                send_sem=send_sems.at[7 * a + k - 1], recv_sem=recv_sems.at[7 * a + k - 1],
                device_id=tgt, device_id_type=MESH))
    return copies


def _scatter_start(name, grads):
    n = len(grads)

    def body(*refs):
        srcs, lands = refs[:n], refs[n:2 * n]
        send_sems, recv_sems = refs[2 * n], refs[2 * n + 1]
        token = refs[-1]
        for cp in _scatter_copies(srcs, lands, send_sems, recv_sems):
            cp.start()
        token[...] = jnp.zeros_like(token)

    hbm = lambda a: pltpu.with_memory_space_constraint(a, pltpu.HBM)
    outs = pl.pallas_call(
        body, name=name,
        out_shape=(pltpu.SemaphoreType.DMA((7 * n,)), pltpu.SemaphoreType.DMA((7 * n,)),
                   *[pltpu.HBM(g.shape, g.dtype) for g in grads], *[pltpu.HBM(g.shape, g.dtype) for g in grads],
                   jax.ShapeDtypeStruct((8, 128), f32)),
        in_specs=[_HBM] * (2 * n),
        out_specs=(_SEM, _SEM, *([_HBM] * (2 * n)), pl.BlockSpec(memory_space=pltpu.VMEM)),
        input_output_aliases={i: 2 + i for i in range(2 * n)},
        compiler_params=pltpu.CompilerParams(has_side_effects=_EFFECT),
    )(*[hbm(g) for g in grads], *[hbm(lax.empty(g.shape, g.dtype)) for g in grads])
    return outs[0], outs[1], list(outs[2:2 + n]), list(outs[2 + n:2 + 2 * n]), outs[-1]


def _scatter_wait(name, send_sems, recv_sems, srcs, lands, after):
    n = len(srcs)

    def body(*refs):
        src_refs, land_refs = refs[:n], refs[n:2 * n]
        s_sems, r_sems = refs[2 * n], refs[2 * n + 1]
        for cp in _scatter_copies(src_refs, land_refs, s_sems, r_sems):
            cp.wait_send()
            cp.wait_recv()

    outs = pl.pallas_call(
        body, name=name,
        out_shape=tuple(pltpu.HBM(a.shape, a.dtype) for a in list(srcs) + list(lands)),
        in_specs=[_HBM] * (2 * n) + [_SEM, _SEM, pl.BlockSpec(memory_space=pl.ANY)],
        out_specs=tuple([_HBM] * (2 * n)),
        input_output_aliases={i: i for i in range(2 * n)},
        compiler_params=pltpu.CompilerParams(has_side_effects=_EFFECT),
    )(*srcs, *lands, send_sems, recv_sems, after)
    return list(outs[:n]), list(outs[n:])


SMALL_ROWS = 64


def _allreduce_small(name, sg):
    def body(sg_ref, out_ref, buf, send_sems, recv_sems):
        x, y, c, me = _my_pos()
        buf[me] = sg_ref[...]
        copies = []
        for k in _PEER_ORDER:
            tgt, _ = _peer(x, y, c, k)
            cp = pltpu.make_async_remote_copy(
                src_ref=sg_ref, dst_ref=buf.at[me], send_sem=send_sems.at[k - 1], recv_sem=recv_sems.at[k - 1],
                device_id=tgt, device_id_type=MESH)
            cp.start()
            copies.append(cp)
        for cp in copies:
            cp.wait()
        acc = buf[0]
        for p in range(1, N_DEV):
            acc = acc + buf[p]
        out_ref[...] = acc

    return pl.pallas_call(
        body, name=name,
        in_specs=[pl.BlockSpec(memory_space=pltpu.VMEM)], out_specs=pl.BlockSpec(memory_space=pltpu.VMEM),
        out_shape=jax.ShapeDtypeStruct(sg.shape, f32),
        scratch_shapes=[pltpu.VMEM((N_DEV,) + sg.shape, f32), pltpu.SemaphoreType.DMA((7,)),
                        pltpu.SemaphoreType.DMA((7,))],
    )(sg)


def _adam_math(g, w, m, v):
    m = ADAM_B1 * m + (1.0 - ADAM_B1) * g
    v = ADAM_B2 * v + (1.0 - ADAM_B2) * (g * g)
    m_hat = m / (1.0 - ADAM_B1 ** ADAM_STEP)
    v_hat = v / (1.0 - ADAM_B2 ** ADAM_STEP)
    delta = -ADAM_LR * (m_hat / (jnp.sqrt(v_hat) + ADAM_EPS) + ADAM_WD * w)
    return delta, m, v


def _adam_slots(name, me, slots, own, w, m, v, tr):
    rows = w.shape[0]

    def body(me_ref, s_ref, own_ref, w_ref, m_ref, v_ref, g_ref, d_ref, nm_ref, nv_ref):
        mine = own_ref[...]
        g = None
        for p in range(N_DEV):
            term = lax.cond(me_ref[0] == p, lambda: mine, lambda p=p: s_ref[p]).astype(f32)
            g = term if g is None else g + term
        delta, nm, nv = _adam_math(g, w_ref[...], m_ref[...], v_ref[...])
        g_ref[...] = g
        d_ref[...] = delta
        nm_ref[...] = nm
        nv_ref[...] = nv

    rs = pl.BlockSpec((tr, D), lambda i, me_ref: (i, 0))
    return pl.pallas_call(
        body, name=name,
        grid_spec=pltpu.PrefetchScalarGridSpec(
            num_scalar_prefetch=1, grid=(rows // tr,),
            in_specs=[pl.BlockSpec((N_DEV, tr, D), lambda i, me_ref: (0, i, 0)),
                      pl.BlockSpec((None, tr, D), lambda i, me_ref: (me_ref[0], i, 0)), rs, rs, rs],
            out_specs=[rs] * 4),
        out_shape=[jax.ShapeDtypeStruct((rows, D), f32)] * 4,
        compiler_params=_cparams(("parallel",)))(me, slots, own, w, m, v)


def _adam_small(g, w, m, v):
    def body(g_ref, w_ref, m_ref, v_ref, d_ref, nm_ref, nv_ref):
        delta, nm, nv = _adam_math(g_ref[...], w_ref[...], m_ref[...], v_ref[...])
        d_ref[...] = delta
        nm_ref[...] = nm
        nv_ref[...] = nv

    return pl.pallas_call(body, name="adam_small", out_shape=[jax.ShapeDtypeStruct(g.shape, f32)] * 3)(g, w, m, v)


FFN_PAD = 6 * D


def _pack_small(norm1_g, gate_b, conv_w, conv_b, conv_norm_g, q_norm_g, k_norm_g, norm2_g, ffn_conv_w, ffn_conv_b):
    pad_h = lambda a: jnp.pad(a, ((0, 0), (0, D - HEAD_DIM)))
    pad_f = lambda a: jnp.pad(a, ((0, 0), (0, FFN_PAD - 2 * D_FF))).reshape(-1, D)
    parts = [norm1_g, gate_b.reshape(2, D), conv_w, conv_b, conv_norm_g, pad_h(q_norm_g), pad_h(k_norm_g), norm2_g,
             pad_f(ffn_conv_w), pad_f(ffn_conv_b)]
    out = jnp.concatenate(parts, axis=0)
    return jnp.pad(out, ((0, SMALL_ROWS - out.shape[0]), (0, 0)))


def _unpack_small(p):
    ffn = lambda a: a.reshape(-1, FFN_PAD)[:, :2 * D_FF]
    return dict(
        norm1_g=p[0:1], gate_b=p[1:3].reshape(1, 2 * D), conv_w=p[3:34], conv_b=p[34:35], conv_norm_g=p[35:36],
        q_norm_g=p[36:37, :HEAD_DIM], k_norm_g=p[37:38, :HEAD_DIM], norm2_g=p[38:39],
        ffn_conv_w=ffn(p[39:57]), ffn_conv_b=ffn(p[57:63]))


_ADAM_TILE = {896: 128, 704: 64, 128: 128, 352: 176}


def kernel(x, norm1_g, w_in, gate_b, conv_w, conv_b, conv_norm_g, w_conv_out, q_norm_g, k_norm_g, w_attn_out, w_out, norm2_g, w_up, ffn_conv_w, ffn_conv_b, w_down, loss_target, m_norm1_g, m_w_in, m_gate_b, m_conv_w, m_conv_b, m_conv_norm_g, m_w_conv_out, m_q_norm_g, m_k_norm_g, m_w_attn_out, m_w_out, m_norm2_g, m_w_up, m_ffn_conv_w, m_ffn_conv_b, m_w_down, v_norm1_g, v_w_in, v_gate_b, v_conv_w, v_conv_b, v_conv_norm_g, v_w_conv_out, v_q_norm_g, v_k_norm_g, v_w_attn_out, v_w_out, v_norm2_g, v_w_up, v_ffn_conv_w, v_ffn_conv_b, v_w_down):
    BL, S, _ = x.shape
    T = BL * S
    me = 4 * lax.axis_index("x") + 2 * lax.axis_index("y") + lax.axis_index("c")
    xt = x.reshape(T, D)
    target = loss_target.reshape(T, D)

    big = dict(w_in=(w_in[0].T, m_w_in[0].T, v_w_in[0].T), w_up=(w_up[0].T, m_w_up[0].T, v_w_up[0].T),
               w_conv_out=(w_conv_out[0], m_w_conv_out[0], v_w_conv_out[0]),
               w_attn_out=(w_attn_out[0], m_w_attn_out[0], v_w_attn_out[0]),
               w_out=(w_out[0], m_w_out[0], v_w_out[0]), w_down=(w_down[0], m_w_down[0], v_w_down[0]))
    order = ["w_in", "w_conv_out", "w_attn_out", "w_out", "w_up", "w_down"]
    gathered = _allgather_rows([big[n][0].astype(bf16) for n in order])
    W = {n: g.reshape(-1, D) for n, g in zip(order, gathered)}

    def place_cols(shard, full_cols):
        z = jnp.zeros((shard.shape[0], full_cols), f32)
        return lax.dynamic_update_slice(z, shard, (0, me * shard.shape[1]))

    zr = lambda a: jnp.zeros_like(a)
    conv_local = _pack_small(
        zr(norm1_g), zr(gate_b), place_cols(conv_w[0], D), zr(conv_b), zr(conv_norm_g), zr(q_norm_g), zr(k_norm_g),
        zr(norm2_g), place_cols(ffn_conv_w[0], 2 * D_FF), zr(ffn_conv_b))
    conv_all = _unpack_small(_allreduce_small("gather_conv_weights", conv_local))
    conv_w_full, ffn_w_full = conv_all["conv_w"], conv_all["ffn_conv_w"]

    bd = (jnp.arange(128)[:, None] // HEAD_DIM == jnp.arange(128)[None, :] // HEAD_DIM).astype(bf16)
    bias = _attn_bias()
    qg = jnp.tile(q_norm_g, (1, N_HEADS))
    kg = jnp.tile(k_norm_g, (1, N_HEADS))

    h = _norm1_fwd(xt, norm1_g)
    z8 = _matmul_call(
        "mm_z", h, W["w_in"],
        pl.BlockSpec((1024, D), lambda i, j, k: (i, 0)),
        pl.BlockSpec((1024, D), lambda i, j, k: (_wsec_of_zsec(j), 0)),
        pl.BlockSpec((None, 1024, D), lambda i, j, k: (j, i, 0)),
        jax.ShapeDtypeStruct((8, T, D), f32), (T // 1024, 7, 1), "nt", 1, 1024, 1024)
    c = _conv_fwd(z8, conv_w_full, conv_b, S)
    s = _convnorm_fwd(c, conv_norm_g)
    ya = _matmul("mm_ya", s, W["w_conv_out"], "nn", f32)
    qn, kn = _qk_fwd(z8, qg, kg, bd)
    o, ob, lse = _attn_fwd(qn, kn, z8, bias, S)
    yb = _matmul("mm_yb", ob, W["w_attn_out"], "nn", f32)
    mixed = _gate_fwd(z8, gate_b, ya, yb)
    t1 = _matmul("mm_t1", mixed, W["w_out"], "nn", f32)
    x1, h2 = _norm2_fwd(xt, t1, norm2_g)
    TNU = D_FF // 2
    u3 = _matmul_call(
        "mm_u", h2, W["w_up"],
        pl.BlockSpec((1024, D), lambda i, j, k: (i, 0)),
        pl.BlockSpec((TNU, D), lambda i, j, k: (j, 0)),
        pl.BlockSpec((None, 1024, TNU), lambda i, j, k: (j // 2, i, j % 2)),
        jax.ShapeDtypeStruct((2, T, D_FF), f32), (T // 1024, 4, 1), "nt", 1, 1024, TNU)
    f = _ffn_fwd(u3, ffn_w_full, ffn_conv_b, S)
    t2 = _matmul("mm_t2", f, W["w_down"], "nn", f32, tk=TNU)
    dy, dyb, lacc = _loss_fwd(x1, t2, target)
    loss = lax.psum(0.5 / D * jnp.sum(lacc), ("x", "y", "c"))

    df = _matmul("mm_df", dyb, W["w_down"], "nt", f32, tn=TNU)
    g_w_down = _matmul("mm_dwdn", f, dyb, "tn", bf16, tm=TNU, tk=1024)
    du3, dffn = _ffn_bwd(u3, df, ffn_w_full, ffn_conv_b, S)
    g_w_up = _matmul_call(
        "mm_dwup", du3, h2,
        pl.BlockSpec((None, 1024, TNU), lambda i, j, k: (i // 2, k, i % 2)),
        pl.BlockSpec((1024, D), lambda i, j, k: (k, 0)),
        pl.BlockSpec((TNU, D), lambda i, j, k: (i, 0)),
        jax.ShapeDtypeStruct((2 * D_FF, D), bf16), (4, 1, T // 1024), "tn", T // 1024, TNU, D)
    blocks8 = lambda a: a.reshape(N_DEV, -1, D)
    ex_ffn = _scatter_start("scatter_start_ffn", [blocks8(g_w_up), blocks8(g_w_down)])
    du3 = lax.optimization_barrier((ex_ffn[4], du3))[1]
    dh2 = _matmul_call(
        "mm_dh2", du3, W["w_up"],
        pl.BlockSpec((None, 1024, TNU), lambda i, j, k: (k // 2, i, k % 2)),
        pl.BlockSpec((TNU, D), lambda i, j, k: (k, 0)),
        pl.BlockSpec((1024, D), lambda i, j, k: (i, 0)),
        jax.ShapeDtypeStruct((T, D), f32), (T // 1024, 1, 4), "nn", 4, 1024, D)
    dx1, dx1b, dg_norm2 = _norm2_bwd(x1, dh2, dy, norm2_g)
    dmixed = _matmul("mm_dmixed", dx1b, W["w_out"], "nt", f32)
    g_w_out = _matmul("mm_dwo", mixed, dx1b, "tn", bf16, tk=1024)
    dz8 = lax.empty((8, T, D), bf16)
    dya, dyb2, dz8, dg_gate = _gate_bwd(dmixed, z8, gate_b, ya, yb, dz8)
    ds = _matmul("mm_ds", dya, W["w_conv_out"], "nt", f32)
    g_w_conv_out = _matmul("mm_dwco", s, dya, "tn", bf16, tk=1024)
    do = _matmul("mm_do", dyb2, W["w_attn_out"], "nt", f32)
    g_w_attn_out = _matmul("mm_dwao", ob, dyb2, "tn", bf16, tk=1024)
    ex_proj = _scatter_start("scatter_start_proj", [blocks8(g_w_conv_out), blocks8(g_w_attn_out), blocks8(g_w_out)])
    ds = lax.optimization_barrier((ex_proj[4], ds))[1]
    dc, dg_convnorm = _convnorm_bwd(c, ds, conv_norm_g)
    dz8a, dconv = _conv_bwd(dc, z8, conv_w_full, dz8, S)
    dqn, dkn, dv = _attn_bwd(qn, kn, z8, do, o, lse, bias, bd, S)
    dz8b, dg_q, dg_k = _qk_bwd(z8, dqn, dkn, dv, qg, kg, bd, dz8a)
    g_w_in = _matmul_call(
        "mm_dwin", dz8b, h,
        pl.BlockSpec((None, 1024, D), lambda i, j, k: (_zsec_of_wsec(i), k, 0)),
        pl.BlockSpec((1024, D), lambda i, j, k: (k, 0)),
        pl.BlockSpec((1024, D), lambda i, j, k: (i, 0)),
        jax.ShapeDtypeStruct((7 * D, D), bf16), (7, 1, T // 1024), "tn", T // 1024, D, D)
    ex_in = _scatter_start("scatter_start_in", [blocks8(g_w_in)])
    dz8b = lax.optimization_barrier((ex_in[4], dz8b))[1]
    dh = _matmul_call(
        "mm_dh", dz8b, W["w_in"],
        pl.BlockSpec((None, 1024, D), lambda i, j, k: (k, i, 0)),
        pl.BlockSpec((1024, D), lambda i, j, k: (_wsec_of_zsec(k), 0)),
        pl.BlockSpec((1024, D), lambda i, j, k: (i, 0)),
        jax.ShapeDtypeStruct((T, D), f32), (T // 1024, 1, 7), "nn", 7, 1024, D)
    grad_x, dg_norm1 = _norm1_bwd(xt, dh, dx1, norm1_g)

    own, slots = {}, {}
    for tag, ex, names_ in (("ffn", ex_ffn, ("w_up", "w_down")),
                            ("proj", ex_proj, ("w_conv_out", "w_attn_out", "w_out")), ("in", ex_in, ("w_in",))):
        sent, landed = _scatter_wait("scatter_wait_" + tag, ex[0], ex[1], ex[2], ex[3], dg_norm1)
        for n, src, land in zip(names_, sent, landed):
            own[n], slots[n] = src, land

    sum8 = lambda a: a.reshape(-1, 8, a.shape[-1]).sum(axis=1)
    dconv_s = sum8(dconv.sum(axis=0))
    dffn_s = dffn.sum(axis=0).reshape(2, 4, 8, D_FF).sum(axis=2)
    dffn_w = jnp.concatenate([dffn_s[0, :3], dffn_s[1, :3]], axis=1)
    dffn_b = jnp.concatenate([dffn_s[0, 3:4], dffn_s[1, 3:4]], axis=1)
    fold = lambda a: sum8(a).reshape(N_HEADS, HEAD_DIM).sum(axis=0)[None]
    small_g_local = _pack_small(
        sum8(dg_norm1), sum8(dg_gate), dconv_s[:CONV_WIDTH], dconv_s[CONV_WIDTH:], sum8(dg_convnorm),
        fold(dg_q), fold(dg_k), sum8(dg_norm2), dffn_w, dffn_b)
    small_g = _allreduce_small("allreduce_small_grads", small_g_local)

    res = {}
    for n in order:
        w, m, v = big[n]
        outs = _adam_slots("adam_" + n, me.reshape(1), slots[n], own[n], w, m, v, _ADAM_TILE[w.shape[0]])
        if n in ("w_in", "w_up"):
            outs = [a.T for a in outs]
        res[n] = [a[None] for a in outs]

    col = lambda a, width: lax.dynamic_slice(a, (0, me * width), (a.shape[0], width))
    small_w_true = _pack_small(norm1_g, gate_b, conv_w_full, conv_b, conv_norm_g, q_norm_g, k_norm_g, norm2_g,
                               ffn_w_full, ffn_conv_b)
    place_m = lambda a, full: place_cols(a[0], full)
    small_m = _pack_small(m_norm1_g, m_gate_b, place_m(m_conv_w, D), m_conv_b, m_conv_norm_g, m_q_norm_g, m_k_norm_g,
                          m_norm2_g, place_m(m_ffn_conv_w, 2 * D_FF), m_ffn_conv_b)
    small_v = _pack_small(v_norm1_g, v_gate_b, place_m(v_conv_w, D), v_conv_b, v_conv_norm_g, v_q_norm_g, v_k_norm_g,
                          v_norm2_g, place_m(v_ffn_conv_w, 2 * D_FF), v_ffn_conv_b)
    sd, sm, sv = _adam_small(small_g, small_w_true, small_m, small_v)
    for i, packed in enumerate((small_g, sd, sm, sv)):
        u = _unpack_small(packed)
        u["conv_w"] = col(u["conv_w"], D // N_DEV)
        u["ffn_conv_w"] = col(u["ffn_conv_w"], 2 * D_FF // N_DEV)
        for n, a in u.items():
            res.setdefault(n, [None] * 4)[i] = a[None] if n in ("conv_w", "ffn_conv_w") else a

    names = ["norm1_g", "w_in", "gate_b", "conv_w", "conv_b", "conv_norm_g", "w_conv_out", "q_norm_g", "k_norm_g",
             "w_attn_out", "w_out", "norm2_g", "w_up", "ffn_conv_w", "ffn_conv_b", "w_down"]
    out = [loss, grad_x.reshape(BL, S, D)]
    for i in range(4):
        out += [res[n][i] for n in names]
    return tuple(out)
```

```python
import functools

import jax
import jax.numpy as jnp
from jax import lax
from jax.experimental import pallas as pl
from jax.experimental.pallas import tpu as pltpu

f32 = jnp.float32
bf16 = jnp.bfloat16

D = 1024
N_HEADS = 16
HEAD_DIM = 64
CONV_WIDTH = 31
D_FF = 2816
GROUPS = ((128, 1), (512, 4), (2048, 16))
ATTN_BLOCK = 128
EPS = 1e-6
N_DEV = 8
MESH = pl.DeviceIdType.MESH

ADAM_LR = 0.001
ADAM_B1 = 0.9
ADAM_B2 = 0.999
ADAM_EPS = 1e-08
ADAM_WD = 0.01
ADAM_STEP = 10

VMEM_LIMIT = 56 * 1024 * 1024
MASK_BIAS = 1e30

Z_AVAL, Z_AGATE, Z_GA, Z_GB, Z_Q, Z_K, Z_V = 0, 1, 2, 3, 4, 5, 6


def _wsec_of_zsec(j):
    return jnp.where(j < 2, j, jnp.where(j < 4, j + 3, j - 2))


def _zsec_of_wsec(w):
    return jnp.where(w < 2, w, jnp.where(w < 5, w + 2, w - 3))


def _sig(x):
    return 1.0 / (1.0 + jnp.exp(-x))


def _colsum8(x):
    return x.reshape(-1, 8, x.shape[-1]).sum(axis=0)


def _cparams(sem):
    return pltpu.CompilerParams(dimension_semantics=sem, vmem_limit_bytes=VMEM_LIMIT)


def _my_pos():
    x, y, c = lax.axis_index("x"), lax.axis_index("y"), lax.axis_index("c")
    return x, y, c, 4 * x + 2 * y + c


_DIMS = {"nn": ((1,), (0,)), "nt": ((1,), (1,)), "tn": ((0,), (0,))}


def _matmul_call(name, a, b, a_spec, b_spec, o_spec, out_shape, grid, mode, nk, tm, tn, after=None):
    dims = (_DIMS[mode], ((), ()))
    extra = [] if after is None else [after]

    def body(a_ref, b_ref, *rest):
        o_ref, scratch = rest[len(extra)], rest[len(extra) + 1:]
        part = lax.dot_general(a_ref[...], b_ref[...], dims, preferred_element_type=f32)
        if nk == 1:
            o_ref[...] = part.astype(o_ref.dtype)
        else:
            acc = scratch[0]
            k = pl.program_id(2)

            @pl.when(k == 0)
            def _():
                acc[...] = part

            @pl.when(k > 0)
            def _():
                acc[...] += part

            @pl.when(k == nk - 1)
            def _():
                o_ref[...] = acc[...].astype(o_ref.dtype)

    scratch = [] if nk == 1 else [pltpu.VMEM((tm, tn), f32)]
    return pl.pallas_call(
        body, name=name, grid=grid, in_specs=[a_spec, b_spec] + [pl.BlockSpec(memory_space=pl.ANY)] * len(extra),
        out_specs=o_spec, out_shape=out_shape,
        scratch_shapes=scratch, compiler_params=_cparams(("parallel", "parallel", "arbitrary")),
    )(a, b, *extra)


def _matmul(name, a, b, mode, out_dtype, tm=1024, tn=1024, tk=None, after=None):
    if mode == "nn":
        (M, K), (_, N) = a.shape, b.shape
    elif mode == "nt":
        (M, K), (N, _) = a.shape, b.shape
    else:
        (K, M), (_, N) = a.shape, b.shape
    tm, tn = min(tm, M), min(tn, N)
    tk = K if tk is None else tk
    nk = K // tk
    assert M % tm == 0 and N % tn == 0 and K % tk == 0
    if mode == "tn":
        a_spec = pl.BlockSpec((tk, tm), lambda i, j, k: (k, i))
    else:
        a_spec = pl.BlockSpec((tm, tk), lambda i, j, k: (i, k))
    if mode == "nt":
        b_spec = pl.BlockSpec((tn, tk), lambda i, j, k: (j, k))
    else:
        b_spec = pl.BlockSpec((tk, tn), lambda i, j, k: (k, j))
    o_spec = pl.BlockSpec((tm, tn), lambda i, j, k: (i, j))
    return _matmul_call(name, a, b, a_spec, b_spec, o_spec, jax.ShapeDtypeStruct((M, N), out_dtype),
                        (M // tm, N // tn, nk), mode, nk, tm, tn, after=after)


TT = 512


def _rows(c, cb=0, tt=TT):
    return pl.BlockSpec((tt, c), lambda i: (i, cb))


def _sec(s, tt=TT):
    return pl.BlockSpec((None, tt, D), lambda i: (s, i, 0))


def _const(shape):
    return pl.BlockSpec(shape, lambda i: (0,) * len(shape))


def _acc_spec(c):
    return pl.BlockSpec((8, c), lambda i: (0, 0))


def _rms(x):
    return lax.rsqrt(jnp.mean(x * x, axis=-1, keepdims=True) + EPS)


def _rms_bwd(dy_g, xn, rstd):
    return rstd * (dy_g - xn * jnp.mean(dy_g * xn, axis=-1, keepdims=True))


def _head_sum(x, bd):
    parts = []
    for cb in range(x.shape[-1] // 128):
        xb = x[:, cb * 128:(cb + 1) * 128]
        hi = xb.astype(bf16)
        lo = (xb - hi.astype(f32)).astype(bf16)
        parts.append(jnp.dot(hi, bd, preferred_element_type=f32) + jnp.dot(lo, bd, preferred_element_type=f32))
    return parts[0] if len(parts) == 1 else jnp.concatenate(parts, axis=1)


def _norm1_fwd(x, g):
    T = x.shape[0]

    def body(x_ref, g_ref, h_ref):
        xv = x_ref[...]
        h_ref[...] = (xv * _rms(xv) * g_ref[...]).astype(bf16)

    return pl.pallas_call(
        body, name="norm1_fwd", grid=(T // TT,), in_specs=[_rows(D), _const((1, D))], out_specs=_rows(D),
        out_shape=jax.ShapeDtypeStruct((T, D), bf16), compiler_params=_cparams(("parallel",)))(x, g)


def _convnorm_fwd(c, g):
    T = c.shape[0]

    def body(c_ref, g_ref, s_ref):
        cv = c_ref[...]
        r = cv * _rms(cv) * g_ref[...]
        s_ref[...] = (r * _sig(r)).astype(bf16)

    return pl.pallas_call(
        body, name="convnorm_fwd", grid=(T // TT,), in_specs=[_rows(D), _const((1, D))], out_specs=_rows(D),
        out_shape=jax.ShapeDtypeStruct((T, D), bf16), compiler_params=_cparams(("parallel",)))(c, g)


def _qk_fwd(z8, qg, kg, bd):
    T = z8.shape[1]

    def body(q_ref, k_ref, qg_ref, kg_ref, bd_ref, qn_ref, kn_ref):
        bdv = bd_ref[...]
        q = q_ref[...]
        qn_ref[...] = q * lax.rsqrt(_head_sum(q * q, bdv) * (1.0 / HEAD_DIM) + EPS) * qg_ref[...] * (HEAD_DIM ** -0.5)
        k = k_ref[...]
        kn_ref[...] = k * lax.rsqrt(_head_sum(k * k, bdv) * (1.0 / HEAD_DIM) + EPS) * kg_ref[...]

    return pl.pallas_call(
        body, name="qk_fwd", grid=(T // TT,),
        in_specs=[_sec(Z_Q), _sec(Z_K), _const((1, D)), _const((1, D)), _const((128, 128))],
        out_specs=[_rows(D), _rows(D)],
        out_shape=[jax.ShapeDtypeStruct((T, D), f32)] * 2, compiler_params=_cparams(("parallel",)))(z8, z8, qg, kg, bd)


def _gate_fwd(z8, gate_b, ya, yb):
    T = ya.shape[0]

    def body(ga_ref, gb_ref, b_ref, ya_ref, yb_ref, mixed_ref):
        g_a = _sig(ga_ref[...] + b_ref[:, :D])
        g_b = _sig(gb_ref[...] + b_ref[:, D:])
        mixed_ref[...] = (g_a * ya_ref[...] + g_b * yb_ref[...]).astype(bf16)

    return pl.pallas_call(
        body, name="gate_fwd", grid=(T // TT,),
        in_specs=[_sec(Z_GA), _sec(Z_GB), _const((1, 2 * D)), _rows(D), _rows(D)], out_specs=_rows(D),
        out_shape=jax.ShapeDtypeStruct((T, D), bf16), compiler_params=_cparams(("parallel",)))(z8, z8, gate_b, ya, yb)


def _norm2_fwd(x, t1, g):
    T = x.shape[0]

    def body(x_ref, t_ref, g_ref, x1_ref, h2_ref):
        x1 = x_ref[...] + t_ref[...]
        x1_ref[...] = x1
        h2_ref[...] = (x1 * _rms(x1) * g_ref[...]).astype(bf16)

    return pl.pallas_call(
        body, name="norm2_fwd", grid=(T // TT,), in_specs=[_rows(D), _rows(D), _const((1, D))],
        out_specs=[_rows(D), _rows(D)],
        out_shape=[jax.ShapeDtypeStruct((T, D), f32), jax.ShapeDtypeStruct((T, D), bf16)],
        compiler_params=_cparams(("parallel",)))(x, t1, g)


def _loss_fwd(x1, t2, target):
    T = x1.shape[0]

    def body(x1_ref, t_ref, tg_ref, dy_ref, dyb_ref, acc_ref):
        diff = x1_ref[...] + t_ref[...] - tg_ref[...]
        dy = diff * (1.0 / D)
        dy_ref[...] = dy
        dyb_ref[...] = dy.astype(bf16)

        @pl.when(pl.program_id(0) == 0)
        def _():
            acc_ref[...] = jnp.zeros_like(acc_ref)

        acc_ref[...] += _colsum8(diff * diff)

    return pl.pallas_call(
        body, name="loss_fwd", grid=(T // TT,), in_specs=[_rows(D)] * 3,
        out_specs=[_rows(D), _rows(D), _acc_spec(D)],
        out_shape=[jax.ShapeDtypeStruct((T, D), f32), jax.ShapeDtypeStruct((T, D), bf16),
                   jax.ShapeDtypeStruct((8, D), f32)],
        compiler_params=_cparams(("arbitrary",)))(x1, t2, target)


def _norm2_bwd(x1, dh2, dy, g):
    T = x1.shape[0]

    def body(x1_ref, dh_ref, dy_ref, g_ref, dx1_ref, dx1b_ref, dg_ref):
        x1 = x1_ref[...]
        rstd = _rms(x1)
        xn = x1 * rstd
        dh = dh_ref[...]
        dx1 = dy_ref[...] + _rms_bwd(dh * g_ref[...], xn, rstd)
        dx1_ref[...] = dx1
        dx1b_ref[...] = dx1.astype(bf16)

        @pl.when(pl.program_id(0) == 0)
        def _():
            dg_ref[...] = jnp.zeros_like(dg_ref)

        dg_ref[...] += _colsum8(dh * xn)

    return pl.pallas_call(
        body, name="norm2_bwd", grid=(T // TT,), in_specs=[_rows(D), _rows(D), _rows(D), _const((1, D))],
        out_specs=[_rows(D), _rows(D), _acc_spec(D)],
        out_shape=[jax.ShapeDtypeStruct((T, D), f32), jax.ShapeDtypeStruct((T, D), bf16),
                   jax.ShapeDtypeStruct((8, D), f32)],
        compiler_params=_cparams(("arbitrary",)))(x1, dh2, dy, g)


def _gate_bwd(dmixed, z8, gate_b, ya, yb, dz8):
    T = ya.shape[0]

    def body(dm_ref, ga_ref, gb_ref, b_ref, ya_ref, yb_ref, dz_in, dya_ref, dyb_ref, dz_ref, dgb_ref):
        del dz_in
        dm = dm_ref[...]
        g_a = _sig(ga_ref[...] + b_ref[:, :D])
        g_b = _sig(gb_ref[...] + b_ref[:, D:])
        dya_ref[...] = (dm * g_a).astype(bf16)
        dyb_ref[...] = (dm * g_b).astype(bf16)
        dla = dm * ya_ref[...] * g_a * (1.0 - g_a)
        dlb = dm * yb_ref[...] * g_b * (1.0 - g_b)
        dz_ref[0] = dla.astype(bf16)
        dz_ref[1] = dlb.astype(bf16)

        @pl.when(pl.program_id(0) == 0)
        def _():
            dgb_ref[...] = jnp.zeros_like(dgb_ref)

        dgb_ref[:, :D] += _colsum8(dla)
        dgb_ref[:, D:] += _colsum8(dlb)

    return pl.pallas_call(
        body, name="gate_bwd", grid=(T // TT,),
        in_specs=[_rows(D), _sec(Z_GA), _sec(Z_GB), _const((1, 2 * D)), _rows(D), _rows(D),
                  pl.BlockSpec(memory_space=pl.ANY)],
        out_specs=[_rows(D), _rows(D), pl.BlockSpec((2, TT, D), lambda i: (1, i, 0)), _acc_spec(2 * D)],
        out_shape=[jax.ShapeDtypeStruct((T, D), bf16), jax.ShapeDtypeStruct((T, D), bf16),
                   jax.ShapeDtypeStruct(dz8.shape, bf16), jax.ShapeDtypeStruct((8, 2 * D), f32)],
        input_output_aliases={6: 2},
        compiler_params=_cparams(("arbitrary",)))(dmixed, z8, z8, gate_b, ya, yb, dz8)


def _convnorm_bwd(c, ds, g):
    T = c.shape[0]

    def body(c_ref, ds_ref, g_ref, dc_ref, dg_ref):
        cv = c_ref[...]
        rstd = _rms(cv)
        r0 = cv * rstd
        gv = g_ref[...]
        r = r0 * gv
        sg = _sig(r)
        dr = ds_ref[...] * sg * (1.0 + r * (1.0 - sg))
        dc_ref[...] = _rms_bwd(dr * gv, r0, rstd)

        @pl.when(pl.program_id(0) == 0)
        def _():
            dg_ref[...] = jnp.zeros_like(dg_ref)

        dg_ref[...] += _colsum8(dr * r0)

    return pl.pallas_call(
        body, name="convnorm_bwd", grid=(T // TT,), in_specs=[_rows(D), _rows(D), _const((1, D))],
        out_specs=[_rows(D), _acc_spec(D)],
        out_shape=[jax.ShapeDtypeStruct((T, D), f32), jax.ShapeDtypeStruct((8, D), f32)],
        compiler_params=_cparams(("arbitrary",)))(c, ds, g)


def _qk_bwd(z8, dqn, dkn, dv, qg, kg, bd, dz8):
    T = dqn.shape[0]

    def body(q_ref, k_ref, dqn_ref, dkn_ref, dv_ref, qg_ref, kg_ref, bd_ref, dz_in, dz_ref, dqg_ref, dkg_ref):
        del dz_in
        bdv = bd_ref[...]

        @pl.when(pl.program_id(0) == 0)
        def _():
            dqg_ref[...] = jnp.zeros_like(dqg_ref)
            dkg_ref[...] = jnp.zeros_like(dkg_ref)

        def one(raw, dn_scaled, g, dg_ref, sec):
            rstd = lax.rsqrt(_head_sum(raw * raw, bdv) * (1.0 / HEAD_DIM) + EPS)
            n = raw * rstd
            dg_ref[...] += _colsum8(dn_scaled * n)
            dn = dn_scaled * g
            draw = rstd * (dn - n * (_head_sum(dn * n, bdv) * (1.0 / HEAD_DIM)))
            dz_ref[sec] = draw.astype(bf16)

        one(q_ref[...], dqn_ref[...] * (HEAD_DIM ** -0.5), qg_ref[...], dqg_ref, 0)
        one(k_ref[...], dkn_ref[...], kg_ref[...], dkg_ref, 1)
        dz_ref[2] = dv_ref[...].astype(bf16)
        dz_ref[3] = jnp.zeros((TT, D), bf16)

    return pl.pallas_call(
        body, name="qk_bwd", grid=(T // TT,),
        in_specs=[_sec(Z_Q), _sec(Z_K), _rows(D), _rows(D), _rows(D), _const((1, D)), _const((1, D)),
                  _const((128, 128)), pl.BlockSpec(memory_space=pl.ANY)],
        out_specs=[pl.BlockSpec((4, TT, D), lambda i: (1, i, 0)), _acc_spec(D), _acc_spec(D)],
        out_shape=[jax.ShapeDtypeStruct(dz8.shape, bf16), jax.ShapeDtypeStruct((8, D), f32),
                   jax.ShapeDtypeStruct((8, D), f32)],
        input_output_aliases={8: 0},
        compiler_params=_cparams(("arbitrary",)))(z8, z8, dqn, dkn, dv, qg, kg, bd, dz8)


def _norm1_bwd(x, dh, dx1, g):
    T = x.shape[0]

    def body(x_ref, dh_ref, dx1_ref, g_ref, gx_ref, dg_ref):
        xv = x_ref[...]
        rstd = _rms(xv)
        xn = xv * rstd
        dh = dh_ref[...]
        gx_ref[...] = dx1_ref[...] + _rms_bwd(dh * g_ref[...], xn, rstd)

        @pl.when(pl.program_id(0) == 0)
        def _():
            dg_ref[...] = jnp.zeros_like(dg_ref)

        dg_ref[...] += _colsum8(dh * xn)

    return pl.pallas_call(
        body, name="norm1_bwd", grid=(T // TT,), in_specs=[_rows(D), _rows(D), _rows(D), _const((1, D))],
        out_specs=[_rows(D), _acc_spec(D)],
        out_shape=[jax.ShapeDtypeStruct((T, D), f32), jax.ShapeDtypeStruct((8, D), f32)],
        compiler_params=_cparams(("arbitrary",)))(x, dh, dx1, g)


CCW = 256
CR = 64
HALO = 32


def _conv_fwd(z8, conv_w, conv_b, S):
    T = z8.shape[1]
    nb = T // S
    ncb = D // CCW

    def body(av_ref, ag_ref, w_ref, b_ref, c_ref, pad):
        pad[0:HALO, :] = jnp.zeros((HALO, CCW), f32)

        def fill(i, carry):
            r0 = pl.multiple_of(i * 256, 256)
            pad[pl.ds(HALO + r0, 256), :] = av_ref[pl.ds(r0, 256), :] * _sig(ag_ref[pl.ds(r0, 256), :])
            return carry

        lax.fori_loop(0, S // 256, fill, 0)
        bias = b_ref[...]

        def chunk(i, carry):
            r0 = pl.multiple_of(i * CR, CR)
            win = pad[pl.ds(r0, CR + HALO), :]
            acc = jnp.zeros((CR, CCW), f32) + bias
            for j in range(CONV_WIDTH):
                acc = acc + win[2 + j:2 + j + CR, :] * w_ref[j:j + 1, :]
            c_ref[pl.ds(r0, CR), :] = acc
            return carry

        lax.fori_loop(0, S // CR, chunk, 0)

    zs = lambda s: pl.BlockSpec((None, S, CCW), lambda b, cb: (s, b, cb))
    return pl.pallas_call(
        body, name="conv_fwd", grid=(nb, ncb),
        in_specs=[zs(Z_AVAL), zs(Z_AGATE), pl.BlockSpec((CONV_WIDTH, CCW), lambda b, cb: (0, cb)),
                  pl.BlockSpec((1, CCW), lambda b, cb: (0, cb))],
        out_specs=pl.BlockSpec((S, CCW), lambda b, cb: (b, cb)),
        out_shape=jax.ShapeDtypeStruct((T, D), f32),
        scratch_shapes=[pltpu.VMEM((S + HALO, CCW), f32)],
        compiler_params=_cparams(("parallel", "parallel")))(z8, z8, conv_w, conv_b)


def _conv_bwd(dc, z8, conv_w, dz8, S):
    T = dc.shape[0]
    nb = T // S
    ncb = D // CCW

    def body(dc_ref, av_ref, ag_ref, w_ref, dz_in, dz_ref, dw_ref, apad, dpad):
        del dz_in
        apad[0:HALO, :] = jnp.zeros((HALO, CCW), f32)
        dpad[S:S + HALO, :] = jnp.zeros((HALO, CCW), f32)
        dw_ref[...] = jnp.zeros_like(dw_ref)

        def fill(i, carry):
            r0 = pl.multiple_of(i * 256, 256)
            apad[pl.ds(HALO + r0, 256), :] = av_ref[pl.ds(r0, 256), :] * _sig(ag_ref[pl.ds(r0, 256), :])
            dpad[pl.ds(r0, 256), :] = dc_ref[pl.ds(r0, 256), :]
            return carry

        lax.fori_loop(0, S // 256, fill, 0)

        def chunk(i, carry):
            r0 = pl.multiple_of(i * CR, CR)
            awin = apad[pl.ds(r0, CR + HALO), :]
            dwin = dpad[pl.ds(r0, CR + HALO), :]
            dcc = dwin[0:CR, :]
            da = jnp.zeros((CR, CCW), f32)
            for j in range(CONV_WIDTH):
                da = da + dwin[30 - j:30 - j + CR, :] * w_ref[j:j + 1, :]
                dw_ref[8 * j:8 * j + 8, :] += _colsum8(dcc * awin[2 + j:2 + j + CR, :])
            dw_ref[8 * CONV_WIDTH:8 * CONV_WIDTH + 8, :] += _colsum8(dcc)
            av = av_ref[pl.ds(r0, CR), :]
            sg = _sig(ag_ref[pl.ds(r0, CR), :])
            dz_ref[0, pl.ds(r0, CR), :] = (da * sg).astype(bf16)
            dz_ref[1, pl.ds(r0, CR), :] = (da * av * sg * (1.0 - sg)).astype(bf16)
            return carry

        lax.fori_loop(0, S // CR, chunk, 0)

    zs = lambda s: pl.BlockSpec((None, S, CCW), lambda b, cb: (s, b, cb))
    return pl.pallas_call(
        body, name="conv_bwd", grid=(nb, ncb),
        in_specs=[pl.BlockSpec((S, CCW), lambda b, cb: (b, cb)), zs(Z_AVAL), zs(Z_AGATE),
                  pl.BlockSpec((CONV_WIDTH, CCW), lambda b, cb: (0, cb)), pl.BlockSpec(memory_space=pl.ANY)],
        out_specs=[pl.BlockSpec((2, S, CCW), lambda b, cb: (0, b, cb)),
                   pl.BlockSpec((None, 256, CCW), lambda b, cb: (b, 0, cb))],
        out_shape=[jax.ShapeDtypeStruct(dz8.shape, bf16), jax.ShapeDtypeStruct((nb, 256, D), f32)],
        input_output_aliases={4: 0},
        scratch_shapes=[pltpu.VMEM((S + HALO, CCW), f32), pltpu.VMEM((S + HALO, CCW), f32)],
        compiler_params=_cparams(("parallel", "parallel")))(dc, z8, z8, conv_w, dz8)


FR = 128
NFB = D_FF // CCW


def _ffn_window(ref, i, r0):
    return ref[pl.ds(r0 - 8, FR + 8), :]


def _ffn_u(win, w_ref, b_ref):
    return (win[6:6 + FR, :] * w_ref[0:1, :] + win[7:7 + FR, :] * w_ref[1:2, :]
            + win[8:8 + FR, :] * w_ref[2:3, :] + b_ref[...])


def _ffn_fwd(u3, ffn_w, ffn_b, S):
    T = u3.shape[1]
    nb = T // S

    def body(uv_ref, ug_ref, wv_ref, wg_ref, bv_ref, bg_ref, f_ref):
        def chunk(first, i):
            r0 = 0 if first else pl.multiple_of(i * FR, FR)
            if first:
                z = jnp.zeros((8, CCW), f32)
                wv = jnp.concatenate([z, uv_ref[0:FR, :]], axis=0)
                wg = jnp.concatenate([z, ug_ref[0:FR, :]], axis=0)
            else:
                wv = _ffn_window(uv_ref, i, r0)
                wg = _ffn_window(ug_ref, i, r0)
            u_val = _ffn_u(wv, wv_ref, bv_ref)
            u_gate = _ffn_u(wg, wg_ref, bg_ref)
            f_ref[pl.ds(r0, FR), :] = (u_gate * _sig(u_gate) * u_val).astype(bf16)

        chunk(True, 0)

        def loop(i, carry):
            chunk(False, i)
            return carry

        lax.fori_loop(1, S // FR, loop, 0)

    us = lambda h: pl.BlockSpec((None, S, CCW), lambda b, cb: (h, b, cb))
    ws = lambda h: pl.BlockSpec((3, CCW), lambda b, cb: (0, h * NFB + cb))
    bs = lambda h: pl.BlockSpec((1, CCW), lambda b, cb: (0, h * NFB + cb))
    return pl.pallas_call(
        body, name="ffn_fwd", grid=(nb, NFB),
        in_specs=[us(0), us(1), ws(0), ws(1), bs(0), bs(1)],
        out_specs=pl.BlockSpec((S, CCW), lambda b, cb: (b, cb)),
        out_shape=jax.ShapeDtypeStruct((T, D_FF), bf16),
        compiler_params=_cparams(("parallel", "parallel")))(u3, u3, ffn_w, ffn_w, ffn_b, ffn_b)


def _ffn_bwd(u3, df, ffn_w, ffn_b, S):
    T = u3.shape[1]
    nb = T // S

    def body(uv_ref, ug_ref, df_ref, wv_ref, wg_ref, bv_ref, bg_ref, du_ref, dw_ref, dvpad, dgpad):
        dvpad[S:S + 8, :] = jnp.zeros((8, CCW), f32)
        dgpad[S:S + 8, :] = jnp.zeros((8, CCW), f32)
        dw_ref[...] = jnp.zeros_like(dw_ref)

        def chunk(first, i):
            r0 = 0 if first else pl.multiple_of(i * FR, FR)
            if first:
                z = jnp.zeros((8, CCW), f32)
                wv = jnp.concatenate([z, uv_ref[0:FR, :]], axis=0)
                wg = jnp.concatenate([z, ug_ref[0:FR, :]], axis=0)
            else:
                wv = _ffn_window(uv_ref, i, r0)
                wg = _ffn_window(ug_ref, i, r0)
            u_val = _ffn_u(wv, wv_ref, bv_ref)
            u_gate = _ffn_u(wg, wg_ref, bg_ref)
            dfc = df_ref[pl.ds(r0, FR), :]
            sg = _sig(u_gate)
            d_val = dfc * u_gate * sg
            d_gate = dfc * u_val * sg * (1.0 + u_gate * (1.0 - sg))
            dvpad[pl.ds(r0, FR), :] = d_val
            dgpad[pl.ds(r0, FR), :] = d_gate
            for h, (dd, win) in enumerate(((d_val, wv), (d_gate, wg))):
                for j in range(3):
                    dw_ref[h, 8 * j:8 * j + 8, :] += _colsum8(dd * win[6 + j:6 + j + FR, :])
                dw_ref[h, 24:32, :] += _colsum8(dd)

        chunk(True, 0)

        def loop(i, carry):
            chunk(False, i)
            return carry

        lax.fori_loop(1, S // FR, loop, 0)

        def back(i, carry):
            r0 = pl.multiple_of(i * FR, FR)
            for h, (dpad, w_ref) in enumerate(((dvpad, wv_ref), (dgpad, wg_ref))):
                win = dpad[pl.ds(r0, FR + 8), :]
                du = (win[0:FR, :] * w_ref[2:3, :] + win[1:1 + FR, :] * w_ref[1:2, :]
                      + win[2:2 + FR, :] * w_ref[0:1, :])
                du_ref[h, pl.ds(r0, FR), :] = du.astype(bf16)
            return carry

        lax.fori_loop(0, S // FR, back, 0)

    us = lambda h: pl.BlockSpec((None, S, CCW), lambda b, cb: (h, b, cb))
    ws = lambda h: pl.BlockSpec((3, CCW), lambda b, cb: (0, h * NFB + cb))
    bs = lambda h: pl.BlockSpec((1, CCW), lambda b, cb: (0, h * NFB + cb))
    return pl.pallas_call(
        body, name="ffn_bwd", grid=(nb, NFB),
        in_specs=[us(0), us(1), pl.BlockSpec((S, CCW), lambda b, cb: (b, cb)), ws(0), ws(1), bs(0), bs(1)],
        out_specs=[pl.BlockSpec((2, S, CCW), lambda b, cb: (0, b, cb)),
                   pl.BlockSpec((None, 2, 32, CCW), lambda b, cb: (b, 0, 0, cb))],
        out_shape=[jax.ShapeDtypeStruct((2, T, D_FF), bf16), jax.ShapeDtypeStruct((nb, 2, 32, D_FF), f32)],
        scratch_shapes=[pltpu.VMEM((S + 8, CCW), f32), pltpu.VMEM((S + 8, CCW), f32)],
        compiler_params=_cparams(("parallel", "parallel")))(u3, u3, df, ffn_w, ffn_w, ffn_b, ffn_b)


AB = ATTN_BLOCK


def _attn_bias():
    slopes = 2.0 ** (-8.0 * jnp.arange(1, N_HEADS + 1, dtype=f32) / N_HEADS)
    steps = (jnp.arange(AB)[:, None] + AB) - jnp.arange(2 * AB)[None, :]
    own = (jnp.arange(2 * AB) >= AB)[None, :]
    out = []
    for window, dil in GROUPS:
        valid = (steps >= 0) & (steps <= window // dil)
        dist = slopes[:, None, None] * (steps * dil).astype(f32)[None]
        kinds = [jnp.where(v[None], dist, MASK_BIAS) for v in (valid, valid & own)]
        out.append(jnp.stack(kinds, axis=1))
    return jnp.stack(out)


def _head_masks():
    lane = lax.broadcasted_iota(jnp.int32, (1, 128), 1)
    return (lane < HEAD_DIM, lane >= HEAD_DIM)


def _perm_chunks(S, d):
    L = S // d
    ch = min(L, 256)
    out = []
    for r in range(d):
        for c in range(L // ch):
            start = r + d * ch * c
            out.append((pl.ds(start, ch, stride=d) if d > 1 else pl.ds(start, ch), r * L + c * ch, ch))
    return out


def _stack_heads(x, masks):
    return jnp.concatenate([jnp.where(masks[0], x, 0), jnp.where(masks[1], x, 0)], axis=0)


_NT = (((1,), (1,)), ((), ()))
_TN = (((0,), (0,)), ((), ()))
SCH = 64


def _attn_fwd(qn, kn, z8, bias, S):
    T = qn.shape[0]
    nb = T // S
    nblk = S // AB

    def body(q_ref, k_ref, v_ref, bias_ref, o_ref, ob_ref, lse_ref, qs, ks, vs, s2, p2, ogp, lgp, *group_scratch):
        og, lg = group_scratch[:3], group_scratch[3:]
        masks = _head_masks()
        ks[0:AB, :] = jnp.zeros((AB, 128), bf16)
        vs[0:AB, :] = jnp.zeros((AB, 128), bf16)

        for g, (_, d) in enumerate(GROUPS):
            nsub = S // (d * AB)
            chunks = _perm_chunks(S, d)
            for src, dst, ch in chunks:
                qs[dst:dst + ch, :] = q_ref[src, :].astype(bf16)
                ks[AB + dst:AB + dst + ch, :] = k_ref[src, :].astype(bf16)
                vs[AB + dst:AB + dst + ch, :] = v_ref[src, :].astype(bf16)
            od, ld = (og[g], lg[g]) if d == 1 else (ogp, lgp)

            def scores(j, carry):
                r0 = pl.multiple_of(j * AB, AB)
                q2 = _stack_heads(qs[pl.ds(r0, AB), :], masks)
                s2[j] = lax.dot_general(q2, ks[pl.ds(r0, 2 * AB), :], _NT, preferred_element_type=f32)
                return carry

            lax.fori_loop(0, nblk, scores, 0, unroll=8)

            def softmax(j, carry, g=g, nsub=nsub, ld=ld):
                r0 = pl.multiple_of(j * AB, AB)
                kind = (j % nsub == 0).astype(jnp.int32)
                for cc in range(AB // SCH):
                    lses = []
                    for hh in range(2):
                        rows = pl.ds(hh * AB + cc * SCH, SCH)
                        sb = s2[j, rows, :] - bias_ref[g, hh, kind, cc * SCH:(cc + 1) * SCH, :]
                        m = jnp.max(sb, axis=-1, keepdims=True)
                        p = jnp.exp(sb - m)
                        den = jnp.sum(p, axis=-1, keepdims=True)
                        p2[j, rows, :] = (p * (1.0 / den)).astype(bf16)
                        lses.append(m + jnp.log(den))
                    ld[pl.ds(r0 + cc * SCH, SCH), :] = jnp.where(masks[0], lses[0], lses[1])
                return carry

            lax.fori_loop(0, nblk, softmax, 0, unroll=2)

            def values(j, carry, od=od):
                r0 = pl.multiple_of(j * AB, AB)
                pv2 = jnp.dot(p2[j], vs[pl.ds(r0, 2 * AB), :], preferred_element_type=f32)
                od[pl.ds(r0, AB), :] = jnp.where(masks[0], pv2[:AB], pv2[AB:])
                return carry

            lax.fori_loop(0, nblk, values, 0, unroll=8)

            if d > 1:
                for src, dst, ch in chunks:
                    og[g][src, :] = ogp[dst:dst + ch, :]
                    lg[g][src, :] = lgp[dst:dst + ch, :]

        def combine(i, carry):
            rr = pl.ds(pl.multiple_of(i * 256, 256), 256)
            l0, l1, l2 = lg[0][rr, :], lg[1][rr, :], lg[2][rr, :]
            mx = jnp.maximum(jnp.maximum(l0, l1), l2)
            e0, e1, e2 = jnp.exp(l0 - mx), jnp.exp(l1 - mx), jnp.exp(l2 - mx)
            den = e0 + e1 + e2
            o = (e0 * og[0][rr, :] + e1 * og[1][rr, :] + e2 * og[2][rr, :]) / den
            o_ref[rr, :] = o
            ob_ref[rr, :] = o.astype(bf16)
            lse_ref[rr, :] = mx + jnp.log(den)
            return carry

        lax.fori_loop(0, S // 256, combine, 0)

    blk = pl.BlockSpec((S, 128), lambda b, hp: (b, hp))
    return pl.pallas_call(
        body, name="attn_fwd", grid=(nb, N_HEADS // 2),
        in_specs=[blk, blk, pl.BlockSpec((None, S, 128), lambda b, hp: (Z_V, b, hp)),
                  pl.BlockSpec((3, 2, 2, AB, 2 * AB), lambda b, hp: (0, hp, 0, 0, 0))],
        out_specs=[blk, blk, blk],
        out_shape=[jax.ShapeDtypeStruct((T, D), f32), jax.ShapeDtypeStruct((T, D), bf16),
                   jax.ShapeDtypeStruct((T, D), f32)],
        scratch_shapes=[pltpu.VMEM((S, 128), bf16), pltpu.VMEM((S + AB, 128), bf16), pltpu.VMEM((S + AB, 128), bf16),
                        pltpu.VMEM((nblk, 2 * AB, 2 * AB), f32), pltpu.VMEM((nblk, 2 * AB, 2 * AB), bf16),
                        pltpu.VMEM((S, 128), f32), pltpu.VMEM((S, 128), f32)] + [pltpu.VMEM((S, 128), f32)] * 6,
        compiler_params=_cparams(("parallel", "parallel")))(qn, kn, z8, bias)


def _attn_bwd(qn, kn, z8, do, o, lse, bias, bd, S):
    T = qn.shape[0]
    nb = T // S

    nblk = S // AB

    def body(q_ref, k_ref, v_ref, do_ref, o_ref, lse_ref, bias_ref, bd_ref, dq_ref, dk_ref, dv_ref,
             delta, qs, ks, vs, dos, lsp, dlp, s2, dp2, p2, ds2, dqp, dkp, dvp):
        masks = _head_masks()
        bdv = bd_ref[...]
        dq_ref[...] = jnp.zeros_like(dq_ref)
        dk_ref[...] = jnp.zeros_like(dk_ref)
        dv_ref[...] = jnp.zeros_like(dv_ref)
        ks[0:AB, :] = jnp.zeros((AB, 128), bf16)
        vs[0:AB, :] = jnp.zeros((AB, 128), bf16)

        def prep(i, carry):
            rr = pl.ds(pl.multiple_of(i * 256, 256), 256)
            delta[rr, :] = _head_sum(do_ref[rr, :] * o_ref[rr, :], bdv)
            return carry

        lax.fori_loop(0, S // 256, prep, 0)

        for g, (_, d) in enumerate(GROUPS):
            nsub = S // (d * AB)
            chunks = _perm_chunks(S, d)
            for src, dst, ch in chunks:
                qs[dst:dst + ch, :] = q_ref[src, :].astype(bf16)
                ks[AB + dst:AB + dst + ch, :] = k_ref[src, :].astype(bf16)
                vs[AB + dst:AB + dst + ch, :] = v_ref[src, :].astype(bf16)
                dos[dst:dst + ch, :] = do_ref[src, :].astype(bf16)
                lsp[dst:dst + ch, :] = lse_ref[src, :]
                dlp[dst:dst + ch, :] = delta[src, :]
            dkp[...] = jnp.zeros_like(dkp)
            dvp[...] = jnp.zeros_like(dvp)

            def scores(j, carry):
                r0 = pl.multiple_of(j * AB, AB)
                q2 = _stack_heads(qs[pl.ds(r0, AB), :], masks)
                do2 = _stack_heads(dos[pl.ds(r0, AB), :], masks)
                s2[j] = lax.dot_general(q2, ks[pl.ds(r0, 2 * AB), :], _NT, preferred_element_type=f32)
                dp2[j] = lax.dot_general(do2, vs[pl.ds(r0, 2 * AB), :], _NT, preferred_element_type=f32)
                return carry

            lax.fori_loop(0, nblk, scores, 0, unroll=8)

            def probs(j, carry, g=g, nsub=nsub):
                r0 = pl.multiple_of(j * AB, AB)
                kind = (j % nsub == 0).astype(jnp.int32)
                for cc in range(AB // SCH):
                    lse_c = lsp[pl.ds(r0 + cc * SCH, SCH), :]
                    del_c = dlp[pl.ds(r0 + cc * SCH, SCH), :]
                    for hh in range(2):
                        c0 = hh * HEAD_DIM
                        rows = pl.ds(hh * AB + cc * SCH, SCH)
                        sb = s2[j, rows, :] - bias_ref[g, hh, kind, cc * SCH:(cc + 1) * SCH, :]
                        p = jnp.exp(sb - lse_c[:, c0:c0 + 1])
                        p2[j, rows, :] = p.astype(bf16)
                        ds2[j, rows, :] = (p * (dp2[j, rows, :] - del_c[:, c0:c0 + 1])).astype(bf16)
                return carry

            lax.fori_loop(0, nblk, probs, 0, unroll=2)

            def grads(j, carry):
                r0 = pl.multiple_of(j * AB, AB)
                q2 = _stack_heads(qs[pl.ds(r0, AB), :], masks)
                do2 = _stack_heads(dos[pl.ds(r0, AB), :], masks)
                dsb = ds2[j]
                t = jnp.dot(dsb, ks[pl.ds(r0, 2 * AB), :], preferred_element_type=f32)
                dqp[pl.ds(r0, AB), :] = jnp.where(masks[0], t[:AB], t[AB:])
                dkp[pl.ds(r0, 2 * AB), :] += lax.dot_general(dsb, q2, _TN, preferred_element_type=f32)
                dvp[pl.ds(r0, 2 * AB), :] += lax.dot_general(p2[j], do2, _TN, preferred_element_type=f32)
                return carry

            lax.fori_loop(0, nblk, grads, 0, unroll=4)

            for src, dst, ch in chunks:
                dq_ref[src, :] += dqp[dst:dst + ch, :]
                dk_ref[src, :] += dkp[AB + dst:AB + dst + ch, :]
                dv_ref[src, :] += dvp[AB + dst:AB + dst + ch, :]

    blk = pl.BlockSpec((S, 128), lambda b, hp: (b, hp))
    row = lambda dt, pad=0: pltpu.VMEM((S + pad, 128), dt)
    blocks = lambda dt: pltpu.VMEM((nblk, 2 * AB, 2 * AB), dt)
    return pl.pallas_call(
        body, name="attn_bwd", grid=(nb, N_HEADS // 2),
        in_specs=[blk, blk, pl.BlockSpec((None, S, 128), lambda b, hp: (Z_V, b, hp)), blk, blk, blk,
                  pl.BlockSpec((3, 2, 2, AB, 2 * AB), lambda b, hp: (0, hp, 0, 0, 0)),
                  pl.BlockSpec((128, 128), lambda b, hp: (0, 0))],
        out_specs=[blk, blk, blk],
        out_shape=[jax.ShapeDtypeStruct((T, D), f32)] * 3,
        scratch_shapes=[row(f32), row(bf16), row(bf16, AB), row(bf16, AB), row(bf16), row(f32), row(f32),
                        blocks(f32), blocks(f32), blocks(bf16), blocks(bf16), row(f32), row(f32, AB), row(f32, AB)],
        compiler_params=_cparams(("parallel", "parallel")))(qn, kn, z8, do, o, lse, bias, bd)


def _any_spec():
    return pl.BlockSpec(memory_space=pl.ANY)


def _allgather_rows(shards, n_full):
    n = len(shards)

    def body(*refs):
        ins, outs = refs[:n], refs[n:2 * n]
        send_sems, recv_sems, local_sems = refs[2 * n:]
        x, y, c, me = _my_pos()
        sibling = (x, y, 1 - c)
        chips = [(1 - x, y), (x, 1 - y), (1 - x, 1 - y)]

        def idx(px, py, pc):
            return 4 * px + 2 * py + pc

        def copy(a, k, blk, to, src=None):
            return pltpu.make_async_remote_copy(
                src_ref=outs[a].at[blk] if src is None else src, dst_ref=outs[a].at[blk],
                send_sem=send_sems.at[a, k], recv_sem=recv_sems.at[a, k], device_id=to, device_id_type=MESH)

        mine = [pltpu.make_async_copy(ins[a], outs[a].at[me], local_sems.at[a]) for a in range(n)]
        for cp in mine:
            cp.start()
        first = []
        for a in range(n_full):
            first.append(copy(a, 0, me, sibling, src=ins[a]))
            first += [copy(a, 1 + j, me, (*chip, c), src=ins[a]) for j, chip in enumerate(chips)]
        for cp in first:
            cp.start()
        passed = []
        for a in range(n_full):
            for j, chip in enumerate(chips):
                blk = idx(*chip, c)
                copy(a, 1 + j, blk, (x, y, c)).wait_recv()
                cp = copy(a, 4 + j, blk, sibling)
                cp.start()
                passed.append(cp)
        for a in range(n_full):
            copy(a, 0, idx(x, y, 1 - c), (x, y, c)).wait_recv()
            for j, chip in enumerate(chips):
                copy(a, 4 + j, idx(*chip, 1 - c), (x, y, c)).wait_recv()
        for cp in first + passed:
            cp.wait_send()
        for cp in mine:
            cp.wait()

    return pl.pallas_call(
        body, name="allgather_weights",
        in_specs=[_any_spec()] * n, out_specs=[_any_spec()] * n,
        out_shape=[jax.ShapeDtypeStruct((N_DEV,) + s.shape, s.dtype) for s in shards],
        scratch_shapes=[pltpu.SemaphoreType.DMA((n_full, 7)), pltpu.SemaphoreType.DMA((n_full, 7)),
                        pltpu.SemaphoreType.DMA((n,))],
    )(*shards)


def _peer(x, y, c, k):
    tx = 1 - x if (k >> 2) & 1 else x
    ty = 1 - y if (k >> 1) & 1 else y
    tc = 1 - c if k & 1 else c
    return (tx, ty, tc), 4 * tx + 2 * ty + tc


_PEER_ORDER = (2, 4, 6, 3, 5, 7, 1)


_HBM = pl.BlockSpec(memory_space=pltpu.HBM)
_SEM = pl.BlockSpec(memory_space=pltpu.SEMAPHORE)
_EFFECT = pltpu.SideEffectType.DATAFLOW_SIDE_EFFECTING


def _exchange_copies(srcs, lands, send_sems, recv_sems, gather):
    x, y, c, me = _my_pos()
    copies = []
    for k in _PEER_ORDER:
        tgt, tidx = _peer(x, y, c, k)
        for a in range(len(srcs)):
            copies.append(pltpu.make_async_remote_copy(
                src_ref=srcs[a] if gather else srcs[a].at[tidx], dst_ref=lands[a].at[me],
                send_sem=send_sems.at[7 * a + k - 1], recv_sem=recv_sems.at[7 * a + k - 1],
                device_id=tgt, device_id_type=MESH))
    return copies


def _exchange_start(name, srcs, lands=None, after=None):
    n = len(srcs)
    gather = lands is not None
    if lands is None:
        lands = [lax.empty(g.shape, g.dtype) for g in srcs]
    extra = [] if after is None else [after]

    def body(*refs):
        src_refs, land_refs = refs[:n], refs[n:2 * n]
        send_sems, recv_sems = refs[2 * n + len(extra)], refs[2 * n + len(extra) + 1]
        token = refs[-1]
        for cp in _exchange_copies(src_refs, land_refs, send_sems, recv_sems, gather):
            cp.start()
        token[...] = jnp.zeros_like(token)

    hbm = lambda a: pltpu.with_memory_space_constraint(a, pltpu.HBM)
    outs = pl.pallas_call(
        body, name=name,
        out_shape=(pltpu.SemaphoreType.DMA((7 * n,)), pltpu.SemaphoreType.DMA((7 * n,)),
                   *[pltpu.HBM(g.shape, g.dtype) for g in list(srcs) + list(lands)],
                   jax.ShapeDtypeStruct((8, 128), f32)),
        in_specs=[_HBM] * (2 * n) + [pl.BlockSpec(memory_space=pl.ANY)] * len(extra),
        out_specs=(_SEM, _SEM, *([_HBM] * (2 * n)), pl.BlockSpec(memory_space=pltpu.VMEM)),
        input_output_aliases={i: 2 + i for i in range(2 * n)},
        compiler_params=pltpu.CompilerParams(has_side_effects=_EFFECT),
    )(*[hbm(g) for g in srcs], *[hbm(g) for g in lands], *extra)
    return outs[0], outs[1], list(outs[2:2 + n]), list(outs[2 + n:2 + 2 * n]), outs[-1], gather


def _exchange_wait(name, started, after):
    send_sems, recv_sems, srcs, lands, _, gather = started
    n = len(srcs)

    def body(*refs):
        src_refs, land_refs = refs[:n], refs[n:2 * n]
        s_sems, r_sems = refs[2 * n], refs[2 * n + 1]
        for cp in _exchange_copies(src_refs, land_refs, s_sems, r_sems, gather):
            cp.wait_send()
            cp.wait_recv()

    outs = pl.pallas_call(
        body, name=name,
        out_shape=tuple(pltpu.HBM(a.shape, a.dtype) for a in list(srcs) + list(lands)),
        in_specs=[_HBM] * (2 * n) + [_SEM, _SEM, pl.BlockSpec(memory_space=pl.ANY)],
        out_specs=tuple([_HBM] * (2 * n)),
        input_output_aliases={i: i for i in range(2 * n)},
        compiler_params=pltpu.CompilerParams(has_side_effects=_EFFECT),
    )(*srcs, *lands, send_sems, recv_sems, after)
    return list(outs[:n]), list(outs[n:])


SMALL_ROWS = 64


def _allreduce_small(name, sg):
    def body(sg_ref, out_ref, buf, send_sems, recv_sems):
        x, y, c, me = _my_pos()
        buf[me] = sg_ref[...]
        copies = []
        for k in _PEER_ORDER:
            tgt, _ = _peer(x, y, c, k)
            cp = pltpu.make_async_remote_copy(
                src_ref=sg_ref, dst_ref=buf.at[me], send_sem=send_sems.at[k - 1], recv_sem=recv_sems.at[k - 1],
                device_id=tgt, device_id_type=MESH)
            cp.start()
            copies.append(cp)
        for cp in copies:
            cp.wait()
        acc = buf[0]
        for p in range(1, N_DEV):
            acc = acc + buf[p]
        out_ref[...] = acc

    return pl.pallas_call(
        body, name=name,
        in_specs=[pl.BlockSpec(memory_space=pltpu.VMEM)], out_specs=pl.BlockSpec(memory_space=pltpu.VMEM),
        out_shape=jax.ShapeDtypeStruct(sg.shape, f32),
        scratch_shapes=[pltpu.VMEM((N_DEV,) + sg.shape, f32), pltpu.SemaphoreType.DMA((7,)),
                        pltpu.SemaphoreType.DMA((7,))],
    )(sg)


def _adam_math(g, w, m, v):
    m = ADAM_B1 * m + (1.0 - ADAM_B1) * g
    v = ADAM_B2 * v + (1.0 - ADAM_B2) * (g * g)
    m_hat = m / (1.0 - ADAM_B1 ** ADAM_STEP)
    v_hat = v / (1.0 - ADAM_B2 ** ADAM_STEP)
    delta = -ADAM_LR * (m_hat / (jnp.sqrt(v_hat) + ADAM_EPS) + ADAM_WD * w)
    return delta, m, v


def _adam_slots(name, me, slots, own, w, m, v, tr):
    rows = w.shape[0]

    def body(me_ref, s_ref, own_ref, w_ref, m_ref, v_ref, g_ref, d_ref, nm_ref, nv_ref):
        mine = own_ref[...]
        g = None
        for p in range(N_DEV):
            term = lax.cond(me_ref[0] == p, lambda: mine, lambda p=p: s_ref[p]).astype(f32)
            g = term if g is None else g + term
        delta, nm, nv = _adam_math(g, w_ref[...], m_ref[...], v_ref[...])
        g_ref[...] = g
        d_ref[...] = delta
        nm_ref[...] = nm
        nv_ref[...] = nv

    rs = pl.BlockSpec((tr, D), lambda i, me_ref: (i, 0))
    return pl.pallas_call(
        body, name=name,
        grid_spec=pltpu.PrefetchScalarGridSpec(
            num_scalar_prefetch=1, grid=(rows // tr,),
            in_specs=[pl.BlockSpec((N_DEV, tr, D), lambda i, me_ref: (0, i, 0)),
                      pl.BlockSpec((None, tr, D), lambda i, me_ref: (me_ref[0], i, 0)), rs, rs, rs],
            out_specs=[rs] * 4),
        out_shape=[jax.ShapeDtypeStruct((rows, D), f32)] * 4,
        compiler_params=_cparams(("parallel",)))(me, slots, own, w, m, v)


def _adam_small(g, w, m, v):
    def body(g_ref, w_ref, m_ref, v_ref, d_ref, nm_ref, nv_ref):
        delta, nm, nv = _adam_math(g_ref[...], w_ref[...], m_ref[...], v_ref[...])
        d_ref[...] = delta
        nm_ref[...] = nm
        nv_ref[...] = nv

    return pl.pallas_call(body, name="adam_small", out_shape=[jax.ShapeDtypeStruct(g.shape, f32)] * 3)(g, w, m, v)


FFN_PAD = 6 * D


def _pack_small(norm1_g, gate_b, conv_w, conv_b, conv_norm_g, q_norm_g, k_norm_g, norm2_g, ffn_conv_w, ffn_conv_b):
    pad_h = lambda a: jnp.pad(a, ((0, 0), (0, D - HEAD_DIM)))
    pad_f = lambda a: jnp.pad(a, ((0, 0), (0, FFN_PAD - 2 * D_FF))).reshape(-1, D)
    parts = [norm1_g, gate_b.reshape(2, D), conv_w, conv_b, conv_norm_g, pad_h(q_norm_g), pad_h(k_norm_g), norm2_g,
             pad_f(ffn_conv_w), pad_f(ffn_conv_b)]
    out = jnp.concatenate(parts, axis=0)
    return jnp.pad(out, ((0, SMALL_ROWS - out.shape[0]), (0, 0)))


def _unpack_small(p):
    ffn = lambda a: a.reshape(-1, FFN_PAD)[:, :2 * D_FF]
    return dict(
        norm1_g=p[0:1], gate_b=p[1:3].reshape(1, 2 * D), conv_w=p[3:34], conv_b=p[34:35], conv_norm_g=p[35:36],
        q_norm_g=p[36:37, :HEAD_DIM], k_norm_g=p[37:38, :HEAD_DIM], norm2_g=p[38:39],
        ffn_conv_w=ffn(p[39:57]), ffn_conv_b=ffn(p[57:63]))


_ADAM_TILE = {896: 128, 704: 64, 128: 128, 352: 176}


def kernel(x, norm1_g, w_in, gate_b, conv_w, conv_b, conv_norm_g, w_conv_out, q_norm_g, k_norm_g, w_attn_out, w_out, norm2_g, w_up, ffn_conv_w, ffn_conv_b, w_down, loss_target, m_norm1_g, m_w_in, m_gate_b, m_conv_w, m_conv_b, m_conv_norm_g, m_w_conv_out, m_q_norm_g, m_k_norm_g, m_w_attn_out, m_w_out, m_norm2_g, m_w_up, m_ffn_conv_w, m_ffn_conv_b, m_w_down, v_norm1_g, v_w_in, v_gate_b, v_conv_w, v_conv_b, v_conv_norm_g, v_w_conv_out, v_q_norm_g, v_k_norm_g, v_w_attn_out, v_w_out, v_norm2_g, v_w_up, v_ffn_conv_w, v_ffn_conv_b, v_w_down):
    BL, S, _ = x.shape
    T = BL * S
    me = 4 * lax.axis_index("x") + 2 * lax.axis_index("y") + lax.axis_index("c")
    xt = x.reshape(T, D)
    target = loss_target.reshape(T, D)

    big = dict(w_in=(w_in[0].T, m_w_in[0].T, v_w_in[0].T), w_up=(w_up[0].T, m_w_up[0].T, v_w_up[0].T),
               w_conv_out=(w_conv_out[0], m_w_conv_out[0], v_w_conv_out[0]),
               w_attn_out=(w_attn_out[0], m_w_attn_out[0], v_w_attn_out[0]),
               w_out=(w_out[0], m_w_out[0], v_w_out[0]), w_down=(w_down[0], m_w_down[0], v_w_down[0]))
    order = ["w_in", "w_conv_out", "w_attn_out", "w_out", "w_up", "w_down"]
    shards = [big[n][0].astype(bf16) for n in order]
    gathered = _allgather_rows(shards, 1)
    ga_proj = _exchange_start("gather_start_proj", shards[1:4], gathered[1:4], after=gathered[0])
    ga_ffn = _exchange_start("gather_start_ffn", shards[4:6], gathered[4:6], after=ga_proj[4])
    W = {"w_in": gathered[0].reshape(-1, D)}

    def place_cols(shard, full_cols):
        z = jnp.zeros((shard.shape[0], full_cols), f32)
        return lax.dynamic_update_slice(z, shard, (0, me * shard.shape[1]))

    zr = lambda a: jnp.zeros_like(a)
    conv_local = _pack_small(
        zr(norm1_g), zr(gate_b), place_cols(conv_w[0], D), zr(conv_b), zr(conv_norm_g), zr(q_norm_g), zr(k_norm_g),
        zr(norm2_g), place_cols(ffn_conv_w[0], 2 * D_FF), zr(ffn_conv_b))
    conv_all = _unpack_small(_allreduce_small("gather_conv_weights", conv_local))
    conv_w_full, ffn_w_full = conv_all["conv_w"], conv_all["ffn_conv_w"]

    bd = (jnp.arange(128)[:, None] // HEAD_DIM == jnp.arange(128)[None, :] // HEAD_DIM).astype(bf16)
    bias = _attn_bias()
    qg = jnp.tile(q_norm_g, (1, N_HEADS))
    kg = jnp.tile(k_norm_g, (1, N_HEADS))

    h = _norm1_fwd(xt, norm1_g)
    z8 = _matmul_call(
        "mm_z", h, W["w_in"],
        pl.BlockSpec((1024, D), lambda i, j, k: (i, 0)),
        pl.BlockSpec((1024, D), lambda i, j, k: (_wsec_of_zsec(j), 0)),
        pl.BlockSpec((None, 1024, D), lambda i, j, k: (j, i, 0)),
        jax.ShapeDtypeStruct((8, T, D), f32), (T // 1024, 7, 1), "nt", 1, 1024, 1024, after=ga_ffn[4])
    c = _conv_fwd(z8, conv_w_full, conv_b, S)
    s = _convnorm_fwd(c, conv_norm_g)
    qn, kn = _qk_fwd(z8, qg, kg, bd)
    for n, g in zip(order[1:4], _exchange_wait("gather_wait_proj", ga_proj, qn)[1]):
        W[n] = g.reshape(-1, D)
    ya = _matmul("mm_ya", s, W["w_conv_out"], "nn", f32)
    o, ob, lse = _attn_fwd(qn, kn, z8, bias, S)
    yb = _matmul("mm_yb", ob, W["w_attn_out"], "nn", f32)
    mixed = _gate_fwd(z8, gate_b, ya, yb)
    t1 = _matmul("mm_t1", mixed, W["w_out"], "nn", f32)
    for n, g in zip(order[4:6], _exchange_wait("gather_wait_ffn", ga_ffn, t1)[1]):
        W[n] = g.reshape(-1, D)
    x1, h2 = _norm2_fwd(xt, t1, norm2_g)
    TNU = D_FF // 2
    u3 = _matmul_call(
        "mm_u", h2, W["w_up"],
        pl.BlockSpec((1024, D), lambda i, j, k: (i, 0)),
        pl.BlockSpec((TNU, D), lambda i, j, k: (j, 0)),
        pl.BlockSpec((None, 1024, TNU), lambda i, j, k: (j // 2, i, j % 2)),
        jax.ShapeDtypeStruct((2, T, D_FF), f32), (T // 1024, 4, 1), "nt", 1, 1024, TNU)
    f = _ffn_fwd(u3, ffn_w_full, ffn_conv_b, S)
    t2 = _matmul("mm_t2", f, W["w_down"], "nn", f32, tk=TNU)
    dy, dyb, lacc = _loss_fwd(x1, t2, target)
    loss = lax.psum(0.5 / D * jnp.sum(lacc), ("x", "y", "c"))

    df = _matmul("mm_df", dyb, W["w_down"], "nt", f32, tn=TNU)
    g_w_down = _matmul("mm_dwdn", f, dyb, "tn", bf16, tm=TNU, tk=1024)
    du3, dffn = _ffn_bwd(u3, df, ffn_w_full, ffn_conv_b, S)
    g_w_up = _matmul_call(
        "mm_dwup", du3, h2,
        pl.BlockSpec((None, 1024, TNU), lambda i, j, k: (i // 2, k, i % 2)),
        pl.BlockSpec((1024, D), lambda i, j, k: (k, 0)),
        pl.BlockSpec((TNU, D), lambda i, j, k: (i, 0)),
        jax.ShapeDtypeStruct((2 * D_FF, D), bf16), (4, 1, T // 1024), "tn", T // 1024, TNU, D)
    blocks8 = lambda a: a.reshape(N_DEV, -1, D)
    ex_ffn = _exchange_start("scatter_start_ffn", [blocks8(g_w_up), blocks8(g_w_down)])
    dh2 = _matmul_call(
        "mm_dh2", du3, W["w_up"],
        pl.BlockSpec((None, 1024, TNU), lambda i, j, k: (k // 2, i, k % 2)),
        pl.BlockSpec((TNU, D), lambda i, j, k: (k, 0)),
        pl.BlockSpec((1024, D), lambda i, j, k: (i, 0)),
        jax.ShapeDtypeStruct((T, D), f32), (T // 1024, 1, 4), "nn", 4, 1024, D, after=ex_ffn[4])
    dx1, dx1b, dg_norm2 = _norm2_bwd(x1, dh2, dy, norm2_g)
    dmixed = _matmul("mm_dmixed", dx1b, W["w_out"], "nt", f32)
    g_w_out = _matmul("mm_dwo", mixed, dx1b, "tn", bf16, tk=1024)
    dz8 = lax.empty((8, T, D), bf16)
    dya, dyb2, dz8, dg_gate = _gate_bwd(dmixed, z8, gate_b, ya, yb, dz8)
    ds = _matmul("mm_ds", dya, W["w_conv_out"], "nt", f32)
    g_w_conv_out = _matmul("mm_dwco", s, dya, "tn", bf16, tk=1024)
    g_w_attn_out = _matmul("mm_dwao", ob, dyb2, "tn", bf16, tk=1024)
    ex_proj = _exchange_start("scatter_start_proj", [blocks8(g_w_conv_out), blocks8(g_w_attn_out), blocks8(g_w_out)])
    do = _matmul("mm_do", dyb2, W["w_attn_out"], "nt", f32, after=ex_proj[4])
    dc, dg_convnorm = _convnorm_bwd(c, ds, conv_norm_g)
    dz8a, dconv = _conv_bwd(dc, z8, conv_w_full, dz8, S)
    dqn, dkn, dv = _attn_bwd(qn, kn, z8, do, o, lse, bias, bd, S)
    dz8b, dg_q, dg_k = _qk_bwd(z8, dqn, dkn, dv, qg, kg, bd, dz8a)
    g_w_in = _matmul_call(
        "mm_dwin", dz8b, h,
        pl.BlockSpec((None, 1024, D), lambda i, j, k: (_zsec_of_wsec(i), k, 0)),
        pl.BlockSpec((1024, D), lambda i, j, k: (k, 0)),
        pl.BlockSpec((1024, D), lambda i, j, k: (i, 0)),
        jax.ShapeDtypeStruct((7 * D, D), bf16), (7, 1, T // 1024), "tn", T // 1024, D, D)
    ex_in = _exchange_start("scatter_start_in", [blocks8(g_w_in)])
    dh = _matmul_call(
        "mm_dh", dz8b, W["w_in"],
        pl.BlockSpec((None, 1024, D), lambda i, j, k: (k, i, 0)),
        pl.BlockSpec((1024, D), lambda i, j, k: (_wsec_of_zsec(k), 0)),
        pl.BlockSpec((1024, D), lambda i, j, k: (i, 0)),
        jax.ShapeDtypeStruct((T, D), f32), (T // 1024, 1, 7), "nn", 7, 1024, D, after=ex_in[4])
    grad_x, dg_norm1 = _norm1_bwd(xt, dh, dx1, norm1_g)

    own, slots = {}, {}
    for tag, ex, names_ in (("ffn", ex_ffn, ("w_up", "w_down")),
                            ("proj", ex_proj, ("w_conv_out", "w_attn_out", "w_out")), ("in", ex_in, ("w_in",))):
        sent, landed = _exchange_wait("scatter_wait_" + tag, ex, dg_norm1)
        for n, src, land in zip(names_, sent, landed):
            own[n], slots[n] = src, land

    sum8 = lambda a: a.reshape(-1, 8, a.shape[-1]).sum(axis=1)
    dconv_s = sum8(dconv.sum(axis=0))
    dffn_s = dffn.sum(axis=0).reshape(2, 4, 8, D_FF).sum(axis=2)
    dffn_w = jnp.concatenate([dffn_s[0, :3], dffn_s[1, :3]], axis=1)
    dffn_b = jnp.concatenate([dffn_s[0, 3:4], dffn_s[1, 3:4]], axis=1)
    fold = lambda a: sum8(a).reshape(N_HEADS, HEAD_DIM).sum(axis=0)[None]
    small_g_local = _pack_small(
        sum8(dg_norm1), sum8(dg_gate), dconv_s[:CONV_WIDTH], dconv_s[CONV_WIDTH:], sum8(dg_convnorm),
        fold(dg_q), fold(dg_k), sum8(dg_norm2), dffn_w, dffn_b)
    small_g = _allreduce_small("allreduce_small_grads", small_g_local)

    res = {}
    for n in order:
        w, m, v = big[n]
        outs = _adam_slots("adam_" + n, me.reshape(1), slots[n], own[n], w, m, v, _ADAM_TILE[w.shape[0]])
        if n in ("w_in", "w_up"):
            outs = [a.T for a in outs]
        res[n] = [a[None] for a in outs]

    col = lambda a, width: lax.dynamic_slice(a, (0, me * width), (a.shape[0], width))
    small_w_true = _pack_small(norm1_g, gate_b, conv_w_full, conv_b, conv_norm_g, q_norm_g, k_norm_g, norm2_g,
                               ffn_w_full, ffn_conv_b)
    place_m = lambda a, full: place_cols(a[0], full)
    small_m = _pack_small(m_norm1_g, m_gate_b, place_m(m_conv_w, D), m_conv_b, m_conv_norm_g, m_q_norm_g, m_k_norm_g,
                          m_norm2_g, place_m(m_ffn_conv_w, 2 * D_FF), m_ffn_conv_b)
    small_v = _pack_small(v_norm1_g, v_gate_b, place_m(v_conv_w, D), v_conv_b, v_conv_norm_g, v_q_norm_g, v_k_norm_g,
                          v_norm2_g, place_m(v_ffn_conv_w, 2 * D_FF), v_ffn_conv_b)
    sd, sm, sv = _adam_small(small_g, small_w_true, small_m, small_v)
    for i, packed in enumerate((small_g, sd, sm, sv)):
        u = _unpack_small(packed)
        u["conv_w"] = col(u["conv_w"], D // N_DEV)
        u["ffn_conv_w"] = col(u["ffn_conv_w"], 2 * D_FF // N_DEV)
        for n, a in u.items():
            res.setdefault(n, [None] * 4)[i] = a[None] if n in ("conv_w", "ffn_conv_w") else a

    names = ["norm1_g", "w_in", "gate_b", "conv_w", "conv_b", "conv_norm_g", "w_conv_out", "q_norm_g", "k_norm_g",
             "w_attn_out", "w_out", "norm2_g", "w_up", "ffn_conv_w", "ffn_conv_b", "w_down"]
    out = [loss, grad_x.reshape(BL, S, D)]
    for i in range(4):
        out += [res[n][i] for n in names]
    return tuple(out)
```

```python
import functools

import jax
import jax.numpy as jnp
from jax import lax
from jax.experimental import pallas as pl
from jax.experimental.pallas import tpu as pltpu

f32 = jnp.float32
bf16 = jnp.bfloat16

D = 1024
N_HEADS = 16
HEAD_DIM = 64
CONV_WIDTH = 31
D_FF = 2816
GROUPS = ((128, 1), (512, 4), (2048, 16))
ATTN_BLOCK = 128
EPS = 1e-6
N_DEV = 8
MESH = pl.DeviceIdType.MESH

ADAM_LR = 0.001
ADAM_B1 = 0.9
ADAM_B2 = 0.999
ADAM_EPS = 1e-08
ADAM_WD = 0.01
ADAM_STEP = 10

VMEM_LIMIT = 56 * 1024 * 1024
MASK_BIAS = 1e30

Z_AVAL, Z_AGATE, Z_GA, Z_GB, Z_Q, Z_K, Z_V = 0, 1, 2, 3, 4, 5, 6


def _wsec_of_zsec(j):
    return jnp.where(j < 2, j, jnp.where(j < 4, j + 3, j - 2))


def _zsec_of_wsec(w):
    return jnp.where(w < 2, w, jnp.where(w < 5, w + 2, w - 3))


def _sig(x):
    return 1.0 / (1.0 + jnp.exp(-x))


def _colsum8(x):
    return x.reshape(-1, 8, x.shape[-1]).sum(axis=0)


def _cparams(sem):
    return pltpu.CompilerParams(dimension_semantics=sem, vmem_limit_bytes=VMEM_LIMIT)


def _my_pos():
    x, y, c = lax.axis_index("x"), lax.axis_index("y"), lax.axis_index("c")
    return x, y, c, 4 * x + 2 * y + c


_DIMS = {"nn": ((1,), (0,)), "nt": ((1,), (1,)), "tn": ((0,), (0,))}


def _matmul_call(name, a, b, a_spec, b_spec, o_spec, out_shape, grid, mode, nk, tm, tn, after=None):
    dims = (_DIMS[mode], ((), ()))
    extra = [] if after is None else [after]

    def body(a_ref, b_ref, *rest):
        o_ref, scratch = rest[len(extra)], rest[len(extra) + 1:]
        part = lax.dot_general(a_ref[...], b_ref[...], dims, preferred_element_type=f32)
        if nk == 1:
            o_ref[...] = part.astype(o_ref.dtype)
        else:
            acc = scratch[0]
            k = pl.program_id(2)

            @pl.when(k == 0)
            def _():
                acc[...] = part

            @pl.when(k > 0)
            def _():
                acc[...] += part

            @pl.when(k == nk - 1)
            def _():
                o_ref[...] = acc[...].astype(o_ref.dtype)

    scratch = [] if nk == 1 else [pltpu.VMEM((tm, tn), f32)]
    return pl.pallas_call(
        body, name=name, grid=grid, in_specs=[a_spec, b_spec] + [pl.BlockSpec(memory_space=pl.ANY)] * len(extra),
        out_specs=o_spec, out_shape=out_shape,
        scratch_shapes=scratch, compiler_params=_cparams(("parallel", "parallel", "arbitrary")),
    )(a, b, *extra)


def _matmul(name, a, b, mode, out_dtype, tm=1024, tn=1024, tk=None, after=None):
    if mode == "nn":
        (M, K), (_, N) = a.shape, b.shape
    elif mode == "nt":
        (M, K), (N, _) = a.shape, b.shape
    else:
        (K, M), (_, N) = a.shape, b.shape
    tm, tn = min(tm, M), min(tn, N)
    tk = K if tk is None else tk
    nk = K // tk
    assert M % tm == 0 and N % tn == 0 and K % tk == 0
    if mode == "tn":
        a_spec = pl.BlockSpec((tk, tm), lambda i, j, k: (k, i))
    else:
        a_spec = pl.BlockSpec((tm, tk), lambda i, j, k: (i, k))
    if mode == "nt":
        b_spec = pl.BlockSpec((tn, tk), lambda i, j, k: (j, k))
    else:
        b_spec = pl.BlockSpec((tk, tn), lambda i, j, k: (k, j))
    o_spec = pl.BlockSpec((tm, tn), lambda i, j, k: (i, j))
    return _matmul_call(name, a, b, a_spec, b_spec, o_spec, jax.ShapeDtypeStruct((M, N), out_dtype),
                        (M // tm, N // tn, nk), mode, nk, tm, tn, after=after)


TT = 512


def _rows(c, cb=0, tt=TT):
    return pl.BlockSpec((tt, c), lambda i: (i, cb))


def _sec(s, tt=TT):
    return pl.BlockSpec((None, tt, D), lambda i: (s, i, 0))


def _const(shape):
    return pl.BlockSpec(shape, lambda i: (0,) * len(shape))


def _acc_spec(c):
    return pl.BlockSpec((8, c), lambda i: (0, 0))


def _rms(x):
    return lax.rsqrt(jnp.mean(x * x, axis=-1, keepdims=True) + EPS)


def _rms_bwd(dy_g, xn, rstd):
    return rstd * (dy_g - xn * jnp.mean(dy_g * xn, axis=-1, keepdims=True))


def _head_sum(x, bd):
    parts = []
    for cb in range(x.shape[-1] // 128):
        xb = x[:, cb * 128:(cb + 1) * 128]
        hi = xb.astype(bf16)
        lo = (xb - hi.astype(f32)).astype(bf16)
        parts.append(jnp.dot(hi, bd, preferred_element_type=f32) + jnp.dot(lo, bd, preferred_element_type=f32))
    return parts[0] if len(parts) == 1 else jnp.concatenate(parts, axis=1)


def _norm1_fwd(x, g):
    T = x.shape[0]

    def body(x_ref, g_ref, h_ref):
        xv = x_ref[...]
        h_ref[...] = (xv * _rms(xv) * g_ref[...]).astype(bf16)

    return pl.pallas_call(
        body, name="norm1_fwd", grid=(T // TT,), in_specs=[_rows(D), _const((1, D))], out_specs=_rows(D),
        out_shape=jax.ShapeDtypeStruct((T, D), bf16), compiler_params=_cparams(("parallel",)))(x, g)


def _convnorm_fwd(c, g):
    T = c.shape[0]

    def body(c_ref, g_ref, s_ref):
        cv = c_ref[...]
        r = cv * _rms(cv) * g_ref[...]
        s_ref[...] = (r * _sig(r)).astype(bf16)

    return pl.pallas_call(
        body, name="convnorm_fwd", grid=(T // TT,), in_specs=[_rows(D), _const((1, D))], out_specs=_rows(D),
        out_shape=jax.ShapeDtypeStruct((T, D), bf16), compiler_params=_cparams(("parallel",)))(c, g)


def _qk_fwd(z8, qg, kg, bd):
    T = z8.shape[1]

    def body(q_ref, k_ref, qg_ref, kg_ref, bd_ref, qn_ref, kn_ref):
        bdv = bd_ref[...]
        q = q_ref[...]
        qn_ref[...] = q * lax.rsqrt(_head_sum(q * q, bdv) * (1.0 / HEAD_DIM) + EPS) * qg_ref[...] * (HEAD_DIM ** -0.5)
        k = k_ref[...]
        kn_ref[...] = k * lax.rsqrt(_head_sum(k * k, bdv) * (1.0 / HEAD_DIM) + EPS) * kg_ref[...]

    return pl.pallas_call(
        body, name="qk_fwd", grid=(T // TT,),
        in_specs=[_sec(Z_Q), _sec(Z_K), _const((1, D)), _const((1, D)), _const((128, 128))],
        out_specs=[_rows(D), _rows(D)],
        out_shape=[jax.ShapeDtypeStruct((T, D), f32)] * 2, compiler_params=_cparams(("parallel",)))(z8, z8, qg, kg, bd)


def _gate_fwd(z8, gate_b, ya, yb):
    T = ya.shape[0]

    def body(ga_ref, gb_ref, b_ref, ya_ref, yb_ref, mixed_ref):
        g_a = _sig(ga_ref[...] + b_ref[:, :D])
        g_b = _sig(gb_ref[...] + b_ref[:, D:])
        mixed_ref[...] = (g_a * ya_ref[...] + g_b * yb_ref[...]).astype(bf16)

    return pl.pallas_call(
        body, name="gate_fwd", grid=(T // TT,),
        in_specs=[_sec(Z_GA), _sec(Z_GB), _const((1, 2 * D)), _rows(D), _rows(D)], out_specs=_rows(D),
        out_shape=jax.ShapeDtypeStruct((T, D), bf16), compiler_params=_cparams(("parallel",)))(z8, z8, gate_b, ya, yb)


def _norm2_fwd(x, t1, g):
    T = x.shape[0]

    def body(x_ref, t_ref, g_ref, x1_ref, h2_ref):
        x1 = x_ref[...] + t_ref[...]
        x1_ref[...] = x1
        h2_ref[...] = (x1 * _rms(x1) * g_ref[...]).astype(bf16)

    return pl.pallas_call(
        body, name="norm2_fwd", grid=(T // TT,), in_specs=[_rows(D), _rows(D), _const((1, D))],
        out_specs=[_rows(D), _rows(D)],
        out_shape=[jax.ShapeDtypeStruct((T, D), f32), jax.ShapeDtypeStruct((T, D), bf16)],
        compiler_params=_cparams(("parallel",)))(x, t1, g)


def _loss_fwd(x1, t2, target):
    T = x1.shape[0]

    def body(x1_ref, t_ref, tg_ref, dy_ref, dyb_ref, acc_ref):
        diff = x1_ref[...] + t_ref[...] - tg_ref[...]
        dy = diff * (1.0 / D)
        dy_ref[...] = dy
        dyb_ref[...] = dy.astype(bf16)

        @pl.when(pl.program_id(0) == 0)
        def _():
            acc_ref[...] = jnp.zeros_like(acc_ref)

        acc_ref[...] += _colsum8(diff * diff)

    return pl.pallas_call(
        body, name="loss_fwd", grid=(T // TT,), in_specs=[_rows(D)] * 3,
        out_specs=[_rows(D), _rows(D), _acc_spec(D)],
        out_shape=[jax.ShapeDtypeStruct((T, D), f32), jax.ShapeDtypeStruct((T, D), bf16),
                   jax.ShapeDtypeStruct((8, D), f32)],
        compiler_params=_cparams(("arbitrary",)))(x1, t2, target)


def _norm2_bwd(x1, dh2, dy, g):
    T = x1.shape[0]

    def body(x1_ref, dh_ref, dy_ref, g_ref, dx1_ref, dx1b_ref, dg_ref):
        x1 = x1_ref[...]
        rstd = _rms(x1)
        xn = x1 * rstd
        dh = dh_ref[...]
        dx1 = dy_ref[...] + _rms_bwd(dh * g_ref[...], xn, rstd)
        dx1_ref[...] = dx1
        dx1b_ref[...] = dx1.astype(bf16)

        @pl.when(pl.program_id(0) == 0)
        def _():
            dg_ref[...] = jnp.zeros_like(dg_ref)

        dg_ref[...] += _colsum8(dh * xn)

    return pl.pallas_call(
        body, name="norm2_bwd", grid=(T // TT,), in_specs=[_rows(D), _rows(D), _rows(D), _const((1, D))],
        out_specs=[_rows(D), _rows(D), _acc_spec(D)],
        out_shape=[jax.ShapeDtypeStruct((T, D), f32), jax.ShapeDtypeStruct((T, D), bf16),
                   jax.ShapeDtypeStruct((8, D), f32)],
        compiler_params=_cparams(("arbitrary",)))(x1, dh2, dy, g)


def _gate_bwd(dmixed, z8, gate_b, ya, yb, dz8):
    T = ya.shape[0]

    def body(dm_ref, ga_ref, gb_ref, b_ref, ya_ref, yb_ref, dz_in, dya_ref, dyb_ref, dz_ref, dgb_ref):
        del dz_in
        dm = dm_ref[...]
        g_a = _sig(ga_ref[...] + b_ref[:, :D])
        g_b = _sig(gb_ref[...] + b_ref[:, D:])
        dya_ref[...] = (dm * g_a).astype(bf16)
        dyb_ref[...] = (dm * g_b).astype(bf16)
        dla = dm * ya_ref[...] * g_a * (1.0 - g_a)
        dlb = dm * yb_ref[...] * g_b * (1.0 - g_b)
        dz_ref[0] = dla.astype(bf16)
        dz_ref[1] = dlb.astype(bf16)

        @pl.when(pl.program_id(0) == 0)
        def _():
            dgb_ref[...] = jnp.zeros_like(dgb_ref)

        dgb_ref[:, :D] += _colsum8(dla)
        dgb_ref[:, D:] += _colsum8(dlb)

    return pl.pallas_call(
        body, name="gate_bwd", grid=(T // TT,),
        in_specs=[_rows(D), _sec(Z_GA), _sec(Z_GB), _const((1, 2 * D)), _rows(D), _rows(D),
                  pl.BlockSpec(memory_space=pl.ANY)],
        out_specs=[_rows(D), _rows(D), pl.BlockSpec((2, TT, D), lambda i: (1, i, 0)), _acc_spec(2 * D)],
        out_shape=[jax.ShapeDtypeStruct((T, D), bf16), jax.ShapeDtypeStruct((T, D), bf16),
                   jax.ShapeDtypeStruct(dz8.shape, bf16), jax.ShapeDtypeStruct((8, 2 * D), f32)],
        input_output_aliases={6: 2},
        compiler_params=_cparams(("arbitrary",)))(dmixed, z8, z8, gate_b, ya, yb, dz8)


def _convnorm_bwd(c, ds, g):
    T = c.shape[0]

    def body(c_ref, ds_ref, g_ref, dc_ref, dg_ref):
        cv = c_ref[...]
        rstd = _rms(cv)
        r0 = cv * rstd
        gv = g_ref[...]
        r = r0 * gv
        sg = _sig(r)
        dr = ds_ref[...] * sg * (1.0 + r * (1.0 - sg))
        dc_ref[...] = _rms_bwd(dr * gv, r0, rstd)

        @pl.when(pl.program_id(0) == 0)
        def _():
            dg_ref[...] = jnp.zeros_like(dg_ref)

        dg_ref[...] += _colsum8(dr * r0)

    return pl.pallas_call(
        body, name="convnorm_bwd", grid=(T // TT,), in_specs=[_rows(D), _rows(D), _const((1, D))],
        out_specs=[_rows(D), _acc_spec(D)],
        out_shape=[jax.ShapeDtypeStruct((T, D), f32), jax.ShapeDtypeStruct((8, D), f32)],
        compiler_params=_cparams(("arbitrary",)))(c, ds, g)


def _qk_bwd(z8, dqn, dkn, dv, qg, kg, bd, dz8):
    T = dqn.shape[0]

    def body(q_ref, k_ref, dqn_ref, dkn_ref, dv_ref, qg_ref, kg_ref, bd_ref, dz_in, dz_ref, dqg_ref, dkg_ref):
        del dz_in
        bdv = bd_ref[...]

        @pl.when(pl.program_id(0) == 0)
        def _():
            dqg_ref[...] = jnp.zeros_like(dqg_ref)
            dkg_ref[...] = jnp.zeros_like(dkg_ref)

        def one(raw, dn_scaled, g, dg_ref, sec):
            rstd = lax.rsqrt(_head_sum(raw * raw, bdv) * (1.0 / HEAD_DIM) + EPS)
            n = raw * rstd
            dg_ref[...] += _colsum8(dn_scaled * n)
            dn = dn_scaled * g
            draw = rstd * (dn - n * (_head_sum(dn * n, bdv) * (1.0 / HEAD_DIM)))
            dz_ref[sec] = draw.astype(bf16)

        one(q_ref[...], dqn_ref[...] * (HEAD_DIM ** -0.5), qg_ref[...], dqg_ref, 0)
        one(k_ref[...], dkn_ref[...], kg_ref[...], dkg_ref, 1)
        dz_ref[2] = dv_ref[...].astype(bf16)
        dz_ref[3] = jnp.zeros((TT, D), bf16)

    return pl.pallas_call(
        body, name="qk_bwd", grid=(T // TT,),
        in_specs=[_sec(Z_Q), _sec(Z_K), _rows(D), _rows(D), _rows(D), _const((1, D)), _const((1, D)),
                  _const((128, 128)), pl.BlockSpec(memory_space=pl.ANY)],
        out_specs=[pl.BlockSpec((4, TT, D), lambda i: (1, i, 0)), _acc_spec(D), _acc_spec(D)],
        out_shape=[jax.ShapeDtypeStruct(dz8.shape, bf16), jax.ShapeDtypeStruct((8, D), f32),
                   jax.ShapeDtypeStruct((8, D), f32)],
        input_output_aliases={8: 0},
        compiler_params=_cparams(("arbitrary",)))(z8, z8, dqn, dkn, dv, qg, kg, bd, dz8)


def _norm1_bwd(x, dh, dx1, g):
    T = x.shape[0]

    def body(x_ref, dh_ref, dx1_ref, g_ref, gx_ref, dg_ref):
        xv = x_ref[...]
        rstd = _rms(xv)
        xn = xv * rstd
        dh = dh_ref[...]
        gx_ref[...] = dx1_ref[...] + _rms_bwd(dh * g_ref[...], xn, rstd)

        @pl.when(pl.program_id(0) == 0)
        def _():
            dg_ref[...] = jnp.zeros_like(dg_ref)

        dg_ref[...] += _colsum8(dh * xn)

    return pl.pallas_call(
        body, name="norm1_bwd", grid=(T // TT,), in_specs=[_rows(D), _rows(D), _rows(D), _const((1, D))],
        out_specs=[_rows(D), _acc_spec(D)],
        out_shape=[jax.ShapeDtypeStruct((T, D), f32), jax.ShapeDtypeStruct((8, D), f32)],
        compiler_params=_cparams(("arbitrary",)))(x, dh, dx1, g)


CCW = 256
CR = 64
HALO = 32


def _conv_fwd(z8, conv_w, conv_b, S):
    T = z8.shape[1]
    nb = T // S
    ncb = D // CCW

    def body(av_ref, ag_ref, w_ref, b_ref, c_ref, pad):
        pad[0:HALO, :] = jnp.zeros((HALO, CCW), f32)

        def fill(i, carry):
            r0 = pl.multiple_of(i * 256, 256)
            pad[pl.ds(HALO + r0, 256), :] = av_ref[pl.ds(r0, 256), :] * _sig(ag_ref[pl.ds(r0, 256), :])
            return carry

        lax.fori_loop(0, S // 256, fill, 0)
        bias = b_ref[...]

        def chunk(i, carry):
            r0 = pl.multiple_of(i * CR, CR)
            win = pad[pl.ds(r0, CR + HALO), :]
            acc = jnp.zeros((CR, CCW), f32) + bias
            for s in range(8):
                part = None
                for m in range((CONV_WIDTH - 1 - s) // 8 + 1):
                    j = CONV_WIDTH - 1 - 8 * m - s
                    term = win[24 - 8 * m:24 - 8 * m + CR + 8, :] * w_ref[j:j + 1, :]
                    part = term if part is None else part + term
                acc = acc + part[8 - s:8 - s + CR, :]
            c_ref[pl.ds(r0, CR), :] = acc
            return carry

        lax.fori_loop(0, S // CR, chunk, 0)

    zs = lambda s: pl.BlockSpec((None, S, CCW), lambda b, cb: (s, b, cb))
    return pl.pallas_call(
        body, name="conv_fwd", grid=(nb, ncb),
        in_specs=[zs(Z_AVAL), zs(Z_AGATE), pl.BlockSpec((CONV_WIDTH, CCW), lambda b, cb: (0, cb)),
                  pl.BlockSpec((1, CCW), lambda b, cb: (0, cb))],
        out_specs=pl.BlockSpec((S, CCW), lambda b, cb: (b, cb)),
        out_shape=jax.ShapeDtypeStruct((T, D), f32),
        scratch_shapes=[pltpu.VMEM((S + HALO, CCW), f32)],
        compiler_params=_cparams(("parallel", "parallel")))(z8, z8, conv_w, conv_b)


def _conv_bwd(dc, z8, conv_w, dz8, S):
    T = dc.shape[0]
    nb = T // S
    ncb = D // CCW

    def body(dc_ref, av_ref, ag_ref, w_ref, dz_in, dz_ref, dw_ref, apad, dpad, shbuf):
        del dz_in
        apad[0:HALO, :] = jnp.zeros((HALO, CCW), f32)
        dpad[S:S + HALO, :] = jnp.zeros((HALO, CCW), f32)
        dw_ref[...] = jnp.zeros_like(dw_ref)

        def fill(i, carry):
            r0 = pl.multiple_of(i * 256, 256)
            apad[pl.ds(HALO + r0, 256), :] = av_ref[pl.ds(r0, 256), :] * _sig(ag_ref[pl.ds(r0, 256), :])
            dpad[pl.ds(r0, 256), :] = dc_ref[pl.ds(r0, 256), :]
            return carry

        lax.fori_loop(0, S // 256, fill, 0)

        def chunk(i, carry):
            r0 = pl.multiple_of(i * CR, CR)
            dwin = dpad[pl.ds(r0, CR + HALO), :]
            da = jnp.zeros((CR, CCW), f32)
            for s in range(8):
                shbuf[...] = dwin[s:s + CR, :]
                dshift = shbuf[...]
                part = None
                for m in range((CONV_WIDTH - 1 - s) // 8 + 1):
                    j = CONV_WIDTH - 1 - 8 * m - s
                    term = dwin[8 * m:8 * m + CR + 8, :] * w_ref[j:j + 1, :]
                    part = term if part is None else part + term
                    a_lag = apad[pl.ds(r0 + HALO - 8 * m, CR), :]
                    dw_ref[8 * j:8 * j + 8, :] += _colsum8(dshift * a_lag)
                da = da + part[s:s + CR, :]
            dw_ref[8 * CONV_WIDTH:8 * CONV_WIDTH + 8, :] += _colsum8(dwin[0:CR, :])
            av = av_ref[pl.ds(r0, CR), :]
            sg = _sig(ag_ref[pl.ds(r0, CR), :])
            dz_ref[0, pl.ds(r0, CR), :] = (da * sg).astype(bf16)
            dz_ref[1, pl.ds(r0, CR), :] = (da * av * sg * (1.0 - sg)).astype(bf16)
            return carry

        lax.fori_loop(0, S // CR, chunk, 0)

    zs = lambda s: pl.BlockSpec((None, S, CCW), lambda b, cb: (s, b, cb))
    return pl.pallas_call(
        body, name="conv_bwd", grid=(nb, ncb),
        in_specs=[pl.BlockSpec((S, CCW), lambda b, cb: (b, cb)), zs(Z_AVAL), zs(Z_AGATE),
                  pl.BlockSpec((CONV_WIDTH, CCW), lambda b, cb: (0, cb)), pl.BlockSpec(memory_space=pl.ANY)],
        out_specs=[pl.BlockSpec((2, S, CCW), lambda b, cb: (0, b, cb)),
                   pl.BlockSpec((None, 256, CCW), lambda b, cb: (b, 0, cb))],
        out_shape=[jax.ShapeDtypeStruct(dz8.shape, bf16), jax.ShapeDtypeStruct((nb, 256, D), f32)],
        input_output_aliases={4: 0},
        scratch_shapes=[pltpu.VMEM((S + HALO, CCW), f32), pltpu.VMEM((S + HALO, CCW), f32),
                        pltpu.VMEM((CR, CCW), f32)],
        compiler_params=_cparams(("parallel", "parallel")))(dc, z8, z8, conv_w, dz8)


FR = 128
NFB = D_FF // CCW


def _ffn_window(ref, i, r0):
    return ref[pl.ds(r0 - 8, FR + 8), :]


def _ffn_u(win, w_ref, b_ref):
    return (win[6:6 + FR, :] * w_ref[0:1, :] + win[7:7 + FR, :] * w_ref[1:2, :]
            + win[8:8 + FR, :] * w_ref[2:3, :] + b_ref[...])


def _ffn_fwd(u3, ffn_w, ffn_b, S):
    T = u3.shape[1]
    nb = T // S

    def body(uv_ref, ug_ref, wv_ref, wg_ref, bv_ref, bg_ref, f_ref):
        def chunk(first, i):
            r0 = 0 if first else pl.multiple_of(i * FR, FR)
            if first:
                z = jnp.zeros((8, CCW), f32)
                wv = jnp.concatenate([z, uv_ref[0:FR, :]], axis=0)
                wg = jnp.concatenate([z, ug_ref[0:FR, :]], axis=0)
            else:
                wv = _ffn_window(uv_ref, i, r0)
                wg = _ffn_window(ug_ref, i, r0)
            u_val = _ffn_u(wv, wv_ref, bv_ref)
            u_gate = _ffn_u(wg, wg_ref, bg_ref)
            f_ref[pl.ds(r0, FR), :] = (u_gate * _sig(u_gate) * u_val).astype(bf16)

        chunk(True, 0)

        def loop(i, carry):
            chunk(False, i)
            return carry

        lax.fori_loop(1, S // FR, loop, 0)

    us = lambda h: pl.BlockSpec((None, S, CCW), lambda b, cb: (h, b, cb))
    ws = lambda h: pl.BlockSpec((3, CCW), lambda b, cb: (0, h * NFB + cb))
    bs = lambda h: pl.BlockSpec((1, CCW), lambda b, cb: (0, h * NFB + cb))
    return pl.pallas_call(
        body, name="ffn_fwd", grid=(nb, NFB),
        in_specs=[us(0), us(1), ws(0), ws(1), bs(0), bs(1)],
        out_specs=pl.BlockSpec((S, CCW), lambda b, cb: (b, cb)),
        out_shape=jax.ShapeDtypeStruct((T, D_FF), bf16),
        compiler_params=_cparams(("parallel", "parallel")))(u3, u3, ffn_w, ffn_w, ffn_b, ffn_b)


def _ffn_bwd(u3, df, ffn_w, ffn_b, S):
    T = u3.shape[1]
    nb = T // S

    def body(uv_ref, ug_ref, df_ref, wv_ref, wg_ref, bv_ref, bg_ref, du_ref, dw_ref, dvpad, dgpad, shbuf):
        dvpad[S:S + 8, :] = jnp.zeros((8, CCW), f32)
        dgpad[S:S + 8, :] = jnp.zeros((8, CCW), f32)
        dw_ref[...] = jnp.zeros_like(dw_ref)

        def chunk(first, i):
            r0 = 0 if first else pl.multiple_of(i * FR, FR)
            if first:
                z = jnp.zeros((8, CCW), f32)
                wv = jnp.concatenate([z, uv_ref[0:FR, :]], axis=0)
                wg = jnp.concatenate([z, ug_ref[0:FR, :]], axis=0)
            else:
                wv = _ffn_window(uv_ref, i, r0)
                wg = _ffn_window(ug_ref, i, r0)
            taps = []
            for h, win in enumerate((wv, wg)):
                shbuf[2 * h] = win[6:6 + FR, :]
                shbuf[2 * h + 1] = win[7:7 + FR, :]
                taps.append((shbuf[2 * h], shbuf[2 * h + 1], win[8:8 + FR, :]))
            conv = lambda x, w_ref, b_ref: (x[0] * w_ref[0:1, :] + x[1] * w_ref[1:2, :] + x[2] * w_ref[2:3, :]
                                            + b_ref[...])
            u_val = conv(taps[0], wv_ref, bv_ref)
            u_gate = conv(taps[1], wg_ref, bg_ref)
            dfc = df_ref[pl.ds(r0, FR), :]
            sg = _sig(u_gate)
            d_val = dfc * u_gate * sg
            d_gate = dfc * u_val * sg * (1.0 + u_gate * (1.0 - sg))
            dvpad[pl.ds(r0, FR), :] = d_val
            dgpad[pl.ds(r0, FR), :] = d_gate
            for h, dd in enumerate((d_val, d_gate)):
                for j in range(3):
                    dw_ref[h, 8 * j:8 * j + 8, :] += _colsum8(dd * taps[h][j])
                dw_ref[h, 24:32, :] += _colsum8(dd)

        chunk(True, 0)

        def loop(i, carry):
            chunk(False, i)
            return carry

        lax.fori_loop(1, S // FR, loop, 0)

        def back(i, carry):
            r0 = pl.multiple_of(i * FR, FR)
            for h, (dpad, w_ref) in enumerate(((dvpad, wv_ref), (dgpad, wg_ref))):
                win = dpad[pl.ds(r0, FR + 8), :]
                du = (win[0:FR, :] * w_ref[2:3, :] + win[1:1 + FR, :] * w_ref[1:2, :]
                      + win[2:2 + FR, :] * w_ref[0:1, :])
                du_ref[h, pl.ds(r0, FR), :] = du.astype(bf16)
            return carry

        lax.fori_loop(0, S // FR, back, 0)

    us = lambda h: pl.BlockSpec((None, S, CCW), lambda b, cb: (h, b, cb))
    ws = lambda h: pl.BlockSpec((3, CCW), lambda b, cb: (0, h * NFB + cb))
    bs = lambda h: pl.BlockSpec((1, CCW), lambda b, cb: (0, h * NFB + cb))
    return pl.pallas_call(
        body, name="ffn_bwd", grid=(nb, NFB),
        in_specs=[us(0), us(1), pl.BlockSpec((S, CCW), lambda b, cb: (b, cb)), ws(0), ws(1), bs(0), bs(1)],
        out_specs=[pl.BlockSpec((2, S, CCW), lambda b, cb: (0, b, cb)),
                   pl.BlockSpec((None, 2, 32, CCW), lambda b, cb: (b, 0, 0, cb))],
        out_shape=[jax.ShapeDtypeStruct((2, T, D_FF), bf16), jax.ShapeDtypeStruct((nb, 2, 32, D_FF), f32)],
        scratch_shapes=[pltpu.VMEM((S + 8, CCW), f32), pltpu.VMEM((S + 8, CCW), f32),
                        pltpu.VMEM((4, FR, CCW), f32)],
        compiler_params=_cparams(("parallel", "parallel")))(u3, u3, df, ffn_w, ffn_w, ffn_b, ffn_b)


AB = ATTN_BLOCK


def _attn_bias():
    slopes = 2.0 ** (-8.0 * jnp.arange(1, N_HEADS + 1, dtype=f32) / N_HEADS)
    steps = (jnp.arange(AB)[:, None] + AB) - jnp.arange(2 * AB)[None, :]
    own = (jnp.arange(2 * AB) >= AB)[None, :]
    out = []
    for window, dil in GROUPS:
        valid = (steps >= 0) & (steps <= window // dil)
        dist = slopes[:, None, None] * (steps * dil).astype(f32)[None]
        kinds = [jnp.where(v[None], dist, MASK_BIAS) for v in (valid, valid & own)]
        out.append(jnp.stack(kinds, axis=1))
    return jnp.stack(out)


def _head_masks():
    lane = lax.broadcasted_iota(jnp.int32, (1, 128), 1)
    return (lane < HEAD_DIM, lane >= HEAD_DIM)


def _perm_chunks(S, d):
    L = S // d
    ch = min(L, 256)
    out = []
    for r in range(d):
        for c in range(L // ch):
            start = r + d * ch * c
            out.append((pl.ds(start, ch, stride=d) if d > 1 else pl.ds(start, ch), r * L + c * ch, ch))
    return out


def _stack_heads(x, masks):
    return jnp.concatenate([jnp.where(masks[0], x, 0), jnp.where(masks[1], x, 0)], axis=0)


_NT = (((1,), (1,)), ((), ()))
_TN = (((0,), (0,)), ((), ()))
SCH = 32


def _attn_fwd(qn, kn, z8, bias, S):
    T = qn.shape[0]
    nb = T // S
    nblk = S // AB

    def body(q_ref, k_ref, v_ref, bias_ref, o_ref, ob_ref, lse_ref, qs, ks, vs, s2, p2, ogp, lgp, *group_scratch):
        og, lg = group_scratch[:3], group_scratch[3:]
        masks = _head_masks()
        ks[0:AB, :] = jnp.zeros((AB, 128), bf16)
        vs[0:AB, :] = jnp.zeros((AB, 128), bf16)

        for g, (_, d) in enumerate(GROUPS):
            nsub = S // (d * AB)
            chunks = _perm_chunks(S, d)
            for src, dst, ch in chunks:
                qs[dst:dst + ch, :] = q_ref[src, :].astype(bf16)
                ks[AB + dst:AB + dst + ch, :] = k_ref[src, :].astype(bf16)
                vs[AB + dst:AB + dst + ch, :] = v_ref[src, :].astype(bf16)
            od, ld = (og[g], lg[g]) if d == 1 else (ogp, lgp)

            def scores(j, carry):
                r0 = pl.multiple_of(j * AB, AB)
                q2 = _stack_heads(qs[pl.ds(r0, AB), :], masks)
                s2[j] = lax.dot_general(q2, ks[pl.ds(r0, 2 * AB), :], _NT, preferred_element_type=f32)
                return carry

            lax.fori_loop(0, nblk, scores, 0, unroll=8)

            def softmax(j, carry, g=g, nsub=nsub, ld=ld):
                r0 = pl.multiple_of(j * AB, AB)
                kind = (j % nsub == 0).astype(jnp.int32)
                for cc in range(AB // SCH):
                    lses = []
                    for hh in range(2):
                        rows = pl.ds(hh * AB + cc * SCH, SCH)
                        sb = s2[j, rows, :] - bias_ref[g, hh, kind, cc * SCH:(cc + 1) * SCH, :]
                        m = jnp.max(sb, axis=-1, keepdims=True)
                        p = jnp.exp(sb - m)
                        den = jnp.sum(p, axis=-1, keepdims=True)
                        p2[j, rows, :] = (p * (1.0 / den)).astype(bf16)
                        lses.append(m + jnp.log(den))
                    ld[pl.ds(r0 + cc * SCH, SCH), :] = jnp.where(masks[0], lses[0], lses[1])
                return carry

            lax.fori_loop(0, nblk, softmax, 0, unroll=2)

            def values(j, carry, od=od):
                r0 = pl.multiple_of(j * AB, AB)
                pv2 = jnp.dot(p2[j], vs[pl.ds(r0, 2 * AB), :], preferred_element_type=f32)
                od[pl.ds(r0, AB), :] = jnp.where(masks[0], pv2[:AB], pv2[AB:])
                return carry

            lax.fori_loop(0, nblk, values, 0, unroll=8)

            if d > 1:
                for src, dst, ch in chunks:
                    og[g][src, :] = ogp[dst:dst + ch, :]
                    lg[g][src, :] = lgp[dst:dst + ch, :]

        def combine(i, carry):
            rr = pl.ds(pl.multiple_of(i * 256, 256), 256)
            l0, l1, l2 = lg[0][rr, :], lg[1][rr, :], lg[2][rr, :]
            mx = jnp.maximum(jnp.maximum(l0, l1), l2)
            e0, e1, e2 = jnp.exp(l0 - mx), jnp.exp(l1 - mx), jnp.exp(l2 - mx)
            den = e0 + e1 + e2
            o = (e0 * og[0][rr, :] + e1 * og[1][rr, :] + e2 * og[2][rr, :]) / den
            o_ref[rr, :] = o
            ob_ref[rr, :] = o.astype(bf16)
            lse_ref[rr, :] = mx + jnp.log(den)
            return carry

        lax.fori_loop(0, S // 256, combine, 0)

    blk = pl.BlockSpec((S, 128), lambda b, hp: (b, hp))
    return pl.pallas_call(
        body, name="attn_fwd", grid=(nb, N_HEADS // 2),
        in_specs=[blk, blk, pl.BlockSpec((None, S, 128), lambda b, hp: (Z_V, b, hp)),
                  pl.BlockSpec((3, 2, 2, AB, 2 * AB), lambda b, hp: (0, hp, 0, 0, 0))],
        out_specs=[blk, blk, blk],
        out_shape=[jax.ShapeDtypeStruct((T, D), f32), jax.ShapeDtypeStruct((T, D), bf16),
                   jax.ShapeDtypeStruct((T, D), f32)],
        scratch_shapes=[pltpu.VMEM((S, 128), bf16), pltpu.VMEM((S + AB, 128), bf16), pltpu.VMEM((S + AB, 128), bf16),
                        pltpu.VMEM((nblk, 2 * AB, 2 * AB), f32), pltpu.VMEM((nblk, 2 * AB, 2 * AB), bf16),
                        pltpu.VMEM((S, 128), f32), pltpu.VMEM((S, 128), f32)] + [pltpu.VMEM((S, 128), f32)] * 6,
        compiler_params=_cparams(("parallel", "parallel")))(qn, kn, z8, bias)


def _attn_bwd(qn, kn, z8, do, o, lse, bias, bd, S):
    T = qn.shape[0]
    nb = T // S

    nblk = S // AB

    def body(q_ref, k_ref, v_ref, do_ref, o_ref, lse_ref, bias_ref, bd_ref, dq_ref, dk_ref, dv_ref,
             delta, qs, ks, vs, dos, lsp, dlp, s2, dp2, p2, ds2, dqp, dkp, dvp):
        masks = _head_masks()
        bdv = bd_ref[...]
        dq_ref[...] = jnp.zeros_like(dq_ref)
        dk_ref[...] = jnp.zeros_like(dk_ref)
        dv_ref[...] = jnp.zeros_like(dv_ref)
        ks[0:AB, :] = jnp.zeros((AB, 128), bf16)
        vs[0:AB, :] = jnp.zeros((AB, 128), bf16)

        def prep(i, carry):
            rr = pl.ds(pl.multiple_of(i * 256, 256), 256)
            delta[rr, :] = _head_sum(do_ref[rr, :] * o_ref[rr, :], bdv)
            return carry

        lax.fori_loop(0, S // 256, prep, 0)

        for g, (_, d) in enumerate(GROUPS):
            nsub = S // (d * AB)
            chunks = _perm_chunks(S, d)
            for src, dst, ch in chunks:
                qs[dst:dst + ch, :] = q_ref[src, :].astype(bf16)
                ks[AB + dst:AB + dst + ch, :] = k_ref[src, :].astype(bf16)
                vs[AB + dst:AB + dst + ch, :] = v_ref[src, :].astype(bf16)
                dos[dst:dst + ch, :] = do_ref[src, :].astype(bf16)
                lsp[dst:dst + ch, :] = lse_ref[src, :]
                dlp[dst:dst + ch, :] = delta[src, :]
            dkp[...] = jnp.zeros_like(dkp)
            dvp[...] = jnp.zeros_like(dvp)

            def scores(j, carry):
                r0 = pl.multiple_of(j * AB, AB)
                q2 = _stack_heads(qs[pl.ds(r0, AB), :], masks)
                do2 = _stack_heads(dos[pl.ds(r0, AB), :], masks)
                s2[j] = lax.dot_general(q2, ks[pl.ds(r0, 2 * AB), :], _NT, preferred_element_type=f32)
                dp2[j] = lax.dot_general(do2, vs[pl.ds(r0, 2 * AB), :], _NT, preferred_element_type=f32)
                return carry

            lax.fori_loop(0, nblk, scores, 0, unroll=8)

            def probs(j, carry, g=g, nsub=nsub):
                r0 = pl.multiple_of(j * AB, AB)
                kind = (j % nsub == 0).astype(jnp.int32)
                for cc in range(AB // SCH):
                    lse_c = lsp[pl.ds(r0 + cc * SCH, SCH), :]
                    del_c = dlp[pl.ds(r0 + cc * SCH, SCH), :]
                    for hh in range(2):
                        c0 = hh * HEAD_DIM
                        rows = pl.ds(hh * AB + cc * SCH, SCH)
                        sb = s2[j, rows, :] - bias_ref[g, hh, kind, cc * SCH:(cc + 1) * SCH, :]
                        p = jnp.exp(sb - lse_c[:, c0:c0 + 1])
                        p2[j, rows, :] = p.astype(bf16)
                        ds2[j, rows, :] = (p * (dp2[j, rows, :] - del_c[:, c0:c0 + 1])).astype(bf16)
                return carry

            lax.fori_loop(0, nblk, probs, 0, unroll=2)

            def grads(j, carry):
                r0 = pl.multiple_of(j * AB, AB)
                q2 = _stack_heads(qs[pl.ds(r0, AB), :], masks)
                do2 = _stack_heads(dos[pl.ds(r0, AB), :], masks)
                dsb = ds2[j]
                t = jnp.dot(dsb, ks[pl.ds(r0, 2 * AB), :], preferred_element_type=f32)
                dqp[pl.ds(r0, AB), :] = jnp.where(masks[0], t[:AB], t[AB:])
                dkp[pl.ds(r0, 2 * AB), :] += lax.dot_general(dsb, q2, _TN, preferred_element_type=f32)
                dvp[pl.ds(r0, 2 * AB), :] += lax.dot_general(p2[j], do2, _TN, preferred_element_type=f32)
                return carry

            lax.fori_loop(0, nblk, grads, 0, unroll=4)

            for src, dst, ch in chunks:
                dq_ref[src, :] += dqp[dst:dst + ch, :]
                dk_ref[src, :] += dkp[AB + dst:AB + dst + ch, :]
                dv_ref[src, :] += dvp[AB + dst:AB + dst + ch, :]

    blk = pl.BlockSpec((S, 128), lambda b, hp: (b, hp))
    row = lambda dt, pad=0: pltpu.VMEM((S + pad, 128), dt)
    blocks = lambda dt: pltpu.VMEM((nblk, 2 * AB, 2 * AB), dt)
    return pl.pallas_call(
        body, name="attn_bwd", grid=(nb, N_HEADS // 2),
        in_specs=[blk, blk, pl.BlockSpec((None, S, 128), lambda b, hp: (Z_V, b, hp)), blk, blk, blk,
                  pl.BlockSpec((3, 2, 2, AB, 2 * AB), lambda b, hp: (0, hp, 0, 0, 0)),
                  pl.BlockSpec((128, 128), lambda b, hp: (0, 0))],
        out_specs=[blk, blk, blk],
        out_shape=[jax.ShapeDtypeStruct((T, D), f32)] * 3,
        scratch_shapes=[row(f32), row(bf16), row(bf16, AB), row(bf16, AB), row(bf16), row(f32), row(f32),
                        blocks(f32), blocks(f32), blocks(bf16), blocks(bf16), row(f32), row(f32, AB), row(f32, AB)],
        compiler_params=_cparams(("parallel", "parallel")))(qn, kn, z8, do, o, lse, bias, bd)


def _any_spec():
    return pl.BlockSpec(memory_space=pl.ANY)


def _allgather_rows(shards, n_full):
    n = len(shards)

    def body(*refs):
        ins, outs = refs[:n], refs[n:2 * n]
        send_sems, recv_sems, local_sems = refs[2 * n:]
        x, y, c, me = _my_pos()
        sibling = (x, y, 1 - c)
        chips = [(1 - x, y), (x, 1 - y), (1 - x, 1 - y)]

        def idx(px, py, pc):
            return 4 * px + 2 * py + pc

        def copy(a, k, blk, to, src=None):
            return pltpu.make_async_remote_copy(
                src_ref=outs[a].at[blk] if src is None else src, dst_ref=outs[a].at[blk],
                send_sem=send_sems.at[a, k], recv_sem=recv_sems.at[a, k], device_id=to, device_id_type=MESH)

        mine = [pltpu.make_async_copy(ins[a], outs[a].at[me], local_sems.at[a]) for a in range(n)]
        for cp in mine:
            cp.start()
        first = []
        for a in range(n_full):
            first.append(copy(a, 0, me, sibling, src=ins[a]))
            first += [copy(a, 1 + j, me, (*chip, c), src=ins[a]) for j, chip in enumerate(chips)]
        for cp in first:
            cp.start()
        passed = []
        for a in range(n_full):
            for j, chip in enumerate(chips):
                blk = idx(*chip, c)
                copy(a, 1 + j, blk, (x, y, c)).wait_recv()
                cp = copy(a, 4 + j, blk, sibling)
                cp.start()
                passed.append(cp)
        for a in range(n_full):
            copy(a, 0, idx(x, y, 1 - c), (x, y, c)).wait_recv()
            for j, chip in enumerate(chips):
                copy(a, 4 + j, idx(*chip, 1 - c), (x, y, c)).wait_recv()
        for cp in first + passed:
            cp.wait_send()
        for cp in mine:
            cp.wait()

    return pl.pallas_call(
        body, name="allgather_weights",
        in_specs=[_any_spec()] * n, out_specs=[_any_spec()] * n,
        out_shape=[jax.ShapeDtypeStruct((N_DEV,) + s.shape, s.dtype) for s in shards],
        scratch_shapes=[pltpu.SemaphoreType.DMA((n_full, 7)), pltpu.SemaphoreType.DMA((n_full, 7)),
                        pltpu.SemaphoreType.DMA((n,))],
    )(*shards)


def _peer(x, y, c, k):
    tx = 1 - x if (k >> 2) & 1 else x
    ty = 1 - y if (k >> 1) & 1 else y
    tc = 1 - c if k & 1 else c
    return (tx, ty, tc), 4 * tx + 2 * ty + tc


_PEER_ORDER = (2, 4, 6, 3, 5, 7, 1)


_HBM = pl.BlockSpec(memory_space=pltpu.HBM)
_SEM = pl.BlockSpec(memory_space=pltpu.SEMAPHORE)
_EFFECT = pltpu.SideEffectType.DATAFLOW_SIDE_EFFECTING


def _exchange_copies(srcs, lands, send_sems, recv_sems, gather):
    x, y, c, me = _my_pos()
    copies = []
    for k in _PEER_ORDER:
        tgt, tidx = _peer(x, y, c, k)
        for a in range(len(srcs)):
            copies.append(pltpu.make_async_remote_copy(
                src_ref=srcs[a] if gather else srcs[a].at[tidx], dst_ref=lands[a].at[me],
                send_sem=send_sems.at[7 * a + k - 1], recv_sem=recv_sems.at[7 * a + k - 1],
                device_id=tgt, device_id_type=MESH))
    return copies


def _exchange_start(name, srcs, lands=None, after=None):
    n = len(srcs)
    gather = lands is not None
    if lands is None:
        lands = [lax.empty(g.shape, g.dtype) for g in srcs]
    extra = [] if after is None else [after]

    def body(*refs):
        src_refs, land_refs = refs[:n], refs[n:2 * n]
        send_sems, recv_sems = refs[2 * n + len(extra)], refs[2 * n + len(extra) + 1]
        token = refs[-1]
        for cp in _exchange_copies(src_refs, land_refs, send_sems, recv_sems, gather):
            cp.start()
        token[...] = jnp.zeros_like(token)

    hbm = lambda a: pltpu.with_memory_space_constraint(a, pltpu.HBM)
    outs = pl.pallas_call(
        body, name=name,
        out_shape=(pltpu.SemaphoreType.DMA((7 * n,)), pltpu.SemaphoreType.DMA((7 * n,)),
                   *[pltpu.HBM(g.shape, g.dtype) for g in list(srcs) + list(lands)],
                   jax.ShapeDtypeStruct((8, 128), f32)),
        in_specs=[_HBM] * (2 * n) + [pl.BlockSpec(memory_space=pl.ANY)] * len(extra),
        out_specs=(_SEM, _SEM, *([_HBM] * (2 * n)), pl.BlockSpec(memory_space=pltpu.VMEM)),
        input_output_aliases={i: 2 + i for i in range(2 * n)},
        compiler_params=pltpu.CompilerParams(has_side_effects=_EFFECT),
    )(*[hbm(g) for g in srcs], *[hbm(g) for g in lands], *extra)
    return outs[0], outs[1], list(outs[2:2 + n]), list(outs[2 + n:2 + 2 * n]), outs[-1], gather


def _exchange_wait(name, started, after):
    send_sems, recv_sems, srcs, lands, _, gather = started
    n = len(srcs)

    def body(*refs):
        src_refs, land_refs = refs[:n], refs[n:2 * n]
        s_sems, r_sems = refs[2 * n], refs[2 * n + 1]
        for cp in _exchange_copies(src_refs, land_refs, s_sems, r_sems, gather):
            cp.wait_send()
            cp.wait_recv()

    outs = pl.pallas_call(
        body, name=name,
        out_shape=tuple(pltpu.HBM(a.shape, a.dtype) for a in list(srcs) + list(lands)),
        in_specs=[_HBM] * (2 * n) + [_SEM, _SEM, pl.BlockSpec(memory_space=pl.ANY)],
        out_specs=tuple([_HBM] * (2 * n)),
        input_output_aliases={i: i for i in range(2 * n)},
        compiler_params=pltpu.CompilerParams(has_side_effects=_EFFECT),
    )(*srcs, *lands, send_sems, recv_sems, after)
    return list(outs[:n]), list(outs[n:])


SMALL_ROWS = 128


def _small_start(name, sg, after=None):
    return _exchange_start(name, [sg], [lax.empty((N_DEV,) + sg.shape, f32)], after=after)


def _small_sum(name, me, started, after):
    (own,), (slots,) = _exchange_wait(name + "_wait", started, after)

    def body(me_ref, s_ref, own_ref, out_ref):
        acc = None
        for p in range(N_DEV):
            term = lax.cond(me_ref[0] == p, lambda: own_ref[...], lambda p=p: s_ref[p])
            acc = term if acc is None else acc + term
        out_ref[...] = acc

    return pl.pallas_call(
        body, name=name + "_sum",
        in_specs=[pl.BlockSpec(memory_space=pltpu.SMEM), pl.BlockSpec(memory_space=pltpu.VMEM),
                  pl.BlockSpec(memory_space=pltpu.VMEM)],
        out_specs=pl.BlockSpec(memory_space=pltpu.VMEM),
        out_shape=jax.ShapeDtypeStruct(own.shape, f32))(me, slots, own)


def _adam_math(g, w, m, v):
    m = ADAM_B1 * m + (1.0 - ADAM_B1) * g
    v = ADAM_B2 * v + (1.0 - ADAM_B2) * (g * g)
    m_hat = m / (1.0 - ADAM_B1 ** ADAM_STEP)
    v_hat = v / (1.0 - ADAM_B2 ** ADAM_STEP)
    delta = -ADAM_LR * (m_hat / (jnp.sqrt(v_hat) + ADAM_EPS) + ADAM_WD * w)
    return delta, m, v


def _adam_slots(name, me, slots, own, w, m, v, tr):
    rows = w.shape[0]

    def body(me_ref, s_ref, own_ref, w_ref, m_ref, v_ref, g_ref, d_ref, nm_ref, nv_ref):
        mine = own_ref[...]
        g = None
        for p in range(N_DEV):
            term = lax.cond(me_ref[0] == p, lambda: mine, lambda p=p: s_ref[p]).astype(f32)
            g = term if g is None else g + term
        delta, nm, nv = _adam_math(g, w_ref[...], m_ref[...], v_ref[...])
        g_ref[...] = g
        d_ref[...] = delta
        nm_ref[...] = nm
        nv_ref[...] = nv

    rs = pl.BlockSpec((tr, D), lambda i, me_ref: (i, 0))
    return pl.pallas_call(
        body, name=name,
        grid_spec=pltpu.PrefetchScalarGridSpec(
            num_scalar_prefetch=1, grid=(rows // tr,),
            in_specs=[pl.BlockSpec((N_DEV, tr, D), lambda i, me_ref: (0, i, 0)),
                      pl.BlockSpec((None, tr, D), lambda i, me_ref: (me_ref[0], i, 0)), rs, rs, rs],
            out_specs=[rs] * 4),
        out_shape=[jax.ShapeDtypeStruct((rows, D), f32)] * 4,
        compiler_params=_cparams(("parallel",)))(me, slots, own, w, m, v)


def _adam_small(g, w, m, v):
    def body(g_ref, w_ref, m_ref, v_ref, d_ref, nm_ref, nv_ref):
        delta, nm, nv = _adam_math(g_ref[...], w_ref[...], m_ref[...], v_ref[...])
        d_ref[...] = delta
        nm_ref[...] = nm
        nv_ref[...] = nv

    return pl.pallas_call(body, name="adam_small", out_shape=[jax.ShapeDtypeStruct(g.shape, f32)] * 3)(g, w, m, v)


FFN_PAD = 6 * D


_SMALL_PARTS = (("norm1_g", 1), ("gate_b", 2), ("conv_w", CONV_WIDTH), ("conv_b", 1), ("conv_norm_g", 1),
                ("q_norm_g", 1), ("k_norm_g", 1), ("norm2_g", 1), ("ffn_conv_w", 18), ("ffn_conv_b", 6), ("last", 1))


def _small_offsets():
    out, row = {}, 0
    for name, rows in _SMALL_PARTS:
        out[name] = row
        row += -(-rows // 8) * 8
    assert row == SMALL_ROWS
    return out


def _pack_small(norm1_g, gate_b, conv_w, conv_b, conv_norm_g, q_norm_g, k_norm_g, norm2_g, ffn_conv_w, ffn_conv_b,
                last_row=None):
    pad_h = lambda a: jnp.pad(a, ((0, 0), (0, D - HEAD_DIM)))
    pad_f = lambda a: jnp.pad(a, ((0, 0), (0, FFN_PAD - 2 * D_FF))).reshape(-1, D)
    parts = [norm1_g, gate_b.reshape(2, D), conv_w, conv_b, conv_norm_g, pad_h(q_norm_g), pad_h(k_norm_g), norm2_g,
             pad_f(ffn_conv_w), pad_f(ffn_conv_b), jnp.zeros((1, D), f32) if last_row is None else last_row]
    return jnp.concatenate([jnp.pad(p, ((0, -p.shape[0] % 8), (0, 0))) for p in parts], axis=0)


def _unpack_small(p):
    o = _small_offsets()
    rows = lambda name, n: p[o[name]:o[name] + n]
    ffn = lambda a: a.reshape(-1, FFN_PAD)[:, :2 * D_FF]
    return dict(
        norm1_g=rows("norm1_g", 1), gate_b=rows("gate_b", 2).reshape(1, 2 * D), conv_w=rows("conv_w", CONV_WIDTH),
        conv_b=rows("conv_b", 1), conv_norm_g=rows("conv_norm_g", 1), q_norm_g=rows("q_norm_g", 1)[:, :HEAD_DIM],
        k_norm_g=rows("k_norm_g", 1)[:, :HEAD_DIM], norm2_g=rows("norm2_g", 1),
        ffn_conv_w=ffn(rows("ffn_conv_w", 18)), ffn_conv_b=ffn(rows("ffn_conv_b", 6)))


_ADAM_TILE = {896: 128, 704: 64, 128: 128, 352: 176}


def kernel(x, norm1_g, w_in, gate_b, conv_w, conv_b, conv_norm_g, w_conv_out, q_norm_g, k_norm_g, w_attn_out, w_out, norm2_g, w_up, ffn_conv_w, ffn_conv_b, w_down, loss_target, m_norm1_g, m_w_in, m_gate_b, m_conv_w, m_conv_b, m_conv_norm_g, m_w_conv_out, m_q_norm_g, m_k_norm_g, m_w_attn_out, m_w_out, m_norm2_g, m_w_up, m_ffn_conv_w, m_ffn_conv_b, m_w_down, v_norm1_g, v_w_in, v_gate_b, v_conv_w, v_conv_b, v_conv_norm_g, v_w_conv_out, v_q_norm_g, v_k_norm_g, v_w_attn_out, v_w_out, v_norm2_g, v_w_up, v_ffn_conv_w, v_ffn_conv_b, v_w_down):
    BL, S, _ = x.shape
    T = BL * S
    me = 4 * lax.axis_index("x") + 2 * lax.axis_index("y") + lax.axis_index("c")
    xt = x.reshape(T, D)
    target = loss_target.reshape(T, D)

    big = dict(w_in=(w_in[0].T, m_w_in[0].T, v_w_in[0].T), w_up=(w_up[0].T, m_w_up[0].T, v_w_up[0].T),
               w_conv_out=(w_conv_out[0], m_w_conv_out[0], v_w_conv_out[0]),
               w_attn_out=(w_attn_out[0], m_w_attn_out[0], v_w_attn_out[0]),
               w_out=(w_out[0], m_w_out[0], v_w_out[0]), w_down=(w_down[0], m_w_down[0], v_w_down[0]))
    order = ["w_in", "w_conv_out", "w_attn_out", "w_out", "w_up", "w_down"]
    shards = [big[n][0].astype(bf16) for n in order]
    gathered = _allgather_rows(shards, 1)
    ga_proj = _exchange_start("gather_start_proj", shards[1:4], gathered[1:4], after=gathered[0])
    ga_ffn = _exchange_start("gather_start_ffn", shards[4:6], gathered[4:6], after=ga_proj[4])
    W = {"w_in": gathered[0].reshape(-1, D)}

    def place_cols(shard, full_cols):
        z = jnp.zeros((shard.shape[0], full_cols), f32)
        return lax.dynamic_update_slice(z, shard, (0, me * shard.shape[1]))

    zr = lambda a: jnp.zeros_like(a)
    conv_local = _pack_small(
        zr(norm1_g), zr(gate_b), place_cols(conv_w[0], D), zr(conv_b), zr(conv_norm_g), zr(q_norm_g), zr(k_norm_g),
        zr(norm2_g), place_cols(ffn_conv_w[0], 2 * D_FF), zr(ffn_conv_b))
    ga_conv = _small_start("gather_conv_start", conv_local, after=ga_ffn[4])

    bd = (jnp.arange(128)[:, None] // HEAD_DIM == jnp.arange(128)[None, :] // HEAD_DIM).astype(bf16)
    bias = _attn_bias()
    qg = jnp.tile(q_norm_g, (1, N_HEADS))
    kg = jnp.tile(k_norm_g, (1, N_HEADS))

    h = _norm1_fwd(xt, norm1_g)
    z8 = _matmul_call(
        "mm_z", h, W["w_in"],
        pl.BlockSpec((1024, D), lambda i, j, k: (i, 0)),
        pl.BlockSpec((1024, D), lambda i, j, k: (_wsec_of_zsec(j), 0)),
        pl.BlockSpec((None, 1024, D), lambda i, j, k: (j, i, 0)),
        jax.ShapeDtypeStruct((8, T, D), f32), (T // 1024, 7, 1), "nt", 1, 1024, 1024, after=ga_conv[4])
    conv_all = _unpack_small(_small_sum("gather_conv", me.reshape(1), ga_conv, z8))
    conv_w_full, ffn_w_full = conv_all["conv_w"], conv_all["ffn_conv_w"]
    c = _conv_fwd(z8, conv_w_full, conv_b, S)
    s = _convnorm_fwd(c, conv_norm_g)
    qn, kn = _qk_fwd(z8, qg, kg, bd)
    for n, g in zip(order[1:4], _exchange_wait("gather_wait_proj", ga_proj, qn)[1]):
        W[n] = g.reshape(-1, D)
    ya = _matmul("mm_ya", s, W["w_conv_out"], "nn", f32)
    o, ob, lse = _attn_fwd(qn, kn, z8, bias, S)
    yb = _matmul("mm_yb", ob, W["w_attn_out"], "nn", f32)
    mixed = _gate_fwd(z8, gate_b, ya, yb)
    t1 = _matmul("mm_t1", mixed, W["w_out"], "nn", f32)
    for n, g in zip(order[4:6], _exchange_wait("gather_wait_ffn", ga_ffn, t1)[1]):
        W[n] = g.reshape(-1, D)
    x1, h2 = _norm2_fwd(xt, t1, norm2_g)
    TNU = D_FF // 2
    u3 = _matmul_call(
        "mm_u", h2, W["w_up"],
        pl.BlockSpec((1024, D), lambda i, j, k: (i, 0)),
        pl.BlockSpec((TNU, D), lambda i, j, k: (j, 0)),
        pl.BlockSpec((None, 1024, TNU), lambda i, j, k: (j // 2, i, j % 2)),
        jax.ShapeDtypeStruct((2, T, D_FF), f32), (T // 1024, 4, 1), "nt", 1, 1024, TNU)
    f = _ffn_fwd(u3, ffn_w_full, ffn_conv_b, S)
    t2 = _matmul("mm_t2", f, W["w_down"], "nn", f32, tk=TNU)
    dy, dyb, lacc = _loss_fwd(x1, t2, target)
    loss_local = 0.5 / D * jnp.sum(lacc)

    df = _matmul("mm_df", dyb, W["w_down"], "nt", f32, tn=TNU)
    g_w_down = _matmul("mm_dwdn", f, dyb, "tn", bf16, tm=TNU, tk=1024)
    du3, dffn = _ffn_bwd(u3, df, ffn_w_full, ffn_conv_b, S)
    g_w_up = _matmul_call(
        "mm_dwup", du3, h2,
        pl.BlockSpec((None, 1024, TNU), lambda i, j, k: (i // 2, k, i % 2)),
        pl.BlockSpec((1024, D), lambda i, j, k: (k, 0)),
        pl.BlockSpec((TNU, D), lambda i, j, k: (i, 0)),
        jax.ShapeDtypeStruct((2 * D_FF, D), bf16), (4, 1, T // 1024), "tn", T // 1024, TNU, D)
    blocks8 = lambda a: a.reshape(N_DEV, -1, D)
    ex_ffn = _exchange_start("scatter_start_ffn", [blocks8(g_w_up), blocks8(g_w_down)])
    dh2 = _matmul_call(
        "mm_dh2", du3, W["w_up"],
        pl.BlockSpec((None, 1024, TNU), lambda i, j, k: (k // 2, i, k % 2)),
        pl.BlockSpec((TNU, D), lambda i, j, k: (k, 0)),
        pl.BlockSpec((1024, D), lambda i, j, k: (i, 0)),
        jax.ShapeDtypeStruct((T, D), f32), (T // 1024, 1, 4), "nn", 4, 1024, D, after=ex_ffn[4])
    dx1, dx1b, dg_norm2 = _norm2_bwd(x1, dh2, dy, norm2_g)
    dmixed = _matmul("mm_dmixed", dx1b, W["w_out"], "nt", f32)
    g_w_out = _matmul("mm_dwo", mixed, dx1b, "tn", bf16, tk=1024)
    dz8 = lax.empty((8, T, D), bf16)
    dya, dyb2, dz8, dg_gate = _gate_bwd(dmixed, z8, gate_b, ya, yb, dz8)
    ds = _matmul("mm_ds", dya, W["w_conv_out"], "nt", f32)
    g_w_conv_out = _matmul("mm_dwco", s, dya, "tn", bf16, tk=1024)
    g_w_attn_out = _matmul("mm_dwao", ob, dyb2, "tn", bf16, tk=1024)
    ex_proj = _exchange_start("scatter_start_proj", [blocks8(g_w_conv_out), blocks8(g_w_attn_out), blocks8(g_w_out)])
    do = _matmul("mm_do", dyb2, W["w_attn_out"], "nt", f32, after=ex_proj[4])
    dc, dg_convnorm = _convnorm_bwd(c, ds, conv_norm_g)
    dz8a, dconv = _conv_bwd(dc, z8, conv_w_full, dz8, S)
    dqn, dkn, dv = _attn_bwd(qn, kn, z8, do, o, lse, bias, bd, S)
    dz8b, dg_q, dg_k = _qk_bwd(z8, dqn, dkn, dv, qg, kg, bd, dz8a)
    g_w_in = _matmul_call(
        "mm_dwin", dz8b, h,
        pl.BlockSpec((None, 1024, D), lambda i, j, k: (_zsec_of_wsec(i), k, 0)),
        pl.BlockSpec((1024, D), lambda i, j, k: (k, 0)),
        pl.BlockSpec((1024, D), lambda i, j, k: (i, 0)),
        jax.ShapeDtypeStruct((7 * D, D), bf16), (7, 1, T // 1024), "tn", T // 1024, D, D)
    ex_in = _exchange_start("scatter_start_in", [blocks8(g_w_in)])
    dh = _matmul_call(
        "mm_dh", dz8b, W["w_in"],
        pl.BlockSpec((None, 1024, D), lambda i, j, k: (k, i, 0)),
        pl.BlockSpec((1024, D), lambda i, j, k: (_wsec_of_zsec(k), 0)),
        pl.BlockSpec((1024, D), lambda i, j, k: (i, 0)),
        jax.ShapeDtypeStruct((T, D), f32), (T // 1024, 1, 7), "nn", 7, 1024, D, after=ex_in[4])
    grad_x, dg_norm1 = _norm1_bwd(xt, dh, dx1, norm1_g)

    sum8 = lambda a: a.reshape(-1, 8, a.shape[-1]).sum(axis=1)
    dconv_s = sum8(dconv.sum(axis=0))
    dffn_s = dffn.sum(axis=0).reshape(2, 4, 8, D_FF).sum(axis=2)
    dffn_w = jnp.concatenate([dffn_s[0, :3], dffn_s[1, :3]], axis=1)
    dffn_b = jnp.concatenate([dffn_s[0, 3:4], dffn_s[1, 3:4]], axis=1)
    fold = lambda a: sum8(a).reshape(N_HEADS, HEAD_DIM).sum(axis=0)[None]
    small_g_local = _pack_small(
        sum8(dg_norm1), sum8(dg_gate), dconv_s[:CONV_WIDTH], dconv_s[CONV_WIDTH:], sum8(dg_convnorm),
        fold(dg_q), fold(dg_k), sum8(dg_norm2), dffn_w, dffn_b,
        last_row=jnp.pad(loss_local.reshape(1, 1), ((0, 0), (0, D - 1))))
    sg_start = _small_start("small_grads_start", small_g_local)

    own, slots = {}, {}
    for tag, ex, names_ in (("ffn", ex_ffn, ("w_up", "w_down")),
                            ("proj", ex_proj, ("w_conv_out", "w_attn_out", "w_out")), ("in", ex_in, ("w_in",))):
        sent, landed = _exchange_wait("scatter_wait_" + tag, ex, sg_start[4])
        for n, src, land in zip(names_, sent, landed):
            own[n], slots[n] = src, land

    res = {}
    for n in order:
        w, m, v = big[n]
        outs = _adam_slots("adam_" + n, me.reshape(1), slots[n], own[n], w, m, v, _ADAM_TILE[w.shape[0]])
        last_big = outs[0]
        if n in ("w_in", "w_up"):
            outs = [a.T for a in outs]
        res[n] = [a[None] for a in outs]
    small_g = _small_sum("small_grads", me.reshape(1), sg_start, last_big)
    loss = small_g[_small_offsets()["last"], 0]

    col = lambda a, width: lax.dynamic_slice(a, (0, me * width), (a.shape[0], width))
    small_w_true = _pack_small(norm1_g, gate_b, conv_w_full, conv_b, conv_norm_g, q_norm_g, k_norm_g, norm2_g,
                               ffn_w_full, ffn_conv_b)
    place_m = lambda a, full: place_cols(a[0], full)
    small_m = _pack_small(m_norm1_g, m_gate_b, place_m(m_conv_w, D), m_conv_b, m_conv_norm_g, m_q_norm_g, m_k_norm_g,
                          m_norm2_g, place_m(m_ffn_conv_w, 2 * D_FF), m_ffn_conv_b)
    small_v = _pack_small(v_norm1_g, v_gate_b, place_m(v_conv_w, D), v_conv_b, v_conv_norm_g, v_q_norm_g, v_k_norm_g,
                          v_norm2_g, place_m(v_ffn_conv_w, 2 * D_FF), v_ffn_conv_b)
    sd, sm, sv = _adam_small(small_g, small_w_true, small_m, small_v)
    for i, packed in enumerate((small_g, sd, sm, sv)):
        u = _unpack_small(packed)
        u["conv_w"] = col(u["conv_w"], D // N_DEV)
        u["ffn_conv_w"] = col(u["ffn_conv_w"], 2 * D_FF // N_DEV)
        for n, a in u.items():
            res.setdefault(n, [None] * 4)[i] = a[None] if n in ("conv_w", "ffn_conv_w") else a

    names = ["norm1_g", "w_in", "gate_b", "conv_w", "conv_b", "conv_norm_g", "w_conv_out", "q_norm_g", "k_norm_g",
             "w_attn_out", "w_out", "norm2_g", "w_up", "ffn_conv_w", "ffn_conv_b", "w_down"]
    out = [loss, grad_x.reshape(BL, S, D)]
    for i in range(4):
        out += [res[n][i] for n in names]
    return tuple(out)
```

```python
import functools

import jax
import jax.numpy as jnp
import numpy as np
from jax import lax
from jax.experimental import pallas as pl
from jax.experimental.pallas import tpu as pltpu

f32 = jnp.float32
bf16 = jnp.bfloat16

D = 1024
N_HEADS = 16
HEAD_DIM = 64
CONV_WIDTH = 31
D_FF = 2816
GROUPS = ((128, 1), (512, 4), (2048, 16))
ATTN_BLOCK = 128
EPS = 1e-6
N_DEV = 8
MESH = pl.DeviceIdType.MESH

ADAM_LR = 0.001
ADAM_B1 = 0.9
ADAM_B2 = 0.999
ADAM_EPS = 1e-08
ADAM_WD = 0.01
ADAM_STEP = 10

VMEM_LIMIT = 56 * 1024 * 1024
MASK_BIAS = 1e30

Z_AVAL, Z_AGATE, Z_GA, Z_GB, Z_Q, Z_K, Z_V = 0, 1, 2, 3, 4, 5, 6


def _wsec_of_zsec(j):
    return jnp.where(j < 2, j, jnp.where(j < 4, j + 3, j - 2))


def _zsec_of_wsec(w):
    return jnp.where(w < 2, w, jnp.where(w < 5, w + 2, w - 3))


def _sig(x):
    return 1.0 / (1.0 + jnp.exp(-x))


def _colsum8(x):
    return x.reshape(-1, 8, x.shape[-1]).sum(axis=0)


def _cparams(sem):
    return pltpu.CompilerParams(dimension_semantics=sem, vmem_limit_bytes=VMEM_LIMIT)


def _my_pos():
    x, y, c = lax.axis_index("x"), lax.axis_index("y"), lax.axis_index("c")
    return x, y, c, 4 * x + 2 * y + c


_DIMS = {"nn": ((1,), (0,)), "nt": ((1,), (1,)), "tn": ((0,), (0,))}


def _matmul_call(name, a, b, a_spec, b_spec, o_spec, out_shape, grid, mode, nk, tm, tn, after=None):
    dims = (_DIMS[mode], ((), ()))
    extra = [] if after is None else [after]

    def body(a_ref, b_ref, *rest):
        o_ref, scratch = rest[len(extra)], rest[len(extra) + 1:]
        part = lax.dot_general(a_ref[...], b_ref[...], dims, preferred_element_type=f32)
        if nk == 1:
            o_ref[...] = part.astype(o_ref.dtype)
        else:
            acc = scratch[0]
            k = pl.program_id(2)

            @pl.when(k == 0)
            def _():
                acc[...] = part

            @pl.when(k > 0)
            def _():
                acc[...] += part

            @pl.when(k == nk - 1)
            def _():
                o_ref[...] = acc[...].astype(o_ref.dtype)

    scratch = [] if nk == 1 else [pltpu.VMEM((tm, tn), f32)]
    return pl.pallas_call(
        body, name=name, grid=grid, in_specs=[a_spec, b_spec] + [pl.BlockSpec(memory_space=pl.ANY)] * len(extra),
        out_specs=o_spec, out_shape=out_shape,
        scratch_shapes=scratch, compiler_params=_cparams(("parallel", "parallel", "arbitrary")),
    )(a, b, *extra)


def _matmul(name, a, b, mode, out_dtype, tm=1024, tn=1024, tk=None, after=None):
    if mode == "nn":
        (M, K), (_, N) = a.shape, b.shape
    elif mode == "nt":
        (M, K), (N, _) = a.shape, b.shape
    else:
        (K, M), (_, N) = a.shape, b.shape
    tm, tn = min(tm, M), min(tn, N)
    tk = K if tk is None else tk
    nk = K // tk
    assert M % tm == 0 and N % tn == 0 and K % tk == 0
    if mode == "tn":
        a_spec = pl.BlockSpec((tk, tm), lambda i, j, k: (k, i))
    else:
        a_spec = pl.BlockSpec((tm, tk), lambda i, j, k: (i, k))
    if mode == "nt":
        b_spec = pl.BlockSpec((tn, tk), lambda i, j, k: (j, k))
    else:
        b_spec = pl.BlockSpec((tk, tn), lambda i, j, k: (k, j))
    o_spec = pl.BlockSpec((tm, tn), lambda i, j, k: (i, j))
    return _matmul_call(name, a, b, a_spec, b_spec, o_spec, jax.ShapeDtypeStruct((M, N), out_dtype),
                        (M // tm, N // tn, nk), mode, nk, tm, tn, after=after)


TT = 512


def _rows(c, cb=0, tt=TT):
    return pl.BlockSpec((tt, c), lambda i: (i, cb))


def _sec(s, tt=TT):
    return pl.BlockSpec((None, tt, D), lambda i: (s, i, 0))


def _const(shape):
    return pl.BlockSpec(shape, lambda i: (0,) * len(shape))


def _acc_spec(c):
    return pl.BlockSpec((8, c), lambda i: (0, 0))


def _rms(x):
    return lax.rsqrt(jnp.mean(x * x, axis=-1, keepdims=True) + EPS)


def _rms_bwd(dy_g, xn, rstd):
    return rstd * (dy_g - xn * jnp.mean(dy_g * xn, axis=-1, keepdims=True))


def _head_sum(x, bd):
    parts = []
    for cb in range(x.shape[-1] // 128):
        xb = x[:, cb * 128:(cb + 1) * 128]
        hi = xb.astype(bf16)
        lo = (xb - hi.astype(f32)).astype(bf16)
        parts.append(jnp.dot(hi, bd, preferred_element_type=f32) + jnp.dot(lo, bd, preferred_element_type=f32))
    return parts[0] if len(parts) == 1 else jnp.concatenate(parts, axis=1)


def _norm1_fwd(x, g):
    T = x.shape[0]

    def body(x_ref, g_ref, h_ref):
        xv = x_ref[...]
        h_ref[...] = (xv * _rms(xv) * g_ref[...]).astype(bf16)

    return pl.pallas_call(
        body, name="norm1_fwd", grid=(T // TT,), in_specs=[_rows(D), _const((1, D))], out_specs=_rows(D),
        out_shape=jax.ShapeDtypeStruct((T, D), bf16), compiler_params=_cparams(("parallel",)))(x, g)


def _convnorm_fwd(c, g):
    T = c.shape[0]

    def body(c_ref, g_ref, s_ref):
        cv = c_ref[...]
        r = cv * _rms(cv) * g_ref[...]
        s_ref[...] = (r * _sig(r)).astype(bf16)

    return pl.pallas_call(
        body, name="convnorm_fwd", grid=(T // TT,), in_specs=[_rows(D), _const((1, D))], out_specs=_rows(D),
        out_shape=jax.ShapeDtypeStruct((T, D), bf16), compiler_params=_cparams(("parallel",)))(c, g)


def _qk_fwd(z8, qg, kg, bd):
    T = z8.shape[1]

    def body(q_ref, k_ref, qg_ref, kg_ref, bd_ref, qn_ref, kn_ref):
        bdv = bd_ref[...]
        q = q_ref[...]
        qn_ref[...] = q * lax.rsqrt(_head_sum(q * q, bdv) * (1.0 / HEAD_DIM) + EPS) * qg_ref[...] * (HEAD_DIM ** -0.5)
        k = k_ref[...]
        kn_ref[...] = k * lax.rsqrt(_head_sum(k * k, bdv) * (1.0 / HEAD_DIM) + EPS) * kg_ref[...]

    return pl.pallas_call(
        body, name="qk_fwd", grid=(T // TT,),
        in_specs=[_sec(Z_Q), _sec(Z_K), _const((1, D)), _const((1, D)), _const((128, 128))],
        out_specs=[_rows(D), _rows(D)],
        out_shape=[jax.ShapeDtypeStruct((T, D), f32)] * 2, compiler_params=_cparams(("parallel",)))(z8, z8, qg, kg, bd)


def _gate_fwd(z8, gate_b, ya, yb):
    T = ya.shape[0]

    def body(ga_ref, gb_ref, b_ref, ya_ref, yb_ref, mixed_ref):
        g_a = _sig(ga_ref[...] + b_ref[:, :D])
        g_b = _sig(gb_ref[...] + b_ref[:, D:])
        mixed_ref[...] = (g_a * ya_ref[...] + g_b * yb_ref[...]).astype(bf16)

    return pl.pallas_call(
        body, name="gate_fwd", grid=(T // TT,),
        in_specs=[_sec(Z_GA), _sec(Z_GB), _const((1, 2 * D)), _rows(D), _rows(D)], out_specs=_rows(D),
        out_shape=jax.ShapeDtypeStruct((T, D), bf16), compiler_params=_cparams(("parallel",)))(z8, z8, gate_b, ya, yb)


def _norm2_fwd(x, t1, g):
    T = x.shape[0]

    def body(x_ref, t_ref, g_ref, x1_ref, h2_ref):
        x1 = x_ref[...] + t_ref[...]
        x1_ref[...] = x1
        h2_ref[...] = (x1 * _rms(x1) * g_ref[...]).astype(bf16)

    return pl.pallas_call(
        body, name="norm2_fwd", grid=(T // TT,), in_specs=[_rows(D), _rows(D), _const((1, D))],
        out_specs=[_rows(D), _rows(D)],
        out_shape=[jax.ShapeDtypeStruct((T, D), f32), jax.ShapeDtypeStruct((T, D), bf16)],
        compiler_params=_cparams(("parallel",)))(x, t1, g)


def _loss_fwd(x1, t2, target):
    T = x1.shape[0]

    def body(x1_ref, t_ref, tg_ref, dy_ref, dyb_ref, acc_ref):
        diff = x1_ref[...] + t_ref[...] - tg_ref[...]
        dy = diff * (1.0 / D)
        dy_ref[...] = dy
        dyb_ref[...] = dy.astype(bf16)

        @pl.when(pl.program_id(0) == 0)
        def _():
            acc_ref[...] = jnp.zeros_like(acc_ref)

        acc_ref[...] += _colsum8(diff * diff)

    return pl.pallas_call(
        body, name="loss_fwd", grid=(T // TT,), in_specs=[_rows(D)] * 3,
        out_specs=[_rows(D), _rows(D), _acc_spec(D)],
        out_shape=[jax.ShapeDtypeStruct((T, D), f32), jax.ShapeDtypeStruct((T, D), bf16),
                   jax.ShapeDtypeStruct((8, D), f32)],
        compiler_params=_cparams(("arbitrary",)))(x1, t2, target)


def _norm2_bwd(x1, dh2, dy, g):
    T = x1.shape[0]

    def body(x1_ref, dh_ref, dy_ref, g_ref, dx1_ref, dx1b_ref, dg_ref):
        x1 = x1_ref[...]
        rstd = _rms(x1)
        xn = x1 * rstd
        dh = dh_ref[...]
        dx1 = dy_ref[...] + _rms_bwd(dh * g_ref[...], xn, rstd)
        dx1_ref[...] = dx1
        dx1b_ref[...] = dx1.astype(bf16)

        @pl.when(pl.program_id(0) == 0)
        def _():
            dg_ref[...] = jnp.zeros_like(dg_ref)

        dg_ref[...] += _colsum8(dh * xn)

    return pl.pallas_call(
        body, name="norm2_bwd", grid=(T // TT,), in_specs=[_rows(D), _rows(D), _rows(D), _const((1, D))],
        out_specs=[_rows(D), _rows(D), _acc_spec(D)],
        out_shape=[jax.ShapeDtypeStruct((T, D), f32), jax.ShapeDtypeStruct((T, D), bf16),
                   jax.ShapeDtypeStruct((8, D), f32)],
        compiler_params=_cparams(("arbitrary",)))(x1, dh2, dy, g)


def _gate_bwd(dmixed, z8, gate_b, ya, yb, dz8):
    T = ya.shape[0]

    def body(dm_ref, ga_ref, gb_ref, b_ref, ya_ref, yb_ref, dz_in, dya_ref, dyb_ref, dz_ref, dgb_ref):
        del dz_in
        dm = dm_ref[...]
        g_a = _sig(ga_ref[...] + b_ref[:, :D])
        g_b = _sig(gb_ref[...] + b_ref[:, D:])
        dya_ref[...] = (dm * g_a).astype(bf16)
        dyb_ref[...] = (dm * g_b).astype(bf16)
        dla = dm * ya_ref[...] * g_a * (1.0 - g_a)
        dlb = dm * yb_ref[...] * g_b * (1.0 - g_b)
        dz_ref[0] = dla.astype(bf16)
        dz_ref[1] = dlb.astype(bf16)

        @pl.when(pl.program_id(0) == 0)
        def _():
            dgb_ref[...] = jnp.zeros_like(dgb_ref)

        dgb_ref[:, :D] += _colsum8(dla)
        dgb_ref[:, D:] += _colsum8(dlb)

    return pl.pallas_call(
        body, name="gate_bwd", grid=(T // TT,),
        in_specs=[_rows(D), _sec(Z_GA), _sec(Z_GB), _const((1, 2 * D)), _rows(D), _rows(D),
                  pl.BlockSpec(memory_space=pl.ANY)],
        out_specs=[_rows(D), _rows(D), pl.BlockSpec((2, TT, D), lambda i: (1, i, 0)), _acc_spec(2 * D)],
        out_shape=[jax.ShapeDtypeStruct((T, D), bf16), jax.ShapeDtypeStruct((T, D), bf16),
                   jax.ShapeDtypeStruct(dz8.shape, bf16), jax.ShapeDtypeStruct((8, 2 * D), f32)],
        input_output_aliases={6: 2},
        compiler_params=_cparams(("arbitrary",)))(dmixed, z8, z8, gate_b, ya, yb, dz8)


def _convnorm_bwd(c, ds, g):
    T = c.shape[0]

    def body(c_ref, ds_ref, g_ref, dc_ref, dg_ref):
        cv = c_ref[...]
        rstd = _rms(cv)
        r0 = cv * rstd
        gv = g_ref[...]
        r = r0 * gv
        sg = _sig(r)
        dr = ds_ref[...] * sg * (1.0 + r * (1.0 - sg))
        dc_ref[...] = _rms_bwd(dr * gv, r0, rstd)

        @pl.when(pl.program_id(0) == 0)
        def _():
            dg_ref[...] = jnp.zeros_like(dg_ref)

        dg_ref[...] += _colsum8(dr * r0)

    return pl.pallas_call(
        body, name="convnorm_bwd", grid=(T // TT,), in_specs=[_rows(D), _rows(D), _const((1, D))],
        out_specs=[_rows(D), _acc_spec(D)],
        out_shape=[jax.ShapeDtypeStruct((T, D), f32), jax.ShapeDtypeStruct((8, D), f32)],
        compiler_params=_cparams(("arbitrary",)))(c, ds, g)


def _qk_bwd(z8, dqn, dkn, dv, qg, kg, bd, dz8):
    T = dqn.shape[0]

    def body(q_ref, k_ref, dqn_ref, dkn_ref, dv_ref, qg_ref, kg_ref, bd_ref, dz_in, dz_ref, dqg_ref, dkg_ref):
        del dz_in
        bdv = bd_ref[...]

        @pl.when(pl.program_id(0) == 0)
        def _():
            dqg_ref[...] = jnp.zeros_like(dqg_ref)
            dkg_ref[...] = jnp.zeros_like(dkg_ref)

        def one(raw, dn_scaled, g, dg_ref, sec):
            rstd = lax.rsqrt(_head_sum(raw * raw, bdv) * (1.0 / HEAD_DIM) + EPS)
            n = raw * rstd
            dg_ref[...] += _colsum8(dn_scaled * n)
            dn = dn_scaled * g
            draw = rstd * (dn - n * (_head_sum(dn * n, bdv) * (1.0 / HEAD_DIM)))
            dz_ref[sec] = draw.astype(bf16)

        one(q_ref[...], dqn_ref[...] * (HEAD_DIM ** -0.5), qg_ref[...], dqg_ref, 0)
        one(k_ref[...], dkn_ref[...], kg_ref[...], dkg_ref, 1)
        dz_ref[2] = dv_ref[...].astype(bf16)
        dz_ref[3] = jnp.zeros((TT, D), bf16)

    return pl.pallas_call(
        body, name="qk_bwd", grid=(T // TT,),
        in_specs=[_sec(Z_Q), _sec(Z_K), _rows(D), _rows(D), _rows(D), _const((1, D)), _const((1, D)),
                  _const((128, 128)), pl.BlockSpec(memory_space=pl.ANY)],
        out_specs=[pl.BlockSpec((4, TT, D), lambda i: (1, i, 0)), _acc_spec(D), _acc_spec(D)],
        out_shape=[jax.ShapeDtypeStruct(dz8.shape, bf16), jax.ShapeDtypeStruct((8, D), f32),
                   jax.ShapeDtypeStruct((8, D), f32)],
        input_output_aliases={8: 0},
        compiler_params=_cparams(("arbitrary",)))(z8, z8, dqn, dkn, dv, qg, kg, bd, dz8)


def _norm1_bwd(x, dh, dx1, g):
    T = x.shape[0]

    def body(x_ref, dh_ref, dx1_ref, g_ref, gx_ref, dg_ref):
        xv = x_ref[...]
        rstd = _rms(xv)
        xn = xv * rstd
        dh = dh_ref[...]
        gx_ref[...] = dx1_ref[...] + _rms_bwd(dh * g_ref[...], xn, rstd)

        @pl.when(pl.program_id(0) == 0)
        def _():
            dg_ref[...] = jnp.zeros_like(dg_ref)

        dg_ref[...] += _colsum8(dh * xn)

    return pl.pallas_call(
        body, name="norm1_bwd", grid=(T // TT,), in_specs=[_rows(D), _rows(D), _rows(D), _const((1, D))],
        out_specs=[_rows(D), _acc_spec(D)],
        out_shape=[jax.ShapeDtypeStruct((T, D), f32), jax.ShapeDtypeStruct((8, D), f32)],
        compiler_params=_cparams(("arbitrary",)))(x, dh, dx1, g)


CCW = 256
CR = 64
HALO = 32


def _conv_fwd(z8, conv_w, conv_b, S):
    T = z8.shape[1]
    nb = T // S
    ncb = D // CCW

    def body(av_ref, ag_ref, w_ref, b_ref, c_ref, pad):
        pad[0:HALO, :] = jnp.zeros((HALO, CCW), f32)

        def fill(i, carry):
            r0 = pl.multiple_of(i * 256, 256)
            pad[pl.ds(HALO + r0, 256), :] = av_ref[pl.ds(r0, 256), :] * _sig(ag_ref[pl.ds(r0, 256), :])
            return carry

        lax.fori_loop(0, S // 256, fill, 0)
        bias = b_ref[...]

        def chunk(i, carry):
            r0 = pl.multiple_of(i * CR, CR)
            win = pad[pl.ds(r0, CR + HALO), :]
            acc = jnp.zeros((CR, CCW), f32) + bias
            for s in range(8):
                part = None
                for m in range((CONV_WIDTH - 1 - s) // 8 + 1):
                    j = CONV_WIDTH - 1 - 8 * m - s
                    term = win[24 - 8 * m:24 - 8 * m + CR + 8, :] * w_ref[j:j + 1, :]
                    part = term if part is None else part + term
                acc = acc + part[8 - s:8 - s + CR, :]
            c_ref[pl.ds(r0, CR), :] = acc
            return carry

        lax.fori_loop(0, S // CR, chunk, 0)

    zs = lambda s: pl.BlockSpec((None, S, CCW), lambda b, cb: (s, b, cb))
    return pl.pallas_call(
        body, name="conv_fwd", grid=(nb, ncb),
        in_specs=[zs(Z_AVAL), zs(Z_AGATE), pl.BlockSpec((CONV_WIDTH, CCW), lambda b, cb: (0, cb)),
                  pl.BlockSpec((1, CCW), lambda b, cb: (0, cb))],
        out_specs=pl.BlockSpec((S, CCW), lambda b, cb: (b, cb)),
        out_shape=jax.ShapeDtypeStruct((T, D), f32),
        scratch_shapes=[pltpu.VMEM((S + HALO, CCW), f32)],
        compiler_params=_cparams(("parallel", "parallel")))(z8, z8, conv_w, conv_b)


def _conv_bwd(dc, z8, conv_w, dz8, S):
    T = dc.shape[0]
    nb = T // S
    ncb = D // CCW

    def body(dc_ref, av_ref, ag_ref, w_ref, dz_in, dz_ref, dw_ref, apad, dpad, shbuf):
        del dz_in
        apad[0:HALO, :] = jnp.zeros((HALO, CCW), f32)
        dpad[S:S + HALO, :] = jnp.zeros((HALO, CCW), f32)
        dw_ref[...] = jnp.zeros_like(dw_ref)

        def fill(i, carry):
            r0 = pl.multiple_of(i * 256, 256)
            apad[pl.ds(HALO + r0, 256), :] = av_ref[pl.ds(r0, 256), :] * _sig(ag_ref[pl.ds(r0, 256), :])
            dpad[pl.ds(r0, 256), :] = dc_ref[pl.ds(r0, 256), :]
            return carry

        lax.fori_loop(0, S // 256, fill, 0)

        def chunk(i, carry):
            r0 = pl.multiple_of(i * CR, CR)
            dwin = dpad[pl.ds(r0, CR + HALO), :]
            da = jnp.zeros((CR, CCW), f32)
            for s in range(8):
                shbuf[...] = dwin[s:s + CR, :]
                dshift = shbuf[...]
                part = None
                for m in range((CONV_WIDTH - 1 - s) // 8 + 1):
                    j = CONV_WIDTH - 1 - 8 * m - s
                    term = dwin[8 * m:8 * m + CR + 8, :] * w_ref[j:j + 1, :]
                    part = term if part is None else part + term
                    a_lag = apad[pl.ds(r0 + HALO - 8 * m, CR), :]
                    dw_ref[8 * j:8 * j + 8, :] += _colsum8(dshift * a_lag)
                da = da + part[s:s + CR, :]
            dw_ref[8 * CONV_WIDTH:8 * CONV_WIDTH + 8, :] += _colsum8(dwin[0:CR, :])
            av = av_ref[pl.ds(r0, CR), :]
            sg = _sig(ag_ref[pl.ds(r0, CR), :])
            dz_ref[0, pl.ds(r0, CR), :] = (da * sg).astype(bf16)
            dz_ref[1, pl.ds(r0, CR), :] = (da * av * sg * (1.0 - sg)).astype(bf16)
            return carry

        lax.fori_loop(0, S // CR, chunk, 0)

    zs = lambda s: pl.BlockSpec((None, S, CCW), lambda b, cb: (s, b, cb))
    return pl.pallas_call(
        body, name="conv_bwd", grid=(nb, ncb),
        in_specs=[pl.BlockSpec((S, CCW), lambda b, cb: (b, cb)), zs(Z_AVAL), zs(Z_AGATE),
                  pl.BlockSpec((CONV_WIDTH, CCW), lambda b, cb: (0, cb)), pl.BlockSpec(memory_space=pl.ANY)],
        out_specs=[pl.BlockSpec((2, S, CCW), lambda b, cb: (0, b, cb)),
                   pl.BlockSpec((None, 256, CCW), lambda b, cb: (b, 0, cb))],
        out_shape=[jax.ShapeDtypeStruct(dz8.shape, bf16), jax.ShapeDtypeStruct((nb, 256, D), f32)],
        input_output_aliases={4: 0},
        scratch_shapes=[pltpu.VMEM((S + HALO, CCW), f32), pltpu.VMEM((S + HALO, CCW), f32),
                        pltpu.VMEM((CR, CCW), f32)],
        compiler_params=_cparams(("parallel", "parallel")))(dc, z8, z8, conv_w, dz8)


FR = 128
NFB = D_FF // CCW


def _ffn_window(ref, i, r0):
    return ref[pl.ds(r0 - 8, FR + 8), :]


def _ffn_u(win, w_ref, b_ref):
    return (win[6:6 + FR, :] * w_ref[0:1, :] + win[7:7 + FR, :] * w_ref[1:2, :]
            + win[8:8 + FR, :] * w_ref[2:3, :] + b_ref[...])


def _ffn_fwd(u3, ffn_w, ffn_b, S):
    T = u3.shape[1]
    nb = T // S

    def body(uv_ref, ug_ref, wv_ref, wg_ref, bv_ref, bg_ref, f_ref):
        def chunk(first, i):
            r0 = 0 if first else pl.multiple_of(i * FR, FR)
            if first:
                z = jnp.zeros((8, CCW), f32)
                wv = jnp.concatenate([z, uv_ref[0:FR, :]], axis=0)
                wg = jnp.concatenate([z, ug_ref[0:FR, :]], axis=0)
            else:
                wv = _ffn_window(uv_ref, i, r0)
                wg = _ffn_window(ug_ref, i, r0)
            u_val = _ffn_u(wv, wv_ref, bv_ref)
            u_gate = _ffn_u(wg, wg_ref, bg_ref)
            f_ref[pl.ds(r0, FR), :] = (u_gate * _sig(u_gate) * u_val).astype(bf16)

        chunk(True, 0)

        def loop(i, carry):
            chunk(False, i)
            return carry

        lax.fori_loop(1, S // FR, loop, 0)

    us = lambda h: pl.BlockSpec((None, S, CCW), lambda b, cb: (h, b, cb))
    ws = lambda h: pl.BlockSpec((3, CCW), lambda b, cb: (0, h * NFB + cb))
    bs = lambda h: pl.BlockSpec((1, CCW), lambda b, cb: (0, h * NFB + cb))
    return pl.pallas_call(
        body, name="ffn_fwd", grid=(nb, NFB),
        in_specs=[us(0), us(1), ws(0), ws(1), bs(0), bs(1)],
        out_specs=pl.BlockSpec((S, CCW), lambda b, cb: (b, cb)),
        out_shape=jax.ShapeDtypeStruct((T, D_FF), bf16),
        compiler_params=_cparams(("parallel", "parallel")))(u3, u3, ffn_w, ffn_w, ffn_b, ffn_b)


def _ffn_bwd(u3, df, ffn_w, ffn_b, S):
    T = u3.shape[1]
    nb = T // S

    def body(uv_ref, ug_ref, df_ref, wv_ref, wg_ref, bv_ref, bg_ref, du_ref, dw_ref, dvpad, dgpad, shbuf):
        dvpad[S:S + 8, :] = jnp.zeros((8, CCW), f32)
        dgpad[S:S + 8, :] = jnp.zeros((8, CCW), f32)
        dw_ref[...] = jnp.zeros_like(dw_ref)

        def chunk(first, i):
            r0 = 0 if first else pl.multiple_of(i * FR, FR)
            if first:
                z = jnp.zeros((8, CCW), f32)
                wv = jnp.concatenate([z, uv_ref[0:FR, :]], axis=0)
                wg = jnp.concatenate([z, ug_ref[0:FR, :]], axis=0)
            else:
                wv = _ffn_window(uv_ref, i, r0)
                wg = _ffn_window(ug_ref, i, r0)
            taps = []
            for h, win in enumerate((wv, wg)):
                shbuf[2 * h] = win[6:6 + FR, :]
                shbuf[2 * h + 1] = win[7:7 + FR, :]
                taps.append((shbuf[2 * h], shbuf[2 * h + 1], win[8:8 + FR, :]))
            conv = lambda x, w_ref, b_ref: (x[0] * w_ref[0:1, :] + x[1] * w_ref[1:2, :] + x[2] * w_ref[2:3, :]
                                            + b_ref[...])
            u_val = conv(taps[0], wv_ref, bv_ref)
            u_gate = conv(taps[1], wg_ref, bg_ref)
            dfc = df_ref[pl.ds(r0, FR), :]
            sg = _sig(u_gate)
            d_val = dfc * u_gate * sg
            d_gate = dfc * u_val * sg * (1.0 + u_gate * (1.0 - sg))
            dvpad[pl.ds(r0, FR), :] = d_val
            dgpad[pl.ds(r0, FR), :] = d_gate
            for h, dd in enumerate((d_val, d_gate)):
                for j in range(3):
                    dw_ref[h, 8 * j:8 * j + 8, :] += _colsum8(dd * taps[h][j])
                dw_ref[h, 24:32, :] += _colsum8(dd)

        chunk(True, 0)

        def loop(i, carry):
            chunk(False, i)
            return carry

        lax.fori_loop(1, S // FR, loop, 0)

        def back(i, carry):
            r0 = pl.multiple_of(i * FR, FR)
            for h, (dpad, w_ref) in enumerate(((dvpad, wv_ref), (dgpad, wg_ref))):
                win = dpad[pl.ds(r0, FR + 8), :]
                du = (win[0:FR, :] * w_ref[2:3, :] + win[1:1 + FR, :] * w_ref[1:2, :]
                      + win[2:2 + FR, :] * w_ref[0:1, :])
                du_ref[h, pl.ds(r0, FR), :] = du.astype(bf16)
            return carry

        lax.fori_loop(0, S // FR, back, 0)

    us = lambda h: pl.BlockSpec((None, S, CCW), lambda b, cb: (h, b, cb))
    ws = lambda h: pl.BlockSpec((3, CCW), lambda b, cb: (0, h * NFB + cb))
    bs = lambda h: pl.BlockSpec((1, CCW), lambda b, cb: (0, h * NFB + cb))
    return pl.pallas_call(
        body, name="ffn_bwd", grid=(nb, NFB),
        in_specs=[us(0), us(1), pl.BlockSpec((S, CCW), lambda b, cb: (b, cb)), ws(0), ws(1), bs(0), bs(1)],
        out_specs=[pl.BlockSpec((2, S, CCW), lambda b, cb: (0, b, cb)),
                   pl.BlockSpec((None, 2, 32, CCW), lambda b, cb: (b, 0, 0, cb))],
        out_shape=[jax.ShapeDtypeStruct((2, T, D_FF), bf16), jax.ShapeDtypeStruct((nb, 2, 32, D_FF), f32)],
        scratch_shapes=[pltpu.VMEM((S + 8, CCW), f32), pltpu.VMEM((S + 8, CCW), f32),
                        pltpu.VMEM((4, FR, CCW), f32)],
        compiler_params=_cparams(("parallel", "parallel")))(u3, u3, df, ffn_w, ffn_w, ffn_b, ffn_b)


AB = ATTN_BLOCK


def _attn_bias():
    slopes = (np.float32(2.0) ** (np.float32(-8.0) * np.arange(1, N_HEADS + 1, dtype=np.float32)
                                  / np.float32(N_HEADS))).astype(np.float32)
    steps = (np.arange(AB)[:, None] + AB) - np.arange(2 * AB)[None, :]
    own = (np.arange(2 * AB) >= AB)[None, :]
    out = []
    for window, dil in GROUPS:
        valid = (steps >= 0) & (steps <= window // dil)
        dist = slopes[:, None, None] * (steps * dil).astype(np.float32)[None]
        kinds = [np.where(v[None], dist, np.float32(MASK_BIAS)) for v in (valid, valid & own)]
        out.append(np.stack(kinds, axis=1))
    return jnp.asarray(np.stack(out).astype(np.float32))


def _head_masks():
    lane = lax.broadcasted_iota(jnp.int32, (1, 128), 1)
    return (lane < HEAD_DIM, lane >= HEAD_DIM)


def _perm_chunks(S, d):
    L = S // d
    ch = min(L, 256)
    out = []
    for r in range(d):
        for c in range(L // ch):
            start = r + d * ch * c
            out.append((pl.ds(start, ch, stride=d) if d > 1 else pl.ds(start, ch), r * L + c * ch, ch))
    return out


def _stack_heads(x, masks):
    return jnp.concatenate([jnp.where(masks[0], x, 0), jnp.where(masks[1], x, 0)], axis=0)


_NT = (((1,), (1,)), ((), ()))
_TN = (((0,), (0,)), ((), ()))
SCH = 32


def _attn_fwd(qn, kn, z8, bias, S):
    T = qn.shape[0]
    nb = T // S
    nblk = S // AB

    def body(q_ref, k_ref, v_ref, bias_ref, o_ref, ob_ref, lse_ref, qs, ks, vs, s2, p2, ogp, lgp, *group_scratch):
        og, lg = group_scratch[:3], group_scratch[3:]
        masks = _head_masks()
        ks[0:AB, :] = jnp.zeros((AB, 128), bf16)
        vs[0:AB, :] = jnp.zeros((AB, 128), bf16)

        for g, (_, d) in enumerate(GROUPS):
            nsub = S // (d * AB)
            chunks = _perm_chunks(S, d)
            for src, dst, ch in chunks:
                qs[dst:dst + ch, :] = q_ref[src, :].astype(bf16)
                ks[AB + dst:AB + dst + ch, :] = k_ref[src, :].astype(bf16)
                vs[AB + dst:AB + dst + ch, :] = v_ref[src, :].astype(bf16)
            od, ld = (og[g], lg[g]) if d == 1 else (ogp, lgp)

            def scores(j, carry):
                r0 = pl.multiple_of(j * AB, AB)
                q2 = _stack_heads(qs[pl.ds(r0, AB), :], masks)
                s2[j] = lax.dot_general(q2, ks[pl.ds(r0, 2 * AB), :], _NT, preferred_element_type=f32)
                return carry

            lax.fori_loop(0, nblk, scores, 0, unroll=8)

            def softmax(j, carry, g=g, nsub=nsub, ld=ld):
                r0 = pl.multiple_of(j * AB, AB)
                kind = (j % nsub == 0).astype(jnp.int32)
                for cc in range(AB // SCH):
                    lses = []
                    for hh in range(2):
                        rows = pl.ds(hh * AB + cc * SCH, SCH)
                        sb = s2[j, rows, :] - bias_ref[g, hh, kind, cc * SCH:(cc + 1) * SCH, :]
                        m = jnp.max(sb, axis=-1, keepdims=True)
                        p = jnp.exp(sb - m)
                        den = jnp.sum(p, axis=-1, keepdims=True)
                        p2[j, rows, :] = (p * (1.0 / den)).astype(bf16)
                        lses.append(m + jnp.log(den))
                    ld[pl.ds(r0 + cc * SCH, SCH), :] = jnp.where(masks[0], lses[0], lses[1])
                return carry

            lax.fori_loop(0, nblk, softmax, 0, unroll=2)

            def values(j, carry, od=od):
                r0 = pl.multiple_of(j * AB, AB)
                pv2 = jnp.dot(p2[j], vs[pl.ds(r0, 2 * AB), :], preferred_element_type=f32)
                od[pl.ds(r0, AB), :] = jnp.where(masks[0], pv2[:AB], pv2[AB:])
                return carry

            lax.fori_loop(0, nblk, values, 0, unroll=8)

            if d > 1:
                for src, dst, ch in chunks:
                    og[g][src, :] = ogp[dst:dst + ch, :]
                    lg[g][src, :] = lgp[dst:dst + ch, :]

        def combine(i, carry):
            rr = pl.ds(pl.multiple_of(i * 256, 256), 256)
            l0, l1, l2 = lg[0][rr, :], lg[1][rr, :], lg[2][rr, :]
            mx = jnp.maximum(jnp.maximum(l0, l1), l2)
            e0, e1, e2 = jnp.exp(l0 - mx), jnp.exp(l1 - mx), jnp.exp(l2 - mx)
            den = e0 + e1 + e2
            o = (e0 * og[0][rr, :] + e1 * og[1][rr, :] + e2 * og[2][rr, :]) / den
            o_ref[rr, :] = o
            ob_ref[rr, :] = o.astype(bf16)
            lse_ref[rr, :] = mx + jnp.log(den)
            return carry

        lax.fori_loop(0, S // 256, combine, 0)

    blk = pl.BlockSpec((S, 128), lambda b, hp: (b, hp))
    return pl.pallas_call(
        body, name="attn_fwd", grid=(nb, N_HEADS // 2),
        in_specs=[blk, blk, pl.BlockSpec((None, S, 128), lambda b, hp: (Z_V, b, hp)),
                  pl.BlockSpec((3, 2, 2, AB, 2 * AB), lambda b, hp: (0, hp, 0, 0, 0))],
        out_specs=[blk, blk, blk],
        out_shape=[jax.ShapeDtypeStruct((T, D), f32), jax.ShapeDtypeStruct((T, D), bf16),
                   jax.ShapeDtypeStruct((T, D), f32)],
        scratch_shapes=[pltpu.VMEM((S, 128), bf16), pltpu.VMEM((S + AB, 128), bf16), pltpu.VMEM((S + AB, 128), bf16),
                        pltpu.VMEM((nblk, 2 * AB, 2 * AB), f32), pltpu.VMEM((nblk, 2 * AB, 2 * AB), bf16),
                        pltpu.VMEM((S, 128), f32), pltpu.VMEM((S, 128), f32)] + [pltpu.VMEM((S, 128), f32)] * 6,
        compiler_params=_cparams(("parallel", "parallel")))(qn, kn, z8, bias)


def _attn_bwd(qn, kn, z8, do, o, lse, bias, bd, S):
    T = qn.shape[0]
    nb = T // S

    nblk = S // AB

    def body(q_ref, k_ref, v_ref, do_ref, o_ref, lse_ref, bias_ref, bd_ref, dq_ref, dk_ref, dv_ref,
             delta, qs, ks, vs, dos, lsp, dlp, s2, dp2, p2, ds2, dqp, dkp, dvp):
        masks = _head_masks()
        bdv = bd_ref[...]
        dq_ref[...] = jnp.zeros_like(dq_ref)
        dk_ref[...] = jnp.zeros_like(dk_ref)
        dv_ref[...] = jnp.zeros_like(dv_ref)
        ks[0:AB, :] = jnp.zeros((AB, 128), bf16)
        vs[0:AB, :] = jnp.zeros((AB, 128), bf16)

        def prep(i, carry):
            rr = pl.ds(pl.multiple_of(i * 256, 256), 256)
            delta[rr, :] = _head_sum(do_ref[rr, :] * o_ref[rr, :], bdv)
            return carry

        lax.fori_loop(0, S // 256, prep, 0)

        for g, (_, d) in enumerate(GROUPS):
            nsub = S // (d * AB)
            chunks = _perm_chunks(S, d)
            for src, dst, ch in chunks:
                qs[dst:dst + ch, :] = q_ref[src, :].astype(bf16)
                ks[AB + dst:AB + dst + ch, :] = k_ref[src, :].astype(bf16)
                vs[AB + dst:AB + dst + ch, :] = v_ref[src, :].astype(bf16)
                dos[dst:dst + ch, :] = do_ref[src, :].astype(bf16)
                lsp[dst:dst + ch, :] = lse_ref[src, :]
                dlp[dst:dst + ch, :] = delta[src, :]
            dkp[...] = jnp.zeros_like(dkp)
            dvp[...] = jnp.zeros_like(dvp)

            def scores(j, carry):
                r0 = pl.multiple_of(j * AB, AB)
                q2 = _stack_heads(qs[pl.ds(r0, AB), :], masks)
                do2 = _stack_heads(dos[pl.ds(r0, AB), :], masks)
                s2[j] = lax.dot_general(q2, ks[pl.ds(r0, 2 * AB), :], _NT, preferred_element_type=f32)
                dp2[j] = lax.dot_general(do2, vs[pl.ds(r0, 2 * AB), :], _NT, preferred_element_type=f32)
                return carry

            lax.fori_loop(0, nblk, scores, 0, unroll=8)

            def probs(j, carry, g=g, nsub=nsub):
                r0 = pl.multiple_of(j * AB, AB)
                kind = (j % nsub == 0).astype(jnp.int32)
                for cc in range(AB // SCH):
                    lse_c = lsp[pl.ds(r0 + cc * SCH, SCH), :]
                    del_c = dlp[pl.ds(r0 + cc * SCH, SCH), :]
                    for hh in range(2):
                        c0 = hh * HEAD_DIM
                        rows = pl.ds(hh * AB + cc * SCH, SCH)
                        sb = s2[j, rows, :] - bias_ref[g, hh, kind, cc * SCH:(cc + 1) * SCH, :]
                        p = jnp.exp(sb - lse_c[:, c0:c0 + 1])
                        p2[j, rows, :] = p.astype(bf16)
                        ds2[j, rows, :] = (p * (dp2[j, rows, :] - del_c[:, c0:c0 + 1])).astype(bf16)
                return carry

            lax.fori_loop(0, nblk, probs, 0, unroll=2)

            def grads(j, carry):
                r0 = pl.multiple_of(j * AB, AB)
                q2 = _stack_heads(qs[pl.ds(r0, AB), :], masks)
                do2 = _stack_heads(dos[pl.ds(r0, AB), :], masks)
                dsb = ds2[j]
                t = jnp.dot(dsb, ks[pl.ds(r0, 2 * AB), :], preferred_element_type=f32)
                dqp[pl.ds(r0, AB), :] = jnp.where(masks[0], t[:AB], t[AB:])
                dkp[pl.ds(r0, 2 * AB), :] += lax.dot_general(dsb, q2, _TN, preferred_element_type=f32)
                dvp[pl.ds(r0, 2 * AB), :] += lax.dot_general(p2[j], do2, _TN, preferred_element_type=f32)
                return carry

            lax.fori_loop(0, nblk, grads, 0, unroll=4)

            for src, dst, ch in chunks:
                dq_ref[src, :] += dqp[dst:dst + ch, :]
                dk_ref[src, :] += dkp[AB + dst:AB + dst + ch, :]
                dv_ref[src, :] += dvp[AB + dst:AB + dst + ch, :]

    blk = pl.BlockSpec((S, 128), lambda b, hp: (b, hp))
    row = lambda dt, pad=0: pltpu.VMEM((S + pad, 128), dt)
    blocks = lambda dt: pltpu.VMEM((nblk, 2 * AB, 2 * AB), dt)
    return pl.pallas_call(
        body, name="attn_bwd", grid=(nb, N_HEADS // 2),
        in_specs=[blk, blk, pl.BlockSpec((None, S, 128), lambda b, hp: (Z_V, b, hp)), blk, blk, blk,
                  pl.BlockSpec((3, 2, 2, AB, 2 * AB), lambda b, hp: (0, hp, 0, 0, 0)),
                  pl.BlockSpec((128, 128), lambda b, hp: (0, 0))],
        out_specs=[blk, blk, blk],
        out_shape=[jax.ShapeDtypeStruct((T, D), f32)] * 3,
        scratch_shapes=[row(f32), row(bf16), row(bf16, AB), row(bf16, AB), row(bf16), row(f32), row(f32),
                        blocks(f32), blocks(f32), blocks(bf16), blocks(bf16), row(f32), row(f32, AB), row(f32, AB)],
        compiler_params=_cparams(("parallel", "parallel")))(qn, kn, z8, do, o, lse, bias, bd)


def _any_spec():
    return pl.BlockSpec(memory_space=pl.ANY)


def _allgather_rows(shards, n_full):
    n = len(shards)

    def body(*refs):
        ins, outs = refs[:n], refs[n:2 * n]
        send_sems, recv_sems, local_sems = refs[2 * n:]
        x, y, c, me = _my_pos()
        sibling = (x, y, 1 - c)
        chips = [(1 - x, y), (x, 1 - y), (1 - x, 1 - y)]

        def idx(px, py, pc):
            return 4 * px + 2 * py + pc

        def copy(a, k, blk, to, src=None):
            return pltpu.make_async_remote_copy(
                src_ref=outs[a].at[blk] if src is None else src, dst_ref=outs[a].at[blk],
                send_sem=send_sems.at[a, k], recv_sem=recv_sems.at[a, k], device_id=to, device_id_type=MESH)

        mine = [pltpu.make_async_copy(ins[a], outs[a].at[me], local_sems.at[a]) for a in range(n)]
        for cp in mine:
            cp.start()
        first = []
        for a in range(n_full):
            first.append(copy(a, 0, me, sibling, src=ins[a]))
            first += [copy(a, 1 + j, me, (*chip, c), src=ins[a]) for j, chip in enumerate(chips)]
        for cp in first:
            cp.start()
        passed = []
        for a in range(n_full):
            for j, chip in enumerate(chips):
                blk = idx(*chip, c)
                copy(a, 1 + j, blk, (x, y, c)).wait_recv()
                cp = copy(a, 4 + j, blk, sibling)
                cp.start()
                passed.append(cp)
        for a in range(n_full):
            copy(a, 0, idx(x, y, 1 - c), (x, y, c)).wait_recv()
            for j, chip in enumerate(chips):
                copy(a, 4 + j, idx(*chip, 1 - c), (x, y, c)).wait_recv()
        for cp in first + passed:
            cp.wait_send()
        for cp in mine:
            cp.wait()

    return pl.pallas_call(
        body, name="allgather_weights",
        in_specs=[_any_spec()] * n, out_specs=[_any_spec()] * n,
        out_shape=[jax.ShapeDtypeStruct((N_DEV,) + s.shape, s.dtype) for s in shards],
        scratch_shapes=[pltpu.SemaphoreType.DMA((n_full, 7)), pltpu.SemaphoreType.DMA((n_full, 7)),
                        pltpu.SemaphoreType.DMA((n,))],
    )(*shards)


def _peer(x, y, c, k):
    tx = 1 - x if (k >> 2) & 1 else x
    ty = 1 - y if (k >> 1) & 1 else y
    tc = 1 - c if k & 1 else c
    return (tx, ty, tc), 4 * tx + 2 * ty + tc


_PEER_ORDER = (2, 4, 6, 3, 5, 7, 1)


_HBM = pl.BlockSpec(memory_space=pltpu.HBM)
_SEM = pl.BlockSpec(memory_space=pltpu.SEMAPHORE)
_EFFECT = pltpu.SideEffectType.DATAFLOW_SIDE_EFFECTING


def _exchange_copies(srcs, lands, send_sems, recv_sems, gather):
    x, y, c, me = _my_pos()
    copies = []
    for k in _PEER_ORDER:
        tgt, tidx = _peer(x, y, c, k)
        for a in range(len(srcs)):
            copies.append(pltpu.make_async_remote_copy(
                src_ref=srcs[a] if gather else srcs[a].at[tidx], dst_ref=lands[a].at[me],
                send_sem=send_sems.at[7 * a + k - 1], recv_sem=recv_sems.at[7 * a + k - 1],
                device_id=tgt, device_id_type=MESH))
    return copies


def _exchange_start(name, srcs, lands=None, after=None):
    n = len(srcs)
    gather = lands is not None
    if lands is None:
        lands = [lax.empty(g.shape, g.dtype) for g in srcs]
    extra = [] if after is None else [after]

    def body(*refs):
        src_refs, land_refs = refs[:n], refs[n:2 * n]
        send_sems, recv_sems = refs[2 * n + len(extra)], refs[2 * n + len(extra) + 1]
        token = refs[-1]
        for cp in _exchange_copies(src_refs, land_refs, send_sems, recv_sems, gather):
            cp.start()
        token[...] = jnp.zeros_like(token)

    hbm = lambda a: pltpu.with_memory_space_constraint(a, pltpu.HBM)
    outs = pl.pallas_call(
        body, name=name,
        out_shape=(pltpu.SemaphoreType.DMA((7 * n,)), pltpu.SemaphoreType.DMA((7 * n,)),
                   *[pltpu.HBM(g.shape, g.dtype) for g in list(srcs) + list(lands)],
                   jax.ShapeDtypeStruct((8, 128), f32)),
        in_specs=[_HBM] * (2 * n) + [pl.BlockSpec(memory_space=pl.ANY)] * len(extra),
        out_specs=(_SEM, _SEM, *([_HBM] * (2 * n)), pl.BlockSpec(memory_space=pltpu.VMEM)),
        input_output_aliases={i: 2 + i for i in range(2 * n)},
        compiler_params=pltpu.CompilerParams(has_side_effects=_EFFECT),
    )(*[hbm(g) for g in srcs], *[hbm(g) for g in lands], *extra)
    return outs[0], outs[1], list(outs[2:2 + n]), list(outs[2 + n:2 + 2 * n]), outs[-1], gather


def _exchange_wait(name, started, after):
    send_sems, recv_sems, srcs, lands, _, gather = started
    n = len(srcs)
    after = list(after) if isinstance(after, (list, tuple)) else [after]

    def body(*refs):
        src_refs, land_refs = refs[:n], refs[n:2 * n]
        s_sems, r_sems = refs[2 * n], refs[2 * n + 1]
        for cp in _exchange_copies(src_refs, land_refs, s_sems, r_sems, gather):
            cp.wait_send()
            cp.wait_recv()

    outs = pl.pallas_call(
        body, name=name,
        out_shape=tuple(pltpu.HBM(a.shape, a.dtype) for a in list(srcs) + list(lands)),
        in_specs=[_HBM] * (2 * n) + [_SEM, _SEM] + [pl.BlockSpec(memory_space=pl.ANY)] * len(after),
        out_specs=tuple([_HBM] * (2 * n)),
        input_output_aliases={i: i for i in range(2 * n)},
        compiler_params=pltpu.CompilerParams(has_side_effects=_EFFECT),
    )(*srcs, *lands, send_sems, recv_sems, *after)
    return list(outs[:n]), list(outs[n:])


SMALL_ROWS = 128


def _small_start(name, sg, after=None):
    return _exchange_start(name, [sg], [lax.empty((N_DEV,) + sg.shape, f32)], after=after)


def _small_sum(name, me, started, after):
    (own,), (slots,) = _exchange_wait(name + "_wait", started, after)

    def body(me_ref, s_ref, own_ref, out_ref):
        acc = None
        for p in range(N_DEV):
            term = lax.cond(me_ref[0] == p, lambda: own_ref[...], lambda p=p: s_ref[p])
            acc = term if acc is None else acc + term
        out_ref[...] = acc

    return pl.pallas_call(
        body, name=name + "_sum",
        in_specs=[pl.BlockSpec(memory_space=pltpu.SMEM), pl.BlockSpec(memory_space=pltpu.VMEM),
                  pl.BlockSpec(memory_space=pltpu.VMEM)],
        out_specs=pl.BlockSpec(memory_space=pltpu.VMEM),
        out_shape=jax.ShapeDtypeStruct(own.shape, f32))(me, slots, own)


def _adam_math(g, w, m, v):
    m = ADAM_B1 * m + (1.0 - ADAM_B1) * g
    v = ADAM_B2 * v + (1.0 - ADAM_B2) * (g * g)
    m_hat = m / (1.0 - ADAM_B1 ** ADAM_STEP)
    v_hat = v / (1.0 - ADAM_B2 ** ADAM_STEP)
    delta = -ADAM_LR * (m_hat / (jnp.sqrt(v_hat) + ADAM_EPS) + ADAM_WD * w)
    return delta, m, v


def _adam_slots(name, me, slots, own, w, m, v, tr, transposed=False):
    rows = slots.shape[1]

    def body(me_ref, s_ref, own_ref, w_ref, m_ref, v_ref, g_ref, d_ref, nm_ref, nv_ref):
        mine = own_ref[...]
        g = None
        for p in range(N_DEV):
            term = lax.cond(me_ref[0] == p, lambda: mine, lambda p=p: s_ref[p]).astype(f32)
            g = term if g is None else g + term
        if transposed:
            g = g.T
        delta, nm, nv = _adam_math(g, w_ref[...], m_ref[...], v_ref[...])
        g_ref[...] = g
        d_ref[...] = delta
        nm_ref[...] = nm
        nv_ref[...] = nv

    if transposed:
        rs = pl.BlockSpec((D, tr), lambda i, me_ref: (0, i))
    else:
        rs = pl.BlockSpec((tr, D), lambda i, me_ref: (i, 0))
    return pl.pallas_call(
        body, name=name,
        grid_spec=pltpu.PrefetchScalarGridSpec(
            num_scalar_prefetch=1, grid=(rows // tr,),
            in_specs=[pl.BlockSpec((N_DEV, tr, D), lambda i, me_ref: (0, i, 0)),
                      pl.BlockSpec((None, tr, D), lambda i, me_ref: (me_ref[0], i, 0)), rs, rs, rs],
            out_specs=[rs] * 4),
        out_shape=[jax.ShapeDtypeStruct(w.shape, f32)] * 4,
        compiler_params=_cparams(("parallel",)))(me, slots, own, w, m, v)


def _adam_small(g, w, m, v):
    def body(g_ref, w_ref, m_ref, v_ref, d_ref, nm_ref, nv_ref):
        delta, nm, nv = _adam_math(g_ref[...], w_ref[...], m_ref[...], v_ref[...])
        d_ref[...] = delta
        nm_ref[...] = nm
        nv_ref[...] = nv

    return pl.pallas_call(body, name="adam_small", out_shape=[jax.ShapeDtypeStruct(g.shape, f32)] * 3)(g, w, m, v)


FFN_PAD = 6 * D


_SMALL_PARTS = (("norm1_g", 1), ("gate_b", 2), ("conv_w", CONV_WIDTH), ("conv_b", 1), ("conv_norm_g", 1),
                ("q_norm_g", 1), ("k_norm_g", 1), ("norm2_g", 1), ("ffn_conv_w", 18), ("ffn_conv_b", 6), ("last", 1))


def _small_offsets():
    out, row = {}, 0
    for name, rows in _SMALL_PARTS:
        out[name] = row
        row += -(-rows // 8) * 8
    assert row == SMALL_ROWS
    return out


def _pack_small(norm1_g, gate_b, conv_w, conv_b, conv_norm_g, q_norm_g, k_norm_g, norm2_g, ffn_conv_w, ffn_conv_b,
                last_row=None):
    pad_h = lambda a: jnp.pad(a, ((0, 0), (0, D - HEAD_DIM)))
    pad_f = lambda a: jnp.pad(a, ((0, 0), (0, FFN_PAD - 2 * D_FF))).reshape(-1, D)
    parts = [norm1_g, gate_b.reshape(2, D), conv_w, conv_b, conv_norm_g, pad_h(q_norm_g), pad_h(k_norm_g), norm2_g,
             pad_f(ffn_conv_w), pad_f(ffn_conv_b), jnp.zeros((1, D), f32) if last_row is None else last_row]
    return jnp.concatenate([jnp.pad(p, ((0, -p.shape[0] % 8), (0, 0))) for p in parts], axis=0)


def _unpack_small(p):
    o = _small_offsets()
    rows = lambda name, n: p[o[name]:o[name] + n]
    ffn = lambda a: a.reshape(-1, FFN_PAD)[:, :2 * D_FF]
    return dict(
        norm1_g=rows("norm1_g", 1), gate_b=rows("gate_b", 2).reshape(1, 2 * D), conv_w=rows("conv_w", CONV_WIDTH),
        conv_b=rows("conv_b", 1), conv_norm_g=rows("conv_norm_g", 1), q_norm_g=rows("q_norm_g", 1)[:, :HEAD_DIM],
        k_norm_g=rows("k_norm_g", 1)[:, :HEAD_DIM], norm2_g=rows("norm2_g", 1),
        ffn_conv_w=ffn(rows("ffn_conv_w", 18)), ffn_conv_b=ffn(rows("ffn_conv_b", 6)))


_ADAM_TILE = {896: 128, 704: 64, 128: 128, 352: 176}


def kernel(x, norm1_g, w_in, gate_b, conv_w, conv_b, conv_norm_g, w_conv_out, q_norm_g, k_norm_g, w_attn_out, w_out, norm2_g, w_up, ffn_conv_w, ffn_conv_b, w_down, loss_target, m_norm1_g, m_w_in, m_gate_b, m_conv_w, m_conv_b, m_conv_norm_g, m_w_conv_out, m_q_norm_g, m_k_norm_g, m_w_attn_out, m_w_out, m_norm2_g, m_w_up, m_ffn_conv_w, m_ffn_conv_b, m_w_down, v_norm1_g, v_w_in, v_gate_b, v_conv_w, v_conv_b, v_conv_norm_g, v_w_conv_out, v_q_norm_g, v_k_norm_g, v_w_attn_out, v_w_out, v_norm2_g, v_w_up, v_ffn_conv_w, v_ffn_conv_b, v_w_down):
    BL, S, _ = x.shape
    T = BL * S
    me = 4 * lax.axis_index("x") + 2 * lax.axis_index("y") + lax.axis_index("c")
    xt = x.reshape(T, D)
    target = loss_target.reshape(T, D)

    big = dict(w_in=(w_in[0], m_w_in[0], v_w_in[0]), w_up=(w_up[0].T, m_w_up[0].T, v_w_up[0].T),
               w_conv_out=(w_conv_out[0], m_w_conv_out[0], v_w_conv_out[0]),
               w_attn_out=(w_attn_out[0], m_w_attn_out[0], v_w_attn_out[0]),
               w_out=(w_out[0], m_w_out[0], v_w_out[0]), w_down=(w_down[0], m_w_down[0], v_w_down[0]))
    order = ["w_in", "w_conv_out", "w_attn_out", "w_out", "w_up", "w_down"]
    shards = [(big[n][0].T if n == "w_in" else big[n][0]).astype(bf16) for n in order]
    gathered = _allgather_rows(shards, 1)
    ga_proj = _exchange_start("gather_start_proj", shards[1:4], gathered[1:4], after=gathered[0])
    ga_ffn = _exchange_start("gather_start_ffn", shards[4:6], gathered[4:6], after=ga_proj[4])
    W = {"w_in": gathered[0].reshape(-1, D)}

    def place_cols(shard, full_cols):
        z = jnp.zeros((shard.shape[0], full_cols), f32)
        return lax.dynamic_update_slice(z, shard, (0, me * shard.shape[1]))

    zr = lambda a: jnp.zeros_like(a)
    conv_local = _pack_small(
        zr(norm1_g), zr(gate_b), place_cols(conv_w[0], D), zr(conv_b), zr(conv_norm_g), zr(q_norm_g), zr(k_norm_g),
        zr(norm2_g), place_cols(ffn_conv_w[0], 2 * D_FF), zr(ffn_conv_b))
    ga_conv = _small_start("gather_conv_start", conv_local, after=ga_ffn[4])

    bd = (jnp.arange(128)[:, None] // HEAD_DIM == jnp.arange(128)[None, :] // HEAD_DIM).astype(bf16)
    bias = _attn_bias()
    qg = jnp.tile(q_norm_g, (1, N_HEADS))
    kg = jnp.tile(k_norm_g, (1, N_HEADS))

    h = _norm1_fwd(xt, norm1_g)
    z8 = _matmul_call(
        "mm_z", h, W["w_in"],
        pl.BlockSpec((1024, D), lambda i, j, k: (i, 0)),
        pl.BlockSpec((1024, D), lambda i, j, k: (_wsec_of_zsec(j), 0)),
        pl.BlockSpec((None, 1024, D), lambda i, j, k: (j, i, 0)),
        jax.ShapeDtypeStruct((8, T, D), f32), (T // 1024, 7, 1), "nt", 1, 1024, 1024, after=ga_conv[4])
    conv_all = _unpack_small(_small_sum("gather_conv", me.reshape(1), ga_conv, z8))
    conv_w_full, ffn_w_full = conv_all["conv_w"], conv_all["ffn_conv_w"]
    c = _conv_fwd(z8, conv_w_full, conv_b, S)
    s = _convnorm_fwd(c, conv_norm_g)
    qn, kn = _qk_fwd(z8, qg, kg, bd)
    for n, g in zip(order[1:4], _exchange_wait("gather_wait_proj", ga_proj, qn)[1]):
        W[n] = g.reshape(-1, D)
    ya = _matmul("mm_ya", s, W["w_conv_out"], "nn", f32)
    o, ob, lse = _attn_fwd(qn, kn, z8, bias, S)
    yb = _matmul("mm_yb", ob, W["w_attn_out"], "nn", f32)
    mixed = _gate_fwd(z8, gate_b, ya, yb)
    t1 = _matmul("mm_t1", mixed, W["w_out"], "nn", f32)
    for n, g in zip(order[4:6], _exchange_wait("gather_wait_ffn", ga_ffn, t1)[1]):
        W[n] = g.reshape(-1, D)
    x1, h2 = _norm2_fwd(xt, t1, norm2_g)
    TNU = D_FF // 2
    u3 = _matmul_call(
        "mm_u", h2, W["w_up"],
        pl.BlockSpec((1024, D), lambda i, j, k: (i, 0)),
        pl.BlockSpec((TNU, D), lambda i, j, k: (j, 0)),
        pl.BlockSpec((None, 1024, TNU), lambda i, j, k: (j // 2, i, j % 2)),
        jax.ShapeDtypeStruct((2, T, D_FF), f32), (T // 1024, 4, 1), "nt", 1, 1024, TNU)
    f = _ffn_fwd(u3, ffn_w_full, ffn_conv_b, S)
    t2 = _matmul("mm_t2", f, W["w_down"], "nn", f32, tk=TNU)
    dy, dyb, lacc = _loss_fwd(x1, t2, target)
    loss_local = 0.5 / D * jnp.sum(lacc)

    df = _matmul("mm_df", dyb, W["w_down"], "nt", f32, tn=TNU)
    g_w_down = _matmul("mm_dwdn", f, dyb, "tn", bf16, tm=TNU, tk=1024)
    du3, dffn = _ffn_bwd(u3, df, ffn_w_full, ffn_conv_b, S)
    g_w_up = _matmul_call(
        "mm_dwup", du3, h2,
        pl.BlockSpec((None, 1024, TNU), lambda i, j, k: (i // 2, k, i % 2)),
        pl.BlockSpec((1024, D), lambda i, j, k: (k, 0)),
        pl.BlockSpec((TNU, D), lambda i, j, k: (i, 0)),
        jax.ShapeDtypeStruct((2 * D_FF, D), bf16), (4, 1, T // 1024), "tn", T // 1024, TNU, D)
    blocks8 = lambda a: a.reshape(N_DEV, -1, D)
    ex_ffn = _exchange_start("scatter_start_ffn", [blocks8(g_w_up), blocks8(g_w_down)])
    dh2 = _matmul_call(
        "mm_dh2", du3, W["w_up"],
        pl.BlockSpec((None, 1024, TNU), lambda i, j, k: (k // 2, i, k % 2)),
        pl.BlockSpec((TNU, D), lambda i, j, k: (k, 0)),
        pl.BlockSpec((1024, D), lambda i, j, k: (i, 0)),
        jax.ShapeDtypeStruct((T, D), f32), (T // 1024, 1, 4), "nn", 4, 1024, D, after=ex_ffn[4])
    dx1, dx1b, dg_norm2 = _norm2_bwd(x1, dh2, dy, norm2_g)
    dmixed = _matmul("mm_dmixed", dx1b, W["w_out"], "nt", f32)
    g_w_out = _matmul("mm_dwo", mixed, dx1b, "tn", bf16, tk=1024)
    dz8 = lax.empty((8, T, D), bf16)
    dya, dyb2, dz8, dg_gate = _gate_bwd(dmixed, z8, gate_b, ya, yb, dz8)
    ds = _matmul("mm_ds", dya, W["w_conv_out"], "nt", f32)
    g_w_conv_out = _matmul("mm_dwco", s, dya, "tn", bf16, tk=1024)
    g_w_attn_out = _matmul("mm_dwao", ob, dyb2, "tn", bf16, tk=1024)
    ex_proj = _exchange_start("scatter_start_proj", [blocks8(g_w_conv_out), blocks8(g_w_attn_out), blocks8(g_w_out)])
    do = _matmul("mm_do", dyb2, W["w_attn_out"], "nt", f32, after=ex_proj[4])
    dc, dg_convnorm = _convnorm_bwd(c, ds, conv_norm_g)
    dz8a, dconv = _conv_bwd(dc, z8, conv_w_full, dz8, S)
    dqn, dkn, dv = _attn_bwd(qn, kn, z8, do, o, lse, bias, bd, S)
    dz8b, dg_q, dg_k = _qk_bwd(z8, dqn, dkn, dv, qg, kg, bd, dz8a)
    g_w_in = _matmul_call(
        "mm_dwin", dz8b, h,
        pl.BlockSpec((None, 1024, D), lambda i, j, k: (_zsec_of_wsec(i), k, 0)),
        pl.BlockSpec((1024, D), lambda i, j, k: (k, 0)),
        pl.BlockSpec((1024, D), lambda i, j, k: (i, 0)),
        jax.ShapeDtypeStruct((7 * D, D), bf16), (7, 1, T // 1024), "tn", T // 1024, D, D)
    ex_in = _exchange_start("scatter_start_in", [blocks8(g_w_in)])
    dh = _matmul_call(
        "mm_dh", dz8b, W["w_in"],
        pl.BlockSpec((None, 1024, D), lambda i, j, k: (k, i, 0)),
        pl.BlockSpec((1024, D), lambda i, j, k: (_wsec_of_zsec(k), 0)),
        pl.BlockSpec((1024, D), lambda i, j, k: (i, 0)),
        jax.ShapeDtypeStruct((T, D), f32), (T // 1024, 1, 7), "nn", 7, 1024, D, after=ex_in[4])
    grad_x, dg_norm1 = _norm1_bwd(xt, dh, dx1, norm1_g)

    sum8 = lambda a: a.reshape(-1, 8, a.shape[-1]).sum(axis=1)
    dconv_s = sum8(dconv.sum(axis=0))
    dffn_s = dffn.sum(axis=0).reshape(2, 4, 8, D_FF).sum(axis=2)
    dffn_w = jnp.concatenate([dffn_s[0, :3], dffn_s[1, :3]], axis=1)
    dffn_b = jnp.concatenate([dffn_s[0, 3:4], dffn_s[1, 3:4]], axis=1)
    fold = lambda a: sum8(a).reshape(N_HEADS, HEAD_DIM).sum(axis=0)[None]
    small_g_local = _pack_small(
        sum8(dg_norm1), sum8(dg_gate), dconv_s[:CONV_WIDTH], dconv_s[CONV_WIDTH:], sum8(dg_convnorm),
        fold(dg_q), fold(dg_k), sum8(dg_norm2), dffn_w, dffn_b,
        last_row=jnp.pad(loss_local.reshape(1, 1), ((0, 0), (0, D - 1))))
    sg_start = _small_start("small_grads_start", small_g_local)

    own, slots = {}, {}
    for tag, ex, names_ in (("ffn", ex_ffn, ("w_up", "w_down")),
                            ("proj", ex_proj, ("w_conv_out", "w_attn_out", "w_out")), ("in", ex_in, ("w_in",))):
        sent, landed = _exchange_wait("scatter_wait_" + tag, ex, sg_start[4])
        for n, src, land in zip(names_, sent, landed):
            own[n], slots[n] = src, land

    res, adam_done = {}, []
    for n in order:
        w, m, v = big[n]
        outs = _adam_slots("adam_" + n, me.reshape(1), slots[n], own[n], w, m, v, _ADAM_TILE[slots[n].shape[1]],
                           transposed=(n == "w_in"))
        adam_done.append(outs[0])
        if n == "w_up":
            outs = [a.T for a in outs]
        res[n] = [a[None] for a in outs]
    small_g = _small_sum("small_grads", me.reshape(1), sg_start, adam_done)
    loss = small_g[_small_offsets()["last"], 0]

    col = lambda a, width: lax.dynamic_slice(a, (0, me * width), (a.shape[0], width))
    small_w_true = _pack_small(norm1_g, gate_b, conv_w_full, conv_b, conv_norm_g, q_norm_g, k_norm_g, norm2_g,
                               ffn_w_full, ffn_conv_b)
    place_m = lambda a, full: place_cols(a[0], full)
    small_m = _pack_small(m_norm1_g, m_gate_b, place_m(m_conv_w, D), m_conv_b, m_conv_norm_g, m_q_norm_g, m_k_norm_g,
                          m_norm2_g, place_m(m_ffn_conv_w, 2 * D_FF), m_ffn_conv_b)
    small_v = _pack_small(v_norm1_g, v_gate_b, place_m(v_conv_w, D), v_conv_b, v_conv_norm_g, v_q_norm_g, v_k_norm_g,
                          v_norm2_g, place_m(v_ffn_conv_w, 2 * D_FF), v_ffn_conv_b)
    sd, sm, sv = _adam_small(small_g, small_w_true, small_m, small_v)
    for i, packed in enumerate((small_g, sd, sm, sv)):
        u = _unpack_small(packed)
        u["conv_w"] = col(u["conv_w"], D // N_DEV)
        u["ffn_conv_w"] = col(u["ffn_conv_w"], 2 * D_FF // N_DEV)
        for n, a in u.items():
            res.setdefault(n, [None] * 4)[i] = a[None] if n in ("conv_w", "ffn_conv_w") else a

    names = ["norm1_g", "w_in", "gate_b", "conv_w", "conv_b", "conv_norm_g", "w_conv_out", "q_norm_g", "k_norm_g",
             "w_attn_out", "w_out", "norm2_g", "w_up", "ffn_conv_w", "ffn_conv_b", "w_down"]
    out = [loss, grad_x.reshape(BL, S, D)]
    for i in range(4):
        out += [res[n][i] for n in names]
    return tuple(out)
```

```python
import functools

import jax
import jax.numpy as jnp
import numpy as np
from jax import lax
from jax.experimental import pallas as pl
from jax.experimental.pallas import tpu as pltpu

f32 = jnp.float32
bf16 = jnp.bfloat16

D = 1024
N_HEADS = 16
HEAD_DIM = 64
CONV_WIDTH = 31
D_FF = 2816
GROUPS = ((128, 1), (512, 4), (2048, 16))
ATTN_BLOCK = 128
EPS = 1e-6
N_DEV = 8
MESH = pl.DeviceIdType.MESH

ADAM_LR = 0.001
ADAM_B1 = 0.9
ADAM_B2 = 0.999
ADAM_EPS = 1e-08
ADAM_WD = 0.01
ADAM_STEP = 10

VMEM_LIMIT = 56 * 1024 * 1024
MASK_BIAS = 1e30

Z_AVAL, Z_AGATE, Z_GA, Z_GB, Z_Q, Z_K, Z_V = 0, 1, 2, 3, 4, 5, 6


_W_OF_Z = (0, 1, 5, 6, 2, 3, 4)


def _wsec_of_zsec(j):
    return jnp.where(j < 2, j, jnp.where(j < 4, j + 3, j - 2))


def _zsec_of_wsec(w):
    return jnp.where(w < 2, w, jnp.where(w < 5, w + 2, w - 3))


def _sig(x):
    return 1.0 / (1.0 + jnp.exp(-x))


def _colsum8(x):
    return x.reshape(-1, 8, x.shape[-1]).sum(axis=0)


def _cparams(sem):
    return pltpu.CompilerParams(dimension_semantics=sem, vmem_limit_bytes=VMEM_LIMIT)


def _my_pos():
    x, y, c = lax.axis_index("x"), lax.axis_index("y"), lax.axis_index("c")
    return x, y, c, 4 * x + 2 * y + c


_DIMS = {"nn": ((1,), (0,)), "nt": ((1,), (1,)), "tn": ((0,), (0,))}


def _matmul_call(name, a, b, a_spec, b_spec, o_spec, out_shape, grid, mode, nk, tm, tn, after=None):
    dims = (_DIMS[mode], ((), ()))
    extra = [] if after is None else [after]

    def body(a_ref, b_ref, *rest):
        o_ref, scratch = rest[len(extra)], rest[len(extra) + 1:]
        part = lax.dot_general(a_ref[...], b_ref[...], dims, preferred_element_type=f32)
        if nk == 1:
            o_ref[...] = part.astype(o_ref.dtype)
        else:
            acc = scratch[0]
            k = pl.program_id(2)

            @pl.when(k == 0)
            def _():
                acc[...] = part

            @pl.when(k > 0)
            def _():
                acc[...] += part

            @pl.when(k == nk - 1)
            def _():
                o_ref[...] = acc[...].astype(o_ref.dtype)

    scratch = [] if nk == 1 else [pltpu.VMEM((tm, tn), f32)]
    return pl.pallas_call(
        body, name=name, grid=grid, in_specs=[a_spec, b_spec] + [pl.BlockSpec(memory_space=pl.ANY)] * len(extra),
        out_specs=o_spec, out_shape=out_shape,
        scratch_shapes=scratch, compiler_params=_cparams(("parallel", "parallel", "arbitrary")),
    )(a, b, *extra)


def _matmul(name, a, b, mode, out_dtype, tm=1024, tn=1024, tk=None, after=None):
    if mode == "nn":
        (M, K), (_, N) = a.shape, b.shape
    elif mode == "nt":
        (M, K), (N, _) = a.shape, b.shape
    else:
        (K, M), (_, N) = a.shape, b.shape
    tm, tn = min(tm, M), min(tn, N)
    tk = K if tk is None else tk
    nk = K // tk
    assert M % tm == 0 and N % tn == 0 and K % tk == 0
    if mode == "tn":
        a_spec = pl.BlockSpec((tk, tm), lambda i, j, k: (k, i))
    else:
        a_spec = pl.BlockSpec((tm, tk), lambda i, j, k: (i, k))
    if mode == "nt":
        b_spec = pl.BlockSpec((tn, tk), lambda i, j, k: (j, k))
    else:
        b_spec = pl.BlockSpec((tk, tn), lambda i, j, k: (k, j))
    o_spec = pl.BlockSpec((tm, tn), lambda i, j, k: (i, j))
    return _matmul_call(name, a, b, a_spec, b_spec, o_spec, jax.ShapeDtypeStruct((M, N), out_dtype),
                        (M // tm, N // tn, nk), mode, nk, tm, tn, after=after)


def _matmul_sections(name, a, b, pairs, tm, after=None):
    sa, M, kk = a.shape
    sb, _, N = b.shape
    na = max(i for i, _ in pairs) + 1
    extra = [] if after is None else [after]

    def body(a_ref, b_ref, *rest):
        o_ref = rest[len(extra)]
        acc = None
        for i, j in pairs:
            part = jnp.dot(a_ref[i], b_ref[j], preferred_element_type=f32)
            acc = part if acc is None else acc + part
        o_ref[...] = acc

    return pl.pallas_call(
        body, name=name, grid=(M // tm,),
        in_specs=[pl.BlockSpec((na, tm, kk), lambda i: (0, i, 0)), pl.BlockSpec((sb, kk, N), lambda i: (0, 0, 0))]
        + [pl.BlockSpec(memory_space=pl.ANY)] * len(extra),
        out_specs=pl.BlockSpec((tm, N), lambda i: (i, 0)), out_shape=jax.ShapeDtypeStruct((M, N), f32),
        compiler_params=_cparams(("parallel",)))(a, b, *extra)


TT = 512


def _rows(c, cb=0, tt=TT):
    return pl.BlockSpec((tt, c), lambda i: (i, cb))


def _sec(s, tt=TT):
    return pl.BlockSpec((None, tt, D), lambda i: (s, i, 0))


def _const(shape):
    return pl.BlockSpec(shape, lambda i: (0,) * len(shape))


def _acc_spec(c):
    return pl.BlockSpec((8, c), lambda i: (0, 0))


def _rms(x):
    return lax.rsqrt(jnp.mean(x * x, axis=-1, keepdims=True) + EPS)


def _rms_bwd(dy_g, xn, rstd):
    return rstd * (dy_g - xn * jnp.mean(dy_g * xn, axis=-1, keepdims=True))


def _head_sum(x, bd):
    parts = []
    for cb in range(x.shape[-1] // 128):
        xb = x[:, cb * 128:(cb + 1) * 128]
        hi = xb.astype(bf16)
        lo = (xb - hi.astype(f32)).astype(bf16)
        parts.append(jnp.dot(hi, bd, preferred_element_type=f32) + jnp.dot(lo, bd, preferred_element_type=f32))
    return parts[0] if len(parts) == 1 else jnp.concatenate(parts, axis=1)


def _norm1_fwd(x, g):
    T = x.shape[0]

    def body(x_ref, g_ref, h_ref):
        xv = x_ref[...]
        h_ref[...] = (xv * _rms(xv) * g_ref[...]).astype(bf16)

    return pl.pallas_call(
        body, name="norm1_fwd", grid=(T // TT,), in_specs=[_rows(D), _const((1, D))], out_specs=_rows(D),
        out_shape=jax.ShapeDtypeStruct((T, D), bf16), compiler_params=_cparams(("parallel",)))(x, g)


def _convnorm_fwd(c, g):
    T = c.shape[0]

    def body(c_ref, g_ref, s_ref):
        cv = c_ref[...]
        r = cv * _rms(cv) * g_ref[...]
        s_ref[...] = (r * _sig(r)).astype(bf16)

    return pl.pallas_call(
        body, name="convnorm_fwd", grid=(T // TT,), in_specs=[_rows(D), _const((1, D))], out_specs=_rows(D),
        out_shape=jax.ShapeDtypeStruct((T, D), bf16), compiler_params=_cparams(("parallel",)))(c, g)


def _qk_fwd(z8, qg, kg, bd):
    T = z8.shape[1]

    def body(q_ref, k_ref, qg_ref, kg_ref, bd_ref, qn_ref, kn_ref):
        bdv = bd_ref[...]
        q = q_ref[...]
        qn_ref[...] = q * lax.rsqrt(_head_sum(q * q, bdv) * (1.0 / HEAD_DIM) + EPS) * qg_ref[...] * (HEAD_DIM ** -0.5)
        k = k_ref[...]
        kn_ref[...] = k * lax.rsqrt(_head_sum(k * k, bdv) * (1.0 / HEAD_DIM) + EPS) * kg_ref[...]

    return pl.pallas_call(
        body, name="qk_fwd", grid=(T // TT,),
        in_specs=[_sec(Z_Q), _sec(Z_K), _const((1, D)), _const((1, D)), _const((128, 128))],
        out_specs=[_rows(D), _rows(D)],
        out_shape=[jax.ShapeDtypeStruct((T, D), f32)] * 2, compiler_params=_cparams(("parallel",)))(z8, z8, qg, kg, bd)


def _gate_fwd(z8, gate_b, ya, yb):
    T = ya.shape[0]

    def body(ga_ref, gb_ref, b_ref, ya_ref, yb_ref, mixed_ref):
        g_a = _sig(ga_ref[...] + b_ref[:, :D])
        g_b = _sig(gb_ref[...] + b_ref[:, D:])
        mixed_ref[...] = (g_a * ya_ref[...] + g_b * yb_ref[...]).astype(bf16)

    return pl.pallas_call(
        body, name="gate_fwd", grid=(T // TT,),
        in_specs=[_sec(Z_GA), _sec(Z_GB), _const((1, 2 * D)), _rows(D), _rows(D)], out_specs=_rows(D),
        out_shape=jax.ShapeDtypeStruct((T, D), bf16), compiler_params=_cparams(("parallel",)))(z8, z8, gate_b, ya, yb)


def _norm2_fwd(x, t1, g):
    T = x.shape[0]

    def body(x_ref, t_ref, g_ref, x1_ref, h2_ref):
        x1 = x_ref[...] + t_ref[...]
        x1_ref[...] = x1
        h2_ref[...] = (x1 * _rms(x1) * g_ref[...]).astype(bf16)

    return pl.pallas_call(
        body, name="norm2_fwd", grid=(T // TT,), in_specs=[_rows(D), _rows(D), _const((1, D))],
        out_specs=[_rows(D), _rows(D)],
        out_shape=[jax.ShapeDtypeStruct((T, D), f32), jax.ShapeDtypeStruct((T, D), bf16)],
        compiler_params=_cparams(("parallel",)))(x, t1, g)


def _loss_fwd(x1, t2, target):
    T = x1.shape[0]

    def body(x1_ref, t_ref, tg_ref, dy_ref, dyb_ref, acc_ref):
        diff = x1_ref[...] + t_ref[...] - tg_ref[...]
        dy = diff * (1.0 / D)
        dy_ref[...] = dy
        dyb_ref[...] = dy.astype(bf16)

        @pl.when(pl.program_id(0) == 0)
        def _():
            acc_ref[...] = jnp.zeros_like(acc_ref)

        acc_ref[...] += _colsum8(diff * diff)

    return pl.pallas_call(
        body, name="loss_fwd", grid=(T // TT,), in_specs=[_rows(D)] * 3,
        out_specs=[_rows(D), _rows(D), _acc_spec(D)],
        out_shape=[jax.ShapeDtypeStruct((T, D), f32), jax.ShapeDtypeStruct((T, D), bf16),
                   jax.ShapeDtypeStruct((8, D), f32)],
        compiler_params=_cparams(("arbitrary",)))(x1, t2, target)


def _norm2_bwd(x1, dh2, dy, g):
    T = x1.shape[0]

    def body(x1_ref, dh_ref, dy_ref, g_ref, dx1_ref, dx1b_ref, dg_ref):
        x1 = x1_ref[...]
        rstd = _rms(x1)
        xn = x1 * rstd
        dh = dh_ref[...]
        dx1 = dy_ref[...] + _rms_bwd(dh * g_ref[...], xn, rstd)
        dx1_ref[...] = dx1
        dx1b_ref[...] = dx1.astype(bf16)

        @pl.when(pl.program_id(0) == 0)
        def _():
            dg_ref[...] = jnp.zeros_like(dg_ref)

        dg_ref[...] += _colsum8(dh * xn)

    return pl.pallas_call(
        body, name="norm2_bwd", grid=(T // TT,), in_specs=[_rows(D), _rows(D), _rows(D), _const((1, D))],
        out_specs=[_rows(D), _rows(D), _acc_spec(D)],
        out_shape=[jax.ShapeDtypeStruct((T, D), f32), jax.ShapeDtypeStruct((T, D), bf16),
                   jax.ShapeDtypeStruct((8, D), f32)],
        compiler_params=_cparams(("arbitrary",)))(x1, dh2, dy, g)


def _gate_bwd(dmixed, z8, gate_b, ya, yb, dz8):
    T = ya.shape[0]

    def body(dm_ref, ga_ref, gb_ref, b_ref, ya_ref, yb_ref, dz_in, dya_ref, dyb_ref, dz_ref, dgb_ref):
        del dz_in
        dm = dm_ref[...]
        g_a = _sig(ga_ref[...] + b_ref[:, :D])
        g_b = _sig(gb_ref[...] + b_ref[:, D:])
        dya_ref[...] = (dm * g_a).astype(bf16)
        dyb_ref[...] = (dm * g_b).astype(bf16)
        dla = dm * ya_ref[...] * g_a * (1.0 - g_a)
        dlb = dm * yb_ref[...] * g_b * (1.0 - g_b)
        dz_ref[0] = dla.astype(bf16)
        dz_ref[1] = dlb.astype(bf16)

        @pl.when(pl.program_id(0) == 0)
        def _():
            dgb_ref[...] = jnp.zeros_like(dgb_ref)

        dgb_ref[:, :D] += _colsum8(dla)
        dgb_ref[:, D:] += _colsum8(dlb)

    return pl.pallas_call(
        body, name="gate_bwd", grid=(T // TT,),
        in_specs=[_rows(D), _sec(Z_GA), _sec(Z_GB), _const((1, 2 * D)), _rows(D), _rows(D),
                  pl.BlockSpec(memory_space=pl.ANY)],
        out_specs=[_rows(D), _rows(D), pl.BlockSpec((2, TT, D), lambda i: (1, i, 0)), _acc_spec(2 * D)],
        out_shape=[jax.ShapeDtypeStruct((T, D), bf16), jax.ShapeDtypeStruct((T, D), bf16),
                   jax.ShapeDtypeStruct(dz8.shape, bf16), jax.ShapeDtypeStruct((8, 2 * D), f32)],
        input_output_aliases={6: 2},
        compiler_params=_cparams(("arbitrary",)))(dmixed, z8, z8, gate_b, ya, yb, dz8)


def _convnorm_bwd(c, ds, g):
    T = c.shape[0]

    def body(c_ref, ds_ref, g_ref, dc_ref, dg_ref):
        cv = c_ref[...]
        rstd = _rms(cv)
        r0 = cv * rstd
        gv = g_ref[...]
        r = r0 * gv
        sg = _sig(r)
        dr = ds_ref[...] * sg * (1.0 + r * (1.0 - sg))
        dc_ref[...] = _rms_bwd(dr * gv, r0, rstd)

        @pl.when(pl.program_id(0) == 0)
        def _():
            dg_ref[...] = jnp.zeros_like(dg_ref)

        dg_ref[...] += _colsum8(dr * r0)

    return pl.pallas_call(
        body, name="convnorm_bwd", grid=(T // TT,), in_specs=[_rows(D), _rows(D), _const((1, D))],
        out_specs=[_rows(D), _acc_spec(D)],
        out_shape=[jax.ShapeDtypeStruct((T, D), f32), jax.ShapeDtypeStruct((8, D), f32)],
        compiler_params=_cparams(("arbitrary",)))(c, ds, g)


def _qk_bwd(z8, dqn, dkn, dv, qg, kg, bd, dz8):
    T = dqn.shape[0]

    def body(q_ref, k_ref, dqn_ref, dkn_ref, dv_ref, qg_ref, kg_ref, bd_ref, dz_in, dz_ref, dqg_ref, dkg_ref):
        del dz_in
        bdv = bd_ref[...]

        @pl.when(pl.program_id(0) == 0)
        def _():
            dqg_ref[...] = jnp.zeros_like(dqg_ref)
            dkg_ref[...] = jnp.zeros_like(dkg_ref)

        def one(raw, dn_scaled, g, dg_ref, sec):
            rstd = lax.rsqrt(_head_sum(raw * raw, bdv) * (1.0 / HEAD_DIM) + EPS)
            n = raw * rstd
            dg_ref[...] += _colsum8(dn_scaled * n)
            dn = dn_scaled * g
            draw = rstd * (dn - n * (_head_sum(dn * n, bdv) * (1.0 / HEAD_DIM)))
            dz_ref[sec] = draw.astype(bf16)

        one(q_ref[...], dqn_ref[...] * (HEAD_DIM ** -0.5), qg_ref[...], dqg_ref, 0)
        one(k_ref[...], dkn_ref[...], kg_ref[...], dkg_ref, 1)
        dz_ref[2] = dv_ref[...].astype(bf16)
        dz_ref[3] = jnp.zeros((TT, D), bf16)

    return pl.pallas_call(
        body, name="qk_bwd", grid=(T // TT,),
        in_specs=[_sec(Z_Q), _sec(Z_K), _rows(D), _rows(D), _rows(D), _const((1, D)), _const((1, D)),
                  _const((128, 128)), pl.BlockSpec(memory_space=pl.ANY)],
        out_specs=[pl.BlockSpec((4, TT, D), lambda i: (1, i, 0)), _acc_spec(D), _acc_spec(D)],
        out_shape=[jax.ShapeDtypeStruct(dz8.shape, bf16), jax.ShapeDtypeStruct((8, D), f32),
                   jax.ShapeDtypeStruct((8, D), f32)],
        input_output_aliases={8: 0},
        compiler_params=_cparams(("arbitrary",)))(z8, z8, dqn, dkn, dv, qg, kg, bd, dz8)


def _norm1_bwd(x, dh, dx1, g):
    T = x.shape[0]

    def body(x_ref, dh_ref, dx1_ref, g_ref, gx_ref, dg_ref):
        xv = x_ref[...]
        rstd = _rms(xv)
        xn = xv * rstd
        dh = dh_ref[...]
        gx_ref[...] = dx1_ref[...] + _rms_bwd(dh * g_ref[...], xn, rstd)

        @pl.when(pl.program_id(0) == 0)
        def _():
            dg_ref[...] = jnp.zeros_like(dg_ref)

        dg_ref[...] += _colsum8(dh * xn)

    return pl.pallas_call(
        body, name="norm1_bwd", grid=(T // TT,), in_specs=[_rows(D), _rows(D), _rows(D), _const((1, D))],
        out_specs=[_rows(D), _acc_spec(D)],
        out_shape=[jax.ShapeDtypeStruct((T, D), f32), jax.ShapeDtypeStruct((8, D), f32)],
        compiler_params=_cparams(("arbitrary",)))(x, dh, dx1, g)


CCW = 256
CR = 64
HALO = 32


def _conv_fwd(z8, conv_w, conv_b, S):
    T = z8.shape[1]
    nb = T // S
    ncb = D // CCW

    def body(av_ref, ag_ref, w_ref, b_ref, c_ref, pad):
        pad[0:HALO, :] = jnp.zeros((HALO, CCW), f32)

        def fill(i, carry):
            r0 = pl.multiple_of(i * 256, 256)
            pad[pl.ds(HALO + r0, 256), :] = av_ref[pl.ds(r0, 256), :] * _sig(ag_ref[pl.ds(r0, 256), :])
            return carry

        lax.fori_loop(0, S // 256, fill, 0)
        bias = b_ref[...]

        def chunk(i, carry):
            r0 = pl.multiple_of(i * CR, CR)
            win = pad[pl.ds(r0, CR + HALO), :]
            acc = jnp.zeros((CR, CCW), f32) + bias
            for s in range(8):
                part = None
                for m in range((CONV_WIDTH - 1 - s) // 8 + 1):
                    j = CONV_WIDTH - 1 - 8 * m - s
                    term = win[24 - 8 * m:24 - 8 * m + CR + 8, :] * w_ref[j:j + 1, :]
                    part = term if part is None else part + term
                acc = acc + part[8 - s:8 - s + CR, :]
            c_ref[pl.ds(r0, CR), :] = acc
            return carry

        lax.fori_loop(0, S // CR, chunk, 0)

    zs = lambda s: pl.BlockSpec((None, S, CCW), lambda b, cb: (s, b, cb))
    return pl.pallas_call(
        body, name="conv_fwd", grid=(nb, ncb),
        in_specs=[zs(Z_AVAL), zs(Z_AGATE), pl.BlockSpec((CONV_WIDTH, CCW), lambda b, cb: (0, cb)),
                  pl.BlockSpec((1, CCW), lambda b, cb: (0, cb))],
        out_specs=pl.BlockSpec((S, CCW), lambda b, cb: (b, cb)),
        out_shape=jax.ShapeDtypeStruct((T, D), f32),
        scratch_shapes=[pltpu.VMEM((S + HALO, CCW), f32)],
        compiler_params=_cparams(("parallel", "parallel")))(z8, z8, conv_w, conv_b)


def _conv_bwd(dc, z8, conv_w, dz8, S):
    T = dc.shape[0]
    nb = T // S
    ncb = D // CCW

    def body(dc_ref, av_ref, ag_ref, w_ref, dz_in, dz_ref, dw_ref, apad, dpad, shbuf):
        del dz_in
        apad[0:HALO, :] = jnp.zeros((HALO, CCW), f32)
        dpad[S:S + HALO, :] = jnp.zeros((HALO, CCW), f32)
        dw_ref[...] = jnp.zeros_like(dw_ref)

        def fill(i, carry):
            r0 = pl.multiple_of(i * 256, 256)
            apad[pl.ds(HALO + r0, 256), :] = av_ref[pl.ds(r0, 256), :] * _sig(ag_ref[pl.ds(r0, 256), :])
            dpad[pl.ds(r0, 256), :] = dc_ref[pl.ds(r0, 256), :]
            return carry

        lax.fori_loop(0, S // 256, fill, 0)

        def chunk(i, carry):
            r0 = pl.multiple_of(i * CR, CR)
            dwin = dpad[pl.ds(r0, CR + HALO), :]
            da = jnp.zeros((CR, CCW), f32)
            for s in range(8):
                shbuf[...] = dwin[s:s + CR, :]
                dshift = shbuf[...]
                part = None
                for m in range((CONV_WIDTH - 1 - s) // 8 + 1):
                    j = CONV_WIDTH - 1 - 8 * m - s
                    term = dwin[8 * m:8 * m + CR + 8, :] * w_ref[j:j + 1, :]
                    part = term if part is None else part + term
                    a_lag = apad[pl.ds(r0 + HALO - 8 * m, CR), :]
                    dw_ref[8 * j:8 * j + 8, :] += _colsum8(dshift * a_lag)
                da = da + part[s:s + CR, :]
            dw_ref[8 * CONV_WIDTH:8 * CONV_WIDTH + 8, :] += _colsum8(dwin[0:CR, :])
            av = av_ref[pl.ds(r0, CR), :]
            sg = _sig(ag_ref[pl.ds(r0, CR), :])
            dz_ref[0, pl.ds(r0, CR), :] = (da * sg).astype(bf16)
            dz_ref[1, pl.ds(r0, CR), :] = (da * av * sg * (1.0 - sg)).astype(bf16)
            return carry

        lax.fori_loop(0, S // CR, chunk, 0)

    zs = lambda s: pl.BlockSpec((None, S, CCW), lambda b, cb: (s, b, cb))
    return pl.pallas_call(
        body, name="conv_bwd", grid=(nb, ncb),
        in_specs=[pl.BlockSpec((S, CCW), lambda b, cb: (b, cb)), zs(Z_AVAL), zs(Z_AGATE),
                  pl.BlockSpec((CONV_WIDTH, CCW), lambda b, cb: (0, cb)), pl.BlockSpec(memory_space=pl.ANY)],
        out_specs=[pl.BlockSpec((2, S, CCW), lambda b, cb: (0, b, cb)),
                   pl.BlockSpec((None, 256, CCW), lambda b, cb: (b, 0, cb))],
        out_shape=[jax.ShapeDtypeStruct(dz8.shape, bf16), jax.ShapeDtypeStruct((nb, 256, D), f32)],
        input_output_aliases={4: 0},
        scratch_shapes=[pltpu.VMEM((S + HALO, CCW), f32), pltpu.VMEM((S + HALO, CCW), f32),
                        pltpu.VMEM((CR, CCW), f32)],
        compiler_params=_cparams(("parallel", "parallel")))(dc, z8, z8, conv_w, dz8)


FR = 128
NFB = D_FF // CCW


def _ffn_window(ref, i, r0):
    return ref[pl.ds(r0 - 8, FR + 8), :]


def _ffn_u(win, w_ref, b_ref):
    return (win[6:6 + FR, :] * w_ref[0:1, :] + win[7:7 + FR, :] * w_ref[1:2, :]
            + win[8:8 + FR, :] * w_ref[2:3, :] + b_ref[...])


def _ffn_fwd(u3, ffn_w, ffn_b, S):
    T = u3.shape[1]
    nb = T // S

    def body(uv_ref, ug_ref, wv_ref, wg_ref, bv_ref, bg_ref, f_ref):
        def chunk(first, i):
            r0 = 0 if first else pl.multiple_of(i * FR, FR)
            if first:
                z = jnp.zeros((8, CCW), f32)
                wv = jnp.concatenate([z, uv_ref[0:FR, :]], axis=0)
                wg = jnp.concatenate([z, ug_ref[0:FR, :]], axis=0)
            else:
                wv = _ffn_window(uv_ref, i, r0)
                wg = _ffn_window(ug_ref, i, r0)
            u_val = _ffn_u(wv, wv_ref, bv_ref)
            u_gate = _ffn_u(wg, wg_ref, bg_ref)
            f_ref[pl.ds(r0, FR), :] = (u_gate * _sig(u_gate) * u_val).astype(bf16)

        chunk(True, 0)

        def loop(i, carry):
            chunk(False, i)
            return carry

        lax.fori_loop(1, S // FR, loop, 0)

    us = lambda h: pl.BlockSpec((None, S, CCW), lambda b, cb: (h, b, cb))
    ws = lambda h: pl.BlockSpec((3, CCW), lambda b, cb: (0, h * NFB + cb))
    bs = lambda h: pl.BlockSpec((1, CCW), lambda b, cb: (0, h * NFB + cb))
    return pl.pallas_call(
        body, name="ffn_fwd", grid=(nb, NFB),
        in_specs=[us(0), us(1), ws(0), ws(1), bs(0), bs(1)],
        out_specs=pl.BlockSpec((S, CCW), lambda b, cb: (b, cb)),
        out_shape=jax.ShapeDtypeStruct((T, D_FF), bf16),
        compiler_params=_cparams(("parallel", "parallel")))(u3, u3, ffn_w, ffn_w, ffn_b, ffn_b)


def _ffn_bwd(u3, df, ffn_w, ffn_b, S):
    T = u3.shape[1]
    nb = T // S

    def body(uv_ref, ug_ref, df_ref, wv_ref, wg_ref, bv_ref, bg_ref, du_ref, dw_ref, dvpad, dgpad, shbuf):
        dvpad[S:S + 8, :] = jnp.zeros((8, CCW), f32)
        dgpad[S:S + 8, :] = jnp.zeros((8, CCW), f32)
        dw_ref[...] = jnp.zeros_like(dw_ref)

        def chunk(first, i):
            r0 = 0 if first else pl.multiple_of(i * FR, FR)
            if first:
                z = jnp.zeros((8, CCW), f32)
                wv = jnp.concatenate([z, uv_ref[0:FR, :]], axis=0)
                wg = jnp.concatenate([z, ug_ref[0:FR, :]], axis=0)
            else:
                wv = _ffn_window(uv_ref, i, r0)
                wg = _ffn_window(ug_ref, i, r0)
            taps = []
            for h, win in enumerate((wv, wg)):
                shbuf[2 * h] = win[6:6 + FR, :]
                shbuf[2 * h + 1] = win[7:7 + FR, :]
                taps.append((shbuf[2 * h], shbuf[2 * h + 1], win[8:8 + FR, :]))
            conv = lambda x, w_ref, b_ref: (x[0] * w_ref[0:1, :] + x[1] * w_ref[1:2, :] + x[2] * w_ref[2:3, :]
                                            + b_ref[...])
            u_val = conv(taps[0], wv_ref, bv_ref)
            u_gate = conv(taps[1], wg_ref, bg_ref)
            dfc = df_ref[pl.ds(r0, FR), :]
            sg = _sig(u_gate)
            d_val = dfc * u_gate * sg
            d_gate = dfc * u_val * sg * (1.0 + u_gate * (1.0 - sg))
            dvpad[pl.ds(r0, FR), :] = d_val
            dgpad[pl.ds(r0, FR), :] = d_gate
            for h, dd in enumerate((d_val, d_gate)):
                for j in range(3):
                    dw_ref[h, 8 * j:8 * j + 8, :] += _colsum8(dd * taps[h][j])
                dw_ref[h, 24:32, :] += _colsum8(dd)

        chunk(True, 0)

        def loop(i, carry):
            chunk(False, i)
            return carry

        lax.fori_loop(1, S // FR, loop, 0)

        def back(i, carry):
            r0 = pl.multiple_of(i * FR, FR)
            for h, (dpad, w_ref) in enumerate(((dvpad, wv_ref), (dgpad, wg_ref))):
                win = dpad[pl.ds(r0, FR + 8), :]
                du = (win[0:FR, :] * w_ref[2:3, :] + win[1:1 + FR, :] * w_ref[1:2, :]
                      + win[2:2 + FR, :] * w_ref[0:1, :])
                du_ref[h, pl.ds(r0, FR), :] = du.astype(bf16)
            return carry

        lax.fori_loop(0, S // FR, back, 0)

    us = lambda h: pl.BlockSpec((None, S, CCW), lambda b, cb: (h, b, cb))
    ws = lambda h: pl.BlockSpec((3, CCW), lambda b, cb: (0, h * NFB + cb))
    bs = lambda h: pl.BlockSpec((1, CCW), lambda b, cb: (0, h * NFB + cb))
    return pl.pallas_call(
        body, name="ffn_bwd", grid=(nb, NFB),
        in_specs=[us(0), us(1), pl.BlockSpec((S, CCW), lambda b, cb: (b, cb)), ws(0), ws(1), bs(0), bs(1)],
        out_specs=[pl.BlockSpec((2, S, CCW), lambda b, cb: (0, b, cb)),
                   pl.BlockSpec((None, 2, 32, CCW), lambda b, cb: (b, 0, 0, cb))],
        out_shape=[jax.ShapeDtypeStruct((2, T, D_FF), bf16), jax.ShapeDtypeStruct((nb, 2, 32, D_FF), f32)],
        scratch_shapes=[pltpu.VMEM((S + 8, CCW), f32), pltpu.VMEM((S + 8, CCW), f32),
                        pltpu.VMEM((4, FR, CCW), f32)],
        compiler_params=_cparams(("parallel", "parallel")))(u3, u3, df, ffn_w, ffn_w, ffn_b, ffn_b)


AB = ATTN_BLOCK


def _attn_bias():
    slopes = (np.float32(2.0) ** (np.float32(-8.0) * np.arange(1, N_HEADS + 1, dtype=np.float32)
                                  / np.float32(N_HEADS))).astype(np.float32)
    steps = (np.arange(AB)[:, None] + AB) - np.arange(2 * AB)[None, :]
    own = (np.arange(2 * AB) >= AB)[None, :]
    out = []
    for window, dil in GROUPS:
        valid = (steps >= 0) & (steps <= window // dil)
        dist = slopes[:, None, None] * (steps * dil).astype(np.float32)[None]
        kinds = [np.where(v[None], dist, np.float32(MASK_BIAS)) for v in (valid, valid & own)]
        out.append(np.stack(kinds, axis=1))
    return jnp.asarray(np.stack(out).astype(np.float32))


def _head_masks():
    lane = lax.broadcasted_iota(jnp.int32, (1, 128), 1)
    return (lane < HEAD_DIM, lane >= HEAD_DIM)


def _perm_chunks(S, d):
    L = S // d
    ch = min(L, 256)
    out = []
    for r in range(d):
        for c in range(L // ch):
            start = r + d * ch * c
            out.append((pl.ds(start, ch, stride=d) if d > 1 else pl.ds(start, ch), r * L + c * ch, ch))
    return out


def _stack_heads(x, masks):
    return jnp.concatenate([jnp.where(masks[0], x, 0), jnp.where(masks[1], x, 0)], axis=0)


_NT = (((1,), (1,)), ((), ()))
_TN = (((0,), (0,)), ((), ()))
SCH = 32


def _attn_fwd(qn, kn, z8, bias, S):
    T = qn.shape[0]
    nb = T // S
    nblk = S // AB

    def body(q_ref, k_ref, v_ref, bias_ref, o_ref, ob_ref, lse_ref, qs, ks, vs, s2, p2, ogp, lgp, *group_scratch):
        og, lg = group_scratch[:3], group_scratch[3:]
        masks = _head_masks()
        ks[0:AB, :] = jnp.zeros((AB, 128), bf16)
        vs[0:AB, :] = jnp.zeros((AB, 128), bf16)

        for g, (_, d) in enumerate(GROUPS):
            nsub = S // (d * AB)
            chunks = _perm_chunks(S, d)
            for src, dst, ch in chunks:
                qs[dst:dst + ch, :] = q_ref[src, :].astype(bf16)
                ks[AB + dst:AB + dst + ch, :] = k_ref[src, :].astype(bf16)
                vs[AB + dst:AB + dst + ch, :] = v_ref[src, :].astype(bf16)
            od, ld = (og[g], lg[g]) if d == 1 else (ogp, lgp)

            def scores(j, carry):
                r0 = pl.multiple_of(j * AB, AB)
                q2 = _stack_heads(qs[pl.ds(r0, AB), :], masks)
                s2[j] = lax.dot_general(q2, ks[pl.ds(r0, 2 * AB), :], _NT, preferred_element_type=f32)
                return carry

            lax.fori_loop(0, nblk, scores, 0, unroll=8)

            def softmax(j, carry, g=g, nsub=nsub, ld=ld):
                r0 = pl.multiple_of(j * AB, AB)
                kind = (j % nsub == 0).astype(jnp.int32)
                for cc in range(AB // SCH):
                    lses = []
                    for hh in range(2):
                        rows = pl.ds(hh * AB + cc * SCH, SCH)
                        sb = s2[j, rows, :] - bias_ref[g, hh, kind, cc * SCH:(cc + 1) * SCH, :]
                        m = jnp.max(sb, axis=-1, keepdims=True)
                        p = jnp.exp(sb - m)
                        den = jnp.sum(p, axis=-1, keepdims=True)
                        p2[j, rows, :] = (p * (1.0 / den)).astype(bf16)
                        lses.append(m + jnp.log(den))
                    ld[pl.ds(r0 + cc * SCH, SCH), :] = jnp.where(masks[0], lses[0], lses[1])
                return carry

            lax.fori_loop(0, nblk, softmax, 0, unroll=2)

            def values(j, carry, od=od):
                r0 = pl.multiple_of(j * AB, AB)
                pv2 = jnp.dot(p2[j], vs[pl.ds(r0, 2 * AB), :], preferred_element_type=f32)
                od[pl.ds(r0, AB), :] = jnp.where(masks[0], pv2[:AB], pv2[AB:])
                return carry

            lax.fori_loop(0, nblk, values, 0, unroll=8)

            if d > 1:
                for src, dst, ch in chunks:
                    og[g][src, :] = ogp[dst:dst + ch, :]
                    lg[g][src, :] = lgp[dst:dst + ch, :]

        def combine(i, carry):
            rr = pl.ds(pl.multiple_of(i * 256, 256), 256)
            l0, l1, l2 = lg[0][rr, :], lg[1][rr, :], lg[2][rr, :]
            mx = jnp.maximum(jnp.maximum(l0, l1), l2)
            e0, e1, e2 = jnp.exp(l0 - mx), jnp.exp(l1 - mx), jnp.exp(l2 - mx)
            den = e0 + e1 + e2
            o = (e0 * og[0][rr, :] + e1 * og[1][rr, :] + e2 * og[2][rr, :]) / den
            o_ref[rr, :] = o
            ob_ref[rr, :] = o.astype(bf16)
            lse_ref[rr, :] = mx + jnp.log(den)
            return carry

        lax.fori_loop(0, S // 256, combine, 0)

    blk = pl.BlockSpec((S, 128), lambda b, hp: (b, hp))
    return pl.pallas_call(
        body, name="attn_fwd", grid=(nb, N_HEADS // 2),
        in_specs=[blk, blk, pl.BlockSpec((None, S, 128), lambda b, hp: (Z_V, b, hp)),
                  pl.BlockSpec((3, 2, 2, AB, 2 * AB), lambda b, hp: (0, hp, 0, 0, 0))],
        out_specs=[blk, blk, blk],
        out_shape=[jax.ShapeDtypeStruct((T, D), f32), jax.ShapeDtypeStruct((T, D), bf16),
                   jax.ShapeDtypeStruct((T, D), f32)],
        scratch_shapes=[pltpu.VMEM((S, 128), bf16), pltpu.VMEM((S + AB, 128), bf16), pltpu.VMEM((S + AB, 128), bf16),
                        pltpu.VMEM((nblk, 2 * AB, 2 * AB), f32), pltpu.VMEM((nblk, 2 * AB, 2 * AB), bf16),
                        pltpu.VMEM((S, 128), f32), pltpu.VMEM((S, 128), f32)] + [pltpu.VMEM((S, 128), f32)] * 6,
        compiler_params=_cparams(("parallel", "parallel")))(qn, kn, z8, bias)


def _attn_bwd(qn, kn, z8, do, o, lse, bias, bd, S):
    T = qn.shape[0]
    nb = T // S

    nblk = S // AB

    def body(q_ref, k_ref, v_ref, do_ref, o_ref, lse_ref, bias_ref, bd_ref, dq_ref, dk_ref, dv_ref,
             delta, qs, ks, vs, dos, lsp, dlp, s2, dp2, p2, ds2, dqp, dkp, dvp):
        masks = _head_masks()
        bdv = bd_ref[...]
        dq_ref[...] = jnp.zeros_like(dq_ref)
        dk_ref[...] = jnp.zeros_like(dk_ref)
        dv_ref[...] = jnp.zeros_like(dv_ref)
        ks[0:AB, :] = jnp.zeros((AB, 128), bf16)
        vs[0:AB, :] = jnp.zeros((AB, 128), bf16)

        def prep(i, carry):
            rr = pl.ds(pl.multiple_of(i * 256, 256), 256)
            delta[rr, :] = _head_sum(do_ref[rr, :] * o_ref[rr, :], bdv)
            return carry

        lax.fori_loop(0, S // 256, prep, 0)

        for g, (_, d) in enumerate(GROUPS):
            nsub = S // (d * AB)
            chunks = _perm_chunks(S, d)
            for src, dst, ch in chunks:
                qs[dst:dst + ch, :] = q_ref[src, :].astype(bf16)
                ks[AB + dst:AB + dst + ch, :] = k_ref[src, :].astype(bf16)
                vs[AB + dst:AB + dst + ch, :] = v_ref[src, :].astype(bf16)
                dos[dst:dst + ch, :] = do_ref[src, :].astype(bf16)
                lsp[dst:dst + ch, :] = lse_ref[src, :]
                dlp[dst:dst + ch, :] = delta[src, :]
            dkp[...] = jnp.zeros_like(dkp)
            dvp[...] = jnp.zeros_like(dvp)

            def scores(j, carry):
                r0 = pl.multiple_of(j * AB, AB)
                q2 = _stack_heads(qs[pl.ds(r0, AB), :], masks)
                do2 = _stack_heads(dos[pl.ds(r0, AB), :], masks)
                s2[j] = lax.dot_general(q2, ks[pl.ds(r0, 2 * AB), :], _NT, preferred_element_type=f32)
                dp2[j] = lax.dot_general(do2, vs[pl.ds(r0, 2 * AB), :], _NT, preferred_element_type=f32)
                return carry

            lax.fori_loop(0, nblk, scores, 0, unroll=8)

            def probs(j, carry, g=g, nsub=nsub):
                r0 = pl.multiple_of(j * AB, AB)
                kind = (j % nsub == 0).astype(jnp.int32)
                for cc in range(AB // SCH):
                    lse_c = lsp[pl.ds(r0 + cc * SCH, SCH), :]
                    del_c = dlp[pl.ds(r0 + cc * SCH, SCH), :]
                    for hh in range(2):
                        c0 = hh * HEAD_DIM
                        rows = pl.ds(hh * AB + cc * SCH, SCH)
                        sb = s2[j, rows, :] - bias_ref[g, hh, kind, cc * SCH:(cc + 1) * SCH, :]
                        p = jnp.exp(sb - lse_c[:, c0:c0 + 1])
                        p2[j, rows, :] = p.astype(bf16)
                        ds2[j, rows, :] = (p * (dp2[j, rows, :] - del_c[:, c0:c0 + 1])).astype(bf16)
                return carry

            lax.fori_loop(0, nblk, probs, 0, unroll=2)

            def grads(j, carry):
                r0 = pl.multiple_of(j * AB, AB)
                q2 = _stack_heads(qs[pl.ds(r0, AB), :], masks)
                do2 = _stack_heads(dos[pl.ds(r0, AB), :], masks)
                dsb = ds2[j]
                t = jnp.dot(dsb, ks[pl.ds(r0, 2 * AB), :], preferred_element_type=f32)
                dqp[pl.ds(r0, AB), :] = jnp.where(masks[0], t[:AB], t[AB:])
                dkp[pl.ds(r0, 2 * AB), :] += lax.dot_general(dsb, q2, _TN, preferred_element_type=f32)
                dvp[pl.ds(r0, 2 * AB), :] += lax.dot_general(p2[j], do2, _TN, preferred_element_type=f32)
                return carry

            lax.fori_loop(0, nblk, grads, 0, unroll=4)

            for src, dst, ch in chunks:
                dq_ref[src, :] += dqp[dst:dst + ch, :]
                dk_ref[src, :] += dkp[AB + dst:AB + dst + ch, :]
                dv_ref[src, :] += dvp[AB + dst:AB + dst + ch, :]

    blk = pl.BlockSpec((S, 128), lambda b, hp: (b, hp))
    row = lambda dt, pad=0: pltpu.VMEM((S + pad, 128), dt)
    blocks = lambda dt: pltpu.VMEM((nblk, 2 * AB, 2 * AB), dt)
    return pl.pallas_call(
        body, name="attn_bwd", grid=(nb, N_HEADS // 2),
        in_specs=[blk, blk, pl.BlockSpec((None, S, 128), lambda b, hp: (Z_V, b, hp)), blk, blk, blk,
                  pl.BlockSpec((3, 2, 2, AB, 2 * AB), lambda b, hp: (0, hp, 0, 0, 0)),
                  pl.BlockSpec((128, 128), lambda b, hp: (0, 0))],
        out_specs=[blk, blk, blk],
        out_shape=[jax.ShapeDtypeStruct((T, D), f32)] * 3,
        scratch_shapes=[row(f32), row(bf16), row(bf16, AB), row(bf16, AB), row(bf16), row(f32), row(f32),
                        blocks(f32), blocks(f32), blocks(bf16), blocks(bf16), row(f32), row(f32, AB), row(f32, AB)],
        compiler_params=_cparams(("parallel", "parallel")))(qn, kn, z8, do, o, lse, bias, bd)


def _any_spec():
    return pl.BlockSpec(memory_space=pl.ANY)


def _allgather_rows(shards, n_full):
    n = len(shards)

    def body(*refs):
        ins, outs = refs[:n], refs[n:2 * n]
        send_sems, recv_sems, local_sems = refs[2 * n:]
        x, y, c, me = _my_pos()
        sibling = (x, y, 1 - c)
        chips = [(1 - x, y), (x, 1 - y), (1 - x, 1 - y)]

        def idx(px, py, pc):
            return 4 * px + 2 * py + pc

        def copy(a, k, blk, to, src=None):
            return pltpu.make_async_remote_copy(
                src_ref=outs[a].at[blk] if src is None else src, dst_ref=outs[a].at[blk],
                send_sem=send_sems.at[a, k], recv_sem=recv_sems.at[a, k], device_id=to, device_id_type=MESH)

        mine = [pltpu.make_async_copy(ins[a], outs[a].at[me], local_sems.at[a]) for a in range(n)]
        for cp in mine:
            cp.start()
        first = []
        for a in range(n_full):
            first.append(copy(a, 0, me, sibling, src=ins[a]))
            first += [copy(a, 1 + j, me, (*chip, c), src=ins[a]) for j, chip in enumerate(chips)]
        for cp in first:
            cp.start()
        passed = []
        for a in range(n_full):
            for j, chip in enumerate(chips):
                blk = idx(*chip, c)
                copy(a, 1 + j, blk, (x, y, c)).wait_recv()
                cp = copy(a, 4 + j, blk, sibling)
                cp.start()
                passed.append(cp)
        for a in range(n_full):
            copy(a, 0, idx(x, y, 1 - c), (x, y, c)).wait_recv()
            for j, chip in enumerate(chips):
                copy(a, 4 + j, idx(*chip, 1 - c), (x, y, c)).wait_recv()
        for cp in first + passed:
            cp.wait_send()
        for cp in mine:
            cp.wait()

    return pl.pallas_call(
        body, name="allgather_weights",
        in_specs=[_any_spec()] * n, out_specs=[_any_spec()] * n,
        out_shape=[jax.ShapeDtypeStruct((N_DEV,) + s.shape, s.dtype) for s in shards],
        scratch_shapes=[pltpu.SemaphoreType.DMA((n_full, 7)), pltpu.SemaphoreType.DMA((n_full, 7)),
                        pltpu.SemaphoreType.DMA((n,))],
    )(*shards)


def _peer(x, y, c, k):
    tx = 1 - x if (k >> 2) & 1 else x
    ty = 1 - y if (k >> 1) & 1 else y
    tc = 1 - c if k & 1 else c
    return (tx, ty, tc), 4 * tx + 2 * ty + tc


_PEER_ORDER = (2, 4, 6, 3, 5, 7, 1)


_HBM = pl.BlockSpec(memory_space=pltpu.HBM)
_SEM = pl.BlockSpec(memory_space=pltpu.SEMAPHORE)
_EFFECT = pltpu.SideEffectType.DATAFLOW_SIDE_EFFECTING


def _exchange_copies(srcs, lands, send_sems, recv_sems, gather):
    x, y, c, me = _my_pos()
    copies = []
    for k in _PEER_ORDER:
        tgt, tidx = _peer(x, y, c, k)
        for a in range(len(srcs)):
            copies.append(pltpu.make_async_remote_copy(
                src_ref=srcs[a] if gather else srcs[a].at[tidx], dst_ref=lands[a].at[me],
                send_sem=send_sems.at[7 * a + k - 1], recv_sem=recv_sems.at[7 * a + k - 1],
                device_id=tgt, device_id_type=MESH))
    return copies


def _exchange_start(name, srcs, lands=None, after=None):
    n = len(srcs)
    gather = lands is not None
    if lands is None:
        lands = [lax.empty(g.shape, g.dtype) for g in srcs]
    extra = [] if after is None else [after]

    def body(*refs):
        src_refs, land_refs = refs[:n], refs[n:2 * n]
        send_sems, recv_sems = refs[2 * n + len(extra)], refs[2 * n + len(extra) + 1]
        token = refs[-1]
        for cp in _exchange_copies(src_refs, land_refs, send_sems, recv_sems, gather):
            cp.start()
        token[...] = jnp.zeros_like(token)

    hbm = lambda a: pltpu.with_memory_space_constraint(a, pltpu.HBM)
    outs = pl.pallas_call(
        body, name=name,
        out_shape=(pltpu.SemaphoreType.DMA((7 * n,)), pltpu.SemaphoreType.DMA((7 * n,)),
                   *[pltpu.HBM(g.shape, g.dtype) for g in list(srcs) + list(lands)],
                   jax.ShapeDtypeStruct((8, 128), f32)),
        in_specs=[_HBM] * (2 * n) + [pl.BlockSpec(memory_space=pl.ANY)] * len(extra),
        out_specs=(_SEM, _SEM, *([_HBM] * (2 * n)), pl.BlockSpec(memory_space=pltpu.VMEM)),
        input_output_aliases={i: 2 + i for i in range(2 * n)},
        compiler_params=pltpu.CompilerParams(has_side_effects=_EFFECT),
    )(*[hbm(g) for g in srcs], *[hbm(g) for g in lands], *extra)
    return outs[0], outs[1], list(outs[2:2 + n]), list(outs[2 + n:2 + 2 * n]), outs[-1], gather


def _exchange_wait(name, started, after):
    send_sems, recv_sems, srcs, lands, _, gather = started
    n = len(srcs)
    after = list(after) if isinstance(after, (list, tuple)) else [after]

    def body(*refs):
        src_refs, land_refs = refs[:n], refs[n:2 * n]
        s_sems, r_sems = refs[2 * n], refs[2 * n + 1]
        for cp in _exchange_copies(src_refs, land_refs, s_sems, r_sems, gather):
            cp.wait_send()
            cp.wait_recv()

    outs = pl.pallas_call(
        body, name=name,
        out_shape=tuple(pltpu.HBM(a.shape, a.dtype) for a in list(srcs) + list(lands)),
        in_specs=[_HBM] * (2 * n) + [_SEM, _SEM] + [pl.BlockSpec(memory_space=pl.ANY)] * len(after),
        out_specs=tuple([_HBM] * (2 * n)),
        input_output_aliases={i: i for i in range(2 * n)},
        compiler_params=pltpu.CompilerParams(has_side_effects=_EFFECT),
    )(*srcs, *lands, send_sems, recv_sems, *after)
    return list(outs[:n]), list(outs[n:])


SMALL_ROWS = 128


def _small_start(name, sg, after=None):
    return _exchange_start(name, [sg], [lax.empty((N_DEV,) + sg.shape, f32)], after=after)


def _small_sum(name, me, started, after):
    (own,), (slots,) = _exchange_wait(name + "_wait", started, after)

    def body(me_ref, s_ref, own_ref, out_ref):
        acc = None
        for p in range(N_DEV):
            term = lax.cond(me_ref[0] == p, lambda: own_ref[...], lambda p=p: s_ref[p])
            acc = term if acc is None else acc + term
        out_ref[...] = acc

    return pl.pallas_call(
        body, name=name + "_sum",
        in_specs=[pl.BlockSpec(memory_space=pltpu.SMEM), pl.BlockSpec(memory_space=pltpu.VMEM),
                  pl.BlockSpec(memory_space=pltpu.VMEM)],
        out_specs=pl.BlockSpec(memory_space=pltpu.VMEM),
        out_shape=jax.ShapeDtypeStruct(own.shape, f32))(me, slots, own)


def _adam_math(g, w, m, v):
    m = ADAM_B1 * m + (1.0 - ADAM_B1) * g
    v = ADAM_B2 * v + (1.0 - ADAM_B2) * (g * g)
    m_hat = m / (1.0 - ADAM_B1 ** ADAM_STEP)
    v_hat = v / (1.0 - ADAM_B2 ** ADAM_STEP)
    delta = -ADAM_LR * (m_hat / (jnp.sqrt(v_hat) + ADAM_EPS) + ADAM_WD * w)
    return delta, m, v


def _adam_slots(name, me, slots, own, w, m, v, tr, transposed=False):
    rows = slots.shape[1]

    def body(me_ref, s_ref, own_ref, w_ref, m_ref, v_ref, g_ref, d_ref, nm_ref, nv_ref):
        mine = own_ref[...]
        g = None
        for p in range(N_DEV):
            term = lax.cond(me_ref[0] == p, lambda: mine, lambda p=p: s_ref[p]).astype(f32)
            g = term if g is None else g + term
        if transposed:
            g = g.T
        delta, nm, nv = _adam_math(g, w_ref[...], m_ref[...], v_ref[...])
        g_ref[...] = g
        d_ref[...] = delta
        nm_ref[...] = nm
        nv_ref[...] = nv

    if transposed:
        rs = pl.BlockSpec((D, tr), lambda i, me_ref: (0, i))
    else:
        rs = pl.BlockSpec((tr, D), lambda i, me_ref: (i, 0))
    return pl.pallas_call(
        body, name=name,
        grid_spec=pltpu.PrefetchScalarGridSpec(
            num_scalar_prefetch=1, grid=(rows // tr,),
            in_specs=[pl.BlockSpec((N_DEV, tr, D), lambda i, me_ref: (0, i, 0)),
                      pl.BlockSpec((None, tr, D), lambda i, me_ref: (me_ref[0], i, 0)), rs, rs, rs],
            out_specs=[rs] * 4),
        out_shape=[jax.ShapeDtypeStruct(w.shape, f32)] * 4,
        compiler_params=_cparams(("parallel",)))(me, slots, own, w, m, v)


def _adam_small(g, w, m, v):
    def body(g_ref, w_ref, m_ref, v_ref, d_ref, nm_ref, nv_ref):
        delta, nm, nv = _adam_math(g_ref[...], w_ref[...], m_ref[...], v_ref[...])
        d_ref[...] = delta
        nm_ref[...] = nm
        nv_ref[...] = nv

    return pl.pallas_call(body, name="adam_small", out_shape=[jax.ShapeDtypeStruct(g.shape, f32)] * 3)(g, w, m, v)


FFN_PAD = 6 * D


_SMALL_PARTS = (("norm1_g", 1), ("gate_b", 2), ("conv_w", CONV_WIDTH), ("conv_b", 1), ("conv_norm_g", 1),
                ("q_norm_g", 1), ("k_norm_g", 1), ("norm2_g", 1), ("ffn_conv_w", 18), ("ffn_conv_b", 6), ("last", 1))


def _small_offsets():
    out, row = {}, 0
    for name, rows in _SMALL_PARTS:
        out[name] = row
        row += -(-rows // 8) * 8
    assert row == SMALL_ROWS
    return out


def _pack_small(norm1_g, gate_b, conv_w, conv_b, conv_norm_g, q_norm_g, k_norm_g, norm2_g, ffn_conv_w, ffn_conv_b,
                last_row=None):
    pad_h = lambda a: jnp.pad(a, ((0, 0), (0, D - HEAD_DIM)))
    pad_f = lambda a: jnp.pad(a, ((0, 0), (0, FFN_PAD - 2 * D_FF))).reshape(-1, D)
    parts = [norm1_g, gate_b.reshape(2, D), conv_w, conv_b, conv_norm_g, pad_h(q_norm_g), pad_h(k_norm_g), norm2_g,
             pad_f(ffn_conv_w), pad_f(ffn_conv_b), jnp.zeros((1, D), f32) if last_row is None else last_row]
    return jnp.concatenate([jnp.pad(p, ((0, -p.shape[0] % 8), (0, 0))) for p in parts], axis=0)


def _unpack_small(p):
    o = _small_offsets()
    rows = lambda name, n: p[o[name]:o[name] + n]
    ffn = lambda a: a.reshape(-1, FFN_PAD)[:, :2 * D_FF]
    return dict(
        norm1_g=rows("norm1_g", 1), gate_b=rows("gate_b", 2).reshape(1, 2 * D), conv_w=rows("conv_w", CONV_WIDTH),
        conv_b=rows("conv_b", 1), conv_norm_g=rows("conv_norm_g", 1), q_norm_g=rows("q_norm_g", 1)[:, :HEAD_DIM],
        k_norm_g=rows("k_norm_g", 1)[:, :HEAD_DIM], norm2_g=rows("norm2_g", 1),
        ffn_conv_w=ffn(rows("ffn_conv_w", 18)), ffn_conv_b=ffn(rows("ffn_conv_b", 6)))


_ADAM_TILE = {896: 128, 704: 64, 128: 128, 352: 176}


def kernel(x, norm1_g, w_in, gate_b, conv_w, conv_b, conv_norm_g, w_conv_out, q_norm_g, k_norm_g, w_attn_out, w_out, norm2_g, w_up, ffn_conv_w, ffn_conv_b, w_down, loss_target, m_norm1_g, m_w_in, m_gate_b, m_conv_w, m_conv_b, m_conv_norm_g, m_w_conv_out, m_q_norm_g, m_k_norm_g, m_w_attn_out, m_w_out, m_norm2_g, m_w_up, m_ffn_conv_w, m_ffn_conv_b, m_w_down, v_norm1_g, v_w_in, v_gate_b, v_conv_w, v_conv_b, v_conv_norm_g, v_w_conv_out, v_q_norm_g, v_k_norm_g, v_w_attn_out, v_w_out, v_norm2_g, v_w_up, v_ffn_conv_w, v_ffn_conv_b, v_w_down):
    BL, S, _ = x.shape
    T = BL * S
    me = 4 * lax.axis_index("x") + 2 * lax.axis_index("y") + lax.axis_index("c")
    xt = x.reshape(T, D)
    target = loss_target.reshape(T, D)

    big = dict(w_in=(w_in[0], m_w_in[0], v_w_in[0]), w_up=(w_up[0].T, m_w_up[0].T, v_w_up[0].T),
               w_conv_out=(w_conv_out[0], m_w_conv_out[0], v_w_conv_out[0]),
               w_attn_out=(w_attn_out[0], m_w_attn_out[0], v_w_attn_out[0]),
               w_out=(w_out[0], m_w_out[0], v_w_out[0]), w_down=(w_down[0], m_w_down[0], v_w_down[0]))
    order = ["w_in", "w_conv_out", "w_attn_out", "w_out", "w_up", "w_down"]
    shards = [(big[n][0].T if n == "w_in" else big[n][0]).astype(bf16) for n in order]
    gathered = _allgather_rows(shards, 1)
    ga_proj = _exchange_start("gather_start_proj", shards[1:4], gathered[1:4], after=gathered[0])
    ga_ffn = _exchange_start("gather_start_ffn", shards[4:6], gathered[4:6], after=ga_proj[4])
    W = {"w_in": gathered[0].reshape(-1, D)}

    def place_cols(shard, full_cols):
        z = jnp.zeros((shard.shape[0], full_cols), f32)
        return lax.dynamic_update_slice(z, shard, (0, me * shard.shape[1]))

    zr = lambda a: jnp.zeros_like(a)
    conv_local = _pack_small(
        zr(norm1_g), zr(gate_b), place_cols(conv_w[0], D), zr(conv_b), zr(conv_norm_g), zr(q_norm_g), zr(k_norm_g),
        zr(norm2_g), place_cols(ffn_conv_w[0], 2 * D_FF), zr(ffn_conv_b))
    ga_conv = _small_start("gather_conv_start", conv_local, after=ga_ffn[4])

    bd = (jnp.arange(128)[:, None] // HEAD_DIM == jnp.arange(128)[None, :] // HEAD_DIM).astype(bf16)
    bias = _attn_bias()
    qg = jnp.tile(q_norm_g, (1, N_HEADS))
    kg = jnp.tile(k_norm_g, (1, N_HEADS))

    h = _norm1_fwd(xt, norm1_g)
    z8 = _matmul_call(
        "mm_z", h, W["w_in"],
        pl.BlockSpec((2048, D), lambda i, j, k: (i, 0)),
        pl.BlockSpec((1024, D), lambda i, j, k: (_wsec_of_zsec(j), 0)),
        pl.BlockSpec((None, 2048, D), lambda i, j, k: (j, i, 0)),
        jax.ShapeDtypeStruct((8, T, D), f32), (T // 2048, 7, 1), "nt", 1, 2048, 1024, after=ga_conv[4])
    conv_all = _unpack_small(_small_sum("gather_conv", me.reshape(1), ga_conv, z8))
    conv_w_full, ffn_w_full = conv_all["conv_w"], conv_all["ffn_conv_w"]
    c = _conv_fwd(z8, conv_w_full, conv_b, S)
    s = _convnorm_fwd(c, conv_norm_g)
    qn, kn = _qk_fwd(z8, qg, kg, bd)
    for n, g in zip(order[1:4], _exchange_wait("gather_wait_proj", ga_proj, qn)[1]):
        W[n] = g.reshape(-1, D)
    ya = _matmul("mm_ya", s, W["w_conv_out"], "nn", f32)
    o, ob, lse = _attn_fwd(qn, kn, z8, bias, S)
    yb = _matmul("mm_yb", ob, W["w_attn_out"], "nn", f32)
    mixed = _gate_fwd(z8, gate_b, ya, yb)
    t1 = _matmul("mm_t1", mixed, W["w_out"], "nn", f32)
    for n, g in zip(order[4:6], _exchange_wait("gather_wait_ffn", ga_ffn, t1)[1]):
        W[n] = g.reshape(-1, D)
    x1, h2 = _norm2_fwd(xt, t1, norm2_g)
    TNU = D_FF // 2
    u3 = _matmul_call(
        "mm_u", h2, W["w_up"],
        pl.BlockSpec((1024, D), lambda i, j, k: (i, 0)),
        pl.BlockSpec((TNU, D), lambda i, j, k: (j, 0)),
        pl.BlockSpec((None, 1024, TNU), lambda i, j, k: (j // 2, i, j % 2)),
        jax.ShapeDtypeStruct((2, T, D_FF), f32), (T // 1024, 4, 1), "nt", 1, 1024, TNU)
    f = _ffn_fwd(u3, ffn_w_full, ffn_conv_b, S)
    t2 = _matmul("mm_t2", f, W["w_down"], "nn", f32)
    dy, dyb, lacc = _loss_fwd(x1, t2, target)
    loss_local = 0.5 / D * jnp.sum(lacc)

    df = _matmul("mm_df", dyb, W["w_down"], "nt", f32, tn=TNU)
    g_w_down = _matmul("mm_dwdn", f, dyb, "tn", bf16, tm=TNU)
    du3, dffn = _ffn_bwd(u3, df, ffn_w_full, ffn_conv_b, S)
    g_w_up = _matmul_call(
        "mm_dwup", du3, h2,
        pl.BlockSpec((None, T, TNU), lambda i, j, k: (i // 2, 0, i % 2)),
        pl.BlockSpec((T, D), lambda i, j, k: (0, 0)),
        pl.BlockSpec((TNU, D), lambda i, j, k: (i, 0)),
        jax.ShapeDtypeStruct((2 * D_FF, D), bf16), (4, 1, 1), "tn", 1, TNU, D)
    blocks8 = lambda a: a.reshape(N_DEV, -1, D)
    ex_ffn = _exchange_start("scatter_start_ffn", [blocks8(g_w_up), blocks8(g_w_down)])
    dh2 = _matmul_sections("mm_dh2", du3, W["w_up"].reshape(2, D_FF, D), ((0, 0), (1, 1)), 512, after=ex_ffn[4])
    dx1, dx1b, dg_norm2 = _norm2_bwd(x1, dh2, dy, norm2_g)
    dmixed = _matmul("mm_dmixed", dx1b, W["w_out"], "nt", f32)
    g_w_out = _matmul("mm_dwo", mixed, dx1b, "tn", bf16, tm=512)
    dz8 = lax.empty((8, T, D), bf16)
    dya, dyb2, dz8, dg_gate = _gate_bwd(dmixed, z8, gate_b, ya, yb, dz8)
    ds = _matmul("mm_ds", dya, W["w_conv_out"], "nt", f32)
    g_w_conv_out = _matmul("mm_dwco", s, dya, "tn", bf16, tm=512)
    g_w_attn_out = _matmul("mm_dwao", ob, dyb2, "tn", bf16, tm=512)
    ex_proj = _exchange_start("scatter_start_proj", [blocks8(g_w_conv_out), blocks8(g_w_attn_out), blocks8(g_w_out)])
    do = _matmul("mm_do", dyb2, W["w_attn_out"], "nt", f32, after=ex_proj[4])
    dc, dg_convnorm = _convnorm_bwd(c, ds, conv_norm_g)
    dz8a, dconv = _conv_bwd(dc, z8, conv_w_full, dz8, S)
    dqn, dkn, dv = _attn_bwd(qn, kn, z8, do, o, lse, bias, bd, S)
    dz8b, dg_q, dg_k = _qk_bwd(z8, dqn, dkn, dv, qg, kg, bd, dz8a)
    g_w_in = _matmul_call(
        "mm_dwin", dz8b, h,
        pl.BlockSpec((None, T, D), lambda i, j, k: (_zsec_of_wsec(i), 0, 0)),
        pl.BlockSpec((T, D), lambda i, j, k: (0, 0)),
        pl.BlockSpec((1024, D), lambda i, j, k: (i, 0)),
        jax.ShapeDtypeStruct((7 * D, D), bf16), (7, 1, 1), "tn", 1, D, D)
    ex_in = _exchange_start("scatter_start_in", [blocks8(g_w_in)])
    dh = _matmul_sections("mm_dh", dz8b, W["w_in"].reshape(7, D, D), tuple(zip(range(7), _W_OF_Z)), 512,
                          after=ex_in[4])
    grad_x, dg_norm1 = _norm1_bwd(xt, dh, dx1, norm1_g)

    sum8 = lambda a: a.reshape(-1, 8, a.shape[-1]).sum(axis=1)
    dconv_s = sum8(dconv.sum(axis=0))
    dffn_s = dffn.sum(axis=0).reshape(2, 4, 8, D_FF).sum(axis=2)
    dffn_w = jnp.concatenate([dffn_s[0, :3], dffn_s[1, :3]], axis=1)
    dffn_b = jnp.concatenate([dffn_s[0, 3:4], dffn_s[1, 3:4]], axis=1)
    fold = lambda a: sum8(a).reshape(N_HEADS, HEAD_DIM).sum(axis=0)[None]
    small_g_local = _pack_small(
        sum8(dg_norm1), sum8(dg_gate), dconv_s[:CONV_WIDTH], dconv_s[CONV_WIDTH:], sum8(dg_convnorm),
        fold(dg_q), fold(dg_k), sum8(dg_norm2), dffn_w, dffn_b,
        last_row=jnp.pad(loss_local.reshape(1, 1), ((0, 0), (0, D - 1))))
    sg_start = _small_start("small_grads_start", small_g_local)

    own, slots = {}, {}
    for tag, ex, names_ in (("ffn", ex_ffn, ("w_up", "w_down")),
                            ("proj", ex_proj, ("w_conv_out", "w_attn_out", "w_out")), ("in", ex_in, ("w_in",))):
        sent, landed = _exchange_wait("scatter_wait_" + tag, ex, sg_start[4])
        for n, src, land in zip(names_, sent, landed):
            own[n], slots[n] = src, land

    res, adam_done = {}, []
    for n in order:
        w, m, v = big[n]
        outs = _adam_slots("adam_" + n, me.reshape(1), slots[n], own[n], w, m, v, _ADAM_TILE[slots[n].shape[1]],
                           transposed=(n == "w_in"))
        adam_done.append(outs[0])
        if n == "w_up":
            outs = [a.T for a in outs]
        res[n] = [a[None] for a in outs]
    small_g = _small_sum("small_grads", me.reshape(1), sg_start, adam_done)
    loss = small_g[_small_offsets()["last"], 0]

    col = lambda a, width: lax.dynamic_slice(a, (0, me * width), (a.shape[0], width))
    small_w_true = _pack_small(norm1_g, gate_b, conv_w_full, conv_b, conv_norm_g, q_norm_g, k_norm_g, norm2_g,
                               ffn_w_full, ffn_conv_b)
    place_m = lambda a, full: place_cols(a[0], full)
    small_m = _pack_small(m_norm1_g, m_gate_b, place_m(m_conv_w, D), m_conv_b, m_conv_norm_g, m_q_norm_g, m_k_norm_g,
                          m_norm2_g, place_m(m_ffn_conv_w, 2 * D_FF), m_ffn_conv_b)
    small_v = _pack_small(v_norm1_g, v_gate_b, place_m(v_conv_w, D), v_conv_b, v_conv_norm_g, v_q_norm_g, v_k_norm_g,
                          v_norm2_g, place_m(v_ffn_conv_w, 2 * D_FF), v_ffn_conv_b)
    sd, sm, sv = _adam_small(small_g, small_w_true, small_m, small_v)
    for i, packed in enumerate((small_g, sd, sm, sv)):
        u = _unpack_small(packed)
        u["conv_w"] = col(u["conv_w"], D // N_DEV)
        u["ffn_conv_w"] = col(u["ffn_conv_w"], 2 * D_FF // N_DEV)
        for n, a in u.items():
            res.setdefault(n, [None] * 4)[i] = a[None] if n in ("conv_w", "ffn_conv_w") else a

    names = ["norm1_g", "w_in", "gate_b", "conv_w", "conv_b", "conv_norm_g", "w_conv_out", "q_norm_g", "k_norm_g",
             "w_attn_out", "w_out", "norm2_g", "w_up", "ffn_conv_w", "ffn_conv_b", "w_down"]
    out = [loss, grad_x.reshape(BL, S, D)]
    for i in range(4):
        out += [res[n][i] for n in names]
    return tuple(out)
```

```python
import functools

import jax
import jax.numpy as jnp
import numpy as np
from jax import lax
from jax.experimental import pallas as pl
from jax.experimental.pallas import tpu as pltpu

f32 = jnp.float32
bf16 = jnp.bfloat16

D = 1024
N_HEADS = 16
HEAD_DIM = 64
CONV_WIDTH = 31
D_FF = 2816
GROUPS = ((128, 1), (512, 4), (2048, 16))
ATTN_BLOCK = 128
EPS = 1e-6
N_DEV = 8
MESH = pl.DeviceIdType.MESH

ADAM_LR = 0.001
ADAM_B1 = 0.9
ADAM_B2 = 0.999
ADAM_EPS = 1e-08
ADAM_WD = 0.01
ADAM_STEP = 10

VMEM_LIMIT = 56 * 1024 * 1024
MASK_BIAS = 1e30

Z_AVAL, Z_AGATE, Z_GA, Z_GB, Z_Q, Z_K, Z_V = 0, 1, 2, 3, 4, 5, 6


_W_OF_Z = (0, 1, 5, 6, 2, 3, 4)


def _wsec_of_zsec(j):
    return jnp.where(j < 2, j, jnp.where(j < 4, j + 3, j - 2))


def _zsec_of_wsec(w):
    return jnp.where(w < 2, w, jnp.where(w < 5, w + 2, w - 3))


def _sig(x):
    return 1.0 / (1.0 + jnp.exp(-x))


def _colsum8(x):
    return x.reshape(-1, 8, x.shape[-1]).sum(axis=0)


def _cparams(sem):
    return pltpu.CompilerParams(dimension_semantics=sem, vmem_limit_bytes=VMEM_LIMIT)


def _my_pos():
    x, y, c = lax.axis_index("x"), lax.axis_index("y"), lax.axis_index("c")
    return x, y, c, 4 * x + 2 * y + c


_DIMS = {"nn": ((1,), (0,)), "nt": ((1,), (1,)), "tn": ((0,), (0,))}


def _matmul_call(name, a, b, a_spec, b_spec, o_spec, out_shape, grid, mode, nk, tm, tn, after=None):
    dims = (_DIMS[mode], ((), ()))
    extra = [] if after is None else [after]

    def body(a_ref, b_ref, *rest):
        o_ref, scratch = rest[len(extra)], rest[len(extra) + 1:]
        part = lax.dot_general(a_ref[...], b_ref[...], dims, preferred_element_type=f32)
        if nk == 1:
            o_ref[...] = part.astype(o_ref.dtype)
        else:
            acc = scratch[0]
            k = pl.program_id(2)

            @pl.when(k == 0)
            def _():
                acc[...] = part

            @pl.when(k > 0)
            def _():
                acc[...] += part

            @pl.when(k == nk - 1)
            def _():
                o_ref[...] = acc[...].astype(o_ref.dtype)

    scratch = [] if nk == 1 else [pltpu.VMEM((tm, tn), f32)]
    return pl.pallas_call(
        body, name=name, grid=grid, in_specs=[a_spec, b_spec] + [pl.BlockSpec(memory_space=pl.ANY)] * len(extra),
        out_specs=o_spec, out_shape=out_shape,
        scratch_shapes=scratch, compiler_params=_cparams(("parallel", "parallel", "arbitrary")),
    )(a, b, *extra)


def _matmul(name, a, b, mode, out_dtype, tm=1024, tn=1024, tk=None, after=None):
    if mode == "nn":
        (M, K), (_, N) = a.shape, b.shape
    elif mode == "nt":
        (M, K), (N, _) = a.shape, b.shape
    else:
        (K, M), (_, N) = a.shape, b.shape
    tm, tn = min(tm, M), min(tn, N)
    tk = K if tk is None else tk
    nk = K // tk
    assert M % tm == 0 and N % tn == 0 and K % tk == 0
    if mode == "tn":
        a_spec = pl.BlockSpec((tk, tm), lambda i, j, k: (k, i))
    else:
        a_spec = pl.BlockSpec((tm, tk), lambda i, j, k: (i, k))
    if mode == "nt":
        b_spec = pl.BlockSpec((tn, tk), lambda i, j, k: (j, k))
    else:
        b_spec = pl.BlockSpec((tk, tn), lambda i, j, k: (k, j))
    o_spec = pl.BlockSpec((tm, tn), lambda i, j, k: (i, j))
    return _matmul_call(name, a, b, a_spec, b_spec, o_spec, jax.ShapeDtypeStruct((M, N), out_dtype),
                        (M // tm, N // tn, nk), mode, nk, tm, tn, after=after)


FTM = 512


def _matmul_fused(name, a, b, pairs, epilogue, extras, consts, outs, nt=False, sums=False, passed=(), aliases=None):
    sa, M, kk = a.shape
    na = max(i for i, _ in pairs) + 1
    ne, nc, npass = len(extras), len(consts), len(passed)
    dims = (_DIMS["nt" if nt else "nn"], ((), ()))

    def body(a_ref, b_ref, *rest):
        acc = None
        for i, j in pairs:
            part = lax.dot_general(a_ref[i], b_ref[j], dims, preferred_element_type=f32)
            acc = part if acc is None else acc + part
        epilogue(acc, rest[:ne], rest[ne:ne + nc], rest[ne + nc + npass:])

    whole = lambda arr: pl.BlockSpec(arr.shape, lambda i, nd=arr.ndim: (0,) * nd, pipeline_mode=pl.Buffered(1))
    io_alias = {2 + ne + nc + k: v for k, v in (aliases or {}).items()}
    return pl.pallas_call(
        body, name=name, grid=(M // FTM,),
        in_specs=[pl.BlockSpec((na, FTM, kk), lambda i: (0, i, 0)), whole(b)] + [s for _, s in extras]
        + [whole(c) for c in consts] + [pl.BlockSpec(memory_space=pl.ANY)] * npass,
        out_specs=[s for _, s in outs], out_shape=[s for s, _ in outs], input_output_aliases=io_alias,
        compiler_params=_cparams(("arbitrary" if sums else "parallel",)),
    )(a, b, *[x for x, _ in extras], *consts, *passed)


def _frows(c=D):
    return pl.BlockSpec((FTM, c), lambda i: (i, 0))


def _fsec(s):
    return pl.BlockSpec((None, FTM, D), lambda i: (s, i, 0))


def _rowshape(T, dtype, c=D):
    return (jax.ShapeDtypeStruct((T, c), dtype), _frows(c))


def _sumshape(c=D):
    return (jax.ShapeDtypeStruct((8, c), f32), pl.BlockSpec((8, c), lambda i: (0, 0)))


def _add_colsum(ref, x, cols=None):
    @pl.when(pl.program_id(0) == 0)
    def _():
        if cols is None:
            ref[...] = jnp.zeros_like(ref)
        else:
            ref[:, cols] = jnp.zeros((8, x.shape[-1]), f32)

    if cols is None:
        ref[...] += _colsum8(x)
    else:
        ref[:, cols] += _colsum8(x)


TT = 512


def _rows(c, cb=0, tt=TT):
    return pl.BlockSpec((tt, c), lambda i: (i, cb))


def _sec(s, tt=TT):
    return pl.BlockSpec((None, tt, D), lambda i: (s, i, 0))


def _const(shape):
    return pl.BlockSpec(shape, lambda i: (0,) * len(shape))


def _acc_spec(c):
    return pl.BlockSpec((8, c), lambda i: (0, 0))


def _rms(x):
    return lax.rsqrt(jnp.mean(x * x, axis=-1, keepdims=True) + EPS)


def _rms_bwd(dy_g, xn, rstd):
    return rstd * (dy_g - xn * jnp.mean(dy_g * xn, axis=-1, keepdims=True))


def _head_sum(x, bd):
    parts = []
    for cb in range(x.shape[-1] // 128):
        xb = x[:, cb * 128:(cb + 1) * 128]
        hi = xb.astype(bf16)
        lo = (xb - hi.astype(f32)).astype(bf16)
        parts.append(jnp.dot(hi, bd, preferred_element_type=f32) + jnp.dot(lo, bd, preferred_element_type=f32))
    return parts[0] if len(parts) == 1 else jnp.concatenate(parts, axis=1)


def _norm1_fwd(x, g):
    T = x.shape[0]

    def body(x_ref, g_ref, h_ref):
        xv = x_ref[...]
        h_ref[...] = (xv * _rms(xv) * g_ref[...]).astype(bf16)

    return pl.pallas_call(
        body, name="norm1_fwd", grid=(T // TT,), in_specs=[_rows(D), _const((1, D))], out_specs=_rows(D),
        out_shape=jax.ShapeDtypeStruct((T, D), bf16), compiler_params=_cparams(("parallel",)))(x, g)


def _convnorm_fwd(c, g):
    T = c.shape[0]

    def body(c_ref, g_ref, s_ref):
        cv = c_ref[...]
        r = cv * _rms(cv) * g_ref[...]
        s_ref[...] = (r * _sig(r)).astype(bf16)

    return pl.pallas_call(
        body, name="convnorm_fwd", grid=(T // TT,), in_specs=[_rows(D), _const((1, D))], out_specs=_rows(D),
        out_shape=jax.ShapeDtypeStruct((T, D), bf16), compiler_params=_cparams(("parallel",)))(c, g)


def _qk_fwd(z8, qg, kg, bd):
    T = z8.shape[1]

    def body(q_ref, k_ref, qg_ref, kg_ref, bd_ref, qn_ref, kn_ref):
        bdv = bd_ref[...]
        q = q_ref[...]
        qn_ref[...] = q * lax.rsqrt(_head_sum(q * q, bdv) * (1.0 / HEAD_DIM) + EPS) * qg_ref[...] * (HEAD_DIM ** -0.5)
        k = k_ref[...]
        kn_ref[...] = k * lax.rsqrt(_head_sum(k * k, bdv) * (1.0 / HEAD_DIM) + EPS) * kg_ref[...]

    return pl.pallas_call(
        body, name="qk_fwd", grid=(T // TT,),
        in_specs=[_sec(Z_Q), _sec(Z_K), _const((1, D)), _const((1, D)), _const((128, 128))],
        out_specs=[_rows(D), _rows(D)],
        out_shape=[jax.ShapeDtypeStruct((T, D), f32)] * 2, compiler_params=_cparams(("parallel",)))(z8, z8, qg, kg, bd)


def _gate_fwd(z8, gate_b, ya, yb):
    T = ya.shape[0]

    def body(ga_ref, gb_ref, b_ref, ya_ref, yb_ref, mixed_ref):
        g_a = _sig(ga_ref[...] + b_ref[:, :D])
        g_b = _sig(gb_ref[...] + b_ref[:, D:])
        mixed_ref[...] = (g_a * ya_ref[...] + g_b * yb_ref[...]).astype(bf16)

    return pl.pallas_call(
        body, name="gate_fwd", grid=(T // TT,),
        in_specs=[_sec(Z_GA), _sec(Z_GB), _const((1, 2 * D)), _rows(D), _rows(D)], out_specs=_rows(D),
        out_shape=jax.ShapeDtypeStruct((T, D), bf16), compiler_params=_cparams(("parallel",)))(z8, z8, gate_b, ya, yb)


def _out_norm2_fwd(mixed, w_out, x, g):
    T = x.shape[0]

    def epilogue(acc, extra, const, out):
        x1 = extra[0][...] + acc
        out[0][...] = x1
        out[1][...] = (x1 * _rms(x1) * const[0][...]).astype(bf16)

    return _matmul_fused("mm_t1_norm2", mixed[None], w_out[None], ((0, 0),), epilogue, [(x, _frows())], [g],
                         [_rowshape(T, f32), _rowshape(T, bf16)])


def _down_loss_fwd(f, w_down, x1, target):
    T = x1.shape[0]

    def epilogue(acc, extra, const, out):
        diff = extra[0][...] + acc - extra[1][...]
        dy = diff * (1.0 / D)
        out[0][...] = dy
        out[1][...] = dy.astype(bf16)
        _add_colsum(out[2], diff * diff)

    return _matmul_fused("mm_t2_loss", f[None], w_down[None], ((0, 0),), epilogue, [(x1, _frows()), (target, _frows())],
                         [], [_rowshape(T, f32), _rowshape(T, bf16), _sumshape()], sums=True)


def _up_norm2_bwd(du3, w_up_t, x1, dy, g, token):
    T = x1.shape[0]

    def epilogue(dh, extra, const, out):
        x1v = extra[0][...]
        rstd = _rms(x1v)
        xn = x1v * rstd
        dx1 = extra[1][...] + _rms_bwd(dh * const[0][...], xn, rstd)
        out[0][...] = dx1
        out[1][...] = dx1.astype(bf16)
        _add_colsum(out[2], dh * xn)

    return _matmul_fused("mm_dh2_norm2", du3, w_up_t.reshape(2, D_FF, D), ((0, 0), (1, 1)), epilogue,
                         [(x1, _frows()), (dy, _frows())], [g],
                         [_rowshape(T, f32), _rowshape(T, bf16), _sumshape()], sums=True, passed=[token])


def _out_gate_bwd(dx1b, w_out, z8, gate_b, ya, yb, dz8):
    T = ya.shape[0]

    def epilogue(dm, extra, const, out):
        b_ref = const[0]
        g_a = _sig(extra[0][...] + b_ref[:, :D])
        g_b = _sig(extra[1][...] + b_ref[:, D:])
        out[0][...] = (dm * g_a).astype(bf16)
        out[1][...] = (dm * g_b).astype(bf16)
        dla = dm * extra[2][...] * g_a * (1.0 - g_a)
        dlb = dm * extra[3][...] * g_b * (1.0 - g_b)
        out[2][0] = dla.astype(bf16)
        out[2][1] = dlb.astype(bf16)
        _add_colsum(out[3], dla, slice(0, D))
        _add_colsum(out[3], dlb, slice(D, 2 * D))

    return _matmul_fused(
        "mm_dmixed_gate", dx1b[None], w_out[None], ((0, 0),), epilogue,
        [(z8, _fsec(Z_GA)), (z8, _fsec(Z_GB)), (ya, _frows()), (yb, _frows())], [gate_b],
        [_rowshape(T, bf16), _rowshape(T, bf16),
         (jax.ShapeDtypeStruct(dz8.shape, bf16), pl.BlockSpec((2, FTM, D), lambda i: (1, i, 0))), _sumshape(2 * D)],
        nt=True, sums=True, passed=[dz8], aliases={0: 2})


def _convnorm_bwd(c, ds, g):
    T = c.shape[0]

    def body(c_ref, ds_ref, g_ref, dc_ref, dg_ref):
        cv = c_ref[...]
        rstd = _rms(cv)
        r0 = cv * rstd
        gv = g_ref[...]
        r = r0 * gv
        sg = _sig(r)
        dr = ds_ref[...] * sg * (1.0 + r * (1.0 - sg))
        dc_ref[...] = _rms_bwd(dr * gv, r0, rstd)

        @pl.when(pl.program_id(0) == 0)
        def _():
            dg_ref[...] = jnp.zeros_like(dg_ref)

        dg_ref[...] += _colsum8(dr * r0)

    return pl.pallas_call(
        body, name="convnorm_bwd", grid=(T // TT,), in_specs=[_rows(D), _rows(D), _const((1, D))],
        out_specs=[_rows(D), _acc_spec(D)],
        out_shape=[jax.ShapeDtypeStruct((T, D), f32), jax.ShapeDtypeStruct((8, D), f32)],
        compiler_params=_cparams(("arbitrary",)))(c, ds, g)


def _qk_bwd(z8, dqn, dkn, dv, qg, kg, bd, dz8):
    T = dqn.shape[0]

    def body(q_ref, k_ref, dqn_ref, dkn_ref, dv_ref, qg_ref, kg_ref, bd_ref, dz_in, dz_ref, dqg_ref, dkg_ref):
        del dz_in
        bdv = bd_ref[...]

        @pl.when(pl.program_id(0) == 0)
        def _():
            dqg_ref[...] = jnp.zeros_like(dqg_ref)
            dkg_ref[...] = jnp.zeros_like(dkg_ref)

        def one(raw, dn_scaled, g, dg_ref, sec):
            rstd = lax.rsqrt(_head_sum(raw * raw, bdv) * (1.0 / HEAD_DIM) + EPS)
            n = raw * rstd
            dg_ref[...] += _colsum8(dn_scaled * n)
            dn = dn_scaled * g
            draw = rstd * (dn - n * (_head_sum(dn * n, bdv) * (1.0 / HEAD_DIM)))
            dz_ref[sec] = draw.astype(bf16)

        one(q_ref[...], dqn_ref[...] * (HEAD_DIM ** -0.5), qg_ref[...], dqg_ref, 0)
        one(k_ref[...], dkn_ref[...], kg_ref[...], dkg_ref, 1)
        dz_ref[2] = dv_ref[...].astype(bf16)
        dz_ref[3] = jnp.zeros((TT, D), bf16)

    return pl.pallas_call(
        body, name="qk_bwd", grid=(T // TT,),
        in_specs=[_sec(Z_Q), _sec(Z_K), _rows(D), _rows(D), _rows(D), _const((1, D)), _const((1, D)),
                  _const((128, 128)), pl.BlockSpec(memory_space=pl.ANY)],
        out_specs=[pl.BlockSpec((4, TT, D), lambda i: (1, i, 0)), _acc_spec(D), _acc_spec(D)],
        out_shape=[jax.ShapeDtypeStruct(dz8.shape, bf16), jax.ShapeDtypeStruct((8, D), f32),
                   jax.ShapeDtypeStruct((8, D), f32)],
        input_output_aliases={8: 0},
        compiler_params=_cparams(("arbitrary",)))(z8, z8, dqn, dkn, dv, qg, kg, bd, dz8)


def _in_norm1_bwd(dz8, w_in_t, x, dx1, g, token):
    T = x.shape[0]

    def epilogue(dh, extra, const, out):
        xv = extra[0][...]
        rstd = _rms(xv)
        xn = xv * rstd
        out[0][...] = extra[1][...] + _rms_bwd(dh * const[0][...], xn, rstd)
        _add_colsum(out[1], dh * xn)

    return _matmul_fused("mm_dh_norm1", dz8, w_in_t.reshape(7, D, D), tuple(zip(range(7), _W_OF_Z)), epilogue,
                         [(x, _frows()), (dx1, _frows())], [g], [_rowshape(T, f32), _sumshape()],
                         sums=True, passed=[token])


CCW = 256
CR = 64
HALO = 32


def _conv_fwd(z8, conv_w, conv_b, S):
    T = z8.shape[1]
    nb = T // S
    ncb = D // CCW

    def body(av_ref, ag_ref, w_ref, b_ref, c_ref, pad):
        pad[0:HALO, :] = jnp.zeros((HALO, CCW), f32)

        def fill(i, carry):
            r0 = pl.multiple_of(i * 256, 256)
            pad[pl.ds(HALO + r0, 256), :] = av_ref[pl.ds(r0, 256), :] * _sig(ag_ref[pl.ds(r0, 256), :])
            return carry

        lax.fori_loop(0, S // 256, fill, 0)
        bias = b_ref[...]

        def chunk(i, carry):
            r0 = pl.multiple_of(i * CR, CR)
            win = pad[pl.ds(r0, CR + HALO), :]
            acc = jnp.zeros((CR, CCW), f32) + bias
            for s in range(8):
                part = None
                for m in range((CONV_WIDTH - 1 - s) // 8 + 1):
                    j = CONV_WIDTH - 1 - 8 * m - s
                    term = win[24 - 8 * m:24 - 8 * m + CR + 8, :] * w_ref[j:j + 1, :]
                    part = term if part is None else part + term
                acc = acc + part[8 - s:8 - s + CR, :]
            c_ref[pl.ds(r0, CR), :] = acc
            return carry

        lax.fori_loop(0, S // CR, chunk, 0)

    zs = lambda s: pl.BlockSpec((None, S, CCW), lambda b, cb: (s, b, cb))
    return pl.pallas_call(
        body, name="conv_fwd", grid=(nb, ncb),
        in_specs=[zs(Z_AVAL), zs(Z_AGATE), pl.BlockSpec((CONV_WIDTH, CCW), lambda b, cb: (0, cb)),
                  pl.BlockSpec((1, CCW), lambda b, cb: (0, cb))],
        out_specs=pl.BlockSpec((S, CCW), lambda b, cb: (b, cb)),
        out_shape=jax.ShapeDtypeStruct((T, D), f32),
        scratch_shapes=[pltpu.VMEM((S + HALO, CCW), f32)],
        compiler_params=_cparams(("parallel", "parallel")))(z8, z8, conv_w, conv_b)


def _conv_bwd(dc, z8, conv_w, dz8, S):
    T = dc.shape[0]
    nb = T // S
    ncb = D // CCW

    def body(dc_ref, av_ref, ag_ref, w_ref, dz_in, dz_ref, dw_ref, apad, dpad, shbuf):
        del dz_in
        apad[0:HALO, :] = jnp.zeros((HALO, CCW), f32)
        dpad[S:S + HALO, :] = jnp.zeros((HALO, CCW), f32)
        dw_ref[...] = jnp.zeros_like(dw_ref)

        def fill(i, carry):
            r0 = pl.multiple_of(i * 256, 256)
            apad[pl.ds(HALO + r0, 256), :] = av_ref[pl.ds(r0, 256), :] * _sig(ag_ref[pl.ds(r0, 256), :])
            dpad[pl.ds(r0, 256), :] = dc_ref[pl.ds(r0, 256), :]
            return carry

        lax.fori_loop(0, S // 256, fill, 0)

        def chunk(i, carry):
            r0 = pl.multiple_of(i * CR, CR)
            dwin = dpad[pl.ds(r0, CR + HALO), :]
            da = jnp.zeros((CR, CCW), f32)
            for s in range(8):
                shbuf[...] = dwin[s:s + CR, :]
                dshift = shbuf[...]
                part = None
                for m in range((CONV_WIDTH - 1 - s) // 8 + 1):
                    j = CONV_WIDTH - 1 - 8 * m - s
                    term = dwin[8 * m:8 * m + CR + 8, :] * w_ref[j:j + 1, :]
                    part = term if part is None else part + term
                    a_lag = apad[pl.ds(r0 + HALO - 8 * m, CR), :]
                    dw_ref[8 * j:8 * j + 8, :] += _colsum8(dshift * a_lag)
                da = da + part[s:s + CR, :]
            dw_ref[8 * CONV_WIDTH:8 * CONV_WIDTH + 8, :] += _colsum8(dwin[0:CR, :])
            av = av_ref[pl.ds(r0, CR), :]
            sg = _sig(ag_ref[pl.ds(r0, CR), :])
            dz_ref[0, pl.ds(r0, CR), :] = (da * sg).astype(bf16)
            dz_ref[1, pl.ds(r0, CR), :] = (da * av * sg * (1.0 - sg)).astype(bf16)
            return carry

        lax.fori_loop(0, S // CR, chunk, 0)

    zs = lambda s: pl.BlockSpec((None, S, CCW), lambda b, cb: (s, b, cb))
    return pl.pallas_call(
        body, name="conv_bwd", grid=(nb, ncb),
        in_specs=[pl.BlockSpec((S, CCW), lambda b, cb: (b, cb)), zs(Z_AVAL), zs(Z_AGATE),
                  pl.BlockSpec((CONV_WIDTH, CCW), lambda b, cb: (0, cb)), pl.BlockSpec(memory_space=pl.ANY)],
        out_specs=[pl.BlockSpec((2, S, CCW), lambda b, cb: (0, b, cb)),
                   pl.BlockSpec((None, 256, CCW), lambda b, cb: (b, 0, cb))],
        out_shape=[jax.ShapeDtypeStruct(dz8.shape, bf16), jax.ShapeDtypeStruct((nb, 256, D), f32)],
        input_output_aliases={4: 0},
        scratch_shapes=[pltpu.VMEM((S + HALO, CCW), f32), pltpu.VMEM((S + HALO, CCW), f32),
                        pltpu.VMEM((CR, CCW), f32)],
        compiler_params=_cparams(("parallel", "parallel")))(dc, z8, z8, conv_w, dz8)


FR = 128
NFB = D_FF // CCW


def _ffn_window(ref, i, r0):
    return ref[pl.ds(r0 - 8, FR + 8), :]


def _ffn_u(win, w_ref, b_ref):
    return (win[6:6 + FR, :] * w_ref[0:1, :] + win[7:7 + FR, :] * w_ref[1:2, :]
            + win[8:8 + FR, :] * w_ref[2:3, :] + b_ref[...])


def _ffn_fwd(u3, ffn_w, ffn_b, S):
    T = u3.shape[1]
    nb = T // S

    def body(uv_ref, ug_ref, wv_ref, wg_ref, bv_ref, bg_ref, f_ref):
        def chunk(first, i):
            r0 = 0 if first else pl.multiple_of(i * FR, FR)
            if first:
                z = jnp.zeros((8, CCW), f32)
                wv = jnp.concatenate([z, uv_ref[0:FR, :]], axis=0)
                wg = jnp.concatenate([z, ug_ref[0:FR, :]], axis=0)
            else:
                wv = _ffn_window(uv_ref, i, r0)
                wg = _ffn_window(ug_ref, i, r0)
            u_val = _ffn_u(wv, wv_ref, bv_ref)
            u_gate = _ffn_u(wg, wg_ref, bg_ref)
            f_ref[pl.ds(r0, FR), :] = (u_gate * _sig(u_gate) * u_val).astype(bf16)

        chunk(True, 0)

        def loop(i, carry):
            chunk(False, i)
            return carry

        lax.fori_loop(1, S // FR, loop, 0)

    us = lambda h: pl.BlockSpec((None, S, CCW), lambda b, cb: (h, b, cb))
    ws = lambda h: pl.BlockSpec((3, CCW), lambda b, cb: (0, h * NFB + cb))
    bs = lambda h: pl.BlockSpec((1, CCW), lambda b, cb: (0, h * NFB + cb))
    return pl.pallas_call(
        body, name="ffn_fwd", grid=(nb, NFB),
        in_specs=[us(0), us(1), ws(0), ws(1), bs(0), bs(1)],
        out_specs=pl.BlockSpec((S, CCW), lambda b, cb: (b, cb)),
        out_shape=jax.ShapeDtypeStruct((T, D_FF), bf16),
        compiler_params=_cparams(("parallel", "parallel")))(u3, u3, ffn_w, ffn_w, ffn_b, ffn_b)


def _ffn_bwd(u3, df, ffn_w, ffn_b, S):
    T = u3.shape[1]
    nb = T // S

    def body(uv_ref, ug_ref, df_ref, wv_ref, wg_ref, bv_ref, bg_ref, du_ref, dw_ref, dvpad, dgpad, shbuf):
        dvpad[S:S + 8, :] = jnp.zeros((8, CCW), f32)
        dgpad[S:S + 8, :] = jnp.zeros((8, CCW), f32)
        dw_ref[...] = jnp.zeros_like(dw_ref)

        def chunk(first, i):
            r0 = 0 if first else pl.multiple_of(i * FR, FR)
            if first:
                z = jnp.zeros((8, CCW), f32)
                wv = jnp.concatenate([z, uv_ref[0:FR, :]], axis=0)
                wg = jnp.concatenate([z, ug_ref[0:FR, :]], axis=0)
            else:
                wv = _ffn_window(uv_ref, i, r0)
                wg = _ffn_window(ug_ref, i, r0)
            taps = []
            for h, win in enumerate((wv, wg)):
                shbuf[2 * h] = win[6:6 + FR, :]
                shbuf[2 * h + 1] = win[7:7 + FR, :]
                taps.append((shbuf[2 * h], shbuf[2 * h + 1], win[8:8 + FR, :]))
            conv = lambda x, w_ref, b_ref: (x[0] * w_ref[0:1, :] + x[1] * w_ref[1:2, :] + x[2] * w_ref[2:3, :]
                                            + b_ref[...])
            u_val = conv(taps[0], wv_ref, bv_ref)
            u_gate = conv(taps[1], wg_ref, bg_ref)
            dfc = df_ref[pl.ds(r0, FR), :]
            sg = _sig(u_gate)
            d_val = dfc * u_gate * sg
            d_gate = dfc * u_val * sg * (1.0 + u_gate * (1.0 - sg))
            dvpad[pl.ds(r0, FR), :] = d_val
            dgpad[pl.ds(r0, FR), :] = d_gate
            for h, dd in enumerate((d_val, d_gate)):
                for j in range(3):
                    dw_ref[h, 8 * j:8 * j + 8, :] += _colsum8(dd * taps[h][j])
                dw_ref[h, 24:32, :] += _colsum8(dd)

        chunk(True, 0)

        def loop(i, carry):
            chunk(False, i)
            return carry

        lax.fori_loop(1, S // FR, loop, 0)

        def back(i, carry):
            r0 = pl.multiple_of(i * FR, FR)
            for h, (dpad, w_ref) in enumerate(((dvpad, wv_ref), (dgpad, wg_ref))):
                win = dpad[pl.ds(r0, FR + 8), :]
                du = (win[0:FR, :] * w_ref[2:3, :] + win[1:1 + FR, :] * w_ref[1:2, :]
                      + win[2:2 + FR, :] * w_ref[0:1, :])
                du_ref[h, pl.ds(r0, FR), :] = du.astype(bf16)
            return carry

        lax.fori_loop(0, S // FR, back, 0)

    us = lambda h: pl.BlockSpec((None, S, CCW), lambda b, cb: (h, b, cb))
    ws = lambda h: pl.BlockSpec((3, CCW), lambda b, cb: (0, h * NFB + cb))
    bs = lambda h: pl.BlockSpec((1, CCW), lambda b, cb: (0, h * NFB + cb))
    return pl.pallas_call(
        body, name="ffn_bwd", grid=(nb, NFB),
        in_specs=[us(0), us(1), pl.BlockSpec((S, CCW), lambda b, cb: (b, cb)), ws(0), ws(1), bs(0), bs(1)],
        out_specs=[pl.BlockSpec((2, S, CCW), lambda b, cb: (0, b, cb)),
                   pl.BlockSpec((None, 2, 32, CCW), lambda b, cb: (b, 0, 0, cb))],
        out_shape=[jax.ShapeDtypeStruct((2, T, D_FF), bf16), jax.ShapeDtypeStruct((nb, 2, 32, D_FF), f32)],
        scratch_shapes=[pltpu.VMEM((S + 8, CCW), f32), pltpu.VMEM((S + 8, CCW), f32),
                        pltpu.VMEM((4, FR, CCW), f32)],
        compiler_params=_cparams(("parallel", "parallel")))(u3, u3, df, ffn_w, ffn_w, ffn_b, ffn_b)


AB = ATTN_BLOCK


def _attn_bias():
    slopes = (np.float32(2.0) ** (np.float32(-8.0) * np.arange(1, N_HEADS + 1, dtype=np.float32)
                                  / np.float32(N_HEADS))).astype(np.float32)
    steps = (np.arange(AB)[:, None] + AB) - np.arange(2 * AB)[None, :]
    own = (np.arange(2 * AB) >= AB)[None, :]
    out = []
    for window, dil in GROUPS:
        valid = (steps >= 0) & (steps <= window // dil)
        dist = slopes[:, None, None] * (steps * dil).astype(np.float32)[None]
        kinds = [np.where(v[None], dist, np.float32(MASK_BIAS)) for v in (valid, valid & own)]
        out.append(np.stack(kinds, axis=1))
    return jnp.asarray(np.stack(out).astype(np.float32))


def _head_masks():
    lane = lax.broadcasted_iota(jnp.int32, (1, 128), 1)
    return (lane < HEAD_DIM, lane >= HEAD_DIM)


def _perm_chunks(S, d):
    L = S // d
    ch = min(L, 256)
    out = []
    for r in range(d):
        for c in range(L // ch):
            start = r + d * ch * c
            out.append((pl.ds(start, ch, stride=d) if d > 1 else pl.ds(start, ch), r * L + c * ch, ch))
    return out


def _stack_heads(x, masks):
    return jnp.concatenate([jnp.where(masks[0], x, 0), jnp.where(masks[1], x, 0)], axis=0)


_NT = (((1,), (1,)), ((), ()))
_TN = (((0,), (0,)), ((), ()))
SCH = 32


def _attn_fwd(qn, kn, z8, bias, S):
    T = qn.shape[0]
    nb = T // S
    nblk = S // AB

    def body(q_ref, k_ref, v_ref, bias_ref, o_ref, ob_ref, lse_ref, qs, ks, vs, s2, p2, ogp, lgp, *group_scratch):
        og, lg = group_scratch[:3], group_scratch[3:]
        masks = _head_masks()
        ks[0:AB, :] = jnp.zeros((AB, 128), bf16)
        vs[0:AB, :] = jnp.zeros((AB, 128), bf16)

        for g, (_, d) in enumerate(GROUPS):
            nsub = S // (d * AB)
            chunks = _perm_chunks(S, d)
            for src, dst, ch in chunks:
                qs[dst:dst + ch, :] = q_ref[src, :].astype(bf16)
                ks[AB + dst:AB + dst + ch, :] = k_ref[src, :].astype(bf16)
                vs[AB + dst:AB + dst + ch, :] = v_ref[src, :].astype(bf16)
            od, ld = (og[g], lg[g]) if d == 1 else (ogp, lgp)

            def scores(j, carry):
                r0 = pl.multiple_of(j * AB, AB)
                q2 = _stack_heads(qs[pl.ds(r0, AB), :], masks)
                s2[j] = lax.dot_general(q2, ks[pl.ds(r0, 2 * AB), :], _NT, preferred_element_type=f32)
                return carry

            lax.fori_loop(0, nblk, scores, 0, unroll=8)

            def softmax(j, carry, g=g, nsub=nsub, ld=ld):
                r0 = pl.multiple_of(j * AB, AB)
                kind = (j % nsub == 0).astype(jnp.int32)
                for cc in range(AB // SCH):
                    lses = []
                    for hh in range(2):
                        rows = pl.ds(hh * AB + cc * SCH, SCH)
                        sb = s2[j, rows, :] - bias_ref[g, hh, kind, cc * SCH:(cc + 1) * SCH, :]
                        m = jnp.max(sb, axis=-1, keepdims=True)
                        p = jnp.exp(sb - m)
                        den = jnp.sum(p, axis=-1, keepdims=True)
                        p2[j, rows, :] = (p * (1.0 / den)).astype(bf16)
                        lses.append(m + jnp.log(den))
                    ld[pl.ds(r0 + cc * SCH, SCH), :] = jnp.where(masks[0], lses[0], lses[1])
                return carry

            lax.fori_loop(0, nblk, softmax, 0, unroll=2)

            def values(j, carry, od=od):
                r0 = pl.multiple_of(j * AB, AB)
                pv2 = jnp.dot(p2[j], vs[pl.ds(r0, 2 * AB), :], preferred_element_type=f32)
                od[pl.ds(r0, AB), :] = jnp.where(masks[0], pv2[:AB], pv2[AB:])
                return carry

            lax.fori_loop(0, nblk, values, 0, unroll=8)

            if d > 1:
                for src, dst, ch in chunks:
                    og[g][src, :] = ogp[dst:dst + ch, :]
                    lg[g][src, :] = lgp[dst:dst + ch, :]

        def combine(i, carry):
            rr = pl.ds(pl.multiple_of(i * 256, 256), 256)
            l0, l1, l2 = lg[0][rr, :], lg[1][rr, :], lg[2][rr, :]
            mx = jnp.maximum(jnp.maximum(l0, l1), l2)
            e0, e1, e2 = jnp.exp(l0 - mx), jnp.exp(l1 - mx), jnp.exp(l2 - mx)
            den = e0 + e1 + e2
            o = (e0 * og[0][rr, :] + e1 * og[1][rr, :] + e2 * og[2][rr, :]) / den
            o_ref[rr, :] = o
            ob_ref[rr, :] = o.astype(bf16)
            lse_ref[rr, :] = mx + jnp.log(den)
            return carry

        lax.fori_loop(0, S // 256, combine, 0)

    blk = pl.BlockSpec((S, 128), lambda b, hp: (b, hp))
    return pl.pallas_call(
        body, name="attn_fwd", grid=(nb, N_HEADS // 2),
        in_specs=[blk, blk, pl.BlockSpec((None, S, 128), lambda b, hp: (Z_V, b, hp)),
                  pl.BlockSpec((3, 2, 2, AB, 2 * AB), lambda b, hp: (0, hp, 0, 0, 0))],
        out_specs=[blk, blk, blk],
        out_shape=[jax.ShapeDtypeStruct((T, D), f32), jax.ShapeDtypeStruct((T, D), bf16),
                   jax.ShapeDtypeStruct((T, D), f32)],
        scratch_shapes=[pltpu.VMEM((S, 128), bf16), pltpu.VMEM((S + AB, 128), bf16), pltpu.VMEM((S + AB, 128), bf16),
                        pltpu.VMEM((nblk, 2 * AB, 2 * AB), f32), pltpu.VMEM((nblk, 2 * AB, 2 * AB), bf16),
                        pltpu.VMEM((S, 128), f32), pltpu.VMEM((S, 128), f32)] + [pltpu.VMEM((S, 128), f32)] * 6,
        compiler_params=_cparams(("parallel", "parallel")))(qn, kn, z8, bias)


def _attn_bwd(qn, kn, z8, do, o, lse, bias, bd, S):
    T = qn.shape[0]
    nb = T // S

    nblk = S // AB

    def body(q_ref, k_ref, v_ref, do_ref, o_ref, lse_ref, bias_ref, bd_ref, dq_ref, dk_ref, dv_ref,
             delta, qs, ks, vs, dos, lsp, dlp, s2, dp2, p2, ds2, dqp, dkp, dvp):
        masks = _head_masks()
        bdv = bd_ref[...]
        dq_ref[...] = jnp.zeros_like(dq_ref)
        dk_ref[...] = jnp.zeros_like(dk_ref)
        dv_ref[...] = jnp.zeros_like(dv_ref)
        ks[0:AB, :] = jnp.zeros((AB, 128), bf16)
        vs[0:AB, :] = jnp.zeros((AB, 128), bf16)

        def prep(i, carry):
            rr = pl.ds(pl.multiple_of(i * 256, 256), 256)
            delta[rr, :] = _head_sum(do_ref[rr, :] * o_ref[rr, :], bdv)
            return carry

        lax.fori_loop(0, S // 256, prep, 0)

        for g, (_, d) in enumerate(GROUPS):
            nsub = S // (d * AB)
            chunks = _perm_chunks(S, d)
            for src, dst, ch in chunks:
                qs[dst:dst + ch, :] = q_ref[src, :].astype(bf16)
                ks[AB + dst:AB + dst + ch, :] = k_ref[src, :].astype(bf16)
                vs[AB + dst:AB + dst + ch, :] = v_ref[src, :].astype(bf16)
                dos[dst:dst + ch, :] = do_ref[src, :].astype(bf16)
                lsp[dst:dst + ch, :] = lse_ref[src, :]
                dlp[dst:dst + ch, :] = delta[src, :]
            dkp[...] = jnp.zeros_like(dkp)
            dvp[...] = jnp.zeros_like(dvp)

            def scores(j, carry):
                r0 = pl.multiple_of(j * AB, AB)
                q2 = _stack_heads(qs[pl.ds(r0, AB), :], masks)
                do2 = _stack_heads(dos[pl.ds(r0, AB), :], masks)
                s2[j] = lax.dot_general(q2, ks[pl.ds(r0, 2 * AB), :], _NT, preferred_element_type=f32)
                dp2[j] = lax.dot_general(do2, vs[pl.ds(r0, 2 * AB), :], _NT, preferred_element_type=f32)
                return carry

            lax.fori_loop(0, nblk, scores, 0, unroll=8)

            def probs(j, carry, g=g, nsub=nsub):
                r0 = pl.multiple_of(j * AB, AB)
                kind = (j % nsub == 0).astype(jnp.int32)
                for cc in range(AB // SCH):
                    lse_c = lsp[pl.ds(r0 + cc * SCH, SCH), :]
                    del_c = dlp[pl.ds(r0 + cc * SCH, SCH), :]
                    for hh in range(2):
                        c0 = hh * HEAD_DIM
                        rows = pl.ds(hh * AB + cc * SCH, SCH)
                        sb = s2[j, rows, :] - bias_ref[g, hh, kind, cc * SCH:(cc + 1) * SCH, :]
                        p = jnp.exp(sb - lse_c[:, c0:c0 + 1])
                        p2[j, rows, :] = p.astype(bf16)
                        ds2[j, rows, :] = (p * (dp2[j, rows, :] - del_c[:, c0:c0 + 1])).astype(bf16)
                return carry

            lax.fori_loop(0, nblk, probs, 0, unroll=2)

            def grads(j, carry):
                r0 = pl.multiple_of(j * AB, AB)
                q2 = _stack_heads(qs[pl.ds(r0, AB), :], masks)
                do2 = _stack_heads(dos[pl.ds(r0, AB), :], masks)
                dsb = ds2[j]
                t = jnp.dot(dsb, ks[pl.ds(r0, 2 * AB), :], preferred_element_type=f32)
                dqp[pl.ds(r0, AB), :] = jnp.where(masks[0], t[:AB], t[AB:])
                dkp[pl.ds(r0, 2 * AB), :] += lax.dot_general(dsb, q2, _TN, preferred_element_type=f32)
                dvp[pl.ds(r0, 2 * AB), :] += lax.dot_general(p2[j], do2, _TN, preferred_element_type=f32)
                return carry

            lax.fori_loop(0, nblk, grads, 0, unroll=4)

            for src, dst, ch in chunks:
                dq_ref[src, :] += dqp[dst:dst + ch, :]
                dk_ref[src, :] += dkp[AB + dst:AB + dst + ch, :]
                dv_ref[src, :] += dvp[AB + dst:AB + dst + ch, :]

    blk = pl.BlockSpec((S, 128), lambda b, hp: (b, hp))
    row = lambda dt, pad=0: pltpu.VMEM((S + pad, 128), dt)
    blocks = lambda dt: pltpu.VMEM((nblk, 2 * AB, 2 * AB), dt)
    return pl.pallas_call(
        body, name="attn_bwd", grid=(nb, N_HEADS // 2),
        in_specs=[blk, blk, pl.BlockSpec((None, S, 128), lambda b, hp: (Z_V, b, hp)), blk, blk, blk,
                  pl.BlockSpec((3, 2, 2, AB, 2 * AB), lambda b, hp: (0, hp, 0, 0, 0)),
                  pl.BlockSpec((128, 128), lambda b, hp: (0, 0))],
        out_specs=[blk, blk, blk],
        out_shape=[jax.ShapeDtypeStruct((T, D), f32)] * 3,
        scratch_shapes=[row(f32), row(bf16), row(bf16, AB), row(bf16, AB), row(bf16), row(f32), row(f32),
                        blocks(f32), blocks(f32), blocks(bf16), blocks(bf16), row(f32), row(f32, AB), row(f32, AB)],
        compiler_params=_cparams(("parallel", "parallel")))(qn, kn, z8, do, o, lse, bias, bd)


def _any_spec():
    return pl.BlockSpec(memory_space=pl.ANY)


def _allgather_rows(shards, n_full):
    n = len(shards)

    def body(*refs):
        ins, outs = refs[:n], refs[n:2 * n]
        send_sems, recv_sems, local_sems = refs[2 * n:]
        x, y, c, me = _my_pos()
        sibling = (x, y, 1 - c)
        chips = [(1 - x, y), (x, 1 - y), (1 - x, 1 - y)]

        def idx(px, py, pc):
            return 4 * px + 2 * py + pc

        def copy(a, k, blk, to, src=None):
            return pltpu.make_async_remote_copy(
                src_ref=outs[a].at[blk] if src is None else src, dst_ref=outs[a].at[blk],
                send_sem=send_sems.at[a, k], recv_sem=recv_sems.at[a, k], device_id=to, device_id_type=MESH)

        mine = [pltpu.make_async_copy(ins[a], outs[a].at[me], local_sems.at[a]) for a in range(n)]
        for cp in mine:
            cp.start()
        first = []
        for a in range(n_full):
            first.append(copy(a, 0, me, sibling, src=ins[a]))
            first += [copy(a, 1 + j, me, (*chip, c), src=ins[a]) for j, chip in enumerate(chips)]
        for cp in first:
            cp.start()
        passed = []
        for a in range(n_full):
            for j, chip in enumerate(chips):
                blk = idx(*chip, c)
                copy(a, 1 + j, blk, (x, y, c)).wait_recv()
                cp = copy(a, 4 + j, blk, sibling)
                cp.start()
                passed.append(cp)
        for a in range(n_full):
            copy(a, 0, idx(x, y, 1 - c), (x, y, c)).wait_recv()
            for j, chip in enumerate(chips):
                copy(a, 4 + j, idx(*chip, 1 - c), (x, y, c)).wait_recv()
        for cp in first + passed:
            cp.wait_send()
        for cp in mine:
            cp.wait()

    return pl.pallas_call(
        body, name="allgather_weights",
        in_specs=[_any_spec()] * n, out_specs=[_any_spec()] * n,
        out_shape=[jax.ShapeDtypeStruct((N_DEV,) + s.shape, s.dtype) for s in shards],
        scratch_shapes=[pltpu.SemaphoreType.DMA((n_full, 7)), pltpu.SemaphoreType.DMA((n_full, 7)),
                        pltpu.SemaphoreType.DMA((n,))],
    )(*shards)


def _peer(x, y, c, k):
    tx = 1 - x if (k >> 2) & 1 else x
    ty = 1 - y if (k >> 1) & 1 else y
    tc = 1 - c if k & 1 else c
    return (tx, ty, tc), 4 * tx + 2 * ty + tc


_PEER_ORDER = (2, 4, 6, 3, 5, 7, 1)


_HBM = pl.BlockSpec(memory_space=pltpu.HBM)
_SEM = pl.BlockSpec(memory_space=pltpu.SEMAPHORE)
_EFFECT = pltpu.SideEffectType.DATAFLOW_SIDE_EFFECTING


def _exchange_copies(srcs, lands, send_sems, recv_sems, gather):
    x, y, c, me = _my_pos()
    copies = []
    for k in _PEER_ORDER:
        tgt, tidx = _peer(x, y, c, k)
        for a in range(len(srcs)):
            copies.append(pltpu.make_async_remote_copy(
                src_ref=srcs[a] if gather else srcs[a].at[tidx], dst_ref=lands[a].at[me],
                send_sem=send_sems.at[7 * a + k - 1], recv_sem=recv_sems.at[7 * a + k - 1],
                device_id=tgt, device_id_type=MESH))
    return copies


def _exchange_start(name, srcs, lands=None, after=None):
    n = len(srcs)
    gather = lands is not None
    if lands is None:
        lands = [lax.empty(g.shape, g.dtype) for g in srcs]
    extra = [] if after is None else [after]

    def body(*refs):
        src_refs, land_refs = refs[:n], refs[n:2 * n]
        send_sems, recv_sems = refs[2 * n + len(extra)], refs[2 * n + len(extra) + 1]
        token = refs[-1]
        for cp in _exchange_copies(src_refs, land_refs, send_sems, recv_sems, gather):
            cp.start()
        token[...] = jnp.zeros_like(token)

    hbm = lambda a: pltpu.with_memory_space_constraint(a, pltpu.HBM)
    outs = pl.pallas_call(
        body, name=name,
        out_shape=(pltpu.SemaphoreType.DMA((7 * n,)), pltpu.SemaphoreType.DMA((7 * n,)),
                   *[pltpu.HBM(g.shape, g.dtype) for g in list(srcs) + list(lands)],
                   jax.ShapeDtypeStruct((8, 128), f32)),
        in_specs=[_HBM] * (2 * n) + [pl.BlockSpec(memory_space=pl.ANY)] * len(extra),
        out_specs=(_SEM, _SEM, *([_HBM] * (2 * n)), pl.BlockSpec(memory_space=pltpu.VMEM)),
        input_output_aliases={i: 2 + i for i in range(2 * n)},
        compiler_params=pltpu.CompilerParams(has_side_effects=_EFFECT),
    )(*[hbm(g) for g in srcs], *[hbm(g) for g in lands], *extra)
    return outs[0], outs[1], list(outs[2:2 + n]), list(outs[2 + n:2 + 2 * n]), outs[-1], gather


def _exchange_wait(name, started, after):
    send_sems, recv_sems, srcs, lands, _, gather = started
    n = len(srcs)
    after = list(after) if isinstance(after, (list, tuple)) else [after]

    def body(*refs):
        src_refs, land_refs = refs[:n], refs[n:2 * n]
        s_sems, r_sems = refs[2 * n], refs[2 * n + 1]
        for cp in _exchange_copies(src_refs, land_refs, s_sems, r_sems, gather):
            cp.wait_send()
            cp.wait_recv()

    outs = pl.pallas_call(
        body, name=name,
        out_shape=tuple(pltpu.HBM(a.shape, a.dtype) for a in list(srcs) + list(lands)),
        in_specs=[_HBM] * (2 * n) + [_SEM, _SEM] + [pl.BlockSpec(memory_space=pl.ANY)] * len(after),
        out_specs=tuple([_HBM] * (2 * n)),
        input_output_aliases={i: i for i in range(2 * n)},
        compiler_params=pltpu.CompilerParams(has_side_effects=_EFFECT),
    )(*srcs, *lands, send_sems, recv_sems, *after)
    return list(outs[:n]), list(outs[n:])


SMALL_ROWS = 128


def _small_start(name, sg, after=None):
    return _exchange_start(name, [sg], [lax.empty((N_DEV,) + sg.shape, f32)], after=after)


def _small_sum(name, me, started, after):
    (own,), (slots,) = _exchange_wait(name + "_wait", started, after)

    def body(me_ref, s_ref, own_ref, out_ref):
        acc = None
        for p in range(N_DEV):
            term = lax.cond(me_ref[0] == p, lambda: own_ref[...], lambda p=p: s_ref[p])
            acc = term if acc is None else acc + term
        out_ref[...] = acc

    return pl.pallas_call(
        body, name=name + "_sum",
        in_specs=[pl.BlockSpec(memory_space=pltpu.SMEM), pl.BlockSpec(memory_space=pltpu.VMEM),
                  pl.BlockSpec(memory_space=pltpu.VMEM)],
        out_specs=pl.BlockSpec(memory_space=pltpu.VMEM),
        out_shape=jax.ShapeDtypeStruct(own.shape, f32))(me, slots, own)


def _adam_math(g, w, m, v):
    m = ADAM_B1 * m + (1.0 - ADAM_B1) * g
    v = ADAM_B2 * v + (1.0 - ADAM_B2) * (g * g)
    m_hat = m / (1.0 - ADAM_B1 ** ADAM_STEP)
    v_hat = v / (1.0 - ADAM_B2 ** ADAM_STEP)
    delta = -ADAM_LR * (m_hat / (jnp.sqrt(v_hat) + ADAM_EPS) + ADAM_WD * w)
    return delta, m, v


def _adam_slots(name, me, slots, own, w, m, v, tr, transposed=False):
    rows = slots.shape[1]

    def body(me_ref, s_ref, own_ref, w_ref, m_ref, v_ref, g_ref, d_ref, nm_ref, nv_ref):
        mine = own_ref[...]
        g = None
        for p in range(N_DEV):
            term = lax.cond(me_ref[0] == p, lambda: mine, lambda p=p: s_ref[p]).astype(f32)
            g = term if g is None else g + term
        if transposed:
            g = g.T
        delta, nm, nv = _adam_math(g, w_ref[...], m_ref[...], v_ref[...])
        g_ref[...] = g
        d_ref[...] = delta
        nm_ref[...] = nm
        nv_ref[...] = nv

    if transposed:
        rs = pl.BlockSpec((D, tr), lambda i, me_ref: (0, i))
    else:
        rs = pl.BlockSpec((tr, D), lambda i, me_ref: (i, 0))
    return pl.pallas_call(
        body, name=name,
        grid_spec=pltpu.PrefetchScalarGridSpec(
            num_scalar_prefetch=1, grid=(rows // tr,),
            in_specs=[pl.BlockSpec((N_DEV, tr, D), lambda i, me_ref: (0, i, 0)),
                      pl.BlockSpec((None, tr, D), lambda i, me_ref: (me_ref[0], i, 0)), rs, rs, rs],
            out_specs=[rs] * 4),
        out_shape=[jax.ShapeDtypeStruct(w.shape, f32)] * 4,
        compiler_params=_cparams(("parallel",)))(me, slots, own, w, m, v)


def _adam_small(g, w, m, v):
    def body(g_ref, w_ref, m_ref, v_ref, d_ref, nm_ref, nv_ref):
        delta, nm, nv = _adam_math(g_ref[...], w_ref[...], m_ref[...], v_ref[...])
        d_ref[...] = delta
        nm_ref[...] = nm
        nv_ref[...] = nv

    return pl.pallas_call(body, name="adam_small", out_shape=[jax.ShapeDtypeStruct(g.shape, f32)] * 3)(g, w, m, v)


FFN_PAD = 6 * D


_SMALL_PARTS = (("norm1_g", 1), ("gate_b", 2), ("conv_w", CONV_WIDTH), ("conv_b", 1), ("conv_norm_g", 1),
                ("q_norm_g", 1), ("k_norm_g", 1), ("norm2_g", 1), ("ffn_conv_w", 18), ("ffn_conv_b", 6), ("last", 1))


def _small_offsets():
    out, row = {}, 0
    for name, rows in _SMALL_PARTS:
        out[name] = row
        row += -(-rows // 8) * 8
    assert row == SMALL_ROWS
    return out


def _pack_small(norm1_g, gate_b, conv_w, conv_b, conv_norm_g, q_norm_g, k_norm_g, norm2_g, ffn_conv_w, ffn_conv_b,
                last_row=None):
    pad_h = lambda a: jnp.pad(a, ((0, 0), (0, D - HEAD_DIM)))
    pad_f = lambda a: jnp.pad(a, ((0, 0), (0, FFN_PAD - 2 * D_FF))).reshape(-1, D)
    parts = [norm1_g, gate_b.reshape(2, D), conv_w, conv_b, conv_norm_g, pad_h(q_norm_g), pad_h(k_norm_g), norm2_g,
             pad_f(ffn_conv_w), pad_f(ffn_conv_b), jnp.zeros((1, D), f32) if last_row is None else last_row]
    return jnp.concatenate([jnp.pad(p, ((0, -p.shape[0] % 8), (0, 0))) for p in parts], axis=0)


def _unpack_small(p):
    o = _small_offsets()
    rows = lambda name, n: p[o[name]:o[name] + n]
    ffn = lambda a: a.reshape(-1, FFN_PAD)[:, :2 * D_FF]
    return dict(
        norm1_g=rows("norm1_g", 1), gate_b=rows("gate_b", 2).reshape(1, 2 * D), conv_w=rows("conv_w", CONV_WIDTH),
        conv_b=rows("conv_b", 1), conv_norm_g=rows("conv_norm_g", 1), q_norm_g=rows("q_norm_g", 1)[:, :HEAD_DIM],
        k_norm_g=rows("k_norm_g", 1)[:, :HEAD_DIM], norm2_g=rows("norm2_g", 1),
        ffn_conv_w=ffn(rows("ffn_conv_w", 18)), ffn_conv_b=ffn(rows("ffn_conv_b", 6)))


_ADAM_TILE = {896: 128, 704: 64, 128: 128, 352: 176}


def kernel(x, norm1_g, w_in, gate_b, conv_w, conv_b, conv_norm_g, w_conv_out, q_norm_g, k_norm_g, w_attn_out, w_out, norm2_g, w_up, ffn_conv_w, ffn_conv_b, w_down, loss_target, m_norm1_g, m_w_in, m_gate_b, m_conv_w, m_conv_b, m_conv_norm_g, m_w_conv_out, m_q_norm_g, m_k_norm_g, m_w_attn_out, m_w_out, m_norm2_g, m_w_up, m_ffn_conv_w, m_ffn_conv_b, m_w_down, v_norm1_g, v_w_in, v_gate_b, v_conv_w, v_conv_b, v_conv_norm_g, v_w_conv_out, v_q_norm_g, v_k_norm_g, v_w_attn_out, v_w_out, v_norm2_g, v_w_up, v_ffn_conv_w, v_ffn_conv_b, v_w_down):
    BL, S, _ = x.shape
    T = BL * S
    me = 4 * lax.axis_index("x") + 2 * lax.axis_index("y") + lax.axis_index("c")
    xt = x.reshape(T, D)
    target = loss_target.reshape(T, D)

    big = dict(w_in=(w_in[0], m_w_in[0], v_w_in[0]), w_up=(w_up[0].T, m_w_up[0].T, v_w_up[0].T),
               w_conv_out=(w_conv_out[0], m_w_conv_out[0], v_w_conv_out[0]),
               w_attn_out=(w_attn_out[0], m_w_attn_out[0], v_w_attn_out[0]),
               w_out=(w_out[0], m_w_out[0], v_w_out[0]), w_down=(w_down[0], m_w_down[0], v_w_down[0]))
    order = ["w_in", "w_conv_out", "w_attn_out", "w_out", "w_up", "w_down"]
    shards = [(big[n][0].T if n == "w_in" else big[n][0]).astype(bf16) for n in order]
    gathered = _allgather_rows(shards, 1)
    ga_proj = _exchange_start("gather_start_proj", shards[1:4], gathered[1:4], after=gathered[0])
    ga_ffn = _exchange_start("gather_start_ffn", shards[4:6], gathered[4:6], after=ga_proj[4])
    W = {"w_in": gathered[0].reshape(-1, D)}

    def place_cols(shard, full_cols):
        z = jnp.zeros((shard.shape[0], full_cols), f32)
        return lax.dynamic_update_slice(z, shard, (0, me * shard.shape[1]))

    zr = lambda a: jnp.zeros_like(a)
    conv_local = _pack_small(
        zr(norm1_g), zr(gate_b), place_cols(conv_w[0], D), zr(conv_b), zr(conv_norm_g), zr(q_norm_g), zr(k_norm_g),
        zr(norm2_g), place_cols(ffn_conv_w[0], 2 * D_FF), zr(ffn_conv_b))
    ga_conv = _small_start("gather_conv_start", conv_local, after=ga_ffn[4])

    bd = (jnp.arange(128)[:, None] // HEAD_DIM == jnp.arange(128)[None, :] // HEAD_DIM).astype(bf16)
    bias = _attn_bias()
    qg = jnp.tile(q_norm_g, (1, N_HEADS))
    kg = jnp.tile(k_norm_g, (1, N_HEADS))

    h = _norm1_fwd(xt, norm1_g)
    z8 = _matmul_call(
        "mm_z", h, W["w_in"],
        pl.BlockSpec((2048, D), lambda i, j, k: (i, 0)),
        pl.BlockSpec((1024, D), lambda i, j, k: (_wsec_of_zsec(j), 0)),
        pl.BlockSpec((None, 2048, D), lambda i, j, k: (j, i, 0)),
        jax.ShapeDtypeStruct((8, T, D), f32), (T // 2048, 7, 1), "nt", 1, 2048, 1024, after=ga_conv[4])
    conv_all = _unpack_small(_small_sum("gather_conv", me.reshape(1), ga_conv, z8))
    conv_w_full, ffn_w_full = conv_all["conv_w"], conv_all["ffn_conv_w"]
    c = _conv_fwd(z8, conv_w_full, conv_b, S)
    s = _convnorm_fwd(c, conv_norm_g)
    qn, kn = _qk_fwd(z8, qg, kg, bd)
    for n, g in zip(order[1:4], _exchange_wait("gather_wait_proj", ga_proj, qn)[1]):
        W[n] = g.reshape(-1, D)
    ya = _matmul("mm_ya", s, W["w_conv_out"], "nn", f32)
    o, ob, lse = _attn_fwd(qn, kn, z8, bias, S)
    yb = _matmul("mm_yb", ob, W["w_attn_out"], "nn", f32)
    mixed = _gate_fwd(z8, gate_b, ya, yb)
    x1, h2 = _out_norm2_fwd(mixed, W["w_out"], xt, norm2_g)
    for n, g in zip(order[4:6], _exchange_wait("gather_wait_ffn", ga_ffn, x1)[1]):
        W[n] = g.reshape(-1, D)
    TNU = D_FF // 2
    u3 = _matmul_call(
        "mm_u", h2, W["w_up"],
        pl.BlockSpec((1024, D), lambda i, j, k: (i, 0)),
        pl.BlockSpec((TNU, D), lambda i, j, k: (j, 0)),
        pl.BlockSpec((None, 1024, TNU), lambda i, j, k: (j // 2, i, j % 2)),
        jax.ShapeDtypeStruct((2, T, D_FF), f32), (T // 1024, 4, 1), "nt", 1, 1024, TNU)
    f = _ffn_fwd(u3, ffn_w_full, ffn_conv_b, S)
    dy, dyb, lacc = _down_loss_fwd(f, W["w_down"], x1, target)
    loss_local = 0.5 / D * jnp.sum(lacc)

    df = _matmul("mm_df", dyb, W["w_down"], "nt", f32, tn=TNU)
    g_w_down = _matmul("mm_dwdn", f, dyb, "tn", bf16, tm=TNU)
    du3, dffn = _ffn_bwd(u3, df, ffn_w_full, ffn_conv_b, S)
    g_w_up = _matmul_call(
        "mm_dwup", du3, h2,
        pl.BlockSpec((None, T, TNU), lambda i, j, k: (i // 2, 0, i % 2)),
        pl.BlockSpec((T, D), lambda i, j, k: (0, 0)),
        pl.BlockSpec((TNU, D), lambda i, j, k: (i, 0)),
        jax.ShapeDtypeStruct((2 * D_FF, D), bf16), (4, 1, 1), "tn", 1, TNU, D)
    blocks8 = lambda a: a.reshape(N_DEV, -1, D)
    ex_ffn = _exchange_start("scatter_start_ffn", [blocks8(g_w_up), blocks8(g_w_down)])
    dx1, dx1b, dg_norm2 = _up_norm2_bwd(du3, W["w_up"], x1, dy, norm2_g, ex_ffn[4])
    g_w_out = _matmul("mm_dwo", mixed, dx1b, "tn", bf16, tm=512)
    dz8 = lax.empty((8, T, D), bf16)
    dya, dyb2, dz8, dg_gate = _out_gate_bwd(dx1b, W["w_out"], z8, gate_b, ya, yb, dz8)
    ds = _matmul("mm_ds", dya, W["w_conv_out"], "nt", f32)
    g_w_conv_out = _matmul("mm_dwco", s, dya, "tn", bf16, tm=512)
    g_w_attn_out = _matmul("mm_dwao", ob, dyb2, "tn", bf16, tm=512)
    ex_proj = _exchange_start("scatter_start_proj", [blocks8(g_w_conv_out), blocks8(g_w_attn_out), blocks8(g_w_out)])
    do = _matmul("mm_do", dyb2, W["w_attn_out"], "nt", f32, after=ex_proj[4])
    dc, dg_convnorm = _convnorm_bwd(c, ds, conv_norm_g)
    dz8a, dconv = _conv_bwd(dc, z8, conv_w_full, dz8, S)
    dqn, dkn, dv = _attn_bwd(qn, kn, z8, do, o, lse, bias, bd, S)
    dz8b, dg_q, dg_k = _qk_bwd(z8, dqn, dkn, dv, qg, kg, bd, dz8a)
    g_w_in = _matmul_call(
        "mm_dwin", dz8b, h,
        pl.BlockSpec((None, T, D), lambda i, j, k: (_zsec_of_wsec(i), 0, 0)),
        pl.BlockSpec((T, D), lambda i, j, k: (0, 0)),
        pl.BlockSpec((1024, D), lambda i, j, k: (i, 0)),
        jax.ShapeDtypeStruct((7 * D, D), bf16), (7, 1, 1), "tn", 1, D, D)
    ex_in = _exchange_start("scatter_start_in", [blocks8(g_w_in)])
    grad_x, dg_norm1 = _in_norm1_bwd(dz8b, W["w_in"], xt, dx1, norm1_g, ex_in[4])

    sum8 = lambda a: a.reshape(-1, 8, a.shape[-1]).sum(axis=1)
    dconv_s = sum8(dconv.sum(axis=0))
    dffn_s = dffn.sum(axis=0).reshape(2, 4, 8, D_FF).sum(axis=2)
    dffn_w = jnp.concatenate([dffn_s[0, :3], dffn_s[1, :3]], axis=1)
    dffn_b = jnp.concatenate([dffn_s[0, 3:4], dffn_s[1, 3:4]], axis=1)
    fold = lambda a: sum8(a).reshape(N_HEADS, HEAD_DIM).sum(axis=0)[None]
    small_g_local = _pack_small(
        sum8(dg_norm1), sum8(dg_gate), dconv_s[:CONV_WIDTH], dconv_s[CONV_WIDTH:], sum8(dg_convnorm),
        fold(dg_q), fold(dg_k), sum8(dg_norm2), dffn_w, dffn_b,
        last_row=jnp.pad(loss_local.reshape(1, 1), ((0, 0), (0, D - 1))))
    sg_start = _small_start("small_grads_start", small_g_local)

    own, slots = {}, {}
    for tag, ex, names_ in (("ffn", ex_ffn, ("w_up", "w_down")),
                            ("proj", ex_proj, ("w_conv_out", "w_attn_out", "w_out")), ("in", ex_in, ("w_in",))):
        sent, landed = _exchange_wait("scatter_wait_" + tag, ex, sg_start[4])
        for n, src, land in zip(names_, sent, landed):
            own[n], slots[n] = src, land

    res, adam_done = {}, []
    for n in order:
        w, m, v = big[n]
        outs = _adam_slots("adam_" + n, me.reshape(1), slots[n], own[n], w, m, v, _ADAM_TILE[slots[n].shape[1]],
                           transposed=(n == "w_in"))
        adam_done.append(outs[0])
        if n == "w_up":
            outs = [a.T for a in outs]
        res[n] = [a[None] for a in outs]
    small_g = _small_sum("small_grads", me.reshape(1), sg_start, adam_done)
    loss = small_g[_small_offsets()["last"], 0]

    col = lambda a, width: lax.dynamic_slice(a, (0, me * width), (a.shape[0], width))
    small_w_true = _pack_small(norm1_g, gate_b, conv_w_full, conv_b, conv_norm_g, q_norm_g, k_norm_g, norm2_g,
                               ffn_w_full, ffn_conv_b)
    place_m = lambda a, full: place_cols(a[0], full)
    small_m = _pack_small(m_norm1_g, m_gate_b, place_m(m_conv_w, D), m_conv_b, m_conv_norm_g, m_q_norm_g, m_k_norm_g,
                          m_norm2_g, place_m(m_ffn_conv_w, 2 * D_FF), m_ffn_conv_b)
    small_v = _pack_small(v_norm1_g, v_gate_b, place_m(v_conv_w, D), v_conv_b, v_conv_norm_g, v_q_norm_g, v_k_norm_g,
                          v_norm2_g, place_m(v_ffn_conv_w, 2 * D_FF), v_ffn_conv_b)
    sd, sm, sv = _adam_small(small_g, small_w_true, small_m, small_v)
    for i, packed in enumerate((small_g, sd, sm, sv)):
        u = _unpack_small(packed)
        u["conv_w"] = col(u["conv_w"], D // N_DEV)
        u["ffn_conv_w"] = col(u["ffn_conv_w"], 2 * D_FF // N_DEV)
        for n, a in u.items():
            res.setdefault(n, [None] * 4)[i] = a[None] if n in ("conv_w", "ffn_conv_w") else a

    names = ["norm1_g", "w_in", "gate_b", "conv_w", "conv_b", "conv_norm_g", "w_conv_out", "q_norm_g", "k_norm_g",
             "w_attn_out", "w_out", "norm2_g", "w_up", "ffn_conv_w", "ffn_conv_b", "w_down"]
    out = [loss, grad_x.reshape(BL, S, D)]
    for i in range(4):
        out += [res[n][i] for n in names]
    return tuple(out)
```

```python
import functools

import jax
import jax.numpy as jnp
import numpy as np
from jax import lax
from jax.experimental import pallas as pl
from jax.experimental.pallas import tpu as pltpu

f32 = jnp.float32
bf16 = jnp.bfloat16

D = 1024
N_HEADS = 16
HEAD_DIM = 64
CONV_WIDTH = 31
D_FF = 2816
GROUPS = ((128, 1), (512, 4), (2048, 16))
ATTN_BLOCK = 128
EPS = 1e-6
N_DEV = 8
MESH = pl.DeviceIdType.MESH

ADAM_LR = 0.001
ADAM_B1 = 0.9
ADAM_B2 = 0.999
ADAM_EPS = 1e-08
ADAM_WD = 0.01
ADAM_STEP = 10

VMEM_LIMIT = 56 * 1024 * 1024
MASK_BIAS = 1e30

Z_AVAL, Z_AGATE, Z_GA, Z_GB, Z_Q, Z_K, Z_V = 0, 1, 2, 3, 4, 5, 6


_W_OF_Z = (0, 1, 5, 6, 2, 3, 4)


def _wsec_of_zsec(j):
    return jnp.where(j < 2, j, jnp.where(j < 4, j + 3, j - 2))


def _zsec_of_wsec(w):
    return jnp.where(w < 2, w, jnp.where(w < 5, w + 2, w - 3))


def _sig(x):
    return 1.0 / (1.0 + jnp.exp(-x))


def _colsum8(x):
    return x.reshape(-1, 8, x.shape[-1]).sum(axis=0)


def _cparams(sem):
    return pltpu.CompilerParams(dimension_semantics=sem, vmem_limit_bytes=VMEM_LIMIT)


def _my_pos():
    x, y, c = lax.axis_index("x"), lax.axis_index("y"), lax.axis_index("c")
    return x, y, c, 4 * x + 2 * y + c


_DIMS = {"nn": ((1,), (0,)), "nt": ((1,), (1,)), "tn": ((0,), (0,))}


def _matmul_call(name, a, b, a_spec, b_spec, o_spec, out_shape, grid, mode, nk, tm, tn, after=None, fill=None):
    dims = (_DIMS[mode], ((), ()))
    extra = ([] if after is None else [after]) + ([] if fill is None else [fill])

    def body(a_ref, b_ref, *rest):
        o_ref, scratch = rest[len(extra)], rest[len(extra) + 1:]
        part = lax.dot_general(a_ref[...], b_ref[...], dims, preferred_element_type=f32)
        if nk == 1:
            o_ref[...] = part.astype(o_ref.dtype)
        else:
            acc = scratch[0]
            k = pl.program_id(2)

            @pl.when(k == 0)
            def _():
                acc[...] = part

            @pl.when(k > 0)
            def _():
                acc[...] += part

            @pl.when(k == nk - 1)
            def _():
                o_ref[...] = acc[...].astype(o_ref.dtype)

    scratch = [] if nk == 1 else [pltpu.VMEM((tm, tn), f32)]
    return pl.pallas_call(
        body, name=name, grid=grid, in_specs=[a_spec, b_spec] + [pl.BlockSpec(memory_space=pl.ANY)] * len(extra),
        out_specs=o_spec, out_shape=out_shape, input_output_aliases={} if fill is None else {1 + len(extra): 0},
        scratch_shapes=scratch, compiler_params=_cparams(("parallel", "parallel", "arbitrary")),
    )(a, b, *extra)


def _matmul(name, a, b, mode, out_dtype, tm=1024, tn=1024, tk=None, after=None):
    if mode == "nn":
        (M, K), (_, N) = a.shape, b.shape
    elif mode == "nt":
        (M, K), (N, _) = a.shape, b.shape
    else:
        (K, M), (_, N) = a.shape, b.shape
    tm, tn = min(tm, M), min(tn, N)
    tk = K if tk is None else tk
    nk = K // tk
    assert M % tm == 0 and N % tn == 0 and K % tk == 0
    if mode == "tn":
        a_spec = pl.BlockSpec((tk, tm), lambda i, j, k: (k, i))
    else:
        a_spec = pl.BlockSpec((tm, tk), lambda i, j, k: (i, k))
    if mode == "nt":
        b_spec = pl.BlockSpec((tn, tk), lambda i, j, k: (j, k))
    else:
        b_spec = pl.BlockSpec((tk, tn), lambda i, j, k: (k, j))
    o_spec = pl.BlockSpec((tm, tn), lambda i, j, k: (i, j))
    return _matmul_call(name, a, b, a_spec, b_spec, o_spec, jax.ShapeDtypeStruct((M, N), out_dtype),
                        (M // tm, N // tn, nk), mode, nk, tm, tn, after=after)


FTM = 512


def _matmul_fused(name, a, b, pairs, epilogue, extras, consts, outs, nt=False, sums=False, passed=(), aliases=None):
    sa, M, kk = a.shape
    na = max(i for i, _ in pairs) + 1
    ne, nc, npass = len(extras), len(consts), len(passed)
    dims = (_DIMS["nt" if nt else "nn"], ((), ()))

    def body(a_ref, b_ref, *rest):
        acc = None
        for i, j in pairs:
            part = lax.dot_general(a_ref[i], b_ref[j], dims, preferred_element_type=f32)
            acc = part if acc is None else acc + part
        epilogue(acc, rest[:ne], rest[ne:ne + nc], rest[ne + nc + npass:])

    whole = lambda arr: pl.BlockSpec(arr.shape, lambda i, nd=arr.ndim: (0,) * nd, pipeline_mode=pl.Buffered(1))
    io_alias = {2 + ne + nc + k: v for k, v in (aliases or {}).items()}
    return pl.pallas_call(
        body, name=name, grid=(M // FTM,),
        in_specs=[pl.BlockSpec((na, FTM, kk), lambda i: (0, i, 0)), whole(b)] + [s for _, s in extras]
        + [whole(c) for c in consts] + [pl.BlockSpec(memory_space=pl.ANY)] * npass,
        out_specs=[s for _, s in outs], out_shape=[s for s, _ in outs], input_output_aliases=io_alias,
        compiler_params=_cparams(("arbitrary" if sums else "parallel",)),
    )(a, b, *[x for x, _ in extras], *consts, *passed)


def _frows(c=D):
    return pl.BlockSpec((FTM, c), lambda i: (i, 0))


def _fsec(s):
    return pl.BlockSpec((None, FTM, D), lambda i: (s, i, 0))


def _rowshape(T, dtype, c=D):
    return (jax.ShapeDtypeStruct((T, c), dtype), _frows(c))


def _sumshape(c=D):
    return (jax.ShapeDtypeStruct((8, c), f32), pl.BlockSpec((8, c), lambda i: (0, 0)))


def _add_colsum(ref, x, cols=None):
    @pl.when(pl.program_id(0) == 0)
    def _():
        if cols is None:
            ref[...] = jnp.zeros_like(ref)
        else:
            ref[:, cols] = jnp.zeros((8, x.shape[-1]), f32)

    if cols is None:
        ref[...] += _colsum8(x)
    else:
        ref[:, cols] += _colsum8(x)


TT = 512


def _rows(c, cb=0, tt=TT):
    return pl.BlockSpec((tt, c), lambda i: (i, cb))


def _sec(s, tt=TT):
    return pl.BlockSpec((None, tt, D), lambda i: (s, i, 0))


def _const(shape):
    return pl.BlockSpec(shape, lambda i: (0,) * len(shape))


def _acc_spec(c):
    return pl.BlockSpec((8, c), lambda i: (0, 0))


def _rms(x):
    return lax.rsqrt(jnp.mean(x * x, axis=-1, keepdims=True) + EPS)


def _rms_bwd(dy_g, xn, rstd):
    return rstd * (dy_g - xn * jnp.mean(dy_g * xn, axis=-1, keepdims=True))


def _head_sum(x, bd):
    parts = []
    for cb in range(x.shape[-1] // 128):
        xb = x[:, cb * 128:(cb + 1) * 128]
        hi = xb.astype(bf16)
        lo = (xb - hi.astype(f32)).astype(bf16)
        parts.append(jnp.dot(hi, bd, preferred_element_type=f32) + jnp.dot(lo, bd, preferred_element_type=f32))
    return parts[0] if len(parts) == 1 else jnp.concatenate(parts, axis=1)


def _norm1_fwd(x, g):
    T = x.shape[0]

    def body(x_ref, g_ref, h_ref):
        xv = x_ref[...]
        h_ref[...] = (xv * _rms(xv) * g_ref[...]).astype(bf16)

    return pl.pallas_call(
        body, name="norm1_fwd", grid=(T // TT,), in_specs=[_rows(D), _const((1, D))], out_specs=_rows(D),
        out_shape=jax.ShapeDtypeStruct((T, D), bf16), compiler_params=_cparams(("parallel",)))(x, g)


def _convnorm_fwd(c, g):
    T = c.shape[0]

    def body(c_ref, g_ref, s_ref):
        cv = c_ref[...]
        r = cv * _rms(cv) * g_ref[...]
        s_ref[...] = (r * _sig(r)).astype(bf16)

    return pl.pallas_call(
        body, name="convnorm_fwd", grid=(T // TT,), in_specs=[_rows(D), _const((1, D))], out_specs=_rows(D),
        out_shape=jax.ShapeDtypeStruct((T, D), bf16), compiler_params=_cparams(("parallel",)))(c, g)


def _qk_fwd(z8, qg, kg, bd):
    T = z8.shape[1]

    def body(q_ref, k_ref, qg_ref, kg_ref, bd_ref, qn_ref, kn_ref):
        bdv = bd_ref[...]
        q = q_ref[...]
        qn_ref[...] = q * lax.rsqrt(_head_sum(q * q, bdv) * (1.0 / HEAD_DIM) + EPS) * qg_ref[...] * (HEAD_DIM ** -0.5)
        k = k_ref[...]
        kn_ref[...] = k * lax.rsqrt(_head_sum(k * k, bdv) * (1.0 / HEAD_DIM) + EPS) * kg_ref[...]

    return pl.pallas_call(
        body, name="qk_fwd", grid=(T // TT,),
        in_specs=[_sec(Z_Q), _sec(Z_K), _const((1, D)), _const((1, D)), _const((128, 128))],
        out_specs=[_rows(D), _rows(D)],
        out_shape=[jax.ShapeDtypeStruct((T, D), f32)] * 2, compiler_params=_cparams(("parallel",)))(z8, z8, qg, kg, bd)


def _gate_fwd(z8, gate_b, ya, yb):
    T = ya.shape[0]

    def body(ga_ref, gb_ref, b_ref, ya_ref, yb_ref, mixed_ref):
        g_a = _sig(ga_ref[...] + b_ref[:, :D])
        g_b = _sig(gb_ref[...] + b_ref[:, D:])
        mixed_ref[...] = (g_a * ya_ref[...] + g_b * yb_ref[...]).astype(bf16)

    return pl.pallas_call(
        body, name="gate_fwd", grid=(T // TT,),
        in_specs=[_sec(Z_GA), _sec(Z_GB), _const((1, 2 * D)), _rows(D), _rows(D)], out_specs=_rows(D),
        out_shape=jax.ShapeDtypeStruct((T, D), bf16), compiler_params=_cparams(("parallel",)))(z8, z8, gate_b, ya, yb)


def _out_norm2_fwd(mixed, w_out, x, g):
    T = x.shape[0]

    def epilogue(acc, extra, const, out):
        x1 = extra[0][...] + acc
        out[0][...] = x1
        out[1][...] = (x1 * _rms(x1) * const[0][...]).astype(bf16)

    return _matmul_fused("mm_t1_norm2", mixed[None], w_out[None], ((0, 0),), epilogue, [(x, _frows())], [g],
                         [_rowshape(T, f32), _rowshape(T, bf16)])


def _down_loss_fwd(f, w_down, x1, target):
    T = x1.shape[0]

    def epilogue(acc, extra, const, out):
        diff = extra[0][...] + acc - extra[1][...]
        dy = diff * (1.0 / D)
        out[0][...] = dy
        out[1][...] = dy.astype(bf16)
        _add_colsum(out[2], diff * diff)

    return _matmul_fused("mm_t2_loss", f[None], w_down[None], ((0, 0),), epilogue, [(x1, _frows()), (target, _frows())],
                         [], [_rowshape(T, f32), _rowshape(T, bf16), _sumshape()], sums=True)


def _up_norm2_bwd(du3, w_up_t, x1, dy, g, token):
    T = x1.shape[0]

    def epilogue(dh, extra, const, out):
        x1v = extra[0][...]
        rstd = _rms(x1v)
        xn = x1v * rstd
        dx1 = extra[1][...] + _rms_bwd(dh * const[0][...], xn, rstd)
        out[0][...] = dx1
        out[1][...] = dx1.astype(bf16)
        _add_colsum(out[2], dh * xn)

    return _matmul_fused("mm_dh2_norm2", du3, w_up_t.reshape(2, D_FF, D), ((0, 0), (1, 1)), epilogue,
                         [(x1, _frows()), (dy, _frows())], [g],
                         [_rowshape(T, f32), _rowshape(T, bf16), _sumshape()], sums=True, passed=[token])


def _out_gate_bwd(dx1b, w_out, z8, gate_b, ya, yb, dz8):
    T = ya.shape[0]

    def epilogue(dm, extra, const, out):
        b_ref = const[0]
        g_a = _sig(extra[0][...] + b_ref[:, :D])
        g_b = _sig(extra[1][...] + b_ref[:, D:])
        out[0][...] = (dm * g_a).astype(bf16)
        out[1][...] = (dm * g_b).astype(bf16)
        dla = dm * extra[2][...] * g_a * (1.0 - g_a)
        dlb = dm * extra[3][...] * g_b * (1.0 - g_b)
        out[2][0] = dla.astype(bf16)
        out[2][1] = dlb.astype(bf16)
        _add_colsum(out[3], dla, slice(0, D))
        _add_colsum(out[3], dlb, slice(D, 2 * D))

    return _matmul_fused(
        "mm_dmixed_gate", dx1b[None], w_out[None], ((0, 0),), epilogue,
        [(z8, _fsec(Z_GA)), (z8, _fsec(Z_GB)), (ya, _frows()), (yb, _frows())], [gate_b],
        [_rowshape(T, bf16), _rowshape(T, bf16),
         (jax.ShapeDtypeStruct(dz8.shape, bf16), pl.BlockSpec((2, FTM, D), lambda i: (1, i, 0))), _sumshape(2 * D)],
        nt=True, sums=True, passed=[dz8], aliases={0: 2})


def _convnorm_bwd(c, ds, g):
    T = c.shape[0]

    def body(c_ref, ds_ref, g_ref, dc_ref, dg_ref):
        cv = c_ref[...]
        rstd = _rms(cv)
        r0 = cv * rstd
        gv = g_ref[...]
        r = r0 * gv
        sg = _sig(r)
        dr = ds_ref[...] * sg * (1.0 + r * (1.0 - sg))
        dc_ref[...] = _rms_bwd(dr * gv, r0, rstd)

        @pl.when(pl.program_id(0) == 0)
        def _():
            dg_ref[...] = jnp.zeros_like(dg_ref)

        dg_ref[...] += _colsum8(dr * r0)

    return pl.pallas_call(
        body, name="convnorm_bwd", grid=(T // TT,), in_specs=[_rows(D), _rows(D), _const((1, D))],
        out_specs=[_rows(D), _acc_spec(D)],
        out_shape=[jax.ShapeDtypeStruct((T, D), f32), jax.ShapeDtypeStruct((8, D), f32)],
        compiler_params=_cparams(("arbitrary",)))(c, ds, g)


def _qk_bwd(z8, dqn, dkn, dv, qg, kg, bd, dz8):
    T = dqn.shape[0]

    def body(q_ref, k_ref, dqn_ref, dkn_ref, dv_ref, qg_ref, kg_ref, bd_ref, dz_in, dz_ref, dqg_ref, dkg_ref):
        del dz_in
        bdv = bd_ref[...]

        @pl.when(pl.program_id(0) == 0)
        def _():
            dqg_ref[...] = jnp.zeros_like(dqg_ref)
            dkg_ref[...] = jnp.zeros_like(dkg_ref)

        def one(raw, dn_scaled, g, dg_ref, sec):
            rstd = lax.rsqrt(_head_sum(raw * raw, bdv) * (1.0 / HEAD_DIM) + EPS)
            n = raw * rstd
            dg_ref[...] += _colsum8(dn_scaled * n)
            dn = dn_scaled * g
            draw = rstd * (dn - n * (_head_sum(dn * n, bdv) * (1.0 / HEAD_DIM)))
            dz_ref[sec] = draw.astype(bf16)

        one(q_ref[...], dqn_ref[...] * (HEAD_DIM ** -0.5), qg_ref[...], dqg_ref, 0)
        one(k_ref[...], dkn_ref[...], kg_ref[...], dkg_ref, 1)
        dz_ref[2] = dv_ref[...].astype(bf16)
        dz_ref[3] = jnp.zeros((TT, D), bf16)

    return pl.pallas_call(
        body, name="qk_bwd", grid=(T // TT,),
        in_specs=[_sec(Z_Q), _sec(Z_K), _rows(D), _rows(D), _rows(D), _const((1, D)), _const((1, D)),
                  _const((128, 128)), pl.BlockSpec(memory_space=pl.ANY)],
        out_specs=[pl.BlockSpec((4, TT, D), lambda i: (1, i, 0)), _acc_spec(D), _acc_spec(D)],
        out_shape=[jax.ShapeDtypeStruct(dz8.shape, bf16), jax.ShapeDtypeStruct((8, D), f32),
                   jax.ShapeDtypeStruct((8, D), f32)],
        input_output_aliases={8: 0},
        compiler_params=_cparams(("arbitrary",)))(z8, z8, dqn, dkn, dv, qg, kg, bd, dz8)


def _in_norm1_bwd(dz8, w_in_t, x, dx1, g, token):
    T = x.shape[0]

    def epilogue(dh, extra, const, out):
        xv = extra[0][...]
        rstd = _rms(xv)
        xn = xv * rstd
        out[0][...] = extra[1][...] + _rms_bwd(dh * const[0][...], xn, rstd)
        _add_colsum(out[1], dh * xn)

    return _matmul_fused("mm_dh_norm1", dz8, w_in_t.reshape(7, D, D), tuple(zip(range(7), _W_OF_Z)), epilogue,
                         [(x, _frows()), (dx1, _frows())], [g], [_rowshape(T, f32), _sumshape()],
                         sums=True, passed=[token])


CCW = 256
CR = 64
HALO = 32


def _conv_fwd(z8, conv_w, conv_b, S):
    T = z8.shape[1]
    nb = T // S
    ncb = D // CCW

    def body(av_ref, ag_ref, w_ref, b_ref, c_ref, pad):
        pad[0:HALO, :] = jnp.zeros((HALO, CCW), f32)

        def fill(i, carry):
            r0 = pl.multiple_of(i * 256, 256)
            pad[pl.ds(HALO + r0, 256), :] = av_ref[pl.ds(r0, 256), :] * _sig(ag_ref[pl.ds(r0, 256), :])
            return carry

        lax.fori_loop(0, S // 256, fill, 0)
        bias = b_ref[...]

        def chunk(i, carry):
            r0 = pl.multiple_of(i * CR, CR)
            win = pad[pl.ds(r0, CR + HALO), :]
            acc = jnp.zeros((CR, CCW), f32) + bias
            for s in range(8):
                part = None
                for m in range((CONV_WIDTH - 1 - s) // 8 + 1):
                    j = CONV_WIDTH - 1 - 8 * m - s
                    term = win[24 - 8 * m:24 - 8 * m + CR + 8, :] * w_ref[j:j + 1, :]
                    part = term if part is None else part + term
                acc = acc + part[8 - s:8 - s + CR, :]
            c_ref[pl.ds(r0, CR), :] = acc
            return carry

        lax.fori_loop(0, S // CR, chunk, 0)

    zs = lambda s: pl.BlockSpec((None, S, CCW), lambda b, cb: (s, b, cb))
    return pl.pallas_call(
        body, name="conv_fwd", grid=(nb, ncb),
        in_specs=[zs(Z_AVAL), zs(Z_AGATE), pl.BlockSpec((CONV_WIDTH, CCW), lambda b, cb: (0, cb)),
                  pl.BlockSpec((1, CCW), lambda b, cb: (0, cb))],
        out_specs=pl.BlockSpec((S, CCW), lambda b, cb: (b, cb)),
        out_shape=jax.ShapeDtypeStruct((T, D), f32),
        scratch_shapes=[pltpu.VMEM((S + HALO, CCW), f32)],
        compiler_params=_cparams(("parallel", "parallel")))(z8, z8, conv_w, conv_b)


def _conv_bwd(dc, z8, conv_w, dz8, S):
    T = dc.shape[0]
    nb = T // S
    ncb = D // CCW

    def body(dc_ref, av_ref, ag_ref, w_ref, dz_in, dz_ref, dw_ref, apad, dpad, shbuf):
        del dz_in
        apad[0:HALO, :] = jnp.zeros((HALO, CCW), f32)
        dpad[S:S + HALO, :] = jnp.zeros((HALO, CCW), f32)
        dw_ref[...] = jnp.zeros_like(dw_ref)

        def fill(i, carry):
            r0 = pl.multiple_of(i * 256, 256)
            apad[pl.ds(HALO + r0, 256), :] = av_ref[pl.ds(r0, 256), :] * _sig(ag_ref[pl.ds(r0, 256), :])
            dpad[pl.ds(r0, 256), :] = dc_ref[pl.ds(r0, 256), :]
            return carry

        lax.fori_loop(0, S // 256, fill, 0)

        def chunk(i, carry):
            r0 = pl.multiple_of(i * CR, CR)
            dwin = dpad[pl.ds(r0, CR + HALO), :]
            da = jnp.zeros((CR, CCW), f32)
            for s in range(8):
                shbuf[...] = dwin[s:s + CR, :]
                dshift = shbuf[...]
                part = None
                for m in range((CONV_WIDTH - 1 - s) // 8 + 1):
                    j = CONV_WIDTH - 1 - 8 * m - s
                    term = dwin[8 * m:8 * m + CR + 8, :] * w_ref[j:j + 1, :]
                    part = term if part is None else part + term
                    a_lag = apad[pl.ds(r0 + HALO - 8 * m, CR), :]
                    dw_ref[8 * j:8 * j + 8, :] += _colsum8(dshift * a_lag)
                da = da + part[s:s + CR, :]
            dw_ref[8 * CONV_WIDTH:8 * CONV_WIDTH + 8, :] += _colsum8(dwin[0:CR, :])
            av = av_ref[pl.ds(r0, CR), :]
            sg = _sig(ag_ref[pl.ds(r0, CR), :])
            dz_ref[0, pl.ds(r0, CR), :] = (da * sg).astype(bf16)
            dz_ref[1, pl.ds(r0, CR), :] = (da * av * sg * (1.0 - sg)).astype(bf16)
            return carry

        lax.fori_loop(0, S // CR, chunk, 0)

    zs = lambda s: pl.BlockSpec((None, S, CCW), lambda b, cb: (s, b, cb))
    return pl.pallas_call(
        body, name="conv_bwd", grid=(nb, ncb),
        in_specs=[pl.BlockSpec((S, CCW), lambda b, cb: (b, cb)), zs(Z_AVAL), zs(Z_AGATE),
                  pl.BlockSpec((CONV_WIDTH, CCW), lambda b, cb: (0, cb)), pl.BlockSpec(memory_space=pl.ANY)],
        out_specs=[pl.BlockSpec((2, S, CCW), lambda b, cb: (0, b, cb)),
                   pl.BlockSpec((None, 256, CCW), lambda b, cb: (b, 0, cb))],
        out_shape=[jax.ShapeDtypeStruct(dz8.shape, bf16), jax.ShapeDtypeStruct((nb, 256, D), f32)],
        input_output_aliases={4: 0},
        scratch_shapes=[pltpu.VMEM((S + HALO, CCW), f32), pltpu.VMEM((S + HALO, CCW), f32),
                        pltpu.VMEM((CR, CCW), f32)],
        compiler_params=_cparams(("parallel", "parallel")))(dc, z8, z8, conv_w, dz8)


FR = 128
NFB = D_FF // CCW


def _ffn_window(ref, i, r0):
    return ref[pl.ds(r0 - 8, FR + 8), :]


def _ffn_u(win, w_ref, b_ref):
    return (win[6:6 + FR, :] * w_ref[0:1, :] + win[7:7 + FR, :] * w_ref[1:2, :]
            + win[8:8 + FR, :] * w_ref[2:3, :] + b_ref[...])


def _ffn_fwd(u3, ffn_w, ffn_b, S):
    T = u3.shape[1]
    nb = T // S

    def body(uv_ref, ug_ref, wv_ref, wg_ref, bv_ref, bg_ref, f_ref):
        def chunk(first, i):
            r0 = 0 if first else pl.multiple_of(i * FR, FR)
            if first:
                z = jnp.zeros((8, CCW), f32)
                wv = jnp.concatenate([z, uv_ref[0:FR, :]], axis=0)
                wg = jnp.concatenate([z, ug_ref[0:FR, :]], axis=0)
            else:
                wv = _ffn_window(uv_ref, i, r0)
                wg = _ffn_window(ug_ref, i, r0)
            u_val = _ffn_u(wv, wv_ref, bv_ref)
            u_gate = _ffn_u(wg, wg_ref, bg_ref)
            f_ref[pl.ds(r0, FR), :] = (u_gate * _sig(u_gate) * u_val).astype(bf16)

        chunk(True, 0)

        def loop(i, carry):
            chunk(False, i)
            return carry

        lax.fori_loop(1, S // FR, loop, 0)

    us = lambda h: pl.BlockSpec((None, S, CCW), lambda b, cb: (h, b, cb))
    ws = lambda h: pl.BlockSpec((3, CCW), lambda b, cb: (0, h * NFB + cb))
    bs = lambda h: pl.BlockSpec((1, CCW), lambda b, cb: (0, h * NFB + cb))
    return pl.pallas_call(
        body, name="ffn_fwd", grid=(nb, NFB),
        in_specs=[us(0), us(1), ws(0), ws(1), bs(0), bs(1)],
        out_specs=pl.BlockSpec((S, CCW), lambda b, cb: (b, cb)),
        out_shape=jax.ShapeDtypeStruct((T, D_FF), bf16),
        compiler_params=_cparams(("parallel", "parallel")))(u3, u3, ffn_w, ffn_w, ffn_b, ffn_b)


def _ffn_bwd(u3, df, ffn_w, ffn_b, S):
    T = u3.shape[1]
    nb = T // S

    def body(uv_ref, ug_ref, df_ref, wv_ref, wg_ref, bv_ref, bg_ref, du_ref, dw_ref, dvpad, dgpad, shbuf):
        dvpad[S:S + 8, :] = jnp.zeros((8, CCW), f32)
        dgpad[S:S + 8, :] = jnp.zeros((8, CCW), f32)
        dw_ref[...] = jnp.zeros_like(dw_ref)

        def chunk(first, i):
            r0 = 0 if first else pl.multiple_of(i * FR, FR)
            if first:
                z = jnp.zeros((8, CCW), f32)
                wv = jnp.concatenate([z, uv_ref[0:FR, :]], axis=0)
                wg = jnp.concatenate([z, ug_ref[0:FR, :]], axis=0)
            else:
                wv = _ffn_window(uv_ref, i, r0)
                wg = _ffn_window(ug_ref, i, r0)
            taps = []
            for h, win in enumerate((wv, wg)):
                shbuf[2 * h] = win[6:6 + FR, :]
                shbuf[2 * h + 1] = win[7:7 + FR, :]
                taps.append((shbuf[2 * h], shbuf[2 * h + 1], win[8:8 + FR, :]))
            conv = lambda x, w_ref, b_ref: (x[0] * w_ref[0:1, :] + x[1] * w_ref[1:2, :] + x[2] * w_ref[2:3, :]
                                            + b_ref[...])
            u_val = conv(taps[0], wv_ref, bv_ref)
            u_gate = conv(taps[1], wg_ref, bg_ref)
            dfc = df_ref[pl.ds(r0, FR), :]
            sg = _sig(u_gate)
            d_val = dfc * u_gate * sg
            d_gate = dfc * u_val * sg * (1.0 + u_gate * (1.0 - sg))
            dvpad[pl.ds(r0, FR), :] = d_val
            dgpad[pl.ds(r0, FR), :] = d_gate
            for h, dd in enumerate((d_val, d_gate)):
                for j in range(3):
                    dw_ref[h, 8 * j:8 * j + 8, :] += _colsum8(dd * taps[h][j])
                dw_ref[h, 24:32, :] += _colsum8(dd)

        chunk(True, 0)

        def loop(i, carry):
            chunk(False, i)
            return carry

        lax.fori_loop(1, S // FR, loop, 0)

        def back(i, carry):
            r0 = pl.multiple_of(i * FR, FR)
            for h, (dpad, w_ref) in enumerate(((dvpad, wv_ref), (dgpad, wg_ref))):
                win = dpad[pl.ds(r0, FR + 8), :]
                du = (win[0:FR, :] * w_ref[2:3, :] + win[1:1 + FR, :] * w_ref[1:2, :]
                      + win[2:2 + FR, :] * w_ref[0:1, :])
                du_ref[h, pl.ds(r0, FR), :] = du.astype(bf16)
            return carry

        lax.fori_loop(0, S // FR, back, 0)

    us = lambda h: pl.BlockSpec((None, S, CCW), lambda b, cb: (h, b, cb))
    ws = lambda h: pl.BlockSpec((3, CCW), lambda b, cb: (0, h * NFB + cb))
    bs = lambda h: pl.BlockSpec((1, CCW), lambda b, cb: (0, h * NFB + cb))
    return pl.pallas_call(
        body, name="ffn_bwd", grid=(nb, NFB),
        in_specs=[us(0), us(1), pl.BlockSpec((S, CCW), lambda b, cb: (b, cb)), ws(0), ws(1), bs(0), bs(1)],
        out_specs=[pl.BlockSpec((2, S, CCW), lambda b, cb: (0, b, cb)),
                   pl.BlockSpec((None, 2, 32, CCW), lambda b, cb: (b, 0, 0, cb))],
        out_shape=[jax.ShapeDtypeStruct((2, T, D_FF), bf16), jax.ShapeDtypeStruct((nb, 2, 32, D_FF), f32)],
        scratch_shapes=[pltpu.VMEM((S + 8, CCW), f32), pltpu.VMEM((S + 8, CCW), f32),
                        pltpu.VMEM((4, FR, CCW), f32)],
        compiler_params=_cparams(("parallel", "parallel")))(u3, u3, df, ffn_w, ffn_w, ffn_b, ffn_b)


AB = ATTN_BLOCK


def _attn_bias():
    slopes = (np.float32(2.0) ** (np.float32(-8.0) * np.arange(1, N_HEADS + 1, dtype=np.float32)
                                  / np.float32(N_HEADS))).astype(np.float32)
    steps = (np.arange(AB)[:, None] + AB) - np.arange(2 * AB)[None, :]
    own = (np.arange(2 * AB) >= AB)[None, :]
    out = []
    for window, dil in GROUPS:
        valid = (steps >= 0) & (steps <= window // dil)
        dist = slopes[:, None, None] * (steps * dil).astype(np.float32)[None]
        kinds = [np.where(v[None], dist, np.float32(MASK_BIAS)) for v in (valid, valid & own)]
        out.append(np.stack(kinds, axis=1))
    return jnp.asarray(np.stack(out).astype(np.float32))


def _head_masks():
    lane = lax.broadcasted_iota(jnp.int32, (1, 128), 1)
    return (lane < HEAD_DIM, lane >= HEAD_DIM)


def _perm_chunks(S, d):
    L = S // d
    ch = min(L, 256)
    out = []
    for r in range(d):
        for c in range(L // ch):
            start = r + d * ch * c
            out.append((pl.ds(start, ch, stride=d) if d > 1 else pl.ds(start, ch), r * L + c * ch, ch))
    return out


def _stack_heads(x, masks):
    return jnp.concatenate([jnp.where(masks[0], x, 0), jnp.where(masks[1], x, 0)], axis=0)


_NT = (((1,), (1,)), ((), ()))
_TN = (((0,), (0,)), ((), ()))
SCH = 32


def _attn_fwd(qn, kn, z8, bias, S):
    T = qn.shape[0]
    nb = T // S
    nblk = S // AB

    def body(q_ref, k_ref, v_ref, bias_ref, o_ref, ob_ref, lse_ref, qs, ks, vs, s2, p2, ogp, lgp, *group_scratch):
        og, lg = group_scratch[:3], group_scratch[3:]
        masks = _head_masks()
        ks[0:AB, :] = jnp.zeros((AB, 128), bf16)
        vs[0:AB, :] = jnp.zeros((AB, 128), bf16)

        for g, (_, d) in enumerate(GROUPS):
            nsub = S // (d * AB)
            chunks = _perm_chunks(S, d)
            for src, dst, ch in chunks:
                qs[dst:dst + ch, :] = q_ref[src, :].astype(bf16)
                ks[AB + dst:AB + dst + ch, :] = k_ref[src, :].astype(bf16)
                vs[AB + dst:AB + dst + ch, :] = v_ref[src, :].astype(bf16)
            od, ld = (og[g], lg[g]) if d == 1 else (ogp, lgp)

            def scores(j, carry):
                r0 = pl.multiple_of(j * AB, AB)
                q2 = _stack_heads(qs[pl.ds(r0, AB), :], masks)
                s2[j] = lax.dot_general(q2, ks[pl.ds(r0, 2 * AB), :], _NT, preferred_element_type=f32)
                return carry

            lax.fori_loop(0, nblk, scores, 0, unroll=8)

            def softmax(j, carry, g=g, nsub=nsub, ld=ld):
                r0 = pl.multiple_of(j * AB, AB)
                kind = (j % nsub == 0).astype(jnp.int32)
                for cc in range(AB // SCH):
                    lses = []
                    for hh in range(2):
                        rows = pl.ds(hh * AB + cc * SCH, SCH)
                        sb = s2[j, rows, :] - bias_ref[g, hh, kind, cc * SCH:(cc + 1) * SCH, :]
                        m = jnp.max(sb, axis=-1, keepdims=True)
                        p = jnp.exp(sb - m)
                        den = jnp.sum(p, axis=-1, keepdims=True)
                        p2[j, rows, :] = (p * (1.0 / den)).astype(bf16)
                        lses.append(m + jnp.log(den))
                    ld[pl.ds(r0 + cc * SCH, SCH), :] = jnp.where(masks[0], lses[0], lses[1])
                return carry

            lax.fori_loop(0, nblk, softmax, 0, unroll=2)

            def values(j, carry, od=od):
                r0 = pl.multiple_of(j * AB, AB)
                pv2 = jnp.dot(p2[j], vs[pl.ds(r0, 2 * AB), :], preferred_element_type=f32)
                od[pl.ds(r0, AB), :] = jnp.where(masks[0], pv2[:AB], pv2[AB:])
                return carry

            lax.fori_loop(0, nblk, values, 0, unroll=8)

            if d > 1:
                for src, dst, ch in chunks:
                    og[g][src, :] = ogp[dst:dst + ch, :]
                    lg[g][src, :] = lgp[dst:dst + ch, :]

        def combine(i, carry):
            rr = pl.ds(pl.multiple_of(i * 256, 256), 256)
            l0, l1, l2 = lg[0][rr, :], lg[1][rr, :], lg[2][rr, :]
            mx = jnp.maximum(jnp.maximum(l0, l1), l2)
            e0, e1, e2 = jnp.exp(l0 - mx), jnp.exp(l1 - mx), jnp.exp(l2 - mx)
            den = e0 + e1 + e2
            o = (e0 * og[0][rr, :] + e1 * og[1][rr, :] + e2 * og[2][rr, :]) / den
            o_ref[rr, :] = o
            ob_ref[rr, :] = o.astype(bf16)
            lse_ref[rr, :] = mx + jnp.log(den)
            return carry

        lax.fori_loop(0, S // 256, combine, 0)

    blk = pl.BlockSpec((S, 128), lambda b, hp: (b, hp))
    return pl.pallas_call(
        body, name="attn_fwd", grid=(nb, N_HEADS // 2),
        in_specs=[blk, blk, pl.BlockSpec((None, S, 128), lambda b, hp: (Z_V, b, hp)),
                  pl.BlockSpec((3, 2, 2, AB, 2 * AB), lambda b, hp: (0, hp, 0, 0, 0))],
        out_specs=[blk, blk, blk],
        out_shape=[jax.ShapeDtypeStruct((T, D), f32), jax.ShapeDtypeStruct((T, D), bf16),
                   jax.ShapeDtypeStruct((T, D), f32)],
        scratch_shapes=[pltpu.VMEM((S, 128), bf16), pltpu.VMEM((S + AB, 128), bf16), pltpu.VMEM((S + AB, 128), bf16),
                        pltpu.VMEM((nblk, 2 * AB, 2 * AB), f32), pltpu.VMEM((nblk, 2 * AB, 2 * AB), bf16),
                        pltpu.VMEM((S, 128), f32), pltpu.VMEM((S, 128), f32)] + [pltpu.VMEM((S, 128), f32)] * 6,
        compiler_params=_cparams(("parallel", "parallel")))(qn, kn, z8, bias)


def _attn_bwd(qn, kn, z8, do, o, lse, bias, bd, S, after):
    T = qn.shape[0]
    nb = T // S

    nblk = S // AB

    def body(q_ref, k_ref, v_ref, do_ref, o_ref, lse_ref, bias_ref, bd_ref, after_ref, dq_ref, dk_ref, dv_ref,
             delta, qs, ks, vs, dos, lsp, dlp, s2, dp2, p2, ds2, dqp, dkp, dvp):
        del after_ref
        masks = _head_masks()
        bdv = bd_ref[...]
        dq_ref[...] = jnp.zeros_like(dq_ref)
        dk_ref[...] = jnp.zeros_like(dk_ref)
        dv_ref[...] = jnp.zeros_like(dv_ref)
        ks[0:AB, :] = jnp.zeros((AB, 128), bf16)
        vs[0:AB, :] = jnp.zeros((AB, 128), bf16)

        def prep(i, carry):
            rr = pl.ds(pl.multiple_of(i * 256, 256), 256)
            delta[rr, :] = _head_sum(do_ref[rr, :] * o_ref[rr, :], bdv)
            return carry

        lax.fori_loop(0, S // 256, prep, 0)

        for g, (_, d) in enumerate(GROUPS):
            nsub = S // (d * AB)
            chunks = _perm_chunks(S, d)
            for src, dst, ch in chunks:
                qs[dst:dst + ch, :] = q_ref[src, :].astype(bf16)
                ks[AB + dst:AB + dst + ch, :] = k_ref[src, :].astype(bf16)
                vs[AB + dst:AB + dst + ch, :] = v_ref[src, :].astype(bf16)
                dos[dst:dst + ch, :] = do_ref[src, :].astype(bf16)
                lsp[dst:dst + ch, :] = lse_ref[src, :]
                dlp[dst:dst + ch, :] = delta[src, :]
            dkp[...] = jnp.zeros_like(dkp)
            dvp[...] = jnp.zeros_like(dvp)

            def scores(j, carry):
                r0 = pl.multiple_of(j * AB, AB)
                q2 = _stack_heads(qs[pl.ds(r0, AB), :], masks)
                do2 = _stack_heads(dos[pl.ds(r0, AB), :], masks)
                s2[j] = lax.dot_general(q2, ks[pl.ds(r0, 2 * AB), :], _NT, preferred_element_type=f32)
                dp2[j] = lax.dot_general(do2, vs[pl.ds(r0, 2 * AB), :], _NT, preferred_element_type=f32)
                return carry

            lax.fori_loop(0, nblk, scores, 0, unroll=8)

            def probs(j, carry, g=g, nsub=nsub):
                r0 = pl.multiple_of(j * AB, AB)
                kind = (j % nsub == 0).astype(jnp.int32)
                for cc in range(AB // SCH):
                    lse_c = lsp[pl.ds(r0 + cc * SCH, SCH), :]
                    del_c = dlp[pl.ds(r0 + cc * SCH, SCH), :]
                    for hh in range(2):
                        c0 = hh * HEAD_DIM
                        rows = pl.ds(hh * AB + cc * SCH, SCH)
                        sb = s2[j, rows, :] - bias_ref[g, hh, kind, cc * SCH:(cc + 1) * SCH, :]
                        p = jnp.exp(sb - lse_c[:, c0:c0 + 1])
                        p2[j, rows, :] = p.astype(bf16)
                        ds2[j, rows, :] = (p * (dp2[j, rows, :] - del_c[:, c0:c0 + 1])).astype(bf16)
                return carry

            lax.fori_loop(0, nblk, probs, 0, unroll=2)

            def grads(j, carry):
                r0 = pl.multiple_of(j * AB, AB)
                q2 = _stack_heads(qs[pl.ds(r0, AB), :], masks)
                do2 = _stack_heads(dos[pl.ds(r0, AB), :], masks)
                dsb = ds2[j]
                t = jnp.dot(dsb, ks[pl.ds(r0, 2 * AB), :], preferred_element_type=f32)
                dqp[pl.ds(r0, AB), :] = jnp.where(masks[0], t[:AB], t[AB:])
                dkp[pl.ds(r0, 2 * AB), :] += lax.dot_general(dsb, q2, _TN, preferred_element_type=f32)
                dvp[pl.ds(r0, 2 * AB), :] += lax.dot_general(p2[j], do2, _TN, preferred_element_type=f32)
                return carry

            lax.fori_loop(0, nblk, grads, 0, unroll=4)

            for src, dst, ch in chunks:
                dq_ref[src, :] += dqp[dst:dst + ch, :]
                dk_ref[src, :] += dkp[AB + dst:AB + dst + ch, :]
                dv_ref[src, :] += dvp[AB + dst:AB + dst + ch, :]

    blk = pl.BlockSpec((S, 128), lambda b, hp: (b, hp))
    row = lambda dt, pad=0: pltpu.VMEM((S + pad, 128), dt)
    blocks = lambda dt: pltpu.VMEM((nblk, 2 * AB, 2 * AB), dt)
    return pl.pallas_call(
        body, name="attn_bwd", grid=(nb, N_HEADS // 2),
        in_specs=[blk, blk, pl.BlockSpec((None, S, 128), lambda b, hp: (Z_V, b, hp)), blk, blk, blk,
                  pl.BlockSpec((3, 2, 2, AB, 2 * AB), lambda b, hp: (0, hp, 0, 0, 0)),
                  pl.BlockSpec((128, 128), lambda b, hp: (0, 0)), pl.BlockSpec(memory_space=pl.ANY)],
        out_specs=[blk, blk, blk],
        out_shape=[jax.ShapeDtypeStruct((T, D), f32)] * 3,
        scratch_shapes=[row(f32), row(bf16), row(bf16, AB), row(bf16, AB), row(bf16), row(f32), row(f32),
                        blocks(f32), blocks(f32), blocks(bf16), blocks(bf16), row(f32), row(f32, AB), row(f32, AB)],
        compiler_params=_cparams(("parallel", "parallel")))(qn, kn, z8, do, o, lse, bias, bd, after)


def _any_spec():
    return pl.BlockSpec(memory_space=pl.ANY)


def _allgather_rows(shards, n_full):
    n = len(shards)

    def body(*refs):
        ins, outs = refs[:n], refs[n:2 * n]
        send_sems, recv_sems, local_sems = refs[2 * n:]
        x, y, c, me = _my_pos()
        sibling = (x, y, 1 - c)
        chips = [(1 - x, y), (x, 1 - y), (1 - x, 1 - y)]

        def idx(px, py, pc):
            return 4 * px + 2 * py + pc

        def copy(a, k, blk, to, src=None):
            return pltpu.make_async_remote_copy(
                src_ref=outs[a].at[blk] if src is None else src, dst_ref=outs[a].at[blk],
                send_sem=send_sems.at[a, k], recv_sem=recv_sems.at[a, k], device_id=to, device_id_type=MESH)

        mine = [pltpu.make_async_copy(ins[a], outs[a].at[me], local_sems.at[a]) for a in range(n)]
        for cp in mine:
            cp.start()
        first = []
        for a in range(n_full):
            first.append(copy(a, 0, me, sibling, src=ins[a]))
            first += [copy(a, 1 + j, me, (*chip, c), src=ins[a]) for j, chip in enumerate(chips)]
        for cp in first:
            cp.start()
        passed = []
        for a in range(n_full):
            for j, chip in enumerate(chips):
                blk = idx(*chip, c)
                copy(a, 1 + j, blk, (x, y, c)).wait_recv()
                cp = copy(a, 4 + j, blk, sibling)
                cp.start()
                passed.append(cp)
        for a in range(n_full):
            copy(a, 0, idx(x, y, 1 - c), (x, y, c)).wait_recv()
            for j, chip in enumerate(chips):
                copy(a, 4 + j, idx(*chip, 1 - c), (x, y, c)).wait_recv()
        for cp in first + passed:
            cp.wait_send()
        for cp in mine:
            cp.wait()

    return pl.pallas_call(
        body, name="allgather_weights",
        in_specs=[_any_spec()] * n, out_specs=[_any_spec()] * n,
        out_shape=[jax.ShapeDtypeStruct((N_DEV,) + s.shape, s.dtype) for s in shards],
        scratch_shapes=[pltpu.SemaphoreType.DMA((n_full, 7)), pltpu.SemaphoreType.DMA((n_full, 7)),
                        pltpu.SemaphoreType.DMA((n,))],
    )(*shards)


def _peer(x, y, c, k):
    tx = 1 - x if (k >> 2) & 1 else x
    ty = 1 - y if (k >> 1) & 1 else y
    tc = 1 - c if k & 1 else c
    return (tx, ty, tc), 4 * tx + 2 * ty + tc


_PEER_ORDER = (2, 4, 6, 3, 5, 7, 1)


_HBM = pl.BlockSpec(memory_space=pltpu.HBM)
_SEM = pl.BlockSpec(memory_space=pltpu.SEMAPHORE)
_EFFECT = pltpu.SideEffectType.DATAFLOW_SIDE_EFFECTING


def _exchange_copies(srcs, lands, send_sems, recv_sems, gather, half):
    x, y, c, me = _my_pos()
    pick = lambda px, py: None if half is None else ((px == py) if half == 0 else (px != py))
    copies = []
    for k in _PEER_ORDER:
        tgt, tidx = _peer(x, y, c, k)
        for a in range(len(srcs)):
            copies.append((pltpu.make_async_remote_copy(
                src_ref=srcs[a] if gather else srcs[a].at[tidx], dst_ref=lands[a].at[me],
                send_sem=send_sems.at[7 * a + k - 1], recv_sem=recv_sems.at[7 * a + k - 1],
                device_id=tgt, device_id_type=MESH), pick(tgt[0], tgt[1])))
    return copies, pick(x, y)


def _when(cond, fn):
    if cond is None:
        fn()
    else:
        pl.when(cond)(fn)


def _exchange_start(name, srcs, lands=None, after=None, gather=None, half=None):
    n = len(srcs)
    gather = (lands is not None) if gather is None else gather
    if lands is None:
        lands = [lax.empty(g.shape, g.dtype) for g in srcs]
    extra = [] if after is None else [after]

    def body(*refs):
        src_refs, land_refs = refs[:n], refs[n:2 * n]
        send_sems, recv_sems = refs[2 * n + len(extra)], refs[2 * n + len(extra) + 1]
        token = refs[-1]
        for cp, sends in _exchange_copies(src_refs, land_refs, send_sems, recv_sems, gather, half)[0]:
            _when(sends, cp.start)
        token[...] = jnp.zeros_like(token)

    hbm = lambda a: pltpu.with_memory_space_constraint(a, pltpu.HBM)
    outs = pl.pallas_call(
        body, name=name,
        out_shape=(pltpu.SemaphoreType.DMA((7 * n,)), pltpu.SemaphoreType.DMA((7 * n,)),
                   *[pltpu.HBM(g.shape, g.dtype) for g in list(srcs) + list(lands)],
                   jax.ShapeDtypeStruct((8, 128), f32)),
        in_specs=[_HBM] * (2 * n) + [pl.BlockSpec(memory_space=pl.ANY)] * len(extra),
        out_specs=(_SEM, _SEM, *([_HBM] * (2 * n)), pl.BlockSpec(memory_space=pltpu.VMEM)),
        input_output_aliases={i: 2 + i for i in range(2 * n)},
        compiler_params=pltpu.CompilerParams(has_side_effects=_EFFECT),
    )(*[hbm(g) for g in srcs], *[hbm(g) for g in lands], *extra)
    return outs[0], outs[1], list(outs[2:2 + n]), list(outs[2 + n:2 + 2 * n]), outs[-1], gather, half


def _exchange_wait(name, started, after):
    send_sems, recv_sems, srcs, lands, _, gather, half = started
    n = len(srcs)
    after = list(after) if isinstance(after, (list, tuple)) else [after]

    def body(*refs):
        src_refs, land_refs = refs[:n], refs[n:2 * n]
        s_sems, r_sems = refs[2 * n], refs[2 * n + 1]
        copies, receives = _exchange_copies(src_refs, land_refs, s_sems, r_sems, gather, half)
        for cp, sends in copies:
            _when(sends, cp.wait_send)
            _when(receives, cp.wait_recv)

    outs = pl.pallas_call(
        body, name=name,
        out_shape=tuple(pltpu.HBM(a.shape, a.dtype) for a in list(srcs) + list(lands)),
        in_specs=[_HBM] * (2 * n) + [_SEM, _SEM] + [pl.BlockSpec(memory_space=pl.ANY)] * len(after),
        out_specs=tuple([_HBM] * (2 * n)),
        input_output_aliases={i: i for i in range(2 * n)},
        compiler_params=pltpu.CompilerParams(has_side_effects=_EFFECT),
    )(*srcs, *lands, send_sems, recv_sems, *after)
    return list(outs[:n]), list(outs[n:])


SMALL_ROWS = 128


def _small_start(name, sg, after=None):
    return _exchange_start(name, [sg], [lax.empty((N_DEV,) + sg.shape, f32)], after=after)


def _small_sum(name, me, started, after):
    (own,), (slots,) = _exchange_wait(name + "_wait", started, after)

    def body(me_ref, s_ref, own_ref, out_ref):
        acc = None
        for p in range(N_DEV):
            term = lax.cond(me_ref[0] == p, lambda: own_ref[...], lambda p=p: s_ref[p])
            acc = term if acc is None else acc + term
        out_ref[...] = acc

    return pl.pallas_call(
        body, name=name + "_sum",
        in_specs=[pl.BlockSpec(memory_space=pltpu.SMEM), pl.BlockSpec(memory_space=pltpu.VMEM),
                  pl.BlockSpec(memory_space=pltpu.VMEM)],
        out_specs=pl.BlockSpec(memory_space=pltpu.VMEM),
        out_shape=jax.ShapeDtypeStruct(own.shape, f32))(me, slots, own)


def _adam_math(g, w, m, v):
    m = ADAM_B1 * m + (1.0 - ADAM_B1) * g
    v = ADAM_B2 * v + (1.0 - ADAM_B2) * (g * g)
    m_hat = m / (1.0 - ADAM_B1 ** ADAM_STEP)
    v_hat = v / (1.0 - ADAM_B2 ** ADAM_STEP)
    delta = -ADAM_LR * (m_hat / (jnp.sqrt(v_hat) + ADAM_EPS) + ADAM_WD * w)
    return delta, m, v


def _adam_slots(name, me, slots, own, w, m, v, tr, transposed=False):
    rows = slots.shape[1]

    def body(me_ref, s_ref, own_ref, w_ref, m_ref, v_ref, g_ref, d_ref, nm_ref, nv_ref):
        mine = own_ref[...]
        g = None
        for p in range(N_DEV):
            term = lax.cond(me_ref[0] == p, lambda: mine, lambda p=p: s_ref[p]).astype(f32)
            g = term if g is None else g + term
        if transposed:
            g = g.T
        delta, nm, nv = _adam_math(g, w_ref[...], m_ref[...], v_ref[...])
        g_ref[...] = g
        d_ref[...] = delta
        nm_ref[...] = nm
        nv_ref[...] = nv

    if transposed:
        rs = pl.BlockSpec((D, tr), lambda i, me_ref: (0, i))
    else:
        rs = pl.BlockSpec((tr, D), lambda i, me_ref: (i, 0))
    return pl.pallas_call(
        body, name=name,
        grid_spec=pltpu.PrefetchScalarGridSpec(
            num_scalar_prefetch=1, grid=(rows // tr,),
            in_specs=[pl.BlockSpec((N_DEV, tr, D), lambda i, me_ref: (0, i, 0)),
                      pl.BlockSpec((None, tr, D), lambda i, me_ref: (me_ref[0], i, 0)), rs, rs, rs],
            out_specs=[rs] * 4),
        out_shape=[jax.ShapeDtypeStruct(w.shape, f32)] * 4,
        compiler_params=_cparams(("parallel",)))(me, slots, own, w, m, v)


def _adam_small(g, w, m, v):
    def body(g_ref, w_ref, m_ref, v_ref, d_ref, nm_ref, nv_ref):
        delta, nm, nv = _adam_math(g_ref[...], w_ref[...], m_ref[...], v_ref[...])
        d_ref[...] = delta
        nm_ref[...] = nm
        nv_ref[...] = nv

    return pl.pallas_call(body, name="adam_small", out_shape=[jax.ShapeDtypeStruct(g.shape, f32)] * 3)(g, w, m, v)


FFN_PAD = 6 * D


_SMALL_PARTS = (("norm1_g", 1), ("gate_b", 2), ("conv_w", CONV_WIDTH), ("conv_b", 1), ("conv_norm_g", 1),
                ("q_norm_g", 1), ("k_norm_g", 1), ("norm2_g", 1), ("ffn_conv_w", 18), ("ffn_conv_b", 6), ("last", 1))


def _small_offsets():
    out, row = {}, 0
    for name, rows in _SMALL_PARTS:
        out[name] = row
        row += -(-rows // 8) * 8
    assert row == SMALL_ROWS
    return out


def _pack_small(norm1_g, gate_b, conv_w, conv_b, conv_norm_g, q_norm_g, k_norm_g, norm2_g, ffn_conv_w, ffn_conv_b,
                last_row=None):
    pad_h = lambda a: jnp.pad(a, ((0, 0), (0, D - HEAD_DIM)))
    pad_f = lambda a: jnp.pad(a, ((0, 0), (0, FFN_PAD - 2 * D_FF))).reshape(-1, D)
    parts = [norm1_g, gate_b.reshape(2, D), conv_w, conv_b, conv_norm_g, pad_h(q_norm_g), pad_h(k_norm_g), norm2_g,
             pad_f(ffn_conv_w), pad_f(ffn_conv_b), jnp.zeros((1, D), f32) if last_row is None else last_row]
    return jnp.concatenate([jnp.pad(p, ((0, -p.shape[0] % 8), (0, 0))) for p in parts], axis=0)


def _unpack_small(p):
    o = _small_offsets()
    rows = lambda name, n: p[o[name]:o[name] + n]
    ffn = lambda a: a.reshape(-1, FFN_PAD)[:, :2 * D_FF]
    return dict(
        norm1_g=rows("norm1_g", 1), gate_b=rows("gate_b", 2).reshape(1, 2 * D), conv_w=rows("conv_w", CONV_WIDTH),
        conv_b=rows("conv_b", 1), conv_norm_g=rows("conv_norm_g", 1), q_norm_g=rows("q_norm_g", 1)[:, :HEAD_DIM],
        k_norm_g=rows("k_norm_g", 1)[:, :HEAD_DIM], norm2_g=rows("norm2_g", 1),
        ffn_conv_w=ffn(rows("ffn_conv_w", 18)), ffn_conv_b=ffn(rows("ffn_conv_b", 6)))


_ADAM_TILE = {896: 128, 704: 64, 128: 128, 352: 176}


def kernel(x, norm1_g, w_in, gate_b, conv_w, conv_b, conv_norm_g, w_conv_out, q_norm_g, k_norm_g, w_attn_out, w_out, norm2_g, w_up, ffn_conv_w, ffn_conv_b, w_down, loss_target, m_norm1_g, m_w_in, m_gate_b, m_conv_w, m_conv_b, m_conv_norm_g, m_w_conv_out, m_q_norm_g, m_k_norm_g, m_w_attn_out, m_w_out, m_norm2_g, m_w_up, m_ffn_conv_w, m_ffn_conv_b, m_w_down, v_norm1_g, v_w_in, v_gate_b, v_conv_w, v_conv_b, v_conv_norm_g, v_w_conv_out, v_q_norm_g, v_k_norm_g, v_w_attn_out, v_w_out, v_norm2_g, v_w_up, v_ffn_conv_w, v_ffn_conv_b, v_w_down):
    BL, S, _ = x.shape
    T = BL * S
    me = 4 * lax.axis_index("x") + 2 * lax.axis_index("y") + lax.axis_index("c")
    xt = x.reshape(T, D)
    target = loss_target.reshape(T, D)

    big = dict(w_in=(w_in[0], m_w_in[0], v_w_in[0]), w_up=(w_up[0].T, m_w_up[0].T, v_w_up[0].T),
               w_conv_out=(w_conv_out[0], m_w_conv_out[0], v_w_conv_out[0]),
               w_attn_out=(w_attn_out[0], m_w_attn_out[0], v_w_attn_out[0]),
               w_out=(w_out[0], m_w_out[0], v_w_out[0]), w_down=(w_down[0], m_w_down[0], v_w_down[0]))
    order = ["w_in", "w_conv_out", "w_attn_out", "w_out", "w_up", "w_down"]
    shards = [(big[n][0].T if n == "w_in" else big[n][0]).astype(bf16) for n in order]
    gathered = _allgather_rows(shards, 1)
    ga_proj = _exchange_start("gather_start_proj", shards[1:4], gathered[1:4], after=gathered[0])
    ga_ffn = _exchange_start("gather_start_ffn", shards[4:6], gathered[4:6], after=ga_proj[4])
    W = {"w_in": gathered[0].reshape(-1, D)}

    def place_cols(shard, full_cols):
        z = jnp.zeros((shard.shape[0], full_cols), f32)
        return lax.dynamic_update_slice(z, shard, (0, me * shard.shape[1]))

    zr = lambda a: jnp.zeros_like(a)
    conv_local = _pack_small(
        zr(norm1_g), zr(gate_b), place_cols(conv_w[0], D), zr(conv_b), zr(conv_norm_g), zr(q_norm_g), zr(k_norm_g),
        zr(norm2_g), place_cols(ffn_conv_w[0], 2 * D_FF), zr(ffn_conv_b))
    ga_conv = _small_start("gather_conv_start", conv_local, after=ga_ffn[4])

    bd = (jnp.arange(128)[:, None] // HEAD_DIM == jnp.arange(128)[None, :] // HEAD_DIM).astype(bf16)
    bias = _attn_bias()
    qg = jnp.tile(q_norm_g, (1, N_HEADS))
    kg = jnp.tile(k_norm_g, (1, N_HEADS))

    h = _norm1_fwd(xt, norm1_g)
    z8 = _matmul_call(
        "mm_z", h, W["w_in"],
        pl.BlockSpec((2048, D), lambda i, j, k: (i, 0)),
        pl.BlockSpec((1024, D), lambda i, j, k: (_wsec_of_zsec(j), 0)),
        pl.BlockSpec((None, 2048, D), lambda i, j, k: (j, i, 0)),
        jax.ShapeDtypeStruct((8, T, D), f32), (T // 2048, 7, 1), "nt", 1, 2048, 1024, after=ga_conv[4])
    conv_all = _unpack_small(_small_sum("gather_conv", me.reshape(1), ga_conv, z8))
    conv_w_full, ffn_w_full = conv_all["conv_w"], conv_all["ffn_conv_w"]
    c = _conv_fwd(z8, conv_w_full, conv_b, S)
    s = _convnorm_fwd(c, conv_norm_g)
    qn, kn = _qk_fwd(z8, qg, kg, bd)
    for n, g in zip(order[1:4], _exchange_wait("gather_wait_proj", ga_proj, qn)[1]):
        W[n] = g.reshape(-1, D)
    ya = _matmul("mm_ya", s, W["w_conv_out"], "nn", f32)
    o, ob, lse = _attn_fwd(qn, kn, z8, bias, S)
    yb = _matmul("mm_yb", ob, W["w_attn_out"], "nn", f32)
    mixed = _gate_fwd(z8, gate_b, ya, yb)
    x1, h2 = _out_norm2_fwd(mixed, W["w_out"], xt, norm2_g)
    for n, g in zip(order[4:6], _exchange_wait("gather_wait_ffn", ga_ffn, x1)[1]):
        W[n] = g.reshape(-1, D)
    TNU = D_FF // 2
    u3 = _matmul_call(
        "mm_u", h2, W["w_up"],
        pl.BlockSpec((1024, D), lambda i, j, k: (i, 0)),
        pl.BlockSpec((TNU, D), lambda i, j, k: (j, 0)),
        pl.BlockSpec((None, 1024, TNU), lambda i, j, k: (j // 2, i, j % 2)),
        jax.ShapeDtypeStruct((2, T, D_FF), f32), (T // 1024, 4, 1), "nt", 1, 1024, TNU)
    f = _ffn_fwd(u3, ffn_w_full, ffn_conv_b, S)
    dy, dyb, lacc = _down_loss_fwd(f, W["w_down"], x1, target)
    loss_local = 0.5 / D * jnp.sum(lacc)

    df = _matmul("mm_df", dyb, W["w_down"], "nt", f32, tn=TNU)
    g_w_down = _matmul("mm_dwdn", f, dyb, "tn", bf16, tm=TNU)
    du3, dffn = _ffn_bwd(u3, df, ffn_w_full, ffn_conv_b, S)
    g_w_up = _matmul_call(
        "mm_dwup", du3, h2,
        pl.BlockSpec((None, T, TNU), lambda i, j, k: (i // 2, 0, i % 2)),
        pl.BlockSpec((T, D), lambda i, j, k: (0, 0)),
        pl.BlockSpec((TNU, D), lambda i, j, k: (i, 0)),
        jax.ShapeDtypeStruct((2 * D_FF, D), bf16), (4, 1, 1), "tn", 1, TNU, D)
    blocks8 = lambda a: a.reshape(N_DEV, -1, D)
    ex_ffn = _exchange_start("scatter_start_ffn", [blocks8(g_w_up), blocks8(g_w_down)])
    dx1, dx1b, dg_norm2 = _up_norm2_bwd(du3, W["w_up"], x1, dy, norm2_g, ex_ffn[4])
    g_w_out = _matmul("mm_dwo", mixed, dx1b, "tn", bf16, tm=512)
    dz8 = lax.empty((8, T, D), bf16)
    dya, dyb2, dz8, dg_gate = _out_gate_bwd(dx1b, W["w_out"], z8, gate_b, ya, yb, dz8)
    ds = _matmul("mm_ds", dya, W["w_conv_out"], "nt", f32)
    g_w_conv_out = _matmul("mm_dwco", s, dya, "tn", bf16, tm=512)
    g_w_attn_out = _matmul("mm_dwao", ob, dyb2, "tn", bf16, tm=512)
    ex_proj = _exchange_start("scatter_start_proj", [blocks8(g_w_conv_out), blocks8(g_w_attn_out), blocks8(g_w_out)])
    do = _matmul("mm_do", dyb2, W["w_attn_out"], "nt", f32, after=ex_proj[4])
    dc, dg_convnorm = _convnorm_bwd(c, ds, conv_norm_g)
    dz8a, dconv = _conv_bwd(dc, z8, conv_w_full, dz8, S)
    dwin_specs = lambda zsec, wsec: (
        pl.BlockSpec((None, T, D), lambda i, j, k: (zsec(i), 0, 0)), pl.BlockSpec((T, D), lambda i, j, k: (0, 0)),
        pl.BlockSpec((1024, D), lambda i, j, k: (wsec(i), 0)), jax.ShapeDtypeStruct((7 * D, D), bf16))
    g_w_in = _matmul_call("mm_dwin_a", dz8a, h, *dwin_specs(lambda i: i, lambda i: jnp.where(i < 2, i, i + 3)),
                          (4, 1, 1), "tn", 1, D, D)
    ex_in_a = _exchange_start("scatter_start_in_a", [blocks8(g_w_in)], half=0)
    dqn, dkn, dv = _attn_bwd(qn, kn, z8, do, o, lse, bias, bd, S, ex_in_a[4])
    dz8b, dg_q, dg_k = _qk_bwd(z8, dqn, dkn, dv, qg, kg, bd, dz8a)
    g_w_in = _matmul_call("mm_dwin_b", dz8b, h, *dwin_specs(lambda i: i + 4, lambda i: i + 2),
                          (3, 1, 1), "tn", 1, D, D, fill=ex_in_a[2][0].reshape(7 * D, D))
    ex_in_b = _exchange_start("scatter_start_in_b", [blocks8(g_w_in)], ex_in_a[3], gather=False, half=1)
    grad_x, dg_norm1 = _in_norm1_bwd(dz8b, W["w_in"], xt, dx1, norm1_g, ex_in_b[4])

    sum8 = lambda a: a.reshape(-1, 8, a.shape[-1]).sum(axis=1)
    dconv_s = sum8(dconv.sum(axis=0))
    dffn_s = dffn.sum(axis=0).reshape(2, 4, 8, D_FF).sum(axis=2)
    dffn_w = jnp.concatenate([dffn_s[0, :3], dffn_s[1, :3]], axis=1)
    dffn_b = jnp.concatenate([dffn_s[0, 3:4], dffn_s[1, 3:4]], axis=1)
    fold = lambda a: sum8(a).reshape(N_HEADS, HEAD_DIM).sum(axis=0)[None]
    small_g_local = _pack_small(
        sum8(dg_norm1), sum8(dg_gate), dconv_s[:CONV_WIDTH], dconv_s[CONV_WIDTH:], sum8(dg_convnorm),
        fold(dg_q), fold(dg_k), sum8(dg_norm2), dffn_w, dffn_b,
        last_row=jnp.pad(loss_local.reshape(1, 1), ((0, 0), (0, D - 1))))
    sg_start = _small_start("small_grads_start", small_g_local)

    own, slots = {}, {}
    for tag, ex, names_ in (("ffn", ex_ffn, ("w_up", "w_down")),
                            ("proj", ex_proj, ("w_conv_out", "w_attn_out", "w_out"))):
        sent, landed = _exchange_wait("scatter_wait_" + tag, ex, sg_start[4])
        for n, src, land in zip(names_, sent, landed):
            own[n], slots[n] = src, land
    sent, landed = _exchange_wait("scatter_wait_in_a", ex_in_a[:2] + (ex_in_b[2], ex_in_b[3]) + ex_in_a[4:],
                                  sg_start[4])
    sent, landed = _exchange_wait("scatter_wait_in_b", ex_in_b[:2] + (sent, landed) + ex_in_b[4:], sg_start[4])
    own["w_in"], slots["w_in"] = sent[0], landed[0]

    res, adam_done = {}, []
    for n in order:
        w, m, v = big[n]
        outs = _adam_slots("adam_" + n, me.reshape(1), slots[n], own[n], w, m, v, _ADAM_TILE[slots[n].shape[1]],
                           transposed=(n == "w_in"))
        adam_done.append(outs[0])
        if n == "w_up":
            outs = [a.T for a in outs]
        res[n] = [a[None] for a in outs]
    small_g = _small_sum("small_grads", me.reshape(1), sg_start, adam_done)
    loss = small_g[_small_offsets()["last"], 0]

    col = lambda a, width: lax.dynamic_slice(a, (0, me * width), (a.shape[0], width))
    small_w_true = _pack_small(norm1_g, gate_b, conv_w_full, conv_b, conv_norm_g, q_norm_g, k_norm_g, norm2_g,
                               ffn_w_full, ffn_conv_b)
    place_m = lambda a, full: place_cols(a[0], full)
    small_m = _pack_small(m_norm1_g, m_gate_b, place_m(m_conv_w, D), m_conv_b, m_conv_norm_g, m_q_norm_g, m_k_norm_g,
                          m_norm2_g, place_m(m_ffn_conv_w, 2 * D_FF), m_ffn_conv_b)
    small_v = _pack_small(v_norm1_g, v_gate_b, place_m(v_conv_w, D), v_conv_b, v_conv_norm_g, v_q_norm_g, v_k_norm_g,
                          v_norm2_g, place_m(v_ffn_conv_w, 2 * D_FF), v_ffn_conv_b)
    sd, sm, sv = _adam_small(small_g, small_w_true, small_m, small_v)
    for i, packed in enumerate((small_g, sd, sm, sv)):
        u = _unpack_small(packed)
        u["conv_w"] = col(u["conv_w"], D // N_DEV)
        u["ffn_conv_w"] = col(u["ffn_conv_w"], 2 * D_FF // N_DEV)
        for n, a in u.items():
            res.setdefault(n, [None] * 4)[i] = a[None] if n in ("conv_w", "ffn_conv_w") else a

    names = ["norm1_g", "w_in", "gate_b", "conv_w", "conv_b", "conv_norm_g", "w_conv_out", "q_norm_g", "k_norm_g",
             "w_attn_out", "w_out", "norm2_g", "w_up", "ffn_conv_w", "ffn_conv_b", "w_down"]
    out = [loss, grad_x.reshape(BL, S, D)]
    for i in range(4):
        out += [res[n][i] for n in names]
    return tuple(out)
```

```python
import functools

import jax
import jax.numpy as jnp
import numpy as np
from jax import lax
from jax.experimental import pallas as pl
from jax.experimental.pallas import tpu as pltpu

f32 = jnp.float32
bf16 = jnp.bfloat16

D = 1024
N_HEADS = 16
HEAD_DIM = 64
CONV_WIDTH = 31
D_FF = 2816
GROUPS = ((128, 1), (512, 4), (2048, 16))
ATTN_BLOCK = 128
EPS = 1e-6
N_DEV = 8
MESH = pl.DeviceIdType.MESH

ADAM_LR = 0.001
ADAM_B1 = 0.9
ADAM_B2 = 0.999
ADAM_EPS = 1e-08
ADAM_WD = 0.01
ADAM_STEP = 10

VMEM_LIMIT = 56 * 1024 * 1024
MASK_BIAS = 1e30

Z_AVAL, Z_AGATE, Z_GA, Z_GB, Z_Q, Z_K, Z_V = 0, 1, 2, 3, 4, 5, 6


_W_OF_Z = (0, 1, 5, 6, 2, 3, 4)


def _wsec_of_zsec(j):
    return jnp.where(j < 2, j, jnp.where(j < 4, j + 3, j - 2))


def _zsec_of_wsec(w):
    return jnp.where(w < 2, w, jnp.where(w < 5, w + 2, w - 3))


def _sig(x):
    return 1.0 / (1.0 + jnp.exp(-x))


def _colsum8(x):
    return x.reshape(-1, 8, x.shape[-1]).sum(axis=0)


def _cparams(sem):
    return pltpu.CompilerParams(dimension_semantics=sem, vmem_limit_bytes=VMEM_LIMIT)


def _my_pos():
    x, y, c = lax.axis_index("x"), lax.axis_index("y"), lax.axis_index("c")
    return x, y, c, 4 * x + 2 * y + c


_DIMS = {"nn": ((1,), (0,)), "nt": ((1,), (1,)), "tn": ((0,), (0,))}


def _matmul_call(name, a, b, a_spec, b_spec, o_spec, out_shape, grid, mode, nk, tm, tn, after=None, fill=None):
    dims = (_DIMS[mode], ((), ()))
    extra = ([] if after is None else [after]) + ([] if fill is None else [fill])

    def body(a_ref, b_ref, *rest):
        o_ref, scratch = rest[len(extra)], rest[len(extra) + 1:]
        part = lax.dot_general(a_ref[...], b_ref[...], dims, preferred_element_type=f32)
        if nk == 1:
            o_ref[...] = part.astype(o_ref.dtype)
        else:
            acc = scratch[0]
            k = pl.program_id(2)

            @pl.when(k == 0)
            def _():
                acc[...] = part

            @pl.when(k > 0)
            def _():
                acc[...] += part

            @pl.when(k == nk - 1)
            def _():
                o_ref[...] = acc[...].astype(o_ref.dtype)

    scratch = [] if nk == 1 else [pltpu.VMEM((tm, tn), f32)]
    return pl.pallas_call(
        body, name=name, grid=grid, in_specs=[a_spec, b_spec] + [pl.BlockSpec(memory_space=pl.ANY)] * len(extra),
        out_specs=o_spec, out_shape=out_shape, input_output_aliases={} if fill is None else {1 + len(extra): 0},
        scratch_shapes=scratch, compiler_params=_cparams(("parallel", "parallel", "arbitrary")),
    )(a, b, *extra)


def _matmul(name, a, b, mode, out_dtype, tm=1024, tn=1024, tk=None, after=None):
    if mode == "nn":
        (M, K), (_, N) = a.shape, b.shape
    elif mode == "nt":
        (M, K), (N, _) = a.shape, b.shape
    else:
        (K, M), (_, N) = a.shape, b.shape
    tm, tn = min(tm, M), min(tn, N)
    tk = K if tk is None else tk
    nk = K // tk
    assert M % tm == 0 and N % tn == 0 and K % tk == 0
    if mode == "tn":
        a_spec = pl.BlockSpec((tk, tm), lambda i, j, k: (k, i))
    else:
        a_spec = pl.BlockSpec((tm, tk), lambda i, j, k: (i, k))
    if mode == "nt":
        b_spec = pl.BlockSpec((tn, tk), lambda i, j, k: (j, k))
    else:
        b_spec = pl.BlockSpec((tk, tn), lambda i, j, k: (k, j))
    o_spec = pl.BlockSpec((tm, tn), lambda i, j, k: (i, j))
    return _matmul_call(name, a, b, a_spec, b_spec, o_spec, jax.ShapeDtypeStruct((M, N), out_dtype),
                        (M // tm, N // tn, nk), mode, nk, tm, tn, after=after)


FTM = 512


def _matmul_fused(name, a, b, pairs, epilogue, extras, consts, outs, nt=False, sums=False, passed=(), aliases=None):
    sa, M, kk = a.shape
    na = max(i for i, _ in pairs) + 1
    ne, nc, npass = len(extras), len(consts), len(passed)
    dims = (_DIMS["nt" if nt else "nn"], ((), ()))

    def body(a_ref, b_ref, *rest):
        acc = None
        for i, j in pairs:
            part = lax.dot_general(a_ref[i], b_ref[j], dims, preferred_element_type=f32)
            acc = part if acc is None else acc + part
        epilogue(acc, rest[:ne], rest[ne:ne + nc], rest[ne + nc + npass:])

    whole = lambda arr: pl.BlockSpec(arr.shape, lambda i, nd=arr.ndim: (0,) * nd, pipeline_mode=pl.Buffered(1))
    io_alias = {2 + ne + nc + k: v for k, v in (aliases or {}).items()}
    return pl.pallas_call(
        body, name=name, grid=(M // FTM,),
        in_specs=[pl.BlockSpec((na, FTM, kk), lambda i: (0, i, 0)), whole(b)] + [s for _, s in extras]
        + [whole(c) for c in consts] + [pl.BlockSpec(memory_space=pl.ANY)] * npass,
        out_specs=[s for _, s in outs], out_shape=[s for s, _ in outs], input_output_aliases=io_alias,
        compiler_params=_cparams(("arbitrary" if sums else "parallel",)),
    )(a, b, *[x for x, _ in extras], *consts, *passed)


def _frows(c=D):
    return pl.BlockSpec((FTM, c), lambda i: (i, 0))


def _fsec(s):
    return pl.BlockSpec((None, FTM, D), lambda i: (s, i, 0))


def _rowshape(T, dtype, c=D):
    return (jax.ShapeDtypeStruct((T, c), dtype), _frows(c))


def _sumshape(c=D):
    return (jax.ShapeDtypeStruct((8, c), f32), pl.BlockSpec((8, c), lambda i: (0, 0)))


def _add_colsum(ref, x, cols=None):
    @pl.when(pl.program_id(0) == 0)
    def _():
        if cols is None:
            ref[...] = jnp.zeros_like(ref)
        else:
            ref[:, cols] = jnp.zeros((8, x.shape[-1]), f32)

    if cols is None:
        ref[...] += _colsum8(x)
    else:
        ref[:, cols] += _colsum8(x)


TT = 512


def _rows(c, cb=0, tt=TT):
    return pl.BlockSpec((tt, c), lambda i: (i, cb))


def _sec(s, tt=TT):
    return pl.BlockSpec((None, tt, D), lambda i: (s, i, 0))


def _const(shape):
    return pl.BlockSpec(shape, lambda i: (0,) * len(shape))


def _acc_spec(c):
    return pl.BlockSpec((8, c), lambda i: (0, 0))


def _rms(x):
    return lax.rsqrt(jnp.mean(x * x, axis=-1, keepdims=True) + EPS)


def _rms_bwd(dy_g, xn, rstd):
    return rstd * (dy_g - xn * jnp.mean(dy_g * xn, axis=-1, keepdims=True))


def _head_sum(x, bd):
    parts = []
    for cb in range(x.shape[-1] // 128):
        xb = x[:, cb * 128:(cb + 1) * 128]
        hi = xb.astype(bf16)
        lo = (xb - hi.astype(f32)).astype(bf16)
        parts.append(jnp.dot(hi, bd, preferred_element_type=f32) + jnp.dot(lo, bd, preferred_element_type=f32))
    return parts[0] if len(parts) == 1 else jnp.concatenate(parts, axis=1)


def _norm1_fwd(x, g):
    T = x.shape[0]

    def body(x_ref, g_ref, h_ref):
        xv = x_ref[...]
        h_ref[...] = (xv * _rms(xv) * g_ref[...]).astype(bf16)

    return pl.pallas_call(
        body, name="norm1_fwd", grid=(T // TT,), in_specs=[_rows(D), _const((1, D))], out_specs=_rows(D),
        out_shape=jax.ShapeDtypeStruct((T, D), bf16), compiler_params=_cparams(("parallel",)))(x, g)


def _convnorm_fwd(c, g):
    T = c.shape[0]

    def body(c_ref, g_ref, s_ref):
        cv = c_ref[...]
        r = cv * _rms(cv) * g_ref[...]
        s_ref[...] = (r * _sig(r)).astype(bf16)

    return pl.pallas_call(
        body, name="convnorm_fwd", grid=(T // TT,), in_specs=[_rows(D), _const((1, D))], out_specs=_rows(D),
        out_shape=jax.ShapeDtypeStruct((T, D), bf16), compiler_params=_cparams(("parallel",)))(c, g)


def _qk_fwd(z8, qg, kg, bd):
    T = z8.shape[1]

    def body(q_ref, k_ref, qg_ref, kg_ref, bd_ref, qn_ref, kn_ref):
        bdv = bd_ref[...]
        q = q_ref[...]
        qn_ref[...] = q * lax.rsqrt(_head_sum(q * q, bdv) * (1.0 / HEAD_DIM) + EPS) * qg_ref[...] * (HEAD_DIM ** -0.5)
        k = k_ref[...]
        kn_ref[...] = k * lax.rsqrt(_head_sum(k * k, bdv) * (1.0 / HEAD_DIM) + EPS) * kg_ref[...]

    return pl.pallas_call(
        body, name="qk_fwd", grid=(T // TT,),
        in_specs=[_sec(Z_Q), _sec(Z_K), _const((1, D)), _const((1, D)), _const((128, 128))],
        out_specs=[_rows(D), _rows(D)],
        out_shape=[jax.ShapeDtypeStruct((T, D), f32)] * 2, compiler_params=_cparams(("parallel",)))(z8, z8, qg, kg, bd)


def _gate_fwd(z8, gate_b, ya, yb):
    T = ya.shape[0]

    def body(ga_ref, gb_ref, b_ref, ya_ref, yb_ref, mixed_ref):
        g_a = _sig(ga_ref[...] + b_ref[:, :D])
        g_b = _sig(gb_ref[...] + b_ref[:, D:])
        mixed_ref[...] = (g_a * ya_ref[...] + g_b * yb_ref[...]).astype(bf16)

    return pl.pallas_call(
        body, name="gate_fwd", grid=(T // TT,),
        in_specs=[_sec(Z_GA), _sec(Z_GB), _const((1, 2 * D)), _rows(D), _rows(D)], out_specs=_rows(D),
        out_shape=jax.ShapeDtypeStruct((T, D), bf16), compiler_params=_cparams(("parallel",)))(z8, z8, gate_b, ya, yb)


def _out_norm2_fwd(mixed, w_out, x, g):
    T = x.shape[0]

    def epilogue(acc, extra, const, out):
        x1 = extra[0][...] + acc
        out[0][...] = x1
        out[1][...] = (x1 * _rms(x1) * const[0][...]).astype(bf16)

    return _matmul_fused("mm_t1_norm2", mixed[None], w_out[None], ((0, 0),), epilogue, [(x, _frows())], [g],
                         [_rowshape(T, f32), _rowshape(T, bf16)])


def _down_loss_fwd(f, w_down, x1, target):
    T = x1.shape[0]

    def epilogue(acc, extra, const, out):
        diff = extra[0][...] + acc - extra[1][...]
        dy = diff * (1.0 / D)
        out[0][...] = dy
        out[1][...] = dy.astype(bf16)
        _add_colsum(out[2], diff * diff)

    return _matmul_fused("mm_t2_loss", f[None], w_down[None], ((0, 0),), epilogue, [(x1, _frows()), (target, _frows())],
                         [], [_rowshape(T, f32), _rowshape(T, bf16), _sumshape()], sums=True)


def _up_norm2_bwd(du3, w_up_t, x1, dy, g, token):
    T = x1.shape[0]

    def epilogue(dh, extra, const, out):
        x1v = extra[0][...]
        rstd = _rms(x1v)
        xn = x1v * rstd
        dx1 = extra[1][...] + _rms_bwd(dh * const[0][...], xn, rstd)
        out[0][...] = dx1
        out[1][...] = dx1.astype(bf16)
        _add_colsum(out[2], dh * xn)

    return _matmul_fused("mm_dh2_norm2", du3, w_up_t.reshape(2, D_FF, D), ((0, 0), (1, 1)), epilogue,
                         [(x1, _frows()), (dy, _frows())], [g],
                         [_rowshape(T, f32), _rowshape(T, bf16), _sumshape()], sums=True, passed=[token])


def _out_gate_bwd(dx1b, w_out, z8, gate_b, ya, yb, dz8):
    T = ya.shape[0]

    def epilogue(dm, extra, const, out):
        b_ref = const[0]
        g_a = _sig(extra[0][...] + b_ref[:, :D])
        g_b = _sig(extra[1][...] + b_ref[:, D:])
        out[0][...] = (dm * g_a).astype(bf16)
        out[1][...] = (dm * g_b).astype(bf16)
        dla = dm * extra[2][...] * g_a * (1.0 - g_a)
        dlb = dm * extra[3][...] * g_b * (1.0 - g_b)
        out[2][0] = dla.astype(bf16)
        out[2][1] = dlb.astype(bf16)
        _add_colsum(out[3], dla, slice(0, D))
        _add_colsum(out[3], dlb, slice(D, 2 * D))

    return _matmul_fused(
        "mm_dmixed_gate", dx1b[None], w_out[None], ((0, 0),), epilogue,
        [(z8, _fsec(Z_GA)), (z8, _fsec(Z_GB)), (ya, _frows()), (yb, _frows())], [gate_b],
        [_rowshape(T, bf16), _rowshape(T, bf16),
         (jax.ShapeDtypeStruct(dz8.shape, bf16), pl.BlockSpec((2, FTM, D), lambda i: (1, i, 0))), _sumshape(2 * D)],
        nt=True, sums=True, passed=[dz8], aliases={0: 2})


def _convnorm_bwd(c, ds, g):
    T = c.shape[0]

    def body(c_ref, ds_ref, g_ref, dc_ref, dg_ref):
        cv = c_ref[...]
        rstd = _rms(cv)
        r0 = cv * rstd
        gv = g_ref[...]
        r = r0 * gv
        sg = _sig(r)
        dr = ds_ref[...] * sg * (1.0 + r * (1.0 - sg))
        dc_ref[...] = _rms_bwd(dr * gv, r0, rstd)

        @pl.when(pl.program_id(0) == 0)
        def _():
            dg_ref[...] = jnp.zeros_like(dg_ref)

        dg_ref[...] += _colsum8(dr * r0)

    return pl.pallas_call(
        body, name="convnorm_bwd", grid=(T // TT,), in_specs=[_rows(D), _rows(D), _const((1, D))],
        out_specs=[_rows(D), _acc_spec(D)],
        out_shape=[jax.ShapeDtypeStruct((T, D), f32), jax.ShapeDtypeStruct((8, D), f32)],
        compiler_params=_cparams(("arbitrary",)))(c, ds, g)


def _qk_bwd(z8, dqn, dkn, dv, qg, kg, bd, dz8):
    T = dqn.shape[0]

    def body(q_ref, k_ref, dqn_ref, dkn_ref, dv_ref, qg_ref, kg_ref, bd_ref, dz_in, dz_ref, dqg_ref, dkg_ref):
        del dz_in
        bdv = bd_ref[...]

        @pl.when(pl.program_id(0) == 0)
        def _():
            dqg_ref[...] = jnp.zeros_like(dqg_ref)
            dkg_ref[...] = jnp.zeros_like(dkg_ref)

        def one(raw, dn_scaled, g, dg_ref, sec):
            rstd = lax.rsqrt(_head_sum(raw * raw, bdv) * (1.0 / HEAD_DIM) + EPS)
            n = raw * rstd
            dg_ref[...] += _colsum8(dn_scaled * n)
            dn = dn_scaled * g
            draw = rstd * (dn - n * (_head_sum(dn * n, bdv) * (1.0 / HEAD_DIM)))
            dz_ref[sec] = draw.astype(bf16)

        one(q_ref[...], dqn_ref[...] * (HEAD_DIM ** -0.5), qg_ref[...], dqg_ref, 0)
        one(k_ref[...], dkn_ref[...], kg_ref[...], dkg_ref, 1)
        dz_ref[2] = dv_ref[...].astype(bf16)
        dz_ref[3] = jnp.zeros((TT, D), bf16)

    return pl.pallas_call(
        body, name="qk_bwd", grid=(T // TT,),
        in_specs=[_sec(Z_Q), _sec(Z_K), _rows(D), _rows(D), _rows(D), _const((1, D)), _const((1, D)),
                  _const((128, 128)), pl.BlockSpec(memory_space=pl.ANY)],
        out_specs=[pl.BlockSpec((4, TT, D), lambda i: (1, i, 0)), _acc_spec(D), _acc_spec(D)],
        out_shape=[jax.ShapeDtypeStruct(dz8.shape, bf16), jax.ShapeDtypeStruct((8, D), f32),
                   jax.ShapeDtypeStruct((8, D), f32)],
        input_output_aliases={8: 0},
        compiler_params=_cparams(("arbitrary",)))(z8, z8, dqn, dkn, dv, qg, kg, bd, dz8)


def _in_norm1_bwd(dz8, w_in_t, x, dx1, g, token):
    T = x.shape[0]

    def epilogue(dh, extra, const, out):
        xv = extra[0][...]
        rstd = _rms(xv)
        xn = xv * rstd
        out[0][...] = extra[1][...] + _rms_bwd(dh * const[0][...], xn, rstd)
        _add_colsum(out[1], dh * xn)

    return _matmul_fused("mm_dh_norm1", dz8, w_in_t.reshape(7, D, D), tuple(zip(range(7), _W_OF_Z)), epilogue,
                         [(x, _frows()), (dx1, _frows())], [g], [_rowshape(T, f32), _sumshape()],
                         sums=True, passed=[token])


CCW = 256
CR = 64
HALO = 32


def _conv_fwd(z8, conv_w, conv_b, S):
    T = z8.shape[1]
    nb = T // S
    ncb = D // CCW

    def body(av_ref, ag_ref, w_ref, b_ref, c_ref, pad):
        pad[0:HALO, :] = jnp.zeros((HALO, CCW), f32)

        def fill(i, carry):
            r0 = pl.multiple_of(i * 256, 256)
            pad[pl.ds(HALO + r0, 256), :] = av_ref[pl.ds(r0, 256), :] * _sig(ag_ref[pl.ds(r0, 256), :])
            return carry

        lax.fori_loop(0, S // 256, fill, 0)
        bias = b_ref[...]

        def chunk(i, carry):
            r0 = pl.multiple_of(i * CR, CR)
            win = pad[pl.ds(r0, CR + HALO), :]
            acc = jnp.zeros((CR, CCW), f32) + bias
            for s in range(8):
                part = None
                for m in range((CONV_WIDTH - 1 - s) // 8 + 1):
                    j = CONV_WIDTH - 1 - 8 * m - s
                    term = win[24 - 8 * m:24 - 8 * m + CR + 8, :] * w_ref[j:j + 1, :]
                    part = term if part is None else part + term
                acc = acc + part[8 - s:8 - s + CR, :]
            c_ref[pl.ds(r0, CR), :] = acc
            return carry

        lax.fori_loop(0, S // CR, chunk, 0)

    zs = lambda s: pl.BlockSpec((None, S, CCW), lambda b, cb: (s, b, cb))
    return pl.pallas_call(
        body, name="conv_fwd", grid=(nb, ncb),
        in_specs=[zs(Z_AVAL), zs(Z_AGATE), pl.BlockSpec((CONV_WIDTH, CCW), lambda b, cb: (0, cb)),
                  pl.BlockSpec((1, CCW), lambda b, cb: (0, cb))],
        out_specs=pl.BlockSpec((S, CCW), lambda b, cb: (b, cb)),
        out_shape=jax.ShapeDtypeStruct((T, D), f32),
        scratch_shapes=[pltpu.VMEM((S + HALO, CCW), f32)],
        compiler_params=_cparams(("parallel", "parallel")))(z8, z8, conv_w, conv_b)


def _conv_bwd(dc, z8, conv_w, dz8, S):
    T = dc.shape[0]
    nb = T // S
    ncb = D // CCW

    def body(dc_ref, av_ref, ag_ref, w_ref, dz_in, dz_ref, dw_ref, apad, dpad, shbuf):
        del dz_in
        apad[0:HALO, :] = jnp.zeros((HALO, CCW), f32)
        dpad[S:S + HALO, :] = jnp.zeros((HALO, CCW), f32)
        dw_ref[...] = jnp.zeros_like(dw_ref)

        def fill(i, carry):
            r0 = pl.multiple_of(i * 256, 256)
            apad[pl.ds(HALO + r0, 256), :] = av_ref[pl.ds(r0, 256), :] * _sig(ag_ref[pl.ds(r0, 256), :])
            dpad[pl.ds(r0, 256), :] = dc_ref[pl.ds(r0, 256), :]
            return carry

        lax.fori_loop(0, S // 256, fill, 0)

        def chunk(i, carry):
            r0 = pl.multiple_of(i * CR, CR)
            dwin = dpad[pl.ds(r0, CR + HALO), :]
            da = jnp.zeros((CR, CCW), f32)
            for s in range(8):
                shbuf[...] = dwin[s:s + CR, :]
                dshift = shbuf[...]
                part = None
                for m in range((CONV_WIDTH - 1 - s) // 8 + 1):
                    j = CONV_WIDTH - 1 - 8 * m - s
                    term = dwin[8 * m:8 * m + CR + 8, :] * w_ref[j:j + 1, :]
                    part = term if part is None else part + term
                    a_lag = apad[pl.ds(r0 + HALO - 8 * m, CR), :]
                    dw_ref[8 * j:8 * j + 8, :] += _colsum8(dshift * a_lag)
                da = da + part[s:s + CR, :]
            dw_ref[8 * CONV_WIDTH:8 * CONV_WIDTH + 8, :] += _colsum8(dwin[0:CR, :])
            av = av_ref[pl.ds(r0, CR), :]
            sg = _sig(ag_ref[pl.ds(r0, CR), :])
            dz_ref[0, pl.ds(r0, CR), :] = (da * sg).astype(bf16)
            dz_ref[1, pl.ds(r0, CR), :] = (da * av * sg * (1.0 - sg)).astype(bf16)
            return carry

        lax.fori_loop(0, S // CR, chunk, 0)

    zs = lambda s: pl.BlockSpec((None, S, CCW), lambda b, cb: (s, b, cb))
    return pl.pallas_call(
        body, name="conv_bwd", grid=(nb, ncb),
        in_specs=[pl.BlockSpec((S, CCW), lambda b, cb: (b, cb)), zs(Z_AVAL), zs(Z_AGATE),
                  pl.BlockSpec((CONV_WIDTH, CCW), lambda b, cb: (0, cb)), pl.BlockSpec(memory_space=pl.ANY)],
        out_specs=[pl.BlockSpec((2, S, CCW), lambda b, cb: (0, b, cb)),
                   pl.BlockSpec((None, 256, CCW), lambda b, cb: (b, 0, cb))],
        out_shape=[jax.ShapeDtypeStruct(dz8.shape, bf16), jax.ShapeDtypeStruct((nb, 256, D), f32)],
        input_output_aliases={4: 0},
        scratch_shapes=[pltpu.VMEM((S + HALO, CCW), f32), pltpu.VMEM((S + HALO, CCW), f32),
                        pltpu.VMEM((CR, CCW), f32)],
        compiler_params=_cparams(("parallel", "parallel")))(dc, z8, z8, conv_w, dz8)


FR = 128
NFB = D_FF // CCW


def _ffn_window(ref, i, r0):
    return ref[pl.ds(r0 - 8, FR + 8), :]


def _ffn_u(win, w_ref, b_ref):
    return (win[6:6 + FR, :] * w_ref[0:1, :] + win[7:7 + FR, :] * w_ref[1:2, :]
            + win[8:8 + FR, :] * w_ref[2:3, :] + b_ref[...])


def _ffn_fwd(u3, ffn_w, ffn_b, S):
    T = u3.shape[1]
    nb = T // S

    def body(uv_ref, ug_ref, wv_ref, wg_ref, bv_ref, bg_ref, f_ref):
        def chunk(first, i):
            r0 = 0 if first else pl.multiple_of(i * FR, FR)
            if first:
                z = jnp.zeros((8, CCW), f32)
                wv = jnp.concatenate([z, uv_ref[0:FR, :]], axis=0)
                wg = jnp.concatenate([z, ug_ref[0:FR, :]], axis=0)
            else:
                wv = _ffn_window(uv_ref, i, r0)
                wg = _ffn_window(ug_ref, i, r0)
            u_val = _ffn_u(wv, wv_ref, bv_ref)
            u_gate = _ffn_u(wg, wg_ref, bg_ref)
            f_ref[pl.ds(r0, FR), :] = (u_gate * _sig(u_gate) * u_val).astype(bf16)

        chunk(True, 0)

        def loop(i, carry):
            chunk(False, i)
            return carry

        lax.fori_loop(1, S // FR, loop, 0)

    us = lambda h: pl.BlockSpec((None, S, CCW), lambda b, cb: (h, b, cb))
    ws = lambda h: pl.BlockSpec((3, CCW), lambda b, cb: (0, h * NFB + cb))
    bs = lambda h: pl.BlockSpec((1, CCW), lambda b, cb: (0, h * NFB + cb))
    return pl.pallas_call(
        body, name="ffn_fwd", grid=(nb, NFB),
        in_specs=[us(0), us(1), ws(0), ws(1), bs(0), bs(1)],
        out_specs=pl.BlockSpec((S, CCW), lambda b, cb: (b, cb)),
        out_shape=jax.ShapeDtypeStruct((T, D_FF), bf16),
        compiler_params=_cparams(("parallel", "parallel")))(u3, u3, ffn_w, ffn_w, ffn_b, ffn_b)


def _ffn_bwd(u3, df, ffn_w, ffn_b, S):
    T = u3.shape[1]
    nb = T // S

    def body(uv_ref, ug_ref, df_ref, wv_ref, wg_ref, bv_ref, bg_ref, du_ref, dw_ref, dvpad, dgpad, shbuf):
        dvpad[S:S + 8, :] = jnp.zeros((8, CCW), f32)
        dgpad[S:S + 8, :] = jnp.zeros((8, CCW), f32)
        dw_ref[...] = jnp.zeros_like(dw_ref)

        def chunk(first, i):
            r0 = 0 if first else pl.multiple_of(i * FR, FR)
            if first:
                z = jnp.zeros((8, CCW), f32)
                wv = jnp.concatenate([z, uv_ref[0:FR, :]], axis=0)
                wg = jnp.concatenate([z, ug_ref[0:FR, :]], axis=0)
            else:
                wv = _ffn_window(uv_ref, i, r0)
                wg = _ffn_window(ug_ref, i, r0)
            taps = []
            for h, win in enumerate((wv, wg)):
                shbuf[2 * h] = win[6:6 + FR, :]
                shbuf[2 * h + 1] = win[7:7 + FR, :]
                taps.append((shbuf[2 * h], shbuf[2 * h + 1], win[8:8 + FR, :]))
            conv = lambda x, w_ref, b_ref: (x[0] * w_ref[0:1, :] + x[1] * w_ref[1:2, :] + x[2] * w_ref[2:3, :]
                                            + b_ref[...])
            u_val = conv(taps[0], wv_ref, bv_ref)
            u_gate = conv(taps[1], wg_ref, bg_ref)
            dfc = df_ref[pl.ds(r0, FR), :]
            sg = _sig(u_gate)
            d_val = dfc * u_gate * sg
            d_gate = dfc * u_val * sg * (1.0 + u_gate * (1.0 - sg))
            dvpad[pl.ds(r0, FR), :] = d_val
            dgpad[pl.ds(r0, FR), :] = d_gate
            for h, dd in enumerate((d_val, d_gate)):
                for j in range(3):
                    dw_ref[h, 8 * j:8 * j + 8, :] += _colsum8(dd * taps[h][j])
                dw_ref[h, 24:32, :] += _colsum8(dd)

        chunk(True, 0)

        def loop(i, carry):
            chunk(False, i)
            return carry

        lax.fori_loop(1, S // FR, loop, 0)

        def back(i, carry):
            r0 = pl.multiple_of(i * FR, FR)
            for h, (dpad, w_ref) in enumerate(((dvpad, wv_ref), (dgpad, wg_ref))):
                win = dpad[pl.ds(r0, FR + 8), :]
                du = (win[0:FR, :] * w_ref[2:3, :] + win[1:1 + FR, :] * w_ref[1:2, :]
                      + win[2:2 + FR, :] * w_ref[0:1, :])
                du_ref[h, pl.ds(r0, FR), :] = du.astype(bf16)
            return carry

        lax.fori_loop(0, S // FR, back, 0)

    us = lambda h: pl.BlockSpec((None, S, CCW), lambda b, cb: (h, b, cb))
    ws = lambda h: pl.BlockSpec((3, CCW), lambda b, cb: (0, h * NFB + cb))
    bs = lambda h: pl.BlockSpec((1, CCW), lambda b, cb: (0, h * NFB + cb))
    return pl.pallas_call(
        body, name="ffn_bwd", grid=(nb, NFB),
        in_specs=[us(0), us(1), pl.BlockSpec((S, CCW), lambda b, cb: (b, cb)), ws(0), ws(1), bs(0), bs(1)],
        out_specs=[pl.BlockSpec((2, S, CCW), lambda b, cb: (0, b, cb)),
                   pl.BlockSpec((None, 2, 32, CCW), lambda b, cb: (b, 0, 0, cb))],
        out_shape=[jax.ShapeDtypeStruct((2, T, D_FF), bf16), jax.ShapeDtypeStruct((nb, 2, 32, D_FF), f32)],
        scratch_shapes=[pltpu.VMEM((S + 8, CCW), f32), pltpu.VMEM((S + 8, CCW), f32),
                        pltpu.VMEM((4, FR, CCW), f32)],
        compiler_params=_cparams(("parallel", "parallel")))(u3, u3, df, ffn_w, ffn_w, ffn_b, ffn_b)


AB = ATTN_BLOCK


def _attn_bias_np():
    slopes = (np.float32(2.0) ** (np.float32(-8.0) * np.arange(1, N_HEADS + 1, dtype=np.float32)
                                  / np.float32(N_HEADS))).astype(np.float32)
    steps = (np.arange(AB)[:, None] + AB) - np.arange(2 * AB)[None, :]
    own = (np.arange(2 * AB) >= AB)[None, :]
    out = []
    for window, dil in GROUPS:
        valid = (steps >= 0) & (steps <= window // dil)
        dist = slopes[:, None, None] * (steps * dil).astype(np.float32)[None]
        kinds = [np.where(v[None], dist, np.float32(MASK_BIAS)) for v in (valid, valid & own)]
        out.append(np.stack(kinds, axis=1))
    return np.stack(out).astype(np.float32)


def _attn_bias():
    return jnp.asarray(_attn_bias_np())


def _head_masks():
    lane = lax.broadcasted_iota(jnp.int32, (1, 128), 1)
    return (lane < HEAD_DIM, lane >= HEAD_DIM)


def _perm_chunks(S, d):
    L = S // d
    ch = min(L, 256)
    out = []
    for r in range(d):
        for c in range(L // ch):
            start = r + d * ch * c
            out.append((pl.ds(start, ch, stride=d) if d > 1 else pl.ds(start, ch), r * L + c * ch, ch))
    return out


def _stack_heads(x, masks):
    return jnp.concatenate([jnp.where(masks[0], x, 0), jnp.where(masks[1], x, 0)], axis=0)


_NT = (((1,), (1,)), ((), ()))
_TN = (((0,), (0,)), ((), ()))
SCH = 32


def _attn_fwd(qn, kn, z8, bias, S):
    T = qn.shape[0]
    nb = T // S
    nblk = S // AB

    def body(q_ref, k_ref, v_ref, bias_ref, o_ref, ob_ref, lse_ref, qs, ks, vs, s2, p2, ogp, lgp, *group_scratch):
        og, lg = group_scratch[:3], group_scratch[3:]
        masks = _head_masks()
        ks[0:AB, :] = jnp.zeros((AB, 128), bf16)
        vs[0:AB, :] = jnp.zeros((AB, 128), bf16)

        for g, (_, d) in enumerate(GROUPS):
            nsub = S // (d * AB)
            chunks = _perm_chunks(S, d)
            for src, dst, ch in chunks:
                qs[dst:dst + ch, :] = q_ref[src, :].astype(bf16)
                ks[AB + dst:AB + dst + ch, :] = k_ref[src, :].astype(bf16)
                vs[AB + dst:AB + dst + ch, :] = v_ref[src, :].astype(bf16)
            od, ld = (og[g], lg[g]) if d == 1 else (ogp, lgp)

            def scores(j, carry):
                r0 = pl.multiple_of(j * AB, AB)
                q2 = _stack_heads(qs[pl.ds(r0, AB), :], masks)
                s2[j] = lax.dot_general(q2, ks[pl.ds(r0, 2 * AB), :], _NT, preferred_element_type=f32)
                return carry

            lax.fori_loop(0, nblk, scores, 0, unroll=8)

            def softmax(j, carry, g=g, nsub=nsub, ld=ld):
                r0 = pl.multiple_of(j * AB, AB)
                kind = (j % nsub == 0).astype(jnp.int32)
                for cc in range(AB // SCH):
                    lses = []
                    for hh in range(2):
                        rows = pl.ds(hh * AB + cc * SCH, SCH)
                        sb = s2[j, rows, :] - bias_ref[g, hh, kind, cc * SCH:(cc + 1) * SCH, :]
                        m = jnp.max(sb, axis=-1, keepdims=True)
                        p = jnp.exp(sb - m)
                        den = jnp.sum(p, axis=-1, keepdims=True)
                        p2[j, rows, :] = (p * (1.0 / den)).astype(bf16)
                        lses.append(m + jnp.log(den))
                    ld[pl.ds(r0 + cc * SCH, SCH), :] = jnp.where(masks[0], lses[0], lses[1])
                return carry

            lax.fori_loop(0, nblk, softmax, 0, unroll=2)

            def values(j, carry, od=od):
                r0 = pl.multiple_of(j * AB, AB)
                pv2 = jnp.dot(p2[j], vs[pl.ds(r0, 2 * AB), :], preferred_element_type=f32)
                od[pl.ds(r0, AB), :] = jnp.where(masks[0], pv2[:AB], pv2[AB:])
                return carry

            lax.fori_loop(0, nblk, values, 0, unroll=8)

            if d > 1:
                for src, dst, ch in chunks:
                    og[g][src, :] = ogp[dst:dst + ch, :]
                    lg[g][src, :] = lgp[dst:dst + ch, :]

        def combine(i, carry):
            rr = pl.ds(pl.multiple_of(i * 256, 256), 256)
            l0, l1, l2 = lg[0][rr, :], lg[1][rr, :], lg[2][rr, :]
            mx = jnp.maximum(jnp.maximum(l0, l1), l2)
            e0, e1, e2 = jnp.exp(l0 - mx), jnp.exp(l1 - mx), jnp.exp(l2 - mx)
            den = e0 + e1 + e2
            o = (e0 * og[0][rr, :] + e1 * og[1][rr, :] + e2 * og[2][rr, :]) / den
            o_ref[rr, :] = o
            ob_ref[rr, :] = o.astype(bf16)
            lse_ref[rr, :] = mx + jnp.log(den)
            return carry

        lax.fori_loop(0, S // 256, combine, 0)

    blk = pl.BlockSpec((S, 128), lambda b, hp: (b, hp))
    return pl.pallas_call(
        body, name="attn_fwd", grid=(nb, N_HEADS // 2),
        in_specs=[blk, blk, pl.BlockSpec((None, S, 128), lambda b, hp: (Z_V, b, hp)),
                  pl.BlockSpec((3, 2, 2, AB, 2 * AB), lambda b, hp: (0, hp, 0, 0, 0))],
        out_specs=[blk, blk, blk],
        out_shape=[jax.ShapeDtypeStruct((T, D), f32), jax.ShapeDtypeStruct((T, D), bf16),
                   jax.ShapeDtypeStruct((T, D), f32)],
        scratch_shapes=[pltpu.VMEM((S, 128), bf16), pltpu.VMEM((S + AB, 128), bf16), pltpu.VMEM((S + AB, 128), bf16),
                        pltpu.VMEM((nblk, 2 * AB, 2 * AB), f32), pltpu.VMEM((nblk, 2 * AB, 2 * AB), bf16),
                        pltpu.VMEM((S, 128), f32), pltpu.VMEM((S, 128), f32)] + [pltpu.VMEM((S, 128), f32)] * 6,
        compiler_params=_cparams(("parallel", "parallel")))(qn, kn, z8, bias)


def _attn_bwd(qn, kn, z8, do, o, lse, bias, bd, S, after):
    T = qn.shape[0]
    nb = T // S

    nblk = S // AB

    def body(q_ref, k_ref, v_ref, do_ref, o_ref, lse_ref, bias_ref, bd_ref, after_ref, dq_ref, dk_ref, dv_ref,
             delta, qs, ks, vs, dos, lsp, dlp, s2, dp2, p2, ds2, dqp, dkp, dvp):
        del after_ref
        masks = _head_masks()
        bdv = bd_ref[...]
        dq_ref[...] = jnp.zeros_like(dq_ref)
        dk_ref[...] = jnp.zeros_like(dk_ref)
        dv_ref[...] = jnp.zeros_like(dv_ref)
        ks[0:AB, :] = jnp.zeros((AB, 128), bf16)
        vs[0:AB, :] = jnp.zeros((AB, 128), bf16)

        def prep(i, carry):
            rr = pl.ds(pl.multiple_of(i * 256, 256), 256)
            delta[rr, :] = _head_sum(do_ref[rr, :] * o_ref[rr, :], bdv)
            return carry

        lax.fori_loop(0, S // 256, prep, 0)

        for g, (_, d) in enumerate(GROUPS):
            nsub = S // (d * AB)
            chunks = _perm_chunks(S, d)
            for src, dst, ch in chunks:
                qs[dst:dst + ch, :] = q_ref[src, :].astype(bf16)
                ks[AB + dst:AB + dst + ch, :] = k_ref[src, :].astype(bf16)
                vs[AB + dst:AB + dst + ch, :] = v_ref[src, :].astype(bf16)
                dos[dst:dst + ch, :] = do_ref[src, :].astype(bf16)
                lsp[dst:dst + ch, :] = lse_ref[src, :]
                dlp[dst:dst + ch, :] = delta[src, :]
            dkp[...] = jnp.zeros_like(dkp)
            dvp[...] = jnp.zeros_like(dvp)

            def scores(j, carry):
                r0 = pl.multiple_of(j * AB, AB)
                q2 = _stack_heads(qs[pl.ds(r0, AB), :], masks)
                do2 = _stack_heads(dos[pl.ds(r0, AB), :], masks)
                s2[j] = lax.dot_general(q2, ks[pl.ds(r0, 2 * AB), :], _NT, preferred_element_type=f32)
                dp2[j] = lax.dot_general(do2, vs[pl.ds(r0, 2 * AB), :], _NT, preferred_element_type=f32)
                return carry

            lax.fori_loop(0, nblk, scores, 0, unroll=8)

            def probs(j, carry, g=g, nsub=nsub):
                r0 = pl.multiple_of(j * AB, AB)
                kind = (j % nsub == 0).astype(jnp.int32)
                for cc in range(AB // SCH):
                    lse_c = lsp[pl.ds(r0 + cc * SCH, SCH), :]
                    del_c = dlp[pl.ds(r0 + cc * SCH, SCH), :]
                    for hh in range(2):
                        c0 = hh * HEAD_DIM
                        rows = pl.ds(hh * AB + cc * SCH, SCH)
                        sb = s2[j, rows, :] - bias_ref[g, hh, kind, cc * SCH:(cc + 1) * SCH, :]
                        p = jnp.exp(sb - lse_c[:, c0:c0 + 1])
                        p2[j, rows, :] = p.astype(bf16)
                        ds2[j, rows, :] = (p * (dp2[j, rows, :] - del_c[:, c0:c0 + 1])).astype(bf16)
                return carry

            lax.fori_loop(0, nblk, probs, 0, unroll=2)

            def grads(j, carry):
                r0 = pl.multiple_of(j * AB, AB)
                q2 = _stack_heads(qs[pl.ds(r0, AB), :], masks)
                do2 = _stack_heads(dos[pl.ds(r0, AB), :], masks)
                dsb = ds2[j]
                t = jnp.dot(dsb, ks[pl.ds(r0, 2 * AB), :], preferred_element_type=f32)
                dqp[pl.ds(r0, AB), :] = jnp.where(masks[0], t[:AB], t[AB:])
                dkp[pl.ds(r0, 2 * AB), :] += lax.dot_general(dsb, q2, _TN, preferred_element_type=f32)
                dvp[pl.ds(r0, 2 * AB), :] += lax.dot_general(p2[j], do2, _TN, preferred_element_type=f32)
                return carry

            lax.fori_loop(0, nblk, grads, 0, unroll=4)

            for src, dst, ch in chunks:
                dq_ref[src, :] += dqp[dst:dst + ch, :]
                dk_ref[src, :] += dkp[AB + dst:AB + dst + ch, :]
                dv_ref[src, :] += dvp[AB + dst:AB + dst + ch, :]

    blk = pl.BlockSpec((S, 128), lambda b, hp: (b, hp))
    row = lambda dt, pad=0: pltpu.VMEM((S + pad, 128), dt)
    blocks = lambda dt: pltpu.VMEM((nblk, 2 * AB, 2 * AB), dt)
    return pl.pallas_call(
        body, name="attn_bwd", grid=(nb, N_HEADS // 2),
        in_specs=[blk, blk, pl.BlockSpec((None, S, 128), lambda b, hp: (Z_V, b, hp)), blk, blk, blk,
                  pl.BlockSpec((3, 2, 2, AB, 2 * AB), lambda b, hp: (0, hp, 0, 0, 0)),
                  pl.BlockSpec((128, 128), lambda b, hp: (0, 0)), pl.BlockSpec(memory_space=pl.ANY)],
        out_specs=[blk, blk, blk],
        out_shape=[jax.ShapeDtypeStruct((T, D), f32)] * 3,
        scratch_shapes=[row(f32), row(bf16), row(bf16, AB), row(bf16, AB), row(bf16), row(f32), row(f32),
                        blocks(f32), blocks(f32), blocks(bf16), blocks(bf16), row(f32), row(f32, AB), row(f32, AB)],
        compiler_params=_cparams(("parallel", "parallel")))(qn, kn, z8, do, o, lse, bias, bd, after)


def _any_spec():
    return pl.BlockSpec(memory_space=pl.ANY)


def _allgather_rows(shards, n_full):
    n = len(shards)

    def body(*refs):
        ins, outs = refs[:n], refs[n:2 * n]
        send_sems, recv_sems, local_sems = refs[2 * n:]
        x, y, c, me = _my_pos()
        sibling = (x, y, 1 - c)
        chips = [(1 - x, y), (x, 1 - y), (1 - x, 1 - y)]

        def idx(px, py, pc):
            return 4 * px + 2 * py + pc

        def copy(a, k, blk, to, src=None):
            return pltpu.make_async_remote_copy(
                src_ref=outs[a].at[blk] if src is None else src, dst_ref=outs[a].at[blk],
                send_sem=send_sems.at[a, k], recv_sem=recv_sems.at[a, k], device_id=to, device_id_type=MESH)

        mine = [pltpu.make_async_copy(ins[a], outs[a].at[me], local_sems.at[a]) for a in range(n)]
        for cp in mine:
            cp.start()
        first = []
        for a in range(n_full):
            first.append(copy(a, 0, me, sibling, src=ins[a]))
            first += [copy(a, 1 + j, me, (*chip, c), src=ins[a]) for j, chip in enumerate(chips)]
        for cp in first:
            cp.start()
        passed = []
        for a in range(n_full):
            for j, chip in enumerate(chips):
                blk = idx(*chip, c)
                copy(a, 1 + j, blk, (x, y, c)).wait_recv()
                cp = copy(a, 4 + j, blk, sibling)
                cp.start()
                passed.append(cp)
        for a in range(n_full):
            copy(a, 0, idx(x, y, 1 - c), (x, y, c)).wait_recv()
            for j, chip in enumerate(chips):
                copy(a, 4 + j, idx(*chip, 1 - c), (x, y, c)).wait_recv()
        for cp in first + passed:
            cp.wait_send()
        for cp in mine:
            cp.wait()

    return pl.pallas_call(
        body, name="allgather_weights",
        in_specs=[_any_spec()] * n, out_specs=[_any_spec()] * n,
        out_shape=[jax.ShapeDtypeStruct((N_DEV,) + s.shape, s.dtype) for s in shards],
        scratch_shapes=[pltpu.SemaphoreType.DMA((n_full, 7)), pltpu.SemaphoreType.DMA((n_full, 7)),
                        pltpu.SemaphoreType.DMA((n,))],
    )(*shards)


def _peer(x, y, c, k):
    tx = 1 - x if (k >> 2) & 1 else x
    ty = 1 - y if (k >> 1) & 1 else y
    tc = 1 - c if k & 1 else c
    return (tx, ty, tc), 4 * tx + 2 * ty + tc


_PEER_ORDER = (2, 4, 6, 3, 5, 7, 1)


_HBM = pl.BlockSpec(memory_space=pltpu.HBM)
_SEM = pl.BlockSpec(memory_space=pltpu.SEMAPHORE)
_EFFECT = pltpu.SideEffectType.DATAFLOW_SIDE_EFFECTING


def _exchange_copies(srcs, lands, send_sems, recv_sems, gather, half):
    x, y, c, me = _my_pos()
    pick = lambda px, py: None if half is None else ((px == py) if half == 0 else (px != py))
    copies = []
    for k in _PEER_ORDER:
        tgt, tidx = _peer(x, y, c, k)
        for a in range(len(srcs)):
            copies.append((pltpu.make_async_remote_copy(
                src_ref=srcs[a] if gather else srcs[a].at[tidx], dst_ref=lands[a].at[me],
                send_sem=send_sems.at[7 * a + k - 1], recv_sem=recv_sems.at[7 * a + k - 1],
                device_id=tgt, device_id_type=MESH), pick(tgt[0], tgt[1])))
    return copies, pick(x, y)


def _when(cond, fn):
    if cond is None:
        fn()
    else:
        pl.when(cond)(fn)


def _exchange_start(name, srcs, lands=None, after=None, gather=None, half=None):
    n = len(srcs)
    gather = (lands is not None) if gather is None else gather
    if lands is None:
        lands = [lax.empty(g.shape, g.dtype) for g in srcs]
    extra = [] if after is None else [after]

    def body(*refs):
        src_refs, land_refs = refs[:n], refs[n:2 * n]
        send_sems, recv_sems = refs[2 * n + len(extra)], refs[2 * n + len(extra) + 1]
        token = refs[-1]
        for cp, sends in _exchange_copies(src_refs, land_refs, send_sems, recv_sems, gather, half)[0]:
            _when(sends, cp.start)
        token[...] = jnp.zeros_like(token)

    hbm = lambda a: pltpu.with_memory_space_constraint(a, pltpu.HBM)
    outs = pl.pallas_call(
        body, name=name,
        out_shape=(pltpu.SemaphoreType.DMA((7 * n,)), pltpu.SemaphoreType.DMA((7 * n,)),
                   *[pltpu.HBM(g.shape, g.dtype) for g in list(srcs) + list(lands)],
                   jax.ShapeDtypeStruct((8, 128), f32)),
        in_specs=[_HBM] * (2 * n) + [pl.BlockSpec(memory_space=pl.ANY)] * len(extra),
        out_specs=(_SEM, _SEM, *([_HBM] * (2 * n)), pl.BlockSpec(memory_space=pltpu.VMEM)),
        input_output_aliases={i: 2 + i for i in range(2 * n)},
        compiler_params=pltpu.CompilerParams(has_side_effects=_EFFECT),
    )(*[hbm(g) for g in srcs], *[hbm(g) for g in lands], *extra)
    return outs[0], outs[1], list(outs[2:2 + n]), list(outs[2 + n:2 + 2 * n]), outs[-1], gather, half


def _exchange_wait(name, started, after):
    send_sems, recv_sems, srcs, lands, _, gather, half = started
    n = len(srcs)
    after = list(after) if isinstance(after, (list, tuple)) else [after]

    def body(*refs):
        src_refs, land_refs = refs[:n], refs[n:2 * n]
        s_sems, r_sems = refs[2 * n], refs[2 * n + 1]
        copies, receives = _exchange_copies(src_refs, land_refs, s_sems, r_sems, gather, half)
        for cp, sends in copies:
            _when(sends, cp.wait_send)
            _when(receives, cp.wait_recv)

    outs = pl.pallas_call(
        body, name=name,
        out_shape=tuple(pltpu.HBM(a.shape, a.dtype) for a in list(srcs) + list(lands)),
        in_specs=[_HBM] * (2 * n) + [_SEM, _SEM] + [pl.BlockSpec(memory_space=pl.ANY)] * len(after),
        out_specs=tuple([_HBM] * (2 * n)),
        input_output_aliases={i: i for i in range(2 * n)},
        compiler_params=pltpu.CompilerParams(has_side_effects=_EFFECT),
    )(*srcs, *lands, send_sems, recv_sems, *after)
    return list(outs[:n]), list(outs[n:])


SMALL_ROWS = 128


def _small_start(name, sg, after=None):
    return _exchange_start(name, [sg], [lax.empty((N_DEV,) + sg.shape, f32)], after=after)


def _small_sum(name, me, started, after):
    (own,), (slots,) = _exchange_wait(name + "_wait", started, after)

    def body(me_ref, s_ref, own_ref, out_ref):
        acc = None
        for p in range(N_DEV):
            term = lax.cond(me_ref[0] == p, lambda: own_ref[...], lambda p=p: s_ref[p])
            acc = term if acc is None else acc + term
        out_ref[...] = acc

    return pl.pallas_call(
        body, name=name + "_sum",
        in_specs=[pl.BlockSpec(memory_space=pltpu.SMEM), pl.BlockSpec(memory_space=pltpu.VMEM),
                  pl.BlockSpec(memory_space=pltpu.VMEM)],
        out_specs=pl.BlockSpec(memory_space=pltpu.VMEM),
        out_shape=jax.ShapeDtypeStruct(own.shape, f32))(me, slots, own)


def _adam_math(g, w, m, v):
    m = ADAM_B1 * m + (1.0 - ADAM_B1) * g
    v = ADAM_B2 * v + (1.0 - ADAM_B2) * (g * g)
    m_hat = m / (1.0 - ADAM_B1 ** ADAM_STEP)
    v_hat = v / (1.0 - ADAM_B2 ** ADAM_STEP)
    delta = -ADAM_LR * (m_hat / (jnp.sqrt(v_hat) + ADAM_EPS) + ADAM_WD * w)
    return delta, m, v


def _adam_slots(name, me, slots, own, w, m, v, tr, transposed=False):
    rows = slots.shape[1]

    def body(me_ref, s_ref, own_ref, w_ref, m_ref, v_ref, g_ref, d_ref, nm_ref, nv_ref):
        mine = own_ref[...]
        g = None
        for p in range(N_DEV):
            term = lax.cond(me_ref[0] == p, lambda: mine, lambda p=p: s_ref[p]).astype(f32)
            g = term if g is None else g + term
        if transposed:
            g = g.T
        delta, nm, nv = _adam_math(g, w_ref[...], m_ref[...], v_ref[...])
        g_ref[...] = g
        d_ref[...] = delta
        nm_ref[...] = nm
        nv_ref[...] = nv

    mode = dict(pipeline_mode=pl.Buffered(1)) if rows == tr else {}
    if transposed:
        rs = pl.BlockSpec((D, tr), lambda i, me_ref: (0, i))
        rs_in = pl.BlockSpec((D, tr), lambda i, me_ref: (0, i), **mode)
    else:
        rs = pl.BlockSpec((tr, D), lambda i, me_ref: (i, 0))
        rs_in = pl.BlockSpec((tr, D), lambda i, me_ref: (i, 0), **mode)
    return pl.pallas_call(
        body, name=name,
        grid_spec=pltpu.PrefetchScalarGridSpec(
            num_scalar_prefetch=1, grid=(rows // tr,),
            in_specs=[pl.BlockSpec((N_DEV, tr, D), lambda i, me_ref: (0, i, 0), **mode),
                      pl.BlockSpec((None, tr, D), lambda i, me_ref: (me_ref[0], i, 0), **mode), rs_in, rs_in, rs_in],
            out_specs=[rs] * 4),
        out_shape=[jax.ShapeDtypeStruct(w.shape, f32)] * 4,
        compiler_params=_cparams(("parallel",)))(me, slots, own, w, m, v)


def _adam_small(g, w, m, v):
    def body(g_ref, w_ref, m_ref, v_ref, d_ref, nm_ref, nv_ref):
        delta, nm, nv = _adam_math(g_ref[...], w_ref[...], m_ref[...], v_ref[...])
        d_ref[...] = delta
        nm_ref[...] = nm
        nv_ref[...] = nv

    return pl.pallas_call(body, name="adam_small", out_shape=[jax.ShapeDtypeStruct(g.shape, f32)] * 3)(g, w, m, v)


FFN_PAD = 6 * D


_SMALL_PARTS = (("norm1_g", 1), ("gate_b", 2), ("conv_w", CONV_WIDTH), ("conv_b", 1), ("conv_norm_g", 1),
                ("q_norm_g", 1), ("k_norm_g", 1), ("norm2_g", 1), ("ffn_conv_w", 18), ("ffn_conv_b", 6), ("last", 1))


def _small_offsets():
    out, row = {}, 0
    for name, rows in _SMALL_PARTS:
        out[name] = row
        row += -(-rows // 8) * 8
    assert row == SMALL_ROWS
    return out


def _pack_small(norm1_g, gate_b, conv_w, conv_b, conv_norm_g, q_norm_g, k_norm_g, norm2_g, ffn_conv_w, ffn_conv_b,
                last_row=None):
    pad_h = lambda a: jnp.pad(a, ((0, 0), (0, D - HEAD_DIM)))
    pad_f = lambda a: jnp.pad(a, ((0, 0), (0, FFN_PAD - 2 * D_FF))).reshape(-1, D)
    parts = [norm1_g, gate_b.reshape(2, D), conv_w, conv_b, conv_norm_g, pad_h(q_norm_g), pad_h(k_norm_g), norm2_g,
             pad_f(ffn_conv_w), pad_f(ffn_conv_b), jnp.zeros((1, D), f32) if last_row is None else last_row]
    return jnp.concatenate([jnp.pad(p, ((0, -p.shape[0] % 8), (0, 0))) for p in parts], axis=0)


def _unpack_small(p):
    o = _small_offsets()
    rows = lambda name, n: p[o[name]:o[name] + n]
    ffn = lambda a: a.reshape(-1, FFN_PAD)[:, :2 * D_FF]
    return dict(
        norm1_g=rows("norm1_g", 1), gate_b=rows("gate_b", 2).reshape(1, 2 * D), conv_w=rows("conv_w", CONV_WIDTH),
        conv_b=rows("conv_b", 1), conv_norm_g=rows("conv_norm_g", 1), q_norm_g=rows("q_norm_g", 1)[:, :HEAD_DIM],
        k_norm_g=rows("k_norm_g", 1)[:, :HEAD_DIM], norm2_g=rows("norm2_g", 1),
        ffn_conv_w=ffn(rows("ffn_conv_w", 18)), ffn_conv_b=ffn(rows("ffn_conv_b", 6)))


_ADAM_TILE = {896: 128, 704: 704, 128: 128, 352: 176}


def kernel(x, norm1_g, w_in, gate_b, conv_w, conv_b, conv_norm_g, w_conv_out, q_norm_g, k_norm_g, w_attn_out, w_out, norm2_g, w_up, ffn_conv_w, ffn_conv_b, w_down, loss_target, m_norm1_g, m_w_in, m_gate_b, m_conv_w, m_conv_b, m_conv_norm_g, m_w_conv_out, m_q_norm_g, m_k_norm_g, m_w_attn_out, m_w_out, m_norm2_g, m_w_up, m_ffn_conv_w, m_ffn_conv_b, m_w_down, v_norm1_g, v_w_in, v_gate_b, v_conv_w, v_conv_b, v_conv_norm_g, v_w_conv_out, v_q_norm_g, v_k_norm_g, v_w_attn_out, v_w_out, v_norm2_g, v_w_up, v_ffn_conv_w, v_ffn_conv_b, v_w_down):
    BL, S, _ = x.shape
    T = BL * S
    me = 4 * lax.axis_index("x") + 2 * lax.axis_index("y") + lax.axis_index("c")
    xt = x.reshape(T, D)
    target = loss_target.reshape(T, D)

    big = dict(w_in=(w_in[0], m_w_in[0], v_w_in[0]), w_up=(w_up[0], m_w_up[0], v_w_up[0]),
               w_conv_out=(w_conv_out[0], m_w_conv_out[0], v_w_conv_out[0]),
               w_attn_out=(w_attn_out[0], m_w_attn_out[0], v_w_attn_out[0]),
               w_out=(w_out[0], m_w_out[0], v_w_out[0]), w_down=(w_down[0], m_w_down[0], v_w_down[0]))
    order = ["w_in", "w_conv_out", "w_attn_out", "w_out", "w_up", "w_down"]
    shards = [(big[n][0].T if n in ("w_in", "w_up") else big[n][0]).astype(bf16) for n in order]
    gathered = _allgather_rows(shards, 1)
    ga_proj = _exchange_start("gather_start_proj", shards[1:4], gathered[1:4], after=gathered[0])
    ga_ffn = _exchange_start("gather_start_ffn", shards[4:6], gathered[4:6], after=ga_proj[4])
    W = {"w_in": gathered[0].reshape(-1, D)}

    def place_cols(shard, full_cols):
        z = jnp.zeros((shard.shape[0], full_cols), f32)
        return lax.dynamic_update_slice(z, shard, (0, me * shard.shape[1]))

    zr = lambda a: jnp.zeros_like(a)
    conv_local = _pack_small(
        zr(norm1_g), zr(gate_b), place_cols(conv_w[0], D), zr(conv_b), zr(conv_norm_g), zr(q_norm_g), zr(k_norm_g),
        zr(norm2_g), place_cols(ffn_conv_w[0], 2 * D_FF), zr(ffn_conv_b))
    ga_conv = _small_start("gather_conv_start", conv_local, after=ga_ffn[4])

    bd = (jnp.arange(128)[:, None] // HEAD_DIM == jnp.arange(128)[None, :] // HEAD_DIM).astype(bf16)
    bias = _attn_bias()
    qg = jnp.tile(q_norm_g, (1, N_HEADS))
    kg = jnp.tile(k_norm_g, (1, N_HEADS))

    h = _norm1_fwd(xt, norm1_g)
    z8 = _matmul_call(
        "mm_z", h, W["w_in"],
        pl.BlockSpec((2048, D), lambda i, j, k: (i, 0)),
        pl.BlockSpec((1024, D), lambda i, j, k: (_wsec_of_zsec(j), 0)),
        pl.BlockSpec((None, 2048, D), lambda i, j, k: (j, i, 0)),
        jax.ShapeDtypeStruct((8, T, D), f32), (T // 2048, 7, 1), "nt", 1, 2048, 1024, after=ga_conv[4])
    conv_all = _unpack_small(_small_sum("gather_conv", me.reshape(1), ga_conv, z8))
    conv_w_full, ffn_w_full = conv_all["conv_w"], conv_all["ffn_conv_w"]
    c = _conv_fwd(z8, conv_w_full, conv_b, S)
    s = _convnorm_fwd(c, conv_norm_g)
    qn, kn = _qk_fwd(z8, qg, kg, bd)
    for n, g in zip(order[1:4], _exchange_wait("gather_wait_proj", ga_proj, qn)[1]):
        W[n] = g.reshape(-1, D)
    ya = _matmul("mm_ya", s, W["w_conv_out"], "nn", f32)
    o, ob, lse = _attn_fwd(qn, kn, z8, bias, S)
    yb = _matmul("mm_yb", ob, W["w_attn_out"], "nn", f32)
    mixed = _gate_fwd(z8, gate_b, ya, yb)
    x1, h2 = _out_norm2_fwd(mixed, W["w_out"], xt, norm2_g)
    for n, g in zip(order[4:6], _exchange_wait("gather_wait_ffn", ga_ffn, x1)[1]):
        W[n] = g.reshape(-1, D)
    TNU = D_FF // 2
    u3 = _matmul_call(
        "mm_u", h2, W["w_up"],
        pl.BlockSpec((1024, D), lambda i, j, k: (i, 0)),
        pl.BlockSpec((TNU, D), lambda i, j, k: (j, 0)),
        pl.BlockSpec((None, 1024, TNU), lambda i, j, k: (j // 2, i, j % 2)),
        jax.ShapeDtypeStruct((2, T, D_FF), f32), (T // 1024, 4, 1), "nt", 1, 1024, TNU)
    f = _ffn_fwd(u3, ffn_w_full, ffn_conv_b, S)
    dy, dyb, lacc = _down_loss_fwd(f, W["w_down"], x1, target)
    loss_local = 0.5 / D * jnp.sum(lacc)

    df = _matmul("mm_df", dyb, W["w_down"], "nt", f32, tn=TNU)
    g_w_down = _matmul("mm_dwdn", f, dyb, "tn", bf16, tm=TNU)
    du3, dffn = _ffn_bwd(u3, df, ffn_w_full, ffn_conv_b, S)
    g_w_up = _matmul_call(
        "mm_dwup", du3, h2,
        pl.BlockSpec((None, T, TNU), lambda i, j, k: (i // 2, 0, i % 2)),
        pl.BlockSpec((T, D), lambda i, j, k: (0, 0)),
        pl.BlockSpec((TNU, D), lambda i, j, k: (i, 0)),
        jax.ShapeDtypeStruct((2 * D_FF, D), bf16), (4, 1, 1), "tn", 1, TNU, D)
    blocks8 = lambda a: a.reshape(N_DEV, -1, D)
    ex_ffn = _exchange_start("scatter_start_ffn", [blocks8(g_w_up), blocks8(g_w_down)])
    dx1, dx1b, dg_norm2 = _up_norm2_bwd(du3, W["w_up"], x1, dy, norm2_g, ex_ffn[4])
    g_w_out = _matmul("mm_dwo", mixed, dx1b, "tn", bf16, tm=512)
    dz8 = lax.empty((8, T, D), bf16)
    dya, dyb2, dz8, dg_gate = _out_gate_bwd(dx1b, W["w_out"], z8, gate_b, ya, yb, dz8)
    ds = _matmul("mm_ds", dya, W["w_conv_out"], "nt", f32)
    g_w_conv_out = _matmul("mm_dwco", s, dya, "tn", bf16, tm=512)
    g_w_attn_out = _matmul("mm_dwao", ob, dyb2, "tn", bf16, tm=512)
    ex_proj = _exchange_start("scatter_start_proj", [blocks8(g_w_conv_out), blocks8(g_w_attn_out), blocks8(g_w_out)])
    do = _matmul("mm_do", dyb2, W["w_attn_out"], "nt", f32, after=ex_proj[4])
    dc, dg_convnorm = _convnorm_bwd(c, ds, conv_norm_g)
    dz8a, dconv = _conv_bwd(dc, z8, conv_w_full, dz8, S)
    dwin_specs = lambda zsec, wsec: (
        pl.BlockSpec((None, T, D), lambda i, j, k: (zsec(i), 0, 0)), pl.BlockSpec((T, D), lambda i, j, k: (0, 0)),
        pl.BlockSpec((1024, D), lambda i, j, k: (wsec(i), 0)), jax.ShapeDtypeStruct((7 * D, D), bf16))
    g_w_in = _matmul_call("mm_dwin_a", dz8a, h, *dwin_specs(lambda i: i, lambda i: jnp.where(i < 2, i, i + 3)),
                          (4, 1, 1), "tn", 1, D, D)
    ex_in_a = _exchange_start("scatter_start_in_a", [blocks8(g_w_in)], half=0)
    dqn, dkn, dv = _attn_bwd(qn, kn, z8, do, o, lse, bias, bd, S, ex_in_a[4])
    dz8b, dg_q, dg_k = _qk_bwd(z8, dqn, dkn, dv, qg, kg, bd, dz8a)
    g_w_in = _matmul_call("mm_dwin_b", dz8b, h, *dwin_specs(lambda i: i + 4, lambda i: i + 2),
                          (3, 1, 1), "tn", 1, D, D, fill=ex_in_a[2][0].reshape(7 * D, D))
    ex_in_b = _exchange_start("scatter_start_in_b", [blocks8(g_w_in)], ex_in_a[3], gather=False, half=1)
    grad_x, dg_norm1 = _in_norm1_bwd(dz8b, W["w_in"], xt, dx1, norm1_g, ex_in_b[4])

    sum8 = lambda a: a.reshape(-1, 8, a.shape[-1]).sum(axis=1)
    dconv_s = sum8(dconv.sum(axis=0))
    dffn_s = dffn.sum(axis=0).reshape(2, 4, 8, D_FF).sum(axis=2)
    dffn_w = jnp.concatenate([dffn_s[0, :3], dffn_s[1, :3]], axis=1)
    dffn_b = jnp.concatenate([dffn_s[0, 3:4], dffn_s[1, 3:4]], axis=1)
    fold = lambda a: sum8(a).reshape(N_HEADS, HEAD_DIM).sum(axis=0)[None]
    small_g_local = _pack_small(
        sum8(dg_norm1), sum8(dg_gate), dconv_s[:CONV_WIDTH], dconv_s[CONV_WIDTH:], sum8(dg_convnorm),
        fold(dg_q), fold(dg_k), sum8(dg_norm2), dffn_w, dffn_b,
        last_row=jnp.pad(loss_local.reshape(1, 1), ((0, 0), (0, D - 1))))
    sg_start = _small_start("small_grads_start", small_g_local)

    own, slots = {}, {}
    for tag, ex, names_ in (("ffn", ex_ffn, ("w_up", "w_down")),
                            ("proj", ex_proj, ("w_conv_out", "w_attn_out", "w_out"))):
        sent, landed = _exchange_wait("scatter_wait_" + tag, ex, sg_start[4])
        for n, src, land in zip(names_, sent, landed):
            own[n], slots[n] = src, land
    sent, landed = _exchange_wait("scatter_wait_in_a", ex_in_a[:2] + (ex_in_b[2], ex_in_b[3]) + ex_in_a[4:],
                                  sg_start[4])
    sent, landed = _exchange_wait("scatter_wait_in_b", ex_in_b[:2] + (sent, landed) + ex_in_b[4:], sg_start[4])
    own["w_in"], slots["w_in"] = sent[0], landed[0]

    res, adam_done = {}, []
    for n in order:
        w, m, v = big[n]
        outs = _adam_slots("adam_" + n, me.reshape(1), slots[n], own[n], w, m, v, _ADAM_TILE[slots[n].shape[1]],
                           transposed=n in ("w_in", "w_up"))
        adam_done.append(outs[0])
        res[n] = [a[None] for a in outs]
    small_g = _small_sum("small_grads", me.reshape(1), sg_start, adam_done)
    loss = small_g[_small_offsets()["last"], 0]

    col = lambda a, width: lax.dynamic_slice(a, (0, me * width), (a.shape[0], width))
    small_w_true = _pack_small(norm1_g, gate_b, conv_w_full, conv_b, conv_norm_g, q_norm_g, k_norm_g, norm2_g,
                               ffn_w_full, ffn_conv_b)
    place_m = lambda a, full: place_cols(a[0], full)
    small_m = _pack_small(m_norm1_g, m_gate_b, place_m(m_conv_w, D), m_conv_b, m_conv_norm_g, m_q_norm_g, m_k_norm_g,
                          m_norm2_g, place_m(m_ffn_conv_w, 2 * D_FF), m_ffn_conv_b)
    small_v = _pack_small(v_norm1_g, v_gate_b, place_m(v_conv_w, D), v_conv_b, v_conv_norm_g, v_q_norm_g, v_k_norm_g,
                          v_norm2_g, place_m(v_ffn_conv_w, 2 * D_FF), v_ffn_conv_b)
    sd, sm, sv = _adam_small(small_g, small_w_true, small_m, small_v)
    for i, packed in enumerate((small_g, sd, sm, sv)):
        u = _unpack_small(packed)
        u["conv_w"] = col(u["conv_w"], D // N_DEV)
        u["ffn_conv_w"] = col(u["ffn_conv_w"], 2 * D_FF // N_DEV)
        for n, a in u.items():
            res.setdefault(n, [None] * 4)[i] = a[None] if n in ("conv_w", "ffn_conv_w") else a

    names = ["norm1_g", "w_in", "gate_b", "conv_w", "conv_b", "conv_norm_g", "w_conv_out", "q_norm_g", "k_norm_g",
             "w_attn_out", "w_out", "norm2_g", "w_up", "ffn_conv_w", "ffn_conv_b", "w_down"]
    out = [loss, grad_x.reshape(BL, S, D)]
    for i in range(4):
        out += [res[n][i] for n in names]
    return tuple(out)
```

```python
import functools

import jax
import jax.numpy as jnp
import numpy as np
from jax import lax
from jax.experimental import pallas as pl
from jax.experimental.pallas import tpu as pltpu

f32 = jnp.float32
bf16 = jnp.bfloat16

D = 1024
N_HEADS = 16
HEAD_DIM = 64
CONV_WIDTH = 31
D_FF = 2816
GROUPS = ((128, 1), (512, 4), (2048, 16))
ATTN_BLOCK = 128
EPS = 1e-6
N_DEV = 8
MESH = pl.DeviceIdType.MESH

ADAM_LR = 0.001
ADAM_B1 = 0.9
ADAM_B2 = 0.999
ADAM_EPS = 1e-08
ADAM_WD = 0.01
ADAM_STEP = 10

VMEM_LIMIT = 56 * 1024 * 1024
MASK_BIAS = 1e30

Z_AVAL, Z_AGATE, Z_GA, Z_GB, Z_Q, Z_K, Z_V = 0, 1, 2, 3, 4, 5, 6


_W_OF_Z = (0, 1, 5, 6, 2, 3, 4)


def _wsec_of_zsec(j):
    return jnp.where(j < 2, j, jnp.where(j < 4, j + 3, j - 2))


def _zsec_of_wsec(w):
    return jnp.where(w < 2, w, jnp.where(w < 5, w + 2, w - 3))


def _sig(x):
    return 1.0 / (1.0 + jnp.exp(-x))


def _colsum8(x):
    return x.reshape(-1, 8, x.shape[-1]).sum(axis=0)


def _cparams(sem):
    return pltpu.CompilerParams(dimension_semantics=sem, vmem_limit_bytes=VMEM_LIMIT)


def _my_pos():
    x, y, c = lax.axis_index("x"), lax.axis_index("y"), lax.axis_index("c")
    return x, y, c, 4 * x + 2 * y + c


_DIMS = {"nn": ((1,), (0,)), "nt": ((1,), (1,)), "tn": ((0,), (0,))}


def _matmul_call(name, a, b, a_spec, b_spec, o_spec, out_shape, grid, mode, nk, tm, tn, after=None, fill=None):
    dims = (_DIMS[mode], ((), ()))
    extra = ([] if after is None else [after]) + ([] if fill is None else [fill])

    def body(a_ref, b_ref, *rest):
        o_ref, scratch = rest[len(extra)], rest[len(extra) + 1:]
        part = lax.dot_general(a_ref[...], b_ref[...], dims, preferred_element_type=f32)
        if nk == 1:
            o_ref[...] = part.astype(o_ref.dtype)
        else:
            acc = scratch[0]
            k = pl.program_id(2)

            @pl.when(k == 0)
            def _():
                acc[...] = part

            @pl.when(k > 0)
            def _():
                acc[...] += part

            @pl.when(k == nk - 1)
            def _():
                o_ref[...] = acc[...].astype(o_ref.dtype)

    scratch = [] if nk == 1 else [pltpu.VMEM((tm, tn), f32)]
    return pl.pallas_call(
        body, name=name, grid=grid, in_specs=[a_spec, b_spec] + [pl.BlockSpec(memory_space=pl.ANY)] * len(extra),
        out_specs=o_spec, out_shape=out_shape, input_output_aliases={} if fill is None else {1 + len(extra): 0},
        scratch_shapes=scratch, compiler_params=_cparams(("parallel", "parallel", "arbitrary")),
    )(a, b, *extra)


def _matmul(name, a, b, mode, out_dtype, tm=1024, tn=1024, tk=None, after=None):
    if mode == "nn":
        (M, K), (_, N) = a.shape, b.shape
    elif mode == "nt":
        (M, K), (N, _) = a.shape, b.shape
    else:
        (K, M), (_, N) = a.shape, b.shape
    tm, tn = min(tm, M), min(tn, N)
    tk = K if tk is None else tk
    nk = K // tk
    assert M % tm == 0 and N % tn == 0 and K % tk == 0
    if mode == "tn":
        a_spec = pl.BlockSpec((tk, tm), lambda i, j, k: (k, i))
    else:
        a_spec = pl.BlockSpec((tm, tk), lambda i, j, k: (i, k))
    if mode == "nt":
        b_spec = pl.BlockSpec((tn, tk), lambda i, j, k: (j, k))
    else:
        b_spec = pl.BlockSpec((tk, tn), lambda i, j, k: (k, j))
    o_spec = pl.BlockSpec((tm, tn), lambda i, j, k: (i, j))
    return _matmul_call(name, a, b, a_spec, b_spec, o_spec, jax.ShapeDtypeStruct((M, N), out_dtype),
                        (M // tm, N // tn, nk), mode, nk, tm, tn, after=after)


FTM = 512


def _matmul_fused(name, a, b, pairs, epilogue, extras, consts, outs, nt=False, sums=False, passed=(), aliases=None):
    sa, M, kk = a.shape
    na = max(i for i, _ in pairs) + 1
    ne, nc, npass = len(extras), len(consts), len(passed)
    dims = (_DIMS["nt" if nt else "nn"], ((), ()))

    def body(a_ref, b_ref, *rest):
        acc = None
        for i, j in pairs:
            part = lax.dot_general(a_ref[i], b_ref[j], dims, preferred_element_type=f32)
            acc = part if acc is None else acc + part
        epilogue(acc, rest[:ne], rest[ne:ne + nc], rest[ne + nc + npass:])

    whole = lambda arr: pl.BlockSpec(arr.shape, lambda i, nd=arr.ndim: (0,) * nd, pipeline_mode=pl.Buffered(1))
    io_alias = {2 + ne + nc + k: v for k, v in (aliases or {}).items()}
    return pl.pallas_call(
        body, name=name, grid=(M // FTM,),
        in_specs=[pl.BlockSpec((na, FTM, kk), lambda i: (0, i, 0)), whole(b)] + [s for _, s in extras]
        + [whole(c) for c in consts] + [pl.BlockSpec(memory_space=pl.ANY)] * npass,
        out_specs=[s for _, s in outs], out_shape=[s for s, _ in outs], input_output_aliases=io_alias,
        compiler_params=_cparams(("arbitrary" if sums else "parallel",)),
    )(a, b, *[x for x, _ in extras], *consts, *passed)


def _frows(c=D):
    return pl.BlockSpec((FTM, c), lambda i: (i, 0))


def _fsec(s):
    return pl.BlockSpec((None, FTM, D), lambda i: (s, i, 0))


def _rowshape(T, dtype, c=D):
    return (jax.ShapeDtypeStruct((T, c), dtype), _frows(c))


def _sumshape(c=D):
    return (jax.ShapeDtypeStruct((8, c), f32), pl.BlockSpec((8, c), lambda i: (0, 0)))


def _add_colsum(ref, x, cols=None):
    @pl.when(pl.program_id(0) == 0)
    def _():
        if cols is None:
            ref[...] = jnp.zeros_like(ref)
        else:
            ref[:, cols] = jnp.zeros((8, x.shape[-1]), f32)

    if cols is None:
        ref[...] += _colsum8(x)
    else:
        ref[:, cols] += _colsum8(x)


TT = 512


def _rows(c, cb=0, tt=TT):
    return pl.BlockSpec((tt, c), lambda i: (i, cb))


def _sec(s, tt=TT):
    return pl.BlockSpec((None, tt, D), lambda i: (s, i, 0))


def _const(shape):
    return pl.BlockSpec(shape, lambda i: (0,) * len(shape))


def _acc_spec(c):
    return pl.BlockSpec((8, c), lambda i: (0, 0))


def _rms(x):
    return lax.rsqrt(jnp.mean(x * x, axis=-1, keepdims=True) + EPS)


def _rms_bwd(dy_g, xn, rstd):
    return rstd * (dy_g - xn * jnp.mean(dy_g * xn, axis=-1, keepdims=True))


def _head_sum(x, bd):
    parts = []
    for cb in range(x.shape[-1] // 128):
        xb = x[:, cb * 128:(cb + 1) * 128]
        hi = xb.astype(bf16)
        lo = (xb - hi.astype(f32)).astype(bf16)
        parts.append(jnp.dot(hi, bd, preferred_element_type=f32) + jnp.dot(lo, bd, preferred_element_type=f32))
    return parts[0] if len(parts) == 1 else jnp.concatenate(parts, axis=1)


def _norm1_fwd(x, g):
    T = x.shape[0]

    def body(x_ref, g_ref, h_ref):
        xv = x_ref[...]
        h_ref[...] = (xv * _rms(xv) * g_ref[...]).astype(bf16)

    return pl.pallas_call(
        body, name="norm1_fwd", grid=(T // TT,), in_specs=[_rows(D), _const((1, D))], out_specs=_rows(D),
        out_shape=jax.ShapeDtypeStruct((T, D), bf16), compiler_params=_cparams(("parallel",)))(x, g)


def _convnorm_fwd(c, g):
    T = c.shape[0]

    def body(c_ref, g_ref, s_ref):
        cv = c_ref[...]
        r = cv * _rms(cv) * g_ref[...]
        s_ref[...] = (r * _sig(r)).astype(bf16)

    return pl.pallas_call(
        body, name="convnorm_fwd", grid=(T // TT,), in_specs=[_rows(D), _const((1, D))], out_specs=_rows(D),
        out_shape=jax.ShapeDtypeStruct((T, D), bf16), compiler_params=_cparams(("parallel",)))(c, g)


def _qk_fwd(z8, qg, kg, bd):
    T = z8.shape[1]

    def body(q_ref, k_ref, qg_ref, kg_ref, bd_ref, qn_ref, kn_ref):
        bdv = bd_ref[...]
        q = q_ref[...]
        qn_ref[...] = q * lax.rsqrt(_head_sum(q * q, bdv) * (1.0 / HEAD_DIM) + EPS) * qg_ref[...] * (HEAD_DIM ** -0.5)
        k = k_ref[...]
        kn_ref[...] = k * lax.rsqrt(_head_sum(k * k, bdv) * (1.0 / HEAD_DIM) + EPS) * kg_ref[...]

    return pl.pallas_call(
        body, name="qk_fwd", grid=(T // TT,),
        in_specs=[_sec(Z_Q), _sec(Z_K), _const((1, D)), _const((1, D)), _const((128, 128))],
        out_specs=[_rows(D), _rows(D)],
        out_shape=[jax.ShapeDtypeStruct((T, D), f32)] * 2, compiler_params=_cparams(("parallel",)))(z8, z8, qg, kg, bd)


def _gate_fwd(z8, gate_b, ya, yb):
    T = ya.shape[0]

    def body(ga_ref, gb_ref, b_ref, ya_ref, yb_ref, mixed_ref):
        g_a = _sig(ga_ref[...] + b_ref[:, :D])
        g_b = _sig(gb_ref[...] + b_ref[:, D:])
        mixed_ref[...] = (g_a * ya_ref[...] + g_b * yb_ref[...]).astype(bf16)

    return pl.pallas_call(
        body, name="gate_fwd", grid=(T // TT,),
        in_specs=[_sec(Z_GA), _sec(Z_GB), _const((1, 2 * D)), _rows(D), _rows(D)], out_specs=_rows(D),
        out_shape=jax.ShapeDtypeStruct((T, D), bf16), compiler_params=_cparams(("parallel",)))(z8, z8, gate_b, ya, yb)


def _out_norm2_fwd(mixed, w_out, x, g):
    T = x.shape[0]

    def epilogue(acc, extra, const, out):
        x1 = extra[0][...] + acc
        out[0][...] = x1
        out[1][...] = (x1 * _rms(x1) * const[0][...]).astype(bf16)

    return _matmul_fused("mm_t1_norm2", mixed[None], w_out[None], ((0, 0),), epilogue, [(x, _frows())], [g],
                         [_rowshape(T, f32), _rowshape(T, bf16)])


def _down_loss_fwd(f, w_down, x1, target):
    T = x1.shape[0]

    def epilogue(acc, extra, const, out):
        diff = extra[0][...] + acc - extra[1][...]
        dy = diff * (1.0 / D)
        out[0][...] = dy
        out[1][...] = dy.astype(bf16)
        _add_colsum(out[2], diff * diff)

    return _matmul_fused("mm_t2_loss", f[None], w_down[None], ((0, 0),), epilogue, [(x1, _frows()), (target, _frows())],
                         [], [_rowshape(T, f32), _rowshape(T, bf16), _sumshape()], sums=True)


def _up_norm2_bwd(du3, w_up_t, x1, dy, g, token):
    T = x1.shape[0]

    def epilogue(dh, extra, const, out):
        x1v = extra[0][...]
        rstd = _rms(x1v)
        xn = x1v * rstd
        dx1 = extra[1][...] + _rms_bwd(dh * const[0][...], xn, rstd)
        out[0][...] = dx1
        out[1][...] = dx1.astype(bf16)
        _add_colsum(out[2], dh * xn)

    return _matmul_fused("mm_dh2_norm2", du3, w_up_t.reshape(2, D_FF, D), ((0, 0), (1, 1)), epilogue,
                         [(x1, _frows()), (dy, _frows())], [g],
                         [_rowshape(T, f32), _rowshape(T, bf16), _sumshape()], sums=True, passed=[token])


def _out_gate_bwd(dx1b, w_out, z8, gate_b, ya, yb, dz8):
    T = ya.shape[0]

    def epilogue(dm, extra, const, out):
        b_ref = const[0]
        g_a = _sig(extra[0][...] + b_ref[:, :D])
        g_b = _sig(extra[1][...] + b_ref[:, D:])
        out[0][...] = (dm * g_a).astype(bf16)
        out[1][...] = (dm * g_b).astype(bf16)
        dla = dm * extra[2][...] * g_a * (1.0 - g_a)
        dlb = dm * extra[3][...] * g_b * (1.0 - g_b)
        out[2][0] = dla.astype(bf16)
        out[2][1] = dlb.astype(bf16)
        _add_colsum(out[3], dla, slice(0, D))
        _add_colsum(out[3], dlb, slice(D, 2 * D))

    return _matmul_fused(
        "mm_dmixed_gate", dx1b[None], w_out[None], ((0, 0),), epilogue,
        [(z8, _fsec(Z_GA)), (z8, _fsec(Z_GB)), (ya, _frows()), (yb, _frows())], [gate_b],
        [_rowshape(T, bf16), _rowshape(T, bf16),
         (jax.ShapeDtypeStruct(dz8.shape, bf16), pl.BlockSpec((2, FTM, D), lambda i: (1, i, 0))), _sumshape(2 * D)],
        nt=True, sums=True, passed=[dz8], aliases={0: 2})


def _convnorm_bwd(c, ds, g):
    T = c.shape[0]

    def body(c_ref, ds_ref, g_ref, dc_ref, dg_ref):
        cv = c_ref[...]
        rstd = _rms(cv)
        r0 = cv * rstd
        gv = g_ref[...]
        r = r0 * gv
        sg = _sig(r)
        dr = ds_ref[...] * sg * (1.0 + r * (1.0 - sg))
        dc_ref[...] = _rms_bwd(dr * gv, r0, rstd)

        @pl.when(pl.program_id(0) == 0)
        def _():
            dg_ref[...] = jnp.zeros_like(dg_ref)

        dg_ref[...] += _colsum8(dr * r0)

    return pl.pallas_call(
        body, name="convnorm_bwd", grid=(T // TT,), in_specs=[_rows(D), _rows(D), _const((1, D))],
        out_specs=[_rows(D), _acc_spec(D)],
        out_shape=[jax.ShapeDtypeStruct((T, D), f32), jax.ShapeDtypeStruct((8, D), f32)],
        compiler_params=_cparams(("arbitrary",)))(c, ds, g)


def _qk_bwd(z8, dqn, dkn, dv, qg, kg, bd, dz8):
    T = dqn.shape[0]

    def body(q_ref, k_ref, dqn_ref, dkn_ref, dv_ref, qg_ref, kg_ref, bd_ref, dz_in, dz_ref, dqg_ref, dkg_ref):
        del dz_in
        bdv = bd_ref[...]

        @pl.when(pl.program_id(0) == 0)
        def _():
            dqg_ref[...] = jnp.zeros_like(dqg_ref)
            dkg_ref[...] = jnp.zeros_like(dkg_ref)

        def one(raw, dn_scaled, g, dg_ref, sec):
            rstd = lax.rsqrt(_head_sum(raw * raw, bdv) * (1.0 / HEAD_DIM) + EPS)
            n = raw * rstd
            dg_ref[...] += _colsum8(dn_scaled * n)
            dn = dn_scaled * g
            draw = rstd * (dn - n * (_head_sum(dn * n, bdv) * (1.0 / HEAD_DIM)))
            dz_ref[sec] = draw.astype(bf16)

        one(q_ref[...], dqn_ref[...] * (HEAD_DIM ** -0.5), qg_ref[...], dqg_ref, 0)
        one(k_ref[...], dkn_ref[...], kg_ref[...], dkg_ref, 1)
        dz_ref[2] = dv_ref[...].astype(bf16)
        dz_ref[3] = jnp.zeros((TT, D), bf16)

    return pl.pallas_call(
        body, name="qk_bwd", grid=(T // TT,),
        in_specs=[_sec(Z_Q), _sec(Z_K), _rows(D), _rows(D), _rows(D), _const((1, D)), _const((1, D)),
                  _const((128, 128)), pl.BlockSpec(memory_space=pl.ANY)],
        out_specs=[pl.BlockSpec((4, TT, D), lambda i: (1, i, 0)), _acc_spec(D), _acc_spec(D)],
        out_shape=[jax.ShapeDtypeStruct(dz8.shape, bf16), jax.ShapeDtypeStruct((8, D), f32),
                   jax.ShapeDtypeStruct((8, D), f32)],
        input_output_aliases={8: 0},
        compiler_params=_cparams(("arbitrary",)))(z8, z8, dqn, dkn, dv, qg, kg, bd, dz8)


def _in_norm1_bwd(dz8, w_in_t, x, dx1, g, token):
    T = x.shape[0]

    def epilogue(dh, extra, const, out):
        xv = extra[0][...]
        rstd = _rms(xv)
        xn = xv * rstd
        out[0][...] = extra[1][...] + _rms_bwd(dh * const[0][...], xn, rstd)
        _add_colsum(out[1], dh * xn)

    return _matmul_fused("mm_dh_norm1", dz8, w_in_t.reshape(7, D, D), tuple(zip(range(7), _W_OF_Z)), epilogue,
                         [(x, _frows()), (dx1, _frows())], [g], [_rowshape(T, f32), _sumshape()],
                         sums=True, passed=[token])


CCW = 256
CR = 64
HALO = 32


def _conv_fwd(z8, conv_w, conv_b, S):
    T = z8.shape[1]
    nb = T // S
    ncb = D // CCW

    def body(av_ref, ag_ref, w_ref, b_ref, c_ref, pad):
        pad[0:HALO, :] = jnp.zeros((HALO, CCW), f32)

        def fill(i, carry):
            r0 = pl.multiple_of(i * 256, 256)
            pad[pl.ds(HALO + r0, 256), :] = av_ref[pl.ds(r0, 256), :] * _sig(ag_ref[pl.ds(r0, 256), :])
            return carry

        lax.fori_loop(0, S // 256, fill, 0)
        bias = b_ref[...]

        def chunk(i, carry):
            r0 = pl.multiple_of(i * CR, CR)
            win = pad[pl.ds(r0, CR + HALO), :]
            acc = jnp.zeros((CR, CCW), f32) + bias
            for s in range(8):
                part = None
                for m in range((CONV_WIDTH - 1 - s) // 8 + 1):
                    j = CONV_WIDTH - 1 - 8 * m - s
                    term = win[24 - 8 * m:24 - 8 * m + CR + 8, :] * w_ref[j:j + 1, :]
                    part = term if part is None else part + term
                acc = acc + part[8 - s:8 - s + CR, :]
            c_ref[pl.ds(r0, CR), :] = acc
            return carry

        lax.fori_loop(0, S // CR, chunk, 0)

    zs = lambda s: pl.BlockSpec((None, S, CCW), lambda b, cb: (s, b, cb))
    return pl.pallas_call(
        body, name="conv_fwd", grid=(nb, ncb),
        in_specs=[zs(Z_AVAL), zs(Z_AGATE), pl.BlockSpec((CONV_WIDTH, CCW), lambda b, cb: (0, cb)),
                  pl.BlockSpec((1, CCW), lambda b, cb: (0, cb))],
        out_specs=pl.BlockSpec((S, CCW), lambda b, cb: (b, cb)),
        out_shape=jax.ShapeDtypeStruct((T, D), f32),
        scratch_shapes=[pltpu.VMEM((S + HALO, CCW), f32)],
        compiler_params=_cparams(("parallel", "parallel")))(z8, z8, conv_w, conv_b)


def _conv_bwd(dc, z8, conv_w, dz8, S):
    T = dc.shape[0]
    nb = T // S
    ncb = D // CCW

    def body(dc_ref, av_ref, ag_ref, w_ref, dz_in, dz_ref, dw_ref, apad, dpad, shbuf):
        del dz_in
        apad[0:HALO, :] = jnp.zeros((HALO, CCW), f32)
        dpad[S:S + HALO, :] = jnp.zeros((HALO, CCW), f32)
        dw_ref[...] = jnp.zeros_like(dw_ref)

        def fill(i, carry):
            r0 = pl.multiple_of(i * 256, 256)
            apad[pl.ds(HALO + r0, 256), :] = av_ref[pl.ds(r0, 256), :] * _sig(ag_ref[pl.ds(r0, 256), :])
            dpad[pl.ds(r0, 256), :] = dc_ref[pl.ds(r0, 256), :]
            return carry

        lax.fori_loop(0, S // 256, fill, 0)

        def chunk(i, carry):
            r0 = pl.multiple_of(i * CR, CR)
            dwin = dpad[pl.ds(r0, CR + HALO), :]
            da = jnp.zeros((CR, CCW), f32)
            for s in range(8):
                shbuf[...] = dwin[s:s + CR, :]
                dshift = shbuf[...]
                part = None
                for m in range((CONV_WIDTH - 1 - s) // 8 + 1):
                    j = CONV_WIDTH - 1 - 8 * m - s
                    term = dwin[8 * m:8 * m + CR + 8, :] * w_ref[j:j + 1, :]
                    part = term if part is None else part + term
                    a_lag = apad[pl.ds(r0 + HALO - 8 * m, CR), :]
                    dw_ref[8 * j:8 * j + 8, :] += _colsum8(dshift * a_lag)
                da = da + part[s:s + CR, :]
            dw_ref[8 * CONV_WIDTH:8 * CONV_WIDTH + 8, :] += _colsum8(dwin[0:CR, :])
            av = av_ref[pl.ds(r0, CR), :]
            sg = _sig(ag_ref[pl.ds(r0, CR), :])
            dz_ref[0, pl.ds(r0, CR), :] = (da * sg).astype(bf16)
            dz_ref[1, pl.ds(r0, CR), :] = (da * av * sg * (1.0 - sg)).astype(bf16)
            return carry

        lax.fori_loop(0, S // CR, chunk, 0)

    zs = lambda s: pl.BlockSpec((None, S, CCW), lambda b, cb: (s, b, cb))
    return pl.pallas_call(
        body, name="conv_bwd", grid=(nb, ncb),
        in_specs=[pl.BlockSpec((S, CCW), lambda b, cb: (b, cb)), zs(Z_AVAL), zs(Z_AGATE),
                  pl.BlockSpec((CONV_WIDTH, CCW), lambda b, cb: (0, cb)), pl.BlockSpec(memory_space=pl.ANY)],
        out_specs=[pl.BlockSpec((2, S, CCW), lambda b, cb: (0, b, cb)),
                   pl.BlockSpec((None, 256, CCW), lambda b, cb: (b, 0, cb))],
        out_shape=[jax.ShapeDtypeStruct(dz8.shape, bf16), jax.ShapeDtypeStruct((nb, 256, D), f32)],
        input_output_aliases={4: 0},
        scratch_shapes=[pltpu.VMEM((S + HALO, CCW), f32), pltpu.VMEM((S + HALO, CCW), f32),
                        pltpu.VMEM((CR, CCW), f32)],
        compiler_params=_cparams(("parallel", "parallel")))(dc, z8, z8, conv_w, dz8)


FR = 128
NFB = D_FF // CCW


def _ffn_window(ref, i, r0):
    return ref[pl.ds(r0 - 8, FR + 8), :]


def _ffn_u(win, w_ref, b_ref):
    return (win[6:6 + FR, :] * w_ref[0:1, :] + win[7:7 + FR, :] * w_ref[1:2, :]
            + win[8:8 + FR, :] * w_ref[2:3, :] + b_ref[...])


def _ffn_fwd(u3, ffn_w, ffn_b, S):
    T = u3.shape[1]
    nb = T // S

    def body(uv_ref, ug_ref, wv_ref, wg_ref, bv_ref, bg_ref, f_ref):
        def chunk(first, i):
            r0 = 0 if first else pl.multiple_of(i * FR, FR)
            if first:
                z = jnp.zeros((8, CCW), f32)
                wv = jnp.concatenate([z, uv_ref[0:FR, :]], axis=0)
                wg = jnp.concatenate([z, ug_ref[0:FR, :]], axis=0)
            else:
                wv = _ffn_window(uv_ref, i, r0)
                wg = _ffn_window(ug_ref, i, r0)
            u_val = _ffn_u(wv, wv_ref, bv_ref)
            u_gate = _ffn_u(wg, wg_ref, bg_ref)
            f_ref[pl.ds(r0, FR), :] = (u_gate * _sig(u_gate) * u_val).astype(bf16)

        chunk(True, 0)

        def loop(i, carry):
            chunk(False, i)
            return carry

        lax.fori_loop(1, S // FR, loop, 0)

    us = lambda h: pl.BlockSpec((None, S, CCW), lambda b, cb: (h, b, cb))
    ws = lambda h: pl.BlockSpec((3, CCW), lambda b, cb: (0, h * NFB + cb))
    bs = lambda h: pl.BlockSpec((1, CCW), lambda b, cb: (0, h * NFB + cb))
    return pl.pallas_call(
        body, name="ffn_fwd", grid=(nb, NFB),
        in_specs=[us(0), us(1), ws(0), ws(1), bs(0), bs(1)],
        out_specs=pl.BlockSpec((S, CCW), lambda b, cb: (b, cb)),
        out_shape=jax.ShapeDtypeStruct((T, D_FF), bf16),
        compiler_params=_cparams(("parallel", "parallel")))(u3, u3, ffn_w, ffn_w, ffn_b, ffn_b)


def _ffn_bwd(u3, df, ffn_w, ffn_b, S):
    T = u3.shape[1]
    nb = T // S

    def body(uv_ref, ug_ref, df_ref, wv_ref, wg_ref, bv_ref, bg_ref, du_ref, dw_ref, dvpad, dgpad, shbuf):
        dvpad[S:S + 8, :] = jnp.zeros((8, CCW), f32)
        dgpad[S:S + 8, :] = jnp.zeros((8, CCW), f32)
        dw_ref[...] = jnp.zeros_like(dw_ref)

        def chunk(first, i):
            r0 = 0 if first else pl.multiple_of(i * FR, FR)
            if first:
                z = jnp.zeros((8, CCW), f32)
                wv = jnp.concatenate([z, uv_ref[0:FR, :]], axis=0)
                wg = jnp.concatenate([z, ug_ref[0:FR, :]], axis=0)
            else:
                wv = _ffn_window(uv_ref, i, r0)
                wg = _ffn_window(ug_ref, i, r0)
            taps = []
            for h, win in enumerate((wv, wg)):
                shbuf[2 * h] = win[6:6 + FR, :]
                shbuf[2 * h + 1] = win[7:7 + FR, :]
                taps.append((shbuf[2 * h], shbuf[2 * h + 1], win[8:8 + FR, :]))
            conv = lambda x, w_ref, b_ref: (x[0] * w_ref[0:1, :] + x[1] * w_ref[1:2, :] + x[2] * w_ref[2:3, :]
                                            + b_ref[...])
            u_val = conv(taps[0], wv_ref, bv_ref)
            u_gate = conv(taps[1], wg_ref, bg_ref)
            dfc = df_ref[pl.ds(r0, FR), :]
            sg = _sig(u_gate)
            d_val = dfc * u_gate * sg
            d_gate = dfc * u_val * sg * (1.0 + u_gate * (1.0 - sg))
            dvpad[pl.ds(r0, FR), :] = d_val
            dgpad[pl.ds(r0, FR), :] = d_gate
            for h, dd in enumerate((d_val, d_gate)):
                for j in range(3):
                    dw_ref[h, 8 * j:8 * j + 8, :] += _colsum8(dd * taps[h][j])
                dw_ref[h, 24:32, :] += _colsum8(dd)

        chunk(True, 0)

        def loop(i, carry):
            chunk(False, i)
            return carry

        lax.fori_loop(1, S // FR, loop, 0)

        def back(i, carry):
            r0 = pl.multiple_of(i * FR, FR)
            for h, (dpad, w_ref) in enumerate(((dvpad, wv_ref), (dgpad, wg_ref))):
                win = dpad[pl.ds(r0, FR + 8), :]
                du = (win[0:FR, :] * w_ref[2:3, :] + win[1:1 + FR, :] * w_ref[1:2, :]
                      + win[2:2 + FR, :] * w_ref[0:1, :])
                du_ref[h, pl.ds(r0, FR), :] = du.astype(bf16)
            return carry

        lax.fori_loop(0, S // FR, back, 0)

    us = lambda h: pl.BlockSpec((None, S, CCW), lambda b, cb: (h, b, cb))
    ws = lambda h: pl.BlockSpec((3, CCW), lambda b, cb: (0, h * NFB + cb))
    bs = lambda h: pl.BlockSpec((1, CCW), lambda b, cb: (0, h * NFB + cb))
    return pl.pallas_call(
        body, name="ffn_bwd", grid=(nb, NFB),
        in_specs=[us(0), us(1), pl.BlockSpec((S, CCW), lambda b, cb: (b, cb)), ws(0), ws(1), bs(0), bs(1)],
        out_specs=[pl.BlockSpec((2, S, CCW), lambda b, cb: (0, b, cb)),
                   pl.BlockSpec((None, 2, 32, CCW), lambda b, cb: (b, 0, 0, cb))],
        out_shape=[jax.ShapeDtypeStruct((2, T, D_FF), bf16), jax.ShapeDtypeStruct((nb, 2, 32, D_FF), f32)],
        scratch_shapes=[pltpu.VMEM((S + 8, CCW), f32), pltpu.VMEM((S + 8, CCW), f32),
                        pltpu.VMEM((4, FR, CCW), f32)],
        compiler_params=_cparams(("parallel", "parallel")))(u3, u3, df, ffn_w, ffn_w, ffn_b, ffn_b)


AB = ATTN_BLOCK


def _attn_bias_np():
    slopes = (np.float32(2.0) ** (np.float32(-8.0) * np.arange(1, N_HEADS + 1, dtype=np.float32)
                                  / np.float32(N_HEADS))).astype(np.float32)
    steps = (np.arange(AB)[:, None] + AB) - np.arange(2 * AB)[None, :]
    own = (np.arange(2 * AB) >= AB)[None, :]
    out = []
    for window, dil in GROUPS:
        valid = (steps >= 0) & (steps <= window // dil)
        dist = slopes[:, None, None] * (steps * dil).astype(np.float32)[None]
        kinds = [np.where(v[None], dist, np.float32(MASK_BIAS)) for v in (valid, valid & own)]
        out.append(np.stack(kinds, axis=1))
    return np.stack(out).astype(np.float32)


def _attn_bias():
    return jnp.asarray(_attn_bias_np())


def _head_masks():
    lane = lax.broadcasted_iota(jnp.int32, (1, 128), 1)
    return (lane < HEAD_DIM, lane >= HEAD_DIM)


def _perm_chunks(S, d):
    L = S // d
    ch = min(L, 256)
    out = []
    for r in range(d):
        for c in range(L // ch):
            start = r + d * ch * c
            out.append((pl.ds(start, ch, stride=d) if d > 1 else pl.ds(start, ch), r * L + c * ch, ch))
    return out


def _stack_heads(x, masks):
    return jnp.concatenate([jnp.where(masks[0], x, 0), jnp.where(masks[1], x, 0)], axis=0)


def _block_row(j):
    return j * AB if isinstance(j, int) else pl.multiple_of(j * AB, AB)


def _three_stages(n, stage_a, stage_b, stage_c, unroll):
    stage_a(0)
    stage_a(1)
    stage_b(0)

    def body(j, carry):
        stage_c(j - 1)
        stage_b(j)
        stage_a(j + 1)
        return carry

    lax.fori_loop(1, n - 1, body, 0, unroll=unroll)
    stage_c(n - 2)
    stage_b(n - 1)
    stage_c(n - 1)


_NT = (((1,), (1,)), ((), ()))
_TN = (((0,), (0,)), ((), ()))
SCH = 32


def _attn_fwd(qn, kn, z8, bias, S):
    T = qn.shape[0]
    nb = T // S
    nblk = S // AB

    def body(q_ref, k_ref, v_ref, bias_ref, o_ref, ob_ref, lse_ref, qs, ks, vs, s2, p2, ogp, lgp, *group_scratch):
        og, lg = group_scratch[:3], group_scratch[3:]
        masks = _head_masks()
        ks[0:AB, :] = jnp.zeros((AB, 128), bf16)
        vs[0:AB, :] = jnp.zeros((AB, 128), bf16)

        for g, (_, d) in enumerate(GROUPS):
            nsub = S // (d * AB)
            chunks = _perm_chunks(S, d)
            for src, dst, ch in chunks:
                qs[dst:dst + ch, :] = q_ref[src, :].astype(bf16)
                ks[AB + dst:AB + dst + ch, :] = k_ref[src, :].astype(bf16)
                vs[AB + dst:AB + dst + ch, :] = v_ref[src, :].astype(bf16)
            od, ld = (og[g], lg[g]) if d == 1 else (ogp, lgp)

            def scores(j):
                r0 = _block_row(j)
                q2 = _stack_heads(qs[pl.ds(r0, AB), :], masks)
                s2[j] = lax.dot_general(q2, ks[pl.ds(r0, 2 * AB), :], _NT, preferred_element_type=f32)

            def softmax(j, g=g, nsub=nsub, ld=ld):
                r0 = _block_row(j)
                kind = int(j % nsub == 0) if isinstance(j, int) else (j % nsub == 0).astype(jnp.int32)
                for cc in range(AB // SCH):
                    lses = []
                    for hh in range(2):
                        rows = pl.ds(hh * AB + cc * SCH, SCH)
                        sb = s2[j, rows, :] - bias_ref[g, hh, kind, cc * SCH:(cc + 1) * SCH, :]
                        m = jnp.max(sb, axis=-1, keepdims=True)
                        p = jnp.exp(sb - m)
                        den = jnp.sum(p, axis=-1, keepdims=True)
                        p2[j, rows, :] = (p * (1.0 / den)).astype(bf16)
                        lses.append(m + jnp.log(den))
                    ld[pl.ds(r0 + cc * SCH, SCH), :] = jnp.where(masks[0], lses[0], lses[1])

            def values(j, od=od):
                r0 = _block_row(j)
                pv2 = jnp.dot(p2[j], vs[pl.ds(r0, 2 * AB), :], preferred_element_type=f32)
                od[pl.ds(r0, AB), :] = jnp.where(masks[0], pv2[:AB], pv2[AB:])

            _three_stages(nblk, scores, softmax, values, 7)

            if d > 1:
                for src, dst, ch in chunks:
                    og[g][src, :] = ogp[dst:dst + ch, :]
                    lg[g][src, :] = lgp[dst:dst + ch, :]

        def combine(i, carry):
            rr = pl.ds(pl.multiple_of(i * 256, 256), 256)
            l0, l1, l2 = lg[0][rr, :], lg[1][rr, :], lg[2][rr, :]
            mx = jnp.maximum(jnp.maximum(l0, l1), l2)
            e0, e1, e2 = jnp.exp(l0 - mx), jnp.exp(l1 - mx), jnp.exp(l2 - mx)
            den = e0 + e1 + e2
            o = (e0 * og[0][rr, :] + e1 * og[1][rr, :] + e2 * og[2][rr, :]) / den
            o_ref[rr, :] = o
            ob_ref[rr, :] = o.astype(bf16)
            lse_ref[rr, :] = mx + jnp.log(den)
            return carry

        lax.fori_loop(0, S // 256, combine, 0)

    blk = pl.BlockSpec((S, 128), lambda b, hp: (b, hp))
    return pl.pallas_call(
        body, name="attn_fwd", grid=(nb, N_HEADS // 2),
        in_specs=[blk, blk, pl.BlockSpec((None, S, 128), lambda b, hp: (Z_V, b, hp)),
                  pl.BlockSpec((3, 2, 2, AB, 2 * AB), lambda b, hp: (0, hp, 0, 0, 0))],
        out_specs=[blk, blk, blk],
        out_shape=[jax.ShapeDtypeStruct((T, D), f32), jax.ShapeDtypeStruct((T, D), bf16),
                   jax.ShapeDtypeStruct((T, D), f32)],
        scratch_shapes=[pltpu.VMEM((S, 128), bf16), pltpu.VMEM((S + AB, 128), bf16), pltpu.VMEM((S + AB, 128), bf16),
                        pltpu.VMEM((nblk, 2 * AB, 2 * AB), f32), pltpu.VMEM((nblk, 2 * AB, 2 * AB), bf16),
                        pltpu.VMEM((S, 128), f32), pltpu.VMEM((S, 128), f32)] + [pltpu.VMEM((S, 128), f32)] * 6,
        compiler_params=_cparams(("parallel", "parallel")))(qn, kn, z8, bias)


def _attn_bwd(qn, kn, z8, do, o, lse, bias, bd, S, after):
    T = qn.shape[0]
    nb = T // S

    nblk = S // AB

    def body(q_ref, k_ref, v_ref, do_ref, o_ref, lse_ref, bias_ref, bd_ref, after_ref, dq_ref, dk_ref, dv_ref,
             delta, qs, ks, vs, dos, lsp, dlp, s2, dp2, p2, ds2, dqp, dkp, dvp):
        del after_ref
        masks = _head_masks()
        bdv = bd_ref[...]
        dq_ref[...] = jnp.zeros_like(dq_ref)
        dk_ref[...] = jnp.zeros_like(dk_ref)
        dv_ref[...] = jnp.zeros_like(dv_ref)
        ks[0:AB, :] = jnp.zeros((AB, 128), bf16)
        vs[0:AB, :] = jnp.zeros((AB, 128), bf16)

        def prep(i, carry):
            rr = pl.ds(pl.multiple_of(i * 256, 256), 256)
            delta[rr, :] = _head_sum(do_ref[rr, :] * o_ref[rr, :], bdv)
            return carry

        lax.fori_loop(0, S // 256, prep, 0)

        for g, (_, d) in enumerate(GROUPS):
            nsub = S // (d * AB)
            chunks = _perm_chunks(S, d)
            for src, dst, ch in chunks:
                qs[dst:dst + ch, :] = q_ref[src, :].astype(bf16)
                ks[AB + dst:AB + dst + ch, :] = k_ref[src, :].astype(bf16)
                vs[AB + dst:AB + dst + ch, :] = v_ref[src, :].astype(bf16)
                dos[dst:dst + ch, :] = do_ref[src, :].astype(bf16)
                lsp[dst:dst + ch, :] = lse_ref[src, :]
                dlp[dst:dst + ch, :] = delta[src, :]
            dkp[...] = jnp.zeros_like(dkp)
            dvp[...] = jnp.zeros_like(dvp)

            def scores(j):
                r0 = _block_row(j)
                q2 = _stack_heads(qs[pl.ds(r0, AB), :], masks)
                do2 = _stack_heads(dos[pl.ds(r0, AB), :], masks)
                s2[j] = lax.dot_general(q2, ks[pl.ds(r0, 2 * AB), :], _NT, preferred_element_type=f32)
                dp2[j] = lax.dot_general(do2, vs[pl.ds(r0, 2 * AB), :], _NT, preferred_element_type=f32)

            def probs(j, g=g, nsub=nsub):
                r0 = _block_row(j)
                kind = int(j % nsub == 0) if isinstance(j, int) else (j % nsub == 0).astype(jnp.int32)
                for cc in range(AB // SCH):
                    lse_c = lsp[pl.ds(r0 + cc * SCH, SCH), :]
                    del_c = dlp[pl.ds(r0 + cc * SCH, SCH), :]
                    for hh in range(2):
                        c0 = hh * HEAD_DIM
                        rows = pl.ds(hh * AB + cc * SCH, SCH)
                        sb = s2[j, rows, :] - bias_ref[g, hh, kind, cc * SCH:(cc + 1) * SCH, :]
                        p = jnp.exp(sb - lse_c[:, c0:c0 + 1])
                        p2[j, rows, :] = p.astype(bf16)
                        ds2[j, rows, :] = (p * (dp2[j, rows, :] - del_c[:, c0:c0 + 1])).astype(bf16)

            def grads(j):
                r0 = _block_row(j)
                q2 = _stack_heads(qs[pl.ds(r0, AB), :], masks)
                do2 = _stack_heads(dos[pl.ds(r0, AB), :], masks)
                dsb = ds2[j]
                t = jnp.dot(dsb, ks[pl.ds(r0, 2 * AB), :], preferred_element_type=f32)
                dqp[pl.ds(r0, AB), :] = jnp.where(masks[0], t[:AB], t[AB:])
                dkp[pl.ds(r0, 2 * AB), :] += lax.dot_general(dsb, q2, _TN, preferred_element_type=f32)
                dvp[pl.ds(r0, 2 * AB), :] += lax.dot_general(p2[j], do2, _TN, preferred_element_type=f32)

            _three_stages(nblk, scores, probs, grads, 7)

            for src, dst, ch in chunks:
                dq_ref[src, :] += dqp[dst:dst + ch, :]
                dk_ref[src, :] += dkp[AB + dst:AB + dst + ch, :]
                dv_ref[src, :] += dvp[AB + dst:AB + dst + ch, :]

    blk = pl.BlockSpec((S, 128), lambda b, hp: (b, hp))
    row = lambda dt, pad=0: pltpu.VMEM((S + pad, 128), dt)
    blocks = lambda dt: pltpu.VMEM((nblk, 2 * AB, 2 * AB), dt)
    return pl.pallas_call(
        body, name="attn_bwd", grid=(nb, N_HEADS // 2),
        in_specs=[blk, blk, pl.BlockSpec((None, S, 128), lambda b, hp: (Z_V, b, hp)), blk, blk, blk,
                  pl.BlockSpec((3, 2, 2, AB, 2 * AB), lambda b, hp: (0, hp, 0, 0, 0)),
                  pl.BlockSpec((128, 128), lambda b, hp: (0, 0)), pl.BlockSpec(memory_space=pl.ANY)],
        out_specs=[blk, blk, blk],
        out_shape=[jax.ShapeDtypeStruct((T, D), f32)] * 3,
        scratch_shapes=[row(f32), row(bf16), row(bf16, AB), row(bf16, AB), row(bf16), row(f32), row(f32),
                        blocks(f32), blocks(f32), blocks(bf16), blocks(bf16), row(f32), row(f32, AB), row(f32, AB)],
        compiler_params=_cparams(("parallel", "parallel")))(qn, kn, z8, do, o, lse, bias, bd, after)


def _any_spec():
    return pl.BlockSpec(memory_space=pl.ANY)


def _allgather_rows(shards, n_full):
    n = len(shards)

    def body(*refs):
        ins, outs = refs[:n], refs[n:2 * n]
        send_sems, recv_sems, local_sems = refs[2 * n:]
        x, y, c, me = _my_pos()
        sibling = (x, y, 1 - c)
        chips = [(1 - x, y), (x, 1 - y), (1 - x, 1 - y)]

        def idx(px, py, pc):
            return 4 * px + 2 * py + pc

        def copy(a, k, blk, to, src=None):
            return pltpu.make_async_remote_copy(
                src_ref=outs[a].at[blk] if src is None else src, dst_ref=outs[a].at[blk],
                send_sem=send_sems.at[a, k], recv_sem=recv_sems.at[a, k], device_id=to, device_id_type=MESH)

        mine = [pltpu.make_async_copy(ins[a], outs[a].at[me], local_sems.at[a]) for a in range(n)]
        for cp in mine:
            cp.start()
        first = []
        for a in range(n_full):
            first.append(copy(a, 0, me, sibling, src=ins[a]))
            first += [copy(a, 1 + j, me, (*chip, c), src=ins[a]) for j, chip in enumerate(chips)]
        for cp in first:
            cp.start()
        passed = []
        for a in range(n_full):
            for j, chip in enumerate(chips):
                blk = idx(*chip, c)
                copy(a, 1 + j, blk, (x, y, c)).wait_recv()
                cp = copy(a, 4 + j, blk, sibling)
                cp.start()
                passed.append(cp)
        for a in range(n_full):
            copy(a, 0, idx(x, y, 1 - c), (x, y, c)).wait_recv()
            for j, chip in enumerate(chips):
                copy(a, 4 + j, idx(*chip, 1 - c), (x, y, c)).wait_recv()
        for cp in first + passed:
            cp.wait_send()
        for cp in mine:
            cp.wait()

    return pl.pallas_call(
        body, name="allgather_weights",
        in_specs=[_any_spec()] * n, out_specs=[_any_spec()] * n,
        out_shape=[jax.ShapeDtypeStruct((N_DEV,) + s.shape, s.dtype) for s in shards],
        scratch_shapes=[pltpu.SemaphoreType.DMA((n_full, 7)), pltpu.SemaphoreType.DMA((n_full, 7)),
                        pltpu.SemaphoreType.DMA((n,))],
    )(*shards)


def _peer(x, y, c, k):
    tx = 1 - x if (k >> 2) & 1 else x
    ty = 1 - y if (k >> 1) & 1 else y
    tc = 1 - c if k & 1 else c
    return (tx, ty, tc), 4 * tx + 2 * ty + tc


_PEER_ORDER = (2, 4, 6, 3, 5, 7, 1)


_HBM = pl.BlockSpec(memory_space=pltpu.HBM)
_SEM = pl.BlockSpec(memory_space=pltpu.SEMAPHORE)
_EFFECT = pltpu.SideEffectType.DATAFLOW_SIDE_EFFECTING


def _exchange_copies(srcs, lands, send_sems, recv_sems, gather, half):
    x, y, c, me = _my_pos()
    pick = lambda px, py: None if half is None else ((px == py) if half == 0 else (px != py))
    copies = []
    for k in _PEER_ORDER:
        tgt, tidx = _peer(x, y, c, k)
        for a in range(len(srcs)):
            copies.append((pltpu.make_async_remote_copy(
                src_ref=srcs[a] if gather else srcs[a].at[tidx], dst_ref=lands[a].at[me],
                send_sem=send_sems.at[7 * a + k - 1], recv_sem=recv_sems.at[7 * a + k - 1],
                device_id=tgt, device_id_type=MESH), pick(tgt[0], tgt[1])))
    return copies, pick(x, y)


def _when(cond, fn):
    if cond is None:
        fn()
    else:
        pl.when(cond)(fn)


def _exchange_start(name, srcs, lands=None, after=None, gather=None, half=None):
    n = len(srcs)
    gather = (lands is not None) if gather is None else gather
    if lands is None:
        lands = [lax.empty(g.shape, g.dtype) for g in srcs]
    extra = [] if after is None else [after]

    def body(*refs):
        src_refs, land_refs = refs[:n], refs[n:2 * n]
        send_sems, recv_sems = refs[2 * n + len(extra)], refs[2 * n + len(extra) + 1]
        token = refs[-1]
        for cp, sends in _exchange_copies(src_refs, land_refs, send_sems, recv_sems, gather, half)[0]:
            _when(sends, cp.start)
        token[...] = jnp.zeros_like(token)

    hbm = lambda a: pltpu.with_memory_space_constraint(a, pltpu.HBM)
    outs = pl.pallas_call(
        body, name=name,
        out_shape=(pltpu.SemaphoreType.DMA((7 * n,)), pltpu.SemaphoreType.DMA((7 * n,)),
                   *[pltpu.HBM(g.shape, g.dtype) for g in list(srcs) + list(lands)],
                   jax.ShapeDtypeStruct((8, 128), f32)),
        in_specs=[_HBM] * (2 * n) + [pl.BlockSpec(memory_space=pl.ANY)] * len(extra),
        out_specs=(_SEM, _SEM, *([_HBM] * (2 * n)), pl.BlockSpec(memory_space=pltpu.VMEM)),
        input_output_aliases={i: 2 + i for i in range(2 * n)},
        compiler_params=pltpu.CompilerParams(has_side_effects=_EFFECT),
    )(*[hbm(g) for g in srcs], *[hbm(g) for g in lands], *extra)
    return outs[0], outs[1], list(outs[2:2 + n]), list(outs[2 + n:2 + 2 * n]), outs[-1], gather, half


def _exchange_wait(name, started, after):
    send_sems, recv_sems, srcs, lands, _, gather, half = started
    n = len(srcs)
    after = list(after) if isinstance(after, (list, tuple)) else [after]

    def body(*refs):
        src_refs, land_refs = refs[:n], refs[n:2 * n]
        s_sems, r_sems = refs[2 * n], refs[2 * n + 1]
        copies, receives = _exchange_copies(src_refs, land_refs, s_sems, r_sems, gather, half)
        for cp, sends in copies:
            _when(sends, cp.wait_send)
            _when(receives, cp.wait_recv)

    outs = pl.pallas_call(
        body, name=name,
        out_shape=tuple(pltpu.HBM(a.shape, a.dtype) for a in list(srcs) + list(lands)),
        in_specs=[_HBM] * (2 * n) + [_SEM, _SEM] + [pl.BlockSpec(memory_space=pl.ANY)] * len(after),
        out_specs=tuple([_HBM] * (2 * n)),
        input_output_aliases={i: i for i in range(2 * n)},
        compiler_params=pltpu.CompilerParams(has_side_effects=_EFFECT),
    )(*srcs, *lands, send_sems, recv_sems, *after)
    return list(outs[:n]), list(outs[n:])


SMALL_ROWS = 128


def _small_start(name, sg, after=None):
    return _exchange_start(name, [sg], [lax.empty((N_DEV,) + sg.shape, f32)], after=after)


def _small_sum(name, me, started, after):
    (own,), (slots,) = _exchange_wait(name + "_wait", started, after)

    def body(me_ref, s_ref, own_ref, out_ref):
        acc = None
        for p in range(N_DEV):
            term = lax.cond(me_ref[0] == p, lambda: own_ref[...], lambda p=p: s_ref[p])
            acc = term if acc is None else acc + term
        out_ref[...] = acc

    return pl.pallas_call(
        body, name=name + "_sum",
        in_specs=[pl.BlockSpec(memory_space=pltpu.SMEM), pl.BlockSpec(memory_space=pltpu.VMEM),
                  pl.BlockSpec(memory_space=pltpu.VMEM)],
        out_specs=pl.BlockSpec(memory_space=pltpu.VMEM),
        out_shape=jax.ShapeDtypeStruct(own.shape, f32))(me, slots, own)


def _adam_math(g, w, m, v):
    m = ADAM_B1 * m + (1.0 - ADAM_B1) * g
    v = ADAM_B2 * v + (1.0 - ADAM_B2) * (g * g)
    m_hat = m / (1.0 - ADAM_B1 ** ADAM_STEP)
    v_hat = v / (1.0 - ADAM_B2 ** ADAM_STEP)
    delta = -ADAM_LR * (m_hat / (jnp.sqrt(v_hat) + ADAM_EPS) + ADAM_WD * w)
    return delta, m, v


def _adam_slots(name, me, slots, own, w, m, v, tr, transposed=False):
    rows = slots.shape[1]

    def body(me_ref, s_ref, own_ref, w_ref, m_ref, v_ref, g_ref, d_ref, nm_ref, nv_ref):
        mine = own_ref[...]
        g = None
        for p in range(N_DEV):
            term = lax.cond(me_ref[0] == p, lambda: mine, lambda p=p: s_ref[p]).astype(f32)
            g = term if g is None else g + term
        if transposed:
            g = g.T
        delta, nm, nv = _adam_math(g, w_ref[...], m_ref[...], v_ref[...])
        g_ref[...] = g
        d_ref[...] = delta
        nm_ref[...] = nm
        nv_ref[...] = nv

    mode = dict(pipeline_mode=pl.Buffered(1)) if rows == tr else {}
    if transposed:
        rs = pl.BlockSpec((D, tr), lambda i, me_ref: (0, i))
        rs_in = pl.BlockSpec((D, tr), lambda i, me_ref: (0, i), **mode)
    else:
        rs = pl.BlockSpec((tr, D), lambda i, me_ref: (i, 0))
        rs_in = pl.BlockSpec((tr, D), lambda i, me_ref: (i, 0), **mode)
    return pl.pallas_call(
        body, name=name,
        grid_spec=pltpu.PrefetchScalarGridSpec(
            num_scalar_prefetch=1, grid=(rows // tr,),
            in_specs=[pl.BlockSpec((N_DEV, tr, D), lambda i, me_ref: (0, i, 0), **mode),
                      pl.BlockSpec((None, tr, D), lambda i, me_ref: (me_ref[0], i, 0), **mode), rs_in, rs_in, rs_in],
            out_specs=[rs] * 4),
        out_shape=[jax.ShapeDtypeStruct(w.shape, f32)] * 4,
        compiler_params=_cparams(("parallel",)))(me, slots, own, w, m, v)


def _adam_small(g, w, m, v):
    def body(g_ref, w_ref, m_ref, v_ref, d_ref, nm_ref, nv_ref):
        delta, nm, nv = _adam_math(g_ref[...], w_ref[...], m_ref[...], v_ref[...])
        d_ref[...] = delta
        nm_ref[...] = nm
        nv_ref[...] = nv

    return pl.pallas_call(body, name="adam_small", out_shape=[jax.ShapeDtypeStruct(g.shape, f32)] * 3)(g, w, m, v)


FFN_PAD = 6 * D


_SMALL_PARTS = (("norm1_g", 1), ("gate_b", 2), ("conv_w", CONV_WIDTH), ("conv_b", 1), ("conv_norm_g", 1),
                ("q_norm_g", 1), ("k_norm_g", 1), ("norm2_g", 1), ("ffn_conv_w", 18), ("ffn_conv_b", 6), ("last", 1))


def _small_offsets():
    out, row = {}, 0
    for name, rows in _SMALL_PARTS:
        out[name] = row
        row += -(-rows // 8) * 8
    assert row == SMALL_ROWS
    return out


def _pack_small(norm1_g, gate_b, conv_w, conv_b, conv_norm_g, q_norm_g, k_norm_g, norm2_g, ffn_conv_w, ffn_conv_b,
                last_row=None):
    pad_h = lambda a: jnp.pad(a, ((0, 0), (0, D - HEAD_DIM)))
    pad_f = lambda a: jnp.pad(a, ((0, 0), (0, FFN_PAD - 2 * D_FF))).reshape(-1, D)
    parts = [norm1_g, gate_b.reshape(2, D), conv_w, conv_b, conv_norm_g, pad_h(q_norm_g), pad_h(k_norm_g), norm2_g,
             pad_f(ffn_conv_w), pad_f(ffn_conv_b), jnp.zeros((1, D), f32) if last_row is None else last_row]
    return jnp.concatenate([jnp.pad(p, ((0, -p.shape[0] % 8), (0, 0))) for p in parts], axis=0)


def _unpack_small(p):
    o = _small_offsets()
    rows = lambda name, n: p[o[name]:o[name] + n]
    ffn = lambda a: a.reshape(-1, FFN_PAD)[:, :2 * D_FF]
    return dict(
        norm1_g=rows("norm1_g", 1), gate_b=rows("gate_b", 2).reshape(1, 2 * D), conv_w=rows("conv_w", CONV_WIDTH),
        conv_b=rows("conv_b", 1), conv_norm_g=rows("conv_norm_g", 1), q_norm_g=rows("q_norm_g", 1)[:, :HEAD_DIM],
        k_norm_g=rows("k_norm_g", 1)[:, :HEAD_DIM], norm2_g=rows("norm2_g", 1),
        ffn_conv_w=ffn(rows("ffn_conv_w", 18)), ffn_conv_b=ffn(rows("ffn_conv_b", 6)))


_ADAM_TILE = {896: 128, 704: 704, 128: 128, 352: 176}


def kernel(x, norm1_g, w_in, gate_b, conv_w, conv_b, conv_norm_g, w_conv_out, q_norm_g, k_norm_g, w_attn_out, w_out, norm2_g, w_up, ffn_conv_w, ffn_conv_b, w_down, loss_target, m_norm1_g, m_w_in, m_gate_b, m_conv_w, m_conv_b, m_conv_norm_g, m_w_conv_out, m_q_norm_g, m_k_norm_g, m_w_attn_out, m_w_out, m_norm2_g, m_w_up, m_ffn_conv_w, m_ffn_conv_b, m_w_down, v_norm1_g, v_w_in, v_gate_b, v_conv_w, v_conv_b, v_conv_norm_g, v_w_conv_out, v_q_norm_g, v_k_norm_g, v_w_attn_out, v_w_out, v_norm2_g, v_w_up, v_ffn_conv_w, v_ffn_conv_b, v_w_down):
    BL, S, _ = x.shape
    T = BL * S
    me = 4 * lax.axis_index("x") + 2 * lax.axis_index("y") + lax.axis_index("c")
    xt = x.reshape(T, D)
    target = loss_target.reshape(T, D)

    big = dict(w_in=(w_in[0], m_w_in[0], v_w_in[0]), w_up=(w_up[0], m_w_up[0], v_w_up[0]),
               w_conv_out=(w_conv_out[0], m_w_conv_out[0], v_w_conv_out[0]),
               w_attn_out=(w_attn_out[0], m_w_attn_out[0], v_w_attn_out[0]),
               w_out=(w_out[0], m_w_out[0], v_w_out[0]), w_down=(w_down[0], m_w_down[0], v_w_down[0]))
    order = ["w_in", "w_conv_out", "w_attn_out", "w_out", "w_up", "w_down"]
    shards = [(big[n][0].T if n in ("w_in", "w_up") else big[n][0]).astype(bf16) for n in order]
    gathered = _allgather_rows(shards, 1)
    ga_proj = _exchange_start("gather_start_proj", shards[1:4], gathered[1:4], after=gathered[0])
    ga_ffn = _exchange_start("gather_start_ffn", shards[4:6], gathered[4:6], after=ga_proj[4])
    W = {"w_in": gathered[0].reshape(-1, D)}

    def place_cols(shard, full_cols):
        z = jnp.zeros((shard.shape[0], full_cols), f32)
        return lax.dynamic_update_slice(z, shard, (0, me * shard.shape[1]))

    zr = lambda a: jnp.zeros_like(a)
    conv_local = _pack_small(
        zr(norm1_g), zr(gate_b), place_cols(conv_w[0], D), zr(conv_b), zr(conv_norm_g), zr(q_norm_g), zr(k_norm_g),
        zr(norm2_g), place_cols(ffn_conv_w[0], 2 * D_FF), zr(ffn_conv_b))
    ga_conv = _small_start("gather_conv_start", conv_local, after=ga_ffn[4])

    bd = (jnp.arange(128)[:, None] // HEAD_DIM == jnp.arange(128)[None, :] // HEAD_DIM).astype(bf16)
    bias = _attn_bias()
    qg = jnp.tile(q_norm_g, (1, N_HEADS))
    kg = jnp.tile(k_norm_g, (1, N_HEADS))

    h = _norm1_fwd(xt, norm1_g)
    z8 = _matmul_call(
        "mm_z", h, W["w_in"],
        pl.BlockSpec((2048, D), lambda i, j, k: (i, 0)),
        pl.BlockSpec((1024, D), lambda i, j, k: (_wsec_of_zsec(j), 0)),
        pl.BlockSpec((None, 2048, D), lambda i, j, k: (j, i, 0)),
        jax.ShapeDtypeStruct((8, T, D), f32), (T // 2048, 7, 1), "nt", 1, 2048, 1024, after=ga_conv[4])
    conv_all = _unpack_small(_small_sum("gather_conv", me.reshape(1), ga_conv, z8))
    conv_w_full, ffn_w_full = conv_all["conv_w"], conv_all["ffn_conv_w"]
    c = _conv_fwd(z8, conv_w_full, conv_b, S)
    s = _convnorm_fwd(c, conv_norm_g)
    qn, kn = _qk_fwd(z8, qg, kg, bd)
    for n, g in zip(order[1:4], _exchange_wait("gather_wait_proj", ga_proj, qn)[1]):
        W[n] = g.reshape(-1, D)
    ya = _matmul("mm_ya", s, W["w_conv_out"], "nn", f32)
    o, ob, lse = _attn_fwd(qn, kn, z8, bias, S)
    yb = _matmul("mm_yb", ob, W["w_attn_out"], "nn", f32)
    mixed = _gate_fwd(z8, gate_b, ya, yb)
    x1, h2 = _out_norm2_fwd(mixed, W["w_out"], xt, norm2_g)
    for n, g in zip(order[4:6], _exchange_wait("gather_wait_ffn", ga_ffn, x1)[1]):
        W[n] = g.reshape(-1, D)
    TNU = D_FF // 2
    u3 = _matmul_call(
        "mm_u", h2, W["w_up"],
        pl.BlockSpec((1024, D), lambda i, j, k: (i, 0)),
        pl.BlockSpec((TNU, D), lambda i, j, k: (j, 0)),
        pl.BlockSpec((None, 1024, TNU), lambda i, j, k: (j // 2, i, j % 2)),
        jax.ShapeDtypeStruct((2, T, D_FF), f32), (T // 1024, 4, 1), "nt", 1, 1024, TNU)
    f = _ffn_fwd(u3, ffn_w_full, ffn_conv_b, S)
    dy, dyb, lacc = _down_loss_fwd(f, W["w_down"], x1, target)
    loss_local = 0.5 / D * jnp.sum(lacc)

    df = _matmul("mm_df", dyb, W["w_down"], "nt", f32, tn=TNU)
    g_w_down = _matmul("mm_dwdn", f, dyb, "tn", bf16, tm=TNU)
    du3, dffn = _ffn_bwd(u3, df, ffn_w_full, ffn_conv_b, S)
    g_w_up = _matmul_call(
        "mm_dwup", du3, h2,
        pl.BlockSpec((None, T, TNU), lambda i, j, k: (i // 2, 0, i % 2)),
        pl.BlockSpec((T, D), lambda i, j, k: (0, 0)),
        pl.BlockSpec((TNU, D), lambda i, j, k: (i, 0)),
        jax.ShapeDtypeStruct((2 * D_FF, D), bf16), (4, 1, 1), "tn", 1, TNU, D)
    blocks8 = lambda a: a.reshape(N_DEV, -1, D)
    ex_ffn = _exchange_start("scatter_start_ffn", [blocks8(g_w_up), blocks8(g_w_down)])
    dx1, dx1b, dg_norm2 = _up_norm2_bwd(du3, W["w_up"], x1, dy, norm2_g, ex_ffn[4])
    g_w_out = _matmul("mm_dwo", mixed, dx1b, "tn", bf16, tm=512)
    dz8 = lax.empty((8, T, D), bf16)
    dya, dyb2, dz8, dg_gate = _out_gate_bwd(dx1b, W["w_out"], z8, gate_b, ya, yb, dz8)
    ds = _matmul("mm_ds", dya, W["w_conv_out"], "nt", f32)
    g_w_conv_out = _matmul("mm_dwco", s, dya, "tn", bf16, tm=512)
    g_w_attn_out = _matmul("mm_dwao", ob, dyb2, "tn", bf16, tm=512)
    ex_proj = _exchange_start("scatter_start_proj", [blocks8(g_w_conv_out), blocks8(g_w_attn_out), blocks8(g_w_out)])
    do = _matmul("mm_do", dyb2, W["w_attn_out"], "nt", f32, after=ex_proj[4])
    dc, dg_convnorm = _convnorm_bwd(c, ds, conv_norm_g)
    dz8a, dconv = _conv_bwd(dc, z8, conv_w_full, dz8, S)
    dwin_specs = lambda zsec, wsec: (
        pl.BlockSpec((None, T, D), lambda i, j, k: (zsec(i), 0, 0)), pl.BlockSpec((T, D), lambda i, j, k: (0, 0)),
        pl.BlockSpec((1024, D), lambda i, j, k: (wsec(i), 0)), jax.ShapeDtypeStruct((7 * D, D), bf16))
    g_w_in = _matmul_call("mm_dwin_a", dz8a, h, *dwin_specs(lambda i: i, lambda i: jnp.where(i < 2, i, i + 3)),
                          (4, 1, 1), "tn", 1, D, D)
    ex_in_a = _exchange_start("scatter_start_in_a", [blocks8(g_w_in)], half=0)
    dqn, dkn, dv = _attn_bwd(qn, kn, z8, do, o, lse, bias, bd, S, ex_in_a[4])
    dz8b, dg_q, dg_k = _qk_bwd(z8, dqn, dkn, dv, qg, kg, bd, dz8a)
    g_w_in = _matmul_call("mm_dwin_b", dz8b, h, *dwin_specs(lambda i: i + 4, lambda i: i + 2),
                          (3, 1, 1), "tn", 1, D, D, fill=ex_in_a[2][0].reshape(7 * D, D))
    ex_in_b = _exchange_start("scatter_start_in_b", [blocks8(g_w_in)], ex_in_a[3], gather=False, half=1)
    grad_x, dg_norm1 = _in_norm1_bwd(dz8b, W["w_in"], xt, dx1, norm1_g, ex_in_b[4])

    sum8 = lambda a: a.reshape(-1, 8, a.shape[-1]).sum(axis=1)
    dconv_s = sum8(dconv.sum(axis=0))
    dffn_s = dffn.sum(axis=0).reshape(2, 4, 8, D_FF).sum(axis=2)
    dffn_w = jnp.concatenate([dffn_s[0, :3], dffn_s[1, :3]], axis=1)
    dffn_b = jnp.concatenate([dffn_s[0, 3:4], dffn_s[1, 3:4]], axis=1)
    fold = lambda a: sum8(a).reshape(N_HEADS, HEAD_DIM).sum(axis=0)[None]
    small_g_local = _pack_small(
        sum8(dg_norm1), sum8(dg_gate), dconv_s[:CONV_WIDTH], dconv_s[CONV_WIDTH:], sum8(dg_convnorm),
        fold(dg_q), fold(dg_k), sum8(dg_norm2), dffn_w, dffn_b,
        last_row=jnp.pad(loss_local.reshape(1, 1), ((0, 0), (0, D - 1))))
    sg_start = _small_start("small_grads_start", small_g_local)

    own, slots = {}, {}
    for tag, ex, names_ in (("ffn", ex_ffn, ("w_up", "w_down")),
                            ("proj", ex_proj, ("w_conv_out", "w_attn_out", "w_out"))):
        sent, landed = _exchange_wait("scatter_wait_" + tag, ex, sg_start[4])
        for n, src, land in zip(names_, sent, landed):
            own[n], slots[n] = src, land
    sent, landed = _exchange_wait("scatter_wait_in_a", ex_in_a[:2] + (ex_in_b[2], ex_in_b[3]) + ex_in_a[4:],
                                  sg_start[4])
    sent, landed = _exchange_wait("scatter_wait_in_b", ex_in_b[:2] + (sent, landed) + ex_in_b[4:], sg_start[4])
    own["w_in"], slots["w_in"] = sent[0], landed[0]

    res, adam_done = {}, []
    for n in order:
        w, m, v = big[n]
        outs = _adam_slots("adam_" + n, me.reshape(1), slots[n], own[n], w, m, v, _ADAM_TILE[slots[n].shape[1]],
                           transposed=n in ("w_in", "w_up"))
        adam_done.append(outs[0])
        res[n] = [a[None] for a in outs]
    small_g = _small_sum("small_grads", me.reshape(1), sg_start, adam_done)
    loss = small_g[_small_offsets()["last"], 0]

    col = lambda a, width: lax.dynamic_slice(a, (0, me * width), (a.shape[0], width))
    small_w_true = _pack_small(norm1_g, gate_b, conv_w_full, conv_b, conv_norm_g, q_norm_g, k_norm_g, norm2_g,
                               ffn_w_full, ffn_conv_b)
    place_m = lambda a, full: place_cols(a[0], full)
    small_m = _pack_small(m_norm1_g, m_gate_b, place_m(m_conv_w, D), m_conv_b, m_conv_norm_g, m_q_norm_g, m_k_norm_g,
                          m_norm2_g, place_m(m_ffn_conv_w, 2 * D_FF), m_ffn_conv_b)
    small_v = _pack_small(v_norm1_g, v_gate_b, place_m(v_conv_w, D), v_conv_b, v_conv_norm_g, v_q_norm_g, v_k_norm_g,
                          v_norm2_g, place_m(v_ffn_conv_w, 2 * D_FF), v_ffn_conv_b)
    sd, sm, sv = _adam_small(small_g, small_w_true, small_m, small_v)
    for i, packed in enumerate((small_g, sd, sm, sv)):
        u = _unpack_small(packed)
        u["conv_w"] = col(u["conv_w"], D // N_DEV)
        u["ffn_conv_w"] = col(u["ffn_conv_w"], 2 * D_FF // N_DEV)
        for n, a in u.items():
            res.setdefault(n, [None] * 4)[i] = a[None] if n in ("conv_w", "ffn_conv_w") else a

    names = ["norm1_g", "w_in", "gate_b", "conv_w", "conv_b", "conv_norm_g", "w_conv_out", "q_norm_g", "k_norm_g",
             "w_attn_out", "w_out", "norm2_g", "w_up", "ffn_conv_w", "ffn_conv_b", "w_down"]
    out = [loss, grad_x.reshape(BL, S, D)]
    for i in range(4):
        out += [res[n][i] for n in names]
    return tuple(out)
```

```python
import functools

import jax
import jax.numpy as jnp
import numpy as np
from jax import lax
from jax.experimental import pallas as pl
from jax.experimental.pallas import tpu as pltpu

f32 = jnp.float32
bf16 = jnp.bfloat16

D = 1024
N_HEADS = 16
HEAD_DIM = 64
CONV_WIDTH = 31
D_FF = 2816
GROUPS = ((128, 1), (512, 4), (2048, 16))
ATTN_BLOCK = 128
EPS = 1e-6
N_DEV = 8
MESH = pl.DeviceIdType.MESH

ADAM_LR = 0.001
ADAM_B1 = 0.9
ADAM_B2 = 0.999
ADAM_EPS = 1e-08
ADAM_WD = 0.01
ADAM_STEP = 10

VMEM_LIMIT = 56 * 1024 * 1024
MASK_BIAS = 1e30

Z_AVAL, Z_AGATE, Z_GA, Z_GB, Z_Q, Z_K, Z_V = 0, 1, 2, 3, 4, 5, 6


_W_OF_Z = (0, 1, 5, 6, 2, 3, 4)


def _wsec_of_zsec(j):
    return jnp.where(j < 2, j, jnp.where(j < 4, j + 3, j - 2))


def _zsec_of_wsec(w):
    return jnp.where(w < 2, w, jnp.where(w < 5, w + 2, w - 3))


def _sig(x):
    return 1.0 / (1.0 + jnp.exp(-x))


def _colsum8(x):
    return x.reshape(-1, 8, x.shape[-1]).sum(axis=0)


def _cparams(sem):
    return pltpu.CompilerParams(dimension_semantics=sem, vmem_limit_bytes=VMEM_LIMIT)


def _my_pos():
    x, y, c = lax.axis_index("x"), lax.axis_index("y"), lax.axis_index("c")
    return x, y, c, 4 * x + 2 * y + c


_DIMS = {"nn": ((1,), (0,)), "nt": ((1,), (1,)), "tn": ((0,), (0,))}


def _matmul_call(name, a, b, a_spec, b_spec, o_spec, out_shape, grid, mode, nk, tm, tn, after=None, fill=None):
    dims = (_DIMS[mode], ((), ()))
    extra = ([] if after is None else [after]) + ([] if fill is None else [fill])

    def body(a_ref, b_ref, *rest):
        o_ref, scratch = rest[len(extra)], rest[len(extra) + 1:]
        part = lax.dot_general(a_ref[...], b_ref[...], dims, preferred_element_type=f32)
        if nk == 1:
            o_ref[...] = part.astype(o_ref.dtype)
        else:
            acc = scratch[0]
            k = pl.program_id(2)

            @pl.when(k == 0)
            def _():
                acc[...] = part

            @pl.when(k > 0)
            def _():
                acc[...] += part

            @pl.when(k == nk - 1)
            def _():
                o_ref[...] = acc[...].astype(o_ref.dtype)

    scratch = [] if nk == 1 else [pltpu.VMEM((tm, tn), f32)]
    return pl.pallas_call(
        body, name=name, grid=grid, in_specs=[a_spec, b_spec] + [pl.BlockSpec(memory_space=pl.ANY)] * len(extra),
        out_specs=o_spec, out_shape=out_shape, input_output_aliases={} if fill is None else {1 + len(extra): 0},
        scratch_shapes=scratch, compiler_params=_cparams(("parallel", "parallel", "arbitrary")),
    )(a, b, *extra)


def _matmul(name, a, b, mode, out_dtype, tm=1024, tn=1024, tk=None, after=None):
    if mode == "nn":
        (M, K), (_, N) = a.shape, b.shape
    elif mode == "nt":
        (M, K), (N, _) = a.shape, b.shape
    else:
        (K, M), (_, N) = a.shape, b.shape
    tm, tn = min(tm, M), min(tn, N)
    tk = K if tk is None else tk
    nk = K // tk
    assert M % tm == 0 and N % tn == 0 and K % tk == 0
    if mode == "tn":
        a_spec = pl.BlockSpec((tk, tm), lambda i, j, k: (k, i))
    else:
        a_spec = pl.BlockSpec((tm, tk), lambda i, j, k: (i, k))
    if mode == "nt":
        b_spec = pl.BlockSpec((tn, tk), lambda i, j, k: (j, k))
    else:
        b_spec = pl.BlockSpec((tk, tn), lambda i, j, k: (k, j))
    o_spec = pl.BlockSpec((tm, tn), lambda i, j, k: (i, j))
    return _matmul_call(name, a, b, a_spec, b_spec, o_spec, jax.ShapeDtypeStruct((M, N), out_dtype),
                        (M // tm, N // tn, nk), mode, nk, tm, tn, after=after)


FTM = 512


def _matmul_fused(name, a, b, pairs, epilogue, extras, consts, outs, nt=False, sums=False, passed=(), aliases=None):
    sa, M, kk = a.shape
    na = max(i for i, _ in pairs) + 1
    ne, nc, npass = len(extras), len(consts), len(passed)
    dims = (_DIMS["nt" if nt else "nn"], ((), ()))

    def body(a_ref, b_ref, *rest):
        acc = None
        for i, j in pairs:
            part = lax.dot_general(a_ref[i], b_ref[j], dims, preferred_element_type=f32)
            acc = part if acc is None else acc + part
        epilogue(acc, rest[:ne], rest[ne:ne + nc], rest[ne + nc + npass:])

    whole = lambda arr: pl.BlockSpec(arr.shape, lambda i, nd=arr.ndim: (0,) * nd, pipeline_mode=pl.Buffered(1))
    io_alias = {2 + ne + nc + k: v for k, v in (aliases or {}).items()}
    return pl.pallas_call(
        body, name=name, grid=(M // FTM,),
        in_specs=[pl.BlockSpec((na, FTM, kk), lambda i: (0, i, 0)), whole(b)] + [s for _, s in extras]
        + [whole(c) for c in consts] + [pl.BlockSpec(memory_space=pl.ANY)] * npass,
        out_specs=[s for _, s in outs], out_shape=[s for s, _ in outs], input_output_aliases=io_alias,
        compiler_params=_cparams(("arbitrary" if sums else "parallel",)),
    )(a, b, *[x for x, _ in extras], *consts, *passed)


def _frows(c=D):
    return pl.BlockSpec((FTM, c), lambda i: (i, 0))


def _fsec(s):
    return pl.BlockSpec((None, FTM, D), lambda i: (s, i, 0))


def _rowshape(T, dtype, c=D):
    return (jax.ShapeDtypeStruct((T, c), dtype), _frows(c))


def _sumshape(c=D):
    return (jax.ShapeDtypeStruct((8, c), f32), pl.BlockSpec((8, c), lambda i: (0, 0)))


def _add_colsum(ref, x, cols=None):
    @pl.when(pl.program_id(0) == 0)
    def _():
        if cols is None:
            ref[...] = jnp.zeros_like(ref)
        else:
            ref[:, cols] = jnp.zeros((8, x.shape[-1]), f32)

    if cols is None:
        ref[...] += _colsum8(x)
    else:
        ref[:, cols] += _colsum8(x)


TT = 512


def _rows(c, cb=0, tt=TT):
    return pl.BlockSpec((tt, c), lambda i: (i, cb))


def _sec(s, tt=TT):
    return pl.BlockSpec((None, tt, D), lambda i: (s, i, 0))


def _const(shape):
    return pl.BlockSpec(shape, lambda i: (0,) * len(shape))


def _acc_spec(c):
    return pl.BlockSpec((8, c), lambda i: (0, 0))


def _rms(x):
    return lax.rsqrt(jnp.mean(x * x, axis=-1, keepdims=True) + EPS)


def _rms_bwd(dy_g, xn, rstd):
    return rstd * (dy_g - xn * jnp.mean(dy_g * xn, axis=-1, keepdims=True))


def _head_sum(x, bd):
    parts = []
    for cb in range(x.shape[-1] // 128):
        xb = x[:, cb * 128:(cb + 1) * 128]
        hi = xb.astype(bf16)
        lo = (xb - hi.astype(f32)).astype(bf16)
        parts.append(jnp.dot(hi, bd, preferred_element_type=f32) + jnp.dot(lo, bd, preferred_element_type=f32))
    return parts[0] if len(parts) == 1 else jnp.concatenate(parts, axis=1)


def _norm1_fwd(x, g):
    T = x.shape[0]

    def body(x_ref, g_ref, h_ref):
        xv = x_ref[...]
        h_ref[...] = (xv * _rms(xv) * g_ref[...]).astype(bf16)

    return pl.pallas_call(
        body, name="norm1_fwd", grid=(T // TT,), in_specs=[_rows(D), _const((1, D))], out_specs=_rows(D),
        out_shape=jax.ShapeDtypeStruct((T, D), bf16), compiler_params=_cparams(("parallel",)))(x, g)


def _convnorm_fwd(c, g):
    T = c.shape[0]

    def body(c_ref, g_ref, s_ref):
        cv = c_ref[...]
        r = cv * _rms(cv) * g_ref[...]
        s_ref[...] = (r * _sig(r)).astype(bf16)

    return pl.pallas_call(
        body, name="convnorm_fwd", grid=(T // TT,), in_specs=[_rows(D), _const((1, D))], out_specs=_rows(D),
        out_shape=jax.ShapeDtypeStruct((T, D), bf16), compiler_params=_cparams(("parallel",)))(c, g)


def _qk_fwd(z8, qg, kg, bd):
    T = z8.shape[1]

    def body(q_ref, k_ref, qg_ref, kg_ref, bd_ref, qn_ref, kn_ref):
        bdv = bd_ref[...]
        q = q_ref[...]
        qn_ref[...] = q * lax.rsqrt(_head_sum(q * q, bdv) * (1.0 / HEAD_DIM) + EPS) * qg_ref[...] * (HEAD_DIM ** -0.5)
        k = k_ref[...]
        kn_ref[...] = k * lax.rsqrt(_head_sum(k * k, bdv) * (1.0 / HEAD_DIM) + EPS) * kg_ref[...]

    return pl.pallas_call(
        body, name="qk_fwd", grid=(T // TT,),
        in_specs=[_sec(Z_Q), _sec(Z_K), _const((1, D)), _const((1, D)), _const((128, 128))],
        out_specs=[_rows(D), _rows(D)],
        out_shape=[jax.ShapeDtypeStruct((T, D), f32)] * 2, compiler_params=_cparams(("parallel",)))(z8, z8, qg, kg, bd)


def _gate_fwd(z8, gate_b, ya, yb):
    T = ya.shape[0]

    def body(ga_ref, gb_ref, b_ref, ya_ref, yb_ref, mixed_ref):
        g_a = _sig(ga_ref[...] + b_ref[:, :D])
        g_b = _sig(gb_ref[...] + b_ref[:, D:])
        mixed_ref[...] = (g_a * ya_ref[...] + g_b * yb_ref[...]).astype(bf16)

    return pl.pallas_call(
        body, name="gate_fwd", grid=(T // TT,),
        in_specs=[_sec(Z_GA), _sec(Z_GB), _const((1, 2 * D)), _rows(D), _rows(D)], out_specs=_rows(D),
        out_shape=jax.ShapeDtypeStruct((T, D), bf16), compiler_params=_cparams(("parallel",)))(z8, z8, gate_b, ya, yb)


def _out_norm2_fwd(mixed, w_out, x, g):
    T = x.shape[0]

    def epilogue(acc, extra, const, out):
        x1 = extra[0][...] + acc
        out[0][...] = x1
        out[1][...] = (x1 * _rms(x1) * const[0][...]).astype(bf16)

    return _matmul_fused("mm_t1_norm2", mixed[None], w_out[None], ((0, 0),), epilogue, [(x, _frows())], [g],
                         [_rowshape(T, f32), _rowshape(T, bf16)])


def _down_loss_fwd(f, w_down, x1, target):
    T = x1.shape[0]

    def epilogue(acc, extra, const, out):
        diff = extra[0][...] + acc - extra[1][...]
        dy = diff * (1.0 / D)
        out[0][...] = dy
        out[1][...] = dy.astype(bf16)
        _add_colsum(out[2], diff * diff)

    return _matmul_fused("mm_t2_loss", f[None], w_down[None], ((0, 0),), epilogue, [(x1, _frows()), (target, _frows())],
                         [], [_rowshape(T, f32), _rowshape(T, bf16), _sumshape()], sums=True)


def _up_norm2_bwd(du3, w_up_t, x1, dy, g, token):
    T = x1.shape[0]

    def epilogue(dh, extra, const, out):
        x1v = extra[0][...]
        rstd = _rms(x1v)
        xn = x1v * rstd
        dx1 = extra[1][...] + _rms_bwd(dh * const[0][...], xn, rstd)
        out[0][...] = dx1
        out[1][...] = dx1.astype(bf16)
        _add_colsum(out[2], dh * xn)

    return _matmul_fused("mm_dh2_norm2", du3, w_up_t.reshape(2, D_FF, D), ((0, 0), (1, 1)), epilogue,
                         [(x1, _frows()), (dy, _frows())], [g],
                         [_rowshape(T, f32), _rowshape(T, bf16), _sumshape()], sums=True, passed=[token])


def _out_gate_bwd(dx1b, w_out, z8, gate_b, ya, yb, dz8):
    T = ya.shape[0]

    def epilogue(dm, extra, const, out):
        b_ref = const[0]
        g_a = _sig(extra[0][...] + b_ref[:, :D])
        g_b = _sig(extra[1][...] + b_ref[:, D:])
        out[0][...] = (dm * g_a).astype(bf16)
        out[1][...] = (dm * g_b).astype(bf16)
        dla = dm * extra[2][...] * g_a * (1.0 - g_a)
        dlb = dm * extra[3][...] * g_b * (1.0 - g_b)
        out[2][0] = dla.astype(bf16)
        out[2][1] = dlb.astype(bf16)
        _add_colsum(out[3], dla, slice(0, D))
        _add_colsum(out[3], dlb, slice(D, 2 * D))

    return _matmul_fused(
        "mm_dmixed_gate", dx1b[None], w_out[None], ((0, 0),), epilogue,
        [(z8, _fsec(Z_GA)), (z8, _fsec(Z_GB)), (ya, _frows()), (yb, _frows())], [gate_b],
        [_rowshape(T, bf16), _rowshape(T, bf16),
         (jax.ShapeDtypeStruct(dz8.shape, bf16), pl.BlockSpec((2, FTM, D), lambda i: (1, i, 0))), _sumshape(2 * D)],
        nt=True, sums=True, passed=[dz8], aliases={0: 2})


def _convnorm_bwd(c, ds, g):
    T = c.shape[0]

    def body(c_ref, ds_ref, g_ref, dc_ref, dg_ref):
        cv = c_ref[...]
        rstd = _rms(cv)
        r0 = cv * rstd
        gv = g_ref[...]
        r = r0 * gv
        sg = _sig(r)
        dr = ds_ref[...] * sg * (1.0 + r * (1.0 - sg))
        dc_ref[...] = _rms_bwd(dr * gv, r0, rstd)

        @pl.when(pl.program_id(0) == 0)
        def _():
            dg_ref[...] = jnp.zeros_like(dg_ref)

        dg_ref[...] += _colsum8(dr * r0)

    return pl.pallas_call(
        body, name="convnorm_bwd", grid=(T // TT,), in_specs=[_rows(D), _rows(D), _const((1, D))],
        out_specs=[_rows(D), _acc_spec(D)],
        out_shape=[jax.ShapeDtypeStruct((T, D), f32), jax.ShapeDtypeStruct((8, D), f32)],
        compiler_params=_cparams(("arbitrary",)))(c, ds, g)


def _qk_bwd(z8, dqn, dkn, dv, qg, kg, bd, dz8):
    T = dqn.shape[0]

    def body(q_ref, k_ref, dqn_ref, dkn_ref, dv_ref, qg_ref, kg_ref, bd_ref, dz_in, dz_ref, dqg_ref, dkg_ref):
        del dz_in
        bdv = bd_ref[...]

        @pl.when(pl.program_id(0) == 0)
        def _():
            dqg_ref[...] = jnp.zeros_like(dqg_ref)
            dkg_ref[...] = jnp.zeros_like(dkg_ref)

        def one(raw, dn_scaled, g, dg_ref, sec):
            rstd = lax.rsqrt(_head_sum(raw * raw, bdv) * (1.0 / HEAD_DIM) + EPS)
            n = raw * rstd
            dg_ref[...] += _colsum8(dn_scaled * n)
            dn = dn_scaled * g
            draw = rstd * (dn - n * (_head_sum(dn * n, bdv) * (1.0 / HEAD_DIM)))
            dz_ref[sec] = draw.astype(bf16)

        one(q_ref[...], dqn_ref[...] * (HEAD_DIM ** -0.5), qg_ref[...], dqg_ref, 0)
        one(k_ref[...], dkn_ref[...], kg_ref[...], dkg_ref, 1)
        dz_ref[2] = dv_ref[...].astype(bf16)
        dz_ref[3] = jnp.zeros((TT, D), bf16)

    return pl.pallas_call(
        body, name="qk_bwd", grid=(T // TT,),
        in_specs=[_sec(Z_Q), _sec(Z_K), _rows(D), _rows(D), _rows(D), _const((1, D)), _const((1, D)),
                  _const((128, 128)), pl.BlockSpec(memory_space=pl.ANY)],
        out_specs=[pl.BlockSpec((4, TT, D), lambda i: (1, i, 0)), _acc_spec(D), _acc_spec(D)],
        out_shape=[jax.ShapeDtypeStruct(dz8.shape, bf16), jax.ShapeDtypeStruct((8, D), f32),
                   jax.ShapeDtypeStruct((8, D), f32)],
        input_output_aliases={8: 0},
        compiler_params=_cparams(("arbitrary",)))(z8, z8, dqn, dkn, dv, qg, kg, bd, dz8)


def _in_norm1_bwd(dz8, w_in_t, x, dx1, g, token):
    T = x.shape[0]

    def epilogue(dh, extra, const, out):
        xv = extra[0][...]
        rstd = _rms(xv)
        xn = xv * rstd
        out[0][...] = extra[1][...] + _rms_bwd(dh * const[0][...], xn, rstd)
        _add_colsum(out[1], dh * xn)

    return _matmul_fused("mm_dh_norm1", dz8, w_in_t.reshape(7, D, D), tuple(zip(range(7), _W_OF_Z)), epilogue,
                         [(x, _frows()), (dx1, _frows())], [g], [_rowshape(T, f32), _sumshape()],
                         sums=True, passed=[token])


CCW = 256
CR = 64
HALO = 32


def _conv_fwd(z8, conv_w, conv_b, S):
    T = z8.shape[1]
    nb = T // S
    ncb = D // CCW

    def body(av_ref, ag_ref, w_ref, b_ref, c_ref, pad):
        pad[0:HALO, :] = jnp.zeros((HALO, CCW), f32)

        def fill(i, carry):
            r0 = pl.multiple_of(i * 256, 256)
            pad[pl.ds(HALO + r0, 256), :] = av_ref[pl.ds(r0, 256), :] * _sig(ag_ref[pl.ds(r0, 256), :])
            return carry

        lax.fori_loop(0, S // 256, fill, 0)
        bias = b_ref[...]

        def chunk(i, carry):
            r0 = pl.multiple_of(i * CR, CR)
            win = pad[pl.ds(r0, CR + HALO), :]
            acc = jnp.zeros((CR, CCW), f32) + bias
            for s in range(8):
                part = None
                for m in range((CONV_WIDTH - 1 - s) // 8 + 1):
                    j = CONV_WIDTH - 1 - 8 * m - s
                    term = win[24 - 8 * m:24 - 8 * m + CR + 8, :] * w_ref[j:j + 1, :]
                    part = term if part is None else part + term
                acc = acc + part[8 - s:8 - s + CR, :]
            c_ref[pl.ds(r0, CR), :] = acc
            return carry

        lax.fori_loop(0, S // CR, chunk, 0)

    zs = lambda s: pl.BlockSpec((None, S, CCW), lambda b, cb: (s, b, cb))
    return pl.pallas_call(
        body, name="conv_fwd", grid=(nb, ncb),
        in_specs=[zs(Z_AVAL), zs(Z_AGATE), pl.BlockSpec((CONV_WIDTH, CCW), lambda b, cb: (0, cb)),
                  pl.BlockSpec((1, CCW), lambda b, cb: (0, cb))],
        out_specs=pl.BlockSpec((S, CCW), lambda b, cb: (b, cb)),
        out_shape=jax.ShapeDtypeStruct((T, D), f32),
        scratch_shapes=[pltpu.VMEM((S + HALO, CCW), f32)],
        compiler_params=_cparams(("parallel", "parallel")))(z8, z8, conv_w, conv_b)


def _conv_bwd(dc, z8, conv_w, dz8, S):
    T = dc.shape[0]
    nb = T // S
    ncb = D // CCW

    def body(dc_ref, av_ref, ag_ref, w_ref, dz_in, dz_ref, dw_ref, apad, dpad, shbuf):
        del dz_in
        apad[0:HALO, :] = jnp.zeros((HALO, CCW), f32)
        dpad[S:S + HALO, :] = jnp.zeros((HALO, CCW), f32)
        dw_ref[...] = jnp.zeros_like(dw_ref)

        def fill(i, carry):
            r0 = pl.multiple_of(i * 256, 256)
            apad[pl.ds(HALO + r0, 256), :] = av_ref[pl.ds(r0, 256), :] * _sig(ag_ref[pl.ds(r0, 256), :])
            dpad[pl.ds(r0, 256), :] = dc_ref[pl.ds(r0, 256), :]
            return carry

        lax.fori_loop(0, S // 256, fill, 0)

        def chunk(i, carry):
            r0 = pl.multiple_of(i * CR, CR)
            dwin = dpad[pl.ds(r0, CR + HALO), :]
            da = jnp.zeros((CR, CCW), f32)
            for s in range(8):
                shbuf[...] = dwin[s:s + CR, :]
                dshift = shbuf[...]
                part = None
                for m in range((CONV_WIDTH - 1 - s) // 8 + 1):
                    j = CONV_WIDTH - 1 - 8 * m - s
                    term = dwin[8 * m:8 * m + CR + 8, :] * w_ref[j:j + 1, :]
                    part = term if part is None else part + term
                    a_lag = apad[pl.ds(r0 + HALO - 8 * m, CR), :]
                    dw_ref[8 * j:8 * j + 8, :] += _colsum8(dshift * a_lag)
                da = da + part[s:s + CR, :]
            dw_ref[8 * CONV_WIDTH:8 * CONV_WIDTH + 8, :] += _colsum8(dwin[0:CR, :])
            av = av_ref[pl.ds(r0, CR), :]
            sg = _sig(ag_ref[pl.ds(r0, CR), :])
            dz_ref[0, pl.ds(r0, CR), :] = (da * sg).astype(bf16)
            dz_ref[1, pl.ds(r0, CR), :] = (da * av * sg * (1.0 - sg)).astype(bf16)
            return carry

        lax.fori_loop(0, S // CR, chunk, 0)

    zs = lambda s: pl.BlockSpec((None, S, CCW), lambda b, cb: (s, b, cb))
    return pl.pallas_call(
        body, name="conv_bwd", grid=(nb, ncb),
        in_specs=[pl.BlockSpec((S, CCW), lambda b, cb: (b, cb)), zs(Z_AVAL), zs(Z_AGATE),
                  pl.BlockSpec((CONV_WIDTH, CCW), lambda b, cb: (0, cb)), pl.BlockSpec(memory_space=pl.ANY)],
        out_specs=[pl.BlockSpec((2, S, CCW), lambda b, cb: (0, b, cb)),
                   pl.BlockSpec((None, 256, CCW), lambda b, cb: (b, 0, cb))],
        out_shape=[jax.ShapeDtypeStruct(dz8.shape, bf16), jax.ShapeDtypeStruct((nb, 256, D), f32)],
        input_output_aliases={4: 0},
        scratch_shapes=[pltpu.VMEM((S + HALO, CCW), f32), pltpu.VMEM((S + HALO, CCW), f32),
                        pltpu.VMEM((CR, CCW), f32)],
        compiler_params=_cparams(("parallel", "parallel")))(dc, z8, z8, conv_w, dz8)


FR = 128
NFB = D_FF // CCW


def _ffn_window(ref, i, r0):
    return ref[pl.ds(r0 - 8, FR + 8), :]


def _ffn_u(win, w_ref, b_ref):
    return (win[6:6 + FR, :] * w_ref[0:1, :] + win[7:7 + FR, :] * w_ref[1:2, :]
            + win[8:8 + FR, :] * w_ref[2:3, :] + b_ref[...])


def _ffn_fwd(u3, ffn_w, ffn_b, S):
    T = u3.shape[1]
    nb = T // S

    def body(uv_ref, ug_ref, wv_ref, wg_ref, bv_ref, bg_ref, f_ref):
        def chunk(first, i):
            r0 = 0 if first else pl.multiple_of(i * FR, FR)
            if first:
                z = jnp.zeros((8, CCW), f32)
                wv = jnp.concatenate([z, uv_ref[0:FR, :]], axis=0)
                wg = jnp.concatenate([z, ug_ref[0:FR, :]], axis=0)
            else:
                wv = _ffn_window(uv_ref, i, r0)
                wg = _ffn_window(ug_ref, i, r0)
            u_val = _ffn_u(wv, wv_ref, bv_ref)
            u_gate = _ffn_u(wg, wg_ref, bg_ref)
            f_ref[pl.ds(r0, FR), :] = (u_gate * _sig(u_gate) * u_val).astype(bf16)

        chunk(True, 0)

        def loop(i, carry):
            chunk(False, i)
            return carry

        lax.fori_loop(1, S // FR, loop, 0)

    us = lambda h: pl.BlockSpec((None, S, CCW), lambda b, cb: (h, b, cb))
    ws = lambda h: pl.BlockSpec((3, CCW), lambda b, cb: (0, h * NFB + cb))
    bs = lambda h: pl.BlockSpec((1, CCW), lambda b, cb: (0, h * NFB + cb))
    return pl.pallas_call(
        body, name="ffn_fwd", grid=(nb, NFB),
        in_specs=[us(0), us(1), ws(0), ws(1), bs(0), bs(1)],
        out_specs=pl.BlockSpec((S, CCW), lambda b, cb: (b, cb)),
        out_shape=jax.ShapeDtypeStruct((T, D_FF), bf16),
        compiler_params=_cparams(("parallel", "parallel")))(u3, u3, ffn_w, ffn_w, ffn_b, ffn_b)


def _ffn_bwd(u3, df, ffn_w, ffn_b, S):
    T = u3.shape[1]
    nb = T // S

    def body(uv_ref, ug_ref, df_ref, wv_ref, wg_ref, bv_ref, bg_ref, du_ref, dw_ref, dvpad, dgpad, shbuf):
        dvpad[S:S + 8, :] = jnp.zeros((8, CCW), f32)
        dgpad[S:S + 8, :] = jnp.zeros((8, CCW), f32)
        dw_ref[...] = jnp.zeros_like(dw_ref)

        def chunk(first, i):
            r0 = 0 if first else pl.multiple_of(i * FR, FR)
            if first:
                z = jnp.zeros((8, CCW), f32)
                wv = jnp.concatenate([z, uv_ref[0:FR, :]], axis=0)
                wg = jnp.concatenate([z, ug_ref[0:FR, :]], axis=0)
            else:
                wv = _ffn_window(uv_ref, i, r0)
                wg = _ffn_window(ug_ref, i, r0)
            taps = []
            for h, win in enumerate((wv, wg)):
                shbuf[2 * h] = win[6:6 + FR, :]
                shbuf[2 * h + 1] = win[7:7 + FR, :]
                taps.append((shbuf[2 * h], shbuf[2 * h + 1], win[8:8 + FR, :]))
            conv = lambda x, w_ref, b_ref: (x[0] * w_ref[0:1, :] + x[1] * w_ref[1:2, :] + x[2] * w_ref[2:3, :]
                                            + b_ref[...])
            u_val = conv(taps[0], wv_ref, bv_ref)
            u_gate = conv(taps[1], wg_ref, bg_ref)
            dfc = df_ref[pl.ds(r0, FR), :]
            sg = _sig(u_gate)
            d_val = dfc * u_gate * sg
            d_gate = dfc * u_val * sg * (1.0 + u_gate * (1.0 - sg))
            dvpad[pl.ds(r0, FR), :] = d_val
            dgpad[pl.ds(r0, FR), :] = d_gate
            for h, dd in enumerate((d_val, d_gate)):
                for j in range(3):
                    dw_ref[h, 8 * j:8 * j + 8, :] += _colsum8(dd * taps[h][j])
                dw_ref[h, 24:32, :] += _colsum8(dd)

        chunk(True, 0)

        def loop(i, carry):
            chunk(False, i)
            return carry

        lax.fori_loop(1, S // FR, loop, 0)

        def back(i, carry):
            r0 = pl.multiple_of(i * FR, FR)
            for h, (dpad, w_ref) in enumerate(((dvpad, wv_ref), (dgpad, wg_ref))):
                win = dpad[pl.ds(r0, FR + 8), :]
                du = (win[0:FR, :] * w_ref[2:3, :] + win[1:1 + FR, :] * w_ref[1:2, :]
                      + win[2:2 + FR, :] * w_ref[0:1, :])
                du_ref[h, pl.ds(r0, FR), :] = du.astype(bf16)
            return carry

        lax.fori_loop(0, S // FR, back, 0)

    us = lambda h: pl.BlockSpec((None, S, CCW), lambda b, cb: (h, b, cb))
    ws = lambda h: pl.BlockSpec((3, CCW), lambda b, cb: (0, h * NFB + cb))
    bs = lambda h: pl.BlockSpec((1, CCW), lambda b, cb: (0, h * NFB + cb))
    return pl.pallas_call(
        body, name="ffn_bwd", grid=(nb, NFB),
        in_specs=[us(0), us(1), pl.BlockSpec((S, CCW), lambda b, cb: (b, cb)), ws(0), ws(1), bs(0), bs(1)],
        out_specs=[pl.BlockSpec((2, S, CCW), lambda b, cb: (0, b, cb)),
                   pl.BlockSpec((None, 2, 32, CCW), lambda b, cb: (b, 0, 0, cb))],
        out_shape=[jax.ShapeDtypeStruct((2, T, D_FF), bf16), jax.ShapeDtypeStruct((nb, 2, 32, D_FF), f32)],
        scratch_shapes=[pltpu.VMEM((S + 8, CCW), f32), pltpu.VMEM((S + 8, CCW), f32),
                        pltpu.VMEM((4, FR, CCW), f32)],
        compiler_params=_cparams(("parallel", "parallel")))(u3, u3, df, ffn_w, ffn_w, ffn_b, ffn_b)


AB = ATTN_BLOCK


def _attn_bias_np():
    slopes = (np.float32(2.0) ** (np.float32(-8.0) * np.arange(1, N_HEADS + 1, dtype=np.float32)
                                  / np.float32(N_HEADS))).astype(np.float32)
    steps = (np.arange(AB)[:, None] + AB) - np.arange(2 * AB)[None, :]
    own = (np.arange(2 * AB) >= AB)[None, :]
    out = []
    for window, dil in GROUPS:
        valid = (steps >= 0) & (steps <= window // dil)
        dist = slopes[:, None, None] * (steps * dil).astype(np.float32)[None]
        kinds = [np.where(v[None], dist, np.float32(MASK_BIAS)) for v in (valid, valid & own)]
        out.append(np.stack(kinds, axis=1))
    return np.stack(out).astype(np.float32)


def _attn_bias():
    return jnp.asarray(_attn_bias_np())


def _head_masks():
    lane = lax.broadcasted_iota(jnp.int32, (1, 128), 1)
    return (lane < HEAD_DIM, lane >= HEAD_DIM)


def _perm_chunks(S, d):
    L = S // d
    ch = min(L, 256)
    out = []
    for r in range(d):
        for c in range(L // ch):
            start = r + d * ch * c
            out.append((pl.ds(start, ch, stride=d) if d > 1 else pl.ds(start, ch), r * L + c * ch, ch))
    return out


def _stack_heads(x, masks):
    return jnp.concatenate([jnp.where(masks[0], x, 0), jnp.where(masks[1], x, 0)], axis=0)


def _block_row(j):
    return j * AB if isinstance(j, int) else pl.multiple_of(j * AB, AB)


def _three_stages(n, stage_a, stage_b, stage_c, unroll):
    stage_a(0)
    stage_a(1)
    stage_b(0)

    def body(j, carry):
        stage_c(j - 1)
        stage_b(j)
        stage_a(j + 1)
        return carry

    lax.fori_loop(1, n - 1, body, 0, unroll=unroll)
    stage_c(n - 2)
    stage_b(n - 1)
    stage_c(n - 1)


_NT = (((1,), (1,)), ((), ()))
_TN = (((0,), (0,)), ((), ()))
SCH = 64


def _attn_fwd(qn, kn, z8, bias, S):
    T = qn.shape[0]
    nb = T // S
    nblk = S // AB

    def body(q_ref, k_ref, v_ref, bias_ref, o_ref, ob_ref, lse_ref, qs, ks, vs, s2, p2, ogp, lgp, *group_scratch):
        og, lg = group_scratch[:3], group_scratch[3:]
        masks = _head_masks()
        ks[0:AB, :] = jnp.zeros((AB, 128), bf16)
        vs[0:AB, :] = jnp.zeros((AB, 128), bf16)

        for g, (_, d) in enumerate(GROUPS):
            nsub = S // (d * AB)
            chunks = _perm_chunks(S, d)
            for src, dst, ch in chunks:
                qs[dst:dst + ch, :] = q_ref[src, :].astype(bf16)
                ks[AB + dst:AB + dst + ch, :] = k_ref[src, :].astype(bf16)
                vs[AB + dst:AB + dst + ch, :] = v_ref[src, :].astype(bf16)
            od, ld = (og[g], lg[g]) if d == 1 else (ogp, lgp)

            def scores(j):
                r0 = _block_row(j)
                q2 = _stack_heads(qs[pl.ds(r0, AB), :], masks)
                s2[j] = lax.dot_general(q2, ks[pl.ds(r0, 2 * AB), :], _NT, preferred_element_type=f32)

            def softmax(j, g=g, nsub=nsub, ld=ld):
                r0 = _block_row(j)
                kind = int(j % nsub == 0) if isinstance(j, int) else (j % nsub == 0).astype(jnp.int32)
                for cc in range(AB // SCH):
                    lses = []
                    for hh in range(2):
                        rows = pl.ds(hh * AB + cc * SCH, SCH)
                        sb = s2[j, rows, :] - bias_ref[g, hh, kind, cc * SCH:(cc + 1) * SCH, :]
                        m = jnp.max(sb, axis=-1, keepdims=True)
                        p = jnp.exp(sb - m)
                        den = jnp.sum(p, axis=-1, keepdims=True)
                        p2[j, rows, :] = (p * (1.0 / den)).astype(bf16)
                        lses.append(m + jnp.log(den))
                    ld[pl.ds(r0 + cc * SCH, SCH), :] = jnp.where(masks[0], lses[0], lses[1])

            def values(j, od=od):
                r0 = _block_row(j)
                pv2 = jnp.dot(p2[j], vs[pl.ds(r0, 2 * AB), :], preferred_element_type=f32)
                od[pl.ds(r0, AB), :] = jnp.where(masks[0], pv2[:AB], pv2[AB:])

            _three_stages(nblk, scores, softmax, values, nblk - 2)

            if d > 1:
                for src, dst, ch in chunks:
                    og[g][src, :] = ogp[dst:dst + ch, :]
                    lg[g][src, :] = lgp[dst:dst + ch, :]

        def combine(i, carry):
            rr = pl.ds(pl.multiple_of(i * 256, 256), 256)
            l0, l1, l2 = lg[0][rr, :], lg[1][rr, :], lg[2][rr, :]
            mx = jnp.maximum(jnp.maximum(l0, l1), l2)
            e0, e1, e2 = jnp.exp(l0 - mx), jnp.exp(l1 - mx), jnp.exp(l2 - mx)
            den = e0 + e1 + e2
            o = (e0 * og[0][rr, :] + e1 * og[1][rr, :] + e2 * og[2][rr, :]) / den
            o_ref[rr, :] = o
            ob_ref[rr, :] = o.astype(bf16)
            lse_ref[rr, :] = mx + jnp.log(den)
            return carry

        lax.fori_loop(0, S // 256, combine, 0)

    blk = pl.BlockSpec((S, 128), lambda b, hp: (b, hp))
    return pl.pallas_call(
        body, name="attn_fwd", grid=(nb, N_HEADS // 2),
        in_specs=[blk, blk, pl.BlockSpec((None, S, 128), lambda b, hp: (Z_V, b, hp)),
                  pl.BlockSpec((3, 2, 2, AB, 2 * AB), lambda b, hp: (0, hp, 0, 0, 0))],
        out_specs=[blk, blk, blk],
        out_shape=[jax.ShapeDtypeStruct((T, D), f32), jax.ShapeDtypeStruct((T, D), bf16),
                   jax.ShapeDtypeStruct((T, D), f32)],
        scratch_shapes=[pltpu.VMEM((S, 128), bf16), pltpu.VMEM((S + AB, 128), bf16), pltpu.VMEM((S + AB, 128), bf16),
                        pltpu.VMEM((nblk, 2 * AB, 2 * AB), f32), pltpu.VMEM((nblk, 2 * AB, 2 * AB), bf16),
                        pltpu.VMEM((S, 128), f32), pltpu.VMEM((S, 128), f32)] + [pltpu.VMEM((S, 128), f32)] * 6,
        compiler_params=_cparams(("parallel", "parallel")))(qn, kn, z8, bias)


def _attn_bwd(qn, kn, z8, do, o, lse, bias, bd, S, after):
    T = qn.shape[0]
    nb = T // S

    nblk = S // AB

    def body(q_ref, k_ref, v_ref, do_ref, o_ref, lse_ref, bias_ref, bd_ref, after_ref, dq_ref, dk_ref, dv_ref,
             delta, qs, ks, vs, dos, lsp, dlp, s2, dp2, p2, ds2, dqp, dkp, dvp):
        del after_ref
        masks = _head_masks()
        bdv = bd_ref[...]
        dq_ref[...] = jnp.zeros_like(dq_ref)
        dk_ref[...] = jnp.zeros_like(dk_ref)
        dv_ref[...] = jnp.zeros_like(dv_ref)
        ks[0:AB, :] = jnp.zeros((AB, 128), bf16)
        vs[0:AB, :] = jnp.zeros((AB, 128), bf16)

        def prep(i, carry):
            rr = pl.ds(pl.multiple_of(i * 256, 256), 256)
            delta[rr, :] = _head_sum(do_ref[rr, :] * o_ref[rr, :], bdv)
            return carry

        lax.fori_loop(0, S // 256, prep, 0, unroll=True)

        for g, (_, d) in enumerate(GROUPS):
            nsub = S // (d * AB)
            chunks = _perm_chunks(S, d)
            for src, dst, ch in chunks:
                qs[dst:dst + ch, :] = q_ref[src, :].astype(bf16)
                ks[AB + dst:AB + dst + ch, :] = k_ref[src, :].astype(bf16)
                vs[AB + dst:AB + dst + ch, :] = v_ref[src, :].astype(bf16)
                dos[dst:dst + ch, :] = do_ref[src, :].astype(bf16)
                lsp[dst:dst + ch, :] = lse_ref[src, :]
                dlp[dst:dst + ch, :] = delta[src, :]
            dkp[...] = jnp.zeros_like(dkp)
            dvp[...] = jnp.zeros_like(dvp)

            def scores(j):
                r0 = _block_row(j)
                q2 = _stack_heads(qs[pl.ds(r0, AB), :], masks)
                do2 = _stack_heads(dos[pl.ds(r0, AB), :], masks)
                s2[j] = lax.dot_general(q2, ks[pl.ds(r0, 2 * AB), :], _NT, preferred_element_type=f32)
                dp2[j] = lax.dot_general(do2, vs[pl.ds(r0, 2 * AB), :], _NT, preferred_element_type=f32)

            def probs(j, g=g, nsub=nsub):
                r0 = _block_row(j)
                kind = int(j % nsub == 0) if isinstance(j, int) else (j % nsub == 0).astype(jnp.int32)
                for cc in range(AB // SCH):
                    lse_c = lsp[pl.ds(r0 + cc * SCH, SCH), :]
                    del_c = dlp[pl.ds(r0 + cc * SCH, SCH), :]
                    for hh in range(2):
                        c0 = hh * HEAD_DIM
                        rows = pl.ds(hh * AB + cc * SCH, SCH)
                        sb = s2[j, rows, :] - bias_ref[g, hh, kind, cc * SCH:(cc + 1) * SCH, :]
                        p = jnp.exp(sb - lse_c[:, c0:c0 + 1])
                        p2[j, rows, :] = p.astype(bf16)
                        ds2[j, rows, :] = (p * (dp2[j, rows, :] - del_c[:, c0:c0 + 1])).astype(bf16)

            def grads(j):
                r0 = _block_row(j)
                q2 = _stack_heads(qs[pl.ds(r0, AB), :], masks)
                do2 = _stack_heads(dos[pl.ds(r0, AB), :], masks)
                dsb = ds2[j]
                t = jnp.dot(dsb, ks[pl.ds(r0, 2 * AB), :], preferred_element_type=f32)
                dqp[pl.ds(r0, AB), :] = jnp.where(masks[0], t[:AB], t[AB:])
                dkp[pl.ds(r0, 2 * AB), :] += lax.dot_general(dsb, q2, _TN, preferred_element_type=f32)
                dvp[pl.ds(r0, 2 * AB), :] += lax.dot_general(p2[j], do2, _TN, preferred_element_type=f32)

            _three_stages(nblk, scores, probs, grads, nblk - 2)

            for src, dst, ch in chunks:
                dq_ref[src, :] += dqp[dst:dst + ch, :]
                dk_ref[src, :] += dkp[AB + dst:AB + dst + ch, :]
                dv_ref[src, :] += dvp[AB + dst:AB + dst + ch, :]

    blk = pl.BlockSpec((S, 128), lambda b, hp: (b, hp))
    row = lambda dt, pad=0: pltpu.VMEM((S + pad, 128), dt)
    blocks = lambda dt: pltpu.VMEM((nblk, 2 * AB, 2 * AB), dt)
    return pl.pallas_call(
        body, name="attn_bwd", grid=(nb, N_HEADS // 2),
        in_specs=[blk, blk, pl.BlockSpec((None, S, 128), lambda b, hp: (Z_V, b, hp)), blk, blk, blk,
                  pl.BlockSpec((3, 2, 2, AB, 2 * AB), lambda b, hp: (0, hp, 0, 0, 0)),
                  pl.BlockSpec((128, 128), lambda b, hp: (0, 0)), pl.BlockSpec(memory_space=pl.ANY)],
        out_specs=[blk, blk, blk],
        out_shape=[jax.ShapeDtypeStruct((T, D), f32)] * 3,
        scratch_shapes=[row(f32), row(bf16), row(bf16, AB), row(bf16, AB), row(bf16), row(f32), row(f32),
                        blocks(f32), blocks(f32), blocks(bf16), blocks(bf16), row(f32), row(f32, AB), row(f32, AB)],
        compiler_params=_cparams(("parallel", "parallel")))(qn, kn, z8, do, o, lse, bias, bd, after)


def _any_spec():
    return pl.BlockSpec(memory_space=pl.ANY)


def _allgather_rows(shards, n_full):
    n = len(shards)

    def body(*refs):
        ins, outs = refs[:n], refs[n:2 * n]
        send_sems, recv_sems, local_sems = refs[2 * n:]
        x, y, c, me = _my_pos()
        sibling = (x, y, 1 - c)
        chips = [(1 - x, y), (x, 1 - y), (1 - x, 1 - y)]

        def idx(px, py, pc):
            return 4 * px + 2 * py + pc

        def copy(a, k, blk, to, src=None):
            return pltpu.make_async_remote_copy(
                src_ref=outs[a].at[blk] if src is None else src, dst_ref=outs[a].at[blk],
                send_sem=send_sems.at[a, k], recv_sem=recv_sems.at[a, k], device_id=to, device_id_type=MESH)

        mine = [pltpu.make_async_copy(ins[a], outs[a].at[me], local_sems.at[a]) for a in range(n)]
        for cp in mine:
            cp.start()
        first = []
        for a in range(n_full):
            first.append(copy(a, 0, me, sibling, src=ins[a]))
            first += [copy(a, 1 + j, me, (*chip, c), src=ins[a]) for j, chip in enumerate(chips)]
        for cp in first:
            cp.start()
        passed = []
        for a in range(n_full):
            for j, chip in enumerate(chips):
                blk = idx(*chip, c)
                copy(a, 1 + j, blk, (x, y, c)).wait_recv()
                cp = copy(a, 4 + j, blk, sibling)
                cp.start()
                passed.append(cp)
        for a in range(n_full):
            copy(a, 0, idx(x, y, 1 - c), (x, y, c)).wait_recv()
            for j, chip in enumerate(chips):
                copy(a, 4 + j, idx(*chip, 1 - c), (x, y, c)).wait_recv()
        for cp in first + passed:
            cp.wait_send()
        for cp in mine:
            cp.wait()

    return pl.pallas_call(
        body, name="allgather_weights",
        in_specs=[_any_spec()] * n, out_specs=[_any_spec()] * n,
        out_shape=[jax.ShapeDtypeStruct((N_DEV,) + s.shape, s.dtype) for s in shards],
        scratch_shapes=[pltpu.SemaphoreType.DMA((n_full, 7)), pltpu.SemaphoreType.DMA((n_full, 7)),
                        pltpu.SemaphoreType.DMA((n,))],
    )(*shards)


def _peer(x, y, c, k):
    tx = 1 - x if (k >> 2) & 1 else x
    ty = 1 - y if (k >> 1) & 1 else y
    tc = 1 - c if k & 1 else c
    return (tx, ty, tc), 4 * tx + 2 * ty + tc


_PEER_ORDER = (2, 4, 6, 3, 5, 7, 1)


_HBM = pl.BlockSpec(memory_space=pltpu.HBM)
_SEM = pl.BlockSpec(memory_space=pltpu.SEMAPHORE)
_EFFECT = pltpu.SideEffectType.DATAFLOW_SIDE_EFFECTING


def _exchange_copies(srcs, lands, send_sems, recv_sems, gather, half):
    x, y, c, me = _my_pos()
    pick = lambda px, py: None if half is None else ((px == py) if half == 0 else (px != py))
    copies = []
    for k in _PEER_ORDER:
        tgt, tidx = _peer(x, y, c, k)
        for a in range(len(srcs)):
            copies.append((pltpu.make_async_remote_copy(
                src_ref=srcs[a] if gather else srcs[a].at[tidx], dst_ref=lands[a].at[me],
                send_sem=send_sems.at[7 * a + k - 1], recv_sem=recv_sems.at[7 * a + k - 1],
                device_id=tgt, device_id_type=MESH), pick(tgt[0], tgt[1])))
    return copies, pick(x, y)


def _when(cond, fn):
    if cond is None:
        fn()
    else:
        pl.when(cond)(fn)


def _exchange_start(name, srcs, lands=None, after=None, gather=None, half=None):
    n = len(srcs)
    gather = (lands is not None) if gather is None else gather
    if lands is None:
        lands = [lax.empty(g.shape, g.dtype) for g in srcs]
    extra = [] if after is None else [after]

    def body(*refs):
        src_refs, land_refs = refs[:n], refs[n:2 * n]
        send_sems, recv_sems = refs[2 * n + len(extra)], refs[2 * n + len(extra) + 1]
        token = refs[-1]
        for cp, sends in _exchange_copies(src_refs, land_refs, send_sems, recv_sems, gather, half)[0]:
            _when(sends, cp.start)
        token[...] = jnp.zeros_like(token)

    hbm = lambda a: pltpu.with_memory_space_constraint(a, pltpu.HBM)
    outs = pl.pallas_call(
        body, name=name,
        out_shape=(pltpu.SemaphoreType.DMA((7 * n,)), pltpu.SemaphoreType.DMA((7 * n,)),
                   *[pltpu.HBM(g.shape, g.dtype) for g in list(srcs) + list(lands)],
                   jax.ShapeDtypeStruct((8, 128), f32)),
        in_specs=[_HBM] * (2 * n) + [pl.BlockSpec(memory_space=pl.ANY)] * len(extra),
        out_specs=(_SEM, _SEM, *([_HBM] * (2 * n)), pl.BlockSpec(memory_space=pltpu.VMEM)),
        input_output_aliases={i: 2 + i for i in range(2 * n)},
        compiler_params=pltpu.CompilerParams(has_side_effects=_EFFECT),
    )(*[hbm(g) for g in srcs], *[hbm(g) for g in lands], *extra)
    return outs[0], outs[1], list(outs[2:2 + n]), list(outs[2 + n:2 + 2 * n]), outs[-1], gather, half


def _exchange_wait(name, started, after):
    send_sems, recv_sems, srcs, lands, _, gather, half = started
    n = len(srcs)
    after = list(after) if isinstance(after, (list, tuple)) else [after]

    def body(*refs):
        src_refs, land_refs = refs[:n], refs[n:2 * n]
        s_sems, r_sems = refs[2 * n], refs[2 * n + 1]
        copies, receives = _exchange_copies(src_refs, land_refs, s_sems, r_sems, gather, half)
        for cp, sends in copies:
            _when(sends, cp.wait_send)
            _when(receives, cp.wait_recv)

    outs = pl.pallas_call(
        body, name=name,
        out_shape=tuple(pltpu.HBM(a.shape, a.dtype) for a in list(srcs) + list(lands)),
        in_specs=[_HBM] * (2 * n) + [_SEM, _SEM] + [pl.BlockSpec(memory_space=pl.ANY)] * len(after),
        out_specs=tuple([_HBM] * (2 * n)),
        input_output_aliases={i: i for i in range(2 * n)},
        compiler_params=pltpu.CompilerParams(has_side_effects=_EFFECT),
    )(*srcs, *lands, send_sems, recv_sems, *after)
    return list(outs[:n]), list(outs[n:])


SMALL_ROWS = 128


def _small_start(name, sg, after=None):
    return _exchange_start(name, [sg], [lax.empty((N_DEV,) + sg.shape, f32)], after=after)


def _small_sum(name, me, started, after):
    (own,), (slots,) = _exchange_wait(name + "_wait", started, after)

    def body(me_ref, s_ref, own_ref, out_ref):
        acc = None
        for p in range(N_DEV):
            term = lax.cond(me_ref[0] == p, lambda: own_ref[...], lambda p=p: s_ref[p])
            acc = term if acc is None else acc + term
        out_ref[...] = acc

    return pl.pallas_call(
        body, name=name + "_sum",
        in_specs=[pl.BlockSpec(memory_space=pltpu.SMEM), pl.BlockSpec(memory_space=pltpu.VMEM),
                  pl.BlockSpec(memory_space=pltpu.VMEM)],
        out_specs=pl.BlockSpec(memory_space=pltpu.VMEM),
        out_shape=jax.ShapeDtypeStruct(own.shape, f32))(me, slots, own)


def _adam_math(g, w, m, v):
    m = ADAM_B1 * m + (1.0 - ADAM_B1) * g
    v = ADAM_B2 * v + (1.0 - ADAM_B2) * (g * g)
    m_hat = m / (1.0 - ADAM_B1 ** ADAM_STEP)
    v_hat = v / (1.0 - ADAM_B2 ** ADAM_STEP)
    delta = -ADAM_LR * (m_hat / (jnp.sqrt(v_hat) + ADAM_EPS) + ADAM_WD * w)
    return delta, m, v


def _adam_slots(name, me, slots, own, w, m, v, tr, transposed=False):
    rows = slots.shape[1]

    def body(me_ref, s_ref, own_ref, w_ref, m_ref, v_ref, g_ref, d_ref, nm_ref, nv_ref):
        mine = own_ref[...]
        g = None
        for p in range(N_DEV):
            term = lax.cond(me_ref[0] == p, lambda: mine, lambda p=p: s_ref[p]).astype(f32)
            g = term if g is None else g + term
        if transposed:
            g = g.T
        delta, nm, nv = _adam_math(g, w_ref[...], m_ref[...], v_ref[...])
        g_ref[...] = g
        d_ref[...] = delta
        nm_ref[...] = nm
        nv_ref[...] = nv

    mode = dict(pipeline_mode=pl.Buffered(1)) if rows == tr else {}
    if transposed:
        rs = pl.BlockSpec((D, tr), lambda i, me_ref: (0, i))
        rs_in = pl.BlockSpec((D, tr), lambda i, me_ref: (0, i), **mode)
    else:
        rs = pl.BlockSpec((tr, D), lambda i, me_ref: (i, 0))
        rs_in = pl.BlockSpec((tr, D), lambda i, me_ref: (i, 0), **mode)
    return pl.pallas_call(
        body, name=name,
        grid_spec=pltpu.PrefetchScalarGridSpec(
            num_scalar_prefetch=1, grid=(rows // tr,),
            in_specs=[pl.BlockSpec((N_DEV, tr, D), lambda i, me_ref: (0, i, 0), **mode),
                      pl.BlockSpec((None, tr, D), lambda i, me_ref: (me_ref[0], i, 0), **mode), rs_in, rs_in, rs_in],
            out_specs=[rs] * 4),
        out_shape=[jax.ShapeDtypeStruct(w.shape, f32)] * 4,
        compiler_params=_cparams(("parallel",)))(me, slots, own, w, m, v)


def _adam_small(g, w, m, v):
    def body(g_ref, w_ref, m_ref, v_ref, d_ref, nm_ref, nv_ref):
        delta, nm, nv = _adam_math(g_ref[...], w_ref[...], m_ref[...], v_ref[...])
        d_ref[...] = delta
        nm_ref[...] = nm
        nv_ref[...] = nv

    return pl.pallas_call(body, name="adam_small", out_shape=[jax.ShapeDtypeStruct(g.shape, f32)] * 3)(g, w, m, v)


FFN_PAD = 6 * D


_SMALL_PARTS = (("norm1_g", 1), ("gate_b", 2), ("conv_w", CONV_WIDTH), ("conv_b", 1), ("conv_norm_g", 1),
                ("q_norm_g", 1), ("k_norm_g", 1), ("norm2_g", 1), ("ffn_conv_w", 18), ("ffn_conv_b", 6), ("last", 1))


def _small_offsets():
    out, row = {}, 0
    for name, rows in _SMALL_PARTS:
        out[name] = row
        row += -(-rows // 8) * 8
    assert row == SMALL_ROWS
    return out


def _pack_small(norm1_g, gate_b, conv_w, conv_b, conv_norm_g, q_norm_g, k_norm_g, norm2_g, ffn_conv_w, ffn_conv_b,
                last_row=None):
    pad_h = lambda a: jnp.pad(a, ((0, 0), (0, D - HEAD_DIM)))
    pad_f = lambda a: jnp.pad(a, ((0, 0), (0, FFN_PAD - 2 * D_FF))).reshape(-1, D)
    parts = [norm1_g, gate_b.reshape(2, D), conv_w, conv_b, conv_norm_g, pad_h(q_norm_g), pad_h(k_norm_g), norm2_g,
             pad_f(ffn_conv_w), pad_f(ffn_conv_b), jnp.zeros((1, D), f32) if last_row is None else last_row]
    return jnp.concatenate([jnp.pad(p, ((0, -p.shape[0] % 8), (0, 0))) for p in parts], axis=0)


def _unpack_small(p):
    o = _small_offsets()
    rows = lambda name, n: p[o[name]:o[name] + n]
    ffn = lambda a: a.reshape(-1, FFN_PAD)[:, :2 * D_FF]
    return dict(
        norm1_g=rows("norm1_g", 1), gate_b=rows("gate_b", 2).reshape(1, 2 * D), conv_w=rows("conv_w", CONV_WIDTH),
        conv_b=rows("conv_b", 1), conv_norm_g=rows("conv_norm_g", 1), q_norm_g=rows("q_norm_g", 1)[:, :HEAD_DIM],
        k_norm_g=rows("k_norm_g", 1)[:, :HEAD_DIM], norm2_g=rows("norm2_g", 1),
        ffn_conv_w=ffn(rows("ffn_conv_w", 18)), ffn_conv_b=ffn(rows("ffn_conv_b", 6)))


_ADAM_TILE = {896: 128, 704: 704, 128: 128, 352: 176}


def kernel(x, norm1_g, w_in, gate_b, conv_w, conv_b, conv_norm_g, w_conv_out, q_norm_g, k_norm_g, w_attn_out, w_out, norm2_g, w_up, ffn_conv_w, ffn_conv_b, w_down, loss_target, m_norm1_g, m_w_in, m_gate_b, m_conv_w, m_conv_b, m_conv_norm_g, m_w_conv_out, m_q_norm_g, m_k_norm_g, m_w_attn_out, m_w_out, m_norm2_g, m_w_up, m_ffn_conv_w, m_ffn_conv_b, m_w_down, v_norm1_g, v_w_in, v_gate_b, v_conv_w, v_conv_b, v_conv_norm_g, v_w_conv_out, v_q_norm_g, v_k_norm_g, v_w_attn_out, v_w_out, v_norm2_g, v_w_up, v_ffn_conv_w, v_ffn_conv_b, v_w_down):
    BL, S, _ = x.shape
    T = BL * S
    me = 4 * lax.axis_index("x") + 2 * lax.axis_index("y") + lax.axis_index("c")
    xt = x.reshape(T, D)
    target = loss_target.reshape(T, D)

    big = dict(w_in=(w_in[0], m_w_in[0], v_w_in[0]), w_up=(w_up[0], m_w_up[0], v_w_up[0]),
               w_conv_out=(w_conv_out[0], m_w_conv_out[0], v_w_conv_out[0]),
               w_attn_out=(w_attn_out[0], m_w_attn_out[0], v_w_attn_out[0]),
               w_out=(w_out[0], m_w_out[0], v_w_out[0]), w_down=(w_down[0], m_w_down[0], v_w_down[0]))
    order = ["w_in", "w_conv_out", "w_attn_out", "w_out", "w_up", "w_down"]
    shards = [(big[n][0].T if n in ("w_in", "w_up") else big[n][0]).astype(bf16) for n in order]
    gathered = _allgather_rows(shards, 1)
    ga_proj = _exchange_start("gather_start_proj", shards[1:4], gathered[1:4], after=gathered[0])
    ga_ffn = _exchange_start("gather_start_ffn", shards[4:6], gathered[4:6], after=ga_proj[4])
    W = {"w_in": gathered[0].reshape(-1, D)}

    def place_cols(shard, full_cols):
        z = jnp.zeros((shard.shape[0], full_cols), f32)
        return lax.dynamic_update_slice(z, shard, (0, me * shard.shape[1]))

    zr = lambda a: jnp.zeros_like(a)
    conv_local = _pack_small(
        zr(norm1_g), zr(gate_b), place_cols(conv_w[0], D), zr(conv_b), zr(conv_norm_g), zr(q_norm_g), zr(k_norm_g),
        zr(norm2_g), place_cols(ffn_conv_w[0], 2 * D_FF), zr(ffn_conv_b))
    ga_conv = _small_start("gather_conv_start", conv_local, after=ga_ffn[4])

    bd = (jnp.arange(128)[:, None] // HEAD_DIM == jnp.arange(128)[None, :] // HEAD_DIM).astype(bf16)
    bias = _attn_bias()
    qg = jnp.tile(q_norm_g, (1, N_HEADS))
    kg = jnp.tile(k_norm_g, (1, N_HEADS))

    h = _norm1_fwd(xt, norm1_g)
    z8 = _matmul_call(
        "mm_z", h, W["w_in"],
        pl.BlockSpec((2048, D), lambda i, j, k: (i, 0)),
        pl.BlockSpec((1024, D), lambda i, j, k: (_wsec_of_zsec(j), 0)),
        pl.BlockSpec((None, 2048, D), lambda i, j, k: (j, i, 0)),
        jax.ShapeDtypeStruct((8, T, D), f32), (T // 2048, 7, 1), "nt", 1, 2048, 1024, after=ga_conv[4])
    conv_all = _unpack_small(_small_sum("gather_conv", me.reshape(1), ga_conv, z8))
    conv_w_full, ffn_w_full = conv_all["conv_w"], conv_all["ffn_conv_w"]
    c = _conv_fwd(z8, conv_w_full, conv_b, S)
    s = _convnorm_fwd(c, conv_norm_g)
    qn, kn = _qk_fwd(z8, qg, kg, bd)
    for n, g in zip(order[1:4], _exchange_wait("gather_wait_proj", ga_proj, qn)[1]):
        W[n] = g.reshape(-1, D)
    ya = _matmul("mm_ya", s, W["w_conv_out"], "nn", f32)
    o, ob, lse = _attn_fwd(qn, kn, z8, bias, S)
    yb = _matmul("mm_yb", ob, W["w_attn_out"], "nn", f32)
    mixed = _gate_fwd(z8, gate_b, ya, yb)
    x1, h2 = _out_norm2_fwd(mixed, W["w_out"], xt, norm2_g)
    for n, g in zip(order[4:6], _exchange_wait("gather_wait_ffn", ga_ffn, x1)[1]):
        W[n] = g.reshape(-1, D)
    TNU = D_FF // 2
    u3 = _matmul_call(
        "mm_u", h2, W["w_up"],
        pl.BlockSpec((1024, D), lambda i, j, k: (i, 0)),
        pl.BlockSpec((TNU, D), lambda i, j, k: (j, 0)),
        pl.BlockSpec((None, 1024, TNU), lambda i, j, k: (j // 2, i, j % 2)),
        jax.ShapeDtypeStruct((2, T, D_FF), f32), (T // 1024, 4, 1), "nt", 1, 1024, TNU)
    f = _ffn_fwd(u3, ffn_w_full, ffn_conv_b, S)
    dy, dyb, lacc = _down_loss_fwd(f, W["w_down"], x1, target)
    loss_local = 0.5 / D * jnp.sum(lacc)

    df = _matmul("mm_df", dyb, W["w_down"], "nt", f32, tn=TNU)
    g_w_down = _matmul("mm_dwdn", f, dyb, "tn", bf16, tm=TNU)
    du3, dffn = _ffn_bwd(u3, df, ffn_w_full, ffn_conv_b, S)
    g_w_up = _matmul_call(
        "mm_dwup", du3, h2,
        pl.BlockSpec((None, T, TNU), lambda i, j, k: (i // 2, 0, i % 2)),
        pl.BlockSpec((T, D), lambda i, j, k: (0, 0)),
        pl.BlockSpec((TNU, D), lambda i, j, k: (i, 0)),
        jax.ShapeDtypeStruct((2 * D_FF, D), bf16), (4, 1, 1), "tn", 1, TNU, D)
    blocks8 = lambda a: a.reshape(N_DEV, -1, D)
    ex_ffn = _exchange_start("scatter_start_ffn", [blocks8(g_w_up), blocks8(g_w_down)])
    dx1, dx1b, dg_norm2 = _up_norm2_bwd(du3, W["w_up"], x1, dy, norm2_g, ex_ffn[4])
    g_w_out = _matmul("mm_dwo", mixed, dx1b, "tn", bf16, tm=512)
    dz8 = lax.empty((8, T, D), bf16)
    dya, dyb2, dz8, dg_gate = _out_gate_bwd(dx1b, W["w_out"], z8, gate_b, ya, yb, dz8)
    ds = _matmul("mm_ds", dya, W["w_conv_out"], "nt", f32)
    g_w_conv_out = _matmul("mm_dwco", s, dya, "tn", bf16, tm=512)
    g_w_attn_out = _matmul("mm_dwao", ob, dyb2, "tn", bf16, tm=512)
    ex_proj = _exchange_start("scatter_start_proj", [blocks8(g_w_conv_out), blocks8(g_w_attn_out), blocks8(g_w_out)])
    do = _matmul("mm_do", dyb2, W["w_attn_out"], "nt", f32, after=ex_proj[4])
    dc, dg_convnorm = _convnorm_bwd(c, ds, conv_norm_g)
    dz8a, dconv = _conv_bwd(dc, z8, conv_w_full, dz8, S)
    dwin_specs = lambda zsec, wsec: (
        pl.BlockSpec((None, T, D), lambda i, j, k: (zsec(i), 0, 0)), pl.BlockSpec((T, D), lambda i, j, k: (0, 0)),
        pl.BlockSpec((1024, D), lambda i, j, k: (wsec(i), 0)), jax.ShapeDtypeStruct((7 * D, D), bf16))
    g_w_in = _matmul_call("mm_dwin_a", dz8a, h, *dwin_specs(lambda i: i, lambda i: jnp.where(i < 2, i, i + 3)),
                          (4, 1, 1), "tn", 1, D, D)
    ex_in_a = _exchange_start("scatter_start_in_a", [blocks8(g_w_in)], half=0)
    dqn, dkn, dv = _attn_bwd(qn, kn, z8, do, o, lse, bias, bd, S, ex_in_a[4])
    dz8b, dg_q, dg_k = _qk_bwd(z8, dqn, dkn, dv, qg, kg, bd, dz8a)
    g_w_in = _matmul_call("mm_dwin_b", dz8b, h, *dwin_specs(lambda i: i + 4, lambda i: i + 2),
                          (3, 1, 1), "tn", 1, D, D, fill=ex_in_a[2][0].reshape(7 * D, D))
    ex_in_b = _exchange_start("scatter_start_in_b", [blocks8(g_w_in)], ex_in_a[3], gather=False, half=1)
    grad_x, dg_norm1 = _in_norm1_bwd(dz8b, W["w_in"], xt, dx1, norm1_g, ex_in_b[4])

    sum8 = lambda a: a.reshape(-1, 8, a.shape[-1]).sum(axis=1)
    dconv_s = sum8(dconv.sum(axis=0))
    dffn_s = dffn.sum(axis=0).reshape(2, 4, 8, D_FF).sum(axis=2)
    dffn_w = jnp.concatenate([dffn_s[0, :3], dffn_s[1, :3]], axis=1)
    dffn_b = jnp.concatenate([dffn_s[0, 3:4], dffn_s[1, 3:4]], axis=1)
    fold = lambda a: sum8(a).reshape(N_HEADS, HEAD_DIM).sum(axis=0)[None]
    small_g_local = _pack_small(
        sum8(dg_norm1), sum8(dg_gate), dconv_s[:CONV_WIDTH], dconv_s[CONV_WIDTH:], sum8(dg_convnorm),
        fold(dg_q), fold(dg_k), sum8(dg_norm2), dffn_w, dffn_b,
        last_row=jnp.pad(loss_local.reshape(1, 1), ((0, 0), (0, D - 1))))
    sg_start = _small_start("small_grads_start", small_g_local)

    own, slots = {}, {}
    for tag, ex, names_ in (("ffn", ex_ffn, ("w_up", "w_down")),
                            ("proj", ex_proj, ("w_conv_out", "w_attn_out", "w_out"))):
        sent, landed = _exchange_wait("scatter_wait_" + tag, ex, sg_start[4])
        for n, src, land in zip(names_, sent, landed):
            own[n], slots[n] = src, land
    sent, landed = _exchange_wait("scatter_wait_in_a", ex_in_a[:2] + (ex_in_b[2], ex_in_b[3]) + ex_in_a[4:],
                                  sg_start[4])
    sent, landed = _exchange_wait("scatter_wait_in_b", ex_in_b[:2] + (sent, landed) + ex_in_b[4:], sg_start[4])
    own["w_in"], slots["w_in"] = sent[0], landed[0]

    res, adam_done = {}, []
    for n in order:
        w, m, v = big[n]
        outs = _adam_slots("adam_" + n, me.reshape(1), slots[n], own[n], w, m, v, _ADAM_TILE[slots[n].shape[1]],
                           transposed=n in ("w_in", "w_up"))
        adam_done.append(outs[0])
        res[n] = [a[None] for a in outs]
    small_g = _small_sum("small_grads", me.reshape(1), sg_start, adam_done)
    loss = small_g[_small_offsets()["last"], 0]

    col = lambda a, width: lax.dynamic_slice(a, (0, me * width), (a.shape[0], width))
    small_w_true = _pack_small(norm1_g, gate_b, conv_w_full, conv_b, conv_norm_g, q_norm_g, k_norm_g, norm2_g,
                               ffn_w_full, ffn_conv_b)
    place_m = lambda a, full: place_cols(a[0], full)
    small_m = _pack_small(m_norm1_g, m_gate_b, place_m(m_conv_w, D), m_conv_b, m_conv_norm_g, m_q_norm_g, m_k_norm_g,
                          m_norm2_g, place_m(m_ffn_conv_w, 2 * D_FF), m_ffn_conv_b)
    small_v = _pack_small(v_norm1_g, v_gate_b, place_m(v_conv_w, D), v_conv_b, v_conv_norm_g, v_q_norm_g, v_k_norm_g,
                          v_norm2_g, place_m(v_ffn_conv_w, 2 * D_FF), v_ffn_conv_b)
    sd, sm, sv = _adam_small(small_g, small_w_true, small_m, small_v)
    for i, packed in enumerate((small_g, sd, sm, sv)):
        u = _unpack_small(packed)
        u["conv_w"] = col(u["conv_w"], D // N_DEV)
        u["ffn_conv_w"] = col(u["ffn_conv_w"], 2 * D_FF // N_DEV)
        for n, a in u.items():
            res.setdefault(n, [None] * 4)[i] = a[None] if n in ("conv_w", "ffn_conv_w") else a

    names = ["norm1_g", "w_in", "gate_b", "conv_w", "conv_b", "conv_norm_g", "w_conv_out", "q_norm_g", "k_norm_g",
             "w_attn_out", "w_out", "norm2_g", "w_up", "ffn_conv_w", "ffn_conv_b", "w_down"]
    out = [loss, grad_x.reshape(BL, S, D)]
    for i in range(4):
        out += [res[n][i] for n in names]
    return tuple(out)
```

```python
import functools

import jax
import jax.numpy as jnp
import numpy as np
from jax import lax
from jax.experimental import pallas as pl
from jax.experimental.pallas import tpu as pltpu

f32 = jnp.float32
bf16 = jnp.bfloat16

D = 1024
N_HEADS = 16
HEAD_DIM = 64
CONV_WIDTH = 31
D_FF = 2816
GROUPS = ((128, 1), (512, 4), (2048, 16))
ATTN_BLOCK = 128
EPS = 1e-6
N_DEV = 8
MESH = pl.DeviceIdType.MESH

ADAM_LR = 0.001
ADAM_B1 = 0.9
ADAM_B2 = 0.999
ADAM_EPS = 1e-08
ADAM_WD = 0.01
ADAM_STEP = 10

VMEM_LIMIT = 56 * 1024 * 1024
MASK_BIAS = 1e30

Z_AVAL, Z_AGATE, Z_GA, Z_GB, Z_Q, Z_K, Z_V = 0, 1, 2, 3, 4, 5, 6


_W_OF_Z = (0, 1, 5, 6, 2, 3, 4)


def _wsec_of_zsec(j):
    return jnp.where(j < 2, j, jnp.where(j < 4, j + 3, j - 2))


def _zsec_of_wsec(w):
    return jnp.where(w < 2, w, jnp.where(w < 5, w + 2, w - 3))


def _sig(x):
    return 1.0 / (1.0 + jnp.exp(-x))


def _colsum8(x):
    return x.reshape(-1, 8, x.shape[-1]).sum(axis=0)


def _cparams(sem):
    return pltpu.CompilerParams(dimension_semantics=sem, vmem_limit_bytes=VMEM_LIMIT)


def _my_pos():
    x, y, c = lax.axis_index("x"), lax.axis_index("y"), lax.axis_index("c")
    return x, y, c, 4 * x + 2 * y + c


_DIMS = {"nn": ((1,), (0,)), "nt": ((1,), (1,)), "tn": ((0,), (0,))}


def _matmul_call(name, a, b, a_spec, b_spec, o_spec, out_shape, grid, mode, nk, tm, tn, after=None, fill=None):
    dims = (_DIMS[mode], ((), ()))
    extra = ([] if after is None else [after]) + ([] if fill is None else [fill])

    def body(a_ref, b_ref, *rest):
        o_ref, scratch = rest[len(extra)], rest[len(extra) + 1:]
        part = lax.dot_general(a_ref[...], b_ref[...], dims, preferred_element_type=f32)
        if nk == 1:
            o_ref[...] = part.astype(o_ref.dtype)
        else:
            acc = scratch[0]
            k = pl.program_id(2)

            @pl.when(k == 0)
            def _():
                acc[...] = part

            @pl.when(k > 0)
            def _():
                acc[...] += part

            @pl.when(k == nk - 1)
            def _():
                o_ref[...] = acc[...].astype(o_ref.dtype)

    scratch = [] if nk == 1 else [pltpu.VMEM((tm, tn), f32)]
    return pl.pallas_call(
        body, name=name, grid=grid, in_specs=[a_spec, b_spec] + [pl.BlockSpec(memory_space=pl.ANY)] * len(extra),
        out_specs=o_spec, out_shape=out_shape, input_output_aliases={} if fill is None else {1 + len(extra): 0},
        scratch_shapes=scratch, compiler_params=_cparams(("parallel", "parallel", "arbitrary")),
    )(a, b, *extra)


def _matmul(name, a, b, mode, out_dtype, tm=1024, tn=1024, tk=None, after=None):
    if mode == "nn":
        (M, K), (_, N) = a.shape, b.shape
    elif mode == "nt":
        (M, K), (N, _) = a.shape, b.shape
    else:
        (K, M), (_, N) = a.shape, b.shape
    tm, tn = min(tm, M), min(tn, N)
    tk = K if tk is None else tk
    nk = K // tk
    assert M % tm == 0 and N % tn == 0 and K % tk == 0
    if mode == "tn":
        a_spec = pl.BlockSpec((tk, tm), lambda i, j, k: (k, i))
    else:
        a_spec = pl.BlockSpec((tm, tk), lambda i, j, k: (i, k))
    if mode == "nt":
        b_spec = pl.BlockSpec((tn, tk), lambda i, j, k: (j, k))
    else:
        b_spec = pl.BlockSpec((tk, tn), lambda i, j, k: (k, j))
    o_spec = pl.BlockSpec((tm, tn), lambda i, j, k: (i, j))
    return _matmul_call(name, a, b, a_spec, b_spec, o_spec, jax.ShapeDtypeStruct((M, N), out_dtype),
                        (M // tm, N // tn, nk), mode, nk, tm, tn, after=after)


FTM = 512


def _matmul_fused(name, a, b, pairs, epilogue, extras, consts, outs, nt=False, sums=False, passed=(), aliases=None):
    sa, M, kk = a.shape
    na = max(i for i, _ in pairs) + 1
    ne, nc, npass = len(extras), len(consts), len(passed)
    dims = (_DIMS["nt" if nt else "nn"], ((), ()))

    def body(a_ref, b_ref, *rest):
        acc = None
        for i, j in pairs:
            part = lax.dot_general(a_ref[i], b_ref[j], dims, preferred_element_type=f32)
            acc = part if acc is None else acc + part
        epilogue(acc, rest[:ne], rest[ne:ne + nc], rest[ne + nc + npass:])

    whole = lambda arr: pl.BlockSpec(arr.shape, lambda i, nd=arr.ndim: (0,) * nd, pipeline_mode=pl.Buffered(1))
    io_alias = {2 + ne + nc + k: v for k, v in (aliases or {}).items()}
    return pl.pallas_call(
        body, name=name, grid=(M // FTM,),
        in_specs=[pl.BlockSpec((na, FTM, kk), lambda i: (0, i, 0)), whole(b)] + [s for _, s in extras]
        + [whole(c) for c in consts] + [pl.BlockSpec(memory_space=pl.ANY)] * npass,
        out_specs=[s for _, s in outs], out_shape=[s for s, _ in outs], input_output_aliases=io_alias,
        compiler_params=_cparams(("arbitrary" if sums else "parallel",)),
    )(a, b, *[x for x, _ in extras], *consts, *passed)


def _frows(c=D):
    return pl.BlockSpec((FTM, c), lambda i: (i, 0))


def _fsec(s):
    return pl.BlockSpec((None, FTM, D), lambda i: (s, i, 0))


def _rowshape(T, dtype, c=D):
    return (jax.ShapeDtypeStruct((T, c), dtype), _frows(c))


def _sumshape(c=D):
    return (jax.ShapeDtypeStruct((8, c), f32), pl.BlockSpec((8, c), lambda i: (0, 0)))


def _add_colsum(ref, x, cols=None):
    @pl.when(pl.program_id(0) == 0)
    def _():
        if cols is None:
            ref[...] = jnp.zeros_like(ref)
        else:
            ref[:, cols] = jnp.zeros((8, x.shape[-1]), f32)

    if cols is None:
        ref[...] += _colsum8(x)
    else:
        ref[:, cols] += _colsum8(x)


TT = 512


def _rows(c, cb=0, tt=TT):
    return pl.BlockSpec((tt, c), lambda i: (i, cb))


def _sec(s, tt=TT):
    return pl.BlockSpec((None, tt, D), lambda i: (s, i, 0))


def _const(shape):
    return pl.BlockSpec(shape, lambda i: (0,) * len(shape))


def _acc_spec(c):
    return pl.BlockSpec((8, c), lambda i: (0, 0))


def _rms(x):
    return lax.rsqrt(jnp.mean(x * x, axis=-1, keepdims=True) + EPS)


def _rms_bwd(dy_g, xn, rstd):
    return rstd * (dy_g - xn * jnp.mean(dy_g * xn, axis=-1, keepdims=True))


def _head_sum(x, bd):
    parts = []
    for cb in range(x.shape[-1] // 128):
        xb = x[:, cb * 128:(cb + 1) * 128]
        hi = xb.astype(bf16)
        lo = (xb - hi.astype(f32)).astype(bf16)
        parts.append(jnp.dot(hi, bd, preferred_element_type=f32) + jnp.dot(lo, bd, preferred_element_type=f32))
    return parts[0] if len(parts) == 1 else jnp.concatenate(parts, axis=1)


def _norm1_fwd(x, g):
    T = x.shape[0]

    def body(x_ref, g_ref, h_ref):
        xv = x_ref[...]
        h_ref[...] = (xv * _rms(xv) * g_ref[...]).astype(bf16)

    return pl.pallas_call(
        body, name="norm1_fwd", grid=(T // TT,), in_specs=[_rows(D), _const((1, D))], out_specs=_rows(D),
        out_shape=jax.ShapeDtypeStruct((T, D), bf16), compiler_params=_cparams(("parallel",)))(x, g)


def _convnorm_fwd(c, g):
    T = c.shape[0]

    def body(c_ref, g_ref, s_ref):
        cv = c_ref[...]
        r = cv * _rms(cv) * g_ref[...]
        s_ref[...] = (r * _sig(r)).astype(bf16)

    return pl.pallas_call(
        body, name="convnorm_fwd", grid=(T // TT,), in_specs=[_rows(D), _const((1, D))], out_specs=_rows(D),
        out_shape=jax.ShapeDtypeStruct((T, D), bf16), compiler_params=_cparams(("parallel",)))(c, g)


def _qk_fwd(z8, qg, kg, bd):
    T = z8.shape[1]

    def body(q_ref, k_ref, qg_ref, kg_ref, bd_ref, qn_ref, kn_ref):
        bdv = bd_ref[...]
        q = q_ref[...]
        qn_ref[...] = q * lax.rsqrt(_head_sum(q * q, bdv) * (1.0 / HEAD_DIM) + EPS) * qg_ref[...] * (HEAD_DIM ** -0.5)
        k = k_ref[...]
        kn_ref[...] = k * lax.rsqrt(_head_sum(k * k, bdv) * (1.0 / HEAD_DIM) + EPS) * kg_ref[...]

    return pl.pallas_call(
        body, name="qk_fwd", grid=(T // TT,),
        in_specs=[_sec(Z_Q), _sec(Z_K), _const((1, D)), _const((1, D)), _const((128, 128))],
        out_specs=[_rows(D), _rows(D)],
        out_shape=[jax.ShapeDtypeStruct((T, D), f32)] * 2, compiler_params=_cparams(("parallel",)))(z8, z8, qg, kg, bd)


def _gate_fwd(z8, gate_b, ya, yb):
    T = ya.shape[0]

    def body(ga_ref, gb_ref, b_ref, ya_ref, yb_ref, mixed_ref):
        g_a = _sig(ga_ref[...] + b_ref[:, :D])
        g_b = _sig(gb_ref[...] + b_ref[:, D:])
        mixed_ref[...] = (g_a * ya_ref[...] + g_b * yb_ref[...]).astype(bf16)

    return pl.pallas_call(
        body, name="gate_fwd", grid=(T // TT,),
        in_specs=[_sec(Z_GA), _sec(Z_GB), _const((1, 2 * D)), _rows(D), _rows(D)], out_specs=_rows(D),
        out_shape=jax.ShapeDtypeStruct((T, D), bf16), compiler_params=_cparams(("parallel",)))(z8, z8, gate_b, ya, yb)


def _out_norm2_fwd(mixed, w_out, x, g):
    T = x.shape[0]

    def epilogue(acc, extra, const, out):
        x1 = extra[0][...] + acc
        out[0][...] = x1
        out[1][...] = (x1 * _rms(x1) * const[0][...]).astype(bf16)

    return _matmul_fused("mm_t1_norm2", mixed[None], w_out[None], ((0, 0),), epilogue, [(x, _frows())], [g],
                         [_rowshape(T, f32), _rowshape(T, bf16)])


def _down_loss_fwd(f, w_down, x1, target):
    T = x1.shape[0]

    def epilogue(acc, extra, const, out):
        diff = extra[0][...] + acc - extra[1][...]
        dy = diff * (1.0 / D)
        out[0][...] = dy
        out[1][...] = dy.astype(bf16)
        _add_colsum(out[2], diff * diff)

    return _matmul_fused("mm_t2_loss", f[None], w_down[None], ((0, 0),), epilogue, [(x1, _frows()), (target, _frows())],
                         [], [_rowshape(T, f32), _rowshape(T, bf16), _sumshape()], sums=True)


def _up_norm2_bwd(du3, w_up_t, x1, dy, g, token):
    T = x1.shape[0]

    def epilogue(dh, extra, const, out):
        x1v = extra[0][...]
        rstd = _rms(x1v)
        xn = x1v * rstd
        dx1 = extra[1][...] + _rms_bwd(dh * const[0][...], xn, rstd)
        out[0][...] = dx1
        out[1][...] = dx1.astype(bf16)
        _add_colsum(out[2], dh * xn)

    return _matmul_fused("mm_dh2_norm2", du3, w_up_t.reshape(2, D_FF, D), ((0, 0), (1, 1)), epilogue,
                         [(x1, _frows()), (dy, _frows())], [g],
                         [_rowshape(T, f32), _rowshape(T, bf16), _sumshape()], sums=True, passed=[token])


def _out_gate_bwd(dx1b, w_out, z8, gate_b, ya, yb, dz8):
    T = ya.shape[0]

    def epilogue(dm, extra, const, out):
        b_ref = const[0]
        g_a = _sig(extra[0][...] + b_ref[:, :D])
        g_b = _sig(extra[1][...] + b_ref[:, D:])
        out[0][...] = (dm * g_a).astype(bf16)
        out[1][...] = (dm * g_b).astype(bf16)
        dla = dm * extra[2][...] * g_a * (1.0 - g_a)
        dlb = dm * extra[3][...] * g_b * (1.0 - g_b)
        out[2][0] = dla.astype(bf16)
        out[2][1] = dlb.astype(bf16)
        _add_colsum(out[3], dla, slice(0, D))
        _add_colsum(out[3], dlb, slice(D, 2 * D))

    return _matmul_fused(
        "mm_dmixed_gate", dx1b[None], w_out[None], ((0, 0),), epilogue,
        [(z8, _fsec(Z_GA)), (z8, _fsec(Z_GB)), (ya, _frows()), (yb, _frows())], [gate_b],
        [_rowshape(T, bf16), _rowshape(T, bf16),
         (jax.ShapeDtypeStruct(dz8.shape, bf16), pl.BlockSpec((2, FTM, D), lambda i: (1, i, 0))), _sumshape(2 * D)],
        nt=True, sums=True, passed=[dz8], aliases={0: 2})


def _convnorm_bwd(c, ds, g):
    T = c.shape[0]

    def body(c_ref, ds_ref, g_ref, dc_ref, dg_ref):
        cv = c_ref[...]
        rstd = _rms(cv)
        r0 = cv * rstd
        gv = g_ref[...]
        r = r0 * gv
        sg = _sig(r)
        dr = ds_ref[...] * sg * (1.0 + r * (1.0 - sg))
        dc_ref[...] = _rms_bwd(dr * gv, r0, rstd)

        @pl.when(pl.program_id(0) == 0)
        def _():
            dg_ref[...] = jnp.zeros_like(dg_ref)

        dg_ref[...] += _colsum8(dr * r0)

    return pl.pallas_call(
        body, name="convnorm_bwd", grid=(T // TT,), in_specs=[_rows(D), _rows(D), _const((1, D))],
        out_specs=[_rows(D), _acc_spec(D)],
        out_shape=[jax.ShapeDtypeStruct((T, D), f32), jax.ShapeDtypeStruct((8, D), f32)],
        compiler_params=_cparams(("arbitrary",)))(c, ds, g)


def _qk_bwd(z8, dqn, dkn, dv, qg, kg, bd, dz8):
    T = dqn.shape[0]

    def body(q_ref, k_ref, dqn_ref, dkn_ref, dv_ref, qg_ref, kg_ref, bd_ref, dz_in, dz_ref, dqg_ref, dkg_ref):
        del dz_in
        bdv = bd_ref[...]

        @pl.when(pl.program_id(0) == 0)
        def _():
            dqg_ref[...] = jnp.zeros_like(dqg_ref)
            dkg_ref[...] = jnp.zeros_like(dkg_ref)

        def one(raw, dn_scaled, g, dg_ref, sec):
            rstd = lax.rsqrt(_head_sum(raw * raw, bdv) * (1.0 / HEAD_DIM) + EPS)
            n = raw * rstd
            dg_ref[...] += _colsum8(dn_scaled * n)
            dn = dn_scaled * g
            draw = rstd * (dn - n * (_head_sum(dn * n, bdv) * (1.0 / HEAD_DIM)))
            dz_ref[sec] = draw.astype(bf16)

        one(q_ref[...], dqn_ref[...] * (HEAD_DIM ** -0.5), qg_ref[...], dqg_ref, 0)
        one(k_ref[...], dkn_ref[...], kg_ref[...], dkg_ref, 1)
        dz_ref[2] = dv_ref[...].astype(bf16)
        dz_ref[3] = jnp.zeros((TT, D), bf16)

    return pl.pallas_call(
        body, name="qk_bwd", grid=(T // TT,),
        in_specs=[_sec(Z_Q), _sec(Z_K), _rows(D), _rows(D), _rows(D), _const((1, D)), _const((1, D)),
                  _const((128, 128)), pl.BlockSpec(memory_space=pl.ANY)],
        out_specs=[pl.BlockSpec((4, TT, D), lambda i: (1, i, 0)), _acc_spec(D), _acc_spec(D)],
        out_shape=[jax.ShapeDtypeStruct(dz8.shape, bf16), jax.ShapeDtypeStruct((8, D), f32),
                   jax.ShapeDtypeStruct((8, D), f32)],
        input_output_aliases={8: 0},
        compiler_params=_cparams(("arbitrary",)))(z8, z8, dqn, dkn, dv, qg, kg, bd, dz8)


def _in_norm1_bwd(dz8, w_in_t, x, dx1, g, token):
    T = x.shape[0]

    def epilogue(dh, extra, const, out):
        xv = extra[0][...]
        rstd = _rms(xv)
        xn = xv * rstd
        out[0][...] = extra[1][...] + _rms_bwd(dh * const[0][...], xn, rstd)
        _add_colsum(out[1], dh * xn)

    return _matmul_fused("mm_dh_norm1", dz8, w_in_t.reshape(7, D, D), tuple(zip(range(7), _W_OF_Z)), epilogue,
                         [(x, _frows()), (dx1, _frows())], [g], [_rowshape(T, f32), _sumshape()],
                         sums=True, passed=[token])


CCW = 256
CR = 64
HALO = 32


def _conv_fwd(z8, conv_w, conv_b, S):
    T = z8.shape[1]
    nb = T // S
    ncb = D // CCW

    def body(av_ref, ag_ref, w_ref, b_ref, c_ref, pad):
        pad[0:HALO, :] = jnp.zeros((HALO, CCW), f32)

        def fill(i, carry):
            r0 = pl.multiple_of(i * 256, 256)
            pad[pl.ds(HALO + r0, 256), :] = av_ref[pl.ds(r0, 256), :] * _sig(ag_ref[pl.ds(r0, 256), :])
            return carry

        lax.fori_loop(0, S // 256, fill, 0)
        bias = b_ref[...]

        def chunk(i, carry):
            r0 = pl.multiple_of(i * CR, CR)
            win = pad[pl.ds(r0, CR + HALO), :]
            acc = jnp.zeros((CR, CCW), f32) + bias
            for s in range(8):
                part = None
                for m in range((CONV_WIDTH - 1 - s) // 8 + 1):
                    j = CONV_WIDTH - 1 - 8 * m - s
                    term = win[24 - 8 * m:24 - 8 * m + CR + 8, :] * w_ref[j:j + 1, :]
                    part = term if part is None else part + term
                acc = acc + part[8 - s:8 - s + CR, :]
            c_ref[pl.ds(r0, CR), :] = acc
            return carry

        lax.fori_loop(0, S // CR, chunk, 0)

    zs = lambda s: pl.BlockSpec((None, S, CCW), lambda b, cb: (s, b, cb))
    return pl.pallas_call(
        body, name="conv_fwd", grid=(nb, ncb),
        in_specs=[zs(Z_AVAL), zs(Z_AGATE), pl.BlockSpec((CONV_WIDTH, CCW), lambda b, cb: (0, cb)),
                  pl.BlockSpec((1, CCW), lambda b, cb: (0, cb))],
        out_specs=pl.BlockSpec((S, CCW), lambda b, cb: (b, cb)),
        out_shape=jax.ShapeDtypeStruct((T, D), f32),
        scratch_shapes=[pltpu.VMEM((S + HALO, CCW), f32)],
        compiler_params=_cparams(("parallel", "parallel")))(z8, z8, conv_w, conv_b)


def _conv_bwd(dc, z8, conv_w, dz8, S):
    T = dc.shape[0]
    nb = T // S
    ncb = D // CCW

    def body(dc_ref, av_ref, ag_ref, w_ref, dz_in, dz_ref, dw_ref, apad, dpad, shbuf):
        del dz_in
        apad[0:HALO, :] = jnp.zeros((HALO, CCW), f32)
        dpad[S:S + HALO, :] = jnp.zeros((HALO, CCW), f32)
        dw_ref[...] = jnp.zeros_like(dw_ref)

        def fill(i, carry):
            r0 = pl.multiple_of(i * 256, 256)
            apad[pl.ds(HALO + r0, 256), :] = av_ref[pl.ds(r0, 256), :] * _sig(ag_ref[pl.ds(r0, 256), :])
            dpad[pl.ds(r0, 256), :] = dc_ref[pl.ds(r0, 256), :]
            return carry

        lax.fori_loop(0, S // 256, fill, 0)

        def chunk(i, carry):
            r0 = pl.multiple_of(i * CR, CR)
            dwin = dpad[pl.ds(r0, CR + HALO), :]
            da = jnp.zeros((CR, CCW), f32)
            for s in range(8):
                shbuf[...] = dwin[s:s + CR, :]
                dshift = shbuf[...]
                part = None
                for m in range((CONV_WIDTH - 1 - s) // 8 + 1):
                    j = CONV_WIDTH - 1 - 8 * m - s
                    term = dwin[8 * m:8 * m + CR + 8, :] * w_ref[j:j + 1, :]
                    part = term if part is None else part + term
                    a_lag = apad[pl.ds(r0 + HALO - 8 * m, CR), :]
                    dw_ref[8 * j:8 * j + 8, :] += _colsum8(dshift * a_lag)
                da = da + part[s:s + CR, :]
            dw_ref[8 * CONV_WIDTH:8 * CONV_WIDTH + 8, :] += _colsum8(dwin[0:CR, :])
            av = av_ref[pl.ds(r0, CR), :]
            sg = _sig(ag_ref[pl.ds(r0, CR), :])
            dz_ref[0, pl.ds(r0, CR), :] = (da * sg).astype(bf16)
            dz_ref[1, pl.ds(r0, CR), :] = (da * av * sg * (1.0 - sg)).astype(bf16)
            return carry

        lax.fori_loop(0, S // CR, chunk, 0)

    zs = lambda s: pl.BlockSpec((None, S, CCW), lambda b, cb: (s, b, cb))
    return pl.pallas_call(
        body, name="conv_bwd", grid=(nb, ncb),
        in_specs=[pl.BlockSpec((S, CCW), lambda b, cb: (b, cb)), zs(Z_AVAL), zs(Z_AGATE),
                  pl.BlockSpec((CONV_WIDTH, CCW), lambda b, cb: (0, cb)), pl.BlockSpec(memory_space=pl.ANY)],
        out_specs=[pl.BlockSpec((2, S, CCW), lambda b, cb: (0, b, cb)),
                   pl.BlockSpec((None, 256, CCW), lambda b, cb: (b, 0, cb))],
        out_shape=[jax.ShapeDtypeStruct(dz8.shape, bf16), jax.ShapeDtypeStruct((nb, 256, D), f32)],
        input_output_aliases={4: 0},
        scratch_shapes=[pltpu.VMEM((S + HALO, CCW), f32), pltpu.VMEM((S + HALO, CCW), f32),
                        pltpu.VMEM((CR, CCW), f32)],
        compiler_params=_cparams(("parallel", "parallel")))(dc, z8, z8, conv_w, dz8)


FR = 128
NFB = D_FF // CCW


def _ffn_window(ref, i, r0):
    return ref[pl.ds(r0 - 8, FR + 8), :]


def _ffn_u(win, w_ref, b_ref):
    return (win[6:6 + FR, :] * w_ref[0:1, :] + win[7:7 + FR, :] * w_ref[1:2, :]
            + win[8:8 + FR, :] * w_ref[2:3, :] + b_ref[...])


def _ffn_fwd(u3, ffn_w, ffn_b, S):
    T = u3.shape[1]
    nb = T // S

    def body(uv_ref, ug_ref, wv_ref, wg_ref, bv_ref, bg_ref, f_ref):
        def chunk(first, i):
            r0 = 0 if first else pl.multiple_of(i * FR, FR)
            if first:
                z = jnp.zeros((8, CCW), f32)
                wv = jnp.concatenate([z, uv_ref[0:FR, :]], axis=0)
                wg = jnp.concatenate([z, ug_ref[0:FR, :]], axis=0)
            else:
                wv = _ffn_window(uv_ref, i, r0)
                wg = _ffn_window(ug_ref, i, r0)
            u_val = _ffn_u(wv, wv_ref, bv_ref)
            u_gate = _ffn_u(wg, wg_ref, bg_ref)
            f_ref[pl.ds(r0, FR), :] = (u_gate * _sig(u_gate) * u_val).astype(bf16)

        chunk(True, 0)

        def loop(i, carry):
            chunk(False, i)
            return carry

        lax.fori_loop(1, S // FR, loop, 0)

    us = lambda h: pl.BlockSpec((None, S, CCW), lambda b, cb: (h, b, cb))
    ws = lambda h: pl.BlockSpec((3, CCW), lambda b, cb: (0, h * NFB + cb))
    bs = lambda h: pl.BlockSpec((1, CCW), lambda b, cb: (0, h * NFB + cb))
    return pl.pallas_call(
        body, name="ffn_fwd", grid=(nb, NFB),
        in_specs=[us(0), us(1), ws(0), ws(1), bs(0), bs(1)],
        out_specs=pl.BlockSpec((S, CCW), lambda b, cb: (b, cb)),
        out_shape=jax.ShapeDtypeStruct((T, D_FF), bf16),
        compiler_params=_cparams(("parallel", "parallel")))(u3, u3, ffn_w, ffn_w, ffn_b, ffn_b)


def _ffn_bwd(u3, df, ffn_w, ffn_b, S):
    T = u3.shape[1]
    nb = T // S

    def body(uv_ref, ug_ref, df_ref, wv_ref, wg_ref, bv_ref, bg_ref, du_ref, dw_ref, dvpad, dgpad, shbuf):
        dvpad[S:S + 8, :] = jnp.zeros((8, CCW), f32)
        dgpad[S:S + 8, :] = jnp.zeros((8, CCW), f32)
        dw_ref[...] = jnp.zeros_like(dw_ref)

        def chunk(first, i):
            r0 = 0 if first else pl.multiple_of(i * FR, FR)
            if first:
                z = jnp.zeros((8, CCW), f32)
                wv = jnp.concatenate([z, uv_ref[0:FR, :]], axis=0)
                wg = jnp.concatenate([z, ug_ref[0:FR, :]], axis=0)
            else:
                wv = _ffn_window(uv_ref, i, r0)
                wg = _ffn_window(ug_ref, i, r0)
            taps = []
            for h, win in enumerate((wv, wg)):
                shbuf[2 * h] = win[6:6 + FR, :]
                shbuf[2 * h + 1] = win[7:7 + FR, :]
                taps.append((shbuf[2 * h], shbuf[2 * h + 1], win[8:8 + FR, :]))
            conv = lambda x, w_ref, b_ref: (x[0] * w_ref[0:1, :] + x[1] * w_ref[1:2, :] + x[2] * w_ref[2:3, :]
                                            + b_ref[...])
            u_val = conv(taps[0], wv_ref, bv_ref)
            u_gate = conv(taps[1], wg_ref, bg_ref)
            dfc = df_ref[pl.ds(r0, FR), :]
            sg = _sig(u_gate)
            d_val = dfc * u_gate * sg
            d_gate = dfc * u_val * sg * (1.0 + u_gate * (1.0 - sg))
            dvpad[pl.ds(r0, FR), :] = d_val
            dgpad[pl.ds(r0, FR), :] = d_gate
            for h, dd in enumerate((d_val, d_gate)):
                for j in range(3):
                    dw_ref[h, 8 * j:8 * j + 8, :] += _colsum8(dd * taps[h][j])
                dw_ref[h, 24:32, :] += _colsum8(dd)

        chunk(True, 0)

        def loop(i, carry):
            chunk(False, i)
            return carry

        lax.fori_loop(1, S // FR, loop, 0)

        def back(i, carry):
            r0 = pl.multiple_of(i * FR, FR)
            for h, (dpad, w_ref) in enumerate(((dvpad, wv_ref), (dgpad, wg_ref))):
                win = dpad[pl.ds(r0, FR + 8), :]
                du = (win[0:FR, :] * w_ref[2:3, :] + win[1:1 + FR, :] * w_ref[1:2, :]
                      + win[2:2 + FR, :] * w_ref[0:1, :])
                du_ref[h, pl.ds(r0, FR), :] = du.astype(bf16)
            return carry

        lax.fori_loop(0, S // FR, back, 0)

    us = lambda h: pl.BlockSpec((None, S, CCW), lambda b, cb: (h, b, cb))
    ws = lambda h: pl.BlockSpec((3, CCW), lambda b, cb: (0, h * NFB + cb))
    bs = lambda h: pl.BlockSpec((1, CCW), lambda b, cb: (0, h * NFB + cb))
    return pl.pallas_call(
        body, name="ffn_bwd", grid=(nb, NFB),
        in_specs=[us(0), us(1), pl.BlockSpec((S, CCW), lambda b, cb: (b, cb)), ws(0), ws(1), bs(0), bs(1)],
        out_specs=[pl.BlockSpec((2, S, CCW), lambda b, cb: (0, b, cb)),
                   pl.BlockSpec((None, 2, 32, CCW), lambda b, cb: (b, 0, 0, cb))],
        out_shape=[jax.ShapeDtypeStruct((2, T, D_FF), bf16), jax.ShapeDtypeStruct((nb, 2, 32, D_FF), f32)],
        scratch_shapes=[pltpu.VMEM((S + 8, CCW), f32), pltpu.VMEM((S + 8, CCW), f32),
                        pltpu.VMEM((4, FR, CCW), f32)],
        compiler_params=_cparams(("parallel", "parallel")))(u3, u3, df, ffn_w, ffn_w, ffn_b, ffn_b)


AB = ATTN_BLOCK


def _attn_bias_np():
    slopes = (np.float32(2.0) ** (np.float32(-8.0) * np.arange(1, N_HEADS + 1, dtype=np.float32)
                                  / np.float32(N_HEADS))).astype(np.float32)
    steps = (np.arange(AB)[:, None] + AB) - np.arange(2 * AB)[None, :]
    own = (np.arange(2 * AB) >= AB)[None, :]
    out = []
    for window, dil in GROUPS:
        valid = (steps >= 0) & (steps <= window // dil)
        dist = slopes[:, None, None] * (steps * dil).astype(np.float32)[None]
        kinds = [np.where(v[None], dist, np.float32(MASK_BIAS)) for v in (valid, valid & own)]
        out.append(np.stack(kinds, axis=1))
    return np.stack(out).astype(np.float32)


def _attn_bias():
    return jnp.asarray(_attn_bias_np())


def _head_masks():
    lane = lax.broadcasted_iota(jnp.int32, (1, 128), 1)
    return (lane < HEAD_DIM, lane >= HEAD_DIM)


def _perm_chunks(S, d):
    L = S // d
    ch = min(L, 256)
    out = []
    for r in range(d):
        for c in range(L // ch):
            start = r + d * ch * c
            out.append((pl.ds(start, ch, stride=d) if d > 1 else pl.ds(start, ch), r * L + c * ch, ch))
    return out


def _stack_heads(x, masks):
    return jnp.concatenate([jnp.where(masks[0], x, 0), jnp.where(masks[1], x, 0)], axis=0)


def _block_row(j):
    return j * AB if isinstance(j, int) else pl.multiple_of(j * AB, AB)


def _three_stages(n, stage_a, stage_b, stage_c, unroll):
    stage_a(0)
    stage_a(1)
    stage_b(0)

    def body(j, carry):
        stage_c(j - 1)
        stage_b(j)
        stage_a(j + 1)
        return carry

    lax.fori_loop(1, n - 1, body, 0, unroll=unroll)
    stage_c(n - 2)
    stage_b(n - 1)
    stage_c(n - 1)


_NT = (((1,), (1,)), ((), ()))
_TN = (((0,), (0,)), ((), ()))
SCH = 64


def _attn_fwd(qn, kn, z8, bias, S):
    T = qn.shape[0]
    nb = T // S
    nblk = S // AB

    def body(q_ref, k_ref, v_ref, bias_ref, o_ref, ob_ref, lse_ref, qs, ks, vs, s2, p2, ogp, lgp, *group_scratch):
        og, lg = group_scratch[:3], group_scratch[3:]
        masks = _head_masks()
        ks[0:AB, :] = jnp.zeros((AB, 128), bf16)
        vs[0:AB, :] = jnp.zeros((AB, 128), bf16)

        for g, (_, d) in enumerate(GROUPS):
            nsub = S // (d * AB)
            chunks = _perm_chunks(S, d)
            for src, dst, ch in chunks:
                qs[dst:dst + ch, :] = q_ref[src, :].astype(bf16)
                ks[AB + dst:AB + dst + ch, :] = k_ref[src, :].astype(bf16)
                vs[AB + dst:AB + dst + ch, :] = v_ref[src, :].astype(bf16)
            od, ld = (og[g], lg[g]) if d == 1 else (ogp, lgp)

            def scores(j):
                r0 = _block_row(j)
                q2 = _stack_heads(qs[pl.ds(r0, AB), :], masks)
                s2[j] = lax.dot_general(q2, ks[pl.ds(r0, 2 * AB), :], _NT, preferred_element_type=f32)

            def softmax(j, g=g, nsub=nsub, ld=ld):
                r0 = _block_row(j)
                kind = int(j % nsub == 0) if isinstance(j, int) else (j % nsub == 0).astype(jnp.int32)
                for cc in range(AB // SCH):
                    lses = []
                    for hh in range(2):
                        rows = pl.ds(hh * AB + cc * SCH, SCH)
                        sb = s2[j, rows, :] - bias_ref[g, hh, kind, cc * SCH:(cc + 1) * SCH, :]
                        m = jnp.max(sb, axis=-1, keepdims=True)
                        p = jnp.exp(sb - m)
                        den = jnp.sum(p, axis=-1, keepdims=True)
                        p2[j, rows, :] = (p * (1.0 / den)).astype(bf16)
                        lses.append(m + jnp.log(den))
                    ld[pl.ds(r0 + cc * SCH, SCH), :] = jnp.where(masks[0], lses[0], lses[1])

            def values(j, od=od):
                r0 = _block_row(j)
                pv2 = jnp.dot(p2[j], vs[pl.ds(r0, 2 * AB), :], preferred_element_type=f32)
                od[pl.ds(r0, AB), :] = jnp.where(masks[0], pv2[:AB], pv2[AB:])

            _three_stages(nblk, scores, softmax, values, nblk - 2)

            if d > 1:
                for src, dst, ch in chunks:
                    og[g][src, :] = ogp[dst:dst + ch, :]
                    lg[g][src, :] = lgp[dst:dst + ch, :]

        def combine(i, carry):
            rr = pl.ds(pl.multiple_of(i * 256, 256), 256)
            l0, l1, l2 = lg[0][rr, :], lg[1][rr, :], lg[2][rr, :]
            mx = jnp.maximum(jnp.maximum(l0, l1), l2)
            e0, e1, e2 = jnp.exp(l0 - mx), jnp.exp(l1 - mx), jnp.exp(l2 - mx)
            den = e0 + e1 + e2
            o = (e0 * og[0][rr, :] + e1 * og[1][rr, :] + e2 * og[2][rr, :]) / den
            o_ref[rr, :] = o
            ob_ref[rr, :] = o.astype(bf16)
            lse_ref[rr, :] = mx + jnp.log(den)
            return carry

        lax.fori_loop(0, S // 256, combine, 0)

    blk = pl.BlockSpec((S, 128), lambda b, hp: (b, hp))
    return pl.pallas_call(
        body, name="attn_fwd", grid=(nb, N_HEADS // 2),
        in_specs=[blk, blk, pl.BlockSpec((None, S, 128), lambda b, hp: (Z_V, b, hp)),
                  pl.BlockSpec((3, 2, 2, AB, 2 * AB), lambda b, hp: (0, hp, 0, 0, 0))],
        out_specs=[blk, blk, blk],
        out_shape=[jax.ShapeDtypeStruct((T, D), f32), jax.ShapeDtypeStruct((T, D), bf16),
                   jax.ShapeDtypeStruct((T, D), f32)],
        scratch_shapes=[pltpu.VMEM((S, 128), bf16), pltpu.VMEM((S + AB, 128), bf16), pltpu.VMEM((S + AB, 128), bf16),
                        pltpu.VMEM((nblk, 2 * AB, 2 * AB), f32), pltpu.VMEM((nblk, 2 * AB, 2 * AB), bf16),
                        pltpu.VMEM((S, 128), f32), pltpu.VMEM((S, 128), f32)] + [pltpu.VMEM((S, 128), f32)] * 6,
        compiler_params=_cparams(("parallel", "parallel")))(qn, kn, z8, bias)


def _attn_bwd(qn, kn, z8, do, o, lse, bias, bd, S, after):
    T = qn.shape[0]
    nb = T // S

    nblk = S // AB

    def body(q_ref, k_ref, v_ref, do_ref, o_ref, lse_ref, bias_ref, bd_ref, after_ref, dq_ref, dk_ref, dv_ref,
             delta, qs, ks, vs, dos, lsp, dlp, s2, dp2, p2, ds2, dqp, dkp, dvp):
        del after_ref
        masks = _head_masks()
        bdv = bd_ref[...]
        dq_ref[...] = jnp.zeros_like(dq_ref)
        dk_ref[...] = jnp.zeros_like(dk_ref)
        dv_ref[...] = jnp.zeros_like(dv_ref)
        ks[0:AB, :] = jnp.zeros((AB, 128), bf16)
        vs[0:AB, :] = jnp.zeros((AB, 128), bf16)

        def prep(i, carry):
            rr = pl.ds(pl.multiple_of(i * 256, 256), 256)
            delta[rr, :] = _head_sum(do_ref[rr, :] * o_ref[rr, :], bdv)
            return carry

        lax.fori_loop(0, S // 256, prep, 0, unroll=True)

        for g, (_, d) in enumerate(GROUPS):
            nsub = S // (d * AB)
            chunks = _perm_chunks(S, d)
            for src, dst, ch in chunks:
                qs[dst:dst + ch, :] = q_ref[src, :].astype(bf16)
                ks[AB + dst:AB + dst + ch, :] = k_ref[src, :].astype(bf16)
                vs[AB + dst:AB + dst + ch, :] = v_ref[src, :].astype(bf16)
                dos[dst:dst + ch, :] = do_ref[src, :].astype(bf16)
                lsp[dst:dst + ch, :] = lse_ref[src, :]
                dlp[dst:dst + ch, :] = delta[src, :]
            dkp[...] = jnp.zeros_like(dkp)
            dvp[...] = jnp.zeros_like(dvp)

            def scores(j):
                r0 = _block_row(j)
                q2 = _stack_heads(qs[pl.ds(r0, AB), :], masks)
                do2 = _stack_heads(dos[pl.ds(r0, AB), :], masks)
                s2[j] = lax.dot_general(q2, ks[pl.ds(r0, 2 * AB), :], _NT, preferred_element_type=f32)
                dp2[j] = lax.dot_general(do2, vs[pl.ds(r0, 2 * AB), :], _NT, preferred_element_type=f32)

            def probs(j, g=g, nsub=nsub):
                r0 = _block_row(j)
                kind = int(j % nsub == 0) if isinstance(j, int) else (j % nsub == 0).astype(jnp.int32)
                for cc in range(AB // SCH):
                    lse_c = lsp[pl.ds(r0 + cc * SCH, SCH), :]
                    del_c = dlp[pl.ds(r0 + cc * SCH, SCH), :]
                    for hh in range(2):
                        c0 = hh * HEAD_DIM
                        rows = pl.ds(hh * AB + cc * SCH, SCH)
                        sb = s2[j, rows, :] - bias_ref[g, hh, kind, cc * SCH:(cc + 1) * SCH, :]
                        p = jnp.exp(sb - lse_c[:, c0:c0 + 1])
                        p2[j, rows, :] = p.astype(bf16)
                        ds2[j, rows, :] = (p * (dp2[j, rows, :] - del_c[:, c0:c0 + 1])).astype(bf16)

            def grads(j):
                r0 = _block_row(j)
                q2 = _stack_heads(qs[pl.ds(r0, AB), :], masks)
                do2 = _stack_heads(dos[pl.ds(r0, AB), :], masks)
                dsb = ds2[j]
                t = jnp.dot(dsb, ks[pl.ds(r0, 2 * AB), :], preferred_element_type=f32)
                dqp[pl.ds(r0, AB), :] = jnp.where(masks[0], t[:AB], t[AB:])
                dkp[pl.ds(r0, 2 * AB), :] += lax.dot_general(dsb, q2, _TN, preferred_element_type=f32)
                dvp[pl.ds(r0, 2 * AB), :] += lax.dot_general(p2[j], do2, _TN, preferred_element_type=f32)

            _three_stages(nblk, scores, probs, grads, nblk - 2)

            for src, dst, ch in chunks:
                dq_ref[src, :] += dqp[dst:dst + ch, :]
                dk_ref[src, :] += dkp[AB + dst:AB + dst + ch, :]
                dv_ref[src, :] += dvp[AB + dst:AB + dst + ch, :]

    blk = pl.BlockSpec((S, 128), lambda b, hp: (b, hp))
    row = lambda dt, pad=0: pltpu.VMEM((S + pad, 128), dt)
    blocks = lambda dt: pltpu.VMEM((nblk, 2 * AB, 2 * AB), dt)
    return pl.pallas_call(
        body, name="attn_bwd", grid=(nb, N_HEADS // 2),
        in_specs=[blk, blk, pl.BlockSpec((None, S, 128), lambda b, hp: (Z_V, b, hp)), blk, blk, blk,
                  pl.BlockSpec((3, 2, 2, AB, 2 * AB), lambda b, hp: (0, hp, 0, 0, 0)),
                  pl.BlockSpec((128, 128), lambda b, hp: (0, 0)), pl.BlockSpec(memory_space=pl.ANY)],
        out_specs=[blk, blk, blk],
        out_shape=[jax.ShapeDtypeStruct((T, D), f32)] * 3,
        scratch_shapes=[row(f32), row(bf16), row(bf16, AB), row(bf16, AB), row(bf16), row(f32), row(f32),
                        blocks(f32), blocks(f32), blocks(bf16), blocks(bf16), row(f32), row(f32, AB), row(f32, AB)],
        compiler_params=_cparams(("parallel", "parallel")))(qn, kn, z8, do, o, lse, bias, bd, after)


def _any_spec():
    return pl.BlockSpec(memory_space=pl.ANY)


AG_CHUNKS = 4


def _allgather_rows(shards, n_full):
    n = len(shards)
    parts = [(a, q) for a in range(n_full) for q in range(AG_CHUNKS)]

    def body(*refs):
        ins, outs = refs[:n], refs[n:2 * n]
        send_sems, recv_sems, local_sems = refs[2 * n:]
        x, y, c, me = _my_pos()
        sibling = (x, y, 1 - c)
        chips = [(1 - x, y), (x, 1 - y), (1 - x, 1 - y)]

        def idx(px, py, pc):
            return 4 * px + 2 * py + pc

        def copy(v, k, blk, to, own=False):
            a, q = parts[v]
            rows = pl.ds(q * (shards[a].shape[0] // AG_CHUNKS), shards[a].shape[0] // AG_CHUNKS)
            return pltpu.make_async_remote_copy(
                src_ref=ins[a].at[rows] if own else outs[a].at[blk, rows], dst_ref=outs[a].at[blk, rows],
                send_sem=send_sems.at[v, k], recv_sem=recv_sems.at[v, k], device_id=to, device_id_type=MESH)

        mine = [pltpu.make_async_copy(ins[a], outs[a].at[me], local_sems.at[a]) for a in range(n)]
        for cp in mine:
            cp.start()
        first = []
        for v in range(len(parts)):
            first.append(copy(v, 0, me, sibling, own=True))
            first += [copy(v, 1 + j, me, (*chip, c), own=True) for j, chip in enumerate(chips)]
        for cp in first:
            cp.start()
        passed = []
        for v in range(len(parts)):
            for j, chip in enumerate(chips):
                blk = idx(*chip, c)
                copy(v, 1 + j, blk, (x, y, c)).wait_recv()
                cp = copy(v, 4 + j, blk, sibling)
                cp.start()
                passed.append(cp)
        for v in range(len(parts)):
            copy(v, 0, idx(x, y, 1 - c), (x, y, c)).wait_recv()
            for j, chip in enumerate(chips):
                copy(v, 4 + j, idx(*chip, 1 - c), (x, y, c)).wait_recv()
        for cp in first + passed:
            cp.wait_send()
        for cp in mine:
            cp.wait()

    return pl.pallas_call(
        body, name="allgather_weights",
        in_specs=[_any_spec()] * n, out_specs=[_any_spec()] * n,
        out_shape=[jax.ShapeDtypeStruct((N_DEV,) + s.shape, s.dtype) for s in shards],
        scratch_shapes=[pltpu.SemaphoreType.DMA((len(parts), 7)), pltpu.SemaphoreType.DMA((len(parts), 7)),
                        pltpu.SemaphoreType.DMA((n,))],
    )(*shards)


def _peer(x, y, c, k):
    tx = 1 - x if (k >> 2) & 1 else x
    ty = 1 - y if (k >> 1) & 1 else y
    tc = 1 - c if k & 1 else c
    return (tx, ty, tc), 4 * tx + 2 * ty + tc


_PEER_ORDER = (2, 4, 6, 3, 5, 7, 1)


_HBM = pl.BlockSpec(memory_space=pltpu.HBM)
_SEM = pl.BlockSpec(memory_space=pltpu.SEMAPHORE)
_EFFECT = pltpu.SideEffectType.DATAFLOW_SIDE_EFFECTING


def _exchange_copies(srcs, lands, send_sems, recv_sems, gather, half):
    x, y, c, me = _my_pos()
    pick = lambda px, py: None if half is None else ((px == py) if half == 0 else (px != py))
    copies = []
    for k in _PEER_ORDER:
        tgt, tidx = _peer(x, y, c, k)
        for a in range(len(srcs)):
            copies.append((pltpu.make_async_remote_copy(
                src_ref=srcs[a] if gather else srcs[a].at[tidx], dst_ref=lands[a].at[me],
                send_sem=send_sems.at[7 * a + k - 1], recv_sem=recv_sems.at[7 * a + k - 1],
                device_id=tgt, device_id_type=MESH), pick(tgt[0], tgt[1])))
    return copies, pick(x, y)


def _when(cond, fn):
    if cond is None:
        fn()
    else:
        pl.when(cond)(fn)


def _exchange_start(name, srcs, lands=None, after=None, gather=None, half=None):
    n = len(srcs)
    gather = (lands is not None) if gather is None else gather
    if lands is None:
        lands = [lax.empty(g.shape, g.dtype) for g in srcs]
    extra = [] if after is None else [after]

    def body(*refs):
        src_refs, land_refs = refs[:n], refs[n:2 * n]
        send_sems, recv_sems = refs[2 * n + len(extra)], refs[2 * n + len(extra) + 1]
        token = refs[-1]
        for cp, sends in _exchange_copies(src_refs, land_refs, send_sems, recv_sems, gather, half)[0]:
            _when(sends, cp.start)
        token[...] = jnp.zeros_like(token)

    hbm = lambda a: pltpu.with_memory_space_constraint(a, pltpu.HBM)
    outs = pl.pallas_call(
        body, name=name,
        out_shape=(pltpu.SemaphoreType.DMA((7 * n,)), pltpu.SemaphoreType.DMA((7 * n,)),
                   *[pltpu.HBM(g.shape, g.dtype) for g in list(srcs) + list(lands)],
                   jax.ShapeDtypeStruct((8, 128), f32)),
        in_specs=[_HBM] * (2 * n) + [pl.BlockSpec(memory_space=pl.ANY)] * len(extra),
        out_specs=(_SEM, _SEM, *([_HBM] * (2 * n)), pl.BlockSpec(memory_space=pltpu.VMEM)),
        input_output_aliases={i: 2 + i for i in range(2 * n)},
        compiler_params=pltpu.CompilerParams(has_side_effects=_EFFECT),
    )(*[hbm(g) for g in srcs], *[hbm(g) for g in lands], *extra)
    return outs[0], outs[1], list(outs[2:2 + n]), list(outs[2 + n:2 + 2 * n]), outs[-1], gather, half


def _exchange_wait(name, started, after):
    send_sems, recv_sems, srcs, lands, _, gather, half = started
    n = len(srcs)
    after = list(after) if isinstance(after, (list, tuple)) else [after]

    def body(*refs):
        src_refs, land_refs = refs[:n], refs[n:2 * n]
        s_sems, r_sems = refs[2 * n], refs[2 * n + 1]
        copies, receives = _exchange_copies(src_refs, land_refs, s_sems, r_sems, gather, half)
        for cp, sends in copies:
            _when(sends, cp.wait_send)
            _when(receives, cp.wait_recv)

    outs = pl.pallas_call(
        body, name=name,
        out_shape=tuple(pltpu.HBM(a.shape, a.dtype) for a in list(srcs) + list(lands)),
        in_specs=[_HBM] * (2 * n) + [_SEM, _SEM] + [pl.BlockSpec(memory_space=pl.ANY)] * len(after),
        out_specs=tuple([_HBM] * (2 * n)),
        input_output_aliases={i: i for i in range(2 * n)},
        compiler_params=pltpu.CompilerParams(has_side_effects=_EFFECT),
    )(*srcs, *lands, send_sems, recv_sems, *after)
    return list(outs[:n]), list(outs[n:])


SMALL_ROWS = 128


def _small_start(name, sg, after=None):
    return _exchange_start(name, [sg], [lax.empty((N_DEV,) + sg.shape, f32)], after=after)


def _small_sum(name, me, started, after):
    (own,), (slots,) = _exchange_wait(name + "_wait", started, after)

    def body(me_ref, s_ref, own_ref, out_ref):
        acc = None
        for p in range(N_DEV):
            term = lax.cond(me_ref[0] == p, lambda: own_ref[...], lambda p=p: s_ref[p])
            acc = term if acc is None else acc + term
        out_ref[...] = acc

    return pl.pallas_call(
        body, name=name + "_sum",
        in_specs=[pl.BlockSpec(memory_space=pltpu.SMEM), pl.BlockSpec(memory_space=pltpu.VMEM),
                  pl.BlockSpec(memory_space=pltpu.VMEM)],
        out_specs=pl.BlockSpec(memory_space=pltpu.VMEM),
        out_shape=jax.ShapeDtypeStruct(own.shape, f32))(me, slots, own)


def _adam_math(g, w, m, v):
    m = ADAM_B1 * m + (1.0 - ADAM_B1) * g
    v = ADAM_B2 * v + (1.0 - ADAM_B2) * (g * g)
    m_hat = m / (1.0 - ADAM_B1 ** ADAM_STEP)
    v_hat = v / (1.0 - ADAM_B2 ** ADAM_STEP)
    delta = -ADAM_LR * (m_hat / (jnp.sqrt(v_hat) + ADAM_EPS) + ADAM_WD * w)
    return delta, m, v


def _adam_slots(name, me, slots, own, w, m, v, tr, transposed=False):
    rows = slots.shape[1]

    def body(me_ref, s_ref, own_ref, w_ref, m_ref, v_ref, g_ref, d_ref, nm_ref, nv_ref):
        mine = own_ref[...]
        g = None
        for p in range(N_DEV):
            term = lax.cond(me_ref[0] == p, lambda: mine, lambda p=p: s_ref[p]).astype(f32)
            g = term if g is None else g + term
        if transposed:
            g = g.T
        delta, nm, nv = _adam_math(g, w_ref[...], m_ref[...], v_ref[...])
        g_ref[...] = g
        d_ref[...] = delta
        nm_ref[...] = nm
        nv_ref[...] = nv

    mode = dict(pipeline_mode=pl.Buffered(1)) if rows == tr else {}
    if transposed:
        rs = pl.BlockSpec((D, tr), lambda i, me_ref: (0, i))
        rs_in = pl.BlockSpec((D, tr), lambda i, me_ref: (0, i), **mode)
    else:
        rs = pl.BlockSpec((tr, D), lambda i, me_ref: (i, 0))
        rs_in = pl.BlockSpec((tr, D), lambda i, me_ref: (i, 0), **mode)
    return pl.pallas_call(
        body, name=name,
        grid_spec=pltpu.PrefetchScalarGridSpec(
            num_scalar_prefetch=1, grid=(rows // tr,),
            in_specs=[pl.BlockSpec((N_DEV, tr, D), lambda i, me_ref: (0, i, 0), **mode),
                      pl.BlockSpec((None, tr, D), lambda i, me_ref: (me_ref[0], i, 0), **mode), rs_in, rs_in, rs_in],
            out_specs=[rs] * 4),
        out_shape=[jax.ShapeDtypeStruct(w.shape, f32)] * 4,
        compiler_params=_cparams(("parallel",)))(me, slots, own, w, m, v)


def _adam_small(g, w, m, v):
    def body(g_ref, w_ref, m_ref, v_ref, d_ref, nm_ref, nv_ref):
        delta, nm, nv = _adam_math(g_ref[...], w_ref[...], m_ref[...], v_ref[...])
        d_ref[...] = delta
        nm_ref[...] = nm
        nv_ref[...] = nv

    return pl.pallas_call(body, name="adam_small", out_shape=[jax.ShapeDtypeStruct(g.shape, f32)] * 3)(g, w, m, v)


FFN_PAD = 6 * D


_SMALL_PARTS = (("norm1_g", 1), ("gate_b", 2), ("conv_w", CONV_WIDTH), ("conv_b", 1), ("conv_norm_g", 1),
                ("q_norm_g", 1), ("k_norm_g", 1), ("norm2_g", 1), ("ffn_conv_w", 18), ("ffn_conv_b", 6), ("last", 1))


def _small_offsets():
    out, row = {}, 0
    for name, rows in _SMALL_PARTS:
        out[name] = row
        row += -(-rows // 8) * 8
    assert row == SMALL_ROWS
    return out


def _pack_small(norm1_g, gate_b, conv_w, conv_b, conv_norm_g, q_norm_g, k_norm_g, norm2_g, ffn_conv_w, ffn_conv_b,
                last_row=None):
    pad_h = lambda a: jnp.pad(a, ((0, 0), (0, D - HEAD_DIM)))
    pad_f = lambda a: jnp.pad(a, ((0, 0), (0, FFN_PAD - 2 * D_FF))).reshape(-1, D)
    parts = [norm1_g, gate_b.reshape(2, D), conv_w, conv_b, conv_norm_g, pad_h(q_norm_g), pad_h(k_norm_g), norm2_g,
             pad_f(ffn_conv_w), pad_f(ffn_conv_b), jnp.zeros((1, D), f32) if last_row is None else last_row]
    return jnp.concatenate([jnp.pad(p, ((0, -p.shape[0] % 8), (0, 0))) for p in parts], axis=0)


def _unpack_small(p):
    o = _small_offsets()
    rows = lambda name, n: p[o[name]:o[name] + n]
    ffn = lambda a: a.reshape(-1, FFN_PAD)[:, :2 * D_FF]
    return dict(
        norm1_g=rows("norm1_g", 1), gate_b=rows("gate_b", 2).reshape(1, 2 * D), conv_w=rows("conv_w", CONV_WIDTH),
        conv_b=rows("conv_b", 1), conv_norm_g=rows("conv_norm_g", 1), q_norm_g=rows("q_norm_g", 1)[:, :HEAD_DIM],
        k_norm_g=rows("k_norm_g", 1)[:, :HEAD_DIM], norm2_g=rows("norm2_g", 1),
        ffn_conv_w=ffn(rows("ffn_conv_w", 18)), ffn_conv_b=ffn(rows("ffn_conv_b", 6)))


_ADAM_TILE = {896: 128, 704: 704, 128: 128, 352: 176}


def kernel(x, norm1_g, w_in, gate_b, conv_w, conv_b, conv_norm_g, w_conv_out, q_norm_g, k_norm_g, w_attn_out, w_out, norm2_g, w_up, ffn_conv_w, ffn_conv_b, w_down, loss_target, m_norm1_g, m_w_in, m_gate_b, m_conv_w, m_conv_b, m_conv_norm_g, m_w_conv_out, m_q_norm_g, m_k_norm_g, m_w_attn_out, m_w_out, m_norm2_g, m_w_up, m_ffn_conv_w, m_ffn_conv_b, m_w_down, v_norm1_g, v_w_in, v_gate_b, v_conv_w, v_conv_b, v_conv_norm_g, v_w_conv_out, v_q_norm_g, v_k_norm_g, v_w_attn_out, v_w_out, v_norm2_g, v_w_up, v_ffn_conv_w, v_ffn_conv_b, v_w_down):
    BL, S, _ = x.shape
    T = BL * S
    me = 4 * lax.axis_index("x") + 2 * lax.axis_index("y") + lax.axis_index("c")
    xt = x.reshape(T, D)
    target = loss_target.reshape(T, D)

    big = dict(w_in=(w_in[0], m_w_in[0], v_w_in[0]), w_up=(w_up[0], m_w_up[0], v_w_up[0]),
               w_conv_out=(w_conv_out[0], m_w_conv_out[0], v_w_conv_out[0]),
               w_attn_out=(w_attn_out[0], m_w_attn_out[0], v_w_attn_out[0]),
               w_out=(w_out[0], m_w_out[0], v_w_out[0]), w_down=(w_down[0], m_w_down[0], v_w_down[0]))
    order = ["w_in", "w_conv_out", "w_attn_out", "w_out", "w_up", "w_down"]
    shards = [(big[n][0].T if n in ("w_in", "w_up") else big[n][0]).astype(bf16) for n in order]
    gathered = _allgather_rows(shards, 1)
    ga_proj = _exchange_start("gather_start_proj", shards[1:4], gathered[1:4], after=gathered[0])
    ga_ffn = _exchange_start("gather_start_ffn", shards[4:6], gathered[4:6], after=ga_proj[4])
    W = {"w_in": gathered[0].reshape(-1, D)}

    def place_cols(shard, full_cols):
        z = jnp.zeros((shard.shape[0], full_cols), f32)
        return lax.dynamic_update_slice(z, shard, (0, me * shard.shape[1]))

    zr = lambda a: jnp.zeros_like(a)
    conv_local = _pack_small(
        zr(norm1_g), zr(gate_b), place_cols(conv_w[0], D), zr(conv_b), zr(conv_norm_g), zr(q_norm_g), zr(k_norm_g),
        zr(norm2_g), place_cols(ffn_conv_w[0], 2 * D_FF), zr(ffn_conv_b))
    ga_conv = _small_start("gather_conv_start", conv_local, after=ga_ffn[4])

    bd = (jnp.arange(128)[:, None] // HEAD_DIM == jnp.arange(128)[None, :] // HEAD_DIM).astype(bf16)
    bias = _attn_bias()
    qg = jnp.tile(q_norm_g, (1, N_HEADS))
    kg = jnp.tile(k_norm_g, (1, N_HEADS))

    h = _norm1_fwd(xt, norm1_g)
    z8 = _matmul_call(
        "mm_z", h, W["w_in"],
        pl.BlockSpec((2048, D), lambda i, j, k: (i, 0)),
        pl.BlockSpec((1024, D), lambda i, j, k: (_wsec_of_zsec(j), 0)),
        pl.BlockSpec((None, 2048, D), lambda i, j, k: (j, i, 0)),
        jax.ShapeDtypeStruct((8, T, D), f32), (T // 2048, 7, 1), "nt", 1, 2048, 1024, after=ga_conv[4])
    conv_all = _unpack_small(_small_sum("gather_conv", me.reshape(1), ga_conv, z8))
    conv_w_full, ffn_w_full = conv_all["conv_w"], conv_all["ffn_conv_w"]
    c = _conv_fwd(z8, conv_w_full, conv_b, S)
    s = _convnorm_fwd(c, conv_norm_g)
    qn, kn = _qk_fwd(z8, qg, kg, bd)
    for n, g in zip(order[1:4], _exchange_wait("gather_wait_proj", ga_proj, qn)[1]):
        W[n] = g.reshape(-1, D)
    ya = _matmul("mm_ya", s, W["w_conv_out"], "nn", f32)
    o, ob, lse = _attn_fwd(qn, kn, z8, bias, S)
    yb = _matmul("mm_yb", ob, W["w_attn_out"], "nn", f32)
    mixed = _gate_fwd(z8, gate_b, ya, yb)
    x1, h2 = _out_norm2_fwd(mixed, W["w_out"], xt, norm2_g)
    for n, g in zip(order[4:6], _exchange_wait("gather_wait_ffn", ga_ffn, x1)[1]):
        W[n] = g.reshape(-1, D)
    TNU = D_FF // 2
    u3 = _matmul_call(
        "mm_u", h2, W["w_up"],
        pl.BlockSpec((1024, D), lambda i, j, k: (i, 0)),
        pl.BlockSpec((TNU, D), lambda i, j, k: (j, 0)),
        pl.BlockSpec((None, 1024, TNU), lambda i, j, k: (j // 2, i, j % 2)),
        jax.ShapeDtypeStruct((2, T, D_FF), f32), (T // 1024, 4, 1), "nt", 1, 1024, TNU)
    f = _ffn_fwd(u3, ffn_w_full, ffn_conv_b, S)
    dy, dyb, lacc = _down_loss_fwd(f, W["w_down"], x1, target)
    loss_local = 0.5 / D * jnp.sum(lacc)

    df = _matmul("mm_df", dyb, W["w_down"], "nt", f32, tn=TNU)
    g_w_down = _matmul("mm_dwdn", f, dyb, "tn", bf16, tm=TNU)
    du3, dffn = _ffn_bwd(u3, df, ffn_w_full, ffn_conv_b, S)
    g_w_up = _matmul_call(
        "mm_dwup", du3, h2,
        pl.BlockSpec((None, T, TNU), lambda i, j, k: (i // 2, 0, i % 2)),
        pl.BlockSpec((T, D), lambda i, j, k: (0, 0)),
        pl.BlockSpec((TNU, D), lambda i, j, k: (i, 0)),
        jax.ShapeDtypeStruct((2 * D_FF, D), bf16), (4, 1, 1), "tn", 1, TNU, D)
    blocks8 = lambda a: a.reshape(N_DEV, -1, D)
    ex_ffn = _exchange_start("scatter_start_ffn", [blocks8(g_w_up), blocks8(g_w_down)])
    dx1, dx1b, dg_norm2 = _up_norm2_bwd(du3, W["w_up"], x1, dy, norm2_g, ex_ffn[4])
    g_w_out = _matmul("mm_dwo", mixed, dx1b, "tn", bf16, tm=512)
    dz8 = lax.empty((8, T, D), bf16)
    dya, dyb2, dz8, dg_gate = _out_gate_bwd(dx1b, W["w_out"], z8, gate_b, ya, yb, dz8)
    ds = _matmul("mm_ds", dya, W["w_conv_out"], "nt", f32)
    g_w_conv_out = _matmul("mm_dwco", s, dya, "tn", bf16, tm=512)
    g_w_attn_out = _matmul("mm_dwao", ob, dyb2, "tn", bf16, tm=512)
    ex_proj = _exchange_start("scatter_start_proj", [blocks8(g_w_conv_out), blocks8(g_w_attn_out), blocks8(g_w_out)])
    do = _matmul("mm_do", dyb2, W["w_attn_out"], "nt", f32, after=ex_proj[4])
    dc, dg_convnorm = _convnorm_bwd(c, ds, conv_norm_g)
    dz8a, dconv = _conv_bwd(dc, z8, conv_w_full, dz8, S)
    dwin_specs = lambda zsec, wsec: (
        pl.BlockSpec((None, T, D), lambda i, j, k: (zsec(i), 0, 0)), pl.BlockSpec((T, D), lambda i, j, k: (0, 0)),
        pl.BlockSpec((1024, D), lambda i, j, k: (wsec(i), 0)), jax.ShapeDtypeStruct((7 * D, D), bf16))
    g_w_in = _matmul_call("mm_dwin_a", dz8a, h, *dwin_specs(lambda i: i, lambda i: jnp.where(i < 2, i, i + 3)),
                          (4, 1, 1), "tn", 1, D, D)
    ex_in_a = _exchange_start("scatter_start_in_a", [blocks8(g_w_in)], half=0)
    dqn, dkn, dv = _attn_bwd(qn, kn, z8, do, o, lse, bias, bd, S, ex_in_a[4])
    dz8b, dg_q, dg_k = _qk_bwd(z8, dqn, dkn, dv, qg, kg, bd, dz8a)
    g_w_in = _matmul_call("mm_dwin_b", dz8b, h, *dwin_specs(lambda i: i + 4, lambda i: i + 2),
                          (3, 1, 1), "tn", 1, D, D, fill=ex_in_a[2][0].reshape(7 * D, D))
    ex_in_b = _exchange_start("scatter_start_in_b", [blocks8(g_w_in)], ex_in_a[3], gather=False, half=1)
    grad_x, dg_norm1 = _in_norm1_bwd(dz8b, W["w_in"], xt, dx1, norm1_g, ex_in_b[4])

    sum8 = lambda a: a.reshape(-1, 8, a.shape[-1]).sum(axis=1)
    dconv_s = sum8(dconv.sum(axis=0))
    dffn_s = dffn.sum(axis=0).reshape(2, 4, 8, D_FF).sum(axis=2)
    dffn_w = jnp.concatenate([dffn_s[0, :3], dffn_s[1, :3]], axis=1)
    dffn_b = jnp.concatenate([dffn_s[0, 3:4], dffn_s[1, 3:4]], axis=1)
    fold = lambda a: sum8(a).reshape(N_HEADS, HEAD_DIM).sum(axis=0)[None]
    small_g_local = _pack_small(
        sum8(dg_norm1), sum8(dg_gate), dconv_s[:CONV_WIDTH], dconv_s[CONV_WIDTH:], sum8(dg_convnorm),
        fold(dg_q), fold(dg_k), sum8(dg_norm2), dffn_w, dffn_b,
        last_row=jnp.pad(loss_local.reshape(1, 1), ((0, 0), (0, D - 1))))
    sg_start = _small_start("small_grads_start", small_g_local)

    own, slots = {}, {}
    for tag, ex, names_ in (("ffn", ex_ffn, ("w_up", "w_down")),
                            ("proj", ex_proj, ("w_conv_out", "w_attn_out", "w_out"))):
        sent, landed = _exchange_wait("scatter_wait_" + tag, ex, sg_start[4])
        for n, src, land in zip(names_, sent, landed):
            own[n], slots[n] = src, land
    sent, landed = _exchange_wait("scatter_wait_in_a", ex_in_a[:2] + (ex_in_b[2], ex_in_b[3]) + ex_in_a[4:],
                                  sg_start[4])
    sent, landed = _exchange_wait("scatter_wait_in_b", ex_in_b[:2] + (sent, landed) + ex_in_b[4:], sg_start[4])
    own["w_in"], slots["w_in"] = sent[0], landed[0]

    res, adam_done = {}, []
    for n in order:
        w, m, v = big[n]
        outs = _adam_slots("adam_" + n, me.reshape(1), slots[n], own[n], w, m, v, _ADAM_TILE[slots[n].shape[1]],
                           transposed=n in ("w_in", "w_up"))
        adam_done.append(outs[0])
        res[n] = [a[None] for a in outs]
    small_g = _small_sum("small_grads", me.reshape(1), sg_start, adam_done)
    loss = small_g[_small_offsets()["last"], 0]

    col = lambda a, width: lax.dynamic_slice(a, (0, me * width), (a.shape[0], width))
    small_w_true = _pack_small(norm1_g, gate_b, conv_w_full, conv_b, conv_norm_g, q_norm_g, k_norm_g, norm2_g,
                               ffn_w_full, ffn_conv_b)
    place_m = lambda a, full: place_cols(a[0], full)
    small_m = _pack_small(m_norm1_g, m_gate_b, place_m(m_conv_w, D), m_conv_b, m_conv_norm_g, m_q_norm_g, m_k_norm_g,
                          m_norm2_g, place_m(m_ffn_conv_w, 2 * D_FF), m_ffn_conv_b)
    small_v = _pack_small(v_norm1_g, v_gate_b, place_m(v_conv_w, D), v_conv_b, v_conv_norm_g, v_q_norm_g, v_k_norm_g,
                          v_norm2_g, place_m(v_ffn_conv_w, 2 * D_FF), v_ffn_conv_b)
    sd, sm, sv = _adam_small(small_g, small_w_true, small_m, small_v)
    for i, packed in enumerate((small_g, sd, sm, sv)):
        u = _unpack_small(packed)
        u["conv_w"] = col(u["conv_w"], D // N_DEV)
        u["ffn_conv_w"] = col(u["ffn_conv_w"], 2 * D_FF // N_DEV)
        for n, a in u.items():
            res.setdefault(n, [None] * 4)[i] = a[None] if n in ("conv_w", "ffn_conv_w") else a

    names = ["norm1_g", "w_in", "gate_b", "conv_w", "conv_b", "conv_norm_g", "w_conv_out", "q_norm_g", "k_norm_g",
             "w_attn_out", "w_out", "norm2_g", "w_up", "ffn_conv_w", "ffn_conv_b", "w_down"]
    out = [loss, grad_x.reshape(BL, S, D)]
    for i in range(4):
        out += [res[n][i] for n in names]
    return tuple(out)
```

```python
import functools

import jax
import jax.numpy as jnp
import numpy as np
from jax import lax
from jax.experimental import pallas as pl
from jax.experimental.pallas import tpu as pltpu

f32 = jnp.float32
bf16 = jnp.bfloat16

D = 1024
N_HEADS = 16
HEAD_DIM = 64
CONV_WIDTH = 31
D_FF = 2816
GROUPS = ((128, 1), (512, 4), (2048, 16))
ATTN_BLOCK = 128
EPS = 1e-6
N_DEV = 8
MESH = pl.DeviceIdType.MESH

ADAM_LR = 0.001
ADAM_B1 = 0.9
ADAM_B2 = 0.999
ADAM_EPS = 1e-08
ADAM_WD = 0.01
ADAM_STEP = 10

VMEM_LIMIT = 56 * 1024 * 1024
MASK_BIAS = 1e30

Z_AVAL, Z_AGATE, Z_GA, Z_GB, Z_Q, Z_K, Z_V = 0, 1, 2, 3, 4, 5, 6


_W_OF_Z = (0, 1, 5, 6, 2, 3, 4)


def _wsec_of_zsec(j):
    return jnp.where(j < 2, j, jnp.where(j < 4, j + 3, j - 2))


def _zsec_of_wsec(w):
    return jnp.where(w < 2, w, jnp.where(w < 5, w + 2, w - 3))


def _sig(x):
    return 1.0 / (1.0 + jnp.exp(-x))


def _colsum8(x):
    return x.reshape(-1, 8, x.shape[-1]).sum(axis=0)


def _cparams(sem):
    return pltpu.CompilerParams(dimension_semantics=sem, vmem_limit_bytes=VMEM_LIMIT)


def _my_pos():
    x, y, c = lax.axis_index("x"), lax.axis_index("y"), lax.axis_index("c")
    return x, y, c, 4 * x + 2 * y + c


_DIMS = {"nn": ((1,), (0,)), "nt": ((1,), (1,)), "tn": ((0,), (0,))}


def _matmul_call(name, a, b, a_spec, b_spec, o_spec, out_shape, grid, mode, nk, tm, tn, after=None, fill=None):
    dims = (_DIMS[mode], ((), ()))
    extra = ([] if after is None else [after]) + ([] if fill is None else [fill])

    def body(a_ref, b_ref, *rest):
        o_ref, scratch = rest[len(extra)], rest[len(extra) + 1:]
        part = lax.dot_general(a_ref[...], b_ref[...], dims, preferred_element_type=f32)
        if nk == 1:
            o_ref[...] = part.astype(o_ref.dtype)
        else:
            acc = scratch[0]
            k = pl.program_id(2)

            @pl.when(k == 0)
            def _():
                acc[...] = part

            @pl.when(k > 0)
            def _():
                acc[...] += part

            @pl.when(k == nk - 1)
            def _():
                o_ref[...] = acc[...].astype(o_ref.dtype)

    scratch = [] if nk == 1 else [pltpu.VMEM((tm, tn), f32)]
    return pl.pallas_call(
        body, name=name, grid=grid, in_specs=[a_spec, b_spec] + [pl.BlockSpec(memory_space=pl.ANY)] * len(extra),
        out_specs=o_spec, out_shape=out_shape, input_output_aliases={} if fill is None else {1 + len(extra): 0},
        scratch_shapes=scratch, compiler_params=_cparams(("parallel", "parallel", "arbitrary")),
    )(a, b, *extra)


def _matmul(name, a, b, mode, out_dtype, tm=1024, tn=1024, tk=None, after=None):
    if mode == "nn":
        (M, K), (_, N) = a.shape, b.shape
    elif mode == "nt":
        (M, K), (N, _) = a.shape, b.shape
    else:
        (K, M), (_, N) = a.shape, b.shape
    tm, tn = min(tm, M), min(tn, N)
    tk = K if tk is None else tk
    nk = K // tk
    assert M % tm == 0 and N % tn == 0 and K % tk == 0
    if mode == "tn":
        a_spec = pl.BlockSpec((tk, tm), lambda i, j, k: (k, i))
    else:
        a_spec = pl.BlockSpec((tm, tk), lambda i, j, k: (i, k))
    if mode == "nt":
        b_spec = pl.BlockSpec((tn, tk), lambda i, j, k: (j, k))
    else:
        b_spec = pl.BlockSpec((tk, tn), lambda i, j, k: (k, j))
    o_spec = pl.BlockSpec((tm, tn), lambda i, j, k: (i, j))
    return _matmul_call(name, a, b, a_spec, b_spec, o_spec, jax.ShapeDtypeStruct((M, N), out_dtype),
                        (M // tm, N // tn, nk), mode, nk, tm, tn, after=after)


FTM = 512


def _matmul_fused(name, a, b, pairs, epilogue, extras, consts, outs, nt=False, sums=False, passed=(), aliases=None):
    sa, M, kk = a.shape
    na = max(i for i, _ in pairs) + 1
    ne, nc, npass = len(extras), len(consts), len(passed)
    dims = (_DIMS["nt" if nt else "nn"], ((), ()))

    def body(a_ref, b_ref, *rest):
        acc = None
        for i, j in pairs:
            part = lax.dot_general(a_ref[i], b_ref[j], dims, preferred_element_type=f32)
            acc = part if acc is None else acc + part
        epilogue(acc, rest[:ne], rest[ne:ne + nc], rest[ne + nc + npass:])

    whole = lambda arr: pl.BlockSpec(arr.shape, lambda i, nd=arr.ndim: (0,) * nd, pipeline_mode=pl.Buffered(1))
    io_alias = {2 + ne + nc + k: v for k, v in (aliases or {}).items()}
    return pl.pallas_call(
        body, name=name, grid=(M // FTM,),
        in_specs=[pl.BlockSpec((na, FTM, kk), lambda i: (0, i, 0)), whole(b)] + [s for _, s in extras]
        + [whole(c) for c in consts] + [pl.BlockSpec(memory_space=pl.ANY)] * npass,
        out_specs=[s for _, s in outs], out_shape=[s for s, _ in outs], input_output_aliases=io_alias,
        compiler_params=_cparams(("arbitrary" if sums else "parallel",)),
    )(a, b, *[x for x, _ in extras], *consts, *passed)


def _frows(c=D):
    return pl.BlockSpec((FTM, c), lambda i: (i, 0))


def _fsec(s):
    return pl.BlockSpec((None, FTM, D), lambda i: (s, i, 0))


def _rowshape(T, dtype, c=D):
    return (jax.ShapeDtypeStruct((T, c), dtype), _frows(c))


def _sumshape(c=D):
    return (jax.ShapeDtypeStruct((8, c), f32), pl.BlockSpec((8, c), lambda i: (0, 0)))


def _add_colsum(ref, x, cols=None):
    @pl.when(pl.program_id(0) == 0)
    def _():
        if cols is None:
            ref[...] = jnp.zeros_like(ref)
        else:
            ref[:, cols] = jnp.zeros((8, x.shape[-1]), f32)

    if cols is None:
        ref[...] += _colsum8(x)
    else:
        ref[:, cols] += _colsum8(x)


TT = 512


def _rows(c, cb=0, tt=TT):
    return pl.BlockSpec((tt, c), lambda i: (i, cb))


def _sec(s, tt=TT):
    return pl.BlockSpec((None, tt, D), lambda i: (s, i, 0))


def _const(shape):
    return pl.BlockSpec(shape, lambda i: (0,) * len(shape))


def _acc_spec(c):
    return pl.BlockSpec((8, c), lambda i: (0, 0))


def _rms(x):
    return lax.rsqrt(jnp.mean(x * x, axis=-1, keepdims=True) + EPS)


def _rms_bwd(dy_g, xn, rstd):
    return rstd * (dy_g - xn * jnp.mean(dy_g * xn, axis=-1, keepdims=True))


def _head_sum(x, bd):
    parts = []
    for cb in range(x.shape[-1] // 128):
        xb = x[:, cb * 128:(cb + 1) * 128]
        hi = xb.astype(bf16)
        lo = (xb - hi.astype(f32)).astype(bf16)
        parts.append(jnp.dot(hi, bd, preferred_element_type=f32) + jnp.dot(lo, bd, preferred_element_type=f32))
    return parts[0] if len(parts) == 1 else jnp.concatenate(parts, axis=1)


def _norm1_fwd(x, g):
    T = x.shape[0]

    def body(x_ref, g_ref, h_ref):
        xv = x_ref[...]
        h_ref[...] = (xv * _rms(xv) * g_ref[...]).astype(bf16)

    return pl.pallas_call(
        body, name="norm1_fwd", grid=(T // TT,), in_specs=[_rows(D), _const((1, D))], out_specs=_rows(D),
        out_shape=jax.ShapeDtypeStruct((T, D), bf16), compiler_params=_cparams(("parallel",)))(x, g)


def _convnorm_fwd(c, g):
    T = c.shape[0]

    def body(c_ref, g_ref, s_ref):
        cv = c_ref[...]
        r = cv * _rms(cv) * g_ref[...]
        s_ref[...] = (r * _sig(r)).astype(bf16)

    return pl.pallas_call(
        body, name="convnorm_fwd", grid=(T // TT,), in_specs=[_rows(D), _const((1, D))], out_specs=_rows(D),
        out_shape=jax.ShapeDtypeStruct((T, D), bf16), compiler_params=_cparams(("parallel",)))(c, g)


def _qk_fwd(z8, qg, kg, bd):
    T = z8.shape[1]

    def body(q_ref, k_ref, qg_ref, kg_ref, bd_ref, qn_ref, kn_ref):
        bdv = bd_ref[...]
        q = q_ref[...]
        qn_ref[...] = q * lax.rsqrt(_head_sum(q * q, bdv) * (1.0 / HEAD_DIM) + EPS) * qg_ref[...] * (HEAD_DIM ** -0.5)
        k = k_ref[...]
        kn_ref[...] = k * lax.rsqrt(_head_sum(k * k, bdv) * (1.0 / HEAD_DIM) + EPS) * kg_ref[...]

    return pl.pallas_call(
        body, name="qk_fwd", grid=(T // TT,),
        in_specs=[_sec(Z_Q), _sec(Z_K), _const((1, D)), _const((1, D)), _const((128, 128))],
        out_specs=[_rows(D), _rows(D)],
        out_shape=[jax.ShapeDtypeStruct((T, D), f32)] * 2, compiler_params=_cparams(("parallel",)))(z8, z8, qg, kg, bd)


def _gate_fwd(z8, gate_b, ya, yb):
    T = ya.shape[0]

    def body(ga_ref, gb_ref, b_ref, ya_ref, yb_ref, mixed_ref):
        g_a = _sig(ga_ref[...] + b_ref[:, :D])
        g_b = _sig(gb_ref[...] + b_ref[:, D:])
        mixed_ref[...] = (g_a * ya_ref[...] + g_b * yb_ref[...]).astype(bf16)

    return pl.pallas_call(
        body, name="gate_fwd", grid=(T // TT,),
        in_specs=[_sec(Z_GA), _sec(Z_GB), _const((1, 2 * D)), _rows(D), _rows(D)], out_specs=_rows(D),
        out_shape=jax.ShapeDtypeStruct((T, D), bf16), compiler_params=_cparams(("parallel",)))(z8, z8, gate_b, ya, yb)


def _out_norm2_fwd(mixed, w_out, x, g):
    T = x.shape[0]

    def epilogue(acc, extra, const, out):
        x1 = extra[0][...] + acc
        out[0][...] = x1
        out[1][...] = (x1 * _rms(x1) * const[0][...]).astype(bf16)

    return _matmul_fused("mm_t1_norm2", mixed[None], w_out[None], ((0, 0),), epilogue, [(x, _frows())], [g],
                         [_rowshape(T, f32), _rowshape(T, bf16)])


def _down_loss_fwd(f, w_down, x1, target):
    T = x1.shape[0]

    def epilogue(acc, extra, const, out):
        diff = extra[0][...] + acc - extra[1][...]
        dy = diff * (1.0 / D)
        out[0][...] = dy
        out[1][...] = dy.astype(bf16)
        _add_colsum(out[2], diff * diff)

    return _matmul_fused("mm_t2_loss", f[None], w_down[None], ((0, 0),), epilogue, [(x1, _frows()), (target, _frows())],
                         [], [_rowshape(T, f32), _rowshape(T, bf16), _sumshape()], sums=True)


def _up_norm2_bwd(du3, w_up_t, x1, dy, g, token):
    T = x1.shape[0]

    def epilogue(dh, extra, const, out):
        x1v = extra[0][...]
        rstd = _rms(x1v)
        xn = x1v * rstd
        dx1 = extra[1][...] + _rms_bwd(dh * const[0][...], xn, rstd)
        out[0][...] = dx1
        out[1][...] = dx1.astype(bf16)
        _add_colsum(out[2], dh * xn)

    return _matmul_fused("mm_dh2_norm2", du3, w_up_t.reshape(2, D_FF, D), ((0, 0), (1, 1)), epilogue,
                         [(x1, _frows()), (dy, _frows())], [g],
                         [_rowshape(T, f32), _rowshape(T, bf16), _sumshape()], sums=True, passed=[token])


def _out_gate_bwd(dx1b, w_out, z8, gate_b, ya, yb, dz8):
    T = ya.shape[0]

    def epilogue(dm, extra, const, out):
        b_ref = const[0]
        g_a = _sig(extra[0][...] + b_ref[:, :D])
        g_b = _sig(extra[1][...] + b_ref[:, D:])
        out[0][...] = (dm * g_a).astype(bf16)
        out[1][...] = (dm * g_b).astype(bf16)
        dla = dm * extra[2][...] * g_a * (1.0 - g_a)
        dlb = dm * extra[3][...] * g_b * (1.0 - g_b)
        out[2][0] = dla.astype(bf16)
        out[2][1] = dlb.astype(bf16)
        _add_colsum(out[3], dla, slice(0, D))
        _add_colsum(out[3], dlb, slice(D, 2 * D))

    return _matmul_fused(
        "mm_dmixed_gate", dx1b[None], w_out[None], ((0, 0),), epilogue,
        [(z8, _fsec(Z_GA)), (z8, _fsec(Z_GB)), (ya, _frows()), (yb, _frows())], [gate_b],
        [_rowshape(T, bf16), _rowshape(T, bf16),
         (jax.ShapeDtypeStruct(dz8.shape, bf16), pl.BlockSpec((2, FTM, D), lambda i: (1, i, 0))), _sumshape(2 * D)],
        nt=True, sums=True, passed=[dz8], aliases={0: 2})


def _convnorm_bwd(c, ds, g):
    T = c.shape[0]

    def body(c_ref, ds_ref, g_ref, dc_ref, dg_ref):
        cv = c_ref[...]
        rstd = _rms(cv)
        r0 = cv * rstd
        gv = g_ref[...]
        r = r0 * gv
        sg = _sig(r)
        dr = ds_ref[...] * sg * (1.0 + r * (1.0 - sg))
        dc_ref[...] = _rms_bwd(dr * gv, r0, rstd)

        @pl.when(pl.program_id(0) == 0)
        def _():
            dg_ref[...] = jnp.zeros_like(dg_ref)

        dg_ref[...] += _colsum8(dr * r0)

    return pl.pallas_call(
        body, name="convnorm_bwd", grid=(T // TT,), in_specs=[_rows(D), _rows(D), _const((1, D))],
        out_specs=[_rows(D), _acc_spec(D)],
        out_shape=[jax.ShapeDtypeStruct((T, D), f32), jax.ShapeDtypeStruct((8, D), f32)],
        compiler_params=_cparams(("arbitrary",)))(c, ds, g)


def _qk_bwd(z8, dqn, dkn, dv, qg, kg, bd, dz8):
    T = dqn.shape[0]

    def body(q_ref, k_ref, dqn_ref, dkn_ref, dv_ref, qg_ref, kg_ref, bd_ref, dz_in, dz_ref, dqg_ref, dkg_ref):
        del dz_in
        bdv = bd_ref[...]

        @pl.when(pl.program_id(0) == 0)
        def _():
            dqg_ref[...] = jnp.zeros_like(dqg_ref)
            dkg_ref[...] = jnp.zeros_like(dkg_ref)

        def one(raw, dn_scaled, g, dg_ref, sec):
            rstd = lax.rsqrt(_head_sum(raw * raw, bdv) * (1.0 / HEAD_DIM) + EPS)
            n = raw * rstd
            dg_ref[...] += _colsum8(dn_scaled * n)
            dn = dn_scaled * g
            draw = rstd * (dn - n * (_head_sum(dn * n, bdv) * (1.0 / HEAD_DIM)))
            dz_ref[sec] = draw.astype(bf16)

        one(q_ref[...], dqn_ref[...] * (HEAD_DIM ** -0.5), qg_ref[...], dqg_ref, 0)
        one(k_ref[...], dkn_ref[...], kg_ref[...], dkg_ref, 1)
        dz_ref[2] = dv_ref[...].astype(bf16)
        dz_ref[3] = jnp.zeros((TT, D), bf16)

    return pl.pallas_call(
        body, name="qk_bwd", grid=(T // TT,),
        in_specs=[_sec(Z_Q), _sec(Z_K), _rows(D), _rows(D), _rows(D), _const((1, D)), _const((1, D)),
                  _const((128, 128)), pl.BlockSpec(memory_space=pl.ANY)],
        out_specs=[pl.BlockSpec((4, TT, D), lambda i: (1, i, 0)), _acc_spec(D), _acc_spec(D)],
        out_shape=[jax.ShapeDtypeStruct(dz8.shape, bf16), jax.ShapeDtypeStruct((8, D), f32),
                   jax.ShapeDtypeStruct((8, D), f32)],
        input_output_aliases={8: 0},
        compiler_params=_cparams(("arbitrary",)))(z8, z8, dqn, dkn, dv, qg, kg, bd, dz8)


def _in_norm1_bwd(dz8, w_in_t, x, dx1, g, token):
    T = x.shape[0]

    def epilogue(dh, extra, const, out):
        xv = extra[0][...]
        rstd = _rms(xv)
        xn = xv * rstd
        out[0][...] = extra[1][...] + _rms_bwd(dh * const[0][...], xn, rstd)
        _add_colsum(out[1], dh * xn)

    return _matmul_fused("mm_dh_norm1", dz8, w_in_t.reshape(7, D, D), tuple(zip(range(7), _W_OF_Z)), epilogue,
                         [(x, _frows()), (dx1, _frows())], [g], [_rowshape(T, f32), _sumshape()],
                         sums=True, passed=[token])


CCW = 256
CR = 64
HALO = 32


def _conv_fwd(z8, conv_w, conv_b, S):
    T = z8.shape[1]
    nb = T // S
    ncb = D // CCW

    def body(av_ref, ag_ref, w_ref, b_ref, c_ref, pad):
        pad[0:HALO, :] = jnp.zeros((HALO, CCW), f32)

        def fill(i, carry):
            r0 = pl.multiple_of(i * 256, 256)
            pad[pl.ds(HALO + r0, 256), :] = av_ref[pl.ds(r0, 256), :] * _sig(ag_ref[pl.ds(r0, 256), :])
            return carry

        lax.fori_loop(0, S // 256, fill, 0)
        bias = b_ref[...]

        def chunk(i, carry):
            r0 = pl.multiple_of(i * CR, CR)
            win = pad[pl.ds(r0, CR + HALO), :]
            acc = jnp.zeros((CR, CCW), f32) + bias
            for s in range(8):
                part = None
                for m in range((CONV_WIDTH - 1 - s) // 8 + 1):
                    j = CONV_WIDTH - 1 - 8 * m - s
                    term = win[24 - 8 * m:24 - 8 * m + CR + 8, :] * w_ref[j:j + 1, :]
                    part = term if part is None else part + term
                acc = acc + part[8 - s:8 - s + CR, :]
            c_ref[pl.ds(r0, CR), :] = acc
            return carry

        lax.fori_loop(0, S // CR, chunk, 0)

    zs = lambda s: pl.BlockSpec((None, S, CCW), lambda b, cb: (s, b, cb))
    return pl.pallas_call(
        body, name="conv_fwd", grid=(nb, ncb),
        in_specs=[zs(Z_AVAL), zs(Z_AGATE), pl.BlockSpec((CONV_WIDTH, CCW), lambda b, cb: (0, cb)),
                  pl.BlockSpec((1, CCW), lambda b, cb: (0, cb))],
        out_specs=pl.BlockSpec((S, CCW), lambda b, cb: (b, cb)),
        out_shape=jax.ShapeDtypeStruct((T, D), f32),
        scratch_shapes=[pltpu.VMEM((S + HALO, CCW), f32)],
        compiler_params=_cparams(("parallel", "parallel")))(z8, z8, conv_w, conv_b)


def _conv_bwd(dc, z8, conv_w, dz8, S):
    T = dc.shape[0]
    nb = T // S
    ncb = D // CCW

    def body(dc_ref, av_ref, ag_ref, w_ref, dz_in, dz_ref, dw_ref, apad, dpad, shbuf):
        del dz_in
        apad[0:HALO, :] = jnp.zeros((HALO, CCW), f32)
        dpad[S:S + HALO, :] = jnp.zeros((HALO, CCW), f32)
        dw_ref[...] = jnp.zeros_like(dw_ref)

        def fill(i, carry):
            r0 = pl.multiple_of(i * 256, 256)
            apad[pl.ds(HALO + r0, 256), :] = av_ref[pl.ds(r0, 256), :] * _sig(ag_ref[pl.ds(r0, 256), :])
            dpad[pl.ds(r0, 256), :] = dc_ref[pl.ds(r0, 256), :]
            return carry

        lax.fori_loop(0, S // 256, fill, 0)

        def chunk(i, carry):
            r0 = pl.multiple_of(i * CR, CR)
            dwin = dpad[pl.ds(r0, CR + HALO), :]
            da = jnp.zeros((CR, CCW), f32)
            for s in range(8):
                shbuf[...] = dwin[s:s + CR, :]
                dshift = shbuf[...]
                part = None
                for m in range((CONV_WIDTH - 1 - s) // 8 + 1):
                    j = CONV_WIDTH - 1 - 8 * m - s
                    term = dwin[8 * m:8 * m + CR + 8, :] * w_ref[j:j + 1, :]
                    part = term if part is None else part + term
                    a_lag = apad[pl.ds(r0 + HALO - 8 * m, CR), :]
                    dw_ref[8 * j:8 * j + 8, :] += _colsum8(dshift * a_lag)
                da = da + part[s:s + CR, :]
            dw_ref[8 * CONV_WIDTH:8 * CONV_WIDTH + 8, :] += _colsum8(dwin[0:CR, :])
            av = av_ref[pl.ds(r0, CR), :]
            sg = _sig(ag_ref[pl.ds(r0, CR), :])
            dz_ref[0, pl.ds(r0, CR), :] = (da * sg).astype(bf16)
            dz_ref[1, pl.ds(r0, CR), :] = (da * av * sg * (1.0 - sg)).astype(bf16)
            return carry

        lax.fori_loop(0, S // CR, chunk, 0)

    zs = lambda s: pl.BlockSpec((None, S, CCW), lambda b, cb: (s, b, cb))
    return pl.pallas_call(
        body, name="conv_bwd", grid=(nb, ncb),
        in_specs=[pl.BlockSpec((S, CCW), lambda b, cb: (b, cb)), zs(Z_AVAL), zs(Z_AGATE),
                  pl.BlockSpec((CONV_WIDTH, CCW), lambda b, cb: (0, cb)), pl.BlockSpec(memory_space=pl.ANY)],
        out_specs=[pl.BlockSpec((2, S, CCW), lambda b, cb: (0, b, cb)),
                   pl.BlockSpec((None, 256, CCW), lambda b, cb: (b, 0, cb))],
        out_shape=[jax.ShapeDtypeStruct(dz8.shape, bf16), jax.ShapeDtypeStruct((nb, 256, D), f32)],
        input_output_aliases={4: 0},
        scratch_shapes=[pltpu.VMEM((S + HALO, CCW), f32), pltpu.VMEM((S + HALO, CCW), f32),
                        pltpu.VMEM((CR, CCW), f32)],
        compiler_params=_cparams(("parallel", "parallel")))(dc, z8, z8, conv_w, dz8)


FR = 128
NFB = D_FF // CCW


def _ffn_window(ref, i, r0):
    return ref[pl.ds(r0 - 8, FR + 8), :]


def _ffn_u(win, w_ref, b_ref):
    return (win[6:6 + FR, :] * w_ref[0:1, :] + win[7:7 + FR, :] * w_ref[1:2, :]
            + win[8:8 + FR, :] * w_ref[2:3, :] + b_ref[...])


def _ffn_fwd(u3, ffn_w, ffn_b, S):
    T = u3.shape[1]
    nb = T // S

    def body(uv_ref, ug_ref, wv_ref, wg_ref, bv_ref, bg_ref, f_ref):
        def chunk(first, i):
            r0 = 0 if first else pl.multiple_of(i * FR, FR)
            if first:
                z = jnp.zeros((8, CCW), f32)
                wv = jnp.concatenate([z, uv_ref[0:FR, :]], axis=0)
                wg = jnp.concatenate([z, ug_ref[0:FR, :]], axis=0)
            else:
                wv = _ffn_window(uv_ref, i, r0)
                wg = _ffn_window(ug_ref, i, r0)
            u_val = _ffn_u(wv, wv_ref, bv_ref)
            u_gate = _ffn_u(wg, wg_ref, bg_ref)
            f_ref[pl.ds(r0, FR), :] = (u_gate * _sig(u_gate) * u_val).astype(bf16)

        chunk(True, 0)

        def loop(i, carry):
            chunk(False, i)
            return carry

        lax.fori_loop(1, S // FR, loop, 0)

    us = lambda h: pl.BlockSpec((None, S, CCW), lambda b, cb: (h, b, cb))
    ws = lambda h: pl.BlockSpec((3, CCW), lambda b, cb: (0, h * NFB + cb))
    bs = lambda h: pl.BlockSpec((1, CCW), lambda b, cb: (0, h * NFB + cb))
    return pl.pallas_call(
        body, name="ffn_fwd", grid=(nb, NFB),
        in_specs=[us(0), us(1), ws(0), ws(1), bs(0), bs(1)],
        out_specs=pl.BlockSpec((S, CCW), lambda b, cb: (b, cb)),
        out_shape=jax.ShapeDtypeStruct((T, D_FF), bf16),
        compiler_params=_cparams(("parallel", "parallel")))(u3, u3, ffn_w, ffn_w, ffn_b, ffn_b)


def _ffn_bwd(u3, df, ffn_w, ffn_b, S):
    T = u3.shape[1]
    nb = T // S

    def body(uv_ref, ug_ref, df_ref, wv_ref, wg_ref, bv_ref, bg_ref, du_ref, dw_ref, dvpad, dgpad, shbuf):
        dvpad[S:S + 8, :] = jnp.zeros((8, CCW), f32)
        dgpad[S:S + 8, :] = jnp.zeros((8, CCW), f32)
        dw_ref[...] = jnp.zeros_like(dw_ref)

        def chunk(first, i):
            r0 = 0 if first else pl.multiple_of(i * FR, FR)
            if first:
                z = jnp.zeros((8, CCW), f32)
                wv = jnp.concatenate([z, uv_ref[0:FR, :]], axis=0)
                wg = jnp.concatenate([z, ug_ref[0:FR, :]], axis=0)
            else:
                wv = _ffn_window(uv_ref, i, r0)
                wg = _ffn_window(ug_ref, i, r0)
            taps = []
            for h, win in enumerate((wv, wg)):
                shbuf[2 * h] = win[6:6 + FR, :]
                shbuf[2 * h + 1] = win[7:7 + FR, :]
                taps.append((shbuf[2 * h], shbuf[2 * h + 1], win[8:8 + FR, :]))
            conv = lambda x, w_ref, b_ref: (x[0] * w_ref[0:1, :] + x[1] * w_ref[1:2, :] + x[2] * w_ref[2:3, :]
                                            + b_ref[...])
            u_val = conv(taps[0], wv_ref, bv_ref)
            u_gate = conv(taps[1], wg_ref, bg_ref)
            dfc = df_ref[pl.ds(r0, FR), :]
            sg = _sig(u_gate)
            d_val = dfc * u_gate * sg
            d_gate = dfc * u_val * sg * (1.0 + u_gate * (1.0 - sg))
            dvpad[pl.ds(r0, FR), :] = d_val
            dgpad[pl.ds(r0, FR), :] = d_gate
            for h, dd in enumerate((d_val, d_gate)):
                for j in range(3):
                    dw_ref[h, 8 * j:8 * j + 8, :] += _colsum8(dd * taps[h][j])
                dw_ref[h, 24:32, :] += _colsum8(dd)

        chunk(True, 0)

        def loop(i, carry):
            chunk(False, i)
            return carry

        lax.fori_loop(1, S // FR, loop, 0)

        def back(i, carry):
            r0 = pl.multiple_of(i * FR, FR)
            for h, (dpad, w_ref) in enumerate(((dvpad, wv_ref), (dgpad, wg_ref))):
                win = dpad[pl.ds(r0, FR + 8), :]
                du = (win[0:FR, :] * w_ref[2:3, :] + win[1:1 + FR, :] * w_ref[1:2, :]
                      + win[2:2 + FR, :] * w_ref[0:1, :])
                du_ref[h, pl.ds(r0, FR), :] = du.astype(bf16)
            return carry

        lax.fori_loop(0, S // FR, back, 0)

    us = lambda h: pl.BlockSpec((None, S, CCW), lambda b, cb: (h, b, cb))
    ws = lambda h: pl.BlockSpec((3, CCW), lambda b, cb: (0, h * NFB + cb))
    bs = lambda h: pl.BlockSpec((1, CCW), lambda b, cb: (0, h * NFB + cb))
    return pl.pallas_call(
        body, name="ffn_bwd", grid=(nb, NFB),
        in_specs=[us(0), us(1), pl.BlockSpec((S, CCW), lambda b, cb: (b, cb)), ws(0), ws(1), bs(0), bs(1)],
        out_specs=[pl.BlockSpec((2, S, CCW), lambda b, cb: (0, b, cb)),
                   pl.BlockSpec((None, 2, 32, CCW), lambda b, cb: (b, 0, 0, cb))],
        out_shape=[jax.ShapeDtypeStruct((2, T, D_FF), bf16), jax.ShapeDtypeStruct((nb, 2, 32, D_FF), f32)],
        scratch_shapes=[pltpu.VMEM((S + 8, CCW), f32), pltpu.VMEM((S + 8, CCW), f32),
                        pltpu.VMEM((4, FR, CCW), f32)],
        compiler_params=_cparams(("parallel", "parallel")))(u3, u3, df, ffn_w, ffn_w, ffn_b, ffn_b)


AB = ATTN_BLOCK


def _attn_bias_np():
    slopes = (np.float32(2.0) ** (np.float32(-8.0) * np.arange(1, N_HEADS + 1, dtype=np.float32)
                                  / np.float32(N_HEADS))).astype(np.float32)
    steps = (np.arange(AB)[:, None] + AB) - np.arange(2 * AB)[None, :]
    own = (np.arange(2 * AB) >= AB)[None, :]
    out = []
    for window, dil in GROUPS:
        valid = (steps >= 0) & (steps <= window // dil)
        dist = slopes[:, None, None] * (steps * dil).astype(np.float32)[None]
        kinds = [np.where(v[None], dist, np.float32(MASK_BIAS)) for v in (valid, valid & own)]
        out.append(np.stack(kinds, axis=1))
    return np.stack(out).astype(np.float32)


def _attn_bias():
    return jnp.asarray(_attn_bias_np())


def _head_masks():
    lane = lax.broadcasted_iota(jnp.int32, (1, 128), 1)
    return (lane < HEAD_DIM, lane >= HEAD_DIM)


def _perm_chunks(S, d):
    L = S // d
    ch = min(L, 256)
    out = []
    for r in range(d):
        for c in range(L // ch):
            start = r + d * ch * c
            out.append((pl.ds(start, ch, stride=d) if d > 1 else pl.ds(start, ch), r * L + c * ch, ch))
    return out


def _stack_heads(x, masks):
    return jnp.concatenate([jnp.where(masks[0], x, 0), jnp.where(masks[1], x, 0)], axis=0)


def _block_row(j):
    return j * AB if isinstance(j, int) else pl.multiple_of(j * AB, AB)


def _three_stages(n, stage_a, stage_b, stage_c, unroll):
    stage_a(0)
    stage_a(1)
    stage_b(0)

    def body(j, carry):
        stage_c(j - 1)
        stage_b(j)
        stage_a(j + 1)
        return carry

    lax.fori_loop(1, n - 1, body, 0, unroll=unroll)
    stage_c(n - 2)
    stage_b(n - 1)
    stage_c(n - 1)


_NT = (((1,), (1,)), ((), ()))
_TN = (((0,), (0,)), ((), ()))
SCH = 64


def _attn_fwd(qn, kn, z8, bias, S):
    T = qn.shape[0]
    nb = T // S
    nblk = S // AB

    def body(q_ref, k_ref, v_ref, bias_ref, o_ref, ob_ref, lse_ref, qs, ks, vs, s2, p2, ogp, lgp, *group_scratch):
        og, lg = group_scratch[:3], group_scratch[3:]
        masks = _head_masks()
        ks[0:AB, :] = jnp.zeros((AB, 128), bf16)
        vs[0:AB, :] = jnp.zeros((AB, 128), bf16)

        for g, (_, d) in enumerate(GROUPS):
            nsub = S // (d * AB)
            chunks = _perm_chunks(S, d)
            for src, dst, ch in chunks:
                qs[dst:dst + ch, :] = q_ref[src, :].astype(bf16)
                ks[AB + dst:AB + dst + ch, :] = k_ref[src, :].astype(bf16)
                vs[AB + dst:AB + dst + ch, :] = v_ref[src, :].astype(bf16)
            od, ld = (og[g], lg[g]) if d == 1 else (ogp, lgp)

            def scores(j):
                r0 = _block_row(j)
                q2 = _stack_heads(qs[pl.ds(r0, AB), :], masks)
                s2[j] = lax.dot_general(q2, ks[pl.ds(r0, 2 * AB), :], _NT, preferred_element_type=f32)

            def softmax(j, g=g, nsub=nsub, ld=ld):
                r0 = _block_row(j)
                kind = int(j % nsub == 0) if isinstance(j, int) else (j % nsub == 0).astype(jnp.int32)
                for cc in range(AB // SCH):
                    lses = []
                    for hh in range(2):
                        rows = pl.ds(hh * AB + cc * SCH, SCH)
                        sb = s2[j, rows, :] - bias_ref[g, hh, kind, cc * SCH:(cc + 1) * SCH, :]
                        m = jnp.max(sb, axis=-1, keepdims=True)
                        p = jnp.exp(sb - m)
                        den = jnp.sum(p, axis=-1, keepdims=True)
                        p2[j, rows, :] = (p * (1.0 / den)).astype(bf16)
                        lses.append(m + jnp.log(den))
                    ld[pl.ds(r0 + cc * SCH, SCH), :] = jnp.where(masks[0], lses[0], lses[1])

            def values(j, od=od):
                r0 = _block_row(j)
                pv2 = jnp.dot(p2[j], vs[pl.ds(r0, 2 * AB), :], preferred_element_type=f32)
                od[pl.ds(r0, AB), :] = jnp.where(masks[0], pv2[:AB], pv2[AB:])

            _three_stages(nblk, scores, softmax, values, nblk - 2)

            if d > 1:
                for src, dst, ch in chunks:
                    og[g][src, :] = ogp[dst:dst + ch, :]
                    lg[g][src, :] = lgp[dst:dst + ch, :]

        def combine(i, carry):
            rr = pl.ds(pl.multiple_of(i * 256, 256), 256)
            l0, l1, l2 = lg[0][rr, :], lg[1][rr, :], lg[2][rr, :]
            mx = jnp.maximum(jnp.maximum(l0, l1), l2)
            e0, e1, e2 = jnp.exp(l0 - mx), jnp.exp(l1 - mx), jnp.exp(l2 - mx)
            den = e0 + e1 + e2
            o = (e0 * og[0][rr, :] + e1 * og[1][rr, :] + e2 * og[2][rr, :]) / den
            o_ref[rr, :] = o
            ob_ref[rr, :] = o.astype(bf16)
            lse_ref[rr, :] = mx + jnp.log(den)
            return carry

        lax.fori_loop(0, S // 256, combine, 0)

    blk = pl.BlockSpec((S, 128), lambda b, hp: (b, hp))
    return pl.pallas_call(
        body, name="attn_fwd", grid=(nb, N_HEADS // 2),
        in_specs=[blk, blk, pl.BlockSpec((None, S, 128), lambda b, hp: (Z_V, b, hp)),
                  pl.BlockSpec((3, 2, 2, AB, 2 * AB), lambda b, hp: (0, hp, 0, 0, 0))],
        out_specs=[blk, blk, blk],
        out_shape=[jax.ShapeDtypeStruct((T, D), f32), jax.ShapeDtypeStruct((T, D), bf16),
                   jax.ShapeDtypeStruct((T, D), f32)],
        scratch_shapes=[pltpu.VMEM((S, 128), bf16), pltpu.VMEM((S + AB, 128), bf16), pltpu.VMEM((S + AB, 128), bf16),
                        pltpu.VMEM((nblk, 2 * AB, 2 * AB), f32), pltpu.VMEM((nblk, 2 * AB, 2 * AB), bf16),
                        pltpu.VMEM((S, 128), f32), pltpu.VMEM((S, 128), f32)] + [pltpu.VMEM((S, 128), f32)] * 6,
        compiler_params=_cparams(("parallel", "parallel")))(qn, kn, z8, bias)


def _attn_bwd(qn, kn, z8, do, o, lse, bias, bd, S, after):
    T = qn.shape[0]
    nb = T // S

    nblk = S // AB

    def body(q_ref, k_ref, v_ref, do_ref, o_ref, lse_ref, bias_ref, bd_ref, after_ref, dq_ref, dk_ref, dv_ref,
             delta, qs, ks, vs, dos, lsp, dlp, s2, dp2, p2, ds2, dqp, dkp, dvp):
        del after_ref
        masks = _head_masks()
        bdv = bd_ref[...]
        dq_ref[...] = jnp.zeros_like(dq_ref)
        dk_ref[...] = jnp.zeros_like(dk_ref)
        dv_ref[...] = jnp.zeros_like(dv_ref)
        ks[0:AB, :] = jnp.zeros((AB, 128), bf16)
        vs[0:AB, :] = jnp.zeros((AB, 128), bf16)

        def prep(i, carry):
            rr = pl.ds(pl.multiple_of(i * 256, 256), 256)
            delta[rr, :] = _head_sum(do_ref[rr, :] * o_ref[rr, :], bdv)
            return carry

        lax.fori_loop(0, S // 256, prep, 0, unroll=True)

        for g, (_, d) in enumerate(GROUPS):
            nsub = S // (d * AB)
            chunks = _perm_chunks(S, d)
            for src, dst, ch in chunks:
                qs[dst:dst + ch, :] = q_ref[src, :].astype(bf16)
                ks[AB + dst:AB + dst + ch, :] = k_ref[src, :].astype(bf16)
                vs[AB + dst:AB + dst + ch, :] = v_ref[src, :].astype(bf16)
                dos[dst:dst + ch, :] = do_ref[src, :].astype(bf16)
                lsp[dst:dst + ch, :] = lse_ref[src, :]
                dlp[dst:dst + ch, :] = delta[src, :]
            dkp[...] = jnp.zeros_like(dkp)
            dvp[...] = jnp.zeros_like(dvp)

            def scores(j):
                r0 = _block_row(j)
                q2 = _stack_heads(qs[pl.ds(r0, AB), :], masks)
                do2 = _stack_heads(dos[pl.ds(r0, AB), :], masks)
                s2[j] = lax.dot_general(q2, ks[pl.ds(r0, 2 * AB), :], _NT, preferred_element_type=f32)
                dp2[j] = lax.dot_general(do2, vs[pl.ds(r0, 2 * AB), :], _NT, preferred_element_type=f32)

            def probs(j, g=g, nsub=nsub):
                r0 = _block_row(j)
                kind = int(j % nsub == 0) if isinstance(j, int) else (j % nsub == 0).astype(jnp.int32)
                for cc in range(AB // SCH):
                    lse_c = lsp[pl.ds(r0 + cc * SCH, SCH), :]
                    del_c = dlp[pl.ds(r0 + cc * SCH, SCH), :]
                    for hh in range(2):
                        c0 = hh * HEAD_DIM
                        rows = pl.ds(hh * AB + cc * SCH, SCH)
                        sb = s2[j, rows, :] - bias_ref[g, hh, kind, cc * SCH:(cc + 1) * SCH, :]
                        p = jnp.exp(sb - lse_c[:, c0:c0 + 1])
                        p2[j, rows, :] = p.astype(bf16)
                        ds2[j, rows, :] = (p * (dp2[j, rows, :] - del_c[:, c0:c0 + 1])).astype(bf16)

            def grads(j):
                r0 = _block_row(j)
                q2 = _stack_heads(qs[pl.ds(r0, AB), :], masks)
                do2 = _stack_heads(dos[pl.ds(r0, AB), :], masks)
                dsb = ds2[j]
                t = jnp.dot(dsb, ks[pl.ds(r0, 2 * AB), :], preferred_element_type=f32)
                dqp[pl.ds(r0, AB), :] = jnp.where(masks[0], t[:AB], t[AB:])
                dkp[pl.ds(r0, 2 * AB), :] += lax.dot_general(dsb, q2, _TN, preferred_element_type=f32)
                dvp[pl.ds(r0, 2 * AB), :] += lax.dot_general(p2[j], do2, _TN, preferred_element_type=f32)

            _three_stages(nblk, scores, probs, grads, nblk - 2)

            for src, dst, ch in chunks:
                dq_ref[src, :] += dqp[dst:dst + ch, :]
                dk_ref[src, :] += dkp[AB + dst:AB + dst + ch, :]
                dv_ref[src, :] += dvp[AB + dst:AB + dst + ch, :]

    blk = pl.BlockSpec((S, 128), lambda b, hp: (b, hp))
    row = lambda dt, pad=0: pltpu.VMEM((S + pad, 128), dt)
    blocks = lambda dt: pltpu.VMEM((nblk, 2 * AB, 2 * AB), dt)
    return pl.pallas_call(
        body, name="attn_bwd", grid=(nb, N_HEADS // 2),
        in_specs=[blk, blk, pl.BlockSpec((None, S, 128), lambda b, hp: (Z_V, b, hp)), blk, blk, blk,
                  pl.BlockSpec((3, 2, 2, AB, 2 * AB), lambda b, hp: (0, hp, 0, 0, 0)),
                  pl.BlockSpec((128, 128), lambda b, hp: (0, 0)), pl.BlockSpec(memory_space=pl.ANY)],
        out_specs=[blk, blk, blk],
        out_shape=[jax.ShapeDtypeStruct((T, D), f32)] * 3,
        scratch_shapes=[row(f32), row(bf16), row(bf16, AB), row(bf16, AB), row(bf16), row(f32), row(f32),
                        blocks(f32), blocks(f32), blocks(bf16), blocks(bf16), row(f32), row(f32, AB), row(f32, AB)],
        compiler_params=_cparams(("parallel", "parallel")))(qn, kn, z8, do, o, lse, bias, bd, after)


def _any_spec():
    return pl.BlockSpec(memory_space=pl.ANY)


AG_CHUNKS = 4


def _allgather_rows(shards, n_full):
    n = len(shards)
    parts = [(a, q) for a in range(n_full) for q in range(AG_CHUNKS)]

    def body(*refs):
        ins, outs = refs[:n], refs[n:2 * n]
        send_sems, recv_sems, local_sems = refs[2 * n:]
        x, y, c, me = _my_pos()
        sibling = (x, y, 1 - c)
        chips = [(1 - x, y), (x, 1 - y), (1 - x, 1 - y)]

        def idx(px, py, pc):
            return 4 * px + 2 * py + pc

        def copy(v, k, blk, to, own=False):
            a, q = parts[v]
            rows = pl.ds(q * (shards[a].shape[0] // AG_CHUNKS), shards[a].shape[0] // AG_CHUNKS)
            return pltpu.make_async_remote_copy(
                src_ref=ins[a].at[rows] if own else outs[a].at[blk, rows], dst_ref=outs[a].at[blk, rows],
                send_sem=send_sems.at[v, k], recv_sem=recv_sems.at[v, k], device_id=to, device_id_type=MESH)

        mine = [pltpu.make_async_copy(ins[a], outs[a].at[me], local_sems.at[a]) for a in range(n)]
        for cp in mine:
            cp.start()
        first = []
        for v in range(len(parts)):
            first.append(copy(v, 0, me, sibling, own=True))
            first += [copy(v, 1 + j, me, (*chip, c), own=True) for j, chip in enumerate(chips[:2])]
        for cp in first:
            cp.start()
        relay_blk = jnp.where(c == 1, idx(1 - x, y, c), idx(x, 1 - y, c))
        relay_to = (jnp.where(c == 1, x, 1 - x), jnp.where(c == 1, 1 - y, y), c)
        passed = []
        for v in range(len(parts)):
            for j, chip in enumerate(chips[:2]):
                copy(v, 1 + j, idx(*chip, c), (x, y, c)).wait_recv()
            cp = copy(v, 3, relay_blk, relay_to)
            cp.start()
            passed.append(cp)
            for j, chip in enumerate(chips):
                if j == 2:
                    copy(v, 3, idx(*chip, c), (x, y, c)).wait_recv()
                cp = copy(v, 4 + j, idx(*chip, c), sibling)
                cp.start()
                passed.append(cp)
        for v in range(len(parts)):
            copy(v, 0, idx(x, y, 1 - c), (x, y, c)).wait_recv()
            for j, chip in enumerate(chips):
                copy(v, 4 + j, idx(*chip, 1 - c), (x, y, c)).wait_recv()
        for cp in first + passed:
            cp.wait_send()
        for cp in mine:
            cp.wait()

    return pl.pallas_call(
        body, name="allgather_weights",
        in_specs=[_any_spec()] * n, out_specs=[_any_spec()] * n,
        out_shape=[jax.ShapeDtypeStruct((N_DEV,) + s.shape, s.dtype) for s in shards],
        scratch_shapes=[pltpu.SemaphoreType.DMA((len(parts), 7)), pltpu.SemaphoreType.DMA((len(parts), 7)),
                        pltpu.SemaphoreType.DMA((n,))],
    )(*shards)


def _peer(x, y, c, k):
    tx = 1 - x if (k >> 2) & 1 else x
    ty = 1 - y if (k >> 1) & 1 else y
    tc = 1 - c if k & 1 else c
    return (tx, ty, tc), 4 * tx + 2 * ty + tc


_PEER_ORDER = (2, 4, 6, 3, 5, 7, 1)


_HBM = pl.BlockSpec(memory_space=pltpu.HBM)
_SEM = pl.BlockSpec(memory_space=pltpu.SEMAPHORE)
_EFFECT = pltpu.SideEffectType.DATAFLOW_SIDE_EFFECTING


def _exchange_copies(srcs, lands, send_sems, recv_sems, gather, half):
    x, y, c, me = _my_pos()
    pick = lambda px, py: None if half is None else ((px == py) if half == 0 else (px != py))
    copies = []
    for k in _PEER_ORDER:
        tgt, tidx = _peer(x, y, c, k)
        for a in range(len(srcs)):
            copies.append((pltpu.make_async_remote_copy(
                src_ref=srcs[a] if gather else srcs[a].at[tidx], dst_ref=lands[a].at[me],
                send_sem=send_sems.at[7 * a + k - 1], recv_sem=recv_sems.at[7 * a + k - 1],
                device_id=tgt, device_id_type=MESH), pick(tgt[0], tgt[1])))
    return copies, pick(x, y)


def _when(cond, fn):
    if cond is None:
        fn()
    else:
        pl.when(cond)(fn)


def _exchange_start(name, srcs, lands=None, after=None, gather=None, half=None):
    n = len(srcs)
    gather = (lands is not None) if gather is None else gather
    if lands is None:
        lands = [lax.empty(g.shape, g.dtype) for g in srcs]
    extra = [] if after is None else [after]

    def body(*refs):
        src_refs, land_refs = refs[:n], refs[n:2 * n]
        send_sems, recv_sems = refs[2 * n + len(extra)], refs[2 * n + len(extra) + 1]
        token = refs[-1]
        for cp, sends in _exchange_copies(src_refs, land_refs, send_sems, recv_sems, gather, half)[0]:
            _when(sends, cp.start)
        token[...] = jnp.zeros_like(token)

    hbm = lambda a: pltpu.with_memory_space_constraint(a, pltpu.HBM)
    outs = pl.pallas_call(
        body, name=name,
        out_shape=(pltpu.SemaphoreType.DMA((7 * n,)), pltpu.SemaphoreType.DMA((7 * n,)),
                   *[pltpu.HBM(g.shape, g.dtype) for g in list(srcs) + list(lands)],
                   jax.ShapeDtypeStruct((8, 128), f32)),
        in_specs=[_HBM] * (2 * n) + [pl.BlockSpec(memory_space=pl.ANY)] * len(extra),
        out_specs=(_SEM, _SEM, *([_HBM] * (2 * n)), pl.BlockSpec(memory_space=pltpu.VMEM)),
        input_output_aliases={i: 2 + i for i in range(2 * n)},
        compiler_params=pltpu.CompilerParams(has_side_effects=_EFFECT),
    )(*[hbm(g) for g in srcs], *[hbm(g) for g in lands], *extra)
    return outs[0], outs[1], list(outs[2:2 + n]), list(outs[2 + n:2 + 2 * n]), outs[-1], gather, half


def _exchange_wait(name, started, after):
    send_sems, recv_sems, srcs, lands, _, gather, half = started
    n = len(srcs)
    after = list(after) if isinstance(after, (list, tuple)) else [after]

    def body(*refs):
        src_refs, land_refs = refs[:n], refs[n:2 * n]
        s_sems, r_sems = refs[2 * n], refs[2 * n + 1]
        copies, receives = _exchange_copies(src_refs, land_refs, s_sems, r_sems, gather, half)
        for cp, sends in copies:
            _when(sends, cp.wait_send)
            _when(receives, cp.wait_recv)

    outs = pl.pallas_call(
        body, name=name,
        out_shape=tuple(pltpu.HBM(a.shape, a.dtype) for a in list(srcs) + list(lands)),
        in_specs=[_HBM] * (2 * n) + [_SEM, _SEM] + [pl.BlockSpec(memory_space=pl.ANY)] * len(after),
        out_specs=tuple([_HBM] * (2 * n)),
        input_output_aliases={i: i for i in range(2 * n)},
        compiler_params=pltpu.CompilerParams(has_side_effects=_EFFECT),
    )(*srcs, *lands, send_sems, recv_sems, *after)
    return list(outs[:n]), list(outs[n:])


SMALL_ROWS = 128


def _small_start(name, sg, after=None):
    return _exchange_start(name, [sg], [lax.empty((N_DEV,) + sg.shape, f32)], after=after)


def _small_sum(name, me, started, after):
    (own,), (slots,) = _exchange_wait(name + "_wait", started, after)

    def body(me_ref, s_ref, own_ref, out_ref):
        acc = None
        for p in range(N_DEV):
            term = lax.cond(me_ref[0] == p, lambda: own_ref[...], lambda p=p: s_ref[p])
            acc = term if acc is None else acc + term
        out_ref[...] = acc

    return pl.pallas_call(
        body, name=name + "_sum",
        in_specs=[pl.BlockSpec(memory_space=pltpu.SMEM), pl.BlockSpec(memory_space=pltpu.VMEM),
                  pl.BlockSpec(memory_space=pltpu.VMEM)],
        out_specs=pl.BlockSpec(memory_space=pltpu.VMEM),
        out_shape=jax.ShapeDtypeStruct(own.shape, f32))(me, slots, own)


def _adam_math(g, w, m, v):
    m = ADAM_B1 * m + (1.0 - ADAM_B1) * g
    v = ADAM_B2 * v + (1.0 - ADAM_B2) * (g * g)
    m_hat = m / (1.0 - ADAM_B1 ** ADAM_STEP)
    v_hat = v / (1.0 - ADAM_B2 ** ADAM_STEP)
    delta = -ADAM_LR * (m_hat / (jnp.sqrt(v_hat) + ADAM_EPS) + ADAM_WD * w)
    return delta, m, v


def _adam_slots(name, me, slots, own, w, m, v, tr, transposed=False):
    rows = slots.shape[1]

    def body(me_ref, s_ref, own_ref, w_ref, m_ref, v_ref, g_ref, d_ref, nm_ref, nv_ref):
        mine = own_ref[...]
        g = None
        for p in range(N_DEV):
            term = lax.cond(me_ref[0] == p, lambda: mine, lambda p=p: s_ref[p]).astype(f32)
            g = term if g is None else g + term
        if transposed:
            g = g.T
        delta, nm, nv = _adam_math(g, w_ref[...], m_ref[...], v_ref[...])
        g_ref[...] = g
        d_ref[...] = delta
        nm_ref[...] = nm
        nv_ref[...] = nv

    mode = dict(pipeline_mode=pl.Buffered(1)) if rows == tr else {}
    if transposed:
        rs = pl.BlockSpec((D, tr), lambda i, me_ref: (0, i))
        rs_in = pl.BlockSpec((D, tr), lambda i, me_ref: (0, i), **mode)
    else:
        rs = pl.BlockSpec((tr, D), lambda i, me_ref: (i, 0))
        rs_in = pl.BlockSpec((tr, D), lambda i, me_ref: (i, 0), **mode)
    return pl.pallas_call(
        body, name=name,
        grid_spec=pltpu.PrefetchScalarGridSpec(
            num_scalar_prefetch=1, grid=(rows // tr,),
            in_specs=[pl.BlockSpec((N_DEV, tr, D), lambda i, me_ref: (0, i, 0), **mode),
                      pl.BlockSpec((None, tr, D), lambda i, me_ref: (me_ref[0], i, 0), **mode), rs_in, rs_in, rs_in],
            out_specs=[rs] * 4),
        out_shape=[jax.ShapeDtypeStruct(w.shape, f32)] * 4,
        compiler_params=_cparams(("parallel",)))(me, slots, own, w, m, v)


def _adam_small(g, w, m, v):
    def body(g_ref, w_ref, m_ref, v_ref, d_ref, nm_ref, nv_ref):
        delta, nm, nv = _adam_math(g_ref[...], w_ref[...], m_ref[...], v_ref[...])
        d_ref[...] = delta
        nm_ref[...] = nm
        nv_ref[...] = nv

    return pl.pallas_call(body, name="adam_small", out_shape=[jax.ShapeDtypeStruct(g.shape, f32)] * 3)(g, w, m, v)


FFN_PAD = 6 * D


_SMALL_PARTS = (("norm1_g", 1), ("gate_b", 2), ("conv_w", CONV_WIDTH), ("conv_b", 1), ("conv_norm_g", 1),
                ("q_norm_g", 1), ("k_norm_g", 1), ("norm2_g", 1), ("ffn_conv_w", 18), ("ffn_conv_b", 6), ("last", 1))


def _small_offsets():
    out, row = {}, 0
    for name, rows in _SMALL_PARTS:
        out[name] = row
        row += -(-rows // 8) * 8
    assert row == SMALL_ROWS
    return out


def _pack_small(norm1_g, gate_b, conv_w, conv_b, conv_norm_g, q_norm_g, k_norm_g, norm2_g, ffn_conv_w, ffn_conv_b,
                last_row=None):
    pad_h = lambda a: jnp.pad(a, ((0, 0), (0, D - HEAD_DIM)))
    pad_f = lambda a: jnp.pad(a, ((0, 0), (0, FFN_PAD - 2 * D_FF))).reshape(-1, D)
    parts = [norm1_g, gate_b.reshape(2, D), conv_w, conv_b, conv_norm_g, pad_h(q_norm_g), pad_h(k_norm_g), norm2_g,
             pad_f(ffn_conv_w), pad_f(ffn_conv_b), jnp.zeros((1, D), f32) if last_row is None else last_row]
    return jnp.concatenate([jnp.pad(p, ((0, -p.shape[0] % 8), (0, 0))) for p in parts], axis=0)


def _unpack_small(p):
    o = _small_offsets()
    rows = lambda name, n: p[o[name]:o[name] + n]
    ffn = lambda a: a.reshape(-1, FFN_PAD)[:, :2 * D_FF]
    return dict(
        norm1_g=rows("norm1_g", 1), gate_b=rows("gate_b", 2).reshape(1, 2 * D), conv_w=rows("conv_w", CONV_WIDTH),
        conv_b=rows("conv_b", 1), conv_norm_g=rows("conv_norm_g", 1), q_norm_g=rows("q_norm_g", 1)[:, :HEAD_DIM],
        k_norm_g=rows("k_norm_g", 1)[:, :HEAD_DIM], norm2_g=rows("norm2_g", 1),
        ffn_conv_w=ffn(rows("ffn_conv_w", 18)), ffn_conv_b=ffn(rows("ffn_conv_b", 6)))


_ADAM_TILE = {896: 128, 704: 704, 128: 128, 352: 176}


def kernel(x, norm1_g, w_in, gate_b, conv_w, conv_b, conv_norm_g, w_conv_out, q_norm_g, k_norm_g, w_attn_out, w_out, norm2_g, w_up, ffn_conv_w, ffn_conv_b, w_down, loss_target, m_norm1_g, m_w_in, m_gate_b, m_conv_w, m_conv_b, m_conv_norm_g, m_w_conv_out, m_q_norm_g, m_k_norm_g, m_w_attn_out, m_w_out, m_norm2_g, m_w_up, m_ffn_conv_w, m_ffn_conv_b, m_w_down, v_norm1_g, v_w_in, v_gate_b, v_conv_w, v_conv_b, v_conv_norm_g, v_w_conv_out, v_q_norm_g, v_k_norm_g, v_w_attn_out, v_w_out, v_norm2_g, v_w_up, v_ffn_conv_w, v_ffn_conv_b, v_w_down):
    BL, S, _ = x.shape
    T = BL * S
    me = 4 * lax.axis_index("x") + 2 * lax.axis_index("y") + lax.axis_index("c")
    xt = x.reshape(T, D)
    target = loss_target.reshape(T, D)

    big = dict(w_in=(w_in[0], m_w_in[0], v_w_in[0]), w_up=(w_up[0], m_w_up[0], v_w_up[0]),
               w_conv_out=(w_conv_out[0], m_w_conv_out[0], v_w_conv_out[0]),
               w_attn_out=(w_attn_out[0], m_w_attn_out[0], v_w_attn_out[0]),
               w_out=(w_out[0], m_w_out[0], v_w_out[0]), w_down=(w_down[0], m_w_down[0], v_w_down[0]))
    order = ["w_in", "w_conv_out", "w_attn_out", "w_out", "w_up", "w_down"]
    shards = [(big[n][0].T if n in ("w_in", "w_up") else big[n][0]).astype(bf16) for n in order]
    gathered = _allgather_rows(shards, 1)
    ga_proj = _exchange_start("gather_start_proj", shards[1:4], gathered[1:4], after=gathered[0])
    ga_ffn = _exchange_start("gather_start_ffn", shards[4:6], gathered[4:6], after=ga_proj[4])
    W = {"w_in": gathered[0].reshape(-1, D)}

    def place_cols(shard, full_cols):
        z = jnp.zeros((shard.shape[0], full_cols), f32)
        return lax.dynamic_update_slice(z, shard, (0, me * shard.shape[1]))

    zr = lambda a: jnp.zeros_like(a)
    conv_local = _pack_small(
        zr(norm1_g), zr(gate_b), place_cols(conv_w[0], D), zr(conv_b), zr(conv_norm_g), zr(q_norm_g), zr(k_norm_g),
        zr(norm2_g), place_cols(ffn_conv_w[0], 2 * D_FF), zr(ffn_conv_b))
    ga_conv = _small_start("gather_conv_start", conv_local, after=ga_ffn[4])

    bd = (jnp.arange(128)[:, None] // HEAD_DIM == jnp.arange(128)[None, :] // HEAD_DIM).astype(bf16)
    bias = _attn_bias()
    qg = jnp.tile(q_norm_g, (1, N_HEADS))
    kg = jnp.tile(k_norm_g, (1, N_HEADS))

    h = _norm1_fwd(xt, norm1_g)
    z8 = _matmul_call(
        "mm_z", h, W["w_in"],
        pl.BlockSpec((2048, D), lambda i, j, k: (i, 0)),
        pl.BlockSpec((1024, D), lambda i, j, k: (_wsec_of_zsec(j), 0)),
        pl.BlockSpec((None, 2048, D), lambda i, j, k: (j, i, 0)),
        jax.ShapeDtypeStruct((8, T, D), f32), (T // 2048, 7, 1), "nt", 1, 2048, 1024, after=ga_conv[4])
    conv_all = _unpack_small(_small_sum("gather_conv", me.reshape(1), ga_conv, z8))
    conv_w_full, ffn_w_full = conv_all["conv_w"], conv_all["ffn_conv_w"]
    c = _conv_fwd(z8, conv_w_full, conv_b, S)
    s = _convnorm_fwd(c, conv_norm_g)
    qn, kn = _qk_fwd(z8, qg, kg, bd)
    for n, g in zip(order[1:4], _exchange_wait("gather_wait_proj", ga_proj, qn)[1]):
        W[n] = g.reshape(-1, D)
    ya = _matmul("mm_ya", s, W["w_conv_out"], "nn", f32)
    o, ob, lse = _attn_fwd(qn, kn, z8, bias, S)
    yb = _matmul("mm_yb", ob, W["w_attn_out"], "nn", f32)
    mixed = _gate_fwd(z8, gate_b, ya, yb)
    x1, h2 = _out_norm2_fwd(mixed, W["w_out"], xt, norm2_g)
    for n, g in zip(order[4:6], _exchange_wait("gather_wait_ffn", ga_ffn, x1)[1]):
        W[n] = g.reshape(-1, D)
    TNU = D_FF // 2
    u3 = _matmul_call(
        "mm_u", h2, W["w_up"],
        pl.BlockSpec((1024, D), lambda i, j, k: (i, 0)),
        pl.BlockSpec((TNU, D), lambda i, j, k: (j, 0)),
        pl.BlockSpec((None, 1024, TNU), lambda i, j, k: (j // 2, i, j % 2)),
        jax.ShapeDtypeStruct((2, T, D_FF), f32), (T // 1024, 4, 1), "nt", 1, 1024, TNU)
    f = _ffn_fwd(u3, ffn_w_full, ffn_conv_b, S)
    dy, dyb, lacc = _down_loss_fwd(f, W["w_down"], x1, target)
    loss_local = 0.5 / D * jnp.sum(lacc)

    df = _matmul("mm_df", dyb, W["w_down"], "nt", f32, tn=TNU)
    g_w_down = _matmul("mm_dwdn", f, dyb, "tn", bf16, tm=TNU)
    du3, dffn = _ffn_bwd(u3, df, ffn_w_full, ffn_conv_b, S)
    g_w_up = _matmul_call(
        "mm_dwup", du3, h2,
        pl.BlockSpec((None, T, TNU), lambda i, j, k: (i // 2, 0, i % 2)),
        pl.BlockSpec((T, D), lambda i, j, k: (0, 0)),
        pl.BlockSpec((TNU, D), lambda i, j, k: (i, 0)),
        jax.ShapeDtypeStruct((2 * D_FF, D), bf16), (4, 1, 1), "tn", 1, TNU, D)
    blocks8 = lambda a: a.reshape(N_DEV, -1, D)
    ex_ffn = _exchange_start("scatter_start_ffn", [blocks8(g_w_up), blocks8(g_w_down)])
    dx1, dx1b, dg_norm2 = _up_norm2_bwd(du3, W["w_up"], x1, dy, norm2_g, ex_ffn[4])
    g_w_out = _matmul("mm_dwo", mixed, dx1b, "tn", bf16, tm=512)
    dz8 = lax.empty((8, T, D), bf16)
    dya, dyb2, dz8, dg_gate = _out_gate_bwd(dx1b, W["w_out"], z8, gate_b, ya, yb, dz8)
    ds = _matmul("mm_ds", dya, W["w_conv_out"], "nt", f32)
    g_w_conv_out = _matmul("mm_dwco", s, dya, "tn", bf16, tm=512)
    g_w_attn_out = _matmul("mm_dwao", ob, dyb2, "tn", bf16, tm=512)
    ex_proj = _exchange_start("scatter_start_proj", [blocks8(g_w_conv_out), blocks8(g_w_attn_out), blocks8(g_w_out)])
    do = _matmul("mm_do", dyb2, W["w_attn_out"], "nt", f32, after=ex_proj[4])
    dc, dg_convnorm = _convnorm_bwd(c, ds, conv_norm_g)
    dz8a, dconv = _conv_bwd(dc, z8, conv_w_full, dz8, S)
    dwin_specs = lambda zsec, wsec: (
        pl.BlockSpec((None, T, D), lambda i, j, k: (zsec(i), 0, 0)), pl.BlockSpec((T, D), lambda i, j, k: (0, 0)),
        pl.BlockSpec((1024, D), lambda i, j, k: (wsec(i), 0)), jax.ShapeDtypeStruct((7 * D, D), bf16))
    g_w_in = _matmul_call("mm_dwin_a", dz8a, h, *dwin_specs(lambda i: i, lambda i: jnp.where(i < 2, i, i + 3)),
                          (4, 1, 1), "tn", 1, D, D)
    ex_in_a = _exchange_start("scatter_start_in_a", [blocks8(g_w_in)], half=0)
    dqn, dkn, dv = _attn_bwd(qn, kn, z8, do, o, lse, bias, bd, S, ex_in_a[4])
    dz8b, dg_q, dg_k = _qk_bwd(z8, dqn, dkn, dv, qg, kg, bd, dz8a)
    g_w_in = _matmul_call("mm_dwin_b", dz8b, h, *dwin_specs(lambda i: i + 4, lambda i: i + 2),
                          (3, 1, 1), "tn", 1, D, D, fill=ex_in_a[2][0].reshape(7 * D, D))
    ex_in_b = _exchange_start("scatter_start_in_b", [blocks8(g_w_in)], ex_in_a[3], gather=False, half=1)
    grad_x, dg_norm1 = _in_norm1_bwd(dz8b, W["w_in"], xt, dx1, norm1_g, ex_in_b[4])

    sum8 = lambda a: a.reshape(-1, 8, a.shape[-1]).sum(axis=1)
    dconv_s = sum8(dconv.sum(axis=0))
    dffn_s = dffn.sum(axis=0).reshape(2, 4, 8, D_FF).sum(axis=2)
    dffn_w = jnp.concatenate([dffn_s[0, :3], dffn_s[1, :3]], axis=1)
    dffn_b = jnp.concatenate([dffn_s[0, 3:4], dffn_s[1, 3:4]], axis=1)
    fold = lambda a: sum8(a).reshape(N_HEADS, HEAD_DIM).sum(axis=0)[None]
    small_g_local = _pack_small(
        sum8(dg_norm1), sum8(dg_gate), dconv_s[:CONV_WIDTH], dconv_s[CONV_WIDTH:], sum8(dg_convnorm),
        fold(dg_q), fold(dg_k), sum8(dg_norm2), dffn_w, dffn_b,
        last_row=jnp.pad(loss_local.reshape(1, 1), ((0, 0), (0, D - 1))))
    sg_start = _small_start("small_grads_start", small_g_local)

    own, slots = {}, {}
    for tag, ex, names_ in (("ffn", ex_ffn, ("w_up", "w_down")),
                            ("proj", ex_proj, ("w_conv_out", "w_attn_out", "w_out"))):
        sent, landed = _exchange_wait("scatter_wait_" + tag, ex, sg_start[4])
        for n, src, land in zip(names_, sent, landed):
            own[n], slots[n] = src, land
    sent, landed = _exchange_wait("scatter_wait_in_a", ex_in_a[:2] + (ex_in_b[2], ex_in_b[3]) + ex_in_a[4:],
                                  sg_start[4])
    sent, landed = _exchange_wait("scatter_wait_in_b", ex_in_b[:2] + (sent, landed) + ex_in_b[4:], sg_start[4])
    own["w_in"], slots["w_in"] = sent[0], landed[0]

    res, adam_done = {}, []
    for n in order:
        w, m, v = big[n]
        outs = _adam_slots("adam_" + n, me.reshape(1), slots[n], own[n], w, m, v, _ADAM_TILE[slots[n].shape[1]],
                           transposed=n in ("w_in", "w_up"))
        adam_done.append(outs[0])
        res[n] = [a[None] for a in outs]
    small_g = _small_sum("small_grads", me.reshape(1), sg_start, adam_done)
    loss = small_g[_small_offsets()["last"], 0]

    col = lambda a, width: lax.dynamic_slice(a, (0, me * width), (a.shape[0], width))
    small_w_true = _pack_small(norm1_g, gate_b, conv_w_full, conv_b, conv_norm_g, q_norm_g, k_norm_g, norm2_g,
                               ffn_w_full, ffn_conv_b)
    place_m = lambda a, full: place_cols(a[0], full)
    small_m = _pack_small(m_norm1_g, m_gate_b, place_m(m_conv_w, D), m_conv_b, m_conv_norm_g, m_q_norm_g, m_k_norm_g,
                          m_norm2_g, place_m(m_ffn_conv_w, 2 * D_FF), m_ffn_conv_b)
    small_v = _pack_small(v_norm1_g, v_gate_b, place_m(v_conv_w, D), v_conv_b, v_conv_norm_g, v_q_norm_g, v_k_norm_g,
                          v_norm2_g, place_m(v_ffn_conv_w, 2 * D_FF), v_ffn_conv_b)
    sd, sm, sv = _adam_small(small_g, small_w_true, small_m, small_v)
    for i, packed in enumerate((small_g, sd, sm, sv)):
        u = _unpack_small(packed)
        u["conv_w"] = col(u["conv_w"], D // N_DEV)
        u["ffn_conv_w"] = col(u["ffn_conv_w"], 2 * D_FF // N_DEV)
        for n, a in u.items():
            res.setdefault(n, [None] * 4)[i] = a[None] if n in ("conv_w", "ffn_conv_w") else a

    names = ["norm1_g", "w_in", "gate_b", "conv_w", "conv_b", "conv_norm_g", "w_conv_out", "q_norm_g", "k_norm_g",
             "w_attn_out", "w_out", "norm2_g", "w_up", "ffn_conv_w", "ffn_conv_b", "w_down"]
    out = [loss, grad_x.reshape(BL, S, D)]
    for i in range(4):
        out += [res[n][i] for n in names]
    return tuple(out)
```

```python
import functools

import jax
import jax.numpy as jnp
import numpy as np
from jax import lax
from jax.experimental import pallas as pl
from jax.experimental.pallas import tpu as pltpu

f32 = jnp.float32
bf16 = jnp.bfloat16

D = 1024
N_HEADS = 16
HEAD_DIM = 64
CONV_WIDTH = 31
D_FF = 2816
GROUPS = ((128, 1), (512, 4), (2048, 16))
ATTN_BLOCK = 128
EPS = 1e-6
N_DEV = 8
MESH = pl.DeviceIdType.MESH

ADAM_LR = 0.001
ADAM_B1 = 0.9
ADAM_B2 = 0.999
ADAM_EPS = 1e-08
ADAM_WD = 0.01
ADAM_STEP = 10

VMEM_LIMIT = 56 * 1024 * 1024
MASK_BIAS = 1e30

Z_AVAL, Z_AGATE, Z_GA, Z_GB, Z_Q, Z_K, Z_V = 0, 1, 2, 3, 4, 5, 6


_W_OF_Z = (0, 1, 5, 6, 2, 3, 4)


def _wsec_of_zsec(j):
    return jnp.where(j < 2, j, jnp.where(j < 4, j + 3, j - 2))


def _zsec_of_wsec(w):
    return jnp.where(w < 2, w, jnp.where(w < 5, w + 2, w - 3))


def _sig(x):
    return 1.0 / (1.0 + jnp.exp(-x))


def _colsum8(x):
    return x.reshape(-1, 8, x.shape[-1]).sum(axis=0)


def _cparams(sem):
    return pltpu.CompilerParams(dimension_semantics=sem, vmem_limit_bytes=VMEM_LIMIT)


def _my_pos():
    x, y, c = lax.axis_index("x"), lax.axis_index("y"), lax.axis_index("c")
    return x, y, c, 4 * x + 2 * y + c


_DIMS = {"nn": ((1,), (0,)), "nt": ((1,), (1,)), "tn": ((0,), (0,))}


def _matmul_call(name, a, b, a_spec, b_spec, o_spec, out_shape, grid, mode, nk, tm, tn, after=None, fill=None):
    dims = (_DIMS[mode], ((), ()))
    extra = ([] if after is None else [after]) + ([] if fill is None else [fill])

    def body(a_ref, b_ref, *rest):
        o_ref, scratch = rest[len(extra)], rest[len(extra) + 1:]
        part = lax.dot_general(a_ref[...], b_ref[...], dims, preferred_element_type=f32)
        if nk == 1:
            o_ref[...] = part.astype(o_ref.dtype)
        else:
            acc = scratch[0]
            k = pl.program_id(2)

            @pl.when(k == 0)
            def _():
                acc[...] = part

            @pl.when(k > 0)
            def _():
                acc[...] += part

            @pl.when(k == nk - 1)
            def _():
                o_ref[...] = acc[...].astype(o_ref.dtype)

    scratch = [] if nk == 1 else [pltpu.VMEM((tm, tn), f32)]
    return pl.pallas_call(
        body, name=name, grid=grid, in_specs=[a_spec, b_spec] + [pl.BlockSpec(memory_space=pl.ANY)] * len(extra),
        out_specs=o_spec, out_shape=out_shape, input_output_aliases={} if fill is None else {1 + len(extra): 0},
        scratch_shapes=scratch, compiler_params=_cparams(("parallel", "parallel", "arbitrary")),
    )(a, b, *extra)


def _matmul(name, a, b, mode, out_dtype, tm=1024, tn=1024, tk=None, after=None):
    if mode == "nn":
        (M, K), (_, N) = a.shape, b.shape
    elif mode == "nt":
        (M, K), (N, _) = a.shape, b.shape
    else:
        (K, M), (_, N) = a.shape, b.shape
    tm, tn = min(tm, M), min(tn, N)
    tk = K if tk is None else tk
    nk = K // tk
    assert M % tm == 0 and N % tn == 0 and K % tk == 0
    if mode == "tn":
        a_spec = pl.BlockSpec((tk, tm), lambda i, j, k: (k, i))
    else:
        a_spec = pl.BlockSpec((tm, tk), lambda i, j, k: (i, k))
    if mode == "nt":
        b_spec = pl.BlockSpec((tn, tk), lambda i, j, k: (j, k))
    else:
        b_spec = pl.BlockSpec((tk, tn), lambda i, j, k: (k, j))
    o_spec = pl.BlockSpec((tm, tn), lambda i, j, k: (i, j))
    return _matmul_call(name, a, b, a_spec, b_spec, o_spec, jax.ShapeDtypeStruct((M, N), out_dtype),
                        (M // tm, N // tn, nk), mode, nk, tm, tn, after=after)


FTM = 512


def _matmul_fused(name, a, b, pairs, epilogue, extras, consts, outs, nt=False, sums=False, passed=(), aliases=None):
    sa, M, kk = a.shape
    na = max(i for i, _ in pairs) + 1
    ne, nc, npass = len(extras), len(consts), len(passed)
    dims = (_DIMS["nt" if nt else "nn"], ((), ()))

    def body(a_ref, b_ref, *rest):
        acc = None
        for i, j in pairs:
            part = lax.dot_general(a_ref[i], b_ref[j], dims, preferred_element_type=f32)
            acc = part if acc is None else acc + part
        epilogue(acc, rest[:ne], rest[ne:ne + nc], rest[ne + nc + npass:])

    whole = lambda arr: pl.BlockSpec(arr.shape, lambda i, nd=arr.ndim: (0,) * nd, pipeline_mode=pl.Buffered(1))
    io_alias = {2 + ne + nc + k: v for k, v in (aliases or {}).items()}
    return pl.pallas_call(
        body, name=name, grid=(M // FTM,),
        in_specs=[pl.BlockSpec((na, FTM, kk), lambda i: (0, i, 0)), whole(b)] + [s for _, s in extras]
        + [whole(c) for c in consts] + [pl.BlockSpec(memory_space=pl.ANY)] * npass,
        out_specs=[s for _, s in outs], out_shape=[s for s, _ in outs], input_output_aliases=io_alias,
        compiler_params=_cparams(("arbitrary" if sums else "parallel",)),
    )(a, b, *[x for x, _ in extras], *consts, *passed)


def _frows(c=D):
    return pl.BlockSpec((FTM, c), lambda i: (i, 0))


def _fsec(s):
    return pl.BlockSpec((None, FTM, D), lambda i: (s, i, 0))


def _rowshape(T, dtype, c=D):
    return (jax.ShapeDtypeStruct((T, c), dtype), _frows(c))


def _sumshape(c=D):
    return (jax.ShapeDtypeStruct((8, c), f32), pl.BlockSpec((8, c), lambda i: (0, 0)))


def _add_colsum(ref, x, cols=None):
    @pl.when(pl.program_id(0) == 0)
    def _():
        if cols is None:
            ref[...] = jnp.zeros_like(ref)
        else:
            ref[:, cols] = jnp.zeros((8, x.shape[-1]), f32)

    if cols is None:
        ref[...] += _colsum8(x)
    else:
        ref[:, cols] += _colsum8(x)


TT = 512


def _rows(c, cb=0, tt=TT):
    return pl.BlockSpec((tt, c), lambda i: (i, cb))


def _sec(s, tt=TT):
    return pl.BlockSpec((None, tt, D), lambda i: (s, i, 0))


def _const(shape):
    return pl.BlockSpec(shape, lambda i: (0,) * len(shape))


def _acc_spec(c):
    return pl.BlockSpec((8, c), lambda i: (0, 0))


def _rms(x):
    return lax.rsqrt(jnp.mean(x * x, axis=-1, keepdims=True) + EPS)


def _rms_bwd(dy_g, xn, rstd):
    return rstd * (dy_g - xn * jnp.mean(dy_g * xn, axis=-1, keepdims=True))


def _head_sum(x, bd):
    parts = []
    for cb in range(x.shape[-1] // 128):
        xb = x[:, cb * 128:(cb + 1) * 128]
        hi = xb.astype(bf16)
        lo = (xb - hi.astype(f32)).astype(bf16)
        parts.append(jnp.dot(hi, bd, preferred_element_type=f32) + jnp.dot(lo, bd, preferred_element_type=f32))
    return parts[0] if len(parts) == 1 else jnp.concatenate(parts, axis=1)


ZTM = 1024


def _in_proj_fwd(x, g, w_in_t, qg, kg, bd, after):
    T = x.shape[0]

    def body(x_ref, g_ref, w_ref, qg_ref, kg_ref, bd_ref, after_ref, z_ref, h_ref, qn_ref, kn_ref, hbuf):
        del after_ref
        j = pl.program_id(1)

        @pl.when(j == 0)
        def _():
            xv = x_ref[...]
            hv = (xv * _rms(xv) * g_ref[...]).astype(bf16)
            hbuf[...] = hv
            h_ref[...] = hv

        z = lax.dot_general(hbuf[...], w_ref[...], (_DIMS["nt"], ((), ())), preferred_element_type=f32)
        z_ref[...] = z

        def head_norm(gain_ref, scale):
            return z * lax.rsqrt(_head_sum(z * z, bd_ref[...]) * (1.0 / HEAD_DIM) + EPS) * gain_ref[...] * scale

        @pl.when(j == Z_Q)
        def _():
            qn_ref[...] = head_norm(qg_ref, HEAD_DIM ** -0.5)

        @pl.when(j == Z_K)
        def _():
            kn_ref[...] = head_norm(kg_ref, 1.0)

    tile = pl.BlockSpec((ZTM, D), lambda i, j: (i, 0))
    row = pl.BlockSpec((1, D), lambda i, j: (0, 0))
    return pl.pallas_call(
        body, name="mm_z", grid=(T // ZTM, 7),
        in_specs=[tile, row, pl.BlockSpec((D, D), lambda i, j: (_wsec_of_zsec(j), 0)), row, row,
                  pl.BlockSpec((128, 128), lambda i, j: (0, 0)), pl.BlockSpec(memory_space=pl.ANY)],
        out_specs=[pl.BlockSpec((None, ZTM, D), lambda i, j: (j, i, 0)), tile, tile, tile],
        out_shape=[jax.ShapeDtypeStruct((8, T, D), f32), jax.ShapeDtypeStruct((T, D), bf16),
                   jax.ShapeDtypeStruct((T, D), f32), jax.ShapeDtypeStruct((T, D), f32)],
        scratch_shapes=[pltpu.VMEM((ZTM, D), bf16)],
        compiler_params=_cparams(("parallel", "arbitrary")))(x, g, w_in_t, qg, kg, bd, after)


def _branches_fwd(c, ob, z8, g, gate_b, w_conv_out, w_attn_out):
    T = c.shape[0]

    def epilogue(yb, extra, const, out):
        cv = extra[0][...]
        r = cv * _rms(cv) * const[0][...]
        s = (r * _sig(r)).astype(bf16)
        ya = jnp.dot(s, const[2][...], preferred_element_type=f32)
        b_ref = const[1]
        g_a = _sig(extra[1][...] + b_ref[:, :D])
        g_b = _sig(extra[2][...] + b_ref[:, D:])
        out[0][...] = s
        out[1][...] = ya
        out[2][...] = yb
        out[3][...] = (g_a * ya + g_b * yb).astype(bf16)

    return _matmul_fused("mm_branches", ob[None], w_attn_out[None], ((0, 0),), epilogue,
                         [(c, _frows()), (z8, _fsec(Z_GA)), (z8, _fsec(Z_GB))], [g, gate_b, w_conv_out],
                         [_rowshape(T, bf16), _rowshape(T, f32), _rowshape(T, f32), _rowshape(T, bf16)])


def _out_norm2_fwd(mixed, w_out, x, g):
    T = x.shape[0]

    def epilogue(acc, extra, const, out):
        x1 = extra[0][...] + acc
        out[0][...] = x1
        out[1][...] = (x1 * _rms(x1) * const[0][...]).astype(bf16)

    return _matmul_fused("mm_t1_norm2", mixed[None], w_out[None], ((0, 0),), epilogue, [(x, _frows())], [g],
                         [_rowshape(T, f32), _rowshape(T, bf16)])


def _down_loss_fwd(f, w_down, x1, target):
    T = x1.shape[0]

    def epilogue(acc, extra, const, out):
        diff = extra[0][...] + acc - extra[1][...]
        dy = diff * (1.0 / D)
        out[0][...] = dy
        out[1][...] = dy.astype(bf16)
        _add_colsum(out[2], diff * diff)

    return _matmul_fused("mm_t2_loss", f[None], w_down[None], ((0, 0),), epilogue, [(x1, _frows()), (target, _frows())],
                         [], [_rowshape(T, f32), _rowshape(T, bf16), _sumshape()], sums=True)


def _up_norm2_bwd(du3, w_up_t, x1, dy, g, token):
    T = x1.shape[0]

    def epilogue(dh, extra, const, out):
        x1v = extra[0][...]
        rstd = _rms(x1v)
        xn = x1v * rstd
        dx1 = extra[1][...] + _rms_bwd(dh * const[0][...], xn, rstd)
        out[0][...] = dx1
        out[1][...] = dx1.astype(bf16)
        _add_colsum(out[2], dh * xn)

    return _matmul_fused("mm_dh2_norm2", du3, w_up_t.reshape(2, D_FF, D), ((0, 0), (1, 1)), epilogue,
                         [(x1, _frows()), (dy, _frows())], [g],
                         [_rowshape(T, f32), _rowshape(T, bf16), _sumshape()], sums=True, passed=[token])


def _out_gate_bwd(dx1b, w_out, z8, gate_b, ya, yb, dz8):
    T = ya.shape[0]

    def epilogue(dm, extra, const, out):
        b_ref = const[0]
        g_a = _sig(extra[0][...] + b_ref[:, :D])
        g_b = _sig(extra[1][...] + b_ref[:, D:])
        out[0][...] = (dm * g_a).astype(bf16)
        out[1][...] = (dm * g_b).astype(bf16)
        dla = dm * extra[2][...] * g_a * (1.0 - g_a)
        dlb = dm * extra[3][...] * g_b * (1.0 - g_b)
        out[2][0] = dla.astype(bf16)
        out[2][1] = dlb.astype(bf16)
        _add_colsum(out[3], dla, slice(0, D))
        _add_colsum(out[3], dlb, slice(D, 2 * D))

    return _matmul_fused(
        "mm_dmixed_gate", dx1b[None], w_out[None], ((0, 0),), epilogue,
        [(z8, _fsec(Z_GA)), (z8, _fsec(Z_GB)), (ya, _frows()), (yb, _frows())], [gate_b],
        [_rowshape(T, bf16), _rowshape(T, bf16),
         (jax.ShapeDtypeStruct(dz8.shape, bf16), pl.BlockSpec((2, FTM, D), lambda i: (1, i, 0))), _sumshape(2 * D)],
        nt=True, sums=True, passed=[dz8], aliases={0: 2})


def _convnorm_bwd(dya, w_conv_out, c, g):
    T = c.shape[0]

    def epilogue(ds, extra, const, out):
        cv = extra[0][...]
        rstd = _rms(cv)
        r0 = cv * rstd
        gv = const[0][...]
        r = r0 * gv
        sg = _sig(r)
        dr = ds * sg * (1.0 + r * (1.0 - sg))
        out[0][...] = _rms_bwd(dr * gv, r0, rstd)
        _add_colsum(out[1], dr * r0)

    return _matmul_fused("mm_ds_convnorm", dya[None], w_conv_out[None], ((0, 0),), epilogue, [(c, _frows())], [g],
                         [_rowshape(T, f32), _sumshape()], nt=True, sums=True)


def _qk_bwd(z8, dqn, dkn, dv, qg, kg, bd, dz8):
    T = dqn.shape[0]

    def body(q_ref, k_ref, dqn_ref, dkn_ref, dv_ref, qg_ref, kg_ref, bd_ref, dz_in, dz_ref, dqg_ref, dkg_ref):
        del dz_in
        bdv = bd_ref[...]

        @pl.when(pl.program_id(0) == 0)
        def _():
            dqg_ref[...] = jnp.zeros_like(dqg_ref)
            dkg_ref[...] = jnp.zeros_like(dkg_ref)

        def one(raw, dn_scaled, g, dg_ref, sec):
            rstd = lax.rsqrt(_head_sum(raw * raw, bdv) * (1.0 / HEAD_DIM) + EPS)
            n = raw * rstd
            dg_ref[...] += _colsum8(dn_scaled * n)
            dn = dn_scaled * g
            draw = rstd * (dn - n * (_head_sum(dn * n, bdv) * (1.0 / HEAD_DIM)))
            dz_ref[sec] = draw.astype(bf16)

        one(q_ref[...], dqn_ref[...] * (HEAD_DIM ** -0.5), qg_ref[...], dqg_ref, 0)
        one(k_ref[...], dkn_ref[...], kg_ref[...], dkg_ref, 1)
        dz_ref[2] = dv_ref[...].astype(bf16)
        dz_ref[3] = jnp.zeros((TT, D), bf16)

    return pl.pallas_call(
        body, name="qk_bwd", grid=(T // TT,),
        in_specs=[_sec(Z_Q), _sec(Z_K), _rows(D), _rows(D), _rows(D), _const((1, D)), _const((1, D)),
                  _const((128, 128)), pl.BlockSpec(memory_space=pl.ANY)],
        out_specs=[pl.BlockSpec((4, TT, D), lambda i: (1, i, 0)), _acc_spec(D), _acc_spec(D)],
        out_shape=[jax.ShapeDtypeStruct(dz8.shape, bf16), jax.ShapeDtypeStruct((8, D), f32),
                   jax.ShapeDtypeStruct((8, D), f32)],
        input_output_aliases={8: 0},
        compiler_params=_cparams(("arbitrary",)))(z8, z8, dqn, dkn, dv, qg, kg, bd, dz8)


def _in_norm1_bwd(dz8, w_in_t, x, dx1, g, token):
    T = x.shape[0]

    def epilogue(dh, extra, const, out):
        xv = extra[0][...]
        rstd = _rms(xv)
        xn = xv * rstd
        out[0][...] = extra[1][...] + _rms_bwd(dh * const[0][...], xn, rstd)
        _add_colsum(out[1], dh * xn)

    return _matmul_fused("mm_dh_norm1", dz8, w_in_t.reshape(7, D, D), tuple(zip(range(7), _W_OF_Z)), epilogue,
                         [(x, _frows()), (dx1, _frows())], [g], [_rowshape(T, f32), _sumshape()],
                         sums=True, passed=[token])


CCW = 256
CR = 64
HALO = 32


def _conv_fwd(z8, conv_w, conv_b, S):
    T = z8.shape[1]
    nb = T // S
    ncb = D // CCW

    def body(av_ref, ag_ref, w_ref, b_ref, c_ref, pad):
        pad[0:HALO, :] = jnp.zeros((HALO, CCW), f32)

        def fill(i, carry):
            r0 = pl.multiple_of(i * 256, 256)
            pad[pl.ds(HALO + r0, 256), :] = av_ref[pl.ds(r0, 256), :] * _sig(ag_ref[pl.ds(r0, 256), :])
            return carry

        lax.fori_loop(0, S // 256, fill, 0)
        bias = b_ref[...]

        def chunk(i, carry):
            r0 = pl.multiple_of(i * CR, CR)
            win = pad[pl.ds(r0, CR + HALO), :]
            acc = jnp.zeros((CR, CCW), f32) + bias
            for s in range(8):
                part = None
                for m in range((CONV_WIDTH - 1 - s) // 8 + 1):
                    j = CONV_WIDTH - 1 - 8 * m - s
                    term = win[24 - 8 * m:24 - 8 * m + CR + 8, :] * w_ref[j:j + 1, :]
                    part = term if part is None else part + term
                acc = acc + part[8 - s:8 - s + CR, :]
            c_ref[pl.ds(r0, CR), :] = acc
            return carry

        lax.fori_loop(0, S // CR, chunk, 0)

    zs = lambda s: pl.BlockSpec((None, S, CCW), lambda b, cb: (s, b, cb))
    return pl.pallas_call(
        body, name="conv_fwd", grid=(nb, ncb),
        in_specs=[zs(Z_AVAL), zs(Z_AGATE), pl.BlockSpec((CONV_WIDTH, CCW), lambda b, cb: (0, cb)),
                  pl.BlockSpec((1, CCW), lambda b, cb: (0, cb))],
        out_specs=pl.BlockSpec((S, CCW), lambda b, cb: (b, cb)),
        out_shape=jax.ShapeDtypeStruct((T, D), f32),
        scratch_shapes=[pltpu.VMEM((S + HALO, CCW), f32)],
        compiler_params=_cparams(("parallel", "parallel")))(z8, z8, conv_w, conv_b)


def _conv_bwd(dc, z8, conv_w, dz8, S):
    T = dc.shape[0]
    nb = T // S
    ncb = D // CCW

    def body(dc_ref, av_ref, ag_ref, w_ref, dz_in, dz_ref, dw_ref, apad, dpad, shbuf):
        del dz_in
        apad[0:HALO, :] = jnp.zeros((HALO, CCW), f32)
        dpad[S:S + HALO, :] = jnp.zeros((HALO, CCW), f32)
        dw_ref[...] = jnp.zeros_like(dw_ref)

        def fill(i, carry):
            r0 = pl.multiple_of(i * 256, 256)
            apad[pl.ds(HALO + r0, 256), :] = av_ref[pl.ds(r0, 256), :] * _sig(ag_ref[pl.ds(r0, 256), :])
            dpad[pl.ds(r0, 256), :] = dc_ref[pl.ds(r0, 256), :]
            return carry

        lax.fori_loop(0, S // 256, fill, 0)

        def chunk(i, carry):
            r0 = pl.multiple_of(i * CR, CR)
            dwin = dpad[pl.ds(r0, CR + HALO), :]
            da = jnp.zeros((CR, CCW), f32)
            for s in range(8):
                shbuf[...] = dwin[s:s + CR, :]
                dshift = shbuf[...]
                part = None
                for m in range((CONV_WIDTH - 1 - s) // 8 + 1):
                    j = CONV_WIDTH - 1 - 8 * m - s
                    term = dwin[8 * m:8 * m + CR + 8, :] * w_ref[j:j + 1, :]
                    part = term if part is None else part + term
                    a_lag = apad[pl.ds(r0 + HALO - 8 * m, CR), :]
                    dw_ref[8 * j:8 * j + 8, :] += _colsum8(dshift * a_lag)
                da = da + part[s:s + CR, :]
            dw_ref[8 * CONV_WIDTH:8 * CONV_WIDTH + 8, :] += _colsum8(dwin[0:CR, :])
            av = av_ref[pl.ds(r0, CR), :]
            sg = _sig(ag_ref[pl.ds(r0, CR), :])
            dz_ref[0, pl.ds(r0, CR), :] = (da * sg).astype(bf16)
            dz_ref[1, pl.ds(r0, CR), :] = (da * av * sg * (1.0 - sg)).astype(bf16)
            return carry

        lax.fori_loop(0, S // CR, chunk, 0)

    zs = lambda s: pl.BlockSpec((None, S, CCW), lambda b, cb: (s, b, cb))
    return pl.pallas_call(
        body, name="conv_bwd", grid=(nb, ncb),
        in_specs=[pl.BlockSpec((S, CCW), lambda b, cb: (b, cb)), zs(Z_AVAL), zs(Z_AGATE),
                  pl.BlockSpec((CONV_WIDTH, CCW), lambda b, cb: (0, cb)), pl.BlockSpec(memory_space=pl.ANY)],
        out_specs=[pl.BlockSpec((2, S, CCW), lambda b, cb: (0, b, cb)),
                   pl.BlockSpec((None, 256, CCW), lambda b, cb: (b, 0, cb))],
        out_shape=[jax.ShapeDtypeStruct(dz8.shape, bf16), jax.ShapeDtypeStruct((nb, 256, D), f32)],
        input_output_aliases={4: 0},
        scratch_shapes=[pltpu.VMEM((S + HALO, CCW), f32), pltpu.VMEM((S + HALO, CCW), f32),
                        pltpu.VMEM((CR, CCW), f32)],
        compiler_params=_cparams(("parallel", "parallel")))(dc, z8, z8, conv_w, dz8)


FR = 128
NFB = D_FF // CCW


def _ffn_window(ref, i, r0):
    return ref[pl.ds(r0 - 8, FR + 8), :]


def _ffn_u(win, w_ref, b_ref):
    return (win[6:6 + FR, :] * w_ref[0:1, :] + win[7:7 + FR, :] * w_ref[1:2, :]
            + win[8:8 + FR, :] * w_ref[2:3, :] + b_ref[...])


def _ffn_fwd(u3, ffn_w, ffn_b, S):
    T = u3.shape[1]
    nb = T // S

    def body(uv_ref, ug_ref, wv_ref, wg_ref, bv_ref, bg_ref, f_ref):
        def chunk(first, i):
            r0 = 0 if first else pl.multiple_of(i * FR, FR)
            if first:
                z = jnp.zeros((8, CCW), f32)
                wv = jnp.concatenate([z, uv_ref[0:FR, :]], axis=0)
                wg = jnp.concatenate([z, ug_ref[0:FR, :]], axis=0)
            else:
                wv = _ffn_window(uv_ref, i, r0)
                wg = _ffn_window(ug_ref, i, r0)
            u_val = _ffn_u(wv, wv_ref, bv_ref)
            u_gate = _ffn_u(wg, wg_ref, bg_ref)
            f_ref[pl.ds(r0, FR), :] = (u_gate * _sig(u_gate) * u_val).astype(bf16)

        chunk(True, 0)

        def loop(i, carry):
            chunk(False, i)
            return carry

        lax.fori_loop(1, S // FR, loop, 0)

    us = lambda h: pl.BlockSpec((None, S, CCW), lambda b, cb: (h, b, cb))
    ws = lambda h: pl.BlockSpec((3, CCW), lambda b, cb: (0, h * NFB + cb))
    bs = lambda h: pl.BlockSpec((1, CCW), lambda b, cb: (0, h * NFB + cb))
    return pl.pallas_call(
        body, name="ffn_fwd", grid=(nb, NFB),
        in_specs=[us(0), us(1), ws(0), ws(1), bs(0), bs(1)],
        out_specs=pl.BlockSpec((S, CCW), lambda b, cb: (b, cb)),
        out_shape=jax.ShapeDtypeStruct((T, D_FF), bf16),
        compiler_params=_cparams(("parallel", "parallel")))(u3, u3, ffn_w, ffn_w, ffn_b, ffn_b)


def _ffn_bwd(u3, df, ffn_w, ffn_b, S):
    T = u3.shape[1]
    nb = T // S

    def body(uv_ref, ug_ref, df_ref, wv_ref, wg_ref, bv_ref, bg_ref, du_ref, dw_ref, dvpad, dgpad, shbuf):
        dvpad[S:S + 8, :] = jnp.zeros((8, CCW), f32)
        dgpad[S:S + 8, :] = jnp.zeros((8, CCW), f32)
        dw_ref[...] = jnp.zeros_like(dw_ref)

        def chunk(first, i):
            r0 = 0 if first else pl.multiple_of(i * FR, FR)
            if first:
                z = jnp.zeros((8, CCW), f32)
                wv = jnp.concatenate([z, uv_ref[0:FR, :]], axis=0)
                wg = jnp.concatenate([z, ug_ref[0:FR, :]], axis=0)
            else:
                wv = _ffn_window(uv_ref, i, r0)
                wg = _ffn_window(ug_ref, i, r0)
            taps = []
            for h, win in enumerate((wv, wg)):
                shbuf[2 * h] = win[6:6 + FR, :]
                shbuf[2 * h + 1] = win[7:7 + FR, :]
                taps.append((shbuf[2 * h], shbuf[2 * h + 1], win[8:8 + FR, :]))
            conv = lambda x, w_ref, b_ref: (x[0] * w_ref[0:1, :] + x[1] * w_ref[1:2, :] + x[2] * w_ref[2:3, :]
                                            + b_ref[...])
            u_val = conv(taps[0], wv_ref, bv_ref)
            u_gate = conv(taps[1], wg_ref, bg_ref)
            dfc = df_ref[pl.ds(r0, FR), :]
            sg = _sig(u_gate)
            d_val = dfc * u_gate * sg
            d_gate = dfc * u_val * sg * (1.0 + u_gate * (1.0 - sg))
            dvpad[pl.ds(r0, FR), :] = d_val
            dgpad[pl.ds(r0, FR), :] = d_gate
            for h, dd in enumerate((d_val, d_gate)):
                for j in range(3):
                    dw_ref[h, 8 * j:8 * j + 8, :] += _colsum8(dd * taps[h][j])
                dw_ref[h, 24:32, :] += _colsum8(dd)

        chunk(True, 0)

        def loop(i, carry):
            chunk(False, i)
            return carry

        lax.fori_loop(1, S // FR, loop, 0)

        def back(i, carry):
            r0 = pl.multiple_of(i * FR, FR)
            for h, (dpad, w_ref) in enumerate(((dvpad, wv_ref), (dgpad, wg_ref))):
                win = dpad[pl.ds(r0, FR + 8), :]
                du = (win[0:FR, :] * w_ref[2:3, :] + win[1:1 + FR, :] * w_ref[1:2, :]
                      + win[2:2 + FR, :] * w_ref[0:1, :])
                du_ref[h, pl.ds(r0, FR), :] = du.astype(bf16)
            return carry

        lax.fori_loop(0, S // FR, back, 0)

    us = lambda h: pl.BlockSpec((None, S, CCW), lambda b, cb: (h, b, cb))
    ws = lambda h: pl.BlockSpec((3, CCW), lambda b, cb: (0, h * NFB + cb))
    bs = lambda h: pl.BlockSpec((1, CCW), lambda b, cb: (0, h * NFB + cb))
    return pl.pallas_call(
        body, name="ffn_bwd", grid=(nb, NFB),
        in_specs=[us(0), us(1), pl.BlockSpec((S, CCW), lambda b, cb: (b, cb)), ws(0), ws(1), bs(0), bs(1)],
        out_specs=[pl.BlockSpec((2, S, CCW), lambda b, cb: (0, b, cb)),
                   pl.BlockSpec((None, 2, 32, CCW), lambda b, cb: (b, 0, 0, cb))],
        out_shape=[jax.ShapeDtypeStruct((2, T, D_FF), bf16), jax.ShapeDtypeStruct((nb, 2, 32, D_FF), f32)],
        scratch_shapes=[pltpu.VMEM((S + 8, CCW), f32), pltpu.VMEM((S + 8, CCW), f32),
                        pltpu.VMEM((4, FR, CCW), f32)],
        compiler_params=_cparams(("parallel", "parallel")))(u3, u3, df, ffn_w, ffn_w, ffn_b, ffn_b)


AB = ATTN_BLOCK


def _attn_bias_np():
    slopes = (np.float32(2.0) ** (np.float32(-8.0) * np.arange(1, N_HEADS + 1, dtype=np.float32)
                                  / np.float32(N_HEADS))).astype(np.float32)
    steps = (np.arange(AB)[:, None] + AB) - np.arange(2 * AB)[None, :]
    own = (np.arange(2 * AB) >= AB)[None, :]
    out = []
    for window, dil in GROUPS:
        valid = (steps >= 0) & (steps <= window // dil)
        dist = slopes[:, None, None] * (steps * dil).astype(np.float32)[None]
        kinds = [np.where(v[None], dist, np.float32(MASK_BIAS)) for v in (valid, valid & own)]
        out.append(np.stack(kinds, axis=1))
    return np.stack(out).astype(np.float32)


def _attn_bias():
    return jnp.asarray(_attn_bias_np())


def _head_masks():
    lane = lax.broadcasted_iota(jnp.int32, (1, 128), 1)
    return (lane < HEAD_DIM, lane >= HEAD_DIM)


def _perm_chunks(S, d):
    L = S // d
    ch = min(L, 256)
    out = []
    for r in range(d):
        for c in range(L // ch):
            start = r + d * ch * c
            out.append((pl.ds(start, ch, stride=d) if d > 1 else pl.ds(start, ch), r * L + c * ch, ch))
    return out


def _stack_heads(x, masks):
    return jnp.concatenate([jnp.where(masks[0], x, 0), jnp.where(masks[1], x, 0)], axis=0)


def _block_row(j):
    return j * AB if isinstance(j, int) else pl.multiple_of(j * AB, AB)


def _three_stages(n, stage_a, stage_b, stage_c, unroll):
    stage_a(0)
    stage_a(1)
    stage_b(0)

    def body(j, carry):
        stage_c(j - 1)
        stage_b(j)
        stage_a(j + 1)
        return carry

    lax.fori_loop(1, n - 1, body, 0, unroll=unroll)
    stage_c(n - 2)
    stage_b(n - 1)
    stage_c(n - 1)


_NT = (((1,), (1,)), ((), ()))
_TN = (((0,), (0,)), ((), ()))
SCH = 64


def _attn_fwd(qn, kn, z8, bias, S):
    T = qn.shape[0]
    nb = T // S
    nblk = S // AB

    def body(q_ref, k_ref, v_ref, bias_ref, o_ref, ob_ref, lse_ref, qs, ks, vs, s2, p2, ogp, lgp, *group_scratch):
        og, lg = group_scratch[:3], group_scratch[3:]
        masks = _head_masks()
        ks[0:AB, :] = jnp.zeros((AB, 128), bf16)
        vs[0:AB, :] = jnp.zeros((AB, 128), bf16)

        for g, (_, d) in enumerate(GROUPS):
            nsub = S // (d * AB)
            chunks = _perm_chunks(S, d)
            for src, dst, ch in chunks:
                qs[dst:dst + ch, :] = q_ref[src, :].astype(bf16)
                ks[AB + dst:AB + dst + ch, :] = k_ref[src, :].astype(bf16)
                vs[AB + dst:AB + dst + ch, :] = v_ref[src, :].astype(bf16)
            od, ld = (og[g], lg[g]) if d == 1 else (ogp, lgp)

            def scores(j):
                r0 = _block_row(j)
                q2 = _stack_heads(qs[pl.ds(r0, AB), :], masks)
                s2[j] = lax.dot_general(q2, ks[pl.ds(r0, 2 * AB), :], _NT, preferred_element_type=f32)

            def softmax(j, g=g, nsub=nsub, ld=ld):
                r0 = _block_row(j)
                kind = int(j % nsub == 0) if isinstance(j, int) else (j % nsub == 0).astype(jnp.int32)
                for cc in range(AB // SCH):
                    lses = []
                    for hh in range(2):
                        rows = pl.ds(hh * AB + cc * SCH, SCH)
                        sb = s2[j, rows, :] - bias_ref[g, hh, kind, cc * SCH:(cc + 1) * SCH, :]
                        m = jnp.max(sb, axis=-1, keepdims=True)
                        p = jnp.exp(sb - m)
                        den = jnp.sum(p, axis=-1, keepdims=True)
                        p2[j, rows, :] = (p * (1.0 / den)).astype(bf16)
                        lses.append(m + jnp.log(den))
                    ld[pl.ds(r0 + cc * SCH, SCH), :] = jnp.where(masks[0], lses[0], lses[1])

            def values(j, od=od):
                r0 = _block_row(j)
                pv2 = jnp.dot(p2[j], vs[pl.ds(r0, 2 * AB), :], preferred_element_type=f32)
                od[pl.ds(r0, AB), :] = jnp.where(masks[0], pv2[:AB], pv2[AB:])

            _three_stages(nblk, scores, softmax, values, nblk - 2)

            if d > 1:
                for src, dst, ch in chunks:
                    og[g][src, :] = ogp[dst:dst + ch, :]
                    lg[g][src, :] = lgp[dst:dst + ch, :]

        def combine(i, carry):
            rr = pl.ds(pl.multiple_of(i * 256, 256), 256)
            l0, l1, l2 = lg[0][rr, :], lg[1][rr, :], lg[2][rr, :]
            mx = jnp.maximum(jnp.maximum(l0, l1), l2)
            e0, e1, e2 = jnp.exp(l0 - mx), jnp.exp(l1 - mx), jnp.exp(l2 - mx)
            den = e0 + e1 + e2
            o = (e0 * og[0][rr, :] + e1 * og[1][rr, :] + e2 * og[2][rr, :]) / den
            o_ref[rr, :] = o
            ob_ref[rr, :] = o.astype(bf16)
            lse_ref[rr, :] = mx + jnp.log(den)
            return carry

        lax.fori_loop(0, S // 256, combine, 0)

    blk = pl.BlockSpec((S, 128), lambda b, hp: (b, hp))
    return pl.pallas_call(
        body, name="attn_fwd", grid=(nb, N_HEADS // 2),
        in_specs=[blk, blk, pl.BlockSpec((None, S, 128), lambda b, hp: (Z_V, b, hp)),
                  pl.BlockSpec((3, 2, 2, AB, 2 * AB), lambda b, hp: (0, hp, 0, 0, 0))],
        out_specs=[blk, blk, blk],
        out_shape=[jax.ShapeDtypeStruct((T, D), f32), jax.ShapeDtypeStruct((T, D), bf16),
                   jax.ShapeDtypeStruct((T, D), f32)],
        scratch_shapes=[pltpu.VMEM((S, 128), bf16), pltpu.VMEM((S + AB, 128), bf16), pltpu.VMEM((S + AB, 128), bf16),
                        pltpu.VMEM((nblk, 2 * AB, 2 * AB), f32), pltpu.VMEM((nblk, 2 * AB, 2 * AB), bf16),
                        pltpu.VMEM((S, 128), f32), pltpu.VMEM((S, 128), f32)] + [pltpu.VMEM((S, 128), f32)] * 6,
        compiler_params=_cparams(("parallel", "parallel")))(qn, kn, z8, bias)


def _attn_bwd(qn, kn, z8, do, o, lse, bias, bd, S, after):
    T = qn.shape[0]
    nb = T // S

    nblk = S // AB

    def body(q_ref, k_ref, v_ref, do_ref, o_ref, lse_ref, bias_ref, bd_ref, after_ref, dq_ref, dk_ref, dv_ref,
             delta, qs, ks, vs, dos, lsp, dlp, s2, dp2, p2, ds2, dqp, dkp, dvp):
        del after_ref
        masks = _head_masks()
        bdv = bd_ref[...]
        dq_ref[...] = jnp.zeros_like(dq_ref)
        dk_ref[...] = jnp.zeros_like(dk_ref)
        dv_ref[...] = jnp.zeros_like(dv_ref)
        ks[0:AB, :] = jnp.zeros((AB, 128), bf16)
        vs[0:AB, :] = jnp.zeros((AB, 128), bf16)

        def prep(i, carry):
            rr = pl.ds(pl.multiple_of(i * 256, 256), 256)
            delta[rr, :] = _head_sum(do_ref[rr, :] * o_ref[rr, :], bdv)
            return carry

        lax.fori_loop(0, S // 256, prep, 0, unroll=True)

        for g, (_, d) in enumerate(GROUPS):
            nsub = S // (d * AB)
            chunks = _perm_chunks(S, d)
            for src, dst, ch in chunks:
                qs[dst:dst + ch, :] = q_ref[src, :].astype(bf16)
                ks[AB + dst:AB + dst + ch, :] = k_ref[src, :].astype(bf16)
                vs[AB + dst:AB + dst + ch, :] = v_ref[src, :].astype(bf16)
                dos[dst:dst + ch, :] = do_ref[src, :].astype(bf16)
                lsp[dst:dst + ch, :] = lse_ref[src, :]
                dlp[dst:dst + ch, :] = delta[src, :]
            dkp[...] = jnp.zeros_like(dkp)
            dvp[...] = jnp.zeros_like(dvp)

            def scores(j):
                r0 = _block_row(j)
                q2 = _stack_heads(qs[pl.ds(r0, AB), :], masks)
                do2 = _stack_heads(dos[pl.ds(r0, AB), :], masks)
                s2[j] = lax.dot_general(q2, ks[pl.ds(r0, 2 * AB), :], _NT, preferred_element_type=f32)
                dp2[j] = lax.dot_general(do2, vs[pl.ds(r0, 2 * AB), :], _NT, preferred_element_type=f32)

            def probs(j, g=g, nsub=nsub):
                r0 = _block_row(j)
                kind = int(j % nsub == 0) if isinstance(j, int) else (j % nsub == 0).astype(jnp.int32)
                for cc in range(AB // SCH):
                    lse_c = lsp[pl.ds(r0 + cc * SCH, SCH), :]
                    del_c = dlp[pl.ds(r0 + cc * SCH, SCH), :]
                    for hh in range(2):
                        c0 = hh * HEAD_DIM
                        rows = pl.ds(hh * AB + cc * SCH, SCH)
                        sb = s2[j, rows, :] - bias_ref[g, hh, kind, cc * SCH:(cc + 1) * SCH, :]
                        p = jnp.exp(sb - lse_c[:, c0:c0 + 1])
                        p2[j, rows, :] = p.astype(bf16)
                        ds2[j, rows, :] = (p * (dp2[j, rows, :] - del_c[:, c0:c0 + 1])).astype(bf16)

            def grads(j):
                r0 = _block_row(j)
                q2 = _stack_heads(qs[pl.ds(r0, AB), :], masks)
                do2 = _stack_heads(dos[pl.ds(r0, AB), :], masks)
                dsb = ds2[j]
                t = jnp.dot(dsb, ks[pl.ds(r0, 2 * AB), :], preferred_element_type=f32)
                dqp[pl.ds(r0, AB), :] = jnp.where(masks[0], t[:AB], t[AB:])
                dkp[pl.ds(r0, 2 * AB), :] += lax.dot_general(dsb, q2, _TN, preferred_element_type=f32)
                dvp[pl.ds(r0, 2 * AB), :] += lax.dot_general(p2[j], do2, _TN, preferred_element_type=f32)

            _three_stages(nblk, scores, probs, grads, nblk - 2)

            for src, dst, ch in chunks:
                dq_ref[src, :] += dqp[dst:dst + ch, :]
                dk_ref[src, :] += dkp[AB + dst:AB + dst + ch, :]
                dv_ref[src, :] += dvp[AB + dst:AB + dst + ch, :]

    blk = pl.BlockSpec((S, 128), lambda b, hp: (b, hp))
    row = lambda dt, pad=0: pltpu.VMEM((S + pad, 128), dt)
    blocks = lambda dt: pltpu.VMEM((nblk, 2 * AB, 2 * AB), dt)
    return pl.pallas_call(
        body, name="attn_bwd", grid=(nb, N_HEADS // 2),
        in_specs=[blk, blk, pl.BlockSpec((None, S, 128), lambda b, hp: (Z_V, b, hp)), blk, blk, blk,
                  pl.BlockSpec((3, 2, 2, AB, 2 * AB), lambda b, hp: (0, hp, 0, 0, 0)),
                  pl.BlockSpec((128, 128), lambda b, hp: (0, 0)), pl.BlockSpec(memory_space=pl.ANY)],
        out_specs=[blk, blk, blk],
        out_shape=[jax.ShapeDtypeStruct((T, D), f32)] * 3,
        scratch_shapes=[row(f32), row(bf16), row(bf16, AB), row(bf16, AB), row(bf16), row(f32), row(f32),
                        blocks(f32), blocks(f32), blocks(bf16), blocks(bf16), row(f32), row(f32, AB), row(f32, AB)],
        compiler_params=_cparams(("parallel", "parallel")))(qn, kn, z8, do, o, lse, bias, bd, after)


def _any_spec():
    return pl.BlockSpec(memory_space=pl.ANY)


AG_CHUNKS = 4


def _allgather_rows(shards, n_full):
    n = len(shards)
    parts = [(a, q) for a in range(n_full) for q in range(AG_CHUNKS)]

    def body(*refs):
        ins, outs = refs[:n], refs[n:2 * n]
        send_sems, recv_sems, local_sems = refs[2 * n:]
        x, y, c, me = _my_pos()
        sibling = (x, y, 1 - c)
        chips = [(1 - x, y), (x, 1 - y), (1 - x, 1 - y)]

        def idx(px, py, pc):
            return 4 * px + 2 * py + pc

        def copy(v, k, blk, to, own=False):
            a, q = parts[v]
            rows = pl.ds(q * (shards[a].shape[0] // AG_CHUNKS), shards[a].shape[0] // AG_CHUNKS)
            return pltpu.make_async_remote_copy(
                src_ref=ins[a].at[rows] if own else outs[a].at[blk, rows], dst_ref=outs[a].at[blk, rows],
                send_sem=send_sems.at[v, k], recv_sem=recv_sems.at[v, k], device_id=to, device_id_type=MESH)

        mine = [pltpu.make_async_copy(ins[a], outs[a].at[me], local_sems.at[a]) for a in range(n)]
        for cp in mine:
            cp.start()
        first = []
        for v in range(len(parts)):
            first.append(copy(v, 0, me, sibling, own=True))
            first += [copy(v, 1 + j, me, (*chip, c), own=True) for j, chip in enumerate(chips[:2])]
        for cp in first:
            cp.start()
        relay_blk = jnp.where(c == 1, idx(1 - x, y, c), idx(x, 1 - y, c))
        relay_to = (jnp.where(c == 1, x, 1 - x), jnp.where(c == 1, 1 - y, y), c)
        passed = []
        for v in range(len(parts)):
            for j, chip in enumerate(chips[:2]):
                copy(v, 1 + j, idx(*chip, c), (x, y, c)).wait_recv()
            cp = copy(v, 3, relay_blk, relay_to)
            cp.start()
            passed.append(cp)
            for j, chip in enumerate(chips):
                if j == 2:
                    copy(v, 3, idx(*chip, c), (x, y, c)).wait_recv()
                cp = copy(v, 4 + j, idx(*chip, c), sibling)
                cp.start()
                passed.append(cp)
        for v in range(len(parts)):
            copy(v, 0, idx(x, y, 1 - c), (x, y, c)).wait_recv()
            for j, chip in enumerate(chips):
                copy(v, 4 + j, idx(*chip, 1 - c), (x, y, c)).wait_recv()
        for cp in first + passed:
            cp.wait_send()
        for cp in mine:
            cp.wait()

    return pl.pallas_call(
        body, name="allgather_weights",
        in_specs=[_any_spec()] * n, out_specs=[_any_spec()] * n,
        out_shape=[jax.ShapeDtypeStruct((N_DEV,) + s.shape, s.dtype) for s in shards],
        scratch_shapes=[pltpu.SemaphoreType.DMA((len(parts), 7)), pltpu.SemaphoreType.DMA((len(parts), 7)),
                        pltpu.SemaphoreType.DMA((n,))],
    )(*shards)


def _peer(x, y, c, k):
    tx = 1 - x if (k >> 2) & 1 else x
    ty = 1 - y if (k >> 1) & 1 else y
    tc = 1 - c if k & 1 else c
    return (tx, ty, tc), 4 * tx + 2 * ty + tc


_PEER_ORDER = (2, 4, 6, 3, 5, 7, 1)


_HBM = pl.BlockSpec(memory_space=pltpu.HBM)
_SEM = pl.BlockSpec(memory_space=pltpu.SEMAPHORE)
_EFFECT = pltpu.SideEffectType.DATAFLOW_SIDE_EFFECTING


def _exchange_copies(srcs, lands, send_sems, recv_sems, gather, half):
    x, y, c, me = _my_pos()
    pick = lambda px, py: None if half is None else ((px == py) if half == 0 else (px != py))
    copies = []
    for k in _PEER_ORDER:
        tgt, tidx = _peer(x, y, c, k)
        for a in range(len(srcs)):
            copies.append((pltpu.make_async_remote_copy(
                src_ref=srcs[a] if gather else srcs[a].at[tidx], dst_ref=lands[a].at[me],
                send_sem=send_sems.at[7 * a + k - 1], recv_sem=recv_sems.at[7 * a + k - 1],
                device_id=tgt, device_id_type=MESH), pick(tgt[0], tgt[1])))
    return copies, pick(x, y)


def _when(cond, fn):
    if cond is None:
        fn()
    else:
        pl.when(cond)(fn)


def _exchange_start(name, srcs, lands=None, after=None, gather=None, half=None):
    n = len(srcs)
    gather = (lands is not None) if gather is None else gather
    if lands is None:
        lands = [lax.empty(g.shape, g.dtype) for g in srcs]
    extra = [] if after is None else [after]

    def body(*refs):
        src_refs, land_refs = refs[:n], refs[n:2 * n]
        send_sems, recv_sems = refs[2 * n + len(extra)], refs[2 * n + len(extra) + 1]
        token = refs[-1]
        for cp, sends in _exchange_copies(src_refs, land_refs, send_sems, recv_sems, gather, half)[0]:
            _when(sends, cp.start)
        token[...] = jnp.zeros_like(token)

    hbm = lambda a: pltpu.with_memory_space_constraint(a, pltpu.HBM)
    outs = pl.pallas_call(
        body, name=name,
        out_shape=(pltpu.SemaphoreType.DMA((7 * n,)), pltpu.SemaphoreType.DMA((7 * n,)),
                   *[pltpu.HBM(g.shape, g.dtype) for g in list(srcs) + list(lands)],
                   jax.ShapeDtypeStruct((8, 128), f32)),
        in_specs=[_HBM] * (2 * n) + [pl.BlockSpec(memory_space=pl.ANY)] * len(extra),
        out_specs=(_SEM, _SEM, *([_HBM] * (2 * n)), pl.BlockSpec(memory_space=pltpu.VMEM)),
        input_output_aliases={i: 2 + i for i in range(2 * n)},
        compiler_params=pltpu.CompilerParams(has_side_effects=_EFFECT),
    )(*[hbm(g) for g in srcs], *[hbm(g) for g in lands], *extra)
    return outs[0], outs[1], list(outs[2:2 + n]), list(outs[2 + n:2 + 2 * n]), outs[-1], gather, half


def _exchange_wait(name, started, after):
    send_sems, recv_sems, srcs, lands, _, gather, half = started
    n = len(srcs)
    after = list(after) if isinstance(after, (list, tuple)) else [after]

    def body(*refs):
        src_refs, land_refs = refs[:n], refs[n:2 * n]
        s_sems, r_sems = refs[2 * n], refs[2 * n + 1]
        copies, receives = _exchange_copies(src_refs, land_refs, s_sems, r_sems, gather, half)
        for cp, sends in copies:
            _when(sends, cp.wait_send)
            _when(receives, cp.wait_recv)

    outs = pl.pallas_call(
        body, name=name,
        out_shape=tuple(pltpu.HBM(a.shape, a.dtype) for a in list(srcs) + list(lands)),
        in_specs=[_HBM] * (2 * n) + [_SEM, _SEM] + [pl.BlockSpec(memory_space=pl.ANY)] * len(after),
        out_specs=tuple([_HBM] * (2 * n)),
        input_output_aliases={i: i for i in range(2 * n)},
        compiler_params=pltpu.CompilerParams(has_side_effects=_EFFECT),
    )(*srcs, *lands, send_sems, recv_sems, *after)
    return list(outs[:n]), list(outs[n:])


SMALL_ROWS = 128


def _small_start(name, sg, after=None):
    return _exchange_start(name, [sg], [lax.empty((N_DEV,) + sg.shape, f32)], after=after)


def _small_sum(name, me, started, after):
    (own,), (slots,) = _exchange_wait(name + "_wait", started, after)

    def body(me_ref, s_ref, own_ref, out_ref):
        acc = None
        for p in range(N_DEV):
            term = lax.cond(me_ref[0] == p, lambda: own_ref[...], lambda p=p: s_ref[p])
            acc = term if acc is None else acc + term
        out_ref[...] = acc

    return pl.pallas_call(
        body, name=name + "_sum",
        in_specs=[pl.BlockSpec(memory_space=pltpu.SMEM), pl.BlockSpec(memory_space=pltpu.VMEM),
                  pl.BlockSpec(memory_space=pltpu.VMEM)],
        out_specs=pl.BlockSpec(memory_space=pltpu.VMEM),
        out_shape=jax.ShapeDtypeStruct(own.shape, f32))(me, slots, own)


def _adam_math(g, w, m, v):
    m = ADAM_B1 * m + (1.0 - ADAM_B1) * g
    v = ADAM_B2 * v + (1.0 - ADAM_B2) * (g * g)
    m_hat = m / (1.0 - ADAM_B1 ** ADAM_STEP)
    v_hat = v / (1.0 - ADAM_B2 ** ADAM_STEP)
    delta = -ADAM_LR * (m_hat / (jnp.sqrt(v_hat) + ADAM_EPS) + ADAM_WD * w)
    return delta, m, v


def _adam_slots(name, me, slots, own, w, m, v, tr, transposed=False):
    rows = slots.shape[1]

    def body(me_ref, s_ref, own_ref, w_ref, m_ref, v_ref, g_ref, d_ref, nm_ref, nv_ref):
        mine = own_ref[...]
        g = None
        for p in range(N_DEV):
            term = lax.cond(me_ref[0] == p, lambda: mine, lambda p=p: s_ref[p]).astype(f32)
            g = term if g is None else g + term
        if transposed:
            g = g.T
        delta, nm, nv = _adam_math(g, w_ref[...], m_ref[...], v_ref[...])
        g_ref[...] = g
        d_ref[...] = delta
        nm_ref[...] = nm
        nv_ref[...] = nv

    mode = dict(pipeline_mode=pl.Buffered(1)) if rows == tr else {}
    if transposed:
        rs = pl.BlockSpec((D, tr), lambda i, me_ref: (0, i))
        rs_in = pl.BlockSpec((D, tr), lambda i, me_ref: (0, i), **mode)
    else:
        rs = pl.BlockSpec((tr, D), lambda i, me_ref: (i, 0))
        rs_in = pl.BlockSpec((tr, D), lambda i, me_ref: (i, 0), **mode)
    return pl.pallas_call(
        body, name=name,
        grid_spec=pltpu.PrefetchScalarGridSpec(
            num_scalar_prefetch=1, grid=(rows // tr,),
            in_specs=[pl.BlockSpec((N_DEV, tr, D), lambda i, me_ref: (0, i, 0), **mode),
                      pl.BlockSpec((None, tr, D), lambda i, me_ref: (me_ref[0], i, 0), **mode), rs_in, rs_in, rs_in],
            out_specs=[rs] * 4),
        out_shape=[jax.ShapeDtypeStruct(w.shape, f32)] * 4,
        compiler_params=_cparams(("parallel",)))(me, slots, own, w, m, v)


def _adam_small(g, w, m, v):
    def body(g_ref, w_ref, m_ref, v_ref, d_ref, nm_ref, nv_ref):
        delta, nm, nv = _adam_math(g_ref[...], w_ref[...], m_ref[...], v_ref[...])
        d_ref[...] = delta
        nm_ref[...] = nm
        nv_ref[...] = nv

    return pl.pallas_call(body, name="adam_small", out_shape=[jax.ShapeDtypeStruct(g.shape, f32)] * 3)(g, w, m, v)


FFN_PAD = 6 * D


_SMALL_PARTS = (("norm1_g", 1), ("gate_b", 2), ("conv_w", CONV_WIDTH), ("conv_b", 1), ("conv_norm_g", 1),
                ("q_norm_g", 1), ("k_norm_g", 1), ("norm2_g", 1), ("ffn_conv_w", 18), ("ffn_conv_b", 6), ("last", 1))


def _small_offsets():
    out, row = {}, 0
    for name, rows in _SMALL_PARTS:
        out[name] = row
        row += -(-rows // 8) * 8
    assert row == SMALL_ROWS
    return out


def _pack_small(norm1_g, gate_b, conv_w, conv_b, conv_norm_g, q_norm_g, k_norm_g, norm2_g, ffn_conv_w, ffn_conv_b,
                last_row=None):
    pad_h = lambda a: jnp.pad(a, ((0, 0), (0, D - HEAD_DIM)))
    pad_f = lambda a: jnp.pad(a, ((0, 0), (0, FFN_PAD - 2 * D_FF))).reshape(-1, D)
    parts = [norm1_g, gate_b.reshape(2, D), conv_w, conv_b, conv_norm_g, pad_h(q_norm_g), pad_h(k_norm_g), norm2_g,
             pad_f(ffn_conv_w), pad_f(ffn_conv_b), jnp.zeros((1, D), f32) if last_row is None else last_row]
    return jnp.concatenate([jnp.pad(p, ((0, -p.shape[0] % 8), (0, 0))) for p in parts], axis=0)


def _unpack_small(p):
    o = _small_offsets()
    rows = lambda name, n: p[o[name]:o[name] + n]
    ffn = lambda a: a.reshape(-1, FFN_PAD)[:, :2 * D_FF]
    return dict(
        norm1_g=rows("norm1_g", 1), gate_b=rows("gate_b", 2).reshape(1, 2 * D), conv_w=rows("conv_w", CONV_WIDTH),
        conv_b=rows("conv_b", 1), conv_norm_g=rows("conv_norm_g", 1), q_norm_g=rows("q_norm_g", 1)[:, :HEAD_DIM],
        k_norm_g=rows("k_norm_g", 1)[:, :HEAD_DIM], norm2_g=rows("norm2_g", 1),
        ffn_conv_w=ffn(rows("ffn_conv_w", 18)), ffn_conv_b=ffn(rows("ffn_conv_b", 6)))


_ADAM_TILE = {896: 128, 704: 704, 128: 128, 352: 176}


def kernel(x, norm1_g, w_in, gate_b, conv_w, conv_b, conv_norm_g, w_conv_out, q_norm_g, k_norm_g, w_attn_out, w_out, norm2_g, w_up, ffn_conv_w, ffn_conv_b, w_down, loss_target, m_norm1_g, m_w_in, m_gate_b, m_conv_w, m_conv_b, m_conv_norm_g, m_w_conv_out, m_q_norm_g, m_k_norm_g, m_w_attn_out, m_w_out, m_norm2_g, m_w_up, m_ffn_conv_w, m_ffn_conv_b, m_w_down, v_norm1_g, v_w_in, v_gate_b, v_conv_w, v_conv_b, v_conv_norm_g, v_w_conv_out, v_q_norm_g, v_k_norm_g, v_w_attn_out, v_w_out, v_norm2_g, v_w_up, v_ffn_conv_w, v_ffn_conv_b, v_w_down):
    BL, S, _ = x.shape
    T = BL * S
    me = 4 * lax.axis_index("x") + 2 * lax.axis_index("y") + lax.axis_index("c")
    xt = x.reshape(T, D)
    target = loss_target.reshape(T, D)

    big = dict(w_in=(w_in[0], m_w_in[0], v_w_in[0]), w_up=(w_up[0], m_w_up[0], v_w_up[0]),
               w_conv_out=(w_conv_out[0], m_w_conv_out[0], v_w_conv_out[0]),
               w_attn_out=(w_attn_out[0], m_w_attn_out[0], v_w_attn_out[0]),
               w_out=(w_out[0], m_w_out[0], v_w_out[0]), w_down=(w_down[0], m_w_down[0], v_w_down[0]))
    order = ["w_in", "w_conv_out", "w_attn_out", "w_out", "w_up", "w_down"]
    shards = [(big[n][0].T if n in ("w_in", "w_up") else big[n][0]).astype(bf16) for n in order]
    gathered = _allgather_rows(shards, 1)
    ga_proj = _exchange_start("gather_start_proj", shards[1:4], gathered[1:4], after=gathered[0])
    ga_ffn = _exchange_start("gather_start_ffn", shards[4:6], gathered[4:6], after=ga_proj[4])
    W = {"w_in": gathered[0].reshape(-1, D)}

    def place_cols(shard, full_cols):
        z = jnp.zeros((shard.shape[0], full_cols), f32)
        return lax.dynamic_update_slice(z, shard, (0, me * shard.shape[1]))

    zr = lambda a: jnp.zeros_like(a)
    conv_local = _pack_small(
        zr(norm1_g), zr(gate_b), place_cols(conv_w[0], D), zr(conv_b), zr(conv_norm_g), zr(q_norm_g), zr(k_norm_g),
        zr(norm2_g), place_cols(ffn_conv_w[0], 2 * D_FF), zr(ffn_conv_b))
    ga_conv = _small_start("gather_conv_start", conv_local, after=ga_ffn[4])

    bd = (jnp.arange(128)[:, None] // HEAD_DIM == jnp.arange(128)[None, :] // HEAD_DIM).astype(bf16)
    bias = _attn_bias()
    qg = jnp.tile(q_norm_g, (1, N_HEADS))
    kg = jnp.tile(k_norm_g, (1, N_HEADS))

    z8, h, qn, kn = _in_proj_fwd(xt, norm1_g, W["w_in"], qg, kg, bd, ga_conv[4])
    conv_all = _unpack_small(_small_sum("gather_conv", me.reshape(1), ga_conv, z8))
    conv_w_full, ffn_w_full = conv_all["conv_w"], conv_all["ffn_conv_w"]
    c = _conv_fwd(z8, conv_w_full, conv_b, S)
    o, ob, lse = _attn_fwd(qn, kn, z8, bias, S)
    for n, g in zip(order[1:4], _exchange_wait("gather_wait_proj", ga_proj, ob)[1]):
        W[n] = g.reshape(-1, D)
    s, ya, yb, mixed = _branches_fwd(c, ob, z8, conv_norm_g, gate_b, W["w_conv_out"], W["w_attn_out"])
    x1, h2 = _out_norm2_fwd(mixed, W["w_out"], xt, norm2_g)
    for n, g in zip(order[4:6], _exchange_wait("gather_wait_ffn", ga_ffn, x1)[1]):
        W[n] = g.reshape(-1, D)
    TNU = D_FF // 2
    u3 = _matmul_call(
        "mm_u", h2, W["w_up"],
        pl.BlockSpec((1024, D), lambda i, j, k: (i, 0)),
        pl.BlockSpec((TNU, D), lambda i, j, k: (j, 0)),
        pl.BlockSpec((None, 1024, TNU), lambda i, j, k: (j // 2, i, j % 2)),
        jax.ShapeDtypeStruct((2, T, D_FF), f32), (T // 1024, 4, 1), "nt", 1, 1024, TNU)
    f = _ffn_fwd(u3, ffn_w_full, ffn_conv_b, S)
    dy, dyb, lacc = _down_loss_fwd(f, W["w_down"], x1, target)
    loss_local = 0.5 / D * jnp.sum(lacc)

    df = _matmul("mm_df", dyb, W["w_down"], "nt", f32, tn=TNU)
    g_w_down = _matmul("mm_dwdn", f, dyb, "tn", bf16, tm=TNU)
    du3, dffn = _ffn_bwd(u3, df, ffn_w_full, ffn_conv_b, S)
    g_w_up = _matmul_call(
        "mm_dwup", du3, h2,
        pl.BlockSpec((None, T, TNU), lambda i, j, k: (i // 2, 0, i % 2)),
        pl.BlockSpec((T, D), lambda i, j, k: (0, 0)),
        pl.BlockSpec((TNU, D), lambda i, j, k: (i, 0)),
        jax.ShapeDtypeStruct((2 * D_FF, D), bf16), (4, 1, 1), "tn", 1, TNU, D)
    blocks8 = lambda a: a.reshape(N_DEV, -1, D)
    ex_ffn = _exchange_start("scatter_start_ffn", [blocks8(g_w_up), blocks8(g_w_down)])
    dx1, dx1b, dg_norm2 = _up_norm2_bwd(du3, W["w_up"], x1, dy, norm2_g, ex_ffn[4])
    g_w_out = _matmul("mm_dwo", mixed, dx1b, "tn", bf16, tm=512)
    dz8 = lax.empty((8, T, D), bf16)
    dya, dyb2, dz8, dg_gate = _out_gate_bwd(dx1b, W["w_out"], z8, gate_b, ya, yb, dz8)
    g_w_conv_out = _matmul("mm_dwco", s, dya, "tn", bf16, tm=512)
    g_w_attn_out = _matmul("mm_dwao", ob, dyb2, "tn", bf16, tm=512)
    ex_proj = _exchange_start("scatter_start_proj", [blocks8(g_w_conv_out), blocks8(g_w_attn_out), blocks8(g_w_out)])
    do = _matmul("mm_do", dyb2, W["w_attn_out"], "nt", f32, after=ex_proj[4])
    dc, dg_convnorm = _convnorm_bwd(dya, W["w_conv_out"], c, conv_norm_g)
    dz8a, dconv = _conv_bwd(dc, z8, conv_w_full, dz8, S)
    dwin_specs = lambda zsec, wsec: (
        pl.BlockSpec((None, T, D), lambda i, j, k: (zsec(i), 0, 0)), pl.BlockSpec((T, D), lambda i, j, k: (0, 0)),
        pl.BlockSpec((1024, D), lambda i, j, k: (wsec(i), 0)), jax.ShapeDtypeStruct((7 * D, D), bf16))
    g_w_in = _matmul_call("mm_dwin_a", dz8a, h, *dwin_specs(lambda i: i, lambda i: jnp.where(i < 2, i, i + 3)),
                          (4, 1, 1), "tn", 1, D, D)
    ex_in_a = _exchange_start("scatter_start_in_a", [blocks8(g_w_in)], half=0)
    dqn, dkn, dv = _attn_bwd(qn, kn, z8, do, o, lse, bias, bd, S, ex_in_a[4])
    dz8b, dg_q, dg_k = _qk_bwd(z8, dqn, dkn, dv, qg, kg, bd, dz8a)
    g_w_in = _matmul_call("mm_dwin_b", dz8b, h, *dwin_specs(lambda i: i + 4, lambda i: i + 2),
                          (3, 1, 1), "tn", 1, D, D, fill=ex_in_a[2][0].reshape(7 * D, D))
    ex_in_b = _exchange_start("scatter_start_in_b", [blocks8(g_w_in)], ex_in_a[3], gather=False, half=1)
    grad_x, dg_norm1 = _in_norm1_bwd(dz8b, W["w_in"], xt, dx1, norm1_g, ex_in_b[4])

    sum8 = lambda a: a.reshape(-1, 8, a.shape[-1]).sum(axis=1)
    dconv_s = sum8(dconv.sum(axis=0))
    dffn_s = dffn.sum(axis=0).reshape(2, 4, 8, D_FF).sum(axis=2)
    dffn_w = jnp.concatenate([dffn_s[0, :3], dffn_s[1, :3]], axis=1)
    dffn_b = jnp.concatenate([dffn_s[0, 3:4], dffn_s[1, 3:4]], axis=1)
    fold = lambda a: sum8(a).reshape(N_HEADS, HEAD_DIM).sum(axis=0)[None]
    small_g_local = _pack_small(
        sum8(dg_norm1), sum8(dg_gate), dconv_s[:CONV_WIDTH], dconv_s[CONV_WIDTH:], sum8(dg_convnorm),
        fold(dg_q), fold(dg_k), sum8(dg_norm2), dffn_w, dffn_b,
        last_row=jnp.pad(loss_local.reshape(1, 1), ((0, 0), (0, D - 1))))
    sg_start = _small_start("small_grads_start", small_g_local)

    own, slots = {}, {}
    for tag, ex, names_ in (("ffn", ex_ffn, ("w_up", "w_down")),
                            ("proj", ex_proj, ("w_conv_out", "w_attn_out", "w_out"))):
        sent, landed = _exchange_wait("scatter_wait_" + tag, ex, sg_start[4])
        for n, src, land in zip(names_, sent, landed):
            own[n], slots[n] = src, land
    sent, landed = _exchange_wait("scatter_wait_in_a", ex_in_a[:2] + (ex_in_b[2], ex_in_b[3]) + ex_in_a[4:],
                                  sg_start[4])
    sent, landed = _exchange_wait("scatter_wait_in_b", ex_in_b[:2] + (sent, landed) + ex_in_b[4:], sg_start[4])
    own["w_in"], slots["w_in"] = sent[0], landed[0]

    res, adam_done = {}, []
    for n in order:
        w, m, v = big[n]
        outs = _adam_slots("adam_" + n, me.reshape(1), slots[n], own[n], w, m, v, _ADAM_TILE[slots[n].shape[1]],
                           transposed=n in ("w_in", "w_up"))
        adam_done.append(outs[0])
        res[n] = [a[None] for a in outs]
    small_g = _small_sum("small_grads", me.reshape(1), sg_start, adam_done)
    loss = small_g[_small_offsets()["last"], 0]

    col = lambda a, width: lax.dynamic_slice(a, (0, me * width), (a.shape[0], width))
    small_w_true = _pack_small(norm1_g, gate_b, conv_w_full, conv_b, conv_norm_g, q_norm_g, k_norm_g, norm2_g,
                               ffn_w_full, ffn_conv_b)
    place_m = lambda a, full: place_cols(a[0], full)
    small_m = _pack_small(m_norm1_g, m_gate_b, place_m(m_conv_w, D), m_conv_b, m_conv_norm_g, m_q_norm_g, m_k_norm_g,
                          m_norm2_g, place_m(m_ffn_conv_w, 2 * D_FF), m_ffn_conv_b)
    small_v = _pack_small(v_norm1_g, v_gate_b, place_m(v_conv_w, D), v_conv_b, v_conv_norm_g, v_q_norm_g, v_k_norm_g,
                          v_norm2_g, place_m(v_ffn_conv_w, 2 * D_FF), v_ffn_conv_b)
    sd, sm, sv = _adam_small(small_g, small_w_true, small_m, small_v)
    for i, packed in enumerate((small_g, sd, sm, sv)):
        u = _unpack_small(packed)
        u["conv_w"] = col(u["conv_w"], D // N_DEV)
        u["ffn_conv_w"] = col(u["ffn_conv_w"], 2 * D_FF // N_DEV)
        for n, a in u.items():
            res.setdefault(n, [None] * 4)[i] = a[None] if n in ("conv_w", "ffn_conv_w") else a

    names = ["norm1_g", "w_in", "gate_b", "conv_w", "conv_b", "conv_norm_g", "w_conv_out", "q_norm_g", "k_norm_g",
             "w_attn_out", "w_out", "norm2_g", "w_up", "ffn_conv_w", "ffn_conv_b", "w_down"]
    out = [loss, grad_x.reshape(BL, S, D)]
    for i in range(4):
        out += [res[n][i] for n in names]
    return tuple(out)
```

```python
import functools

import jax
import jax.numpy as jnp
import numpy as np
from jax import lax
from jax.experimental import pallas as pl
from jax.experimental.pallas import tpu as pltpu

f32 = jnp.float32
bf16 = jnp.bfloat16

D = 1024
N_HEADS = 16
HEAD_DIM = 64
CONV_WIDTH = 31
D_FF = 2816
GROUPS = ((128, 1), (512, 4), (2048, 16))
ATTN_BLOCK = 128
EPS = 1e-6
N_DEV = 8
MESH = pl.DeviceIdType.MESH

ADAM_LR = 0.001
ADAM_B1 = 0.9
ADAM_B2 = 0.999
ADAM_EPS = 1e-08
ADAM_WD = 0.01
ADAM_STEP = 10

VMEM_LIMIT = 56 * 1024 * 1024
MASK_BIAS = 1e30

Z_AVAL, Z_AGATE, Z_GA, Z_GB, Z_Q, Z_K, Z_V = 0, 1, 2, 3, 4, 5, 6


_W_OF_Z = (0, 1, 5, 6, 2, 3, 4)


def _wsec_of_zsec(j):
    return jnp.where(j < 2, j, jnp.where(j < 4, j + 3, j - 2))


def _zsec_of_wsec(w):
    return jnp.where(w < 2, w, jnp.where(w < 5, w + 2, w - 3))


def _sig(x):
    return 1.0 / (1.0 + jnp.exp(-x))


def _colsum8(x):
    return x.reshape(-1, 8, x.shape[-1]).sum(axis=0)


def _cparams(sem):
    return pltpu.CompilerParams(dimension_semantics=sem, vmem_limit_bytes=VMEM_LIMIT)


def _my_pos():
    x, y, c = lax.axis_index("x"), lax.axis_index("y"), lax.axis_index("c")
    return x, y, c, 4 * x + 2 * y + c


_DIMS = {"nn": ((1,), (0,)), "nt": ((1,), (1,)), "tn": ((0,), (0,))}


def _matmul_call(name, a, b, a_spec, b_spec, o_spec, out_shape, grid, mode, nk, tm, tn, after=None, fill=None):
    dims = (_DIMS[mode], ((), ()))
    extra = ([] if after is None else [after]) + ([] if fill is None else [fill])

    def body(a_ref, b_ref, *rest):
        o_ref, scratch = rest[len(extra)], rest[len(extra) + 1:]
        part = lax.dot_general(a_ref[...], b_ref[...], dims, preferred_element_type=f32)
        if nk == 1:
            o_ref[...] = part.astype(o_ref.dtype)
        else:
            acc = scratch[0]
            k = pl.program_id(2)

            @pl.when(k == 0)
            def _():
                acc[...] = part

            @pl.when(k > 0)
            def _():
                acc[...] += part

            @pl.when(k == nk - 1)
            def _():
                o_ref[...] = acc[...].astype(o_ref.dtype)

    scratch = [] if nk == 1 else [pltpu.VMEM((tm, tn), f32)]
    return pl.pallas_call(
        body, name=name, grid=grid, in_specs=[a_spec, b_spec] + [pl.BlockSpec(memory_space=pl.ANY)] * len(extra),
        out_specs=o_spec, out_shape=out_shape, input_output_aliases={} if fill is None else {1 + len(extra): 0},
        scratch_shapes=scratch, compiler_params=_cparams(("parallel", "parallel", "arbitrary")),
    )(a, b, *extra)


def _matmul(name, a, b, mode, out_dtype, tm=1024, tn=1024, tk=None, after=None):
    if mode == "nn":
        (M, K), (_, N) = a.shape, b.shape
    elif mode == "nt":
        (M, K), (N, _) = a.shape, b.shape
    else:
        (K, M), (_, N) = a.shape, b.shape
    tm, tn = min(tm, M), min(tn, N)
    tk = K if tk is None else tk
    nk = K // tk
    assert M % tm == 0 and N % tn == 0 and K % tk == 0
    if mode == "tn":
        a_spec = pl.BlockSpec((tk, tm), lambda i, j, k: (k, i))
    else:
        a_spec = pl.BlockSpec((tm, tk), lambda i, j, k: (i, k))
    if mode == "nt":
        b_spec = pl.BlockSpec((tn, tk), lambda i, j, k: (j, k))
    else:
        b_spec = pl.BlockSpec((tk, tn), lambda i, j, k: (k, j))
    o_spec = pl.BlockSpec((tm, tn), lambda i, j, k: (i, j))
    return _matmul_call(name, a, b, a_spec, b_spec, o_spec, jax.ShapeDtypeStruct((M, N), out_dtype),
                        (M // tm, N // tn, nk), mode, nk, tm, tn, after=after)


FTM = 512


def _matmul_fused(name, a, b, pairs, epilogue, extras, consts, outs, nt=False, sums=False, passed=(), aliases=None):
    sa, M, kk = a.shape
    na = max(i for i, _ in pairs) + 1
    ne, nc, npass = len(extras), len(consts), len(passed)
    dims = (_DIMS["nt" if nt else "nn"], ((), ()))

    def body(a_ref, b_ref, *rest):
        acc = None
        for i, j in pairs:
            part = lax.dot_general(a_ref[i], b_ref[j], dims, preferred_element_type=f32)
            acc = part if acc is None else acc + part
        epilogue(acc, rest[:ne], rest[ne:ne + nc], rest[ne + nc + npass:])

    whole = lambda arr: pl.BlockSpec(arr.shape, lambda i, nd=arr.ndim: (0,) * nd, pipeline_mode=pl.Buffered(1))
    io_alias = {2 + ne + nc + k: v for k, v in (aliases or {}).items()}
    return pl.pallas_call(
        body, name=name, grid=(M // FTM,),
        in_specs=[pl.BlockSpec((na, FTM, kk), lambda i: (0, i, 0)), whole(b)] + [s for _, s in extras]
        + [whole(c) for c in consts] + [pl.BlockSpec(memory_space=pl.ANY)] * npass,
        out_specs=[s for _, s in outs], out_shape=[s for s, _ in outs], input_output_aliases=io_alias,
        compiler_params=_cparams(("arbitrary" if sums else "parallel",)),
    )(a, b, *[x for x, _ in extras], *consts, *passed)


def _frows(c=D):
    return pl.BlockSpec((FTM, c), lambda i: (i, 0))


def _fsec(s):
    return pl.BlockSpec((None, FTM, D), lambda i: (s, i, 0))


def _rowshape(T, dtype, c=D):
    return (jax.ShapeDtypeStruct((T, c), dtype), _frows(c))


def _sumshape(c=D):
    return (jax.ShapeDtypeStruct((8, c), f32), pl.BlockSpec((8, c), lambda i: (0, 0)))


def _add_colsum(ref, x, cols=None):
    @pl.when(pl.program_id(0) == 0)
    def _():
        if cols is None:
            ref[...] = jnp.zeros_like(ref)
        else:
            ref[:, cols] = jnp.zeros((8, x.shape[-1]), f32)

    if cols is None:
        ref[...] += _colsum8(x)
    else:
        ref[:, cols] += _colsum8(x)


TT = 512


def _rows(c, cb=0, tt=TT):
    return pl.BlockSpec((tt, c), lambda i: (i, cb))


def _sec(s, tt=TT):
    return pl.BlockSpec((None, tt, D), lambda i: (s, i, 0))


def _const(shape):
    return pl.BlockSpec(shape, lambda i: (0,) * len(shape))


def _acc_spec(c):
    return pl.BlockSpec((8, c), lambda i: (0, 0))


def _rms(x):
    return lax.rsqrt(jnp.mean(x * x, axis=-1, keepdims=True) + EPS)


def _rms_bwd(dy_g, xn, rstd):
    return rstd * (dy_g - xn * jnp.mean(dy_g * xn, axis=-1, keepdims=True))


def _head_sum(x, bd):
    parts = []
    for cb in range(x.shape[-1] // 128):
        xb = x[:, cb * 128:(cb + 1) * 128]
        hi = xb.astype(bf16)
        lo = (xb - hi.astype(f32)).astype(bf16)
        parts.append(jnp.dot(hi, bd, preferred_element_type=f32) + jnp.dot(lo, bd, preferred_element_type=f32))
    return parts[0] if len(parts) == 1 else jnp.concatenate(parts, axis=1)


ZTM = 1024


def _in_proj_fwd(x, g, w_in_t, qg, kg, bd, after):
    T = x.shape[0]

    def body(x_ref, g_ref, w_ref, qg_ref, kg_ref, bd_ref, after_ref, z_ref, h_ref, qn_ref, kn_ref, hbuf):
        del after_ref
        j = pl.program_id(1)

        @pl.when(j == 0)
        def _():
            xv = x_ref[...]
            hv = (xv * _rms(xv) * g_ref[...]).astype(bf16)
            hbuf[...] = hv
            h_ref[...] = hv

        z = lax.dot_general(hbuf[...], w_ref[...], (_DIMS["nt"], ((), ())), preferred_element_type=f32)
        z_ref[...] = z

        def head_norm(gain_ref, scale):
            return z * lax.rsqrt(_head_sum(z * z, bd_ref[...]) * (1.0 / HEAD_DIM) + EPS) * gain_ref[...] * scale

        @pl.when(j == Z_Q)
        def _():
            qn_ref[...] = head_norm(qg_ref, HEAD_DIM ** -0.5)

        @pl.when(j == Z_K)
        def _():
            kn_ref[...] = head_norm(kg_ref, 1.0)

    tile = pl.BlockSpec((ZTM, D), lambda i, j: (i, 0))
    row = pl.BlockSpec((1, D), lambda i, j: (0, 0))
    return pl.pallas_call(
        body, name="mm_z", grid=(T // ZTM, 7),
        in_specs=[tile, row, pl.BlockSpec((D, D), lambda i, j: (_wsec_of_zsec(j), 0)), row, row,
                  pl.BlockSpec((128, 128), lambda i, j: (0, 0)), pl.BlockSpec(memory_space=pl.ANY)],
        out_specs=[pl.BlockSpec((None, ZTM, D), lambda i, j: (j, i, 0)), tile, tile, tile],
        out_shape=[jax.ShapeDtypeStruct((8, T, D), f32), jax.ShapeDtypeStruct((T, D), bf16),
                   jax.ShapeDtypeStruct((T, D), f32), jax.ShapeDtypeStruct((T, D), f32)],
        scratch_shapes=[pltpu.VMEM((ZTM, D), bf16)],
        compiler_params=_cparams(("parallel", "arbitrary")))(x, g, w_in_t, qg, kg, bd, after)


def _branches_fwd(c, ob, z8, g, gate_b, w_conv_out, w_attn_out):
    T = c.shape[0]

    def epilogue(yb, extra, const, out):
        cv = extra[0][...]
        r = cv * _rms(cv) * const[0][...]
        s = (r * _sig(r)).astype(bf16)
        ya = jnp.dot(s, const[2][...], preferred_element_type=f32)
        b_ref = const[1]
        g_a = _sig(extra[1][...] + b_ref[:, :D])
        g_b = _sig(extra[2][...] + b_ref[:, D:])
        out[0][...] = s
        out[1][...] = ya
        out[2][...] = yb
        out[3][...] = (g_a * ya + g_b * yb).astype(bf16)

    return _matmul_fused("mm_branches", ob[None], w_attn_out[None], ((0, 0),), epilogue,
                         [(c, _frows()), (z8, _fsec(Z_GA)), (z8, _fsec(Z_GB))], [g, gate_b, w_conv_out],
                         [_rowshape(T, bf16), _rowshape(T, f32), _rowshape(T, f32), _rowshape(T, bf16)])


def _out_norm2_fwd(mixed, w_out, x, g):
    T = x.shape[0]

    def epilogue(acc, extra, const, out):
        x1 = extra[0][...] + acc
        out[0][...] = x1
        out[1][...] = (x1 * _rms(x1) * const[0][...]).astype(bf16)

    return _matmul_fused("mm_t1_norm2", mixed[None], w_out[None], ((0, 0),), epilogue, [(x, _frows())], [g],
                         [_rowshape(T, f32), _rowshape(T, bf16)])


def _down_loss_fwd(f, w_down, x1, target):
    T = x1.shape[0]

    def epilogue(acc, extra, const, out):
        diff = extra[0][...] + acc - extra[1][...]
        dy = diff * (1.0 / D)
        out[0][...] = dy
        out[1][...] = dy.astype(bf16)
        _add_colsum(out[2], diff * diff)

    return _matmul_fused("mm_t2_loss", f[None], w_down[None], ((0, 0),), epilogue, [(x1, _frows()), (target, _frows())],
                         [], [_rowshape(T, f32), _rowshape(T, bf16), _sumshape()], sums=True)


def _up_norm2_bwd(du3, w_up_t, x1, dy, g, token):
    T = x1.shape[0]

    def epilogue(dh, extra, const, out):
        x1v = extra[0][...]
        rstd = _rms(x1v)
        xn = x1v * rstd
        dx1 = extra[1][...] + _rms_bwd(dh * const[0][...], xn, rstd)
        out[0][...] = dx1
        out[1][...] = dx1.astype(bf16)
        _add_colsum(out[2], dh * xn)

    return _matmul_fused("mm_dh2_norm2", du3, w_up_t.reshape(2, D_FF, D), ((0, 0), (1, 1)), epilogue,
                         [(x1, _frows()), (dy, _frows())], [g],
                         [_rowshape(T, f32), _rowshape(T, bf16), _sumshape()], sums=True, passed=[token])


def _out_gate_bwd(dx1b, w_out, z8, gate_b, ya, yb, dz8):
    T = ya.shape[0]

    def epilogue(dm, extra, const, out):
        b_ref = const[0]
        g_a = _sig(extra[0][...] + b_ref[:, :D])
        g_b = _sig(extra[1][...] + b_ref[:, D:])
        out[0][...] = (dm * g_a).astype(bf16)
        out[1][...] = (dm * g_b).astype(bf16)
        dla = dm * extra[2][...] * g_a * (1.0 - g_a)
        dlb = dm * extra[3][...] * g_b * (1.0 - g_b)
        out[2][0] = dla.astype(bf16)
        out[2][1] = dlb.astype(bf16)
        _add_colsum(out[3], dla, slice(0, D))
        _add_colsum(out[3], dlb, slice(D, 2 * D))

    return _matmul_fused(
        "mm_dmixed_gate", dx1b[None], w_out[None], ((0, 0),), epilogue,
        [(z8, _fsec(Z_GA)), (z8, _fsec(Z_GB)), (ya, _frows()), (yb, _frows())], [gate_b],
        [_rowshape(T, bf16), _rowshape(T, bf16),
         (jax.ShapeDtypeStruct(dz8.shape, bf16), pl.BlockSpec((2, FTM, D), lambda i: (1, i, 0))), _sumshape(2 * D)],
        nt=True, sums=True, passed=[dz8], aliases={0: 2})


def _convnorm_bwd(dya, w_conv_out, c, g):
    T = c.shape[0]

    def epilogue(ds, extra, const, out):
        cv = extra[0][...]
        rstd = _rms(cv)
        r0 = cv * rstd
        gv = const[0][...]
        r = r0 * gv
        sg = _sig(r)
        dr = ds * sg * (1.0 + r * (1.0 - sg))
        out[0][...] = _rms_bwd(dr * gv, r0, rstd)
        _add_colsum(out[1], dr * r0)

    return _matmul_fused("mm_ds_convnorm", dya[None], w_conv_out[None], ((0, 0),), epilogue, [(c, _frows())], [g],
                         [_rowshape(T, f32), _sumshape()], nt=True, sums=True)


def _qk_bwd(z8, dqn, dkn, dv, qg, kg, bd, dz8):
    T = dqn.shape[0]

    def body(q_ref, k_ref, dqn_ref, dkn_ref, dv_ref, qg_ref, kg_ref, bd_ref, dz_in, dz_ref, dqg_ref, dkg_ref):
        del dz_in
        bdv = bd_ref[...]

        @pl.when(pl.program_id(0) == 0)
        def _():
            dqg_ref[...] = jnp.zeros_like(dqg_ref)
            dkg_ref[...] = jnp.zeros_like(dkg_ref)

        def one(raw, dn_scaled, g, dg_ref, sec):
            rstd = lax.rsqrt(_head_sum(raw * raw, bdv) * (1.0 / HEAD_DIM) + EPS)
            n = raw * rstd
            dg_ref[...] += _colsum8(dn_scaled * n)
            dn = dn_scaled * g
            draw = rstd * (dn - n * (_head_sum(dn * n, bdv) * (1.0 / HEAD_DIM)))
            dz_ref[sec] = draw.astype(bf16)

        one(q_ref[...], dqn_ref[...] * (HEAD_DIM ** -0.5), qg_ref[...], dqg_ref, 0)
        one(k_ref[...], dkn_ref[...], kg_ref[...], dkg_ref, 1)
        dz_ref[2] = dv_ref[...].astype(bf16)
        dz_ref[3] = jnp.zeros((TT, D), bf16)

    return pl.pallas_call(
        body, name="qk_bwd", grid=(T // TT,),
        in_specs=[_sec(Z_Q), _sec(Z_K), _rows(D), _rows(D), _rows(D), _const((1, D)), _const((1, D)),
                  _const((128, 128)), pl.BlockSpec(memory_space=pl.ANY)],
        out_specs=[pl.BlockSpec((4, TT, D), lambda i: (1, i, 0)), _acc_spec(D), _acc_spec(D)],
        out_shape=[jax.ShapeDtypeStruct(dz8.shape, bf16), jax.ShapeDtypeStruct((8, D), f32),
                   jax.ShapeDtypeStruct((8, D), f32)],
        input_output_aliases={8: 0},
        compiler_params=_cparams(("arbitrary",)))(z8, z8, dqn, dkn, dv, qg, kg, bd, dz8)


def _in_norm1_bwd(dz8, w_in_t, x, dx1, g, token):
    T = x.shape[0]

    def epilogue(dh, extra, const, out):
        xv = extra[0][...]
        rstd = _rms(xv)
        xn = xv * rstd
        out[0][...] = extra[1][...] + _rms_bwd(dh * const[0][...], xn, rstd)
        _add_colsum(out[1], dh * xn)

    return _matmul_fused("mm_dh_norm1", dz8, w_in_t.reshape(7, D, D), tuple(zip(range(7), _W_OF_Z)), epilogue,
                         [(x, _frows()), (dx1, _frows())], [g], [_rowshape(T, f32), _sumshape()],
                         sums=True, passed=[token])


CCW = 256
CR = 64
HALO = 32


def _conv_fwd(z8, conv_w, conv_b, S):
    T = z8.shape[1]
    nb = T // S
    ncb = D // CCW

    def body(av_ref, ag_ref, w_ref, b_ref, c_ref, pad):
        pad[0:HALO, :] = jnp.zeros((HALO, CCW), f32)

        def fill(i, carry):
            r0 = pl.multiple_of(i * 256, 256)
            pad[pl.ds(HALO + r0, 256), :] = av_ref[pl.ds(r0, 256), :] * _sig(ag_ref[pl.ds(r0, 256), :])
            return carry

        lax.fori_loop(0, S // 256, fill, 0)
        bias = b_ref[...]

        def chunk(i, carry):
            r0 = pl.multiple_of(i * CR, CR)
            win = pad[pl.ds(r0, CR + HALO), :]
            acc = jnp.zeros((CR, CCW), f32) + bias
            for s in range(8):
                part = None
                for m in range((CONV_WIDTH - 1 - s) // 8 + 1):
                    j = CONV_WIDTH - 1 - 8 * m - s
                    term = win[24 - 8 * m:24 - 8 * m + CR + 8, :] * w_ref[j:j + 1, :]
                    part = term if part is None else part + term
                acc = acc + part[8 - s:8 - s + CR, :]
            c_ref[pl.ds(r0, CR), :] = acc
            return carry

        lax.fori_loop(0, S // CR, chunk, 0)

    zs = lambda s: pl.BlockSpec((None, S, CCW), lambda b, cb: (s, b, cb))
    return pl.pallas_call(
        body, name="conv_fwd", grid=(nb, ncb),
        in_specs=[zs(Z_AVAL), zs(Z_AGATE), pl.BlockSpec((CONV_WIDTH, CCW), lambda b, cb: (0, cb)),
                  pl.BlockSpec((1, CCW), lambda b, cb: (0, cb))],
        out_specs=pl.BlockSpec((S, CCW), lambda b, cb: (b, cb)),
        out_shape=jax.ShapeDtypeStruct((T, D), f32),
        scratch_shapes=[pltpu.VMEM((S + HALO, CCW), f32)],
        compiler_params=_cparams(("parallel", "parallel")))(z8, z8, conv_w, conv_b)


def _conv_bwd(dc, z8, conv_w, dz8, S):
    T = dc.shape[0]
    nb = T // S
    ncb = D // CCW

    def body(dc_ref, av_ref, ag_ref, w_ref, dz_in, dz_ref, dw_ref, apad, dpad, shbuf):
        del dz_in
        apad[0:HALO, :] = jnp.zeros((HALO, CCW), f32)
        dpad[S:S + HALO, :] = jnp.zeros((HALO, CCW), f32)
        dw_ref[...] = jnp.zeros_like(dw_ref)

        def fill(i, carry):
            r0 = pl.multiple_of(i * 256, 256)
            apad[pl.ds(HALO + r0, 256), :] = av_ref[pl.ds(r0, 256), :] * _sig(ag_ref[pl.ds(r0, 256), :])
            dpad[pl.ds(r0, 256), :] = dc_ref[pl.ds(r0, 256), :]
            return carry

        lax.fori_loop(0, S // 256, fill, 0)

        def chunk(i, carry):
            r0 = pl.multiple_of(i * CR, CR)
            dwin = dpad[pl.ds(r0, CR + HALO), :]
            da = jnp.zeros((CR, CCW), f32)
            for s in range(8):
                shbuf[...] = dwin[s:s + CR, :]
                dshift = shbuf[...]
                part = None
                for m in range((CONV_WIDTH - 1 - s) // 8 + 1):
                    j = CONV_WIDTH - 1 - 8 * m - s
                    term = dwin[8 * m:8 * m + CR + 8, :] * w_ref[j:j + 1, :]
                    part = term if part is None else part + term
                    a_lag = apad[pl.ds(r0 + HALO - 8 * m, CR), :]
                    dw_ref[8 * j:8 * j + 8, :] += _colsum8(dshift * a_lag)
                da = da + part[s:s + CR, :]
            dw_ref[8 * CONV_WIDTH:8 * CONV_WIDTH + 8, :] += _colsum8(dwin[0:CR, :])
            av = av_ref[pl.ds(r0, CR), :]
            sg = _sig(ag_ref[pl.ds(r0, CR), :])
            dz_ref[0, pl.ds(r0, CR), :] = (da * sg).astype(bf16)
            dz_ref[1, pl.ds(r0, CR), :] = (da * av * sg * (1.0 - sg)).astype(bf16)
            return carry

        lax.fori_loop(0, S // CR, chunk, 0)

    zs = lambda s: pl.BlockSpec((None, S, CCW), lambda b, cb: (s, b, cb))
    return pl.pallas_call(
        body, name="conv_bwd", grid=(nb, ncb),
        in_specs=[pl.BlockSpec((S, CCW), lambda b, cb: (b, cb)), zs(Z_AVAL), zs(Z_AGATE),
                  pl.BlockSpec((CONV_WIDTH, CCW), lambda b, cb: (0, cb)), pl.BlockSpec(memory_space=pl.ANY)],
        out_specs=[pl.BlockSpec((2, S, CCW), lambda b, cb: (0, b, cb)),
                   pl.BlockSpec((None, 256, CCW), lambda b, cb: (b, 0, cb))],
        out_shape=[jax.ShapeDtypeStruct(dz8.shape, bf16), jax.ShapeDtypeStruct((nb, 256, D), f32)],
        input_output_aliases={4: 0},
        scratch_shapes=[pltpu.VMEM((S + HALO, CCW), f32), pltpu.VMEM((S + HALO, CCW), f32),
                        pltpu.VMEM((CR, CCW), f32)],
        compiler_params=_cparams(("parallel", "parallel")))(dc, z8, z8, conv_w, dz8)


FR = 128
NFB = D_FF // CCW


def _ffn_window(ref, i, r0):
    return ref[pl.ds(r0 - 8, FR + 8), :]


def _ffn_u(win, w_ref, b_ref):
    return (win[6:6 + FR, :] * w_ref[0:1, :] + win[7:7 + FR, :] * w_ref[1:2, :]
            + win[8:8 + FR, :] * w_ref[2:3, :] + b_ref[...])


def _ffn_fwd(u3, ffn_w, ffn_b, S):
    T = u3.shape[1]
    nb = T // S

    def body(uv_ref, ug_ref, wv_ref, wg_ref, bv_ref, bg_ref, f_ref):
        def chunk(first, i):
            r0 = 0 if first else pl.multiple_of(i * FR, FR)
            if first:
                z = jnp.zeros((8, CCW), f32)
                wv = jnp.concatenate([z, uv_ref[0:FR, :]], axis=0)
                wg = jnp.concatenate([z, ug_ref[0:FR, :]], axis=0)
            else:
                wv = _ffn_window(uv_ref, i, r0)
                wg = _ffn_window(ug_ref, i, r0)
            u_val = _ffn_u(wv, wv_ref, bv_ref)
            u_gate = _ffn_u(wg, wg_ref, bg_ref)
            f_ref[pl.ds(r0, FR), :] = (u_gate * _sig(u_gate) * u_val).astype(bf16)

        chunk(True, 0)

        def loop(i, carry):
            chunk(False, i)
            return carry

        lax.fori_loop(1, S // FR, loop, 0)

    us = lambda h: pl.BlockSpec((None, S, CCW), lambda b, cb: (h, b, cb))
    ws = lambda h: pl.BlockSpec((3, CCW), lambda b, cb: (0, h * NFB + cb))
    bs = lambda h: pl.BlockSpec((1, CCW), lambda b, cb: (0, h * NFB + cb))
    return pl.pallas_call(
        body, name="ffn_fwd", grid=(nb, NFB),
        in_specs=[us(0), us(1), ws(0), ws(1), bs(0), bs(1)],
        out_specs=pl.BlockSpec((S, CCW), lambda b, cb: (b, cb)),
        out_shape=jax.ShapeDtypeStruct((T, D_FF), bf16),
        compiler_params=_cparams(("parallel", "parallel")))(u3, u3, ffn_w, ffn_w, ffn_b, ffn_b)


def _ffn_bwd(u3, df, ffn_w, ffn_b, S):
    T = u3.shape[1]
    nb = T // S

    def body(uv_ref, ug_ref, df_ref, wv_ref, wg_ref, bv_ref, bg_ref, du_ref, dw_ref, dvpad, dgpad, shbuf):
        dvpad[S:S + 8, :] = jnp.zeros((8, CCW), f32)
        dgpad[S:S + 8, :] = jnp.zeros((8, CCW), f32)
        dw_ref[...] = jnp.zeros_like(dw_ref)

        def chunk(first, i):
            r0 = 0 if first else pl.multiple_of(i * FR, FR)
            if first:
                z = jnp.zeros((8, CCW), f32)
                wv = jnp.concatenate([z, uv_ref[0:FR, :]], axis=0)
                wg = jnp.concatenate([z, ug_ref[0:FR, :]], axis=0)
            else:
                wv = _ffn_window(uv_ref, i, r0)
                wg = _ffn_window(ug_ref, i, r0)
            taps = []
            for h, win in enumerate((wv, wg)):
                shbuf[2 * h] = win[6:6 + FR, :]
                shbuf[2 * h + 1] = win[7:7 + FR, :]
                taps.append((shbuf[2 * h], shbuf[2 * h + 1], win[8:8 + FR, :]))
            conv = lambda x, w_ref, b_ref: (x[0] * w_ref[0:1, :] + x[1] * w_ref[1:2, :] + x[2] * w_ref[2:3, :]
                                            + b_ref[...])
            u_val = conv(taps[0], wv_ref, bv_ref)
            u_gate = conv(taps[1], wg_ref, bg_ref)
            dfc = df_ref[pl.ds(r0, FR), :]
            sg = _sig(u_gate)
            d_val = dfc * u_gate * sg
            d_gate = dfc * u_val * sg * (1.0 + u_gate * (1.0 - sg))
            dvpad[pl.ds(r0, FR), :] = d_val
            dgpad[pl.ds(r0, FR), :] = d_gate
            for h, dd in enumerate((d_val, d_gate)):
                for j in range(3):
                    dw_ref[h, 8 * j:8 * j + 8, :] += _colsum8(dd * taps[h][j])
                dw_ref[h, 24:32, :] += _colsum8(dd)

        chunk(True, 0)

        def loop(i, carry):
            chunk(False, i)
            return carry

        lax.fori_loop(1, S // FR, loop, 0)

        def back(i, carry):
            r0 = pl.multiple_of(i * FR, FR)
            for h, (dpad, w_ref) in enumerate(((dvpad, wv_ref), (dgpad, wg_ref))):
                win = dpad[pl.ds(r0, FR + 8), :]
                du = (win[0:FR, :] * w_ref[2:3, :] + win[1:1 + FR, :] * w_ref[1:2, :]
                      + win[2:2 + FR, :] * w_ref[0:1, :])
                du_ref[h, pl.ds(r0, FR), :] = du.astype(bf16)
            return carry

        lax.fori_loop(0, S // FR, back, 0)

    us = lambda h: pl.BlockSpec((None, S, CCW), lambda b, cb: (h, b, cb))
    ws = lambda h: pl.BlockSpec((3, CCW), lambda b, cb: (0, h * NFB + cb))
    bs = lambda h: pl.BlockSpec((1, CCW), lambda b, cb: (0, h * NFB + cb))
    return pl.pallas_call(
        body, name="ffn_bwd", grid=(nb, NFB),
        in_specs=[us(0), us(1), pl.BlockSpec((S, CCW), lambda b, cb: (b, cb)), ws(0), ws(1), bs(0), bs(1)],
        out_specs=[pl.BlockSpec((2, S, CCW), lambda b, cb: (0, b, cb)),
                   pl.BlockSpec((None, 2, 32, CCW), lambda b, cb: (b, 0, 0, cb))],
        out_shape=[jax.ShapeDtypeStruct((2, T, D_FF), bf16), jax.ShapeDtypeStruct((nb, 2, 32, D_FF), f32)],
        scratch_shapes=[pltpu.VMEM((S + 8, CCW), f32), pltpu.VMEM((S + 8, CCW), f32),
                        pltpu.VMEM((4, FR, CCW), f32)],
        compiler_params=_cparams(("parallel", "parallel")))(u3, u3, df, ffn_w, ffn_w, ffn_b, ffn_b)


AB = ATTN_BLOCK


def _attn_bias_np():
    slopes = (np.float32(2.0) ** (np.float32(-8.0) * np.arange(1, N_HEADS + 1, dtype=np.float32)
                                  / np.float32(N_HEADS))).astype(np.float32)
    steps = (np.arange(AB)[:, None] + AB) - np.arange(2 * AB)[None, :]
    own = (np.arange(2 * AB) >= AB)[None, :]
    out = []
    for window, dil in GROUPS:
        valid = (steps >= 0) & (steps <= window // dil)
        dist = slopes[:, None, None] * (steps * dil).astype(np.float32)[None]
        kinds = [np.where(v[None], dist, np.float32(MASK_BIAS)) for v in (valid, valid & own)]
        out.append(np.stack(kinds, axis=1))
    return np.stack(out).astype(np.float32)


def _attn_bias():
    return jnp.asarray(_attn_bias_np())


def _head_masks():
    lane = lax.broadcasted_iota(jnp.int32, (1, 128), 1)
    return (lane < HEAD_DIM, lane >= HEAD_DIM)


def _perm_chunks(S, d):
    L = S // d
    ch = min(L, 256)
    out = []
    for r in range(d):
        for c in range(L // ch):
            start = r + d * ch * c
            out.append((pl.ds(start, ch, stride=d) if d > 1 else pl.ds(start, ch), r * L + c * ch, ch))
    return out


def _stack_heads(x, masks):
    return jnp.concatenate([jnp.where(masks[0], x, 0), jnp.where(masks[1], x, 0)], axis=0)


def _block_row(j):
    return j * AB if isinstance(j, int) else pl.multiple_of(j * AB, AB)


def _three_stages(n, stage_a, stage_b, stage_c, unroll):
    stage_a(0)
    stage_a(1)
    stage_b(0)

    def body(j, carry):
        stage_c(j - 1)
        stage_b(j)
        stage_a(j + 1)
        return carry

    lax.fori_loop(1, n - 1, body, 0, unroll=unroll)
    stage_c(n - 2)
    stage_b(n - 1)
    stage_c(n - 1)


_NT = (((1,), (1,)), ((), ()))
_TN = (((0,), (0,)), ((), ()))
SCH = 64


def _attn_fwd(qn, kn, z8, bias, S):
    T = qn.shape[0]
    nb = T // S
    nblk = S // AB

    def body(q_ref, k_ref, v_ref, bias_ref, o_ref, ob_ref, lse_ref, qs, ks, vs, s2, p2, ogp, lgp, *group_scratch):
        og, lg = group_scratch[:3], group_scratch[3:]
        masks = _head_masks()
        ks[0:AB, :] = jnp.zeros((AB, 128), bf16)
        vs[0:AB, :] = jnp.zeros((AB, 128), bf16)

        for g, (_, d) in enumerate(GROUPS):
            nsub = S // (d * AB)
            chunks = _perm_chunks(S, d)
            for src, dst, ch in chunks:
                qs[dst:dst + ch, :] = q_ref[src, :].astype(bf16)
                ks[AB + dst:AB + dst + ch, :] = k_ref[src, :].astype(bf16)
                vs[AB + dst:AB + dst + ch, :] = v_ref[src, :].astype(bf16)
            od, ld = (og[g], lg[g]) if d == 1 else (ogp, lgp)

            def scores(j):
                r0 = _block_row(j)
                q2 = _stack_heads(qs[pl.ds(r0, AB), :], masks)
                s2[j] = lax.dot_general(q2, ks[pl.ds(r0, 2 * AB), :], _NT, preferred_element_type=f32)

            def softmax(j, g=g, nsub=nsub, ld=ld):
                r0 = _block_row(j)
                kind = int(j % nsub == 0) if isinstance(j, int) else (j % nsub == 0).astype(jnp.int32)
                for cc in range(AB // SCH):
                    lses = []
                    for hh in range(2):
                        rows = pl.ds(hh * AB + cc * SCH, SCH)
                        sb = s2[j, rows, :] - bias_ref[g, hh, kind, cc * SCH:(cc + 1) * SCH, :]
                        m = jnp.max(sb, axis=-1, keepdims=True)
                        p = jnp.exp(sb - m)
                        den = jnp.sum(p, axis=-1, keepdims=True)
                        p2[j, rows, :] = (p * (1.0 / den)).astype(bf16)
                        lses.append(m + jnp.log(den))
                    ld[pl.ds(r0 + cc * SCH, SCH), :] = jnp.where(masks[0], lses[0], lses[1])

            def values(j, od=od):
                r0 = _block_row(j)
                pv2 = jnp.dot(p2[j], vs[pl.ds(r0, 2 * AB), :], preferred_element_type=f32)
                od[pl.ds(r0, AB), :] = jnp.where(masks[0], pv2[:AB], pv2[AB:])

            _three_stages(nblk, scores, softmax, values, nblk - 2)

            if d > 1:
                for src, dst, ch in chunks:
                    og[g][src, :] = ogp[dst:dst + ch, :]
                    lg[g][src, :] = lgp[dst:dst + ch, :]

        def combine(i, carry):
            rr = pl.ds(pl.multiple_of(i * 256, 256), 256)
            l0, l1, l2 = lg[0][rr, :], lg[1][rr, :], lg[2][rr, :]
            mx = jnp.maximum(jnp.maximum(l0, l1), l2)
            e0, e1, e2 = jnp.exp(l0 - mx), jnp.exp(l1 - mx), jnp.exp(l2 - mx)
            den = e0 + e1 + e2
            o = (e0 * og[0][rr, :] + e1 * og[1][rr, :] + e2 * og[2][rr, :]) / den
            o_ref[rr, :] = o
            ob_ref[rr, :] = o.astype(bf16)
            lse_ref[rr, :] = mx + jnp.log(den)
            return carry

        lax.fori_loop(0, S // 256, combine, 0)

    blk = pl.BlockSpec((S, 128), lambda b, hp: (b, hp))
    return pl.pallas_call(
        body, name="attn_fwd", grid=(nb, N_HEADS // 2),
        in_specs=[blk, blk, pl.BlockSpec((None, S, 128), lambda b, hp: (Z_V, b, hp)),
                  pl.BlockSpec((3, 2, 2, AB, 2 * AB), lambda b, hp: (0, hp, 0, 0, 0))],
        out_specs=[blk, blk, blk],
        out_shape=[jax.ShapeDtypeStruct((T, D), f32), jax.ShapeDtypeStruct((T, D), bf16),
                   jax.ShapeDtypeStruct((T, D), f32)],
        scratch_shapes=[pltpu.VMEM((S, 128), bf16), pltpu.VMEM((S + AB, 128), bf16), pltpu.VMEM((S + AB, 128), bf16),
                        pltpu.VMEM((nblk, 2 * AB, 2 * AB), f32), pltpu.VMEM((nblk, 2 * AB, 2 * AB), bf16),
                        pltpu.VMEM((S, 128), f32), pltpu.VMEM((S, 128), f32)] + [pltpu.VMEM((S, 128), f32)] * 6,
        compiler_params=_cparams(("parallel", "parallel")))(qn, kn, z8, bias)


def _attn_bwd(qn, kn, z8, do, o, lse, bias, bd, S, after):
    T = qn.shape[0]
    nb = T // S

    nblk = S // AB

    def body(q_ref, k_ref, v_ref, do_ref, o_ref, lse_ref, bias_ref, bd_ref, after_ref, dq_ref, dk_ref, dv_ref,
             delta, qs, ks, vs, dos, lsp, dlp, s2, dp2, p2, ds2, dqp, dkp, dvp):
        del after_ref
        masks = _head_masks()
        bdv = bd_ref[...]
        dq_ref[...] = jnp.zeros_like(dq_ref)
        dk_ref[...] = jnp.zeros_like(dk_ref)
        dv_ref[...] = jnp.zeros_like(dv_ref)
        ks[0:AB, :] = jnp.zeros((AB, 128), bf16)
        vs[0:AB, :] = jnp.zeros((AB, 128), bf16)

        def prep(i, carry):
            rr = pl.ds(pl.multiple_of(i * 256, 256), 256)
            delta[rr, :] = _head_sum(do_ref[rr, :] * o_ref[rr, :], bdv)
            return carry

        lax.fori_loop(0, S // 256, prep, 0, unroll=True)

        for g, (_, d) in enumerate(GROUPS):
            nsub = S // (d * AB)
            chunks = _perm_chunks(S, d)
            for src, dst, ch in chunks:
                qs[dst:dst + ch, :] = q_ref[src, :].astype(bf16)
                ks[AB + dst:AB + dst + ch, :] = k_ref[src, :].astype(bf16)
                vs[AB + dst:AB + dst + ch, :] = v_ref[src, :].astype(bf16)
                dos[dst:dst + ch, :] = do_ref[src, :].astype(bf16)
                lsp[dst:dst + ch, :] = lse_ref[src, :]
                dlp[dst:dst + ch, :] = delta[src, :]
            dkp[...] = jnp.zeros_like(dkp)
            dvp[...] = jnp.zeros_like(dvp)

            def scores(j):
                r0 = _block_row(j)
                q2 = _stack_heads(qs[pl.ds(r0, AB), :], masks)
                do2 = _stack_heads(dos[pl.ds(r0, AB), :], masks)
                s2[j] = lax.dot_general(q2, ks[pl.ds(r0, 2 * AB), :], _NT, preferred_element_type=f32)
                dp2[j] = lax.dot_general(do2, vs[pl.ds(r0, 2 * AB), :], _NT, preferred_element_type=f32)

            def probs(j, g=g, nsub=nsub):
                r0 = _block_row(j)
                kind = int(j % nsub == 0) if isinstance(j, int) else (j % nsub == 0).astype(jnp.int32)
                for cc in range(AB // SCH):
                    lse_c = lsp[pl.ds(r0 + cc * SCH, SCH), :]
                    del_c = dlp[pl.ds(r0 + cc * SCH, SCH), :]
                    for hh in range(2):
                        c0 = hh * HEAD_DIM
                        rows = pl.ds(hh * AB + cc * SCH, SCH)
                        sb = s2[j, rows, :] - bias_ref[g, hh, kind, cc * SCH:(cc + 1) * SCH, :]
                        p = jnp.exp(sb - lse_c[:, c0:c0 + 1])
                        p2[j, rows, :] = p.astype(bf16)
                        ds2[j, rows, :] = (p * (dp2[j, rows, :] - del_c[:, c0:c0 + 1])).astype(bf16)

            def grads(j):
                r0 = _block_row(j)
                q2 = _stack_heads(qs[pl.ds(r0, AB), :], masks)
                do2 = _stack_heads(dos[pl.ds(r0, AB), :], masks)
                dsb = ds2[j]
                t = jnp.dot(dsb, ks[pl.ds(r0, 2 * AB), :], preferred_element_type=f32)
                dqp[pl.ds(r0, AB), :] = jnp.where(masks[0], t[:AB], t[AB:])
                dkp[pl.ds(r0, 2 * AB), :] += lax.dot_general(dsb, q2, _TN, preferred_element_type=f32)
                dvp[pl.ds(r0, 2 * AB), :] += lax.dot_general(p2[j], do2, _TN, preferred_element_type=f32)

            _three_stages(nblk, scores, probs, grads, nblk - 2)

            for src, dst, ch in chunks:
                dq_ref[src, :] += dqp[dst:dst + ch, :]
                dk_ref[src, :] += dkp[AB + dst:AB + dst + ch, :]
                dv_ref[src, :] += dvp[AB + dst:AB + dst + ch, :]

    blk = pl.BlockSpec((S, 128), lambda b, hp: (b, hp))
    row = lambda dt, pad=0: pltpu.VMEM((S + pad, 128), dt)
    blocks = lambda dt: pltpu.VMEM((nblk, 2 * AB, 2 * AB), dt)
    return pl.pallas_call(
        body, name="attn_bwd", grid=(nb, N_HEADS // 2),
        in_specs=[blk, blk, pl.BlockSpec((None, S, 128), lambda b, hp: (Z_V, b, hp)), blk, blk, blk,
                  pl.BlockSpec((3, 2, 2, AB, 2 * AB), lambda b, hp: (0, hp, 0, 0, 0)),
                  pl.BlockSpec((128, 128), lambda b, hp: (0, 0)), pl.BlockSpec(memory_space=pl.ANY)],
        out_specs=[blk, blk, blk],
        out_shape=[jax.ShapeDtypeStruct((T, D), f32)] * 3,
        scratch_shapes=[row(f32), row(bf16), row(bf16, AB), row(bf16, AB), row(bf16), row(f32), row(f32),
                        blocks(f32), blocks(f32), blocks(bf16), blocks(bf16), row(f32), row(f32, AB), row(f32, AB)],
        compiler_params=_cparams(("parallel", "parallel")))(qn, kn, z8, do, o, lse, bias, bd, after)


def _any_spec():
    return pl.BlockSpec(memory_space=pl.ANY)


AG_CHUNKS = 4


def _allgather_rows(shards, n_full):
    n = len(shards)
    parts = [(a, q) for a in range(n_full) for q in range(AG_CHUNKS)]

    def body(*refs):
        ins, outs = refs[:n], refs[n:2 * n]
        send_sems, recv_sems, local_sems = refs[2 * n:]
        x, y, c, me = _my_pos()
        sibling = (x, y, 1 - c)
        chips = [(1 - x, y), (x, 1 - y), (1 - x, 1 - y)]

        def idx(px, py, pc):
            return 4 * px + 2 * py + pc

        def copy(v, k, blk, to, own=False):
            a, q = parts[v]
            rows = pl.ds(q * (shards[a].shape[0] // AG_CHUNKS), shards[a].shape[0] // AG_CHUNKS)
            return pltpu.make_async_remote_copy(
                src_ref=ins[a].at[rows] if own else outs[a].at[blk, rows], dst_ref=outs[a].at[blk, rows],
                send_sem=send_sems.at[v, k], recv_sem=recv_sems.at[v, k], device_id=to, device_id_type=MESH)

        mine = [pltpu.make_async_copy(ins[a], outs[a].at[me], local_sems.at[a]) for a in range(n)]
        for cp in mine:
            cp.start()
        first = []
        for v in range(len(parts)):
            first.append(copy(v, 0, me, sibling, own=True))
            first += [copy(v, 1 + j, me, (*chip, c), own=True) for j, chip in enumerate(chips[:2])]
        for cp in first:
            cp.start()
        relay_blk = jnp.where(c == 1, idx(1 - x, y, c), idx(x, 1 - y, c))
        relay_to = (jnp.where(c == 1, x, 1 - x), jnp.where(c == 1, 1 - y, y), c)
        passed = []
        for v in range(len(parts)):
            for j, chip in enumerate(chips[:2]):
                copy(v, 1 + j, idx(*chip, c), (x, y, c)).wait_recv()
            cp = copy(v, 3, relay_blk, relay_to)
            cp.start()
            passed.append(cp)
            for j, chip in enumerate(chips):
                if j == 2:
                    copy(v, 3, idx(*chip, c), (x, y, c)).wait_recv()
                cp = copy(v, 4 + j, idx(*chip, c), sibling)
                cp.start()
                passed.append(cp)
        for v in range(len(parts)):
            copy(v, 0, idx(x, y, 1 - c), (x, y, c)).wait_recv()
            for j, chip in enumerate(chips):
                copy(v, 4 + j, idx(*chip, 1 - c), (x, y, c)).wait_recv()
        for cp in first + passed:
            cp.wait_send()
        for cp in mine:
            cp.wait()

    return pl.pallas_call(
        body, name="allgather_weights",
        in_specs=[_any_spec()] * n, out_specs=[_any_spec()] * n,
        out_shape=[jax.ShapeDtypeStruct((N_DEV,) + s.shape, s.dtype) for s in shards],
        scratch_shapes=[pltpu.SemaphoreType.DMA((len(parts), 7)), pltpu.SemaphoreType.DMA((len(parts), 7)),
                        pltpu.SemaphoreType.DMA((n,))],
    )(*shards)


def _peer(x, y, c, k):
    tx = 1 - x if (k >> 2) & 1 else x
    ty = 1 - y if (k >> 1) & 1 else y
    tc = 1 - c if k & 1 else c
    return (tx, ty, tc), 4 * tx + 2 * ty + tc


_PEER_ORDER = (2, 4, 6, 3, 5, 7, 1)


_HBM = pl.BlockSpec(memory_space=pltpu.HBM)
_SEM = pl.BlockSpec(memory_space=pltpu.SEMAPHORE)
_EFFECT = pltpu.SideEffectType.DATAFLOW_SIDE_EFFECTING


def _exchange_copies(srcs, lands, send_sems, recv_sems, gather, half):
    x, y, c, me = _my_pos()
    pick = lambda px, py: None if half is None else ((px == py) if half == 0 else (px != py))
    copies = []
    for k in _PEER_ORDER:
        tgt, tidx = _peer(x, y, c, k)
        for a in range(len(srcs)):
            copies.append((pltpu.make_async_remote_copy(
                src_ref=srcs[a] if gather else srcs[a].at[tidx], dst_ref=lands[a].at[me],
                send_sem=send_sems.at[7 * a + k - 1], recv_sem=recv_sems.at[7 * a + k - 1],
                device_id=tgt, device_id_type=MESH), pick(tgt[0], tgt[1])))
    return copies, pick(x, y)


def _when(cond, fn):
    if cond is None:
        fn()
    else:
        pl.when(cond)(fn)


def _exchange_start(name, srcs, lands=None, after=None, gather=None, half=None):
    n = len(srcs)
    gather = (lands is not None) if gather is None else gather
    if lands is None:
        lands = [lax.empty(g.shape, g.dtype) for g in srcs]
    extra = [] if after is None else [after]

    def body(*refs):
        src_refs, land_refs = refs[:n], refs[n:2 * n]
        send_sems, recv_sems = refs[2 * n + len(extra)], refs[2 * n + len(extra) + 1]
        token = refs[-1]
        for cp, sends in _exchange_copies(src_refs, land_refs, send_sems, recv_sems, gather, half)[0]:
            _when(sends, cp.start)
        token[...] = jnp.zeros_like(token)

    hbm = lambda a: pltpu.with_memory_space_constraint(a, pltpu.HBM)
    outs = pl.pallas_call(
        body, name=name,
        out_shape=(pltpu.SemaphoreType.DMA((7 * n,)), pltpu.SemaphoreType.DMA((7 * n,)),
                   *[pltpu.HBM(g.shape, g.dtype) for g in list(srcs) + list(lands)],
                   jax.ShapeDtypeStruct((8, 128), f32)),
        in_specs=[_HBM] * (2 * n) + [pl.BlockSpec(memory_space=pl.ANY)] * len(extra),
        out_specs=(_SEM, _SEM, *([_HBM] * (2 * n)), pl.BlockSpec(memory_space=pltpu.VMEM)),
        input_output_aliases={i: 2 + i for i in range(2 * n)},
        compiler_params=pltpu.CompilerParams(has_side_effects=_EFFECT),
    )(*[hbm(g) for g in srcs], *[hbm(g) for g in lands], *extra)
    return outs[0], outs[1], list(outs[2:2 + n]), list(outs[2 + n:2 + 2 * n]), outs[-1], gather, half


def _exchange_wait(name, started, after):
    send_sems, recv_sems, srcs, lands, _, gather, half = started
    n = len(srcs)
    after = list(after) if isinstance(after, (list, tuple)) else [after]

    def body(*refs):
        src_refs, land_refs = refs[:n], refs[n:2 * n]
        s_sems, r_sems = refs[2 * n], refs[2 * n + 1]
        copies, receives = _exchange_copies(src_refs, land_refs, s_sems, r_sems, gather, half)
        for cp, sends in copies:
            _when(sends, cp.wait_send)
            _when(receives, cp.wait_recv)

    outs = pl.pallas_call(
        body, name=name,
        out_shape=tuple(pltpu.HBM(a.shape, a.dtype) for a in list(srcs) + list(lands)),
        in_specs=[_HBM] * (2 * n) + [_SEM, _SEM] + [pl.BlockSpec(memory_space=pl.ANY)] * len(after),
        out_specs=tuple([_HBM] * (2 * n)),
        input_output_aliases={i: i for i in range(2 * n)},
        compiler_params=pltpu.CompilerParams(has_side_effects=_EFFECT),
    )(*srcs, *lands, send_sems, recv_sems, *after)
    return list(outs[:n]), list(outs[n:])


SMALL_ROWS = 128


def _small_start(name, sg, after=None):
    return _exchange_start(name, [sg], [lax.empty((N_DEV,) + sg.shape, f32)], after=after)


def _small_sum(name, me, started, after):
    (own,), (slots,) = _exchange_wait(name + "_wait", started, after)

    def body(me_ref, s_ref, own_ref, out_ref):
        acc = None
        for p in range(N_DEV):
            term = lax.cond(me_ref[0] == p, lambda: own_ref[...], lambda p=p: s_ref[p])
            acc = term if acc is None else acc + term
        out_ref[...] = acc

    return pl.pallas_call(
        body, name=name + "_sum",
        in_specs=[pl.BlockSpec(memory_space=pltpu.SMEM), pl.BlockSpec(memory_space=pltpu.VMEM),
                  pl.BlockSpec(memory_space=pltpu.VMEM)],
        out_specs=pl.BlockSpec(memory_space=pltpu.VMEM),
        out_shape=jax.ShapeDtypeStruct(own.shape, f32))(me, slots, own)


def _adam_math(g, w, m, v):
    m = ADAM_B1 * m + (1.0 - ADAM_B1) * g
    v = ADAM_B2 * v + (1.0 - ADAM_B2) * (g * g)
    m_hat = m / (1.0 - ADAM_B1 ** ADAM_STEP)
    v_hat = v / (1.0 - ADAM_B2 ** ADAM_STEP)
    delta = -ADAM_LR * (m_hat / (jnp.sqrt(v_hat) + ADAM_EPS) + ADAM_WD * w)
    return delta, m, v


def _adam_slots(name, me, slots, own, w, m, v, tr, transposed=False):
    rows = slots.shape[1]

    def body(me_ref, s_ref, own_ref, w_ref, m_ref, v_ref, g_ref, d_ref, nm_ref, nv_ref):
        mine = own_ref[...]
        g = None
        for p in range(N_DEV):
            term = lax.cond(me_ref[0] == p, lambda: mine, lambda p=p: s_ref[p]).astype(f32)
            g = term if g is None else g + term
        if transposed:
            g = g.T
        delta, nm, nv = _adam_math(g, w_ref[...], m_ref[...], v_ref[...])
        g_ref[...] = g
        d_ref[...] = delta
        nm_ref[...] = nm
        nv_ref[...] = nv

    mode = dict(pipeline_mode=pl.Buffered(1)) if rows == tr else {}
    if transposed:
        rs = pl.BlockSpec((D, tr), lambda i, me_ref: (0, i))
        rs_in = pl.BlockSpec((D, tr), lambda i, me_ref: (0, i), **mode)
    else:
        rs = pl.BlockSpec((tr, D), lambda i, me_ref: (i, 0))
        rs_in = pl.BlockSpec((tr, D), lambda i, me_ref: (i, 0), **mode)
    return pl.pallas_call(
        body, name=name,
        grid_spec=pltpu.PrefetchScalarGridSpec(
            num_scalar_prefetch=1, grid=(rows // tr,),
            in_specs=[pl.BlockSpec((N_DEV, tr, D), lambda i, me_ref: (0, i, 0), **mode),
                      pl.BlockSpec((None, tr, D), lambda i, me_ref: (me_ref[0], i, 0), **mode), rs_in, rs_in, rs_in],
            out_specs=[rs] * 4),
        out_shape=[jax.ShapeDtypeStruct(w.shape, f32)] * 4,
        compiler_params=_cparams(("parallel",)))(me, slots, own, w, m, v)


def _adam_small(g, w, m, v):
    def body(g_ref, w_ref, m_ref, v_ref, d_ref, nm_ref, nv_ref):
        delta, nm, nv = _adam_math(g_ref[...], w_ref[...], m_ref[...], v_ref[...])
        d_ref[...] = delta
        nm_ref[...] = nm
        nv_ref[...] = nv

    return pl.pallas_call(body, name="adam_small", out_shape=[jax.ShapeDtypeStruct(g.shape, f32)] * 3)(g, w, m, v)


FFN_PAD = 6 * D


_SMALL_PARTS = (("norm1_g", 1), ("gate_b", 2), ("conv_w", CONV_WIDTH), ("conv_b", 1), ("conv_norm_g", 1),
                ("q_norm_g", 1), ("k_norm_g", 1), ("norm2_g", 1), ("ffn_conv_w", 18), ("ffn_conv_b", 6), ("last", 1))


def _small_offsets():
    out, row = {}, 0
    for name, rows in _SMALL_PARTS:
        out[name] = row
        row += -(-rows // 8) * 8
    assert row == SMALL_ROWS
    return out


def _pack_small(norm1_g, gate_b, conv_w, conv_b, conv_norm_g, q_norm_g, k_norm_g, norm2_g, ffn_conv_w, ffn_conv_b,
                last_row=None):
    pad_h = lambda a: jnp.pad(a, ((0, 0), (0, D - HEAD_DIM)))
    pad_f = lambda a: jnp.pad(a, ((0, 0), (0, FFN_PAD - 2 * D_FF))).reshape(-1, D)
    parts = [norm1_g, gate_b.reshape(2, D), conv_w, conv_b, conv_norm_g, pad_h(q_norm_g), pad_h(k_norm_g), norm2_g,
             pad_f(ffn_conv_w), pad_f(ffn_conv_b), jnp.zeros((1, D), f32) if last_row is None else last_row]
    return jnp.concatenate([jnp.pad(p, ((0, -p.shape[0] % 8), (0, 0))) for p in parts], axis=0)


def _unpack_small(p):
    o = _small_offsets()
    rows = lambda name, n: p[o[name]:o[name] + n]
    ffn = lambda a: a.reshape(-1, FFN_PAD)[:, :2 * D_FF]
    return dict(
        norm1_g=rows("norm1_g", 1), gate_b=rows("gate_b", 2).reshape(1, 2 * D), conv_w=rows("conv_w", CONV_WIDTH),
        conv_b=rows("conv_b", 1), conv_norm_g=rows("conv_norm_g", 1), q_norm_g=rows("q_norm_g", 1)[:, :HEAD_DIM],
        k_norm_g=rows("k_norm_g", 1)[:, :HEAD_DIM], norm2_g=rows("norm2_g", 1),
        ffn_conv_w=ffn(rows("ffn_conv_w", 18)), ffn_conv_b=ffn(rows("ffn_conv_b", 6)))


_ADAM_TILE = {896: 128, 704: 704, 128: 128, 352: 176}


def kernel(x, norm1_g, w_in, gate_b, conv_w, conv_b, conv_norm_g, w_conv_out, q_norm_g, k_norm_g, w_attn_out, w_out, norm2_g, w_up, ffn_conv_w, ffn_conv_b, w_down, loss_target, m_norm1_g, m_w_in, m_gate_b, m_conv_w, m_conv_b, m_conv_norm_g, m_w_conv_out, m_q_norm_g, m_k_norm_g, m_w_attn_out, m_w_out, m_norm2_g, m_w_up, m_ffn_conv_w, m_ffn_conv_b, m_w_down, v_norm1_g, v_w_in, v_gate_b, v_conv_w, v_conv_b, v_conv_norm_g, v_w_conv_out, v_q_norm_g, v_k_norm_g, v_w_attn_out, v_w_out, v_norm2_g, v_w_up, v_ffn_conv_w, v_ffn_conv_b, v_w_down):
    BL, S, _ = x.shape
    T = BL * S
    me = 4 * lax.axis_index("x") + 2 * lax.axis_index("y") + lax.axis_index("c")
    xt = x.reshape(T, D)
    target = loss_target.reshape(T, D)

    big = dict(w_in=(w_in[0], m_w_in[0], v_w_in[0]), w_up=(w_up[0], m_w_up[0], v_w_up[0]),
               w_conv_out=(w_conv_out[0], m_w_conv_out[0], v_w_conv_out[0]),
               w_attn_out=(w_attn_out[0], m_w_attn_out[0], v_w_attn_out[0]),
               w_out=(w_out[0], m_w_out[0], v_w_out[0]), w_down=(w_down[0], m_w_down[0], v_w_down[0]))
    order = ["w_in", "w_conv_out", "w_attn_out", "w_out", "w_up", "w_down"]
    shards = [(big[n][0].T if n in ("w_in", "w_up") else big[n][0]).astype(bf16) for n in order]
    gathered = _allgather_rows(shards, 1)
    W = {"w_in": gathered[0].reshape(-1, D)}

    def place_cols(shard, full_cols):
        z = jnp.zeros((shard.shape[0], full_cols), f32)
        return lax.dynamic_update_slice(z, shard, (0, me * shard.shape[1]))

    zr = lambda a: jnp.zeros_like(a)
    conv_local = _pack_small(
        zr(norm1_g), zr(gate_b), place_cols(conv_w[0], D), zr(conv_b), zr(conv_norm_g), zr(q_norm_g), zr(k_norm_g),
        zr(norm2_g), place_cols(ffn_conv_w[0], 2 * D_FF), zr(ffn_conv_b))
    ga_conv = _small_start("gather_conv_start", conv_local, after=gathered[0])
    ga_proj = _exchange_start("gather_start_proj", shards[1:4], gathered[1:4], after=ga_conv[4])
    ga_ffn = _exchange_start("gather_start_ffn", shards[4:6], gathered[4:6], after=ga_proj[4])

    bd = (jnp.arange(128)[:, None] // HEAD_DIM == jnp.arange(128)[None, :] // HEAD_DIM).astype(bf16)
    bias = _attn_bias()
    qg = jnp.tile(q_norm_g, (1, N_HEADS))
    kg = jnp.tile(k_norm_g, (1, N_HEADS))

    z8, h, qn, kn = _in_proj_fwd(xt, norm1_g, W["w_in"], qg, kg, bd, ga_ffn[4])
    conv_all = _unpack_small(_small_sum("gather_conv", me.reshape(1), ga_conv, z8))
    conv_w_full, ffn_w_full = conv_all["conv_w"], conv_all["ffn_conv_w"]
    c = _conv_fwd(z8, conv_w_full, conv_b, S)
    o, ob, lse = _attn_fwd(qn, kn, z8, bias, S)
    for n, g in zip(order[1:4], _exchange_wait("gather_wait_proj", ga_proj, ob)[1]):
        W[n] = g.reshape(-1, D)
    s, ya, yb, mixed = _branches_fwd(c, ob, z8, conv_norm_g, gate_b, W["w_conv_out"], W["w_attn_out"])
    x1, h2 = _out_norm2_fwd(mixed, W["w_out"], xt, norm2_g)
    for n, g in zip(order[4:6], _exchange_wait("gather_wait_ffn", ga_ffn, x1)[1]):
        W[n] = g.reshape(-1, D)
    TNU = D_FF // 2
    u3 = _matmul_call(
        "mm_u", h2, W["w_up"],
        pl.BlockSpec((1024, D), lambda i, j, k: (i, 0)),
        pl.BlockSpec((TNU, D), lambda i, j, k: (j, 0)),
        pl.BlockSpec((None, 1024, TNU), lambda i, j, k: (j // 2, i, j % 2)),
        jax.ShapeDtypeStruct((2, T, D_FF), f32), (T // 1024, 4, 1), "nt", 1, 1024, TNU)
    f = _ffn_fwd(u3, ffn_w_full, ffn_conv_b, S)
    dy, dyb, lacc = _down_loss_fwd(f, W["w_down"], x1, target)
    loss_local = 0.5 / D * jnp.sum(lacc)

    df = _matmul("mm_df", dyb, W["w_down"], "nt", f32, tn=TNU)
    g_w_down = _matmul("mm_dwdn", f, dyb, "tn", bf16, tm=TNU)
    du3, dffn = _ffn_bwd(u3, df, ffn_w_full, ffn_conv_b, S)
    g_w_up = _matmul_call(
        "mm_dwup", du3, h2,
        pl.BlockSpec((None, T, TNU), lambda i, j, k: (i // 2, 0, i % 2)),
        pl.BlockSpec((T, D), lambda i, j, k: (0, 0)),
        pl.BlockSpec((TNU, D), lambda i, j, k: (i, 0)),
        jax.ShapeDtypeStruct((2 * D_FF, D), bf16), (4, 1, 1), "tn", 1, TNU, D)
    blocks8 = lambda a: a.reshape(N_DEV, -1, D)
    ex_ffn = _exchange_start("scatter_start_ffn", [blocks8(g_w_up), blocks8(g_w_down)])
    dx1, dx1b, dg_norm2 = _up_norm2_bwd(du3, W["w_up"], x1, dy, norm2_g, ex_ffn[4])
    g_w_out = _matmul("mm_dwo", mixed, dx1b, "tn", bf16, tm=512)
    dz8 = lax.empty((8, T, D), bf16)
    dya, dyb2, dz8, dg_gate = _out_gate_bwd(dx1b, W["w_out"], z8, gate_b, ya, yb, dz8)
    g_w_conv_out = _matmul("mm_dwco", s, dya, "tn", bf16, tm=512)
    g_w_attn_out = _matmul("mm_dwao", ob, dyb2, "tn", bf16, tm=512)
    ex_proj = _exchange_start("scatter_start_proj", [blocks8(g_w_conv_out), blocks8(g_w_attn_out), blocks8(g_w_out)])
    do = _matmul("mm_do", dyb2, W["w_attn_out"], "nt", f32, after=ex_proj[4])
    dc, dg_convnorm = _convnorm_bwd(dya, W["w_conv_out"], c, conv_norm_g)
    dz8a, dconv = _conv_bwd(dc, z8, conv_w_full, dz8, S)
    dwin_specs = lambda zsec, wsec: (
        pl.BlockSpec((None, T, D), lambda i, j, k: (zsec(i), 0, 0)), pl.BlockSpec((T, D), lambda i, j, k: (0, 0)),
        pl.BlockSpec((1024, D), lambda i, j, k: (wsec(i), 0)), jax.ShapeDtypeStruct((7 * D, D), bf16))
    g_w_in = _matmul_call("mm_dwin_a", dz8a, h, *dwin_specs(lambda i: i, lambda i: jnp.where(i < 2, i, i + 3)),
                          (4, 1, 1), "tn", 1, D, D)
    ex_in_a = _exchange_start("scatter_start_in_a", [blocks8(g_w_in)], half=0)
    dqn, dkn, dv = _attn_bwd(qn, kn, z8, do, o, lse, bias, bd, S, ex_in_a[4])
    dz8b, dg_q, dg_k = _qk_bwd(z8, dqn, dkn, dv, qg, kg, bd, dz8a)
    g_w_in = _matmul_call("mm_dwin_b", dz8b, h, *dwin_specs(lambda i: i + 4, lambda i: i + 2),
                          (3, 1, 1), "tn", 1, D, D, fill=ex_in_a[2][0].reshape(7 * D, D))
    ex_in_b = _exchange_start("scatter_start_in_b", [blocks8(g_w_in)], ex_in_a[3], gather=False, half=1)
    grad_x, dg_norm1 = _in_norm1_bwd(dz8b, W["w_in"], xt, dx1, norm1_g, ex_in_b[4])

    sum8 = lambda a: a.reshape(-1, 8, a.shape[-1]).sum(axis=1)
    dconv_s = sum8(dconv.sum(axis=0))
    dffn_s = dffn.sum(axis=0).reshape(2, 4, 8, D_FF).sum(axis=2)
    dffn_w = jnp.concatenate([dffn_s[0, :3], dffn_s[1, :3]], axis=1)
    dffn_b = jnp.concatenate([dffn_s[0, 3:4], dffn_s[1, 3:4]], axis=1)
    fold = lambda a: sum8(a).reshape(N_HEADS, HEAD_DIM).sum(axis=0)[None]
    small_g_local = _pack_small(
        sum8(dg_norm1), sum8(dg_gate), dconv_s[:CONV_WIDTH], dconv_s[CONV_WIDTH:], sum8(dg_convnorm),
        fold(dg_q), fold(dg_k), sum8(dg_norm2), dffn_w, dffn_b,
        last_row=jnp.pad(loss_local.reshape(1, 1), ((0, 0), (0, D - 1))))
    sg_start = _small_start("small_grads_start", small_g_local)

    own, slots = {}, {}
    for tag, ex, names_ in (("ffn", ex_ffn, ("w_up", "w_down")),
                            ("proj", ex_proj, ("w_conv_out", "w_attn_out", "w_out"))):
        sent, landed = _exchange_wait("scatter_wait_" + tag, ex, sg_start[4])
        for n, src, land in zip(names_, sent, landed):
            own[n], slots[n] = src, land
    sent, landed = _exchange_wait("scatter_wait_in_a", ex_in_a[:2] + (ex_in_b[2], ex_in_b[3]) + ex_in_a[4:],
                                  sg_start[4])
    sent, landed = _exchange_wait("scatter_wait_in_b", ex_in_b[:2] + (sent, landed) + ex_in_b[4:], sg_start[4])
    own["w_in"], slots["w_in"] = sent[0], landed[0]

    res, adam_done = {}, []
    for n in order:
        w, m, v = big[n]
        outs = _adam_slots("adam_" + n, me.reshape(1), slots[n], own[n], w, m, v, _ADAM_TILE[slots[n].shape[1]],
                           transposed=n in ("w_in", "w_up"))
        adam_done.append(outs[0])
        res[n] = [a[None] for a in outs]
    small_g = _small_sum("small_grads", me.reshape(1), sg_start, adam_done)
    loss = small_g[_small_offsets()["last"], 0]

    col = lambda a, width: lax.dynamic_slice(a, (0, me * width), (a.shape[0], width))
    small_w_true = _pack_small(norm1_g, gate_b, conv_w_full, conv_b, conv_norm_g, q_norm_g, k_norm_g, norm2_g,
                               ffn_w_full, ffn_conv_b)
    place_m = lambda a, full: place_cols(a[0], full)
    small_m = _pack_small(m_norm1_g, m_gate_b, place_m(m_conv_w, D), m_conv_b, m_conv_norm_g, m_q_norm_g, m_k_norm_g,
                          m_norm2_g, place_m(m_ffn_conv_w, 2 * D_FF), m_ffn_conv_b)
    small_v = _pack_small(v_norm1_g, v_gate_b, place_m(v_conv_w, D), v_conv_b, v_conv_norm_g, v_q_norm_g, v_k_norm_g,
                          v_norm2_g, place_m(v_ffn_conv_w, 2 * D_FF), v_ffn_conv_b)
    sd, sm, sv = _adam_small(small_g, small_w_true, small_m, small_v)
    for i, packed in enumerate((small_g, sd, sm, sv)):
        u = _unpack_small(packed)
        u["conv_w"] = col(u["conv_w"], D // N_DEV)
        u["ffn_conv_w"] = col(u["ffn_conv_w"], 2 * D_FF // N_DEV)
        for n, a in u.items():
            res.setdefault(n, [None] * 4)[i] = a[None] if n in ("conv_w", "ffn_conv_w") else a

    names = ["norm1_g", "w_in", "gate_b", "conv_w", "conv_b", "conv_norm_g", "w_conv_out", "q_norm_g", "k_norm_g",
             "w_attn_out", "w_out", "norm2_g", "w_up", "ffn_conv_w", "ffn_conv_b", "w_down"]
    out = [loss, grad_x.reshape(BL, S, D)]
    for i in range(4):
        out += [res[n][i] for n in names]
    return tuple(out)
```

```python
import functools

import jax
import jax.numpy as jnp
import numpy as np
from jax import lax
from jax.experimental import pallas as pl
from jax.experimental.pallas import tpu as pltpu

f32 = jnp.float32
bf16 = jnp.bfloat16

D = 1024
N_HEADS = 16
HEAD_DIM = 64
CONV_WIDTH = 31
D_FF = 2816
GROUPS = ((128, 1), (512, 4), (2048, 16))
ATTN_BLOCK = 128
EPS = 1e-6
N_DEV = 8
MESH = pl.DeviceIdType.MESH

ADAM_LR = 0.001
ADAM_B1 = 0.9
ADAM_B2 = 0.999
ADAM_EPS = 1e-08
ADAM_WD = 0.01
ADAM_STEP = 10

VMEM_LIMIT = 56 * 1024 * 1024
MASK_BIAS = 1e30

Z_AVAL, Z_AGATE, Z_GA, Z_GB, Z_Q, Z_K, Z_V = 0, 1, 2, 3, 4, 5, 6


_W_OF_Z = (0, 1, 5, 6, 2, 3, 4)


def _wsec_of_zsec(j):
    return jnp.where(j < 2, j, jnp.where(j < 4, j + 3, j - 2))


def _zsec_of_wsec(w):
    return jnp.where(w < 2, w, jnp.where(w < 5, w + 2, w - 3))


def _sig(x):
    return 1.0 / (1.0 + jnp.exp(-x))


def _colsum8(x):
    return x.reshape(-1, 8, x.shape[-1]).sum(axis=0)


def _cparams(sem):
    return pltpu.CompilerParams(dimension_semantics=sem, vmem_limit_bytes=VMEM_LIMIT)


def _my_pos():
    x, y, c = lax.axis_index("x"), lax.axis_index("y"), lax.axis_index("c")
    return x, y, c, 4 * x + 2 * y + c


_DIMS = {"nn": ((1,), (0,)), "nt": ((1,), (1,)), "tn": ((0,), (0,))}


def _matmul_call(name, a, b, a_spec, b_spec, o_spec, out_shape, grid, mode, nk, tm, tn, after=None, fill=None):
    dims = (_DIMS[mode], ((), ()))
    extra = ([] if after is None else [after]) + ([] if fill is None else [fill])

    def body(a_ref, b_ref, *rest):
        o_ref, scratch = rest[len(extra)], rest[len(extra) + 1:]
        part = lax.dot_general(a_ref[...], b_ref[...], dims, preferred_element_type=f32)
        if nk == 1:
            o_ref[...] = part.astype(o_ref.dtype)
        else:
            acc = scratch[0]
            k = pl.program_id(2)

            @pl.when(k == 0)
            def _():
                acc[...] = part

            @pl.when(k > 0)
            def _():
                acc[...] += part

            @pl.when(k == nk - 1)
            def _():
                o_ref[...] = acc[...].astype(o_ref.dtype)

    scratch = [] if nk == 1 else [pltpu.VMEM((tm, tn), f32)]
    return pl.pallas_call(
        body, name=name, grid=grid, in_specs=[a_spec, b_spec] + [pl.BlockSpec(memory_space=pl.ANY)] * len(extra),
        out_specs=o_spec, out_shape=out_shape, input_output_aliases={} if fill is None else {1 + len(extra): 0},
        scratch_shapes=scratch, compiler_params=_cparams(("parallel", "parallel", "arbitrary")),
    )(a, b, *extra)


def _matmul(name, a, b, mode, out_dtype, tm=1024, tn=1024, tk=None, after=None):
    if mode == "nn":
        (M, K), (_, N) = a.shape, b.shape
    elif mode == "nt":
        (M, K), (N, _) = a.shape, b.shape
    else:
        (K, M), (_, N) = a.shape, b.shape
    tm, tn = min(tm, M), min(tn, N)
    tk = K if tk is None else tk
    nk = K // tk
    assert M % tm == 0 and N % tn == 0 and K % tk == 0
    if mode == "tn":
        a_spec = pl.BlockSpec((tk, tm), lambda i, j, k: (k, i))
    else:
        a_spec = pl.BlockSpec((tm, tk), lambda i, j, k: (i, k))
    if mode == "nt":
        b_spec = pl.BlockSpec((tn, tk), lambda i, j, k: (j, k))
    else:
        b_spec = pl.BlockSpec((tk, tn), lambda i, j, k: (k, j))
    o_spec = pl.BlockSpec((tm, tn), lambda i, j, k: (i, j))
    return _matmul_call(name, a, b, a_spec, b_spec, o_spec, jax.ShapeDtypeStruct((M, N), out_dtype),
                        (M // tm, N // tn, nk), mode, nk, tm, tn, after=after)


FTM = 512


def _matmul_fused(name, a, b, pairs, epilogue, extras, consts, outs, nt=False, sums=False, passed=(), aliases=None):
    sa, M, kk = a.shape
    na = max(i for i, _ in pairs) + 1
    ne, nc, npass = len(extras), len(consts), len(passed)
    dims = (_DIMS["nt" if nt else "nn"], ((), ()))

    def body(a_ref, b_ref, *rest):
        acc = None
        for i, j in pairs:
            part = lax.dot_general(a_ref[i], b_ref[j], dims, preferred_element_type=f32)
            acc = part if acc is None else acc + part
        epilogue(acc, rest[:ne], rest[ne:ne + nc], rest[ne + nc + npass:])

    whole = lambda arr: pl.BlockSpec(arr.shape, lambda i, nd=arr.ndim: (0,) * nd, pipeline_mode=pl.Buffered(1))
    io_alias = {2 + ne + nc + k: v for k, v in (aliases or {}).items()}
    return pl.pallas_call(
        body, name=name, grid=(M // FTM,),
        in_specs=[pl.BlockSpec((na, FTM, kk), lambda i: (0, i, 0)), whole(b)] + [s for _, s in extras]
        + [whole(c) for c in consts] + [pl.BlockSpec(memory_space=pl.ANY)] * npass,
        out_specs=[s for _, s in outs], out_shape=[s for s, _ in outs], input_output_aliases=io_alias,
        compiler_params=_cparams(("arbitrary" if sums else "parallel",)),
    )(a, b, *[x for x, _ in extras], *consts, *passed)


def _frows(c=D):
    return pl.BlockSpec((FTM, c), lambda i: (i, 0))


def _fsec(s):
    return pl.BlockSpec((None, FTM, D), lambda i: (s, i, 0))


def _rowshape(T, dtype, c=D):
    return (jax.ShapeDtypeStruct((T, c), dtype), _frows(c))


def _sumshape(c=D):
    return (jax.ShapeDtypeStruct((8, c), f32), pl.BlockSpec((8, c), lambda i: (0, 0)))


def _add_colsum(ref, x, cols=None):
    @pl.when(pl.program_id(0) == 0)
    def _():
        if cols is None:
            ref[...] = jnp.zeros_like(ref)
        else:
            ref[:, cols] = jnp.zeros((8, x.shape[-1]), f32)

    if cols is None:
        ref[...] += _colsum8(x)
    else:
        ref[:, cols] += _colsum8(x)


TT = 512


def _rows(c, cb=0, tt=TT):
    return pl.BlockSpec((tt, c), lambda i: (i, cb))


def _sec(s, tt=TT):
    return pl.BlockSpec((None, tt, D), lambda i: (s, i, 0))


def _const(shape):
    return pl.BlockSpec(shape, lambda i: (0,) * len(shape))


def _acc_spec(c):
    return pl.BlockSpec((8, c), lambda i: (0, 0))


def _rms(x):
    return lax.rsqrt(jnp.mean(x * x, axis=-1, keepdims=True) + EPS)


def _rms_bwd(dy_g, xn, rstd):
    return rstd * (dy_g - xn * jnp.mean(dy_g * xn, axis=-1, keepdims=True))


def _head_sum(x, bd):
    parts = []
    for cb in range(x.shape[-1] // 128):
        xb = x[:, cb * 128:(cb + 1) * 128]
        hi = xb.astype(bf16)
        lo = (xb - hi.astype(f32)).astype(bf16)
        parts.append(jnp.dot(hi, bd, preferred_element_type=f32) + jnp.dot(lo, bd, preferred_element_type=f32))
    return parts[0] if len(parts) == 1 else jnp.concatenate(parts, axis=1)


ZTM = 1024


def _in_proj_fwd(x, g, w_in_t, qg, kg, bd, after):
    T = x.shape[0]

    def body(x_ref, g_ref, w_ref, qg_ref, kg_ref, bd_ref, after_ref, z_ref, h_ref, qn_ref, kn_ref, hbuf):
        del after_ref
        j = pl.program_id(1)

        @pl.when(j == 0)
        def _():
            xv = x_ref[...]
            hv = (xv * _rms(xv) * g_ref[...]).astype(bf16)
            hbuf[...] = hv
            h_ref[...] = hv

        z = lax.dot_general(hbuf[...], w_ref[...], (_DIMS["nt"], ((), ())), preferred_element_type=f32)
        z_ref[...] = z

        def head_norm(gain_ref, scale):
            return z * lax.rsqrt(_head_sum(z * z, bd_ref[...]) * (1.0 / HEAD_DIM) + EPS) * gain_ref[...] * scale

        @pl.when(j == Z_Q)
        def _():
            qn_ref[...] = head_norm(qg_ref, HEAD_DIM ** -0.5)

        @pl.when(j == Z_K)
        def _():
            kn_ref[...] = head_norm(kg_ref, 1.0)

    tile = pl.BlockSpec((ZTM, D), lambda i, j: (i, 0))
    row = pl.BlockSpec((1, D), lambda i, j: (0, 0))
    return pl.pallas_call(
        body, name="mm_z", grid=(T // ZTM, 7),
        in_specs=[tile, row, pl.BlockSpec((D, D), lambda i, j: (_wsec_of_zsec(j), 0)), row, row,
                  pl.BlockSpec((128, 128), lambda i, j: (0, 0)), pl.BlockSpec(memory_space=pl.ANY)],
        out_specs=[pl.BlockSpec((None, ZTM, D), lambda i, j: (j, i, 0)), tile, tile, tile],
        out_shape=[jax.ShapeDtypeStruct((8, T, D), f32), jax.ShapeDtypeStruct((T, D), bf16),
                   jax.ShapeDtypeStruct((T, D), f32), jax.ShapeDtypeStruct((T, D), f32)],
        scratch_shapes=[pltpu.VMEM((ZTM, D), bf16)],
        compiler_params=_cparams(("parallel", "arbitrary")))(x, g, w_in_t, qg, kg, bd, after)


def _branches_fwd(c, ob, z8, g, gate_b, w_conv_out, w_attn_out):
    T = c.shape[0]

    def epilogue(yb, extra, const, out):
        cv = extra[0][...]
        r = cv * _rms(cv) * const[0][...]
        s = (r * _sig(r)).astype(bf16)
        ya = jnp.dot(s, const[2][...], preferred_element_type=f32)
        b_ref = const[1]
        g_a = _sig(extra[1][...] + b_ref[:, :D])
        g_b = _sig(extra[2][...] + b_ref[:, D:])
        out[0][...] = s
        out[1][...] = ya
        out[2][...] = yb
        out[3][...] = (g_a * ya + g_b * yb).astype(bf16)

    return _matmul_fused("mm_branches", ob[None], w_attn_out[None], ((0, 0),), epilogue,
                         [(c, _frows()), (z8, _fsec(Z_GA)), (z8, _fsec(Z_GB))], [g, gate_b, w_conv_out],
                         [_rowshape(T, bf16), _rowshape(T, f32), _rowshape(T, f32), _rowshape(T, bf16)])


def _out_norm2_fwd(mixed, w_out, x, g):
    T = x.shape[0]

    def epilogue(acc, extra, const, out):
        x1 = extra[0][...] + acc
        out[0][...] = x1
        out[1][...] = (x1 * _rms(x1) * const[0][...]).astype(bf16)

    return _matmul_fused("mm_t1_norm2", mixed[None], w_out[None], ((0, 0),), epilogue, [(x, _frows())], [g],
                         [_rowshape(T, f32), _rowshape(T, bf16)])


def _down_loss_fwd(f, w_down, x1, target):
    T = x1.shape[0]

    def epilogue(acc, extra, const, out):
        diff = extra[0][...] + acc - extra[1][...]
        dy = diff * (1.0 / D)
        out[0][...] = dy
        out[1][...] = dy.astype(bf16)
        _add_colsum(out[2], diff * diff)

    return _matmul_fused("mm_t2_loss", f[None], w_down[None], ((0, 0),), epilogue, [(x1, _frows()), (target, _frows())],
                         [], [_rowshape(T, f32), _rowshape(T, bf16), _sumshape()], sums=True)


def _up_norm2_bwd(du3, w_up_t, x1, dy, g, token):
    T = x1.shape[0]

    def epilogue(dh, extra, const, out):
        x1v = extra[0][...]
        rstd = _rms(x1v)
        xn = x1v * rstd
        dx1 = extra[1][...] + _rms_bwd(dh * const[0][...], xn, rstd)
        out[0][...] = dx1
        out[1][...] = dx1.astype(bf16)
        _add_colsum(out[2], dh * xn)

    return _matmul_fused("mm_dh2_norm2", du3, w_up_t.reshape(2, D_FF, D), ((0, 0), (1, 1)), epilogue,
                         [(x1, _frows()), (dy, _frows())], [g],
                         [_rowshape(T, f32), _rowshape(T, bf16), _sumshape()], sums=True, passed=[token])


def _out_gate_bwd(dx1b, w_out, z8, gate_b, ya, yb, dz8):
    T = ya.shape[0]

    def epilogue(dm, extra, const, out):
        b_ref = const[0]
        g_a = _sig(extra[0][...] + b_ref[:, :D])
        g_b = _sig(extra[1][...] + b_ref[:, D:])
        out[0][...] = (dm * g_a).astype(bf16)
        out[1][...] = (dm * g_b).astype(bf16)
        dla = dm * extra[2][...] * g_a * (1.0 - g_a)
        dlb = dm * extra[3][...] * g_b * (1.0 - g_b)
        out[2][0] = dla.astype(bf16)
        out[2][1] = dlb.astype(bf16)
        _add_colsum(out[3], dla, slice(0, D))
        _add_colsum(out[3], dlb, slice(D, 2 * D))

    return _matmul_fused(
        "mm_dmixed_gate", dx1b[None], w_out[None], ((0, 0),), epilogue,
        [(z8, _fsec(Z_GA)), (z8, _fsec(Z_GB)), (ya, _frows()), (yb, _frows())], [gate_b],
        [_rowshape(T, bf16), _rowshape(T, bf16),
         (jax.ShapeDtypeStruct(dz8.shape, bf16), pl.BlockSpec((2, FTM, D), lambda i: (1, i, 0))), _sumshape(2 * D)],
        nt=True, sums=True, passed=[dz8], aliases={0: 2})


def _convnorm_bwd(dya, w_conv_out, c, g):
    T = c.shape[0]

    def epilogue(ds, extra, const, out):
        cv = extra[0][...]
        rstd = _rms(cv)
        r0 = cv * rstd
        gv = const[0][...]
        r = r0 * gv
        sg = _sig(r)
        dr = ds * sg * (1.0 + r * (1.0 - sg))
        out[0][...] = _rms_bwd(dr * gv, r0, rstd)
        _add_colsum(out[1], dr * r0)

    return _matmul_fused("mm_ds_convnorm", dya[None], w_conv_out[None], ((0, 0),), epilogue, [(c, _frows())], [g],
                         [_rowshape(T, f32), _sumshape()], nt=True, sums=True)


def _qk_bwd(z8, dqn, dkn, dv, qg, kg, bd, dz8):
    T = dqn.shape[0]

    def body(q_ref, k_ref, dqn_ref, dkn_ref, dv_ref, qg_ref, kg_ref, bd_ref, dz_in, dz_ref, dqg_ref, dkg_ref):
        del dz_in
        bdv = bd_ref[...]

        @pl.when(pl.program_id(0) == 0)
        def _():
            dqg_ref[...] = jnp.zeros_like(dqg_ref)
            dkg_ref[...] = jnp.zeros_like(dkg_ref)

        def one(raw, dn_scaled, g, dg_ref, sec):
            rstd = lax.rsqrt(_head_sum(raw * raw, bdv) * (1.0 / HEAD_DIM) + EPS)
            n = raw * rstd
            dg_ref[...] += _colsum8(dn_scaled * n)
            dn = dn_scaled * g
            draw = rstd * (dn - n * (_head_sum(dn * n, bdv) * (1.0 / HEAD_DIM)))
            dz_ref[sec] = draw.astype(bf16)

        one(q_ref[...], dqn_ref[...] * (HEAD_DIM ** -0.5), qg_ref[...], dqg_ref, 0)
        one(k_ref[...], dkn_ref[...], kg_ref[...], dkg_ref, 1)
        dz_ref[2] = dv_ref[...].astype(bf16)
        dz_ref[3] = jnp.zeros((TT, D), bf16)

    return pl.pallas_call(
        body, name="qk_bwd", grid=(T // TT,),
        in_specs=[_sec(Z_Q), _sec(Z_K), _rows(D), _rows(D), _rows(D), _const((1, D)), _const((1, D)),
                  _const((128, 128)), pl.BlockSpec(memory_space=pl.ANY)],
        out_specs=[pl.BlockSpec((4, TT, D), lambda i: (1, i, 0)), _acc_spec(D), _acc_spec(D)],
        out_shape=[jax.ShapeDtypeStruct(dz8.shape, bf16), jax.ShapeDtypeStruct((8, D), f32),
                   jax.ShapeDtypeStruct((8, D), f32)],
        input_output_aliases={8: 0},
        compiler_params=_cparams(("arbitrary",)))(z8, z8, dqn, dkn, dv, qg, kg, bd, dz8)


def _in_norm1_bwd(dz8, w_in_t, x, dx1, g, token):
    T = x.shape[0]

    def epilogue(dh, extra, const, out):
        xv = extra[0][...]
        rstd = _rms(xv)
        xn = xv * rstd
        out[0][...] = extra[1][...] + _rms_bwd(dh * const[0][...], xn, rstd)
        _add_colsum(out[1], dh * xn)

    return _matmul_fused("mm_dh_norm1", dz8, w_in_t.reshape(7, D, D), tuple(zip(range(7), _W_OF_Z)), epilogue,
                         [(x, _frows()), (dx1, _frows())], [g], [_rowshape(T, f32), _sumshape()],
                         sums=True, passed=[token])


CCW = 256
CR = 64
HALO = 32


def _conv_fwd(z8, conv_w, conv_b, S):
    T = z8.shape[1]
    nb = T // S
    ncb = D // CCW

    def body(av_ref, ag_ref, w_ref, b_ref, c_ref, pad):
        pad[0:HALO, :] = jnp.zeros((HALO, CCW), f32)

        def fill(i, carry):
            r0 = pl.multiple_of(i * 256, 256)
            pad[pl.ds(HALO + r0, 256), :] = av_ref[pl.ds(r0, 256), :] * _sig(ag_ref[pl.ds(r0, 256), :])
            return carry

        lax.fori_loop(0, S // 256, fill, 0)
        bias = b_ref[...]

        def chunk(i, carry):
            r0 = pl.multiple_of(i * CR, CR)
            win = pad[pl.ds(r0, CR + HALO), :]
            acc = jnp.zeros((CR, CCW), f32) + bias
            for s in range(8):
                part = None
                for m in range((CONV_WIDTH - 1 - s) // 8 + 1):
                    j = CONV_WIDTH - 1 - 8 * m - s
                    term = win[24 - 8 * m:24 - 8 * m + CR + 8, :] * w_ref[j:j + 1, :]
                    part = term if part is None else part + term
                acc = acc + part[8 - s:8 - s + CR, :]
            c_ref[pl.ds(r0, CR), :] = acc
            return carry

        lax.fori_loop(0, S // CR, chunk, 0)

    zs = lambda s: pl.BlockSpec((None, S, CCW), lambda b, cb: (s, b, cb))
    return pl.pallas_call(
        body, name="conv_fwd", grid=(nb, ncb),
        in_specs=[zs(Z_AVAL), zs(Z_AGATE), pl.BlockSpec((CONV_WIDTH, CCW), lambda b, cb: (0, cb)),
                  pl.BlockSpec((1, CCW), lambda b, cb: (0, cb))],
        out_specs=pl.BlockSpec((S, CCW), lambda b, cb: (b, cb)),
        out_shape=jax.ShapeDtypeStruct((T, D), f32),
        scratch_shapes=[pltpu.VMEM((S + HALO, CCW), f32)],
        compiler_params=_cparams(("parallel", "parallel")))(z8, z8, conv_w, conv_b)


def _conv_bwd(dc, z8, conv_w, dz8, S):
    T = dc.shape[0]
    nb = T // S
    ncb = D // CCW

    def body(dc_ref, av_ref, ag_ref, w_ref, dz_in, dz_ref, dw_ref, apad, dpad, shbuf):
        del dz_in
        apad[0:HALO, :] = jnp.zeros((HALO, CCW), f32)
        dpad[S:S + HALO, :] = jnp.zeros((HALO, CCW), f32)
        dw_ref[...] = jnp.zeros_like(dw_ref)

        def fill(i, carry):
            r0 = pl.multiple_of(i * 256, 256)
            apad[pl.ds(HALO + r0, 256), :] = av_ref[pl.ds(r0, 256), :] * _sig(ag_ref[pl.ds(r0, 256), :])
            dpad[pl.ds(r0, 256), :] = dc_ref[pl.ds(r0, 256), :]
            return carry

        lax.fori_loop(0, S // 256, fill, 0)

        def chunk(i, carry):
            r0 = pl.multiple_of(i * CR, CR)
            dwin = dpad[pl.ds(r0, CR + HALO), :]
            da = jnp.zeros((CR, CCW), f32)
            for s in range(8):
                shbuf[...] = dwin[s:s + CR, :]
                dshift = shbuf[...]
                part = None
                for m in range((CONV_WIDTH - 1 - s) // 8 + 1):
                    j = CONV_WIDTH - 1 - 8 * m - s
                    term = dwin[8 * m:8 * m + CR + 8, :] * w_ref[j:j + 1, :]
                    part = term if part is None else part + term
                    a_lag = apad[pl.ds(r0 + HALO - 8 * m, CR), :]
                    dw_ref[8 * j:8 * j + 8, :] += _colsum8(dshift * a_lag)
                da = da + part[s:s + CR, :]
            dw_ref[8 * CONV_WIDTH:8 * CONV_WIDTH + 8, :] += _colsum8(dwin[0:CR, :])
            av = av_ref[pl.ds(r0, CR), :]
            sg = _sig(ag_ref[pl.ds(r0, CR), :])
            dz_ref[0, pl.ds(r0, CR), :] = (da * sg).astype(bf16)
            dz_ref[1, pl.ds(r0, CR), :] = (da * av * sg * (1.0 - sg)).astype(bf16)
            return carry

        lax.fori_loop(0, S // CR, chunk, 0)

    zs = lambda s: pl.BlockSpec((None, S, CCW), lambda b, cb: (s, b, cb))
    return pl.pallas_call(
        body, name="conv_bwd", grid=(nb, ncb),
        in_specs=[pl.BlockSpec((S, CCW), lambda b, cb: (b, cb)), zs(Z_AVAL), zs(Z_AGATE),
                  pl.BlockSpec((CONV_WIDTH, CCW), lambda b, cb: (0, cb)), pl.BlockSpec(memory_space=pl.ANY)],
        out_specs=[pl.BlockSpec((2, S, CCW), lambda b, cb: (0, b, cb)),
                   pl.BlockSpec((None, 256, CCW), lambda b, cb: (b, 0, cb))],
        out_shape=[jax.ShapeDtypeStruct(dz8.shape, bf16), jax.ShapeDtypeStruct((nb, 256, D), f32)],
        input_output_aliases={4: 0},
        scratch_shapes=[pltpu.VMEM((S + HALO, CCW), f32), pltpu.VMEM((S + HALO, CCW), f32),
                        pltpu.VMEM((CR, CCW), f32)],
        compiler_params=_cparams(("parallel", "parallel")))(dc, z8, z8, conv_w, dz8)


FR = 128
NFB = D_FF // CCW
FBW = 128


def _ffn_window(ref, i, r0):
    return ref[pl.ds(r0 - 8, FR + 8), :]


def _ffn_u(win, w_ref, b_ref):
    return (win[6:6 + FR, :] * w_ref[0:1, :] + win[7:7 + FR, :] * w_ref[1:2, :]
            + win[8:8 + FR, :] * w_ref[2:3, :] + b_ref[...])


def _ffn_fwd(u3, ffn_w, ffn_b, S):
    T = u3.shape[1]
    nb = T // S

    def body(uv_ref, ug_ref, wv_ref, wg_ref, bv_ref, bg_ref, f_ref):
        def chunk(first, i):
            r0 = 0 if first else pl.multiple_of(i * FR, FR)
            if first:
                z = jnp.zeros((8, CCW), f32)
                wv = jnp.concatenate([z, uv_ref[0:FR, :]], axis=0)
                wg = jnp.concatenate([z, ug_ref[0:FR, :]], axis=0)
            else:
                wv = _ffn_window(uv_ref, i, r0)
                wg = _ffn_window(ug_ref, i, r0)
            u_val = _ffn_u(wv, wv_ref, bv_ref)
            u_gate = _ffn_u(wg, wg_ref, bg_ref)
            f_ref[pl.ds(r0, FR), :] = (u_gate * _sig(u_gate) * u_val).astype(bf16)

        chunk(True, 0)

        def loop(i, carry):
            chunk(False, i)
            return carry

        lax.fori_loop(1, S // FR, loop, 0)

    us = lambda h: pl.BlockSpec((None, S, CCW), lambda b, cb: (h, b, cb))
    ws = lambda h: pl.BlockSpec((3, CCW), lambda b, cb: (0, h * NFB + cb))
    bs = lambda h: pl.BlockSpec((1, CCW), lambda b, cb: (0, h * NFB + cb))
    return pl.pallas_call(
        body, name="ffn_fwd", grid=(nb, NFB),
        in_specs=[us(0), us(1), ws(0), ws(1), bs(0), bs(1)],
        out_specs=pl.BlockSpec((S, CCW), lambda b, cb: (b, cb)),
        out_shape=jax.ShapeDtypeStruct((T, D_FF), bf16),
        compiler_params=_cparams(("parallel", "parallel")))(u3, u3, ffn_w, ffn_w, ffn_b, ffn_b)


def _ffn_bwd(u3, df, ffn_w, ffn_b, S):
    T = u3.shape[1]
    nb = T // S

    def body(uv_ref, ug_ref, df_ref, wv_ref, wg_ref, bv_ref, bg_ref, du_ref, dw_ref, dvpad, dgpad, shbuf):
        dvpad[S:S + 8, :] = jnp.zeros((8, FBW), f32)
        dgpad[S:S + 8, :] = jnp.zeros((8, FBW), f32)
        dw_ref[...] = jnp.zeros_like(dw_ref)

        def chunk(first, i):
            r0 = 0 if first else pl.multiple_of(i * FR, FR)
            if first:
                z = jnp.zeros((8, FBW), f32)
                wv = jnp.concatenate([z, uv_ref[0:FR, :]], axis=0)
                wg = jnp.concatenate([z, ug_ref[0:FR, :]], axis=0)
            else:
                wv = _ffn_window(uv_ref, i, r0)
                wg = _ffn_window(ug_ref, i, r0)
            taps = []
            for h, win in enumerate((wv, wg)):
                shbuf[2 * h] = win[6:6 + FR, :]
                shbuf[2 * h + 1] = win[7:7 + FR, :]
                taps.append((shbuf[2 * h], shbuf[2 * h + 1], win[8:8 + FR, :]))
            conv = lambda x, w_ref, b_ref: (x[0] * w_ref[0:1, :] + x[1] * w_ref[1:2, :] + x[2] * w_ref[2:3, :]
                                            + b_ref[...])
            u_val = conv(taps[0], wv_ref, bv_ref)
            u_gate = conv(taps[1], wg_ref, bg_ref)
            dfc = df_ref[pl.ds(r0, FR), :]
            sg = _sig(u_gate)
            d_val = dfc * u_gate * sg
            d_gate = dfc * u_val * sg * (1.0 + u_gate * (1.0 - sg))
            dvpad[pl.ds(r0, FR), :] = d_val
            dgpad[pl.ds(r0, FR), :] = d_gate
            for h, dd in enumerate((d_val, d_gate)):
                for j in range(3):
                    dw_ref[h, 8 * j:8 * j + 8, :] += _colsum8(dd * taps[h][j])
                dw_ref[h, 24:32, :] += _colsum8(dd)

        chunk(True, 0)

        def loop(i, carry):
            chunk(False, i)
            return carry

        lax.fori_loop(1, S // FR, loop, 0)

        def back(i, carry):
            r0 = pl.multiple_of(i * FR, FR)
            for h, (dpad, w_ref) in enumerate(((dvpad, wv_ref), (dgpad, wg_ref))):
                win = dpad[pl.ds(r0, FR + 8), :]
                du = (win[0:FR, :] * w_ref[2:3, :] + win[1:1 + FR, :] * w_ref[1:2, :]
                      + win[2:2 + FR, :] * w_ref[0:1, :])
                du_ref[h, pl.ds(r0, FR), :] = du.astype(bf16)
            return carry

        lax.fori_loop(0, S // FR, back, 0)

    ncb = D_FF // FBW
    us = lambda h: pl.BlockSpec((None, S, FBW), lambda b, cb: (h, b, cb))
    ws = lambda h: pl.BlockSpec((3, FBW), lambda b, cb: (0, h * ncb + cb))
    bs = lambda h: pl.BlockSpec((1, FBW), lambda b, cb: (0, h * ncb + cb))
    return pl.pallas_call(
        body, name="ffn_bwd", grid=(nb, ncb),
        in_specs=[us(0), us(1), pl.BlockSpec((S, FBW), lambda b, cb: (b, cb)), ws(0), ws(1), bs(0), bs(1)],
        out_specs=[pl.BlockSpec((2, S, FBW), lambda b, cb: (0, b, cb)),
                   pl.BlockSpec((None, 2, 32, FBW), lambda b, cb: (b, 0, 0, cb))],
        out_shape=[jax.ShapeDtypeStruct((2, T, D_FF), bf16), jax.ShapeDtypeStruct((nb, 2, 32, D_FF), f32)],
        scratch_shapes=[pltpu.VMEM((S + 8, FBW), f32), pltpu.VMEM((S + 8, FBW), f32),
                        pltpu.VMEM((4, FR, FBW), f32)],
        compiler_params=_cparams(("parallel", "parallel")))(u3, u3, df, ffn_w, ffn_w, ffn_b, ffn_b)


AB = ATTN_BLOCK


def _attn_bias_np():
    slopes = (np.float32(2.0) ** (np.float32(-8.0) * np.arange(1, N_HEADS + 1, dtype=np.float32)
                                  / np.float32(N_HEADS))).astype(np.float32)
    steps = (np.arange(AB)[:, None] + AB) - np.arange(2 * AB)[None, :]
    own = (np.arange(2 * AB) >= AB)[None, :]
    out = []
    for window, dil in GROUPS:
        valid = (steps >= 0) & (steps <= window // dil)
        dist = slopes[:, None, None] * (steps * dil).astype(np.float32)[None]
        kinds = [np.where(v[None], dist, np.float32(MASK_BIAS)) for v in (valid, valid & own)]
        out.append(np.stack(kinds, axis=1))
    return np.stack(out).astype(np.float32)


def _attn_bias():
    return jnp.asarray(_attn_bias_np())


def _head_masks():
    lane = lax.broadcasted_iota(jnp.int32, (1, 128), 1)
    return (lane < HEAD_DIM, lane >= HEAD_DIM)


def _perm_chunks(S, d):
    L = S // d
    ch = min(L, 256)
    out = []
    for r in range(d):
        for c in range(L // ch):
            start = r + d * ch * c
            out.append((pl.ds(start, ch, stride=d) if d > 1 else pl.ds(start, ch), r * L + c * ch, ch))
    return out


def _stack_heads(x, masks):
    return jnp.concatenate([jnp.where(masks[0], x, 0), jnp.where(masks[1], x, 0)], axis=0)


def _block_row(j):
    return j * AB if isinstance(j, int) else pl.multiple_of(j * AB, AB)


def _three_stages(n, stage_a, stage_b, stage_c, unroll):
    stage_a(0)
    stage_a(1)
    stage_b(0)

    def body(j, carry):
        stage_c(j - 1)
        stage_b(j)
        stage_a(j + 1)
        return carry

    lax.fori_loop(1, n - 1, body, 0, unroll=unroll)
    stage_c(n - 2)
    stage_b(n - 1)
    stage_c(n - 1)


_NT = (((1,), (1,)), ((), ()))
_TN = (((0,), (0,)), ((), ()))
SCH = 64


def _attn_fwd(qn, kn, z8, bias, S):
    T = qn.shape[0]
    nb = T // S
    nblk = S // AB

    def body(q_ref, k_ref, v_ref, bias_ref, o_ref, ob_ref, lse_ref, qs, ks, vs, s2, p2, ogp, lgp, *group_scratch):
        og, lg = group_scratch[:3], group_scratch[3:]
        masks = _head_masks()
        ks[0:AB, :] = jnp.zeros((AB, 128), bf16)
        vs[0:AB, :] = jnp.zeros((AB, 128), bf16)

        for g, (_, d) in enumerate(GROUPS):
            nsub = S // (d * AB)
            chunks = _perm_chunks(S, d)
            for src, dst, ch in chunks:
                qs[dst:dst + ch, :] = q_ref[src, :].astype(bf16)
                ks[AB + dst:AB + dst + ch, :] = k_ref[src, :].astype(bf16)
                vs[AB + dst:AB + dst + ch, :] = v_ref[src, :].astype(bf16)
            od, ld = (og[g], lg[g]) if d == 1 else (ogp, lgp)

            def scores(j):
                r0 = _block_row(j)
                q2 = _stack_heads(qs[pl.ds(r0, AB), :], masks)
                s2[j] = lax.dot_general(q2, ks[pl.ds(r0, 2 * AB), :], _NT, preferred_element_type=f32)

            def softmax(j, g=g, nsub=nsub, ld=ld):
                r0 = _block_row(j)
                kind = int(j % nsub == 0) if isinstance(j, int) else (j % nsub == 0).astype(jnp.int32)
                for cc in range(AB // SCH):
                    lses = []
                    for hh in range(2):
                        rows = pl.ds(hh * AB + cc * SCH, SCH)
                        sb = s2[j, rows, :] - bias_ref[g, hh, kind, cc * SCH:(cc + 1) * SCH, :]
                        m = jnp.max(sb, axis=-1, keepdims=True)
                        p = jnp.exp(sb - m)
                        den = jnp.sum(p, axis=-1, keepdims=True)
                        p2[j, rows, :] = (p * (1.0 / den)).astype(bf16)
                        lses.append(m + jnp.log(den))
                    ld[pl.ds(r0 + cc * SCH, SCH), :] = jnp.where(masks[0], lses[0], lses[1])

            def values(j, od=od):
                r0 = _block_row(j)
                pv2 = jnp.dot(p2[j], vs[pl.ds(r0, 2 * AB), :], preferred_element_type=f32)
                od[pl.ds(r0, AB), :] = jnp.where(masks[0], pv2[:AB], pv2[AB:])

            _three_stages(nblk, scores, softmax, values, nblk - 2)

            if d > 1:
                for src, dst, ch in chunks:
                    og[g][src, :] = ogp[dst:dst + ch, :]
                    lg[g][src, :] = lgp[dst:dst + ch, :]

        def combine(i, carry):
            rr = pl.ds(pl.multiple_of(i * 256, 256), 256)
            l0, l1, l2 = lg[0][rr, :], lg[1][rr, :], lg[2][rr, :]
            mx = jnp.maximum(jnp.maximum(l0, l1), l2)
            e0, e1, e2 = jnp.exp(l0 - mx), jnp.exp(l1 - mx), jnp.exp(l2 - mx)
            den = e0 + e1 + e2
            o = (e0 * og[0][rr, :] + e1 * og[1][rr, :] + e2 * og[2][rr, :]) / den
            o_ref[rr, :] = o
            ob_ref[rr, :] = o.astype(bf16)
            lse_ref[rr, :] = mx + jnp.log(den)
            return carry

        lax.fori_loop(0, S // 256, combine, 0)

    blk = pl.BlockSpec((S, 128), lambda b, hp: (b, hp))
    return pl.pallas_call(
        body, name="attn_fwd", grid=(nb, N_HEADS // 2),
        in_specs=[blk, blk, pl.BlockSpec((None, S, 128), lambda b, hp: (Z_V, b, hp)),
                  pl.BlockSpec((3, 2, 2, AB, 2 * AB), lambda b, hp: (0, hp, 0, 0, 0))],
        out_specs=[blk, blk, blk],
        out_shape=[jax.ShapeDtypeStruct((T, D), f32), jax.ShapeDtypeStruct((T, D), bf16),
                   jax.ShapeDtypeStruct((T, D), f32)],
        scratch_shapes=[pltpu.VMEM((S, 128), bf16), pltpu.VMEM((S + AB, 128), bf16), pltpu.VMEM((S + AB, 128), bf16),
                        pltpu.VMEM((nblk, 2 * AB, 2 * AB), f32), pltpu.VMEM((nblk, 2 * AB, 2 * AB), bf16),
                        pltpu.VMEM((S, 128), f32), pltpu.VMEM((S, 128), f32)] + [pltpu.VMEM((S, 128), f32)] * 6,
        compiler_params=_cparams(("parallel", "parallel")))(qn, kn, z8, bias)


def _attn_bwd(qn, kn, z8, do, o, lse, bias, bd, S, after):
    T = qn.shape[0]
    nb = T // S

    nblk = S // AB

    def body(q_ref, k_ref, v_ref, do_ref, o_ref, lse_ref, bias_ref, bd_ref, after_ref, dq_ref, dk_ref, dv_ref,
             delta, qs, ks, vs, dos, lsp, dlp, s2, dp2, p2, ds2, dqp, dkp, dvp):
        del after_ref
        masks = _head_masks()
        bdv = bd_ref[...]
        dq_ref[...] = jnp.zeros_like(dq_ref)
        dk_ref[...] = jnp.zeros_like(dk_ref)
        dv_ref[...] = jnp.zeros_like(dv_ref)
        ks[0:AB, :] = jnp.zeros((AB, 128), bf16)
        vs[0:AB, :] = jnp.zeros((AB, 128), bf16)

        def prep(i, carry):
            rr = pl.ds(pl.multiple_of(i * 256, 256), 256)
            delta[rr, :] = _head_sum(do_ref[rr, :] * o_ref[rr, :], bdv)
            return carry

        lax.fori_loop(0, S // 256, prep, 0, unroll=True)

        for g, (_, d) in enumerate(GROUPS):
            nsub = S // (d * AB)
            chunks = _perm_chunks(S, d)
            for src, dst, ch in chunks:
                qs[dst:dst + ch, :] = q_ref[src, :].astype(bf16)
                ks[AB + dst:AB + dst + ch, :] = k_ref[src, :].astype(bf16)
                vs[AB + dst:AB + dst + ch, :] = v_ref[src, :].astype(bf16)
                dos[dst:dst + ch, :] = do_ref[src, :].astype(bf16)
                lsp[dst:dst + ch, :] = lse_ref[src, :]
                dlp[dst:dst + ch, :] = delta[src, :]
            dkp[...] = jnp.zeros_like(dkp)
            dvp[...] = jnp.zeros_like(dvp)

            def scores(j):
                r0 = _block_row(j)
                q2 = _stack_heads(qs[pl.ds(r0, AB), :], masks)
                do2 = _stack_heads(dos[pl.ds(r0, AB), :], masks)
                s2[j] = lax.dot_general(q2, ks[pl.ds(r0, 2 * AB), :], _NT, preferred_element_type=f32)
                dp2[j] = lax.dot_general(do2, vs[pl.ds(r0, 2 * AB), :], _NT, preferred_element_type=f32)

            def probs(j, g=g, nsub=nsub):
                r0 = _block_row(j)
                kind = int(j % nsub == 0) if isinstance(j, int) else (j % nsub == 0).astype(jnp.int32)
                for cc in range(AB // SCH):
                    lse_c = lsp[pl.ds(r0 + cc * SCH, SCH), :]
                    del_c = dlp[pl.ds(r0 + cc * SCH, SCH), :]
                    for hh in range(2):
                        c0 = hh * HEAD_DIM
                        rows = pl.ds(hh * AB + cc * SCH, SCH)
                        sb = s2[j, rows, :] - bias_ref[g, hh, kind, cc * SCH:(cc + 1) * SCH, :]
                        p = jnp.exp(sb - lse_c[:, c0:c0 + 1])
                        p2[j, rows, :] = p.astype(bf16)
                        ds2[j, rows, :] = (p * (dp2[j, rows, :] - del_c[:, c0:c0 + 1])).astype(bf16)

            def grads(j):
                r0 = _block_row(j)
                q2 = _stack_heads(qs[pl.ds(r0, AB), :], masks)
                do2 = _stack_heads(dos[pl.ds(r0, AB), :], masks)
                dsb = ds2[j]
                t = jnp.dot(dsb, ks[pl.ds(r0, 2 * AB), :], preferred_element_type=f32)
                dqp[pl.ds(r0, AB), :] = jnp.where(masks[0], t[:AB], t[AB:])
                dkp[pl.ds(r0, 2 * AB), :] += lax.dot_general(dsb, q2, _TN, preferred_element_type=f32)
                dvp[pl.ds(r0, 2 * AB), :] += lax.dot_general(p2[j], do2, _TN, preferred_element_type=f32)

            _three_stages(nblk, scores, probs, grads, nblk - 2)

            for src, dst, ch in chunks:
                dq_ref[src, :] += dqp[dst:dst + ch, :]
                dk_ref[src, :] += dkp[AB + dst:AB + dst + ch, :]
                dv_ref[src, :] += dvp[AB + dst:AB + dst + ch, :]

    blk = pl.BlockSpec((S, 128), lambda b, hp: (b, hp))
    row = lambda dt, pad=0: pltpu.VMEM((S + pad, 128), dt)
    blocks = lambda dt: pltpu.VMEM((nblk, 2 * AB, 2 * AB), dt)
    return pl.pallas_call(
        body, name="attn_bwd", grid=(nb, N_HEADS // 2),
        in_specs=[blk, blk, pl.BlockSpec((None, S, 128), lambda b, hp: (Z_V, b, hp)), blk, blk, blk,
                  pl.BlockSpec((3, 2, 2, AB, 2 * AB), lambda b, hp: (0, hp, 0, 0, 0)),
                  pl.BlockSpec((128, 128), lambda b, hp: (0, 0)), pl.BlockSpec(memory_space=pl.ANY)],
        out_specs=[blk, blk, blk],
        out_shape=[jax.ShapeDtypeStruct((T, D), f32)] * 3,
        scratch_shapes=[row(f32), row(bf16), row(bf16, AB), row(bf16, AB), row(bf16), row(f32), row(f32),
                        blocks(f32), blocks(f32), blocks(bf16), blocks(bf16), row(f32), row(f32, AB), row(f32, AB)],
        compiler_params=_cparams(("parallel", "parallel")))(qn, kn, z8, do, o, lse, bias, bd, after)


def _any_spec():
    return pl.BlockSpec(memory_space=pl.ANY)


AG_CHUNKS = 4


def _allgather_rows(shards, n_full):
    n = len(shards)
    parts = [(a, q) for a in range(n_full) for q in range(AG_CHUNKS)]

    def body(*refs):
        ins, outs = refs[:n], refs[n:2 * n]
        send_sems, recv_sems, local_sems = refs[2 * n:]
        x, y, c, me = _my_pos()
        sibling = (x, y, 1 - c)
        chips = [(1 - x, y), (x, 1 - y), (1 - x, 1 - y)]

        def idx(px, py, pc):
            return 4 * px + 2 * py + pc

        def copy(v, k, blk, to, own=False):
            a, q = parts[v]
            rows = pl.ds(q * (shards[a].shape[0] // AG_CHUNKS), shards[a].shape[0] // AG_CHUNKS)
            return pltpu.make_async_remote_copy(
                src_ref=ins[a].at[rows] if own else outs[a].at[blk, rows], dst_ref=outs[a].at[blk, rows],
                send_sem=send_sems.at[v, k], recv_sem=recv_sems.at[v, k], device_id=to, device_id_type=MESH)

        mine = [pltpu.make_async_copy(ins[a], outs[a].at[me], local_sems.at[a]) for a in range(n)]
        for cp in mine:
            cp.start()
        first = []
        for v in range(len(parts)):
            first.append(copy(v, 0, me, sibling, own=True))
            first += [copy(v, 1 + j, me, (*chip, c), own=True) for j, chip in enumerate(chips[:2])]
        for cp in first:
            cp.start()
        relay_blk = jnp.where(c == 1, idx(1 - x, y, c), idx(x, 1 - y, c))
        relay_to = (jnp.where(c == 1, x, 1 - x), jnp.where(c == 1, 1 - y, y), c)
        passed = []
        for v in range(len(parts)):
            for j, chip in enumerate(chips[:2]):
                copy(v, 1 + j, idx(*chip, c), (x, y, c)).wait_recv()
            cp = copy(v, 3, relay_blk, relay_to)
            cp.start()
            passed.append(cp)
            for j, chip in enumerate(chips):
                if j == 2:
                    copy(v, 3, idx(*chip, c), (x, y, c)).wait_recv()
                cp = copy(v, 4 + j, idx(*chip, c), sibling)
                cp.start()
                passed.append(cp)
        for v in range(len(parts)):
            copy(v, 0, idx(x, y, 1 - c), (x, y, c)).wait_recv()
            for j, chip in enumerate(chips):
                copy(v, 4 + j, idx(*chip, 1 - c), (x, y, c)).wait_recv()
        for cp in first + passed:
            cp.wait_send()
        for cp in mine:
            cp.wait()

    return pl.pallas_call(
        body, name="allgather_weights",
        in_specs=[_any_spec()] * n, out_specs=[_any_spec()] * n,
        out_shape=[jax.ShapeDtypeStruct((N_DEV,) + s.shape, s.dtype) for s in shards],
        scratch_shapes=[pltpu.SemaphoreType.DMA((len(parts), 7)), pltpu.SemaphoreType.DMA((len(parts), 7)),
                        pltpu.SemaphoreType.DMA((n,))],
    )(*shards)


def _peer(x, y, c, k):
    tx = 1 - x if (k >> 2) & 1 else x
    ty = 1 - y if (k >> 1) & 1 else y
    tc = 1 - c if k & 1 else c
    return (tx, ty, tc), 4 * tx + 2 * ty + tc


_PEER_ORDER = (2, 4, 6, 3, 5, 7, 1)


_HBM = pl.BlockSpec(memory_space=pltpu.HBM)
_SEM = pl.BlockSpec(memory_space=pltpu.SEMAPHORE)
_EFFECT = pltpu.SideEffectType.DATAFLOW_SIDE_EFFECTING


def _exchange_copies(srcs, lands, send_sems, recv_sems, gather, half):
    x, y, c, me = _my_pos()
    pick = lambda px, py: None if half is None else ((px == py) if half == 0 else (px != py))
    copies = []
    for k in _PEER_ORDER:
        tgt, tidx = _peer(x, y, c, k)
        for a in range(len(srcs)):
            copies.append((pltpu.make_async_remote_copy(
                src_ref=srcs[a] if gather else srcs[a].at[tidx], dst_ref=lands[a].at[me],
                send_sem=send_sems.at[7 * a + k - 1], recv_sem=recv_sems.at[7 * a + k - 1],
                device_id=tgt, device_id_type=MESH), pick(tgt[0], tgt[1])))
    return copies, pick(x, y)


def _when(cond, fn):
    if cond is None:
        fn()
    else:
        pl.when(cond)(fn)


def _exchange_start(name, srcs, lands=None, after=None, gather=None, half=None):
    n = len(srcs)
    gather = (lands is not None) if gather is None else gather
    if lands is None:
        lands = [lax.empty(g.shape, g.dtype) for g in srcs]
    extra = [] if after is None else [after]

    def body(*refs):
        src_refs, land_refs = refs[:n], refs[n:2 * n]
        send_sems, recv_sems = refs[2 * n + len(extra)], refs[2 * n + len(extra) + 1]
        token = refs[-1]
        for cp, sends in _exchange_copies(src_refs, land_refs, send_sems, recv_sems, gather, half)[0]:
            _when(sends, cp.start)
        token[...] = jnp.zeros_like(token)

    hbm = lambda a: pltpu.with_memory_space_constraint(a, pltpu.HBM)
    outs = pl.pallas_call(
        body, name=name,
        out_shape=(pltpu.SemaphoreType.DMA((7 * n,)), pltpu.SemaphoreType.DMA((7 * n,)),
                   *[pltpu.HBM(g.shape, g.dtype) for g in list(srcs) + list(lands)],
                   jax.ShapeDtypeStruct((8, 128), f32)),
        in_specs=[_HBM] * (2 * n) + [pl.BlockSpec(memory_space=pl.ANY)] * len(extra),
        out_specs=(_SEM, _SEM, *([_HBM] * (2 * n)), pl.BlockSpec(memory_space=pltpu.VMEM)),
        input_output_aliases={i: 2 + i for i in range(2 * n)},
        compiler_params=pltpu.CompilerParams(has_side_effects=_EFFECT),
    )(*[hbm(g) for g in srcs], *[hbm(g) for g in lands], *extra)
    return outs[0], outs[1], list(outs[2:2 + n]), list(outs[2 + n:2 + 2 * n]), outs[-1], gather, half


def _exchange_wait(name, started, after):
    send_sems, recv_sems, srcs, lands, _, gather, half = started
    n = len(srcs)
    after = list(after) if isinstance(after, (list, tuple)) else [after]

    def body(*refs):
        src_refs, land_refs = refs[:n], refs[n:2 * n]
        s_sems, r_sems = refs[2 * n], refs[2 * n + 1]
        copies, receives = _exchange_copies(src_refs, land_refs, s_sems, r_sems, gather, half)
        for cp, sends in copies:
            _when(sends, cp.wait_send)
            _when(receives, cp.wait_recv)

    outs = pl.pallas_call(
        body, name=name,
        out_shape=tuple(pltpu.HBM(a.shape, a.dtype) for a in list(srcs) + list(lands)),
        in_specs=[_HBM] * (2 * n) + [_SEM, _SEM] + [pl.BlockSpec(memory_space=pl.ANY)] * len(after),
        out_specs=tuple([_HBM] * (2 * n)),
        input_output_aliases={i: i for i in range(2 * n)},
        compiler_params=pltpu.CompilerParams(has_side_effects=_EFFECT),
    )(*srcs, *lands, send_sems, recv_sems, *after)
    return list(outs[:n]), list(outs[n:])


SMALL_ROWS = 128


def _small_start(name, sg, after=None):
    return _exchange_start(name, [sg], [lax.empty((N_DEV,) + sg.shape, f32)], after=after)


def _small_sum(name, me, started, after):
    (own,), (slots,) = _exchange_wait(name + "_wait", started, after)

    def body(me_ref, s_ref, own_ref, out_ref):
        acc = None
        for p in range(N_DEV):
            term = lax.cond(me_ref[0] == p, lambda: own_ref[...], lambda p=p: s_ref[p])
            acc = term if acc is None else acc + term
        out_ref[...] = acc

    return pl.pallas_call(
        body, name=name + "_sum",
        in_specs=[pl.BlockSpec(memory_space=pltpu.SMEM), pl.BlockSpec(memory_space=pltpu.VMEM),
                  pl.BlockSpec(memory_space=pltpu.VMEM)],
        out_specs=pl.BlockSpec(memory_space=pltpu.VMEM),
        out_shape=jax.ShapeDtypeStruct(own.shape, f32))(me, slots, own)


def _adam_math(g, w, m, v):
    m = ADAM_B1 * m + (1.0 - ADAM_B1) * g
    v = ADAM_B2 * v + (1.0 - ADAM_B2) * (g * g)
    m_hat = m / (1.0 - ADAM_B1 ** ADAM_STEP)
    v_hat = v / (1.0 - ADAM_B2 ** ADAM_STEP)
    delta = -ADAM_LR * (m_hat / (jnp.sqrt(v_hat) + ADAM_EPS) + ADAM_WD * w)
    return delta, m, v


def _adam_slots(name, me, slots, own, w, m, v, tr, transposed=False):
    rows = slots.shape[1]

    def body(me_ref, s_ref, own_ref, w_ref, m_ref, v_ref, g_ref, d_ref, nm_ref, nv_ref):
        mine = own_ref[...]
        g = None
        for p in range(N_DEV):
            term = lax.cond(me_ref[0] == p, lambda: mine, lambda p=p: s_ref[p]).astype(f32)
            g = term if g is None else g + term
        if transposed:
            g = g.T
        delta, nm, nv = _adam_math(g, w_ref[...], m_ref[...], v_ref[...])
        g_ref[...] = g
        d_ref[...] = delta
        nm_ref[...] = nm
        nv_ref[...] = nv

    mode = dict(pipeline_mode=pl.Buffered(1)) if rows == tr else {}
    if transposed:
        rs = pl.BlockSpec((D, tr), lambda i, me_ref: (0, i))
        rs_in = pl.BlockSpec((D, tr), lambda i, me_ref: (0, i), **mode)
    else:
        rs = pl.BlockSpec((tr, D), lambda i, me_ref: (i, 0))
        rs_in = pl.BlockSpec((tr, D), lambda i, me_ref: (i, 0), **mode)
    return pl.pallas_call(
        body, name=name,
        grid_spec=pltpu.PrefetchScalarGridSpec(
            num_scalar_prefetch=1, grid=(rows // tr,),
            in_specs=[pl.BlockSpec((N_DEV, tr, D), lambda i, me_ref: (0, i, 0), **mode),
                      pl.BlockSpec((None, tr, D), lambda i, me_ref: (me_ref[0], i, 0), **mode), rs_in, rs_in, rs_in],
            out_specs=[rs] * 4),
        out_shape=[jax.ShapeDtypeStruct(w.shape, f32)] * 4,
        compiler_params=_cparams(("parallel",)))(me, slots, own, w, m, v)


def _adam_small(g, w, m, v):
    def body(g_ref, w_ref, m_ref, v_ref, d_ref, nm_ref, nv_ref):
        delta, nm, nv = _adam_math(g_ref[...], w_ref[...], m_ref[...], v_ref[...])
        d_ref[...] = delta
        nm_ref[...] = nm
        nv_ref[...] = nv

    return pl.pallas_call(body, name="adam_small", out_shape=[jax.ShapeDtypeStruct(g.shape, f32)] * 3)(g, w, m, v)


FFN_PAD = 6 * D


_SMALL_PARTS = (("norm1_g", 1), ("gate_b", 2), ("conv_w", CONV_WIDTH), ("conv_b", 1), ("conv_norm_g", 1),
                ("q_norm_g", 1), ("k_norm_g", 1), ("norm2_g", 1), ("ffn_conv_w", 18), ("ffn_conv_b", 6), ("last", 1))


def _small_offsets():
    out, row = {}, 0
    for name, rows in _SMALL_PARTS:
        out[name] = row
        row += -(-rows // 8) * 8
    assert row == SMALL_ROWS
    return out


def _pack_small(norm1_g, gate_b, conv_w, conv_b, conv_norm_g, q_norm_g, k_norm_g, norm2_g, ffn_conv_w, ffn_conv_b,
                last_row=None):
    pad_h = lambda a: jnp.pad(a, ((0, 0), (0, D - HEAD_DIM)))
    pad_f = lambda a: jnp.pad(a, ((0, 0), (0, FFN_PAD - 2 * D_FF))).reshape(-1, D)
    parts = [norm1_g, gate_b.reshape(2, D), conv_w, conv_b, conv_norm_g, pad_h(q_norm_g), pad_h(k_norm_g), norm2_g,
             pad_f(ffn_conv_w), pad_f(ffn_conv_b), jnp.zeros((1, D), f32) if last_row is None else last_row]
    return jnp.concatenate([jnp.pad(p, ((0, -p.shape[0] % 8), (0, 0))) for p in parts], axis=0)


def _unpack_small(p):
    o = _small_offsets()
    rows = lambda name, n: p[o[name]:o[name] + n]
    ffn = lambda a: a.reshape(-1, FFN_PAD)[:, :2 * D_FF]
    return dict(
        norm1_g=rows("norm1_g", 1), gate_b=rows("gate_b", 2).reshape(1, 2 * D), conv_w=rows("conv_w", CONV_WIDTH),
        conv_b=rows("conv_b", 1), conv_norm_g=rows("conv_norm_g", 1), q_norm_g=rows("q_norm_g", 1)[:, :HEAD_DIM],
        k_norm_g=rows("k_norm_g", 1)[:, :HEAD_DIM], norm2_g=rows("norm2_g", 1),
        ffn_conv_w=ffn(rows("ffn_conv_w", 18)), ffn_conv_b=ffn(rows("ffn_conv_b", 6)))


_ADAM_TILE = {896: 128, 704: 704, 128: 128, 352: 176}


def kernel(x, norm1_g, w_in, gate_b, conv_w, conv_b, conv_norm_g, w_conv_out, q_norm_g, k_norm_g, w_attn_out, w_out, norm2_g, w_up, ffn_conv_w, ffn_conv_b, w_down, loss_target, m_norm1_g, m_w_in, m_gate_b, m_conv_w, m_conv_b, m_conv_norm_g, m_w_conv_out, m_q_norm_g, m_k_norm_g, m_w_attn_out, m_w_out, m_norm2_g, m_w_up, m_ffn_conv_w, m_ffn_conv_b, m_w_down, v_norm1_g, v_w_in, v_gate_b, v_conv_w, v_conv_b, v_conv_norm_g, v_w_conv_out, v_q_norm_g, v_k_norm_g, v_w_attn_out, v_w_out, v_norm2_g, v_w_up, v_ffn_conv_w, v_ffn_conv_b, v_w_down):
    BL, S, _ = x.shape
    T = BL * S
    me = 4 * lax.axis_index("x") + 2 * lax.axis_index("y") + lax.axis_index("c")
    xt = x.reshape(T, D)
    target = loss_target.reshape(T, D)

    big = dict(w_in=(w_in[0], m_w_in[0], v_w_in[0]), w_up=(w_up[0], m_w_up[0], v_w_up[0]),
               w_conv_out=(w_conv_out[0], m_w_conv_out[0], v_w_conv_out[0]),
               w_attn_out=(w_attn_out[0], m_w_attn_out[0], v_w_attn_out[0]),
               w_out=(w_out[0], m_w_out[0], v_w_out[0]), w_down=(w_down[0], m_w_down[0], v_w_down[0]))
    order = ["w_in", "w_conv_out", "w_attn_out", "w_out", "w_up", "w_down"]
    shards = [(big[n][0].T if n in ("w_in", "w_up") else big[n][0]).astype(bf16) for n in order]
    gathered = _allgather_rows(shards, 1)
    W = {"w_in": gathered[0].reshape(-1, D)}

    def place_cols(shard, full_cols):
        z = jnp.zeros((shard.shape[0], full_cols), f32)
        return lax.dynamic_update_slice(z, shard, (0, me * shard.shape[1]))

    zr = lambda a: jnp.zeros_like(a)
    conv_local = _pack_small(
        zr(norm1_g), zr(gate_b), place_cols(conv_w[0], D), zr(conv_b), zr(conv_norm_g), zr(q_norm_g), zr(k_norm_g),
        zr(norm2_g), place_cols(ffn_conv_w[0], 2 * D_FF), zr(ffn_conv_b))
    ga_conv = _small_start("gather_conv_start", conv_local, after=gathered[0])
    ga_proj = _exchange_start("gather_start_proj", shards[1:4], gathered[1:4], after=ga_conv[4])
    ga_ffn = _exchange_start("gather_start_ffn", shards[4:6], gathered[4:6], after=ga_proj[4])

    bd = (jnp.arange(128)[:, None] // HEAD_DIM == jnp.arange(128)[None, :] // HEAD_DIM).astype(bf16)
    bias = _attn_bias()
    qg = jnp.tile(q_norm_g, (1, N_HEADS))
    kg = jnp.tile(k_norm_g, (1, N_HEADS))

    z8, h, qn, kn = _in_proj_fwd(xt, norm1_g, W["w_in"], qg, kg, bd, ga_ffn[4])
    conv_all = _unpack_small(_small_sum("gather_conv", me.reshape(1), ga_conv, z8))
    conv_w_full, ffn_w_full = conv_all["conv_w"], conv_all["ffn_conv_w"]
    c = _conv_fwd(z8, conv_w_full, conv_b, S)
    o, ob, lse = _attn_fwd(qn, kn, z8, bias, S)
    for n, g in zip(order[1:4], _exchange_wait("gather_wait_proj", ga_proj, ob)[1]):
        W[n] = g.reshape(-1, D)
    s, ya, yb, mixed = _branches_fwd(c, ob, z8, conv_norm_g, gate_b, W["w_conv_out"], W["w_attn_out"])
    x1, h2 = _out_norm2_fwd(mixed, W["w_out"], xt, norm2_g)
    for n, g in zip(order[4:6], _exchange_wait("gather_wait_ffn", ga_ffn, x1)[1]):
        W[n] = g.reshape(-1, D)
    TNU = D_FF // 2
    u3 = _matmul_call(
        "mm_u", h2, W["w_up"],
        pl.BlockSpec((1024, D), lambda i, j, k: (i, 0)),
        pl.BlockSpec((TNU, D), lambda i, j, k: (j, 0)),
        pl.BlockSpec((None, 1024, TNU), lambda i, j, k: (j // 2, i, j % 2)),
        jax.ShapeDtypeStruct((2, T, D_FF), f32), (T // 1024, 4, 1), "nt", 1, 1024, TNU)
    f = _ffn_fwd(u3, ffn_w_full, ffn_conv_b, S)
    dy, dyb, lacc = _down_loss_fwd(f, W["w_down"], x1, target)
    loss_local = 0.5 / D * jnp.sum(lacc)

    df = _matmul("mm_df", dyb, W["w_down"], "nt", f32, tn=TNU)
    g_w_down = _matmul("mm_dwdn", f, dyb, "tn", bf16, tm=TNU)
    du3, dffn = _ffn_bwd(u3, df, ffn_w_full, ffn_conv_b, S)
    g_w_up = _matmul_call(
        "mm_dwup", du3, h2,
        pl.BlockSpec((None, T, TNU), lambda i, j, k: (i // 2, 0, i % 2)),
        pl.BlockSpec((T, D), lambda i, j, k: (0, 0)),
        pl.BlockSpec((TNU, D), lambda i, j, k: (i, 0)),
        jax.ShapeDtypeStruct((2 * D_FF, D), bf16), (4, 1, 1), "tn", 1, TNU, D)
    blocks8 = lambda a: a.reshape(N_DEV, -1, D)
    ex_ffn = _exchange_start("scatter_start_ffn", [blocks8(g_w_up), blocks8(g_w_down)])
    dx1, dx1b, dg_norm2 = _up_norm2_bwd(du3, W["w_up"], x1, dy, norm2_g, ex_ffn[4])
    g_w_out = _matmul("mm_dwo", mixed, dx1b, "tn", bf16, tm=512)
    dz8 = lax.empty((8, T, D), bf16)
    dya, dyb2, dz8, dg_gate = _out_gate_bwd(dx1b, W["w_out"], z8, gate_b, ya, yb, dz8)
    g_w_conv_out = _matmul("mm_dwco", s, dya, "tn", bf16, tm=512)
    g_w_attn_out = _matmul("mm_dwao", ob, dyb2, "tn", bf16, tm=512)
    ex_proj = _exchange_start("scatter_start_proj", [blocks8(g_w_conv_out), blocks8(g_w_attn_out), blocks8(g_w_out)])
    do = _matmul("mm_do", dyb2, W["w_attn_out"], "nt", f32, after=ex_proj[4])
    dc, dg_convnorm = _convnorm_bwd(dya, W["w_conv_out"], c, conv_norm_g)
    dz8a, dconv = _conv_bwd(dc, z8, conv_w_full, dz8, S)
    dwin_specs = lambda zsec, wsec: (
        pl.BlockSpec((None, T, D), lambda i, j, k: (zsec(i), 0, 0)), pl.BlockSpec((T, D), lambda i, j, k: (0, 0)),
        pl.BlockSpec((1024, D), lambda i, j, k: (wsec(i), 0)), jax.ShapeDtypeStruct((7 * D, D), bf16))
    g_w_in = _matmul_call("mm_dwin_a", dz8a, h, *dwin_specs(lambda i: i, lambda i: jnp.where(i < 2, i, i + 3)),
                          (4, 1, 1), "tn", 1, D, D)
    ex_in_a = _exchange_start("scatter_start_in_a", [blocks8(g_w_in)], half=0)
    dqn, dkn, dv = _attn_bwd(qn, kn, z8, do, o, lse, bias, bd, S, ex_in_a[4])
    dz8b, dg_q, dg_k = _qk_bwd(z8, dqn, dkn, dv, qg, kg, bd, dz8a)
    g_w_in = _matmul_call("mm_dwin_b", dz8b, h, *dwin_specs(lambda i: i + 4, lambda i: i + 2),
                          (3, 1, 1), "tn", 1, D, D, fill=ex_in_a[2][0].reshape(7 * D, D))
    ex_in_b = _exchange_start("scatter_start_in_b", [blocks8(g_w_in)], ex_in_a[3], gather=False, half=1)
    grad_x, dg_norm1 = _in_norm1_bwd(dz8b, W["w_in"], xt, dx1, norm1_g, ex_in_b[4])

    sum8 = lambda a: a.reshape(-1, 8, a.shape[-1]).sum(axis=1)
    dconv_s = sum8(dconv.sum(axis=0))
    dffn_s = dffn.sum(axis=0).reshape(2, 4, 8, D_FF).sum(axis=2)
    dffn_w = jnp.concatenate([dffn_s[0, :3], dffn_s[1, :3]], axis=1)
    dffn_b = jnp.concatenate([dffn_s[0, 3:4], dffn_s[1, 3:4]], axis=1)
    fold = lambda a: sum8(a).reshape(N_HEADS, HEAD_DIM).sum(axis=0)[None]
    small_g_local = _pack_small(
        sum8(dg_norm1), sum8(dg_gate), dconv_s[:CONV_WIDTH], dconv_s[CONV_WIDTH:], sum8(dg_convnorm),
        fold(dg_q), fold(dg_k), sum8(dg_norm2), dffn_w, dffn_b,
        last_row=jnp.pad(loss_local.reshape(1, 1), ((0, 0), (0, D - 1))))
    sg_start = _small_start("small_grads_start", small_g_local)

    own, slots = {}, {}
    for tag, ex, names_ in (("ffn", ex_ffn, ("w_up", "w_down")),
                            ("proj", ex_proj, ("w_conv_out", "w_attn_out", "w_out"))):
        sent, landed = _exchange_wait("scatter_wait_" + tag, ex, sg_start[4])
        for n, src, land in zip(names_, sent, landed):
            own[n], slots[n] = src, land
    sent, landed = _exchange_wait("scatter_wait_in_a", ex_in_a[:2] + (ex_in_b[2], ex_in_b[3]) + ex_in_a[4:],
                                  sg_start[4])
    sent, landed = _exchange_wait("scatter_wait_in_b", ex_in_b[:2] + (sent, landed) + ex_in_b[4:], sg_start[4])
    own["w_in"], slots["w_in"] = sent[0], landed[0]

    res, adam_done = {}, []
    for n in order:
        w, m, v = big[n]
        outs = _adam_slots("adam_" + n, me.reshape(1), slots[n], own[n], w, m, v, _ADAM_TILE[slots[n].shape[1]],
                           transposed=n in ("w_in", "w_up"))
        adam_done.append(outs[0])
        res[n] = [a[None] for a in outs]
    small_g = _small_sum("small_grads", me.reshape(1), sg_start, adam_done)
    loss = small_g[_small_offsets()["last"], 0]

    col = lambda a, width: lax.dynamic_slice(a, (0, me * width), (a.shape[0], width))
    small_w_true = _pack_small(norm1_g, gate_b, conv_w_full, conv_b, conv_norm_g, q_norm_g, k_norm_g, norm2_g,
                               ffn_w_full, ffn_conv_b)
    place_m = lambda a, full: place_cols(a[0], full)
    small_m = _pack_small(m_norm1_g, m_gate_b, place_m(m_conv_w, D), m_conv_b, m_conv_norm_g, m_q_norm_g, m_k_norm_g,
                          m_norm2_g, place_m(m_ffn_conv_w, 2 * D_FF), m_ffn_conv_b)
    small_v = _pack_small(v_norm1_g, v_gate_b, place_m(v_conv_w, D), v_conv_b, v_conv_norm_g, v_q_norm_g, v_k_norm_g,
                          v_norm2_g, place_m(v_ffn_conv_w, 2 * D_FF), v_ffn_conv_b)
    sd, sm, sv = _adam_small(small_g, small_w_true, small_m, small_v)
    for i, packed in enumerate((small_g, sd, sm, sv)):
        u = _unpack_small(packed)
        u["conv_w"] = col(u["conv_w"], D // N_DEV)
        u["ffn_conv_w"] = col(u["ffn_conv_w"], 2 * D_FF // N_DEV)
        for n, a in u.items():
            res.setdefault(n, [None] * 4)[i] = a[None] if n in ("conv_w", "ffn_conv_w") else a

    names = ["norm1_g", "w_in", "gate_b", "conv_w", "conv_b", "conv_norm_g", "w_conv_out", "q_norm_g", "k_norm_g",
             "w_attn_out", "w_out", "norm2_g", "w_up", "ffn_conv_w", "ffn_conv_b", "w_down"]
    out = [loss, grad_x.reshape(BL, S, D)]
    for i in range(4):
        out += [res[n][i] for n in names]
    return tuple(out)
```

```python
import functools

import jax
import jax.numpy as jnp
import numpy as np
from jax import lax
from jax.experimental import pallas as pl
from jax.experimental.pallas import tpu as pltpu

f32 = jnp.float32
bf16 = jnp.bfloat16

D = 1024
N_HEADS = 16
HEAD_DIM = 64
CONV_WIDTH = 31
D_FF = 2816
GROUPS = ((128, 1), (512, 4), (2048, 16))
ATTN_BLOCK = 128
EPS = 1e-6
N_DEV = 8
MESH = pl.DeviceIdType.MESH

ADAM_LR = 0.001
ADAM_B1 = 0.9
ADAM_B2 = 0.999
ADAM_EPS = 1e-08
ADAM_WD = 0.01
ADAM_STEP = 10

VMEM_LIMIT = 56 * 1024 * 1024
MASK_BIAS = 1e30

Z_AVAL, Z_AGATE, Z_GA, Z_GB, Z_Q, Z_K, Z_V = 0, 1, 2, 3, 4, 5, 6


_W_OF_Z = (0, 1, 5, 6, 2, 3, 4)


def _wsec_of_zsec(j):
    return jnp.where(j < 2, j, jnp.where(j < 4, j + 3, j - 2))


def _sig(x):
    return 1.0 / (1.0 + jnp.exp(-x))


def _colsum8(x):
    return x.reshape(-1, 8, x.shape[-1]).sum(axis=0)


def _cparams(sem):
    return pltpu.CompilerParams(dimension_semantics=sem, vmem_limit_bytes=VMEM_LIMIT)


def _my_pos():
    x, y, c = lax.axis_index("x"), lax.axis_index("y"), lax.axis_index("c")
    return x, y, c, 4 * x + 2 * y + c


_DIMS = {"nn": ((1,), (0,)), "nt": ((1,), (1,)), "tn": ((0,), (0,))}


def _matmul_call(name, a, b, a_spec, b_spec, o_spec, out_shape, grid, mode, nk, tm, tn, after=None, fill=None):
    dims = (_DIMS[mode], ((), ()))
    extra = ([] if after is None else [after]) + ([] if fill is None else [fill])

    def body(a_ref, b_ref, *rest):
        o_ref, scratch = rest[len(extra)], rest[len(extra) + 1:]
        part = lax.dot_general(a_ref[...], b_ref[...], dims, preferred_element_type=f32)
        if nk == 1:
            o_ref[...] = part.astype(o_ref.dtype)
        else:
            acc = scratch[0]
            k = pl.program_id(2)

            @pl.when(k == 0)
            def _():
                acc[...] = part

            @pl.when(k > 0)
            def _():
                acc[...] += part

            @pl.when(k == nk - 1)
            def _():
                o_ref[...] = acc[...].astype(o_ref.dtype)

    scratch = [] if nk == 1 else [pltpu.VMEM((tm, tn), f32)]
    return pl.pallas_call(
        body, name=name, grid=grid, in_specs=[a_spec, b_spec] + [pl.BlockSpec(memory_space=pl.ANY)] * len(extra),
        out_specs=o_spec, out_shape=out_shape, input_output_aliases={} if fill is None else {1 + len(extra): 0},
        scratch_shapes=scratch, compiler_params=_cparams(("parallel", "parallel", "arbitrary")),
    )(a, b, *extra)


def _matmul(name, a, b, mode, out_dtype, tm=1024, tn=1024, tk=None, after=None):
    if mode == "nn":
        (M, K), (_, N) = a.shape, b.shape
    elif mode == "nt":
        (M, K), (N, _) = a.shape, b.shape
    else:
        (K, M), (_, N) = a.shape, b.shape
    tm, tn = min(tm, M), min(tn, N)
    tk = K if tk is None else tk
    nk = K // tk
    assert M % tm == 0 and N % tn == 0 and K % tk == 0
    if mode == "tn":
        a_spec = pl.BlockSpec((tk, tm), lambda i, j, k: (k, i))
    else:
        a_spec = pl.BlockSpec((tm, tk), lambda i, j, k: (i, k))
    if mode == "nt":
        b_spec = pl.BlockSpec((tn, tk), lambda i, j, k: (j, k))
    else:
        b_spec = pl.BlockSpec((tk, tn), lambda i, j, k: (k, j))
    o_spec = pl.BlockSpec((tm, tn), lambda i, j, k: (i, j))
    return _matmul_call(name, a, b, a_spec, b_spec, o_spec, jax.ShapeDtypeStruct((M, N), out_dtype),
                        (M // tm, N // tn, nk), mode, nk, tm, tn, after=after)


FTM = 512


def _matmul_fused(name, a, b, pairs, epilogue, extras, consts, outs, nt=False, sums=False, passed=(), aliases=None):
    sa, M, kk = a.shape
    na = max(i for i, _ in pairs) + 1
    ne, nc, npass = len(extras), len(consts), len(passed)
    dims = (_DIMS["nt" if nt else "nn"], ((), ()))

    def body(a_ref, b_ref, *rest):
        acc = None
        for i, j in pairs:
            part = lax.dot_general(a_ref[i], b_ref[j], dims, preferred_element_type=f32)
            acc = part if acc is None else acc + part
        epilogue(acc, rest[:ne], rest[ne:ne + nc], rest[ne + nc + npass:])

    whole = lambda arr: pl.BlockSpec(arr.shape, lambda i, nd=arr.ndim: (0,) * nd, pipeline_mode=pl.Buffered(1))
    io_alias = {2 + ne + nc + k: v for k, v in (aliases or {}).items()}
    return pl.pallas_call(
        body, name=name, grid=(M // FTM,),
        in_specs=[pl.BlockSpec((na, FTM, kk), lambda i: (0, i, 0)), whole(b)] + [s for _, s in extras]
        + [whole(c) for c in consts] + [pl.BlockSpec(memory_space=pl.ANY)] * npass,
        out_specs=[s for _, s in outs], out_shape=[s for s, _ in outs], input_output_aliases=io_alias,
        compiler_params=_cparams(("arbitrary" if sums else "parallel",)),
    )(a, b, *[x for x, _ in extras], *consts, *passed)


def _frows(c=D):
    return pl.BlockSpec((FTM, c), lambda i: (i, 0))


def _fsec(s):
    return pl.BlockSpec((None, FTM, D), lambda i: (s, i, 0))


def _rowshape(T, dtype, c=D):
    return (jax.ShapeDtypeStruct((T, c), dtype), _frows(c))


def _sumshape(c=D):
    return (jax.ShapeDtypeStruct((8, c), f32), pl.BlockSpec((8, c), lambda i: (0, 0)))


def _add_colsum(ref, x, cols=None):
    @pl.when(pl.program_id(0) == 0)
    def _():
        if cols is None:
            ref[...] = jnp.zeros_like(ref)
        else:
            ref[:, cols] = jnp.zeros((8, x.shape[-1]), f32)

    if cols is None:
        ref[...] += _colsum8(x)
    else:
        ref[:, cols] += _colsum8(x)


def _rms(x):
    return lax.rsqrt(jnp.mean(x * x, axis=-1, keepdims=True) + EPS)


def _rms_bwd(dy_g, xn, rstd):
    return rstd * (dy_g - xn * jnp.mean(dy_g * xn, axis=-1, keepdims=True))


def _head_sum(x, bd):
    parts = []
    for cb in range(x.shape[-1] // 128):
        xb = x[:, cb * 128:(cb + 1) * 128]
        hi = xb.astype(bf16)
        lo = (xb - hi.astype(f32)).astype(bf16)
        parts.append(jnp.dot(hi, bd, preferred_element_type=f32) + jnp.dot(lo, bd, preferred_element_type=f32))
    return parts[0] if len(parts) == 1 else jnp.concatenate(parts, axis=1)


ZTM = 1024


def _in_proj_fwd(x, g, w_in_t, qg, kg, bd, after):
    T = x.shape[0]

    def body(x_ref, g_ref, w_ref, qg_ref, kg_ref, bd_ref, after_ref, z_ref, h_ref, qn_ref, kn_ref, hbuf):
        del after_ref
        j = pl.program_id(1)

        @pl.when(j == 0)
        def _():
            xv = x_ref[...]
            hv = (xv * _rms(xv) * g_ref[...]).astype(bf16)
            hbuf[...] = hv
            h_ref[...] = hv

        z = lax.dot_general(hbuf[...], w_ref[...], (_DIMS["nt"], ((), ())), preferred_element_type=f32)
        z_ref[...] = z

        def head_norm(gain_ref, scale):
            return z * lax.rsqrt(_head_sum(z * z, bd_ref[...]) * (1.0 / HEAD_DIM) + EPS) * gain_ref[...] * scale

        @pl.when(j == Z_Q)
        def _():
            qn_ref[...] = head_norm(qg_ref, HEAD_DIM ** -0.5)

        @pl.when(j == Z_K)
        def _():
            kn_ref[...] = head_norm(kg_ref, 1.0)

    tile = pl.BlockSpec((ZTM, D), lambda i, j: (i, 0))
    row = pl.BlockSpec((1, D), lambda i, j: (0, 0))
    return pl.pallas_call(
        body, name="mm_z", grid=(T // ZTM, 7),
        in_specs=[tile, row, pl.BlockSpec((D, D), lambda i, j: (_wsec_of_zsec(j), 0)), row, row,
                  pl.BlockSpec((128, 128), lambda i, j: (0, 0)), pl.BlockSpec(memory_space=pl.ANY)],
        out_specs=[pl.BlockSpec((None, ZTM, D), lambda i, j: (j, i, 0)), tile, tile, tile],
        out_shape=[jax.ShapeDtypeStruct((8, T, D), f32), jax.ShapeDtypeStruct((T, D), bf16),
                   jax.ShapeDtypeStruct((T, D), f32), jax.ShapeDtypeStruct((T, D), f32)],
        scratch_shapes=[pltpu.VMEM((ZTM, D), bf16)],
        compiler_params=_cparams(("parallel", "arbitrary")))(x, g, w_in_t, qg, kg, bd, after)


def _branches_fwd(c, ob, z8, g, gate_b, w_conv_out, w_attn_out):
    T = c.shape[0]

    def epilogue(yb, extra, const, out):
        cv = extra[0][...]
        r = cv * _rms(cv) * const[0][...]
        s = (r * _sig(r)).astype(bf16)
        ya = jnp.dot(s, const[2][...], preferred_element_type=f32)
        b_ref = const[1]
        g_a = _sig(extra[1][...] + b_ref[:, :D])
        g_b = _sig(extra[2][...] + b_ref[:, D:])
        out[0][...] = s
        out[1][...] = ya
        out[2][...] = yb
        out[3][...] = (g_a * ya + g_b * yb).astype(bf16)

    return _matmul_fused("mm_branches", ob[None], w_attn_out[None], ((0, 0),), epilogue,
                         [(c, _frows()), (z8, _fsec(Z_GA)), (z8, _fsec(Z_GB))], [g, gate_b, w_conv_out],
                         [_rowshape(T, bf16), _rowshape(T, f32), _rowshape(T, f32), _rowshape(T, bf16)])


def _out_norm2_fwd(mixed, w_out, x, g):
    T = x.shape[0]

    def epilogue(acc, extra, const, out):
        x1 = extra[0][...] + acc
        out[0][...] = x1
        out[1][...] = (x1 * _rms(x1) * const[0][...]).astype(bf16)

    return _matmul_fused("mm_t1_norm2", mixed[None], w_out[None], ((0, 0),), epilogue, [(x, _frows())], [g],
                         [_rowshape(T, f32), _rowshape(T, bf16)])


def _down_loss_fwd(f, w_down, x1, target):
    T = x1.shape[0]

    def epilogue(acc, extra, const, out):
        diff = extra[0][...] + acc - extra[1][...]
        dy = diff * (1.0 / D)
        out[0][...] = dy
        out[1][...] = dy.astype(bf16)
        _add_colsum(out[2], diff * diff)

    return _matmul_fused("mm_t2_loss", f[None], w_down[None], ((0, 0),), epilogue, [(x1, _frows()), (target, _frows())],
                         [], [_rowshape(T, f32), _rowshape(T, bf16), _sumshape()], sums=True)


def _up_norm2_bwd(du3, w_up_t, x1, dy, g, token):
    T = x1.shape[0]

    def epilogue(dh, extra, const, out):
        x1v = extra[0][...]
        rstd = _rms(x1v)
        xn = x1v * rstd
        dx1 = extra[1][...] + _rms_bwd(dh * const[0][...], xn, rstd)
        out[0][...] = dx1
        out[1][...] = dx1.astype(bf16)
        _add_colsum(out[2], dh * xn)

    return _matmul_fused("mm_dh2_norm2", du3, w_up_t.reshape(2, D_FF, D), ((0, 0), (1, 1)), epilogue,
                         [(x1, _frows()), (dy, _frows())], [g],
                         [_rowshape(T, f32), _rowshape(T, bf16), _sumshape()], sums=True, passed=[token])


def _out_gate_bwd(dx1b, w_out, z8, gate_b, ya, yb, dz8):
    T = ya.shape[0]

    def epilogue(dm, extra, const, out):
        b_ref = const[0]
        g_a = _sig(extra[0][...] + b_ref[:, :D])
        g_b = _sig(extra[1][...] + b_ref[:, D:])
        out[0][...] = (dm * g_a).astype(bf16)
        out[1][...] = (dm * g_b).astype(bf16)
        dla = dm * extra[2][...] * g_a * (1.0 - g_a)
        dlb = dm * extra[3][...] * g_b * (1.0 - g_b)
        out[2][0] = dla.astype(bf16)
        out[2][1] = dlb.astype(bf16)
        _add_colsum(out[3], dla, slice(0, D))
        _add_colsum(out[3], dlb, slice(D, 2 * D))

    return _matmul_fused(
        "mm_dmixed_gate", dx1b[None], w_out[None], ((0, 0),), epilogue,
        [(z8, _fsec(Z_GA)), (z8, _fsec(Z_GB)), (ya, _frows()), (yb, _frows())], [gate_b],
        [_rowshape(T, bf16), _rowshape(T, bf16),
         (jax.ShapeDtypeStruct(dz8.shape, bf16), pl.BlockSpec((2, FTM, D), lambda i: (1, i, 0))), _sumshape(2 * D)],
        nt=True, sums=True, passed=[dz8], aliases={0: 2})


def _convnorm_bwd(dya, w_conv_out, c, g):
    T = c.shape[0]

    def epilogue(ds, extra, const, out):
        cv = extra[0][...]
        rstd = _rms(cv)
        r0 = cv * rstd
        gv = const[0][...]
        r = r0 * gv
        sg = _sig(r)
        dr = ds * sg * (1.0 + r * (1.0 - sg))
        out[0][...] = _rms_bwd(dr * gv, r0, rstd)
        _add_colsum(out[1], dr * r0)

    return _matmul_fused("mm_ds_convnorm", dya[None], w_conv_out[None], ((0, 0),), epilogue, [(c, _frows())], [g],
                         [_rowshape(T, f32), _sumshape()], nt=True, sums=True)


def _in_norm1_bwd(dz8, w_in_t, x, dx1, g, token):
    T = x.shape[0]

    def epilogue(dh, extra, const, out):
        xv = extra[0][...]
        rstd = _rms(xv)
        xn = xv * rstd
        out[0][...] = extra[1][...] + _rms_bwd(dh * const[0][...], xn, rstd)
        _add_colsum(out[1], dh * xn)

    return _matmul_fused("mm_dh_norm1", dz8, w_in_t.reshape(7, D, D), tuple(zip(range(7), _W_OF_Z)), epilogue,
                         [(x, _frows()), (dx1, _frows())], [g], [_rowshape(T, f32), _sumshape()],
                         sums=True, passed=[token])


CCW = 256
CR = 64
HALO = 32


def _conv_fwd(z8, conv_w, conv_b, S):
    T = z8.shape[1]
    nb = T // S
    ncb = D // CCW

    def body(av_ref, ag_ref, w_ref, b_ref, c_ref, pad):
        pad[0:HALO, :] = jnp.zeros((HALO, CCW), f32)

        def fill(i, carry):
            r0 = pl.multiple_of(i * 256, 256)
            pad[pl.ds(HALO + r0, 256), :] = av_ref[pl.ds(r0, 256), :] * _sig(ag_ref[pl.ds(r0, 256), :])
            return carry

        lax.fori_loop(0, S // 256, fill, 0)
        bias = b_ref[...]

        def chunk(i, carry):
            r0 = pl.multiple_of(i * CR, CR)
            win = pad[pl.ds(r0, CR + HALO), :]
            acc = jnp.zeros((CR, CCW), f32) + bias
            for s in range(8):
                part = None
                for m in range((CONV_WIDTH - 1 - s) // 8 + 1):
                    j = CONV_WIDTH - 1 - 8 * m - s
                    term = win[24 - 8 * m:24 - 8 * m + CR + 8, :] * w_ref[j:j + 1, :]
                    part = term if part is None else part + term
                acc = acc + part[8 - s:8 - s + CR, :]
            c_ref[pl.ds(r0, CR), :] = acc
            return carry

        lax.fori_loop(0, S // CR, chunk, 0)

    zs = lambda s: pl.BlockSpec((None, S, CCW), lambda b, cb: (s, b, cb))
    return pl.pallas_call(
        body, name="conv_fwd", grid=(nb, ncb),
        in_specs=[zs(Z_AVAL), zs(Z_AGATE), pl.BlockSpec((CONV_WIDTH, CCW), lambda b, cb: (0, cb)),
                  pl.BlockSpec((1, CCW), lambda b, cb: (0, cb))],
        out_specs=pl.BlockSpec((S, CCW), lambda b, cb: (b, cb)),
        out_shape=jax.ShapeDtypeStruct((T, D), f32),
        scratch_shapes=[pltpu.VMEM((S + HALO, CCW), f32)],
        compiler_params=_cparams(("parallel", "parallel")))(z8, z8, conv_w, conv_b)


def _conv_bwd(dc, z8, conv_w, dz8, S):
    T = dc.shape[0]
    nb = T // S
    ncb = D // CCW

    def body(dc_ref, av_ref, ag_ref, w_ref, dz_in, dz_ref, dw_ref, apad, dpad, shbuf):
        del dz_in
        apad[0:HALO, :] = jnp.zeros((HALO, CCW), f32)
        dpad[S:S + HALO, :] = jnp.zeros((HALO, CCW), f32)
        dw_ref[...] = jnp.zeros_like(dw_ref)

        def fill(i, carry):
            r0 = pl.multiple_of(i * 256, 256)
            apad[pl.ds(HALO + r0, 256), :] = av_ref[pl.ds(r0, 256), :] * _sig(ag_ref[pl.ds(r0, 256), :])
            dpad[pl.ds(r0, 256), :] = dc_ref[pl.ds(r0, 256), :]
            return carry

        lax.fori_loop(0, S // 256, fill, 0)

        def chunk(i, carry):
            r0 = pl.multiple_of(i * CR, CR)
            dwin = dpad[pl.ds(r0, CR + HALO), :]
            da = jnp.zeros((CR, CCW), f32)
            for s in range(8):
                shbuf[...] = dwin[s:s + CR, :]
                dshift = shbuf[...]
                part = None
                for m in range((CONV_WIDTH - 1 - s) // 8 + 1):
                    j = CONV_WIDTH - 1 - 8 * m - s
                    term = dwin[8 * m:8 * m + CR + 8, :] * w_ref[j:j + 1, :]
                    part = term if part is None else part + term
                    a_lag = apad[pl.ds(r0 + HALO - 8 * m, CR), :]
                    dw_ref[8 * j:8 * j + 8, :] += _colsum8(dshift * a_lag)
                da = da + part[s:s + CR, :]
            dw_ref[8 * CONV_WIDTH:8 * CONV_WIDTH + 8, :] += _colsum8(dwin[0:CR, :])
            av = av_ref[pl.ds(r0, CR), :]
            sg = _sig(ag_ref[pl.ds(r0, CR), :])
            dz_ref[0, pl.ds(r0, CR), :] = (da * sg).astype(bf16)
            dz_ref[1, pl.ds(r0, CR), :] = (da * av * sg * (1.0 - sg)).astype(bf16)
            return carry

        lax.fori_loop(0, S // CR, chunk, 0)

    zs = lambda s: pl.BlockSpec((None, S, CCW), lambda b, cb: (s, b, cb))
    return pl.pallas_call(
        body, name="conv_bwd", grid=(nb, ncb),
        in_specs=[pl.BlockSpec((S, CCW), lambda b, cb: (b, cb)), zs(Z_AVAL), zs(Z_AGATE),
                  pl.BlockSpec((CONV_WIDTH, CCW), lambda b, cb: (0, cb)), pl.BlockSpec(memory_space=pl.ANY)],
        out_specs=[pl.BlockSpec((2, S, CCW), lambda b, cb: (0, b, cb)),
                   pl.BlockSpec((None, 256, CCW), lambda b, cb: (b, 0, cb))],
        out_shape=[jax.ShapeDtypeStruct(dz8.shape, bf16), jax.ShapeDtypeStruct((nb, 256, D), f32)],
        input_output_aliases={4: 0},
        scratch_shapes=[pltpu.VMEM((S + HALO, CCW), f32), pltpu.VMEM((S + HALO, CCW), f32),
                        pltpu.VMEM((CR, CCW), f32)],
        compiler_params=_cparams(("parallel", "parallel")))(dc, z8, z8, conv_w, dz8)


FR = 128
NFB = D_FF // CCW
FBW = 128


def _ffn_window(ref, i, r0):
    return ref[pl.ds(r0 - 8, FR + 8), :]


def _ffn_u(win, w_ref, b_ref):
    return (win[6:6 + FR, :] * w_ref[0:1, :] + win[7:7 + FR, :] * w_ref[1:2, :]
            + win[8:8 + FR, :] * w_ref[2:3, :] + b_ref[...])


def _ffn_fwd(u3, ffn_w, ffn_b, S):
    T = u3.shape[1]
    nb = T // S

    def body(uv_ref, ug_ref, wv_ref, wg_ref, bv_ref, bg_ref, f_ref):
        def chunk(first, i):
            r0 = 0 if first else pl.multiple_of(i * FR, FR)
            if first:
                z = jnp.zeros((8, CCW), f32)
                wv = jnp.concatenate([z, uv_ref[0:FR, :]], axis=0)
                wg = jnp.concatenate([z, ug_ref[0:FR, :]], axis=0)
            else:
                wv = _ffn_window(uv_ref, i, r0)
                wg = _ffn_window(ug_ref, i, r0)
            u_val = _ffn_u(wv, wv_ref, bv_ref)
            u_gate = _ffn_u(wg, wg_ref, bg_ref)
            f_ref[pl.ds(r0, FR), :] = (u_gate * _sig(u_gate) * u_val).astype(bf16)

        chunk(True, 0)

        def loop(i, carry):
            chunk(False, i)
            return carry

        lax.fori_loop(1, S // FR, loop, 0)

    us = lambda h: pl.BlockSpec((None, S, CCW), lambda b, cb: (h, b, cb))
    ws = lambda h: pl.BlockSpec((3, CCW), lambda b, cb: (0, h * NFB + cb))
    bs = lambda h: pl.BlockSpec((1, CCW), lambda b, cb: (0, h * NFB + cb))
    return pl.pallas_call(
        body, name="ffn_fwd", grid=(nb, NFB),
        in_specs=[us(0), us(1), ws(0), ws(1), bs(0), bs(1)],
        out_specs=pl.BlockSpec((S, CCW), lambda b, cb: (b, cb)),
        out_shape=jax.ShapeDtypeStruct((T, D_FF), bf16),
        compiler_params=_cparams(("parallel", "parallel")))(u3, u3, ffn_w, ffn_w, ffn_b, ffn_b)


def _ffn_bwd(u3, df, ffn_w, ffn_b, S):
    T = u3.shape[1]
    nb = T // S

    def body(uv_ref, ug_ref, df_ref, wv_ref, wg_ref, bv_ref, bg_ref, du_ref, dw_ref, dvpad, dgpad, shbuf):
        dvpad[S:S + 8, :] = jnp.zeros((8, FBW), f32)
        dgpad[S:S + 8, :] = jnp.zeros((8, FBW), f32)
        dw_ref[...] = jnp.zeros_like(dw_ref)

        def chunk(first, i):
            r0 = 0 if first else pl.multiple_of(i * FR, FR)
            if first:
                z = jnp.zeros((8, FBW), f32)
                wv = jnp.concatenate([z, uv_ref[0:FR, :]], axis=0)
                wg = jnp.concatenate([z, ug_ref[0:FR, :]], axis=0)
            else:
                wv = _ffn_window(uv_ref, i, r0)
                wg = _ffn_window(ug_ref, i, r0)
            taps = []
            for h, win in enumerate((wv, wg)):
                shbuf[2 * h] = win[6:6 + FR, :]
                shbuf[2 * h + 1] = win[7:7 + FR, :]
                taps.append((shbuf[2 * h], shbuf[2 * h + 1], win[8:8 + FR, :]))
            conv = lambda x, w_ref, b_ref: (x[0] * w_ref[0:1, :] + x[1] * w_ref[1:2, :] + x[2] * w_ref[2:3, :]
                                            + b_ref[...])
            u_val = conv(taps[0], wv_ref, bv_ref)
            u_gate = conv(taps[1], wg_ref, bg_ref)
            dfc = df_ref[pl.ds(r0, FR), :]
            sg = _sig(u_gate)
            d_val = dfc * u_gate * sg
            d_gate = dfc * u_val * sg * (1.0 + u_gate * (1.0 - sg))
            dvpad[pl.ds(r0, FR), :] = d_val
            dgpad[pl.ds(r0, FR), :] = d_gate
            for h, dd in enumerate((d_val, d_gate)):
                for j in range(3):
                    dw_ref[h, 8 * j:8 * j + 8, :] += _colsum8(dd * taps[h][j])
                dw_ref[h, 24:32, :] += _colsum8(dd)

        chunk(True, 0)

        def loop(i, carry):
            chunk(False, i)
            return carry

        lax.fori_loop(1, S // FR, loop, 0)

        def back(i, carry):
            r0 = pl.multiple_of(i * FR, FR)
            for h, (dpad, w_ref) in enumerate(((dvpad, wv_ref), (dgpad, wg_ref))):
                win = dpad[pl.ds(r0, FR + 8), :]
                du = (win[0:FR, :] * w_ref[2:3, :] + win[1:1 + FR, :] * w_ref[1:2, :]
                      + win[2:2 + FR, :] * w_ref[0:1, :])
                du_ref[h, pl.ds(r0, FR), :] = du.astype(bf16)
            return carry

        lax.fori_loop(0, S // FR, back, 0)

    ncb = D_FF // FBW
    us = lambda h: pl.BlockSpec((None, S, FBW), lambda b, cb: (h, b, cb))
    ws = lambda h: pl.BlockSpec((3, FBW), lambda b, cb: (0, h * ncb + cb))
    bs = lambda h: pl.BlockSpec((1, FBW), lambda b, cb: (0, h * ncb + cb))
    return pl.pallas_call(
        body, name="ffn_bwd", grid=(nb, ncb),
        in_specs=[us(0), us(1), pl.BlockSpec((S, FBW), lambda b, cb: (b, cb)), ws(0), ws(1), bs(0), bs(1)],
        out_specs=[pl.BlockSpec((2, S, FBW), lambda b, cb: (0, b, cb)),
                   pl.BlockSpec((None, 2, 32, FBW), lambda b, cb: (b, 0, 0, cb))],
        out_shape=[jax.ShapeDtypeStruct((2, T, D_FF), bf16), jax.ShapeDtypeStruct((nb, 2, 32, D_FF), f32)],
        scratch_shapes=[pltpu.VMEM((S + 8, FBW), f32), pltpu.VMEM((S + 8, FBW), f32),
                        pltpu.VMEM((4, FR, FBW), f32)],
        compiler_params=_cparams(("parallel", "parallel")))(u3, u3, df, ffn_w, ffn_w, ffn_b, ffn_b)


AB = ATTN_BLOCK


def _attn_bias_np():
    slopes = (np.float32(2.0) ** (np.float32(-8.0) * np.arange(1, N_HEADS + 1, dtype=np.float32)
                                  / np.float32(N_HEADS))).astype(np.float32)
    steps = (np.arange(AB)[:, None] + AB) - np.arange(2 * AB)[None, :]
    own = (np.arange(2 * AB) >= AB)[None, :]
    out = []
    for window, dil in GROUPS:
        valid = (steps >= 0) & (steps <= window // dil)
        dist = slopes[:, None, None] * (steps * dil).astype(np.float32)[None]
        kinds = [np.where(v[None], dist, np.float32(MASK_BIAS)) for v in (valid, valid & own)]
        out.append(np.stack(kinds, axis=1))
    return np.stack(out).astype(np.float32)


def _attn_bias():
    return jnp.asarray(_attn_bias_np())


def _head_masks():
    lane = lax.broadcasted_iota(jnp.int32, (1, 128), 1)
    return (lane < HEAD_DIM, lane >= HEAD_DIM)


def _perm_chunks(S, d):
    L = S // d
    ch = min(L, 256)
    out = []
    for r in range(d):
        for c in range(L // ch):
            start = r + d * ch * c
            out.append((pl.ds(start, ch, stride=d) if d > 1 else pl.ds(start, ch), r * L + c * ch, ch))
    return out


def _stack_heads(x, masks):
    return jnp.concatenate([jnp.where(masks[0], x, 0), jnp.where(masks[1], x, 0)], axis=0)


def _block_row(j):
    return j * AB if isinstance(j, int) else pl.multiple_of(j * AB, AB)


def _three_stages(n, stage_a, stage_b, stage_c, unroll):
    stage_a(0)
    stage_a(1)
    stage_b(0)

    def body(j, carry):
        stage_c(j - 1)
        stage_b(j)
        stage_a(j + 1)
        return carry

    lax.fori_loop(1, n - 1, body, 0, unroll=unroll)
    stage_c(n - 2)
    stage_b(n - 1)
    stage_c(n - 1)


_NT = (((1,), (1,)), ((), ()))
_TN = (((0,), (0,)), ((), ()))
SCH = 64


def _attn_fwd(qn, kn, z8, bias, S):
    T = qn.shape[0]
    nb = T // S
    nblk = S // AB

    def body(q_ref, k_ref, v_ref, bias_ref, o_ref, ob_ref, lse_ref, qs, ks, vs, s2, p2, ogp, lgp, *group_scratch):
        og, lg = group_scratch[:3], group_scratch[3:]
        masks = _head_masks()
        ks[0:AB, :] = jnp.zeros((AB, 128), bf16)
        vs[0:AB, :] = jnp.zeros((AB, 128), bf16)

        for g, (_, d) in enumerate(GROUPS):
            nsub = S // (d * AB)
            chunks = _perm_chunks(S, d)
            for src, dst, ch in chunks:
                qs[dst:dst + ch, :] = q_ref[src, :].astype(bf16)
                ks[AB + dst:AB + dst + ch, :] = k_ref[src, :].astype(bf16)
                vs[AB + dst:AB + dst + ch, :] = v_ref[src, :].astype(bf16)
            od, ld = (og[g], lg[g]) if d == 1 else (ogp, lgp)

            def scores(j):
                r0 = _block_row(j)
                q2 = _stack_heads(qs[pl.ds(r0, AB), :], masks)
                s2[j] = lax.dot_general(q2, ks[pl.ds(r0, 2 * AB), :], _NT, preferred_element_type=f32)

            def softmax(j, g=g, nsub=nsub, ld=ld):
                r0 = _block_row(j)
                kind = int(j % nsub == 0) if isinstance(j, int) else (j % nsub == 0).astype(jnp.int32)
                for cc in range(AB // SCH):
                    lses = []
                    for hh in range(2):
                        rows = pl.ds(hh * AB + cc * SCH, SCH)
                        sb = s2[j, rows, :] - bias_ref[g, hh, kind, cc * SCH:(cc + 1) * SCH, :]
                        m = jnp.max(sb, axis=-1, keepdims=True)
                        p = jnp.exp(sb - m)
                        den = jnp.sum(p, axis=-1, keepdims=True)
                        p2[j, rows, :] = (p * (1.0 / den)).astype(bf16)
                        lses.append(m + jnp.log(den))
                    ld[pl.ds(r0 + cc * SCH, SCH), :] = jnp.where(masks[0], lses[0], lses[1])

            def values(j, od=od):
                r0 = _block_row(j)
                pv2 = jnp.dot(p2[j], vs[pl.ds(r0, 2 * AB), :], preferred_element_type=f32)
                od[pl.ds(r0, AB), :] = jnp.where(masks[0], pv2[:AB], pv2[AB:])

            _three_stages(nblk, scores, softmax, values, nblk - 2)

            if d > 1:
                for src, dst, ch in chunks:
                    og[g][src, :] = ogp[dst:dst + ch, :]
                    lg[g][src, :] = lgp[dst:dst + ch, :]

        def combine(i, carry):
            rr = pl.ds(pl.multiple_of(i * 256, 256), 256)
            l0, l1, l2 = lg[0][rr, :], lg[1][rr, :], lg[2][rr, :]
            mx = jnp.maximum(jnp.maximum(l0, l1), l2)
            e0, e1, e2 = jnp.exp(l0 - mx), jnp.exp(l1 - mx), jnp.exp(l2 - mx)
            den = e0 + e1 + e2
            o = (e0 * og[0][rr, :] + e1 * og[1][rr, :] + e2 * og[2][rr, :]) / den
            o_ref[rr, :] = o
            ob_ref[rr, :] = o.astype(bf16)
            lse_ref[rr, :] = mx + jnp.log(den)
            return carry

        lax.fori_loop(0, S // 256, combine, 0)

    blk = pl.BlockSpec((S, 128), lambda b, hp: (b, hp))
    return pl.pallas_call(
        body, name="attn_fwd", grid=(nb, N_HEADS // 2),
        in_specs=[blk, blk, pl.BlockSpec((None, S, 128), lambda b, hp: (Z_V, b, hp)),
                  pl.BlockSpec((3, 2, 2, AB, 2 * AB), lambda b, hp: (0, hp, 0, 0, 0))],
        out_specs=[blk, blk, blk],
        out_shape=[jax.ShapeDtypeStruct((T, D), f32), jax.ShapeDtypeStruct((T, D), bf16),
                   jax.ShapeDtypeStruct((T, D), f32)],
        scratch_shapes=[pltpu.VMEM((S, 128), bf16), pltpu.VMEM((S + AB, 128), bf16), pltpu.VMEM((S + AB, 128), bf16),
                        pltpu.VMEM((nblk, 2 * AB, 2 * AB), f32), pltpu.VMEM((nblk, 2 * AB, 2 * AB), bf16),
                        pltpu.VMEM((S, 128), f32), pltpu.VMEM((S, 128), f32)] + [pltpu.VMEM((S, 128), f32)] * 6,
        compiler_params=_cparams(("parallel", "parallel")))(qn, kn, z8, bias)


def _attn_bwd(qn, kn, z8, do, o, lse, bias, bd, qg, kg, dz8, S, after):
    T = qn.shape[0]
    nb = T // S

    nblk = S // AB

    def body(q_ref, k_ref, v_ref, do_ref, o_ref, lse_ref, bias_ref, bd_ref, qraw_ref, kraw_ref, qg_ref, kg_ref,
             dz_in, after_ref, dz_ref, dqg_ref, dkg_ref,
             dq_ref, dk_ref, dv_ref, delta, qs, ks, vs, dos, lsp, dlp, s2, dp2, p2, ds2, dqp, dkp, dvp):
        del dz_in, after_ref
        masks = _head_masks()
        bdv = bd_ref[...]
        dq_ref[...] = jnp.zeros_like(dq_ref)
        dk_ref[...] = jnp.zeros_like(dk_ref)
        dv_ref[...] = jnp.zeros_like(dv_ref)
        ks[0:AB, :] = jnp.zeros((AB, 128), bf16)
        vs[0:AB, :] = jnp.zeros((AB, 128), bf16)

        def prep(i, carry):
            rr = pl.ds(pl.multiple_of(i * 256, 256), 256)
            delta[rr, :] = _head_sum(do_ref[rr, :] * o_ref[rr, :], bdv)
            return carry

        lax.fori_loop(0, S // 256, prep, 0, unroll=True)

        for g, (_, d) in enumerate(GROUPS):
            nsub = S // (d * AB)
            chunks = _perm_chunks(S, d)
            for src, dst, ch in chunks:
                qs[dst:dst + ch, :] = q_ref[src, :].astype(bf16)
                ks[AB + dst:AB + dst + ch, :] = k_ref[src, :].astype(bf16)
                vs[AB + dst:AB + dst + ch, :] = v_ref[src, :].astype(bf16)
                dos[dst:dst + ch, :] = do_ref[src, :].astype(bf16)
                lsp[dst:dst + ch, :] = lse_ref[src, :]
                dlp[dst:dst + ch, :] = delta[src, :]
            dkp[...] = jnp.zeros_like(dkp)
            dvp[...] = jnp.zeros_like(dvp)

            def scores(j):
                r0 = _block_row(j)
                q2 = _stack_heads(qs[pl.ds(r0, AB), :], masks)
                do2 = _stack_heads(dos[pl.ds(r0, AB), :], masks)
                s2[j] = lax.dot_general(q2, ks[pl.ds(r0, 2 * AB), :], _NT, preferred_element_type=f32)
                dp2[j] = lax.dot_general(do2, vs[pl.ds(r0, 2 * AB), :], _NT, preferred_element_type=f32)

            def probs(j, g=g, nsub=nsub):
                r0 = _block_row(j)
                kind = int(j % nsub == 0) if isinstance(j, int) else (j % nsub == 0).astype(jnp.int32)
                for cc in range(AB // SCH):
                    lse_c = lsp[pl.ds(r0 + cc * SCH, SCH), :]
                    del_c = dlp[pl.ds(r0 + cc * SCH, SCH), :]
                    for hh in range(2):
                        c0 = hh * HEAD_DIM
                        rows = pl.ds(hh * AB + cc * SCH, SCH)
                        sb = s2[j, rows, :] - bias_ref[g, hh, kind, cc * SCH:(cc + 1) * SCH, :]
                        p = jnp.exp(sb - lse_c[:, c0:c0 + 1])
                        p2[j, rows, :] = p.astype(bf16)
                        ds2[j, rows, :] = (p * (dp2[j, rows, :] - del_c[:, c0:c0 + 1])).astype(bf16)

            def grads(j):
                r0 = _block_row(j)
                q2 = _stack_heads(qs[pl.ds(r0, AB), :], masks)
                do2 = _stack_heads(dos[pl.ds(r0, AB), :], masks)
                dsb = ds2[j]
                t = jnp.dot(dsb, ks[pl.ds(r0, 2 * AB), :], preferred_element_type=f32)
                dqp[pl.ds(r0, AB), :] = jnp.where(masks[0], t[:AB], t[AB:])
                dkp[pl.ds(r0, 2 * AB), :] += lax.dot_general(dsb, q2, _TN, preferred_element_type=f32)
                dvp[pl.ds(r0, 2 * AB), :] += lax.dot_general(p2[j], do2, _TN, preferred_element_type=f32)

            _three_stages(nblk, scores, probs, grads, nblk - 2)

            for src, dst, ch in chunks:
                dq_ref[src, :] += dqp[dst:dst + ch, :]
                dk_ref[src, :] += dkp[AB + dst:AB + dst + ch, :]
                dv_ref[src, :] += dvp[AB + dst:AB + dst + ch, :]

        @pl.when(pl.program_id(1) == 0)
        def _():
            dqg_ref[...] = jnp.zeros_like(dqg_ref)
            dkg_ref[...] = jnp.zeros_like(dkg_ref)

        def norms(i, carry):
            rr = pl.ds(pl.multiple_of(i * 256, 256), 256)

            def one(raw, dn_scaled, g, dg_ref, sec):
                rstd = lax.rsqrt(_head_sum(raw * raw, bdv) * (1.0 / HEAD_DIM) + EPS)
                n = raw * rstd
                dg_ref[...] += _colsum8(dn_scaled * n)
                dn = dn_scaled * g
                draw = rstd * (dn - n * (_head_sum(dn * n, bdv) * (1.0 / HEAD_DIM)))
                dz_ref[sec, rr, :] = draw.astype(bf16)

            one(qraw_ref[rr, :], dq_ref[rr, :] * (HEAD_DIM ** -0.5), qg_ref[...], dqg_ref, 0)
            one(kraw_ref[rr, :], dk_ref[rr, :], kg_ref[...], dkg_ref, 1)
            dz_ref[2, rr, :] = dv_ref[rr, :].astype(bf16)
            dz_ref[3, rr, :] = jnp.zeros((256, 128), bf16)
            return carry

        lax.fori_loop(0, S // 256, norms, 0)

    blk = pl.BlockSpec((S, 128), lambda hp, b: (b, hp))
    sec = lambda s: pl.BlockSpec((None, S, 128), lambda hp, b: (s, b, hp))
    gain = pl.BlockSpec((1, 128), lambda hp, b: (0, hp))
    row = lambda dt, pad=0: pltpu.VMEM((S + pad, 128), dt)
    blocks = lambda dt: pltpu.VMEM((nblk, 2 * AB, 2 * AB), dt)
    return pl.pallas_call(
        body, name="attn_bwd", grid=(N_HEADS // 2, nb),
        in_specs=[blk, blk, sec(Z_V), blk, blk, blk,
                  pl.BlockSpec((3, 2, 2, AB, 2 * AB), lambda hp, b: (0, hp, 0, 0, 0)),
                  pl.BlockSpec((128, 128), lambda hp, b: (0, 0)), sec(Z_Q), sec(Z_K), gain, gain,
                  pl.BlockSpec(memory_space=pl.ANY), pl.BlockSpec(memory_space=pl.ANY)],
        out_specs=[pl.BlockSpec((4, S, 128), lambda hp, b: (1, b, hp)),
                   pl.BlockSpec((8, 128), lambda hp, b: (0, hp)), pl.BlockSpec((8, 128), lambda hp, b: (0, hp))],
        out_shape=[jax.ShapeDtypeStruct(dz8.shape, bf16), jax.ShapeDtypeStruct((8, D), f32),
                   jax.ShapeDtypeStruct((8, D), f32)],
        input_output_aliases={12: 0},
        scratch_shapes=[row(f32), row(f32), row(f32),
                        row(f32), row(bf16), row(bf16, AB), row(bf16, AB), row(bf16), row(f32), row(f32),
                        blocks(f32), blocks(f32), blocks(bf16), blocks(bf16), row(f32), row(f32, AB), row(f32, AB)],
        compiler_params=_cparams(("parallel", "arbitrary")))(qn, kn, z8, do, o, lse, bias, bd, z8, z8, qg, kg, dz8, after)


def _any_spec():
    return pl.BlockSpec(memory_space=pl.ANY)


AG_CHUNKS = 4


def _allgather_rows(shards, n_full):
    n = len(shards)
    parts = [(a, q) for a in range(n_full) for q in range(AG_CHUNKS)]

    def body(*refs):
        ins, outs = refs[:n], refs[n:2 * n]
        send_sems, recv_sems, local_sems = refs[2 * n:]
        x, y, c, me = _my_pos()
        sibling = (x, y, 1 - c)
        chips = [(1 - x, y), (x, 1 - y), (1 - x, 1 - y)]

        def idx(px, py, pc):
            return 4 * px + 2 * py + pc

        def copy(v, k, blk, to, own=False):
            a, q = parts[v]
            rows = pl.ds(q * (shards[a].shape[0] // AG_CHUNKS), shards[a].shape[0] // AG_CHUNKS)
            return pltpu.make_async_remote_copy(
                src_ref=ins[a].at[rows] if own else outs[a].at[blk, rows], dst_ref=outs[a].at[blk, rows],
                send_sem=send_sems.at[v, k], recv_sem=recv_sems.at[v, k], device_id=to, device_id_type=MESH)

        mine = [pltpu.make_async_copy(ins[a], outs[a].at[me], local_sems.at[a]) for a in range(n)]
        for cp in mine:
            cp.start()
        first = []
        for v in range(len(parts)):
            first.append(copy(v, 0, me, sibling, own=True))
            first += [copy(v, 1 + j, me, (*chip, c), own=True) for j, chip in enumerate(chips[:2])]
        for cp in first:
            cp.start()
        relay_blk = jnp.where(c == 1, idx(1 - x, y, c), idx(x, 1 - y, c))
        relay_to = (jnp.where(c == 1, x, 1 - x), jnp.where(c == 1, 1 - y, y), c)
        passed = []
        for v in range(len(parts)):
            for j, chip in enumerate(chips[:2]):
                copy(v, 1 + j, idx(*chip, c), (x, y, c)).wait_recv()
            cp = copy(v, 3, relay_blk, relay_to)
            cp.start()
            passed.append(cp)
            for j, chip in enumerate(chips):
                if j == 2:
                    copy(v, 3, idx(*chip, c), (x, y, c)).wait_recv()
                cp = copy(v, 4 + j, idx(*chip, c), sibling)
                cp.start()
                passed.append(cp)
        for v in range(len(parts)):
            copy(v, 0, idx(x, y, 1 - c), (x, y, c)).wait_recv()
            for j, chip in enumerate(chips):
                copy(v, 4 + j, idx(*chip, 1 - c), (x, y, c)).wait_recv()
        for cp in first + passed:
            cp.wait_send()
        for cp in mine:
            cp.wait()

    return pl.pallas_call(
        body, name="allgather_weights",
        in_specs=[_any_spec()] * n, out_specs=[_any_spec()] * n,
        out_shape=[jax.ShapeDtypeStruct((N_DEV,) + s.shape, s.dtype) for s in shards],
        scratch_shapes=[pltpu.SemaphoreType.DMA((len(parts), 7)), pltpu.SemaphoreType.DMA((len(parts), 7)),
                        pltpu.SemaphoreType.DMA((n,))],
    )(*shards)


def _peer(x, y, c, k):
    tx = 1 - x if (k >> 2) & 1 else x
    ty = 1 - y if (k >> 1) & 1 else y
    tc = 1 - c if k & 1 else c
    return (tx, ty, tc), 4 * tx + 2 * ty + tc


_PEER_ORDER = (2, 4, 6, 3, 5, 7, 1)


_HBM = pl.BlockSpec(memory_space=pltpu.HBM)
_SEM = pl.BlockSpec(memory_space=pltpu.SEMAPHORE)
_EFFECT = pltpu.SideEffectType.DATAFLOW_SIDE_EFFECTING


def _exchange_copies(srcs, lands, send_sems, recv_sems, gather, half):
    x, y, c, me = _my_pos()
    pick = lambda px, py: None if half is None else ((px == py) if half == 0 else (px != py))
    copies = []
    for k in _PEER_ORDER:
        tgt, tidx = _peer(x, y, c, k)
        for a in range(len(srcs)):
            copies.append((pltpu.make_async_remote_copy(
                src_ref=srcs[a] if gather else srcs[a].at[tidx], dst_ref=lands[a].at[me],
                send_sem=send_sems.at[7 * a + k - 1], recv_sem=recv_sems.at[7 * a + k - 1],
                device_id=tgt, device_id_type=MESH), pick(tgt[0], tgt[1])))
    return copies, pick(x, y)


def _when(cond, fn):
    if cond is None:
        fn()
    else:
        pl.when(cond)(fn)


def _exchange_start(name, srcs, lands=None, after=None, gather=None, half=None):
    n = len(srcs)
    gather = (lands is not None) if gather is None else gather
    if lands is None:
        lands = [lax.empty(g.shape, g.dtype) for g in srcs]
    extra = [] if after is None else [after]

    def body(*refs):
        src_refs, land_refs = refs[:n], refs[n:2 * n]
        send_sems, recv_sems = refs[2 * n + len(extra)], refs[2 * n + len(extra) + 1]
        token = refs[-1]
        for cp, sends in _exchange_copies(src_refs, land_refs, send_sems, recv_sems, gather, half)[0]:
            _when(sends, cp.start)
        token[...] = jnp.zeros_like(token)

    hbm = lambda a: pltpu.with_memory_space_constraint(a, pltpu.HBM)
    outs = pl.pallas_call(
        body, name=name,
        out_shape=(pltpu.SemaphoreType.DMA((7 * n,)), pltpu.SemaphoreType.DMA((7 * n,)),
                   *[pltpu.HBM(g.shape, g.dtype) for g in list(srcs) + list(lands)],
                   jax.ShapeDtypeStruct((8, 128), f32)),
        in_specs=[_HBM] * (2 * n) + [pl.BlockSpec(memory_space=pl.ANY)] * len(extra),
        out_specs=(_SEM, _SEM, *([_HBM] * (2 * n)), pl.BlockSpec(memory_space=pltpu.VMEM)),
        input_output_aliases={i: 2 + i for i in range(2 * n)},
        compiler_params=pltpu.CompilerParams(has_side_effects=_EFFECT),
    )(*[hbm(g) for g in srcs], *[hbm(g) for g in lands], *extra)
    return outs[0], outs[1], list(outs[2:2 + n]), list(outs[2 + n:2 + 2 * n]), outs[-1], gather, half


def _exchange_wait(name, started, after):
    send_sems, recv_sems, srcs, lands, _, gather, half = started
    n = len(srcs)
    after = list(after) if isinstance(after, (list, tuple)) else [after]

    def body(*refs):
        src_refs, land_refs = refs[:n], refs[n:2 * n]
        s_sems, r_sems = refs[2 * n], refs[2 * n + 1]
        copies, receives = _exchange_copies(src_refs, land_refs, s_sems, r_sems, gather, half)
        for cp, sends in copies:
            _when(sends, cp.wait_send)
            _when(receives, cp.wait_recv)

    outs = pl.pallas_call(
        body, name=name,
        out_shape=tuple(pltpu.HBM(a.shape, a.dtype) for a in list(srcs) + list(lands)),
        in_specs=[_HBM] * (2 * n) + [_SEM, _SEM] + [pl.BlockSpec(memory_space=pl.ANY)] * len(after),
        out_specs=tuple([_HBM] * (2 * n)),
        input_output_aliases={i: i for i in range(2 * n)},
        compiler_params=pltpu.CompilerParams(has_side_effects=_EFFECT),
    )(*srcs, *lands, send_sems, recv_sems, *after)
    return list(outs[:n]), list(outs[n:])


SMALL_ROWS = 128


def _small_start(name, sg, after=None):
    return _exchange_start(name, [sg], [lax.empty((N_DEV,) + sg.shape, f32)], after=after)


def _small_sum(name, me, started, after):
    (own,), (slots,) = _exchange_wait(name + "_wait", started, after)

    def body(me_ref, s_ref, own_ref, out_ref):
        acc = None
        for p in range(N_DEV):
            term = lax.cond(me_ref[0] == p, lambda: own_ref[...], lambda p=p: s_ref[p])
            acc = term if acc is None else acc + term
        out_ref[...] = acc

    return pl.pallas_call(
        body, name=name + "_sum",
        in_specs=[pl.BlockSpec(memory_space=pltpu.SMEM), pl.BlockSpec(memory_space=pltpu.VMEM),
                  pl.BlockSpec(memory_space=pltpu.VMEM)],
        out_specs=pl.BlockSpec(memory_space=pltpu.VMEM),
        out_shape=jax.ShapeDtypeStruct(own.shape, f32))(me, slots, own)


def _adam_math(g, w, m, v):
    m = ADAM_B1 * m + (1.0 - ADAM_B1) * g
    v = ADAM_B2 * v + (1.0 - ADAM_B2) * (g * g)
    m_hat = m / (1.0 - ADAM_B1 ** ADAM_STEP)
    v_hat = v / (1.0 - ADAM_B2 ** ADAM_STEP)
    delta = -ADAM_LR * (m_hat / (jnp.sqrt(v_hat) + ADAM_EPS) + ADAM_WD * w)
    return delta, m, v


def _adam_slots(name, me, slots, own, w, m, v, tr, transposed=False):
    rows = slots.shape[1]

    def body(me_ref, s_ref, own_ref, w_ref, m_ref, v_ref, g_ref, d_ref, nm_ref, nv_ref):
        mine = own_ref[...]
        g = None
        for p in range(N_DEV):
            term = lax.cond(me_ref[0] == p, lambda: mine, lambda p=p: s_ref[p]).astype(f32)
            g = term if g is None else g + term
        if transposed:
            g = g.T
        delta, nm, nv = _adam_math(g, w_ref[...], m_ref[...], v_ref[...])
        g_ref[...] = g
        d_ref[...] = delta
        nm_ref[...] = nm
        nv_ref[...] = nv

    mode = dict(pipeline_mode=pl.Buffered(1)) if rows == tr else {}
    if transposed:
        rs = pl.BlockSpec((D, tr), lambda i, me_ref: (0, i))
        rs_in = pl.BlockSpec((D, tr), lambda i, me_ref: (0, i), **mode)
    else:
        rs = pl.BlockSpec((tr, D), lambda i, me_ref: (i, 0))
        rs_in = pl.BlockSpec((tr, D), lambda i, me_ref: (i, 0), **mode)
    return pl.pallas_call(
        body, name=name,
        grid_spec=pltpu.PrefetchScalarGridSpec(
            num_scalar_prefetch=1, grid=(rows // tr,),
            in_specs=[pl.BlockSpec((N_DEV, tr, D), lambda i, me_ref: (0, i, 0), **mode),
                      pl.BlockSpec((None, tr, D), lambda i, me_ref: (me_ref[0], i, 0), **mode), rs_in, rs_in, rs_in],
            out_specs=[rs] * 4),
        out_shape=[jax.ShapeDtypeStruct(w.shape, f32)] * 4,
        compiler_params=_cparams(("parallel",)))(me, slots, own, w, m, v)


def _adam_small(g, w, m, v):
    def body(g_ref, w_ref, m_ref, v_ref, d_ref, nm_ref, nv_ref):
        delta, nm, nv = _adam_math(g_ref[...], w_ref[...], m_ref[...], v_ref[...])
        d_ref[...] = delta
        nm_ref[...] = nm
        nv_ref[...] = nv

    return pl.pallas_call(body, name="adam_small", out_shape=[jax.ShapeDtypeStruct(g.shape, f32)] * 3)(g, w, m, v)


FFN_PAD = 6 * D


_SMALL_PARTS = (("norm1_g", 1), ("gate_b", 2), ("conv_w", CONV_WIDTH), ("conv_b", 1), ("conv_norm_g", 1),
                ("q_norm_g", 1), ("k_norm_g", 1), ("norm2_g", 1), ("ffn_conv_w", 18), ("ffn_conv_b", 6), ("last", 1))


def _small_offsets():
    out, row = {}, 0
    for name, rows in _SMALL_PARTS:
        out[name] = row
        row += -(-rows // 8) * 8
    assert row == SMALL_ROWS
    return out


def _pack_small(norm1_g, gate_b, conv_w, conv_b, conv_norm_g, q_norm_g, k_norm_g, norm2_g, ffn_conv_w, ffn_conv_b,
                last_row=None):
    pad_h = lambda a: jnp.pad(a, ((0, 0), (0, D - HEAD_DIM)))
    pad_f = lambda a: jnp.pad(a, ((0, 0), (0, FFN_PAD - 2 * D_FF))).reshape(-1, D)
    parts = [norm1_g, gate_b.reshape(2, D), conv_w, conv_b, conv_norm_g, pad_h(q_norm_g), pad_h(k_norm_g), norm2_g,
             pad_f(ffn_conv_w), pad_f(ffn_conv_b), jnp.zeros((1, D), f32) if last_row is None else last_row]
    return jnp.concatenate([jnp.pad(p, ((0, -p.shape[0] % 8), (0, 0))) for p in parts], axis=0)


def _unpack_small(p):
    o = _small_offsets()
    rows = lambda name, n: p[o[name]:o[name] + n]
    ffn = lambda a: a.reshape(-1, FFN_PAD)[:, :2 * D_FF]
    return dict(
        norm1_g=rows("norm1_g", 1), gate_b=rows("gate_b", 2).reshape(1, 2 * D), conv_w=rows("conv_w", CONV_WIDTH),
        conv_b=rows("conv_b", 1), conv_norm_g=rows("conv_norm_g", 1), q_norm_g=rows("q_norm_g", 1)[:, :HEAD_DIM],
        k_norm_g=rows("k_norm_g", 1)[:, :HEAD_DIM], norm2_g=rows("norm2_g", 1),
        ffn_conv_w=ffn(rows("ffn_conv_w", 18)), ffn_conv_b=ffn(rows("ffn_conv_b", 6)))


_ADAM_TILE = {896: 128, 704: 704, 128: 128, 352: 176}


def kernel(x, norm1_g, w_in, gate_b, conv_w, conv_b, conv_norm_g, w_conv_out, q_norm_g, k_norm_g, w_attn_out, w_out, norm2_g, w_up, ffn_conv_w, ffn_conv_b, w_down, loss_target, m_norm1_g, m_w_in, m_gate_b, m_conv_w, m_conv_b, m_conv_norm_g, m_w_conv_out, m_q_norm_g, m_k_norm_g, m_w_attn_out, m_w_out, m_norm2_g, m_w_up, m_ffn_conv_w, m_ffn_conv_b, m_w_down, v_norm1_g, v_w_in, v_gate_b, v_conv_w, v_conv_b, v_conv_norm_g, v_w_conv_out, v_q_norm_g, v_k_norm_g, v_w_attn_out, v_w_out, v_norm2_g, v_w_up, v_ffn_conv_w, v_ffn_conv_b, v_w_down):
    BL, S, _ = x.shape
    T = BL * S
    me = 4 * lax.axis_index("x") + 2 * lax.axis_index("y") + lax.axis_index("c")
    xt = x.reshape(T, D)
    target = loss_target.reshape(T, D)

    big = dict(w_in=(w_in[0], m_w_in[0], v_w_in[0]), w_up=(w_up[0], m_w_up[0], v_w_up[0]),
               w_conv_out=(w_conv_out[0], m_w_conv_out[0], v_w_conv_out[0]),
               w_attn_out=(w_attn_out[0], m_w_attn_out[0], v_w_attn_out[0]),
               w_out=(w_out[0], m_w_out[0], v_w_out[0]), w_down=(w_down[0], m_w_down[0], v_w_down[0]))
    order = ["w_in", "w_conv_out", "w_attn_out", "w_out", "w_up", "w_down"]
    shards = [(big[n][0].T if n in ("w_in", "w_up") else big[n][0]).astype(bf16) for n in order]
    gathered = _allgather_rows(shards, 1)
    W = {"w_in": gathered[0].reshape(-1, D)}

    def place_cols(shard, full_cols):
        z = jnp.zeros((shard.shape[0], full_cols), f32)
        return lax.dynamic_update_slice(z, shard, (0, me * shard.shape[1]))

    zr = lambda a: jnp.zeros_like(a)
    conv_local = _pack_small(
        zr(norm1_g), zr(gate_b), place_cols(conv_w[0], D), zr(conv_b), zr(conv_norm_g), zr(q_norm_g), zr(k_norm_g),
        zr(norm2_g), place_cols(ffn_conv_w[0], 2 * D_FF), zr(ffn_conv_b))
    ga_conv = _small_start("gather_conv_start", conv_local, after=gathered[0])
    ga_proj = _exchange_start("gather_start_proj", shards[1:4], gathered[1:4], after=ga_conv[4])
    ga_ffn = _exchange_start("gather_start_ffn", shards[4:6], gathered[4:6], after=ga_proj[4])

    bd = (jnp.arange(128)[:, None] // HEAD_DIM == jnp.arange(128)[None, :] // HEAD_DIM).astype(bf16)
    bias = _attn_bias()
    qg = jnp.tile(q_norm_g, (1, N_HEADS))
    kg = jnp.tile(k_norm_g, (1, N_HEADS))

    z8, h, qn, kn = _in_proj_fwd(xt, norm1_g, W["w_in"], qg, kg, bd, ga_ffn[4])
    conv_all = _unpack_small(_small_sum("gather_conv", me.reshape(1), ga_conv, z8))
    conv_w_full, ffn_w_full = conv_all["conv_w"], conv_all["ffn_conv_w"]
    c = _conv_fwd(z8, conv_w_full, conv_b, S)
    o, ob, lse = _attn_fwd(qn, kn, z8, bias, S)
    for n, g in zip(order[1:4], _exchange_wait("gather_wait_proj", ga_proj, ob)[1]):
        W[n] = g.reshape(-1, D)
    s, ya, yb, mixed = _branches_fwd(c, ob, z8, conv_norm_g, gate_b, W["w_conv_out"], W["w_attn_out"])
    x1, h2 = _out_norm2_fwd(mixed, W["w_out"], xt, norm2_g)
    for n, g in zip(order[4:6], _exchange_wait("gather_wait_ffn", ga_ffn, x1)[1]):
        W[n] = g.reshape(-1, D)
    TNU = D_FF // 2
    u3 = _matmul_call(
        "mm_u", h2, W["w_up"],
        pl.BlockSpec((1024, D), lambda i, j, k: (i, 0)),
        pl.BlockSpec((TNU, D), lambda i, j, k: (j, 0)),
        pl.BlockSpec((None, 1024, TNU), lambda i, j, k: (j // 2, i, j % 2)),
        jax.ShapeDtypeStruct((2, T, D_FF), f32), (T // 1024, 4, 1), "nt", 1, 1024, TNU)
    f = _ffn_fwd(u3, ffn_w_full, ffn_conv_b, S)
    dy, dyb, lacc = _down_loss_fwd(f, W["w_down"], x1, target)
    loss_local = 0.5 / D * jnp.sum(lacc)

    df = _matmul("mm_df", dyb, W["w_down"], "nt", f32, tn=TNU)
    g_w_down = _matmul("mm_dwdn", f, dyb, "tn", bf16, tm=TNU)
    du3, dffn = _ffn_bwd(u3, df, ffn_w_full, ffn_conv_b, S)
    g_w_up = _matmul_call(
        "mm_dwup", du3, h2,
        pl.BlockSpec((None, T, TNU), lambda i, j, k: (i // 2, 0, i % 2)),
        pl.BlockSpec((T, D), lambda i, j, k: (0, 0)),
        pl.BlockSpec((TNU, D), lambda i, j, k: (i, 0)),
        jax.ShapeDtypeStruct((2 * D_FF, D), bf16), (4, 1, 1), "tn", 1, TNU, D)
    blocks8 = lambda a: a.reshape(N_DEV, -1, D)
    ex_ffn = _exchange_start("scatter_start_ffn", [blocks8(g_w_up), blocks8(g_w_down)])
    dx1, dx1b, dg_norm2 = _up_norm2_bwd(du3, W["w_up"], x1, dy, norm2_g, ex_ffn[4])
    g_w_out = _matmul("mm_dwo", mixed, dx1b, "tn", bf16, tm=512)
    dz8 = lax.empty((8, T, D), bf16)
    dya, dyb2, dz8, dg_gate = _out_gate_bwd(dx1b, W["w_out"], z8, gate_b, ya, yb, dz8)
    g_w_conv_out = _matmul("mm_dwco", s, dya, "tn", bf16, tm=512)
    g_w_attn_out = _matmul("mm_dwao", ob, dyb2, "tn", bf16, tm=512)
    ex_proj = _exchange_start("scatter_start_proj", [blocks8(g_w_conv_out), blocks8(g_w_attn_out), blocks8(g_w_out)])
    do = _matmul("mm_do", dyb2, W["w_attn_out"], "nt", f32, after=ex_proj[4])
    dc, dg_convnorm = _convnorm_bwd(dya, W["w_conv_out"], c, conv_norm_g)
    dz8a, dconv = _conv_bwd(dc, z8, conv_w_full, dz8, S)
    dwin_specs = lambda zsec, wsec: (
        pl.BlockSpec((None, T, D), lambda i, j, k: (zsec(i), 0, 0)), pl.BlockSpec((T, D), lambda i, j, k: (0, 0)),
        pl.BlockSpec((1024, D), lambda i, j, k: (wsec(i), 0)), jax.ShapeDtypeStruct((7 * D, D), bf16))
    g_w_in = _matmul_call("mm_dwin_a", dz8a, h, *dwin_specs(lambda i: i, lambda i: jnp.where(i < 2, i, i + 3)),
                          (4, 1, 1), "tn", 1, D, D)
    ex_in_a = _exchange_start("scatter_start_in_a", [blocks8(g_w_in)], half=0)
    dz8b, dg_q, dg_k = _attn_bwd(qn, kn, z8, do, o, lse, bias, bd, qg, kg, dz8a, S, ex_in_a[4])
    g_w_in = _matmul_call("mm_dwin_b", dz8b, h, *dwin_specs(lambda i: i + 4, lambda i: i + 2),
                          (3, 1, 1), "tn", 1, D, D, fill=ex_in_a[2][0].reshape(7 * D, D))
    ex_in_b = _exchange_start("scatter_start_in_b", [blocks8(g_w_in)], ex_in_a[3], gather=False, half=1)
    grad_x, dg_norm1 = _in_norm1_bwd(dz8b, W["w_in"], xt, dx1, norm1_g, ex_in_b[4])

    sum8 = lambda a: a.reshape(-1, 8, a.shape[-1]).sum(axis=1)
    dconv_s = sum8(dconv.sum(axis=0))
    dffn_s = dffn.sum(axis=0).reshape(2, 4, 8, D_FF).sum(axis=2)
    dffn_w = jnp.concatenate([dffn_s[0, :3], dffn_s[1, :3]], axis=1)
    dffn_b = jnp.concatenate([dffn_s[0, 3:4], dffn_s[1, 3:4]], axis=1)
    fold = lambda a: sum8(a).reshape(N_HEADS, HEAD_DIM).sum(axis=0)[None]
    small_g_local = _pack_small(
        sum8(dg_norm1), sum8(dg_gate), dconv_s[:CONV_WIDTH], dconv_s[CONV_WIDTH:], sum8(dg_convnorm),
        fold(dg_q), fold(dg_k), sum8(dg_norm2), dffn_w, dffn_b,
        last_row=jnp.pad(loss_local.reshape(1, 1), ((0, 0), (0, D - 1))))
    sg_start = _small_start("small_grads_start", small_g_local)

    own, slots = {}, {}
    for tag, ex, names_ in (("ffn", ex_ffn, ("w_up", "w_down")),
                            ("proj", ex_proj, ("w_conv_out", "w_attn_out", "w_out"))):
        sent, landed = _exchange_wait("scatter_wait_" + tag, ex, sg_start[4])
        for n, src, land in zip(names_, sent, landed):
            own[n], slots[n] = src, land
    sent, landed = _exchange_wait("scatter_wait_in_a", ex_in_a[:2] + (ex_in_b[2], ex_in_b[3]) + ex_in_a[4:],
                                  sg_start[4])
    sent, landed = _exchange_wait("scatter_wait_in_b", ex_in_b[:2] + (sent, landed) + ex_in_b[4:], sg_start[4])
    own["w_in"], slots["w_in"] = sent[0], landed[0]

    res, adam_done = {}, []
    for n in order:
        w, m, v = big[n]
        outs = _adam_slots("adam_" + n, me.reshape(1), slots[n], own[n], w, m, v, _ADAM_TILE[slots[n].shape[1]],
                           transposed=n in ("w_in", "w_up"))
        adam_done.append(outs[0])
        res[n] = [a[None] for a in outs]
    small_g = _small_sum("small_grads", me.reshape(1), sg_start, adam_done)
    loss = small_g[_small_offsets()["last"], 0]

    col = lambda a, width: lax.dynamic_slice(a, (0, me * width), (a.shape[0], width))
    small_w_true = _pack_small(norm1_g, gate_b, conv_w_full, conv_b, conv_norm_g, q_norm_g, k_norm_g, norm2_g,
                               ffn_w_full, ffn_conv_b)
    place_m = lambda a, full: place_cols(a[0], full)
    small_m = _pack_small(m_norm1_g, m_gate_b, place_m(m_conv_w, D), m_conv_b, m_conv_norm_g, m_q_norm_g, m_k_norm_g,
                          m_norm2_g, place_m(m_ffn_conv_w, 2 * D_FF), m_ffn_conv_b)
    small_v = _pack_small(v_norm1_g, v_gate_b, place_m(v_conv_w, D), v_conv_b, v_conv_norm_g, v_q_norm_g, v_k_norm_g,
                          v_norm2_g, place_m(v_ffn_conv_w, 2 * D_FF), v_ffn_conv_b)
    sd, sm, sv = _adam_small(small_g, small_w_true, small_m, small_v)
    for i, packed in enumerate((small_g, sd, sm, sv)):
        u = _unpack_small(packed)
        u["conv_w"] = col(u["conv_w"], D // N_DEV)
        u["ffn_conv_w"] = col(u["ffn_conv_w"], 2 * D_FF // N_DEV)
        for n, a in u.items():
            res.setdefault(n, [None] * 4)[i] = a[None] if n in ("conv_w", "ffn_conv_w") else a

    names = ["norm1_g", "w_in", "gate_b", "conv_w", "conv_b", "conv_norm_g", "w_conv_out", "q_norm_g", "k_norm_g",
             "w_attn_out", "w_out", "norm2_g", "w_up", "ffn_conv_w", "ffn_conv_b", "w_down"]
    out = [loss, grad_x.reshape(BL, S, D)]
    for i in range(4):
        out += [res[n][i] for n in names]
    return tuple(out)
```

```python
import functools

import jax
import jax.numpy as jnp
import numpy as np
from jax import lax
from jax.experimental import pallas as pl
from jax.experimental.pallas import tpu as pltpu

f32 = jnp.float32
bf16 = jnp.bfloat16

D = 1024
N_HEADS = 16
HEAD_DIM = 64
CONV_WIDTH = 31
D_FF = 2816
GROUPS = ((128, 1), (512, 4), (2048, 16))
ATTN_BLOCK = 128
EPS = 1e-6
N_DEV = 8
MESH = pl.DeviceIdType.MESH

ADAM_LR = 0.001
ADAM_B1 = 0.9
ADAM_B2 = 0.999
ADAM_EPS = 1e-08
ADAM_WD = 0.01
ADAM_STEP = 10

VMEM_LIMIT = 56 * 1024 * 1024
MASK_BIAS = 1e30

Z_AVAL, Z_AGATE, Z_GA, Z_GB, Z_Q, Z_K, Z_V = 0, 1, 2, 3, 4, 5, 6


_W_OF_Z = (0, 1, 5, 6, 2, 3, 4)


def _wsec_of_zsec(j):
    return jnp.where(j < 2, j, jnp.where(j < 4, j + 3, j - 2))


def _sig(x):
    return 1.0 / (1.0 + jnp.exp(-x))


def _colsum8(x):
    return x.reshape(-1, 8, x.shape[-1]).sum(axis=0)


def _cparams(sem):
    return pltpu.CompilerParams(dimension_semantics=sem, vmem_limit_bytes=VMEM_LIMIT)


def _my_pos():
    x, y, c = lax.axis_index("x"), lax.axis_index("y"), lax.axis_index("c")
    return x, y, c, 4 * x + 2 * y + c


_DIMS = {"nn": ((1,), (0,)), "nt": ((1,), (1,)), "tn": ((0,), (0,))}


def _matmul_call(name, a, b, a_spec, b_spec, o_spec, out_shape, grid, mode, nk, tm, tn, after=None, fill=None):
    dims = (_DIMS[mode], ((), ()))
    extra = ([] if after is None else [after]) + ([] if fill is None else [fill])

    def body(a_ref, b_ref, *rest):
        o_ref, scratch = rest[len(extra)], rest[len(extra) + 1:]
        part = lax.dot_general(a_ref[...], b_ref[...], dims, preferred_element_type=f32)
        if nk == 1:
            o_ref[...] = part.astype(o_ref.dtype)
        else:
            acc = scratch[0]
            k = pl.program_id(2)

            @pl.when(k == 0)
            def _():
                acc[...] = part

            @pl.when(k > 0)
            def _():
                acc[...] += part

            @pl.when(k == nk - 1)
            def _():
                o_ref[...] = acc[...].astype(o_ref.dtype)

    scratch = [] if nk == 1 else [pltpu.VMEM((tm, tn), f32)]
    return pl.pallas_call(
        body, name=name, grid=grid, in_specs=[a_spec, b_spec] + [pl.BlockSpec(memory_space=pl.ANY)] * len(extra),
        out_specs=o_spec, out_shape=out_shape, input_output_aliases={} if fill is None else {1 + len(extra): 0},
        scratch_shapes=scratch, compiler_params=_cparams(("parallel", "parallel", "arbitrary")),
    )(a, b, *extra)


def _matmul(name, a, b, mode, out_dtype, tm=1024, tn=1024, tk=None, after=None):
    if mode == "nn":
        (M, K), (_, N) = a.shape, b.shape
    elif mode == "nt":
        (M, K), (N, _) = a.shape, b.shape
    else:
        (K, M), (_, N) = a.shape, b.shape
    tm, tn = min(tm, M), min(tn, N)
    tk = K if tk is None else tk
    nk = K // tk
    assert M % tm == 0 and N % tn == 0 and K % tk == 0
    if mode == "tn":
        a_spec = pl.BlockSpec((tk, tm), lambda i, j, k: (k, i))
    else:
        a_spec = pl.BlockSpec((tm, tk), lambda i, j, k: (i, k))
    if mode == "nt":
        b_spec = pl.BlockSpec((tn, tk), lambda i, j, k: (j, k))
    else:
        b_spec = pl.BlockSpec((tk, tn), lambda i, j, k: (k, j))
    o_spec = pl.BlockSpec((tm, tn), lambda i, j, k: (i, j))
    return _matmul_call(name, a, b, a_spec, b_spec, o_spec, jax.ShapeDtypeStruct((M, N), out_dtype),
                        (M // tm, N // tn, nk), mode, nk, tm, tn, after=after)


FTM = 512


def _matmul_fused(name, a, b, pairs, epilogue, extras, consts, outs, nt=False, sums=False, passed=(), aliases=None):
    sa, M, kk = a.shape
    na = max(i for i, _ in pairs) + 1
    ne, nc, npass = len(extras), len(consts), len(passed)
    dims = (_DIMS["nt" if nt else "nn"], ((), ()))

    def body(a_ref, b_ref, *rest):
        acc = None
        for i, j in pairs:
            part = lax.dot_general(a_ref[i], b_ref[j], dims, preferred_element_type=f32)
            acc = part if acc is None else acc + part
        epilogue(acc, rest[:ne], rest[ne:ne + nc], rest[ne + nc + npass:])

    whole = lambda arr: pl.BlockSpec(arr.shape, lambda i, nd=arr.ndim: (0,) * nd, pipeline_mode=pl.Buffered(1))
    io_alias = {2 + ne + nc + k: v for k, v in (aliases or {}).items()}
    return pl.pallas_call(
        body, name=name, grid=(M // FTM,),
        in_specs=[pl.BlockSpec((na, FTM, kk), lambda i: (0, i, 0)), whole(b)] + [s for _, s in extras]
        + [whole(c) for c in consts] + [pl.BlockSpec(memory_space=pl.ANY)] * npass,
        out_specs=[s for _, s in outs], out_shape=[s for s, _ in outs], input_output_aliases=io_alias,
        compiler_params=_cparams(("arbitrary" if sums else "parallel",)),
    )(a, b, *[x for x, _ in extras], *consts, *passed)


def _frows(c=D):
    return pl.BlockSpec((FTM, c), lambda i: (i, 0))


def _fsec(s):
    return pl.BlockSpec((None, FTM, D), lambda i: (s, i, 0))


def _rowshape(T, dtype, c=D):
    return (jax.ShapeDtypeStruct((T, c), dtype), _frows(c))


def _sumshape(c=D):
    return (jax.ShapeDtypeStruct((8, c), f32), pl.BlockSpec((8, c), lambda i: (0, 0)))


def _add_colsum(ref, x, cols=None):
    @pl.when(pl.program_id(0) == 0)
    def _():
        if cols is None:
            ref[...] = jnp.zeros_like(ref)
        else:
            ref[:, cols] = jnp.zeros((8, x.shape[-1]), f32)

    if cols is None:
        ref[...] += _colsum8(x)
    else:
        ref[:, cols] += _colsum8(x)


def _rms(x):
    return lax.rsqrt(jnp.mean(x * x, axis=-1, keepdims=True) + EPS)


def _rms_bwd(dy_g, xn, rstd):
    return rstd * (dy_g - xn * jnp.mean(dy_g * xn, axis=-1, keepdims=True))


def _head_sum(x, bd):
    parts = []
    for cb in range(x.shape[-1] // 128):
        xb = x[:, cb * 128:(cb + 1) * 128]
        hi = xb.astype(bf16)
        lo = (xb - hi.astype(f32)).astype(bf16)
        parts.append(jnp.dot(hi, bd, preferred_element_type=f32) + jnp.dot(lo, bd, preferred_element_type=f32))
    return parts[0] if len(parts) == 1 else jnp.concatenate(parts, axis=1)


ZTM = 1024


def _in_proj_fwd(x, g, w_in_t, qg, kg, bd, after):
    T = x.shape[0]

    def body(x_ref, g_ref, w_ref, qg_ref, kg_ref, bd_ref, after_ref, z_ref, h_ref, qn_ref, kn_ref, hbuf):
        del after_ref
        j = pl.program_id(1)

        @pl.when(j == 0)
        def _():
            xv = x_ref[...]
            hv = (xv * _rms(xv) * g_ref[...]).astype(bf16)
            hbuf[...] = hv
            h_ref[...] = hv

        z = lax.dot_general(hbuf[...], w_ref[...], (_DIMS["nt"], ((), ())), preferred_element_type=f32)
        z_ref[...] = z

        def head_norm(gain_ref, scale):
            return z * lax.rsqrt(_head_sum(z * z, bd_ref[...]) * (1.0 / HEAD_DIM) + EPS) * gain_ref[...] * scale

        @pl.when(j == Z_Q)
        def _():
            qn_ref[...] = head_norm(qg_ref, HEAD_DIM ** -0.5)

        @pl.when(j == Z_K)
        def _():
            kn_ref[...] = head_norm(kg_ref, 1.0)

    tile = pl.BlockSpec((ZTM, D), lambda i, j: (i, 0))
    row = pl.BlockSpec((1, D), lambda i, j: (0, 0))
    return pl.pallas_call(
        body, name="mm_z", grid=(T // ZTM, 7),
        in_specs=[tile, row, pl.BlockSpec((D, D), lambda i, j: (_wsec_of_zsec(j), 0)), row, row,
                  pl.BlockSpec((128, 128), lambda i, j: (0, 0)), pl.BlockSpec(memory_space=pl.ANY)],
        out_specs=[pl.BlockSpec((None, ZTM, D), lambda i, j: (j, i, 0)), tile, tile, tile],
        out_shape=[jax.ShapeDtypeStruct((8, T, D), f32), jax.ShapeDtypeStruct((T, D), bf16),
                   jax.ShapeDtypeStruct((T, D), f32), jax.ShapeDtypeStruct((T, D), f32)],
        scratch_shapes=[pltpu.VMEM((ZTM, D), bf16)],
        compiler_params=_cparams(("parallel", "arbitrary")))(x, g, w_in_t, qg, kg, bd, after)


def _branches_fwd(c, ob, z8, g, gate_b, w_conv_out, w_attn_out):
    T = c.shape[0]

    def epilogue(yb, extra, const, out):
        cv = extra[0][...]
        r = cv * _rms(cv) * const[0][...]
        s = (r * _sig(r)).astype(bf16)
        ya = jnp.dot(s, const[2][...], preferred_element_type=f32)
        b_ref = const[1]
        g_a = _sig(extra[1][...] + b_ref[:, :D])
        g_b = _sig(extra[2][...] + b_ref[:, D:])
        out[0][...] = s
        out[1][...] = ya
        out[2][...] = yb
        out[3][...] = (g_a * ya + g_b * yb).astype(bf16)

    return _matmul_fused("mm_branches", ob[None], w_attn_out[None], ((0, 0),), epilogue,
                         [(c, _frows()), (z8, _fsec(Z_GA)), (z8, _fsec(Z_GB))], [g, gate_b, w_conv_out],
                         [_rowshape(T, bf16), _rowshape(T, f32), _rowshape(T, f32), _rowshape(T, bf16)])


def _out_norm2_fwd(mixed, w_out, x, g):
    T = x.shape[0]

    def epilogue(acc, extra, const, out):
        x1 = extra[0][...] + acc
        out[0][...] = x1
        out[1][...] = (x1 * _rms(x1) * const[0][...]).astype(bf16)

    return _matmul_fused("mm_t1_norm2", mixed[None], w_out[None], ((0, 0),), epilogue, [(x, _frows())], [g],
                         [_rowshape(T, f32), _rowshape(T, bf16)])


def _down_loss_fwd(f, w_down, x1, target):
    T = x1.shape[0]

    def epilogue(acc, extra, const, out):
        diff = extra[0][...] + acc - extra[1][...]
        dy = diff * (1.0 / D)
        out[0][...] = dy
        out[1][...] = dy.astype(bf16)
        _add_colsum(out[2], diff * diff)

    return _matmul_fused("mm_t2_loss", f[None], w_down[None], ((0, 0),), epilogue, [(x1, _frows()), (target, _frows())],
                         [], [_rowshape(T, f32), _rowshape(T, bf16), _sumshape()], sums=True)


def _up_norm2_bwd(du3, w_up_t, x1, dy, g, token):
    T = x1.shape[0]

    def epilogue(dh, extra, const, out):
        x1v = extra[0][...]
        rstd = _rms(x1v)
        xn = x1v * rstd
        dx1 = extra[1][...] + _rms_bwd(dh * const[0][...], xn, rstd)
        out[0][...] = dx1
        out[1][...] = dx1.astype(bf16)
        _add_colsum(out[2], dh * xn)

    return _matmul_fused("mm_dh2_norm2", du3, w_up_t.reshape(2, D_FF, D), ((0, 0), (1, 1)), epilogue,
                         [(x1, _frows()), (dy, _frows())], [g],
                         [_rowshape(T, f32), _rowshape(T, bf16), _sumshape()], sums=True, passed=[token])


def _out_gate_bwd(dx1b, w_out, z8, gate_b, ya, yb, dz8):
    T = ya.shape[0]

    def epilogue(dm, extra, const, out):
        b_ref = const[0]
        g_a = _sig(extra[0][...] + b_ref[:, :D])
        g_b = _sig(extra[1][...] + b_ref[:, D:])
        out[0][...] = (dm * g_a).astype(bf16)
        out[1][...] = (dm * g_b).astype(bf16)
        dla = dm * extra[2][...] * g_a * (1.0 - g_a)
        dlb = dm * extra[3][...] * g_b * (1.0 - g_b)
        out[2][0] = dla.astype(bf16)
        out[2][1] = dlb.astype(bf16)
        _add_colsum(out[3], dla, slice(0, D))
        _add_colsum(out[3], dlb, slice(D, 2 * D))

    return _matmul_fused(
        "mm_dmixed_gate", dx1b[None], w_out[None], ((0, 0),), epilogue,
        [(z8, _fsec(Z_GA)), (z8, _fsec(Z_GB)), (ya, _frows()), (yb, _frows())], [gate_b],
        [_rowshape(T, bf16), _rowshape(T, bf16),
         (jax.ShapeDtypeStruct(dz8.shape, bf16), pl.BlockSpec((2, FTM, D), lambda i: (1, i, 0))), _sumshape(2 * D)],
        nt=True, sums=True, passed=[dz8], aliases={0: 2})


def _convnorm_bwd(dya, w_conv_out, c, g):
    T = c.shape[0]

    def epilogue(ds, extra, const, out):
        cv = extra[0][...]
        rstd = _rms(cv)
        r0 = cv * rstd
        gv = const[0][...]
        r = r0 * gv
        sg = _sig(r)
        dr = ds * sg * (1.0 + r * (1.0 - sg))
        out[0][...] = _rms_bwd(dr * gv, r0, rstd)
        _add_colsum(out[1], dr * r0)

    return _matmul_fused("mm_ds_convnorm", dya[None], w_conv_out[None], ((0, 0),), epilogue, [(c, _frows())], [g],
                         [_rowshape(T, f32), _sumshape()], nt=True, sums=True)


def _in_norm1_bwd(dz8, w_in_t, x, dx1, g, token):
    T = x.shape[0]

    def epilogue(dh, extra, const, out):
        xv = extra[0][...]
        rstd = _rms(xv)
        xn = xv * rstd
        out[0][...] = extra[1][...] + _rms_bwd(dh * const[0][...], xn, rstd)
        _add_colsum(out[1], dh * xn)

    return _matmul_fused("mm_dh_norm1", dz8, w_in_t.reshape(7, D, D), tuple(zip(range(7), _W_OF_Z)), epilogue,
                         [(x, _frows()), (dx1, _frows())], [g], [_rowshape(T, f32), _sumshape()],
                         sums=True, passed=[token])


CCW = 256
CR = 64
HALO = 32


def _conv_fwd(z8, conv_w, conv_b, S):
    T = z8.shape[1]
    nb = T // S
    ncb = D // CCW

    def body(av_ref, ag_ref, w_ref, b_ref, c_ref, pad):
        pad[0:HALO, :] = jnp.zeros((HALO, CCW), f32)

        def fill(i, carry):
            r0 = pl.multiple_of(i * 256, 256)
            pad[pl.ds(HALO + r0, 256), :] = av_ref[pl.ds(r0, 256), :] * _sig(ag_ref[pl.ds(r0, 256), :])
            return carry

        lax.fori_loop(0, S // 256, fill, 0)
        bias = b_ref[...]

        def chunk(i, carry):
            r0 = pl.multiple_of(i * CR, CR)
            win = pad[pl.ds(r0, CR + HALO), :]
            acc = jnp.zeros((CR, CCW), f32) + bias
            for s in range(8):
                part = None
                for m in range((CONV_WIDTH - 1 - s) // 8 + 1):
                    j = CONV_WIDTH - 1 - 8 * m - s
                    term = win[24 - 8 * m:24 - 8 * m + CR + 8, :] * w_ref[j:j + 1, :]
                    part = term if part is None else part + term
                acc = acc + part[8 - s:8 - s + CR, :]
            c_ref[pl.ds(r0, CR), :] = acc
            return carry

        lax.fori_loop(0, S // CR, chunk, 0)

    zs = lambda s: pl.BlockSpec((None, S, CCW), lambda b, cb: (s, b, cb))
    return pl.pallas_call(
        body, name="conv_fwd", grid=(nb, ncb),
        in_specs=[zs(Z_AVAL), zs(Z_AGATE), pl.BlockSpec((CONV_WIDTH, CCW), lambda b, cb: (0, cb)),
                  pl.BlockSpec((1, CCW), lambda b, cb: (0, cb))],
        out_specs=pl.BlockSpec((S, CCW), lambda b, cb: (b, cb)),
        out_shape=jax.ShapeDtypeStruct((T, D), f32),
        scratch_shapes=[pltpu.VMEM((S + HALO, CCW), f32)],
        compiler_params=_cparams(("parallel", "parallel")))(z8, z8, conv_w, conv_b)


def _conv_bwd(dc, z8, conv_w, dz8, S):
    T = dc.shape[0]
    nb = T // S
    ncb = D // CCW

    def body(dc_ref, av_ref, ag_ref, w_ref, dz_in, dz_ref, dw_ref, apad, dpad, shbuf):
        del dz_in
        apad[0:HALO, :] = jnp.zeros((HALO, CCW), f32)
        dpad[S:S + HALO, :] = jnp.zeros((HALO, CCW), f32)
        dw_ref[...] = jnp.zeros_like(dw_ref)

        def fill(i, carry):
            r0 = pl.multiple_of(i * 256, 256)
            apad[pl.ds(HALO + r0, 256), :] = av_ref[pl.ds(r0, 256), :] * _sig(ag_ref[pl.ds(r0, 256), :])
            dpad[pl.ds(r0, 256), :] = dc_ref[pl.ds(r0, 256), :]
            return carry

        lax.fori_loop(0, S // 256, fill, 0)

        def chunk(i, carry):
            r0 = pl.multiple_of(i * CR, CR)
            dwin = dpad[pl.ds(r0, CR + HALO), :]
            da = jnp.zeros((CR, CCW), f32)
            for s in range(8):
                shbuf[...] = dwin[s:s + CR, :]
                dshift = shbuf[...]
                part = None
                for m in range((CONV_WIDTH - 1 - s) // 8 + 1):
                    j = CONV_WIDTH - 1 - 8 * m - s
                    term = dwin[8 * m:8 * m + CR + 8, :] * w_ref[j:j + 1, :]
                    part = term if part is None else part + term
                    a_lag = apad[pl.ds(r0 + HALO - 8 * m, CR), :]
                    dw_ref[8 * j:8 * j + 8, :] += _colsum8(dshift * a_lag)
                da = da + part[s:s + CR, :]
            dw_ref[8 * CONV_WIDTH:8 * CONV_WIDTH + 8, :] += _colsum8(dwin[0:CR, :])
            av = av_ref[pl.ds(r0, CR), :]
            sg = _sig(ag_ref[pl.ds(r0, CR), :])
            dz_ref[0, pl.ds(r0, CR), :] = (da * sg).astype(bf16)
            dz_ref[1, pl.ds(r0, CR), :] = (da * av * sg * (1.0 - sg)).astype(bf16)
            return carry

        lax.fori_loop(0, S // CR, chunk, 0)

    zs = lambda s: pl.BlockSpec((None, S, CCW), lambda b, cb: (s, b, cb))
    return pl.pallas_call(
        body, name="conv_bwd", grid=(nb, ncb),
        in_specs=[pl.BlockSpec((S, CCW), lambda b, cb: (b, cb)), zs(Z_AVAL), zs(Z_AGATE),
                  pl.BlockSpec((CONV_WIDTH, CCW), lambda b, cb: (0, cb)), pl.BlockSpec(memory_space=pl.ANY)],
        out_specs=[pl.BlockSpec((2, S, CCW), lambda b, cb: (0, b, cb)),
                   pl.BlockSpec((None, 256, CCW), lambda b, cb: (b, 0, cb))],
        out_shape=[jax.ShapeDtypeStruct(dz8.shape, bf16), jax.ShapeDtypeStruct((nb, 256, D), f32)],
        input_output_aliases={4: 0},
        scratch_shapes=[pltpu.VMEM((S + HALO, CCW), f32), pltpu.VMEM((S + HALO, CCW), f32),
                        pltpu.VMEM((CR, CCW), f32)],
        compiler_params=_cparams(("parallel", "parallel")))(dc, z8, z8, conv_w, dz8)


FR = 128
NFB = D_FF // CCW
FBW = 128


def _ffn_window(ref, i, r0):
    return ref[pl.ds(r0 - 8, FR + 8), :]


def _ffn_u(win, w_ref, b_ref):
    return (win[6:6 + FR, :] * w_ref[0:1, :] + win[7:7 + FR, :] * w_ref[1:2, :]
            + win[8:8 + FR, :] * w_ref[2:3, :] + b_ref[...])


def _ffn_fwd(u3, ffn_w, ffn_b, S):
    T = u3.shape[1]
    nb = T // S

    def body(uv_ref, ug_ref, wv_ref, wg_ref, bv_ref, bg_ref, f_ref):
        def chunk(first, i):
            r0 = 0 if first else pl.multiple_of(i * FR, FR)
            if first:
                z = jnp.zeros((8, CCW), f32)
                wv = jnp.concatenate([z, uv_ref[0:FR, :]], axis=0)
                wg = jnp.concatenate([z, ug_ref[0:FR, :]], axis=0)
            else:
                wv = _ffn_window(uv_ref, i, r0)
                wg = _ffn_window(ug_ref, i, r0)
            u_val = _ffn_u(wv, wv_ref, bv_ref)
            u_gate = _ffn_u(wg, wg_ref, bg_ref)
            f_ref[pl.ds(r0, FR), :] = (u_gate * _sig(u_gate) * u_val).astype(bf16)

        chunk(True, 0)

        def loop(i, carry):
            chunk(False, i)
            return carry

        lax.fori_loop(1, S // FR, loop, 0)

    us = lambda h: pl.BlockSpec((None, S, CCW), lambda b, cb: (h, b, cb))
    ws = lambda h: pl.BlockSpec((3, CCW), lambda b, cb: (0, h * NFB + cb))
    bs = lambda h: pl.BlockSpec((1, CCW), lambda b, cb: (0, h * NFB + cb))
    return pl.pallas_call(
        body, name="ffn_fwd", grid=(nb, NFB),
        in_specs=[us(0), us(1), ws(0), ws(1), bs(0), bs(1)],
        out_specs=pl.BlockSpec((S, CCW), lambda b, cb: (b, cb)),
        out_shape=jax.ShapeDtypeStruct((T, D_FF), bf16),
        compiler_params=_cparams(("parallel", "parallel")))(u3, u3, ffn_w, ffn_w, ffn_b, ffn_b)


def _ffn_bwd(u3, df, ffn_w, ffn_b, S):
    T = u3.shape[1]
    nb = T // S

    def body(uv_ref, ug_ref, df_ref, wv_ref, wg_ref, bv_ref, bg_ref, du_ref, dw_ref, dvpad, dgpad, shbuf):
        dvpad[S:S + 8, :] = jnp.zeros((8, FBW), f32)
        dgpad[S:S + 8, :] = jnp.zeros((8, FBW), f32)
        dw_ref[...] = jnp.zeros_like(dw_ref)

        def chunk(first, i):
            r0 = 0 if first else pl.multiple_of(i * FR, FR)
            if first:
                z = jnp.zeros((8, FBW), f32)
                wv = jnp.concatenate([z, uv_ref[0:FR, :]], axis=0)
                wg = jnp.concatenate([z, ug_ref[0:FR, :]], axis=0)
            else:
                wv = _ffn_window(uv_ref, i, r0)
                wg = _ffn_window(ug_ref, i, r0)
            taps = []
            for h, win in enumerate((wv, wg)):
                shbuf[2 * h] = win[6:6 + FR, :]
                shbuf[2 * h + 1] = win[7:7 + FR, :]
                taps.append((shbuf[2 * h], shbuf[2 * h + 1], win[8:8 + FR, :]))
            conv = lambda x, w_ref, b_ref: (x[0] * w_ref[0:1, :] + x[1] * w_ref[1:2, :] + x[2] * w_ref[2:3, :]
                                            + b_ref[...])
            u_val = conv(taps[0], wv_ref, bv_ref)
            u_gate = conv(taps[1], wg_ref, bg_ref)
            dfc = df_ref[pl.ds(r0, FR), :]
            sg = _sig(u_gate)
            d_val = dfc * u_gate * sg
            d_gate = dfc * u_val * sg * (1.0 + u_gate * (1.0 - sg))
            dvpad[pl.ds(r0, FR), :] = d_val
            dgpad[pl.ds(r0, FR), :] = d_gate
            for h, dd in enumerate((d_val, d_gate)):
                for j in range(3):
                    dw_ref[h, 8 * j:8 * j + 8, :] += _colsum8(dd * taps[h][j])
                dw_ref[h, 24:32, :] += _colsum8(dd)

        chunk(True, 0)

        def loop(i, carry):
            chunk(False, i)
            return carry

        lax.fori_loop(1, S // FR, loop, 0)

        def back(i, carry):
            r0 = pl.multiple_of(i * FR, FR)
            for h, (dpad, w_ref) in enumerate(((dvpad, wv_ref), (dgpad, wg_ref))):
                win = dpad[pl.ds(r0, FR + 8), :]
                du = (win[0:FR, :] * w_ref[2:3, :] + win[1:1 + FR, :] * w_ref[1:2, :]
                      + win[2:2 + FR, :] * w_ref[0:1, :])
                du_ref[h, pl.ds(r0, FR), :] = du.astype(bf16)
            return carry

        lax.fori_loop(0, S // FR, back, 0)

    ncb = D_FF // FBW
    us = lambda h: pl.BlockSpec((None, S, FBW), lambda b, cb: (h, b, cb))
    ws = lambda h: pl.BlockSpec((3, FBW), lambda b, cb: (0, h * ncb + cb))
    bs = lambda h: pl.BlockSpec((1, FBW), lambda b, cb: (0, h * ncb + cb))
    return pl.pallas_call(
        body, name="ffn_bwd", grid=(nb, ncb),
        in_specs=[us(0), us(1), pl.BlockSpec((S, FBW), lambda b, cb: (b, cb)), ws(0), ws(1), bs(0), bs(1)],
        out_specs=[pl.BlockSpec((2, S, FBW), lambda b, cb: (0, b, cb)),
                   pl.BlockSpec((None, 2, 32, FBW), lambda b, cb: (b, 0, 0, cb))],
        out_shape=[jax.ShapeDtypeStruct((2, T, D_FF), bf16), jax.ShapeDtypeStruct((nb, 2, 32, D_FF), f32)],
        scratch_shapes=[pltpu.VMEM((S + 8, FBW), f32), pltpu.VMEM((S + 8, FBW), f32),
                        pltpu.VMEM((4, FR, FBW), f32)],
        compiler_params=_cparams(("parallel", "parallel")))(u3, u3, df, ffn_w, ffn_w, ffn_b, ffn_b)


AB = ATTN_BLOCK


def _attn_bias_np():
    slopes = (np.float32(2.0) ** (np.float32(-8.0) * np.arange(1, N_HEADS + 1, dtype=np.float32)
                                  / np.float32(N_HEADS))).astype(np.float32)
    steps = (np.arange(AB)[:, None] + AB) - np.arange(2 * AB)[None, :]
    own = (np.arange(2 * AB) >= AB)[None, :]
    out = []
    for window, dil in GROUPS:
        valid = (steps >= 0) & (steps <= window // dil)
        dist = slopes[:, None, None] * (steps * dil).astype(np.float32)[None]
        kinds = [np.where(v[None], dist, np.float32(MASK_BIAS)) for v in (valid, valid & own)]
        out.append(np.stack(kinds, axis=1))
    return np.stack(out).astype(np.float32)


def _attn_bias():
    return jnp.asarray(_attn_bias_np())


def _head_masks():
    lane = lax.broadcasted_iota(jnp.int32, (1, 128), 1)
    return (lane < HEAD_DIM, lane >= HEAD_DIM)


def _perm_chunks(S, d):
    L = S // d
    ch = min(L, 256)
    out = []
    for r in range(d):
        for c in range(L // ch):
            start = r + d * ch * c
            out.append((pl.ds(start, ch, stride=d) if d > 1 else pl.ds(start, ch), r * L + c * ch, ch))
    return out


def _stack_heads(x, masks):
    return jnp.concatenate([jnp.where(masks[0], x, 0), jnp.where(masks[1], x, 0)], axis=0)


def _block_row(j):
    return j * AB if isinstance(j, int) else pl.multiple_of(j * AB, AB)


def _three_stages(n, stage_a, stage_b, stage_c, unroll):
    stage_a(0)
    stage_a(1)
    stage_b(0)

    def body(j, carry):
        stage_c(j - 1)
        stage_b(j)
        stage_a(j + 1)
        return carry

    lax.fori_loop(1, n - 1, body, 0, unroll=unroll)
    stage_c(n - 2)
    stage_b(n - 1)
    stage_c(n - 1)


_NT = (((1,), (1,)), ((), ()))
_TN = (((0,), (0,)), ((), ()))
SCH = 64


def _attn_fwd(qn, kn, z8, bias, S):
    T = qn.shape[0]
    nb = T // S
    nblk = S // AB

    def body(q_ref, k_ref, v_ref, bias_ref, o_ref, ob_ref, lse_ref, qs, ks, vs, s2, p2, ogp, lgp, *group_scratch):
        og, lg = group_scratch[:3], group_scratch[3:]
        masks = _head_masks()
        ks[0:AB, :] = jnp.zeros((AB, 128), bf16)
        vs[0:AB, :] = jnp.zeros((AB, 128), bf16)

        for g, (_, d) in enumerate(GROUPS):
            nsub = S // (d * AB)
            chunks = _perm_chunks(S, d)
            for src, dst, ch in chunks:
                qs[dst:dst + ch, :] = q_ref[src, :].astype(bf16)
                ks[AB + dst:AB + dst + ch, :] = k_ref[src, :].astype(bf16)
                vs[AB + dst:AB + dst + ch, :] = v_ref[src, :].astype(bf16)
            od, ld = (og[g], lg[g]) if d == 1 else (ogp, lgp)

            def scores(j):
                r0 = _block_row(j)
                q2 = _stack_heads(qs[pl.ds(r0, AB), :], masks)
                s2[j] = lax.dot_general(q2, ks[pl.ds(r0, 2 * AB), :], _NT, preferred_element_type=f32)

            def softmax(j, g=g, nsub=nsub, ld=ld):
                r0 = _block_row(j)
                kind = int(j % nsub == 0) if isinstance(j, int) else (j % nsub == 0).astype(jnp.int32)
                for cc in range(AB // SCH):
                    lses = []
                    for hh in range(2):
                        rows = pl.ds(hh * AB + cc * SCH, SCH)
                        sb = s2[j, rows, :] - bias_ref[g, hh, kind, cc * SCH:(cc + 1) * SCH, :]
                        m = jnp.max(sb, axis=-1, keepdims=True)
                        p = jnp.exp(sb - m)
                        den = jnp.sum(p, axis=-1, keepdims=True)
                        p2[j, rows, :] = (p * (1.0 / den)).astype(bf16)
                        lses.append(m + jnp.log(den))
                    ld[pl.ds(r0 + cc * SCH, SCH), :] = jnp.where(masks[0], lses[0], lses[1])

            def values(j, od=od):
                r0 = _block_row(j)
                pv2 = jnp.dot(p2[j], vs[pl.ds(r0, 2 * AB), :], preferred_element_type=f32)
                od[pl.ds(r0, AB), :] = jnp.where(masks[0], pv2[:AB], pv2[AB:])

            _three_stages(nblk, scores, softmax, values, nblk - 2)

            if d > 1:
                for src, dst, ch in chunks:
                    og[g][src, :] = ogp[dst:dst + ch, :]
                    lg[g][src, :] = lgp[dst:dst + ch, :]

        def combine(i, carry):
            rr = pl.ds(pl.multiple_of(i * 256, 256), 256)
            l0, l1, l2 = lg[0][rr, :], lg[1][rr, :], lg[2][rr, :]
            mx = jnp.maximum(jnp.maximum(l0, l1), l2)
            e0, e1, e2 = jnp.exp(l0 - mx), jnp.exp(l1 - mx), jnp.exp(l2 - mx)
            den = e0 + e1 + e2
            o = (e0 * og[0][rr, :] + e1 * og[1][rr, :] + e2 * og[2][rr, :]) / den
            o_ref[rr, :] = o
            ob_ref[rr, :] = o.astype(bf16)
            lse_ref[rr, :] = mx + jnp.log(den)
            return carry

        lax.fori_loop(0, S // 256, combine, 0)

    blk = pl.BlockSpec((S, 128), lambda b, hp: (b, hp))
    return pl.pallas_call(
        body, name="attn_fwd", grid=(nb, N_HEADS // 2),
        in_specs=[blk, blk, pl.BlockSpec((None, S, 128), lambda b, hp: (Z_V, b, hp)),
                  pl.BlockSpec((3, 2, 2, AB, 2 * AB), lambda b, hp: (0, hp, 0, 0, 0))],
        out_specs=[blk, blk, blk],
        out_shape=[jax.ShapeDtypeStruct((T, D), f32), jax.ShapeDtypeStruct((T, D), bf16),
                   jax.ShapeDtypeStruct((T, D), f32)],
        scratch_shapes=[pltpu.VMEM((S, 128), bf16), pltpu.VMEM((S + AB, 128), bf16), pltpu.VMEM((S + AB, 128), bf16),
                        pltpu.VMEM((nblk, 2 * AB, 2 * AB), f32), pltpu.VMEM((nblk, 2 * AB, 2 * AB), bf16),
                        pltpu.VMEM((S, 128), f32), pltpu.VMEM((S, 128), f32)] + [pltpu.VMEM((S, 128), f32)] * 6,
        compiler_params=_cparams(("parallel", "parallel")))(qn, kn, z8, bias)


def _attn_bwd(qn, kn, z8, do, o, lse, bias, bd, qg, kg, dz8, S, after):
    T = qn.shape[0]
    nb = T // S

    nblk = S // AB

    def body(q_ref, k_ref, v_ref, do_ref, o_ref, lse_ref, bias_ref, bd_ref, qraw_ref, kraw_ref, qg_ref, kg_ref,
             dz_in, after_ref, dz_ref, dqg_ref, dkg_ref,
             dq_ref, dk_ref, dv_ref, delta, qs, ks, vs, dos, lsp, dlp, s2, dp2, p2, ds2, dqp, dkp, dvp):
        del dz_in, after_ref
        masks = _head_masks()
        bdv = bd_ref[...]
        dq_ref[...] = jnp.zeros_like(dq_ref)
        dk_ref[...] = jnp.zeros_like(dk_ref)
        dv_ref[...] = jnp.zeros_like(dv_ref)
        ks[0:AB, :] = jnp.zeros((AB, 128), bf16)
        vs[0:AB, :] = jnp.zeros((AB, 128), bf16)

        def prep(i, carry):
            rr = pl.ds(pl.multiple_of(i * 256, 256), 256)
            delta[rr, :] = _head_sum(do_ref[rr, :] * o_ref[rr, :], bdv)
            return carry

        lax.fori_loop(0, S // 256, prep, 0, unroll=True)

        for g, (_, d) in enumerate(GROUPS):
            nsub = S // (d * AB)
            chunks = _perm_chunks(S, d)
            for src, dst, ch in chunks:
                qs[dst:dst + ch, :] = q_ref[src, :].astype(bf16)
                ks[AB + dst:AB + dst + ch, :] = k_ref[src, :].astype(bf16)
                vs[AB + dst:AB + dst + ch, :] = v_ref[src, :].astype(bf16)
                dos[dst:dst + ch, :] = do_ref[src, :].astype(bf16)
                lsp[dst:dst + ch, :] = lse_ref[src, :]
                dlp[dst:dst + ch, :] = delta[src, :]
            dkp[...] = jnp.zeros_like(dkp)
            dvp[...] = jnp.zeros_like(dvp)

            def scores(j):
                r0 = _block_row(j)
                q2 = _stack_heads(qs[pl.ds(r0, AB), :], masks)
                do2 = _stack_heads(dos[pl.ds(r0, AB), :], masks)
                s2[j] = lax.dot_general(q2, ks[pl.ds(r0, 2 * AB), :], _NT, preferred_element_type=f32)
                dp2[j] = lax.dot_general(do2, vs[pl.ds(r0, 2 * AB), :], _NT, preferred_element_type=f32)

            def probs(j, g=g, nsub=nsub):
                r0 = _block_row(j)
                kind = int(j % nsub == 0) if isinstance(j, int) else (j % nsub == 0).astype(jnp.int32)
                for cc in range(AB // SCH):
                    lse_c = lsp[pl.ds(r0 + cc * SCH, SCH), :]
                    del_c = dlp[pl.ds(r0 + cc * SCH, SCH), :]
                    for hh in range(2):
                        c0 = hh * HEAD_DIM
                        rows = pl.ds(hh * AB + cc * SCH, SCH)
                        sb = s2[j, rows, :] - bias_ref[g, hh, kind, cc * SCH:(cc + 1) * SCH, :]
                        p = jnp.exp(sb - lse_c[:, c0:c0 + 1])
                        p2[j, rows, :] = p.astype(bf16)
                        ds2[j, rows, :] = (p * (dp2[j, rows, :] - del_c[:, c0:c0 + 1])).astype(bf16)

            def grads(j):
                r0 = _block_row(j)
                q2 = _stack_heads(qs[pl.ds(r0, AB), :], masks)
                do2 = _stack_heads(dos[pl.ds(r0, AB), :], masks)
                dsb = ds2[j]
                t = jnp.dot(dsb, ks[pl.ds(r0, 2 * AB), :], preferred_element_type=f32)
                dqp[pl.ds(r0, AB), :] = jnp.where(masks[0], t[:AB], t[AB:])
                dkp[pl.ds(r0, 2 * AB), :] += lax.dot_general(dsb, q2, _TN, preferred_element_type=f32)
                dvp[pl.ds(r0, 2 * AB), :] += lax.dot_general(p2[j], do2, _TN, preferred_element_type=f32)

            _three_stages(nblk, scores, probs, grads, nblk - 2)

            for src, dst, ch in chunks:
                dq_ref[src, :] += dqp[dst:dst + ch, :]
                dk_ref[src, :] += dkp[AB + dst:AB + dst + ch, :]
                dv_ref[src, :] += dvp[AB + dst:AB + dst + ch, :]

        @pl.when(pl.program_id(1) == 0)
        def _():
            dqg_ref[...] = jnp.zeros_like(dqg_ref)
            dkg_ref[...] = jnp.zeros_like(dkg_ref)

        def norms(i, carry):
            rr = pl.ds(pl.multiple_of(i * 256, 256), 256)

            def one(raw, dn_scaled, g, dg_ref, sec):
                rstd = lax.rsqrt(_head_sum(raw * raw, bdv) * (1.0 / HEAD_DIM) + EPS)
                n = raw * rstd
                dg_ref[...] += _colsum8(dn_scaled * n)
                dn = dn_scaled * g
                draw = rstd * (dn - n * (_head_sum(dn * n, bdv) * (1.0 / HEAD_DIM)))
                dz_ref[sec, rr, :] = draw.astype(bf16)

            one(qraw_ref[rr, :], dq_ref[rr, :] * (HEAD_DIM ** -0.5), qg_ref[...], dqg_ref, 0)
            one(kraw_ref[rr, :], dk_ref[rr, :], kg_ref[...], dkg_ref, 1)
            dz_ref[2, rr, :] = dv_ref[rr, :].astype(bf16)
            dz_ref[3, rr, :] = jnp.zeros((256, 128), bf16)
            return carry

        lax.fori_loop(0, S // 256, norms, 0, unroll=True)

    blk = pl.BlockSpec((S, 128), lambda hp, b: (b, hp))
    sec = lambda s: pl.BlockSpec((None, S, 128), lambda hp, b: (s, b, hp))
    gain = pl.BlockSpec((1, 128), lambda hp, b: (0, hp))
    row = lambda dt, pad=0: pltpu.VMEM((S + pad, 128), dt)
    blocks = lambda dt: pltpu.VMEM((nblk, 2 * AB, 2 * AB), dt)
    return pl.pallas_call(
        body, name="attn_bwd", grid=(N_HEADS // 2, nb),
        in_specs=[blk, blk, sec(Z_V), blk, blk, blk,
                  pl.BlockSpec((3, 2, 2, AB, 2 * AB), lambda hp, b: (0, hp, 0, 0, 0)),
                  pl.BlockSpec((128, 128), lambda hp, b: (0, 0)), sec(Z_Q), sec(Z_K), gain, gain,
                  pl.BlockSpec(memory_space=pl.ANY), pl.BlockSpec(memory_space=pl.ANY)],
        out_specs=[pl.BlockSpec((4, S, 128), lambda hp, b: (1, b, hp)),
                   pl.BlockSpec((8, 128), lambda hp, b: (0, hp)), pl.BlockSpec((8, 128), lambda hp, b: (0, hp))],
        out_shape=[jax.ShapeDtypeStruct(dz8.shape, bf16), jax.ShapeDtypeStruct((8, D), f32),
                   jax.ShapeDtypeStruct((8, D), f32)],
        input_output_aliases={12: 0},
        scratch_shapes=[row(f32), row(f32), row(f32),
                        row(f32), row(bf16), row(bf16, AB), row(bf16, AB), row(bf16), row(f32), row(f32),
                        blocks(f32), blocks(f32), blocks(bf16), blocks(bf16), row(f32), row(f32, AB), row(f32, AB)],
        compiler_params=_cparams(("parallel", "arbitrary")))(qn, kn, z8, do, o, lse, bias, bd, z8, z8, qg, kg, dz8, after)


def _any_spec():
    return pl.BlockSpec(memory_space=pl.ANY)


AG_CHUNKS = 4


def _allgather_rows(shards, n_full):
    n = len(shards)
    parts = [(a, q) for a in range(n_full) for q in range(AG_CHUNKS)]

    def body(*refs):
        ins, outs = refs[:n], refs[n:2 * n]
        send_sems, recv_sems, local_sems = refs[2 * n:]
        x, y, c, me = _my_pos()
        sibling = (x, y, 1 - c)
        chips = [(1 - x, y), (x, 1 - y), (1 - x, 1 - y)]

        def idx(px, py, pc):
            return 4 * px + 2 * py + pc

        def copy(v, k, blk, to, own=False):
            a, q = parts[v]
            rows = pl.ds(q * (shards[a].shape[0] // AG_CHUNKS), shards[a].shape[0] // AG_CHUNKS)
            return pltpu.make_async_remote_copy(
                src_ref=ins[a].at[rows] if own else outs[a].at[blk, rows], dst_ref=outs[a].at[blk, rows],
                send_sem=send_sems.at[v, k], recv_sem=recv_sems.at[v, k], device_id=to, device_id_type=MESH)

        mine = [pltpu.make_async_copy(ins[a], outs[a].at[me], local_sems.at[a]) for a in range(n)]
        for cp in mine:
            cp.start()
        first = []
        for v in range(len(parts)):
            first.append(copy(v, 0, me, sibling, own=True))
            first += [copy(v, 1 + j, me, (*chip, c), own=True) for j, chip in enumerate(chips[:2])]
        for cp in first:
            cp.start()
        relay_blk = jnp.where(c == 1, idx(1 - x, y, c), idx(x, 1 - y, c))
        relay_to = (jnp.where(c == 1, x, 1 - x), jnp.where(c == 1, 1 - y, y), c)
        passed = []
        for v in range(len(parts)):
            for j, chip in enumerate(chips[:2]):
                copy(v, 1 + j, idx(*chip, c), (x, y, c)).wait_recv()
            cp = copy(v, 3, relay_blk, relay_to)
            cp.start()
            passed.append(cp)
            for j, chip in enumerate(chips):
                if j == 2:
                    copy(v, 3, idx(*chip, c), (x, y, c)).wait_recv()
                cp = copy(v, 4 + j, idx(*chip, c), sibling)
                cp.start()
                passed.append(cp)
        for v in range(len(parts)):
            copy(v, 0, idx(x, y, 1 - c), (x, y, c)).wait_recv()
            for j, chip in enumerate(chips):
                copy(v, 4 + j, idx(*chip, 1 - c), (x, y, c)).wait_recv()
        for cp in first + passed:
            cp.wait_send()
        for cp in mine:
            cp.wait()

    return pl.pallas_call(
        body, name="allgather_weights",
        in_specs=[_any_spec()] * n, out_specs=[_any_spec()] * n,
        out_shape=[jax.ShapeDtypeStruct((N_DEV,) + s.shape, s.dtype) for s in shards],
        scratch_shapes=[pltpu.SemaphoreType.DMA((len(parts), 7)), pltpu.SemaphoreType.DMA((len(parts), 7)),
                        pltpu.SemaphoreType.DMA((n,))],
    )(*shards)


def _peer(x, y, c, k):
    tx = 1 - x if (k >> 2) & 1 else x
    ty = 1 - y if (k >> 1) & 1 else y
    tc = 1 - c if k & 1 else c
    return (tx, ty, tc), 4 * tx + 2 * ty + tc


_PEER_ORDER = (2, 4, 6, 3, 5, 7, 1)


_HBM = pl.BlockSpec(memory_space=pltpu.HBM)
_SEM = pl.BlockSpec(memory_space=pltpu.SEMAPHORE)
_EFFECT = pltpu.SideEffectType.DATAFLOW_SIDE_EFFECTING


def _exchange_copies(srcs, lands, send_sems, recv_sems, gather, half):
    x, y, c, me = _my_pos()
    pick = lambda px, py: None if half is None else ((px == py) if half == 0 else (px != py))
    copies = []
    for k in _PEER_ORDER:
        tgt, tidx = _peer(x, y, c, k)
        for a in range(len(srcs)):
            copies.append((pltpu.make_async_remote_copy(
                src_ref=srcs[a] if gather else srcs[a].at[tidx], dst_ref=lands[a].at[me],
                send_sem=send_sems.at[7 * a + k - 1], recv_sem=recv_sems.at[7 * a + k - 1],
                device_id=tgt, device_id_type=MESH), pick(tgt[0], tgt[1])))
    return copies, pick(x, y)


def _when(cond, fn):
    if cond is None:
        fn()
    else:
        pl.when(cond)(fn)


def _exchange_start(name, srcs, lands=None, after=None, gather=None, half=None):
    n = len(srcs)
    gather = (lands is not None) if gather is None else gather
    if lands is None:
        lands = [lax.empty(g.shape, g.dtype) for g in srcs]
    extra = [] if after is None else [after]

    def body(*refs):
        src_refs, land_refs = refs[:n], refs[n:2 * n]
        send_sems, recv_sems = refs[2 * n + len(extra)], refs[2 * n + len(extra) + 1]
        token = refs[-1]
        for cp, sends in _exchange_copies(src_refs, land_refs, send_sems, recv_sems, gather, half)[0]:
            _when(sends, cp.start)
        token[...] = jnp.zeros_like(token)

    hbm = lambda a: pltpu.with_memory_space_constraint(a, pltpu.HBM)
    outs = pl.pallas_call(
        body, name=name,
        out_shape=(pltpu.SemaphoreType.DMA((7 * n,)), pltpu.SemaphoreType.DMA((7 * n,)),
                   *[pltpu.HBM(g.shape, g.dtype) for g in list(srcs) + list(lands)],
                   jax.ShapeDtypeStruct((8, 128), f32)),
        in_specs=[_HBM] * (2 * n) + [pl.BlockSpec(memory_space=pl.ANY)] * len(extra),
        out_specs=(_SEM, _SEM, *([_HBM] * (2 * n)), pl.BlockSpec(memory_space=pltpu.VMEM)),
        input_output_aliases={i: 2 + i for i in range(2 * n)},
        compiler_params=pltpu.CompilerParams(has_side_effects=_EFFECT),
    )(*[hbm(g) for g in srcs], *[hbm(g) for g in lands], *extra)
    return outs[0], outs[1], list(outs[2:2 + n]), list(outs[2 + n:2 + 2 * n]), outs[-1], gather, half


def _exchange_wait(name, started, after):
    send_sems, recv_sems, srcs, lands, _, gather, half = started
    n = len(srcs)
    after = list(after) if isinstance(after, (list, tuple)) else [after]

    def body(*refs):
        src_refs, land_refs = refs[:n], refs[n:2 * n]
        s_sems, r_sems = refs[2 * n], refs[2 * n + 1]
        copies, receives = _exchange_copies(src_refs, land_refs, s_sems, r_sems, gather, half)
        for cp, sends in copies:
            _when(sends, cp.wait_send)
            _when(receives, cp.wait_recv)

    outs = pl.pallas_call(
        body, name=name,
        out_shape=tuple(pltpu.HBM(a.shape, a.dtype) for a in list(srcs) + list(lands)),
        in_specs=[_HBM] * (2 * n) + [_SEM, _SEM] + [pl.BlockSpec(memory_space=pl.ANY)] * len(after),
        out_specs=tuple([_HBM] * (2 * n)),
        input_output_aliases={i: i for i in range(2 * n)},
        compiler_params=pltpu.CompilerParams(has_side_effects=_EFFECT),
    )(*srcs, *lands, send_sems, recv_sems, *after)
    return list(outs[:n]), list(outs[n:])


SMALL_ROWS = 128


def _small_start(name, sg, after=None):
    return _exchange_start(name, [sg], [lax.empty((N_DEV,) + sg.shape, f32)], after=after)


def _small_sum(name, me, started, after):
    (own,), (slots,) = _exchange_wait(name + "_wait", started, after)

    def body(me_ref, s_ref, own_ref, out_ref):
        acc = None
        for p in range(N_DEV):
            term = lax.cond(me_ref[0] == p, lambda: own_ref[...], lambda p=p: s_ref[p])
            acc = term if acc is None else acc + term
        out_ref[...] = acc

    return pl.pallas_call(
        body, name=name + "_sum",
        in_specs=[pl.BlockSpec(memory_space=pltpu.SMEM), pl.BlockSpec(memory_space=pltpu.VMEM),
                  pl.BlockSpec(memory_space=pltpu.VMEM)],
        out_specs=pl.BlockSpec(memory_space=pltpu.VMEM),
        out_shape=jax.ShapeDtypeStruct(own.shape, f32))(me, slots, own)


def _adam_math(g, w, m, v):
    m = ADAM_B1 * m + (1.0 - ADAM_B1) * g
    v = ADAM_B2 * v + (1.0 - ADAM_B2) * (g * g)
    m_hat = m / (1.0 - ADAM_B1 ** ADAM_STEP)
    v_hat = v / (1.0 - ADAM_B2 ** ADAM_STEP)
    delta = -ADAM_LR * (m_hat / (jnp.sqrt(v_hat) + ADAM_EPS) + ADAM_WD * w)
    return delta, m, v


def _adam_slots(name, me, slots, own, w, m, v, tr, transposed=False):
    rows = slots.shape[1]

    def body(me_ref, s_ref, own_ref, w_ref, m_ref, v_ref, g_ref, d_ref, nm_ref, nv_ref):
        mine = own_ref[...]
        g = None
        for p in range(N_DEV):
            term = lax.cond(me_ref[0] == p, lambda: mine, lambda p=p: s_ref[p]).astype(f32)
            g = term if g is None else g + term
        if transposed:
            g = g.T
        delta, nm, nv = _adam_math(g, w_ref[...], m_ref[...], v_ref[...])
        g_ref[...] = g
        d_ref[...] = delta
        nm_ref[...] = nm
        nv_ref[...] = nv

    mode = dict(pipeline_mode=pl.Buffered(1)) if rows == tr else {}
    if transposed:
        rs = pl.BlockSpec((D, tr), lambda i, me_ref: (0, i))
        rs_in = pl.BlockSpec((D, tr), lambda i, me_ref: (0, i), **mode)
    else:
        rs = pl.BlockSpec((tr, D), lambda i, me_ref: (i, 0))
        rs_in = pl.BlockSpec((tr, D), lambda i, me_ref: (i, 0), **mode)
    return pl.pallas_call(
        body, name=name,
        grid_spec=pltpu.PrefetchScalarGridSpec(
            num_scalar_prefetch=1, grid=(rows // tr,),
            in_specs=[pl.BlockSpec((N_DEV, tr, D), lambda i, me_ref: (0, i, 0), **mode),
                      pl.BlockSpec((None, tr, D), lambda i, me_ref: (me_ref[0], i, 0), **mode), rs_in, rs_in, rs_in],
            out_specs=[rs] * 4),
        out_shape=[jax.ShapeDtypeStruct(w.shape, f32)] * 4,
        compiler_params=_cparams(("parallel",)))(me, slots, own, w, m, v)


def _adam_small(g, w, m, v):
    def body(g_ref, w_ref, m_ref, v_ref, d_ref, nm_ref, nv_ref):
        delta, nm, nv = _adam_math(g_ref[...], w_ref[...], m_ref[...], v_ref[...])
        d_ref[...] = delta
        nm_ref[...] = nm
        nv_ref[...] = nv

    return pl.pallas_call(body, name="adam_small", out_shape=[jax.ShapeDtypeStruct(g.shape, f32)] * 3)(g, w, m, v)


FFN_PAD = 6 * D


_SMALL_PARTS = (("norm1_g", 1), ("gate_b", 2), ("conv_w", CONV_WIDTH), ("conv_b", 1), ("conv_norm_g", 1),
                ("q_norm_g", 1), ("k_norm_g", 1), ("norm2_g", 1), ("ffn_conv_w", 18), ("ffn_conv_b", 6), ("last", 1))


def _small_offsets():
    out, row = {}, 0
    for name, rows in _SMALL_PARTS:
        out[name] = row
        row += -(-rows // 8) * 8
    assert row == SMALL_ROWS
    return out


def _pack_small(norm1_g, gate_b, conv_w, conv_b, conv_norm_g, q_norm_g, k_norm_g, norm2_g, ffn_conv_w, ffn_conv_b,
                last_row=None):
    pad_h = lambda a: jnp.pad(a, ((0, 0), (0, D - HEAD_DIM)))
    pad_f = lambda a: jnp.pad(a, ((0, 0), (0, FFN_PAD - 2 * D_FF))).reshape(-1, D)
    parts = [norm1_g, gate_b.reshape(2, D), conv_w, conv_b, conv_norm_g, pad_h(q_norm_g), pad_h(k_norm_g), norm2_g,
             pad_f(ffn_conv_w), pad_f(ffn_conv_b), jnp.zeros((1, D), f32) if last_row is None else last_row]
    return jnp.concatenate([jnp.pad(p, ((0, -p.shape[0] % 8), (0, 0))) for p in parts], axis=0)


def _unpack_small(p):
    o = _small_offsets()
    rows = lambda name, n: p[o[name]:o[name] + n]
    ffn = lambda a: a.reshape(-1, FFN_PAD)[:, :2 * D_FF]
    return dict(
        norm1_g=rows("norm1_g", 1), gate_b=rows("gate_b", 2).reshape(1, 2 * D), conv_w=rows("conv_w", CONV_WIDTH),
        conv_b=rows("conv_b", 1), conv_norm_g=rows("conv_norm_g", 1), q_norm_g=rows("q_norm_g", 1)[:, :HEAD_DIM],
        k_norm_g=rows("k_norm_g", 1)[:, :HEAD_DIM], norm2_g=rows("norm2_g", 1),
        ffn_conv_w=ffn(rows("ffn_conv_w", 18)), ffn_conv_b=ffn(rows("ffn_conv_b", 6)))


_ADAM_TILE = {896: 128, 704: 704, 128: 128, 352: 176}


def kernel(x, norm1_g, w_in, gate_b, conv_w, conv_b, conv_norm_g, w_conv_out, q_norm_g, k_norm_g, w_attn_out, w_out, norm2_g, w_up, ffn_conv_w, ffn_conv_b, w_down, loss_target, m_norm1_g, m_w_in, m_gate_b, m_conv_w, m_conv_b, m_conv_norm_g, m_w_conv_out, m_q_norm_g, m_k_norm_g, m_w_attn_out, m_w_out, m_norm2_g, m_w_up, m_ffn_conv_w, m_ffn_conv_b, m_w_down, v_norm1_g, v_w_in, v_gate_b, v_conv_w, v_conv_b, v_conv_norm_g, v_w_conv_out, v_q_norm_g, v_k_norm_g, v_w_attn_out, v_w_out, v_norm2_g, v_w_up, v_ffn_conv_w, v_ffn_conv_b, v_w_down):
    BL, S, _ = x.shape
    T = BL * S
    me = 4 * lax.axis_index("x") + 2 * lax.axis_index("y") + lax.axis_index("c")
    xt = x.reshape(T, D)
    target = loss_target.reshape(T, D)

    big = dict(w_in=(w_in[0], m_w_in[0], v_w_in[0]), w_up=(w_up[0], m_w_up[0], v_w_up[0]),
               w_conv_out=(w_conv_out[0], m_w_conv_out[0], v_w_conv_out[0]),
               w_attn_out=(w_attn_out[0], m_w_attn_out[0], v_w_attn_out[0]),
               w_out=(w_out[0], m_w_out[0], v_w_out[0]), w_down=(w_down[0], m_w_down[0], v_w_down[0]))
    order = ["w_in", "w_conv_out", "w_attn_out", "w_out", "w_up", "w_down"]
    shards = [(big[n][0].T if n in ("w_in", "w_up") else big[n][0]).astype(bf16) for n in order]
    gathered = _allgather_rows(shards, 1)
    W = {"w_in": gathered[0].reshape(-1, D)}

    def place_cols(shard, full_cols):
        z = jnp.zeros((shard.shape[0], full_cols), f32)
        return lax.dynamic_update_slice(z, shard, (0, me * shard.shape[1]))

    zr = lambda a: jnp.zeros_like(a)
    conv_local = _pack_small(
        zr(norm1_g), zr(gate_b), place_cols(conv_w[0], D), zr(conv_b), zr(conv_norm_g), zr(q_norm_g), zr(k_norm_g),
        zr(norm2_g), place_cols(ffn_conv_w[0], 2 * D_FF), zr(ffn_conv_b))
    ga_conv = _small_start("gather_conv_start", conv_local, after=gathered[0])
    ga_proj = _exchange_start("gather_start_proj", shards[1:4], gathered[1:4], after=ga_conv[4])
    ga_ffn = _exchange_start("gather_start_ffn", shards[4:6], gathered[4:6], after=ga_proj[4])

    bd = (jnp.arange(128)[:, None] // HEAD_DIM == jnp.arange(128)[None, :] // HEAD_DIM).astype(bf16)
    bias = _attn_bias()
    qg = jnp.tile(q_norm_g, (1, N_HEADS))
    kg = jnp.tile(k_norm_g, (1, N_HEADS))

    z8, h, qn, kn = _in_proj_fwd(xt, norm1_g, W["w_in"], qg, kg, bd, ga_ffn[4])
    conv_all = _unpack_small(_small_sum("gather_conv", me.reshape(1), ga_conv, z8))
    conv_w_full, ffn_w_full = conv_all["conv_w"], conv_all["ffn_conv_w"]
    c = _conv_fwd(z8, conv_w_full, conv_b, S)
    o, ob, lse = _attn_fwd(qn, kn, z8, bias, S)
    for n, g in zip(order[1:4], _exchange_wait("gather_wait_proj", ga_proj, ob)[1]):
        W[n] = g.reshape(-1, D)
    s, ya, yb, mixed = _branches_fwd(c, ob, z8, conv_norm_g, gate_b, W["w_conv_out"], W["w_attn_out"])
    x1, h2 = _out_norm2_fwd(mixed, W["w_out"], xt, norm2_g)
    for n, g in zip(order[4:6], _exchange_wait("gather_wait_ffn", ga_ffn, x1)[1]):
        W[n] = g.reshape(-1, D)
    TNU = D_FF // 2
    u3 = _matmul_call(
        "mm_u", h2, W["w_up"],
        pl.BlockSpec((1024, D), lambda i, j, k: (i, 0)),
        pl.BlockSpec((TNU, D), lambda i, j, k: (j, 0)),
        pl.BlockSpec((None, 1024, TNU), lambda i, j, k: (j // 2, i, j % 2)),
        jax.ShapeDtypeStruct((2, T, D_FF), f32), (T // 1024, 4, 1), "nt", 1, 1024, TNU)
    f = _ffn_fwd(u3, ffn_w_full, ffn_conv_b, S)
    dy, dyb, lacc = _down_loss_fwd(f, W["w_down"], x1, target)
    loss_local = 0.5 / D * jnp.sum(lacc)

    df = _matmul("mm_df", dyb, W["w_down"], "nt", f32, tn=TNU)
    g_w_down = _matmul("mm_dwdn", f, dyb, "tn", bf16, tm=TNU)
    du3, dffn = _ffn_bwd(u3, df, ffn_w_full, ffn_conv_b, S)
    g_w_up = _matmul_call(
        "mm_dwup", du3, h2,
        pl.BlockSpec((None, T, TNU), lambda i, j, k: (i // 2, 0, i % 2)),
        pl.BlockSpec((T, D), lambda i, j, k: (0, 0)),
        pl.BlockSpec((TNU, D), lambda i, j, k: (i, 0)),
        jax.ShapeDtypeStruct((2 * D_FF, D), bf16), (4, 1, 1), "tn", 1, TNU, D)
    blocks8 = lambda a: a.reshape(N_DEV, -1, D)
    ex_ffn = _exchange_start("scatter_start_ffn", [blocks8(g_w_up), blocks8(g_w_down)])
    dx1, dx1b, dg_norm2 = _up_norm2_bwd(du3, W["w_up"], x1, dy, norm2_g, ex_ffn[4])
    g_w_out = _matmul("mm_dwo", mixed, dx1b, "tn", bf16, tm=512)
    dz8 = lax.empty((8, T, D), bf16)
    dya, dyb2, dz8, dg_gate = _out_gate_bwd(dx1b, W["w_out"], z8, gate_b, ya, yb, dz8)
    g_w_conv_out = _matmul("mm_dwco", s, dya, "tn", bf16, tm=512)
    g_w_attn_out = _matmul("mm_dwao", ob, dyb2, "tn", bf16, tm=512)
    ex_proj = _exchange_start("scatter_start_proj", [blocks8(g_w_conv_out), blocks8(g_w_attn_out), blocks8(g_w_out)])
    do = _matmul("mm_do", dyb2, W["w_attn_out"], "nt", f32, after=ex_proj[4])
    dc, dg_convnorm = _convnorm_bwd(dya, W["w_conv_out"], c, conv_norm_g)
    dz8a, dconv = _conv_bwd(dc, z8, conv_w_full, dz8, S)
    dwin_specs = lambda zsec, wsec: (
        pl.BlockSpec((None, T, D), lambda i, j, k: (zsec(i), 0, 0)), pl.BlockSpec((T, D), lambda i, j, k: (0, 0)),
        pl.BlockSpec((1024, D), lambda i, j, k: (wsec(i), 0)), jax.ShapeDtypeStruct((7 * D, D), bf16))
    g_w_in = _matmul_call("mm_dwin_a", dz8a, h, *dwin_specs(lambda i: i, lambda i: jnp.where(i < 2, i, i + 3)),
                          (4, 1, 1), "tn", 1, D, D)
    ex_in_a = _exchange_start("scatter_start_in_a", [blocks8(g_w_in)], half=0)
    dz8b, dg_q, dg_k = _attn_bwd(qn, kn, z8, do, o, lse, bias, bd, qg, kg, dz8a, S, ex_in_a[4])
    g_w_in = _matmul_call("mm_dwin_b", dz8b, h, *dwin_specs(lambda i: i + 4, lambda i: i + 2),
                          (3, 1, 1), "tn", 1, D, D, fill=ex_in_a[2][0].reshape(7 * D, D))
    ex_in_b = _exchange_start("scatter_start_in_b", [blocks8(g_w_in)], ex_in_a[3], gather=False, half=1)
    grad_x, dg_norm1 = _in_norm1_bwd(dz8b, W["w_in"], xt, dx1, norm1_g, ex_in_b[4])

    sum8 = lambda a: a.reshape(-1, 8, a.shape[-1]).sum(axis=1)
    dconv_s = sum8(dconv.sum(axis=0))
    dffn_s = dffn.sum(axis=0).reshape(2, 4, 8, D_FF).sum(axis=2)
    dffn_w = jnp.concatenate([dffn_s[0, :3], dffn_s[1, :3]], axis=1)
    dffn_b = jnp.concatenate([dffn_s[0, 3:4], dffn_s[1, 3:4]], axis=1)
    fold = lambda a: sum8(a).reshape(N_HEADS, HEAD_DIM).sum(axis=0)[None]
    small_g_local = _pack_small(
        sum8(dg_norm1), sum8(dg_gate), dconv_s[:CONV_WIDTH], dconv_s[CONV_WIDTH:], sum8(dg_convnorm),
        fold(dg_q), fold(dg_k), sum8(dg_norm2), dffn_w, dffn_b,
        last_row=jnp.pad(loss_local.reshape(1, 1), ((0, 0), (0, D - 1))))
    sg_start = _small_start("small_grads_start", small_g_local)

    own, slots = {}, {}
    for tag, ex, names_ in (("ffn", ex_ffn, ("w_up", "w_down")),
                            ("proj", ex_proj, ("w_conv_out", "w_attn_out", "w_out"))):
        sent, landed = _exchange_wait("scatter_wait_" + tag, ex, sg_start[4])
        for n, src, land in zip(names_, sent, landed):
            own[n], slots[n] = src, land
    sent, landed = _exchange_wait("scatter_wait_in_a", ex_in_a[:2] + (ex_in_b[2], ex_in_b[3]) + ex_in_a[4:],
                                  sg_start[4])
    sent, landed = _exchange_wait("scatter_wait_in_b", ex_in_b[:2] + (sent, landed) + ex_in_b[4:], sg_start[4])
    own["w_in"], slots["w_in"] = sent[0], landed[0]

    res, adam_done = {}, []
    for n in order:
        w, m, v = big[n]
        outs = _adam_slots("adam_" + n, me.reshape(1), slots[n], own[n], w, m, v, _ADAM_TILE[slots[n].shape[1]],
                           transposed=n in ("w_in", "w_up"))
        adam_done.append(outs[0])
        res[n] = [a[None] for a in outs]
    small_g = _small_sum("small_grads", me.reshape(1), sg_start, adam_done)
    loss = small_g[_small_offsets()["last"], 0]

    col = lambda a, width: lax.dynamic_slice(a, (0, me * width), (a.shape[0], width))
    small_w_true = _pack_small(norm1_g, gate_b, conv_w_full, conv_b, conv_norm_g, q_norm_g, k_norm_g, norm2_g,
                               ffn_w_full, ffn_conv_b)
    place_m = lambda a, full: place_cols(a[0], full)
    small_m = _pack_small(m_norm1_g, m_gate_b, place_m(m_conv_w, D), m_conv_b, m_conv_norm_g, m_q_norm_g, m_k_norm_g,
                          m_norm2_g, place_m(m_ffn_conv_w, 2 * D_FF), m_ffn_conv_b)
    small_v = _pack_small(v_norm1_g, v_gate_b, place_m(v_conv_w, D), v_conv_b, v_conv_norm_g, v_q_norm_g, v_k_norm_g,
                          v_norm2_g, place_m(v_ffn_conv_w, 2 * D_FF), v_ffn_conv_b)
    sd, sm, sv = _adam_small(small_g, small_w_true, small_m, small_v)
    for i, packed in enumerate((small_g, sd, sm, sv)):
        u = _unpack_small(packed)
        u["conv_w"] = col(u["conv_w"], D // N_DEV)
        u["ffn_conv_w"] = col(u["ffn_conv_w"], 2 * D_FF // N_DEV)
        for n, a in u.items():
            res.setdefault(n, [None] * 4)[i] = a[None] if n in ("conv_w", "ffn_conv_w") else a

    names = ["norm1_g", "w_in", "gate_b", "conv_w", "conv_b", "conv_norm_g", "w_conv_out", "q_norm_g", "k_norm_g",
             "w_attn_out", "w_out", "norm2_g", "w_up", "ffn_conv_w", "ffn_conv_b", "w_down"]
    out = [loss, grad_x.reshape(BL, S, D)]
    for i in range(4):
        out += [res[n][i] for n in names]
    return tuple(out)
```

```python
import functools

import jax
import jax.numpy as jnp
import numpy as np
from jax import lax
from jax.experimental import pallas as pl
from jax.experimental.pallas import tpu as pltpu

f32 = jnp.float32
bf16 = jnp.bfloat16

D = 1024
N_HEADS = 16
HEAD_DIM = 64
CONV_WIDTH = 31
D_FF = 2816
GROUPS = ((128, 1), (512, 4), (2048, 16))
ATTN_BLOCK = 128
EPS = 1e-6
N_DEV = 8
MESH = pl.DeviceIdType.MESH

ADAM_LR = 0.001
ADAM_B1 = 0.9
ADAM_B2 = 0.999
ADAM_EPS = 1e-08
ADAM_WD = 0.01
ADAM_STEP = 10

VMEM_LIMIT = 56 * 1024 * 1024
MASK_BIAS = 1e30

Z_AVAL, Z_AGATE, Z_GA, Z_GB, Z_Q, Z_K, Z_V = 0, 1, 2, 3, 4, 5, 6


_W_OF_Z = (0, 1, 5, 6, 2, 3, 4)


def _wsec_of_zsec(j):
    return jnp.where(j < 2, j, jnp.where(j < 4, j + 3, j - 2))


def _sig(x):
    return 1.0 / (1.0 + jnp.exp(-x))


def _colsum8(x):
    return x.reshape(-1, 8, x.shape[-1]).sum(axis=0)


def _cparams(sem):
    return pltpu.CompilerParams(dimension_semantics=sem, vmem_limit_bytes=VMEM_LIMIT)


def _my_pos():
    x, y, c = lax.axis_index("x"), lax.axis_index("y"), lax.axis_index("c")
    return x, y, c, 4 * x + 2 * y + c


_DIMS = {"nn": ((1,), (0,)), "nt": ((1,), (1,)), "tn": ((0,), (0,))}


def _matmul_call(name, a, b, a_spec, b_spec, o_spec, out_shape, grid, mode, nk, tm, tn, after=None, fill=None):
    dims = (_DIMS[mode], ((), ()))
    extra = ([] if after is None else [after]) + ([] if fill is None else [fill])

    def body(a_ref, b_ref, *rest):
        o_ref, scratch = rest[len(extra)], rest[len(extra) + 1:]
        part = lax.dot_general(a_ref[...], b_ref[...], dims, preferred_element_type=f32)
        if nk == 1:
            o_ref[...] = part.astype(o_ref.dtype)
        else:
            acc = scratch[0]
            k = pl.program_id(2)

            @pl.when(k == 0)
            def _():
                acc[...] = part

            @pl.when(k > 0)
            def _():
                acc[...] += part

            @pl.when(k == nk - 1)
            def _():
                o_ref[...] = acc[...].astype(o_ref.dtype)

    scratch = [] if nk == 1 else [pltpu.VMEM((tm, tn), f32)]
    return pl.pallas_call(
        body, name=name, grid=grid, in_specs=[a_spec, b_spec] + [pl.BlockSpec(memory_space=pl.ANY)] * len(extra),
        out_specs=o_spec, out_shape=out_shape, input_output_aliases={} if fill is None else {1 + len(extra): 0},
        scratch_shapes=scratch, compiler_params=_cparams(("parallel", "parallel", "arbitrary")),
    )(a, b, *extra)


def _matmul(name, a, b, mode, out_dtype, tm=1024, tn=1024, tk=None, after=None):
    if mode == "nn":
        (M, K), (_, N) = a.shape, b.shape
    elif mode == "nt":
        (M, K), (N, _) = a.shape, b.shape
    else:
        (K, M), (_, N) = a.shape, b.shape
    tm, tn = min(tm, M), min(tn, N)
    tk = K if tk is None else tk
    nk = K // tk
    assert M % tm == 0 and N % tn == 0 and K % tk == 0
    if mode == "tn":
        a_spec = pl.BlockSpec((tk, tm), lambda i, j, k: (k, i))
    else:
        a_spec = pl.BlockSpec((tm, tk), lambda i, j, k: (i, k))
    if mode == "nt":
        b_spec = pl.BlockSpec((tn, tk), lambda i, j, k: (j, k))
    else:
        b_spec = pl.BlockSpec((tk, tn), lambda i, j, k: (k, j))
    o_spec = pl.BlockSpec((tm, tn), lambda i, j, k: (i, j))
    return _matmul_call(name, a, b, a_spec, b_spec, o_spec, jax.ShapeDtypeStruct((M, N), out_dtype),
                        (M // tm, N // tn, nk), mode, nk, tm, tn, after=after)


FTM = 512


def _matmul_fused(name, a, b, pairs, epilogue, extras, consts, outs, nt=False, sums=False, passed=(), aliases=None):
    sa, M, kk = a.shape
    na = max(i for i, _ in pairs) + 1
    ne, nc, npass = len(extras), len(consts), len(passed)
    dims = (_DIMS["nt" if nt else "nn"], ((), ()))

    def body(a_ref, b_ref, *rest):
        acc = None
        for i, j in pairs:
            part = lax.dot_general(a_ref[i], b_ref[j], dims, preferred_element_type=f32)
            acc = part if acc is None else acc + part
        epilogue(acc, rest[:ne], rest[ne:ne + nc], rest[ne + nc + npass:])

    whole = lambda arr: pl.BlockSpec(arr.shape, lambda i, nd=arr.ndim: (0,) * nd, pipeline_mode=pl.Buffered(1))
    io_alias = {2 + ne + nc + k: v for k, v in (aliases or {}).items()}
    return pl.pallas_call(
        body, name=name, grid=(M // FTM,),
        in_specs=[pl.BlockSpec((na, FTM, kk), lambda i: (0, i, 0)), whole(b)] + [s for _, s in extras]
        + [whole(c) for c in consts] + [pl.BlockSpec(memory_space=pl.ANY)] * npass,
        out_specs=[s for _, s in outs], out_shape=[s for s, _ in outs], input_output_aliases=io_alias,
        compiler_params=_cparams(("arbitrary" if sums else "parallel",)),
    )(a, b, *[x for x, _ in extras], *consts, *passed)


def _frows(c=D):
    return pl.BlockSpec((FTM, c), lambda i: (i, 0))


def _fsec(s):
    return pl.BlockSpec((None, FTM, D), lambda i: (s, i, 0))


def _rowshape(T, dtype, c=D):
    return (jax.ShapeDtypeStruct((T, c), dtype), _frows(c))


def _sumshape(c=D):
    return (jax.ShapeDtypeStruct((8, c), f32), pl.BlockSpec((8, c), lambda i: (0, 0)))


def _add_colsum(ref, x, cols=None):
    @pl.when(pl.program_id(0) == 0)
    def _():
        if cols is None:
            ref[...] = jnp.zeros_like(ref)
        else:
            ref[:, cols] = jnp.zeros((8, x.shape[-1]), f32)

    if cols is None:
        ref[...] += _colsum8(x)
    else:
        ref[:, cols] += _colsum8(x)


def _rms(x):
    return lax.rsqrt(jnp.mean(x * x, axis=-1, keepdims=True) + EPS)


def _rms_bwd(dy_g, xn, rstd):
    return rstd * (dy_g - xn * jnp.mean(dy_g * xn, axis=-1, keepdims=True))


def _head_sum(x, bd):
    parts = []
    for cb in range(x.shape[-1] // 128):
        xb = x[:, cb * 128:(cb + 1) * 128]
        hi = xb.astype(bf16)
        lo = (xb - hi.astype(f32)).astype(bf16)
        parts.append(jnp.dot(hi, bd, preferred_element_type=f32) + jnp.dot(lo, bd, preferred_element_type=f32))
    return parts[0] if len(parts) == 1 else jnp.concatenate(parts, axis=1)


ZTM = 1024


def _in_proj_fwd(x, g, w_in_t, qg, kg, bd, after):
    T = x.shape[0]

    def body(x_ref, g_ref, w_ref, qg_ref, kg_ref, bd_ref, after_ref, z_ref, h_ref, qn_ref, kn_ref, hbuf):
        del after_ref
        j = pl.program_id(1)

        @pl.when(j == 0)
        def _():
            xv = x_ref[...]
            hv = (xv * _rms(xv) * g_ref[...]).astype(bf16)
            hbuf[...] = hv
            h_ref[...] = hv

        z = lax.dot_general(hbuf[...], w_ref[...], (_DIMS["nt"], ((), ())), preferred_element_type=f32)
        z_ref[...] = z

        def head_norm(gain_ref, scale):
            return z * lax.rsqrt(_head_sum(z * z, bd_ref[...]) * (1.0 / HEAD_DIM) + EPS) * gain_ref[...] * scale

        @pl.when(j == Z_Q)
        def _():
            qn_ref[...] = head_norm(qg_ref, HEAD_DIM ** -0.5)

        @pl.when(j == Z_K)
        def _():
            kn_ref[...] = head_norm(kg_ref, 1.0)

    tile = pl.BlockSpec((ZTM, D), lambda i, j: (i, 0))
    row = pl.BlockSpec((1, D), lambda i, j: (0, 0))
    return pl.pallas_call(
        body, name="mm_z", grid=(T // ZTM, 7),
        in_specs=[tile, row, pl.BlockSpec((D, D), lambda i, j: (_wsec_of_zsec(j), 0)), row, row,
                  pl.BlockSpec((128, 128), lambda i, j: (0, 0)), pl.BlockSpec(memory_space=pl.ANY)],
        out_specs=[pl.BlockSpec((None, ZTM, D), lambda i, j: (j, i, 0)), tile, tile, tile],
        out_shape=[jax.ShapeDtypeStruct((8, T, D), f32), jax.ShapeDtypeStruct((T, D), bf16),
                   jax.ShapeDtypeStruct((T, D), f32), jax.ShapeDtypeStruct((T, D), f32)],
        scratch_shapes=[pltpu.VMEM((ZTM, D), bf16)],
        compiler_params=_cparams(("parallel", "arbitrary")))(x, g, w_in_t, qg, kg, bd, after)


def _branches_fwd(c, ob, z8, g, gate_b, w_conv_out, w_attn_out):
    T = c.shape[0]

    def epilogue(yb, extra, const, out):
        cv = extra[0][...]
        r = cv * _rms(cv) * const[0][...]
        s = (r * _sig(r)).astype(bf16)
        ya = jnp.dot(s, const[2][...], preferred_element_type=f32)
        b_ref = const[1]
        g_a = _sig(extra[1][...] + b_ref[:, :D])
        g_b = _sig(extra[2][...] + b_ref[:, D:])
        out[0][...] = s
        out[1][...] = ya
        out[2][...] = yb
        out[3][...] = (g_a * ya + g_b * yb).astype(bf16)

    return _matmul_fused("mm_branches", ob[None], w_attn_out[None], ((0, 0),), epilogue,
                         [(c, _frows()), (z8, _fsec(Z_GA)), (z8, _fsec(Z_GB))], [g, gate_b, w_conv_out],
                         [_rowshape(T, bf16), _rowshape(T, f32), _rowshape(T, f32), _rowshape(T, bf16)])


def _out_norm2_fwd(mixed, w_out, x, g):
    T = x.shape[0]

    def epilogue(acc, extra, const, out):
        x1 = extra[0][...] + acc
        out[0][...] = x1
        out[1][...] = (x1 * _rms(x1) * const[0][...]).astype(bf16)

    return _matmul_fused("mm_t1_norm2", mixed[None], w_out[None], ((0, 0),), epilogue, [(x, _frows())], [g],
                         [_rowshape(T, f32), _rowshape(T, bf16)])


def _down_loss_fwd(f, w_down, x1, target):
    T = x1.shape[0]

    def epilogue(acc, extra, const, out):
        diff = extra[0][...] + acc - extra[1][...]
        dy = diff * (1.0 / D)
        out[0][...] = dy
        out[1][...] = dy.astype(bf16)
        _add_colsum(out[2], diff * diff)

    return _matmul_fused("mm_t2_loss", f[None], w_down[None], ((0, 0),), epilogue, [(x1, _frows()), (target, _frows())],
                         [], [_rowshape(T, f32), _rowshape(T, bf16), _sumshape()], sums=True)


def _up_norm2_bwd(du3, w_up_t, x1, dy, g, token):
    T = x1.shape[0]

    def epilogue(dh, extra, const, out):
        x1v = extra[0][...]
        rstd = _rms(x1v)
        xn = x1v * rstd
        dx1 = extra[1][...] + _rms_bwd(dh * const[0][...], xn, rstd)
        out[0][...] = dx1
        out[1][...] = dx1.astype(bf16)
        _add_colsum(out[2], dh * xn)

    return _matmul_fused("mm_dh2_norm2", du3, w_up_t.reshape(2, D_FF, D), ((0, 0), (1, 1)), epilogue,
                         [(x1, _frows()), (dy, _frows())], [g],
                         [_rowshape(T, f32), _rowshape(T, bf16), _sumshape()], sums=True, passed=[token])


def _out_gate_bwd(dx1b, w_out, z8, gate_b, ya, yb, dz8):
    T = ya.shape[0]

    def epilogue(dm, extra, const, out):
        b_ref = const[0]
        g_a = _sig(extra[0][...] + b_ref[:, :D])
        g_b = _sig(extra[1][...] + b_ref[:, D:])
        out[0][...] = (dm * g_a).astype(bf16)
        out[1][...] = (dm * g_b).astype(bf16)
        dla = dm * extra[2][...] * g_a * (1.0 - g_a)
        dlb = dm * extra[3][...] * g_b * (1.0 - g_b)
        out[2][0] = dla.astype(bf16)
        out[2][1] = dlb.astype(bf16)
        _add_colsum(out[3], dla, slice(0, D))
        _add_colsum(out[3], dlb, slice(D, 2 * D))

    return _matmul_fused(
        "mm_dmixed_gate", dx1b[None], w_out[None], ((0, 0),), epilogue,
        [(z8, _fsec(Z_GA)), (z8, _fsec(Z_GB)), (ya, _frows()), (yb, _frows())], [gate_b],
        [_rowshape(T, bf16), _rowshape(T, bf16),
         (jax.ShapeDtypeStruct(dz8.shape, bf16), pl.BlockSpec((2, FTM, D), lambda i: (1, i, 0))), _sumshape(2 * D)],
        nt=True, sums=True, passed=[dz8], aliases={0: 2})


def _convnorm_bwd(dya, w_conv_out, c, g):
    T = c.shape[0]

    def epilogue(ds, extra, const, out):
        cv = extra[0][...]
        rstd = _rms(cv)
        r0 = cv * rstd
        gv = const[0][...]
        r = r0 * gv
        sg = _sig(r)
        dr = ds * sg * (1.0 + r * (1.0 - sg))
        out[0][...] = _rms_bwd(dr * gv, r0, rstd)
        _add_colsum(out[1], dr * r0)

    return _matmul_fused("mm_ds_convnorm", dya[None], w_conv_out[None], ((0, 0),), epilogue, [(c, _frows())], [g],
                         [_rowshape(T, f32), _sumshape()], nt=True, sums=True)


def _in_norm1_bwd(dz8, w_in_t, x, dx1, g, token):
    T = x.shape[0]

    def epilogue(dh, extra, const, out):
        xv = extra[0][...]
        rstd = _rms(xv)
        xn = xv * rstd
        out[0][...] = extra[1][...] + _rms_bwd(dh * const[0][...], xn, rstd)
        _add_colsum(out[1], dh * xn)

    return _matmul_fused("mm_dh_norm1", dz8, w_in_t.reshape(7, D, D), tuple(zip(range(7), _W_OF_Z)), epilogue,
                         [(x, _frows()), (dx1, _frows())], [g], [_rowshape(T, f32), _sumshape()],
                         sums=True, passed=[token])


CCW = 256
CR = 64
HALO = 32


def _conv_fwd(z8, conv_w, conv_b, S):
    T = z8.shape[1]
    nb = T // S
    ncb = D // CCW

    def body(av_ref, ag_ref, w_ref, b_ref, c_ref, pad):
        pad[0:HALO, :] = jnp.zeros((HALO, CCW), f32)

        def fill(i, carry):
            r0 = pl.multiple_of(i * 256, 256)
            pad[pl.ds(HALO + r0, 256), :] = av_ref[pl.ds(r0, 256), :] * _sig(ag_ref[pl.ds(r0, 256), :])
            return carry

        lax.fori_loop(0, S // 256, fill, 0)
        bias = b_ref[...]

        def chunk(i, carry):
            r0 = pl.multiple_of(i * CR, CR)
            win = pad[pl.ds(r0, CR + HALO), :]
            acc = jnp.zeros((CR, CCW), f32) + bias
            for s in range(8):
                part = None
                for m in range((CONV_WIDTH - 1 - s) // 8 + 1):
                    j = CONV_WIDTH - 1 - 8 * m - s
                    term = win[24 - 8 * m:24 - 8 * m + CR + 8, :] * w_ref[j:j + 1, :]
                    part = term if part is None else part + term
                acc = acc + part[8 - s:8 - s + CR, :]
            c_ref[pl.ds(r0, CR), :] = acc
            return carry

        lax.fori_loop(0, S // CR, chunk, 0)

    zs = lambda s: pl.BlockSpec((None, S, CCW), lambda b, cb: (s, b, cb))
    return pl.pallas_call(
        body, name="conv_fwd", grid=(nb, ncb),
        in_specs=[zs(Z_AVAL), zs(Z_AGATE), pl.BlockSpec((CONV_WIDTH, CCW), lambda b, cb: (0, cb)),
                  pl.BlockSpec((1, CCW), lambda b, cb: (0, cb))],
        out_specs=pl.BlockSpec((S, CCW), lambda b, cb: (b, cb)),
        out_shape=jax.ShapeDtypeStruct((T, D), f32),
        scratch_shapes=[pltpu.VMEM((S + HALO, CCW), f32)],
        compiler_params=_cparams(("parallel", "parallel")))(z8, z8, conv_w, conv_b)


def _conv_bwd(dc, z8, conv_w, dz8, S):
    T = dc.shape[0]
    nb = T // S
    ncb = D // CCW

    def body(dc_ref, av_ref, ag_ref, w_ref, dz_in, dz_ref, dw_ref, apad, dpad, shbuf):
        del dz_in
        apad[0:HALO, :] = jnp.zeros((HALO, CCW), f32)
        dpad[S:S + HALO, :] = jnp.zeros((HALO, CCW), f32)
        dw_ref[...] = jnp.zeros_like(dw_ref)

        def fill(i, carry):
            r0 = pl.multiple_of(i * 256, 256)
            apad[pl.ds(HALO + r0, 256), :] = av_ref[pl.ds(r0, 256), :] * _sig(ag_ref[pl.ds(r0, 256), :])
            dpad[pl.ds(r0, 256), :] = dc_ref[pl.ds(r0, 256), :]
            return carry

        lax.fori_loop(0, S // 256, fill, 0)

        def chunk(i, carry):
            r0 = pl.multiple_of(i * CR, CR)
            dwin = dpad[pl.ds(r0, CR + HALO), :]
            da = jnp.zeros((CR, CCW), f32)
            for s in range(8):
                shbuf[...] = dwin[s:s + CR, :]
                dshift = shbuf[...]
                part = None
                for m in range((CONV_WIDTH - 1 - s) // 8 + 1):
                    j = CONV_WIDTH - 1 - 8 * m - s
                    term = dwin[8 * m:8 * m + CR + 8, :] * w_ref[j:j + 1, :]
                    part = term if part is None else part + term
                    a_lag = apad[pl.ds(r0 + HALO - 8 * m, CR), :]
                    dw_ref[8 * j:8 * j + 8, :] += _colsum8(dshift * a_lag)
                da = da + part[s:s + CR, :]
            dw_ref[8 * CONV_WIDTH:8 * CONV_WIDTH + 8, :] += _colsum8(dwin[0:CR, :])
            av = av_ref[pl.ds(r0, CR), :]
            sg = _sig(ag_ref[pl.ds(r0, CR), :])
            dz_ref[0, pl.ds(r0, CR), :] = (da * sg).astype(bf16)
            dz_ref[1, pl.ds(r0, CR), :] = (da * av * sg * (1.0 - sg)).astype(bf16)
            return carry

        lax.fori_loop(0, S // CR, chunk, 0)

    zs = lambda s: pl.BlockSpec((None, S, CCW), lambda b, cb: (s, b, cb))
    return pl.pallas_call(
        body, name="conv_bwd", grid=(nb, ncb),
        in_specs=[pl.BlockSpec((S, CCW), lambda b, cb: (b, cb)), zs(Z_AVAL), zs(Z_AGATE),
                  pl.BlockSpec((CONV_WIDTH, CCW), lambda b, cb: (0, cb)), pl.BlockSpec(memory_space=pl.ANY)],
        out_specs=[pl.BlockSpec((2, S, CCW), lambda b, cb: (0, b, cb)),
                   pl.BlockSpec((None, 256, CCW), lambda b, cb: (b, 0, cb))],
        out_shape=[jax.ShapeDtypeStruct(dz8.shape, bf16), jax.ShapeDtypeStruct((nb, 256, D), f32)],
        input_output_aliases={4: 0},
        scratch_shapes=[pltpu.VMEM((S + HALO, CCW), f32), pltpu.VMEM((S + HALO, CCW), f32),
                        pltpu.VMEM((CR, CCW), f32)],
        compiler_params=_cparams(("parallel", "parallel")))(dc, z8, z8, conv_w, dz8)


FR = 128
NFB = D_FF // CCW
FBW = 128


def _ffn_window(ref, i, r0):
    return ref[pl.ds(r0 - 8, FR + 8), :]


def _ffn_u(win, w_ref, b_ref):
    return (win[6:6 + FR, :] * w_ref[0:1, :] + win[7:7 + FR, :] * w_ref[1:2, :]
            + win[8:8 + FR, :] * w_ref[2:3, :] + b_ref[...])


def _ffn_fwd(u3, ffn_w, ffn_b, S):
    T = u3.shape[1]
    nb = T // S

    def body(uv_ref, ug_ref, wv_ref, wg_ref, bv_ref, bg_ref, f_ref):
        def chunk(first, i):
            r0 = 0 if first else pl.multiple_of(i * FR, FR)
            if first:
                z = jnp.zeros((8, CCW), f32)
                wv = jnp.concatenate([z, uv_ref[0:FR, :]], axis=0)
                wg = jnp.concatenate([z, ug_ref[0:FR, :]], axis=0)
            else:
                wv = _ffn_window(uv_ref, i, r0)
                wg = _ffn_window(ug_ref, i, r0)
            u_val = _ffn_u(wv, wv_ref, bv_ref)
            u_gate = _ffn_u(wg, wg_ref, bg_ref)
            f_ref[pl.ds(r0, FR), :] = (u_gate * _sig(u_gate) * u_val).astype(bf16)

        chunk(True, 0)

        def loop(i, carry):
            chunk(False, i)
            return carry

        lax.fori_loop(1, S // FR, loop, 0)

    us = lambda h: pl.BlockSpec((None, S, CCW), lambda b, cb: (h, b, cb))
    ws = lambda h: pl.BlockSpec((3, CCW), lambda b, cb: (0, h * NFB + cb))
    bs = lambda h: pl.BlockSpec((1, CCW), lambda b, cb: (0, h * NFB + cb))
    return pl.pallas_call(
        body, name="ffn_fwd", grid=(nb, NFB),
        in_specs=[us(0), us(1), ws(0), ws(1), bs(0), bs(1)],
        out_specs=pl.BlockSpec((S, CCW), lambda b, cb: (b, cb)),
        out_shape=jax.ShapeDtypeStruct((T, D_FF), bf16),
        compiler_params=_cparams(("parallel", "parallel")))(u3, u3, ffn_w, ffn_w, ffn_b, ffn_b)


def _ffn_bwd(u3, df, ffn_w, ffn_b, S):
    T = u3.shape[1]
    nb = T // S

    def body(uv_ref, ug_ref, df_ref, wv_ref, wg_ref, bv_ref, bg_ref, du_ref, dw_ref, dvpad, dgpad, shbuf):
        dvpad[S:S + 8, :] = jnp.zeros((8, FBW), f32)
        dgpad[S:S + 8, :] = jnp.zeros((8, FBW), f32)
        dw_ref[...] = jnp.zeros_like(dw_ref)

        def chunk(first, i):
            r0 = 0 if first else pl.multiple_of(i * FR, FR)
            if first:
                z = jnp.zeros((8, FBW), f32)
                wv = jnp.concatenate([z, uv_ref[0:FR, :]], axis=0)
                wg = jnp.concatenate([z, ug_ref[0:FR, :]], axis=0)
            else:
                wv = _ffn_window(uv_ref, i, r0)
                wg = _ffn_window(ug_ref, i, r0)
            taps = []
            for h, win in enumerate((wv, wg)):
                shbuf[2 * h] = win[6:6 + FR, :]
                shbuf[2 * h + 1] = win[7:7 + FR, :]
                taps.append((shbuf[2 * h], shbuf[2 * h + 1], win[8:8 + FR, :]))
            conv = lambda x, w_ref, b_ref: (x[0] * w_ref[0:1, :] + x[1] * w_ref[1:2, :] + x[2] * w_ref[2:3, :]
                                            + b_ref[...])
            u_val = conv(taps[0], wv_ref, bv_ref)
            u_gate = conv(taps[1], wg_ref, bg_ref)
            dfc = df_ref[pl.ds(r0, FR), :]
            sg = _sig(u_gate)
            d_val = dfc * u_gate * sg
            d_gate = dfc * u_val * sg * (1.0 + u_gate * (1.0 - sg))
            dvpad[pl.ds(r0, FR), :] = d_val
            dgpad[pl.ds(r0, FR), :] = d_gate
            for h, dd in enumerate((d_val, d_gate)):
                for j in range(3):
                    dw_ref[h, 8 * j:8 * j + 8, :] += _colsum8(dd * taps[h][j])
                dw_ref[h, 24:32, :] += _colsum8(dd)

        chunk(True, 0)

        def loop(i, carry):
            chunk(False, i)
            return carry

        lax.fori_loop(1, S // FR, loop, 0)

        def back(i, carry):
            r0 = pl.multiple_of(i * FR, FR)
            for h, (dpad, w_ref) in enumerate(((dvpad, wv_ref), (dgpad, wg_ref))):
                win = dpad[pl.ds(r0, FR + 8), :]
                du = (win[0:FR, :] * w_ref[2:3, :] + win[1:1 + FR, :] * w_ref[1:2, :]
                      + win[2:2 + FR, :] * w_ref[0:1, :])
                du_ref[h, pl.ds(r0, FR), :] = du.astype(bf16)
            return carry

        lax.fori_loop(0, S // FR, back, 0)

    ncb = D_FF // FBW
    us = lambda h: pl.BlockSpec((None, S, FBW), lambda b, cb: (h, b, cb))
    ws = lambda h: pl.BlockSpec((3, FBW), lambda b, cb: (0, h * ncb + cb))
    bs = lambda h: pl.BlockSpec((1, FBW), lambda b, cb: (0, h * ncb + cb))
    return pl.pallas_call(
        body, name="ffn_bwd", grid=(nb, ncb),
        in_specs=[us(0), us(1), pl.BlockSpec((S, FBW), lambda b, cb: (b, cb)), ws(0), ws(1), bs(0), bs(1)],
        out_specs=[pl.BlockSpec((2, S, FBW), lambda b, cb: (0, b, cb)),
                   pl.BlockSpec((None, 2, 32, FBW), lambda b, cb: (b, 0, 0, cb))],
        out_shape=[jax.ShapeDtypeStruct((2, T, D_FF), bf16), jax.ShapeDtypeStruct((nb, 2, 32, D_FF), f32)],
        scratch_shapes=[pltpu.VMEM((S + 8, FBW), f32), pltpu.VMEM((S + 8, FBW), f32),
                        pltpu.VMEM((4, FR, FBW), f32)],
        compiler_params=_cparams(("parallel", "parallel")))(u3, u3, df, ffn_w, ffn_w, ffn_b, ffn_b)


AB = ATTN_BLOCK


def _attn_bias_np():
    slopes = (np.float32(2.0) ** (np.float32(-8.0) * np.arange(1, N_HEADS + 1, dtype=np.float32)
                                  / np.float32(N_HEADS))).astype(np.float32)
    steps = (np.arange(AB)[:, None] + AB) - np.arange(2 * AB)[None, :]
    own = (np.arange(2 * AB) >= AB)[None, :]
    out = []
    for window, dil in GROUPS:
        valid = (steps >= 0) & (steps <= window // dil)
        dist = slopes[:, None, None] * (steps * dil).astype(np.float32)[None]
        kinds = [np.where(v[None], dist, np.float32(MASK_BIAS)) for v in (valid, valid & own)]
        out.append(np.stack(kinds, axis=1))
    return np.stack(out).astype(np.float32)


def _attn_bias():
    return jnp.asarray(_attn_bias_np())


def _head_masks():
    lane = lax.broadcasted_iota(jnp.int32, (1, 128), 1)
    return (lane < HEAD_DIM, lane >= HEAD_DIM)


def _perm_chunks(S, d):
    L = S // d
    ch = min(L, 256)
    out = []
    for r in range(d):
        for c in range(L // ch):
            start = r + d * ch * c
            out.append((pl.ds(start, ch, stride=d) if d > 1 else pl.ds(start, ch), r * L + c * ch, ch))
    return out


def _stack_heads(x, masks):
    return jnp.concatenate([jnp.where(masks[0], x, 0), jnp.where(masks[1], x, 0)], axis=0)


def _block_row(j):
    return j * AB if isinstance(j, int) else pl.multiple_of(j * AB, AB)


def _three_stages(n, stage_a, stage_b, stage_c, unroll):
    stage_a(0)
    stage_a(1)
    stage_b(0)

    def body(j, carry):
        stage_c(j - 1)
        stage_b(j)
        stage_a(j + 1)
        return carry

    lax.fori_loop(1, n - 1, body, 0, unroll=unroll)
    stage_c(n - 2)
    stage_b(n - 1)
    stage_c(n - 1)


_NT = (((1,), (1,)), ((), ()))
_TN = (((0,), (0,)), ((), ()))
SCH = 64


def _attn_fwd(qn, kn, z8, bias, S):
    T = qn.shape[0]
    nb = T // S
    nblk = S // AB

    def body(q_ref, k_ref, v_ref, bias_ref, o_ref, ob_ref, lse_ref, qs, ks, vs, s2, p2, ogp, lgp, *group_scratch):
        og, lg = group_scratch[:3], group_scratch[3:]
        masks = _head_masks()
        ks[0:AB, :] = jnp.zeros((AB, 128), bf16)
        vs[0:AB, :] = jnp.zeros((AB, 128), bf16)

        for g, (_, d) in enumerate(GROUPS):
            nsub = S // (d * AB)
            chunks = _perm_chunks(S, d)
            for src, dst, ch in chunks:
                qs[dst:dst + ch, :] = q_ref[src, :].astype(bf16)
                ks[AB + dst:AB + dst + ch, :] = k_ref[src, :].astype(bf16)
                vs[AB + dst:AB + dst + ch, :] = v_ref[src, :].astype(bf16)
            od, ld = (og[g], lg[g]) if d == 1 else (ogp, lgp)

            def scores(j):
                r0 = _block_row(j)
                q2 = _stack_heads(qs[pl.ds(r0, AB), :], masks)
                s2[j] = lax.dot_general(q2, ks[pl.ds(r0, 2 * AB), :], _NT, preferred_element_type=f32)

            def softmax(j, g=g, nsub=nsub, ld=ld):
                r0 = _block_row(j)
                kind = int(j % nsub == 0) if isinstance(j, int) else (j % nsub == 0).astype(jnp.int32)
                for cc in range(AB // SCH):
                    lses = []
                    for hh in range(2):
                        rows = pl.ds(hh * AB + cc * SCH, SCH)
                        sb = s2[j, rows, :] - bias_ref[g, hh, kind, cc * SCH:(cc + 1) * SCH, :]
                        m = jnp.max(sb, axis=-1, keepdims=True)
                        p = jnp.exp(sb - m)
                        den = jnp.sum(p, axis=-1, keepdims=True)
                        p2[j, rows, :] = (p * (1.0 / den)).astype(bf16)
                        lses.append(m + jnp.log(den))
                    ld[pl.ds(r0 + cc * SCH, SCH), :] = jnp.where(masks[0], lses[0], lses[1])

            def values(j, od=od):
                r0 = _block_row(j)
                pv2 = jnp.dot(p2[j], vs[pl.ds(r0, 2 * AB), :], preferred_element_type=f32)
                od[pl.ds(r0, AB), :] = jnp.where(masks[0], pv2[:AB], pv2[AB:])

            _three_stages(nblk, scores, softmax, values, nblk - 2)

            if d > 1:
                for src, dst, ch in chunks:
                    og[g][src, :] = ogp[dst:dst + ch, :]
                    lg[g][src, :] = lgp[dst:dst + ch, :]

        def combine(i, carry):
            rr = pl.ds(pl.multiple_of(i * 256, 256), 256)
            l0, l1, l2 = lg[0][rr, :], lg[1][rr, :], lg[2][rr, :]
            mx = jnp.maximum(jnp.maximum(l0, l1), l2)
            e0, e1, e2 = jnp.exp(l0 - mx), jnp.exp(l1 - mx), jnp.exp(l2 - mx)
            den = e0 + e1 + e2
            o = (e0 * og[0][rr, :] + e1 * og[1][rr, :] + e2 * og[2][rr, :]) / den
            o_ref[rr, :] = o
            ob_ref[rr, :] = o.astype(bf16)
            lse_ref[rr, :] = mx + jnp.log(den)
            return carry

        lax.fori_loop(0, S // 256, combine, 0, unroll=True)

    blk = pl.BlockSpec((S, 128), lambda b, hp: (b, hp))
    return pl.pallas_call(
        body, name="attn_fwd", grid=(nb, N_HEADS // 2),
        in_specs=[blk, blk, pl.BlockSpec((None, S, 128), lambda b, hp: (Z_V, b, hp)),
                  pl.BlockSpec((3, 2, 2, AB, 2 * AB), lambda b, hp: (0, hp, 0, 0, 0))],
        out_specs=[blk, blk, blk],
        out_shape=[jax.ShapeDtypeStruct((T, D), f32), jax.ShapeDtypeStruct((T, D), bf16),
                   jax.ShapeDtypeStruct((T, D), f32)],
        scratch_shapes=[pltpu.VMEM((S, 128), bf16), pltpu.VMEM((S + AB, 128), bf16), pltpu.VMEM((S + AB, 128), bf16),
                        pltpu.VMEM((nblk, 2 * AB, 2 * AB), f32), pltpu.VMEM((nblk, 2 * AB, 2 * AB), bf16),
                        pltpu.VMEM((S, 128), f32), pltpu.VMEM((S, 128), f32)] + [pltpu.VMEM((S, 128), f32)] * 6,
        compiler_params=_cparams(("parallel", "parallel")))(qn, kn, z8, bias)


def _attn_bwd(qn, kn, z8, do, o, lse, bias, bd, qg, kg, dz8, S, after):
    T = qn.shape[0]
    nb = T // S

    nblk = S // AB

    def body(q_ref, k_ref, v_ref, do_ref, o_ref, lse_ref, bias_ref, bd_ref, qraw_ref, kraw_ref, qg_ref, kg_ref,
             dz_in, after_ref, dz_ref, dqg_ref, dkg_ref,
             dq_ref, dk_ref, dv_ref, delta, qs, ks, vs, dos, lsp, dlp, s2, dp2, p2, ds2, dqp, dkp, dvp):
        del dz_in, after_ref
        masks = _head_masks()
        bdv = bd_ref[...]
        dq_ref[...] = jnp.zeros_like(dq_ref)
        dk_ref[...] = jnp.zeros_like(dk_ref)
        dv_ref[...] = jnp.zeros_like(dv_ref)
        ks[0:AB, :] = jnp.zeros((AB, 128), bf16)
        vs[0:AB, :] = jnp.zeros((AB, 128), bf16)

        def prep(i, carry):
            rr = pl.ds(pl.multiple_of(i * 256, 256), 256)
            delta[rr, :] = _head_sum(do_ref[rr, :] * o_ref[rr, :], bdv)
            return carry

        lax.fori_loop(0, S // 256, prep, 0, unroll=True)

        for g, (_, d) in enumerate(GROUPS):
            nsub = S // (d * AB)
            chunks = _perm_chunks(S, d)
            for src, dst, ch in chunks:
                qs[dst:dst + ch, :] = q_ref[src, :].astype(bf16)
                ks[AB + dst:AB + dst + ch, :] = k_ref[src, :].astype(bf16)
                vs[AB + dst:AB + dst + ch, :] = v_ref[src, :].astype(bf16)
                dos[dst:dst + ch, :] = do_ref[src, :].astype(bf16)
                lsp[dst:dst + ch, :] = lse_ref[src, :]
                dlp[dst:dst + ch, :] = delta[src, :]
            dkp[...] = jnp.zeros_like(dkp)
            dvp[...] = jnp.zeros_like(dvp)

            def scores(j):
                r0 = _block_row(j)
                q2 = _stack_heads(qs[pl.ds(r0, AB), :], masks)
                do2 = _stack_heads(dos[pl.ds(r0, AB), :], masks)
                s2[j] = lax.dot_general(q2, ks[pl.ds(r0, 2 * AB), :], _NT, preferred_element_type=f32)
                dp2[j] = lax.dot_general(do2, vs[pl.ds(r0, 2 * AB), :], _NT, preferred_element_type=f32)

            def probs(j, g=g, nsub=nsub):
                r0 = _block_row(j)
                kind = int(j % nsub == 0) if isinstance(j, int) else (j % nsub == 0).astype(jnp.int32)
                for cc in range(AB // SCH):
                    lse_c = lsp[pl.ds(r0 + cc * SCH, SCH), :]
                    del_c = dlp[pl.ds(r0 + cc * SCH, SCH), :]
                    for hh in range(2):
                        c0 = hh * HEAD_DIM
                        rows = pl.ds(hh * AB + cc * SCH, SCH)
                        sb = s2[j, rows, :] - bias_ref[g, hh, kind, cc * SCH:(cc + 1) * SCH, :]
                        p = jnp.exp(sb - lse_c[:, c0:c0 + 1])
                        p2[j, rows, :] = p.astype(bf16)
                        ds2[j, rows, :] = (p * (dp2[j, rows, :] - del_c[:, c0:c0 + 1])).astype(bf16)

            def grads(j):
                r0 = _block_row(j)
                q2 = _stack_heads(qs[pl.ds(r0, AB), :], masks)
                do2 = _stack_heads(dos[pl.ds(r0, AB), :], masks)
                dsb = ds2[j]
                t = jnp.dot(dsb, ks[pl.ds(r0, 2 * AB), :], preferred_element_type=f32)
                dqp[pl.ds(r0, AB), :] = jnp.where(masks[0], t[:AB], t[AB:])
                dkp[pl.ds(r0, 2 * AB), :] += lax.dot_general(dsb, q2, _TN, preferred_element_type=f32)
                dvp[pl.ds(r0, 2 * AB), :] += lax.dot_general(p2[j], do2, _TN, preferred_element_type=f32)

            _three_stages(nblk, scores, probs, grads, nblk - 2)

            for src, dst, ch in chunks:
                dq_ref[src, :] += dqp[dst:dst + ch, :]
                dk_ref[src, :] += dkp[AB + dst:AB + dst + ch, :]
                dv_ref[src, :] += dvp[AB + dst:AB + dst + ch, :]

        @pl.when(pl.program_id(1) == 0)
        def _():
            dqg_ref[...] = jnp.zeros_like(dqg_ref)
            dkg_ref[...] = jnp.zeros_like(dkg_ref)

        def norms(i, carry):
            rr = pl.ds(pl.multiple_of(i * 256, 256), 256)

            def one(raw, dn_scaled, g, dg_ref, sec):
                rstd = lax.rsqrt(_head_sum(raw * raw, bdv) * (1.0 / HEAD_DIM) + EPS)
                n = raw * rstd
                dg_ref[...] += _colsum8(dn_scaled * n)
                dn = dn_scaled * g
                draw = rstd * (dn - n * (_head_sum(dn * n, bdv) * (1.0 / HEAD_DIM)))
                dz_ref[sec, rr, :] = draw.astype(bf16)

            one(qraw_ref[rr, :], dq_ref[rr, :] * (HEAD_DIM ** -0.5), qg_ref[...], dqg_ref, 0)
            one(kraw_ref[rr, :], dk_ref[rr, :], kg_ref[...], dkg_ref, 1)
            dz_ref[2, rr, :] = dv_ref[rr, :].astype(bf16)
            dz_ref[3, rr, :] = jnp.zeros((256, 128), bf16)
            return carry

        lax.fori_loop(0, S // 256, norms, 0, unroll=True)

    blk = pl.BlockSpec((S, 128), lambda hp, b: (b, hp))
    sec = lambda s: pl.BlockSpec((None, S, 128), lambda hp, b: (s, b, hp))
    gain = pl.BlockSpec((1, 128), lambda hp, b: (0, hp))
    row = lambda dt, pad=0: pltpu.VMEM((S + pad, 128), dt)
    blocks = lambda dt: pltpu.VMEM((nblk, 2 * AB, 2 * AB), dt)
    return pl.pallas_call(
        body, name="attn_bwd", grid=(N_HEADS // 2, nb),
        in_specs=[blk, blk, sec(Z_V), blk, blk, blk,
                  pl.BlockSpec((3, 2, 2, AB, 2 * AB), lambda hp, b: (0, hp, 0, 0, 0)),
                  pl.BlockSpec((128, 128), lambda hp, b: (0, 0)), sec(Z_Q), sec(Z_K), gain, gain,
                  pl.BlockSpec(memory_space=pl.ANY), pl.BlockSpec(memory_space=pl.ANY)],
        out_specs=[pl.BlockSpec((4, S, 128), lambda hp, b: (1, b, hp)),
                   pl.BlockSpec((8, 128), lambda hp, b: (0, hp)), pl.BlockSpec((8, 128), lambda hp, b: (0, hp))],
        out_shape=[jax.ShapeDtypeStruct(dz8.shape, bf16), jax.ShapeDtypeStruct((8, D), f32),
                   jax.ShapeDtypeStruct((8, D), f32)],
        input_output_aliases={12: 0},
        scratch_shapes=[row(f32), row(f32), row(f32),
                        row(f32), row(bf16), row(bf16, AB), row(bf16, AB), row(bf16), row(f32), row(f32),
                        blocks(f32), blocks(f32), blocks(bf16), blocks(bf16), row(f32), row(f32, AB), row(f32, AB)],
        compiler_params=_cparams(("parallel", "arbitrary")))(qn, kn, z8, do, o, lse, bias, bd, z8, z8, qg, kg, dz8, after)


def _any_spec():
    return pl.BlockSpec(memory_space=pl.ANY)


def _allgather_rows(shards, n_full):
    n = len(shards)

    def body(*refs):
        ins, outs = refs[:n], refs[n:2 * n]
        send_sems, recv_sems, local_sems = refs[2 * n:]
        x, y, c, me = _my_pos()
        sibling = (x, y, 1 - c)
        chips = [(1 - x, y), (x, 1 - y), (1 - x, 1 - y)]

        def idx(px, py, pc):
            return 4 * px + 2 * py + pc

        def copy(a, k, blk, to, src=None):
            return pltpu.make_async_remote_copy(
                src_ref=outs[a].at[blk] if src is None else src, dst_ref=outs[a].at[blk],
                send_sem=send_sems.at[a, k], recv_sem=recv_sems.at[a, k], device_id=to, device_id_type=MESH)

        mine = [pltpu.make_async_copy(ins[a], outs[a].at[me], local_sems.at[a]) for a in range(n)]
        for cp in mine:
            cp.start()
        first = []
        for a in range(n_full):
            first.append(copy(a, 0, me, sibling, src=ins[a]))
            first += [copy(a, 1 + j, me, (*chip, c), src=ins[a]) for j, chip in enumerate(chips)]
        for cp in first:
            cp.start()
        passed = []
        for a in range(n_full):
            for j, chip in enumerate(chips):
                blk = idx(*chip, c)
                copy(a, 1 + j, blk, (x, y, c)).wait_recv()
                cp = copy(a, 4 + j, blk, sibling)
                cp.start()
                passed.append(cp)
        for a in range(n_full):
            copy(a, 0, idx(x, y, 1 - c), (x, y, c)).wait_recv()
            for j, chip in enumerate(chips):
                copy(a, 4 + j, idx(*chip, 1 - c), (x, y, c)).wait_recv()
        for cp in first + passed:
            cp.wait_send()
        for cp in mine:
            cp.wait()

    return pl.pallas_call(
        body, name="allgather_weights",
        in_specs=[_any_spec()] * n, out_specs=[_any_spec()] * n,
        out_shape=[jax.ShapeDtypeStruct((N_DEV,) + s.shape, s.dtype) for s in shards],
        scratch_shapes=[pltpu.SemaphoreType.DMA((n_full, 7)), pltpu.SemaphoreType.DMA((n_full, 7)),
                        pltpu.SemaphoreType.DMA((n,))],
    )(*shards)


def _peer(x, y, c, k):
    tx = 1 - x if (k >> 2) & 1 else x
    ty = 1 - y if (k >> 1) & 1 else y
    tc = 1 - c if k & 1 else c
    return (tx, ty, tc), 4 * tx + 2 * ty + tc


_PEER_ORDER = (2, 4, 6, 3, 5, 7, 1)


_HBM = pl.BlockSpec(memory_space=pltpu.HBM)
_SEM = pl.BlockSpec(memory_space=pltpu.SEMAPHORE)
_EFFECT = pltpu.SideEffectType.DATAFLOW_SIDE_EFFECTING


def _exchange_copies(srcs, lands, send_sems, recv_sems, gather, half):
    x, y, c, me = _my_pos()
    pick = lambda px, py: None if half is None else ((px == py) if half == 0 else (px != py))
    copies = []
    for k in _PEER_ORDER:
        tgt, tidx = _peer(x, y, c, k)
        for a in range(len(srcs)):
            copies.append((pltpu.make_async_remote_copy(
                src_ref=srcs[a] if gather else srcs[a].at[tidx], dst_ref=lands[a].at[me],
                send_sem=send_sems.at[7 * a + k - 1], recv_sem=recv_sems.at[7 * a + k - 1],
                device_id=tgt, device_id_type=MESH), pick(tgt[0], tgt[1])))
    return copies, pick(x, y)


def _when(cond, fn):
    if cond is None:
        fn()
    else:
        pl.when(cond)(fn)


def _exchange_start(name, srcs, lands=None, after=None, gather=None, half=None):
    n = len(srcs)
    gather = (lands is not None) if gather is None else gather
    if lands is None:
        lands = [lax.empty(g.shape, g.dtype) for g in srcs]
    extra = [] if after is None else [after]

    def body(*refs):
        src_refs, land_refs = refs[:n], refs[n:2 * n]
        send_sems, recv_sems = refs[2 * n + len(extra)], refs[2 * n + len(extra) + 1]
        token = refs[-1]
        for cp, sends in _exchange_copies(src_refs, land_refs, send_sems, recv_sems, gather, half)[0]:
            _when(sends, cp.start)
        token[...] = jnp.zeros_like(token)

    hbm = lambda a: pltpu.with_memory_space_constraint(a, pltpu.HBM)
    outs = pl.pallas_call(
        body, name=name,
        out_shape=(pltpu.SemaphoreType.DMA((7 * n,)), pltpu.SemaphoreType.DMA((7 * n,)),
                   *[pltpu.HBM(g.shape, g.dtype) for g in list(srcs) + list(lands)],
                   jax.ShapeDtypeStruct((8, 128), f32)),
        in_specs=[_HBM] * (2 * n) + [pl.BlockSpec(memory_space=pl.ANY)] * len(extra),
        out_specs=(_SEM, _SEM, *([_HBM] * (2 * n)), pl.BlockSpec(memory_space=pltpu.VMEM)),
        input_output_aliases={i: 2 + i for i in range(2 * n)},
        compiler_params=pltpu.CompilerParams(has_side_effects=_EFFECT),
    )(*[hbm(g) for g in srcs], *[hbm(g) for g in lands], *extra)
    return outs[0], outs[1], list(outs[2:2 + n]), list(outs[2 + n:2 + 2 * n]), outs[-1], gather, half


def _exchange_wait(name, started, after):
    send_sems, recv_sems, srcs, lands, _, gather, half = started
    n = len(srcs)
    after = list(after) if isinstance(after, (list, tuple)) else [after]

    def body(*refs):
        src_refs, land_refs = refs[:n], refs[n:2 * n]
        s_sems, r_sems = refs[2 * n], refs[2 * n + 1]
        copies, receives = _exchange_copies(src_refs, land_refs, s_sems, r_sems, gather, half)
        for cp, sends in copies:
            _when(sends, cp.wait_send)
            _when(receives, cp.wait_recv)

    outs = pl.pallas_call(
        body, name=name,
        out_shape=tuple(pltpu.HBM(a.shape, a.dtype) for a in list(srcs) + list(lands)),
        in_specs=[_HBM] * (2 * n) + [_SEM, _SEM] + [pl.BlockSpec(memory_space=pl.ANY)] * len(after),
        out_specs=tuple([_HBM] * (2 * n)),
        input_output_aliases={i: i for i in range(2 * n)},
        compiler_params=pltpu.CompilerParams(has_side_effects=_EFFECT),
    )(*srcs, *lands, send_sems, recv_sems, *after)
    return list(outs[:n]), list(outs[n:])


SMALL_ROWS = 128


def _small_start(name, sg, after=None):
    return _exchange_start(name, [sg], [lax.empty((N_DEV,) + sg.shape, f32)], after=after)


def _small_sum(name, me, started, after):
    (own,), (slots,) = _exchange_wait(name + "_wait", started, after)

    def body(me_ref, s_ref, own_ref, out_ref):
        acc = None
        for p in range(N_DEV):
            term = lax.cond(me_ref[0] == p, lambda: own_ref[...], lambda p=p: s_ref[p])
            acc = term if acc is None else acc + term
        out_ref[...] = acc

    return pl.pallas_call(
        body, name=name + "_sum",
        in_specs=[pl.BlockSpec(memory_space=pltpu.SMEM), pl.BlockSpec(memory_space=pltpu.VMEM),
                  pl.BlockSpec(memory_space=pltpu.VMEM)],
        out_specs=pl.BlockSpec(memory_space=pltpu.VMEM),
        out_shape=jax.ShapeDtypeStruct(own.shape, f32))(me, slots, own)


def _adam_math(g, w, m, v):
    m = ADAM_B1 * m + (1.0 - ADAM_B1) * g
    v = ADAM_B2 * v + (1.0 - ADAM_B2) * (g * g)
    m_hat = m / (1.0 - ADAM_B1 ** ADAM_STEP)
    v_hat = v / (1.0 - ADAM_B2 ** ADAM_STEP)
    delta = -ADAM_LR * (m_hat / (jnp.sqrt(v_hat) + ADAM_EPS) + ADAM_WD * w)
    return delta, m, v


def _adam_slots(name, me, slots, own, w, m, v, tr, transposed=False):
    rows = slots.shape[1]

    def body(me_ref, s_ref, own_ref, w_ref, m_ref, v_ref, g_ref, d_ref, nm_ref, nv_ref):
        mine = own_ref[...]
        g = None
        for p in range(N_DEV):
            term = lax.cond(me_ref[0] == p, lambda: mine, lambda p=p: s_ref[p]).astype(f32)
            g = term if g is None else g + term
        if transposed:
            g = g.T
        delta, nm, nv = _adam_math(g, w_ref[...], m_ref[...], v_ref[...])
        g_ref[...] = g
        d_ref[...] = delta
        nm_ref[...] = nm
        nv_ref[...] = nv

    mode = dict(pipeline_mode=pl.Buffered(1)) if rows == tr else {}
    if transposed:
        rs = pl.BlockSpec((D, tr), lambda i, me_ref: (0, i))
        rs_in = pl.BlockSpec((D, tr), lambda i, me_ref: (0, i), **mode)
    else:
        rs = pl.BlockSpec((tr, D), lambda i, me_ref: (i, 0))
        rs_in = pl.BlockSpec((tr, D), lambda i, me_ref: (i, 0), **mode)
    return pl.pallas_call(
        body, name=name,
        grid_spec=pltpu.PrefetchScalarGridSpec(
            num_scalar_prefetch=1, grid=(rows // tr,),
            in_specs=[pl.BlockSpec((N_DEV, tr, D), lambda i, me_ref: (0, i, 0), **mode),
                      pl.BlockSpec((None, tr, D), lambda i, me_ref: (me_ref[0], i, 0), **mode), rs_in, rs_in, rs_in],
            out_specs=[rs] * 4),
        out_shape=[jax.ShapeDtypeStruct(w.shape, f32)] * 4,
        compiler_params=_cparams(("parallel",)))(me, slots, own, w, m, v)


def _adam_small(g, w, m, v):
    def body(g_ref, w_ref, m_ref, v_ref, d_ref, nm_ref, nv_ref):
        delta, nm, nv = _adam_math(g_ref[...], w_ref[...], m_ref[...], v_ref[...])
        d_ref[...] = delta
        nm_ref[...] = nm
        nv_ref[...] = nv

    return pl.pallas_call(body, name="adam_small", out_shape=[jax.ShapeDtypeStruct(g.shape, f32)] * 3)(g, w, m, v)


FFN_PAD = 6 * D


_SMALL_PARTS = (("norm1_g", 1), ("gate_b", 2), ("conv_w", CONV_WIDTH), ("conv_b", 1), ("conv_norm_g", 1),
                ("q_norm_g", 1), ("k_norm_g", 1), ("norm2_g", 1), ("ffn_conv_w", 18), ("ffn_conv_b", 6), ("last", 1))


def _small_offsets():
    out, row = {}, 0
    for name, rows in _SMALL_PARTS:
        out[name] = row
        row += -(-rows // 8) * 8
    assert row == SMALL_ROWS
    return out


def _pack_small(norm1_g, gate_b, conv_w, conv_b, conv_norm_g, q_norm_g, k_norm_g, norm2_g, ffn_conv_w, ffn_conv_b,
                last_row=None):
    pad_h = lambda a: jnp.pad(a, ((0, 0), (0, D - HEAD_DIM)))
    pad_f = lambda a: jnp.pad(a, ((0, 0), (0, FFN_PAD - 2 * D_FF))).reshape(-1, D)
    parts = [norm1_g, gate_b.reshape(2, D), conv_w, conv_b, conv_norm_g, pad_h(q_norm_g), pad_h(k_norm_g), norm2_g,
             pad_f(ffn_conv_w), pad_f(ffn_conv_b), jnp.zeros((1, D), f32) if last_row is None else last_row]
    return jnp.concatenate([jnp.pad(p, ((0, -p.shape[0] % 8), (0, 0))) for p in parts], axis=0)


def _unpack_small(p):
    o = _small_offsets()
    rows = lambda name, n: p[o[name]:o[name] + n]
    ffn = lambda a: a.reshape(-1, FFN_PAD)[:, :2 * D_FF]
    return dict(
        norm1_g=rows("norm1_g", 1), gate_b=rows("gate_b", 2).reshape(1, 2 * D), conv_w=rows("conv_w", CONV_WIDTH),
        conv_b=rows("conv_b", 1), conv_norm_g=rows("conv_norm_g", 1), q_norm_g=rows("q_norm_g", 1)[:, :HEAD_DIM],
        k_norm_g=rows("k_norm_g", 1)[:, :HEAD_DIM], norm2_g=rows("norm2_g", 1),
        ffn_conv_w=ffn(rows("ffn_conv_w", 18)), ffn_conv_b=ffn(rows("ffn_conv_b", 6)))


_ADAM_TILE = {896: 128, 704: 704, 128: 128, 352: 176}


def kernel(x, norm1_g, w_in, gate_b, conv_w, conv_b, conv_norm_g, w_conv_out, q_norm_g, k_norm_g, w_attn_out, w_out, norm2_g, w_up, ffn_conv_w, ffn_conv_b, w_down, loss_target, m_norm1_g, m_w_in, m_gate_b, m_conv_w, m_conv_b, m_conv_norm_g, m_w_conv_out, m_q_norm_g, m_k_norm_g, m_w_attn_out, m_w_out, m_norm2_g, m_w_up, m_ffn_conv_w, m_ffn_conv_b, m_w_down, v_norm1_g, v_w_in, v_gate_b, v_conv_w, v_conv_b, v_conv_norm_g, v_w_conv_out, v_q_norm_g, v_k_norm_g, v_w_attn_out, v_w_out, v_norm2_g, v_w_up, v_ffn_conv_w, v_ffn_conv_b, v_w_down):
    BL, S, _ = x.shape
    T = BL * S
    me = 4 * lax.axis_index("x") + 2 * lax.axis_index("y") + lax.axis_index("c")
    xt = x.reshape(T, D)
    target = loss_target.reshape(T, D)

    big = dict(w_in=(w_in[0], m_w_in[0], v_w_in[0]), w_up=(w_up[0], m_w_up[0], v_w_up[0]),
               w_conv_out=(w_conv_out[0], m_w_conv_out[0], v_w_conv_out[0]),
               w_attn_out=(w_attn_out[0], m_w_attn_out[0], v_w_attn_out[0]),
               w_out=(w_out[0], m_w_out[0], v_w_out[0]), w_down=(w_down[0], m_w_down[0], v_w_down[0]))
    order = ["w_in", "w_conv_out", "w_attn_out", "w_out", "w_up", "w_down"]
    shards = [(big[n][0].T if n in ("w_in", "w_up") else big[n][0]).astype(bf16) for n in order]
    gathered = _allgather_rows(shards, 1)
    W = {"w_in": gathered[0].reshape(-1, D)}

    def place_cols(shard, full_cols):
        z = jnp.zeros((shard.shape[0], full_cols), f32)
        return lax.dynamic_update_slice(z, shard, (0, me * shard.shape[1]))

    zr = lambda a: jnp.zeros_like(a)
    conv_local = _pack_small(
        zr(norm1_g), zr(gate_b), place_cols(conv_w[0], D), zr(conv_b), zr(conv_norm_g), zr(q_norm_g), zr(k_norm_g),
        zr(norm2_g), place_cols(ffn_conv_w[0], 2 * D_FF), zr(ffn_conv_b))
    ga_conv = _small_start("gather_conv_start", conv_local, after=gathered[0])
    ga_proj = _exchange_start("gather_start_proj", shards[1:4], gathered[1:4], after=ga_conv[4])
    ga_ffn = _exchange_start("gather_start_ffn", shards[4:6], gathered[4:6], after=ga_proj[4])

    bd = (jnp.arange(128)[:, None] // HEAD_DIM == jnp.arange(128)[None, :] // HEAD_DIM).astype(bf16)
    bias = _attn_bias()
    qg = jnp.tile(q_norm_g, (1, N_HEADS))
    kg = jnp.tile(k_norm_g, (1, N_HEADS))

    z8, h, qn, kn = _in_proj_fwd(xt, norm1_g, W["w_in"], qg, kg, bd, ga_ffn[4])
    conv_all = _unpack_small(_small_sum("gather_conv", me.reshape(1), ga_conv, z8))
    conv_w_full, ffn_w_full = conv_all["conv_w"], conv_all["ffn_conv_w"]
    c = _conv_fwd(z8, conv_w_full, conv_b, S)
    o, ob, lse = _attn_fwd(qn, kn, z8, bias, S)
    for n, g in zip(order[1:4], _exchange_wait("gather_wait_proj", ga_proj, ob)[1]):
        W[n] = g.reshape(-1, D)
    s, ya, yb, mixed = _branches_fwd(c, ob, z8, conv_norm_g, gate_b, W["w_conv_out"], W["w_attn_out"])
    x1, h2 = _out_norm2_fwd(mixed, W["w_out"], xt, norm2_g)
    for n, g in zip(order[4:6], _exchange_wait("gather_wait_ffn", ga_ffn, x1)[1]):
        W[n] = g.reshape(-1, D)
    TNU = D_FF // 2
    u3 = _matmul_call(
        "mm_u", h2, W["w_up"],
        pl.BlockSpec((1024, D), lambda i, j, k: (i, 0)),
        pl.BlockSpec((TNU, D), lambda i, j, k: (j, 0)),
        pl.BlockSpec((None, 1024, TNU), lambda i, j, k: (j // 2, i, j % 2)),
        jax.ShapeDtypeStruct((2, T, D_FF), f32), (T // 1024, 4, 1), "nt", 1, 1024, TNU)
    f = _ffn_fwd(u3, ffn_w_full, ffn_conv_b, S)
    dy, dyb, lacc = _down_loss_fwd(f, W["w_down"], x1, target)
    loss_local = 0.5 / D * jnp.sum(lacc)

    df = _matmul("mm_df", dyb, W["w_down"], "nt", f32, tn=TNU)
    g_w_down = _matmul("mm_dwdn", f, dyb, "tn", bf16, tm=TNU)
    du3, dffn = _ffn_bwd(u3, df, ffn_w_full, ffn_conv_b, S)
    g_w_up = _matmul_call(
        "mm_dwup", du3, h2,
        pl.BlockSpec((None, T, TNU), lambda i, j, k: (i // 2, 0, i % 2)),
        pl.BlockSpec((T, D), lambda i, j, k: (0, 0)),
        pl.BlockSpec((TNU, D), lambda i, j, k: (i, 0)),
        jax.ShapeDtypeStruct((2 * D_FF, D), bf16), (4, 1, 1), "tn", 1, TNU, D)
    blocks8 = lambda a: a.reshape(N_DEV, -1, D)
    ex_ffn = _exchange_start("scatter_start_ffn", [blocks8(g_w_up), blocks8(g_w_down)])
    dx1, dx1b, dg_norm2 = _up_norm2_bwd(du3, W["w_up"], x1, dy, norm2_g, ex_ffn[4])
    g_w_out = _matmul("mm_dwo", mixed, dx1b, "tn", bf16, tm=512)
    dz8 = lax.empty((8, T, D), bf16)
    dya, dyb2, dz8, dg_gate = _out_gate_bwd(dx1b, W["w_out"], z8, gate_b, ya, yb, dz8)
    g_w_conv_out = _matmul("mm_dwco", s, dya, "tn", bf16, tm=512)
    g_w_attn_out = _matmul("mm_dwao", ob, dyb2, "tn", bf16, tm=512)
    ex_proj = _exchange_start("scatter_start_proj", [blocks8(g_w_conv_out), blocks8(g_w_attn_out), blocks8(g_w_out)])
    do = _matmul("mm_do", dyb2, W["w_attn_out"], "nt", f32, after=ex_proj[4])
    dc, dg_convnorm = _convnorm_bwd(dya, W["w_conv_out"], c, conv_norm_g)
    dz8a, dconv = _conv_bwd(dc, z8, conv_w_full, dz8, S)
    dwin_specs = lambda zsec, wsec: (
        pl.BlockSpec((None, T, D), lambda i, j, k: (zsec(i), 0, 0)), pl.BlockSpec((T, D), lambda i, j, k: (0, 0)),
        pl.BlockSpec((1024, D), lambda i, j, k: (wsec(i), 0)), jax.ShapeDtypeStruct((7 * D, D), bf16))
    g_w_in = _matmul_call("mm_dwin_a", dz8a, h, *dwin_specs(lambda i: i, lambda i: jnp.where(i < 2, i, i + 3)),
                          (4, 1, 1), "tn", 1, D, D)
    ex_in_a = _exchange_start("scatter_start_in_a", [blocks8(g_w_in)], half=0)
    dz8b, dg_q, dg_k = _attn_bwd(qn, kn, z8, do, o, lse, bias, bd, qg, kg, dz8a, S, ex_in_a[4])
    g_w_in = _matmul_call("mm_dwin_b", dz8b, h, *dwin_specs(lambda i: i + 4, lambda i: i + 2),
                          (3, 1, 1), "tn", 1, D, D, fill=ex_in_a[2][0].reshape(7 * D, D))
    ex_in_b = _exchange_start("scatter_start_in_b", [blocks8(g_w_in)], ex_in_a[3], gather=False, half=1)
    grad_x, dg_norm1 = _in_norm1_bwd(dz8b, W["w_in"], xt, dx1, norm1_g, ex_in_b[4])

    sum8 = lambda a: a.reshape(-1, 8, a.shape[-1]).sum(axis=1)
    dconv_s = sum8(dconv.sum(axis=0))
    dffn_s = dffn.sum(axis=0).reshape(2, 4, 8, D_FF).sum(axis=2)
    dffn_w = jnp.concatenate([dffn_s[0, :3], dffn_s[1, :3]], axis=1)
    dffn_b = jnp.concatenate([dffn_s[0, 3:4], dffn_s[1, 3:4]], axis=1)
    fold = lambda a: sum8(a).reshape(N_HEADS, HEAD_DIM).sum(axis=0)[None]
    small_g_local = _pack_small(
        sum8(dg_norm1), sum8(dg_gate), dconv_s[:CONV_WIDTH], dconv_s[CONV_WIDTH:], sum8(dg_convnorm),
        fold(dg_q), fold(dg_k), sum8(dg_norm2), dffn_w, dffn_b,
        last_row=jnp.pad(loss_local.reshape(1, 1), ((0, 0), (0, D - 1))))
    sg_start = _small_start("small_grads_start", small_g_local)

    own, slots = {}, {}
    for tag, ex, names_ in (("ffn", ex_ffn, ("w_up", "w_down")),
                            ("proj", ex_proj, ("w_conv_out", "w_attn_out", "w_out"))):
        sent, landed = _exchange_wait("scatter_wait_" + tag, ex, sg_start[4])
        for n, src, land in zip(names_, sent, landed):
            own[n], slots[n] = src, land
    sent, landed = _exchange_wait("scatter_wait_in_a", ex_in_a[:2] + (ex_in_b[2], ex_in_b[3]) + ex_in_a[4:],
                                  sg_start[4])
    sent, landed = _exchange_wait("scatter_wait_in_b", ex_in_b[:2] + (sent, landed) + ex_in_b[4:], sg_start[4])
    own["w_in"], slots["w_in"] = sent[0], landed[0]

    res, adam_done = {}, []
    for n in order:
        w, m, v = big[n]
        outs = _adam_slots("adam_" + n, me.reshape(1), slots[n], own[n], w, m, v, _ADAM_TILE[slots[n].shape[1]],
                           transposed=n in ("w_in", "w_up"))
        adam_done.append(outs[0])
        res[n] = [a[None] for a in outs]
    small_g = _small_sum("small_grads", me.reshape(1), sg_start, adam_done)
    loss = small_g[_small_offsets()["last"], 0]

    col = lambda a, width: lax.dynamic_slice(a, (0, me * width), (a.shape[0], width))
    small_w_true = _pack_small(norm1_g, gate_b, conv_w_full, conv_b, conv_norm_g, q_norm_g, k_norm_g, norm2_g,
                               ffn_w_full, ffn_conv_b)
    place_m = lambda a, full: place_cols(a[0], full)
    small_m = _pack_small(m_norm1_g, m_gate_b, place_m(m_conv_w, D), m_conv_b, m_conv_norm_g, m_q_norm_g, m_k_norm_g,
                          m_norm2_g, place_m(m_ffn_conv_w, 2 * D_FF), m_ffn_conv_b)
    small_v = _pack_small(v_norm1_g, v_gate_b, place_m(v_conv_w, D), v_conv_b, v_conv_norm_g, v_q_norm_g, v_k_norm_g,
                          v_norm2_g, place_m(v_ffn_conv_w, 2 * D_FF), v_ffn_conv_b)
    sd, sm, sv = _adam_small(small_g, small_w_true, small_m, small_v)
    for i, packed in enumerate((small_g, sd, sm, sv)):
        u = _unpack_small(packed)
        u["conv_w"] = col(u["conv_w"], D // N_DEV)
        u["ffn_conv_w"] = col(u["ffn_conv_w"], 2 * D_FF // N_DEV)
        for n, a in u.items():
            res.setdefault(n, [None] * 4)[i] = a[None] if n in ("conv_w", "ffn_conv_w") else a

    names = ["norm1_g", "w_in", "gate_b", "conv_w", "conv_b", "conv_norm_g", "w_conv_out", "q_norm_g", "k_norm_g",
             "w_attn_out", "w_out", "norm2_g", "w_up", "ffn_conv_w", "ffn_conv_b", "w_down"]
    out = [loss, grad_x.reshape(BL, S, D)]
    for i in range(4):
        out += [res[n][i] for n in names]
    return tuple(out)
```

```python
import functools

import jax
import jax.numpy as jnp
import numpy as np
from jax import lax
from jax.experimental import pallas as pl
from jax.experimental.pallas import tpu as pltpu

f32 = jnp.float32
bf16 = jnp.bfloat16

D = 1024
N_HEADS = 16
HEAD_DIM = 64
CONV_WIDTH = 31
D_FF = 2816
GROUPS = ((128, 1), (512, 4), (2048, 16))
ATTN_BLOCK = 128
EPS = 1e-6
N_DEV = 8
MESH = pl.DeviceIdType.MESH

ADAM_LR = 0.001
ADAM_B1 = 0.9
ADAM_B2 = 0.999
ADAM_EPS = 1e-08
ADAM_WD = 0.01
ADAM_STEP = 10

VMEM_LIMIT = 56 * 1024 * 1024
MASK_BIAS = 1e30

Z_AVAL, Z_AGATE, Z_GA, Z_GB, Z_Q, Z_K, Z_V = 0, 1, 2, 3, 4, 5, 6


_W_OF_Z = (0, 1, 5, 6, 2, 3, 4)


def _wsec_of_zsec(j):
    return jnp.where(j < 2, j, jnp.where(j < 4, j + 3, j - 2))


def _sig(x):
    return 1.0 / (1.0 + jnp.exp(-x))


def _colsum8(x):
    return x.reshape(-1, 8, x.shape[-1]).sum(axis=0)


def _cparams(sem):
    return pltpu.CompilerParams(dimension_semantics=sem, vmem_limit_bytes=VMEM_LIMIT)


def _my_pos():
    x, y, c = lax.axis_index("x"), lax.axis_index("y"), lax.axis_index("c")
    return x, y, c, 4 * x + 2 * y + c


_DIMS = {"nn": ((1,), (0,)), "nt": ((1,), (1,)), "tn": ((0,), (0,))}


def _matmul_call(name, a, b, a_spec, b_spec, o_spec, out_shape, grid, mode, nk, tm, tn, after=None):
    dims = (_DIMS[mode], ((), ()))
    extra = [] if after is None else [after]

    def body(a_ref, b_ref, *rest):
        o_ref, scratch = rest[len(extra)], rest[len(extra) + 1:]
        part = lax.dot_general(a_ref[...], b_ref[...], dims, preferred_element_type=f32)
        if nk == 1:
            o_ref[...] = part.astype(o_ref.dtype)
        else:
            acc = scratch[0]
            k = pl.program_id(2)

            @pl.when(k == 0)
            def _():
                acc[...] = part

            @pl.when(k > 0)
            def _():
                acc[...] += part

            @pl.when(k == nk - 1)
            def _():
                o_ref[...] = acc[...].astype(o_ref.dtype)

    scratch = [] if nk == 1 else [pltpu.VMEM((tm, tn), f32)]
    return pl.pallas_call(
        body, name=name, grid=grid, in_specs=[a_spec, b_spec] + [pl.BlockSpec(memory_space=pl.ANY)] * len(extra),
        out_specs=o_spec, out_shape=out_shape,
        scratch_shapes=scratch, compiler_params=_cparams(("parallel", "parallel", "arbitrary")),
    )(a, b, *extra)


def _matmul(name, a, b, mode, out_dtype, tm=1024, tn=1024, tk=None, after=None):
    if mode == "nn":
        (M, K), (_, N) = a.shape, b.shape
    elif mode == "nt":
        (M, K), (N, _) = a.shape, b.shape
    else:
        (K, M), (_, N) = a.shape, b.shape
    tm, tn = min(tm, M), min(tn, N)
    tk = K if tk is None else tk
    nk = K // tk
    assert M % tm == 0 and N % tn == 0 and K % tk == 0
    if mode == "tn":
        a_spec = pl.BlockSpec((tk, tm), lambda i, j, k: (k, i))
    else:
        a_spec = pl.BlockSpec((tm, tk), lambda i, j, k: (i, k))
    if mode == "nt":
        b_spec = pl.BlockSpec((tn, tk), lambda i, j, k: (j, k))
    else:
        b_spec = pl.BlockSpec((tk, tn), lambda i, j, k: (k, j))
    o_spec = pl.BlockSpec((tm, tn), lambda i, j, k: (i, j))
    return _matmul_call(name, a, b, a_spec, b_spec, o_spec, jax.ShapeDtypeStruct((M, N), out_dtype),
                        (M // tm, N // tn, nk), mode, nk, tm, tn, after=after)


FTM = 512


def _matmul_fused(name, a, b, pairs, epilogue, extras, consts, outs, nt=False, sums=False, passed=(), aliases=None):
    sa, M, kk = a.shape
    na = max(i for i, _ in pairs) + 1
    ne, nc, npass = len(extras), len(consts), len(passed)
    dims = (_DIMS["nt" if nt else "nn"], ((), ()))

    def body(a_ref, b_ref, *rest):
        acc = None
        for i, j in pairs:
            part = lax.dot_general(a_ref[i], b_ref[j], dims, preferred_element_type=f32)
            acc = part if acc is None else acc + part
        epilogue(acc, rest[:ne], rest[ne:ne + nc], rest[ne + nc + npass:])

    whole = lambda arr: pl.BlockSpec(arr.shape, lambda i, nd=arr.ndim: (0,) * nd, pipeline_mode=pl.Buffered(1))
    io_alias = {2 + ne + nc + k: v for k, v in (aliases or {}).items()}
    return pl.pallas_call(
        body, name=name, grid=(M // FTM,),
        in_specs=[pl.BlockSpec((na, FTM, kk), lambda i: (0, i, 0)), whole(b)] + [s for _, s in extras]
        + [whole(c) for c in consts] + [pl.BlockSpec(memory_space=pl.ANY)] * npass,
        out_specs=[s for _, s in outs], out_shape=[s for s, _ in outs], input_output_aliases=io_alias,
        compiler_params=_cparams(("arbitrary" if sums else "parallel",)),
    )(a, b, *[x for x, _ in extras], *consts, *passed)


def _frows(c=D):
    return pl.BlockSpec((FTM, c), lambda i: (i, 0))


def _fsec(s):
    return pl.BlockSpec((None, FTM, D), lambda i: (s, i, 0))


def _rowshape(T, dtype, c=D):
    return (jax.ShapeDtypeStruct((T, c), dtype), _frows(c))


def _sumshape(c=D):
    return (jax.ShapeDtypeStruct((8, c), f32), pl.BlockSpec((8, c), lambda i: (0, 0)))


def _add_colsum(ref, x, cols=None):
    @pl.when(pl.program_id(0) == 0)
    def _():
        if cols is None:
            ref[...] = jnp.zeros_like(ref)
        else:
            ref[:, cols] = jnp.zeros((8, x.shape[-1]), f32)

    if cols is None:
        ref[...] += _colsum8(x)
    else:
        ref[:, cols] += _colsum8(x)


def _rms(x):
    return lax.rsqrt(jnp.mean(x * x, axis=-1, keepdims=True) + EPS)


def _rms_bwd(dy_g, xn, rstd):
    return rstd * (dy_g - xn * jnp.mean(dy_g * xn, axis=-1, keepdims=True))


def _head_sum(x, bd):
    parts = []
    for cb in range(x.shape[-1] // 128):
        xb = x[:, cb * 128:(cb + 1) * 128]
        hi = xb.astype(bf16)
        lo = (xb - hi.astype(f32)).astype(bf16)
        parts.append(jnp.dot(hi, bd, preferred_element_type=f32) + jnp.dot(lo, bd, preferred_element_type=f32))
    return parts[0] if len(parts) == 1 else jnp.concatenate(parts, axis=1)


ZTM = 1024


def _in_proj_fwd(x, g, w_in_t, qg, kg, bd, after):
    T = x.shape[0]

    def body(x_ref, g_ref, w_ref, qg_ref, kg_ref, bd_ref, after_ref, z_ref, h_ref, qn_ref, kn_ref, hbuf):
        del after_ref
        j = pl.program_id(1)

        @pl.when(j == 0)
        def _():
            xv = x_ref[...]
            hv = (xv * _rms(xv) * g_ref[...]).astype(bf16)
            hbuf[...] = hv
            h_ref[...] = hv

        z = lax.dot_general(hbuf[...], w_ref[...], (_DIMS["nt"], ((), ())), preferred_element_type=f32)
        z_ref[...] = z

        def head_norm(gain_ref, scale):
            return z * lax.rsqrt(_head_sum(z * z, bd_ref[...]) * (1.0 / HEAD_DIM) + EPS) * gain_ref[...] * scale

        @pl.when(j == Z_Q)
        def _():
            qn_ref[...] = head_norm(qg_ref, HEAD_DIM ** -0.5)

        @pl.when(j == Z_K)
        def _():
            kn_ref[...] = head_norm(kg_ref, 1.0)

    tile = pl.BlockSpec((ZTM, D), lambda i, j: (i, 0))
    row = pl.BlockSpec((1, D), lambda i, j: (0, 0))
    return pl.pallas_call(
        body, name="mm_z", grid=(T // ZTM, 7),
        in_specs=[tile, row, pl.BlockSpec((D, D), lambda i, j: (_wsec_of_zsec(j), 0)), row, row,
                  pl.BlockSpec((128, 128), lambda i, j: (0, 0)), pl.BlockSpec(memory_space=pl.ANY)],
        out_specs=[pl.BlockSpec((None, ZTM, D), lambda i, j: (j, i, 0)), tile, tile, tile],
        out_shape=[jax.ShapeDtypeStruct((8, T, D), f32), jax.ShapeDtypeStruct((T, D), bf16),
                   jax.ShapeDtypeStruct((T, D), f32), jax.ShapeDtypeStruct((T, D), f32)],
        scratch_shapes=[pltpu.VMEM((ZTM, D), bf16)],
        compiler_params=_cparams(("parallel", "arbitrary")))(x, g, w_in_t, qg, kg, bd, after)


def _branches_fwd(c, ob, z8, g, gate_b, w_conv_out, w_attn_out):
    T = c.shape[0]

    def epilogue(yb, extra, const, out):
        cv = extra[0][...]
        r = cv * _rms(cv) * const[0][...]
        s = (r * _sig(r)).astype(bf16)
        ya = jnp.dot(s, const[2][...], preferred_element_type=f32)
        b_ref = const[1]
        g_a = _sig(extra[1][...] + b_ref[:, :D])
        g_b = _sig(extra[2][...] + b_ref[:, D:])
        out[0][...] = s
        out[1][...] = ya
        out[2][...] = yb
        out[3][...] = (g_a * ya + g_b * yb).astype(bf16)

    return _matmul_fused("mm_branches", ob[None], w_attn_out[None], ((0, 0),), epilogue,
                         [(c, _frows()), (z8, _fsec(Z_GA)), (z8, _fsec(Z_GB))], [g, gate_b, w_conv_out],
                         [_rowshape(T, bf16), _rowshape(T, f32), _rowshape(T, f32), _rowshape(T, bf16)])


def _out_norm2_fwd(mixed, w_out, x, g):
    T = x.shape[0]

    def epilogue(acc, extra, const, out):
        x1 = extra[0][...] + acc
        out[0][...] = x1
        out[1][...] = (x1 * _rms(x1) * const[0][...]).astype(bf16)

    return _matmul_fused("mm_t1_norm2", mixed[None], w_out[None], ((0, 0),), epilogue, [(x, _frows())], [g],
                         [_rowshape(T, f32), _rowshape(T, bf16)])


def _down_loss_fwd(f, w_down, x1, target):
    T = x1.shape[0]

    def epilogue(acc, extra, const, out):
        diff = extra[0][...] + acc - extra[1][...]
        dy = diff * (1.0 / D)
        out[0][...] = dy
        out[1][...] = dy.astype(bf16)
        _add_colsum(out[2], diff * diff)

    return _matmul_fused("mm_t2_loss", f[None], w_down[None], ((0, 0),), epilogue, [(x1, _frows()), (target, _frows())],
                         [], [_rowshape(T, f32), _rowshape(T, bf16), _sumshape()], sums=True)


def _up_norm2_bwd(du3, w_up_t, x1, dy, g, token):
    T = x1.shape[0]

    def epilogue(dh, extra, const, out):
        x1v = extra[0][...]
        rstd = _rms(x1v)
        xn = x1v * rstd
        dx1 = extra[1][...] + _rms_bwd(dh * const[0][...], xn, rstd)
        out[0][...] = dx1
        out[1][...] = dx1.astype(bf16)
        _add_colsum(out[2], dh * xn)

    return _matmul_fused("mm_dh2_norm2", du3, w_up_t.reshape(2, D_FF, D), ((0, 0), (1, 1)), epilogue,
                         [(x1, _frows()), (dy, _frows())], [g],
                         [_rowshape(T, f32), _rowshape(T, bf16), _sumshape()], sums=True, passed=[token])


def _out_gate_bwd(dx1b, w_out, z8, gate_b, ya, yb, dz8):
    T = ya.shape[0]

    def epilogue(dm, extra, const, out):
        b_ref = const[0]
        g_a = _sig(extra[0][...] + b_ref[:, :D])
        g_b = _sig(extra[1][...] + b_ref[:, D:])
        out[0][...] = (dm * g_a).astype(bf16)
        out[1][...] = (dm * g_b).astype(bf16)
        dla = dm * extra[2][...] * g_a * (1.0 - g_a)
        dlb = dm * extra[3][...] * g_b * (1.0 - g_b)
        out[2][0] = dla.astype(bf16)
        out[2][1] = dlb.astype(bf16)
        _add_colsum(out[3], dla, slice(0, D))
        _add_colsum(out[3], dlb, slice(D, 2 * D))

    return _matmul_fused(
        "mm_dmixed_gate", dx1b[None], w_out[None], ((0, 0),), epilogue,
        [(z8, _fsec(Z_GA)), (z8, _fsec(Z_GB)), (ya, _frows()), (yb, _frows())], [gate_b],
        [_rowshape(T, bf16), _rowshape(T, bf16),
         (jax.ShapeDtypeStruct(dz8.shape, bf16), pl.BlockSpec((2, FTM, D), lambda i: (1, i, 0))), _sumshape(2 * D)],
        nt=True, sums=True, passed=[dz8], aliases={0: 2})


def _convnorm_bwd(dya, w_conv_out, c, g):
    T = c.shape[0]

    def epilogue(ds, extra, const, out):
        cv = extra[0][...]
        rstd = _rms(cv)
        r0 = cv * rstd
        gv = const[0][...]
        r = r0 * gv
        sg = _sig(r)
        dr = ds * sg * (1.0 + r * (1.0 - sg))
        out[0][...] = _rms_bwd(dr * gv, r0, rstd)
        _add_colsum(out[1], dr * r0)

    return _matmul_fused("mm_ds_convnorm", dya[None], w_conv_out[None], ((0, 0),), epilogue, [(c, _frows())], [g],
                         [_rowshape(T, f32), _sumshape()], nt=True, sums=True)


def _in_norm1_bwd(dz8, w_in_t, x, dx1, g, token):
    T = x.shape[0]

    def epilogue(dh, extra, const, out):
        xv = extra[0][...]
        rstd = _rms(xv)
        xn = xv * rstd
        out[0][...] = extra[1][...] + _rms_bwd(dh * const[0][...], xn, rstd)
        _add_colsum(out[1], dh * xn)

    return _matmul_fused("mm_dh_norm1", dz8, w_in_t.reshape(7, D, D), tuple(zip(range(7), _W_OF_Z)), epilogue,
                         [(x, _frows()), (dx1, _frows())], [g], [_rowshape(T, f32), _sumshape()],
                         sums=True, passed=[token])


CCW = 256
CR = 64
HALO = 32


def _conv_fwd(z8, conv_w, conv_b, S):
    T = z8.shape[1]
    nb = T // S
    ncb = D // CCW

    def body(av_ref, ag_ref, w_ref, b_ref, c_ref, pad):
        pad[0:HALO, :] = jnp.zeros((HALO, CCW), f32)

        def fill(i, carry):
            r0 = pl.multiple_of(i * 256, 256)
            pad[pl.ds(HALO + r0, 256), :] = av_ref[pl.ds(r0, 256), :] * _sig(ag_ref[pl.ds(r0, 256), :])
            return carry

        lax.fori_loop(0, S // 256, fill, 0)
        bias = b_ref[...]

        def chunk(i, carry):
            r0 = pl.multiple_of(i * CR, CR)
            win = pad[pl.ds(r0, CR + HALO), :]
            acc = jnp.zeros((CR, CCW), f32) + bias
            for s in range(8):
                part = None
                for m in range((CONV_WIDTH - 1 - s) // 8 + 1):
                    j = CONV_WIDTH - 1 - 8 * m - s
                    term = win[24 - 8 * m:24 - 8 * m + CR + 8, :] * w_ref[j:j + 1, :]
                    part = term if part is None else part + term
                acc = acc + part[8 - s:8 - s + CR, :]
            c_ref[pl.ds(r0, CR), :] = acc
            return carry

        lax.fori_loop(0, S // CR, chunk, 0)

    zs = lambda s: pl.BlockSpec((None, S, CCW), lambda b, cb: (s, b, cb))
    return pl.pallas_call(
        body, name="conv_fwd", grid=(nb, ncb),
        in_specs=[zs(Z_AVAL), zs(Z_AGATE), pl.BlockSpec((CONV_WIDTH, CCW), lambda b, cb: (0, cb)),
                  pl.BlockSpec((1, CCW), lambda b, cb: (0, cb))],
        out_specs=pl.BlockSpec((S, CCW), lambda b, cb: (b, cb)),
        out_shape=jax.ShapeDtypeStruct((T, D), f32),
        scratch_shapes=[pltpu.VMEM((S + HALO, CCW), f32)],
        compiler_params=_cparams(("parallel", "parallel")))(z8, z8, conv_w, conv_b)


def _conv_bwd(dc, z8, conv_w, dz8, S):
    T = dc.shape[0]
    nb = T // S
    ncb = D // CCW

    def body(dc_ref, av_ref, ag_ref, w_ref, dz_in, dz_ref, dw_ref, apad, dpad, shbuf):
        del dz_in
        apad[0:HALO, :] = jnp.zeros((HALO, CCW), f32)
        dpad[S:S + HALO, :] = jnp.zeros((HALO, CCW), f32)
        dw_ref[...] = jnp.zeros_like(dw_ref)

        def fill(i, carry):
            r0 = pl.multiple_of(i * 256, 256)
            apad[pl.ds(HALO + r0, 256), :] = av_ref[pl.ds(r0, 256), :] * _sig(ag_ref[pl.ds(r0, 256), :])
            dpad[pl.ds(r0, 256), :] = dc_ref[pl.ds(r0, 256), :]
            return carry

        lax.fori_loop(0, S // 256, fill, 0)

        def chunk(i, carry):
            r0 = pl.multiple_of(i * CR, CR)
            dwin = dpad[pl.ds(r0, CR + HALO), :]
            da = jnp.zeros((CR, CCW), f32)
            for s in range(8):
                shbuf[...] = dwin[s:s + CR, :]
                dshift = shbuf[...]
                part = None
                for m in range((CONV_WIDTH - 1 - s) // 8 + 1):
                    j = CONV_WIDTH - 1 - 8 * m - s
                    term = dwin[8 * m:8 * m + CR + 8, :] * w_ref[j:j + 1, :]
                    part = term if part is None else part + term
                    a_lag = apad[pl.ds(r0 + HALO - 8 * m, CR), :]
                    dw_ref[8 * j:8 * j + 8, :] += _colsum8(dshift * a_lag)
                da = da + part[s:s + CR, :]
            dw_ref[8 * CONV_WIDTH:8 * CONV_WIDTH + 8, :] += _colsum8(dwin[0:CR, :])
            av = av_ref[pl.ds(r0, CR), :]
            sg = _sig(ag_ref[pl.ds(r0, CR), :])
            dz_ref[0, pl.ds(r0, CR), :] = (da * sg).astype(bf16)
            dz_ref[1, pl.ds(r0, CR), :] = (da * av * sg * (1.0 - sg)).astype(bf16)
            return carry

        lax.fori_loop(0, S // CR, chunk, 0)

    zs = lambda s: pl.BlockSpec((None, S, CCW), lambda b, cb: (s, b, cb))
    return pl.pallas_call(
        body, name="conv_bwd", grid=(nb, ncb),
        in_specs=[pl.BlockSpec((S, CCW), lambda b, cb: (b, cb)), zs(Z_AVAL), zs(Z_AGATE),
                  pl.BlockSpec((CONV_WIDTH, CCW), lambda b, cb: (0, cb)), pl.BlockSpec(memory_space=pl.ANY)],
        out_specs=[pl.BlockSpec((2, S, CCW), lambda b, cb: (0, b, cb)),
                   pl.BlockSpec((None, 256, CCW), lambda b, cb: (b, 0, cb))],
        out_shape=[jax.ShapeDtypeStruct(dz8.shape, bf16), jax.ShapeDtypeStruct((nb, 256, D), f32)],
        input_output_aliases={4: 0},
        scratch_shapes=[pltpu.VMEM((S + HALO, CCW), f32), pltpu.VMEM((S + HALO, CCW), f32),
                        pltpu.VMEM((CR, CCW), f32)],
        compiler_params=_cparams(("parallel", "parallel")))(dc, z8, z8, conv_w, dz8)


FR = 128
NFB = D_FF // CCW
FBW = 128


def _ffn_window(ref, i, r0):
    return ref[pl.ds(r0 - 8, FR + 8), :]


def _ffn_u(win, w_ref, b_ref):
    return (win[6:6 + FR, :] * w_ref[0:1, :] + win[7:7 + FR, :] * w_ref[1:2, :]
            + win[8:8 + FR, :] * w_ref[2:3, :] + b_ref[...])


def _ffn_fwd(u3, ffn_w, ffn_b, S):
    T = u3.shape[1]
    nb = T // S

    def body(uv_ref, ug_ref, wv_ref, wg_ref, bv_ref, bg_ref, f_ref):
        def chunk(first, i):
            r0 = 0 if first else pl.multiple_of(i * FR, FR)
            if first:
                z = jnp.zeros((8, CCW), f32)
                wv = jnp.concatenate([z, uv_ref[0:FR, :]], axis=0)
                wg = jnp.concatenate([z, ug_ref[0:FR, :]], axis=0)
            else:
                wv = _ffn_window(uv_ref, i, r0)
                wg = _ffn_window(ug_ref, i, r0)
            u_val = _ffn_u(wv, wv_ref, bv_ref)
            u_gate = _ffn_u(wg, wg_ref, bg_ref)
            f_ref[pl.ds(r0, FR), :] = (u_gate * _sig(u_gate) * u_val).astype(bf16)

        chunk(True, 0)

        def loop(i, carry):
            chunk(False, i)
            return carry

        lax.fori_loop(1, S // FR, loop, 0)

    us = lambda h: pl.BlockSpec((None, S, CCW), lambda b, cb: (h, b, cb))
    ws = lambda h: pl.BlockSpec((3, CCW), lambda b, cb: (0, h * NFB + cb))
    bs = lambda h: pl.BlockSpec((1, CCW), lambda b, cb: (0, h * NFB + cb))
    return pl.pallas_call(
        body, name="ffn_fwd", grid=(nb, NFB),
        in_specs=[us(0), us(1), ws(0), ws(1), bs(0), bs(1)],
        out_specs=pl.BlockSpec((S, CCW), lambda b, cb: (b, cb)),
        out_shape=jax.ShapeDtypeStruct((T, D_FF), bf16),
        compiler_params=_cparams(("parallel", "parallel")))(u3, u3, ffn_w, ffn_w, ffn_b, ffn_b)


def _ffn_bwd(u3, df, ffn_w, ffn_b, S):
    T = u3.shape[1]
    nb = T // S

    def body(uv_ref, ug_ref, df_ref, wv_ref, wg_ref, bv_ref, bg_ref, du_ref, dw_ref, dvpad, dgpad, shbuf):
        dvpad[S:S + 8, :] = jnp.zeros((8, FBW), f32)
        dgpad[S:S + 8, :] = jnp.zeros((8, FBW), f32)
        dw_ref[...] = jnp.zeros_like(dw_ref)

        def chunk(first, i):
            r0 = 0 if first else pl.multiple_of(i * FR, FR)
            if first:
                z = jnp.zeros((8, FBW), f32)
                wv = jnp.concatenate([z, uv_ref[0:FR, :]], axis=0)
                wg = jnp.concatenate([z, ug_ref[0:FR, :]], axis=0)
            else:
                wv = _ffn_window(uv_ref, i, r0)
                wg = _ffn_window(ug_ref, i, r0)
            taps = []
            for h, win in enumerate((wv, wg)):
                shbuf[2 * h] = win[6:6 + FR, :]
                shbuf[2 * h + 1] = win[7:7 + FR, :]
                taps.append((shbuf[2 * h], shbuf[2 * h + 1], win[8:8 + FR, :]))
            conv = lambda x, w_ref, b_ref: (x[0] * w_ref[0:1, :] + x[1] * w_ref[1:2, :] + x[2] * w_ref[2:3, :]
                                            + b_ref[...])
            u_val = conv(taps[0], wv_ref, bv_ref)
            u_gate = conv(taps[1], wg_ref, bg_ref)
            dfc = df_ref[pl.ds(r0, FR), :]
            sg = _sig(u_gate)
            d_val = dfc * u_gate * sg
            d_gate = dfc * u_val * sg * (1.0 + u_gate * (1.0 - sg))
            dvpad[pl.ds(r0, FR), :] = d_val
            dgpad[pl.ds(r0, FR), :] = d_gate
            for h, dd in enumerate((d_val, d_gate)):
                for j in range(3):
                    dw_ref[h, 8 * j:8 * j + 8, :] += _colsum8(dd * taps[h][j])
                dw_ref[h, 24:32, :] += _colsum8(dd)

        chunk(True, 0)

        def loop(i, carry):
            chunk(False, i)
            return carry

        lax.fori_loop(1, S // FR, loop, 0)

        def back(i, carry):
            r0 = pl.multiple_of(i * FR, FR)
            for h, (dpad, w_ref) in enumerate(((dvpad, wv_ref), (dgpad, wg_ref))):
                win = dpad[pl.ds(r0, FR + 8), :]
                du = (win[0:FR, :] * w_ref[2:3, :] + win[1:1 + FR, :] * w_ref[1:2, :]
                      + win[2:2 + FR, :] * w_ref[0:1, :])
                du_ref[h, pl.ds(r0, FR), :] = du.astype(bf16)
            return carry

        lax.fori_loop(0, S // FR, back, 0)

    ncb = D_FF // FBW
    us = lambda h: pl.BlockSpec((None, S, FBW), lambda b, cb: (h, b, cb))
    ws = lambda h: pl.BlockSpec((3, FBW), lambda b, cb: (0, h * ncb + cb))
    bs = lambda h: pl.BlockSpec((1, FBW), lambda b, cb: (0, h * ncb + cb))
    return pl.pallas_call(
        body, name="ffn_bwd", grid=(nb, ncb),
        in_specs=[us(0), us(1), pl.BlockSpec((S, FBW), lambda b, cb: (b, cb)), ws(0), ws(1), bs(0), bs(1)],
        out_specs=[pl.BlockSpec((2, S, FBW), lambda b, cb: (0, b, cb)),
                   pl.BlockSpec((None, 2, 32, FBW), lambda b, cb: (b, 0, 0, cb))],
        out_shape=[jax.ShapeDtypeStruct((2, T, D_FF), bf16), jax.ShapeDtypeStruct((nb, 2, 32, D_FF), f32)],
        scratch_shapes=[pltpu.VMEM((S + 8, FBW), f32), pltpu.VMEM((S + 8, FBW), f32),
                        pltpu.VMEM((4, FR, FBW), f32)],
        compiler_params=_cparams(("parallel", "parallel")))(u3, u3, df, ffn_w, ffn_w, ffn_b, ffn_b)


AB = ATTN_BLOCK


def _attn_bias_np():
    slopes = (np.float32(2.0) ** (np.float32(-8.0) * np.arange(1, N_HEADS + 1, dtype=np.float32)
                                  / np.float32(N_HEADS))).astype(np.float32)
    steps = (np.arange(AB)[:, None] + AB) - np.arange(2 * AB)[None, :]
    own = (np.arange(2 * AB) >= AB)[None, :]
    out = []
    for window, dil in GROUPS:
        valid = (steps >= 0) & (steps <= window // dil)
        dist = slopes[:, None, None] * (steps * dil).astype(np.float32)[None]
        kinds = [np.where(v[None], dist, np.float32(MASK_BIAS)) for v in (valid, valid & own)]
        out.append(np.stack(kinds, axis=1))
    return np.stack(out).astype(np.float32)


def _attn_bias():
    return jnp.asarray(_attn_bias_np())


def _head_masks():
    lane = lax.broadcasted_iota(jnp.int32, (1, 128), 1)
    return (lane < HEAD_DIM, lane >= HEAD_DIM)


def _perm_chunks(S, d):
    L = S // d
    ch = min(L, 256)
    out = []
    for r in range(d):
        for c in range(L // ch):
            start = r + d * ch * c
            out.append((pl.ds(start, ch, stride=d) if d > 1 else pl.ds(start, ch), r * L + c * ch, ch))
    return out


def _stack_heads(x, masks):
    return jnp.concatenate([jnp.where(masks[0], x, 0), jnp.where(masks[1], x, 0)], axis=0)


def _block_row(j):
    return j * AB if isinstance(j, int) else pl.multiple_of(j * AB, AB)


def _three_stages(n, stage_a, stage_b, stage_c, unroll):
    stage_a(0)
    stage_a(1)
    stage_b(0)

    def body(j, carry):
        stage_c(j - 1)
        stage_b(j)
        stage_a(j + 1)
        return carry

    lax.fori_loop(1, n - 1, body, 0, unroll=unroll)
    stage_c(n - 2)
    stage_b(n - 1)
    stage_c(n - 1)


_NT = (((1,), (1,)), ((), ()))
_TN = (((0,), (0,)), ((), ()))
SCH = 64


def _attn_fwd(qn, kn, z8, bias, S):
    T = qn.shape[0]
    nb = T // S
    nblk = S // AB

    def body(q_ref, k_ref, v_ref, bias_ref, o_ref, ob_ref, lse_ref, qs, ks, vs, s2, p2, ogp, lgp, *group_scratch):
        og, lg = group_scratch[:3], group_scratch[3:]
        masks = _head_masks()
        ks[0:AB, :] = jnp.zeros((AB, 128), bf16)
        vs[0:AB, :] = jnp.zeros((AB, 128), bf16)

        for g, (_, d) in enumerate(GROUPS):
            nsub = S // (d * AB)
            chunks = _perm_chunks(S, d)
            for src, dst, ch in chunks:
                qs[dst:dst + ch, :] = q_ref[src, :].astype(bf16)
                ks[AB + dst:AB + dst + ch, :] = k_ref[src, :].astype(bf16)
                vs[AB + dst:AB + dst + ch, :] = v_ref[src, :].astype(bf16)
            od, ld = (og[g], lg[g]) if d == 1 else (ogp, lgp)

            def scores(j):
                r0 = _block_row(j)
                q2 = _stack_heads(qs[pl.ds(r0, AB), :], masks)
                s2[j] = lax.dot_general(q2, ks[pl.ds(r0, 2 * AB), :], _NT, preferred_element_type=f32)

            def softmax(j, g=g, nsub=nsub, ld=ld):
                r0 = _block_row(j)
                kind = int(j % nsub == 0) if isinstance(j, int) else (j % nsub == 0).astype(jnp.int32)
                for cc in range(AB // SCH):
                    lses = []
                    for hh in range(2):
                        rows = pl.ds(hh * AB + cc * SCH, SCH)
                        sb = s2[j, rows, :] - bias_ref[g, hh, kind, cc * SCH:(cc + 1) * SCH, :]
                        m = jnp.max(sb, axis=-1, keepdims=True)
                        p = jnp.exp(sb - m)
                        den = jnp.sum(p, axis=-1, keepdims=True)
                        p2[j, rows, :] = (p * (1.0 / den)).astype(bf16)
                        lses.append(m + jnp.log(den))
                    ld[pl.ds(r0 + cc * SCH, SCH), :] = jnp.where(masks[0], lses[0], lses[1])

            def values(j, od=od):
                r0 = _block_row(j)
                pv2 = jnp.dot(p2[j], vs[pl.ds(r0, 2 * AB), :], preferred_element_type=f32)
                od[pl.ds(r0, AB), :] = jnp.where(masks[0], pv2[:AB], pv2[AB:])

            _three_stages(nblk, scores, softmax, values, nblk - 2)

            if d > 1:
                for src, dst, ch in chunks:
                    og[g][src, :] = ogp[dst:dst + ch, :]
                    lg[g][src, :] = lgp[dst:dst + ch, :]

        def combine(i, carry):
            rr = pl.ds(pl.multiple_of(i * 256, 256), 256)
            l0, l1, l2 = lg[0][rr, :], lg[1][rr, :], lg[2][rr, :]
            mx = jnp.maximum(jnp.maximum(l0, l1), l2)
            e0, e1, e2 = jnp.exp(l0 - mx), jnp.exp(l1 - mx), jnp.exp(l2 - mx)
            den = e0 + e1 + e2
            o = (e0 * og[0][rr, :] + e1 * og[1][rr, :] + e2 * og[2][rr, :]) / den
            o_ref[rr, :] = o
            ob_ref[rr, :] = o.astype(bf16)
            lse_ref[rr, :] = mx + jnp.log(den)
            return carry

        lax.fori_loop(0, S // 256, combine, 0, unroll=True)

    blk = pl.BlockSpec((S, 128), lambda b, hp: (b, hp))
    return pl.pallas_call(
        body, name="attn_fwd", grid=(nb, N_HEADS // 2),
        in_specs=[blk, blk, pl.BlockSpec((None, S, 128), lambda b, hp: (Z_V, b, hp)),
                  pl.BlockSpec((3, 2, 2, AB, 2 * AB), lambda b, hp: (0, hp, 0, 0, 0))],
        out_specs=[blk, blk, blk],
        out_shape=[jax.ShapeDtypeStruct((T, D), f32), jax.ShapeDtypeStruct((T, D), bf16),
                   jax.ShapeDtypeStruct((T, D), f32)],
        scratch_shapes=[pltpu.VMEM((S, 128), bf16), pltpu.VMEM((S + AB, 128), bf16), pltpu.VMEM((S + AB, 128), bf16),
                        pltpu.VMEM((nblk, 2 * AB, 2 * AB), f32), pltpu.VMEM((nblk, 2 * AB, 2 * AB), bf16),
                        pltpu.VMEM((S, 128), f32), pltpu.VMEM((S, 128), f32)] + [pltpu.VMEM((S, 128), f32)] * 6,
        compiler_params=_cparams(("parallel", "parallel")))(qn, kn, z8, bias)


def _attn_bwd(qn, kn, z8, do, o, lse, bias, bd, qg, kg, dz8, S):
    T = qn.shape[0]
    nb = T // S

    nblk = S // AB

    def body(q_ref, k_ref, v_ref, do_ref, o_ref, lse_ref, bias_ref, bd_ref, qraw_ref, kraw_ref, qg_ref, kg_ref,
             dz_in, dz_ref, dqg_ref, dkg_ref,
             dq_ref, dk_ref, dv_ref, delta, qs, ks, vs, dos, lsp, dlp, s2, dp2, p2, ds2, dqp, dkp, dvp):
        del dz_in
        masks = _head_masks()
        bdv = bd_ref[...]
        dq_ref[...] = jnp.zeros_like(dq_ref)
        dk_ref[...] = jnp.zeros_like(dk_ref)
        dv_ref[...] = jnp.zeros_like(dv_ref)
        ks[0:AB, :] = jnp.zeros((AB, 128), bf16)
        vs[0:AB, :] = jnp.zeros((AB, 128), bf16)

        def prep(i, carry):
            rr = pl.ds(pl.multiple_of(i * 256, 256), 256)
            delta[rr, :] = _head_sum(do_ref[rr, :] * o_ref[rr, :], bdv)
            return carry

        lax.fori_loop(0, S // 256, prep, 0, unroll=True)

        for g, (_, d) in enumerate(GROUPS):
            nsub = S // (d * AB)
            chunks = _perm_chunks(S, d)
            for src, dst, ch in chunks:
                qs[dst:dst + ch, :] = q_ref[src, :].astype(bf16)
                ks[AB + dst:AB + dst + ch, :] = k_ref[src, :].astype(bf16)
                vs[AB + dst:AB + dst + ch, :] = v_ref[src, :].astype(bf16)
                dos[dst:dst + ch, :] = do_ref[src, :].astype(bf16)
                lsp[dst:dst + ch, :] = lse_ref[src, :]
                dlp[dst:dst + ch, :] = delta[src, :]
            dkp[...] = jnp.zeros_like(dkp)
            dvp[...] = jnp.zeros_like(dvp)

            def scores(j):
                r0 = _block_row(j)
                q2 = _stack_heads(qs[pl.ds(r0, AB), :], masks)
                do2 = _stack_heads(dos[pl.ds(r0, AB), :], masks)
                s2[j] = lax.dot_general(q2, ks[pl.ds(r0, 2 * AB), :], _NT, preferred_element_type=f32)
                dp2[j] = lax.dot_general(do2, vs[pl.ds(r0, 2 * AB), :], _NT, preferred_element_type=f32)

            def probs(j, g=g, nsub=nsub):
                r0 = _block_row(j)
                kind = int(j % nsub == 0) if isinstance(j, int) else (j % nsub == 0).astype(jnp.int32)
                for cc in range(AB // SCH):
                    lse_c = lsp[pl.ds(r0 + cc * SCH, SCH), :]
                    del_c = dlp[pl.ds(r0 + cc * SCH, SCH), :]
                    for hh in range(2):
                        c0 = hh * HEAD_DIM
                        rows = pl.ds(hh * AB + cc * SCH, SCH)
                        sb = s2[j, rows, :] - bias_ref[g, hh, kind, cc * SCH:(cc + 1) * SCH, :]
                        p = jnp.exp(sb - lse_c[:, c0:c0 + 1])
                        p2[j, rows, :] = p.astype(bf16)
                        ds2[j, rows, :] = (p * (dp2[j, rows, :] - del_c[:, c0:c0 + 1])).astype(bf16)

            def grads(j):
                r0 = _block_row(j)
                q2 = _stack_heads(qs[pl.ds(r0, AB), :], masks)
                do2 = _stack_heads(dos[pl.ds(r0, AB), :], masks)
                dsb = ds2[j]
                t = jnp.dot(dsb, ks[pl.ds(r0, 2 * AB), :], preferred_element_type=f32)
                dqp[pl.ds(r0, AB), :] = jnp.where(masks[0], t[:AB], t[AB:])
                dkp[pl.ds(r0, 2 * AB), :] += lax.dot_general(dsb, q2, _TN, preferred_element_type=f32)
                dvp[pl.ds(r0, 2 * AB), :] += lax.dot_general(p2[j], do2, _TN, preferred_element_type=f32)

            _three_stages(nblk, scores, probs, grads, nblk - 2)

            for src, dst, ch in chunks:
                dq_ref[src, :] += dqp[dst:dst + ch, :]
                dk_ref[src, :] += dkp[AB + dst:AB + dst + ch, :]
                dv_ref[src, :] += dvp[AB + dst:AB + dst + ch, :]

        @pl.when(pl.program_id(1) == 0)
        def _():
            dqg_ref[...] = jnp.zeros_like(dqg_ref)
            dkg_ref[...] = jnp.zeros_like(dkg_ref)

        def norms(i, carry):
            rr = pl.ds(pl.multiple_of(i * 256, 256), 256)

            def one(raw, dn_scaled, g, dg_ref, sec):
                rstd = lax.rsqrt(_head_sum(raw * raw, bdv) * (1.0 / HEAD_DIM) + EPS)
                n = raw * rstd
                dg_ref[...] += _colsum8(dn_scaled * n)
                dn = dn_scaled * g
                draw = rstd * (dn - n * (_head_sum(dn * n, bdv) * (1.0 / HEAD_DIM)))
                dz_ref[sec, rr, :] = draw.astype(bf16)

            one(qraw_ref[rr, :], dq_ref[rr, :] * (HEAD_DIM ** -0.5), qg_ref[...], dqg_ref, 0)
            one(kraw_ref[rr, :], dk_ref[rr, :], kg_ref[...], dkg_ref, 1)
            dz_ref[2, rr, :] = dv_ref[rr, :].astype(bf16)
            dz_ref[3, rr, :] = jnp.zeros((256, 128), bf16)
            return carry

        lax.fori_loop(0, S // 256, norms, 0, unroll=True)

    blk = pl.BlockSpec((S, 128), lambda hp, b: (b, hp))
    sec = lambda s: pl.BlockSpec((None, S, 128), lambda hp, b: (s, b, hp))
    gain = pl.BlockSpec((1, 128), lambda hp, b: (0, hp))
    row = lambda dt, pad=0: pltpu.VMEM((S + pad, 128), dt)
    blocks = lambda dt: pltpu.VMEM((nblk, 2 * AB, 2 * AB), dt)
    return pl.pallas_call(
        body, name="attn_bwd", grid=(N_HEADS // 2, nb),
        in_specs=[blk, blk, sec(Z_V), blk, blk, blk,
                  pl.BlockSpec((3, 2, 2, AB, 2 * AB), lambda hp, b: (0, hp, 0, 0, 0)),
                  pl.BlockSpec((128, 128), lambda hp, b: (0, 0)), sec(Z_Q), sec(Z_K), gain, gain,
                  pl.BlockSpec(memory_space=pl.ANY)],
        out_specs=[pl.BlockSpec((4, S, 128), lambda hp, b: (1, b, hp)),
                   pl.BlockSpec((8, 128), lambda hp, b: (0, hp)), pl.BlockSpec((8, 128), lambda hp, b: (0, hp))],
        out_shape=[jax.ShapeDtypeStruct(dz8.shape, bf16), jax.ShapeDtypeStruct((8, D), f32),
                   jax.ShapeDtypeStruct((8, D), f32)],
        input_output_aliases={12: 0},
        scratch_shapes=[row(f32), row(f32), row(f32),
                        row(f32), row(bf16), row(bf16, AB), row(bf16, AB), row(bf16), row(f32), row(f32),
                        blocks(f32), blocks(f32), blocks(bf16), blocks(bf16), row(f32), row(f32, AB), row(f32, AB)],
        compiler_params=_cparams(("parallel", "arbitrary")))(qn, kn, z8, do, o, lse, bias, bd, z8, z8, qg, kg, dz8)


def _any_spec():
    return pl.BlockSpec(memory_space=pl.ANY)


def _allgather_rows(shards, n_full):
    n = len(shards)

    def body(*refs):
        ins, outs = refs[:n], refs[n:2 * n]
        send_sems, recv_sems, local_sems = refs[2 * n:]
        x, y, c, me = _my_pos()
        sibling = (x, y, 1 - c)
        chips = [(1 - x, y), (x, 1 - y), (1 - x, 1 - y)]

        def idx(px, py, pc):
            return 4 * px + 2 * py + pc

        def copy(a, k, blk, to, src=None):
            return pltpu.make_async_remote_copy(
                src_ref=outs[a].at[blk] if src is None else src, dst_ref=outs[a].at[blk],
                send_sem=send_sems.at[a, k], recv_sem=recv_sems.at[a, k], device_id=to, device_id_type=MESH)

        mine = [pltpu.make_async_copy(ins[a], outs[a].at[me], local_sems.at[a]) for a in range(n)]
        for cp in mine:
            cp.start()
        first = []
        for a in range(n_full):
            first.append(copy(a, 0, me, sibling, src=ins[a]))
            first += [copy(a, 1 + j, me, (*chip, c), src=ins[a]) for j, chip in enumerate(chips)]
        for cp in first:
            cp.start()
        passed = []
        for a in range(n_full):
            for j, chip in enumerate(chips):
                blk = idx(*chip, c)
                copy(a, 1 + j, blk, (x, y, c)).wait_recv()
                cp = copy(a, 4 + j, blk, sibling)
                cp.start()
                passed.append(cp)
        for a in range(n_full):
            copy(a, 0, idx(x, y, 1 - c), (x, y, c)).wait_recv()
            for j, chip in enumerate(chips):
                copy(a, 4 + j, idx(*chip, 1 - c), (x, y, c)).wait_recv()
        for cp in first + passed:
            cp.wait_send()
        for cp in mine:
            cp.wait()

    return pl.pallas_call(
        body, name="allgather_weights",
        in_specs=[_any_spec()] * n, out_specs=[_any_spec()] * n,
        out_shape=[jax.ShapeDtypeStruct((N_DEV,) + s.shape, s.dtype) for s in shards],
        scratch_shapes=[pltpu.SemaphoreType.DMA((n_full, 7)), pltpu.SemaphoreType.DMA((n_full, 7)),
                        pltpu.SemaphoreType.DMA((n,))],
    )(*shards)


def _peer(x, y, c, k):
    tx = 1 - x if (k >> 2) & 1 else x
    ty = 1 - y if (k >> 1) & 1 else y
    tc = 1 - c if k & 1 else c
    return (tx, ty, tc), 4 * tx + 2 * ty + tc


_PEER_ORDER = (2, 4, 6, 3, 5, 7, 1)


_HBM = pl.BlockSpec(memory_space=pltpu.HBM)
_SEM = pl.BlockSpec(memory_space=pltpu.SEMAPHORE)
_EFFECT = pltpu.SideEffectType.DATAFLOW_SIDE_EFFECTING


def _exchange_copies(srcs, lands, send_sems, recv_sems, gather):
    x, y, c, me = _my_pos()
    copies = []
    for k in _PEER_ORDER:
        tgt, tidx = _peer(x, y, c, k)
        for a in range(len(srcs)):
            copies.append(pltpu.make_async_remote_copy(
                src_ref=srcs[a] if gather else srcs[a].at[tidx], dst_ref=lands[a].at[me],
                send_sem=send_sems.at[7 * a + k - 1], recv_sem=recv_sems.at[7 * a + k - 1],
                device_id=tgt, device_id_type=MESH))
    return copies


def _exchange_start(name, srcs, lands=None, after=None):
    n = len(srcs)
    gather = lands is not None
    if lands is None:
        lands = [lax.empty(g.shape, g.dtype) for g in srcs]
    extra = [] if after is None else [after]

    def body(*refs):
        src_refs, land_refs = refs[:n], refs[n:2 * n]
        send_sems, recv_sems = refs[2 * n + len(extra)], refs[2 * n + len(extra) + 1]
        token = refs[-1]
        for cp in _exchange_copies(src_refs, land_refs, send_sems, recv_sems, gather):
            cp.start()
        token[...] = jnp.zeros_like(token)

    hbm = lambda a: pltpu.with_memory_space_constraint(a, pltpu.HBM)
    outs = pl.pallas_call(
        body, name=name,
        out_shape=(pltpu.SemaphoreType.DMA((7 * n,)), pltpu.SemaphoreType.DMA((7 * n,)),
                   *[pltpu.HBM(g.shape, g.dtype) for g in list(srcs) + list(lands)],
                   jax.ShapeDtypeStruct((8, 128), f32)),
        in_specs=[_HBM] * (2 * n) + [pl.BlockSpec(memory_space=pl.ANY)] * len(extra),
        out_specs=(_SEM, _SEM, *([_HBM] * (2 * n)), pl.BlockSpec(memory_space=pltpu.VMEM)),
        input_output_aliases={i: 2 + i for i in range(2 * n)},
        compiler_params=pltpu.CompilerParams(has_side_effects=_EFFECT),
    )(*[hbm(g) for g in srcs], *[hbm(g) for g in lands], *extra)
    return outs[0], outs[1], list(outs[2:2 + n]), list(outs[2 + n:2 + 2 * n]), outs[-1], gather


def _exchange_wait(name, started, after):
    send_sems, recv_sems, srcs, lands, _, gather = started
    n = len(srcs)
    after = list(after) if isinstance(after, (list, tuple)) else [after]

    def body(*refs):
        src_refs, land_refs = refs[:n], refs[n:2 * n]
        s_sems, r_sems = refs[2 * n], refs[2 * n + 1]
        for cp in _exchange_copies(src_refs, land_refs, s_sems, r_sems, gather):
            cp.wait_send()
            cp.wait_recv()

    outs = pl.pallas_call(
        body, name=name,
        out_shape=tuple(pltpu.HBM(a.shape, a.dtype) for a in list(srcs) + list(lands)),
        in_specs=[_HBM] * (2 * n) + [_SEM, _SEM] + [pl.BlockSpec(memory_space=pl.ANY)] * len(after),
        out_specs=tuple([_HBM] * (2 * n)),
        input_output_aliases={i: i for i in range(2 * n)},
        compiler_params=pltpu.CompilerParams(has_side_effects=_EFFECT),
    )(*srcs, *lands, send_sems, recv_sems, *after)
    return list(outs[:n]), list(outs[n:])


SMALL_ROWS = 128


def _small_start(name, sg, after=None):
    return _exchange_start(name, [sg], [lax.empty((N_DEV,) + sg.shape, f32)], after=after)


def _small_sum(name, me, started, after):
    (own,), (slots,) = _exchange_wait(name + "_wait", started, after)

    def body(me_ref, s_ref, own_ref, out_ref):
        acc = None
        for p in range(N_DEV):
            term = lax.cond(me_ref[0] == p, lambda: own_ref[...], lambda p=p: s_ref[p])
            acc = term if acc is None else acc + term
        out_ref[...] = acc

    return pl.pallas_call(
        body, name=name + "_sum",
        in_specs=[pl.BlockSpec(memory_space=pltpu.SMEM), pl.BlockSpec(memory_space=pltpu.VMEM),
                  pl.BlockSpec(memory_space=pltpu.VMEM)],
        out_specs=pl.BlockSpec(memory_space=pltpu.VMEM),
        out_shape=jax.ShapeDtypeStruct(own.shape, f32))(me, slots, own)


def _adam_math(g, w, m, v):
    m = ADAM_B1 * m + (1.0 - ADAM_B1) * g
    v = ADAM_B2 * v + (1.0 - ADAM_B2) * (g * g)
    m_hat = m / (1.0 - ADAM_B1 ** ADAM_STEP)
    v_hat = v / (1.0 - ADAM_B2 ** ADAM_STEP)
    delta = -ADAM_LR * (m_hat / (jnp.sqrt(v_hat) + ADAM_EPS) + ADAM_WD * w)
    return delta, m, v


def _adam_slots(name, me, slots, own, w, m, v, tr, transposed=False):
    rows = slots.shape[1]

    def body(me_ref, s_ref, own_ref, w_ref, m_ref, v_ref, g_ref, d_ref, nm_ref, nv_ref):
        mine = own_ref[...]
        g = None
        for p in range(N_DEV):
            term = lax.cond(me_ref[0] == p, lambda: mine, lambda p=p: s_ref[p]).astype(f32)
            g = term if g is None else g + term
        if transposed:
            g = g.T
        delta, nm, nv = _adam_math(g, w_ref[...], m_ref[...], v_ref[...])
        g_ref[...] = g
        d_ref[...] = delta
        nm_ref[...] = nm
        nv_ref[...] = nv

    mode = dict(pipeline_mode=pl.Buffered(1)) if rows == tr else {}
    if transposed:
        rs = pl.BlockSpec((D, tr), lambda i, me_ref: (0, i))
        rs_in = pl.BlockSpec((D, tr), lambda i, me_ref: (0, i), **mode)
    else:
        rs = pl.BlockSpec((tr, D), lambda i, me_ref: (i, 0))
        rs_in = pl.BlockSpec((tr, D), lambda i, me_ref: (i, 0), **mode)
    return pl.pallas_call(
        body, name=name,
        grid_spec=pltpu.PrefetchScalarGridSpec(
            num_scalar_prefetch=1, grid=(rows // tr,),
            in_specs=[pl.BlockSpec((N_DEV, tr, D), lambda i, me_ref: (0, i, 0), **mode),
                      pl.BlockSpec((None, tr, D), lambda i, me_ref: (me_ref[0], i, 0), **mode), rs_in, rs_in, rs_in],
            out_specs=[rs] * 4),
        out_shape=[jax.ShapeDtypeStruct(w.shape, f32)] * 4,
        compiler_params=_cparams(("parallel",)))(me, slots, own, w, m, v)


def _adam_small(g, w, m, v):
    def body(g_ref, w_ref, m_ref, v_ref, d_ref, nm_ref, nv_ref):
        delta, nm, nv = _adam_math(g_ref[...], w_ref[...], m_ref[...], v_ref[...])
        d_ref[...] = delta
        nm_ref[...] = nm
        nv_ref[...] = nv

    return pl.pallas_call(body, name="adam_small", out_shape=[jax.ShapeDtypeStruct(g.shape, f32)] * 3)(g, w, m, v)


FFN_PAD = 6 * D


_SMALL_PARTS = (("norm1_g", 1), ("gate_b", 2), ("conv_w", CONV_WIDTH), ("conv_b", 1), ("conv_norm_g", 1),
                ("q_norm_g", 1), ("k_norm_g", 1), ("norm2_g", 1), ("ffn_conv_w", 18), ("ffn_conv_b", 6), ("last", 1))


def _small_offsets():
    out, row = {}, 0
    for name, rows in _SMALL_PARTS:
        out[name] = row
        row += -(-rows // 8) * 8
    assert row == SMALL_ROWS
    return out


def _pack_small(norm1_g, gate_b, conv_w, conv_b, conv_norm_g, q_norm_g, k_norm_g, norm2_g, ffn_conv_w, ffn_conv_b,
                last_row=None):
    pad_h = lambda a: jnp.pad(a, ((0, 0), (0, D - HEAD_DIM)))
    pad_f = lambda a: jnp.pad(a, ((0, 0), (0, FFN_PAD - 2 * D_FF))).reshape(-1, D)
    parts = [norm1_g, gate_b.reshape(2, D), conv_w, conv_b, conv_norm_g, pad_h(q_norm_g), pad_h(k_norm_g), norm2_g,
             pad_f(ffn_conv_w), pad_f(ffn_conv_b), jnp.zeros((1, D), f32) if last_row is None else last_row]
    return jnp.concatenate([jnp.pad(p, ((0, -p.shape[0] % 8), (0, 0))) for p in parts], axis=0)


def _unpack_small(p):
    o = _small_offsets()
    rows = lambda name, n: p[o[name]:o[name] + n]
    ffn = lambda a: a.reshape(-1, FFN_PAD)[:, :2 * D_FF]
    return dict(
        norm1_g=rows("norm1_g", 1), gate_b=rows("gate_b", 2).reshape(1, 2 * D), conv_w=rows("conv_w", CONV_WIDTH),
        conv_b=rows("conv_b", 1), conv_norm_g=rows("conv_norm_g", 1), q_norm_g=rows("q_norm_g", 1)[:, :HEAD_DIM],
        k_norm_g=rows("k_norm_g", 1)[:, :HEAD_DIM], norm2_g=rows("norm2_g", 1),
        ffn_conv_w=ffn(rows("ffn_conv_w", 18)), ffn_conv_b=ffn(rows("ffn_conv_b", 6)))


_ADAM_TILE = {896: 128, 704: 704, 128: 128, 352: 176}


def kernel(x, norm1_g, w_in, gate_b, conv_w, conv_b, conv_norm_g, w_conv_out, q_norm_g, k_norm_g, w_attn_out, w_out, norm2_g, w_up, ffn_conv_w, ffn_conv_b, w_down, loss_target, m_norm1_g, m_w_in, m_gate_b, m_conv_w, m_conv_b, m_conv_norm_g, m_w_conv_out, m_q_norm_g, m_k_norm_g, m_w_attn_out, m_w_out, m_norm2_g, m_w_up, m_ffn_conv_w, m_ffn_conv_b, m_w_down, v_norm1_g, v_w_in, v_gate_b, v_conv_w, v_conv_b, v_conv_norm_g, v_w_conv_out, v_q_norm_g, v_k_norm_g, v_w_attn_out, v_w_out, v_norm2_g, v_w_up, v_ffn_conv_w, v_ffn_conv_b, v_w_down):
    BL, S, _ = x.shape
    T = BL * S
    me = 4 * lax.axis_index("x") + 2 * lax.axis_index("y") + lax.axis_index("c")
    xt = x.reshape(T, D)
    target = loss_target.reshape(T, D)

    big = dict(w_in=(w_in[0], m_w_in[0], v_w_in[0]), w_up=(w_up[0], m_w_up[0], v_w_up[0]),
               w_conv_out=(w_conv_out[0], m_w_conv_out[0], v_w_conv_out[0]),
               w_attn_out=(w_attn_out[0], m_w_attn_out[0], v_w_attn_out[0]),
               w_out=(w_out[0], m_w_out[0], v_w_out[0]), w_down=(w_down[0], m_w_down[0], v_w_down[0]))
    order = ["w_in", "w_conv_out", "w_attn_out", "w_out", "w_up", "w_down"]
    shards = [(big[n][0].T if n in ("w_in", "w_up") else big[n][0]).astype(bf16) for n in order]
    gathered = _allgather_rows(shards, 1)
    W = {"w_in": gathered[0].reshape(-1, D)}

    def place_cols(shard, full_cols):
        z = jnp.zeros((shard.shape[0], full_cols), f32)
        return lax.dynamic_update_slice(z, shard, (0, me * shard.shape[1]))

    zr = lambda a: jnp.zeros_like(a)
    conv_local = _pack_small(
        zr(norm1_g), zr(gate_b), place_cols(conv_w[0], D), zr(conv_b), zr(conv_norm_g), zr(q_norm_g), zr(k_norm_g),
        zr(norm2_g), place_cols(ffn_conv_w[0], 2 * D_FF), zr(ffn_conv_b))
    ga_conv = _small_start("gather_conv_start", conv_local, after=gathered[0])
    ga_proj = _exchange_start("gather_start_proj", shards[1:4], gathered[1:4], after=ga_conv[4])
    ga_ffn = _exchange_start("gather_start_ffn", shards[4:6], gathered[4:6], after=ga_proj[4])

    bd = (jnp.arange(128)[:, None] // HEAD_DIM == jnp.arange(128)[None, :] // HEAD_DIM).astype(bf16)
    bias = _attn_bias()
    qg = jnp.tile(q_norm_g, (1, N_HEADS))
    kg = jnp.tile(k_norm_g, (1, N_HEADS))

    z8, h, qn, kn = _in_proj_fwd(xt, norm1_g, W["w_in"], qg, kg, bd, ga_ffn[4])
    conv_all = _unpack_small(_small_sum("gather_conv", me.reshape(1), ga_conv, z8))
    conv_w_full, ffn_w_full = conv_all["conv_w"], conv_all["ffn_conv_w"]
    c = _conv_fwd(z8, conv_w_full, conv_b, S)
    o, ob, lse = _attn_fwd(qn, kn, z8, bias, S)
    for n, g in zip(order[1:4], _exchange_wait("gather_wait_proj", ga_proj, ob)[1]):
        W[n] = g.reshape(-1, D)
    s, ya, yb, mixed = _branches_fwd(c, ob, z8, conv_norm_g, gate_b, W["w_conv_out"], W["w_attn_out"])
    x1, h2 = _out_norm2_fwd(mixed, W["w_out"], xt, norm2_g)
    for n, g in zip(order[4:6], _exchange_wait("gather_wait_ffn", ga_ffn, x1)[1]):
        W[n] = g.reshape(-1, D)
    TNU = D_FF // 2
    u3 = _matmul_call(
        "mm_u", h2, W["w_up"],
        pl.BlockSpec((1024, D), lambda i, j, k: (i, 0)),
        pl.BlockSpec((TNU, D), lambda i, j, k: (j, 0)),
        pl.BlockSpec((None, 1024, TNU), lambda i, j, k: (j // 2, i, j % 2)),
        jax.ShapeDtypeStruct((2, T, D_FF), f32), (T // 1024, 4, 1), "nt", 1, 1024, TNU)
    f = _ffn_fwd(u3, ffn_w_full, ffn_conv_b, S)
    dy, dyb, lacc = _down_loss_fwd(f, W["w_down"], x1, target)
    loss_local = 0.5 / D * jnp.sum(lacc)

    df = _matmul("mm_df", dyb, W["w_down"], "nt", f32, tn=TNU)
    g_w_down = _matmul("mm_dwdn", f, dyb, "tn", bf16, tm=TNU)
    du3, dffn = _ffn_bwd(u3, df, ffn_w_full, ffn_conv_b, S)
    g_w_up = _matmul_call(
        "mm_dwup", du3, h2,
        pl.BlockSpec((None, T, TNU), lambda i, j, k: (i // 2, 0, i % 2)),
        pl.BlockSpec((T, D), lambda i, j, k: (0, 0)),
        pl.BlockSpec((TNU, D), lambda i, j, k: (i, 0)),
        jax.ShapeDtypeStruct((2 * D_FF, D), bf16), (4, 1, 1), "tn", 1, TNU, D)
    blocks8 = lambda a: a.reshape(N_DEV, -1, D)
    ex_ffn = _exchange_start("scatter_start_ffn", [blocks8(g_w_up), blocks8(g_w_down)])
    dx1, dx1b, dg_norm2 = _up_norm2_bwd(du3, W["w_up"], x1, dy, norm2_g, ex_ffn[4])
    g_w_out = _matmul("mm_dwo", mixed, dx1b, "tn", bf16, tm=512)
    dz8 = lax.empty((8, T, D), bf16)
    dya, dyb2, dz8, dg_gate = _out_gate_bwd(dx1b, W["w_out"], z8, gate_b, ya, yb, dz8)
    g_w_conv_out = _matmul("mm_dwco", s, dya, "tn", bf16, tm=512)
    g_w_attn_out = _matmul("mm_dwao", ob, dyb2, "tn", bf16, tm=512)
    ex_proj = _exchange_start("scatter_start_proj", [blocks8(g_w_conv_out), blocks8(g_w_attn_out), blocks8(g_w_out)])
    do = _matmul("mm_do", dyb2, W["w_attn_out"], "nt", f32, after=ex_proj[4])
    dc, dg_convnorm = _convnorm_bwd(dya, W["w_conv_out"], c, conv_norm_g)
    dz8a, dconv = _conv_bwd(dc, z8, conv_w_full, dz8, S)
    dz8b, dg_q, dg_k = _attn_bwd(qn, kn, z8, do, o, lse, bias, bd, qg, kg, dz8a, S)
    g_w_in = _matmul_call(
        "mm_dwin", dz8b, h,
        pl.BlockSpec((None, T, D), lambda i, j, k: (jnp.where(i < 2, i, jnp.where(i < 5, i + 2, i - 3)), 0, 0)),
        pl.BlockSpec((T, D), lambda i, j, k: (0, 0)), pl.BlockSpec((1024, D), lambda i, j, k: (i, 0)),
        jax.ShapeDtypeStruct((7 * D, D), bf16), (7, 1, 1), "tn", 1, D, D)
    ex_in = _exchange_start("scatter_start_in", [blocks8(g_w_in)])
    grad_x, dg_norm1 = _in_norm1_bwd(dz8b, W["w_in"], xt, dx1, norm1_g, ex_in[4])

    sum8 = lambda a: a.reshape(-1, 8, a.shape[-1]).sum(axis=1)
    dconv_s = sum8(dconv.sum(axis=0))
    dffn_s = dffn.sum(axis=0).reshape(2, 4, 8, D_FF).sum(axis=2)
    dffn_w = jnp.concatenate([dffn_s[0, :3], dffn_s[1, :3]], axis=1)
    dffn_b = jnp.concatenate([dffn_s[0, 3:4], dffn_s[1, 3:4]], axis=1)
    fold = lambda a: sum8(a).reshape(N_HEADS, HEAD_DIM).sum(axis=0)[None]
    small_g_local = _pack_small(
        sum8(dg_norm1), sum8(dg_gate), dconv_s[:CONV_WIDTH], dconv_s[CONV_WIDTH:], sum8(dg_convnorm),
        fold(dg_q), fold(dg_k), sum8(dg_norm2), dffn_w, dffn_b,
        last_row=jnp.pad(loss_local.reshape(1, 1), ((0, 0), (0, D - 1))))
    sg_start = _small_start("small_grads_start", small_g_local)

    own, slots = {}, {}
    for tag, ex, names_ in (("ffn", ex_ffn, ("w_up", "w_down")),
                            ("proj", ex_proj, ("w_conv_out", "w_attn_out", "w_out")), ("in", ex_in, ("w_in",))):
        sent, landed = _exchange_wait("scatter_wait_" + tag, ex, sg_start[4])
        for n, src, land in zip(names_, sent, landed):
            own[n], slots[n] = src, land

    res, adam_done = {}, []
    for n in order:
        w, m, v = big[n]
        outs = _adam_slots("adam_" + n, me.reshape(1), slots[n], own[n], w, m, v, _ADAM_TILE[slots[n].shape[1]],
                           transposed=n in ("w_in", "w_up"))
        adam_done.append(outs[0])
        res[n] = [a[None] for a in outs]
    small_g = _small_sum("small_grads", me.reshape(1), sg_start, adam_done)
    loss = small_g[_small_offsets()["last"], 0]

    col = lambda a, width: lax.dynamic_slice(a, (0, me * width), (a.shape[0], width))
    small_w_true = _pack_small(norm1_g, gate_b, conv_w_full, conv_b, conv_norm_g, q_norm_g, k_norm_g, norm2_g,
                               ffn_w_full, ffn_conv_b)
    place_m = lambda a, full: place_cols(a[0], full)
    small_m = _pack_small(m_norm1_g, m_gate_b, place_m(m_conv_w, D), m_conv_b, m_conv_norm_g, m_q_norm_g, m_k_norm_g,
                          m_norm2_g, place_m(m_ffn_conv_w, 2 * D_FF), m_ffn_conv_b)
    small_v = _pack_small(v_norm1_g, v_gate_b, place_m(v_conv_w, D), v_conv_b, v_conv_norm_g, v_q_norm_g, v_k_norm_g,
                          v_norm2_g, place_m(v_ffn_conv_w, 2 * D_FF), v_ffn_conv_b)
    sd, sm, sv = _adam_small(small_g, small_w_true, small_m, small_v)
    for i, packed in enumerate((small_g, sd, sm, sv)):
        u = _unpack_small(packed)
        u["conv_w"] = col(u["conv_w"], D // N_DEV)
        u["ffn_conv_w"] = col(u["ffn_conv_w"], 2 * D_FF // N_DEV)
        for n, a in u.items():
            res.setdefault(n, [None] * 4)[i] = a[None] if n in ("conv_w", "ffn_conv_w") else a

    names = ["norm1_g", "w_in", "gate_b", "conv_w", "conv_b", "conv_norm_g", "w_conv_out", "q_norm_g", "k_norm_g",
             "w_attn_out", "w_out", "norm2_g", "w_up", "ffn_conv_w", "ffn_conv_b", "w_down"]
    out = [loss, grad_x.reshape(BL, S, D)]
    for i in range(4):
        out += [res[n][i] for n in names]
    return tuple(out)
```

```python
import functools

import jax
import jax.numpy as jnp
import numpy as np
from jax import lax
from jax.experimental import pallas as pl
from jax.experimental.pallas import tpu as pltpu

f32 = jnp.float32
bf16 = jnp.bfloat16

D = 1024
N_HEADS = 16
HEAD_DIM = 64
CONV_WIDTH = 31
D_FF = 2816
GROUPS = ((128, 1), (512, 4), (2048, 16))
ATTN_BLOCK = 128
EPS = 1e-6
N_DEV = 8
MESH = pl.DeviceIdType.MESH

ADAM_LR = 0.001
ADAM_B1 = 0.9
ADAM_B2 = 0.999
ADAM_EPS = 1e-08
ADAM_WD = 0.01
ADAM_STEP = 10

VMEM_LIMIT = 56 * 1024 * 1024
MASK_BIAS = 1e30

Z_AVAL, Z_AGATE, Z_GA, Z_GB, Z_Q, Z_K, Z_V = 0, 1, 2, 3, 4, 5, 6


_W_OF_Z = (0, 1, 5, 6, 2, 3, 4)


def _wsec_of_zsec(j):
    return jnp.where(j < 2, j, jnp.where(j < 4, j + 3, j - 2))


def _sig(x):
    return 1.0 / (1.0 + jnp.exp(-x))


def _colsum8(x):
    return x.reshape(-1, 8, x.shape[-1]).sum(axis=0)


def _cparams(sem):
    return pltpu.CompilerParams(dimension_semantics=sem, vmem_limit_bytes=VMEM_LIMIT)


def _my_pos():
    x, y, c = lax.axis_index("x"), lax.axis_index("y"), lax.axis_index("c")
    return x, y, c, 4 * x + 2 * y + c


_DIMS = {"nn": ((1,), (0,)), "nt": ((1,), (1,)), "tn": ((0,), (0,))}


def _matmul_call(name, a, b, a_spec, b_spec, o_spec, out_shape, grid, mode, nk, tm, tn, after=None):
    dims = (_DIMS[mode], ((), ()))
    extra = [] if after is None else [after]

    def body(a_ref, b_ref, *rest):
        o_ref, scratch = rest[len(extra)], rest[len(extra) + 1:]
        part = lax.dot_general(a_ref[...], b_ref[...], dims, preferred_element_type=f32)
        if nk == 1:
            o_ref[...] = part.astype(o_ref.dtype)
        else:
            acc = scratch[0]
            k = pl.program_id(2)

            @pl.when(k == 0)
            def _():
                acc[...] = part

            @pl.when(k > 0)
            def _():
                acc[...] += part

            @pl.when(k == nk - 1)
            def _():
                o_ref[...] = acc[...].astype(o_ref.dtype)

    scratch = [] if nk == 1 else [pltpu.VMEM((tm, tn), f32)]
    return pl.pallas_call(
        body, name=name, grid=grid, in_specs=[a_spec, b_spec] + [pl.BlockSpec(memory_space=pl.ANY)] * len(extra),
        out_specs=o_spec, out_shape=out_shape,
        scratch_shapes=scratch, compiler_params=_cparams(("parallel", "parallel", "arbitrary")),
    )(a, b, *extra)


def _matmul(name, a, b, mode, out_dtype, tm=1024, tn=1024, tk=None, after=None):
    if mode == "nn":
        (M, K), (_, N) = a.shape, b.shape
    elif mode == "nt":
        (M, K), (N, _) = a.shape, b.shape
    else:
        (K, M), (_, N) = a.shape, b.shape
    tm, tn = min(tm, M), min(tn, N)
    tk = K if tk is None else tk
    nk = K // tk
    assert M % tm == 0 and N % tn == 0 and K % tk == 0
    if mode == "tn":
        a_spec = pl.BlockSpec((tk, tm), lambda i, j, k: (k, i))
    else:
        a_spec = pl.BlockSpec((tm, tk), lambda i, j, k: (i, k))
    if mode == "nt":
        b_spec = pl.BlockSpec((tn, tk), lambda i, j, k: (j, k))
    else:
        b_spec = pl.BlockSpec((tk, tn), lambda i, j, k: (k, j))
    o_spec = pl.BlockSpec((tm, tn), lambda i, j, k: (i, j))
    return _matmul_call(name, a, b, a_spec, b_spec, o_spec, jax.ShapeDtypeStruct((M, N), out_dtype),
                        (M // tm, N // tn, nk), mode, nk, tm, tn, after=after)


FTM = 512


def _matmul_fused(name, a, b, pairs, epilogue, extras, consts, outs, nt=False, sums=False, passed=(), aliases=None):
    sa, M, kk = a.shape
    na = max(i for i, _ in pairs) + 1
    ne, nc, npass = len(extras), len(consts), len(passed)
    dims = (_DIMS["nt" if nt else "nn"], ((), ()))

    def body(a_ref, b_ref, *rest):
        acc = None
        for i, j in pairs:
            part = lax.dot_general(a_ref[i], b_ref[j], dims, preferred_element_type=f32)
            acc = part if acc is None else acc + part
        epilogue(acc, rest[:ne], rest[ne:ne + nc], rest[ne + nc + npass:])

    whole = lambda arr: pl.BlockSpec(arr.shape, lambda i, nd=arr.ndim: (0,) * nd, pipeline_mode=pl.Buffered(1))
    io_alias = {2 + ne + nc + k: v for k, v in (aliases or {}).items()}
    return pl.pallas_call(
        body, name=name, grid=(M // FTM,),
        in_specs=[pl.BlockSpec((na, FTM, kk), lambda i: (0, i, 0)), whole(b)] + [s for _, s in extras]
        + [whole(c) for c in consts] + [pl.BlockSpec(memory_space=pl.ANY)] * npass,
        out_specs=[s for _, s in outs], out_shape=[s for s, _ in outs], input_output_aliases=io_alias,
        compiler_params=_cparams(("arbitrary" if sums else "parallel",)),
    )(a, b, *[x for x, _ in extras], *consts, *passed)


def _frows(c=D):
    return pl.BlockSpec((FTM, c), lambda i: (i, 0))


def _fsec(s):
    return pl.BlockSpec((None, FTM, D), lambda i: (s, i, 0))


def _rowshape(T, dtype, c=D):
    return (jax.ShapeDtypeStruct((T, c), dtype), _frows(c))


def _sumshape(c=D):
    return (jax.ShapeDtypeStruct((8, c), f32), pl.BlockSpec((8, c), lambda i: (0, 0)))


def _add_colsum(ref, x, cols=None):
    @pl.when(pl.program_id(0) == 0)
    def _():
        if cols is None:
            ref[...] = jnp.zeros_like(ref)
        else:
            ref[:, cols] = jnp.zeros((8, x.shape[-1]), f32)

    if cols is None:
        ref[...] += _colsum8(x)
    else:
        ref[:, cols] += _colsum8(x)


def _rms(x):
    return lax.rsqrt(jnp.mean(x * x, axis=-1, keepdims=True) + EPS)


def _rms_bwd(dy_g, xn, rstd):
    return rstd * (dy_g - xn * jnp.mean(dy_g * xn, axis=-1, keepdims=True))


def _head_sum(x, bd):
    parts = []
    for cb in range(x.shape[-1] // 128):
        xb = x[:, cb * 128:(cb + 1) * 128]
        hi = xb.astype(bf16)
        lo = (xb - hi.astype(f32)).astype(bf16)
        parts.append(jnp.dot(hi, bd, preferred_element_type=f32) + jnp.dot(lo, bd, preferred_element_type=f32))
    return parts[0] if len(parts) == 1 else jnp.concatenate(parts, axis=1)


ZTM = 1024


def _in_proj_fwd(x, g, w_in_t, qg, kg, bd, after):
    T = x.shape[0]

    def body(x_ref, g_ref, w_ref, qg_ref, kg_ref, bd_ref, after_ref, z_ref, h_ref, qn_ref, kn_ref, hbuf):
        del after_ref
        j = pl.program_id(1)

        @pl.when(j == 0)
        def _():
            xv = x_ref[...]
            hv = (xv * _rms(xv) * g_ref[...]).astype(bf16)
            hbuf[...] = hv
            h_ref[...] = hv

        z = lax.dot_general(hbuf[...], w_ref[...], (_DIMS["nt"], ((), ())), preferred_element_type=f32)
        z_ref[...] = z

        def head_norm(gain_ref, scale):
            return z * lax.rsqrt(_head_sum(z * z, bd_ref[...]) * (1.0 / HEAD_DIM) + EPS) * gain_ref[...] * scale

        @pl.when(j == Z_Q)
        def _():
            qn_ref[...] = head_norm(qg_ref, HEAD_DIM ** -0.5)

        @pl.when(j == Z_K)
        def _():
            kn_ref[...] = head_norm(kg_ref, 1.0)

    tile = pl.BlockSpec((ZTM, D), lambda i, j: (i, 0))
    row = pl.BlockSpec((1, D), lambda i, j: (0, 0))
    return pl.pallas_call(
        body, name="mm_z", grid=(T // ZTM, 7),
        in_specs=[tile, row, pl.BlockSpec((D, D), lambda i, j: (_wsec_of_zsec(j), 0)), row, row,
                  pl.BlockSpec((128, 128), lambda i, j: (0, 0)), pl.BlockSpec(memory_space=pl.ANY)],
        out_specs=[pl.BlockSpec((None, ZTM, D), lambda i, j: (j, i, 0)), tile, tile, tile],
        out_shape=[jax.ShapeDtypeStruct((8, T, D), f32), jax.ShapeDtypeStruct((T, D), bf16),
                   jax.ShapeDtypeStruct((T, D), f32), jax.ShapeDtypeStruct((T, D), f32)],
        scratch_shapes=[pltpu.VMEM((ZTM, D), bf16)],
        compiler_params=_cparams(("parallel", "arbitrary")))(x, g, w_in_t, qg, kg, bd, after)


def _branches_fwd(c, ob, z8, g, gate_b, w_conv_out, w_attn_out):
    T = c.shape[0]

    def epilogue(yb, extra, const, out):
        cv = extra[0][...]
        r = cv * _rms(cv) * const[0][...]
        s = (r * _sig(r)).astype(bf16)
        ya = jnp.dot(s, const[2][...], preferred_element_type=f32)
        b_ref = const[1]
        g_a = _sig(extra[1][...] + b_ref[:, :D])
        g_b = _sig(extra[2][...] + b_ref[:, D:])
        out[0][...] = s
        out[1][...] = ya
        out[2][...] = yb
        out[3][...] = (g_a * ya + g_b * yb).astype(bf16)

    return _matmul_fused("mm_branches", ob[None], w_attn_out[None], ((0, 0),), epilogue,
                         [(c, _frows()), (z8, _fsec(Z_GA)), (z8, _fsec(Z_GB))], [g, gate_b, w_conv_out],
                         [_rowshape(T, bf16), _rowshape(T, f32), _rowshape(T, f32), _rowshape(T, bf16)])


def _out_norm2_fwd(mixed, w_out, x, g):
    T = x.shape[0]

    def epilogue(acc, extra, const, out):
        x1 = extra[0][...] + acc
        out[0][...] = x1
        out[1][...] = (x1 * _rms(x1) * const[0][...]).astype(bf16)

    return _matmul_fused("mm_t1_norm2", mixed[None], w_out[None], ((0, 0),), epilogue, [(x, _frows())], [g],
                         [_rowshape(T, f32), _rowshape(T, bf16)])


def _down_loss_fwd(f, w_down, x1, target):
    T = x1.shape[0]

    def epilogue(acc, extra, const, out):
        diff = extra[0][...] + acc - extra[1][...]
        dy = diff * (1.0 / D)
        out[0][...] = dy
        out[1][...] = dy.astype(bf16)
        _add_colsum(out[2], diff * diff)

    return _matmul_fused("mm_t2_loss", f[None], w_down[None], ((0, 0),), epilogue, [(x1, _frows()), (target, _frows())],
                         [], [_rowshape(T, f32), _rowshape(T, bf16), _sumshape()], sums=True)


def _up_norm2_bwd(du3, w_up_t, x1, dy, g, token):
    T = x1.shape[0]

    def epilogue(dh, extra, const, out):
        x1v = extra[0][...]
        rstd = _rms(x1v)
        xn = x1v * rstd
        dx1 = extra[1][...] + _rms_bwd(dh * const[0][...], xn, rstd)
        out[0][...] = dx1
        out[1][...] = dx1.astype(bf16)
        _add_colsum(out[2], dh * xn)

    return _matmul_fused("mm_dh2_norm2", du3, w_up_t.reshape(2, D_FF, D), ((0, 0), (1, 1)), epilogue,
                         [(x1, _frows()), (dy, _frows())], [g],
                         [_rowshape(T, f32), _rowshape(T, bf16), _sumshape()], sums=True, passed=[token])


def _out_gate_bwd(dx1b, w_out, z8, gate_b, ya, yb, dz8):
    T = ya.shape[0]

    def epilogue(dm, extra, const, out):
        b_ref = const[0]
        g_a = _sig(extra[0][...] + b_ref[:, :D])
        g_b = _sig(extra[1][...] + b_ref[:, D:])
        out[0][...] = (dm * g_a).astype(bf16)
        out[1][...] = (dm * g_b).astype(bf16)
        dla = dm * extra[2][...] * g_a * (1.0 - g_a)
        dlb = dm * extra[3][...] * g_b * (1.0 - g_b)
        out[2][0] = dla.astype(bf16)
        out[2][1] = dlb.astype(bf16)
        _add_colsum(out[3], dla, slice(0, D))
        _add_colsum(out[3], dlb, slice(D, 2 * D))

    return _matmul_fused(
        "mm_dmixed_gate", dx1b[None], w_out[None], ((0, 0),), epilogue,
        [(z8, _fsec(Z_GA)), (z8, _fsec(Z_GB)), (ya, _frows()), (yb, _frows())], [gate_b],
        [_rowshape(T, bf16), _rowshape(T, bf16),
         (jax.ShapeDtypeStruct(dz8.shape, bf16), pl.BlockSpec((2, FTM, D), lambda i: (1, i, 0))), _sumshape(2 * D)],
        nt=True, sums=True, passed=[dz8], aliases={0: 2})


def _convnorm_bwd(dya, w_conv_out, c, g):
    T = c.shape[0]

    def epilogue(ds, extra, const, out):
        cv = extra[0][...]
        rstd = _rms(cv)
        r0 = cv * rstd
        gv = const[0][...]
        r = r0 * gv
        sg = _sig(r)
        dr = ds * sg * (1.0 + r * (1.0 - sg))
        out[0][...] = _rms_bwd(dr * gv, r0, rstd)
        _add_colsum(out[1], dr * r0)

    return _matmul_fused("mm_ds_convnorm", dya[None], w_conv_out[None], ((0, 0),), epilogue, [(c, _frows())], [g],
                         [_rowshape(T, f32), _sumshape()], nt=True, sums=True)


def _in_norm1_bwd(dz8, w_in_t, x, dx1, g, token):
    T = x.shape[0]

    def epilogue(dh, extra, const, out):
        xv = extra[0][...]
        rstd = _rms(xv)
        xn = xv * rstd
        out[0][...] = extra[1][...] + _rms_bwd(dh * const[0][...], xn, rstd)
        _add_colsum(out[1], dh * xn)

    return _matmul_fused("mm_dh_norm1", dz8, w_in_t.reshape(7, D, D), tuple(zip(range(7), _W_OF_Z)), epilogue,
                         [(x, _frows()), (dx1, _frows())], [g], [_rowshape(T, f32), _sumshape()],
                         sums=True, passed=[token])


CCW = 256
CR = 64
HALO = 32


def _conv_fwd(z8, conv_w, conv_b, S):
    T = z8.shape[1]
    nb = T // S
    ncb = D // CCW

    def body(av_ref, ag_ref, w_ref, b_ref, c_ref, pad):
        pad[0:HALO, :] = jnp.zeros((HALO, CCW), f32)

        def fill(i, carry):
            r0 = pl.multiple_of(i * 256, 256)
            pad[pl.ds(HALO + r0, 256), :] = av_ref[pl.ds(r0, 256), :] * _sig(ag_ref[pl.ds(r0, 256), :])
            return carry

        lax.fori_loop(0, S // 256, fill, 0)
        bias = b_ref[...]

        def chunk(i, carry):
            r0 = pl.multiple_of(i * CR, CR)
            win = pad[pl.ds(r0, CR + HALO), :]
            acc = jnp.zeros((CR, CCW), f32) + bias
            for s in range(8):
                part = None
                for m in range((CONV_WIDTH - 1 - s) // 8 + 1):
                    j = CONV_WIDTH - 1 - 8 * m - s
                    term = win[24 - 8 * m:24 - 8 * m + CR + 8, :] * w_ref[j:j + 1, :]
                    part = term if part is None else part + term
                acc = acc + part[8 - s:8 - s + CR, :]
            c_ref[pl.ds(r0, CR), :] = acc
            return carry

        lax.fori_loop(0, S // CR, chunk, 0)

    zs = lambda s: pl.BlockSpec((None, S, CCW), lambda b, cb: (s, b, cb))
    return pl.pallas_call(
        body, name="conv_fwd", grid=(nb, ncb),
        in_specs=[zs(Z_AVAL), zs(Z_AGATE), pl.BlockSpec((CONV_WIDTH, CCW), lambda b, cb: (0, cb)),
                  pl.BlockSpec((1, CCW), lambda b, cb: (0, cb))],
        out_specs=pl.BlockSpec((S, CCW), lambda b, cb: (b, cb)),
        out_shape=jax.ShapeDtypeStruct((T, D), f32),
        scratch_shapes=[pltpu.VMEM((S + HALO, CCW), f32)],
        compiler_params=_cparams(("parallel", "parallel")))(z8, z8, conv_w, conv_b)


def _conv_bwd(dc, z8, conv_w, dz8, S):
    T = dc.shape[0]
    nb = T // S
    ncb = D // CCW

    def body(dc_ref, av_ref, ag_ref, w_ref, dz_in, dz_ref, dw_ref, apad, dpad, shbuf):
        del dz_in
        apad[0:HALO, :] = jnp.zeros((HALO, CCW), f32)
        dpad[S:S + HALO, :] = jnp.zeros((HALO, CCW), f32)
        dw_ref[...] = jnp.zeros_like(dw_ref)

        def fill(i, carry):
            r0 = pl.multiple_of(i * 256, 256)
            apad[pl.ds(HALO + r0, 256), :] = av_ref[pl.ds(r0, 256), :] * _sig(ag_ref[pl.ds(r0, 256), :])
            dpad[pl.ds(r0, 256), :] = dc_ref[pl.ds(r0, 256), :]
            return carry

        lax.fori_loop(0, S // 256, fill, 0)

        def chunk(i, carry):
            r0 = pl.multiple_of(i * CR, CR)
            dwin = dpad[pl.ds(r0, CR + HALO), :]
            da = jnp.zeros((CR, CCW), f32)
            for s in range(8):
                shbuf[...] = dwin[s:s + CR, :]
                dshift = shbuf[...]
                part = None
                for m in range((CONV_WIDTH - 1 - s) // 8 + 1):
                    j = CONV_WIDTH - 1 - 8 * m - s
                    term = dwin[8 * m:8 * m + CR + 8, :] * w_ref[j:j + 1, :]
                    part = term if part is None else part + term
                    a_lag = apad[pl.ds(r0 + HALO - 8 * m, CR), :]
                    dw_ref[8 * j:8 * j + 8, :] += _colsum8(dshift * a_lag)
                da = da + part[s:s + CR, :]
            dw_ref[8 * CONV_WIDTH:8 * CONV_WIDTH + 8, :] += _colsum8(dwin[0:CR, :])
            av = av_ref[pl.ds(r0, CR), :]
            sg = _sig(ag_ref[pl.ds(r0, CR), :])
            dz_ref[0, pl.ds(r0, CR), :] = (da * sg).astype(bf16)
            dz_ref[1, pl.ds(r0, CR), :] = (da * av * sg * (1.0 - sg)).astype(bf16)
            return carry

        lax.fori_loop(0, S // CR, chunk, 0)

    zs = lambda s: pl.BlockSpec((None, S, CCW), lambda b, cb: (s, b, cb))
    return pl.pallas_call(
        body, name="conv_bwd", grid=(nb, ncb),
        in_specs=[pl.BlockSpec((S, CCW), lambda b, cb: (b, cb)), zs(Z_AVAL), zs(Z_AGATE),
                  pl.BlockSpec((CONV_WIDTH, CCW), lambda b, cb: (0, cb)), pl.BlockSpec(memory_space=pl.ANY)],
        out_specs=[pl.BlockSpec((2, S, CCW), lambda b, cb: (0, b, cb)),
                   pl.BlockSpec((None, 256, CCW), lambda b, cb: (b, 0, cb))],
        out_shape=[jax.ShapeDtypeStruct(dz8.shape, bf16), jax.ShapeDtypeStruct((nb, 256, D), f32)],
        input_output_aliases={4: 0},
        scratch_shapes=[pltpu.VMEM((S + HALO, CCW), f32), pltpu.VMEM((S + HALO, CCW), f32),
                        pltpu.VMEM((CR, CCW), f32)],
        compiler_params=_cparams(("parallel", "parallel")))(dc, z8, z8, conv_w, dz8)


FR = 128
NFB = D_FF // CCW
FBW = 128


def _ffn_window(ref, i, r0):
    return ref[pl.ds(r0 - 8, FR + 8), :]


def _ffn_u(win, w_ref, b_ref):
    return (win[6:6 + FR, :] * w_ref[0:1, :] + win[7:7 + FR, :] * w_ref[1:2, :]
            + win[8:8 + FR, :] * w_ref[2:3, :] + b_ref[...])


def _ffn_fwd(u3, ffn_w, ffn_b, S):
    T = u3.shape[1]
    nb = T // S

    def body(uv_ref, ug_ref, wv_ref, wg_ref, bv_ref, bg_ref, f_ref):
        def chunk(first, i):
            r0 = 0 if first else pl.multiple_of(i * FR, FR)
            if first:
                z = jnp.zeros((8, CCW), f32)
                wv = jnp.concatenate([z, uv_ref[0:FR, :]], axis=0)
                wg = jnp.concatenate([z, ug_ref[0:FR, :]], axis=0)
            else:
                wv = _ffn_window(uv_ref, i, r0)
                wg = _ffn_window(ug_ref, i, r0)
            u_val = _ffn_u(wv, wv_ref, bv_ref)
            u_gate = _ffn_u(wg, wg_ref, bg_ref)
            f_ref[pl.ds(r0, FR), :] = (u_gate * _sig(u_gate) * u_val).astype(bf16)

        chunk(True, 0)

        def loop(i, carry):
            chunk(False, i)
            return carry

        lax.fori_loop(1, S // FR, loop, 0)

    us = lambda h: pl.BlockSpec((None, S, CCW), lambda b, cb: (h, b, cb))
    ws = lambda h: pl.BlockSpec((3, CCW), lambda b, cb: (0, h * NFB + cb))
    bs = lambda h: pl.BlockSpec((1, CCW), lambda b, cb: (0, h * NFB + cb))
    return pl.pallas_call(
        body, name="ffn_fwd", grid=(nb, NFB),
        in_specs=[us(0), us(1), ws(0), ws(1), bs(0), bs(1)],
        out_specs=pl.BlockSpec((S, CCW), lambda b, cb: (b, cb)),
        out_shape=jax.ShapeDtypeStruct((T, D_FF), bf16),
        compiler_params=_cparams(("parallel", "parallel")))(u3, u3, ffn_w, ffn_w, ffn_b, ffn_b)


def _ffn_bwd(u3, df, ffn_w, ffn_b, S):
    T = u3.shape[1]
    nb = T // S

    def body(uv_ref, ug_ref, df_ref, wv_ref, wg_ref, bv_ref, bg_ref, du_ref, dw_ref, dvpad, dgpad, shbuf):
        dvpad[S:S + 8, :] = jnp.zeros((8, FBW), f32)
        dgpad[S:S + 8, :] = jnp.zeros((8, FBW), f32)
        dw_ref[...] = jnp.zeros_like(dw_ref)

        def chunk(first, i):
            r0 = 0 if first else pl.multiple_of(i * FR, FR)
            if first:
                z = jnp.zeros((8, FBW), f32)
                wv = jnp.concatenate([z, uv_ref[0:FR, :]], axis=0)
                wg = jnp.concatenate([z, ug_ref[0:FR, :]], axis=0)
            else:
                wv = _ffn_window(uv_ref, i, r0)
                wg = _ffn_window(ug_ref, i, r0)
            taps = []
            for h, win in enumerate((wv, wg)):
                shbuf[2 * h] = win[6:6 + FR, :]
                shbuf[2 * h + 1] = win[7:7 + FR, :]
                taps.append((shbuf[2 * h], shbuf[2 * h + 1], win[8:8 + FR, :]))
            conv = lambda x, w_ref, b_ref: (x[0] * w_ref[0:1, :] + x[1] * w_ref[1:2, :] + x[2] * w_ref[2:3, :]
                                            + b_ref[...])
            u_val = conv(taps[0], wv_ref, bv_ref)
            u_gate = conv(taps[1], wg_ref, bg_ref)
            dfc = df_ref[pl.ds(r0, FR), :]
            sg = _sig(u_gate)
            d_val = dfc * u_gate * sg
            d_gate = dfc * u_val * sg * (1.0 + u_gate * (1.0 - sg))
            dvpad[pl.ds(r0, FR), :] = d_val
            dgpad[pl.ds(r0, FR), :] = d_gate
            for h, dd in enumerate((d_val, d_gate)):
                for j in range(3):
                    dw_ref[h, 8 * j:8 * j + 8, :] += _colsum8(dd * taps[h][j])
                dw_ref[h, 24:32, :] += _colsum8(dd)

        chunk(True, 0)

        def loop(i, carry):
            chunk(False, i)
            return carry

        lax.fori_loop(1, S // FR, loop, 0)

        def back(i, carry):
            r0 = pl.multiple_of(i * FR, FR)
            for h, (dpad, w_ref) in enumerate(((dvpad, wv_ref), (dgpad, wg_ref))):
                win = dpad[pl.ds(r0, FR + 8), :]
                du = (win[0:FR, :] * w_ref[2:3, :] + win[1:1 + FR, :] * w_ref[1:2, :]
                      + win[2:2 + FR, :] * w_ref[0:1, :])
                du_ref[h, pl.ds(r0, FR), :] = du.astype(bf16)
            return carry

        lax.fori_loop(0, S // FR, back, 0)

    ncb = D_FF // FBW
    us = lambda h: pl.BlockSpec((None, S, FBW), lambda b, cb: (h, b, cb))
    ws = lambda h: pl.BlockSpec((3, FBW), lambda b, cb: (0, h * ncb + cb))
    bs = lambda h: pl.BlockSpec((1, FBW), lambda b, cb: (0, h * ncb + cb))
    return pl.pallas_call(
        body, name="ffn_bwd", grid=(nb, ncb),
        in_specs=[us(0), us(1), pl.BlockSpec((S, FBW), lambda b, cb: (b, cb)), ws(0), ws(1), bs(0), bs(1)],
        out_specs=[pl.BlockSpec((2, S, FBW), lambda b, cb: (0, b, cb)),
                   pl.BlockSpec((None, 2, 32, FBW), lambda b, cb: (b, 0, 0, cb))],
        out_shape=[jax.ShapeDtypeStruct((2, T, D_FF), bf16), jax.ShapeDtypeStruct((nb, 2, 32, D_FF), f32)],
        scratch_shapes=[pltpu.VMEM((S + 8, FBW), f32), pltpu.VMEM((S + 8, FBW), f32),
                        pltpu.VMEM((4, FR, FBW), f32)],
        compiler_params=_cparams(("parallel", "parallel")))(u3, u3, df, ffn_w, ffn_w, ffn_b, ffn_b)


AB = ATTN_BLOCK


def _attn_bias_np():
    slopes = (np.float32(2.0) ** (np.float32(-8.0) * np.arange(1, N_HEADS + 1, dtype=np.float32)
                                  / np.float32(N_HEADS))).astype(np.float32)
    steps = (np.arange(AB)[:, None] + AB) - np.arange(2 * AB)[None, :]
    own = (np.arange(2 * AB) >= AB)[None, :]
    out = []
    for window, dil in GROUPS:
        valid = (steps >= 0) & (steps <= window // dil)
        dist = slopes[:, None, None] * (steps * dil).astype(np.float32)[None]
        kinds = [np.where(v[None], dist, np.float32(MASK_BIAS)) for v in (valid, valid & own)]
        out.append(np.stack(kinds, axis=1))
    return np.stack(out).astype(np.float32)


def _attn_bias():
    return jnp.asarray(_attn_bias_np())


def _head_masks():
    lane = lax.broadcasted_iota(jnp.int32, (1, 128), 1)
    return (lane < HEAD_DIM, lane >= HEAD_DIM)


def _perm_chunks(S, d):
    L = S // d
    ch = min(L, 256)
    out = []
    for r in range(d):
        for c in range(L // ch):
            start = r + d * ch * c
            out.append((pl.ds(start, ch, stride=d) if d > 1 else pl.ds(start, ch), r * L + c * ch, ch))
    return out


def _stack_heads(x, masks):
    return jnp.concatenate([jnp.where(masks[0], x, 0), jnp.where(masks[1], x, 0)], axis=0)


def _block_row(j):
    return j * AB if isinstance(j, int) else pl.multiple_of(j * AB, AB)


def _three_stages(n, stage_a, stage_b, stage_c, unroll):
    stage_a(0)
    stage_a(1)
    stage_b(0)

    def body(j, carry):
        stage_c(j - 1)
        stage_b(j)
        stage_a(j + 1)
        return carry

    lax.fori_loop(1, n - 1, body, 0, unroll=unroll)
    stage_c(n - 2)
    stage_b(n - 1)
    stage_c(n - 1)


_NT = (((1,), (1,)), ((), ()))
_TN = (((0,), (0,)), ((), ()))
SCH = 64


def _attn_fwd(qn, kn, z8, bias, S):
    T = qn.shape[0]
    nb = T // S
    nblk = S // AB

    def body(q_ref, k_ref, v_ref, bias_ref, o_ref, ob_ref, lse_ref, qs, ks, vs, s2, p2, ogp, lgp, *group_scratch):
        og, lg = group_scratch[:3], group_scratch[3:]
        masks = _head_masks()
        ks[0:AB, :] = jnp.zeros((AB, 128), bf16)
        vs[0:AB, :] = jnp.zeros((AB, 128), bf16)

        for g, (_, d) in enumerate(GROUPS):
            nsub = S // (d * AB)
            chunks = _perm_chunks(S, d)
            for src, dst, ch in chunks:
                qs[dst:dst + ch, :] = q_ref[src, :].astype(bf16)
                ks[AB + dst:AB + dst + ch, :] = k_ref[src, :].astype(bf16)
                vs[AB + dst:AB + dst + ch, :] = v_ref[src, :].astype(bf16)
            od, ld = (og[g], lg[g]) if d == 1 else (ogp, lgp)

            def scores(j):
                r0 = _block_row(j)
                q2 = _stack_heads(qs[pl.ds(r0, AB), :], masks)
                s2[j] = lax.dot_general(q2, ks[pl.ds(r0, 2 * AB), :], _NT, preferred_element_type=f32)

            def softmax(j, g=g, nsub=nsub, ld=ld):
                r0 = _block_row(j)
                kind = int(j % nsub == 0) if isinstance(j, int) else (j % nsub == 0).astype(jnp.int32)
                for cc in range(AB // SCH):
                    lses = []
                    for hh in range(2):
                        rows = pl.ds(hh * AB + cc * SCH, SCH)
                        sb = s2[j, rows, :] - bias_ref[g, hh, kind, cc * SCH:(cc + 1) * SCH, :]
                        m = jnp.max(sb, axis=-1, keepdims=True)
                        p = jnp.exp(sb - m)
                        den = jnp.sum(p, axis=-1, keepdims=True)
                        p2[j, rows, :] = (p * (1.0 / den)).astype(bf16)
                        lses.append(m + jnp.log(den))
                    ld[pl.ds(r0 + cc * SCH, SCH), :] = jnp.where(masks[0], lses[0], lses[1])

            def values(j, od=od):
                r0 = _block_row(j)
                pv2 = jnp.dot(p2[j], vs[pl.ds(r0, 2 * AB), :], preferred_element_type=f32)
                od[pl.ds(r0, AB), :] = jnp.where(masks[0], pv2[:AB], pv2[AB:])

            _three_stages(nblk, scores, softmax, values, nblk - 2)

            if d > 1:
                for src, dst, ch in chunks:
                    og[g][src, :] = ogp[dst:dst + ch, :]
                    lg[g][src, :] = lgp[dst:dst + ch, :]

        def combine(i, carry):
            rr = pl.ds(pl.multiple_of(i * 256, 256), 256)
            l0, l1, l2 = lg[0][rr, :], lg[1][rr, :], lg[2][rr, :]
            mx = jnp.maximum(jnp.maximum(l0, l1), l2)
            e0, e1, e2 = jnp.exp(l0 - mx), jnp.exp(l1 - mx), jnp.exp(l2 - mx)
            den = e0 + e1 + e2
            o = (e0 * og[0][rr, :] + e1 * og[1][rr, :] + e2 * og[2][rr, :]) / den
            o_ref[rr, :] = o
            ob_ref[rr, :] = o.astype(bf16)
            lse_ref[rr, :] = mx + jnp.log(den)
            return carry

        lax.fori_loop(0, S // 256, combine, 0, unroll=True)

    blk = pl.BlockSpec((S, 128), lambda b, hp: (b, hp))
    return pl.pallas_call(
        body, name="attn_fwd", grid=(nb, N_HEADS // 2),
        in_specs=[blk, blk, pl.BlockSpec((None, S, 128), lambda b, hp: (Z_V, b, hp)),
                  pl.BlockSpec((3, 2, 2, AB, 2 * AB), lambda b, hp: (0, hp, 0, 0, 0))],
        out_specs=[blk, blk, blk],
        out_shape=[jax.ShapeDtypeStruct((T, D), f32), jax.ShapeDtypeStruct((T, D), bf16),
                   jax.ShapeDtypeStruct((T, D), f32)],
        scratch_shapes=[pltpu.VMEM((S, 128), bf16), pltpu.VMEM((S + AB, 128), bf16), pltpu.VMEM((S + AB, 128), bf16),
                        pltpu.VMEM((nblk, 2 * AB, 2 * AB), f32), pltpu.VMEM((nblk, 2 * AB, 2 * AB), bf16),
                        pltpu.VMEM((S, 128), f32), pltpu.VMEM((S, 128), f32)] + [pltpu.VMEM((S, 128), f32)] * 6,
        compiler_params=_cparams(("parallel", "parallel")))(qn, kn, z8, bias)


def _attn_bwd(qn, kn, z8, do, o, lse, bias, bd, qg, kg, dz8, S):
    T = qn.shape[0]
    nb = T // S

    nblk = S // AB

    def body(q_ref, k_ref, v_ref, do_ref, o_ref, lse_ref, bias_ref, bd_ref, qraw_ref, kraw_ref, qg_ref, kg_ref,
             dz_in, dz_ref, dqg_ref, dkg_ref,
             dq_ref, dk_ref, dv_ref, delta, qs, ks, vs, dos, lsp, dlp, s2, dp2, p2, ds2, dqp, dkp, dvp):
        del dz_in
        masks = _head_masks()
        bdv = bd_ref[...]
        dq_ref[...] = jnp.zeros_like(dq_ref)
        dk_ref[...] = jnp.zeros_like(dk_ref)
        dv_ref[...] = jnp.zeros_like(dv_ref)
        ks[0:AB, :] = jnp.zeros((AB, 128), bf16)
        vs[0:AB, :] = jnp.zeros((AB, 128), bf16)

        def prep(i, carry):
            rr = pl.ds(pl.multiple_of(i * 256, 256), 256)
            delta[rr, :] = _head_sum(do_ref[rr, :] * o_ref[rr, :], bdv)
            return carry

        lax.fori_loop(0, S // 256, prep, 0, unroll=True)

        for g, (_, d) in enumerate(GROUPS):
            nsub = S // (d * AB)
            chunks = _perm_chunks(S, d)
            for src, dst, ch in chunks:
                qs[dst:dst + ch, :] = q_ref[src, :].astype(bf16)
                ks[AB + dst:AB + dst + ch, :] = k_ref[src, :].astype(bf16)
                vs[AB + dst:AB + dst + ch, :] = v_ref[src, :].astype(bf16)
                dos[dst:dst + ch, :] = do_ref[src, :].astype(bf16)
                lsp[dst:dst + ch, :] = lse_ref[src, :]
                dlp[dst:dst + ch, :] = delta[src, :]
            dkp[...] = jnp.zeros_like(dkp)
            dvp[...] = jnp.zeros_like(dvp)

            def scores(j):
                r0 = _block_row(j)
                q2 = _stack_heads(qs[pl.ds(r0, AB), :], masks)
                do2 = _stack_heads(dos[pl.ds(r0, AB), :], masks)
                s2[j] = lax.dot_general(q2, ks[pl.ds(r0, 2 * AB), :], _NT, preferred_element_type=f32)
                dp2[j] = lax.dot_general(do2, vs[pl.ds(r0, 2 * AB), :], _NT, preferred_element_type=f32)

            def probs(j, g=g, nsub=nsub):
                r0 = _block_row(j)
                kind = int(j % nsub == 0) if isinstance(j, int) else (j % nsub == 0).astype(jnp.int32)
                for cc in range(AB // SCH):
                    lse_c = lsp[pl.ds(r0 + cc * SCH, SCH), :]
                    del_c = dlp[pl.ds(r0 + cc * SCH, SCH), :]
                    for hh in range(2):
                        c0 = hh * HEAD_DIM
                        rows = pl.ds(hh * AB + cc * SCH, SCH)
                        sb = s2[j, rows, :] - bias_ref[g, hh, kind, cc * SCH:(cc + 1) * SCH, :]
                        p = jnp.exp(sb - lse_c[:, c0:c0 + 1])
                        p2[j, rows, :] = p.astype(bf16)
                        ds2[j, rows, :] = (p * (dp2[j, rows, :] - del_c[:, c0:c0 + 1])).astype(bf16)

            def grads(j):
                r0 = _block_row(j)
                q2 = _stack_heads(qs[pl.ds(r0, AB), :], masks)
                do2 = _stack_heads(dos[pl.ds(r0, AB), :], masks)
                dsb = ds2[j]
                t = jnp.dot(dsb, ks[pl.ds(r0, 2 * AB), :], preferred_element_type=f32)
                dqp[pl.ds(r0, AB), :] = jnp.where(masks[0], t[:AB], t[AB:])
                dkp[pl.ds(r0, 2 * AB), :] += lax.dot_general(dsb, q2, _TN, preferred_element_type=f32)
                dvp[pl.ds(r0, 2 * AB), :] += lax.dot_general(p2[j], do2, _TN, preferred_element_type=f32)

            _three_stages(nblk, scores, probs, grads, nblk - 2)

            for src, dst, ch in chunks:
                dq_ref[src, :] += dqp[dst:dst + ch, :]
                dk_ref[src, :] += dkp[AB + dst:AB + dst + ch, :]
                dv_ref[src, :] += dvp[AB + dst:AB + dst + ch, :]

        @pl.when(pl.program_id(1) == 0)
        def _():
            dqg_ref[...] = jnp.zeros_like(dqg_ref)
            dkg_ref[...] = jnp.zeros_like(dkg_ref)

        def norms(i, carry):
            rr = pl.ds(pl.multiple_of(i * 256, 256), 256)

            def one(raw, dn_scaled, g, dg_ref, sec):
                rstd = lax.rsqrt(_head_sum(raw * raw, bdv) * (1.0 / HEAD_DIM) + EPS)
                n = raw * rstd
                dg_ref[...] += _colsum8(dn_scaled * n)
                dn = dn_scaled * g
                draw = rstd * (dn - n * (_head_sum(dn * n, bdv) * (1.0 / HEAD_DIM)))
                dz_ref[sec, rr, :] = draw.astype(bf16)

            one(qraw_ref[rr, :], dq_ref[rr, :] * (HEAD_DIM ** -0.5), qg_ref[...], dqg_ref, 0)
            one(kraw_ref[rr, :], dk_ref[rr, :], kg_ref[...], dkg_ref, 1)
            dz_ref[2, rr, :] = dv_ref[rr, :].astype(bf16)
            dz_ref[3, rr, :] = jnp.zeros((256, 128), bf16)
            return carry

        lax.fori_loop(0, S // 256, norms, 0, unroll=True)

    blk = pl.BlockSpec((S, 128), lambda hp, b: (b, hp))
    sec = lambda s: pl.BlockSpec((None, S, 128), lambda hp, b: (s, b, hp))
    gain = pl.BlockSpec((1, 128), lambda hp, b: (0, hp))
    row = lambda dt, pad=0: pltpu.VMEM((S + pad, 128), dt)
    blocks = lambda dt: pltpu.VMEM((nblk, 2 * AB, 2 * AB), dt)
    return pl.pallas_call(
        body, name="attn_bwd", grid=(N_HEADS // 2, nb),
        in_specs=[blk, blk, sec(Z_V), blk, blk, blk,
                  pl.BlockSpec((3, 2, 2, AB, 2 * AB), lambda hp, b: (0, hp, 0, 0, 0)),
                  pl.BlockSpec((128, 128), lambda hp, b: (0, 0)), sec(Z_Q), sec(Z_K), gain, gain,
                  pl.BlockSpec(memory_space=pl.ANY)],
        out_specs=[pl.BlockSpec((4, S, 128), lambda hp, b: (1, b, hp)),
                   pl.BlockSpec((8, 128), lambda hp, b: (0, hp)), pl.BlockSpec((8, 128), lambda hp, b: (0, hp))],
        out_shape=[jax.ShapeDtypeStruct(dz8.shape, bf16), jax.ShapeDtypeStruct((8, D), f32),
                   jax.ShapeDtypeStruct((8, D), f32)],
        input_output_aliases={12: 0},
        scratch_shapes=[row(f32), row(f32), row(f32),
                        row(f32), row(bf16), row(bf16, AB), row(bf16, AB), row(bf16), row(f32), row(f32),
                        blocks(f32), blocks(f32), blocks(bf16), blocks(bf16), row(f32), row(f32, AB), row(f32, AB)],
        compiler_params=_cparams(("parallel", "arbitrary")))(qn, kn, z8, do, o, lse, bias, bd, z8, z8, qg, kg, dz8)


def _any_spec():
    return pl.BlockSpec(memory_space=pl.ANY)


def _allgather_rows(shards, n_full):
    n = len(shards)

    def body(*refs):
        ins, outs = refs[:n], refs[n:2 * n]
        send_sems, recv_sems, local_sems = refs[2 * n:]
        x, y, c, me = _my_pos()
        sibling = (x, y, 1 - c)
        chips = [(1 - x, y), (x, 1 - y), (1 - x, 1 - y)]

        def idx(px, py, pc):
            return 4 * px + 2 * py + pc

        def copy(a, k, blk, to, src=None):
            return pltpu.make_async_remote_copy(
                src_ref=outs[a].at[blk] if src is None else src, dst_ref=outs[a].at[blk],
                send_sem=send_sems.at[a, k], recv_sem=recv_sems.at[a, k], device_id=to, device_id_type=MESH)

        mine = [pltpu.make_async_copy(ins[a], outs[a].at[me], local_sems.at[a]) for a in range(n)]
        for cp in mine:
            cp.start()
        first = []
        for a in range(n_full):
            first.append(copy(a, 0, me, sibling, src=ins[a]))
            first += [copy(a, 1 + j, me, (*chip, c), src=ins[a]) for j, chip in enumerate(chips)]
        for cp in first:
            cp.start()
        passed = []
        for a in range(n_full):
            for j, chip in enumerate(chips):
                blk = idx(*chip, c)
                copy(a, 1 + j, blk, (x, y, c)).wait_recv()
                cp = copy(a, 4 + j, blk, sibling)
                cp.start()
                passed.append(cp)
        for a in range(n_full):
            copy(a, 0, idx(x, y, 1 - c), (x, y, c)).wait_recv()
            for j, chip in enumerate(chips):
                copy(a, 4 + j, idx(*chip, 1 - c), (x, y, c)).wait_recv()
        for cp in first + passed:
            cp.wait_send()
        for cp in mine:
            cp.wait()

    return pl.pallas_call(
        body, name="allgather_weights",
        in_specs=[_any_spec()] * n, out_specs=[_any_spec()] * n,
        out_shape=[jax.ShapeDtypeStruct((N_DEV,) + s.shape, s.dtype) for s in shards],
        scratch_shapes=[pltpu.SemaphoreType.DMA((n_full, 7)), pltpu.SemaphoreType.DMA((n_full, 7)),
                        pltpu.SemaphoreType.DMA((n,))],
    )(*shards)


def _peer(x, y, c, k):
    tx = 1 - x if (k >> 2) & 1 else x
    ty = 1 - y if (k >> 1) & 1 else y
    tc = 1 - c if k & 1 else c
    return (tx, ty, tc), 4 * tx + 2 * ty + tc


_PEER_ORDER = (2, 4, 6, 3, 5, 7, 1)


_HBM = pl.BlockSpec(memory_space=pltpu.HBM)
_SEM = pl.BlockSpec(memory_space=pltpu.SEMAPHORE)
_EFFECT = pltpu.SideEffectType.DATAFLOW_SIDE_EFFECTING


def _exchange_copies(srcs, lands, send_sems, recv_sems, gather):
    x, y, c, me = _my_pos()
    copies = []
    for k in _PEER_ORDER:
        tgt, tidx = _peer(x, y, c, k)
        for a in range(len(srcs)):
            copies.append(pltpu.make_async_remote_copy(
                src_ref=srcs[a] if gather else srcs[a].at[tidx], dst_ref=lands[a].at[me],
                send_sem=send_sems.at[7 * a + k - 1], recv_sem=recv_sems.at[7 * a + k - 1],
                device_id=tgt, device_id_type=MESH))
    return copies


def _exchange_start(name, srcs, lands=None, after=None):
    n = len(srcs)
    gather = lands is not None
    if lands is None:
        lands = [lax.empty(g.shape, g.dtype) for g in srcs]
    extra = [] if after is None else [after]

    def body(*refs):
        src_refs, land_refs = refs[:n], refs[n:2 * n]
        send_sems, recv_sems = refs[2 * n + len(extra)], refs[2 * n + len(extra) + 1]
        token = refs[-1]
        for cp in _exchange_copies(src_refs, land_refs, send_sems, recv_sems, gather):
            cp.start()
        token[...] = jnp.zeros_like(token)

    hbm = lambda a: pltpu.with_memory_space_constraint(a, pltpu.HBM)
    outs = pl.pallas_call(
        body, name=name,
        out_shape=(pltpu.SemaphoreType.DMA((7 * n,)), pltpu.SemaphoreType.DMA((7 * n,)),
                   *[pltpu.HBM(g.shape, g.dtype) for g in list(srcs) + list(lands)],
                   jax.ShapeDtypeStruct((8, 128), f32)),
        in_specs=[_HBM] * (2 * n) + [pl.BlockSpec(memory_space=pl.ANY)] * len(extra),
        out_specs=(_SEM, _SEM, *([_HBM] * (2 * n)), pl.BlockSpec(memory_space=pltpu.VMEM)),
        input_output_aliases={i: 2 + i for i in range(2 * n)},
        compiler_params=pltpu.CompilerParams(has_side_effects=_EFFECT),
    )(*[hbm(g) for g in srcs], *[hbm(g) for g in lands], *extra)
    return outs[0], outs[1], list(outs[2:2 + n]), list(outs[2 + n:2 + 2 * n]), outs[-1], gather


def _exchange_wait(name, started, after):
    send_sems, recv_sems, srcs, lands, _, gather = started
    n = len(srcs)
    after = list(after) if isinstance(after, (list, tuple)) else [after]

    def body(*refs):
        src_refs, land_refs = refs[:n], refs[n:2 * n]
        s_sems, r_sems = refs[2 * n], refs[2 * n + 1]
        for cp in _exchange_copies(src_refs, land_refs, s_sems, r_sems, gather):
            cp.wait_send()
            cp.wait_recv()

    outs = pl.pallas_call(
        body, name=name,
        out_shape=tuple(pltpu.HBM(a.shape, a.dtype) for a in list(srcs) + list(lands)),
        in_specs=[_HBM] * (2 * n) + [_SEM, _SEM] + [pl.BlockSpec(memory_space=pl.ANY)] * len(after),
        out_specs=tuple([_HBM] * (2 * n)),
        input_output_aliases={i: i for i in range(2 * n)},
        compiler_params=pltpu.CompilerParams(has_side_effects=_EFFECT),
    )(*srcs, *lands, send_sems, recv_sems, *after)
    return list(outs[:n]), list(outs[n:])


SMALL_ROWS = 128


def _small_start(name, sg, after=None):
    return _exchange_start(name, [sg], [lax.empty((N_DEV,) + sg.shape, f32)], after=after)


def _small_sum(name, me, started, after):
    (own,), (slots,) = _exchange_wait(name + "_wait", started, after)

    def body(me_ref, s_ref, own_ref, out_ref):
        acc = None
        for p in range(N_DEV):
            term = lax.cond(me_ref[0] == p, lambda: own_ref[...], lambda p=p: s_ref[p])
            acc = term if acc is None else acc + term
        out_ref[...] = acc

    return pl.pallas_call(
        body, name=name + "_sum",
        in_specs=[pl.BlockSpec(memory_space=pltpu.SMEM), pl.BlockSpec(memory_space=pltpu.VMEM),
                  pl.BlockSpec(memory_space=pltpu.VMEM)],
        out_specs=pl.BlockSpec(memory_space=pltpu.VMEM),
        out_shape=jax.ShapeDtypeStruct(own.shape, f32))(me, slots, own)


def _adam_math(g, w, m, v):
    m = ADAM_B1 * m + (1.0 - ADAM_B1) * g
    v = ADAM_B2 * v + (1.0 - ADAM_B2) * (g * g)
    m_hat = m / (1.0 - ADAM_B1 ** ADAM_STEP)
    v_hat = v / (1.0 - ADAM_B2 ** ADAM_STEP)
    delta = -ADAM_LR * (m_hat / (jnp.sqrt(v_hat) + ADAM_EPS) + ADAM_WD * w)
    return delta, m, v


def _adam_slots(name, me, slots, own, w, m, v, tr, transposed=False):
    rows = slots.shape[1]

    def body(me_ref, s_ref, own_ref, w_ref, m_ref, v_ref, g_ref, d_ref, nm_ref, nv_ref):
        mine = own_ref[...]
        g = None
        for p in range(N_DEV):
            term = lax.cond(me_ref[0] == p, lambda: mine, lambda p=p: s_ref[p]).astype(f32)
            g = term if g is None else g + term
        if transposed:
            g = g.T
        delta, nm, nv = _adam_math(g, w_ref[...], m_ref[...], v_ref[...])
        g_ref[...] = g
        d_ref[...] = delta
        nm_ref[...] = nm
        nv_ref[...] = nv

    mode = dict(pipeline_mode=pl.Buffered(1)) if rows == tr else {}
    if transposed:
        rs = pl.BlockSpec((D, tr), lambda i, me_ref: (0, i))
        rs_in = pl.BlockSpec((D, tr), lambda i, me_ref: (0, i), **mode)
    else:
        rs = pl.BlockSpec((tr, D), lambda i, me_ref: (i, 0))
        rs_in = pl.BlockSpec((tr, D), lambda i, me_ref: (i, 0), **mode)
    return pl.pallas_call(
        body, name=name,
        grid_spec=pltpu.PrefetchScalarGridSpec(
            num_scalar_prefetch=1, grid=(rows // tr,),
            in_specs=[pl.BlockSpec((N_DEV, tr, D), lambda i, me_ref: (0, i, 0), **mode),
                      pl.BlockSpec((None, tr, D), lambda i, me_ref: (me_ref[0], i, 0), **mode), rs_in, rs_in, rs_in],
            out_specs=[rs] * 4),
        out_shape=[jax.ShapeDtypeStruct(w.shape, f32)] * 4,
        compiler_params=_cparams(("parallel",)))(me, slots, own, w, m, v)


def _adam_small(g, w, m, v):
    def body(g_ref, w_ref, m_ref, v_ref, d_ref, nm_ref, nv_ref):
        delta, nm, nv = _adam_math(g_ref[...], w_ref[...], m_ref[...], v_ref[...])
        d_ref[...] = delta
        nm_ref[...] = nm
        nv_ref[...] = nv

    return pl.pallas_call(body, name="adam_small", out_shape=[jax.ShapeDtypeStruct(g.shape, f32)] * 3)(g, w, m, v)


FFN_PAD = 6 * D


_SMALL_PARTS = (("norm1_g", 1), ("gate_b", 2), ("conv_w", CONV_WIDTH), ("conv_b", 1), ("conv_norm_g", 1),
                ("q_norm_g", 1), ("k_norm_g", 1), ("norm2_g", 1), ("ffn_conv_w", 18), ("ffn_conv_b", 6), ("last", 1))


def _small_offsets():
    out, row = {}, 0
    for name, rows in _SMALL_PARTS:
        out[name] = row
        row += -(-rows // 8) * 8
    assert row == SMALL_ROWS
    return out


def _pack_small(norm1_g, gate_b, conv_w, conv_b, conv_norm_g, q_norm_g, k_norm_g, norm2_g, ffn_conv_w, ffn_conv_b,
                last_row=None):
    pad_h = lambda a: jnp.pad(a, ((0, 0), (0, D - HEAD_DIM)))
    pad_f = lambda a: jnp.pad(a, ((0, 0), (0, FFN_PAD - 2 * D_FF))).reshape(-1, D)
    parts = [norm1_g, gate_b.reshape(2, D), conv_w, conv_b, conv_norm_g, pad_h(q_norm_g), pad_h(k_norm_g), norm2_g,
             pad_f(ffn_conv_w), pad_f(ffn_conv_b), jnp.zeros((1, D), f32) if last_row is None else last_row]
    return jnp.concatenate([jnp.pad(p, ((0, -p.shape[0] % 8), (0, 0))) for p in parts], axis=0)


def _unpack_small(p):
    o = _small_offsets()
    rows = lambda name, n: p[o[name]:o[name] + n]
    ffn = lambda a: a.reshape(-1, FFN_PAD)[:, :2 * D_FF]
    return dict(
        norm1_g=rows("norm1_g", 1), gate_b=rows("gate_b", 2).reshape(1, 2 * D), conv_w=rows("conv_w", CONV_WIDTH),
        conv_b=rows("conv_b", 1), conv_norm_g=rows("conv_norm_g", 1), q_norm_g=rows("q_norm_g", 1)[:, :HEAD_DIM],
        k_norm_g=rows("k_norm_g", 1)[:, :HEAD_DIM], norm2_g=rows("norm2_g", 1),
        ffn_conv_w=ffn(rows("ffn_conv_w", 18)), ffn_conv_b=ffn(rows("ffn_conv_b", 6)))


_ADAM_TILE = {896: 128, 704: 704, 128: 128, 352: 176}


def kernel(x, norm1_g, w_in, gate_b, conv_w, conv_b, conv_norm_g, w_conv_out, q_norm_g, k_norm_g, w_attn_out, w_out, norm2_g, w_up, ffn_conv_w, ffn_conv_b, w_down, loss_target, m_norm1_g, m_w_in, m_gate_b, m_conv_w, m_conv_b, m_conv_norm_g, m_w_conv_out, m_q_norm_g, m_k_norm_g, m_w_attn_out, m_w_out, m_norm2_g, m_w_up, m_ffn_conv_w, m_ffn_conv_b, m_w_down, v_norm1_g, v_w_in, v_gate_b, v_conv_w, v_conv_b, v_conv_norm_g, v_w_conv_out, v_q_norm_g, v_k_norm_g, v_w_attn_out, v_w_out, v_norm2_g, v_w_up, v_ffn_conv_w, v_ffn_conv_b, v_w_down):
    BL, S, _ = x.shape
    T = BL * S
    me = 4 * lax.axis_index("x") + 2 * lax.axis_index("y") + lax.axis_index("c")
    xt = x.reshape(T, D)
    target = loss_target.reshape(T, D)

    big = dict(w_in=(w_in[0], m_w_in[0], v_w_in[0]), w_up=(w_up[0], m_w_up[0], v_w_up[0]),
               w_conv_out=(w_conv_out[0], m_w_conv_out[0], v_w_conv_out[0]),
               w_attn_out=(w_attn_out[0], m_w_attn_out[0], v_w_attn_out[0]),
               w_out=(w_out[0], m_w_out[0], v_w_out[0]), w_down=(w_down[0], m_w_down[0], v_w_down[0]))
    order = ["w_in", "w_conv_out", "w_attn_out", "w_out", "w_up", "w_down"]
    shards = [(big[n][0].T if n in ("w_in", "w_up") else big[n][0]).astype(bf16) for n in order]
    gathered = _allgather_rows(shards, 1)
    W = {"w_in": gathered[0].reshape(-1, D)}

    def place_cols(shard, full_cols):
        z = jnp.zeros((shard.shape[0], full_cols), f32)
        return lax.dynamic_update_slice(z, shard, (0, me * shard.shape[1]))

    zr = lambda a: jnp.zeros_like(a)
    conv_local = _pack_small(
        zr(norm1_g), zr(gate_b), place_cols(conv_w[0], D), zr(conv_b), zr(conv_norm_g), zr(q_norm_g), zr(k_norm_g),
        zr(norm2_g), place_cols(ffn_conv_w[0], 2 * D_FF), zr(ffn_conv_b))
    ga_conv = _small_start("gather_conv_start", conv_local, after=gathered[0])
    ga_proj = _exchange_start("gather_start_proj", shards[1:4], gathered[1:4], after=ga_conv[4])
    ga_ffn = _exchange_start("gather_start_ffn", shards[4:6], gathered[4:6], after=ga_proj[4])

    bd = (jnp.arange(128)[:, None] // HEAD_DIM == jnp.arange(128)[None, :] // HEAD_DIM).astype(bf16)
    bias = _attn_bias()
    qg = jnp.tile(q_norm_g, (1, N_HEADS))
    kg = jnp.tile(k_norm_g, (1, N_HEADS))

    z8, h, qn, kn = _in_proj_fwd(xt, norm1_g, W["w_in"], qg, kg, bd, ga_ffn[4])
    conv_all = _unpack_small(_small_sum("gather_conv", me.reshape(1), ga_conv, z8))
    conv_w_full, ffn_w_full = conv_all["conv_w"], conv_all["ffn_conv_w"]
    c = _conv_fwd(z8, conv_w_full, conv_b, S)
    o, ob, lse = _attn_fwd(qn, kn, z8, bias, S)
    for n, g in zip(order[1:4], _exchange_wait("gather_wait_proj", ga_proj, ob)[1]):
        W[n] = g.reshape(-1, D)
    s, ya, yb, mixed = _branches_fwd(c, ob, z8, conv_norm_g, gate_b, W["w_conv_out"], W["w_attn_out"])
    x1, h2 = _out_norm2_fwd(mixed, W["w_out"], xt, norm2_g)
    for n, g in zip(order[4:6], _exchange_wait("gather_wait_ffn", ga_ffn, x1)[1]):
        W[n] = g.reshape(-1, D)
    TNU = D_FF // 2
    u3 = _matmul_call(
        "mm_u", h2, W["w_up"],
        pl.BlockSpec((1024, D), lambda i, j, k: (i, 0)),
        pl.BlockSpec((TNU, D), lambda i, j, k: (j, 0)),
        pl.BlockSpec((None, 1024, TNU), lambda i, j, k: (j // 2, i, j % 2)),
        jax.ShapeDtypeStruct((2, T, D_FF), f32), (T // 1024, 4, 1), "nt", 1, 1024, TNU)
    f = _ffn_fwd(u3, ffn_w_full, ffn_conv_b, S)
    dy, dyb, lacc = _down_loss_fwd(f, W["w_down"], x1, target)
    loss_local = 0.5 / D * jnp.sum(lacc)

    df = _matmul("mm_df", dyb, W["w_down"], "nt", f32, tn=TNU)
    g_w_down = _matmul("mm_dwdn", f, dyb, "tn", bf16, tm=TNU)
    du3, dffn = _ffn_bwd(u3, df, ffn_w_full, ffn_conv_b, S)
    g_w_up = _matmul_call(
        "mm_dwup", du3, h2,
        pl.BlockSpec((None, T, TNU), lambda i, j, k: (i // 2, 0, i % 2)),
        pl.BlockSpec((T, D), lambda i, j, k: (0, 0)),
        pl.BlockSpec((TNU, D), lambda i, j, k: (i, 0)),
        jax.ShapeDtypeStruct((2 * D_FF, D), bf16), (4, 1, 1), "tn", 1, TNU, D)
    blocks8 = lambda a: a.reshape(N_DEV, -1, D)
    ex_ffn = _exchange_start("scatter_start_ffn", [blocks8(g_w_up), blocks8(g_w_down)])
    dx1, dx1b, dg_norm2 = _up_norm2_bwd(du3, W["w_up"], x1, dy, norm2_g, ex_ffn[4])
    g_w_out = _matmul("mm_dwo", mixed, dx1b, "tn", bf16, tm=512)
    dz8 = lax.empty((8, T, D), bf16)
    dya, dyb2, dz8, dg_gate = _out_gate_bwd(dx1b, W["w_out"], z8, gate_b, ya, yb, dz8)
    g_w_conv_out = _matmul("mm_dwco", s, dya, "tn", bf16, tm=512)
    g_w_attn_out = _matmul("mm_dwao", ob, dyb2, "tn", bf16, tm=512)
    ex_proj = _exchange_start("scatter_start_proj", [blocks8(g_w_conv_out), blocks8(g_w_attn_out), blocks8(g_w_out)])
    do = _matmul("mm_do", dyb2, W["w_attn_out"], "nt", f32, after=ex_proj[4])
    dc, dg_convnorm = _convnorm_bwd(dya, W["w_conv_out"], c, conv_norm_g)
    dz8a, dconv = _conv_bwd(dc, z8, conv_w_full, dz8, S)
    dz8b, dg_q, dg_k = _attn_bwd(qn, kn, z8, do, o, lse, bias, bd, qg, kg, dz8a, S)
    g_w_in = _matmul_call(
        "mm_dwin", dz8b, h,
        pl.BlockSpec((None, T, D), lambda i, j, k: (jnp.where(i < 2, i, jnp.where(i < 5, i + 2, i - 3)), 0, 0)),
        pl.BlockSpec((T, D), lambda i, j, k: (0, 0)), pl.BlockSpec((1024, D), lambda i, j, k: (i, 0)),
        jax.ShapeDtypeStruct((7 * D, D), bf16), (7, 1, 1), "tn", 1, D, D)
    ex_in = _exchange_start("scatter_start_in", [blocks8(g_w_in)])
    grad_x, dg_norm1 = _in_norm1_bwd(dz8b, W["w_in"], xt, dx1, norm1_g, ex_in[4])

    sum8 = lambda a: a.reshape(-1, 8, a.shape[-1]).sum(axis=1)
    dconv_s = sum8(dconv.sum(axis=0))
    dffn_s = dffn.sum(axis=0).reshape(2, 4, 8, D_FF).sum(axis=2)
    dffn_w = jnp.concatenate([dffn_s[0, :3], dffn_s[1, :3]], axis=1)
    dffn_b = jnp.concatenate([dffn_s[0, 3:4], dffn_s[1, 3:4]], axis=1)
    fold = lambda a: sum8(a).reshape(N_HEADS, HEAD_DIM).sum(axis=0)[None]
    small_g_local = _pack_small(
        sum8(dg_norm1), sum8(dg_gate), dconv_s[:CONV_WIDTH], dconv_s[CONV_WIDTH:], sum8(dg_convnorm),
        fold(dg_q), fold(dg_k), sum8(dg_norm2), dffn_w, dffn_b,
        last_row=jnp.pad(loss_local.reshape(1, 1), ((0, 0), (0, D - 1))))
    sg_start = _small_start("small_grads_start", small_g_local)

    place_m = lambda a, full: place_cols(a[0], full)
    small_w_true = _pack_small(norm1_g, gate_b, conv_w_full, conv_b, conv_norm_g, q_norm_g, k_norm_g, norm2_g,
                               ffn_w_full, ffn_conv_b)
    small_m = _pack_small(m_norm1_g, m_gate_b, place_m(m_conv_w, D), m_conv_b, m_conv_norm_g, m_q_norm_g, m_k_norm_g,
                          m_norm2_g, place_m(m_ffn_conv_w, 2 * D_FF), m_ffn_conv_b)
    small_v = _pack_small(v_norm1_g, v_gate_b, place_m(v_conv_w, D), v_conv_b, v_conv_norm_g, v_q_norm_g, v_k_norm_g,
                          v_norm2_g, place_m(v_ffn_conv_w, 2 * D_FF), v_ffn_conv_b)

    own, slots = {}, {}
    for tag, ex, names_ in (("ffn", ex_ffn, ("w_up", "w_down")),
                            ("proj", ex_proj, ("w_conv_out", "w_attn_out", "w_out")), ("in", ex_in, ("w_in",))):
        sent, landed = _exchange_wait("scatter_wait_" + tag, ex, [sg_start[4], small_w_true, small_m, small_v])
        for n, src, land in zip(names_, sent, landed):
            own[n], slots[n] = src, land

    res, adam_done = {}, []
    for n in order:
        w, m, v = big[n]
        outs = _adam_slots("adam_" + n, me.reshape(1), slots[n], own[n], w, m, v, _ADAM_TILE[slots[n].shape[1]],
                           transposed=n in ("w_in", "w_up"))
        adam_done.append(outs[0])
        res[n] = [a[None] for a in outs]
    small_g = _small_sum("small_grads", me.reshape(1), sg_start, adam_done)
    loss = small_g[_small_offsets()["last"], 0]

    col = lambda a, width: lax.dynamic_slice(a, (0, me * width), (a.shape[0], width))
    sd, sm, sv = _adam_small(small_g, small_w_true, small_m, small_v)
    for i, packed in enumerate((small_g, sd, sm, sv)):
        u = _unpack_small(packed)
        u["conv_w"] = col(u["conv_w"], D // N_DEV)
        u["ffn_conv_w"] = col(u["ffn_conv_w"], 2 * D_FF // N_DEV)
        for n, a in u.items():
            res.setdefault(n, [None] * 4)[i] = a[None] if n in ("conv_w", "ffn_conv_w") else a

    names = ["norm1_g", "w_in", "gate_b", "conv_w", "conv_b", "conv_norm_g", "w_conv_out", "q_norm_g", "k_norm_g",
             "w_attn_out", "w_out", "norm2_g", "w_up", "ffn_conv_w", "ffn_conv_b", "w_down"]
    out = [loss, grad_x.reshape(BL, S, D)]
    for i in range(4):
        out += [res[n][i] for n in names]
    return tuple(out)
```

```python
import functools

import jax
import jax.numpy as jnp
import numpy as np
from jax import lax
from jax.experimental import pallas as pl
from jax.experimental.pallas import tpu as pltpu

f32 = jnp.float32
bf16 = jnp.bfloat16

D = 1024
N_HEADS = 16
HEAD_DIM = 64
CONV_WIDTH = 31
D_FF = 2816
GROUPS = ((128, 1), (512, 4), (2048, 16))
ATTN_BLOCK = 128
EPS = 1e-6
N_DEV = 8
MESH = pl.DeviceIdType.MESH

ADAM_LR = 0.001
ADAM_B1 = 0.9
ADAM_B2 = 0.999
ADAM_EPS = 1e-08
ADAM_WD = 0.01
ADAM_STEP = 10

VMEM_LIMIT = 56 * 1024 * 1024
MASK_BIAS = 1e30

Z_AVAL, Z_AGATE, Z_GA, Z_GB, Z_Q, Z_K, Z_V = 0, 1, 2, 3, 4, 5, 6


_W_OF_Z = (0, 1, 5, 6, 2, 3, 4)


def _wsec_of_zsec(j):
    return jnp.where(j < 2, j, jnp.where(j < 4, j + 3, j - 2))


def _sig(x):
    return 1.0 / (1.0 + jnp.exp(-x))


def _colsum8(x):
    return x.reshape(-1, 8, x.shape[-1]).sum(axis=0)


def _cparams(sem):
    return pltpu.CompilerParams(dimension_semantics=sem, vmem_limit_bytes=VMEM_LIMIT)


def _my_pos():
    x, y, c = lax.axis_index("x"), lax.axis_index("y"), lax.axis_index("c")
    return x, y, c, 4 * x + 2 * y + c


_DIMS = {"nn": ((1,), (0,)), "nt": ((1,), (1,)), "tn": ((0,), (0,))}


def _matmul_call(name, a, b, a_spec, b_spec, o_spec, out_shape, grid, mode, nk, tm, tn, after=None):
    dims = (_DIMS[mode], ((), ()))
    extra = [] if after is None else [after]

    def body(a_ref, b_ref, *rest):
        o_ref, scratch = rest[len(extra)], rest[len(extra) + 1:]
        part = lax.dot_general(a_ref[...], b_ref[...], dims, preferred_element_type=f32)
        if nk == 1:
            o_ref[...] = part.astype(o_ref.dtype)
        else:
            acc = scratch[0]
            k = pl.program_id(2)

            @pl.when(k == 0)
            def _():
                acc[...] = part

            @pl.when(k > 0)
            def _():
                acc[...] += part

            @pl.when(k == nk - 1)
            def _():
                o_ref[...] = acc[...].astype(o_ref.dtype)

    scratch = [] if nk == 1 else [pltpu.VMEM((tm, tn), f32)]
    return pl.pallas_call(
        body, name=name, grid=grid, in_specs=[a_spec, b_spec] + [pl.BlockSpec(memory_space=pl.ANY)] * len(extra),
        out_specs=o_spec, out_shape=out_shape,
        scratch_shapes=scratch, compiler_params=_cparams(("parallel", "parallel", "arbitrary")),
    )(a, b, *extra)


def _matmul(name, a, b, mode, out_dtype, tm=1024, tn=1024, tk=None, after=None):
    if mode == "nn":
        (M, K), (_, N) = a.shape, b.shape
    elif mode == "nt":
        (M, K), (N, _) = a.shape, b.shape
    else:
        (K, M), (_, N) = a.shape, b.shape
    tm, tn = min(tm, M), min(tn, N)
    tk = K if tk is None else tk
    nk = K // tk
    assert M % tm == 0 and N % tn == 0 and K % tk == 0
    if mode == "tn":
        a_spec = pl.BlockSpec((tk, tm), lambda i, j, k: (k, i))
    else:
        a_spec = pl.BlockSpec((tm, tk), lambda i, j, k: (i, k))
    if mode == "nt":
        b_spec = pl.BlockSpec((tn, tk), lambda i, j, k: (j, k))
    else:
        b_spec = pl.BlockSpec((tk, tn), lambda i, j, k: (k, j))
    o_spec = pl.BlockSpec((tm, tn), lambda i, j, k: (i, j))
    return _matmul_call(name, a, b, a_spec, b_spec, o_spec, jax.ShapeDtypeStruct((M, N), out_dtype),
                        (M // tm, N // tn, nk), mode, nk, tm, tn, after=after)


FTM = 512


def _matmul_fused(name, a, b, pairs, epilogue, extras, consts, outs, nt=False, sums=False, passed=(), aliases=None):
    sa, M, kk = a.shape
    na = max(i for i, _ in pairs) + 1
    ne, nc, npass = len(extras), len(consts), len(passed)
    dims = (_DIMS["nt" if nt else "nn"], ((), ()))

    def body(a_ref, b_ref, *rest):
        acc = None
        for i, j in pairs:
            part = lax.dot_general(a_ref[i], b_ref[j], dims, preferred_element_type=f32)
            acc = part if acc is None else acc + part
        epilogue(acc, rest[:ne], rest[ne:ne + nc], rest[ne + nc + npass:])

    whole = lambda arr: pl.BlockSpec(arr.shape, lambda i, nd=arr.ndim: (0,) * nd, pipeline_mode=pl.Buffered(1))
    io_alias = {2 + ne + nc + k: v for k, v in (aliases or {}).items()}
    return pl.pallas_call(
        body, name=name, grid=(M // FTM,),
        in_specs=[pl.BlockSpec((na, FTM, kk), lambda i: (0, i, 0)), whole(b)] + [s for _, s in extras]
        + [whole(c) for c in consts] + [pl.BlockSpec(memory_space=pl.ANY)] * npass,
        out_specs=[s for _, s in outs], out_shape=[s for s, _ in outs], input_output_aliases=io_alias,
        compiler_params=_cparams(("arbitrary" if sums else "parallel",)),
    )(a, b, *[x for x, _ in extras], *consts, *passed)


def _frows(c=D):
    return pl.BlockSpec((FTM, c), lambda i: (i, 0))


def _fsec(s):
    return pl.BlockSpec((None, FTM, D), lambda i: (s, i, 0))


def _rowshape(T, dtype, c=D):
    return (jax.ShapeDtypeStruct((T, c), dtype), _frows(c))


def _sumshape(c=D):
    return (jax.ShapeDtypeStruct((8, c), f32), pl.BlockSpec((8, c), lambda i: (0, 0)))


def _add_colsum(ref, x, cols=None):
    @pl.when(pl.program_id(0) == 0)
    def _():
        if cols is None:
            ref[...] = jnp.zeros_like(ref)
        else:
            ref[:, cols] = jnp.zeros((8, x.shape[-1]), f32)

    if cols is None:
        ref[...] += _colsum8(x)
    else:
        ref[:, cols] += _colsum8(x)


def _rms(x):
    return lax.rsqrt(jnp.mean(x * x, axis=-1, keepdims=True) + EPS)


def _rms_bwd(dy_g, xn, rstd):
    return rstd * (dy_g - xn * jnp.mean(dy_g * xn, axis=-1, keepdims=True))


def _head_sum(x, bd):
    parts = []
    for cb in range(x.shape[-1] // 128):
        xb = x[:, cb * 128:(cb + 1) * 128]
        hi = xb.astype(bf16)
        lo = (xb - hi.astype(f32)).astype(bf16)
        parts.append(jnp.dot(hi, bd, preferred_element_type=f32) + jnp.dot(lo, bd, preferred_element_type=f32))
    return parts[0] if len(parts) == 1 else jnp.concatenate(parts, axis=1)


ZTM = 1024


def _in_proj_fwd(x, g, w_in_t, qg, kg, bd, after):
    T = x.shape[0]

    def body(x_ref, g_ref, w_ref, qg_ref, kg_ref, bd_ref, after_ref, z_ref, h_ref, qn_ref, kn_ref, hbuf):
        del after_ref
        j = pl.program_id(1)

        @pl.when(j == 0)
        def _():
            xv = x_ref[...]
            hv = (xv * _rms(xv) * g_ref[...]).astype(bf16)
            hbuf[...] = hv
            h_ref[...] = hv

        z = lax.dot_general(hbuf[...], w_ref[...], (_DIMS["nt"], ((), ())), preferred_element_type=f32)
        z_ref[...] = z

        def head_norm(gain_ref, scale):
            return z * lax.rsqrt(_head_sum(z * z, bd_ref[...]) * (1.0 / HEAD_DIM) + EPS) * gain_ref[...] * scale

        @pl.when(j == Z_Q)
        def _():
            qn_ref[...] = head_norm(qg_ref, HEAD_DIM ** -0.5)

        @pl.when(j == Z_K)
        def _():
            kn_ref[...] = head_norm(kg_ref, 1.0)

    tile = pl.BlockSpec((ZTM, D), lambda i, j: (i, 0))
    row = pl.BlockSpec((1, D), lambda i, j: (0, 0))
    return pl.pallas_call(
        body, name="mm_z", grid=(T // ZTM, 7),
        in_specs=[tile, row, pl.BlockSpec((D, D), lambda i, j: (_wsec_of_zsec(j), 0)), row, row,
                  pl.BlockSpec((128, 128), lambda i, j: (0, 0)), pl.BlockSpec(memory_space=pl.ANY)],
        out_specs=[pl.BlockSpec((None, ZTM, D), lambda i, j: (j, i, 0)), tile, tile, tile],
        out_shape=[jax.ShapeDtypeStruct((8, T, D), f32), jax.ShapeDtypeStruct((T, D), bf16),
                   jax.ShapeDtypeStruct((T, D), f32), jax.ShapeDtypeStruct((T, D), f32)],
        scratch_shapes=[pltpu.VMEM((ZTM, D), bf16)],
        compiler_params=_cparams(("parallel", "arbitrary")))(x, g, w_in_t, qg, kg, bd, after)


def _branches_fwd(c, ob, z8, g, gate_b, w_conv_out, w_attn_out):
    T = c.shape[0]

    def epilogue(yb, extra, const, out):
        cv = extra[0][...]
        r = cv * _rms(cv) * const[0][...]
        s = (r * _sig(r)).astype(bf16)
        ya = jnp.dot(s, const[2][...], preferred_element_type=f32)
        b_ref = const[1]
        g_a = _sig(extra[1][...] + b_ref[:, :D])
        g_b = _sig(extra[2][...] + b_ref[:, D:])
        out[0][...] = s
        out[1][...] = ya
        out[2][...] = yb
        out[3][...] = (g_a * ya + g_b * yb).astype(bf16)

    return _matmul_fused("mm_branches", ob[None], w_attn_out[None], ((0, 0),), epilogue,
                         [(c, _frows()), (z8, _fsec(Z_GA)), (z8, _fsec(Z_GB))], [g, gate_b, w_conv_out],
                         [_rowshape(T, bf16), _rowshape(T, f32), _rowshape(T, f32), _rowshape(T, bf16)])


def _out_norm2_fwd(mixed, w_out, x, g):
    T = x.shape[0]

    def epilogue(acc, extra, const, out):
        x1 = extra[0][...] + acc
        out[0][...] = x1
        out[1][...] = (x1 * _rms(x1) * const[0][...]).astype(bf16)

    return _matmul_fused("mm_t1_norm2", mixed[None], w_out[None], ((0, 0),), epilogue, [(x, _frows())], [g],
                         [_rowshape(T, f32), _rowshape(T, bf16)])


def _down_loss_fwd(f, w_down, x1, target):
    T = x1.shape[0]

    def epilogue(acc, extra, const, out):
        diff = extra[0][...] + acc - extra[1][...]
        dy = diff * (1.0 / D)
        out[0][...] = dy
        out[1][...] = dy.astype(bf16)
        _add_colsum(out[2], diff * diff)

    return _matmul_fused("mm_t2_loss", f[None], w_down[None], ((0, 0),), epilogue, [(x1, _frows()), (target, _frows())],
                         [], [_rowshape(T, f32), _rowshape(T, bf16), _sumshape()], sums=True)


def _up_norm2_bwd(du3, w_up_t, x1, dy, g, token):
    T = x1.shape[0]

    def epilogue(dh, extra, const, out):
        x1v = extra[0][...]
        rstd = _rms(x1v)
        xn = x1v * rstd
        dx1 = extra[1][...] + _rms_bwd(dh * const[0][...], xn, rstd)
        out[0][...] = dx1
        out[1][...] = dx1.astype(bf16)
        _add_colsum(out[2], dh * xn)

    return _matmul_fused("mm_dh2_norm2", du3, w_up_t.reshape(2, D_FF, D), ((0, 0), (1, 1)), epilogue,
                         [(x1, _frows()), (dy, _frows())], [g],
                         [_rowshape(T, f32), _rowshape(T, bf16), _sumshape()], sums=True, passed=[token])


def _out_gate_bwd(dx1b, w_out, z8, gate_b, ya, yb, dz8):
    T = ya.shape[0]

    def epilogue(dm, extra, const, out):
        b_ref = const[0]
        g_a = _sig(extra[0][...] + b_ref[:, :D])
        g_b = _sig(extra[1][...] + b_ref[:, D:])
        out[0][...] = (dm * g_a).astype(bf16)
        out[1][...] = (dm * g_b).astype(bf16)
        dla = dm * extra[2][...] * g_a * (1.0 - g_a)
        dlb = dm * extra[3][...] * g_b * (1.0 - g_b)
        out[2][0] = dla.astype(bf16)
        out[2][1] = dlb.astype(bf16)
        _add_colsum(out[3], dla, slice(0, D))
        _add_colsum(out[3], dlb, slice(D, 2 * D))

    return _matmul_fused(
        "mm_dmixed_gate", dx1b[None], w_out[None], ((0, 0),), epilogue,
        [(z8, _fsec(Z_GA)), (z8, _fsec(Z_GB)), (ya, _frows()), (yb, _frows())], [gate_b],
        [_rowshape(T, bf16), _rowshape(T, bf16),
         (jax.ShapeDtypeStruct(dz8.shape, bf16), pl.BlockSpec((2, FTM, D), lambda i: (1, i, 0))), _sumshape(2 * D)],
        nt=True, sums=True, passed=[dz8], aliases={0: 2})


def _convnorm_bwd(dya, w_conv_out, c, g):
    T = c.shape[0]

    def epilogue(ds, extra, const, out):
        cv = extra[0][...]
        rstd = _rms(cv)
        r0 = cv * rstd
        gv = const[0][...]
        r = r0 * gv
        sg = _sig(r)
        dr = ds * sg * (1.0 + r * (1.0 - sg))
        out[0][...] = _rms_bwd(dr * gv, r0, rstd)
        _add_colsum(out[1], dr * r0)

    return _matmul_fused("mm_ds_convnorm", dya[None], w_conv_out[None], ((0, 0),), epilogue, [(c, _frows())], [g],
                         [_rowshape(T, f32), _sumshape()], nt=True, sums=True)


def _in_norm1_bwd(dz8, w_in_t, x, dx1, g, token):
    T = x.shape[0]

    def epilogue(dh, extra, const, out):
        xv = extra[0][...]
        rstd = _rms(xv)
        xn = xv * rstd
        out[0][...] = extra[1][...] + _rms_bwd(dh * const[0][...], xn, rstd)
        _add_colsum(out[1], dh * xn)

    return _matmul_fused("mm_dh_norm1", dz8, w_in_t.reshape(7, D, D), tuple(zip(range(7), _W_OF_Z)), epilogue,
                         [(x, _frows()), (dx1, _frows())], [g], [_rowshape(T, f32), _sumshape()],
                         sums=True, passed=[token])


CCW = 256
CR = 64
HALO = 32


def _conv_fwd(z8, conv_w, conv_b, S):
    T = z8.shape[1]
    nb = T // S
    ncb = D // CCW

    def body(av_ref, ag_ref, w_ref, b_ref, c_ref, pad):
        pad[0:HALO, :] = jnp.zeros((HALO, CCW), f32)

        def fill(i, carry):
            r0 = pl.multiple_of(i * 256, 256)
            pad[pl.ds(HALO + r0, 256), :] = av_ref[pl.ds(r0, 256), :] * _sig(ag_ref[pl.ds(r0, 256), :])
            return carry

        lax.fori_loop(0, S // 256, fill, 0)
        bias = b_ref[...]

        def chunk(i, carry):
            r0 = pl.multiple_of(i * CR, CR)
            win = pad[pl.ds(r0, CR + HALO), :]
            acc = jnp.zeros((CR, CCW), f32) + bias
            for s in range(8):
                part = None
                for m in range((CONV_WIDTH - 1 - s) // 8 + 1):
                    j = CONV_WIDTH - 1 - 8 * m - s
                    term = win[24 - 8 * m:24 - 8 * m + CR + 8, :] * w_ref[j:j + 1, :]
                    part = term if part is None else part + term
                acc = acc + part[8 - s:8 - s + CR, :]
            c_ref[pl.ds(r0, CR), :] = acc
            return carry

        lax.fori_loop(0, S // CR, chunk, 0)

    zs = lambda s: pl.BlockSpec((None, S, CCW), lambda b, cb: (s, b, cb))
    return pl.pallas_call(
        body, name="conv_fwd", grid=(nb, ncb),
        in_specs=[zs(Z_AVAL), zs(Z_AGATE), pl.BlockSpec((CONV_WIDTH, CCW), lambda b, cb: (0, cb)),
                  pl.BlockSpec((1, CCW), lambda b, cb: (0, cb))],
        out_specs=pl.BlockSpec((S, CCW), lambda b, cb: (b, cb)),
        out_shape=jax.ShapeDtypeStruct((T, D), f32),
        scratch_shapes=[pltpu.VMEM((S + HALO, CCW), f32)],
        compiler_params=_cparams(("parallel", "parallel")))(z8, z8, conv_w, conv_b)


def _conv_bwd(dc, z8, conv_w, dz8, S):
    T = dc.shape[0]
    nb = T // S
    ncb = D // CCW

    def body(dc_ref, av_ref, ag_ref, w_ref, dz_in, dz_ref, dw_ref, apad, dpad, shbuf):
        del dz_in
        apad[0:HALO, :] = jnp.zeros((HALO, CCW), f32)
        dpad[S:S + HALO, :] = jnp.zeros((HALO, CCW), f32)
        dw_ref[...] = jnp.zeros_like(dw_ref)

        def fill(i, carry):
            r0 = pl.multiple_of(i * 256, 256)
            apad[pl.ds(HALO + r0, 256), :] = av_ref[pl.ds(r0, 256), :] * _sig(ag_ref[pl.ds(r0, 256), :])
            dpad[pl.ds(r0, 256), :] = dc_ref[pl.ds(r0, 256), :]
            return carry

        lax.fori_loop(0, S // 256, fill, 0)

        def chunk(i, carry):
            r0 = pl.multiple_of(i * CR, CR)
            dwin = dpad[pl.ds(r0, CR + HALO), :]
            da = jnp.zeros((CR, CCW), f32)
            for s in range(8):
                shbuf[...] = dwin[s:s + CR, :]
                dshift = shbuf[...]
                part = None
                for m in range((CONV_WIDTH - 1 - s) // 8 + 1):
                    j = CONV_WIDTH - 1 - 8 * m - s
                    term = dwin[8 * m:8 * m + CR + 8, :] * w_ref[j:j + 1, :]
                    part = term if part is None else part + term
                    a_lag = apad[pl.ds(r0 + HALO - 8 * m, CR), :]
                    dw_ref[8 * j:8 * j + 8, :] += _colsum8(dshift * a_lag)
                da = da + part[s:s + CR, :]
            dw_ref[8 * CONV_WIDTH:8 * CONV_WIDTH + 8, :] += _colsum8(dwin[0:CR, :])
            av = av_ref[pl.ds(r0, CR), :]
            sg = _sig(ag_ref[pl.ds(r0, CR), :])
            dz_ref[0, pl.ds(r0, CR), :] = (da * sg).astype(bf16)
            dz_ref[1, pl.ds(r0, CR), :] = (da * av * sg * (1.0 - sg)).astype(bf16)
            return carry

        lax.fori_loop(0, S // CR, chunk, 0)

    zs = lambda s: pl.BlockSpec((None, S, CCW), lambda b, cb: (s, b, cb))
    return pl.pallas_call(
        body, name="conv_bwd", grid=(nb, ncb),
        in_specs=[pl.BlockSpec((S, CCW), lambda b, cb: (b, cb)), zs(Z_AVAL), zs(Z_AGATE),
                  pl.BlockSpec((CONV_WIDTH, CCW), lambda b, cb: (0, cb)), pl.BlockSpec(memory_space=pl.ANY)],
        out_specs=[pl.BlockSpec((2, S, CCW), lambda b, cb: (0, b, cb)),
                   pl.BlockSpec((None, 256, CCW), lambda b, cb: (b, 0, cb))],
        out_shape=[jax.ShapeDtypeStruct(dz8.shape, bf16), jax.ShapeDtypeStruct((nb, 256, D), f32)],
        input_output_aliases={4: 0},
        scratch_shapes=[pltpu.VMEM((S + HALO, CCW), f32), pltpu.VMEM((S + HALO, CCW), f32),
                        pltpu.VMEM((CR, CCW), f32)],
        compiler_params=_cparams(("parallel", "parallel")))(dc, z8, z8, conv_w, dz8)


FR = 128
NFB = D_FF // CCW
FBW = 128


def _ffn_window(ref, i, r0):
    return ref[pl.ds(r0 - 8, FR + 8), :]


def _ffn_u(win, w_ref, b_ref):
    return (win[6:6 + FR, :] * w_ref[0:1, :] + win[7:7 + FR, :] * w_ref[1:2, :]
            + win[8:8 + FR, :] * w_ref[2:3, :] + b_ref[...])


def _ffn_fwd(u3, ffn_w, ffn_b, S):
    T = u3.shape[1]
    nb = T // S

    def body(uv_ref, ug_ref, wv_ref, wg_ref, bv_ref, bg_ref, f_ref):
        def chunk(first, i):
            r0 = 0 if first else pl.multiple_of(i * FR, FR)
            if first:
                z = jnp.zeros((8, CCW), f32)
                wv = jnp.concatenate([z, uv_ref[0:FR, :]], axis=0)
                wg = jnp.concatenate([z, ug_ref[0:FR, :]], axis=0)
            else:
                wv = _ffn_window(uv_ref, i, r0)
                wg = _ffn_window(ug_ref, i, r0)
            u_val = _ffn_u(wv, wv_ref, bv_ref)
            u_gate = _ffn_u(wg, wg_ref, bg_ref)
            f_ref[pl.ds(r0, FR), :] = (u_gate * _sig(u_gate) * u_val).astype(bf16)

        chunk(True, 0)

        def loop(i, carry):
            chunk(False, i)
            return carry

        lax.fori_loop(1, S // FR, loop, 0)

    us = lambda h: pl.BlockSpec((None, S, CCW), lambda b, cb: (h, b, cb))
    ws = lambda h: pl.BlockSpec((3, CCW), lambda b, cb: (0, h * NFB + cb))
    bs = lambda h: pl.BlockSpec((1, CCW), lambda b, cb: (0, h * NFB + cb))
    return pl.pallas_call(
        body, name="ffn_fwd", grid=(nb, NFB),
        in_specs=[us(0), us(1), ws(0), ws(1), bs(0), bs(1)],
        out_specs=pl.BlockSpec((S, CCW), lambda b, cb: (b, cb)),
        out_shape=jax.ShapeDtypeStruct((T, D_FF), bf16),
        compiler_params=_cparams(("parallel", "parallel")))(u3, u3, ffn_w, ffn_w, ffn_b, ffn_b)


def _ffn_bwd(u3, df, ffn_w, ffn_b, S):
    T = u3.shape[1]
    nb = T // S

    def body(uv_ref, ug_ref, df_ref, wv_ref, wg_ref, bv_ref, bg_ref, du_ref, dw_ref, dvpad, dgpad, shbuf):
        dvpad[S:S + 8, :] = jnp.zeros((8, FBW), f32)
        dgpad[S:S + 8, :] = jnp.zeros((8, FBW), f32)
        dw_ref[...] = jnp.zeros_like(dw_ref)

        def chunk(first, i):
            r0 = 0 if first else pl.multiple_of(i * FR, FR)
            if first:
                z = jnp.zeros((8, FBW), f32)
                wv = jnp.concatenate([z, uv_ref[0:FR, :]], axis=0)
                wg = jnp.concatenate([z, ug_ref[0:FR, :]], axis=0)
            else:
                wv = _ffn_window(uv_ref, i, r0)
                wg = _ffn_window(ug_ref, i, r0)
            taps = []
            for h, win in enumerate((wv, wg)):
                shbuf[2 * h] = win[6:6 + FR, :]
                shbuf[2 * h + 1] = win[7:7 + FR, :]
                taps.append((shbuf[2 * h], shbuf[2 * h + 1], win[8:8 + FR, :]))
            conv = lambda x, w_ref, b_ref: (x[0] * w_ref[0:1, :] + x[1] * w_ref[1:2, :] + x[2] * w_ref[2:3, :]
                                            + b_ref[...])
            u_val = conv(taps[0], wv_ref, bv_ref)
            u_gate = conv(taps[1], wg_ref, bg_ref)
            dfc = df_ref[pl.ds(r0, FR), :]
            sg = _sig(u_gate)
            d_val = dfc * u_gate * sg
            d_gate = dfc * u_val * sg * (1.0 + u_gate * (1.0 - sg))
            dvpad[pl.ds(r0, FR), :] = d_val
            dgpad[pl.ds(r0, FR), :] = d_gate
            for h, dd in enumerate((d_val, d_gate)):
                for j in range(3):
                    dw_ref[h, 8 * j:8 * j + 8, :] += _colsum8(dd * taps[h][j])
                dw_ref[h, 24:32, :] += _colsum8(dd)

        chunk(True, 0)

        def loop(i, carry):
            chunk(False, i)
            return carry

        lax.fori_loop(1, S // FR, loop, 0)

        def back(i, carry):
            r0 = pl.multiple_of(i * FR, FR)
            for h, (dpad, w_ref) in enumerate(((dvpad, wv_ref), (dgpad, wg_ref))):
                win = dpad[pl.ds(r0, FR + 8), :]
                du = (win[0:FR, :] * w_ref[2:3, :] + win[1:1 + FR, :] * w_ref[1:2, :]
                      + win[2:2 + FR, :] * w_ref[0:1, :])
                du_ref[h, pl.ds(r0, FR), :] = du.astype(bf16)
            return carry

        lax.fori_loop(0, S // FR, back, 0)

    ncb = D_FF // FBW
    us = lambda h: pl.BlockSpec((None, S, FBW), lambda b, cb: (h, b, cb))
    ws = lambda h: pl.BlockSpec((3, FBW), lambda b, cb: (0, h * ncb + cb))
    bs = lambda h: pl.BlockSpec((1, FBW), lambda b, cb: (0, h * ncb + cb))
    return pl.pallas_call(
        body, name="ffn_bwd", grid=(nb, ncb),
        in_specs=[us(0), us(1), pl.BlockSpec((S, FBW), lambda b, cb: (b, cb)), ws(0), ws(1), bs(0), bs(1)],
        out_specs=[pl.BlockSpec((2, S, FBW), lambda b, cb: (0, b, cb)),
                   pl.BlockSpec((None, 2, 32, FBW), lambda b, cb: (b, 0, 0, cb))],
        out_shape=[jax.ShapeDtypeStruct((2, T, D_FF), bf16), jax.ShapeDtypeStruct((nb, 2, 32, D_FF), f32)],
        scratch_shapes=[pltpu.VMEM((S + 8, FBW), f32), pltpu.VMEM((S + 8, FBW), f32),
                        pltpu.VMEM((4, FR, FBW), f32)],
        compiler_params=_cparams(("parallel", "parallel")))(u3, u3, df, ffn_w, ffn_w, ffn_b, ffn_b)


AB = ATTN_BLOCK


def _attn_bias_np():
    slopes = (np.float32(2.0) ** (np.float32(-8.0) * np.arange(1, N_HEADS + 1, dtype=np.float32)
                                  / np.float32(N_HEADS))).astype(np.float32)
    steps = (np.arange(AB)[:, None] + AB) - np.arange(2 * AB)[None, :]
    own = (np.arange(2 * AB) >= AB)[None, :]
    out = []
    for window, dil in GROUPS:
        valid = (steps >= 0) & (steps <= window // dil)
        dist = slopes[:, None, None] * (steps * dil).astype(np.float32)[None]
        kinds = [np.where(v[None], dist, np.float32(MASK_BIAS)) for v in (valid, valid & own)]
        out.append(np.stack(kinds, axis=1))
    return np.stack(out).astype(np.float32)


def _attn_bias():
    return jnp.asarray(_attn_bias_np())


def _head_masks():
    lane = lax.broadcasted_iota(jnp.int32, (1, 128), 1)
    return (lane < HEAD_DIM, lane >= HEAD_DIM)


def _perm_chunks(S, d):
    L = S // d
    ch = min(L, 256)
    out = []
    for r in range(d):
        for c in range(L // ch):
            start = r + d * ch * c
            out.append((pl.ds(start, ch, stride=d) if d > 1 else pl.ds(start, ch), r * L + c * ch, ch))
    return out


def _stack_heads(x, masks):
    return jnp.concatenate([jnp.where(masks[0], x, 0), jnp.where(masks[1], x, 0)], axis=0)


def _block_row(j):
    return j * AB if isinstance(j, int) else pl.multiple_of(j * AB, AB)


def _three_stages(n, stage_a, stage_b, stage_c, unroll):
    stage_a(0)
    stage_a(1)
    stage_b(0)

    def body(j, carry):
        stage_c(j - 1)
        stage_b(j)
        stage_a(j + 1)
        return carry

    lax.fori_loop(1, n - 1, body, 0, unroll=unroll)
    stage_c(n - 2)
    stage_b(n - 1)
    stage_c(n - 1)


_NT = (((1,), (1,)), ((), ()))
_TN = (((0,), (0,)), ((), ()))
SCH = 64


def _attn_fwd(qn, kn, z8, bias, S):
    T = qn.shape[0]
    nb = T // S
    nblk = S // AB

    def body(q_ref, k_ref, v_ref, bias_ref, o_ref, ob_ref, lse_ref, qs, ks, vs, s2, p2, ogp, lgp, *group_scratch):
        og, lg = group_scratch[:3], group_scratch[3:]
        masks = _head_masks()
        ks[0:AB, :] = jnp.zeros((AB, 128), bf16)
        vs[0:AB, :] = jnp.zeros((AB, 128), bf16)

        for g, (_, d) in enumerate(GROUPS):
            nsub = S // (d * AB)
            chunks = _perm_chunks(S, d)
            for src, dst, ch in chunks:
                qs[dst:dst + ch, :] = q_ref[src, :].astype(bf16)
                ks[AB + dst:AB + dst + ch, :] = k_ref[src, :].astype(bf16)
                vs[AB + dst:AB + dst + ch, :] = v_ref[src, :].astype(bf16)
            od, ld = (og[g], lg[g]) if d == 1 else (ogp, lgp)

            def scores(j):
                r0 = _block_row(j)
                q2 = _stack_heads(qs[pl.ds(r0, AB), :], masks)
                s2[j] = lax.dot_general(q2, ks[pl.ds(r0, 2 * AB), :], _NT, preferred_element_type=f32)

            def softmax(j, g=g, nsub=nsub, ld=ld):
                r0 = _block_row(j)
                kind = int(j % nsub == 0) if isinstance(j, int) else (j % nsub == 0).astype(jnp.int32)
                for cc in range(AB // SCH):
                    lses = []
                    for hh in range(2):
                        rows = pl.ds(hh * AB + cc * SCH, SCH)
                        sb = s2[j, rows, :] - bias_ref[g, hh, kind, cc * SCH:(cc + 1) * SCH, :]
                        m = jnp.max(sb, axis=-1, keepdims=True)
                        p = jnp.exp(sb - m)
                        den = jnp.sum(p, axis=-1, keepdims=True)
                        p2[j, rows, :] = (p * (1.0 / den)).astype(bf16)
                        lses.append(m + jnp.log(den))
                    ld[pl.ds(r0 + cc * SCH, SCH), :] = jnp.where(masks[0], lses[0], lses[1])

            def values(j, od=od):
                r0 = _block_row(j)
                pv2 = jnp.dot(p2[j], vs[pl.ds(r0, 2 * AB), :], preferred_element_type=f32)
                od[pl.ds(r0, AB), :] = jnp.where(masks[0], pv2[:AB], pv2[AB:])

            _three_stages(nblk, scores, softmax, values, nblk - 2)

            if d > 1:
                for src, dst, ch in chunks:
                    og[g][src, :] = ogp[dst:dst + ch, :]
                    lg[g][src, :] = lgp[dst:dst + ch, :]

        def combine(i, carry):
            rr = pl.ds(pl.multiple_of(i * 256, 256), 256)
            l0, l1, l2 = lg[0][rr, :], lg[1][rr, :], lg[2][rr, :]
            mx = jnp.maximum(jnp.maximum(l0, l1), l2)
            e0, e1, e2 = jnp.exp(l0 - mx), jnp.exp(l1 - mx), jnp.exp(l2 - mx)
            den = e0 + e1 + e2
            o = (e0 * og[0][rr, :] + e1 * og[1][rr, :] + e2 * og[2][rr, :]) / den
            o_ref[rr, :] = o
            ob_ref[rr, :] = o.astype(bf16)
            lse_ref[rr, :] = mx + jnp.log(den)
            return carry

        lax.fori_loop(0, S // 256, combine, 0, unroll=True)

    blk = pl.BlockSpec((S, 128), lambda b, hp: (b, hp))
    return pl.pallas_call(
        body, name="attn_fwd", grid=(nb, N_HEADS // 2),
        in_specs=[blk, blk, pl.BlockSpec((None, S, 128), lambda b, hp: (Z_V, b, hp)),
                  pl.BlockSpec((3, 2, 2, AB, 2 * AB), lambda b, hp: (0, hp, 0, 0, 0))],
        out_specs=[blk, blk, blk],
        out_shape=[jax.ShapeDtypeStruct((T, D), f32), jax.ShapeDtypeStruct((T, D), bf16),
                   jax.ShapeDtypeStruct((T, D), f32)],
        scratch_shapes=[pltpu.VMEM((S, 128), bf16), pltpu.VMEM((S + AB, 128), bf16), pltpu.VMEM((S + AB, 128), bf16),
                        pltpu.VMEM((nblk, 2 * AB, 2 * AB), f32), pltpu.VMEM((nblk, 2 * AB, 2 * AB), bf16),
                        pltpu.VMEM((S, 128), f32), pltpu.VMEM((S, 128), f32)] + [pltpu.VMEM((S, 128), f32)] * 6,
        compiler_params=_cparams(("parallel", "parallel")))(qn, kn, z8, bias)


def _attn_bwd(qn, kn, z8, do, o, lse, bias, bd, qg, kg, dz8, S):
    T = qn.shape[0]
    nb = T // S

    nblk = S // AB

    def body(q_ref, k_ref, v_ref, do_ref, o_ref, lse_ref, bias_ref, bd_ref, qraw_ref, kraw_ref, qg_ref, kg_ref,
             dz_in, dz_ref, dqg_ref, dkg_ref,
             dq_ref, dk_ref, dv_ref, delta, qs, ks, vs, dos, lsp, dlp, s2, dp2, p2, ds2, dqp, dkp, dvp):
        del dz_in
        masks = _head_masks()
        bdv = bd_ref[...]
        dq_ref[...] = jnp.zeros_like(dq_ref)
        dk_ref[...] = jnp.zeros_like(dk_ref)
        dv_ref[...] = jnp.zeros_like(dv_ref)
        ks[0:AB, :] = jnp.zeros((AB, 128), bf16)
        vs[0:AB, :] = jnp.zeros((AB, 128), bf16)

        def prep(i, carry):
            rr = pl.ds(pl.multiple_of(i * 256, 256), 256)
            delta[rr, :] = _head_sum(do_ref[rr, :] * o_ref[rr, :], bdv)
            return carry

        lax.fori_loop(0, S // 256, prep, 0, unroll=True)

        for g, (_, d) in enumerate(GROUPS):
            nsub = S // (d * AB)
            chunks = _perm_chunks(S, d)
            for src, dst, ch in chunks:
                qs[dst:dst + ch, :] = q_ref[src, :].astype(bf16)
                ks[AB + dst:AB + dst + ch, :] = k_ref[src, :].astype(bf16)
                vs[AB + dst:AB + dst + ch, :] = v_ref[src, :].astype(bf16)
                dos[dst:dst + ch, :] = do_ref[src, :].astype(bf16)
                lsp[dst:dst + ch, :] = lse_ref[src, :]
                dlp[dst:dst + ch, :] = delta[src, :]
            dkp[...] = jnp.zeros_like(dkp)
            dvp[...] = jnp.zeros_like(dvp)

            def scores(j):
                r0 = _block_row(j)
                q2 = _stack_heads(qs[pl.ds(r0, AB), :], masks)
                do2 = _stack_heads(dos[pl.ds(r0, AB), :], masks)
                s2[j] = lax.dot_general(q2, ks[pl.ds(r0, 2 * AB), :], _NT, preferred_element_type=f32)
                dp2[j] = lax.dot_general(do2, vs[pl.ds(r0, 2 * AB), :], _NT, preferred_element_type=f32)

            def probs(j, g=g, nsub=nsub):
                r0 = _block_row(j)
                kind = int(j % nsub == 0) if isinstance(j, int) else (j % nsub == 0).astype(jnp.int32)
                for cc in range(AB // SCH):
                    lse_c = lsp[pl.ds(r0 + cc * SCH, SCH), :]
                    del_c = dlp[pl.ds(r0 + cc * SCH, SCH), :]
                    for hh in range(2):
                        c0 = hh * HEAD_DIM
                        rows = pl.ds(hh * AB + cc * SCH, SCH)
                        sb = s2[j, rows, :] - bias_ref[g, hh, kind, cc * SCH:(cc + 1) * SCH, :]
                        p = jnp.exp(sb - lse_c[:, c0:c0 + 1])
                        p2[j, rows, :] = p.astype(bf16)
                        ds2[j, rows, :] = (p * (dp2[j, rows, :] - del_c[:, c0:c0 + 1])).astype(bf16)

            def grads(j):
                r0 = _block_row(j)
                q2 = _stack_heads(qs[pl.ds(r0, AB), :], masks)
                do2 = _stack_heads(dos[pl.ds(r0, AB), :], masks)
                dsb = ds2[j]
                t = jnp.dot(dsb, ks[pl.ds(r0, 2 * AB), :], preferred_element_type=f32)
                dqp[pl.ds(r0, AB), :] = jnp.where(masks[0], t[:AB], t[AB:])
                dkp[pl.ds(r0, 2 * AB), :] += lax.dot_general(dsb, q2, _TN, preferred_element_type=f32)
                dvp[pl.ds(r0, 2 * AB), :] += lax.dot_general(p2[j], do2, _TN, preferred_element_type=f32)

            _three_stages(nblk, scores, probs, grads, nblk - 2)

            for src, dst, ch in chunks:
                dq_ref[src, :] += dqp[dst:dst + ch, :]
                dk_ref[src, :] += dkp[AB + dst:AB + dst + ch, :]
                dv_ref[src, :] += dvp[AB + dst:AB + dst + ch, :]

        @pl.when(pl.program_id(1) == 0)
        def _():
            dqg_ref[...] = jnp.zeros_like(dqg_ref)
            dkg_ref[...] = jnp.zeros_like(dkg_ref)

        def norms(i, carry):
            rr = pl.ds(pl.multiple_of(i * 256, 256), 256)

            def one(raw, dn_scaled, g, dg_ref, sec):
                rstd = lax.rsqrt(_head_sum(raw * raw, bdv) * (1.0 / HEAD_DIM) + EPS)
                n = raw * rstd
                dg_ref[...] += _colsum8(dn_scaled * n)
                dn = dn_scaled * g
                draw = rstd * (dn - n * (_head_sum(dn * n, bdv) * (1.0 / HEAD_DIM)))
                dz_ref[sec, rr, :] = draw.astype(bf16)

            one(qraw_ref[rr, :], dq_ref[rr, :] * (HEAD_DIM ** -0.5), qg_ref[...], dqg_ref, 0)
            one(kraw_ref[rr, :], dk_ref[rr, :], kg_ref[...], dkg_ref, 1)
            dz_ref[2, rr, :] = dv_ref[rr, :].astype(bf16)
            dz_ref[3, rr, :] = jnp.zeros((256, 128), bf16)
            return carry

        lax.fori_loop(0, S // 256, norms, 0, unroll=True)

    blk = pl.BlockSpec((S, 128), lambda hp, b: (b, hp))
    sec = lambda s: pl.BlockSpec((None, S, 128), lambda hp, b: (s, b, hp))
    gain = pl.BlockSpec((1, 128), lambda hp, b: (0, hp))
    row = lambda dt, pad=0: pltpu.VMEM((S + pad, 128), dt)
    blocks = lambda dt: pltpu.VMEM((nblk, 2 * AB, 2 * AB), dt)
    return pl.pallas_call(
        body, name="attn_bwd", grid=(N_HEADS // 2, nb),
        in_specs=[blk, blk, sec(Z_V), blk, blk, blk,
                  pl.BlockSpec((3, 2, 2, AB, 2 * AB), lambda hp, b: (0, hp, 0, 0, 0)),
                  pl.BlockSpec((128, 128), lambda hp, b: (0, 0)), sec(Z_Q), sec(Z_K), gain, gain,
                  pl.BlockSpec(memory_space=pl.ANY)],
        out_specs=[pl.BlockSpec((4, S, 128), lambda hp, b: (1, b, hp)),
                   pl.BlockSpec((8, 128), lambda hp, b: (0, hp)), pl.BlockSpec((8, 128), lambda hp, b: (0, hp))],
        out_shape=[jax.ShapeDtypeStruct(dz8.shape, bf16), jax.ShapeDtypeStruct((8, D), f32),
                   jax.ShapeDtypeStruct((8, D), f32)],
        input_output_aliases={12: 0},
        scratch_shapes=[row(f32), row(f32), row(f32),
                        row(f32), row(bf16), row(bf16, AB), row(bf16, AB), row(bf16), row(f32), row(f32),
                        blocks(f32), blocks(f32), blocks(bf16), blocks(bf16), row(f32), row(f32, AB), row(f32, AB)],
        compiler_params=_cparams(("parallel", "arbitrary")))(qn, kn, z8, do, o, lse, bias, bd, z8, z8, qg, kg, dz8)


def _any_spec():
    return pl.BlockSpec(memory_space=pl.ANY)


def _allgather_rows(shards, n_full):
    n = len(shards)

    def body(*refs):
        ins, outs = refs[:n], refs[n:2 * n]
        send_sems, recv_sems, local_sems = refs[2 * n:]
        x, y, c, me = _my_pos()
        sibling = (x, y, 1 - c)
        chips = [(1 - x, y), (x, 1 - y), (1 - x, 1 - y)]

        def idx(px, py, pc):
            return 4 * px + 2 * py + pc

        def copy(a, k, blk, to, src=None):
            return pltpu.make_async_remote_copy(
                src_ref=outs[a].at[blk] if src is None else src, dst_ref=outs[a].at[blk],
                send_sem=send_sems.at[a, k], recv_sem=recv_sems.at[a, k], device_id=to, device_id_type=MESH)

        mine = [pltpu.make_async_copy(ins[a], outs[a].at[me], local_sems.at[a]) for a in range(n)]
        for cp in mine:
            cp.start()
        first = []
        for a in range(n_full):
            first.append(copy(a, 0, me, sibling, src=ins[a]))
            first += [copy(a, 1 + j, me, (*chip, c), src=ins[a]) for j, chip in enumerate(chips)]
        for cp in first:
            cp.start()
        passed = []
        for a in range(n_full):
            for j, chip in enumerate(chips):
                blk = idx(*chip, c)
                copy(a, 1 + j, blk, (x, y, c)).wait_recv()
                cp = copy(a, 4 + j, blk, sibling)
                cp.start()
                passed.append(cp)
        for a in range(n_full):
            copy(a, 0, idx(x, y, 1 - c), (x, y, c)).wait_recv()
            for j, chip in enumerate(chips):
                copy(a, 4 + j, idx(*chip, 1 - c), (x, y, c)).wait_recv()
        for cp in first + passed:
            cp.wait_send()
        for cp in mine:
            cp.wait()

    return pl.pallas_call(
        body, name="allgather_weights",
        in_specs=[_any_spec()] * n, out_specs=[_any_spec()] * n,
        out_shape=[jax.ShapeDtypeStruct((N_DEV,) + s.shape, s.dtype) for s in shards],
        scratch_shapes=[pltpu.SemaphoreType.DMA((n_full, 7)), pltpu.SemaphoreType.DMA((n_full, 7)),
                        pltpu.SemaphoreType.DMA((n,))],
    )(*shards)


def _peer(x, y, c, k):
    tx = 1 - x if (k >> 2) & 1 else x
    ty = 1 - y if (k >> 1) & 1 else y
    tc = 1 - c if k & 1 else c
    return (tx, ty, tc), 4 * tx + 2 * ty + tc


_PEER_ORDER = (2, 4, 6, 3, 5, 7, 1)


_HBM = pl.BlockSpec(memory_space=pltpu.HBM)
_SEM = pl.BlockSpec(memory_space=pltpu.SEMAPHORE)
_EFFECT = pltpu.SideEffectType.DATAFLOW_SIDE_EFFECTING


def _exchange_copies(srcs, lands, send_sems, recv_sems, gather):
    x, y, c, me = _my_pos()
    copies = []
    for k in _PEER_ORDER:
        tgt, tidx = _peer(x, y, c, k)
        for a in range(len(srcs)):
            copies.append(pltpu.make_async_remote_copy(
                src_ref=srcs[a] if gather else srcs[a].at[tidx], dst_ref=lands[a].at[me],
                send_sem=send_sems.at[7 * a + k - 1], recv_sem=recv_sems.at[7 * a + k - 1],
                device_id=tgt, device_id_type=MESH))
    return copies


def _exchange_start(name, srcs, lands=None, after=None):
    n = len(srcs)
    gather = lands is not None
    if lands is None:
        lands = [lax.empty(g.shape, g.dtype) for g in srcs]
    extra = [] if after is None else [after]

    def body(*refs):
        src_refs, land_refs = refs[:n], refs[n:2 * n]
        send_sems, recv_sems = refs[2 * n + len(extra)], refs[2 * n + len(extra) + 1]
        token = refs[-1]
        for cp in _exchange_copies(src_refs, land_refs, send_sems, recv_sems, gather):
            cp.start()
        token[...] = jnp.zeros_like(token)

    hbm = lambda a: pltpu.with_memory_space_constraint(a, pltpu.HBM)
    outs = pl.pallas_call(
        body, name=name,
        out_shape=(pltpu.SemaphoreType.DMA((7 * n,)), pltpu.SemaphoreType.DMA((7 * n,)),
                   *[pltpu.HBM(g.shape, g.dtype) for g in list(srcs) + list(lands)],
                   jax.ShapeDtypeStruct((8, 128), f32)),
        in_specs=[_HBM] * (2 * n) + [pl.BlockSpec(memory_space=pl.ANY)] * len(extra),
        out_specs=(_SEM, _SEM, *([_HBM] * (2 * n)), pl.BlockSpec(memory_space=pltpu.VMEM)),
        input_output_aliases={i: 2 + i for i in range(2 * n)},
        compiler_params=pltpu.CompilerParams(has_side_effects=_EFFECT),
    )(*[hbm(g) for g in srcs], *[hbm(g) for g in lands], *extra)
    return outs[0], outs[1], list(outs[2:2 + n]), list(outs[2 + n:2 + 2 * n]), outs[-1], gather


def _exchange_wait(name, started, after):
    send_sems, recv_sems, srcs, lands, _, gather = started
    n = len(srcs)
    after = list(after) if isinstance(after, (list, tuple)) else [after]

    def body(*refs):
        src_refs, land_refs = refs[:n], refs[n:2 * n]
        s_sems, r_sems = refs[2 * n], refs[2 * n + 1]
        for cp in _exchange_copies(src_refs, land_refs, s_sems, r_sems, gather):
            cp.wait_send()
            cp.wait_recv()

    outs = pl.pallas_call(
        body, name=name,
        out_shape=tuple(pltpu.HBM(a.shape, a.dtype) for a in list(srcs) + list(lands)),
        in_specs=[_HBM] * (2 * n) + [_SEM, _SEM] + [pl.BlockSpec(memory_space=pl.ANY)] * len(after),
        out_specs=tuple([_HBM] * (2 * n)),
        input_output_aliases={i: i for i in range(2 * n)},
        compiler_params=pltpu.CompilerParams(has_side_effects=_EFFECT),
    )(*srcs, *lands, send_sems, recv_sems, *after)
    return list(outs[:n]), list(outs[n:])


SMALL_ROWS = 128


def _small_start(name, sg, after=None):
    return _exchange_start(name, [sg], [lax.empty((N_DEV,) + sg.shape, f32)], after=after)


def _small_sum(name, me, started, after):
    (own,), (slots,) = _exchange_wait(name + "_wait", started, after)

    def body(me_ref, s_ref, own_ref, out_ref):
        acc = None
        for p in range(N_DEV):
            term = lax.cond(me_ref[0] == p, lambda: own_ref[...], lambda p=p: s_ref[p])
            acc = term if acc is None else acc + term
        out_ref[...] = acc

    return pl.pallas_call(
        body, name=name + "_sum",
        in_specs=[pl.BlockSpec(memory_space=pltpu.SMEM), pl.BlockSpec(memory_space=pltpu.VMEM),
                  pl.BlockSpec(memory_space=pltpu.VMEM)],
        out_specs=pl.BlockSpec(memory_space=pltpu.VMEM),
        out_shape=jax.ShapeDtypeStruct(own.shape, f32))(me, slots, own)


def _adam_math(g, w, m, v):
    m = ADAM_B1 * m + (1.0 - ADAM_B1) * g
    v = ADAM_B2 * v + (1.0 - ADAM_B2) * (g * g)
    m_hat = m / (1.0 - ADAM_B1 ** ADAM_STEP)
    v_hat = v / (1.0 - ADAM_B2 ** ADAM_STEP)
    delta = -ADAM_LR * (m_hat / (jnp.sqrt(v_hat) + ADAM_EPS) + ADAM_WD * w)
    return delta, m, v


def _adam_slots(name, me, slots, own, w, m, v, tr, transposed=False):
    rows = slots.shape[1]

    def body(me_ref, s_ref, own_ref, w_ref, m_ref, v_ref, g_ref, d_ref, nm_ref, nv_ref):
        mine = own_ref[...]
        g = None
        for p in range(N_DEV):
            term = lax.cond(me_ref[0] == p, lambda: mine, lambda p=p: s_ref[p]).astype(f32)
            g = term if g is None else g + term
        if transposed:
            g = g.T
        delta, nm, nv = _adam_math(g, w_ref[...], m_ref[...], v_ref[...])
        g_ref[...] = g
        d_ref[...] = delta
        nm_ref[...] = nm
        nv_ref[...] = nv

    mode = dict(pipeline_mode=pl.Buffered(1)) if rows == tr else {}
    if transposed:
        rs = pl.BlockSpec((D, tr), lambda i, me_ref: (0, i))
        rs_in = pl.BlockSpec((D, tr), lambda i, me_ref: (0, i), **mode)
    else:
        rs = pl.BlockSpec((tr, D), lambda i, me_ref: (i, 0))
        rs_in = pl.BlockSpec((tr, D), lambda i, me_ref: (i, 0), **mode)
    return pl.pallas_call(
        body, name=name,
        grid_spec=pltpu.PrefetchScalarGridSpec(
            num_scalar_prefetch=1, grid=(rows // tr,),
            in_specs=[pl.BlockSpec((N_DEV, tr, D), lambda i, me_ref: (0, i, 0), **mode),
                      pl.BlockSpec((None, tr, D), lambda i, me_ref: (me_ref[0], i, 0), **mode), rs_in, rs_in, rs_in],
            out_specs=[rs] * 4),
        out_shape=[jax.ShapeDtypeStruct(w.shape, f32)] * 4,
        compiler_params=_cparams(("parallel",)))(me, slots, own, w, m, v)


def _adam_small(g, w, m, v):
    def body(g_ref, w_ref, m_ref, v_ref, d_ref, nm_ref, nv_ref):
        delta, nm, nv = _adam_math(g_ref[...], w_ref[...], m_ref[...], v_ref[...])
        d_ref[...] = delta
        nm_ref[...] = nm
        nv_ref[...] = nv

    return pl.pallas_call(body, name="adam_small", out_shape=[jax.ShapeDtypeStruct(g.shape, f32)] * 3)(g, w, m, v)


FFN_PAD = 6 * D


_SMALL_PARTS = (("norm1_g", 1), ("gate_b", 2), ("conv_w", CONV_WIDTH), ("conv_b", 1), ("conv_norm_g", 1),
                ("q_norm_g", 1), ("k_norm_g", 1), ("norm2_g", 1), ("ffn_conv_w", 18), ("ffn_conv_b", 6), ("last", 1))


def _small_offsets():
    out, row = {}, 0
    for name, rows in _SMALL_PARTS:
        out[name] = row
        row += -(-rows // 8) * 8
    assert row == SMALL_ROWS
    return out


def _pack_small(norm1_g, gate_b, conv_w, conv_b, conv_norm_g, q_norm_g, k_norm_g, norm2_g, ffn_conv_w, ffn_conv_b,
                last_row=None):
    pad_h = lambda a: jnp.pad(a, ((0, 0), (0, D - HEAD_DIM)))
    pad_f = lambda a: jnp.pad(a, ((0, 0), (0, FFN_PAD - 2 * D_FF))).reshape(-1, D)
    parts = [norm1_g, gate_b.reshape(2, D), conv_w, conv_b, conv_norm_g, pad_h(q_norm_g), pad_h(k_norm_g), norm2_g,
             pad_f(ffn_conv_w), pad_f(ffn_conv_b), jnp.zeros((1, D), f32) if last_row is None else last_row]
    return jnp.concatenate([jnp.pad(p, ((0, -p.shape[0] % 8), (0, 0))) for p in parts], axis=0)


def _unpack_small(p):
    o = _small_offsets()
    rows = lambda name, n: p[o[name]:o[name] + n]
    ffn = lambda a: a.reshape(-1, FFN_PAD)[:, :2 * D_FF]
    return dict(
        norm1_g=rows("norm1_g", 1), gate_b=rows("gate_b", 2).reshape(1, 2 * D), conv_w=rows("conv_w", CONV_WIDTH),
        conv_b=rows("conv_b", 1), conv_norm_g=rows("conv_norm_g", 1), q_norm_g=rows("q_norm_g", 1)[:, :HEAD_DIM],
        k_norm_g=rows("k_norm_g", 1)[:, :HEAD_DIM], norm2_g=rows("norm2_g", 1),
        ffn_conv_w=ffn(rows("ffn_conv_w", 18)), ffn_conv_b=ffn(rows("ffn_conv_b", 6)))


_ADAM_TILE = {896: 128, 704: 704, 128: 128, 352: 176}


def kernel(x, norm1_g, w_in, gate_b, conv_w, conv_b, conv_norm_g, w_conv_out, q_norm_g, k_norm_g, w_attn_out, w_out, norm2_g, w_up, ffn_conv_w, ffn_conv_b, w_down, loss_target, m_norm1_g, m_w_in, m_gate_b, m_conv_w, m_conv_b, m_conv_norm_g, m_w_conv_out, m_q_norm_g, m_k_norm_g, m_w_attn_out, m_w_out, m_norm2_g, m_w_up, m_ffn_conv_w, m_ffn_conv_b, m_w_down, v_norm1_g, v_w_in, v_gate_b, v_conv_w, v_conv_b, v_conv_norm_g, v_w_conv_out, v_q_norm_g, v_k_norm_g, v_w_attn_out, v_w_out, v_norm2_g, v_w_up, v_ffn_conv_w, v_ffn_conv_b, v_w_down):
    BL, S, _ = x.shape
    T = BL * S
    me = 4 * lax.axis_index("x") + 2 * lax.axis_index("y") + lax.axis_index("c")
    xt = x.reshape(T, D)
    target = loss_target.reshape(T, D)

    big = dict(w_in=(w_in[0], m_w_in[0], v_w_in[0]), w_up=(w_up[0], m_w_up[0], v_w_up[0]),
               w_conv_out=(w_conv_out[0], m_w_conv_out[0], v_w_conv_out[0]),
               w_attn_out=(w_attn_out[0], m_w_attn_out[0], v_w_attn_out[0]),
               w_out=(w_out[0], m_w_out[0], v_w_out[0]), w_down=(w_down[0], m_w_down[0], v_w_down[0]))
    order = ["w_in", "w_conv_out", "w_attn_out", "w_out", "w_up", "w_down"]
    shards = [(big[n][0].T if n in ("w_in", "w_up") else big[n][0]).astype(bf16) for n in order]
    gathered = _allgather_rows(shards, 1)
    W = {"w_in": gathered[0].reshape(-1, D)}

    def place_cols(shard, full_cols):
        z = jnp.zeros((shard.shape[0], full_cols), f32)
        return lax.dynamic_update_slice(z, shard, (0, me * shard.shape[1]))

    zr = lambda a: jnp.zeros_like(a)
    conv_local = _pack_small(
        zr(norm1_g), zr(gate_b), place_cols(conv_w[0], D), zr(conv_b), zr(conv_norm_g), zr(q_norm_g), zr(k_norm_g),
        zr(norm2_g), place_cols(ffn_conv_w[0], 2 * D_FF), zr(ffn_conv_b))
    ga_conv = _small_start("gather_conv_start", conv_local, after=gathered[0])
    ga_proj = _exchange_start("gather_start_proj", shards[1:4], gathered[1:4], after=ga_conv[4])
    ga_ffn = _exchange_start("gather_start_ffn", shards[4:6], gathered[4:6], after=ga_proj[4])

    bd = (jnp.arange(128)[:, None] // HEAD_DIM == jnp.arange(128)[None, :] // HEAD_DIM).astype(bf16)
    bias = _attn_bias()
    qg = jnp.tile(q_norm_g, (1, N_HEADS))
    kg = jnp.tile(k_norm_g, (1, N_HEADS))

    z8, h, qn, kn = _in_proj_fwd(xt, norm1_g, W["w_in"], qg, kg, bd, ga_ffn[4])
    conv_all = _unpack_small(_small_sum("gather_conv", me.reshape(1), ga_conv, z8))
    conv_w_full, ffn_w_full = conv_all["conv_w"], conv_all["ffn_conv_w"]
    c = _conv_fwd(z8, conv_w_full, conv_b, S)
    o, ob, lse = _attn_fwd(qn, kn, z8, bias, S)
    for n, g in zip(order[1:4], _exchange_wait("gather_wait_proj", ga_proj, ob)[1]):
        W[n] = g.reshape(-1, D)
    s, ya, yb, mixed = _branches_fwd(c, ob, z8, conv_norm_g, gate_b, W["w_conv_out"], W["w_attn_out"])
    x1, h2 = _out_norm2_fwd(mixed, W["w_out"], xt, norm2_g)
    for n, g in zip(order[4:6], _exchange_wait("gather_wait_ffn", ga_ffn, x1)[1]):
        W[n] = g.reshape(-1, D)
    TNU = D_FF // 2
    u3 = _matmul_call(
        "mm_u", h2, W["w_up"],
        pl.BlockSpec((1024, D), lambda i, j, k: (i, 0)),
        pl.BlockSpec((TNU, D), lambda i, j, k: (j, 0)),
        pl.BlockSpec((None, 1024, TNU), lambda i, j, k: (j // 2, i, j % 2)),
        jax.ShapeDtypeStruct((2, T, D_FF), f32), (T // 1024, 4, 1), "nt", 1, 1024, TNU)
    f = _ffn_fwd(u3, ffn_w_full, ffn_conv_b, S)
    dy, dyb, lacc = _down_loss_fwd(f, W["w_down"], x1, target)
    loss_local = 0.5 / D * jnp.sum(lacc)

    df = _matmul("mm_df", dyb, W["w_down"], "nt", f32, tn=TNU)
    g_w_down = _matmul("mm_dwdn", f, dyb, "tn", bf16, tm=TNU)
    du3, dffn = _ffn_bwd(u3, df, ffn_w_full, ffn_conv_b, S)
    g_w_up = _matmul_call(
        "mm_dwup", du3, h2,
        pl.BlockSpec((None, T, TNU), lambda i, j, k: (i // 2, 0, i % 2)),
        pl.BlockSpec((T, D), lambda i, j, k: (0, 0)),
        pl.BlockSpec((TNU, D), lambda i, j, k: (i, 0)),
        jax.ShapeDtypeStruct((2 * D_FF, D), bf16), (4, 1, 1), "tn", 1, TNU, D)
    blocks8 = lambda a: a.reshape(N_DEV, -1, D)
    ex_ffn = _exchange_start("scatter_start_ffn", [blocks8(g_w_up), blocks8(g_w_down)])
    dx1, dx1b, dg_norm2 = _up_norm2_bwd(du3, W["w_up"], x1, dy, norm2_g, ex_ffn[4])
    g_w_out = _matmul("mm_dwo", mixed, dx1b, "tn", bf16, tm=512)
    dz8 = lax.empty((8, T, D), bf16)
    dya, dyb2, dz8, dg_gate = _out_gate_bwd(dx1b, W["w_out"], z8, gate_b, ya, yb, dz8)
    g_w_conv_out = _matmul("mm_dwco", s, dya, "tn", bf16, tm=512)
    g_w_attn_out = _matmul("mm_dwao", ob, dyb2, "tn", bf16, tm=512)
    ex_proj = _exchange_start("scatter_start_proj", [blocks8(g_w_conv_out), blocks8(g_w_attn_out), blocks8(g_w_out)])
    do = _matmul("mm_do", dyb2, W["w_attn_out"], "nt", f32, after=ex_proj[4])
    dc, dg_convnorm = _convnorm_bwd(dya, W["w_conv_out"], c, conv_norm_g)
    dz8a, dconv = _conv_bwd(dc, z8, conv_w_full, dz8, S)
    dz8b, dg_q, dg_k = _attn_bwd(qn, kn, z8, do, o, lse, bias, bd, qg, kg, dz8a, S)
    sum8 = lambda a: a.reshape(-1, 8, a.shape[-1]).sum(axis=1)
    dconv_s = sum8(dconv.sum(axis=0))
    dffn_s = dffn.sum(axis=0).reshape(2, 4, 8, D_FF).sum(axis=2)
    dffn_w = jnp.concatenate([dffn_s[0, :3], dffn_s[1, :3]], axis=1)
    dffn_b = jnp.concatenate([dffn_s[0, 3:4], dffn_s[1, 3:4]], axis=1)
    fold = lambda a: sum8(a).reshape(N_HEADS, HEAD_DIM).sum(axis=0)[None]
    small_g_local = _pack_small(
        jnp.zeros((1, D), f32), sum8(dg_gate), dconv_s[:CONV_WIDTH], dconv_s[CONV_WIDTH:], sum8(dg_convnorm),
        fold(dg_q), fold(dg_k), sum8(dg_norm2), dffn_w, dffn_b,
        last_row=jnp.pad(loss_local.reshape(1, 1), ((0, 0), (0, D - 1))))
    sg_start = _small_start("small_grads_start", small_g_local)
    g_w_in = _matmul_call(
        "mm_dwin", dz8b, h,
        pl.BlockSpec((None, T, D), lambda i, j, k: (jnp.where(i < 2, i, jnp.where(i < 5, i + 2, i - 3)), 0, 0)),
        pl.BlockSpec((T, D), lambda i, j, k: (0, 0)), pl.BlockSpec((1024, D), lambda i, j, k: (i, 0)),
        jax.ShapeDtypeStruct((7 * D, D), bf16), (7, 1, 1), "tn", 1, D, D)
    ex_in = _exchange_start("scatter_start_in", [blocks8(g_w_in)], after=sg_start[4])
    grad_x, dg_norm1 = _in_norm1_bwd(dz8b, W["w_in"], xt, dx1, norm1_g, ex_in[4])
    n1_start = _small_start("norm1_grad_start", jnp.pad(sum8(dg_norm1), ((0, 7), (0, 0))))


    place_m = lambda a, full: place_cols(a[0], full)
    small_w_true = _pack_small(norm1_g, gate_b, conv_w_full, conv_b, conv_norm_g, q_norm_g, k_norm_g, norm2_g,
                               ffn_w_full, ffn_conv_b)
    small_m = _pack_small(m_norm1_g, m_gate_b, place_m(m_conv_w, D), m_conv_b, m_conv_norm_g, m_q_norm_g, m_k_norm_g,
                          m_norm2_g, place_m(m_ffn_conv_w, 2 * D_FF), m_ffn_conv_b)
    small_v = _pack_small(v_norm1_g, v_gate_b, place_m(v_conv_w, D), v_conv_b, v_conv_norm_g, v_q_norm_g, v_k_norm_g,
                          v_norm2_g, place_m(v_ffn_conv_w, 2 * D_FF), v_ffn_conv_b)

    own, slots = {}, {}
    for tag, ex, names_ in (("ffn", ex_ffn, ("w_up", "w_down")),
                            ("proj", ex_proj, ("w_conv_out", "w_attn_out", "w_out")), ("in", ex_in, ("w_in",))):
        sent, landed = _exchange_wait("scatter_wait_" + tag, ex, [n1_start[4], small_w_true, small_m, small_v])
        for n, src, land in zip(names_, sent, landed):
            own[n], slots[n] = src, land

    res, adam_done = {}, []
    for n in order:
        w, m, v = big[n]
        outs = _adam_slots("adam_" + n, me.reshape(1), slots[n], own[n], w, m, v, _ADAM_TILE[slots[n].shape[1]],
                           transposed=n in ("w_in", "w_up"))
        adam_done.append(outs[0])
        res[n] = [a[None] for a in outs]
    small_g = _small_sum("small_grads", me.reshape(1), sg_start, adam_done)
    n1_g = _small_sum("norm1_grad", me.reshape(1), n1_start, adam_done)
    small_g = small_g + jnp.pad(n1_g, ((0, SMALL_ROWS - 8), (0, 0)))
    loss = small_g[_small_offsets()["last"], 0]

    col = lambda a, width: lax.dynamic_slice(a, (0, me * width), (a.shape[0], width))
    sd, sm, sv = _adam_small(small_g, small_w_true, small_m, small_v)
    for i, packed in enumerate((small_g, sd, sm, sv)):
        u = _unpack_small(packed)
        u["conv_w"] = col(u["conv_w"], D // N_DEV)
        u["ffn_conv_w"] = col(u["ffn_conv_w"], 2 * D_FF // N_DEV)
        for n, a in u.items():
            res.setdefault(n, [None] * 4)[i] = a[None] if n in ("conv_w", "ffn_conv_w") else a

    names = ["norm1_g", "w_in", "gate_b", "conv_w", "conv_b", "conv_norm_g", "w_conv_out", "q_norm_g", "k_norm_g",
             "w_attn_out", "w_out", "norm2_g", "w_up", "ffn_conv_w", "ffn_conv_b", "w_down"]
    out = [loss, grad_x.reshape(BL, S, D)]
    for i in range(4):
        out += [res[n][i] for n in names]
    return tuple(out)
```

```python
import functools

import jax
import jax.numpy as jnp
import numpy as np
from jax import lax
from jax.experimental import pallas as pl
from jax.experimental.pallas import tpu as pltpu

f32 = jnp.float32
bf16 = jnp.bfloat16

D = 1024
N_HEADS = 16
HEAD_DIM = 64
CONV_WIDTH = 31
D_FF = 2816
GROUPS = ((128, 1), (512, 4), (2048, 16))
ATTN_BLOCK = 128
EPS = 1e-6
N_DEV = 8
MESH = pl.DeviceIdType.MESH

ADAM_LR = 0.001
ADAM_B1 = 0.9
ADAM_B2 = 0.999
ADAM_EPS = 1e-08
ADAM_WD = 0.01
ADAM_STEP = 10

VMEM_LIMIT = 56 * 1024 * 1024
MASK_BIAS = 1e30

Z_AVAL, Z_AGATE, Z_GA, Z_GB, Z_Q, Z_K, Z_V = 0, 1, 2, 3, 4, 5, 6


_W_OF_Z = (0, 1, 5, 6, 2, 3, 4)


def _wsec_of_zsec(j):
    return jnp.where(j < 2, j, jnp.where(j < 4, j + 3, j - 2))


def _sig(x):
    return 1.0 / (1.0 + jnp.exp(-x))


def _colsum8(x):
    return x.reshape(-1, 8, x.shape[-1]).sum(axis=0)


def _cparams(sem):
    return pltpu.CompilerParams(dimension_semantics=sem, vmem_limit_bytes=VMEM_LIMIT)


def _my_pos():
    x, y, c = lax.axis_index("x"), lax.axis_index("y"), lax.axis_index("c")
    return x, y, c, 4 * x + 2 * y + c


_DIMS = {"nn": ((1,), (0,)), "nt": ((1,), (1,)), "tn": ((0,), (0,))}


def _matmul_call(name, a, b, a_spec, b_spec, o_spec, out_shape, grid, mode, nk, tm, tn, after=None):
    dims = (_DIMS[mode], ((), ()))
    extra = [] if after is None else [after]

    def body(a_ref, b_ref, *rest):
        o_ref, scratch = rest[len(extra)], rest[len(extra) + 1:]
        part = lax.dot_general(a_ref[...], b_ref[...], dims, preferred_element_type=f32)
        if nk == 1:
            o_ref[...] = part.astype(o_ref.dtype)
        else:
            acc = scratch[0]
            k = pl.program_id(2)

            @pl.when(k == 0)
            def _():
                acc[...] = part

            @pl.when(k > 0)
            def _():
                acc[...] += part

            @pl.when(k == nk - 1)
            def _():
                o_ref[...] = acc[...].astype(o_ref.dtype)

    scratch = [] if nk == 1 else [pltpu.VMEM((tm, tn), f32)]
    return pl.pallas_call(
        body, name=name, grid=grid, in_specs=[a_spec, b_spec] + [pl.BlockSpec(memory_space=pl.ANY)] * len(extra),
        out_specs=o_spec, out_shape=out_shape,
        scratch_shapes=scratch, compiler_params=_cparams(("parallel", "parallel", "arbitrary")),
    )(a, b, *extra)


def _matmul(name, a, b, mode, out_dtype, tm=1024, tn=1024, tk=None, after=None):
    if mode == "nn":
        (M, K), (_, N) = a.shape, b.shape
    elif mode == "nt":
        (M, K), (N, _) = a.shape, b.shape
    else:
        (K, M), (_, N) = a.shape, b.shape
    tm, tn = min(tm, M), min(tn, N)
    tk = K if tk is None else tk
    nk = K // tk
    assert M % tm == 0 and N % tn == 0 and K % tk == 0
    if mode == "tn":
        a_spec = pl.BlockSpec((tk, tm), lambda i, j, k: (k, i))
    else:
        a_spec = pl.BlockSpec((tm, tk), lambda i, j, k: (i, k))
    if mode == "nt":
        b_spec = pl.BlockSpec((tn, tk), lambda i, j, k: (j, k))
    else:
        b_spec = pl.BlockSpec((tk, tn), lambda i, j, k: (k, j))
    o_spec = pl.BlockSpec((tm, tn), lambda i, j, k: (i, j))
    return _matmul_call(name, a, b, a_spec, b_spec, o_spec, jax.ShapeDtypeStruct((M, N), out_dtype),
                        (M // tm, N // tn, nk), mode, nk, tm, tn, after=after)


FTM = 512


def _matmul_fused(name, a, b, pairs, epilogue, extras, consts, outs, nt=False, sums=False, passed=(), aliases=None):
    sa, M, kk = a.shape
    na = max(i for i, _ in pairs) + 1
    ne, nc, npass = len(extras), len(consts), len(passed)
    dims = (_DIMS["nt" if nt else "nn"], ((), ()))

    def body(a_ref, b_ref, *rest):
        acc = None
        for i, j in pairs:
            part = lax.dot_general(a_ref[i], b_ref[j], dims, preferred_element_type=f32)
            acc = part if acc is None else acc + part
        epilogue(acc, rest[:ne], rest[ne:ne + nc], rest[ne + nc + npass:])

    whole = lambda arr: pl.BlockSpec(arr.shape, lambda i, nd=arr.ndim: (0,) * nd, pipeline_mode=pl.Buffered(1))
    io_alias = {2 + ne + nc + k: v for k, v in (aliases or {}).items()}
    return pl.pallas_call(
        body, name=name, grid=(M // FTM,),
        in_specs=[pl.BlockSpec((na, FTM, kk), lambda i: (0, i, 0)), whole(b)] + [s for _, s in extras]
        + [whole(c) for c in consts] + [pl.BlockSpec(memory_space=pl.ANY)] * npass,
        out_specs=[s for _, s in outs], out_shape=[s for s, _ in outs], input_output_aliases=io_alias,
        compiler_params=_cparams(("arbitrary" if sums else "parallel",)),
    )(a, b, *[x for x, _ in extras], *consts, *passed)


def _frows(c=D):
    return pl.BlockSpec((FTM, c), lambda i: (i, 0))


def _fsec(s):
    return pl.BlockSpec((None, FTM, D), lambda i: (s, i, 0))


def _rowshape(T, dtype, c=D):
    return (jax.ShapeDtypeStruct((T, c), dtype), _frows(c))


def _sumshape(c=D):
    return (jax.ShapeDtypeStruct((8, c), f32), pl.BlockSpec((8, c), lambda i: (0, 0)))


def _add_colsum(ref, x, cols=None):
    @pl.when(pl.program_id(0) == 0)
    def _():
        if cols is None:
            ref[...] = jnp.zeros_like(ref)
        else:
            ref[:, cols] = jnp.zeros((8, x.shape[-1]), f32)

    if cols is None:
        ref[...] += _colsum8(x)
    else:
        ref[:, cols] += _colsum8(x)


def _rms(x):
    return lax.rsqrt(jnp.mean(x * x, axis=-1, keepdims=True) + EPS)


def _rms_bwd(dy_g, xn, rstd):
    return rstd * (dy_g - xn * jnp.mean(dy_g * xn, axis=-1, keepdims=True))


def _head_sum(x, bd):
    parts = []
    for cb in range(x.shape[-1] // 128):
        xb = x[:, cb * 128:(cb + 1) * 128]
        hi = xb.astype(bf16)
        lo = (xb - hi.astype(f32)).astype(bf16)
        parts.append(jnp.dot(hi, bd, preferred_element_type=f32) + jnp.dot(lo, bd, preferred_element_type=f32))
    return parts[0] if len(parts) == 1 else jnp.concatenate(parts, axis=1)


ZTM = 1024


def _in_proj_fwd(x, g, w_in_t, qg, kg, bd, after):
    T = x.shape[0]

    def body(x_ref, g_ref, w_ref, qg_ref, kg_ref, bd_ref, after_ref, z_ref, h_ref, qn_ref, kn_ref, hbuf):
        del after_ref
        j = pl.program_id(1)

        @pl.when(j == 0)
        def _():
            xv = x_ref[...]
            hv = (xv * _rms(xv) * g_ref[...]).astype(bf16)
            hbuf[...] = hv
            h_ref[...] = hv

        z = lax.dot_general(hbuf[...], w_ref[...], (_DIMS["nt"], ((), ())), preferred_element_type=f32)
        z_ref[...] = z

        def head_norm(gain_ref, scale):
            return z * lax.rsqrt(_head_sum(z * z, bd_ref[...]) * (1.0 / HEAD_DIM) + EPS) * gain_ref[...] * scale

        @pl.when(j == Z_Q)
        def _():
            qn_ref[...] = head_norm(qg_ref, HEAD_DIM ** -0.5)

        @pl.when(j == Z_K)
        def _():
            kn_ref[...] = head_norm(kg_ref, 1.0)

    tile = pl.BlockSpec((ZTM, D), lambda i, j: (i, 0))
    row = pl.BlockSpec((1, D), lambda i, j: (0, 0))
    return pl.pallas_call(
        body, name="mm_z", grid=(T // ZTM, 7),
        in_specs=[tile, row, pl.BlockSpec((D, D), lambda i, j: (_wsec_of_zsec(j), 0)), row, row,
                  pl.BlockSpec((128, 128), lambda i, j: (0, 0)), pl.BlockSpec(memory_space=pl.ANY)],
        out_specs=[pl.BlockSpec((None, ZTM, D), lambda i, j: (j, i, 0)), tile, tile, tile],
        out_shape=[jax.ShapeDtypeStruct((8, T, D), f32), jax.ShapeDtypeStruct((T, D), bf16),
                   jax.ShapeDtypeStruct((T, D), f32), jax.ShapeDtypeStruct((T, D), f32)],
        scratch_shapes=[pltpu.VMEM((ZTM, D), bf16)],
        compiler_params=_cparams(("parallel", "arbitrary")))(x, g, w_in_t, qg, kg, bd, after)


def _branches_fwd(c, ob, z8, g, gate_b, w_conv_out, w_attn_out):
    T = c.shape[0]

    def epilogue(yb, extra, const, out):
        cv = extra[0][...]
        r = cv * _rms(cv) * const[0][...]
        s = (r * _sig(r)).astype(bf16)
        ya = jnp.dot(s, const[2][...], preferred_element_type=f32)
        b_ref = const[1]
        g_a = _sig(extra[1][...] + b_ref[:, :D])
        g_b = _sig(extra[2][...] + b_ref[:, D:])
        out[0][...] = s
        out[1][...] = ya
        out[2][...] = yb
        out[3][...] = (g_a * ya + g_b * yb).astype(bf16)

    return _matmul_fused("mm_branches", ob[None], w_attn_out[None], ((0, 0),), epilogue,
                         [(c, _frows()), (z8, _fsec(Z_GA)), (z8, _fsec(Z_GB))], [g, gate_b, w_conv_out],
                         [_rowshape(T, bf16), _rowshape(T, f32), _rowshape(T, f32), _rowshape(T, bf16)])


def _out_norm2_fwd(mixed, w_out, x, g):
    T = x.shape[0]

    def epilogue(acc, extra, const, out):
        x1 = extra[0][...] + acc
        out[0][...] = x1
        out[1][...] = (x1 * _rms(x1) * const[0][...]).astype(bf16)

    return _matmul_fused("mm_t1_norm2", mixed[None], w_out[None], ((0, 0),), epilogue, [(x, _frows())], [g],
                         [_rowshape(T, f32), _rowshape(T, bf16)])


def _down_loss_fwd(f, w_down, x1, target):
    T = x1.shape[0]

    def epilogue(acc, extra, const, out):
        diff = extra[0][...] + acc - extra[1][...]
        dy = diff * (1.0 / D)
        out[0][...] = dy
        out[1][...] = dy.astype(bf16)
        _add_colsum(out[2], diff * diff)

    return _matmul_fused("mm_t2_loss", f[None], w_down[None], ((0, 0),), epilogue, [(x1, _frows()), (target, _frows())],
                         [], [_rowshape(T, f32), _rowshape(T, bf16), _sumshape()], sums=True)


def _up_norm2_bwd(du3, w_up_t, x1, dy, g, token):
    T = x1.shape[0]

    def epilogue(dh, extra, const, out):
        x1v = extra[0][...]
        rstd = _rms(x1v)
        xn = x1v * rstd
        dx1 = extra[1][...] + _rms_bwd(dh * const[0][...], xn, rstd)
        out[0][...] = dx1
        out[1][...] = dx1.astype(bf16)
        _add_colsum(out[2], dh * xn)

    return _matmul_fused("mm_dh2_norm2", du3, w_up_t.reshape(2, D_FF, D), ((0, 0), (1, 1)), epilogue,
                         [(x1, _frows()), (dy, _frows())], [g],
                         [_rowshape(T, f32), _rowshape(T, bf16), _sumshape()], sums=True, passed=[token])


def _out_gate_bwd(dx1b, w_out, z8, gate_b, ya, yb, dz8):
    T = ya.shape[0]

    def epilogue(dm, extra, const, out):
        b_ref = const[0]
        g_a = _sig(extra[0][...] + b_ref[:, :D])
        g_b = _sig(extra[1][...] + b_ref[:, D:])
        out[0][...] = (dm * g_a).astype(bf16)
        out[1][...] = (dm * g_b).astype(bf16)
        dla = dm * extra[2][...] * g_a * (1.0 - g_a)
        dlb = dm * extra[3][...] * g_b * (1.0 - g_b)
        out[2][0] = dla.astype(bf16)
        out[2][1] = dlb.astype(bf16)
        _add_colsum(out[3], dla, slice(0, D))
        _add_colsum(out[3], dlb, slice(D, 2 * D))

    return _matmul_fused(
        "mm_dmixed_gate", dx1b[None], w_out[None], ((0, 0),), epilogue,
        [(z8, _fsec(Z_GA)), (z8, _fsec(Z_GB)), (ya, _frows()), (yb, _frows())], [gate_b],
        [_rowshape(T, bf16), _rowshape(T, bf16),
         (jax.ShapeDtypeStruct(dz8.shape, bf16), pl.BlockSpec((2, FTM, D), lambda i: (1, i, 0))), _sumshape(2 * D)],
        nt=True, sums=True, passed=[dz8], aliases={0: 2})


def _convnorm_bwd(dya, w_conv_out, c, g):
    T = c.shape[0]

    def epilogue(ds, extra, const, out):
        cv = extra[0][...]
        rstd = _rms(cv)
        r0 = cv * rstd
        gv = const[0][...]
        r = r0 * gv
        sg = _sig(r)
        dr = ds * sg * (1.0 + r * (1.0 - sg))
        out[0][...] = _rms_bwd(dr * gv, r0, rstd)
        _add_colsum(out[1], dr * r0)

    return _matmul_fused("mm_ds_convnorm", dya[None], w_conv_out[None], ((0, 0),), epilogue, [(c, _frows())], [g],
                         [_rowshape(T, f32), _sumshape()], nt=True, sums=True)


def _in_norm1_bwd(dz8, w_in_t, x, dx1, g, token):
    T = x.shape[0]

    def epilogue(dh, extra, const, out):
        xv = extra[0][...]
        rstd = _rms(xv)
        xn = xv * rstd
        out[0][...] = extra[1][...] + _rms_bwd(dh * const[0][...], xn, rstd)
        _add_colsum(out[1], dh * xn)

    return _matmul_fused("mm_dh_norm1", dz8, w_in_t.reshape(7, D, D), tuple(zip(range(7), _W_OF_Z)), epilogue,
                         [(x, _frows()), (dx1, _frows())], [g], [_rowshape(T, f32), _sumshape()],
                         sums=True, passed=[token])


CCW = 256
CR = 64
HALO = 32


def _conv_fwd(z8, conv_w, conv_b, S):
    T = z8.shape[1]
    nb = T // S
    ncb = D // CCW

    def body(av_ref, ag_ref, w_ref, b_ref, c_ref, pad):
        pad[0:HALO, :] = jnp.zeros((HALO, CCW), f32)

        def fill(i, carry):
            r0 = pl.multiple_of(i * 256, 256)
            pad[pl.ds(HALO + r0, 256), :] = av_ref[pl.ds(r0, 256), :] * _sig(ag_ref[pl.ds(r0, 256), :])
            return carry

        lax.fori_loop(0, S // 256, fill, 0)
        bias = b_ref[...]

        def chunk(i, carry):
            r0 = pl.multiple_of(i * CR, CR)
            win = pad[pl.ds(r0, CR + HALO), :]
            acc = jnp.zeros((CR, CCW), f32) + bias
            for s in range(8):
                part = None
                for m in range((CONV_WIDTH - 1 - s) // 8 + 1):
                    j = CONV_WIDTH - 1 - 8 * m - s
                    term = win[24 - 8 * m:24 - 8 * m + CR + 8, :] * w_ref[j:j + 1, :]
                    part = term if part is None else part + term
                acc = acc + part[8 - s:8 - s + CR, :]
            c_ref[pl.ds(r0, CR), :] = acc
            return carry

        lax.fori_loop(0, S // CR, chunk, 0)

    zs = lambda s: pl.BlockSpec((None, S, CCW), lambda b, cb: (s, b, cb))
    return pl.pallas_call(
        body, name="conv_fwd", grid=(nb, ncb),
        in_specs=[zs(Z_AVAL), zs(Z_AGATE), pl.BlockSpec((CONV_WIDTH, CCW), lambda b, cb: (0, cb)),
                  pl.BlockSpec((1, CCW), lambda b, cb: (0, cb))],
        out_specs=pl.BlockSpec((S, CCW), lambda b, cb: (b, cb)),
        out_shape=jax.ShapeDtypeStruct((T, D), f32),
        scratch_shapes=[pltpu.VMEM((S + HALO, CCW), f32)],
        compiler_params=_cparams(("parallel", "parallel")))(z8, z8, conv_w, conv_b)


def _conv_bwd(dc, z8, conv_w, dz8, S):
    T = dc.shape[0]
    nb = T // S
    ncb = D // CCW

    def body(dc_ref, av_ref, ag_ref, w_ref, dz_in, dz_ref, dw_ref, apad, dpad, shbuf):
        del dz_in
        apad[0:HALO, :] = jnp.zeros((HALO, CCW), f32)
        dpad[S:S + HALO, :] = jnp.zeros((HALO, CCW), f32)
        dw_ref[...] = jnp.zeros_like(dw_ref)

        def fill(i, carry):
            r0 = pl.multiple_of(i * 256, 256)
            apad[pl.ds(HALO + r0, 256), :] = av_ref[pl.ds(r0, 256), :] * _sig(ag_ref[pl.ds(r0, 256), :])
            dpad[pl.ds(r0, 256), :] = dc_ref[pl.ds(r0, 256), :]
            return carry

        lax.fori_loop(0, S // 256, fill, 0)

        def chunk(i, carry):
            r0 = pl.multiple_of(i * CR, CR)
            dwin = dpad[pl.ds(r0, CR + HALO), :]
            da = jnp.zeros((CR, CCW), f32)
            for s in range(8):
                shbuf[...] = dwin[s:s + CR, :]
                dshift = shbuf[...]
                part = None
                for m in range((CONV_WIDTH - 1 - s) // 8 + 1):
                    j = CONV_WIDTH - 1 - 8 * m - s
                    term = dwin[8 * m:8 * m + CR + 8, :] * w_ref[j:j + 1, :]
                    part = term if part is None else part + term
                    a_lag = apad[pl.ds(r0 + HALO - 8 * m, CR), :]
                    dw_ref[8 * j:8 * j + 8, :] += _colsum8(dshift * a_lag)
                da = da + part[s:s + CR, :]
            dw_ref[8 * CONV_WIDTH:8 * CONV_WIDTH + 8, :] += _colsum8(dwin[0:CR, :])
            av = av_ref[pl.ds(r0, CR), :]
            sg = _sig(ag_ref[pl.ds(r0, CR), :])
            dz_ref[0, pl.ds(r0, CR), :] = (da * sg).astype(bf16)
            dz_ref[1, pl.ds(r0, CR), :] = (da * av * sg * (1.0 - sg)).astype(bf16)
            return carry

        lax.fori_loop(0, S // CR, chunk, 0)

    zs = lambda s: pl.BlockSpec((None, S, CCW), lambda b, cb: (s, b, cb))
    return pl.pallas_call(
        body, name="conv_bwd", grid=(nb, ncb),
        in_specs=[pl.BlockSpec((S, CCW), lambda b, cb: (b, cb)), zs(Z_AVAL), zs(Z_AGATE),
                  pl.BlockSpec((CONV_WIDTH, CCW), lambda b, cb: (0, cb)), pl.BlockSpec(memory_space=pl.ANY)],
        out_specs=[pl.BlockSpec((2, S, CCW), lambda b, cb: (0, b, cb)),
                   pl.BlockSpec((None, 256, CCW), lambda b, cb: (b, 0, cb))],
        out_shape=[jax.ShapeDtypeStruct(dz8.shape, bf16), jax.ShapeDtypeStruct((nb, 256, D), f32)],
        input_output_aliases={4: 0},
        scratch_shapes=[pltpu.VMEM((S + HALO, CCW), f32), pltpu.VMEM((S + HALO, CCW), f32),
                        pltpu.VMEM((CR, CCW), f32)],
        compiler_params=_cparams(("parallel", "parallel")))(dc, z8, z8, conv_w, dz8)


FR = 128
NFB = D_FF // CCW
FBW = 128


def _ffn_window(ref, i, r0):
    return ref[pl.ds(r0 - 8, FR + 8), :]


def _ffn_u(win, w_ref, b_ref):
    return (win[6:6 + FR, :] * w_ref[0:1, :] + win[7:7 + FR, :] * w_ref[1:2, :]
            + win[8:8 + FR, :] * w_ref[2:3, :] + b_ref[...])


def _ffn_fwd(u3, ffn_w, ffn_b, S):
    T = u3.shape[1]
    nb = T // S

    def body(uv_ref, ug_ref, wv_ref, wg_ref, bv_ref, bg_ref, f_ref):
        def chunk(first, i):
            r0 = 0 if first else pl.multiple_of(i * FR, FR)
            if first:
                z = jnp.zeros((8, CCW), f32)
                wv = jnp.concatenate([z, uv_ref[0:FR, :]], axis=0)
                wg = jnp.concatenate([z, ug_ref[0:FR, :]], axis=0)
            else:
                wv = _ffn_window(uv_ref, i, r0)
                wg = _ffn_window(ug_ref, i, r0)
            u_val = _ffn_u(wv, wv_ref, bv_ref)
            u_gate = _ffn_u(wg, wg_ref, bg_ref)
            f_ref[pl.ds(r0, FR), :] = (u_gate * _sig(u_gate) * u_val).astype(bf16)

        chunk(True, 0)

        def loop(i, carry):
            chunk(False, i)
            return carry

        lax.fori_loop(1, S // FR, loop, 0)

    us = lambda h: pl.BlockSpec((None, S, CCW), lambda b, cb: (h, b, cb))
    ws = lambda h: pl.BlockSpec((3, CCW), lambda b, cb: (0, h * NFB + cb))
    bs = lambda h: pl.BlockSpec((1, CCW), lambda b, cb: (0, h * NFB + cb))
    return pl.pallas_call(
        body, name="ffn_fwd", grid=(nb, NFB),
        in_specs=[us(0), us(1), ws(0), ws(1), bs(0), bs(1)],
        out_specs=pl.BlockSpec((S, CCW), lambda b, cb: (b, cb)),
        out_shape=jax.ShapeDtypeStruct((T, D_FF), bf16),
        compiler_params=_cparams(("parallel", "parallel")))(u3, u3, ffn_w, ffn_w, ffn_b, ffn_b)


def _ffn_bwd(u3, df, ffn_w, ffn_b, S):
    T = u3.shape[1]
    nb = T // S

    def body(uv_ref, ug_ref, df_ref, wv_ref, wg_ref, bv_ref, bg_ref, du_ref, dw_ref, dvpad, dgpad, shbuf):
        dvpad[S:S + 8, :] = jnp.zeros((8, FBW), f32)
        dgpad[S:S + 8, :] = jnp.zeros((8, FBW), f32)
        dw_ref[...] = jnp.zeros_like(dw_ref)

        def chunk(first, i):
            r0 = 0 if first else pl.multiple_of(i * FR, FR)
            if first:
                z = jnp.zeros((8, FBW), f32)
                wv = jnp.concatenate([z, uv_ref[0:FR, :]], axis=0)
                wg = jnp.concatenate([z, ug_ref[0:FR, :]], axis=0)
            else:
                wv = _ffn_window(uv_ref, i, r0)
                wg = _ffn_window(ug_ref, i, r0)
            taps = []
            for h, win in enumerate((wv, wg)):
                shbuf[2 * h] = win[6:6 + FR, :]
                shbuf[2 * h + 1] = win[7:7 + FR, :]
                taps.append((shbuf[2 * h], shbuf[2 * h + 1], win[8:8 + FR, :]))
            conv = lambda x, w_ref, b_ref: (x[0] * w_ref[0:1, :] + x[1] * w_ref[1:2, :] + x[2] * w_ref[2:3, :]
                                            + b_ref[...])
            u_val = conv(taps[0], wv_ref, bv_ref)
            u_gate = conv(taps[1], wg_ref, bg_ref)
            dfc = df_ref[pl.ds(r0, FR), :]
            sg = _sig(u_gate)
            d_val = dfc * u_gate * sg
            d_gate = dfc * u_val * sg * (1.0 + u_gate * (1.0 - sg))
            dvpad[pl.ds(r0, FR), :] = d_val
            dgpad[pl.ds(r0, FR), :] = d_gate
            for h, dd in enumerate((d_val, d_gate)):
                for j in range(3):
                    dw_ref[h, 8 * j:8 * j + 8, :] += _colsum8(dd * taps[h][j])
                dw_ref[h, 24:32, :] += _colsum8(dd)

        chunk(True, 0)

        def loop(i, carry):
            chunk(False, i)
            return carry

        lax.fori_loop(1, S // FR, loop, 0)

        def back(i, carry):
            r0 = pl.multiple_of(i * FR, FR)
            for h, (dpad, w_ref) in enumerate(((dvpad, wv_ref), (dgpad, wg_ref))):
                win = dpad[pl.ds(r0, FR + 8), :]
                du = (win[0:FR, :] * w_ref[2:3, :] + win[1:1 + FR, :] * w_ref[1:2, :]
                      + win[2:2 + FR, :] * w_ref[0:1, :])
                du_ref[h, pl.ds(r0, FR), :] = du.astype(bf16)
            return carry

        lax.fori_loop(0, S // FR, back, 0)

    ncb = D_FF // FBW
    us = lambda h: pl.BlockSpec((None, S, FBW), lambda b, cb: (h, b, cb))
    ws = lambda h: pl.BlockSpec((3, FBW), lambda b, cb: (0, h * ncb + cb))
    bs = lambda h: pl.BlockSpec((1, FBW), lambda b, cb: (0, h * ncb + cb))
    return pl.pallas_call(
        body, name="ffn_bwd", grid=(nb, ncb),
        in_specs=[us(0), us(1), pl.BlockSpec((S, FBW), lambda b, cb: (b, cb)), ws(0), ws(1), bs(0), bs(1)],
        out_specs=[pl.BlockSpec((2, S, FBW), lambda b, cb: (0, b, cb)),
                   pl.BlockSpec((None, 2, 32, FBW), lambda b, cb: (b, 0, 0, cb))],
        out_shape=[jax.ShapeDtypeStruct((2, T, D_FF), bf16), jax.ShapeDtypeStruct((nb, 2, 32, D_FF), f32)],
        scratch_shapes=[pltpu.VMEM((S + 8, FBW), f32), pltpu.VMEM((S + 8, FBW), f32),
                        pltpu.VMEM((4, FR, FBW), f32)],
        compiler_params=_cparams(("parallel", "parallel")))(u3, u3, df, ffn_w, ffn_w, ffn_b, ffn_b)


AB = ATTN_BLOCK


def _attn_bias_np():
    slopes = (np.float32(2.0) ** (np.float32(-8.0) * np.arange(1, N_HEADS + 1, dtype=np.float32)
                                  / np.float32(N_HEADS))).astype(np.float32)
    steps = (np.arange(AB)[:, None] + AB) - np.arange(2 * AB)[None, :]
    own = (np.arange(2 * AB) >= AB)[None, :]
    out = []
    for window, dil in GROUPS:
        valid = (steps >= 0) & (steps <= window // dil)
        dist = slopes[:, None, None] * (steps * dil).astype(np.float32)[None]
        kinds = [np.where(v[None], dist, np.float32(MASK_BIAS)) for v in (valid, valid & own)]
        out.append(np.stack(kinds, axis=1))
    return np.stack(out).astype(np.float32)


def _attn_bias():
    return jnp.asarray(_attn_bias_np())


def _head_masks():
    lane = lax.broadcasted_iota(jnp.int32, (1, 128), 1)
    return (lane < HEAD_DIM, lane >= HEAD_DIM)


def _perm_chunks(S, d):
    L = S // d
    ch = min(L, 256)
    out = []
    for r in range(d):
        for c in range(L // ch):
            start = r + d * ch * c
            out.append((pl.ds(start, ch, stride=d) if d > 1 else pl.ds(start, ch), r * L + c * ch, ch))
    return out


def _stack_heads(x, masks):
    return jnp.concatenate([jnp.where(masks[0], x, 0), jnp.where(masks[1], x, 0)], axis=0)


def _block_row(j):
    return j * AB if isinstance(j, int) else pl.multiple_of(j * AB, AB)


def _three_stages(n, stage_a, stage_b, stage_c, unroll):
    stage_a(0)
    stage_a(1)
    stage_b(0)

    def body(j, carry):
        stage_c(j - 1)
        stage_b(j)
        stage_a(j + 1)
        return carry

    lax.fori_loop(1, n - 1, body, 0, unroll=unroll)
    stage_c(n - 2)
    stage_b(n - 1)
    stage_c(n - 1)


_NT = (((1,), (1,)), ((), ()))
_TN = (((0,), (0,)), ((), ()))
SCH = 128


def _attn_fwd(qn, kn, z8, bias, S):
    T = qn.shape[0]
    nb = T // S
    nblk = S // AB

    def body(q_ref, k_ref, v_ref, bias_ref, o_ref, ob_ref, lse_ref, qs, ks, vs, s2, p2, ogp, lgp, *group_scratch):
        og, lg = group_scratch[:3], group_scratch[3:]
        masks = _head_masks()
        ks[0:AB, :] = jnp.zeros((AB, 128), bf16)
        vs[0:AB, :] = jnp.zeros((AB, 128), bf16)

        for g, (_, d) in enumerate(GROUPS):
            nsub = S // (d * AB)
            chunks = _perm_chunks(S, d)
            for src, dst, ch in chunks:
                qs[dst:dst + ch, :] = q_ref[src, :].astype(bf16)
                ks[AB + dst:AB + dst + ch, :] = k_ref[src, :].astype(bf16)
                vs[AB + dst:AB + dst + ch, :] = v_ref[src, :].astype(bf16)
            od, ld = (og[g], lg[g]) if d == 1 else (ogp, lgp)

            def scores(j):
                r0 = _block_row(j)
                q2 = _stack_heads(qs[pl.ds(r0, AB), :], masks)
                s2[j] = lax.dot_general(q2, ks[pl.ds(r0, 2 * AB), :], _NT, preferred_element_type=f32)

            def softmax(j, g=g, nsub=nsub, ld=ld):
                r0 = _block_row(j)
                kind = int(j % nsub == 0) if isinstance(j, int) else (j % nsub == 0).astype(jnp.int32)
                for cc in range(AB // SCH):
                    lses = []
                    for hh in range(2):
                        rows = pl.ds(hh * AB + cc * SCH, SCH)
                        sb = s2[j, rows, :] - bias_ref[g, hh, kind, cc * SCH:(cc + 1) * SCH, :]
                        m = jnp.max(sb, axis=-1, keepdims=True)
                        p = jnp.exp(sb - m)
                        den = jnp.sum(p, axis=-1, keepdims=True)
                        p2[j, rows, :] = (p * (1.0 / den)).astype(bf16)
                        lses.append(m + jnp.log(den))
                    ld[pl.ds(r0 + cc * SCH, SCH), :] = jnp.where(masks[0], lses[0], lses[1])

            def values(j, od=od):
                r0 = _block_row(j)
                pv2 = jnp.dot(p2[j], vs[pl.ds(r0, 2 * AB), :], preferred_element_type=f32)
                od[pl.ds(r0, AB), :] = jnp.where(masks[0], pv2[:AB], pv2[AB:])

            _three_stages(nblk, scores, softmax, values, nblk - 2)

            if d > 1:
                for src, dst, ch in chunks:
                    og[g][src, :] = ogp[dst:dst + ch, :]
                    lg[g][src, :] = lgp[dst:dst + ch, :]

        def combine(i, carry):
            rr = pl.ds(pl.multiple_of(i * 256, 256), 256)
            l0, l1, l2 = lg[0][rr, :], lg[1][rr, :], lg[2][rr, :]
            mx = jnp.maximum(jnp.maximum(l0, l1), l2)
            e0, e1, e2 = jnp.exp(l0 - mx), jnp.exp(l1 - mx), jnp.exp(l2 - mx)
            den = e0 + e1 + e2
            o = (e0 * og[0][rr, :] + e1 * og[1][rr, :] + e2 * og[2][rr, :]) / den
            o_ref[rr, :] = o
            ob_ref[rr, :] = o.astype(bf16)
            lse_ref[rr, :] = mx + jnp.log(den)
            return carry

        lax.fori_loop(0, S // 256, combine, 0, unroll=True)

    blk = pl.BlockSpec((S, 128), lambda b, hp: (b, hp))
    return pl.pallas_call(
        body, name="attn_fwd", grid=(nb, N_HEADS // 2),
        in_specs=[blk, blk, pl.BlockSpec((None, S, 128), lambda b, hp: (Z_V, b, hp)),
                  pl.BlockSpec((3, 2, 2, AB, 2 * AB), lambda b, hp: (0, hp, 0, 0, 0))],
        out_specs=[blk, blk, blk],
        out_shape=[jax.ShapeDtypeStruct((T, D), f32), jax.ShapeDtypeStruct((T, D), bf16),
                   jax.ShapeDtypeStruct((T, D), f32)],
        scratch_shapes=[pltpu.VMEM((S, 128), bf16), pltpu.VMEM((S + AB, 128), bf16), pltpu.VMEM((S + AB, 128), bf16),
                        pltpu.VMEM((nblk, 2 * AB, 2 * AB), f32), pltpu.VMEM((nblk, 2 * AB, 2 * AB), bf16),
                        pltpu.VMEM((S, 128), f32), pltpu.VMEM((S, 128), f32)] + [pltpu.VMEM((S, 128), f32)] * 6,
        compiler_params=_cparams(("parallel", "parallel")))(qn, kn, z8, bias)


def _attn_bwd(qn, kn, z8, do, o, lse, bias, bd, qg, kg, dz8, S):
    T = qn.shape[0]
    nb = T // S

    nblk = S // AB

    def body(q_ref, k_ref, v_ref, do_ref, o_ref, lse_ref, bias_ref, bd_ref, qraw_ref, kraw_ref, qg_ref, kg_ref,
             dz_in, dz_ref, dqg_ref, dkg_ref,
             dq_ref, dk_ref, dv_ref, delta, qs, ks, vs, dos, lsp, dlp, s2, dp2, p2, ds2, dqp, dkp, dvp):
        del dz_in
        masks = _head_masks()
        bdv = bd_ref[...]
        dq_ref[...] = jnp.zeros_like(dq_ref)
        dk_ref[...] = jnp.zeros_like(dk_ref)
        dv_ref[...] = jnp.zeros_like(dv_ref)
        ks[0:AB, :] = jnp.zeros((AB, 128), bf16)
        vs[0:AB, :] = jnp.zeros((AB, 128), bf16)

        def prep(i, carry):
            rr = pl.ds(pl.multiple_of(i * 256, 256), 256)
            delta[rr, :] = _head_sum(do_ref[rr, :] * o_ref[rr, :], bdv)
            return carry

        lax.fori_loop(0, S // 256, prep, 0, unroll=True)

        for g, (_, d) in enumerate(GROUPS):
            nsub = S // (d * AB)
            chunks = _perm_chunks(S, d)
            for src, dst, ch in chunks:
                qs[dst:dst + ch, :] = q_ref[src, :].astype(bf16)
                ks[AB + dst:AB + dst + ch, :] = k_ref[src, :].astype(bf16)
                vs[AB + dst:AB + dst + ch, :] = v_ref[src, :].astype(bf16)
                dos[dst:dst + ch, :] = do_ref[src, :].astype(bf16)
                lsp[dst:dst + ch, :] = lse_ref[src, :]
                dlp[dst:dst + ch, :] = delta[src, :]
            dkp[...] = jnp.zeros_like(dkp)
            dvp[...] = jnp.zeros_like(dvp)

            def scores(j):
                r0 = _block_row(j)
                q2 = _stack_heads(qs[pl.ds(r0, AB), :], masks)
                do2 = _stack_heads(dos[pl.ds(r0, AB), :], masks)
                s2[j] = lax.dot_general(q2, ks[pl.ds(r0, 2 * AB), :], _NT, preferred_element_type=f32)
                dp2[j] = lax.dot_general(do2, vs[pl.ds(r0, 2 * AB), :], _NT, preferred_element_type=f32)

            def probs(j, g=g, nsub=nsub):
                r0 = _block_row(j)
                kind = int(j % nsub == 0) if isinstance(j, int) else (j % nsub == 0).astype(jnp.int32)
                for cc in range(AB // SCH):
                    lse_c = lsp[pl.ds(r0 + cc * SCH, SCH), :]
                    del_c = dlp[pl.ds(r0 + cc * SCH, SCH), :]
                    for hh in range(2):
                        c0 = hh * HEAD_DIM
                        rows = pl.ds(hh * AB + cc * SCH, SCH)
                        sb = s2[j, rows, :] - bias_ref[g, hh, kind, cc * SCH:(cc + 1) * SCH, :]
                        p = jnp.exp(sb - lse_c[:, c0:c0 + 1])
                        p2[j, rows, :] = p.astype(bf16)
                        ds2[j, rows, :] = (p * (dp2[j, rows, :] - del_c[:, c0:c0 + 1])).astype(bf16)

            def grads(j):
                r0 = _block_row(j)
                q2 = _stack_heads(qs[pl.ds(r0, AB), :], masks)
                do2 = _stack_heads(dos[pl.ds(r0, AB), :], masks)
                dsb = ds2[j]
                t = jnp.dot(dsb, ks[pl.ds(r0, 2 * AB), :], preferred_element_type=f32)
                dqp[pl.ds(r0, AB), :] = jnp.where(masks[0], t[:AB], t[AB:])
                dkp[pl.ds(r0, 2 * AB), :] += lax.dot_general(dsb, q2, _TN, preferred_element_type=f32)
                dvp[pl.ds(r0, 2 * AB), :] += lax.dot_general(p2[j], do2, _TN, preferred_element_type=f32)

            _three_stages(nblk, scores, probs, grads, nblk - 2)

            for src, dst, ch in chunks:
                dq_ref[src, :] += dqp[dst:dst + ch, :]
                dk_ref[src, :] += dkp[AB + dst:AB + dst + ch, :]
                dv_ref[src, :] += dvp[AB + dst:AB + dst + ch, :]

        @pl.when(pl.program_id(1) == 0)
        def _():
            dqg_ref[...] = jnp.zeros_like(dqg_ref)
            dkg_ref[...] = jnp.zeros_like(dkg_ref)

        def norms(i, carry):
            rr = pl.ds(pl.multiple_of(i * 256, 256), 256)

            def one(raw, dn_scaled, g, dg_ref, sec):
                rstd = lax.rsqrt(_head_sum(raw * raw, bdv) * (1.0 / HEAD_DIM) + EPS)
                n = raw * rstd
                dg_ref[...] += _colsum8(dn_scaled * n)
                dn = dn_scaled * g
                draw = rstd * (dn - n * (_head_sum(dn * n, bdv) * (1.0 / HEAD_DIM)))
                dz_ref[sec, rr, :] = draw.astype(bf16)

            one(qraw_ref[rr, :], dq_ref[rr, :] * (HEAD_DIM ** -0.5), qg_ref[...], dqg_ref, 0)
            one(kraw_ref[rr, :], dk_ref[rr, :], kg_ref[...], dkg_ref, 1)
            dz_ref[2, rr, :] = dv_ref[rr, :].astype(bf16)
            dz_ref[3, rr, :] = jnp.zeros((256, 128), bf16)
            return carry

        lax.fori_loop(0, S // 256, norms, 0, unroll=True)

    blk = pl.BlockSpec((S, 128), lambda hp, b: (b, hp))
    sec = lambda s: pl.BlockSpec((None, S, 128), lambda hp, b: (s, b, hp))
    gain = pl.BlockSpec((1, 128), lambda hp, b: (0, hp))
    row = lambda dt, pad=0: pltpu.VMEM((S + pad, 128), dt)
    blocks = lambda dt: pltpu.VMEM((nblk, 2 * AB, 2 * AB), dt)
    return pl.pallas_call(
        body, name="attn_bwd", grid=(N_HEADS // 2, nb),
        in_specs=[blk, blk, sec(Z_V), blk, blk, blk,
                  pl.BlockSpec((3, 2, 2, AB, 2 * AB), lambda hp, b: (0, hp, 0, 0, 0)),
                  pl.BlockSpec((128, 128), lambda hp, b: (0, 0)), sec(Z_Q), sec(Z_K), gain, gain,
                  pl.BlockSpec(memory_space=pl.ANY)],
        out_specs=[pl.BlockSpec((4, S, 128), lambda hp, b: (1, b, hp)),
                   pl.BlockSpec((8, 128), lambda hp, b: (0, hp)), pl.BlockSpec((8, 128), lambda hp, b: (0, hp))],
        out_shape=[jax.ShapeDtypeStruct(dz8.shape, bf16), jax.ShapeDtypeStruct((8, D), f32),
                   jax.ShapeDtypeStruct((8, D), f32)],
        input_output_aliases={12: 0},
        scratch_shapes=[row(f32), row(f32), row(f32),
                        row(f32), row(bf16), row(bf16, AB), row(bf16, AB), row(bf16), row(f32), row(f32),
                        blocks(f32), blocks(f32), blocks(bf16), blocks(bf16), row(f32), row(f32, AB), row(f32, AB)],
        compiler_params=_cparams(("parallel", "arbitrary")))(qn, kn, z8, do, o, lse, bias, bd, z8, z8, qg, kg, dz8)


def _any_spec():
    return pl.BlockSpec(memory_space=pl.ANY)


def _allgather_rows(shards, n_full):
    n = len(shards)

    def body(*refs):
        ins, outs = refs[:n], refs[n:2 * n]
        send_sems, recv_sems, local_sems = refs[2 * n:]
        x, y, c, me = _my_pos()
        sibling = (x, y, 1 - c)
        chips = [(1 - x, y), (x, 1 - y), (1 - x, 1 - y)]

        def idx(px, py, pc):
            return 4 * px + 2 * py + pc

        def copy(a, k, blk, to, src=None):
            return pltpu.make_async_remote_copy(
                src_ref=outs[a].at[blk] if src is None else src, dst_ref=outs[a].at[blk],
                send_sem=send_sems.at[a, k], recv_sem=recv_sems.at[a, k], device_id=to, device_id_type=MESH)

        mine = [pltpu.make_async_copy(ins[a], outs[a].at[me], local_sems.at[a]) for a in range(n)]
        for cp in mine:
            cp.start()
        first = []
        for a in range(n_full):
            first.append(copy(a, 0, me, sibling, src=ins[a]))
            first += [copy(a, 1 + j, me, (*chip, c), src=ins[a]) for j, chip in enumerate(chips)]
        for cp in first:
            cp.start()
        passed = []
        for a in range(n_full):
            for j, chip in enumerate(chips):
                blk = idx(*chip, c)
                copy(a, 1 + j, blk, (x, y, c)).wait_recv()
                cp = copy(a, 4 + j, blk, sibling)
                cp.start()
                passed.append(cp)
        for a in range(n_full):
            copy(a, 0, idx(x, y, 1 - c), (x, y, c)).wait_recv()
            for j, chip in enumerate(chips):
                copy(a, 4 + j, idx(*chip, 1 - c), (x, y, c)).wait_recv()
        for cp in first + passed:
            cp.wait_send()
        for cp in mine:
            cp.wait()

    return pl.pallas_call(
        body, name="allgather_weights",
        in_specs=[_any_spec()] * n, out_specs=[_any_spec()] * n,
        out_shape=[jax.ShapeDtypeStruct((N_DEV,) + s.shape, s.dtype) for s in shards],
        scratch_shapes=[pltpu.SemaphoreType.DMA((n_full, 7)), pltpu.SemaphoreType.DMA((n_full, 7)),
                        pltpu.SemaphoreType.DMA((n,))],
    )(*shards)


def _peer(x, y, c, k):
    tx = 1 - x if (k >> 2) & 1 else x
    ty = 1 - y if (k >> 1) & 1 else y
    tc = 1 - c if k & 1 else c
    return (tx, ty, tc), 4 * tx + 2 * ty + tc


_PEER_ORDER = (2, 4, 6, 3, 5, 7, 1)


_HBM = pl.BlockSpec(memory_space=pltpu.HBM)
_SEM = pl.BlockSpec(memory_space=pltpu.SEMAPHORE)
_EFFECT = pltpu.SideEffectType.DATAFLOW_SIDE_EFFECTING


def _exchange_copies(srcs, lands, send_sems, recv_sems, gather):
    x, y, c, me = _my_pos()
    copies = []
    for k in _PEER_ORDER:
        tgt, tidx = _peer(x, y, c, k)
        for a in range(len(srcs)):
            copies.append(pltpu.make_async_remote_copy(
                src_ref=srcs[a] if gather else srcs[a].at[tidx], dst_ref=lands[a].at[me],
                send_sem=send_sems.at[7 * a + k - 1], recv_sem=recv_sems.at[7 * a + k - 1],
                device_id=tgt, device_id_type=MESH))
    return copies


def _exchange_start(name, srcs, lands=None, after=None):
    n = len(srcs)
    gather = lands is not None
    if lands is None:
        lands = [lax.empty(g.shape, g.dtype) for g in srcs]
    extra = [] if after is None else [after]

    def body(*refs):
        src_refs, land_refs = refs[:n], refs[n:2 * n]
        send_sems, recv_sems = refs[2 * n + len(extra)], refs[2 * n + len(extra) + 1]
        token = refs[-1]
        for cp in _exchange_copies(src_refs, land_refs, send_sems, recv_sems, gather):
            cp.start()
        token[...] = jnp.zeros_like(token)

    hbm = lambda a: pltpu.with_memory_space_constraint(a, pltpu.HBM)
    outs = pl.pallas_call(
        body, name=name,
        out_shape=(pltpu.SemaphoreType.DMA((7 * n,)), pltpu.SemaphoreType.DMA((7 * n,)),
                   *[pltpu.HBM(g.shape, g.dtype) for g in list(srcs) + list(lands)],
                   jax.ShapeDtypeStruct((8, 128), f32)),
        in_specs=[_HBM] * (2 * n) + [pl.BlockSpec(memory_space=pl.ANY)] * len(extra),
        out_specs=(_SEM, _SEM, *([_HBM] * (2 * n)), pl.BlockSpec(memory_space=pltpu.VMEM)),
        input_output_aliases={i: 2 + i for i in range(2 * n)},
        compiler_params=pltpu.CompilerParams(has_side_effects=_EFFECT),
    )(*[hbm(g) for g in srcs], *[hbm(g) for g in lands], *extra)
    return outs[0], outs[1], list(outs[2:2 + n]), list(outs[2 + n:2 + 2 * n]), outs[-1], gather


def _exchange_wait(name, started, after):
    send_sems, recv_sems, srcs, lands, _, gather = started
    n = len(srcs)
    after = list(after) if isinstance(after, (list, tuple)) else [after]

    def body(*refs):
        src_refs, land_refs = refs[:n], refs[n:2 * n]
        s_sems, r_sems = refs[2 * n], refs[2 * n + 1]
        for cp in _exchange_copies(src_refs, land_refs, s_sems, r_sems, gather):
            cp.wait_send()
            cp.wait_recv()

    outs = pl.pallas_call(
        body, name=name,
        out_shape=tuple(pltpu.HBM(a.shape, a.dtype) for a in list(srcs) + list(lands)),
        in_specs=[_HBM] * (2 * n) + [_SEM, _SEM] + [pl.BlockSpec(memory_space=pl.ANY)] * len(after),
        out_specs=tuple([_HBM] * (2 * n)),
        input_output_aliases={i: i for i in range(2 * n)},
        compiler_params=pltpu.CompilerParams(has_side_effects=_EFFECT),
    )(*srcs, *lands, send_sems, recv_sems, *after)
    return list(outs[:n]), list(outs[n:])


SMALL_ROWS = 128


def _small_start(name, sg, after=None):
    return _exchange_start(name, [sg], [lax.empty((N_DEV,) + sg.shape, f32)], after=after)


def _small_sum(name, me, started, after):
    (own,), (slots,) = _exchange_wait(name + "_wait", started, after)

    def body(me_ref, s_ref, own_ref, out_ref):
        acc = None
        for p in range(N_DEV):
            term = lax.cond(me_ref[0] == p, lambda: own_ref[...], lambda p=p: s_ref[p])
            acc = term if acc is None else acc + term
        out_ref[...] = acc

    return pl.pallas_call(
        body, name=name + "_sum",
        in_specs=[pl.BlockSpec(memory_space=pltpu.SMEM), pl.BlockSpec(memory_space=pltpu.VMEM),
                  pl.BlockSpec(memory_space=pltpu.VMEM)],
        out_specs=pl.BlockSpec(memory_space=pltpu.VMEM),
        out_shape=jax.ShapeDtypeStruct(own.shape, f32))(me, slots, own)


def _adam_math(g, w, m, v):
    m = ADAM_B1 * m + (1.0 - ADAM_B1) * g
    v = ADAM_B2 * v + (1.0 - ADAM_B2) * (g * g)
    m_hat = m / (1.0 - ADAM_B1 ** ADAM_STEP)
    v_hat = v / (1.0 - ADAM_B2 ** ADAM_STEP)
    delta = -ADAM_LR * (m_hat / (jnp.sqrt(v_hat) + ADAM_EPS) + ADAM_WD * w)
    return delta, m, v


def _adam_slots(name, me, slots, own, w, m, v, tr, transposed=False):
    rows = slots.shape[1]

    def body(me_ref, s_ref, own_ref, w_ref, m_ref, v_ref, g_ref, d_ref, nm_ref, nv_ref):
        mine = own_ref[...]
        g = None
        for p in range(N_DEV):
            term = lax.cond(me_ref[0] == p, lambda: mine, lambda p=p: s_ref[p]).astype(f32)
            g = term if g is None else g + term
        if transposed:
            g = g.T
        delta, nm, nv = _adam_math(g, w_ref[...], m_ref[...], v_ref[...])
        g_ref[...] = g
        d_ref[...] = delta
        nm_ref[...] = nm
        nv_ref[...] = nv

    mode = dict(pipeline_mode=pl.Buffered(1)) if rows == tr else {}
    if transposed:
        rs = pl.BlockSpec((D, tr), lambda i, me_ref: (0, i))
        rs_in = pl.BlockSpec((D, tr), lambda i, me_ref: (0, i), **mode)
    else:
        rs = pl.BlockSpec((tr, D), lambda i, me_ref: (i, 0))
        rs_in = pl.BlockSpec((tr, D), lambda i, me_ref: (i, 0), **mode)
    return pl.pallas_call(
        body, name=name,
        grid_spec=pltpu.PrefetchScalarGridSpec(
            num_scalar_prefetch=1, grid=(rows // tr,),
            in_specs=[pl.BlockSpec((N_DEV, tr, D), lambda i, me_ref: (0, i, 0), **mode),
                      pl.BlockSpec((None, tr, D), lambda i, me_ref: (me_ref[0], i, 0), **mode), rs_in, rs_in, rs_in],
            out_specs=[rs] * 4),
        out_shape=[jax.ShapeDtypeStruct(w.shape, f32)] * 4,
        compiler_params=_cparams(("parallel",)))(me, slots, own, w, m, v)


def _adam_small(g, w, m, v):
    def body(g_ref, w_ref, m_ref, v_ref, d_ref, nm_ref, nv_ref):
        delta, nm, nv = _adam_math(g_ref[...], w_ref[...], m_ref[...], v_ref[...])
        d_ref[...] = delta
        nm_ref[...] = nm
        nv_ref[...] = nv

    return pl.pallas_call(body, name="adam_small", out_shape=[jax.ShapeDtypeStruct(g.shape, f32)] * 3)(g, w, m, v)


FFN_PAD = 6 * D


_SMALL_PARTS = (("norm1_g", 1), ("gate_b", 2), ("conv_w", CONV_WIDTH), ("conv_b", 1), ("conv_norm_g", 1),
                ("q_norm_g", 1), ("k_norm_g", 1), ("norm2_g", 1), ("ffn_conv_w", 18), ("ffn_conv_b", 6), ("last", 1))


def _small_offsets():
    out, row = {}, 0
    for name, rows in _SMALL_PARTS:
        out[name] = row
        row += -(-rows // 8) * 8
    assert row == SMALL_ROWS
    return out


def _pack_small(norm1_g, gate_b, conv_w, conv_b, conv_norm_g, q_norm_g, k_norm_g, norm2_g, ffn_conv_w, ffn_conv_b,
                last_row=None):
    pad_h = lambda a: jnp.pad(a, ((0, 0), (0, D - HEAD_DIM)))
    pad_f = lambda a: jnp.pad(a, ((0, 0), (0, FFN_PAD - 2 * D_FF))).reshape(-1, D)
    parts = [norm1_g, gate_b.reshape(2, D), conv_w, conv_b, conv_norm_g, pad_h(q_norm_g), pad_h(k_norm_g), norm2_g,
             pad_f(ffn_conv_w), pad_f(ffn_conv_b), jnp.zeros((1, D), f32) if last_row is None else last_row]
    return jnp.concatenate([jnp.pad(p, ((0, -p.shape[0] % 8), (0, 0))) for p in parts], axis=0)


def _unpack_small(p):
    o = _small_offsets()
    rows = lambda name, n: p[o[name]:o[name] + n]
    ffn = lambda a: a.reshape(-1, FFN_PAD)[:, :2 * D_FF]
    return dict(
        norm1_g=rows("norm1_g", 1), gate_b=rows("gate_b", 2).reshape(1, 2 * D), conv_w=rows("conv_w", CONV_WIDTH),
        conv_b=rows("conv_b", 1), conv_norm_g=rows("conv_norm_g", 1), q_norm_g=rows("q_norm_g", 1)[:, :HEAD_DIM],
        k_norm_g=rows("k_norm_g", 1)[:, :HEAD_DIM], norm2_g=rows("norm2_g", 1),
        ffn_conv_w=ffn(rows("ffn_conv_w", 18)), ffn_conv_b=ffn(rows("ffn_conv_b", 6)))


_ADAM_TILE = {896: 128, 704: 704, 128: 128, 352: 176}


def kernel(x, norm1_g, w_in, gate_b, conv_w, conv_b, conv_norm_g, w_conv_out, q_norm_g, k_norm_g, w_attn_out, w_out, norm2_g, w_up, ffn_conv_w, ffn_conv_b, w_down, loss_target, m_norm1_g, m_w_in, m_gate_b, m_conv_w, m_conv_b, m_conv_norm_g, m_w_conv_out, m_q_norm_g, m_k_norm_g, m_w_attn_out, m_w_out, m_norm2_g, m_w_up, m_ffn_conv_w, m_ffn_conv_b, m_w_down, v_norm1_g, v_w_in, v_gate_b, v_conv_w, v_conv_b, v_conv_norm_g, v_w_conv_out, v_q_norm_g, v_k_norm_g, v_w_attn_out, v_w_out, v_norm2_g, v_w_up, v_ffn_conv_w, v_ffn_conv_b, v_w_down):
    BL, S, _ = x.shape
    T = BL * S
    me = 4 * lax.axis_index("x") + 2 * lax.axis_index("y") + lax.axis_index("c")
    xt = x.reshape(T, D)
    target = loss_target.reshape(T, D)

    big = dict(w_in=(w_in[0], m_w_in[0], v_w_in[0]), w_up=(w_up[0], m_w_up[0], v_w_up[0]),
               w_conv_out=(w_conv_out[0], m_w_conv_out[0], v_w_conv_out[0]),
               w_attn_out=(w_attn_out[0], m_w_attn_out[0], v_w_attn_out[0]),
               w_out=(w_out[0], m_w_out[0], v_w_out[0]), w_down=(w_down[0], m_w_down[0], v_w_down[0]))
    order = ["w_in", "w_conv_out", "w_attn_out", "w_out", "w_up", "w_down"]
    shards = [(big[n][0].T if n in ("w_in", "w_up") else big[n][0]).astype(bf16) for n in order]
    gathered = _allgather_rows(shards, 1)
    W = {"w_in": gathered[0].reshape(-1, D)}

    def place_cols(shard, full_cols):
        z = jnp.zeros((shard.shape[0], full_cols), f32)
        return lax.dynamic_update_slice(z, shard, (0, me * shard.shape[1]))

    zr = lambda a: jnp.zeros_like(a)
    conv_local = _pack_small(
        zr(norm1_g), zr(gate_b), place_cols(conv_w[0], D), zr(conv_b), zr(conv_norm_g), zr(q_norm_g), zr(k_norm_g),
        zr(norm2_g), place_cols(ffn_conv_w[0], 2 * D_FF), zr(ffn_conv_b))
    ga_conv = _small_start("gather_conv_start", conv_local, after=gathered[0])
    ga_proj = _exchange_start("gather_start_proj", shards[1:4], gathered[1:4], after=ga_conv[4])
    ga_ffn = _exchange_start("gather_start_ffn", shards[4:6], gathered[4:6], after=ga_proj[4])

    bd = (jnp.arange(128)[:, None] // HEAD_DIM == jnp.arange(128)[None, :] // HEAD_DIM).astype(bf16)
    bias = _attn_bias()
    qg = jnp.tile(q_norm_g, (1, N_HEADS))
    kg = jnp.tile(k_norm_g, (1, N_HEADS))

    z8, h, qn, kn = _in_proj_fwd(xt, norm1_g, W["w_in"], qg, kg, bd, ga_ffn[4])
    conv_all = _unpack_small(_small_sum("gather_conv", me.reshape(1), ga_conv, z8))
    conv_w_full, ffn_w_full = conv_all["conv_w"], conv_all["ffn_conv_w"]
    c = _conv_fwd(z8, conv_w_full, conv_b, S)
    o, ob, lse = _attn_fwd(qn, kn, z8, bias, S)
    for n, g in zip(order[1:4], _exchange_wait("gather_wait_proj", ga_proj, ob)[1]):
        W[n] = g.reshape(-1, D)
    s, ya, yb, mixed = _branches_fwd(c, ob, z8, conv_norm_g, gate_b, W["w_conv_out"], W["w_attn_out"])
    x1, h2 = _out_norm2_fwd(mixed, W["w_out"], xt, norm2_g)
    for n, g in zip(order[4:6], _exchange_wait("gather_wait_ffn", ga_ffn, x1)[1]):
        W[n] = g.reshape(-1, D)
    TNU = D_FF // 2
    u3 = _matmul_call(
        "mm_u", h2, W["w_up"],
        pl.BlockSpec((1024, D), lambda i, j, k: (i, 0)),
        pl.BlockSpec((TNU, D), lambda i, j, k: (j, 0)),
        pl.BlockSpec((None, 1024, TNU), lambda i, j, k: (j // 2, i, j % 2)),
        jax.ShapeDtypeStruct((2, T, D_FF), f32), (T // 1024, 4, 1), "nt", 1, 1024, TNU)
    f = _ffn_fwd(u3, ffn_w_full, ffn_conv_b, S)
    dy, dyb, lacc = _down_loss_fwd(f, W["w_down"], x1, target)
    loss_local = 0.5 / D * jnp.sum(lacc)

    df = _matmul("mm_df", dyb, W["w_down"], "nt", f32, tn=TNU)
    g_w_down = _matmul("mm_dwdn", f, dyb, "tn", bf16, tm=TNU)
    du3, dffn = _ffn_bwd(u3, df, ffn_w_full, ffn_conv_b, S)
    g_w_up = _matmul_call(
        "mm_dwup", du3, h2,
        pl.BlockSpec((None, T, TNU), lambda i, j, k: (i // 2, 0, i % 2)),
        pl.BlockSpec((T, D), lambda i, j, k: (0, 0)),
        pl.BlockSpec((TNU, D), lambda i, j, k: (i, 0)),
        jax.ShapeDtypeStruct((2 * D_FF, D), bf16), (4, 1, 1), "tn", 1, TNU, D)
    blocks8 = lambda a: a.reshape(N_DEV, -1, D)
    ex_ffn = _exchange_start("scatter_start_ffn", [blocks8(g_w_up), blocks8(g_w_down)])
    dx1, dx1b, dg_norm2 = _up_norm2_bwd(du3, W["w_up"], x1, dy, norm2_g, ex_ffn[4])
    g_w_out = _matmul("mm_dwo", mixed, dx1b, "tn", bf16, tm=512)
    dz8 = lax.empty((8, T, D), bf16)
    dya, dyb2, dz8, dg_gate = _out_gate_bwd(dx1b, W["w_out"], z8, gate_b, ya, yb, dz8)
    g_w_conv_out = _matmul("mm_dwco", s, dya, "tn", bf16, tm=512)
    g_w_attn_out = _matmul("mm_dwao", ob, dyb2, "tn", bf16, tm=512)
    ex_proj = _exchange_start("scatter_start_proj", [blocks8(g_w_conv_out), blocks8(g_w_attn_out), blocks8(g_w_out)])
    do = _matmul("mm_do", dyb2, W["w_attn_out"], "nt", f32, after=ex_proj[4])
    dc, dg_convnorm = _convnorm_bwd(dya, W["w_conv_out"], c, conv_norm_g)
    dz8a, dconv = _conv_bwd(dc, z8, conv_w_full, dz8, S)
    dz8b, dg_q, dg_k = _attn_bwd(qn, kn, z8, do, o, lse, bias, bd, qg, kg, dz8a, S)
    g_w_in = _matmul_call(
        "mm_dwin", dz8b, h,
        pl.BlockSpec((None, T, D), lambda i, j, k: (jnp.where(i < 2, i, jnp.where(i < 5, i + 2, i - 3)), 0, 0)),
        pl.BlockSpec((T, D), lambda i, j, k: (0, 0)), pl.BlockSpec((1024, D), lambda i, j, k: (i, 0)),
        jax.ShapeDtypeStruct((7 * D, D), bf16), (7, 1, 1), "tn", 1, D, D)
    ex_in = _exchange_start("scatter_start_in", [blocks8(g_w_in)])
    grad_x, dg_norm1 = _in_norm1_bwd(dz8b, W["w_in"], xt, dx1, norm1_g, ex_in[4])

    sum8 = lambda a: a.reshape(-1, 8, a.shape[-1]).sum(axis=1)
    dconv_s = sum8(dconv.sum(axis=0))
    dffn_s = dffn.sum(axis=0).reshape(2, 4, 8, D_FF).sum(axis=2)
    dffn_w = jnp.concatenate([dffn_s[0, :3], dffn_s[1, :3]], axis=1)
    dffn_b = jnp.concatenate([dffn_s[0, 3:4], dffn_s[1, 3:4]], axis=1)
    fold = lambda a: sum8(a).reshape(N_HEADS, HEAD_DIM).sum(axis=0)[None]
    small_g_local = _pack_small(
        sum8(dg_norm1), sum8(dg_gate), dconv_s[:CONV_WIDTH], dconv_s[CONV_WIDTH:], sum8(dg_convnorm),
        fold(dg_q), fold(dg_k), sum8(dg_norm2), dffn_w, dffn_b,
        last_row=jnp.pad(loss_local.reshape(1, 1), ((0, 0), (0, D - 1))))
    sg_start = _small_start("small_grads_start", small_g_local)

    place_m = lambda a, full: place_cols(a[0], full)
    small_w_true = _pack_small(norm1_g, gate_b, conv_w_full, conv_b, conv_norm_g, q_norm_g, k_norm_g, norm2_g,
                               ffn_w_full, ffn_conv_b)
    small_m = _pack_small(m_norm1_g, m_gate_b, place_m(m_conv_w, D), m_conv_b, m_conv_norm_g, m_q_norm_g, m_k_norm_g,
                          m_norm2_g, place_m(m_ffn_conv_w, 2 * D_FF), m_ffn_conv_b)
    small_v = _pack_small(v_norm1_g, v_gate_b, place_m(v_conv_w, D), v_conv_b, v_conv_norm_g, v_q_norm_g, v_k_norm_g,
                          v_norm2_g, place_m(v_ffn_conv_w, 2 * D_FF), v_ffn_conv_b)

    own, slots = {}, {}
    for tag, ex, names_ in (("ffn", ex_ffn, ("w_up", "w_down")),
                            ("proj", ex_proj, ("w_conv_out", "w_attn_out", "w_out")), ("in", ex_in, ("w_in",))):
        sent, landed = _exchange_wait("scatter_wait_" + tag, ex, [sg_start[4], small_w_true, small_m, small_v])
        for n, src, land in zip(names_, sent, landed):
            own[n], slots[n] = src, land

    res, adam_done = {}, []
    for n in order:
        w, m, v = big[n]
        outs = _adam_slots("adam_" + n, me.reshape(1), slots[n], own[n], w, m, v, _ADAM_TILE[slots[n].shape[1]],
                           transposed=n in ("w_in", "w_up"))
        adam_done.append(outs[0])
        res[n] = [a[None] for a in outs]
    small_g = _small_sum("small_grads", me.reshape(1), sg_start, adam_done)
    loss = small_g[_small_offsets()["last"], 0]

    col = lambda a, width: lax.dynamic_slice(a, (0, me * width), (a.shape[0], width))
    sd, sm, sv = _adam_small(small_g, small_w_true, small_m, small_v)
    for i, packed in enumerate((small_g, sd, sm, sv)):
        u = _unpack_small(packed)
        u["conv_w"] = col(u["conv_w"], D // N_DEV)
        u["ffn_conv_w"] = col(u["ffn_conv_w"], 2 * D_FF // N_DEV)
        for n, a in u.items():
            res.setdefault(n, [None] * 4)[i] = a[None] if n in ("conv_w", "ffn_conv_w") else a

    names = ["norm1_g", "w_in", "gate_b", "conv_w", "conv_b", "conv_norm_g", "w_conv_out", "q_norm_g", "k_norm_g",
             "w_attn_out", "w_out", "norm2_g", "w_up", "ffn_conv_w", "ffn_conv_b", "w_down"]
    out = [loss, grad_x.reshape(BL, S, D)]
    for i in range(4):
        out += [res[n][i] for n in names]
    return tuple(out)
```

```python
import functools

import jax
import jax.numpy as jnp
import numpy as np
from jax import lax
from jax.experimental import pallas as pl
from jax.experimental.pallas import tpu as pltpu

f32 = jnp.float32
bf16 = jnp.bfloat16

D = 1024
N_HEADS = 16
HEAD_DIM = 64
CONV_WIDTH = 31
D_FF = 2816
GROUPS = ((128, 1), (512, 4), (2048, 16))
ATTN_BLOCK = 128
EPS = 1e-6
N_DEV = 8
MESH = pl.DeviceIdType.MESH

ADAM_LR = 0.001
ADAM_B1 = 0.9
ADAM_B2 = 0.999
ADAM_EPS = 1e-08
ADAM_WD = 0.01
ADAM_STEP = 10

VMEM_LIMIT = 56 * 1024 * 1024
MASK_BIAS = 1e30

Z_AVAL, Z_AGATE, Z_GA, Z_GB, Z_Q, Z_K, Z_V = 0, 1, 2, 3, 4, 5, 6


_W_OF_Z = (0, 1, 5, 6, 2, 3, 4)


def _wsec_of_zsec(j):
    return jnp.where(j < 2, j, jnp.where(j < 4, j + 3, j - 2))


def _sig(x):
    return 1.0 / (1.0 + jnp.exp(-x))


def _colsum8(x):
    return x.reshape(-1, 8, x.shape[-1]).sum(axis=0)


def _cparams(sem):
    return pltpu.CompilerParams(dimension_semantics=sem, vmem_limit_bytes=VMEM_LIMIT)


def _my_pos():
    x, y, c = lax.axis_index("x"), lax.axis_index("y"), lax.axis_index("c")
    return x, y, c, 4 * x + 2 * y + c


_DIMS = {"nn": ((1,), (0,)), "nt": ((1,), (1,)), "tn": ((0,), (0,))}


def _matmul_call(name, a, b, a_spec, b_spec, o_spec, out_shape, grid, mode, nk, tm, tn, after=None):
    dims = (_DIMS[mode], ((), ()))
    extra = [] if after is None else [after]

    def body(a_ref, b_ref, *rest):
        o_ref, scratch = rest[len(extra)], rest[len(extra) + 1:]
        part = lax.dot_general(a_ref[...], b_ref[...], dims, preferred_element_type=f32)
        if nk == 1:
            o_ref[...] = part.astype(o_ref.dtype)
        else:
            acc = scratch[0]
            k = pl.program_id(2)

            @pl.when(k == 0)
            def _():
                acc[...] = part

            @pl.when(k > 0)
            def _():
                acc[...] += part

            @pl.when(k == nk - 1)
            def _():
                o_ref[...] = acc[...].astype(o_ref.dtype)

    scratch = [] if nk == 1 else [pltpu.VMEM((tm, tn), f32)]
    return pl.pallas_call(
        body, name=name, grid=grid, in_specs=[a_spec, b_spec] + [pl.BlockSpec(memory_space=pl.ANY)] * len(extra),
        out_specs=o_spec, out_shape=out_shape,
        scratch_shapes=scratch, compiler_params=_cparams(("parallel", "parallel", "arbitrary")),
    )(a, b, *extra)


def _matmul(name, a, b, mode, out_dtype, tm=1024, tn=1024, tk=None, after=None):
    if mode == "nn":
        (M, K), (_, N) = a.shape, b.shape
    elif mode == "nt":
        (M, K), (N, _) = a.shape, b.shape
    else:
        (K, M), (_, N) = a.shape, b.shape
    tm, tn = min(tm, M), min(tn, N)
    tk = K if tk is None else tk
    nk = K // tk
    assert M % tm == 0 and N % tn == 0 and K % tk == 0
    if mode == "tn":
        a_spec = pl.BlockSpec((tk, tm), lambda i, j, k: (k, i))
    else:
        a_spec = pl.BlockSpec((tm, tk), lambda i, j, k: (i, k))
    if mode == "nt":
        b_spec = pl.BlockSpec((tn, tk), lambda i, j, k: (j, k))
    else:
        b_spec = pl.BlockSpec((tk, tn), lambda i, j, k: (k, j))
    o_spec = pl.BlockSpec((tm, tn), lambda i, j, k: (i, j))
    return _matmul_call(name, a, b, a_spec, b_spec, o_spec, jax.ShapeDtypeStruct((M, N), out_dtype),
                        (M // tm, N // tn, nk), mode, nk, tm, tn, after=after)


FTM = 512


def _matmul_fused(name, a, b, pairs, epilogue, extras, consts, outs, nt=False, sums=False, passed=(), aliases=None):
    sa, M, kk = a.shape
    na = max(i for i, _ in pairs) + 1
    ne, nc, npass = len(extras), len(consts), len(passed)
    dims = (_DIMS["nt" if nt else "nn"], ((), ()))

    def body(a_ref, b_ref, *rest):
        acc = None
        for i, j in pairs:
            part = lax.dot_general(a_ref[i], b_ref[j], dims, preferred_element_type=f32)
            acc = part if acc is None else acc + part
        epilogue(acc, rest[:ne], rest[ne:ne + nc], rest[ne + nc + npass:])

    whole = lambda arr: pl.BlockSpec(arr.shape, lambda i, nd=arr.ndim: (0,) * nd, pipeline_mode=pl.Buffered(1))
    io_alias = {2 + ne + nc + k: v for k, v in (aliases or {}).items()}
    return pl.pallas_call(
        body, name=name, grid=(M // FTM,),
        in_specs=[pl.BlockSpec((na, FTM, kk), lambda i: (0, i, 0)), whole(b)] + [s for _, s in extras]
        + [whole(c) for c in consts] + [pl.BlockSpec(memory_space=pl.ANY)] * npass,
        out_specs=[s for _, s in outs], out_shape=[s for s, _ in outs], input_output_aliases=io_alias,
        compiler_params=_cparams(("arbitrary" if sums else "parallel",)),
    )(a, b, *[x for x, _ in extras], *consts, *passed)


def _frows(c=D):
    return pl.BlockSpec((FTM, c), lambda i: (i, 0))


def _fsec(s):
    return pl.BlockSpec((None, FTM, D), lambda i: (s, i, 0))


def _rowshape(T, dtype, c=D):
    return (jax.ShapeDtypeStruct((T, c), dtype), _frows(c))


def _sumshape(c=D):
    return (jax.ShapeDtypeStruct((8, c), f32), pl.BlockSpec((8, c), lambda i: (0, 0)))


def _add_colsum(ref, x, cols=None):
    @pl.when(pl.program_id(0) == 0)
    def _():
        if cols is None:
            ref[...] = jnp.zeros_like(ref)
        else:
            ref[:, cols] = jnp.zeros((8, x.shape[-1]), f32)

    if cols is None:
        ref[...] += _colsum8(x)
    else:
        ref[:, cols] += _colsum8(x)


def _rms(x):
    return lax.rsqrt(jnp.mean(x * x, axis=-1, keepdims=True) + EPS)


def _rms_bwd(dy_g, xn, rstd):
    return rstd * (dy_g - xn * jnp.mean(dy_g * xn, axis=-1, keepdims=True))


def _head_sum(x, bd):
    parts = []
    for cb in range(x.shape[-1] // 128):
        xb = x[:, cb * 128:(cb + 1) * 128]
        hi = xb.astype(bf16)
        lo = (xb - hi.astype(f32)).astype(bf16)
        parts.append(jnp.dot(hi, bd, preferred_element_type=f32) + jnp.dot(lo, bd, preferred_element_type=f32))
    return parts[0] if len(parts) == 1 else jnp.concatenate(parts, axis=1)


ZTM = 1024


def _in_proj_fwd(x, g, w_in_t, qg, kg, bd, after):
    T = x.shape[0]
    nt = T // ZTM

    def body(x_ref, g_ref, w_ref, qg_ref, kg_ref, bd_ref, after_ref, z_ref, h_ref, qn_ref, kn_ref, hbuf):
        del after_ref
        j, i = pl.program_id(0), pl.program_id(1)
        rows = pl.ds(pl.multiple_of(i * ZTM, ZTM), ZTM)

        @pl.when(j == 0)
        def _():
            xv = x_ref[...]
            hv = (xv * _rms(xv) * g_ref[...]).astype(bf16)
            hbuf[rows, :] = hv
            h_ref[...] = hv

        z = lax.dot_general(hbuf[rows, :], w_ref[...], (_DIMS["nt"], ((), ())), preferred_element_type=f32)
        z_ref[...] = z

        def head_norm(gain_ref, scale):
            return z * lax.rsqrt(_head_sum(z * z, bd_ref[...]) * (1.0 / HEAD_DIM) + EPS) * gain_ref[...] * scale

        @pl.when(j == Z_Q)
        def _():
            qn_ref[...] = head_norm(qg_ref, HEAD_DIM ** -0.5)

        @pl.when(j == Z_K)
        def _():
            kn_ref[...] = head_norm(kg_ref, 1.0)

    def tile_at(sec):
        return pl.BlockSpec((ZTM, D), lambda j, i: (jnp.where(j < sec, 0, jnp.where(j == sec, i, nt - 1)), 0))

    row = pl.BlockSpec((1, D), lambda j, i: (0, 0))
    return pl.pallas_call(
        body, name="mm_z", grid=(7, nt),
        in_specs=[tile_at(0), row, pl.BlockSpec((D, D), lambda j, i: (_wsec_of_zsec(j), 0)), row, row,
                  pl.BlockSpec((128, 128), lambda j, i: (0, 0)), pl.BlockSpec(memory_space=pl.ANY)],
        out_specs=[pl.BlockSpec((None, ZTM, D), lambda j, i: (j, i, 0)), tile_at(0), tile_at(Z_Q), tile_at(Z_K)],
        out_shape=[jax.ShapeDtypeStruct((8, T, D), f32), jax.ShapeDtypeStruct((T, D), bf16),
                   jax.ShapeDtypeStruct((T, D), f32), jax.ShapeDtypeStruct((T, D), f32)],
        scratch_shapes=[pltpu.VMEM((T, D), bf16)],
        compiler_params=_cparams(("arbitrary", "arbitrary")))(x, g, w_in_t, qg, kg, bd, after)


def _branches_fwd(c, ob, z8, g, gate_b, w_conv_out, w_attn_out):
    T = c.shape[0]

    def epilogue(yb, extra, const, out):
        cv = extra[0][...]
        r = cv * _rms(cv) * const[0][...]
        s = (r * _sig(r)).astype(bf16)
        ya = jnp.dot(s, const[2][...], preferred_element_type=f32)
        b_ref = const[1]
        g_a = _sig(extra[1][...] + b_ref[:, :D])
        g_b = _sig(extra[2][...] + b_ref[:, D:])
        out[0][...] = s
        out[1][...] = ya
        out[2][...] = yb
        out[3][...] = (g_a * ya + g_b * yb).astype(bf16)

    return _matmul_fused("mm_branches", ob[None], w_attn_out[None], ((0, 0),), epilogue,
                         [(c, _frows()), (z8, _fsec(Z_GA)), (z8, _fsec(Z_GB))], [g, gate_b, w_conv_out],
                         [_rowshape(T, bf16), _rowshape(T, f32), _rowshape(T, f32), _rowshape(T, bf16)])


def _out_norm2_fwd(mixed, w_out, x, g):
    T = x.shape[0]

    def epilogue(acc, extra, const, out):
        x1 = extra[0][...] + acc
        out[0][...] = x1
        out[1][...] = (x1 * _rms(x1) * const[0][...]).astype(bf16)

    return _matmul_fused("mm_t1_norm2", mixed[None], w_out[None], ((0, 0),), epilogue, [(x, _frows())], [g],
                         [_rowshape(T, f32), _rowshape(T, bf16)])


def _down_loss_fwd(f, w_down, x1, target):
    T = x1.shape[0]

    def epilogue(acc, extra, const, out):
        diff = extra[0][...] + acc - extra[1][...]
        dy = diff * (1.0 / D)
        out[0][...] = dy
        out[1][...] = dy.astype(bf16)
        _add_colsum(out[2], diff * diff)

    return _matmul_fused("mm_t2_loss", f[None], w_down[None], ((0, 0),), epilogue, [(x1, _frows()), (target, _frows())],
                         [], [_rowshape(T, f32), _rowshape(T, bf16), _sumshape()], sums=True)


def _up_norm2_bwd(du3, w_up_t, x1, dy, g, token):
    T = x1.shape[0]

    def epilogue(dh, extra, const, out):
        x1v = extra[0][...]
        rstd = _rms(x1v)
        xn = x1v * rstd
        dx1 = extra[1][...] + _rms_bwd(dh * const[0][...], xn, rstd)
        out[0][...] = dx1
        out[1][...] = dx1.astype(bf16)
        _add_colsum(out[2], dh * xn)

    return _matmul_fused("mm_dh2_norm2", du3, w_up_t.reshape(2, D_FF, D), ((0, 0), (1, 1)), epilogue,
                         [(x1, _frows()), (dy, _frows())], [g],
                         [_rowshape(T, f32), _rowshape(T, bf16), _sumshape()], sums=True, passed=[token])


def _out_gate_bwd(dx1b, w_out, z8, gate_b, ya, yb, dz8):
    T = ya.shape[0]

    def epilogue(dm, extra, const, out):
        b_ref = const[0]
        g_a = _sig(extra[0][...] + b_ref[:, :D])
        g_b = _sig(extra[1][...] + b_ref[:, D:])
        out[0][...] = (dm * g_a).astype(bf16)
        out[1][...] = (dm * g_b).astype(bf16)
        dla = dm * extra[2][...] * g_a * (1.0 - g_a)
        dlb = dm * extra[3][...] * g_b * (1.0 - g_b)
        out[2][0] = dla.astype(bf16)
        out[2][1] = dlb.astype(bf16)
        _add_colsum(out[3], dla, slice(0, D))
        _add_colsum(out[3], dlb, slice(D, 2 * D))

    return _matmul_fused(
        "mm_dmixed_gate", dx1b[None], w_out[None], ((0, 0),), epilogue,
        [(z8, _fsec(Z_GA)), (z8, _fsec(Z_GB)), (ya, _frows()), (yb, _frows())], [gate_b],
        [_rowshape(T, bf16), _rowshape(T, bf16),
         (jax.ShapeDtypeStruct(dz8.shape, bf16), pl.BlockSpec((2, FTM, D), lambda i: (1, i, 0))), _sumshape(2 * D)],
        nt=True, sums=True, passed=[dz8], aliases={0: 2})


def _convnorm_bwd(dya, w_conv_out, c, g):
    T = c.shape[0]

    def epilogue(ds, extra, const, out):
        cv = extra[0][...]
        rstd = _rms(cv)
        r0 = cv * rstd
        gv = const[0][...]
        r = r0 * gv
        sg = _sig(r)
        dr = ds * sg * (1.0 + r * (1.0 - sg))
        out[0][...] = _rms_bwd(dr * gv, r0, rstd)
        _add_colsum(out[1], dr * r0)

    return _matmul_fused("mm_ds_convnorm", dya[None], w_conv_out[None], ((0, 0),), epilogue, [(c, _frows())], [g],
                         [_rowshape(T, f32), _sumshape()], nt=True, sums=True)


def _in_norm1_bwd(dz8, w_in_t, x, dx1, g, token):
    T = x.shape[0]

    def epilogue(dh, extra, const, out):
        xv = extra[0][...]
        rstd = _rms(xv)
        xn = xv * rstd
        out[0][...] = extra[1][...] + _rms_bwd(dh * const[0][...], xn, rstd)
        _add_colsum(out[1], dh * xn)

    return _matmul_fused("mm_dh_norm1", dz8, w_in_t.reshape(7, D, D), tuple(zip(range(7), _W_OF_Z)), epilogue,
                         [(x, _frows()), (dx1, _frows())], [g], [_rowshape(T, f32), _sumshape()],
                         sums=True, passed=[token])


CCW = 256
CR = 64
HALO = 32


def _conv_fwd(z8, conv_w, conv_b, S):
    T = z8.shape[1]
    nb = T // S
    ncb = D // CCW

    def body(av_ref, ag_ref, w_ref, b_ref, c_ref, pad):
        pad[0:HALO, :] = jnp.zeros((HALO, CCW), f32)

        def fill(i, carry):
            r0 = pl.multiple_of(i * 256, 256)
            pad[pl.ds(HALO + r0, 256), :] = av_ref[pl.ds(r0, 256), :] * _sig(ag_ref[pl.ds(r0, 256), :])
            return carry

        lax.fori_loop(0, S // 256, fill, 0)
        bias = b_ref[...]

        def chunk(i, carry):
            r0 = pl.multiple_of(i * CR, CR)
            win = pad[pl.ds(r0, CR + HALO), :]
            acc = jnp.zeros((CR, CCW), f32) + bias
            for s in range(8):
                part = None
                for m in range((CONV_WIDTH - 1 - s) // 8 + 1):
                    j = CONV_WIDTH - 1 - 8 * m - s
                    term = win[24 - 8 * m:24 - 8 * m + CR + 8, :] * w_ref[j:j + 1, :]
                    part = term if part is None else part + term
                acc = acc + part[8 - s:8 - s + CR, :]
            c_ref[pl.ds(r0, CR), :] = acc
            return carry

        lax.fori_loop(0, S // CR, chunk, 0)

    zs = lambda s: pl.BlockSpec((None, S, CCW), lambda b, cb: (s, b, cb))
    return pl.pallas_call(
        body, name="conv_fwd", grid=(nb, ncb),
        in_specs=[zs(Z_AVAL), zs(Z_AGATE), pl.BlockSpec((CONV_WIDTH, CCW), lambda b, cb: (0, cb)),
                  pl.BlockSpec((1, CCW), lambda b, cb: (0, cb))],
        out_specs=pl.BlockSpec((S, CCW), lambda b, cb: (b, cb)),
        out_shape=jax.ShapeDtypeStruct((T, D), f32),
        scratch_shapes=[pltpu.VMEM((S + HALO, CCW), f32)],
        compiler_params=_cparams(("parallel", "parallel")))(z8, z8, conv_w, conv_b)


def _conv_bwd(dc, z8, conv_w, dz8, S):
    T = dc.shape[0]
    nb = T // S
    ncb = D // CCW

    def body(dc_ref, av_ref, ag_ref, w_ref, dz_in, dz_ref, dw_ref, apad, dpad, shbuf):
        del dz_in
        apad[0:HALO, :] = jnp.zeros((HALO, CCW), f32)
        dpad[S:S + HALO, :] = jnp.zeros((HALO, CCW), f32)
        dw_ref[...] = jnp.zeros_like(dw_ref)

        def fill(i, carry):
            r0 = pl.multiple_of(i * 256, 256)
            apad[pl.ds(HALO + r0, 256), :] = av_ref[pl.ds(r0, 256), :] * _sig(ag_ref[pl.ds(r0, 256), :])
            dpad[pl.ds(r0, 256), :] = dc_ref[pl.ds(r0, 256), :]
            return carry

        lax.fori_loop(0, S // 256, fill, 0)

        def chunk(i, carry):
            r0 = pl.multiple_of(i * CR, CR)
            dwin = dpad[pl.ds(r0, CR + HALO), :]
            da = jnp.zeros((CR, CCW), f32)
            for s in range(8):
                shbuf[...] = dwin[s:s + CR, :]
                dshift = shbuf[...]
                part = None
                for m in range((CONV_WIDTH - 1 - s) // 8 + 1):
                    j = CONV_WIDTH - 1 - 8 * m - s
                    term = dwin[8 * m:8 * m + CR + 8, :] * w_ref[j:j + 1, :]
                    part = term if part is None else part + term
                    a_lag = apad[pl.ds(r0 + HALO - 8 * m, CR), :]
                    dw_ref[8 * j:8 * j + 8, :] += _colsum8(dshift * a_lag)
                da = da + part[s:s + CR, :]
            dw_ref[8 * CONV_WIDTH:8 * CONV_WIDTH + 8, :] += _colsum8(dwin[0:CR, :])
            av = av_ref[pl.ds(r0, CR), :]
            sg = _sig(ag_ref[pl.ds(r0, CR), :])
            dz_ref[0, pl.ds(r0, CR), :] = (da * sg).astype(bf16)
            dz_ref[1, pl.ds(r0, CR), :] = (da * av * sg * (1.0 - sg)).astype(bf16)
            return carry

        lax.fori_loop(0, S // CR, chunk, 0)

    zs = lambda s: pl.BlockSpec((None, S, CCW), lambda b, cb: (s, b, cb))
    return pl.pallas_call(
        body, name="conv_bwd", grid=(nb, ncb),
        in_specs=[pl.BlockSpec((S, CCW), lambda b, cb: (b, cb)), zs(Z_AVAL), zs(Z_AGATE),
                  pl.BlockSpec((CONV_WIDTH, CCW), lambda b, cb: (0, cb)), pl.BlockSpec(memory_space=pl.ANY)],
        out_specs=[pl.BlockSpec((2, S, CCW), lambda b, cb: (0, b, cb)),
                   pl.BlockSpec((None, 256, CCW), lambda b, cb: (b, 0, cb))],
        out_shape=[jax.ShapeDtypeStruct(dz8.shape, bf16), jax.ShapeDtypeStruct((nb, 256, D), f32)],
        input_output_aliases={4: 0},
        scratch_shapes=[pltpu.VMEM((S + HALO, CCW), f32), pltpu.VMEM((S + HALO, CCW), f32),
                        pltpu.VMEM((CR, CCW), f32)],
        compiler_params=_cparams(("parallel", "parallel")))(dc, z8, z8, conv_w, dz8)


FR = 128
NFB = D_FF // CCW
FBW = 128


def _ffn_window(ref, i, r0):
    return ref[pl.ds(r0 - 8, FR + 8), :]


def _ffn_u(win, w_ref, b_ref):
    return (win[6:6 + FR, :] * w_ref[0:1, :] + win[7:7 + FR, :] * w_ref[1:2, :]
            + win[8:8 + FR, :] * w_ref[2:3, :] + b_ref[...])


def _ffn_fwd(u3, ffn_w, ffn_b, S):
    T = u3.shape[1]
    nb = T // S

    def body(uv_ref, ug_ref, wv_ref, wg_ref, bv_ref, bg_ref, f_ref):
        def chunk(first, i):
            r0 = 0 if first else pl.multiple_of(i * FR, FR)
            if first:
                z = jnp.zeros((8, CCW), f32)
                wv = jnp.concatenate([z, uv_ref[0:FR, :]], axis=0)
                wg = jnp.concatenate([z, ug_ref[0:FR, :]], axis=0)
            else:
                wv = _ffn_window(uv_ref, i, r0)
                wg = _ffn_window(ug_ref, i, r0)
            u_val = _ffn_u(wv, wv_ref, bv_ref)
            u_gate = _ffn_u(wg, wg_ref, bg_ref)
            f_ref[pl.ds(r0, FR), :] = (u_gate * _sig(u_gate) * u_val).astype(bf16)

        chunk(True, 0)

        def loop(i, carry):
            chunk(False, i)
            return carry

        lax.fori_loop(1, S // FR, loop, 0)

    us = lambda h: pl.BlockSpec((None, S, CCW), lambda b, cb: (h, b, cb))
    ws = lambda h: pl.BlockSpec((3, CCW), lambda b, cb: (0, h * NFB + cb))
    bs = lambda h: pl.BlockSpec((1, CCW), lambda b, cb: (0, h * NFB + cb))
    return pl.pallas_call(
        body, name="ffn_fwd", grid=(nb, NFB),
        in_specs=[us(0), us(1), ws(0), ws(1), bs(0), bs(1)],
        out_specs=pl.BlockSpec((S, CCW), lambda b, cb: (b, cb)),
        out_shape=jax.ShapeDtypeStruct((T, D_FF), bf16),
        compiler_params=_cparams(("parallel", "parallel")))(u3, u3, ffn_w, ffn_w, ffn_b, ffn_b)


def _ffn_bwd(u3, df, ffn_w, ffn_b, S):
    T = u3.shape[1]
    nb = T // S

    def body(uv_ref, ug_ref, df_ref, wv_ref, wg_ref, bv_ref, bg_ref, du_ref, dw_ref, dvpad, dgpad, shbuf):
        dvpad[S:S + 8, :] = jnp.zeros((8, FBW), f32)
        dgpad[S:S + 8, :] = jnp.zeros((8, FBW), f32)
        dw_ref[...] = jnp.zeros_like(dw_ref)

        def chunk(first, i):
            r0 = 0 if first else pl.multiple_of(i * FR, FR)
            if first:
                z = jnp.zeros((8, FBW), f32)
                wv = jnp.concatenate([z, uv_ref[0:FR, :]], axis=0)
                wg = jnp.concatenate([z, ug_ref[0:FR, :]], axis=0)
            else:
                wv = _ffn_window(uv_ref, i, r0)
                wg = _ffn_window(ug_ref, i, r0)
            taps = []
            for h, win in enumerate((wv, wg)):
                shbuf[2 * h] = win[6:6 + FR, :]
                shbuf[2 * h + 1] = win[7:7 + FR, :]
                taps.append((shbuf[2 * h], shbuf[2 * h + 1], win[8:8 + FR, :]))
            conv = lambda x, w_ref, b_ref: (x[0] * w_ref[0:1, :] + x[1] * w_ref[1:2, :] + x[2] * w_ref[2:3, :]
                                            + b_ref[...])
            u_val = conv(taps[0], wv_ref, bv_ref)
            u_gate = conv(taps[1], wg_ref, bg_ref)
            dfc = df_ref[pl.ds(r0, FR), :]
            sg = _sig(u_gate)
            d_val = dfc * u_gate * sg
            d_gate = dfc * u_val * sg * (1.0 + u_gate * (1.0 - sg))
            dvpad[pl.ds(r0, FR), :] = d_val
            dgpad[pl.ds(r0, FR), :] = d_gate
            for h, dd in enumerate((d_val, d_gate)):
                for j in range(3):
                    dw_ref[h, 8 * j:8 * j + 8, :] += _colsum8(dd * taps[h][j])
                dw_ref[h, 24:32, :] += _colsum8(dd)

        chunk(True, 0)

        def loop(i, carry):
            chunk(False, i)
            return carry

        lax.fori_loop(1, S // FR, loop, 0)

        def back(i, carry):
            r0 = pl.multiple_of(i * FR, FR)
            for h, (dpad, w_ref) in enumerate(((dvpad, wv_ref), (dgpad, wg_ref))):
                win = dpad[pl.ds(r0, FR + 8), :]
                du = (win[0:FR, :] * w_ref[2:3, :] + win[1:1 + FR, :] * w_ref[1:2, :]
                      + win[2:2 + FR, :] * w_ref[0:1, :])
                du_ref[h, pl.ds(r0, FR), :] = du.astype(bf16)
            return carry

        lax.fori_loop(0, S // FR, back, 0)

    ncb = D_FF // FBW
    us = lambda h: pl.BlockSpec((None, S, FBW), lambda b, cb: (h, b, cb))
    ws = lambda h: pl.BlockSpec((3, FBW), lambda b, cb: (0, h * ncb + cb))
    bs = lambda h: pl.BlockSpec((1, FBW), lambda b, cb: (0, h * ncb + cb))
    return pl.pallas_call(
        body, name="ffn_bwd", grid=(nb, ncb),
        in_specs=[us(0), us(1), pl.BlockSpec((S, FBW), lambda b, cb: (b, cb)), ws(0), ws(1), bs(0), bs(1)],
        out_specs=[pl.BlockSpec((2, S, FBW), lambda b, cb: (0, b, cb)),
                   pl.BlockSpec((None, 2, 32, FBW), lambda b, cb: (b, 0, 0, cb))],
        out_shape=[jax.ShapeDtypeStruct((2, T, D_FF), bf16), jax.ShapeDtypeStruct((nb, 2, 32, D_FF), f32)],
        scratch_shapes=[pltpu.VMEM((S + 8, FBW), f32), pltpu.VMEM((S + 8, FBW), f32),
                        pltpu.VMEM((4, FR, FBW), f32)],
        compiler_params=_cparams(("parallel", "parallel")))(u3, u3, df, ffn_w, ffn_w, ffn_b, ffn_b)


AB = ATTN_BLOCK


def _attn_bias_np():
    slopes = (np.float32(2.0) ** (np.float32(-8.0) * np.arange(1, N_HEADS + 1, dtype=np.float32)
                                  / np.float32(N_HEADS))).astype(np.float32)
    steps = (np.arange(AB)[:, None] + AB) - np.arange(2 * AB)[None, :]
    own = (np.arange(2 * AB) >= AB)[None, :]
    out = []
    for window, dil in GROUPS:
        valid = (steps >= 0) & (steps <= window // dil)
        dist = slopes[:, None, None] * (steps * dil).astype(np.float32)[None]
        kinds = [np.where(v[None], dist, np.float32(MASK_BIAS)) for v in (valid, valid & own)]
        out.append(np.stack(kinds, axis=1))
    return np.stack(out).astype(np.float32)


def _attn_bias():
    return jnp.asarray(_attn_bias_np())


def _head_masks():
    lane = lax.broadcasted_iota(jnp.int32, (1, 128), 1)
    return (lane < HEAD_DIM, lane >= HEAD_DIM)


def _perm_chunks(S, d):
    L = S // d
    ch = min(L, 256)
    out = []
    for r in range(d):
        for c in range(L // ch):
            start = r + d * ch * c
            out.append((pl.ds(start, ch, stride=d) if d > 1 else pl.ds(start, ch), r * L + c * ch, ch))
    return out


def _stack_heads(x, masks):
    return jnp.concatenate([jnp.where(masks[0], x, 0), jnp.where(masks[1], x, 0)], axis=0)


def _block_row(j):
    return j * AB if isinstance(j, int) else pl.multiple_of(j * AB, AB)


def _three_stages(n, stage_a, stage_b, stage_c, unroll):
    stage_a(0)
    stage_a(1)
    stage_b(0)

    def body(j, carry):
        stage_c(j - 1)
        stage_b(j)
        stage_a(j + 1)
        return carry

    lax.fori_loop(1, n - 1, body, 0, unroll=unroll)
    stage_c(n - 2)
    stage_b(n - 1)
    stage_c(n - 1)


_NT = (((1,), (1,)), ((), ()))
_TN = (((0,), (0,)), ((), ()))
SCH = 128


def _attn_fwd(qn, kn, z8, bias, S):
    T = qn.shape[0]
    nb = T // S
    nblk = S // AB

    def body(q_ref, k_ref, v_ref, bias_ref, o_ref, ob_ref, lse_ref, qs, ks, vs, s2, p2, ogp, lgp, *group_scratch):
        og, lg = group_scratch[:3], group_scratch[3:]
        masks = _head_masks()
        ks[0:AB, :] = jnp.zeros((AB, 128), bf16)
        vs[0:AB, :] = jnp.zeros((AB, 128), bf16)

        for g, (_, d) in enumerate(GROUPS):
            nsub = S // (d * AB)
            chunks = _perm_chunks(S, d)
            for src, dst, ch in chunks:
                qs[dst:dst + ch, :] = q_ref[src, :].astype(bf16)
                ks[AB + dst:AB + dst + ch, :] = k_ref[src, :].astype(bf16)
                vs[AB + dst:AB + dst + ch, :] = v_ref[src, :].astype(bf16)
            od, ld = (og[g], lg[g]) if d == 1 else (ogp, lgp)

            def scores(j):
                r0 = _block_row(j)
                q2 = _stack_heads(qs[pl.ds(r0, AB), :], masks)
                s2[j] = lax.dot_general(q2, ks[pl.ds(r0, 2 * AB), :], _NT, preferred_element_type=f32)

            def softmax(j, g=g, nsub=nsub, ld=ld):
                r0 = _block_row(j)
                kind = int(j % nsub == 0) if isinstance(j, int) else (j % nsub == 0).astype(jnp.int32)
                for cc in range(AB // SCH):
                    lses = []
                    for hh in range(2):
                        rows = pl.ds(hh * AB + cc * SCH, SCH)
                        sb = s2[j, rows, :] - bias_ref[g, hh, kind, cc * SCH:(cc + 1) * SCH, :]
                        m = jnp.max(sb, axis=-1, keepdims=True)
                        p = jnp.exp(sb - m)
                        den = jnp.sum(p, axis=-1, keepdims=True)
                        p2[j, rows, :] = (p * (1.0 / den)).astype(bf16)
                        lses.append(m + jnp.log(den))
                    ld[pl.ds(r0 + cc * SCH, SCH), :] = jnp.where(masks[0], lses[0], lses[1])

            def values(j, od=od):
                r0 = _block_row(j)
                pv2 = jnp.dot(p2[j], vs[pl.ds(r0, 2 * AB), :], preferred_element_type=f32)
                od[pl.ds(r0, AB), :] = jnp.where(masks[0], pv2[:AB], pv2[AB:])

            _three_stages(nblk, scores, softmax, values, nblk - 2)

            if d > 1:
                for src, dst, ch in chunks:
                    og[g][src, :] = ogp[dst:dst + ch, :]
                    lg[g][src, :] = lgp[dst:dst + ch, :]

        def combine(i, carry):
            rr = pl.ds(pl.multiple_of(i * 256, 256), 256)
            l0, l1, l2 = lg[0][rr, :], lg[1][rr, :], lg[2][rr, :]
            mx = jnp.maximum(jnp.maximum(l0, l1), l2)
            e0, e1, e2 = jnp.exp(l0 - mx), jnp.exp(l1 - mx), jnp.exp(l2 - mx)
            den = e0 + e1 + e2
            o = (e0 * og[0][rr, :] + e1 * og[1][rr, :] + e2 * og[2][rr, :]) / den
            o_ref[rr, :] = o
            ob_ref[rr, :] = o.astype(bf16)
            lse_ref[rr, :] = mx + jnp.log(den)
            return carry

        lax.fori_loop(0, S // 256, combine, 0, unroll=True)

    blk = pl.BlockSpec((S, 128), lambda b, hp: (b, hp))
    return pl.pallas_call(
        body, name="attn_fwd", grid=(nb, N_HEADS // 2),
        in_specs=[blk, blk, pl.BlockSpec((None, S, 128), lambda b, hp: (Z_V, b, hp)),
                  pl.BlockSpec((3, 2, 2, AB, 2 * AB), lambda b, hp: (0, hp, 0, 0, 0))],
        out_specs=[blk, blk, blk],
        out_shape=[jax.ShapeDtypeStruct((T, D), f32), jax.ShapeDtypeStruct((T, D), bf16),
                   jax.ShapeDtypeStruct((T, D), f32)],
        scratch_shapes=[pltpu.VMEM((S, 128), bf16), pltpu.VMEM((S + AB, 128), bf16), pltpu.VMEM((S + AB, 128), bf16),
                        pltpu.VMEM((nblk, 2 * AB, 2 * AB), f32), pltpu.VMEM((nblk, 2 * AB, 2 * AB), bf16),
                        pltpu.VMEM((S, 128), f32), pltpu.VMEM((S, 128), f32)] + [pltpu.VMEM((S, 128), f32)] * 6,
        compiler_params=_cparams(("parallel", "parallel")))(qn, kn, z8, bias)


def _attn_bwd(qn, kn, z8, do, o, lse, bias, bd, qg, kg, dz8, S):
    T = qn.shape[0]
    nb = T // S

    nblk = S // AB

    def body(q_ref, k_ref, v_ref, do_ref, o_ref, lse_ref, bias_ref, bd_ref, qraw_ref, kraw_ref, qg_ref, kg_ref,
             dz_in, dz_ref, dqg_ref, dkg_ref,
             dq_ref, dk_ref, dv_ref, delta, qs, ks, vs, dos, lsp, dlp, s2, dp2, p2, ds2, dqp, dkp, dvp):
        del dz_in
        masks = _head_masks()
        bdv = bd_ref[...]
        dq_ref[...] = jnp.zeros_like(dq_ref)
        dk_ref[...] = jnp.zeros_like(dk_ref)
        dv_ref[...] = jnp.zeros_like(dv_ref)
        ks[0:AB, :] = jnp.zeros((AB, 128), bf16)
        vs[0:AB, :] = jnp.zeros((AB, 128), bf16)

        def prep(i, carry):
            rr = pl.ds(pl.multiple_of(i * 256, 256), 256)
            delta[rr, :] = _head_sum(do_ref[rr, :] * o_ref[rr, :], bdv)
            return carry

        lax.fori_loop(0, S // 256, prep, 0, unroll=True)

        for g, (_, d) in enumerate(GROUPS):
            nsub = S // (d * AB)
            chunks = _perm_chunks(S, d)
            for src, dst, ch in chunks:
                qs[dst:dst + ch, :] = q_ref[src, :].astype(bf16)
                ks[AB + dst:AB + dst + ch, :] = k_ref[src, :].astype(bf16)
                vs[AB + dst:AB + dst + ch, :] = v_ref[src, :].astype(bf16)
                dos[dst:dst + ch, :] = do_ref[src, :].astype(bf16)
                lsp[dst:dst + ch, :] = lse_ref[src, :]
                dlp[dst:dst + ch, :] = delta[src, :]
            dkp[...] = jnp.zeros_like(dkp)
            dvp[...] = jnp.zeros_like(dvp)

            def scores(j):
                r0 = _block_row(j)
                q2 = _stack_heads(qs[pl.ds(r0, AB), :], masks)
                do2 = _stack_heads(dos[pl.ds(r0, AB), :], masks)
                s2[j] = lax.dot_general(q2, ks[pl.ds(r0, 2 * AB), :], _NT, preferred_element_type=f32)
                dp2[j] = lax.dot_general(do2, vs[pl.ds(r0, 2 * AB), :], _NT, preferred_element_type=f32)

            def probs(j, g=g, nsub=nsub):
                r0 = _block_row(j)
                kind = int(j % nsub == 0) if isinstance(j, int) else (j % nsub == 0).astype(jnp.int32)
                for cc in range(AB // SCH):
                    lse_c = lsp[pl.ds(r0 + cc * SCH, SCH), :]
                    del_c = dlp[pl.ds(r0 + cc * SCH, SCH), :]
                    for hh in range(2):
                        c0 = hh * HEAD_DIM
                        rows = pl.ds(hh * AB + cc * SCH, SCH)
                        sb = s2[j, rows, :] - bias_ref[g, hh, kind, cc * SCH:(cc + 1) * SCH, :]
                        p = jnp.exp(sb - lse_c[:, c0:c0 + 1])
                        p2[j, rows, :] = p.astype(bf16)
                        ds2[j, rows, :] = (p * (dp2[j, rows, :] - del_c[:, c0:c0 + 1])).astype(bf16)

            def grads(j):
                r0 = _block_row(j)
                q2 = _stack_heads(qs[pl.ds(r0, AB), :], masks)
                do2 = _stack_heads(dos[pl.ds(r0, AB), :], masks)
                dsb = ds2[j]
                t = jnp.dot(dsb, ks[pl.ds(r0, 2 * AB), :], preferred_element_type=f32)
                dqp[pl.ds(r0, AB), :] = jnp.where(masks[0], t[:AB], t[AB:])
                dkp[pl.ds(r0, 2 * AB), :] += lax.dot_general(dsb, q2, _TN, preferred_element_type=f32)
                dvp[pl.ds(r0, 2 * AB), :] += lax.dot_general(p2[j], do2, _TN, preferred_element_type=f32)

            _three_stages(nblk, scores, probs, grads, nblk - 2)

            for src, dst, ch in chunks:
                dq_ref[src, :] += dqp[dst:dst + ch, :]
                dk_ref[src, :] += dkp[AB + dst:AB + dst + ch, :]
                dv_ref[src, :] += dvp[AB + dst:AB + dst + ch, :]

        @pl.when(pl.program_id(1) == 0)
        def _():
            dqg_ref[...] = jnp.zeros_like(dqg_ref)
            dkg_ref[...] = jnp.zeros_like(dkg_ref)

        def norms(i, carry):
            rr = pl.ds(pl.multiple_of(i * 256, 256), 256)

            def one(raw, dn_scaled, g, dg_ref, sec):
                rstd = lax.rsqrt(_head_sum(raw * raw, bdv) * (1.0 / HEAD_DIM) + EPS)
                n = raw * rstd
                dg_ref[...] += _colsum8(dn_scaled * n)
                dn = dn_scaled * g
                draw = rstd * (dn - n * (_head_sum(dn * n, bdv) * (1.0 / HEAD_DIM)))
                dz_ref[sec, rr, :] = draw.astype(bf16)

            one(qraw_ref[rr, :], dq_ref[rr, :] * (HEAD_DIM ** -0.5), qg_ref[...], dqg_ref, 0)
            one(kraw_ref[rr, :], dk_ref[rr, :], kg_ref[...], dkg_ref, 1)
            dz_ref[2, rr, :] = dv_ref[rr, :].astype(bf16)
            dz_ref[3, rr, :] = jnp.zeros((256, 128), bf16)
            return carry

        lax.fori_loop(0, S // 256, norms, 0, unroll=True)

    blk = pl.BlockSpec((S, 128), lambda hp, b: (b, hp))
    sec = lambda s: pl.BlockSpec((None, S, 128), lambda hp, b: (s, b, hp))
    gain = pl.BlockSpec((1, 128), lambda hp, b: (0, hp))
    row = lambda dt, pad=0: pltpu.VMEM((S + pad, 128), dt)
    blocks = lambda dt: pltpu.VMEM((nblk, 2 * AB, 2 * AB), dt)
    return pl.pallas_call(
        body, name="attn_bwd", grid=(N_HEADS // 2, nb),
        in_specs=[blk, blk, sec(Z_V), blk, blk, blk,
                  pl.BlockSpec((3, 2, 2, AB, 2 * AB), lambda hp, b: (0, hp, 0, 0, 0)),
                  pl.BlockSpec((128, 128), lambda hp, b: (0, 0)), sec(Z_Q), sec(Z_K), gain, gain,
                  pl.BlockSpec(memory_space=pl.ANY)],
        out_specs=[pl.BlockSpec((4, S, 128), lambda hp, b: (1, b, hp)),
                   pl.BlockSpec((8, 128), lambda hp, b: (0, hp)), pl.BlockSpec((8, 128), lambda hp, b: (0, hp))],
        out_shape=[jax.ShapeDtypeStruct(dz8.shape, bf16), jax.ShapeDtypeStruct((8, D), f32),
                   jax.ShapeDtypeStruct((8, D), f32)],
        input_output_aliases={12: 0},
        scratch_shapes=[row(f32), row(f32), row(f32),
                        row(f32), row(bf16), row(bf16, AB), row(bf16, AB), row(bf16), row(f32), row(f32),
                        blocks(f32), blocks(f32), blocks(bf16), blocks(bf16), row(f32), row(f32, AB), row(f32, AB)],
        compiler_params=_cparams(("parallel", "arbitrary")))(qn, kn, z8, do, o, lse, bias, bd, z8, z8, qg, kg, dz8)


def _any_spec():
    return pl.BlockSpec(memory_space=pl.ANY)


def _allgather_rows(shards, n_full):
    n = len(shards)

    def body(*refs):
        ins, outs = refs[:n], refs[n:2 * n]
        send_sems, recv_sems, local_sems = refs[2 * n:]
        x, y, c, me = _my_pos()
        sibling = (x, y, 1 - c)
        chips = [(1 - x, y), (x, 1 - y), (1 - x, 1 - y)]

        def idx(px, py, pc):
            return 4 * px + 2 * py + pc

        def copy(a, k, blk, to, src=None):
            return pltpu.make_async_remote_copy(
                src_ref=outs[a].at[blk] if src is None else src, dst_ref=outs[a].at[blk],
                send_sem=send_sems.at[a, k], recv_sem=recv_sems.at[a, k], device_id=to, device_id_type=MESH)

        mine = [pltpu.make_async_copy(ins[a], outs[a].at[me], local_sems.at[a]) for a in range(n)]
        for cp in mine:
            cp.start()
        first = []
        for a in range(n_full):
            first.append(copy(a, 0, me, sibling, src=ins[a]))
            first += [copy(a, 1 + j, me, (*chip, c), src=ins[a]) for j, chip in enumerate(chips)]
        for cp in first:
            cp.start()
        passed = []
        for a in range(n_full):
            for j, chip in enumerate(chips):
                blk = idx(*chip, c)
                copy(a, 1 + j, blk, (x, y, c)).wait_recv()
                cp = copy(a, 4 + j, blk, sibling)
                cp.start()
                passed.append(cp)
        for a in range(n_full):
            copy(a, 0, idx(x, y, 1 - c), (x, y, c)).wait_recv()
            for j, chip in enumerate(chips):
                copy(a, 4 + j, idx(*chip, 1 - c), (x, y, c)).wait_recv()
        for cp in first + passed:
            cp.wait_send()
        for cp in mine:
            cp.wait()

    return pl.pallas_call(
        body, name="allgather_weights",
        in_specs=[_any_spec()] * n, out_specs=[_any_spec()] * n,
        out_shape=[jax.ShapeDtypeStruct((N_DEV,) + s.shape, s.dtype) for s in shards],
        scratch_shapes=[pltpu.SemaphoreType.DMA((n_full, 7)), pltpu.SemaphoreType.DMA((n_full, 7)),
                        pltpu.SemaphoreType.DMA((n,))],
    )(*shards)


def _peer(x, y, c, k):
    tx = 1 - x if (k >> 2) & 1 else x
    ty = 1 - y if (k >> 1) & 1 else y
    tc = 1 - c if k & 1 else c
    return (tx, ty, tc), 4 * tx + 2 * ty + tc


_PEER_ORDER = (2, 4, 6, 3, 5, 7, 1)


_HBM = pl.BlockSpec(memory_space=pltpu.HBM)
_SEM = pl.BlockSpec(memory_space=pltpu.SEMAPHORE)
_EFFECT = pltpu.SideEffectType.DATAFLOW_SIDE_EFFECTING


def _exchange_copies(srcs, lands, send_sems, recv_sems, gather):
    x, y, c, me = _my_pos()
    copies = []
    for k in _PEER_ORDER:
        tgt, tidx = _peer(x, y, c, k)
        for a in range(len(srcs)):
            copies.append(pltpu.make_async_remote_copy(
                src_ref=srcs[a] if gather else srcs[a].at[tidx], dst_ref=lands[a].at[me],
                send_sem=send_sems.at[7 * a + k - 1], recv_sem=recv_sems.at[7 * a + k - 1],
                device_id=tgt, device_id_type=MESH))
    return copies


def _exchange_start(name, srcs, lands=None, after=None):
    n = len(srcs)
    gather = lands is not None
    if lands is None:
        lands = [lax.empty(g.shape, g.dtype) for g in srcs]
    extra = [] if after is None else [after]

    def body(*refs):
        src_refs, land_refs = refs[:n], refs[n:2 * n]
        send_sems, recv_sems = refs[2 * n + len(extra)], refs[2 * n + len(extra) + 1]
        token = refs[-1]
        for cp in _exchange_copies(src_refs, land_refs, send_sems, recv_sems, gather):
            cp.start()
        token[...] = jnp.zeros_like(token)

    hbm = lambda a: pltpu.with_memory_space_constraint(a, pltpu.HBM)
    outs = pl.pallas_call(
        body, name=name,
        out_shape=(pltpu.SemaphoreType.DMA((7 * n,)), pltpu.SemaphoreType.DMA((7 * n,)),
                   *[pltpu.HBM(g.shape, g.dtype) for g in list(srcs) + list(lands)],
                   jax.ShapeDtypeStruct((8, 128), f32)),
        in_specs=[_HBM] * (2 * n) + [pl.BlockSpec(memory_space=pl.ANY)] * len(extra),
        out_specs=(_SEM, _SEM, *([_HBM] * (2 * n)), pl.BlockSpec(memory_space=pltpu.VMEM)),
        input_output_aliases={i: 2 + i for i in range(2 * n)},
        compiler_params=pltpu.CompilerParams(has_side_effects=_EFFECT),
    )(*[hbm(g) for g in srcs], *[hbm(g) for g in lands], *extra)
    return outs[0], outs[1], list(outs[2:2 + n]), list(outs[2 + n:2 + 2 * n]), outs[-1], gather


def _exchange_wait(name, started, after):
    send_sems, recv_sems, srcs, lands, _, gather = started
    n = len(srcs)
    after = list(after) if isinstance(after, (list, tuple)) else [after]

    def body(*refs):
        src_refs, land_refs = refs[:n], refs[n:2 * n]
        s_sems, r_sems = refs[2 * n], refs[2 * n + 1]
        for cp in _exchange_copies(src_refs, land_refs, s_sems, r_sems, gather):
            cp.wait_send()
            cp.wait_recv()

    outs = pl.pallas_call(
        body, name=name,
        out_shape=tuple(pltpu.HBM(a.shape, a.dtype) for a in list(srcs) + list(lands)),
        in_specs=[_HBM] * (2 * n) + [_SEM, _SEM] + [pl.BlockSpec(memory_space=pl.ANY)] * len(after),
        out_specs=tuple([_HBM] * (2 * n)),
        input_output_aliases={i: i for i in range(2 * n)},
        compiler_params=pltpu.CompilerParams(has_side_effects=_EFFECT),
    )(*srcs, *lands, send_sems, recv_sems, *after)
    return list(outs[:n]), list(outs[n:])


SMALL_ROWS = 128


def _small_start(name, sg, after=None):
    return _exchange_start(name, [sg], [lax.empty((N_DEV,) + sg.shape, f32)], after=after)


def _small_sum(name, me, started, after):
    (own,), (slots,) = _exchange_wait(name + "_wait", started, after)

    def body(me_ref, s_ref, own_ref, out_ref):
        acc = None
        for p in range(N_DEV):
            term = lax.cond(me_ref[0] == p, lambda: own_ref[...], lambda p=p: s_ref[p])
            acc = term if acc is None else acc + term
        out_ref[...] = acc

    return pl.pallas_call(
        body, name=name + "_sum",
        in_specs=[pl.BlockSpec(memory_space=pltpu.SMEM), pl.BlockSpec(memory_space=pltpu.VMEM),
                  pl.BlockSpec(memory_space=pltpu.VMEM)],
        out_specs=pl.BlockSpec(memory_space=pltpu.VMEM),
        out_shape=jax.ShapeDtypeStruct(own.shape, f32))(me, slots, own)


def _adam_math(g, w, m, v):
    m = ADAM_B1 * m + (1.0 - ADAM_B1) * g
    v = ADAM_B2 * v + (1.0 - ADAM_B2) * (g * g)
    m_hat = m / (1.0 - ADAM_B1 ** ADAM_STEP)
    v_hat = v / (1.0 - ADAM_B2 ** ADAM_STEP)
    delta = -ADAM_LR * (m_hat / (jnp.sqrt(v_hat) + ADAM_EPS) + ADAM_WD * w)
    return delta, m, v


def _adam_slots(name, me, slots, own, w, m, v, tr, transposed=False):
    rows = slots.shape[1]

    def body(me_ref, s_ref, own_ref, w_ref, m_ref, v_ref, g_ref, d_ref, nm_ref, nv_ref):
        mine = own_ref[...]
        g = None
        for p in range(N_DEV):
            term = lax.cond(me_ref[0] == p, lambda: mine, lambda p=p: s_ref[p]).astype(f32)
            g = term if g is None else g + term
        if transposed:
            g = g.T
        delta, nm, nv = _adam_math(g, w_ref[...], m_ref[...], v_ref[...])
        g_ref[...] = g
        d_ref[...] = delta
        nm_ref[...] = nm
        nv_ref[...] = nv

    mode = dict(pipeline_mode=pl.Buffered(1)) if rows == tr else {}
    if transposed:
        rs = pl.BlockSpec((D, tr), lambda i, me_ref: (0, i))
        rs_in = pl.BlockSpec((D, tr), lambda i, me_ref: (0, i), **mode)
    else:
        rs = pl.BlockSpec((tr, D), lambda i, me_ref: (i, 0))
        rs_in = pl.BlockSpec((tr, D), lambda i, me_ref: (i, 0), **mode)
    return pl.pallas_call(
        body, name=name,
        grid_spec=pltpu.PrefetchScalarGridSpec(
            num_scalar_prefetch=1, grid=(rows // tr,),
            in_specs=[pl.BlockSpec((N_DEV, tr, D), lambda i, me_ref: (0, i, 0), **mode),
                      pl.BlockSpec((None, tr, D), lambda i, me_ref: (me_ref[0], i, 0), **mode), rs_in, rs_in, rs_in],
            out_specs=[rs] * 4),
        out_shape=[jax.ShapeDtypeStruct(w.shape, f32)] * 4,
        compiler_params=_cparams(("parallel",)))(me, slots, own, w, m, v)


def _adam_small(g, w, m, v):
    def body(g_ref, w_ref, m_ref, v_ref, d_ref, nm_ref, nv_ref):
        delta, nm, nv = _adam_math(g_ref[...], w_ref[...], m_ref[...], v_ref[...])
        d_ref[...] = delta
        nm_ref[...] = nm
        nv_ref[...] = nv

    return pl.pallas_call(body, name="adam_small", out_shape=[jax.ShapeDtypeStruct(g.shape, f32)] * 3)(g, w, m, v)


FFN_PAD = 6 * D


_SMALL_PARTS = (("norm1_g", 1), ("gate_b", 2), ("conv_w", CONV_WIDTH), ("conv_b", 1), ("conv_norm_g", 1),
                ("q_norm_g", 1), ("k_norm_g", 1), ("norm2_g", 1), ("ffn_conv_w", 18), ("ffn_conv_b", 6), ("last", 1))


def _small_offsets():
    out, row = {}, 0
    for name, rows in _SMALL_PARTS:
        out[name] = row
        row += -(-rows // 8) * 8
    assert row == SMALL_ROWS
    return out


def _pack_small(norm1_g, gate_b, conv_w, conv_b, conv_norm_g, q_norm_g, k_norm_g, norm2_g, ffn_conv_w, ffn_conv_b,
                last_row=None):
    pad_h = lambda a: jnp.pad(a, ((0, 0), (0, D - HEAD_DIM)))
    pad_f = lambda a: jnp.pad(a, ((0, 0), (0, FFN_PAD - 2 * D_FF))).reshape(-1, D)
    parts = [norm1_g, gate_b.reshape(2, D), conv_w, conv_b, conv_norm_g, pad_h(q_norm_g), pad_h(k_norm_g), norm2_g,
             pad_f(ffn_conv_w), pad_f(ffn_conv_b), jnp.zeros((1, D), f32) if last_row is None else last_row]
    return jnp.concatenate([jnp.pad(p, ((0, -p.shape[0] % 8), (0, 0))) for p in parts], axis=0)


def _unpack_small(p):
    o = _small_offsets()
    rows = lambda name, n: p[o[name]:o[name] + n]
    ffn = lambda a: a.reshape(-1, FFN_PAD)[:, :2 * D_FF]
    return dict(
        norm1_g=rows("norm1_g", 1), gate_b=rows("gate_b", 2).reshape(1, 2 * D), conv_w=rows("conv_w", CONV_WIDTH),
        conv_b=rows("conv_b", 1), conv_norm_g=rows("conv_norm_g", 1), q_norm_g=rows("q_norm_g", 1)[:, :HEAD_DIM],
        k_norm_g=rows("k_norm_g", 1)[:, :HEAD_DIM], norm2_g=rows("norm2_g", 1),
        ffn_conv_w=ffn(rows("ffn_conv_w", 18)), ffn_conv_b=ffn(rows("ffn_conv_b", 6)))


_ADAM_TILE = {896: 128, 704: 704, 128: 128, 352: 176}


def kernel(x, norm1_g, w_in, gate_b, conv_w, conv_b, conv_norm_g, w_conv_out, q_norm_g, k_norm_g, w_attn_out, w_out, norm2_g, w_up, ffn_conv_w, ffn_conv_b, w_down, loss_target, m_norm1_g, m_w_in, m_gate_b, m_conv_w, m_conv_b, m_conv_norm_g, m_w_conv_out, m_q_norm_g, m_k_norm_g, m_w_attn_out, m_w_out, m_norm2_g, m_w_up, m_ffn_conv_w, m_ffn_conv_b, m_w_down, v_norm1_g, v_w_in, v_gate_b, v_conv_w, v_conv_b, v_conv_norm_g, v_w_conv_out, v_q_norm_g, v_k_norm_g, v_w_attn_out, v_w_out, v_norm2_g, v_w_up, v_ffn_conv_w, v_ffn_conv_b, v_w_down):
    BL, S, _ = x.shape
    T = BL * S
    me = 4 * lax.axis_index("x") + 2 * lax.axis_index("y") + lax.axis_index("c")
    xt = x.reshape(T, D)
    target = loss_target.reshape(T, D)

    big = dict(w_in=(w_in[0], m_w_in[0], v_w_in[0]), w_up=(w_up[0], m_w_up[0], v_w_up[0]),
               w_conv_out=(w_conv_out[0], m_w_conv_out[0], v_w_conv_out[0]),
               w_attn_out=(w_attn_out[0], m_w_attn_out[0], v_w_attn_out[0]),
               w_out=(w_out[0], m_w_out[0], v_w_out[0]), w_down=(w_down[0], m_w_down[0], v_w_down[0]))
    order = ["w_in", "w_conv_out", "w_attn_out", "w_out", "w_up", "w_down"]
    shards = [(big[n][0].T if n in ("w_in", "w_up") else big[n][0]).astype(bf16) for n in order]
    gathered = _allgather_rows(shards, 1)
    W = {"w_in": gathered[0].reshape(-1, D)}

    def place_cols(shard, full_cols):
        z = jnp.zeros((shard.shape[0], full_cols), f32)
        return lax.dynamic_update_slice(z, shard, (0, me * shard.shape[1]))

    zr = lambda a: jnp.zeros_like(a)
    conv_local = _pack_small(
        zr(norm1_g), zr(gate_b), place_cols(conv_w[0], D), zr(conv_b), zr(conv_norm_g), zr(q_norm_g), zr(k_norm_g),
        zr(norm2_g), place_cols(ffn_conv_w[0], 2 * D_FF), zr(ffn_conv_b))
    ga_conv = _small_start("gather_conv_start", conv_local, after=gathered[0])
    ga_proj = _exchange_start("gather_start_proj", shards[1:4], gathered[1:4], after=ga_conv[4])
    ga_ffn = _exchange_start("gather_start_ffn", shards[4:6], gathered[4:6], after=ga_proj[4])

    bd = (jnp.arange(128)[:, None] // HEAD_DIM == jnp.arange(128)[None, :] // HEAD_DIM).astype(bf16)
    bias = _attn_bias()
    qg = jnp.tile(q_norm_g, (1, N_HEADS))
    kg = jnp.tile(k_norm_g, (1, N_HEADS))

    z8, h, qn, kn = _in_proj_fwd(xt, norm1_g, W["w_in"], qg, kg, bd, ga_ffn[4])
    conv_all = _unpack_small(_small_sum("gather_conv", me.reshape(1), ga_conv, z8))
    conv_w_full, ffn_w_full = conv_all["conv_w"], conv_all["ffn_conv_w"]
    c = _conv_fwd(z8, conv_w_full, conv_b, S)
    o, ob, lse = _attn_fwd(qn, kn, z8, bias, S)
    for n, g in zip(order[1:4], _exchange_wait("gather_wait_proj", ga_proj, ob)[1]):
        W[n] = g.reshape(-1, D)
    s, ya, yb, mixed = _branches_fwd(c, ob, z8, conv_norm_g, gate_b, W["w_conv_out"], W["w_attn_out"])
    x1, h2 = _out_norm2_fwd(mixed, W["w_out"], xt, norm2_g)
    for n, g in zip(order[4:6], _exchange_wait("gather_wait_ffn", ga_ffn, x1)[1]):
        W[n] = g.reshape(-1, D)
    TNU = D_FF // 2
    u3 = _matmul_call(
        "mm_u", h2, W["w_up"],
        pl.BlockSpec((1024, D), lambda i, j, k: (i, 0)),
        pl.BlockSpec((TNU, D), lambda i, j, k: (j, 0)),
        pl.BlockSpec((None, 1024, TNU), lambda i, j, k: (j // 2, i, j % 2)),
        jax.ShapeDtypeStruct((2, T, D_FF), f32), (T // 1024, 4, 1), "nt", 1, 1024, TNU)
    f = _ffn_fwd(u3, ffn_w_full, ffn_conv_b, S)
    dy, dyb, lacc = _down_loss_fwd(f, W["w_down"], x1, target)
    loss_local = 0.5 / D * jnp.sum(lacc)

    df = _matmul("mm_df", dyb, W["w_down"], "nt", f32, tn=TNU)
    g_w_down = _matmul("mm_dwdn", f, dyb, "tn", bf16, tm=TNU)
    du3, dffn = _ffn_bwd(u3, df, ffn_w_full, ffn_conv_b, S)
    g_w_up = _matmul_call(
        "mm_dwup", du3, h2,
        pl.BlockSpec((None, T, TNU), lambda i, j, k: (i // 2, 0, i % 2)),
        pl.BlockSpec((T, D), lambda i, j, k: (0, 0)),
        pl.BlockSpec((TNU, D), lambda i, j, k: (i, 0)),
        jax.ShapeDtypeStruct((2 * D_FF, D), bf16), (4, 1, 1), "tn", 1, TNU, D)
    blocks8 = lambda a: a.reshape(N_DEV, -1, D)
    ex_ffn = _exchange_start("scatter_start_ffn", [blocks8(g_w_up), blocks8(g_w_down)])
    dx1, dx1b, dg_norm2 = _up_norm2_bwd(du3, W["w_up"], x1, dy, norm2_g, ex_ffn[4])
    g_w_out = _matmul("mm_dwo", mixed, dx1b, "tn", bf16, tm=512)
    dz8 = lax.empty((8, T, D), bf16)
    dya, dyb2, dz8, dg_gate = _out_gate_bwd(dx1b, W["w_out"], z8, gate_b, ya, yb, dz8)
    g_w_conv_out = _matmul("mm_dwco", s, dya, "tn", bf16, tm=512)
    g_w_attn_out = _matmul("mm_dwao", ob, dyb2, "tn", bf16, tm=512)
    ex_proj = _exchange_start("scatter_start_proj", [blocks8(g_w_conv_out), blocks8(g_w_attn_out), blocks8(g_w_out)])
    do = _matmul("mm_do", dyb2, W["w_attn_out"], "nt", f32, after=ex_proj[4])
    dc, dg_convnorm = _convnorm_bwd(dya, W["w_conv_out"], c, conv_norm_g)
    dz8a, dconv = _conv_bwd(dc, z8, conv_w_full, dz8, S)
    dz8b, dg_q, dg_k = _attn_bwd(qn, kn, z8, do, o, lse, bias, bd, qg, kg, dz8a, S)
    g_w_in = _matmul_call(
        "mm_dwin", dz8b, h,
        pl.BlockSpec((None, T, D), lambda i, j, k: (jnp.where(i < 2, i, jnp.where(i < 5, i + 2, i - 3)), 0, 0)),
        pl.BlockSpec((T, D), lambda i, j, k: (0, 0)), pl.BlockSpec((1024, D), lambda i, j, k: (i, 0)),
        jax.ShapeDtypeStruct((7 * D, D), bf16), (7, 1, 1), "tn", 1, D, D)
    ex_in = _exchange_start("scatter_start_in", [blocks8(g_w_in)])
    grad_x, dg_norm1 = _in_norm1_bwd(dz8b, W["w_in"], xt, dx1, norm1_g, ex_in[4])

    sum8 = lambda a: a.reshape(-1, 8, a.shape[-1]).sum(axis=1)
    dconv_s = sum8(dconv.sum(axis=0))
    dffn_s = dffn.sum(axis=0).reshape(2, 4, 8, D_FF).sum(axis=2)
    dffn_w = jnp.concatenate([dffn_s[0, :3], dffn_s[1, :3]], axis=1)
    dffn_b = jnp.concatenate([dffn_s[0, 3:4], dffn_s[1, 3:4]], axis=1)
    fold = lambda a: sum8(a).reshape(N_HEADS, HEAD_DIM).sum(axis=0)[None]
    small_g_local = _pack_small(
        sum8(dg_norm1), sum8(dg_gate), dconv_s[:CONV_WIDTH], dconv_s[CONV_WIDTH:], sum8(dg_convnorm),
        fold(dg_q), fold(dg_k), sum8(dg_norm2), dffn_w, dffn_b,
        last_row=jnp.pad(loss_local.reshape(1, 1), ((0, 0), (0, D - 1))))
    sg_start = _small_start("small_grads_start", small_g_local)

    place_m = lambda a, full: place_cols(a[0], full)
    small_w_true = _pack_small(norm1_g, gate_b, conv_w_full, conv_b, conv_norm_g, q_norm_g, k_norm_g, norm2_g,
                               ffn_w_full, ffn_conv_b)
    small_m = _pack_small(m_norm1_g, m_gate_b, place_m(m_conv_w, D), m_conv_b, m_conv_norm_g, m_q_norm_g, m_k_norm_g,
                          m_norm2_g, place_m(m_ffn_conv_w, 2 * D_FF), m_ffn_conv_b)
    small_v = _pack_small(v_norm1_g, v_gate_b, place_m(v_conv_w, D), v_conv_b, v_conv_norm_g, v_q_norm_g, v_k_norm_g,
                          v_norm2_g, place_m(v_ffn_conv_w, 2 * D_FF), v_ffn_conv_b)

    own, slots = {}, {}
    for tag, ex, names_ in (("ffn", ex_ffn, ("w_up", "w_down")),
                            ("proj", ex_proj, ("w_conv_out", "w_attn_out", "w_out")), ("in", ex_in, ("w_in",))):
        sent, landed = _exchange_wait("scatter_wait_" + tag, ex, [sg_start[4], small_w_true, small_m, small_v])
        for n, src, land in zip(names_, sent, landed):
            own[n], slots[n] = src, land

    res, adam_done = {}, []
    for n in order:
        w, m, v = big[n]
        outs = _adam_slots("adam_" + n, me.reshape(1), slots[n], own[n], w, m, v, _ADAM_TILE[slots[n].shape[1]],
                           transposed=n in ("w_in", "w_up"))
        adam_done.append(outs[0])
        res[n] = [a[None] for a in outs]
    small_g = _small_sum("small_grads", me.reshape(1), sg_start, adam_done)
    loss = small_g[_small_offsets()["last"], 0]

    col = lambda a, width: lax.dynamic_slice(a, (0, me * width), (a.shape[0], width))
    sd, sm, sv = _adam_small(small_g, small_w_true, small_m, small_v)
    for i, packed in enumerate((small_g, sd, sm, sv)):
        u = _unpack_small(packed)
        u["conv_w"] = col(u["conv_w"], D // N_DEV)
        u["ffn_conv_w"] = col(u["ffn_conv_w"], 2 * D_FF // N_DEV)
        for n, a in u.items():
            res.setdefault(n, [None] * 4)[i] = a[None] if n in ("conv_w", "ffn_conv_w") else a

    names = ["norm1_g", "w_in", "gate_b", "conv_w", "conv_b", "conv_norm_g", "w_conv_out", "q_norm_g", "k_norm_g",
             "w_attn_out", "w_out", "norm2_g", "w_up", "ffn_conv_w", "ffn_conv_b", "w_down"]
    out = [loss, grad_x.reshape(BL, S, D)]
    for i in range(4):
        out += [res[n][i] for n in names]
    return tuple(out)
```

```python
import functools

import jax
import jax.numpy as jnp
import numpy as np
from jax import lax
from jax.experimental import pallas as pl
from jax.experimental.pallas import tpu as pltpu

f32 = jnp.float32
bf16 = jnp.bfloat16

D = 1024
N_HEADS = 16
HEAD_DIM = 64
CONV_WIDTH = 31
D_FF = 2816
GROUPS = ((128, 1), (512, 4), (2048, 16))
ATTN_BLOCK = 128
EPS = 1e-6
N_DEV = 8
MESH = pl.DeviceIdType.MESH

ADAM_LR = 0.001
ADAM_B1 = 0.9
ADAM_B2 = 0.999
ADAM_EPS = 1e-08
ADAM_WD = 0.01
ADAM_STEP = 10

VMEM_LIMIT = 56 * 1024 * 1024
MASK_BIAS = 1e30

Z_AVAL, Z_AGATE, Z_GA, Z_GB, Z_Q, Z_K, Z_V = 0, 1, 2, 3, 4, 5, 6


_W_OF_Z = (0, 1, 5, 6, 2, 3, 4)


def _wsec_of_zsec(j):
    return jnp.where(j < 2, j, jnp.where(j < 4, j + 3, j - 2))


def _sig(x):
    return 1.0 / (1.0 + jnp.exp(-x))


def _colsum8(x):
    return x.reshape(-1, 8, x.shape[-1]).sum(axis=0)


def _cparams(sem):
    return pltpu.CompilerParams(dimension_semantics=sem, vmem_limit_bytes=VMEM_LIMIT)


def _my_pos():
    x, y, c = lax.axis_index("x"), lax.axis_index("y"), lax.axis_index("c")
    return x, y, c, 4 * x + 2 * y + c


_DIMS = {"nn": ((1,), (0,)), "nt": ((1,), (1,)), "tn": ((0,), (0,))}


def _matmul_call(name, a, b, a_spec, b_spec, o_spec, out_shape, grid, mode, nk, tm, tn, after=None):
    dims = (_DIMS[mode], ((), ()))
    extra = [] if after is None else [after]

    def body(a_ref, b_ref, *rest):
        o_ref, scratch = rest[len(extra)], rest[len(extra) + 1:]
        part = lax.dot_general(a_ref[...], b_ref[...], dims, preferred_element_type=f32)
        if nk == 1:
            o_ref[...] = part.astype(o_ref.dtype)
        else:
            acc = scratch[0]
            k = pl.program_id(2)

            @pl.when(k == 0)
            def _():
                acc[...] = part

            @pl.when(k > 0)
            def _():
                acc[...] += part

            @pl.when(k == nk - 1)
            def _():
                o_ref[...] = acc[...].astype(o_ref.dtype)

    scratch = [] if nk == 1 else [pltpu.VMEM((tm, tn), f32)]
    return pl.pallas_call(
        body, name=name, grid=grid, in_specs=[a_spec, b_spec] + [pl.BlockSpec(memory_space=pl.ANY)] * len(extra),
        out_specs=o_spec, out_shape=out_shape,
        scratch_shapes=scratch, compiler_params=_cparams(("parallel", "parallel", "arbitrary")),
    )(a, b, *extra)


def _matmul(name, a, b, mode, out_dtype, tm=1024, tn=1024, tk=None, after=None):
    if mode == "nn":
        (M, K), (_, N) = a.shape, b.shape
    elif mode == "nt":
        (M, K), (N, _) = a.shape, b.shape
    else:
        (K, M), (_, N) = a.shape, b.shape
    tm, tn = min(tm, M), min(tn, N)
    tk = K if tk is None else tk
    nk = K // tk
    assert M % tm == 0 and N % tn == 0 and K % tk == 0
    if mode == "tn":
        a_spec = pl.BlockSpec((tk, tm), lambda i, j, k: (k, i))
    else:
        a_spec = pl.BlockSpec((tm, tk), lambda i, j, k: (i, k))
    if mode == "nt":
        b_spec = pl.BlockSpec((tn, tk), lambda i, j, k: (j, k))
    else:
        b_spec = pl.BlockSpec((tk, tn), lambda i, j, k: (k, j))
    o_spec = pl.BlockSpec((tm, tn), lambda i, j, k: (i, j))
    return _matmul_call(name, a, b, a_spec, b_spec, o_spec, jax.ShapeDtypeStruct((M, N), out_dtype),
                        (M // tm, N // tn, nk), mode, nk, tm, tn, after=after)


FTM = 512


def _matmul_fused(name, a, b, pairs, epilogue, extras, consts, outs, nt=False, sums=False, passed=(), aliases=None):
    sa, M, kk = a.shape
    na = max(i for i, _ in pairs) + 1
    ne, nc, npass = len(extras), len(consts), len(passed)
    dims = (_DIMS["nt" if nt else "nn"], ((), ()))

    def body(a_ref, b_ref, *rest):
        acc = None
        for i, j in pairs:
            part = lax.dot_general(a_ref[i], b_ref[j], dims, preferred_element_type=f32)
            acc = part if acc is None else acc + part
        epilogue(acc, rest[:ne], rest[ne:ne + nc], rest[ne + nc + npass:])

    whole = lambda arr: pl.BlockSpec(arr.shape, lambda i, nd=arr.ndim: (0,) * nd, pipeline_mode=pl.Buffered(1))
    io_alias = {2 + ne + nc + k: v for k, v in (aliases or {}).items()}
    return pl.pallas_call(
        body, name=name, grid=(M // FTM,),
        in_specs=[pl.BlockSpec((na, FTM, kk), lambda i: (0, i, 0)), whole(b)] + [s for _, s in extras]
        + [whole(c) for c in consts] + [pl.BlockSpec(memory_space=pl.ANY)] * npass,
        out_specs=[s for _, s in outs], out_shape=[s for s, _ in outs], input_output_aliases=io_alias,
        compiler_params=_cparams(("arbitrary" if sums else "parallel",)),
    )(a, b, *[x for x, _ in extras], *consts, *passed)


def _frows(c=D):
    return pl.BlockSpec((FTM, c), lambda i: (i, 0))


def _fsec(s):
    return pl.BlockSpec((None, FTM, D), lambda i: (s, i, 0))


def _rowshape(T, dtype, c=D):
    return (jax.ShapeDtypeStruct((T, c), dtype), _frows(c))


def _sumshape(c=D):
    return (jax.ShapeDtypeStruct((8, c), f32), pl.BlockSpec((8, c), lambda i: (0, 0)))


def _add_colsum(ref, x, cols=None):
    @pl.when(pl.program_id(0) == 0)
    def _():
        if cols is None:
            ref[...] = jnp.zeros_like(ref)
        else:
            ref[:, cols] = jnp.zeros((8, x.shape[-1]), f32)

    if cols is None:
        ref[...] += _colsum8(x)
    else:
        ref[:, cols] += _colsum8(x)


def _rms(x):
    return lax.rsqrt(jnp.mean(x * x, axis=-1, keepdims=True) + EPS)


def _rms_bwd(dy_g, xn, rstd):
    return rstd * (dy_g - xn * jnp.mean(dy_g * xn, axis=-1, keepdims=True))


def _head_sum(x, bd):
    parts = []
    for cb in range(x.shape[-1] // 128):
        xb = x[:, cb * 128:(cb + 1) * 128]
        hi = xb.astype(bf16)
        lo = (xb - hi.astype(f32)).astype(bf16)
        parts.append(jnp.dot(hi, bd, preferred_element_type=f32) + jnp.dot(lo, bd, preferred_element_type=f32))
    return parts[0] if len(parts) == 1 else jnp.concatenate(parts, axis=1)


ZTM = 1024


def _in_proj_fwd(x, g, w_in_t, qg, kg, bd, after):
    T = x.shape[0]
    nt = T // ZTM

    def body(x_ref, g_ref, w_ref, qg_ref, kg_ref, bd_ref, after_ref, z_ref, h_ref, qn_ref, kn_ref, hbuf):
        del after_ref
        j, i = pl.program_id(0), pl.program_id(1)
        rows = pl.ds(pl.multiple_of(i * ZTM, ZTM), ZTM)

        @pl.when(j == 0)
        def _():
            xv = x_ref[...]
            hv = (xv * _rms(xv) * g_ref[...]).astype(bf16)
            hbuf[rows, :] = hv
            h_ref[...] = hv

        def project():
            z = lax.dot_general(hbuf[rows, :], w_ref[...], (_DIMS["nt"], ((), ())), preferred_element_type=f32)
            z_ref[...] = z
            return z

        def head_norm(z, gain_ref, scale):
            return z * lax.rsqrt(_head_sum(z * z, bd_ref[...]) * (1.0 / HEAD_DIM) + EPS) * gain_ref[...] * scale

        @pl.when(j == Z_Q)
        def _():
            qn_ref[...] = head_norm(project(), qg_ref, HEAD_DIM ** -0.5)

        @pl.when(j == Z_K)
        def _():
            kn_ref[...] = head_norm(project(), kg_ref, 1.0)

        @pl.when((j != Z_Q) & (j != Z_K))
        def _():
            project()

    def tile_at(sec):
        return pl.BlockSpec((ZTM, D), lambda j, i: (jnp.where(j < sec, 0, jnp.where(j == sec, i, nt - 1)), 0))

    row = pl.BlockSpec((1, D), lambda j, i: (0, 0))
    return pl.pallas_call(
        body, name="mm_z", grid=(7, nt),
        in_specs=[tile_at(0), row, pl.BlockSpec((D, D), lambda j, i: (_wsec_of_zsec(j), 0)), row, row,
                  pl.BlockSpec((128, 128), lambda j, i: (0, 0)), pl.BlockSpec(memory_space=pl.ANY)],
        out_specs=[pl.BlockSpec((None, ZTM, D), lambda j, i: (j, i, 0)), tile_at(0), tile_at(Z_Q), tile_at(Z_K)],
        out_shape=[jax.ShapeDtypeStruct((8, T, D), f32), jax.ShapeDtypeStruct((T, D), bf16),
                   jax.ShapeDtypeStruct((T, D), f32), jax.ShapeDtypeStruct((T, D), f32)],
        scratch_shapes=[pltpu.VMEM((T, D), bf16)],
        compiler_params=_cparams(("arbitrary", "arbitrary")))(x, g, w_in_t, qg, kg, bd, after)


def _branches_fwd(c, ob, z8, g, gate_b, w_conv_out, w_attn_out):
    T = c.shape[0]

    def epilogue(yb, extra, const, out):
        cv = extra[0][...]
        r = cv * _rms(cv) * const[0][...]
        s = (r * _sig(r)).astype(bf16)
        ya = jnp.dot(s, const[2][...], preferred_element_type=f32)
        b_ref = const[1]
        g_a = _sig(extra[1][...] + b_ref[:, :D])
        g_b = _sig(extra[2][...] + b_ref[:, D:])
        out[0][...] = s
        out[1][...] = ya
        out[2][...] = yb
        out[3][...] = (g_a * ya + g_b * yb).astype(bf16)

    return _matmul_fused("mm_branches", ob[None], w_attn_out[None], ((0, 0),), epilogue,
                         [(c, _frows()), (z8, _fsec(Z_GA)), (z8, _fsec(Z_GB))], [g, gate_b, w_conv_out],
                         [_rowshape(T, bf16), _rowshape(T, f32), _rowshape(T, f32), _rowshape(T, bf16)])


def _out_norm2_fwd(mixed, w_out, x, g):
    T = x.shape[0]

    def epilogue(acc, extra, const, out):
        x1 = extra[0][...] + acc
        out[0][...] = x1
        out[1][...] = (x1 * _rms(x1) * const[0][...]).astype(bf16)

    return _matmul_fused("mm_t1_norm2", mixed[None], w_out[None], ((0, 0),), epilogue, [(x, _frows())], [g],
                         [_rowshape(T, f32), _rowshape(T, bf16)])


def _down_loss_fwd(f, w_down, x1, target):
    T = x1.shape[0]

    def epilogue(acc, extra, const, out):
        diff = extra[0][...] + acc - extra[1][...]
        dy = diff * (1.0 / D)
        out[0][...] = dy
        out[1][...] = dy.astype(bf16)
        _add_colsum(out[2], diff * diff)

    return _matmul_fused("mm_t2_loss", f[None], w_down[None], ((0, 0),), epilogue, [(x1, _frows()), (target, _frows())],
                         [], [_rowshape(T, f32), _rowshape(T, bf16), _sumshape()], sums=True)


def _up_norm2_bwd(du3, w_up_t, x1, dy, g, token):
    T = x1.shape[0]

    def epilogue(dh, extra, const, out):
        x1v = extra[0][...]
        rstd = _rms(x1v)
        xn = x1v * rstd
        dx1 = extra[1][...] + _rms_bwd(dh * const[0][...], xn, rstd)
        out[0][...] = dx1
        out[1][...] = dx1.astype(bf16)
        _add_colsum(out[2], dh * xn)

    return _matmul_fused("mm_dh2_norm2", du3, w_up_t.reshape(2, D_FF, D), ((0, 0), (1, 1)), epilogue,
                         [(x1, _frows()), (dy, _frows())], [g],
                         [_rowshape(T, f32), _rowshape(T, bf16), _sumshape()], sums=True, passed=[token])


def _out_gate_bwd(dx1b, w_out, z8, gate_b, ya, yb, dz8):
    T = ya.shape[0]

    def epilogue(dm, extra, const, out):
        b_ref = const[0]
        g_a = _sig(extra[0][...] + b_ref[:, :D])
        g_b = _sig(extra[1][...] + b_ref[:, D:])
        out[0][...] = (dm * g_a).astype(bf16)
        out[1][...] = (dm * g_b).astype(bf16)
        dla = dm * extra[2][...] * g_a * (1.0 - g_a)
        dlb = dm * extra[3][...] * g_b * (1.0 - g_b)
        out[2][0] = dla.astype(bf16)
        out[2][1] = dlb.astype(bf16)
        _add_colsum(out[3], dla, slice(0, D))
        _add_colsum(out[3], dlb, slice(D, 2 * D))

    return _matmul_fused(
        "mm_dmixed_gate", dx1b[None], w_out[None], ((0, 0),), epilogue,
        [(z8, _fsec(Z_GA)), (z8, _fsec(Z_GB)), (ya, _frows()), (yb, _frows())], [gate_b],
        [_rowshape(T, bf16), _rowshape(T, bf16),
         (jax.ShapeDtypeStruct(dz8.shape, bf16), pl.BlockSpec((2, FTM, D), lambda i: (1, i, 0))), _sumshape(2 * D)],
        nt=True, sums=True, passed=[dz8], aliases={0: 2})


def _convnorm_bwd(dya, w_conv_out, c, g):
    T = c.shape[0]

    def epilogue(ds, extra, const, out):
        cv = extra[0][...]
        rstd = _rms(cv)
        r0 = cv * rstd
        gv = const[0][...]
        r = r0 * gv
        sg = _sig(r)
        dr = ds * sg * (1.0 + r * (1.0 - sg))
        out[0][...] = _rms_bwd(dr * gv, r0, rstd)
        _add_colsum(out[1], dr * r0)

    return _matmul_fused("mm_ds_convnorm", dya[None], w_conv_out[None], ((0, 0),), epilogue, [(c, _frows())], [g],
                         [_rowshape(T, f32), _sumshape()], nt=True, sums=True)


def _in_norm1_bwd(dz8, w_in_t, x, dx1, g, token):
    T = x.shape[0]

    def epilogue(dh, extra, const, out):
        xv = extra[0][...]
        rstd = _rms(xv)
        xn = xv * rstd
        out[0][...] = extra[1][...] + _rms_bwd(dh * const[0][...], xn, rstd)
        _add_colsum(out[1], dh * xn)

    return _matmul_fused("mm_dh_norm1", dz8, w_in_t.reshape(7, D, D), tuple(zip(range(7), _W_OF_Z)), epilogue,
                         [(x, _frows()), (dx1, _frows())], [g], [_rowshape(T, f32), _sumshape()],
                         sums=True, passed=[token])


CCW = 256
CR = 64
HALO = 32


def _conv_fwd(z8, conv_w, conv_b, S):
    T = z8.shape[1]
    nb = T // S
    ncb = D // CCW

    def body(av_ref, ag_ref, w_ref, b_ref, c_ref, pad):
        pad[0:HALO, :] = jnp.zeros((HALO, CCW), f32)

        def fill(i, carry):
            r0 = pl.multiple_of(i * 256, 256)
            pad[pl.ds(HALO + r0, 256), :] = av_ref[pl.ds(r0, 256), :] * _sig(ag_ref[pl.ds(r0, 256), :])
            return carry

        lax.fori_loop(0, S // 256, fill, 0)
        bias = b_ref[...]

        def chunk(i, carry):
            r0 = pl.multiple_of(i * CR, CR)
            win = pad[pl.ds(r0, CR + HALO), :]
            acc = jnp.zeros((CR, CCW), f32) + bias
            for s in range(8):
                part = None
                for m in range((CONV_WIDTH - 1 - s) // 8 + 1):
                    j = CONV_WIDTH - 1 - 8 * m - s
                    term = win[24 - 8 * m:24 - 8 * m + CR + 8, :] * w_ref[j:j + 1, :]
                    part = term if part is None else part + term
                acc = acc + part[8 - s:8 - s + CR, :]
            c_ref[pl.ds(r0, CR), :] = acc
            return carry

        lax.fori_loop(0, S // CR, chunk, 0)

    zs = lambda s: pl.BlockSpec((None, S, CCW), lambda b, cb: (s, b, cb))
    return pl.pallas_call(
        body, name="conv_fwd", grid=(nb, ncb),
        in_specs=[zs(Z_AVAL), zs(Z_AGATE), pl.BlockSpec((CONV_WIDTH, CCW), lambda b, cb: (0, cb)),
                  pl.BlockSpec((1, CCW), lambda b, cb: (0, cb))],
        out_specs=pl.BlockSpec((S, CCW), lambda b, cb: (b, cb)),
        out_shape=jax.ShapeDtypeStruct((T, D), f32),
        scratch_shapes=[pltpu.VMEM((S + HALO, CCW), f32)],
        compiler_params=_cparams(("parallel", "parallel")))(z8, z8, conv_w, conv_b)


def _conv_bwd(dc, z8, conv_w, dz8, S):
    T = dc.shape[0]
    nb = T // S
    ncb = D // CCW

    def body(dc_ref, av_ref, ag_ref, w_ref, dz_in, dz_ref, dw_ref, apad, dpad, shbuf):
        del dz_in
        apad[0:HALO, :] = jnp.zeros((HALO, CCW), f32)
        dpad[S:S + HALO, :] = jnp.zeros((HALO, CCW), f32)
        dw_ref[...] = jnp.zeros_like(dw_ref)

        def fill(i, carry):
            r0 = pl.multiple_of(i * 256, 256)
            apad[pl.ds(HALO + r0, 256), :] = av_ref[pl.ds(r0, 256), :] * _sig(ag_ref[pl.ds(r0, 256), :])
            dpad[pl.ds(r0, 256), :] = dc_ref[pl.ds(r0, 256), :]
            return carry

        lax.fori_loop(0, S // 256, fill, 0)

        def chunk(i, carry):
            r0 = pl.multiple_of(i * CR, CR)
            dwin = dpad[pl.ds(r0, CR + HALO), :]
            da = jnp.zeros((CR, CCW), f32)
            for s in range(8):
                shbuf[...] = dwin[s:s + CR, :]
                dshift = shbuf[...]
                part = None
                for m in range((CONV_WIDTH - 1 - s) // 8 + 1):
                    j = CONV_WIDTH - 1 - 8 * m - s
                    term = dwin[8 * m:8 * m + CR + 8, :] * w_ref[j:j + 1, :]
                    part = term if part is None else part + term
                    a_lag = apad[pl.ds(r0 + HALO - 8 * m, CR), :]
                    dw_ref[8 * j:8 * j + 8, :] += _colsum8(dshift * a_lag)
                da = da + part[s:s + CR, :]
            dw_ref[8 * CONV_WIDTH:8 * CONV_WIDTH + 8, :] += _colsum8(dwin[0:CR, :])
            av = av_ref[pl.ds(r0, CR), :]
            sg = _sig(ag_ref[pl.ds(r0, CR), :])
            dz_ref[0, pl.ds(r0, CR), :] = (da * sg).astype(bf16)
            dz_ref[1, pl.ds(r0, CR), :] = (da * av * sg * (1.0 - sg)).astype(bf16)
            return carry

        lax.fori_loop(0, S // CR, chunk, 0)

    zs = lambda s: pl.BlockSpec((None, S, CCW), lambda b, cb: (s, b, cb))
    return pl.pallas_call(
        body, name="conv_bwd", grid=(nb, ncb),
        in_specs=[pl.BlockSpec((S, CCW), lambda b, cb: (b, cb)), zs(Z_AVAL), zs(Z_AGATE),
                  pl.BlockSpec((CONV_WIDTH, CCW), lambda b, cb: (0, cb)), pl.BlockSpec(memory_space=pl.ANY)],
        out_specs=[pl.BlockSpec((2, S, CCW), lambda b, cb: (0, b, cb)),
                   pl.BlockSpec((None, 256, CCW), lambda b, cb: (b, 0, cb))],
        out_shape=[jax.ShapeDtypeStruct(dz8.shape, bf16), jax.ShapeDtypeStruct((nb, 256, D), f32)],
        input_output_aliases={4: 0},
        scratch_shapes=[pltpu.VMEM((S + HALO, CCW), f32), pltpu.VMEM((S + HALO, CCW), f32),
                        pltpu.VMEM((CR, CCW), f32)],
        compiler_params=_cparams(("parallel", "parallel")))(dc, z8, z8, conv_w, dz8)


FR = 128
NFB = D_FF // CCW
FBW = 128


def _ffn_window(ref, i, r0):
    return ref[pl.ds(r0 - 8, FR + 8), :]


def _ffn_u(win, w_ref, b_ref):
    return (win[6:6 + FR, :] * w_ref[0:1, :] + win[7:7 + FR, :] * w_ref[1:2, :]
            + win[8:8 + FR, :] * w_ref[2:3, :] + b_ref[...])


def _ffn_fwd(u3, ffn_w, ffn_b, S):
    T = u3.shape[1]
    nb = T // S

    def body(uv_ref, ug_ref, wv_ref, wg_ref, bv_ref, bg_ref, f_ref):
        def chunk(first, i):
            r0 = 0 if first else pl.multiple_of(i * FR, FR)
            if first:
                z = jnp.zeros((8, CCW), f32)
                wv = jnp.concatenate([z, uv_ref[0:FR, :]], axis=0)
                wg = jnp.concatenate([z, ug_ref[0:FR, :]], axis=0)
            else:
                wv = _ffn_window(uv_ref, i, r0)
                wg = _ffn_window(ug_ref, i, r0)
            u_val = _ffn_u(wv, wv_ref, bv_ref)
            u_gate = _ffn_u(wg, wg_ref, bg_ref)
            f_ref[pl.ds(r0, FR), :] = (u_gate * _sig(u_gate) * u_val).astype(bf16)

        chunk(True, 0)

        def loop(i, carry):
            chunk(False, i)
            return carry

        lax.fori_loop(1, S // FR, loop, 0)

    us = lambda h: pl.BlockSpec((None, S, CCW), lambda b, cb: (h, b, cb))
    ws = lambda h: pl.BlockSpec((3, CCW), lambda b, cb: (0, h * NFB + cb))
    bs = lambda h: pl.BlockSpec((1, CCW), lambda b, cb: (0, h * NFB + cb))
    return pl.pallas_call(
        body, name="ffn_fwd", grid=(nb, NFB),
        in_specs=[us(0), us(1), ws(0), ws(1), bs(0), bs(1)],
        out_specs=pl.BlockSpec((S, CCW), lambda b, cb: (b, cb)),
        out_shape=jax.ShapeDtypeStruct((T, D_FF), bf16),
        compiler_params=_cparams(("parallel", "parallel")))(u3, u3, ffn_w, ffn_w, ffn_b, ffn_b)


def _ffn_bwd(u3, df, ffn_w, ffn_b, S):
    T = u3.shape[1]
    nb = T // S

    def body(uv_ref, ug_ref, df_ref, wv_ref, wg_ref, bv_ref, bg_ref, du_ref, dw_ref, dvpad, dgpad, shbuf):
        dvpad[S:S + 8, :] = jnp.zeros((8, FBW), f32)
        dgpad[S:S + 8, :] = jnp.zeros((8, FBW), f32)
        dw_ref[...] = jnp.zeros_like(dw_ref)

        def chunk(first, i):
            r0 = 0 if first else pl.multiple_of(i * FR, FR)
            if first:
                z = jnp.zeros((8, FBW), f32)
                wv = jnp.concatenate([z, uv_ref[0:FR, :]], axis=0)
                wg = jnp.concatenate([z, ug_ref[0:FR, :]], axis=0)
            else:
                wv = _ffn_window(uv_ref, i, r0)
                wg = _ffn_window(ug_ref, i, r0)
            taps = []
            for h, win in enumerate((wv, wg)):
                shbuf[2 * h] = win[6:6 + FR, :]
                shbuf[2 * h + 1] = win[7:7 + FR, :]
                taps.append((shbuf[2 * h], shbuf[2 * h + 1], win[8:8 + FR, :]))
            conv = lambda x, w_ref, b_ref: (x[0] * w_ref[0:1, :] + x[1] * w_ref[1:2, :] + x[2] * w_ref[2:3, :]
                                            + b_ref[...])
            u_val = conv(taps[0], wv_ref, bv_ref)
            u_gate = conv(taps[1], wg_ref, bg_ref)
            dfc = df_ref[pl.ds(r0, FR), :]
            sg = _sig(u_gate)
            d_val = dfc * u_gate * sg
            d_gate = dfc * u_val * sg * (1.0 + u_gate * (1.0 - sg))
            dvpad[pl.ds(r0, FR), :] = d_val
            dgpad[pl.ds(r0, FR), :] = d_gate
            for h, dd in enumerate((d_val, d_gate)):
                for j in range(3):
                    dw_ref[h, 8 * j:8 * j + 8, :] += _colsum8(dd * taps[h][j])
                dw_ref[h, 24:32, :] += _colsum8(dd)

        chunk(True, 0)

        def loop(i, carry):
            chunk(False, i)
            return carry

        lax.fori_loop(1, S // FR, loop, 0)

        def back(i, carry):
            r0 = pl.multiple_of(i * FR, FR)
            for h, (dpad, w_ref) in enumerate(((dvpad, wv_ref), (dgpad, wg_ref))):
                win = dpad[pl.ds(r0, FR + 8), :]
                du = (win[0:FR, :] * w_ref[2:3, :] + win[1:1 + FR, :] * w_ref[1:2, :]
                      + win[2:2 + FR, :] * w_ref[0:1, :])
                du_ref[h, pl.ds(r0, FR), :] = du.astype(bf16)
            return carry

        lax.fori_loop(0, S // FR, back, 0)

    ncb = D_FF // FBW
    us = lambda h: pl.BlockSpec((None, S, FBW), lambda b, cb: (h, b, cb))
    ws = lambda h: pl.BlockSpec((3, FBW), lambda b, cb: (0, h * ncb + cb))
    bs = lambda h: pl.BlockSpec((1, FBW), lambda b, cb: (0, h * ncb + cb))
    return pl.pallas_call(
        body, name="ffn_bwd", grid=(nb, ncb),
        in_specs=[us(0), us(1), pl.BlockSpec((S, FBW), lambda b, cb: (b, cb)), ws(0), ws(1), bs(0), bs(1)],
        out_specs=[pl.BlockSpec((2, S, FBW), lambda b, cb: (0, b, cb)),
                   pl.BlockSpec((None, 2, 32, FBW), lambda b, cb: (b, 0, 0, cb))],
        out_shape=[jax.ShapeDtypeStruct((2, T, D_FF), bf16), jax.ShapeDtypeStruct((nb, 2, 32, D_FF), f32)],
        scratch_shapes=[pltpu.VMEM((S + 8, FBW), f32), pltpu.VMEM((S + 8, FBW), f32),
                        pltpu.VMEM((4, FR, FBW), f32)],
        compiler_params=_cparams(("parallel", "parallel")))(u3, u3, df, ffn_w, ffn_w, ffn_b, ffn_b)


AB = ATTN_BLOCK


def _attn_bias_np():
    slopes = (np.float32(2.0) ** (np.float32(-8.0) * np.arange(1, N_HEADS + 1, dtype=np.float32)
                                  / np.float32(N_HEADS))).astype(np.float32)
    steps = (np.arange(AB)[:, None] + AB) - np.arange(2 * AB)[None, :]
    own = (np.arange(2 * AB) >= AB)[None, :]
    out = []
    for window, dil in GROUPS:
        valid = (steps >= 0) & (steps <= window // dil)
        dist = slopes[:, None, None] * (steps * dil).astype(np.float32)[None]
        kinds = [np.where(v[None], dist, np.float32(MASK_BIAS)) for v in (valid, valid & own)]
        out.append(np.stack(kinds, axis=1))
    return np.stack(out).astype(np.float32)


def _attn_bias():
    return jnp.asarray(_attn_bias_np())


def _head_masks():
    lane = lax.broadcasted_iota(jnp.int32, (1, 128), 1)
    return (lane < HEAD_DIM, lane >= HEAD_DIM)


def _perm_chunks(S, d):
    L = S // d
    ch = min(L, 256)
    out = []
    for r in range(d):
        for c in range(L // ch):
            start = r + d * ch * c
            out.append((pl.ds(start, ch, stride=d) if d > 1 else pl.ds(start, ch), r * L + c * ch, ch))
    return out


def _stack_heads(x, masks):
    return jnp.concatenate([jnp.where(masks[0], x, 0), jnp.where(masks[1], x, 0)], axis=0)


def _block_row(j):
    return j * AB if isinstance(j, int) else pl.multiple_of(j * AB, AB)


def _three_stages(n, stage_a, stage_b, stage_c, unroll):
    stage_a(0)
    stage_a(1)
    stage_b(0)

    def body(j, carry):
        stage_c(j - 1)
        stage_b(j)
        stage_a(j + 1)
        return carry

    lax.fori_loop(1, n - 1, body, 0, unroll=unroll)
    stage_c(n - 2)
    stage_b(n - 1)
    stage_c(n - 1)


_NT = (((1,), (1,)), ((), ()))
_TN = (((0,), (0,)), ((), ()))
SCH = 128


def _attn_fwd(qn, kn, z8, bias, S):
    T = qn.shape[0]
    nb = T // S
    nblk = S // AB

    def body(q_ref, k_ref, v_ref, bias_ref, o_ref, ob_ref, lse_ref, qs, ks, vs, s2, p2, ogp, lgp, *group_scratch):
        og, lg = group_scratch[:3], group_scratch[3:]
        masks = _head_masks()
        ks[0:AB, :] = jnp.zeros((AB, 128), bf16)
        vs[0:AB, :] = jnp.zeros((AB, 128), bf16)

        for g, (_, d) in enumerate(GROUPS):
            nsub = S // (d * AB)
            chunks = _perm_chunks(S, d)
            for src, dst, ch in chunks:
                qs[dst:dst + ch, :] = q_ref[src, :].astype(bf16)
                ks[AB + dst:AB + dst + ch, :] = k_ref[src, :].astype(bf16)
                vs[AB + dst:AB + dst + ch, :] = v_ref[src, :].astype(bf16)
            od, ld = (og[g], lg[g]) if d == 1 else (ogp, lgp)

            def scores(j):
                r0 = _block_row(j)
                q2 = _stack_heads(qs[pl.ds(r0, AB), :], masks)
                s2[j] = lax.dot_general(q2, ks[pl.ds(r0, 2 * AB), :], _NT, preferred_element_type=f32)

            def softmax(j, g=g, nsub=nsub, ld=ld):
                r0 = _block_row(j)
                kind = int(j % nsub == 0) if isinstance(j, int) else (j % nsub == 0).astype(jnp.int32)
                for cc in range(AB // SCH):
                    lses = []
                    for hh in range(2):
                        rows = pl.ds(hh * AB + cc * SCH, SCH)
                        sb = s2[j, rows, :] - bias_ref[g, hh, kind, cc * SCH:(cc + 1) * SCH, :]
                        m = jnp.max(sb, axis=-1, keepdims=True)
                        p = jnp.exp(sb - m)
                        den = jnp.sum(p, axis=-1, keepdims=True)
                        p2[j, rows, :] = (p * (1.0 / den)).astype(bf16)
                        lses.append(m + jnp.log(den))
                    ld[pl.ds(r0 + cc * SCH, SCH), :] = jnp.where(masks[0], lses[0], lses[1])

            def values(j, od=od):
                r0 = _block_row(j)
                pv2 = jnp.dot(p2[j], vs[pl.ds(r0, 2 * AB), :], preferred_element_type=f32)
                od[pl.ds(r0, AB), :] = jnp.where(masks[0], pv2[:AB], pv2[AB:])

            _three_stages(nblk, scores, softmax, values, nblk - 2)

            if d > 1:
                for src, dst, ch in chunks:
                    og[g][src, :] = ogp[dst:dst + ch, :]
                    lg[g][src, :] = lgp[dst:dst + ch, :]

        def combine(i, carry):
            rr = pl.ds(pl.multiple_of(i * 256, 256), 256)
            l0, l1, l2 = lg[0][rr, :], lg[1][rr, :], lg[2][rr, :]
            mx = jnp.maximum(jnp.maximum(l0, l1), l2)
            e0, e1, e2 = jnp.exp(l0 - mx), jnp.exp(l1 - mx), jnp.exp(l2 - mx)
            den = e0 + e1 + e2
            o = (e0 * og[0][rr, :] + e1 * og[1][rr, :] + e2 * og[2][rr, :]) / den
            o_ref[rr, :] = o
            ob_ref[rr, :] = o.astype(bf16)
            lse_ref[rr, :] = mx + jnp.log(den)
            return carry

        lax.fori_loop(0, S // 256, combine, 0, unroll=True)

    blk = pl.BlockSpec((S, 128), lambda b, hp: (b, hp))
    return pl.pallas_call(
        body, name="attn_fwd", grid=(nb, N_HEADS // 2),
        in_specs=[blk, blk, pl.BlockSpec((None, S, 128), lambda b, hp: (Z_V, b, hp)),
                  pl.BlockSpec((3, 2, 2, AB, 2 * AB), lambda b, hp: (0, hp, 0, 0, 0))],
        out_specs=[blk, blk, blk],
        out_shape=[jax.ShapeDtypeStruct((T, D), f32), jax.ShapeDtypeStruct((T, D), bf16),
                   jax.ShapeDtypeStruct((T, D), f32)],
        scratch_shapes=[pltpu.VMEM((S, 128), bf16), pltpu.VMEM((S + AB, 128), bf16), pltpu.VMEM((S + AB, 128), bf16),
                        pltpu.VMEM((nblk, 2 * AB, 2 * AB), f32), pltpu.VMEM((nblk, 2 * AB, 2 * AB), bf16),
                        pltpu.VMEM((S, 128), f32), pltpu.VMEM((S, 128), f32)] + [pltpu.VMEM((S, 128), f32)] * 6,
        compiler_params=_cparams(("parallel", "parallel")))(qn, kn, z8, bias)


def _attn_bwd(qn, kn, z8, do, o, lse, bias, bd, qg, kg, dz8, S):
    T = qn.shape[0]
    nb = T // S

    nblk = S // AB

    def body(q_ref, k_ref, v_ref, do_ref, o_ref, lse_ref, bias_ref, bd_ref, qraw_ref, kraw_ref, qg_ref, kg_ref,
             dz_in, dz_ref, dqg_ref, dkg_ref,
             dq_ref, dk_ref, dv_ref, delta, qs, ks, vs, dos, lsp, dlp, s2, dp2, p2, ds2, dqp, dkp, dvp):
        del dz_in
        masks = _head_masks()
        bdv = bd_ref[...]
        dq_ref[...] = jnp.zeros_like(dq_ref)
        dk_ref[...] = jnp.zeros_like(dk_ref)
        dv_ref[...] = jnp.zeros_like(dv_ref)
        ks[0:AB, :] = jnp.zeros((AB, 128), bf16)
        vs[0:AB, :] = jnp.zeros((AB, 128), bf16)

        def prep(i, carry):
            rr = pl.ds(pl.multiple_of(i * 256, 256), 256)
            delta[rr, :] = _head_sum(do_ref[rr, :] * o_ref[rr, :], bdv)
            return carry

        lax.fori_loop(0, S // 256, prep, 0, unroll=True)

        for g, (_, d) in enumerate(GROUPS):
            nsub = S // (d * AB)
            chunks = _perm_chunks(S, d)
            for src, dst, ch in chunks:
                qs[dst:dst + ch, :] = q_ref[src, :].astype(bf16)
                ks[AB + dst:AB + dst + ch, :] = k_ref[src, :].astype(bf16)
                vs[AB + dst:AB + dst + ch, :] = v_ref[src, :].astype(bf16)
                dos[dst:dst + ch, :] = do_ref[src, :].astype(bf16)
                lsp[dst:dst + ch, :] = lse_ref[src, :]
                dlp[dst:dst + ch, :] = delta[src, :]
            dkp[...] = jnp.zeros_like(dkp)
            dvp[...] = jnp.zeros_like(dvp)

            def scores(j):
                r0 = _block_row(j)
                q2 = _stack_heads(qs[pl.ds(r0, AB), :], masks)
                do2 = _stack_heads(dos[pl.ds(r0, AB), :], masks)
                s2[j] = lax.dot_general(q2, ks[pl.ds(r0, 2 * AB), :], _NT, preferred_element_type=f32)
                dp2[j] = lax.dot_general(do2, vs[pl.ds(r0, 2 * AB), :], _NT, preferred_element_type=f32)

            def probs(j, g=g, nsub=nsub):
                r0 = _block_row(j)
                kind = int(j % nsub == 0) if isinstance(j, int) else (j % nsub == 0).astype(jnp.int32)
                for cc in range(AB // SCH):
                    lse_c = lsp[pl.ds(r0 + cc * SCH, SCH), :]
                    del_c = dlp[pl.ds(r0 + cc * SCH, SCH), :]
                    for hh in range(2):
                        c0 = hh * HEAD_DIM
                        rows = pl.ds(hh * AB + cc * SCH, SCH)
                        sb = s2[j, rows, :] - bias_ref[g, hh, kind, cc * SCH:(cc + 1) * SCH, :]
                        p = jnp.exp(sb - lse_c[:, c0:c0 + 1])
                        p2[j, rows, :] = p.astype(bf16)
                        ds2[j, rows, :] = (p * (dp2[j, rows, :] - del_c[:, c0:c0 + 1])).astype(bf16)

            def grads(j):
                r0 = _block_row(j)
                q2 = _stack_heads(qs[pl.ds(r0, AB), :], masks)
                do2 = _stack_heads(dos[pl.ds(r0, AB), :], masks)
                dsb = ds2[j]
                t = jnp.dot(dsb, ks[pl.ds(r0, 2 * AB), :], preferred_element_type=f32)
                dqp[pl.ds(r0, AB), :] = jnp.where(masks[0], t[:AB], t[AB:])
                dkp[pl.ds(r0, 2 * AB), :] += lax.dot_general(dsb, q2, _TN, preferred_element_type=f32)
                dvp[pl.ds(r0, 2 * AB), :] += lax.dot_general(p2[j], do2, _TN, preferred_element_type=f32)

            _three_stages(nblk, scores, probs, grads, nblk - 2)

            for src, dst, ch in chunks:
                dq_ref[src, :] += dqp[dst:dst + ch, :]
                dk_ref[src, :] += dkp[AB + dst:AB + dst + ch, :]
                dv_ref[src, :] += dvp[AB + dst:AB + dst + ch, :]

        @pl.when(pl.program_id(1) == 0)
        def _():
            dqg_ref[...] = jnp.zeros_like(dqg_ref)
            dkg_ref[...] = jnp.zeros_like(dkg_ref)

        def norms(i, carry):
            rr = pl.ds(pl.multiple_of(i * 256, 256), 256)

            def one(raw, dn_scaled, g, dg_ref, sec):
                rstd = lax.rsqrt(_head_sum(raw * raw, bdv) * (1.0 / HEAD_DIM) + EPS)
                n = raw * rstd
                dg_ref[...] += _colsum8(dn_scaled * n)
                dn = dn_scaled * g
                draw = rstd * (dn - n * (_head_sum(dn * n, bdv) * (1.0 / HEAD_DIM)))
                dz_ref[sec, rr, :] = draw.astype(bf16)

            one(qraw_ref[rr, :], dq_ref[rr, :] * (HEAD_DIM ** -0.5), qg_ref[...], dqg_ref, 0)
            one(kraw_ref[rr, :], dk_ref[rr, :], kg_ref[...], dkg_ref, 1)
            dz_ref[2, rr, :] = dv_ref[rr, :].astype(bf16)
            dz_ref[3, rr, :] = jnp.zeros((256, 128), bf16)
            return carry

        lax.fori_loop(0, S // 256, norms, 0, unroll=True)

    blk = pl.BlockSpec((S, 128), lambda hp, b: (b, hp))
    sec = lambda s: pl.BlockSpec((None, S, 128), lambda hp, b: (s, b, hp))
    gain = pl.BlockSpec((1, 128), lambda hp, b: (0, hp))
    row = lambda dt, pad=0: pltpu.VMEM((S + pad, 128), dt)
    blocks = lambda dt: pltpu.VMEM((nblk, 2 * AB, 2 * AB), dt)
    return pl.pallas_call(
        body, name="attn_bwd", grid=(N_HEADS // 2, nb),
        in_specs=[blk, blk, sec(Z_V), blk, blk, blk,
                  pl.BlockSpec((3, 2, 2, AB, 2 * AB), lambda hp, b: (0, hp, 0, 0, 0)),
                  pl.BlockSpec((128, 128), lambda hp, b: (0, 0)), sec(Z_Q), sec(Z_K), gain, gain,
                  pl.BlockSpec(memory_space=pl.ANY)],
        out_specs=[pl.BlockSpec((4, S, 128), lambda hp, b: (1, b, hp)),
                   pl.BlockSpec((8, 128), lambda hp, b: (0, hp)), pl.BlockSpec((8, 128), lambda hp, b: (0, hp))],
        out_shape=[jax.ShapeDtypeStruct(dz8.shape, bf16), jax.ShapeDtypeStruct((8, D), f32),
                   jax.ShapeDtypeStruct((8, D), f32)],
        input_output_aliases={12: 0},
        scratch_shapes=[row(f32), row(f32), row(f32),
                        row(f32), row(bf16), row(bf16, AB), row(bf16, AB), row(bf16), row(f32), row(f32),
                        blocks(f32), blocks(f32), blocks(bf16), blocks(bf16), row(f32), row(f32, AB), row(f32, AB)],
        compiler_params=_cparams(("parallel", "arbitrary")))(qn, kn, z8, do, o, lse, bias, bd, z8, z8, qg, kg, dz8)


def _any_spec():
    return pl.BlockSpec(memory_space=pl.ANY)


def _allgather_rows(shards, n_full):
    n = len(shards)

    def body(*refs):
        ins, outs = refs[:n], refs[n:2 * n]
        send_sems, recv_sems, local_sems = refs[2 * n:]
        x, y, c, me = _my_pos()
        sibling = (x, y, 1 - c)
        chips = [(1 - x, y), (x, 1 - y), (1 - x, 1 - y)]

        def idx(px, py, pc):
            return 4 * px + 2 * py + pc

        def copy(a, k, blk, to, src=None):
            return pltpu.make_async_remote_copy(
                src_ref=outs[a].at[blk] if src is None else src, dst_ref=outs[a].at[blk],
                send_sem=send_sems.at[a, k], recv_sem=recv_sems.at[a, k], device_id=to, device_id_type=MESH)

        mine = [pltpu.make_async_copy(ins[a], outs[a].at[me], local_sems.at[a]) for a in range(n)]
        for cp in mine:
            cp.start()
        first = []
        for a in range(n_full):
            first.append(copy(a, 0, me, sibling, src=ins[a]))
            first += [copy(a, 1 + j, me, (*chip, c), src=ins[a]) for j, chip in enumerate(chips)]
        for cp in first:
            cp.start()
        passed = []
        for a in range(n_full):
            for j, chip in enumerate(chips):
                blk = idx(*chip, c)
                copy(a, 1 + j, blk, (x, y, c)).wait_recv()
                cp = copy(a, 4 + j, blk, sibling)
                cp.start()
                passed.append(cp)
        for a in range(n_full):
            copy(a, 0, idx(x, y, 1 - c), (x, y, c)).wait_recv()
            for j, chip in enumerate(chips):
                copy(a, 4 + j, idx(*chip, 1 - c), (x, y, c)).wait_recv()
        for cp in first + passed:
            cp.wait_send()
        for cp in mine:
            cp.wait()

    return pl.pallas_call(
        body, name="allgather_weights",
        in_specs=[_any_spec()] * n, out_specs=[_any_spec()] * n,
        out_shape=[jax.ShapeDtypeStruct((N_DEV,) + s.shape, s.dtype) for s in shards],
        scratch_shapes=[pltpu.SemaphoreType.DMA((n_full, 7)), pltpu.SemaphoreType.DMA((n_full, 7)),
                        pltpu.SemaphoreType.DMA((n,))],
    )(*shards)


def _peer(x, y, c, k):
    tx = 1 - x if (k >> 2) & 1 else x
    ty = 1 - y if (k >> 1) & 1 else y
    tc = 1 - c if k & 1 else c
    return (tx, ty, tc), 4 * tx + 2 * ty + tc


_PEER_ORDER = (2, 4, 6, 3, 5, 7, 1)


_HBM = pl.BlockSpec(memory_space=pltpu.HBM)
_SEM = pl.BlockSpec(memory_space=pltpu.SEMAPHORE)
_EFFECT = pltpu.SideEffectType.DATAFLOW_SIDE_EFFECTING


def _exchange_copies(srcs, lands, send_sems, recv_sems, gather):
    x, y, c, me = _my_pos()
    copies = []
    for k in _PEER_ORDER:
        tgt, tidx = _peer(x, y, c, k)
        for a in range(len(srcs)):
            copies.append(pltpu.make_async_remote_copy(
                src_ref=srcs[a] if gather else srcs[a].at[tidx], dst_ref=lands[a].at[me],
                send_sem=send_sems.at[7 * a + k - 1], recv_sem=recv_sems.at[7 * a + k - 1],
                device_id=tgt, device_id_type=MESH))
    return copies


def _exchange_start(name, srcs, lands=None, after=None):
    n = len(srcs)
    gather = lands is not None
    if lands is None:
        lands = [lax.empty(g.shape, g.dtype) for g in srcs]
    extra = [] if after is None else [after]

    def body(*refs):
        src_refs, land_refs = refs[:n], refs[n:2 * n]
        send_sems, recv_sems = refs[2 * n + len(extra)], refs[2 * n + len(extra) + 1]
        token = refs[-1]
        for cp in _exchange_copies(src_refs, land_refs, send_sems, recv_sems, gather):
            cp.start()
        token[...] = jnp.zeros_like(token)

    hbm = lambda a: pltpu.with_memory_space_constraint(a, pltpu.HBM)
    outs = pl.pallas_call(
        body, name=name,
        out_shape=(pltpu.SemaphoreType.DMA((7 * n,)), pltpu.SemaphoreType.DMA((7 * n,)),
                   *[pltpu.HBM(g.shape, g.dtype) for g in list(srcs) + list(lands)],
                   jax.ShapeDtypeStruct((8, 128), f32)),
        in_specs=[_HBM] * (2 * n) + [pl.BlockSpec(memory_space=pl.ANY)] * len(extra),
        out_specs=(_SEM, _SEM, *([_HBM] * (2 * n)), pl.BlockSpec(memory_space=pltpu.VMEM)),
        input_output_aliases={i: 2 + i for i in range(2 * n)},
        compiler_params=pltpu.CompilerParams(has_side_effects=_EFFECT),
    )(*[hbm(g) for g in srcs], *[hbm(g) for g in lands], *extra)
    return outs[0], outs[1], list(outs[2:2 + n]), list(outs[2 + n:2 + 2 * n]), outs[-1], gather


def _exchange_wait(name, started, after):
    send_sems, recv_sems, srcs, lands, _, gather = started
    n = len(srcs)
    after = list(after) if isinstance(after, (list, tuple)) else [after]

    def body(*refs):
        src_refs, land_refs = refs[:n], refs[n:2 * n]
        s_sems, r_sems = refs[2 * n], refs[2 * n + 1]
        for cp in _exchange_copies(src_refs, land_refs, s_sems, r_sems, gather):
            cp.wait_send()
            cp.wait_recv()

    outs = pl.pallas_call(
        body, name=name,
        out_shape=tuple(pltpu.HBM(a.shape, a.dtype) for a in list(srcs) + list(lands)),
        in_specs=[_HBM] * (2 * n) + [_SEM, _SEM] + [pl.BlockSpec(memory_space=pl.ANY)] * len(after),
        out_specs=tuple([_HBM] * (2 * n)),
        input_output_aliases={i: i for i in range(2 * n)},
        compiler_params=pltpu.CompilerParams(has_side_effects=_EFFECT),
    )(*srcs, *lands, send_sems, recv_sems, *after)
    return list(outs[:n]), list(outs[n:])


SMALL_ROWS = 128


def _small_start(name, sg, after=None):
    return _exchange_start(name, [sg], [lax.empty((N_DEV,) + sg.shape, f32)], after=after)


def _small_sum(name, me, started, after):
    (own,), (slots,) = _exchange_wait(name + "_wait", started, after)

    def body(me_ref, s_ref, own_ref, out_ref):
        acc = None
        for p in range(N_DEV):
            term = lax.cond(me_ref[0] == p, lambda: own_ref[...], lambda p=p: s_ref[p])
            acc = term if acc is None else acc + term
        out_ref[...] = acc

    return pl.pallas_call(
        body, name=name + "_sum",
        in_specs=[pl.BlockSpec(memory_space=pltpu.SMEM), pl.BlockSpec(memory_space=pltpu.VMEM),
                  pl.BlockSpec(memory_space=pltpu.VMEM)],
        out_specs=pl.BlockSpec(memory_space=pltpu.VMEM),
        out_shape=jax.ShapeDtypeStruct(own.shape, f32))(me, slots, own)


def _adam_math(g, w, m, v):
    m = ADAM_B1 * m + (1.0 - ADAM_B1) * g
    v = ADAM_B2 * v + (1.0 - ADAM_B2) * (g * g)
    m_hat = m / (1.0 - ADAM_B1 ** ADAM_STEP)
    v_hat = v / (1.0 - ADAM_B2 ** ADAM_STEP)
    delta = -ADAM_LR * (m_hat / (jnp.sqrt(v_hat) + ADAM_EPS) + ADAM_WD * w)
    return delta, m, v


def _adam_slots(name, me, slots, own, w, m, v, tr, transposed=False):
    rows = slots.shape[1]

    def body(me_ref, s_ref, own_ref, w_ref, m_ref, v_ref, g_ref, d_ref, nm_ref, nv_ref):
        mine = own_ref[...]
        g = None
        for p in range(N_DEV):
            term = lax.cond(me_ref[0] == p, lambda: mine, lambda p=p: s_ref[p]).astype(f32)
            g = term if g is None else g + term
        if transposed:
            g = g.T
        delta, nm, nv = _adam_math(g, w_ref[...], m_ref[...], v_ref[...])
        g_ref[...] = g
        d_ref[...] = delta
        nm_ref[...] = nm
        nv_ref[...] = nv

    mode = dict(pipeline_mode=pl.Buffered(1)) if rows == tr else {}
    if transposed:
        rs = pl.BlockSpec((D, tr), lambda i, me_ref: (0, i))
        rs_in = pl.BlockSpec((D, tr), lambda i, me_ref: (0, i), **mode)
    else:
        rs = pl.BlockSpec((tr, D), lambda i, me_ref: (i, 0))
        rs_in = pl.BlockSpec((tr, D), lambda i, me_ref: (i, 0), **mode)
    return pl.pallas_call(
        body, name=name,
        grid_spec=pltpu.PrefetchScalarGridSpec(
            num_scalar_prefetch=1, grid=(rows // tr,),
            in_specs=[pl.BlockSpec((N_DEV, tr, D), lambda i, me_ref: (0, i, 0), **mode),
                      pl.BlockSpec((None, tr, D), lambda i, me_ref: (me_ref[0], i, 0), **mode), rs_in, rs_in, rs_in],
            out_specs=[rs] * 4),
        out_shape=[jax.ShapeDtypeStruct(w.shape, f32)] * 4,
        compiler_params=_cparams(("parallel",)))(me, slots, own, w, m, v)


def _adam_small(g, w, m, v):
    def body(g_ref, w_ref, m_ref, v_ref, d_ref, nm_ref, nv_ref):
        delta, nm, nv = _adam_math(g_ref[...], w_ref[...], m_ref[...], v_ref[...])
        d_ref[...] = delta
        nm_ref[...] = nm
        nv_ref[...] = nv

    return pl.pallas_call(body, name="adam_small", out_shape=[jax.ShapeDtypeStruct(g.shape, f32)] * 3)(g, w, m, v)


FFN_PAD = 6 * D


_SMALL_PARTS = (("norm1_g", 1), ("gate_b", 2), ("conv_w", CONV_WIDTH), ("conv_b", 1), ("conv_norm_g", 1),
                ("q_norm_g", 1), ("k_norm_g", 1), ("norm2_g", 1), ("ffn_conv_w", 18), ("ffn_conv_b", 6), ("last", 1))


def _small_offsets():
    out, row = {}, 0
    for name, rows in _SMALL_PARTS:
        out[name] = row
        row += -(-rows // 8) * 8
    assert row == SMALL_ROWS
    return out


def _pack_small(norm1_g, gate_b, conv_w, conv_b, conv_norm_g, q_norm_g, k_norm_g, norm2_g, ffn_conv_w, ffn_conv_b,
                last_row=None):
    pad_h = lambda a: jnp.pad(a, ((0, 0), (0, D - HEAD_DIM)))
    pad_f = lambda a: jnp.pad(a, ((0, 0), (0, FFN_PAD - 2 * D_FF))).reshape(-1, D)
    parts = [norm1_g, gate_b.reshape(2, D), conv_w, conv_b, conv_norm_g, pad_h(q_norm_g), pad_h(k_norm_g), norm2_g,
             pad_f(ffn_conv_w), pad_f(ffn_conv_b), jnp.zeros((1, D), f32) if last_row is None else last_row]
    return jnp.concatenate([jnp.pad(p, ((0, -p.shape[0] % 8), (0, 0))) for p in parts], axis=0)


def _unpack_small(p):
    o = _small_offsets()
    rows = lambda name, n: p[o[name]:o[name] + n]
    ffn = lambda a: a.reshape(-1, FFN_PAD)[:, :2 * D_FF]
    return dict(
        norm1_g=rows("norm1_g", 1), gate_b=rows("gate_b", 2).reshape(1, 2 * D), conv_w=rows("conv_w", CONV_WIDTH),
        conv_b=rows("conv_b", 1), conv_norm_g=rows("conv_norm_g", 1), q_norm_g=rows("q_norm_g", 1)[:, :HEAD_DIM],
        k_norm_g=rows("k_norm_g", 1)[:, :HEAD_DIM], norm2_g=rows("norm2_g", 1),
        ffn_conv_w=ffn(rows("ffn_conv_w", 18)), ffn_conv_b=ffn(rows("ffn_conv_b", 6)))


_ADAM_TILE = {896: 128, 704: 704, 128: 128, 352: 176}


def kernel(x, norm1_g, w_in, gate_b, conv_w, conv_b, conv_norm_g, w_conv_out, q_norm_g, k_norm_g, w_attn_out, w_out, norm2_g, w_up, ffn_conv_w, ffn_conv_b, w_down, loss_target, m_norm1_g, m_w_in, m_gate_b, m_conv_w, m_conv_b, m_conv_norm_g, m_w_conv_out, m_q_norm_g, m_k_norm_g, m_w_attn_out, m_w_out, m_norm2_g, m_w_up, m_ffn_conv_w, m_ffn_conv_b, m_w_down, v_norm1_g, v_w_in, v_gate_b, v_conv_w, v_conv_b, v_conv_norm_g, v_w_conv_out, v_q_norm_g, v_k_norm_g, v_w_attn_out, v_w_out, v_norm2_g, v_w_up, v_ffn_conv_w, v_ffn_conv_b, v_w_down):
    BL, S, _ = x.shape
    T = BL * S
    me = 4 * lax.axis_index("x") + 2 * lax.axis_index("y") + lax.axis_index("c")
    xt = x.reshape(T, D)
    target = loss_target.reshape(T, D)

    big = dict(w_in=(w_in[0], m_w_in[0], v_w_in[0]), w_up=(w_up[0], m_w_up[0], v_w_up[0]),
               w_conv_out=(w_conv_out[0], m_w_conv_out[0], v_w_conv_out[0]),
               w_attn_out=(w_attn_out[0], m_w_attn_out[0], v_w_attn_out[0]),
               w_out=(w_out[0], m_w_out[0], v_w_out[0]), w_down=(w_down[0], m_w_down[0], v_w_down[0]))
    order = ["w_in", "w_conv_out", "w_attn_out", "w_out", "w_up", "w_down"]
    shards = [(big[n][0].T if n in ("w_in", "w_up") else big[n][0]).astype(bf16) for n in order]
    gathered = _allgather_rows(shards, 1)
    W = {"w_in": gathered[0].reshape(-1, D)}

    def place_cols(shard, full_cols):
        z = jnp.zeros((shard.shape[0], full_cols), f32)
        return lax.dynamic_update_slice(z, shard, (0, me * shard.shape[1]))

    zr = lambda a: jnp.zeros_like(a)
    conv_local = _pack_small(
        zr(norm1_g), zr(gate_b), place_cols(conv_w[0], D), zr(conv_b), zr(conv_norm_g), zr(q_norm_g), zr(k_norm_g),
        zr(norm2_g), place_cols(ffn_conv_w[0], 2 * D_FF), zr(ffn_conv_b))
    ga_conv = _small_start("gather_conv_start", conv_local, after=gathered[0])
    ga_proj = _exchange_start("gather_start_proj", shards[1:4], gathered[1:4], after=ga_conv[4])
    ga_ffn = _exchange_start("gather_start_ffn", shards[4:6], gathered[4:6], after=ga_proj[4])

    bd = (jnp.arange(128)[:, None] // HEAD_DIM == jnp.arange(128)[None, :] // HEAD_DIM).astype(bf16)
    bias = _attn_bias()
    qg = jnp.tile(q_norm_g, (1, N_HEADS))
    kg = jnp.tile(k_norm_g, (1, N_HEADS))

    z8, h, qn, kn = _in_proj_fwd(xt, norm1_g, W["w_in"], qg, kg, bd, ga_ffn[4])
    conv_all = _unpack_small(_small_sum("gather_conv", me.reshape(1), ga_conv, z8))
    conv_w_full, ffn_w_full = conv_all["conv_w"], conv_all["ffn_conv_w"]
    c = _conv_fwd(z8, conv_w_full, conv_b, S)
    o, ob, lse = _attn_fwd(qn, kn, z8, bias, S)
    for n, g in zip(order[1:4], _exchange_wait("gather_wait_proj", ga_proj, ob)[1]):
        W[n] = g.reshape(-1, D)
    s, ya, yb, mixed = _branches_fwd(c, ob, z8, conv_norm_g, gate_b, W["w_conv_out"], W["w_attn_out"])
    x1, h2 = _out_norm2_fwd(mixed, W["w_out"], xt, norm2_g)
    for n, g in zip(order[4:6], _exchange_wait("gather_wait_ffn", ga_ffn, x1)[1]):
        W[n] = g.reshape(-1, D)
    TNU = D_FF // 2
    u3 = _matmul_call(
        "mm_u", h2, W["w_up"],
        pl.BlockSpec((1024, D), lambda i, j, k: (i, 0)),
        pl.BlockSpec((TNU, D), lambda i, j, k: (j, 0)),
        pl.BlockSpec((None, 1024, TNU), lambda i, j, k: (j // 2, i, j % 2)),
        jax.ShapeDtypeStruct((2, T, D_FF), f32), (T // 1024, 4, 1), "nt", 1, 1024, TNU)
    f = _ffn_fwd(u3, ffn_w_full, ffn_conv_b, S)
    dy, dyb, lacc = _down_loss_fwd(f, W["w_down"], x1, target)
    loss_local = 0.5 / D * jnp.sum(lacc)

    df = _matmul("mm_df", dyb, W["w_down"], "nt", f32, tn=TNU)
    g_w_down = _matmul("mm_dwdn", f, dyb, "tn", bf16, tm=TNU)
    du3, dffn = _ffn_bwd(u3, df, ffn_w_full, ffn_conv_b, S)
    g_w_up = _matmul_call(
        "mm_dwup", du3, h2,
        pl.BlockSpec((None, T, TNU), lambda i, j, k: (i // 2, 0, i % 2)),
        pl.BlockSpec((T, D), lambda i, j, k: (0, 0)),
        pl.BlockSpec((TNU, D), lambda i, j, k: (i, 0)),
        jax.ShapeDtypeStruct((2 * D_FF, D), bf16), (4, 1, 1), "tn", 1, TNU, D)
    blocks8 = lambda a: a.reshape(N_DEV, -1, D)
    ex_ffn = _exchange_start("scatter_start_ffn", [blocks8(g_w_up), blocks8(g_w_down)])
    dx1, dx1b, dg_norm2 = _up_norm2_bwd(du3, W["w_up"], x1, dy, norm2_g, ex_ffn[4])
    g_w_out = _matmul("mm_dwo", mixed, dx1b, "tn", bf16, tm=512)
    dz8 = lax.empty((8, T, D), bf16)
    dya, dyb2, dz8, dg_gate = _out_gate_bwd(dx1b, W["w_out"], z8, gate_b, ya, yb, dz8)
    g_w_conv_out = _matmul("mm_dwco", s, dya, "tn", bf16, tm=512)
    g_w_attn_out = _matmul("mm_dwao", ob, dyb2, "tn", bf16, tm=512)
    ex_proj = _exchange_start("scatter_start_proj", [blocks8(g_w_conv_out), blocks8(g_w_attn_out), blocks8(g_w_out)])
    do = _matmul("mm_do", dyb2, W["w_attn_out"], "nt", f32, after=ex_proj[4])
    dc, dg_convnorm = _convnorm_bwd(dya, W["w_conv_out"], c, conv_norm_g)
    dz8a, dconv = _conv_bwd(dc, z8, conv_w_full, dz8, S)
    dz8b, dg_q, dg_k = _attn_bwd(qn, kn, z8, do, o, lse, bias, bd, qg, kg, dz8a, S)
    g_w_in = _matmul_call(
        "mm_dwin", dz8b, h,
        pl.BlockSpec((None, T, D), lambda i, j, k: (jnp.where(i < 2, i, jnp.where(i < 5, i + 2, i - 3)), 0, 0)),
        pl.BlockSpec((T, D), lambda i, j, k: (0, 0)), pl.BlockSpec((1024, D), lambda i, j, k: (i, 0)),
        jax.ShapeDtypeStruct((7 * D, D), bf16), (7, 1, 1), "tn", 1, D, D)
    ex_in = _exchange_start("scatter_start_in", [blocks8(g_w_in)])
    grad_x, dg_norm1 = _in_norm1_bwd(dz8b, W["w_in"], xt, dx1, norm1_g, ex_in[4])

    sum8 = lambda a: a.reshape(-1, 8, a.shape[-1]).sum(axis=1)
    dconv_s = sum8(dconv.sum(axis=0))
    dffn_s = dffn.sum(axis=0).reshape(2, 4, 8, D_FF).sum(axis=2)
    dffn_w = jnp.concatenate([dffn_s[0, :3], dffn_s[1, :3]], axis=1)
    dffn_b = jnp.concatenate([dffn_s[0, 3:4], dffn_s[1, 3:4]], axis=1)
    fold = lambda a: sum8(a).reshape(N_HEADS, HEAD_DIM).sum(axis=0)[None]
    small_g_local = _pack_small(
        sum8(dg_norm1), sum8(dg_gate), dconv_s[:CONV_WIDTH], dconv_s[CONV_WIDTH:], sum8(dg_convnorm),
        fold(dg_q), fold(dg_k), sum8(dg_norm2), dffn_w, dffn_b,
        last_row=jnp.pad(loss_local.reshape(1, 1), ((0, 0), (0, D - 1))))
    sg_start = _small_start("small_grads_start", small_g_local)

    place_m = lambda a, full: place_cols(a[0], full)
    small_w_true = _pack_small(norm1_g, gate_b, conv_w_full, conv_b, conv_norm_g, q_norm_g, k_norm_g, norm2_g,
                               ffn_w_full, ffn_conv_b)
    small_m = _pack_small(m_norm1_g, m_gate_b, place_m(m_conv_w, D), m_conv_b, m_conv_norm_g, m_q_norm_g, m_k_norm_g,
                          m_norm2_g, place_m(m_ffn_conv_w, 2 * D_FF), m_ffn_conv_b)
    small_v = _pack_small(v_norm1_g, v_gate_b, place_m(v_conv_w, D), v_conv_b, v_conv_norm_g, v_q_norm_g, v_k_norm_g,
                          v_norm2_g, place_m(v_ffn_conv_w, 2 * D_FF), v_ffn_conv_b)

    own, slots = {}, {}
    for tag, ex, names_ in (("ffn", ex_ffn, ("w_up", "w_down")),
                            ("proj", ex_proj, ("w_conv_out", "w_attn_out", "w_out")), ("in", ex_in, ("w_in",))):
        sent, landed = _exchange_wait("scatter_wait_" + tag, ex, [sg_start[4], small_w_true, small_m, small_v])
        for n, src, land in zip(names_, sent, landed):
            own[n], slots[n] = src, land

    res, adam_done = {}, []
    for n in order:
        w, m, v = big[n]
        outs = _adam_slots("adam_" + n, me.reshape(1), slots[n], own[n], w, m, v, _ADAM_TILE[slots[n].shape[1]],
                           transposed=n in ("w_in", "w_up"))
        adam_done.append(outs[0])
        res[n] = [a[None] for a in outs]
    small_g = _small_sum("small_grads", me.reshape(1), sg_start, adam_done)
    loss = small_g[_small_offsets()["last"], 0]

    col = lambda a, width: lax.dynamic_slice(a, (0, me * width), (a.shape[0], width))
    sd, sm, sv = _adam_small(small_g, small_w_true, small_m, small_v)
    for i, packed in enumerate((small_g, sd, sm, sv)):
        u = _unpack_small(packed)
        u["conv_w"] = col(u["conv_w"], D // N_DEV)
        u["ffn_conv_w"] = col(u["ffn_conv_w"], 2 * D_FF // N_DEV)
        for n, a in u.items():
            res.setdefault(n, [None] * 4)[i] = a[None] if n in ("conv_w", "ffn_conv_w") else a

    names = ["norm1_g", "w_in", "gate_b", "conv_w", "conv_b", "conv_norm_g", "w_conv_out", "q_norm_g", "k_norm_g",
             "w_attn_out", "w_out", "norm2_g", "w_up", "ffn_conv_w", "ffn_conv_b", "w_down"]
    out = [loss, grad_x.reshape(BL, S, D)]
    for i in range(4):
        out += [res[n][i] for n in names]
    return tuple(out)
```

```python
import functools

import jax
import jax.numpy as jnp
import numpy as np
from jax import lax
from jax.experimental import pallas as pl
from jax.experimental.pallas import tpu as pltpu

f32 = jnp.float32
bf16 = jnp.bfloat16

D = 1024
N_HEADS = 16
HEAD_DIM = 64
CONV_WIDTH = 31
D_FF = 2816
GROUPS = ((128, 1), (512, 4), (2048, 16))
ATTN_BLOCK = 128
EPS = 1e-6
N_DEV = 8
MESH = pl.DeviceIdType.MESH

ADAM_LR = 0.001
ADAM_B1 = 0.9
ADAM_B2 = 0.999
ADAM_EPS = 1e-08
ADAM_WD = 0.01
ADAM_STEP = 10

VMEM_LIMIT = 56 * 1024 * 1024
MASK_BIAS = 1e30

Z_AVAL, Z_AGATE, Z_GA, Z_GB, Z_Q, Z_K, Z_V = 0, 1, 2, 3, 4, 5, 6


_W_OF_Z = (0, 1, 5, 6, 2, 3, 4)


def _wsec_of_zsec(j):
    return jnp.where(j < 2, j, jnp.where(j < 4, j + 3, j - 2))


def _sig(x):
    return 1.0 / (1.0 + jnp.exp(-x))


def _colsum8(x):
    return x.reshape(-1, 8, x.shape[-1]).sum(axis=0)


def _cparams(sem):
    return pltpu.CompilerParams(dimension_semantics=sem, vmem_limit_bytes=VMEM_LIMIT)


def _my_pos():
    x, y, c = lax.axis_index("x"), lax.axis_index("y"), lax.axis_index("c")
    return x, y, c, 4 * x + 2 * y + c


_DIMS = {"nn": ((1,), (0,)), "nt": ((1,), (1,)), "tn": ((0,), (0,))}


def _matmul_call(name, a, b, a_spec, b_spec, o_spec, out_shape, grid, mode, nk, tm, tn, after=None):
    dims = (_DIMS[mode], ((), ()))
    extra = [] if after is None else [after]

    def body(a_ref, b_ref, *rest):
        o_ref, scratch = rest[len(extra)], rest[len(extra) + 1:]
        part = lax.dot_general(a_ref[...], b_ref[...], dims, preferred_element_type=f32)
        if nk == 1:
            o_ref[...] = part.astype(o_ref.dtype)
        else:
            acc = scratch[0]
            k = pl.program_id(2)

            @pl.when(k == 0)
            def _():
                acc[...] = part

            @pl.when(k > 0)
            def _():
                acc[...] += part

            @pl.when(k == nk - 1)
            def _():
                o_ref[...] = acc[...].astype(o_ref.dtype)

    scratch = [] if nk == 1 else [pltpu.VMEM((tm, tn), f32)]
    return pl.pallas_call(
        body, name=name, grid=grid, in_specs=[a_spec, b_spec] + [pl.BlockSpec(memory_space=pl.ANY)] * len(extra),
        out_specs=o_spec, out_shape=out_shape,
        scratch_shapes=scratch, compiler_params=_cparams(("parallel", "parallel", "arbitrary")),
    )(a, b, *extra)


def _matmul(name, a, b, mode, out_dtype, tm=1024, tn=1024, tk=None, after=None):
    if mode == "nn":
        (M, K), (_, N) = a.shape, b.shape
    elif mode == "nt":
        (M, K), (N, _) = a.shape, b.shape
    else:
        (K, M), (_, N) = a.shape, b.shape
    tm, tn = min(tm, M), min(tn, N)
    tk = K if tk is None else tk
    nk = K // tk
    assert M % tm == 0 and N % tn == 0 and K % tk == 0
    if mode == "tn":
        a_spec = pl.BlockSpec((tk, tm), lambda i, j, k: (k, i))
    else:
        a_spec = pl.BlockSpec((tm, tk), lambda i, j, k: (i, k))
    if mode == "nt":
        b_spec = pl.BlockSpec((tn, tk), lambda i, j, k: (j, k))
    else:
        b_spec = pl.BlockSpec((tk, tn), lambda i, j, k: (k, j))
    o_spec = pl.BlockSpec((tm, tn), lambda i, j, k: (i, j))
    return _matmul_call(name, a, b, a_spec, b_spec, o_spec, jax.ShapeDtypeStruct((M, N), out_dtype),
                        (M // tm, N // tn, nk), mode, nk, tm, tn, after=after)


FTM = 512


def _matmul_fused(name, a, b, pairs, epilogue, extras, consts, outs, nt=False, sums=False, passed=(), aliases=None):
    sa, M, kk = a.shape
    na = max(i for i, _ in pairs) + 1
    ne, nc, npass = len(extras), len(consts), len(passed)
    dims = (_DIMS["nt" if nt else "nn"], ((), ()))

    def body(a_ref, b_ref, *rest):
        acc = None
        for i, j in pairs:
            part = lax.dot_general(a_ref[i], b_ref[j], dims, preferred_element_type=f32)
            acc = part if acc is None else acc + part
        epilogue(acc, rest[:ne], rest[ne:ne + nc], rest[ne + nc + npass:])

    whole = lambda arr: pl.BlockSpec(arr.shape, lambda i, nd=arr.ndim: (0,) * nd, pipeline_mode=pl.Buffered(1))
    io_alias = {2 + ne + nc + k: v for k, v in (aliases or {}).items()}
    return pl.pallas_call(
        body, name=name, grid=(M // FTM,),
        in_specs=[pl.BlockSpec((na, FTM, kk), lambda i: (0, i, 0)), whole(b)] + [s for _, s in extras]
        + [whole(c) for c in consts] + [pl.BlockSpec(memory_space=pl.ANY)] * npass,
        out_specs=[s for _, s in outs], out_shape=[s for s, _ in outs], input_output_aliases=io_alias,
        compiler_params=_cparams(("arbitrary" if sums else "parallel",)),
    )(a, b, *[x for x, _ in extras], *consts, *passed)


def _frows(c=D):
    return pl.BlockSpec((FTM, c), lambda i: (i, 0))


def _fsec(s):
    return pl.BlockSpec((None, FTM, D), lambda i: (s, i, 0))


def _rowshape(T, dtype, c=D):
    return (jax.ShapeDtypeStruct((T, c), dtype), _frows(c))


def _sumshape(c=D):
    return (jax.ShapeDtypeStruct((8, c), f32), pl.BlockSpec((8, c), lambda i: (0, 0)))


def _add_colsum(ref, x, cols=None):
    @pl.when(pl.program_id(0) == 0)
    def _():
        if cols is None:
            ref[...] = jnp.zeros_like(ref)
        else:
            ref[:, cols] = jnp.zeros((8, x.shape[-1]), f32)

    if cols is None:
        ref[...] += _colsum8(x)
    else:
        ref[:, cols] += _colsum8(x)


def _rms(x):
    return lax.rsqrt(jnp.mean(x * x, axis=-1, keepdims=True) + EPS)


def _rms_bwd(dy_g, xn, rstd):
    return rstd * (dy_g - xn * jnp.mean(dy_g * xn, axis=-1, keepdims=True))


def _head_sum(x, bd):
    parts = []
    for cb in range(x.shape[-1] // 128):
        xb = x[:, cb * 128:(cb + 1) * 128]
        hi = xb.astype(bf16)
        lo = (xb - hi.astype(f32)).astype(bf16)
        parts.append(jnp.dot(hi, bd, preferred_element_type=f32) + jnp.dot(lo, bd, preferred_element_type=f32))
    return parts[0] if len(parts) == 1 else jnp.concatenate(parts, axis=1)


ZTM = 1024


def _in_proj_fwd(x, g, w_in_t, qg, kg, bd, after):
    T = x.shape[0]
    nt = T // ZTM

    def body(x_ref, g_ref, w_ref, qg_ref, kg_ref, bd_ref, after_ref, z_ref, h_ref, qn_ref, kn_ref, hbuf):
        del after_ref
        j, i = pl.program_id(0), pl.program_id(1)
        rows = pl.ds(pl.multiple_of(i * ZTM, ZTM), ZTM)

        @pl.when(j == 0)
        def _():
            xv = x_ref[...]
            hv = (xv * _rms(xv) * g_ref[...]).astype(bf16)
            hbuf[rows, :] = hv
            h_ref[...] = hv

        def project():
            z = lax.dot_general(hbuf[rows, :], w_ref[...], (_DIMS["nt"], ((), ())), preferred_element_type=f32)
            z_ref[...] = z
            return z

        def head_norm(z, gain_ref, scale):
            return z * lax.rsqrt(_head_sum(z * z, bd_ref[...]) * (1.0 / HEAD_DIM) + EPS) * gain_ref[...] * scale

        @pl.when(j == Z_Q)
        def _():
            qn_ref[...] = head_norm(project(), qg_ref, HEAD_DIM ** -0.5)

        @pl.when(j == Z_K)
        def _():
            kn_ref[...] = head_norm(project(), kg_ref, 1.0)

        @pl.when((j != Z_Q) & (j != Z_K))
        def _():
            project()

    def tile_at(sec):
        return pl.BlockSpec((ZTM, D), lambda j, i: (jnp.where(j < sec, 0, jnp.where(j == sec, i, nt - 1)), 0))

    row = pl.BlockSpec((1, D), lambda j, i: (0, 0))
    return pl.pallas_call(
        body, name="mm_z", grid=(7, nt),
        in_specs=[tile_at(0), row, pl.BlockSpec((D, D), lambda j, i: (_wsec_of_zsec(j), 0)), row, row,
                  pl.BlockSpec((128, 128), lambda j, i: (0, 0)), pl.BlockSpec(memory_space=pl.ANY)],
        out_specs=[pl.BlockSpec((None, ZTM, D), lambda j, i: (j, i, 0)), tile_at(0), tile_at(Z_Q), tile_at(Z_K)],
        out_shape=[jax.ShapeDtypeStruct((8, T, D), f32), jax.ShapeDtypeStruct((T, D), bf16),
                   jax.ShapeDtypeStruct((T, D), f32), jax.ShapeDtypeStruct((T, D), f32)],
        scratch_shapes=[pltpu.VMEM((T, D), bf16)],
        compiler_params=_cparams(("arbitrary", "arbitrary")))(x, g, w_in_t, qg, kg, bd, after)


def _branches_fwd(c, ob, z8, g, gate_b, w_conv_out, w_attn_out):
    T = c.shape[0]

    def epilogue(yb, extra, const, out):
        cv = extra[0][...]
        r = cv * _rms(cv) * const[0][...]
        s = (r * _sig(r)).astype(bf16)
        ya = jnp.dot(s, const[2][...], preferred_element_type=f32)
        b_ref = const[1]
        g_a = _sig(extra[1][...] + b_ref[:, :D])
        g_b = _sig(extra[2][...] + b_ref[:, D:])
        out[0][...] = s
        out[1][...] = ya
        out[2][...] = yb
        out[3][...] = (g_a * ya + g_b * yb).astype(bf16)

    return _matmul_fused("mm_branches", ob[None], w_attn_out[None], ((0, 0),), epilogue,
                         [(c, _frows()), (z8, _fsec(Z_GA)), (z8, _fsec(Z_GB))], [g, gate_b, w_conv_out],
                         [_rowshape(T, bf16), _rowshape(T, f32), _rowshape(T, f32), _rowshape(T, bf16)])


def _out_norm2_fwd(mixed, w_out, x, g):
    T = x.shape[0]

    def epilogue(acc, extra, const, out):
        x1 = extra[0][...] + acc
        out[0][...] = x1
        out[1][...] = (x1 * _rms(x1) * const[0][...]).astype(bf16)

    return _matmul_fused("mm_t1_norm2", mixed[None], w_out[None], ((0, 0),), epilogue, [(x, _frows())], [g],
                         [_rowshape(T, f32), _rowshape(T, bf16)])


def _down_loss_fwd(f, w_down, x1, target):
    T = x1.shape[0]

    def epilogue(acc, extra, const, out):
        diff = extra[0][...] + acc - extra[1][...]
        dy = diff * (1.0 / D)
        out[0][...] = dy
        out[1][...] = dy.astype(bf16)
        _add_colsum(out[2], diff * diff)

    return _matmul_fused("mm_t2_loss", f[None], w_down[None], ((0, 0),), epilogue, [(x1, _frows()), (target, _frows())],
                         [], [_rowshape(T, f32), _rowshape(T, bf16), _sumshape()], sums=True)


def _up_norm2_bwd(du3, w_up_t, x1, dy, g, token):
    T = x1.shape[0]

    def epilogue(dh, extra, const, out):
        x1v = extra[0][...]
        rstd = _rms(x1v)
        xn = x1v * rstd
        dx1 = extra[1][...] + _rms_bwd(dh * const[0][...], xn, rstd)
        out[0][...] = dx1
        out[1][...] = dx1.astype(bf16)
        _add_colsum(out[2], dh * xn)

    return _matmul_fused("mm_dh2_norm2", du3, w_up_t.reshape(2, D_FF, D), ((0, 0), (1, 1)), epilogue,
                         [(x1, _frows()), (dy, _frows())], [g],
                         [_rowshape(T, f32), _rowshape(T, bf16), _sumshape()], sums=True, passed=[token])


def _out_gate_bwd(dx1b, w_out, z8, gate_b, ya, yb, dz8):
    T = ya.shape[0]

    def epilogue(dm, extra, const, out):
        b_ref = const[0]
        g_a = _sig(extra[0][...] + b_ref[:, :D])
        g_b = _sig(extra[1][...] + b_ref[:, D:])
        out[0][...] = (dm * g_a).astype(bf16)
        out[1][...] = (dm * g_b).astype(bf16)
        dla = dm * extra[2][...] * g_a * (1.0 - g_a)
        dlb = dm * extra[3][...] * g_b * (1.0 - g_b)
        out[2][0] = dla.astype(bf16)
        out[2][1] = dlb.astype(bf16)
        _add_colsum(out[3], dla, slice(0, D))
        _add_colsum(out[3], dlb, slice(D, 2 * D))

    return _matmul_fused(
        "mm_dmixed_gate", dx1b[None], w_out[None], ((0, 0),), epilogue,
        [(z8, _fsec(Z_GA)), (z8, _fsec(Z_GB)), (ya, _frows()), (yb, _frows())], [gate_b],
        [_rowshape(T, bf16), _rowshape(T, bf16),
         (jax.ShapeDtypeStruct(dz8.shape, bf16), pl.BlockSpec((2, FTM, D), lambda i: (1, i, 0))), _sumshape(2 * D)],
        nt=True, sums=True, passed=[dz8], aliases={0: 2})


def _convnorm_bwd(dya, w_conv_out, c, g):
    T = c.shape[0]

    def epilogue(ds, extra, const, out):
        cv = extra[0][...]
        rstd = _rms(cv)
        r0 = cv * rstd
        gv = const[0][...]
        r = r0 * gv
        sg = _sig(r)
        dr = ds * sg * (1.0 + r * (1.0 - sg))
        out[0][...] = _rms_bwd(dr * gv, r0, rstd)
        _add_colsum(out[1], dr * r0)

    return _matmul_fused("mm_ds_convnorm", dya[None], w_conv_out[None], ((0, 0),), epilogue, [(c, _frows())], [g],
                         [_rowshape(T, f32), _sumshape()], nt=True, sums=True)


def _in_norm1_bwd(dz8, w_in_t, x, dx1, g, token):
    T = x.shape[0]

    def epilogue(dh, extra, const, out):
        xv = extra[0][...]
        rstd = _rms(xv)
        xn = xv * rstd
        out[0][...] = extra[1][...] + _rms_bwd(dh * const[0][...], xn, rstd)
        _add_colsum(out[1], dh * xn)

    return _matmul_fused("mm_dh_norm1", dz8, w_in_t.reshape(7, D, D), tuple(zip(range(7), _W_OF_Z)), epilogue,
                         [(x, _frows()), (dx1, _frows())], [g], [_rowshape(T, f32), _sumshape()],
                         sums=True, passed=[token])


CCW = 256
CR = 64
HALO = 32


def _conv_fwd(z8, conv_w, conv_b, S):
    T = z8.shape[1]
    nb = T // S
    ncb = D // CCW

    def body(av_ref, ag_ref, w_ref, b_ref, c_ref, pad):
        pad[0:HALO, :] = jnp.zeros((HALO, CCW), f32)

        def fill(i, carry):
            r0 = pl.multiple_of(i * 256, 256)
            pad[pl.ds(HALO + r0, 256), :] = av_ref[pl.ds(r0, 256), :] * _sig(ag_ref[pl.ds(r0, 256), :])
            return carry

        lax.fori_loop(0, S // 256, fill, 0)
        bias = b_ref[...]

        def chunk(i, carry):
            r0 = pl.multiple_of(i * CR, CR)
            win = pad[pl.ds(r0, CR + HALO), :]
            acc = jnp.zeros((CR, CCW), f32) + bias
            for s in range(8):
                part = None
                for m in range((CONV_WIDTH - 1 - s) // 8 + 1):
                    j = CONV_WIDTH - 1 - 8 * m - s
                    term = win[24 - 8 * m:24 - 8 * m + CR + 8, :] * w_ref[j:j + 1, :]
                    part = term if part is None else part + term
                acc = acc + part[8 - s:8 - s + CR, :]
            c_ref[pl.ds(r0, CR), :] = acc
            return carry

        lax.fori_loop(0, S // CR, chunk, 0)

    zs = lambda s: pl.BlockSpec((None, S, CCW), lambda b, cb: (s, b, cb))
    return pl.pallas_call(
        body, name="conv_fwd", grid=(nb, ncb),
        in_specs=[zs(Z_AVAL), zs(Z_AGATE), pl.BlockSpec((CONV_WIDTH, CCW), lambda b, cb: (0, cb)),
                  pl.BlockSpec((1, CCW), lambda b, cb: (0, cb))],
        out_specs=pl.BlockSpec((S, CCW), lambda b, cb: (b, cb)),
        out_shape=jax.ShapeDtypeStruct((T, D), f32),
        scratch_shapes=[pltpu.VMEM((S + HALO, CCW), f32)],
        compiler_params=_cparams(("parallel", "parallel")))(z8, z8, conv_w, conv_b)


def _conv_bwd(dc, z8, conv_w, dz8, S):
    T = dc.shape[0]
    nb = T // S
    ncb = D // CCW

    def body(dc_ref, av_ref, ag_ref, w_ref, dz_in, dz_ref, dw_ref, apad, dpad, shbuf):
        del dz_in
        apad[0:HALO, :] = jnp.zeros((HALO, CCW), f32)
        dpad[S:S + HALO, :] = jnp.zeros((HALO, CCW), f32)
        dw_ref[...] = jnp.zeros_like(dw_ref)

        def fill(i, carry):
            r0 = pl.multiple_of(i * 256, 256)
            apad[pl.ds(HALO + r0, 256), :] = av_ref[pl.ds(r0, 256), :] * _sig(ag_ref[pl.ds(r0, 256), :])
            dpad[pl.ds(r0, 256), :] = dc_ref[pl.ds(r0, 256), :]
            return carry

        lax.fori_loop(0, S // 256, fill, 0)

        def chunk(i, carry):
            r0 = pl.multiple_of(i * CR, CR)
            dwin = dpad[pl.ds(r0, CR + HALO), :]
            da = jnp.zeros((CR, CCW), f32)
            for s in range(8):
                shbuf[...] = dwin[s:s + CR, :]
                dshift = shbuf[...]
                part = None
                for m in range((CONV_WIDTH - 1 - s) // 8 + 1):
                    j = CONV_WIDTH - 1 - 8 * m - s
                    term = dwin[8 * m:8 * m + CR + 8, :] * w_ref[j:j + 1, :]
                    part = term if part is None else part + term
                    a_lag = apad[pl.ds(r0 + HALO - 8 * m, CR), :]
                    dw_ref[8 * j:8 * j + 8, :] += _colsum8(dshift * a_lag)
                da = da + part[s:s + CR, :]
            dw_ref[8 * CONV_WIDTH:8 * CONV_WIDTH + 8, :] += _colsum8(dwin[0:CR, :])
            av = av_ref[pl.ds(r0, CR), :]
            sg = _sig(ag_ref[pl.ds(r0, CR), :])
            dz_ref[0, pl.ds(r0, CR), :] = (da * sg).astype(bf16)
            dz_ref[1, pl.ds(r0, CR), :] = (da * av * sg * (1.0 - sg)).astype(bf16)
            return carry

        lax.fori_loop(0, S // CR, chunk, 0)

    zs = lambda s: pl.BlockSpec((None, S, CCW), lambda b, cb: (s, b, cb))
    return pl.pallas_call(
        body, name="conv_bwd", grid=(nb, ncb),
        in_specs=[pl.BlockSpec((S, CCW), lambda b, cb: (b, cb)), zs(Z_AVAL), zs(Z_AGATE),
                  pl.BlockSpec((CONV_WIDTH, CCW), lambda b, cb: (0, cb)), pl.BlockSpec(memory_space=pl.ANY)],
        out_specs=[pl.BlockSpec((2, S, CCW), lambda b, cb: (0, b, cb)),
                   pl.BlockSpec((None, 256, CCW), lambda b, cb: (b, 0, cb))],
        out_shape=[jax.ShapeDtypeStruct(dz8.shape, bf16), jax.ShapeDtypeStruct((nb, 256, D), f32)],
        input_output_aliases={4: 0},
        scratch_shapes=[pltpu.VMEM((S + HALO, CCW), f32), pltpu.VMEM((S + HALO, CCW), f32),
                        pltpu.VMEM((CR, CCW), f32)],
        compiler_params=_cparams(("parallel", "parallel")))(dc, z8, z8, conv_w, dz8)


FR = 128
NFB = D_FF // CCW
FBW = 128


def _ffn_window(ref, i, r0):
    return ref[pl.ds(r0 - 8, FR + 8), :]


def _ffn_u(win, w_ref, b_ref):
    return (win[6:6 + FR, :] * w_ref[0:1, :] + win[7:7 + FR, :] * w_ref[1:2, :]
            + win[8:8 + FR, :] * w_ref[2:3, :] + b_ref[...])


def _ffn_fwd(u3, ffn_w, ffn_b, S):
    T = u3.shape[1]
    nb = T // S

    def body(uv_ref, ug_ref, wv_ref, wg_ref, bv_ref, bg_ref, f_ref):
        def chunk(first, i):
            r0 = 0 if first else pl.multiple_of(i * FR, FR)
            if first:
                z = jnp.zeros((8, CCW), f32)
                wv = jnp.concatenate([z, uv_ref[0:FR, :]], axis=0)
                wg = jnp.concatenate([z, ug_ref[0:FR, :]], axis=0)
            else:
                wv = _ffn_window(uv_ref, i, r0)
                wg = _ffn_window(ug_ref, i, r0)
            u_val = _ffn_u(wv, wv_ref, bv_ref)
            u_gate = _ffn_u(wg, wg_ref, bg_ref)
            f_ref[pl.ds(r0, FR), :] = (u_gate * _sig(u_gate) * u_val).astype(bf16)

        chunk(True, 0)

        def loop(i, carry):
            chunk(False, i)
            return carry

        lax.fori_loop(1, S // FR, loop, 0)

    us = lambda h: pl.BlockSpec((None, S, CCW), lambda b, cb: (h, b, cb))
    ws = lambda h: pl.BlockSpec((3, CCW), lambda b, cb: (0, h * NFB + cb))
    bs = lambda h: pl.BlockSpec((1, CCW), lambda b, cb: (0, h * NFB + cb))
    return pl.pallas_call(
        body, name="ffn_fwd", grid=(nb, NFB),
        in_specs=[us(0), us(1), ws(0), ws(1), bs(0), bs(1)],
        out_specs=pl.BlockSpec((S, CCW), lambda b, cb: (b, cb)),
        out_shape=jax.ShapeDtypeStruct((T, D_FF), bf16),
        compiler_params=_cparams(("parallel", "parallel")))(u3, u3, ffn_w, ffn_w, ffn_b, ffn_b)


def _ffn_bwd(u3, df, ffn_w, ffn_b, S):
    T = u3.shape[1]
    nb = T // S

    def body(uv_ref, ug_ref, df_ref, wv_ref, wg_ref, bv_ref, bg_ref, du_ref, dw_ref, dvpad, dgpad, shbuf):
        dvpad[S:S + 8, :] = jnp.zeros((8, FBW), f32)
        dgpad[S:S + 8, :] = jnp.zeros((8, FBW), f32)
        dw_ref[...] = jnp.zeros_like(dw_ref)

        def chunk(first, i):
            r0 = 0 if first else pl.multiple_of(i * FR, FR)
            if first:
                z = jnp.zeros((8, FBW), f32)
                wv = jnp.concatenate([z, uv_ref[0:FR, :]], axis=0)
                wg = jnp.concatenate([z, ug_ref[0:FR, :]], axis=0)
            else:
                wv = _ffn_window(uv_ref, i, r0)
                wg = _ffn_window(ug_ref, i, r0)
            taps = []
            for h, win in enumerate((wv, wg)):
                shbuf[2 * h] = win[6:6 + FR, :]
                shbuf[2 * h + 1] = win[7:7 + FR, :]
                taps.append((shbuf[2 * h], shbuf[2 * h + 1], win[8:8 + FR, :]))
            conv = lambda x, w_ref, b_ref: (x[0] * w_ref[0:1, :] + x[1] * w_ref[1:2, :] + x[2] * w_ref[2:3, :]
                                            + b_ref[...])
            u_val = conv(taps[0], wv_ref, bv_ref)
            u_gate = conv(taps[1], wg_ref, bg_ref)
            dfc = df_ref[pl.ds(r0, FR), :]
            sg = _sig(u_gate)
            d_val = dfc * u_gate * sg
            d_gate = dfc * u_val * sg * (1.0 + u_gate * (1.0 - sg))
            dvpad[pl.ds(r0, FR), :] = d_val
            dgpad[pl.ds(r0, FR), :] = d_gate
            for h, dd in enumerate((d_val, d_gate)):
                for j in range(3):
                    dw_ref[h, 8 * j:8 * j + 8, :] += _colsum8(dd * taps[h][j])
                dw_ref[h, 24:32, :] += _colsum8(dd)

        chunk(True, 0)

        def loop(i, carry):
            chunk(False, i)
            return carry

        lax.fori_loop(1, S // FR, loop, 0)

        def back(i, carry):
            r0 = pl.multiple_of(i * FR, FR)
            for h, (dpad, w_ref) in enumerate(((dvpad, wv_ref), (dgpad, wg_ref))):
                win = dpad[pl.ds(r0, FR + 8), :]
                du = (win[0:FR, :] * w_ref[2:3, :] + win[1:1 + FR, :] * w_ref[1:2, :]
                      + win[2:2 + FR, :] * w_ref[0:1, :])
                du_ref[h, pl.ds(r0, FR), :] = du.astype(bf16)
            return carry

        lax.fori_loop(0, S // FR, back, 0)

    ncb = D_FF // FBW
    us = lambda h: pl.BlockSpec((None, S, FBW), lambda b, cb: (h, b, cb))
    ws = lambda h: pl.BlockSpec((3, FBW), lambda b, cb: (0, h * ncb + cb))
    bs = lambda h: pl.BlockSpec((1, FBW), lambda b, cb: (0, h * ncb + cb))
    return pl.pallas_call(
        body, name="ffn_bwd", grid=(nb, ncb),
        in_specs=[us(0), us(1), pl.BlockSpec((S, FBW), lambda b, cb: (b, cb)), ws(0), ws(1), bs(0), bs(1)],
        out_specs=[pl.BlockSpec((2, S, FBW), lambda b, cb: (0, b, cb)),
                   pl.BlockSpec((None, 2, 32, FBW), lambda b, cb: (b, 0, 0, cb))],
        out_shape=[jax.ShapeDtypeStruct((2, T, D_FF), bf16), jax.ShapeDtypeStruct((nb, 2, 32, D_FF), f32)],
        scratch_shapes=[pltpu.VMEM((S + 8, FBW), f32), pltpu.VMEM((S + 8, FBW), f32),
                        pltpu.VMEM((4, FR, FBW), f32)],
        compiler_params=_cparams(("parallel", "parallel")))(u3, u3, df, ffn_w, ffn_w, ffn_b, ffn_b)


AB = ATTN_BLOCK


def _attn_bias_np():
    slopes = (np.float32(2.0) ** (np.float32(-8.0) * np.arange(1, N_HEADS + 1, dtype=np.float32)
                                  / np.float32(N_HEADS))).astype(np.float32)
    steps = (np.arange(AB)[:, None] + AB) - np.arange(2 * AB)[None, :]
    own = (np.arange(2 * AB) >= AB)[None, :]
    out = []
    for window, dil in GROUPS:
        valid = (steps >= 0) & (steps <= window // dil)
        dist = slopes[:, None, None] * (steps * dil).astype(np.float32)[None]
        kinds = [np.where(v[None], dist, np.float32(MASK_BIAS)) for v in (valid, valid & own)]
        out.append(np.stack(kinds, axis=1))
    return np.stack(out).astype(np.float32)


def _attn_bias():
    return jnp.asarray(_attn_bias_np())


def _head_masks():
    lane = lax.broadcasted_iota(jnp.int32, (1, 128), 1)
    return (lane < HEAD_DIM, lane >= HEAD_DIM)


def _perm_chunks(S, d):
    L = S // d
    ch = min(L, 256)
    out = []
    for r in range(d):
        for c in range(L // ch):
            start = r + d * ch * c
            out.append((pl.ds(start, ch, stride=d) if d > 1 else pl.ds(start, ch), r * L + c * ch, ch))
    return out


def _stack_heads(x, masks):
    return jnp.concatenate([jnp.where(masks[0], x, 0), jnp.where(masks[1], x, 0)], axis=0)


def _block_row(j):
    return j * AB if isinstance(j, int) else pl.multiple_of(j * AB, AB)


def _three_stages(n, stage_a, stage_b, stage_c, unroll):
    stage_a(0)
    stage_a(1)
    stage_b(0)

    def body(j, carry):
        stage_c(j - 1)
        stage_b(j)
        stage_a(j + 1)
        return carry

    lax.fori_loop(1, n - 1, body, 0, unroll=unroll)
    stage_c(n - 2)
    stage_b(n - 1)
    stage_c(n - 1)


_NT = (((1,), (1,)), ((), ()))
_TN = (((0,), (0,)), ((), ()))
SCH = 128


def _attn_fwd(qn, kn, z8, bias, S):
    T = qn.shape[0]
    nb = T // S
    nblk = S // AB

    def body(q_ref, k_ref, v_ref, bias_ref, o_ref, ob_ref, lse_ref, qs, ks, vs, s2, p2, ogp, lgp, *group_scratch):
        og, lg = group_scratch[:3], group_scratch[3:]
        masks = _head_masks()
        ks[0:AB, :] = jnp.zeros((AB, 128), bf16)
        vs[0:AB, :] = jnp.zeros((AB, 128), bf16)

        for g, (_, d) in enumerate(GROUPS):
            nsub = S // (d * AB)
            chunks = _perm_chunks(S, d)
            for src, dst, ch in chunks:
                qs[dst:dst + ch, :] = q_ref[src, :].astype(bf16)
                ks[AB + dst:AB + dst + ch, :] = k_ref[src, :].astype(bf16)
                vs[AB + dst:AB + dst + ch, :] = v_ref[src, :].astype(bf16)
            od, ld = (og[g], lg[g]) if d == 1 else (ogp, lgp)

            def scores(j):
                r0 = _block_row(j)
                q2 = _stack_heads(qs[pl.ds(r0, AB), :], masks)
                s2[j] = lax.dot_general(q2, ks[pl.ds(r0, 2 * AB), :], _NT, preferred_element_type=f32)

            def softmax(j, g=g, nsub=nsub, ld=ld):
                r0 = _block_row(j)
                kind = int(j % nsub == 0) if isinstance(j, int) else (j % nsub == 0).astype(jnp.int32)
                for cc in range(AB // SCH):
                    lses = []
                    for hh in range(2):
                        rows = pl.ds(hh * AB + cc * SCH, SCH)
                        sb = s2[j, rows, :] - bias_ref[g, hh, kind, cc * SCH:(cc + 1) * SCH, :]
                        m = jnp.max(sb, axis=-1, keepdims=True)
                        p = jnp.exp(sb - m)
                        den = jnp.sum(p, axis=-1, keepdims=True)
                        p2[j, rows, :] = (p * (1.0 / den)).astype(bf16)
                        lses.append(m + jnp.log(den))
                    ld[pl.ds(r0 + cc * SCH, SCH), :] = jnp.where(masks[0], lses[0], lses[1])

            def values(j, od=od):
                r0 = _block_row(j)
                pv2 = jnp.dot(p2[j], vs[pl.ds(r0, 2 * AB), :], preferred_element_type=f32)
                od[pl.ds(r0, AB), :] = jnp.where(masks[0], pv2[:AB], pv2[AB:])

            _three_stages(nblk, scores, softmax, values, nblk - 2)

            if d > 1:
                for src, dst, ch in chunks:
                    og[g][src, :] = ogp[dst:dst + ch, :]
                    lg[g][src, :] = lgp[dst:dst + ch, :]

        def combine(i, carry):
            rr = pl.ds(pl.multiple_of(i * 256, 256), 256)
            l0, l1, l2 = lg[0][rr, :], lg[1][rr, :], lg[2][rr, :]
            mx = jnp.maximum(jnp.maximum(l0, l1), l2)
            e0, e1, e2 = jnp.exp(l0 - mx), jnp.exp(l1 - mx), jnp.exp(l2 - mx)
            den = e0 + e1 + e2
            o = (e0 * og[0][rr, :] + e1 * og[1][rr, :] + e2 * og[2][rr, :]) / den
            o_ref[rr, :] = o
            ob_ref[rr, :] = o.astype(bf16)
            lse_ref[rr, :] = mx + jnp.log(den)
            return carry

        lax.fori_loop(0, S // 256, combine, 0, unroll=True)

    blk = pl.BlockSpec((S, 128), lambda b, hp: (b, hp))
    return pl.pallas_call(
        body, name="attn_fwd", grid=(nb, N_HEADS // 2),
        in_specs=[blk, blk, pl.BlockSpec((None, S, 128), lambda b, hp: (Z_V, b, hp)),
                  pl.BlockSpec((3, 2, 2, AB, 2 * AB), lambda b, hp: (0, hp, 0, 0, 0))],
        out_specs=[blk, blk, blk],
        out_shape=[jax.ShapeDtypeStruct((T, D), f32), jax.ShapeDtypeStruct((T, D), bf16),
                   jax.ShapeDtypeStruct((T, D), f32)],
        scratch_shapes=[pltpu.VMEM((S, 128), bf16), pltpu.VMEM((S + AB, 128), bf16), pltpu.VMEM((S + AB, 128), bf16),
                        pltpu.VMEM((nblk, 2 * AB, 2 * AB), f32), pltpu.VMEM((nblk, 2 * AB, 2 * AB), bf16),
                        pltpu.VMEM((S, 128), f32), pltpu.VMEM((S, 128), f32)] + [pltpu.VMEM((S, 128), f32)] * 6,
        compiler_params=_cparams(("parallel", "parallel")))(qn, kn, z8, bias)


def _attn_bwd(qn, kn, z8, do, o, lse, bias, bd, qg, kg, dz8, S):
    T = qn.shape[0]
    nb = T // S

    nblk = S // AB

    def body(q_ref, k_ref, v_ref, do_ref, o_ref, lse_ref, bias_ref, bd_ref, qraw_ref, kraw_ref, qg_ref, kg_ref,
             dz_in, dz_ref, dqg_ref, dkg_ref,
             dq_ref, dk_ref, dv_ref, delta, qs, ks, vs, dos, lsp, dlp, s2, dp2, p2, ds2, dqp, dkp, dvp):
        del dz_in
        masks = _head_masks()
        bdv = bd_ref[...]
        dq_ref[...] = jnp.zeros_like(dq_ref)
        dk_ref[...] = jnp.zeros_like(dk_ref)
        dv_ref[...] = jnp.zeros_like(dv_ref)
        ks[0:AB, :] = jnp.zeros((AB, 128), bf16)
        vs[0:AB, :] = jnp.zeros((AB, 128), bf16)

        def prep(i, carry):
            rr = pl.ds(pl.multiple_of(i * 256, 256), 256)
            delta[rr, :] = _head_sum(do_ref[rr, :] * o_ref[rr, :], bdv)
            return carry

        lax.fori_loop(0, S // 256, prep, 0, unroll=True)

        for g, (_, d) in enumerate(GROUPS):
            nsub = S // (d * AB)
            chunks = _perm_chunks(S, d)
            for src, dst, ch in chunks:
                qs[dst:dst + ch, :] = q_ref[src, :].astype(bf16)
                ks[AB + dst:AB + dst + ch, :] = k_ref[src, :].astype(bf16)
                vs[AB + dst:AB + dst + ch, :] = v_ref[src, :].astype(bf16)
                dos[dst:dst + ch, :] = do_ref[src, :].astype(bf16)
                lsp[dst:dst + ch, :] = lse_ref[src, :]
                dlp[dst:dst + ch, :] = delta[src, :]
            dkp[...] = jnp.zeros_like(dkp)
            dvp[...] = jnp.zeros_like(dvp)

            def scores(j):
                r0 = _block_row(j)
                q2 = _stack_heads(qs[pl.ds(r0, AB), :], masks)
                do2 = _stack_heads(dos[pl.ds(r0, AB), :], masks)
                s2[j] = lax.dot_general(q2, ks[pl.ds(r0, 2 * AB), :], _NT, preferred_element_type=f32)
                dp2[j] = lax.dot_general(do2, vs[pl.ds(r0, 2 * AB), :], _NT, preferred_element_type=f32)

            def probs(j, g=g, nsub=nsub):
                r0 = _block_row(j)
                kind = int(j % nsub == 0) if isinstance(j, int) else (j % nsub == 0).astype(jnp.int32)
                for cc in range(AB // SCH):
                    lse_c = lsp[pl.ds(r0 + cc * SCH, SCH), :]
                    del_c = dlp[pl.ds(r0 + cc * SCH, SCH), :]
                    for hh in range(2):
                        c0 = hh * HEAD_DIM
                        rows = pl.ds(hh * AB + cc * SCH, SCH)
                        sb = s2[j, rows, :] - bias_ref[g, hh, kind, cc * SCH:(cc + 1) * SCH, :]
                        p = jnp.exp(sb - lse_c[:, c0:c0 + 1])
                        p2[j, rows, :] = p.astype(bf16)
                        ds2[j, rows, :] = (p * (dp2[j, rows, :] - del_c[:, c0:c0 + 1])).astype(bf16)

            def grads(j):
                r0 = _block_row(j)
                q2 = _stack_heads(qs[pl.ds(r0, AB), :], masks)
                do2 = _stack_heads(dos[pl.ds(r0, AB), :], masks)
                dsb = ds2[j]
                t = jnp.dot(dsb, ks[pl.ds(r0, 2 * AB), :], preferred_element_type=f32)
                dqp[pl.ds(r0, AB), :] = jnp.where(masks[0], t[:AB], t[AB:])
                dkp[pl.ds(r0, 2 * AB), :] += lax.dot_general(dsb, q2, _TN, preferred_element_type=f32)
                dvp[pl.ds(r0, 2 * AB), :] += lax.dot_general(p2[j], do2, _TN, preferred_element_type=f32)

            _three_stages(nblk, scores, probs, grads, nblk - 2)

            for src, dst, ch in chunks:
                dq_ref[src, :] += dqp[dst:dst + ch, :]
                dk_ref[src, :] += dkp[AB + dst:AB + dst + ch, :]
                dv_ref[src, :] += dvp[AB + dst:AB + dst + ch, :]

        @pl.when(pl.program_id(1) == 0)
        def _():
            dqg_ref[...] = jnp.zeros_like(dqg_ref)
            dkg_ref[...] = jnp.zeros_like(dkg_ref)

        def norms(i, carry):
            rr = pl.ds(pl.multiple_of(i * 256, 256), 256)

            def one(raw, dn_scaled, g, dg_ref, sec):
                rstd = lax.rsqrt(_head_sum(raw * raw, bdv) * (1.0 / HEAD_DIM) + EPS)
                n = raw * rstd
                dg_ref[...] += _colsum8(dn_scaled * n)
                dn = dn_scaled * g
                draw = rstd * (dn - n * (_head_sum(dn * n, bdv) * (1.0 / HEAD_DIM)))
                dz_ref[sec, rr, :] = draw.astype(bf16)

            one(qraw_ref[rr, :], dq_ref[rr, :] * (HEAD_DIM ** -0.5), qg_ref[...], dqg_ref, 0)
            one(kraw_ref[rr, :], dk_ref[rr, :], kg_ref[...], dkg_ref, 1)
            dz_ref[2, rr, :] = dv_ref[rr, :].astype(bf16)
            dz_ref[3, rr, :] = jnp.zeros((256, 128), bf16)
            return carry

        lax.fori_loop(0, S // 256, norms, 0, unroll=True)

    blk = pl.BlockSpec((S, 128), lambda hp, b: (b, hp))
    sec = lambda s: pl.BlockSpec((None, S, 128), lambda hp, b: (s, b, hp))
    gain = pl.BlockSpec((1, 128), lambda hp, b: (0, hp))
    row = lambda dt, pad=0: pltpu.VMEM((S + pad, 128), dt)
    blocks = lambda dt: pltpu.VMEM((nblk, 2 * AB, 2 * AB), dt)
    return pl.pallas_call(
        body, name="attn_bwd", grid=(N_HEADS // 2, nb),
        in_specs=[blk, blk, sec(Z_V), blk, blk, blk,
                  pl.BlockSpec((3, 2, 2, AB, 2 * AB), lambda hp, b: (0, hp, 0, 0, 0)),
                  pl.BlockSpec((128, 128), lambda hp, b: (0, 0)), sec(Z_Q), sec(Z_K), gain, gain,
                  pl.BlockSpec(memory_space=pl.ANY)],
        out_specs=[pl.BlockSpec((4, S, 128), lambda hp, b: (1, b, hp)),
                   pl.BlockSpec((8, 128), lambda hp, b: (0, hp)), pl.BlockSpec((8, 128), lambda hp, b: (0, hp))],
        out_shape=[jax.ShapeDtypeStruct(dz8.shape, bf16), jax.ShapeDtypeStruct((8, D), f32),
                   jax.ShapeDtypeStruct((8, D), f32)],
        input_output_aliases={12: 0},
        scratch_shapes=[row(f32), row(f32), row(f32),
                        row(f32), row(bf16), row(bf16, AB), row(bf16, AB), row(bf16), row(f32), row(f32),
                        blocks(f32), blocks(f32), blocks(bf16), blocks(bf16), row(f32), row(f32, AB), row(f32, AB)],
        compiler_params=_cparams(("parallel", "arbitrary")))(qn, kn, z8, do, o, lse, bias, bd, z8, z8, qg, kg, dz8)


def _any_spec():
    return pl.BlockSpec(memory_space=pl.ANY)


def _allgather_rows(shards, n_full):
    n = len(shards)

    def body(*refs):
        ins, outs = refs[:n], refs[n:2 * n]
        send_sems, recv_sems, local_sems = refs[2 * n:]
        x, y, c, me = _my_pos()
        sibling = (x, y, 1 - c)
        chips = [(1 - x, y), (x, 1 - y), (1 - x, 1 - y)]

        def idx(px, py, pc):
            return 4 * px + 2 * py + pc

        def copy(a, k, blk, to, src=None):
            return pltpu.make_async_remote_copy(
                src_ref=outs[a].at[blk] if src is None else src, dst_ref=outs[a].at[blk],
                send_sem=send_sems.at[a, k], recv_sem=recv_sems.at[a, k], device_id=to, device_id_type=MESH)

        mine = [pltpu.make_async_copy(ins[a], outs[a].at[me], local_sems.at[a]) for a in range(n)]
        for cp in mine:
            cp.start()
        first = []
        for a in range(n_full):
            first.append(copy(a, 0, me, sibling, src=ins[a]))
            first += [copy(a, 1 + j, me, (*chip, c), src=ins[a]) for j, chip in enumerate(chips)]
        for cp in first:
            cp.start()
        passed = []
        for a in range(n_full):
            for j, chip in enumerate(chips):
                blk = idx(*chip, c)
                copy(a, 1 + j, blk, (x, y, c)).wait_recv()
                cp = copy(a, 4 + j, blk, sibling)
                cp.start()
                passed.append(cp)
        for a in range(n_full):
            copy(a, 0, idx(x, y, 1 - c), (x, y, c)).wait_recv()
            for j, chip in enumerate(chips):
                copy(a, 4 + j, idx(*chip, 1 - c), (x, y, c)).wait_recv()
        for cp in first + passed:
            cp.wait_send()
        for cp in mine:
            cp.wait()

    return pl.pallas_call(
        body, name="allgather_weights",
        in_specs=[_any_spec()] * n, out_specs=[_any_spec()] * n,
        out_shape=[jax.ShapeDtypeStruct((N_DEV,) + s.shape, s.dtype) for s in shards],
        scratch_shapes=[pltpu.SemaphoreType.DMA((n_full, 7)), pltpu.SemaphoreType.DMA((n_full, 7)),
                        pltpu.SemaphoreType.DMA((n,))],
    )(*shards)


def _peer(x, y, c, k):
    tx = 1 - x if (k >> 2) & 1 else x
    ty = 1 - y if (k >> 1) & 1 else y
    tc = 1 - c if k & 1 else c
    return (tx, ty, tc), 4 * tx + 2 * ty + tc


_PEER_ORDER = (2, 4, 6, 3, 5, 7, 1)


_HBM = pl.BlockSpec(memory_space=pltpu.HBM)
_SEM = pl.BlockSpec(memory_space=pltpu.SEMAPHORE)
_EFFECT = pltpu.SideEffectType.DATAFLOW_SIDE_EFFECTING


def _exchange_copies(srcs, lands, send_sems, recv_sems, gather):
    x, y, c, me = _my_pos()
    copies = []
    for k in _PEER_ORDER:
        tgt, tidx = _peer(x, y, c, k)
        for a in range(len(srcs)):
            copies.append(pltpu.make_async_remote_copy(
                src_ref=srcs[a] if gather else srcs[a].at[tidx], dst_ref=lands[a].at[me],
                send_sem=send_sems.at[7 * a + k - 1], recv_sem=recv_sems.at[7 * a + k - 1],
                device_id=tgt, device_id_type=MESH))
    return copies


def _exchange_start(name, srcs, lands=None, after=None):
    n = len(srcs)
    gather = lands is not None
    if lands is None:
        lands = [lax.empty(g.shape, g.dtype) for g in srcs]
    extra = [] if after is None else [after]

    def body(*refs):
        src_refs, land_refs = refs[:n], refs[n:2 * n]
        send_sems, recv_sems = refs[2 * n + len(extra)], refs[2 * n + len(extra) + 1]
        token = refs[-1]
        for cp in _exchange_copies(src_refs, land_refs, send_sems, recv_sems, gather):
            cp.start()
        token[...] = jnp.zeros_like(token)

    hbm = lambda a: pltpu.with_memory_space_constraint(a, pltpu.HBM)
    outs = pl.pallas_call(
        body, name=name,
        out_shape=(pltpu.SemaphoreType.DMA((7 * n,)), pltpu.SemaphoreType.DMA((7 * n,)),
                   *[pltpu.HBM(g.shape, g.dtype) for g in list(srcs) + list(lands)],
                   jax.ShapeDtypeStruct((8, 128), f32)),
        in_specs=[_HBM] * (2 * n) + [pl.BlockSpec(memory_space=pl.ANY)] * len(extra),
        out_specs=(_SEM, _SEM, *([_HBM] * (2 * n)), pl.BlockSpec(memory_space=pltpu.VMEM)),
        input_output_aliases={i: 2 + i for i in range(2 * n)},
        compiler_params=pltpu.CompilerParams(has_side_effects=_EFFECT),
    )(*[hbm(g) for g in srcs], *[hbm(g) for g in lands], *extra)
    return outs[0], outs[1], list(outs[2:2 + n]), list(outs[2 + n:2 + 2 * n]), outs[-1], gather


def _exchange_wait(name, started, after):
    send_sems, recv_sems, srcs, lands, _, gather = started
    n = len(srcs)
    after = list(after) if isinstance(after, (list, tuple)) else [after]

    def body(*refs):
        src_refs, land_refs = refs[:n], refs[n:2 * n]
        s_sems, r_sems = refs[2 * n], refs[2 * n + 1]
        for cp in _exchange_copies(src_refs, land_refs, s_sems, r_sems, gather):
            cp.wait_send()
            cp.wait_recv()

    outs = pl.pallas_call(
        body, name=name,
        out_shape=tuple(pltpu.HBM(a.shape, a.dtype) for a in list(srcs) + list(lands)),
        in_specs=[_HBM] * (2 * n) + [_SEM, _SEM] + [pl.BlockSpec(memory_space=pl.ANY)] * len(after),
        out_specs=tuple([_HBM] * (2 * n)),
        input_output_aliases={i: i for i in range(2 * n)},
        compiler_params=pltpu.CompilerParams(has_side_effects=_EFFECT),
    )(*srcs, *lands, send_sems, recv_sems, *after)
    return list(outs[:n]), list(outs[n:])


SMALL_ROWS = 128


def _small_start(name, sg, after=None):
    return _exchange_start(name, [sg], [lax.empty((N_DEV,) + sg.shape, f32)], after=after)


def _small_sum(name, me, started, after):
    (own,), (slots,) = _exchange_wait(name + "_wait", started, after)

    def body(me_ref, s_ref, own_ref, out_ref):
        acc = None
        for p in range(N_DEV):
            term = lax.cond(me_ref[0] == p, lambda: own_ref[...], lambda p=p: s_ref[p])
            acc = term if acc is None else acc + term
        out_ref[...] = acc

    return pl.pallas_call(
        body, name=name + "_sum",
        in_specs=[pl.BlockSpec(memory_space=pltpu.SMEM), pl.BlockSpec(memory_space=pltpu.VMEM),
                  pl.BlockSpec(memory_space=pltpu.VMEM)],
        out_specs=pl.BlockSpec(memory_space=pltpu.VMEM),
        out_shape=jax.ShapeDtypeStruct(own.shape, f32))(me, slots, own)


def _adam_math(g, w, m, v):
    m = ADAM_B1 * m + (1.0 - ADAM_B1) * g
    v = ADAM_B2 * v + (1.0 - ADAM_B2) * (g * g)
    m_hat = m / (1.0 - ADAM_B1 ** ADAM_STEP)
    v_hat = v / (1.0 - ADAM_B2 ** ADAM_STEP)
    delta = -ADAM_LR * (m_hat / (jnp.sqrt(v_hat) + ADAM_EPS) + ADAM_WD * w)
    return delta, m, v


def _adam_slots(name, me, slots, own, w, m, v, tr, transposed=False):
    rows = slots.shape[1]

    def body(me_ref, s_ref, own_ref, w_ref, m_ref, v_ref, g_ref, d_ref, nm_ref, nv_ref):
        mine = own_ref[...]
        g = None
        for p in range(N_DEV):
            term = lax.cond(me_ref[0] == p, lambda: mine, lambda p=p: s_ref[p]).astype(f32)
            g = term if g is None else g + term
        if transposed:
            g = g.T
        delta, nm, nv = _adam_math(g, w_ref[...], m_ref[...], v_ref[...])
        g_ref[...] = g
        d_ref[...] = delta
        nm_ref[...] = nm
        nv_ref[...] = nv

    mode = dict(pipeline_mode=pl.Buffered(1)) if rows == tr else {}
    if transposed:
        rs = pl.BlockSpec((D, tr), lambda i, me_ref: (0, i))
        rs_in = pl.BlockSpec((D, tr), lambda i, me_ref: (0, i), **mode)
    else:
        rs = pl.BlockSpec((tr, D), lambda i, me_ref: (i, 0))
        rs_in = pl.BlockSpec((tr, D), lambda i, me_ref: (i, 0), **mode)
    return pl.pallas_call(
        body, name=name,
        grid_spec=pltpu.PrefetchScalarGridSpec(
            num_scalar_prefetch=1, grid=(rows // tr,),
            in_specs=[pl.BlockSpec((N_DEV, tr, D), lambda i, me_ref: (0, i, 0), **mode),
                      pl.BlockSpec((None, tr, D), lambda i, me_ref: (me_ref[0], i, 0), **mode), rs_in, rs_in, rs_in],
            out_specs=[rs] * 4),
        out_shape=[jax.ShapeDtypeStruct(w.shape, f32)] * 4,
        compiler_params=_cparams(("parallel",)))(me, slots, own, w, m, v)


def _adam_small(me, g, w, m, v):
    o = _small_offsets()
    shapes = dict(norm1_g=(1, D), gate_b=(1, 2 * D), conv_w=(CONV_WIDTH, D // N_DEV), conv_b=(1, D),
                  conv_norm_g=(1, D), q_norm_g=(1, HEAD_DIM), k_norm_g=(1, HEAD_DIM), norm2_g=(1, D))
    names = tuple(shapes)

    def body(me_ref, g_ref, w_ref, m_ref, v_ref, d_ref, nm_ref, nv_ref, *parts):
        delta, nm, nv = _adam_math(g_ref[...], w_ref[...], m_ref[...], v_ref[...])
        d_ref[...] = delta
        nm_ref[...] = nm
        nv_ref[...] = nv
        mine = pl.ds(pl.multiple_of(me_ref[0] * (D // N_DEV), D // N_DEV), D // N_DEV)
        for i, src in enumerate((g_ref, d_ref, nm_ref, nv_ref)):
            for k, name in enumerate(names):
                dst, r = parts[i * len(names) + k], o[name]
                if name == "gate_b":
                    dst[:, 0:D] = src[r:r + 1, :]
                    dst[:, D:2 * D] = src[r + 1:r + 2, :]
                elif name == "conv_w":
                    dst[...] = src[r:r + CONV_WIDTH, mine]
                else:
                    dst[...] = src[r:r + 1, 0:shapes[name][1]]

    vmem = pl.BlockSpec(memory_space=pltpu.VMEM)
    outs = pl.pallas_call(
        body, name="adam_small", in_specs=[pl.BlockSpec(memory_space=pltpu.SMEM)] + [vmem] * 4,
        out_shape=[jax.ShapeDtypeStruct(g.shape, f32)] * 3
        + [jax.ShapeDtypeStruct(shapes[n], f32) for _ in range(4) for n in names])(me, g, w, m, v)
    direct = {n: [outs[3 + i * len(names) + k] for i in range(4)] for k, n in enumerate(names)}
    return outs[0], outs[1], outs[2], direct


FFN_PAD = 6 * D


_SMALL_PARTS = (("norm1_g", 1), ("gate_b", 2), ("conv_w", CONV_WIDTH), ("conv_b", 1), ("conv_norm_g", 1),
                ("q_norm_g", 1), ("k_norm_g", 1), ("norm2_g", 1), ("ffn_conv_w", 18), ("ffn_conv_b", 6), ("last", 1))


def _small_offsets():
    out, row = {}, 0
    for name, rows in _SMALL_PARTS:
        out[name] = row
        row += -(-rows // 8) * 8
    assert row == SMALL_ROWS
    return out


def _pack_small(norm1_g, gate_b, conv_w, conv_b, conv_norm_g, q_norm_g, k_norm_g, norm2_g, ffn_conv_w, ffn_conv_b,
                last_row=None):
    pad_h = lambda a: jnp.pad(a, ((0, 0), (0, D - HEAD_DIM)))
    pad_f = lambda a: jnp.pad(a, ((0, 0), (0, FFN_PAD - 2 * D_FF))).reshape(-1, D)
    parts = [norm1_g, gate_b.reshape(2, D), conv_w, conv_b, conv_norm_g, pad_h(q_norm_g), pad_h(k_norm_g), norm2_g,
             pad_f(ffn_conv_w), pad_f(ffn_conv_b), jnp.zeros((1, D), f32) if last_row is None else last_row]
    return jnp.concatenate([jnp.pad(p, ((0, -p.shape[0] % 8), (0, 0))) for p in parts], axis=0)


def _unpack_small(p):
    o = _small_offsets()
    rows = lambda name, n: p[o[name]:o[name] + n]
    ffn = lambda a: a.reshape(-1, FFN_PAD)[:, :2 * D_FF]
    return dict(
        norm1_g=rows("norm1_g", 1), gate_b=rows("gate_b", 2).reshape(1, 2 * D), conv_w=rows("conv_w", CONV_WIDTH),
        conv_b=rows("conv_b", 1), conv_norm_g=rows("conv_norm_g", 1), q_norm_g=rows("q_norm_g", 1)[:, :HEAD_DIM],
        k_norm_g=rows("k_norm_g", 1)[:, :HEAD_DIM], norm2_g=rows("norm2_g", 1),
        ffn_conv_w=ffn(rows("ffn_conv_w", 18)), ffn_conv_b=ffn(rows("ffn_conv_b", 6)))


_ADAM_TILE = {896: 128, 704: 704, 128: 128, 352: 176}


def kernel(x, norm1_g, w_in, gate_b, conv_w, conv_b, conv_norm_g, w_conv_out, q_norm_g, k_norm_g, w_attn_out, w_out, norm2_g, w_up, ffn_conv_w, ffn_conv_b, w_down, loss_target, m_norm1_g, m_w_in, m_gate_b, m_conv_w, m_conv_b, m_conv_norm_g, m_w_conv_out, m_q_norm_g, m_k_norm_g, m_w_attn_out, m_w_out, m_norm2_g, m_w_up, m_ffn_conv_w, m_ffn_conv_b, m_w_down, v_norm1_g, v_w_in, v_gate_b, v_conv_w, v_conv_b, v_conv_norm_g, v_w_conv_out, v_q_norm_g, v_k_norm_g, v_w_attn_out, v_w_out, v_norm2_g, v_w_up, v_ffn_conv_w, v_ffn_conv_b, v_w_down):
    BL, S, _ = x.shape
    T = BL * S
    me = 4 * lax.axis_index("x") + 2 * lax.axis_index("y") + lax.axis_index("c")
    xt = x.reshape(T, D)
    target = loss_target.reshape(T, D)

    big = dict(w_in=(w_in[0], m_w_in[0], v_w_in[0]), w_up=(w_up[0], m_w_up[0], v_w_up[0]),
               w_conv_out=(w_conv_out[0], m_w_conv_out[0], v_w_conv_out[0]),
               w_attn_out=(w_attn_out[0], m_w_attn_out[0], v_w_attn_out[0]),
               w_out=(w_out[0], m_w_out[0], v_w_out[0]), w_down=(w_down[0], m_w_down[0], v_w_down[0]))
    order = ["w_in", "w_conv_out", "w_attn_out", "w_out", "w_up", "w_down"]
    shards = [(big[n][0].T if n in ("w_in", "w_up") else big[n][0]).astype(bf16) for n in order]
    gathered = _allgather_rows(shards, 1)
    W = {"w_in": gathered[0].reshape(-1, D)}

    def place_cols(shard, full_cols):
        z = jnp.zeros((shard.shape[0], full_cols), f32)
        return lax.dynamic_update_slice(z, shard, (0, me * shard.shape[1]))

    zr = lambda a: jnp.zeros_like(a)
    conv_local = _pack_small(
        zr(norm1_g), zr(gate_b), place_cols(conv_w[0], D), zr(conv_b), zr(conv_norm_g), zr(q_norm_g), zr(k_norm_g),
        zr(norm2_g), place_cols(ffn_conv_w[0], 2 * D_FF), zr(ffn_conv_b))
    ga_conv = _small_start("gather_conv_start", conv_local, after=gathered[0])
    ga_proj = _exchange_start("gather_start_proj", shards[1:4], gathered[1:4], after=ga_conv[4])
    ga_ffn = _exchange_start("gather_start_ffn", shards[4:6], gathered[4:6], after=ga_proj[4])

    bd = (jnp.arange(128)[:, None] // HEAD_DIM == jnp.arange(128)[None, :] // HEAD_DIM).astype(bf16)
    bias = _attn_bias()
    qg = jnp.tile(q_norm_g, (1, N_HEADS))
    kg = jnp.tile(k_norm_g, (1, N_HEADS))

    z8, h, qn, kn = _in_proj_fwd(xt, norm1_g, W["w_in"], qg, kg, bd, ga_ffn[4])
    conv_all = _unpack_small(_small_sum("gather_conv", me.reshape(1), ga_conv, z8))
    conv_w_full, ffn_w_full = conv_all["conv_w"], conv_all["ffn_conv_w"]
    c = _conv_fwd(z8, conv_w_full, conv_b, S)
    o, ob, lse = _attn_fwd(qn, kn, z8, bias, S)
    for n, g in zip(order[1:4], _exchange_wait("gather_wait_proj", ga_proj, ob)[1]):
        W[n] = g.reshape(-1, D)
    s, ya, yb, mixed = _branches_fwd(c, ob, z8, conv_norm_g, gate_b, W["w_conv_out"], W["w_attn_out"])
    x1, h2 = _out_norm2_fwd(mixed, W["w_out"], xt, norm2_g)
    for n, g in zip(order[4:6], _exchange_wait("gather_wait_ffn", ga_ffn, x1)[1]):
        W[n] = g.reshape(-1, D)
    TNU = D_FF // 2
    u3 = _matmul_call(
        "mm_u", h2, W["w_up"],
        pl.BlockSpec((1024, D), lambda i, j, k: (i, 0)),
        pl.BlockSpec((TNU, D), lambda i, j, k: (j, 0)),
        pl.BlockSpec((None, 1024, TNU), lambda i, j, k: (j // 2, i, j % 2)),
        jax.ShapeDtypeStruct((2, T, D_FF), f32), (T // 1024, 4, 1), "nt", 1, 1024, TNU)
    f = _ffn_fwd(u3, ffn_w_full, ffn_conv_b, S)
    dy, dyb, lacc = _down_loss_fwd(f, W["w_down"], x1, target)
    loss_local = 0.5 / D * jnp.sum(lacc)

    df = _matmul("mm_df", dyb, W["w_down"], "nt", f32, tn=TNU)
    g_w_down = _matmul("mm_dwdn", f, dyb, "tn", bf16, tm=TNU)
    du3, dffn = _ffn_bwd(u3, df, ffn_w_full, ffn_conv_b, S)
    g_w_up = _matmul_call(
        "mm_dwup", du3, h2,
        pl.BlockSpec((None, T, TNU), lambda i, j, k: (i // 2, 0, i % 2)),
        pl.BlockSpec((T, D), lambda i, j, k: (0, 0)),
        pl.BlockSpec((TNU, D), lambda i, j, k: (i, 0)),
        jax.ShapeDtypeStruct((2 * D_FF, D), bf16), (4, 1, 1), "tn", 1, TNU, D)
    blocks8 = lambda a: a.reshape(N_DEV, -1, D)
    ex_ffn = _exchange_start("scatter_start_ffn", [blocks8(g_w_up), blocks8(g_w_down)])
    dx1, dx1b, dg_norm2 = _up_norm2_bwd(du3, W["w_up"], x1, dy, norm2_g, ex_ffn[4])
    g_w_out = _matmul("mm_dwo", mixed, dx1b, "tn", bf16, tm=512)
    dz8 = lax.empty((8, T, D), bf16)
    dya, dyb2, dz8, dg_gate = _out_gate_bwd(dx1b, W["w_out"], z8, gate_b, ya, yb, dz8)
    g_w_conv_out = _matmul("mm_dwco", s, dya, "tn", bf16, tm=512)
    g_w_attn_out = _matmul("mm_dwao", ob, dyb2, "tn", bf16, tm=512)
    ex_proj = _exchange_start("scatter_start_proj", [blocks8(g_w_conv_out), blocks8(g_w_attn_out), blocks8(g_w_out)])
    do = _matmul("mm_do", dyb2, W["w_attn_out"], "nt", f32, after=ex_proj[4])
    dc, dg_convnorm = _convnorm_bwd(dya, W["w_conv_out"], c, conv_norm_g)
    dz8a, dconv = _conv_bwd(dc, z8, conv_w_full, dz8, S)
    dz8b, dg_q, dg_k = _attn_bwd(qn, kn, z8, do, o, lse, bias, bd, qg, kg, dz8a, S)
    g_w_in = _matmul_call(
        "mm_dwin", dz8b, h,
        pl.BlockSpec((None, T, D), lambda i, j, k: (jnp.where(i < 2, i, jnp.where(i < 5, i + 2, i - 3)), 0, 0)),
        pl.BlockSpec((T, D), lambda i, j, k: (0, 0)), pl.BlockSpec((1024, D), lambda i, j, k: (i, 0)),
        jax.ShapeDtypeStruct((7 * D, D), bf16), (7, 1, 1), "tn", 1, D, D)
    ex_in = _exchange_start("scatter_start_in", [blocks8(g_w_in)])
    grad_x, dg_norm1 = _in_norm1_bwd(dz8b, W["w_in"], xt, dx1, norm1_g, ex_in[4])

    sum8 = lambda a: a.reshape(-1, 8, a.shape[-1]).sum(axis=1)
    dconv_s = sum8(dconv.sum(axis=0))
    dffn_s = dffn.sum(axis=0).reshape(2, 4, 8, D_FF).sum(axis=2)
    dffn_w = jnp.concatenate([dffn_s[0, :3], dffn_s[1, :3]], axis=1)
    dffn_b = jnp.concatenate([dffn_s[0, 3:4], dffn_s[1, 3:4]], axis=1)
    fold = lambda a: sum8(a).reshape(N_HEADS, HEAD_DIM).sum(axis=0)[None]
    small_g_local = _pack_small(
        sum8(dg_norm1), sum8(dg_gate), dconv_s[:CONV_WIDTH], dconv_s[CONV_WIDTH:], sum8(dg_convnorm),
        fold(dg_q), fold(dg_k), sum8(dg_norm2), dffn_w, dffn_b,
        last_row=jnp.pad(loss_local.reshape(1, 1), ((0, 0), (0, D - 1))))
    sg_start = _small_start("small_grads_start", small_g_local)

    place_m = lambda a, full: place_cols(a[0], full)
    small_w_true = _pack_small(norm1_g, gate_b, conv_w_full, conv_b, conv_norm_g, q_norm_g, k_norm_g, norm2_g,
                               ffn_w_full, ffn_conv_b)
    small_m = _pack_small(m_norm1_g, m_gate_b, place_m(m_conv_w, D), m_conv_b, m_conv_norm_g, m_q_norm_g, m_k_norm_g,
                          m_norm2_g, place_m(m_ffn_conv_w, 2 * D_FF), m_ffn_conv_b)
    small_v = _pack_small(v_norm1_g, v_gate_b, place_m(v_conv_w, D), v_conv_b, v_conv_norm_g, v_q_norm_g, v_k_norm_g,
                          v_norm2_g, place_m(v_ffn_conv_w, 2 * D_FF), v_ffn_conv_b)

    own, slots = {}, {}
    for tag, ex, names_ in (("ffn", ex_ffn, ("w_up", "w_down")),
                            ("proj", ex_proj, ("w_conv_out", "w_attn_out", "w_out")), ("in", ex_in, ("w_in",))):
        sent, landed = _exchange_wait("scatter_wait_" + tag, ex, [sg_start[4], small_w_true, small_m, small_v])
        for n, src, land in zip(names_, sent, landed):
            own[n], slots[n] = src, land

    res, adam_done = {}, []
    for n in order:
        w, m, v = big[n]
        outs = _adam_slots("adam_" + n, me.reshape(1), slots[n], own[n], w, m, v, _ADAM_TILE[slots[n].shape[1]],
                           transposed=n in ("w_in", "w_up"))
        adam_done.append(outs[0])
        res[n] = [a[None] for a in outs]
    small_g = _small_sum("small_grads", me.reshape(1), sg_start, adam_done)
    loss = small_g[_small_offsets()["last"], 0]

    col = lambda a, width: lax.dynamic_slice(a, (0, me * width), (a.shape[0], width))
    sd, sm, sv, direct = _adam_small(me.reshape(1), small_g, small_w_true, small_m, small_v)
    for n, four in direct.items():
        res[n] = [a[None] for a in four] if n == "conv_w" else four
    for i, packed in enumerate((small_g, sd, sm, sv)):
        u = _unpack_small(packed)
        res.setdefault("ffn_conv_w", [None] * 4)[i] = col(u["ffn_conv_w"], 2 * D_FF // N_DEV)[None]
        res.setdefault("ffn_conv_b", [None] * 4)[i] = u["ffn_conv_b"]

    names = ["norm1_g", "w_in", "gate_b", "conv_w", "conv_b", "conv_norm_g", "w_conv_out", "q_norm_g", "k_norm_g",
             "w_attn_out", "w_out", "norm2_g", "w_up", "ffn_conv_w", "ffn_conv_b", "w_down"]
    out = [loss, grad_x.reshape(BL, S, D)]
    for i in range(4):
        out += [res[n][i] for n in names]
    return tuple(out)
```

```python
import functools

import jax
import jax.numpy as jnp
import numpy as np
from jax import lax
from jax.experimental import pallas as pl
from jax.experimental.pallas import tpu as pltpu

f32 = jnp.float32
bf16 = jnp.bfloat16

D = 1024
N_HEADS = 16
HEAD_DIM = 64
CONV_WIDTH = 31
D_FF = 2816
GROUPS = ((128, 1), (512, 4), (2048, 16))
ATTN_BLOCK = 128
EPS = 1e-6
N_DEV = 8
MESH = pl.DeviceIdType.MESH

ADAM_LR = 0.001
ADAM_B1 = 0.9
ADAM_B2 = 0.999
ADAM_EPS = 1e-08
ADAM_WD = 0.01
ADAM_STEP = 10

VMEM_LIMIT = 56 * 1024 * 1024
MASK_BIAS = 1e30

Z_AVAL, Z_AGATE, Z_GA, Z_GB, Z_Q, Z_K, Z_V = 0, 1, 2, 3, 4, 5, 6


_W_OF_Z = (0, 1, 5, 6, 2, 3, 4)


def _wsec_of_zsec(j):
    return jnp.where(j < 2, j, jnp.where(j < 4, j + 3, j - 2))


def _sig(x):
    return 1.0 / (1.0 + jnp.exp(-x))


def _colsum8(x):
    return x.reshape(-1, 8, x.shape[-1]).sum(axis=0)


def _cparams(sem):
    return pltpu.CompilerParams(dimension_semantics=sem, vmem_limit_bytes=VMEM_LIMIT)


def _my_pos():
    x, y, c = lax.axis_index("x"), lax.axis_index("y"), lax.axis_index("c")
    return x, y, c, 4 * x + 2 * y + c


_DIMS = {"nn": ((1,), (0,)), "nt": ((1,), (1,)), "tn": ((0,), (0,))}


def _matmul_call(name, a, b, a_spec, b_spec, o_spec, out_shape, grid, mode, nk, tm, tn, after=None):
    dims = (_DIMS[mode], ((), ()))
    extra = [] if after is None else [after]

    def body(a_ref, b_ref, *rest):
        o_ref, scratch = rest[len(extra)], rest[len(extra) + 1:]
        part = lax.dot_general(a_ref[...], b_ref[...], dims, preferred_element_type=f32)
        if nk == 1:
            o_ref[...] = part.astype(o_ref.dtype)
        else:
            acc = scratch[0]
            k = pl.program_id(2)

            @pl.when(k == 0)
            def _():
                acc[...] = part

            @pl.when(k > 0)
            def _():
                acc[...] += part

            @pl.when(k == nk - 1)
            def _():
                o_ref[...] = acc[...].astype(o_ref.dtype)

    scratch = [] if nk == 1 else [pltpu.VMEM((tm, tn), f32)]
    return pl.pallas_call(
        body, name=name, grid=grid, in_specs=[a_spec, b_spec] + [pl.BlockSpec(memory_space=pl.ANY)] * len(extra),
        out_specs=o_spec, out_shape=out_shape,
        scratch_shapes=scratch, compiler_params=_cparams(("parallel", "parallel", "arbitrary")),
    )(a, b, *extra)


def _matmul(name, a, b, mode, out_dtype, tm=1024, tn=1024, tk=None, after=None):
    if mode == "nn":
        (M, K), (_, N) = a.shape, b.shape
    elif mode == "nt":
        (M, K), (N, _) = a.shape, b.shape
    else:
        (K, M), (_, N) = a.shape, b.shape
    tm, tn = min(tm, M), min(tn, N)
    tk = K if tk is None else tk
    nk = K // tk
    assert M % tm == 0 and N % tn == 0 and K % tk == 0
    if mode == "tn":
        a_spec = pl.BlockSpec((tk, tm), lambda i, j, k: (k, i))
    else:
        a_spec = pl.BlockSpec((tm, tk), lambda i, j, k: (i, k))
    if mode == "nt":
        b_spec = pl.BlockSpec((tn, tk), lambda i, j, k: (j, k))
    else:
        b_spec = pl.BlockSpec((tk, tn), lambda i, j, k: (k, j))
    o_spec = pl.BlockSpec((tm, tn), lambda i, j, k: (i, j))
    return _matmul_call(name, a, b, a_spec, b_spec, o_spec, jax.ShapeDtypeStruct((M, N), out_dtype),
                        (M // tm, N // tn, nk), mode, nk, tm, tn, after=after)


FTM = 512


def _matmul_fused(name, a, b, pairs, epilogue, extras, consts, outs, nt=False, sums=False, passed=(), aliases=None):
    sa, M, kk = a.shape
    na = max(i for i, _ in pairs) + 1
    ne, nc, npass = len(extras), len(consts), len(passed)
    dims = (_DIMS["nt" if nt else "nn"], ((), ()))

    def body(a_ref, b_ref, *rest):
        acc = None
        for i, j in pairs:
            part = lax.dot_general(a_ref[i], b_ref[j], dims, preferred_element_type=f32)
            acc = part if acc is None else acc + part
        epilogue(acc, rest[:ne], rest[ne:ne + nc], rest[ne + nc + npass:])

    whole = lambda arr: pl.BlockSpec(arr.shape, lambda i, nd=arr.ndim: (0,) * nd, pipeline_mode=pl.Buffered(1))
    io_alias = {2 + ne + nc + k: v for k, v in (aliases or {}).items()}
    return pl.pallas_call(
        body, name=name, grid=(M // FTM,),
        in_specs=[pl.BlockSpec((na, FTM, kk), lambda i: (0, i, 0)), whole(b)] + [s for _, s in extras]
        + [whole(c) for c in consts] + [pl.BlockSpec(memory_space=pl.ANY)] * npass,
        out_specs=[s for _, s in outs], out_shape=[s for s, _ in outs], input_output_aliases=io_alias,
        compiler_params=_cparams(("arbitrary" if sums else "parallel",)),
    )(a, b, *[x for x, _ in extras], *consts, *passed)


def _frows(c=D):
    return pl.BlockSpec((FTM, c), lambda i: (i, 0))


def _fsec(s):
    return pl.BlockSpec((None, FTM, D), lambda i: (s, i, 0))


def _rowshape(T, dtype, c=D):
    return (jax.ShapeDtypeStruct((T, c), dtype), _frows(c))


def _sumshape(c=D):
    return (jax.ShapeDtypeStruct((8, c), f32), pl.BlockSpec((8, c), lambda i: (0, 0)))


def _add_colsum(ref, x, cols=None):
    @pl.when(pl.program_id(0) == 0)
    def _():
        if cols is None:
            ref[...] = jnp.zeros_like(ref)
        else:
            ref[:, cols] = jnp.zeros((8, x.shape[-1]), f32)

    if cols is None:
        ref[...] += _colsum8(x)
    else:
        ref[:, cols] += _colsum8(x)


def _rms(x):
    return lax.rsqrt(jnp.mean(x * x, axis=-1, keepdims=True) + EPS)


def _rms_bwd(dy_g, xn, rstd):
    return rstd * (dy_g - xn * jnp.mean(dy_g * xn, axis=-1, keepdims=True))


def _head_sum(x, bd):
    parts = []
    for cb in range(x.shape[-1] // 128):
        xb = x[:, cb * 128:(cb + 1) * 128]
        hi = xb.astype(bf16)
        lo = (xb - hi.astype(f32)).astype(bf16)
        parts.append(jnp.dot(hi, bd, preferred_element_type=f32) + jnp.dot(lo, bd, preferred_element_type=f32))
    return parts[0] if len(parts) == 1 else jnp.concatenate(parts, axis=1)


ZTM = 1024


def _in_proj_fwd(x, g, w_in_t, qg, kg, bd, after):
    T = x.shape[0]
    nt = T // ZTM

    def body(x_ref, g_ref, w_ref, qg_ref, kg_ref, bd_ref, after_ref, z_ref, h_ref, qn_ref, kn_ref, hbuf):
        del after_ref
        j, i = pl.program_id(0), pl.program_id(1)
        rows = pl.ds(pl.multiple_of(i * ZTM, ZTM), ZTM)

        @pl.when(j == 0)
        def _():
            xv = x_ref[...]
            hv = (xv * _rms(xv) * g_ref[...]).astype(bf16)
            hbuf[rows, :] = hv
            h_ref[...] = hv

        def project():
            z = lax.dot_general(hbuf[rows, :], w_ref[...], (_DIMS["nt"], ((), ())), preferred_element_type=f32)
            z_ref[...] = z
            return z

        def head_norm(z, gain_ref, scale):
            return z * lax.rsqrt(_head_sum(z * z, bd_ref[...]) * (1.0 / HEAD_DIM) + EPS) * gain_ref[...] * scale

        @pl.when(j == Z_Q)
        def _():
            qn_ref[...] = head_norm(project(), qg_ref, HEAD_DIM ** -0.5)

        @pl.when(j == Z_K)
        def _():
            kn_ref[...] = head_norm(project(), kg_ref, 1.0)

        @pl.when((j != Z_Q) & (j != Z_K))
        def _():
            project()

    def tile_at(sec):
        return pl.BlockSpec((ZTM, D), lambda j, i: (jnp.where(j < sec, 0, jnp.where(j == sec, i, nt - 1)), 0))

    row = pl.BlockSpec((1, D), lambda j, i: (0, 0))
    return pl.pallas_call(
        body, name="mm_z", grid=(7, nt),
        in_specs=[tile_at(0), row, pl.BlockSpec((D, D), lambda j, i: (_wsec_of_zsec(j), 0)), row, row,
                  pl.BlockSpec((128, 128), lambda j, i: (0, 0)), pl.BlockSpec(memory_space=pl.ANY)],
        out_specs=[pl.BlockSpec((None, ZTM, D), lambda j, i: (j, i, 0)), tile_at(0), tile_at(Z_Q), tile_at(Z_K)],
        out_shape=[jax.ShapeDtypeStruct((8, T, D), f32), jax.ShapeDtypeStruct((T, D), bf16),
                   jax.ShapeDtypeStruct((T, D), f32), jax.ShapeDtypeStruct((T, D), f32)],
        scratch_shapes=[pltpu.VMEM((T, D), bf16)],
        compiler_params=_cparams(("arbitrary", "arbitrary")))(x, g, w_in_t, qg, kg, bd, after)


def _branches_fwd(c, ob, z8, g, gate_b, w_conv_out, w_attn_out):
    T = c.shape[0]

    def epilogue(yb, extra, const, out):
        cv = extra[0][...]
        r = cv * _rms(cv) * const[0][...]
        s = (r * _sig(r)).astype(bf16)
        ya = jnp.dot(s, const[2][...], preferred_element_type=f32)
        b_ref = const[1]
        g_a = _sig(extra[1][...] + b_ref[:, :D])
        g_b = _sig(extra[2][...] + b_ref[:, D:])
        out[0][...] = s
        out[1][...] = ya
        out[2][...] = yb
        out[3][...] = (g_a * ya + g_b * yb).astype(bf16)

    return _matmul_fused("mm_branches", ob[None], w_attn_out[None], ((0, 0),), epilogue,
                         [(c, _frows()), (z8, _fsec(Z_GA)), (z8, _fsec(Z_GB))], [g, gate_b, w_conv_out],
                         [_rowshape(T, bf16), _rowshape(T, f32), _rowshape(T, f32), _rowshape(T, bf16)])


def _out_norm2_fwd(mixed, w_out, x, g):
    T = x.shape[0]

    def epilogue(acc, extra, const, out):
        x1 = extra[0][...] + acc
        out[0][...] = x1
        out[1][...] = (x1 * _rms(x1) * const[0][...]).astype(bf16)

    return _matmul_fused("mm_t1_norm2", mixed[None], w_out[None], ((0, 0),), epilogue, [(x, _frows())], [g],
                         [_rowshape(T, f32), _rowshape(T, bf16)])


def _down_loss_fwd(f, w_down, x1, target):
    T = x1.shape[0]

    def epilogue(acc, extra, const, out):
        diff = extra[0][...] + acc - extra[1][...]
        dy = diff * (1.0 / D)
        out[0][...] = dy
        out[1][...] = dy.astype(bf16)
        _add_colsum(out[2], diff * diff)

    return _matmul_fused("mm_t2_loss", f[None], w_down[None], ((0, 0),), epilogue, [(x1, _frows()), (target, _frows())],
                         [], [_rowshape(T, f32), _rowshape(T, bf16), _sumshape()], sums=True)


def _up_norm2_bwd(du3, w_up_t, x1, dy, g, token):
    T = x1.shape[0]

    def epilogue(dh, extra, const, out):
        x1v = extra[0][...]
        rstd = _rms(x1v)
        xn = x1v * rstd
        dx1 = extra[1][...] + _rms_bwd(dh * const[0][...], xn, rstd)
        out[0][...] = dx1
        out[1][...] = dx1.astype(bf16)
        _add_colsum(out[2], dh * xn)

    return _matmul_fused("mm_dh2_norm2", du3, w_up_t.reshape(2, D_FF, D), ((0, 0), (1, 1)), epilogue,
                         [(x1, _frows()), (dy, _frows())], [g],
                         [_rowshape(T, f32), _rowshape(T, bf16), _sumshape()], sums=True, passed=[token])


def _out_gate_bwd(dx1b, w_out, z8, gate_b, ya, yb, dz8):
    T = ya.shape[0]

    def epilogue(dm, extra, const, out):
        b_ref = const[0]
        g_a = _sig(extra[0][...] + b_ref[:, :D])
        g_b = _sig(extra[1][...] + b_ref[:, D:])
        out[0][...] = (dm * g_a).astype(bf16)
        out[1][...] = (dm * g_b).astype(bf16)
        dla = dm * extra[2][...] * g_a * (1.0 - g_a)
        dlb = dm * extra[3][...] * g_b * (1.0 - g_b)
        out[2][0] = dla.astype(bf16)
        out[2][1] = dlb.astype(bf16)
        _add_colsum(out[3], dla, slice(0, D))
        _add_colsum(out[3], dlb, slice(D, 2 * D))

    return _matmul_fused(
        "mm_dmixed_gate", dx1b[None], w_out[None], ((0, 0),), epilogue,
        [(z8, _fsec(Z_GA)), (z8, _fsec(Z_GB)), (ya, _frows()), (yb, _frows())], [gate_b],
        [_rowshape(T, bf16), _rowshape(T, bf16),
         (jax.ShapeDtypeStruct(dz8.shape, bf16), pl.BlockSpec((2, FTM, D), lambda i: (1, i, 0))), _sumshape(2 * D)],
        nt=True, sums=True, passed=[dz8], aliases={0: 2})


def _convnorm_bwd(dya, w_conv_out, c, g):
    T = c.shape[0]

    def epilogue(ds, extra, const, out):
        cv = extra[0][...]
        rstd = _rms(cv)
        r0 = cv * rstd
        gv = const[0][...]
        r = r0 * gv
        sg = _sig(r)
        dr = ds * sg * (1.0 + r * (1.0 - sg))
        out[0][...] = _rms_bwd(dr * gv, r0, rstd)
        _add_colsum(out[1], dr * r0)

    return _matmul_fused("mm_ds_convnorm", dya[None], w_conv_out[None], ((0, 0),), epilogue, [(c, _frows())], [g],
                         [_rowshape(T, f32), _sumshape()], nt=True, sums=True)


def _in_norm1_bwd(dz8, w_in_t, x, dx1, g, token):
    T = x.shape[0]

    def epilogue(dh, extra, const, out):
        xv = extra[0][...]
        rstd = _rms(xv)
        xn = xv * rstd
        out[0][...] = extra[1][...] + _rms_bwd(dh * const[0][...], xn, rstd)
        _add_colsum(out[1], dh * xn)

    return _matmul_fused("mm_dh_norm1", dz8, w_in_t.reshape(7, D, D), tuple(zip(range(7), _W_OF_Z)), epilogue,
                         [(x, _frows()), (dx1, _frows())], [g], [_rowshape(T, f32), _sumshape()],
                         sums=True, passed=[token])


CCW = 256
CR = 64
HALO = 32


def _conv_fwd(z8, conv_w, conv_b, S):
    T = z8.shape[1]
    nb = T // S
    ncb = D // CCW

    def body(av_ref, ag_ref, w_ref, b_ref, c_ref, pad):
        pad[0:HALO, :] = jnp.zeros((HALO, CCW), f32)

        def fill(i, carry):
            r0 = pl.multiple_of(i * 256, 256)
            pad[pl.ds(HALO + r0, 256), :] = av_ref[pl.ds(r0, 256), :] * _sig(ag_ref[pl.ds(r0, 256), :])
            return carry

        lax.fori_loop(0, S // 256, fill, 0)
        bias = b_ref[...]

        def chunk(i, carry):
            r0 = pl.multiple_of(i * CR, CR)
            win = pad[pl.ds(r0, CR + HALO), :]
            acc = jnp.zeros((CR, CCW), f32) + bias
            for s in range(8):
                part = None
                for m in range((CONV_WIDTH - 1 - s) // 8 + 1):
                    j = CONV_WIDTH - 1 - 8 * m - s
                    term = win[24 - 8 * m:24 - 8 * m + CR + 8, :] * w_ref[j:j + 1, :]
                    part = term if part is None else part + term
                acc = acc + part[8 - s:8 - s + CR, :]
            c_ref[pl.ds(r0, CR), :] = acc
            return carry

        lax.fori_loop(0, S // CR, chunk, 0)

    zs = lambda s: pl.BlockSpec((None, S, CCW), lambda b, cb: (s, b, cb))
    return pl.pallas_call(
        body, name="conv_fwd", grid=(nb, ncb),
        in_specs=[zs(Z_AVAL), zs(Z_AGATE), pl.BlockSpec((CONV_WIDTH, CCW), lambda b, cb: (0, cb)),
                  pl.BlockSpec((1, CCW), lambda b, cb: (0, cb))],
        out_specs=pl.BlockSpec((S, CCW), lambda b, cb: (b, cb)),
        out_shape=jax.ShapeDtypeStruct((T, D), f32),
        scratch_shapes=[pltpu.VMEM((S + HALO, CCW), f32)],
        compiler_params=_cparams(("parallel", "parallel")))(z8, z8, conv_w, conv_b)


def _conv_bwd(dc, z8, conv_w, dz8, S):
    T = dc.shape[0]
    nb = T // S
    ncb = D // CCW

    def body(dc_ref, av_ref, ag_ref, w_ref, dz_in, dz_ref, dw_ref, apad, dpad, shbuf):
        del dz_in
        apad[0:HALO, :] = jnp.zeros((HALO, CCW), f32)
        dpad[S:S + HALO, :] = jnp.zeros((HALO, CCW), f32)
        dw_ref[...] = jnp.zeros_like(dw_ref)

        def fill(i, carry):
            r0 = pl.multiple_of(i * 256, 256)
            apad[pl.ds(HALO + r0, 256), :] = av_ref[pl.ds(r0, 256), :] * _sig(ag_ref[pl.ds(r0, 256), :])
            dpad[pl.ds(r0, 256), :] = dc_ref[pl.ds(r0, 256), :]
            return carry

        lax.fori_loop(0, S // 256, fill, 0)

        def chunk(i, carry):
            r0 = pl.multiple_of(i * CR, CR)
            dwin = dpad[pl.ds(r0, CR + HALO), :]
            da = jnp.zeros((CR, CCW), f32)
            for s in range(8):
                shbuf[...] = dwin[s:s + CR, :]
                dshift = shbuf[...]
                part = None
                for m in range((CONV_WIDTH - 1 - s) // 8 + 1):
                    j = CONV_WIDTH - 1 - 8 * m - s
                    term = dwin[8 * m:8 * m + CR + 8, :] * w_ref[j:j + 1, :]
                    part = term if part is None else part + term
                    a_lag = apad[pl.ds(r0 + HALO - 8 * m, CR), :]
                    dw_ref[8 * j:8 * j + 8, :] += _colsum8(dshift * a_lag)
                da = da + part[s:s + CR, :]
            dw_ref[8 * CONV_WIDTH:8 * CONV_WIDTH + 8, :] += _colsum8(dwin[0:CR, :])
            av = av_ref[pl.ds(r0, CR), :]
            sg = _sig(ag_ref[pl.ds(r0, CR), :])
            dz_ref[0, pl.ds(r0, CR), :] = (da * sg).astype(bf16)
            dz_ref[1, pl.ds(r0, CR), :] = (da * av * sg * (1.0 - sg)).astype(bf16)
            return carry

        lax.fori_loop(0, S // CR, chunk, 0)

    zs = lambda s: pl.BlockSpec((None, S, CCW), lambda b, cb: (s, b, cb))
    return pl.pallas_call(
        body, name="conv_bwd", grid=(nb, ncb),
        in_specs=[pl.BlockSpec((S, CCW), lambda b, cb: (b, cb)), zs(Z_AVAL), zs(Z_AGATE),
                  pl.BlockSpec((CONV_WIDTH, CCW), lambda b, cb: (0, cb)), pl.BlockSpec(memory_space=pl.ANY)],
        out_specs=[pl.BlockSpec((2, S, CCW), lambda b, cb: (0, b, cb)),
                   pl.BlockSpec((None, 256, CCW), lambda b, cb: (b, 0, cb))],
        out_shape=[jax.ShapeDtypeStruct(dz8.shape, bf16), jax.ShapeDtypeStruct((nb, 256, D), f32)],
        input_output_aliases={4: 0},
        scratch_shapes=[pltpu.VMEM((S + HALO, CCW), f32), pltpu.VMEM((S + HALO, CCW), f32),
                        pltpu.VMEM((CR, CCW), f32)],
        compiler_params=_cparams(("parallel", "parallel")))(dc, z8, z8, conv_w, dz8)


FR = 128
NFB = D_FF // CCW
FBW = 128


def _ffn_window(ref, i, r0):
    return ref[pl.ds(r0 - 8, FR + 8), :]


def _ffn_u(win, w_ref, b_ref):
    return (win[6:6 + FR, :] * w_ref[0:1, :] + win[7:7 + FR, :] * w_ref[1:2, :]
            + win[8:8 + FR, :] * w_ref[2:3, :] + b_ref[...])


def _ffn_fwd(u3, ffn_w, ffn_b, S):
    T = u3.shape[1]
    nb = T // S

    def body(uv_ref, ug_ref, wv_ref, wg_ref, bv_ref, bg_ref, f_ref):
        def chunk(first, i):
            r0 = 0 if first else pl.multiple_of(i * FR, FR)
            if first:
                z = jnp.zeros((8, CCW), f32)
                wv = jnp.concatenate([z, uv_ref[0:FR, :]], axis=0)
                wg = jnp.concatenate([z, ug_ref[0:FR, :]], axis=0)
            else:
                wv = _ffn_window(uv_ref, i, r0)
                wg = _ffn_window(ug_ref, i, r0)
            u_val = _ffn_u(wv, wv_ref, bv_ref)
            u_gate = _ffn_u(wg, wg_ref, bg_ref)
            f_ref[pl.ds(r0, FR), :] = (u_gate * _sig(u_gate) * u_val).astype(bf16)

        chunk(True, 0)

        def loop(i, carry):
            chunk(False, i)
            return carry

        lax.fori_loop(1, S // FR, loop, 0)

    us = lambda h: pl.BlockSpec((None, S, CCW), lambda b, cb: (h, b, cb))
    ws = lambda h: pl.BlockSpec((3, CCW), lambda b, cb: (0, h * NFB + cb))
    bs = lambda h: pl.BlockSpec((1, CCW), lambda b, cb: (0, h * NFB + cb))
    return pl.pallas_call(
        body, name="ffn_fwd", grid=(nb, NFB),
        in_specs=[us(0), us(1), ws(0), ws(1), bs(0), bs(1)],
        out_specs=pl.BlockSpec((S, CCW), lambda b, cb: (b, cb)),
        out_shape=jax.ShapeDtypeStruct((T, D_FF), bf16),
        compiler_params=_cparams(("parallel", "parallel")))(u3, u3, ffn_w, ffn_w, ffn_b, ffn_b)


def _ffn_bwd(u3, df, ffn_w, ffn_b, S):
    T = u3.shape[1]
    nb = T // S

    def body(uv_ref, ug_ref, df_ref, wv_ref, wg_ref, bv_ref, bg_ref, du_ref, dw_ref, dvpad, dgpad, shbuf):
        dvpad[S:S + 8, :] = jnp.zeros((8, FBW), f32)
        dgpad[S:S + 8, :] = jnp.zeros((8, FBW), f32)
        dw_ref[...] = jnp.zeros_like(dw_ref)

        def chunk(first, i):
            r0 = 0 if first else pl.multiple_of(i * FR, FR)
            if first:
                z = jnp.zeros((8, FBW), f32)
                wv = jnp.concatenate([z, uv_ref[0:FR, :]], axis=0)
                wg = jnp.concatenate([z, ug_ref[0:FR, :]], axis=0)
            else:
                wv = _ffn_window(uv_ref, i, r0)
                wg = _ffn_window(ug_ref, i, r0)
            taps = []
            for h, win in enumerate((wv, wg)):
                shbuf[2 * h] = win[6:6 + FR, :]
                shbuf[2 * h + 1] = win[7:7 + FR, :]
                taps.append((shbuf[2 * h], shbuf[2 * h + 1], win[8:8 + FR, :]))
            conv = lambda x, w_ref, b_ref: (x[0] * w_ref[0:1, :] + x[1] * w_ref[1:2, :] + x[2] * w_ref[2:3, :]
                                            + b_ref[...])
            u_val = conv(taps[0], wv_ref, bv_ref)
            u_gate = conv(taps[1], wg_ref, bg_ref)
            dfc = df_ref[pl.ds(r0, FR), :]
            sg = _sig(u_gate)
            d_val = dfc * u_gate * sg
            d_gate = dfc * u_val * sg * (1.0 + u_gate * (1.0 - sg))
            dvpad[pl.ds(r0, FR), :] = d_val
            dgpad[pl.ds(r0, FR), :] = d_gate
            for h, dd in enumerate((d_val, d_gate)):
                for j in range(3):
                    dw_ref[h, 8 * j:8 * j + 8, :] += _colsum8(dd * taps[h][j])
                dw_ref[h, 24:32, :] += _colsum8(dd)

        chunk(True, 0)

        def loop(i, carry):
            chunk(False, i)
            return carry

        lax.fori_loop(1, S // FR, loop, 0)

        def back(i, carry):
            r0 = pl.multiple_of(i * FR, FR)
            for h, (dpad, w_ref) in enumerate(((dvpad, wv_ref), (dgpad, wg_ref))):
                win = dpad[pl.ds(r0, FR + 8), :]
                du = (win[0:FR, :] * w_ref[2:3, :] + win[1:1 + FR, :] * w_ref[1:2, :]
                      + win[2:2 + FR, :] * w_ref[0:1, :])
                du_ref[h, pl.ds(r0, FR), :] = du.astype(bf16)
            return carry

        lax.fori_loop(0, S // FR, back, 0)

    ncb = D_FF // FBW
    us = lambda h: pl.BlockSpec((None, S, FBW), lambda b, cb: (h, b, cb))
    ws = lambda h: pl.BlockSpec((3, FBW), lambda b, cb: (0, h * ncb + cb))
    bs = lambda h: pl.BlockSpec((1, FBW), lambda b, cb: (0, h * ncb + cb))
    return pl.pallas_call(
        body, name="ffn_bwd", grid=(nb, ncb),
        in_specs=[us(0), us(1), pl.BlockSpec((S, FBW), lambda b, cb: (b, cb)), ws(0), ws(1), bs(0), bs(1)],
        out_specs=[pl.BlockSpec((2, S, FBW), lambda b, cb: (0, b, cb)),
                   pl.BlockSpec((None, 2, 32, FBW), lambda b, cb: (b, 0, 0, cb))],
        out_shape=[jax.ShapeDtypeStruct((2, T, D_FF), bf16), jax.ShapeDtypeStruct((nb, 2, 32, D_FF), f32)],
        scratch_shapes=[pltpu.VMEM((S + 8, FBW), f32), pltpu.VMEM((S + 8, FBW), f32),
                        pltpu.VMEM((4, FR, FBW), f32)],
        compiler_params=_cparams(("parallel", "parallel")))(u3, u3, df, ffn_w, ffn_w, ffn_b, ffn_b)


AB = ATTN_BLOCK


def _attn_bias_np():
    slopes = (np.float32(2.0) ** (np.float32(-8.0) * np.arange(1, N_HEADS + 1, dtype=np.float32)
                                  / np.float32(N_HEADS))).astype(np.float32)
    steps = (np.arange(AB)[:, None] + AB) - np.arange(2 * AB)[None, :]
    own = (np.arange(2 * AB) >= AB)[None, :]
    out = []
    for window, dil in GROUPS:
        valid = (steps >= 0) & (steps <= window // dil)
        dist = slopes[:, None, None] * (steps * dil).astype(np.float32)[None]
        kinds = [np.where(v[None], dist, np.float32(MASK_BIAS)) for v in (valid, valid & own)]
        out.append(np.stack(kinds, axis=1))
    return np.stack(out).astype(np.float32)


def _attn_bias():
    return jnp.asarray(_attn_bias_np())


def _head_masks():
    lane = lax.broadcasted_iota(jnp.int32, (1, 128), 1)
    return (lane < HEAD_DIM, lane >= HEAD_DIM)


def _perm_chunks(S, d):
    L = S // d
    ch = min(L, 256)
    out = []
    for r in range(d):
        for c in range(L // ch):
            start = r + d * ch * c
            out.append((pl.ds(start, ch, stride=d) if d > 1 else pl.ds(start, ch), r * L + c * ch, ch))
    return out


def _stack_heads(x, masks):
    return jnp.concatenate([jnp.where(masks[0], x, 0), jnp.where(masks[1], x, 0)], axis=0)


def _block_row(j):
    return j * AB if isinstance(j, int) else pl.multiple_of(j * AB, AB)


def _three_stages(n, stage_a, stage_b, stage_c, unroll):
    stage_a(0)
    stage_a(1)
    stage_b(0)

    def body(j, carry):
        stage_c(j - 1)
        stage_b(j)
        stage_a(j + 1)
        return carry

    lax.fori_loop(1, n - 1, body, 0, unroll=unroll)
    stage_c(n - 2)
    stage_b(n - 1)
    stage_c(n - 1)


_NT = (((1,), (1,)), ((), ()))
_TN = (((0,), (0,)), ((), ()))
SCH = 128


def _attn_fwd(qn, kn, z8, bias, S):
    T = qn.shape[0]
    nb = T // S
    nblk = S // AB

    def body(q_ref, k_ref, v_ref, bias_ref, o_ref, ob_ref, lse_ref, qs, ks, vs, s2, p2, ogp, lgp, *group_scratch):
        og, lg = group_scratch[:3], group_scratch[3:]
        masks = _head_masks()
        ks[0:AB, :] = jnp.zeros((AB, 128), bf16)
        vs[0:AB, :] = jnp.zeros((AB, 128), bf16)

        for g, (_, d) in enumerate(GROUPS):
            nsub = S // (d * AB)
            chunks = _perm_chunks(S, d)
            for src, dst, ch in chunks:
                qs[dst:dst + ch, :] = q_ref[src, :].astype(bf16)
                ks[AB + dst:AB + dst + ch, :] = k_ref[src, :].astype(bf16)
                vs[AB + dst:AB + dst + ch, :] = v_ref[src, :].astype(bf16)
            od, ld = (og[g], lg[g]) if d == 1 else (ogp, lgp)

            def scores(j):
                r0 = _block_row(j)
                q2 = _stack_heads(qs[pl.ds(r0, AB), :], masks)
                s2[j] = lax.dot_general(q2, ks[pl.ds(r0, 2 * AB), :], _NT, preferred_element_type=f32)

            def softmax(j, g=g, nsub=nsub, ld=ld):
                r0 = _block_row(j)
                kind = int(j % nsub == 0) if isinstance(j, int) else (j % nsub == 0).astype(jnp.int32)
                for cc in range(AB // SCH):
                    lses = []
                    for hh in range(2):
                        rows = pl.ds(hh * AB + cc * SCH, SCH)
                        sb = s2[j, rows, :] - bias_ref[g, hh, kind, cc * SCH:(cc + 1) * SCH, :]
                        m = jnp.max(sb, axis=-1, keepdims=True)
                        p = jnp.exp(sb - m)
                        den = jnp.sum(p, axis=-1, keepdims=True)
                        p2[j, rows, :] = (p * (1.0 / den)).astype(bf16)
                        lses.append(m + jnp.log(den))
                    ld[pl.ds(r0 + cc * SCH, SCH), :] = jnp.where(masks[0], lses[0], lses[1])

            def values(j, od=od):
                r0 = _block_row(j)
                pv2 = jnp.dot(p2[j], vs[pl.ds(r0, 2 * AB), :], preferred_element_type=f32)
                od[pl.ds(r0, AB), :] = jnp.where(masks[0], pv2[:AB], pv2[AB:])

            _three_stages(nblk, scores, softmax, values, nblk - 2)

            if d > 1:
                for src, dst, ch in chunks:
                    og[g][src, :] = ogp[dst:dst + ch, :]
                    lg[g][src, :] = lgp[dst:dst + ch, :]

        def combine(i, carry):
            rr = pl.ds(pl.multiple_of(i * 256, 256), 256)
            l0, l1, l2 = lg[0][rr, :], lg[1][rr, :], lg[2][rr, :]
            mx = jnp.maximum(jnp.maximum(l0, l1), l2)
            e0, e1, e2 = jnp.exp(l0 - mx), jnp.exp(l1 - mx), jnp.exp(l2 - mx)
            den = e0 + e1 + e2
            o = (e0 * og[0][rr, :] + e1 * og[1][rr, :] + e2 * og[2][rr, :]) / den
            o_ref[rr, :] = o
            ob_ref[rr, :] = o.astype(bf16)
            lse_ref[rr, :] = mx + jnp.log(den)
            return carry

        lax.fori_loop(0, S // 256, combine, 0, unroll=True)

    blk = pl.BlockSpec((S, 128), lambda b, hp: (b, hp))
    return pl.pallas_call(
        body, name="attn_fwd", grid=(nb, N_HEADS // 2),
        in_specs=[blk, blk, pl.BlockSpec((None, S, 128), lambda b, hp: (Z_V, b, hp)),
                  pl.BlockSpec((3, 2, 2, AB, 2 * AB), lambda b, hp: (0, hp, 0, 0, 0))],
        out_specs=[blk, blk, blk],
        out_shape=[jax.ShapeDtypeStruct((T, D), f32), jax.ShapeDtypeStruct((T, D), bf16),
                   jax.ShapeDtypeStruct((T, D), f32)],
        scratch_shapes=[pltpu.VMEM((S, 128), bf16), pltpu.VMEM((S + AB, 128), bf16), pltpu.VMEM((S + AB, 128), bf16),
                        pltpu.VMEM((nblk, 2 * AB, 2 * AB), f32), pltpu.VMEM((nblk, 2 * AB, 2 * AB), bf16),
                        pltpu.VMEM((S, 128), f32), pltpu.VMEM((S, 128), f32)] + [pltpu.VMEM((S, 128), f32)] * 6,
        compiler_params=_cparams(("parallel", "parallel")))(qn, kn, z8, bias)


def _attn_bwd(qn, kn, z8, do, o, lse, bias, bd, qg, kg, dz8, S):
    T = qn.shape[0]
    nb = T // S

    nblk = S // AB

    def body(q_ref, k_ref, v_ref, do_ref, o_ref, lse_ref, bias_ref, bd_ref, qraw_ref, kraw_ref, qg_ref, kg_ref,
             dz_in, dz_ref, dqg_ref, dkg_ref,
             dq_ref, dk_ref, dv_ref, delta, qs, ks, vs, dos, lsp, dlp, s2, dp2, p2, ds2, dqp, dkp, dvp):
        del dz_in
        masks = _head_masks()
        bdv = bd_ref[...]
        dq_ref[...] = jnp.zeros_like(dq_ref)
        dk_ref[...] = jnp.zeros_like(dk_ref)
        dv_ref[...] = jnp.zeros_like(dv_ref)
        ks[0:AB, :] = jnp.zeros((AB, 128), bf16)
        vs[0:AB, :] = jnp.zeros((AB, 128), bf16)

        def prep(i, carry):
            rr = pl.ds(pl.multiple_of(i * 256, 256), 256)
            delta[rr, :] = _head_sum(do_ref[rr, :] * o_ref[rr, :], bdv)
            return carry

        lax.fori_loop(0, S // 256, prep, 0, unroll=True)

        for g, (_, d) in enumerate(GROUPS):
            nsub = S // (d * AB)
            chunks = _perm_chunks(S, d)
            for src, dst, ch in chunks:
                qs[dst:dst + ch, :] = q_ref[src, :].astype(bf16)
                ks[AB + dst:AB + dst + ch, :] = k_ref[src, :].astype(bf16)
                vs[AB + dst:AB + dst + ch, :] = v_ref[src, :].astype(bf16)
                dos[dst:dst + ch, :] = do_ref[src, :].astype(bf16)
                lsp[dst:dst + ch, :] = lse_ref[src, :]
                dlp[dst:dst + ch, :] = delta[src, :]
            dkp[...] = jnp.zeros_like(dkp)
            dvp[...] = jnp.zeros_like(dvp)

            def scores(j):
                r0 = _block_row(j)
                q2 = _stack_heads(qs[pl.ds(r0, AB), :], masks)
                do2 = _stack_heads(dos[pl.ds(r0, AB), :], masks)
                s2[j] = lax.dot_general(q2, ks[pl.ds(r0, 2 * AB), :], _NT, preferred_element_type=f32)
                dp2[j] = lax.dot_general(do2, vs[pl.ds(r0, 2 * AB), :], _NT, preferred_element_type=f32)

            def probs(j, g=g, nsub=nsub):
                r0 = _block_row(j)
                kind = int(j % nsub == 0) if isinstance(j, int) else (j % nsub == 0).astype(jnp.int32)
                for cc in range(AB // SCH):
                    lse_c = lsp[pl.ds(r0 + cc * SCH, SCH), :]
                    del_c = dlp[pl.ds(r0 + cc * SCH, SCH), :]
                    for hh in range(2):
                        c0 = hh * HEAD_DIM
                        rows = pl.ds(hh * AB + cc * SCH, SCH)
                        sb = s2[j, rows, :] - bias_ref[g, hh, kind, cc * SCH:(cc + 1) * SCH, :]
                        p = jnp.exp(sb - lse_c[:, c0:c0 + 1])
                        p2[j, rows, :] = p.astype(bf16)
                        ds2[j, rows, :] = (p * (dp2[j, rows, :] - del_c[:, c0:c0 + 1])).astype(bf16)

            def grads(j):
                r0 = _block_row(j)
                q2 = _stack_heads(qs[pl.ds(r0, AB), :], masks)
                do2 = _stack_heads(dos[pl.ds(r0, AB), :], masks)
                dsb = ds2[j]
                t = jnp.dot(dsb, ks[pl.ds(r0, 2 * AB), :], preferred_element_type=f32)
                dqp[pl.ds(r0, AB), :] = jnp.where(masks[0], t[:AB], t[AB:])
                dkp[pl.ds(r0, 2 * AB), :] += lax.dot_general(dsb, q2, _TN, preferred_element_type=f32)
                dvp[pl.ds(r0, 2 * AB), :] += lax.dot_general(p2[j], do2, _TN, preferred_element_type=f32)

            _three_stages(nblk, scores, probs, grads, nblk - 2)

            for src, dst, ch in chunks:
                dq_ref[src, :] += dqp[dst:dst + ch, :]
                dk_ref[src, :] += dkp[AB + dst:AB + dst + ch, :]
                dv_ref[src, :] += dvp[AB + dst:AB + dst + ch, :]

        @pl.when(pl.program_id(1) == 0)
        def _():
            dqg_ref[...] = jnp.zeros_like(dqg_ref)
            dkg_ref[...] = jnp.zeros_like(dkg_ref)

        def norms(i, carry):
            rr = pl.ds(pl.multiple_of(i * 256, 256), 256)

            def one(raw, dn_scaled, g, dg_ref, sec):
                rstd = lax.rsqrt(_head_sum(raw * raw, bdv) * (1.0 / HEAD_DIM) + EPS)
                n = raw * rstd
                dg_ref[...] += _colsum8(dn_scaled * n)
                dn = dn_scaled * g
                draw = rstd * (dn - n * (_head_sum(dn * n, bdv) * (1.0 / HEAD_DIM)))
                dz_ref[sec, rr, :] = draw.astype(bf16)

            one(qraw_ref[rr, :], dq_ref[rr, :] * (HEAD_DIM ** -0.5), qg_ref[...], dqg_ref, 0)
            one(kraw_ref[rr, :], dk_ref[rr, :], kg_ref[...], dkg_ref, 1)
            dz_ref[2, rr, :] = dv_ref[rr, :].astype(bf16)
            dz_ref[3, rr, :] = jnp.zeros((256, 128), bf16)
            return carry

        lax.fori_loop(0, S // 256, norms, 0, unroll=True)

    blk = pl.BlockSpec((S, 128), lambda hp, b: (b, hp))
    sec = lambda s: pl.BlockSpec((None, S, 128), lambda hp, b: (s, b, hp))
    gain = pl.BlockSpec((1, 128), lambda hp, b: (0, hp))
    row = lambda dt, pad=0: pltpu.VMEM((S + pad, 128), dt)
    blocks = lambda dt: pltpu.VMEM((nblk, 2 * AB, 2 * AB), dt)
    return pl.pallas_call(
        body, name="attn_bwd", grid=(N_HEADS // 2, nb),
        in_specs=[blk, blk, sec(Z_V), blk, blk, blk,
                  pl.BlockSpec((3, 2, 2, AB, 2 * AB), lambda hp, b: (0, hp, 0, 0, 0)),
                  pl.BlockSpec((128, 128), lambda hp, b: (0, 0)), sec(Z_Q), sec(Z_K), gain, gain,
                  pl.BlockSpec(memory_space=pl.ANY)],
        out_specs=[pl.BlockSpec((4, S, 128), lambda hp, b: (1, b, hp)),
                   pl.BlockSpec((8, 128), lambda hp, b: (0, hp)), pl.BlockSpec((8, 128), lambda hp, b: (0, hp))],
        out_shape=[jax.ShapeDtypeStruct(dz8.shape, bf16), jax.ShapeDtypeStruct((8, D), f32),
                   jax.ShapeDtypeStruct((8, D), f32)],
        input_output_aliases={12: 0},
        scratch_shapes=[row(f32), row(f32), row(f32),
                        row(f32), row(bf16), row(bf16, AB), row(bf16, AB), row(bf16), row(f32), row(f32),
                        blocks(f32), blocks(f32), blocks(bf16), blocks(bf16), row(f32), row(f32, AB), row(f32, AB)],
        compiler_params=_cparams(("parallel", "arbitrary")))(qn, kn, z8, do, o, lse, bias, bd, z8, z8, qg, kg, dz8)


def _any_spec():
    return pl.BlockSpec(memory_space=pl.ANY)


def _allgather_rows(shards, n_full):
    n = len(shards)

    def body(*refs):
        ins, outs = refs[:n], refs[n:2 * n]
        send_sems, recv_sems, local_sems = refs[2 * n:]
        x, y, c, me = _my_pos()
        sibling = (x, y, 1 - c)
        chips = [(1 - x, y), (x, 1 - y), (1 - x, 1 - y)]

        def idx(px, py, pc):
            return 4 * px + 2 * py + pc

        def copy(a, k, blk, to, src=None):
            return pltpu.make_async_remote_copy(
                src_ref=outs[a].at[blk] if src is None else src, dst_ref=outs[a].at[blk],
                send_sem=send_sems.at[a, k], recv_sem=recv_sems.at[a, k], device_id=to, device_id_type=MESH)

        mine = [pltpu.make_async_copy(ins[a], outs[a].at[me], local_sems.at[a]) for a in range(n)]
        for cp in mine:
            cp.start()
        first = []
        for a in range(n_full):
            first.append(copy(a, 0, me, sibling, src=ins[a]))
            first += [copy(a, 1 + j, me, (*chip, c), src=ins[a]) for j, chip in enumerate(chips)]
        for cp in first:
            cp.start()
        passed = []
        for a in range(n_full):
            for j, chip in enumerate(chips):
                blk = idx(*chip, c)
                copy(a, 1 + j, blk, (x, y, c)).wait_recv()
                cp = copy(a, 4 + j, blk, sibling)
                cp.start()
                passed.append(cp)
        for a in range(n_full):
            copy(a, 0, idx(x, y, 1 - c), (x, y, c)).wait_recv()
            for j, chip in enumerate(chips):
                copy(a, 4 + j, idx(*chip, 1 - c), (x, y, c)).wait_recv()
        for cp in first + passed:
            cp.wait_send()
        for cp in mine:
            cp.wait()

    return pl.pallas_call(
        body, name="allgather_weights",
        in_specs=[_any_spec()] * n, out_specs=[_any_spec()] * n,
        out_shape=[jax.ShapeDtypeStruct((N_DEV,) + s.shape, s.dtype) for s in shards],
        scratch_shapes=[pltpu.SemaphoreType.DMA((n_full, 7)), pltpu.SemaphoreType.DMA((n_full, 7)),
                        pltpu.SemaphoreType.DMA((n,))],
    )(*shards)


def _peer(x, y, c, k):
    tx = 1 - x if (k >> 2) & 1 else x
    ty = 1 - y if (k >> 1) & 1 else y
    tc = 1 - c if k & 1 else c
    return (tx, ty, tc), 4 * tx + 2 * ty + tc


_PEER_ORDER = (2, 4, 6, 3, 5, 7, 1)


_HBM = pl.BlockSpec(memory_space=pltpu.HBM)
_SEM = pl.BlockSpec(memory_space=pltpu.SEMAPHORE)
_EFFECT = pltpu.SideEffectType.DATAFLOW_SIDE_EFFECTING


def _exchange_copies(srcs, lands, send_sems, recv_sems, gather):
    x, y, c, me = _my_pos()
    copies = []
    for k in _PEER_ORDER:
        tgt, tidx = _peer(x, y, c, k)
        for a in range(len(srcs)):
            copies.append(pltpu.make_async_remote_copy(
                src_ref=srcs[a] if gather else srcs[a].at[tidx], dst_ref=lands[a].at[me],
                send_sem=send_sems.at[7 * a + k - 1], recv_sem=recv_sems.at[7 * a + k - 1],
                device_id=tgt, device_id_type=MESH))
    return copies


def _exchange_start(name, srcs, lands=None, after=None):
    n = len(srcs)
    gather = lands is not None
    if lands is None:
        lands = [lax.empty(g.shape, g.dtype) for g in srcs]
    extra = [] if after is None else [after]

    def body(*refs):
        src_refs, land_refs = refs[:n], refs[n:2 * n]
        send_sems, recv_sems = refs[2 * n + len(extra)], refs[2 * n + len(extra) + 1]
        token = refs[-1]
        for cp in _exchange_copies(src_refs, land_refs, send_sems, recv_sems, gather):
            cp.start()
        token[...] = jnp.zeros_like(token)

    hbm = lambda a: pltpu.with_memory_space_constraint(a, pltpu.HBM)
    outs = pl.pallas_call(
        body, name=name,
        out_shape=(pltpu.SemaphoreType.DMA((7 * n,)), pltpu.SemaphoreType.DMA((7 * n,)),
                   *[pltpu.HBM(g.shape, g.dtype) for g in list(srcs) + list(lands)],
                   jax.ShapeDtypeStruct((8, 128), f32)),
        in_specs=[_HBM] * (2 * n) + [pl.BlockSpec(memory_space=pl.ANY)] * len(extra),
        out_specs=(_SEM, _SEM, *([_HBM] * (2 * n)), pl.BlockSpec(memory_space=pltpu.VMEM)),
        input_output_aliases={i: 2 + i for i in range(2 * n)},
        compiler_params=pltpu.CompilerParams(has_side_effects=_EFFECT),
    )(*[hbm(g) for g in srcs], *[hbm(g) for g in lands], *extra)
    return outs[0], outs[1], list(outs[2:2 + n]), list(outs[2 + n:2 + 2 * n]), outs[-1], gather


def _exchange_wait(name, started, after):
    send_sems, recv_sems, srcs, lands, _, gather = started
    n = len(srcs)
    after = list(after) if isinstance(after, (list, tuple)) else [after]

    def body(*refs):
        src_refs, land_refs = refs[:n], refs[n:2 * n]
        s_sems, r_sems = refs[2 * n], refs[2 * n + 1]
        for cp in _exchange_copies(src_refs, land_refs, s_sems, r_sems, gather):
            cp.wait_send()
            cp.wait_recv()

    outs = pl.pallas_call(
        body, name=name,
        out_shape=tuple(pltpu.HBM(a.shape, a.dtype) for a in list(srcs) + list(lands)),
        in_specs=[_HBM] * (2 * n) + [_SEM, _SEM] + [pl.BlockSpec(memory_space=pl.ANY)] * len(after),
        out_specs=tuple([_HBM] * (2 * n)),
        input_output_aliases={i: i for i in range(2 * n)},
        compiler_params=pltpu.CompilerParams(has_side_effects=_EFFECT),
    )(*srcs, *lands, send_sems, recv_sems, *after)
    return list(outs[:n]), list(outs[n:])


SMALL_ROWS = 128


def _small_start(name, sg, after=None):
    return _exchange_start(name, [sg], [lax.empty((N_DEV,) + sg.shape, f32)], after=after)


def _small_sum(name, me, started, after):
    (own,), (slots,) = _exchange_wait(name + "_wait", started, after)

    def body(me_ref, s_ref, own_ref, out_ref):
        acc = None
        for p in range(N_DEV):
            term = lax.cond(me_ref[0] == p, lambda: own_ref[...], lambda p=p: s_ref[p])
            acc = term if acc is None else acc + term
        out_ref[...] = acc

    return pl.pallas_call(
        body, name=name + "_sum",
        in_specs=[pl.BlockSpec(memory_space=pltpu.SMEM), pl.BlockSpec(memory_space=pltpu.VMEM),
                  pl.BlockSpec(memory_space=pltpu.VMEM)],
        out_specs=pl.BlockSpec(memory_space=pltpu.VMEM),
        out_shape=jax.ShapeDtypeStruct(own.shape, f32))(me, slots, own)


def _adam_math(g, w, m, v):
    m = ADAM_B1 * m + (1.0 - ADAM_B1) * g
    v = ADAM_B2 * v + (1.0 - ADAM_B2) * (g * g)
    m_hat = m / (1.0 - ADAM_B1 ** ADAM_STEP)
    v_hat = v / (1.0 - ADAM_B2 ** ADAM_STEP)
    delta = -ADAM_LR * (m_hat / (jnp.sqrt(v_hat) + ADAM_EPS) + ADAM_WD * w)
    return delta, m, v


def _adam_slots(name, me, slots, own, w, m, v, tr, transposed=False):
    rows = slots.shape[1]

    def body(me_ref, s_ref, own_ref, w_ref, m_ref, v_ref, g_ref, d_ref, nm_ref, nv_ref):
        mine = own_ref[...]
        g = None
        for p in range(N_DEV):
            term = lax.cond(me_ref[0] == p, lambda: mine, lambda p=p: s_ref[p]).astype(f32)
            g = term if g is None else g + term
        if transposed:
            g = g.T
        delta, nm, nv = _adam_math(g, w_ref[...], m_ref[...], v_ref[...])
        g_ref[...] = g
        d_ref[...] = delta
        nm_ref[...] = nm
        nv_ref[...] = nv

    mode = dict(pipeline_mode=pl.Buffered(1)) if rows == tr else {}
    if transposed:
        rs = pl.BlockSpec((D, tr), lambda i, me_ref: (0, i))
        rs_in = pl.BlockSpec((D, tr), lambda i, me_ref: (0, i), **mode)
    else:
        rs = pl.BlockSpec((tr, D), lambda i, me_ref: (i, 0))
        rs_in = pl.BlockSpec((tr, D), lambda i, me_ref: (i, 0), **mode)
    return pl.pallas_call(
        body, name=name,
        grid_spec=pltpu.PrefetchScalarGridSpec(
            num_scalar_prefetch=1, grid=(rows // tr,),
            in_specs=[pl.BlockSpec((N_DEV, tr, D), lambda i, me_ref: (0, i, 0), **mode),
                      pl.BlockSpec((None, tr, D), lambda i, me_ref: (me_ref[0], i, 0), **mode), rs_in, rs_in, rs_in],
            out_specs=[rs] * 4),
        out_shape=[jax.ShapeDtypeStruct(w.shape, f32)] * 4,
        compiler_params=_cparams(("parallel",)))(me, slots, own, w, m, v)


def _copy_cols(src, row0, dst, t, c0, n):
    done = 0
    while done < n:
        r, c = divmod(c0 + done, D)
        take = min(n - done, D - c)
        dst[t:t + 1, done:done + take] = src[row0 + r:row0 + r + 1, c:c + take]
        done += take


def _adam_small(me, g, w, m, v):
    o = _small_offsets()
    ffn_cols = 2 * D_FF // N_DEV
    shapes = dict(norm1_g=(1, D), gate_b=(1, 2 * D), conv_w=(CONV_WIDTH, D // N_DEV), conv_b=(1, D),
                  conv_norm_g=(1, D), q_norm_g=(1, HEAD_DIM), k_norm_g=(1, HEAD_DIM), norm2_g=(1, D),
                  ffn_conv_w=(3, ffn_cols), ffn_conv_b=(1, 2 * D_FF))
    names = tuple(shapes)

    def body(me_ref, g_ref, w_ref, m_ref, v_ref, *refs):
        parts, (d_ref, nm_ref, nv_ref) = refs[:-3], refs[-3:]
        delta, nm, nv = _adam_math(g_ref[...], w_ref[...], m_ref[...], v_ref[...])
        d_ref[...] = delta
        nm_ref[...] = nm
        nv_ref[...] = nv
        mine = pl.ds(pl.multiple_of(me_ref[0] * (D // N_DEV), D // N_DEV), D // N_DEV)
        for i, src in enumerate((g_ref, d_ref, nm_ref, nv_ref)):
            for k, name in enumerate(names):
                dst, r = parts[i * len(names) + k], o[name]
                if name == "gate_b":
                    dst[:, 0:D] = src[r:r + 1, :]
                    dst[:, D:2 * D] = src[r + 1:r + 2, :]
                elif name == "conv_w":
                    dst[...] = src[r:r + CONV_WIDTH, mine]
                elif name == "ffn_conv_b":
                    _copy_cols(src, r, dst, 0, 0, 2 * D_FF)
                elif name == "ffn_conv_w":
                    for p in range(N_DEV):
                        @pl.when(me_ref[0] == p)
                        def _(p=p, src=src, dst=dst, r=r):
                            for t in range(3):
                                _copy_cols(src, r + t * (FFN_PAD // D), dst, t, p * ffn_cols, ffn_cols)
                else:
                    dst[...] = src[r:r + 1, 0:shapes[name][1]]

    vmem = pl.BlockSpec(memory_space=pltpu.VMEM)
    outs = pl.pallas_call(
        body, name="adam_small", in_specs=[pl.BlockSpec(memory_space=pltpu.SMEM)] + [vmem] * 4,
        out_shape=[jax.ShapeDtypeStruct(shapes[n], f32) for _ in range(4) for n in names],
        scratch_shapes=[pltpu.VMEM(g.shape, f32)] * 3)(me, g, w, m, v)
    return {n: [outs[i * len(names) + k] for i in range(4)] for k, n in enumerate(names)}


FFN_PAD = 6 * D


_SMALL_PARTS = (("norm1_g", 1), ("gate_b", 2), ("conv_w", CONV_WIDTH), ("conv_b", 1), ("conv_norm_g", 1),
                ("q_norm_g", 1), ("k_norm_g", 1), ("norm2_g", 1), ("ffn_conv_w", 18), ("ffn_conv_b", 6), ("last", 1))


def _small_offsets():
    out, row = {}, 0
    for name, rows in _SMALL_PARTS:
        out[name] = row
        row += -(-rows // 8) * 8
    assert row == SMALL_ROWS
    return out


def _pack_small(norm1_g, gate_b, conv_w, conv_b, conv_norm_g, q_norm_g, k_norm_g, norm2_g, ffn_conv_w, ffn_conv_b,
                last_row=None):
    pad_h = lambda a: jnp.pad(a, ((0, 0), (0, D - HEAD_DIM)))
    pad_f = lambda a: jnp.pad(a, ((0, 0), (0, FFN_PAD - 2 * D_FF))).reshape(-1, D)
    parts = [norm1_g, gate_b.reshape(2, D), conv_w, conv_b, conv_norm_g, pad_h(q_norm_g), pad_h(k_norm_g), norm2_g,
             pad_f(ffn_conv_w), pad_f(ffn_conv_b), jnp.zeros((1, D), f32) if last_row is None else last_row]
    return jnp.concatenate([jnp.pad(p, ((0, -p.shape[0] % 8), (0, 0))) for p in parts], axis=0)


def _unpack_small(p):
    o = _small_offsets()
    rows = lambda name, n: p[o[name]:o[name] + n]
    ffn = lambda a: a.reshape(-1, FFN_PAD)[:, :2 * D_FF]
    return dict(
        norm1_g=rows("norm1_g", 1), gate_b=rows("gate_b", 2).reshape(1, 2 * D), conv_w=rows("conv_w", CONV_WIDTH),
        conv_b=rows("conv_b", 1), conv_norm_g=rows("conv_norm_g", 1), q_norm_g=rows("q_norm_g", 1)[:, :HEAD_DIM],
        k_norm_g=rows("k_norm_g", 1)[:, :HEAD_DIM], norm2_g=rows("norm2_g", 1),
        ffn_conv_w=ffn(rows("ffn_conv_w", 18)), ffn_conv_b=ffn(rows("ffn_conv_b", 6)))


_ADAM_TILE = {896: 128, 704: 704, 128: 128, 352: 176}


def kernel(x, norm1_g, w_in, gate_b, conv_w, conv_b, conv_norm_g, w_conv_out, q_norm_g, k_norm_g, w_attn_out, w_out, norm2_g, w_up, ffn_conv_w, ffn_conv_b, w_down, loss_target, m_norm1_g, m_w_in, m_gate_b, m_conv_w, m_conv_b, m_conv_norm_g, m_w_conv_out, m_q_norm_g, m_k_norm_g, m_w_attn_out, m_w_out, m_norm2_g, m_w_up, m_ffn_conv_w, m_ffn_conv_b, m_w_down, v_norm1_g, v_w_in, v_gate_b, v_conv_w, v_conv_b, v_conv_norm_g, v_w_conv_out, v_q_norm_g, v_k_norm_g, v_w_attn_out, v_w_out, v_norm2_g, v_w_up, v_ffn_conv_w, v_ffn_conv_b, v_w_down):
    BL, S, _ = x.shape
    T = BL * S
    me = 4 * lax.axis_index("x") + 2 * lax.axis_index("y") + lax.axis_index("c")
    xt = x.reshape(T, D)
    target = loss_target.reshape(T, D)

    big = dict(w_in=(w_in[0], m_w_in[0], v_w_in[0]), w_up=(w_up[0], m_w_up[0], v_w_up[0]),
               w_conv_out=(w_conv_out[0], m_w_conv_out[0], v_w_conv_out[0]),
               w_attn_out=(w_attn_out[0], m_w_attn_out[0], v_w_attn_out[0]),
               w_out=(w_out[0], m_w_out[0], v_w_out[0]), w_down=(w_down[0], m_w_down[0], v_w_down[0]))
    order = ["w_in", "w_conv_out", "w_attn_out", "w_out", "w_up", "w_down"]
    shards = [(big[n][0].T if n in ("w_in", "w_up") else big[n][0]).astype(bf16) for n in order]
    gathered = _allgather_rows(shards, 1)
    W = {"w_in": gathered[0].reshape(-1, D)}

    def place_cols(shard, full_cols):
        z = jnp.zeros((shard.shape[0], full_cols), f32)
        return lax.dynamic_update_slice(z, shard, (0, me * shard.shape[1]))

    zr = lambda a: jnp.zeros_like(a)
    conv_local = _pack_small(
        zr(norm1_g), zr(gate_b), place_cols(conv_w[0], D), zr(conv_b), zr(conv_norm_g), zr(q_norm_g), zr(k_norm_g),
        zr(norm2_g), place_cols(ffn_conv_w[0], 2 * D_FF), zr(ffn_conv_b))
    ga_conv = _small_start("gather_conv_start", conv_local, after=gathered[0])
    ga_proj = _exchange_start("gather_start_proj", shards[1:4], gathered[1:4], after=ga_conv[4])
    ga_ffn = _exchange_start("gather_start_ffn", shards[4:6], gathered[4:6], after=ga_proj[4])

    bd = (jnp.arange(128)[:, None] // HEAD_DIM == jnp.arange(128)[None, :] // HEAD_DIM).astype(bf16)
    bias = _attn_bias()
    qg = jnp.tile(q_norm_g, (1, N_HEADS))
    kg = jnp.tile(k_norm_g, (1, N_HEADS))

    z8, h, qn, kn = _in_proj_fwd(xt, norm1_g, W["w_in"], qg, kg, bd, ga_ffn[4])
    conv_all = _unpack_small(_small_sum("gather_conv", me.reshape(1), ga_conv, z8))
    conv_w_full, ffn_w_full = conv_all["conv_w"], conv_all["ffn_conv_w"]
    c = _conv_fwd(z8, conv_w_full, conv_b, S)
    o, ob, lse = _attn_fwd(qn, kn, z8, bias, S)
    for n, g in zip(order[1:4], _exchange_wait("gather_wait_proj", ga_proj, ob)[1]):
        W[n] = g.reshape(-1, D)
    s, ya, yb, mixed = _branches_fwd(c, ob, z8, conv_norm_g, gate_b, W["w_conv_out"], W["w_attn_out"])
    x1, h2 = _out_norm2_fwd(mixed, W["w_out"], xt, norm2_g)
    for n, g in zip(order[4:6], _exchange_wait("gather_wait_ffn", ga_ffn, x1)[1]):
        W[n] = g.reshape(-1, D)
    TNU = D_FF // 2
    u3 = _matmul_call(
        "mm_u", h2, W["w_up"],
        pl.BlockSpec((1024, D), lambda i, j, k: (i, 0)),
        pl.BlockSpec((TNU, D), lambda i, j, k: (j, 0)),
        pl.BlockSpec((None, 1024, TNU), lambda i, j, k: (j // 2, i, j % 2)),
        jax.ShapeDtypeStruct((2, T, D_FF), f32), (T // 1024, 4, 1), "nt", 1, 1024, TNU)
    f = _ffn_fwd(u3, ffn_w_full, ffn_conv_b, S)
    dy, dyb, lacc = _down_loss_fwd(f, W["w_down"], x1, target)
    loss_local = 0.5 / D * jnp.sum(lacc)

    df = _matmul("mm_df", dyb, W["w_down"], "nt", f32, tn=TNU)
    g_w_down = _matmul("mm_dwdn", f, dyb, "tn", bf16, tm=TNU)
    du3, dffn = _ffn_bwd(u3, df, ffn_w_full, ffn_conv_b, S)
    g_w_up = _matmul_call(
        "mm_dwup", du3, h2,
        pl.BlockSpec((None, T, TNU), lambda i, j, k: (i // 2, 0, i % 2)),
        pl.BlockSpec((T, D), lambda i, j, k: (0, 0)),
        pl.BlockSpec((TNU, D), lambda i, j, k: (i, 0)),
        jax.ShapeDtypeStruct((2 * D_FF, D), bf16), (4, 1, 1), "tn", 1, TNU, D)
    blocks8 = lambda a: a.reshape(N_DEV, -1, D)
    ex_ffn = _exchange_start("scatter_start_ffn", [blocks8(g_w_up), blocks8(g_w_down)])
    dx1, dx1b, dg_norm2 = _up_norm2_bwd(du3, W["w_up"], x1, dy, norm2_g, ex_ffn[4])
    g_w_out = _matmul("mm_dwo", mixed, dx1b, "tn", bf16, tm=512)
    dz8 = lax.empty((8, T, D), bf16)
    dya, dyb2, dz8, dg_gate = _out_gate_bwd(dx1b, W["w_out"], z8, gate_b, ya, yb, dz8)
    g_w_conv_out = _matmul("mm_dwco", s, dya, "tn", bf16, tm=512)
    g_w_attn_out = _matmul("mm_dwao", ob, dyb2, "tn", bf16, tm=512)
    ex_proj = _exchange_start("scatter_start_proj", [blocks8(g_w_conv_out), blocks8(g_w_attn_out), blocks8(g_w_out)])
    do = _matmul("mm_do", dyb2, W["w_attn_out"], "nt", f32, after=ex_proj[4])
    dc, dg_convnorm = _convnorm_bwd(dya, W["w_conv_out"], c, conv_norm_g)
    dz8a, dconv = _conv_bwd(dc, z8, conv_w_full, dz8, S)
    dz8b, dg_q, dg_k = _attn_bwd(qn, kn, z8, do, o, lse, bias, bd, qg, kg, dz8a, S)
    g_w_in = _matmul_call(
        "mm_dwin", dz8b, h,
        pl.BlockSpec((None, T, D), lambda i, j, k: (jnp.where(i < 2, i, jnp.where(i < 5, i + 2, i - 3)), 0, 0)),
        pl.BlockSpec((T, D), lambda i, j, k: (0, 0)), pl.BlockSpec((1024, D), lambda i, j, k: (i, 0)),
        jax.ShapeDtypeStruct((7 * D, D), bf16), (7, 1, 1), "tn", 1, D, D)
    ex_in = _exchange_start("scatter_start_in", [blocks8(g_w_in)])
    grad_x, dg_norm1 = _in_norm1_bwd(dz8b, W["w_in"], xt, dx1, norm1_g, ex_in[4])

    sum8 = lambda a: a.reshape(-1, 8, a.shape[-1]).sum(axis=1)
    dconv_s = sum8(dconv.sum(axis=0))
    dffn_s = dffn.sum(axis=0).reshape(2, 4, 8, D_FF).sum(axis=2)
    dffn_w = jnp.concatenate([dffn_s[0, :3], dffn_s[1, :3]], axis=1)
    dffn_b = jnp.concatenate([dffn_s[0, 3:4], dffn_s[1, 3:4]], axis=1)
    fold = lambda a: sum8(a).reshape(N_HEADS, HEAD_DIM).sum(axis=0)[None]
    small_g_local = _pack_small(
        sum8(dg_norm1), sum8(dg_gate), dconv_s[:CONV_WIDTH], dconv_s[CONV_WIDTH:], sum8(dg_convnorm),
        fold(dg_q), fold(dg_k), sum8(dg_norm2), dffn_w, dffn_b,
        last_row=jnp.pad(loss_local.reshape(1, 1), ((0, 0), (0, D - 1))))
    sg_start = _small_start("small_grads_start", small_g_local)

    place_m = lambda a, full: place_cols(a[0], full)
    small_w_true = _pack_small(norm1_g, gate_b, conv_w_full, conv_b, conv_norm_g, q_norm_g, k_norm_g, norm2_g,
                               ffn_w_full, ffn_conv_b)
    small_m = _pack_small(m_norm1_g, m_gate_b, place_m(m_conv_w, D), m_conv_b, m_conv_norm_g, m_q_norm_g, m_k_norm_g,
                          m_norm2_g, place_m(m_ffn_conv_w, 2 * D_FF), m_ffn_conv_b)
    small_v = _pack_small(v_norm1_g, v_gate_b, place_m(v_conv_w, D), v_conv_b, v_conv_norm_g, v_q_norm_g, v_k_norm_g,
                          v_norm2_g, place_m(v_ffn_conv_w, 2 * D_FF), v_ffn_conv_b)

    own, slots = {}, {}
    for tag, ex, names_ in (("ffn", ex_ffn, ("w_up", "w_down")),
                            ("proj", ex_proj, ("w_conv_out", "w_attn_out", "w_out")), ("in", ex_in, ("w_in",))):
        sent, landed = _exchange_wait("scatter_wait_" + tag, ex, [sg_start[4], small_w_true, small_m, small_v])
        for n, src, land in zip(names_, sent, landed):
            own[n], slots[n] = src, land

    res, adam_done = {}, []
    for n in order:
        w, m, v = big[n]
        outs = _adam_slots("adam_" + n, me.reshape(1), slots[n], own[n], w, m, v, _ADAM_TILE[slots[n].shape[1]],
                           transposed=n in ("w_in", "w_up"))
        adam_done.append(outs[0])
        res[n] = [a[None] for a in outs]
    small_g = _small_sum("small_grads", me.reshape(1), sg_start, adam_done)
    loss = small_g[_small_offsets()["last"], 0]

    for n, four in _adam_small(me.reshape(1), small_g, small_w_true, small_m, small_v).items():
        res[n] = [a[None] for a in four] if n in ("conv_w", "ffn_conv_w") else four

    names = ["norm1_g", "w_in", "gate_b", "conv_w", "conv_b", "conv_norm_g", "w_conv_out", "q_norm_g", "k_norm_g",
             "w_attn_out", "w_out", "norm2_g", "w_up", "ffn_conv_w", "ffn_conv_b", "w_down"]
    out = [loss, grad_x.reshape(BL, S, D)]
    for i in range(4):
        out += [res[n][i] for n in names]
    return tuple(out)
```

```python
import functools

import jax
import jax.numpy as jnp
import numpy as np
from jax import lax
from jax.experimental import pallas as pl
from jax.experimental.pallas import tpu as pltpu

f32 = jnp.float32
bf16 = jnp.bfloat16

D = 1024
N_HEADS = 16
HEAD_DIM = 64
CONV_WIDTH = 31
D_FF = 2816
GROUPS = ((128, 1), (512, 4), (2048, 16))
ATTN_BLOCK = 128
EPS = 1e-6
N_DEV = 8
MESH = pl.DeviceIdType.MESH

ADAM_LR = 0.001
ADAM_B1 = 0.9
ADAM_B2 = 0.999
ADAM_EPS = 1e-08
ADAM_WD = 0.01
ADAM_STEP = 10

VMEM_LIMIT = 56 * 1024 * 1024
MASK_BIAS = 1e30

Z_AVAL, Z_AGATE, Z_GA, Z_GB, Z_Q, Z_K, Z_V = 0, 1, 2, 3, 4, 5, 6


_W_OF_Z = (0, 1, 5, 6, 2, 3, 4)


def _wsec_of_zsec(j):
    return jnp.where(j < 2, j, jnp.where(j < 4, j + 3, j - 2))


def _sig(x):
    return 1.0 / (1.0 + jnp.exp(-x))


def _colsum8(x):
    return x.reshape(-1, 8, x.shape[-1]).sum(axis=0)


def _cparams(sem):
    return pltpu.CompilerParams(dimension_semantics=sem, vmem_limit_bytes=VMEM_LIMIT)


def _my_pos():
    x, y, c = lax.axis_index("x"), lax.axis_index("y"), lax.axis_index("c")
    return x, y, c, 4 * x + 2 * y + c


_DIMS = {"nn": ((1,), (0,)), "nt": ((1,), (1,)), "tn": ((0,), (0,))}


def _matmul_call(name, a, b, a_spec, b_spec, o_spec, out_shape, grid, mode, nk, tm, tn, after=None):
    dims = (_DIMS[mode], ((), ()))
    extra = [] if after is None else [after]

    def body(a_ref, b_ref, *rest):
        o_ref, scratch = rest[len(extra)], rest[len(extra) + 1:]
        part = lax.dot_general(a_ref[...], b_ref[...], dims, preferred_element_type=f32)
        if nk == 1:
            o_ref[...] = part.astype(o_ref.dtype)
        else:
            acc = scratch[0]
            k = pl.program_id(2)

            @pl.when(k == 0)
            def _():
                acc[...] = part

            @pl.when(k > 0)
            def _():
                acc[...] += part

            @pl.when(k == nk - 1)
            def _():
                o_ref[...] = acc[...].astype(o_ref.dtype)

    scratch = [] if nk == 1 else [pltpu.VMEM((tm, tn), f32)]
    return pl.pallas_call(
        body, name=name, grid=grid, in_specs=[a_spec, b_spec] + [pl.BlockSpec(memory_space=pl.ANY)] * len(extra),
        out_specs=o_spec, out_shape=out_shape,
        scratch_shapes=scratch, compiler_params=_cparams(("parallel", "parallel", "arbitrary")),
    )(a, b, *extra)


def _matmul(name, a, b, mode, out_dtype, tm=1024, tn=1024, tk=None, after=None):
    if mode == "nn":
        (M, K), (_, N) = a.shape, b.shape
    elif mode == "nt":
        (M, K), (N, _) = a.shape, b.shape
    else:
        (K, M), (_, N) = a.shape, b.shape
    tm, tn = min(tm, M), min(tn, N)
    tk = K if tk is None else tk
    nk = K // tk
    assert M % tm == 0 and N % tn == 0 and K % tk == 0
    if mode == "tn":
        a_spec = pl.BlockSpec((tk, tm), lambda i, j, k: (k, i))
    else:
        a_spec = pl.BlockSpec((tm, tk), lambda i, j, k: (i, k))
    if mode == "nt":
        b_spec = pl.BlockSpec((tn, tk), lambda i, j, k: (j, k))
    else:
        b_spec = pl.BlockSpec((tk, tn), lambda i, j, k: (k, j))
    o_spec = pl.BlockSpec((tm, tn), lambda i, j, k: (i, j))
    return _matmul_call(name, a, b, a_spec, b_spec, o_spec, jax.ShapeDtypeStruct((M, N), out_dtype),
                        (M // tm, N // tn, nk), mode, nk, tm, tn, after=after)


FTM = 512


def _matmul_fused(name, a, b, pairs, epilogue, extras, consts, outs, nt=False, sums=False, passed=(), aliases=None):
    sa, M, kk = a.shape
    na = max(i for i, _ in pairs) + 1
    ne, nc, npass = len(extras), len(consts), len(passed)
    dims = (_DIMS["nt" if nt else "nn"], ((), ()))

    def body(a_ref, b_ref, *rest):
        acc = None
        for i, j in pairs:
            part = lax.dot_general(a_ref[i], b_ref[j], dims, preferred_element_type=f32)
            acc = part if acc is None else acc + part
        epilogue(acc, rest[:ne], rest[ne:ne + nc], rest[ne + nc + npass:])

    whole = lambda arr: pl.BlockSpec(arr.shape, lambda i, nd=arr.ndim: (0,) * nd, pipeline_mode=pl.Buffered(1))
    io_alias = {2 + ne + nc + k: v for k, v in (aliases or {}).items()}
    return pl.pallas_call(
        body, name=name, grid=(M // FTM,),
        in_specs=[pl.BlockSpec((na, FTM, kk), lambda i: (0, i, 0)), whole(b)] + [s for _, s in extras]
        + [whole(c) for c in consts] + [pl.BlockSpec(memory_space=pl.ANY)] * npass,
        out_specs=[s for _, s in outs], out_shape=[s for s, _ in outs], input_output_aliases=io_alias,
        compiler_params=_cparams(("arbitrary" if sums else "parallel",)),
    )(a, b, *[x for x, _ in extras], *consts, *passed)


def _frows(c=D):
    return pl.BlockSpec((FTM, c), lambda i: (i, 0))


def _fsec(s):
    return pl.BlockSpec((None, FTM, D), lambda i: (s, i, 0))


def _rowshape(T, dtype, c=D):
    return (jax.ShapeDtypeStruct((T, c), dtype), _frows(c))


def _sumshape(c=D):
    return (jax.ShapeDtypeStruct((8, c), f32), pl.BlockSpec((8, c), lambda i: (0, 0)))


def _add_colsum(ref, x, cols=None):
    @pl.when(pl.program_id(0) == 0)
    def _():
        if cols is None:
            ref[...] = jnp.zeros_like(ref)
        else:
            ref[:, cols] = jnp.zeros((8, x.shape[-1]), f32)

    if cols is None:
        ref[...] += _colsum8(x)
    else:
        ref[:, cols] += _colsum8(x)


def _rms(x):
    return lax.rsqrt(jnp.mean(x * x, axis=-1, keepdims=True) + EPS)


def _rms_bwd(dy_g, xn, rstd):
    return rstd * (dy_g - xn * jnp.mean(dy_g * xn, axis=-1, keepdims=True))


def _head_sum(x, bd):
    parts = []
    for cb in range(x.shape[-1] // 128):
        xb = x[:, cb * 128:(cb + 1) * 128]
        hi = xb.astype(bf16)
        lo = (xb - hi.astype(f32)).astype(bf16)
        parts.append(jnp.dot(hi, bd, preferred_element_type=f32) + jnp.dot(lo, bd, preferred_element_type=f32))
    return parts[0] if len(parts) == 1 else jnp.concatenate(parts, axis=1)


ZTM = 1024


def _in_proj_fwd(x, g, w_in_t, qg, kg, bd, after):
    T = x.shape[0]
    nt = T // ZTM

    def body(x_ref, g_ref, w_ref, qg_ref, kg_ref, bd_ref, after_ref, z_ref, h_ref, qn_ref, kn_ref, hbuf):
        del after_ref
        j, i = pl.program_id(0), pl.program_id(1)
        rows = pl.ds(pl.multiple_of(i * ZTM, ZTM), ZTM)

        @pl.when(j == 0)
        def _():
            xv = x_ref[...]
            hv = (xv * _rms(xv) * g_ref[...]).astype(bf16)
            hbuf[rows, :] = hv
            h_ref[...] = hv

        def project():
            z = lax.dot_general(hbuf[rows, :], w_ref[...], (_DIMS["nt"], ((), ())), preferred_element_type=f32)
            z_ref[...] = z
            return z

        def head_norm(z, gain_ref, scale):
            return z * lax.rsqrt(_head_sum(z * z, bd_ref[...]) * (1.0 / HEAD_DIM) + EPS) * gain_ref[...] * scale

        @pl.when(j == Z_Q)
        def _():
            qn_ref[...] = head_norm(project(), qg_ref, HEAD_DIM ** -0.5)

        @pl.when(j == Z_K)
        def _():
            kn_ref[...] = head_norm(project(), kg_ref, 1.0)

        @pl.when((j != Z_Q) & (j != Z_K))
        def _():
            project()

    def tile_at(sec):
        return pl.BlockSpec((ZTM, D), lambda j, i: (jnp.where(j < sec, 0, jnp.where(j == sec, i, nt - 1)), 0))

    row = pl.BlockSpec((1, D), lambda j, i: (0, 0))
    return pl.pallas_call(
        body, name="mm_z", grid=(7, nt),
        in_specs=[tile_at(0), row, pl.BlockSpec((D, D), lambda j, i: (_wsec_of_zsec(j), 0)), row, row,
                  pl.BlockSpec((128, 128), lambda j, i: (0, 0)), pl.BlockSpec(memory_space=pl.ANY)],
        out_specs=[pl.BlockSpec((None, ZTM, D), lambda j, i: (j, i, 0)), tile_at(0), tile_at(Z_Q), tile_at(Z_K)],
        out_shape=[jax.ShapeDtypeStruct((8, T, D), f32), jax.ShapeDtypeStruct((T, D), bf16),
                   jax.ShapeDtypeStruct((T, D), f32), jax.ShapeDtypeStruct((T, D), f32)],
        scratch_shapes=[pltpu.VMEM((T, D), bf16)],
        compiler_params=_cparams(("arbitrary", "arbitrary")))(x, g, w_in_t, qg, kg, bd, after)


def _branches_fwd(c, ob, z8, g, gate_b, w_conv_out, w_attn_out):
    T = c.shape[0]

    def epilogue(yb, extra, const, out):
        cv = extra[0][...]
        r = cv * _rms(cv) * const[0][...]
        s = (r * _sig(r)).astype(bf16)
        ya = jnp.dot(s, const[2][...], preferred_element_type=f32)
        b_ref = const[1]
        g_a = _sig(extra[1][...] + b_ref[:, :D])
        g_b = _sig(extra[2][...] + b_ref[:, D:])
        out[0][...] = s
        out[1][...] = ya
        out[2][...] = yb
        out[3][...] = (g_a * ya + g_b * yb).astype(bf16)

    return _matmul_fused("mm_branches", ob[None], w_attn_out[None], ((0, 0),), epilogue,
                         [(c, _frows()), (z8, _fsec(Z_GA)), (z8, _fsec(Z_GB))], [g, gate_b, w_conv_out],
                         [_rowshape(T, bf16), _rowshape(T, f32), _rowshape(T, f32), _rowshape(T, bf16)])


def _out_norm2_fwd(mixed, w_out, x, g):
    T = x.shape[0]

    def epilogue(acc, extra, const, out):
        x1 = extra[0][...] + acc
        out[0][...] = x1
        out[1][...] = (x1 * _rms(x1) * const[0][...]).astype(bf16)

    return _matmul_fused("mm_t1_norm2", mixed[None], w_out[None], ((0, 0),), epilogue, [(x, _frows())], [g],
                         [_rowshape(T, f32), _rowshape(T, bf16)])


def _down_loss_fwd(f, w_down, x1, target):
    T = x1.shape[0]

    def epilogue(acc, extra, const, out):
        diff = extra[0][...] + acc - extra[1][...]
        dy = diff * (1.0 / D)
        out[0][...] = dy
        out[1][...] = dy.astype(bf16)
        _add_colsum(out[2], diff * diff)

    return _matmul_fused("mm_t2_loss", f[None], w_down[None], ((0, 0),), epilogue, [(x1, _frows()), (target, _frows())],
                         [], [_rowshape(T, f32), _rowshape(T, bf16), _sumshape()], sums=True)


def _up_norm2_bwd(du3, w_up_t, x1, dy, g, token):
    T = x1.shape[0]

    def epilogue(dh, extra, const, out):
        x1v = extra[0][...]
        rstd = _rms(x1v)
        xn = x1v * rstd
        dx1 = extra[1][...] + _rms_bwd(dh * const[0][...], xn, rstd)
        out[0][...] = dx1
        out[1][...] = dx1.astype(bf16)
        _add_colsum(out[2], dh * xn)

    return _matmul_fused("mm_dh2_norm2", du3, w_up_t.reshape(2, D_FF, D), ((0, 0), (1, 1)), epilogue,
                         [(x1, _frows()), (dy, _frows())], [g],
                         [_rowshape(T, f32), _rowshape(T, bf16), _sumshape()], sums=True, passed=[token])


def _out_gate_bwd(dx1b, w_out, z8, gate_b, ya, yb, dz8):
    T = ya.shape[0]

    def epilogue(dm, extra, const, out):
        b_ref = const[0]
        g_a = _sig(extra[0][...] + b_ref[:, :D])
        g_b = _sig(extra[1][...] + b_ref[:, D:])
        out[0][...] = (dm * g_a).astype(bf16)
        out[1][...] = (dm * g_b).astype(bf16)
        dla = dm * extra[2][...] * g_a * (1.0 - g_a)
        dlb = dm * extra[3][...] * g_b * (1.0 - g_b)
        out[2][0] = dla.astype(bf16)
        out[2][1] = dlb.astype(bf16)
        _add_colsum(out[3], dla, slice(0, D))
        _add_colsum(out[3], dlb, slice(D, 2 * D))

    return _matmul_fused(
        "mm_dmixed_gate", dx1b[None], w_out[None], ((0, 0),), epilogue,
        [(z8, _fsec(Z_GA)), (z8, _fsec(Z_GB)), (ya, _frows()), (yb, _frows())], [gate_b],
        [_rowshape(T, bf16), _rowshape(T, bf16),
         (jax.ShapeDtypeStruct(dz8.shape, bf16), pl.BlockSpec((2, FTM, D), lambda i: (1, i, 0))), _sumshape(2 * D)],
        nt=True, sums=True, passed=[dz8], aliases={0: 2})


def _convnorm_bwd(dya, w_conv_out, c, g):
    T = c.shape[0]

    def epilogue(ds, extra, const, out):
        cv = extra[0][...]
        rstd = _rms(cv)
        r0 = cv * rstd
        gv = const[0][...]
        r = r0 * gv
        sg = _sig(r)
        dr = ds * sg * (1.0 + r * (1.0 - sg))
        out[0][...] = _rms_bwd(dr * gv, r0, rstd)
        _add_colsum(out[1], dr * r0)

    return _matmul_fused("mm_ds_convnorm", dya[None], w_conv_out[None], ((0, 0),), epilogue, [(c, _frows())], [g],
                         [_rowshape(T, f32), _sumshape()], nt=True, sums=True)


def _in_norm1_bwd(dz8, w_in_t, x, dx1, g, token):
    T = x.shape[0]

    def epilogue(dh, extra, const, out):
        xv = extra[0][...]
        rstd = _rms(xv)
        xn = xv * rstd
        out[0][...] = extra[1][...] + _rms_bwd(dh * const[0][...], xn, rstd)
        _add_colsum(out[1], dh * xn)

    return _matmul_fused("mm_dh_norm1", dz8, w_in_t.reshape(7, D, D), tuple(zip(range(7), _W_OF_Z)), epilogue,
                         [(x, _frows()), (dx1, _frows())], [g], [_rowshape(T, f32), _sumshape()],
                         sums=True, passed=[token])


CCW = 256
CR = 64
HALO = 32


def _conv_fwd(z8, conv_w, conv_b, S):
    T = z8.shape[1]
    nb = T // S
    ncb = D // CCW

    def body(av_ref, ag_ref, w_ref, b_ref, c_ref, pad):
        pad[0:HALO, :] = jnp.zeros((HALO, CCW), f32)

        def fill(i, carry):
            r0 = pl.multiple_of(i * 256, 256)
            pad[pl.ds(HALO + r0, 256), :] = av_ref[pl.ds(r0, 256), :] * _sig(ag_ref[pl.ds(r0, 256), :])
            return carry

        lax.fori_loop(0, S // 256, fill, 0)
        bias = b_ref[...]

        def chunk(i, carry):
            r0 = pl.multiple_of(i * CR, CR)
            win = pad[pl.ds(r0, CR + HALO), :]
            acc = jnp.zeros((CR, CCW), f32) + bias
            for s in range(8):
                part = None
                for m in range((CONV_WIDTH - 1 - s) // 8 + 1):
                    j = CONV_WIDTH - 1 - 8 * m - s
                    term = win[24 - 8 * m:24 - 8 * m + CR + 8, :] * w_ref[j:j + 1, :]
                    part = term if part is None else part + term
                acc = acc + part[8 - s:8 - s + CR, :]
            c_ref[pl.ds(r0, CR), :] = acc
            return carry

        lax.fori_loop(0, S // CR, chunk, 0)

    zs = lambda s: pl.BlockSpec((None, S, CCW), lambda b, cb: (s, b, cb))
    return pl.pallas_call(
        body, name="conv_fwd", grid=(nb, ncb),
        in_specs=[zs(Z_AVAL), zs(Z_AGATE), pl.BlockSpec((CONV_WIDTH, CCW), lambda b, cb: (0, cb)),
                  pl.BlockSpec((1, CCW), lambda b, cb: (0, cb))],
        out_specs=pl.BlockSpec((S, CCW), lambda b, cb: (b, cb)),
        out_shape=jax.ShapeDtypeStruct((T, D), f32),
        scratch_shapes=[pltpu.VMEM((S + HALO, CCW), f32)],
        compiler_params=_cparams(("parallel", "parallel")))(z8, z8, conv_w, conv_b)


def _conv_bwd(dc, z8, conv_w, dz8, S):
    T = dc.shape[0]
    nb = T // S
    ncb = D // CCW

    def body(dc_ref, av_ref, ag_ref, w_ref, dz_in, dz_ref, dw_ref, apad, dpad, shbuf):
        del dz_in
        apad[0:HALO, :] = jnp.zeros((HALO, CCW), f32)
        dpad[S:S + HALO, :] = jnp.zeros((HALO, CCW), f32)
        dw_ref[...] = jnp.zeros_like(dw_ref)

        def fill(i, carry):
            r0 = pl.multiple_of(i * 256, 256)
            apad[pl.ds(HALO + r0, 256), :] = av_ref[pl.ds(r0, 256), :] * _sig(ag_ref[pl.ds(r0, 256), :])
            dpad[pl.ds(r0, 256), :] = dc_ref[pl.ds(r0, 256), :]
            return carry

        lax.fori_loop(0, S // 256, fill, 0)

        def chunk(i, carry):
            r0 = pl.multiple_of(i * CR, CR)
            dwin = dpad[pl.ds(r0, CR + HALO), :]
            da = jnp.zeros((CR, CCW), f32)
            for s in range(8):
                shbuf[...] = dwin[s:s + CR, :]
                dshift = shbuf[...]
                part = None
                for m in range((CONV_WIDTH - 1 - s) // 8 + 1):
                    j = CONV_WIDTH - 1 - 8 * m - s
                    term = dwin[8 * m:8 * m + CR + 8, :] * w_ref[j:j + 1, :]
                    part = term if part is None else part + term
                    a_lag = apad[pl.ds(r0 + HALO - 8 * m, CR), :]
                    dw_ref[8 * j:8 * j + 8, :] += _colsum8(dshift * a_lag)
                da = da + part[s:s + CR, :]
            dw_ref[8 * CONV_WIDTH:8 * CONV_WIDTH + 8, :] += _colsum8(dwin[0:CR, :])
            av = av_ref[pl.ds(r0, CR), :]
            sg = _sig(ag_ref[pl.ds(r0, CR), :])
            dz_ref[0, pl.ds(r0, CR), :] = (da * sg).astype(bf16)
            dz_ref[1, pl.ds(r0, CR), :] = (da * av * sg * (1.0 - sg)).astype(bf16)
            return carry

        lax.fori_loop(0, S // CR, chunk, 0)

    zs = lambda s: pl.BlockSpec((None, S, CCW), lambda b, cb: (s, b, cb))
    return pl.pallas_call(
        body, name="conv_bwd", grid=(nb, ncb),
        in_specs=[pl.BlockSpec((S, CCW), lambda b, cb: (b, cb)), zs(Z_AVAL), zs(Z_AGATE),
                  pl.BlockSpec((CONV_WIDTH, CCW), lambda b, cb: (0, cb)), pl.BlockSpec(memory_space=pl.ANY)],
        out_specs=[pl.BlockSpec((2, S, CCW), lambda b, cb: (0, b, cb)),
                   pl.BlockSpec((None, 256, CCW), lambda b, cb: (b, 0, cb))],
        out_shape=[jax.ShapeDtypeStruct(dz8.shape, bf16), jax.ShapeDtypeStruct((nb, 256, D), f32)],
        input_output_aliases={4: 0},
        scratch_shapes=[pltpu.VMEM((S + HALO, CCW), f32), pltpu.VMEM((S + HALO, CCW), f32),
                        pltpu.VMEM((CR, CCW), f32)],
        compiler_params=_cparams(("parallel", "parallel")))(dc, z8, z8, conv_w, dz8)


FR = 128
NFB = D_FF // CCW
FBW = 128


def _ffn_window(ref, i, r0):
    return ref[pl.ds(r0 - 8, FR + 8), :]


def _ffn_u(win, w_ref, b_ref):
    return (win[6:6 + FR, :] * w_ref[0:1, :] + win[7:7 + FR, :] * w_ref[1:2, :]
            + win[8:8 + FR, :] * w_ref[2:3, :] + b_ref[...])


def _ffn_fwd(u3, ffn_w, ffn_b, S):
    T = u3.shape[1]
    nb = T // S

    def body(uv_ref, ug_ref, wv_ref, wg_ref, bv_ref, bg_ref, f_ref):
        def chunk(first, i):
            r0 = 0 if first else pl.multiple_of(i * FR, FR)
            if first:
                z = jnp.zeros((8, CCW), f32)
                wv = jnp.concatenate([z, uv_ref[0:FR, :]], axis=0)
                wg = jnp.concatenate([z, ug_ref[0:FR, :]], axis=0)
            else:
                wv = _ffn_window(uv_ref, i, r0)
                wg = _ffn_window(ug_ref, i, r0)
            u_val = _ffn_u(wv, wv_ref, bv_ref)
            u_gate = _ffn_u(wg, wg_ref, bg_ref)
            f_ref[pl.ds(r0, FR), :] = (u_gate * _sig(u_gate) * u_val).astype(bf16)

        chunk(True, 0)

        def loop(i, carry):
            chunk(False, i)
            return carry

        lax.fori_loop(1, S // FR, loop, 0)

    us = lambda h: pl.BlockSpec((None, S, CCW), lambda b, cb: (h, b, cb))
    ws = lambda h: pl.BlockSpec((3, CCW), lambda b, cb: (0, h * NFB + cb))
    bs = lambda h: pl.BlockSpec((1, CCW), lambda b, cb: (0, h * NFB + cb))
    return pl.pallas_call(
        body, name="ffn_fwd", grid=(nb, NFB),
        in_specs=[us(0), us(1), ws(0), ws(1), bs(0), bs(1)],
        out_specs=pl.BlockSpec((S, CCW), lambda b, cb: (b, cb)),
        out_shape=jax.ShapeDtypeStruct((T, D_FF), bf16),
        compiler_params=_cparams(("parallel", "parallel")))(u3, u3, ffn_w, ffn_w, ffn_b, ffn_b)


def _ffn_bwd(u3, df, ffn_w, ffn_b, S):
    T = u3.shape[1]
    nb = T // S

    def body(uv_ref, ug_ref, df_ref, wv_ref, wg_ref, bv_ref, bg_ref, du_ref, dw_ref, dvpad, dgpad, shbuf):
        dvpad[S:S + 8, :] = jnp.zeros((8, FBW), f32)
        dgpad[S:S + 8, :] = jnp.zeros((8, FBW), f32)
        dw_ref[...] = jnp.zeros_like(dw_ref)

        def chunk(first, i):
            r0 = 0 if first else pl.multiple_of(i * FR, FR)
            if first:
                z = jnp.zeros((8, FBW), f32)
                wv = jnp.concatenate([z, uv_ref[0:FR, :]], axis=0)
                wg = jnp.concatenate([z, ug_ref[0:FR, :]], axis=0)
            else:
                wv = _ffn_window(uv_ref, i, r0)
                wg = _ffn_window(ug_ref, i, r0)
            taps = []
            for h, win in enumerate((wv, wg)):
                shbuf[2 * h] = win[6:6 + FR, :]
                shbuf[2 * h + 1] = win[7:7 + FR, :]
                taps.append((shbuf[2 * h], shbuf[2 * h + 1], win[8:8 + FR, :]))
            conv = lambda x, w_ref, b_ref: (x[0] * w_ref[0:1, :] + x[1] * w_ref[1:2, :] + x[2] * w_ref[2:3, :]
                                            + b_ref[...])
            u_val = conv(taps[0], wv_ref, bv_ref)
            u_gate = conv(taps[1], wg_ref, bg_ref)
            dfc = df_ref[pl.ds(r0, FR), :]
            sg = _sig(u_gate)
            d_val = dfc * u_gate * sg
            d_gate = dfc * u_val * sg * (1.0 + u_gate * (1.0 - sg))
            dvpad[pl.ds(r0, FR), :] = d_val
            dgpad[pl.ds(r0, FR), :] = d_gate
            for h, dd in enumerate((d_val, d_gate)):
                for j in range(3):
                    dw_ref[h, 8 * j:8 * j + 8, :] += _colsum8(dd * taps[h][j])
                dw_ref[h, 24:32, :] += _colsum8(dd)

        chunk(True, 0)

        def loop(i, carry):
            chunk(False, i)
            return carry

        lax.fori_loop(1, S // FR, loop, 0)

        def back(i, carry):
            r0 = pl.multiple_of(i * FR, FR)
            for h, (dpad, w_ref) in enumerate(((dvpad, wv_ref), (dgpad, wg_ref))):
                win = dpad[pl.ds(r0, FR + 8), :]
                du = (win[0:FR, :] * w_ref[2:3, :] + win[1:1 + FR, :] * w_ref[1:2, :]
                      + win[2:2 + FR, :] * w_ref[0:1, :])
                du_ref[h, pl.ds(r0, FR), :] = du.astype(bf16)
            return carry

        lax.fori_loop(0, S // FR, back, 0)

    ncb = D_FF // FBW
    us = lambda h: pl.BlockSpec((None, S, FBW), lambda b, cb: (h, b, cb))
    ws = lambda h: pl.BlockSpec((3, FBW), lambda b, cb: (0, h * ncb + cb))
    bs = lambda h: pl.BlockSpec((1, FBW), lambda b, cb: (0, h * ncb + cb))
    return pl.pallas_call(
        body, name="ffn_bwd", grid=(nb, ncb),
        in_specs=[us(0), us(1), pl.BlockSpec((S, FBW), lambda b, cb: (b, cb)), ws(0), ws(1), bs(0), bs(1)],
        out_specs=[pl.BlockSpec((2, S, FBW), lambda b, cb: (0, b, cb)),
                   pl.BlockSpec((None, 2, 32, FBW), lambda b, cb: (b, 0, 0, cb))],
        out_shape=[jax.ShapeDtypeStruct((2, T, D_FF), bf16), jax.ShapeDtypeStruct((nb, 2, 32, D_FF), f32)],
        scratch_shapes=[pltpu.VMEM((S + 8, FBW), f32), pltpu.VMEM((S + 8, FBW), f32),
                        pltpu.VMEM((4, FR, FBW), f32)],
        compiler_params=_cparams(("parallel", "parallel")))(u3, u3, df, ffn_w, ffn_w, ffn_b, ffn_b)


AB = ATTN_BLOCK


def _attn_bias_np():
    slopes = (np.float32(2.0) ** (np.float32(-8.0) * np.arange(1, N_HEADS + 1, dtype=np.float32)
                                  / np.float32(N_HEADS))).astype(np.float32)
    steps = (np.arange(AB)[:, None] + AB) - np.arange(2 * AB)[None, :]
    own = (np.arange(2 * AB) >= AB)[None, :]
    out = []
    for window, dil in GROUPS:
        valid = (steps >= 0) & (steps <= window // dil)
        dist = slopes[:, None, None] * (steps * dil).astype(np.float32)[None]
        kinds = [np.where(v[None], dist, np.float32(MASK_BIAS)) for v in (valid, valid & own)]
        out.append(np.stack(kinds, axis=1))
    return np.stack(out).astype(np.float32)


def _attn_bias():
    return jnp.asarray(_attn_bias_np())


def _head_masks():
    lane = lax.broadcasted_iota(jnp.int32, (1, 128), 1)
    return (lane < HEAD_DIM, lane >= HEAD_DIM)


def _perm_chunks(S, d):
    L = S // d
    ch = min(L, 256)
    out = []
    for r in range(d):
        for c in range(L // ch):
            start = r + d * ch * c
            out.append((pl.ds(start, ch, stride=d) if d > 1 else pl.ds(start, ch), r * L + c * ch, ch))
    return out


def _stack_heads(x, masks):
    return jnp.concatenate([jnp.where(masks[0], x, 0), jnp.where(masks[1], x, 0)], axis=0)


def _block_row(j):
    return j * AB if isinstance(j, int) else pl.multiple_of(j * AB, AB)


def _three_stages(n, stage_a, stage_b, stage_c, unroll):
    stage_a(0)
    stage_a(1)
    stage_b(0)

    def body(j, carry):
        stage_c(j - 1)
        stage_b(j)
        stage_a(j + 1)
        return carry

    lax.fori_loop(1, n - 1, body, 0, unroll=unroll)
    stage_c(n - 2)
    stage_b(n - 1)
    stage_c(n - 1)


_NT = (((1,), (1,)), ((), ()))
_TN = (((0,), (0,)), ((), ()))
SCH = 128


def _attn_fwd(qn, kn, z8, bias, S):
    T = qn.shape[0]
    nb = T // S
    nblk = S // AB

    def body(q_ref, k_ref, v_ref, bias_ref, o_ref, ob_ref, lse_ref, qs, ks, vs, s2, p2, ogp, lgp, *group_scratch):
        og, lg = group_scratch[:3], group_scratch[3:]
        masks = _head_masks()
        ks[0:AB, :] = jnp.zeros((AB, 128), bf16)
        vs[0:AB, :] = jnp.zeros((AB, 128), bf16)

        for g, (_, d) in enumerate(GROUPS):
            nsub = S // (d * AB)
            chunks = _perm_chunks(S, d)
            for src, dst, ch in chunks:
                qs[dst:dst + ch, :] = q_ref[src, :].astype(bf16)
                ks[AB + dst:AB + dst + ch, :] = k_ref[src, :].astype(bf16)
                vs[AB + dst:AB + dst + ch, :] = v_ref[src, :].astype(bf16)
            od, ld = (og[g], lg[g]) if d == 1 else (ogp, lgp)

            def scores(j):
                r0 = _block_row(j)
                q2 = _stack_heads(qs[pl.ds(r0, AB), :], masks)
                s2[j] = lax.dot_general(q2, ks[pl.ds(r0, 2 * AB), :], _NT, preferred_element_type=f32)

            def softmax(j, g=g, nsub=nsub, ld=ld):
                r0 = _block_row(j)
                kind = int(j % nsub == 0) if isinstance(j, int) else (j % nsub == 0).astype(jnp.int32)
                for cc in range(AB // SCH):
                    lses = []
                    for hh in range(2):
                        rows = pl.ds(hh * AB + cc * SCH, SCH)
                        sb = s2[j, rows, :] - bias_ref[g, hh, kind, cc * SCH:(cc + 1) * SCH, :]
                        m = jnp.max(sb, axis=-1, keepdims=True)
                        p = jnp.exp(sb - m)
                        den = jnp.sum(p, axis=-1, keepdims=True)
                        p2[j, rows, :] = (p * (1.0 / den)).astype(bf16)
                        lses.append(m + jnp.log(den))
                    ld[pl.ds(r0 + cc * SCH, SCH), :] = jnp.where(masks[0], lses[0], lses[1])

            def values(j, od=od):
                r0 = _block_row(j)
                pv2 = jnp.dot(p2[j], vs[pl.ds(r0, 2 * AB), :], preferred_element_type=f32)
                od[pl.ds(r0, AB), :] = jnp.where(masks[0], pv2[:AB], pv2[AB:])

            _three_stages(nblk, scores, softmax, values, nblk - 2)

            if d > 1:
                for src, dst, ch in chunks:
                    og[g][src, :] = ogp[dst:dst + ch, :]
                    lg[g][src, :] = lgp[dst:dst + ch, :]

        def combine(i, carry):
            rr = pl.ds(pl.multiple_of(i * 256, 256), 256)
            l0, l1, l2 = lg[0][rr, :], lg[1][rr, :], lg[2][rr, :]
            mx = jnp.maximum(jnp.maximum(l0, l1), l2)
            e0, e1, e2 = jnp.exp(l0 - mx), jnp.exp(l1 - mx), jnp.exp(l2 - mx)
            den = e0 + e1 + e2
            o = (e0 * og[0][rr, :] + e1 * og[1][rr, :] + e2 * og[2][rr, :]) / den
            o_ref[rr, :] = o
            ob_ref[rr, :] = o.astype(bf16)
            lse_ref[rr, :] = mx + jnp.log(den)
            return carry

        lax.fori_loop(0, S // 256, combine, 0, unroll=True)

    blk = pl.BlockSpec((S, 128), lambda b, hp: (b, hp))
    return pl.pallas_call(
        body, name="attn_fwd", grid=(nb, N_HEADS // 2),
        in_specs=[blk, blk, pl.BlockSpec((None, S, 128), lambda b, hp: (Z_V, b, hp)),
                  pl.BlockSpec((3, 2, 2, AB, 2 * AB), lambda b, hp: (0, hp, 0, 0, 0))],
        out_specs=[blk, blk, blk],
        out_shape=[jax.ShapeDtypeStruct((T, D), f32), jax.ShapeDtypeStruct((T, D), bf16),
                   jax.ShapeDtypeStruct((T, D), f32)],
        scratch_shapes=[pltpu.VMEM((S, 128), bf16), pltpu.VMEM((S + AB, 128), bf16), pltpu.VMEM((S + AB, 128), bf16),
                        pltpu.VMEM((nblk, 2 * AB, 2 * AB), f32), pltpu.VMEM((nblk, 2 * AB, 2 * AB), bf16),
                        pltpu.VMEM((S, 128), f32), pltpu.VMEM((S, 128), f32)] + [pltpu.VMEM((S, 128), f32)] * 6,
        compiler_params=_cparams(("parallel", "parallel")))(qn, kn, z8, bias)


def _attn_bwd(qn, kn, z8, do, o, lse, bias, bd, qg, kg, dz8, S):
    T = qn.shape[0]
    nb = T // S

    nblk = S // AB

    def body(q_ref, k_ref, v_ref, do_ref, o_ref, lse_ref, bias_ref, bd_ref, qraw_ref, kraw_ref, qg_ref, kg_ref,
             dz_in, dz_ref, dqg_ref, dkg_ref,
             dq_ref, dk_ref, dv_ref, delta, qs, ks, vs, dos, lsp, dlp, s2, dp2, p2, ds2, dqp, dkp, dvp):
        del dz_in
        masks = _head_masks()
        bdv = bd_ref[...]
        dq_ref[...] = jnp.zeros_like(dq_ref)
        dk_ref[...] = jnp.zeros_like(dk_ref)
        dv_ref[...] = jnp.zeros_like(dv_ref)
        ks[0:AB, :] = jnp.zeros((AB, 128), bf16)
        vs[0:AB, :] = jnp.zeros((AB, 128), bf16)

        def prep(i, carry):
            rr = pl.ds(pl.multiple_of(i * 256, 256), 256)
            delta[rr, :] = _head_sum(do_ref[rr, :] * o_ref[rr, :], bdv)
            return carry

        lax.fori_loop(0, S // 256, prep, 0, unroll=True)

        for g, (_, d) in enumerate(GROUPS):
            nsub = S // (d * AB)
            chunks = _perm_chunks(S, d)
            for src, dst, ch in chunks:
                qs[dst:dst + ch, :] = q_ref[src, :].astype(bf16)
                ks[AB + dst:AB + dst + ch, :] = k_ref[src, :].astype(bf16)
                vs[AB + dst:AB + dst + ch, :] = v_ref[src, :].astype(bf16)
                dos[dst:dst + ch, :] = do_ref[src, :].astype(bf16)
                lsp[dst:dst + ch, :] = lse_ref[src, :]
                dlp[dst:dst + ch, :] = delta[src, :]
            dkp[...] = jnp.zeros_like(dkp)
            dvp[...] = jnp.zeros_like(dvp)

            def scores(j):
                r0 = _block_row(j)
                q2 = _stack_heads(qs[pl.ds(r0, AB), :], masks)
                do2 = _stack_heads(dos[pl.ds(r0, AB), :], masks)
                s2[j] = lax.dot_general(q2, ks[pl.ds(r0, 2 * AB), :], _NT, preferred_element_type=f32)
                dp2[j] = lax.dot_general(do2, vs[pl.ds(r0, 2 * AB), :], _NT, preferred_element_type=f32)

            def probs(j, g=g, nsub=nsub):
                r0 = _block_row(j)
                kind = int(j % nsub == 0) if isinstance(j, int) else (j % nsub == 0).astype(jnp.int32)
                for cc in range(AB // SCH):
                    lse_c = lsp[pl.ds(r0 + cc * SCH, SCH), :]
                    del_c = dlp[pl.ds(r0 + cc * SCH, SCH), :]
                    for hh in range(2):
                        c0 = hh * HEAD_DIM
                        rows = pl.ds(hh * AB + cc * SCH, SCH)
                        sb = s2[j, rows, :] - bias_ref[g, hh, kind, cc * SCH:(cc + 1) * SCH, :]
                        p = jnp.exp(sb - lse_c[:, c0:c0 + 1])
                        p2[j, rows, :] = p.astype(bf16)
                        ds2[j, rows, :] = (p * (dp2[j, rows, :] - del_c[:, c0:c0 + 1])).astype(bf16)

            def grads(j):
                r0 = _block_row(j)
                q2 = _stack_heads(qs[pl.ds(r0, AB), :], masks)
                do2 = _stack_heads(dos[pl.ds(r0, AB), :], masks)
                dsb = ds2[j]
                t = jnp.dot(dsb, ks[pl.ds(r0, 2 * AB), :], preferred_element_type=f32)
                dqp[pl.ds(r0, AB), :] = jnp.where(masks[0], t[:AB], t[AB:])
                dkp[pl.ds(r0, 2 * AB), :] += lax.dot_general(dsb, q2, _TN, preferred_element_type=f32)
                dvp[pl.ds(r0, 2 * AB), :] += lax.dot_general(p2[j], do2, _TN, preferred_element_type=f32)

            _three_stages(nblk, scores, probs, grads, nblk - 2)

            for src, dst, ch in chunks:
                dq_ref[src, :] += dqp[dst:dst + ch, :]
                dk_ref[src, :] += dkp[AB + dst:AB + dst + ch, :]
                dv_ref[src, :] += dvp[AB + dst:AB + dst + ch, :]

        @pl.when(pl.program_id(1) == 0)
        def _():
            dqg_ref[...] = jnp.zeros_like(dqg_ref)
            dkg_ref[...] = jnp.zeros_like(dkg_ref)

        def norms(i, carry):
            rr = pl.ds(pl.multiple_of(i * 256, 256), 256)

            def one(raw, dn_scaled, g, dg_ref, sec):
                rstd = lax.rsqrt(_head_sum(raw * raw, bdv) * (1.0 / HEAD_DIM) + EPS)
                n = raw * rstd
                dg_ref[...] += _colsum8(dn_scaled * n)
                dn = dn_scaled * g
                draw = rstd * (dn - n * (_head_sum(dn * n, bdv) * (1.0 / HEAD_DIM)))
                dz_ref[sec, rr, :] = draw.astype(bf16)

            one(qraw_ref[rr, :], dq_ref[rr, :] * (HEAD_DIM ** -0.5), qg_ref[...], dqg_ref, 0)
            one(kraw_ref[rr, :], dk_ref[rr, :], kg_ref[...], dkg_ref, 1)
            dz_ref[2, rr, :] = dv_ref[rr, :].astype(bf16)
            dz_ref[3, rr, :] = jnp.zeros((256, 128), bf16)
            return carry

        lax.fori_loop(0, S // 256, norms, 0, unroll=True)

    blk = pl.BlockSpec((S, 128), lambda hp, b: (b, hp))
    sec = lambda s: pl.BlockSpec((None, S, 128), lambda hp, b: (s, b, hp))
    gain = pl.BlockSpec((1, 128), lambda hp, b: (0, hp))
    row = lambda dt, pad=0: pltpu.VMEM((S + pad, 128), dt)
    blocks = lambda dt: pltpu.VMEM((nblk, 2 * AB, 2 * AB), dt)
    return pl.pallas_call(
        body, name="attn_bwd", grid=(N_HEADS // 2, nb),
        in_specs=[blk, blk, sec(Z_V), blk, blk, blk,
                  pl.BlockSpec((3, 2, 2, AB, 2 * AB), lambda hp, b: (0, hp, 0, 0, 0)),
                  pl.BlockSpec((128, 128), lambda hp, b: (0, 0)), sec(Z_Q), sec(Z_K), gain, gain,
                  pl.BlockSpec(memory_space=pl.ANY)],
        out_specs=[pl.BlockSpec((4, S, 128), lambda hp, b: (1, b, hp)),
                   pl.BlockSpec((8, 128), lambda hp, b: (0, hp)), pl.BlockSpec((8, 128), lambda hp, b: (0, hp))],
        out_shape=[jax.ShapeDtypeStruct(dz8.shape, bf16), jax.ShapeDtypeStruct((8, D), f32),
                   jax.ShapeDtypeStruct((8, D), f32)],
        input_output_aliases={12: 0},
        scratch_shapes=[row(f32), row(f32), row(f32),
                        row(f32), row(bf16), row(bf16, AB), row(bf16, AB), row(bf16), row(f32), row(f32),
                        blocks(f32), blocks(f32), blocks(bf16), blocks(bf16), row(f32), row(f32, AB), row(f32, AB)],
        compiler_params=_cparams(("parallel", "arbitrary")))(qn, kn, z8, do, o, lse, bias, bd, z8, z8, qg, kg, dz8)


def _any_spec():
    return pl.BlockSpec(memory_space=pl.ANY)


def _allgather_rows(shards, n_full):
    n = len(shards)

    def body(*refs):
        ins, outs = refs[:n], refs[n:2 * n]
        send_sems, recv_sems, local_sems = refs[2 * n:]
        x, y, c, me = _my_pos()
        sibling = (x, y, 1 - c)
        chips = [(1 - x, y), (x, 1 - y), (1 - x, 1 - y)]

        def idx(px, py, pc):
            return 4 * px + 2 * py + pc

        def copy(a, k, blk, to, src=None):
            return pltpu.make_async_remote_copy(
                src_ref=outs[a].at[blk] if src is None else src, dst_ref=outs[a].at[blk],
                send_sem=send_sems.at[a, k], recv_sem=recv_sems.at[a, k], device_id=to, device_id_type=MESH)

        mine = [pltpu.make_async_copy(ins[a], outs[a].at[me], local_sems.at[a]) for a in range(n)]
        for cp in mine:
            cp.start()
        first = []
        for a in range(n_full):
            first.append(copy(a, 0, me, sibling, src=ins[a]))
            first += [copy(a, 1 + j, me, (*chip, c), src=ins[a]) for j, chip in enumerate(chips)]
        for cp in first:
            cp.start()
        passed = []
        for a in range(n_full):
            for j, chip in enumerate(chips):
                blk = idx(*chip, c)
                copy(a, 1 + j, blk, (x, y, c)).wait_recv()
                cp = copy(a, 4 + j, blk, sibling)
                cp.start()
                passed.append(cp)
        for a in range(n_full):
            copy(a, 0, idx(x, y, 1 - c), (x, y, c)).wait_recv()
            for j, chip in enumerate(chips):
                copy(a, 4 + j, idx(*chip, 1 - c), (x, y, c)).wait_recv()
        for cp in first + passed:
            cp.wait_send()
        for cp in mine:
            cp.wait()

    return pl.pallas_call(
        body, name="allgather_weights",
        in_specs=[_any_spec()] * n, out_specs=[_any_spec()] * n,
        out_shape=[jax.ShapeDtypeStruct((N_DEV,) + s.shape, s.dtype) for s in shards],
        scratch_shapes=[pltpu.SemaphoreType.DMA((n_full, 7)), pltpu.SemaphoreType.DMA((n_full, 7)),
                        pltpu.SemaphoreType.DMA((n,))],
    )(*shards)


def _peer(x, y, c, k):
    tx = 1 - x if (k >> 2) & 1 else x
    ty = 1 - y if (k >> 1) & 1 else y
    tc = 1 - c if k & 1 else c
    return (tx, ty, tc), 4 * tx + 2 * ty + tc


_PEER_ORDER = (2, 4, 6, 3, 5, 7, 1)


_HBM = pl.BlockSpec(memory_space=pltpu.HBM)
_SEM = pl.BlockSpec(memory_space=pltpu.SEMAPHORE)
_EFFECT = pltpu.SideEffectType.DATAFLOW_SIDE_EFFECTING


def _exchange_copies(srcs, lands, send_sems, recv_sems, gather):
    x, y, c, me = _my_pos()
    copies = []
    for k in _PEER_ORDER:
        tgt, tidx = _peer(x, y, c, k)
        for a in range(len(srcs)):
            copies.append(pltpu.make_async_remote_copy(
                src_ref=srcs[a] if gather else srcs[a].at[tidx], dst_ref=lands[a].at[me],
                send_sem=send_sems.at[7 * a + k - 1], recv_sem=recv_sems.at[7 * a + k - 1],
                device_id=tgt, device_id_type=MESH))
    return copies


def _exchange_start(name, srcs, lands=None, after=None):
    n = len(srcs)
    gather = lands is not None
    if lands is None:
        lands = [lax.empty(g.shape, g.dtype) for g in srcs]
    extra = [] if after is None else [after]

    def body(*refs):
        src_refs, land_refs = refs[:n], refs[n:2 * n]
        send_sems, recv_sems = refs[2 * n + len(extra)], refs[2 * n + len(extra) + 1]
        token = refs[-1]
        for cp in _exchange_copies(src_refs, land_refs, send_sems, recv_sems, gather):
            cp.start()
        token[...] = jnp.zeros_like(token)

    hbm = lambda a: pltpu.with_memory_space_constraint(a, pltpu.HBM)
    outs = pl.pallas_call(
        body, name=name,
        out_shape=(pltpu.SemaphoreType.DMA((7 * n,)), pltpu.SemaphoreType.DMA((7 * n,)),
                   *[pltpu.HBM(g.shape, g.dtype) for g in list(srcs) + list(lands)],
                   jax.ShapeDtypeStruct((8, 128), f32)),
        in_specs=[_HBM] * (2 * n) + [pl.BlockSpec(memory_space=pl.ANY)] * len(extra),
        out_specs=(_SEM, _SEM, *([_HBM] * (2 * n)), pl.BlockSpec(memory_space=pltpu.VMEM)),
        input_output_aliases={i: 2 + i for i in range(2 * n)},
        compiler_params=pltpu.CompilerParams(has_side_effects=_EFFECT),
    )(*[hbm(g) for g in srcs], *[hbm(g) for g in lands], *extra)
    return outs[0], outs[1], list(outs[2:2 + n]), list(outs[2 + n:2 + 2 * n]), outs[-1], gather


def _exchange_wait(name, started, after):
    send_sems, recv_sems, srcs, lands, _, gather = started
    n = len(srcs)
    after = list(after) if isinstance(after, (list, tuple)) else [after]

    def body(*refs):
        src_refs, land_refs = refs[:n], refs[n:2 * n]
        s_sems, r_sems = refs[2 * n], refs[2 * n + 1]
        for cp in _exchange_copies(src_refs, land_refs, s_sems, r_sems, gather):
            cp.wait_send()
            cp.wait_recv()

    outs = pl.pallas_call(
        body, name=name,
        out_shape=tuple(pltpu.HBM(a.shape, a.dtype) for a in list(srcs) + list(lands)),
        in_specs=[_HBM] * (2 * n) + [_SEM, _SEM] + [pl.BlockSpec(memory_space=pl.ANY)] * len(after),
        out_specs=tuple([_HBM] * (2 * n)),
        input_output_aliases={i: i for i in range(2 * n)},
        compiler_params=pltpu.CompilerParams(has_side_effects=_EFFECT),
    )(*srcs, *lands, send_sems, recv_sems, *after)
    return list(outs[:n]), list(outs[n:])


SMALL_ROWS = 128


def _small_start(name, sg, after=None):
    return _exchange_start(name, [sg], [lax.empty((N_DEV,) + sg.shape, f32)], after=after)


def _small_sum(name, me, started, after):
    (own,), (slots,) = _exchange_wait(name + "_wait", started, after)

    def body(me_ref, s_ref, own_ref, out_ref):
        acc = None
        for p in range(N_DEV):
            term = lax.cond(me_ref[0] == p, lambda: own_ref[...], lambda p=p: s_ref[p])
            acc = term if acc is None else acc + term
        out_ref[...] = acc

    return pl.pallas_call(
        body, name=name + "_sum",
        in_specs=[pl.BlockSpec(memory_space=pltpu.SMEM), pl.BlockSpec(memory_space=pltpu.VMEM),
                  pl.BlockSpec(memory_space=pltpu.VMEM)],
        out_specs=pl.BlockSpec(memory_space=pltpu.VMEM),
        out_shape=jax.ShapeDtypeStruct(own.shape, f32))(me, slots, own)


def _adam_math(g, w, m, v):
    m = ADAM_B1 * m + (1.0 - ADAM_B1) * g
    v = ADAM_B2 * v + (1.0 - ADAM_B2) * (g * g)
    m_hat = m / (1.0 - ADAM_B1 ** ADAM_STEP)
    v_hat = v / (1.0 - ADAM_B2 ** ADAM_STEP)
    delta = -ADAM_LR * (m_hat / (jnp.sqrt(v_hat) + ADAM_EPS) + ADAM_WD * w)
    return delta, m, v


def _adam_slots(name, me, slots, own, w, m, v, tr, transposed=False):
    rows = slots.shape[1]

    def body(me_ref, s_ref, own_ref, w_ref, m_ref, v_ref, g_ref, d_ref, nm_ref, nv_ref):
        mine = own_ref[...]
        g = None
        for p in range(N_DEV):
            term = lax.cond(me_ref[0] == p, lambda: mine, lambda p=p: s_ref[p]).astype(f32)
            g = term if g is None else g + term
        if transposed:
            g = g.T
        delta, nm, nv = _adam_math(g, w_ref[...], m_ref[...], v_ref[...])
        g_ref[...] = g
        d_ref[...] = delta
        nm_ref[...] = nm
        nv_ref[...] = nv

    mode = dict(pipeline_mode=pl.Buffered(1)) if rows == tr else {}
    if transposed:
        rs = pl.BlockSpec((None, D, tr), lambda i, me_ref: (0, 0, i))
        rs_in = pl.BlockSpec((None, D, tr), lambda i, me_ref: (0, 0, i), **mode)
    else:
        rs = pl.BlockSpec((None, tr, D), lambda i, me_ref: (0, i, 0))
        rs_in = pl.BlockSpec((None, tr, D), lambda i, me_ref: (0, i, 0), **mode)
    return pl.pallas_call(
        body, name=name,
        grid_spec=pltpu.PrefetchScalarGridSpec(
            num_scalar_prefetch=1, grid=(rows // tr,),
            in_specs=[pl.BlockSpec((N_DEV, tr, D), lambda i, me_ref: (0, i, 0), **mode),
                      pl.BlockSpec((None, tr, D), lambda i, me_ref: (me_ref[0], i, 0), **mode), rs_in, rs_in, rs_in],
            out_specs=[rs] * 4),
        out_shape=[jax.ShapeDtypeStruct(w.shape, f32)] * 4,
        compiler_params=_cparams(("parallel",)))(me, slots, own, w, m, v)


def _copy_cols(src, row0, dst, t, c0, n):
    done = 0
    while done < n:
        r, c = divmod(c0 + done, D)
        take = min(n - done, D - c)
        dst[t:t + 1, done:done + take] = src[row0 + r:row0 + r + 1, c:c + take]
        done += take


def _adam_small(me, g, w, m, v):
    o = _small_offsets()
    ffn_cols = 2 * D_FF // N_DEV
    shapes = dict(norm1_g=(1, D), gate_b=(1, 2 * D), conv_w=(1, CONV_WIDTH, D // N_DEV), conv_b=(1, D),
                  conv_norm_g=(1, D), q_norm_g=(1, HEAD_DIM), k_norm_g=(1, HEAD_DIM), norm2_g=(1, D),
                  ffn_conv_w=(1, 3, ffn_cols), ffn_conv_b=(1, 2 * D_FF))
    names = tuple(shapes)

    def body(me_ref, g_ref, w_ref, m_ref, v_ref, *refs):
        parts, (d_ref, nm_ref, nv_ref) = refs[:-3], refs[-3:]
        delta, nm, nv = _adam_math(g_ref[...], w_ref[...], m_ref[...], v_ref[...])
        d_ref[...] = delta
        nm_ref[...] = nm
        nv_ref[...] = nv
        mine = pl.ds(pl.multiple_of(me_ref[0] * (D // N_DEV), D // N_DEV), D // N_DEV)
        for i, src in enumerate((g_ref, d_ref, nm_ref, nv_ref)):
            for k, name in enumerate(names):
                dst, r = parts[i * len(names) + k], o[name]
                if name == "gate_b":
                    dst[:, 0:D] = src[r:r + 1, :]
                    dst[:, D:2 * D] = src[r + 1:r + 2, :]
                elif name == "conv_w":
                    dst[0] = src[r:r + CONV_WIDTH, mine]
                elif name == "ffn_conv_b":
                    _copy_cols(src, r, dst, 0, 0, 2 * D_FF)
                elif name == "ffn_conv_w":
                    for p in range(N_DEV):
                        @pl.when(me_ref[0] == p)
                        def _(p=p, src=src, dst=dst, r=r):
                            for t in range(3):
                                _copy_cols(src, r + t * (FFN_PAD // D), dst.at[0], t, p * ffn_cols, ffn_cols)
                else:
                    dst[...] = src[r:r + 1, 0:shapes[name][1]]

    vmem = pl.BlockSpec(memory_space=pltpu.VMEM)
    outs = pl.pallas_call(
        body, name="adam_small", in_specs=[pl.BlockSpec(memory_space=pltpu.SMEM)] + [vmem] * 4,
        out_shape=[jax.ShapeDtypeStruct(shapes[n], f32) for _ in range(4) for n in names],
        scratch_shapes=[pltpu.VMEM(g.shape, f32)] * 3)(me, g, w, m, v)
    return {n: [outs[i * len(names) + k] for i in range(4)] for k, n in enumerate(names)}


FFN_PAD = 6 * D


_SMALL_PARTS = (("norm1_g", 1), ("gate_b", 2), ("conv_w", CONV_WIDTH), ("conv_b", 1), ("conv_norm_g", 1),
                ("q_norm_g", 1), ("k_norm_g", 1), ("norm2_g", 1), ("ffn_conv_w", 18), ("ffn_conv_b", 6), ("last", 1))


def _small_offsets():
    out, row = {}, 0
    for name, rows in _SMALL_PARTS:
        out[name] = row
        row += -(-rows // 8) * 8
    assert row == SMALL_ROWS
    return out


def _pack_small(norm1_g, gate_b, conv_w, conv_b, conv_norm_g, q_norm_g, k_norm_g, norm2_g, ffn_conv_w, ffn_conv_b,
                last_row=None):
    pad_h = lambda a: jnp.pad(a, ((0, 0), (0, D - HEAD_DIM)))
    pad_f = lambda a: jnp.pad(a, ((0, 0), (0, FFN_PAD - 2 * D_FF))).reshape(-1, D)
    parts = [norm1_g, gate_b.reshape(2, D), conv_w, conv_b, conv_norm_g, pad_h(q_norm_g), pad_h(k_norm_g), norm2_g,
             pad_f(ffn_conv_w), pad_f(ffn_conv_b), jnp.zeros((1, D), f32) if last_row is None else last_row]
    return jnp.concatenate([jnp.pad(p, ((0, -p.shape[0] % 8), (0, 0))) for p in parts], axis=0)


def _unpack_small(p):
    o = _small_offsets()
    rows = lambda name, n: p[o[name]:o[name] + n]
    ffn = lambda a: a.reshape(-1, FFN_PAD)[:, :2 * D_FF]
    return dict(
        norm1_g=rows("norm1_g", 1), gate_b=rows("gate_b", 2).reshape(1, 2 * D), conv_w=rows("conv_w", CONV_WIDTH),
        conv_b=rows("conv_b", 1), conv_norm_g=rows("conv_norm_g", 1), q_norm_g=rows("q_norm_g", 1)[:, :HEAD_DIM],
        k_norm_g=rows("k_norm_g", 1)[:, :HEAD_DIM], norm2_g=rows("norm2_g", 1),
        ffn_conv_w=ffn(rows("ffn_conv_w", 18)), ffn_conv_b=ffn(rows("ffn_conv_b", 6)))


_ADAM_TILE = {896: 128, 704: 704, 128: 128, 352: 176}


def kernel(x, norm1_g, w_in, gate_b, conv_w, conv_b, conv_norm_g, w_conv_out, q_norm_g, k_norm_g, w_attn_out, w_out, norm2_g, w_up, ffn_conv_w, ffn_conv_b, w_down, loss_target, m_norm1_g, m_w_in, m_gate_b, m_conv_w, m_conv_b, m_conv_norm_g, m_w_conv_out, m_q_norm_g, m_k_norm_g, m_w_attn_out, m_w_out, m_norm2_g, m_w_up, m_ffn_conv_w, m_ffn_conv_b, m_w_down, v_norm1_g, v_w_in, v_gate_b, v_conv_w, v_conv_b, v_conv_norm_g, v_w_conv_out, v_q_norm_g, v_k_norm_g, v_w_attn_out, v_w_out, v_norm2_g, v_w_up, v_ffn_conv_w, v_ffn_conv_b, v_w_down):
    BL, S, _ = x.shape
    T = BL * S
    me = 4 * lax.axis_index("x") + 2 * lax.axis_index("y") + lax.axis_index("c")
    xt = x.reshape(T, D)
    target = loss_target.reshape(T, D)

    big = dict(w_in=(w_in, m_w_in, v_w_in), w_up=(w_up, m_w_up, v_w_up),
               w_conv_out=(w_conv_out, m_w_conv_out, v_w_conv_out), w_attn_out=(w_attn_out, m_w_attn_out, v_w_attn_out),
               w_out=(w_out, m_w_out, v_w_out), w_down=(w_down, m_w_down, v_w_down))
    order = ["w_in", "w_conv_out", "w_attn_out", "w_out", "w_up", "w_down"]
    shards = [(big[n][0][0].T if n in ("w_in", "w_up") else big[n][0][0]).astype(bf16) for n in order]
    gathered = _allgather_rows(shards, 1)
    W = {"w_in": gathered[0].reshape(-1, D)}

    def place_cols(shard, full_cols):
        z = jnp.zeros((shard.shape[0], full_cols), f32)
        return lax.dynamic_update_slice(z, shard, (0, me * shard.shape[1]))

    zr = lambda a: jnp.zeros_like(a)
    conv_local = _pack_small(
        zr(norm1_g), zr(gate_b), place_cols(conv_w[0], D), zr(conv_b), zr(conv_norm_g), zr(q_norm_g), zr(k_norm_g),
        zr(norm2_g), place_cols(ffn_conv_w[0], 2 * D_FF), zr(ffn_conv_b))
    ga_conv = _small_start("gather_conv_start", conv_local, after=gathered[0])
    ga_proj = _exchange_start("gather_start_proj", shards[1:4], gathered[1:4], after=ga_conv[4])
    ga_ffn = _exchange_start("gather_start_ffn", shards[4:6], gathered[4:6], after=ga_proj[4])

    bd = (jnp.arange(128)[:, None] // HEAD_DIM == jnp.arange(128)[None, :] // HEAD_DIM).astype(bf16)
    bias = _attn_bias()
    qg = jnp.tile(q_norm_g, (1, N_HEADS))
    kg = jnp.tile(k_norm_g, (1, N_HEADS))

    z8, h, qn, kn = _in_proj_fwd(xt, norm1_g, W["w_in"], qg, kg, bd, ga_ffn[4])
    conv_all = _unpack_small(_small_sum("gather_conv", me.reshape(1), ga_conv, z8))
    conv_w_full, ffn_w_full = conv_all["conv_w"], conv_all["ffn_conv_w"]
    c = _conv_fwd(z8, conv_w_full, conv_b, S)
    o, ob, lse = _attn_fwd(qn, kn, z8, bias, S)
    for n, g in zip(order[1:4], _exchange_wait("gather_wait_proj", ga_proj, ob)[1]):
        W[n] = g.reshape(-1, D)
    s, ya, yb, mixed = _branches_fwd(c, ob, z8, conv_norm_g, gate_b, W["w_conv_out"], W["w_attn_out"])
    x1, h2 = _out_norm2_fwd(mixed, W["w_out"], xt, norm2_g)
    for n, g in zip(order[4:6], _exchange_wait("gather_wait_ffn", ga_ffn, x1)[1]):
        W[n] = g.reshape(-1, D)
    TNU = D_FF // 2
    u3 = _matmul_call(
        "mm_u", h2, W["w_up"],
        pl.BlockSpec((1024, D), lambda i, j, k: (i, 0)),
        pl.BlockSpec((TNU, D), lambda i, j, k: (j, 0)),
        pl.BlockSpec((None, 1024, TNU), lambda i, j, k: (j // 2, i, j % 2)),
        jax.ShapeDtypeStruct((2, T, D_FF), f32), (T // 1024, 4, 1), "nt", 1, 1024, TNU)
    f = _ffn_fwd(u3, ffn_w_full, ffn_conv_b, S)
    dy, dyb, lacc = _down_loss_fwd(f, W["w_down"], x1, target)
    loss_local = 0.5 / D * jnp.sum(lacc)

    df = _matmul("mm_df", dyb, W["w_down"], "nt", f32, tn=TNU)
    g_w_down = _matmul("mm_dwdn", f, dyb, "tn", bf16, tm=TNU)
    du3, dffn = _ffn_bwd(u3, df, ffn_w_full, ffn_conv_b, S)
    g_w_up = _matmul_call(
        "mm_dwup", du3, h2,
        pl.BlockSpec((None, T, TNU), lambda i, j, k: (i // 2, 0, i % 2)),
        pl.BlockSpec((T, D), lambda i, j, k: (0, 0)),
        pl.BlockSpec((TNU, D), lambda i, j, k: (i, 0)),
        jax.ShapeDtypeStruct((2 * D_FF, D), bf16), (4, 1, 1), "tn", 1, TNU, D)
    blocks8 = lambda a: a.reshape(N_DEV, -1, D)
    ex_ffn = _exchange_start("scatter_start_ffn", [blocks8(g_w_up), blocks8(g_w_down)])
    dx1, dx1b, dg_norm2 = _up_norm2_bwd(du3, W["w_up"], x1, dy, norm2_g, ex_ffn[4])
    g_w_out = _matmul("mm_dwo", mixed, dx1b, "tn", bf16, tm=512)
    dz8 = lax.empty((8, T, D), bf16)
    dya, dyb2, dz8, dg_gate = _out_gate_bwd(dx1b, W["w_out"], z8, gate_b, ya, yb, dz8)
    g_w_conv_out = _matmul("mm_dwco", s, dya, "tn", bf16, tm=512)
    g_w_attn_out = _matmul("mm_dwao", ob, dyb2, "tn", bf16, tm=512)
    ex_proj = _exchange_start("scatter_start_proj", [blocks8(g_w_conv_out), blocks8(g_w_attn_out), blocks8(g_w_out)])
    do = _matmul("mm_do", dyb2, W["w_attn_out"], "nt", f32, after=ex_proj[4])
    dc, dg_convnorm = _convnorm_bwd(dya, W["w_conv_out"], c, conv_norm_g)
    dz8a, dconv = _conv_bwd(dc, z8, conv_w_full, dz8, S)
    dz8b, dg_q, dg_k = _attn_bwd(qn, kn, z8, do, o, lse, bias, bd, qg, kg, dz8a, S)
    g_w_in = _matmul_call(
        "mm_dwin", dz8b, h,
        pl.BlockSpec((None, T, D), lambda i, j, k: (jnp.where(i < 2, i, jnp.where(i < 5, i + 2, i - 3)), 0, 0)),
        pl.BlockSpec((T, D), lambda i, j, k: (0, 0)), pl.BlockSpec((1024, D), lambda i, j, k: (i, 0)),
        jax.ShapeDtypeStruct((7 * D, D), bf16), (7, 1, 1), "tn", 1, D, D)
    ex_in = _exchange_start("scatter_start_in", [blocks8(g_w_in)])
    grad_x, dg_norm1 = _in_norm1_bwd(dz8b, W["w_in"], xt, dx1, norm1_g, ex_in[4])

    sum8 = lambda a: a.reshape(-1, 8, a.shape[-1]).sum(axis=1)
    dconv_s = sum8(dconv.sum(axis=0))
    dffn_s = dffn.sum(axis=0).reshape(2, 4, 8, D_FF).sum(axis=2)
    dffn_w = jnp.concatenate([dffn_s[0, :3], dffn_s[1, :3]], axis=1)
    dffn_b = jnp.concatenate([dffn_s[0, 3:4], dffn_s[1, 3:4]], axis=1)
    fold = lambda a: sum8(a).reshape(N_HEADS, HEAD_DIM).sum(axis=0)[None]
    small_g_local = _pack_small(
        sum8(dg_norm1), sum8(dg_gate), dconv_s[:CONV_WIDTH], dconv_s[CONV_WIDTH:], sum8(dg_convnorm),
        fold(dg_q), fold(dg_k), sum8(dg_norm2), dffn_w, dffn_b,
        last_row=jnp.pad(loss_local.reshape(1, 1), ((0, 0), (0, D - 1))))
    sg_start = _small_start("small_grads_start", small_g_local)

    place_m = lambda a, full: place_cols(a[0], full)
    small_w_true = _pack_small(norm1_g, gate_b, conv_w_full, conv_b, conv_norm_g, q_norm_g, k_norm_g, norm2_g,
                               ffn_w_full, ffn_conv_b)
    small_m = _pack_small(m_norm1_g, m_gate_b, place_m(m_conv_w, D), m_conv_b, m_conv_norm_g, m_q_norm_g, m_k_norm_g,
                          m_norm2_g, place_m(m_ffn_conv_w, 2 * D_FF), m_ffn_conv_b)
    small_v = _pack_small(v_norm1_g, v_gate_b, place_m(v_conv_w, D), v_conv_b, v_conv_norm_g, v_q_norm_g, v_k_norm_g,
                          v_norm2_g, place_m(v_ffn_conv_w, 2 * D_FF), v_ffn_conv_b)

    own, slots = {}, {}
    for tag, ex, names_ in (("ffn", ex_ffn, ("w_up", "w_down")),
                            ("proj", ex_proj, ("w_conv_out", "w_attn_out", "w_out")), ("in", ex_in, ("w_in",))):
        sent, landed = _exchange_wait("scatter_wait_" + tag, ex, [sg_start[4], small_w_true, small_m, small_v])
        for n, src, land in zip(names_, sent, landed):
            own[n], slots[n] = src, land

    res, adam_done = {}, []
    for n in order:
        w, m, v = big[n]
        outs = _adam_slots("adam_" + n, me.reshape(1), slots[n], own[n], w, m, v, _ADAM_TILE[slots[n].shape[1]],
                           transposed=n in ("w_in", "w_up"))
        adam_done.append(outs[0])
        res[n] = list(outs)
    small_g = _small_sum("small_grads", me.reshape(1), sg_start, adam_done)
    loss = small_g[_small_offsets()["last"], 0]

    res.update(_adam_small(me.reshape(1), small_g, small_w_true, small_m, small_v))

    names = ["norm1_g", "w_in", "gate_b", "conv_w", "conv_b", "conv_norm_g", "w_conv_out", "q_norm_g", "k_norm_g",
             "w_attn_out", "w_out", "norm2_g", "w_up", "ffn_conv_w", "ffn_conv_b", "w_down"]
    out = [loss, grad_x.reshape(BL, S, D)]
    for i in range(4):
        out += [res[n][i] for n in names]
    return tuple(out)
```

```python
import functools

import jax
import jax.numpy as jnp
import numpy as np
from jax import lax
from jax.experimental import pallas as pl
from jax.experimental.pallas import tpu as pltpu

f32 = jnp.float32
bf16 = jnp.bfloat16

D = 1024
N_HEADS = 16
HEAD_DIM = 64
CONV_WIDTH = 31
D_FF = 2816
GROUPS = ((128, 1), (512, 4), (2048, 16))
ATTN_BLOCK = 128
EPS = 1e-6
N_DEV = 8
MESH = pl.DeviceIdType.MESH

ADAM_LR = 0.001
ADAM_B1 = 0.9
ADAM_B2 = 0.999
ADAM_EPS = 1e-08
ADAM_WD = 0.01
ADAM_STEP = 10

VMEM_LIMIT = 56 * 1024 * 1024
MASK_BIAS = 1e30

Z_AVAL, Z_AGATE, Z_GA, Z_GB, Z_Q, Z_K, Z_V = 0, 1, 2, 3, 4, 5, 6


_W_OF_Z = (0, 1, 5, 6, 2, 3, 4)


def _wsec_of_zsec(j):
    return jnp.where(j < 2, j, jnp.where(j < 4, j + 3, j - 2))


def _sig(x):
    return 1.0 / (1.0 + jnp.exp(-x))


def _colsum8(x):
    return x.reshape(-1, 8, x.shape[-1]).sum(axis=0)


def _cparams(sem):
    return pltpu.CompilerParams(dimension_semantics=sem, vmem_limit_bytes=VMEM_LIMIT)


def _my_pos():
    x, y, c = lax.axis_index("x"), lax.axis_index("y"), lax.axis_index("c")
    return x, y, c, 4 * x + 2 * y + c


_DIMS = {"nn": ((1,), (0,)), "nt": ((1,), (1,)), "tn": ((0,), (0,))}


def _matmul_call(name, a, b, a_spec, b_spec, o_spec, out_shape, grid, mode, nk, tm, tn, after=None):
    dims = (_DIMS[mode], ((), ()))
    extra = [] if after is None else [after]

    def body(a_ref, b_ref, *rest):
        o_ref, scratch = rest[len(extra)], rest[len(extra) + 1:]
        part = lax.dot_general(a_ref[...], b_ref[...], dims, preferred_element_type=f32)
        if nk == 1:
            o_ref[...] = part.astype(o_ref.dtype)
        else:
            acc = scratch[0]
            k = pl.program_id(2)

            @pl.when(k == 0)
            def _():
                acc[...] = part

            @pl.when(k > 0)
            def _():
                acc[...] += part

            @pl.when(k == nk - 1)
            def _():
                o_ref[...] = acc[...].astype(o_ref.dtype)

    scratch = [] if nk == 1 else [pltpu.VMEM((tm, tn), f32)]
    return pl.pallas_call(
        body, name=name, grid=grid, in_specs=[a_spec, b_spec] + [pl.BlockSpec(memory_space=pl.ANY)] * len(extra),
        out_specs=o_spec, out_shape=out_shape,
        scratch_shapes=scratch, compiler_params=_cparams(("parallel", "parallel", "arbitrary")),
    )(a, b, *extra)


def _matmul(name, a, b, mode, out_dtype, tm=1024, tn=1024, tk=None, after=None):
    if mode == "nn":
        (M, K), (_, N) = a.shape, b.shape
    elif mode == "nt":
        (M, K), (N, _) = a.shape, b.shape
    else:
        (K, M), (_, N) = a.shape, b.shape
    tm, tn = min(tm, M), min(tn, N)
    tk = K if tk is None else tk
    nk = K // tk
    assert M % tm == 0 and N % tn == 0 and K % tk == 0
    if mode == "tn":
        a_spec = pl.BlockSpec((tk, tm), lambda i, j, k: (k, i))
    else:
        a_spec = pl.BlockSpec((tm, tk), lambda i, j, k: (i, k))
    if mode == "nt":
        b_spec = pl.BlockSpec((tn, tk), lambda i, j, k: (j, k))
    else:
        b_spec = pl.BlockSpec((tk, tn), lambda i, j, k: (k, j))
    o_spec = pl.BlockSpec((tm, tn), lambda i, j, k: (i, j))
    return _matmul_call(name, a, b, a_spec, b_spec, o_spec, jax.ShapeDtypeStruct((M, N), out_dtype),
                        (M // tm, N // tn, nk), mode, nk, tm, tn, after=after)


FTM = 512


def _matmul_fused(name, a, b, pairs, epilogue, extras, consts, outs, nt=False, sums=False, passed=(), aliases=None):
    sa, M, kk = a.shape
    na = max(i for i, _ in pairs) + 1
    ne, nc, npass = len(extras), len(consts), len(passed)
    dims = (_DIMS["nt" if nt else "nn"], ((), ()))

    def body(a_ref, b_ref, *rest):
        acc = None
        for i, j in pairs:
            part = lax.dot_general(a_ref[i], b_ref[j], dims, preferred_element_type=f32)
            acc = part if acc is None else acc + part
        epilogue(acc, rest[:ne], rest[ne:ne + nc], rest[ne + nc + npass:])

    whole = lambda arr: pl.BlockSpec(arr.shape, lambda i, nd=arr.ndim: (0,) * nd, pipeline_mode=pl.Buffered(1))
    io_alias = {2 + ne + nc + k: v for k, v in (aliases or {}).items()}
    return pl.pallas_call(
        body, name=name, grid=(M // FTM,),
        in_specs=[pl.BlockSpec((na, FTM, kk), lambda i: (0, i, 0)), whole(b)] + [s for _, s in extras]
        + [whole(c) for c in consts] + [pl.BlockSpec(memory_space=pl.ANY)] * npass,
        out_specs=[s for _, s in outs], out_shape=[s for s, _ in outs], input_output_aliases=io_alias,
        compiler_params=_cparams(("arbitrary" if sums else "parallel",)),
    )(a, b, *[x for x, _ in extras], *consts, *passed)


def _frows(c=D):
    return pl.BlockSpec((FTM, c), lambda i: (i, 0))


def _fsec(s):
    return pl.BlockSpec((None, FTM, D), lambda i: (s, i, 0))


def _rowshape(T, dtype, c=D):
    return (jax.ShapeDtypeStruct((T, c), dtype), _frows(c))


def _sumshape(c=D):
    return (jax.ShapeDtypeStruct((8, c), f32), pl.BlockSpec((8, c), lambda i: (0, 0)))


def _add_colsum(ref, x, cols=None):
    @pl.when(pl.program_id(0) == 0)
    def _():
        if cols is None:
            ref[...] = jnp.zeros_like(ref)
        else:
            ref[:, cols] = jnp.zeros((8, x.shape[-1]), f32)

    if cols is None:
        ref[...] += _colsum8(x)
    else:
        ref[:, cols] += _colsum8(x)


def _rms(x):
    return lax.rsqrt(jnp.mean(x * x, axis=-1, keepdims=True) + EPS)


def _rms_bwd(dy_g, xn, rstd):
    return rstd * (dy_g - xn * jnp.mean(dy_g * xn, axis=-1, keepdims=True))


def _head_sum(x, bd):
    parts = []
    for cb in range(x.shape[-1] // 128):
        xb = x[:, cb * 128:(cb + 1) * 128]
        hi = xb.astype(bf16)
        lo = (xb - hi.astype(f32)).astype(bf16)
        parts.append(jnp.dot(hi, bd, preferred_element_type=f32) + jnp.dot(lo, bd, preferred_element_type=f32))
    return parts[0] if len(parts) == 1 else jnp.concatenate(parts, axis=1)


ZTM = 1024
ZSLOTS = 3


def _in_proj_fwd(x, g, w_in_t, qg, kg, bd, after):
    T = x.shape[0]
    nt = T // ZTM

    last = 7 * nt - 1

    def body(x_ref, g_ref, w_ref, qg_ref, kg_ref, bd_ref, after_ref, z_hbm, h_ref, qn_ref, kn_ref, hbuf, zbuf, zsem):
        del after_ref
        j, i = pl.program_id(0), pl.program_id(1)
        rows = pl.ds(pl.multiple_of(i * ZTM, ZTM), ZTM)
        step = j * nt + i

        def z_copy(s):
            if isinstance(s, int):
                sec, t, slot = s // nt, s % nt, s % ZSLOTS
            else:
                sec, t, slot = lax.div(s, nt), lax.rem(s, nt), lax.rem(s, ZSLOTS)
            tile = pl.ds(t * ZTM if isinstance(s, int) else pl.multiple_of(t * ZTM, ZTM), ZTM)
            return pltpu.make_async_copy(zbuf.at[slot], z_hbm.at[sec, tile, :], zsem.at[slot])

        @pl.when(step >= ZSLOTS)
        def _():
            z_copy(step - ZSLOTS).wait()

        @pl.when(j == 0)
        def _():
            xv = x_ref[...]
            hv = (xv * _rms(xv) * g_ref[...]).astype(bf16)
            hbuf[rows, :] = hv
            h_ref[...] = hv

        def project():
            z = lax.dot_general(hbuf[rows, :], w_ref[...], (_DIMS["nt"], ((), ())), preferred_element_type=f32)
            zbuf[lax.rem(step, ZSLOTS)] = z
            z_copy(step).start()
            return z

        def head_norm(z, gain_ref, scale):
            return z * lax.rsqrt(_head_sum(z * z, bd_ref[...]) * (1.0 / HEAD_DIM) + EPS) * gain_ref[...] * scale

        @pl.when(j == Z_Q)
        def _():
            qn_ref[...] = head_norm(project(), qg_ref, HEAD_DIM ** -0.5)

        @pl.when(j == Z_K)
        def _():
            kn_ref[...] = head_norm(project(), kg_ref, 1.0)

        @pl.when((j != Z_Q) & (j != Z_K))
        def _():
            project()

        @pl.when(step == last)
        def _():
            for s in range(last - ZSLOTS + 1, last + 1):
                z_copy(s).wait()

    def tile_at(sec, **mode):
        return pl.BlockSpec((ZTM, D), lambda j, i: (jnp.where(j < sec, 0, jnp.where(j == sec, i, nt - 1)), 0), **mode)

    row = pl.BlockSpec((1, D), lambda j, i: (0, 0))
    return pl.pallas_call(
        body, name="mm_z", grid=(7, nt),
        in_specs=[tile_at(0, pipeline_mode=pl.Buffered(1)), row, pl.BlockSpec((D, D), lambda j, i: (_wsec_of_zsec(j), 0)), row, row,
                  pl.BlockSpec((128, 128), lambda j, i: (0, 0)), pl.BlockSpec(memory_space=pl.ANY)],
        out_specs=[pl.BlockSpec(memory_space=pl.ANY), tile_at(0), tile_at(Z_Q), tile_at(Z_K)],
        out_shape=[jax.ShapeDtypeStruct((8, T, D), f32), jax.ShapeDtypeStruct((T, D), bf16),
                   jax.ShapeDtypeStruct((T, D), f32), jax.ShapeDtypeStruct((T, D), f32)],
        scratch_shapes=[pltpu.VMEM((T, D), bf16), pltpu.VMEM((ZSLOTS, ZTM, D), f32),
                        pltpu.SemaphoreType.DMA((ZSLOTS,))],
        compiler_params=_cparams(("arbitrary", "arbitrary")))(x, g, w_in_t, qg, kg, bd, after)


def _branches_fwd(c, ob, z8, g, gate_b, w_conv_out, w_attn_out):
    T = c.shape[0]

    def epilogue(yb, extra, const, out):
        cv = extra[0][...]
        r = cv * _rms(cv) * const[0][...]
        s = (r * _sig(r)).astype(bf16)
        ya = jnp.dot(s, const[2][...], preferred_element_type=f32)
        b_ref = const[1]
        g_a = _sig(extra[1][...] + b_ref[:, :D])
        g_b = _sig(extra[2][...] + b_ref[:, D:])
        out[0][...] = s
        out[1][...] = ya
        out[2][...] = yb
        out[3][...] = (g_a * ya + g_b * yb).astype(bf16)

    return _matmul_fused("mm_branches", ob[None], w_attn_out[None], ((0, 0),), epilogue,
                         [(c, _frows()), (z8, _fsec(Z_GA)), (z8, _fsec(Z_GB))], [g, gate_b, w_conv_out],
                         [_rowshape(T, bf16), _rowshape(T, f32), _rowshape(T, f32), _rowshape(T, bf16)])


def _out_norm2_fwd(mixed, w_out, x, g):
    T = x.shape[0]

    def epilogue(acc, extra, const, out):
        x1 = extra[0][...] + acc
        out[0][...] = x1
        out[1][...] = (x1 * _rms(x1) * const[0][...]).astype(bf16)

    return _matmul_fused("mm_t1_norm2", mixed[None], w_out[None], ((0, 0),), epilogue, [(x, _frows())], [g],
                         [_rowshape(T, f32), _rowshape(T, bf16)])


def _down_loss_fwd(f, w_down, x1, target):
    T = x1.shape[0]

    def epilogue(acc, extra, const, out):
        diff = extra[0][...] + acc - extra[1][...]
        dy = diff * (1.0 / D)
        out[0][...] = dy
        out[1][...] = dy.astype(bf16)
        _add_colsum(out[2], diff * diff)

    return _matmul_fused("mm_t2_loss", f[None], w_down[None], ((0, 0),), epilogue, [(x1, _frows()), (target, _frows())],
                         [], [_rowshape(T, f32), _rowshape(T, bf16), _sumshape()], sums=True)


def _up_norm2_bwd(du3, w_up_t, x1, dy, g, token):
    T = x1.shape[0]

    def epilogue(dh, extra, const, out):
        x1v = extra[0][...]
        rstd = _rms(x1v)
        xn = x1v * rstd
        dx1 = extra[1][...] + _rms_bwd(dh * const[0][...], xn, rstd)
        out[0][...] = dx1
        out[1][...] = dx1.astype(bf16)
        _add_colsum(out[2], dh * xn)

    return _matmul_fused("mm_dh2_norm2", du3, w_up_t.reshape(2, D_FF, D), ((0, 0), (1, 1)), epilogue,
                         [(x1, _frows()), (dy, _frows())], [g],
                         [_rowshape(T, f32), _rowshape(T, bf16), _sumshape()], sums=True, passed=[token])


def _out_gate_bwd(dx1b, w_out, z8, gate_b, ya, yb, dz8):
    T = ya.shape[0]

    def epilogue(dm, extra, const, out):
        b_ref = const[0]
        g_a = _sig(extra[0][...] + b_ref[:, :D])
        g_b = _sig(extra[1][...] + b_ref[:, D:])
        out[0][...] = (dm * g_a).astype(bf16)
        out[1][...] = (dm * g_b).astype(bf16)
        dla = dm * extra[2][...] * g_a * (1.0 - g_a)
        dlb = dm * extra[3][...] * g_b * (1.0 - g_b)
        out[2][0] = dla.astype(bf16)
        out[2][1] = dlb.astype(bf16)
        _add_colsum(out[3], dla, slice(0, D))
        _add_colsum(out[3], dlb, slice(D, 2 * D))

    return _matmul_fused(
        "mm_dmixed_gate", dx1b[None], w_out[None], ((0, 0),), epilogue,
        [(z8, _fsec(Z_GA)), (z8, _fsec(Z_GB)), (ya, _frows()), (yb, _frows())], [gate_b],
        [_rowshape(T, bf16), _rowshape(T, bf16),
         (jax.ShapeDtypeStruct(dz8.shape, bf16), pl.BlockSpec((2, FTM, D), lambda i: (1, i, 0))), _sumshape(2 * D)],
        nt=True, sums=True, passed=[dz8], aliases={0: 2})


def _convnorm_bwd(dya, w_conv_out, c, g):
    T = c.shape[0]

    def epilogue(ds, extra, const, out):
        cv = extra[0][...]
        rstd = _rms(cv)
        r0 = cv * rstd
        gv = const[0][...]
        r = r0 * gv
        sg = _sig(r)
        dr = ds * sg * (1.0 + r * (1.0 - sg))
        out[0][...] = _rms_bwd(dr * gv, r0, rstd)
        _add_colsum(out[1], dr * r0)

    return _matmul_fused("mm_ds_convnorm", dya[None], w_conv_out[None], ((0, 0),), epilogue, [(c, _frows())], [g],
                         [_rowshape(T, f32), _sumshape()], nt=True, sums=True)


def _in_norm1_bwd(dz8, w_in_t, x, dx1, g, token):
    T = x.shape[0]

    def epilogue(dh, extra, const, out):
        xv = extra[0][...]
        rstd = _rms(xv)
        xn = xv * rstd
        out[0][...] = extra[1][...] + _rms_bwd(dh * const[0][...], xn, rstd)
        _add_colsum(out[1], dh * xn)

    return _matmul_fused("mm_dh_norm1", dz8, w_in_t.reshape(7, D, D), tuple(zip(range(7), _W_OF_Z)), epilogue,
                         [(x, _frows()), (dx1, _frows())], [g], [_rowshape(T, f32), _sumshape()],
                         sums=True, passed=[token])


CCW = 256
CR = 64
HALO = 32


def _conv_fwd(z8, conv_w, conv_b, S):
    T = z8.shape[1]
    nb = T // S
    ncb = D // CCW

    def body(av_ref, ag_ref, w_ref, b_ref, c_ref, pad):
        pad[0:HALO, :] = jnp.zeros((HALO, CCW), f32)

        def fill(i, carry):
            r0 = pl.multiple_of(i * 256, 256)
            pad[pl.ds(HALO + r0, 256), :] = av_ref[pl.ds(r0, 256), :] * _sig(ag_ref[pl.ds(r0, 256), :])
            return carry

        lax.fori_loop(0, S // 256, fill, 0)
        bias = b_ref[...]

        def chunk(i, carry):
            r0 = pl.multiple_of(i * CR, CR)
            win = pad[pl.ds(r0, CR + HALO), :]
            acc = jnp.zeros((CR, CCW), f32) + bias
            for s in range(8):
                part = None
                for m in range((CONV_WIDTH - 1 - s) // 8 + 1):
                    j = CONV_WIDTH - 1 - 8 * m - s
                    term = win[24 - 8 * m:24 - 8 * m + CR + 8, :] * w_ref[j:j + 1, :]
                    part = term if part is None else part + term
                acc = acc + part[8 - s:8 - s + CR, :]
            c_ref[pl.ds(r0, CR), :] = acc
            return carry

        lax.fori_loop(0, S // CR, chunk, 0)

    zs = lambda s: pl.BlockSpec((None, S, CCW), lambda b, cb: (s, b, cb))
    return pl.pallas_call(
        body, name="conv_fwd", grid=(nb, ncb),
        in_specs=[zs(Z_AVAL), zs(Z_AGATE), pl.BlockSpec((CONV_WIDTH, CCW), lambda b, cb: (0, cb)),
                  pl.BlockSpec((1, CCW), lambda b, cb: (0, cb))],
        out_specs=pl.BlockSpec((S, CCW), lambda b, cb: (b, cb)),
        out_shape=jax.ShapeDtypeStruct((T, D), f32),
        scratch_shapes=[pltpu.VMEM((S + HALO, CCW), f32)],
        compiler_params=_cparams(("parallel", "parallel")))(z8, z8, conv_w, conv_b)


def _conv_bwd(dc, z8, conv_w, dz8, S):
    T = dc.shape[0]
    nb = T // S
    ncb = D // CCW

    def body(dc_ref, av_ref, ag_ref, w_ref, dz_in, dz_ref, dw_ref, apad, dpad, shbuf):
        del dz_in
        apad[0:HALO, :] = jnp.zeros((HALO, CCW), f32)
        dpad[S:S + HALO, :] = jnp.zeros((HALO, CCW), f32)
        dw_ref[...] = jnp.zeros_like(dw_ref)

        def fill(i, carry):
            r0 = pl.multiple_of(i * 256, 256)
            apad[pl.ds(HALO + r0, 256), :] = av_ref[pl.ds(r0, 256), :] * _sig(ag_ref[pl.ds(r0, 256), :])
            dpad[pl.ds(r0, 256), :] = dc_ref[pl.ds(r0, 256), :]
            return carry

        lax.fori_loop(0, S // 256, fill, 0)

        def chunk(i, carry):
            r0 = pl.multiple_of(i * CR, CR)
            dwin = dpad[pl.ds(r0, CR + HALO), :]
            da = jnp.zeros((CR, CCW), f32)
            for s in range(8):
                shbuf[...] = dwin[s:s + CR, :]
                dshift = shbuf[...]
                part = None
                for m in range((CONV_WIDTH - 1 - s) // 8 + 1):
                    j = CONV_WIDTH - 1 - 8 * m - s
                    term = dwin[8 * m:8 * m + CR + 8, :] * w_ref[j:j + 1, :]
                    part = term if part is None else part + term
                    a_lag = apad[pl.ds(r0 + HALO - 8 * m, CR), :]
                    dw_ref[8 * j:8 * j + 8, :] += _colsum8(dshift * a_lag)
                da = da + part[s:s + CR, :]
            dw_ref[8 * CONV_WIDTH:8 * CONV_WIDTH + 8, :] += _colsum8(dwin[0:CR, :])
            av = av_ref[pl.ds(r0, CR), :]
            sg = _sig(ag_ref[pl.ds(r0, CR), :])
            dz_ref[0, pl.ds(r0, CR), :] = (da * sg).astype(bf16)
            dz_ref[1, pl.ds(r0, CR), :] = (da * av * sg * (1.0 - sg)).astype(bf16)
            return carry

        lax.fori_loop(0, S // CR, chunk, 0)

    zs = lambda s: pl.BlockSpec((None, S, CCW), lambda b, cb: (s, b, cb))
    return pl.pallas_call(
        body, name="conv_bwd", grid=(nb, ncb),
        in_specs=[pl.BlockSpec((S, CCW), lambda b, cb: (b, cb)), zs(Z_AVAL), zs(Z_AGATE),
                  pl.BlockSpec((CONV_WIDTH, CCW), lambda b, cb: (0, cb)), pl.BlockSpec(memory_space=pl.ANY)],
        out_specs=[pl.BlockSpec((2, S, CCW), lambda b, cb: (0, b, cb)),
                   pl.BlockSpec((None, 256, CCW), lambda b, cb: (b, 0, cb))],
        out_shape=[jax.ShapeDtypeStruct(dz8.shape, bf16), jax.ShapeDtypeStruct((nb, 256, D), f32)],
        input_output_aliases={4: 0},
        scratch_shapes=[pltpu.VMEM((S + HALO, CCW), f32), pltpu.VMEM((S + HALO, CCW), f32),
                        pltpu.VMEM((CR, CCW), f32)],
        compiler_params=_cparams(("parallel", "parallel")))(dc, z8, z8, conv_w, dz8)


FR = 128
NFB = D_FF // CCW
FBW = 128


def _ffn_window(ref, i, r0):
    return ref[pl.ds(r0 - 8, FR + 8), :]


def _ffn_u(win, w_ref, b_ref):
    return (win[6:6 + FR, :] * w_ref[0:1, :] + win[7:7 + FR, :] * w_ref[1:2, :]
            + win[8:8 + FR, :] * w_ref[2:3, :] + b_ref[...])


def _ffn_fwd(u3, ffn_w, ffn_b, S):
    T = u3.shape[1]
    nb = T // S

    def body(uv_ref, ug_ref, wv_ref, wg_ref, bv_ref, bg_ref, f_ref):
        def chunk(first, i):
            r0 = 0 if first else pl.multiple_of(i * FR, FR)
            if first:
                z = jnp.zeros((8, CCW), f32)
                wv = jnp.concatenate([z, uv_ref[0:FR, :]], axis=0)
                wg = jnp.concatenate([z, ug_ref[0:FR, :]], axis=0)
            else:
                wv = _ffn_window(uv_ref, i, r0)
                wg = _ffn_window(ug_ref, i, r0)
            u_val = _ffn_u(wv, wv_ref, bv_ref)
            u_gate = _ffn_u(wg, wg_ref, bg_ref)
            f_ref[pl.ds(r0, FR), :] = (u_gate * _sig(u_gate) * u_val).astype(bf16)

        chunk(True, 0)

        def loop(i, carry):
            chunk(False, i)
            return carry

        lax.fori_loop(1, S // FR, loop, 0)

    us = lambda h: pl.BlockSpec((None, S, CCW), lambda b, cb: (h, b, cb))
    ws = lambda h: pl.BlockSpec((3, CCW), lambda b, cb: (0, h * NFB + cb))
    bs = lambda h: pl.BlockSpec((1, CCW), lambda b, cb: (0, h * NFB + cb))
    return pl.pallas_call(
        body, name="ffn_fwd", grid=(nb, NFB),
        in_specs=[us(0), us(1), ws(0), ws(1), bs(0), bs(1)],
        out_specs=pl.BlockSpec((S, CCW), lambda b, cb: (b, cb)),
        out_shape=jax.ShapeDtypeStruct((T, D_FF), bf16),
        compiler_params=_cparams(("parallel", "parallel")))(u3, u3, ffn_w, ffn_w, ffn_b, ffn_b)


def _ffn_bwd(u3, df, ffn_w, ffn_b, S):
    T = u3.shape[1]
    nb = T // S

    def body(uv_ref, ug_ref, df_ref, wv_ref, wg_ref, bv_ref, bg_ref, du_ref, dw_ref, dvpad, dgpad, shbuf):
        dvpad[S:S + 8, :] = jnp.zeros((8, FBW), f32)
        dgpad[S:S + 8, :] = jnp.zeros((8, FBW), f32)
        dw_ref[...] = jnp.zeros_like(dw_ref)

        def chunk(first, i):
            r0 = 0 if first else pl.multiple_of(i * FR, FR)
            if first:
                z = jnp.zeros((8, FBW), f32)
                wv = jnp.concatenate([z, uv_ref[0:FR, :]], axis=0)
                wg = jnp.concatenate([z, ug_ref[0:FR, :]], axis=0)
            else:
                wv = _ffn_window(uv_ref, i, r0)
                wg = _ffn_window(ug_ref, i, r0)
            taps = []
            for h, win in enumerate((wv, wg)):
                shbuf[2 * h] = win[6:6 + FR, :]
                shbuf[2 * h + 1] = win[7:7 + FR, :]
                taps.append((shbuf[2 * h], shbuf[2 * h + 1], win[8:8 + FR, :]))
            conv = lambda x, w_ref, b_ref: (x[0] * w_ref[0:1, :] + x[1] * w_ref[1:2, :] + x[2] * w_ref[2:3, :]
                                            + b_ref[...])
            u_val = conv(taps[0], wv_ref, bv_ref)
            u_gate = conv(taps[1], wg_ref, bg_ref)
            dfc = df_ref[pl.ds(r0, FR), :]
            sg = _sig(u_gate)
            d_val = dfc * u_gate * sg
            d_gate = dfc * u_val * sg * (1.0 + u_gate * (1.0 - sg))
            dvpad[pl.ds(r0, FR), :] = d_val
            dgpad[pl.ds(r0, FR), :] = d_gate
            for h, dd in enumerate((d_val, d_gate)):
                for j in range(3):
                    dw_ref[h, 8 * j:8 * j + 8, :] += _colsum8(dd * taps[h][j])
                dw_ref[h, 24:32, :] += _colsum8(dd)

        chunk(True, 0)

        def loop(i, carry):
            chunk(False, i)
            return carry

        lax.fori_loop(1, S // FR, loop, 0)

        def back(i, carry):
            r0 = pl.multiple_of(i * FR, FR)
            for h, (dpad, w_ref) in enumerate(((dvpad, wv_ref), (dgpad, wg_ref))):
                win = dpad[pl.ds(r0, FR + 8), :]
                du = (win[0:FR, :] * w_ref[2:3, :] + win[1:1 + FR, :] * w_ref[1:2, :]
                      + win[2:2 + FR, :] * w_ref[0:1, :])
                du_ref[h, pl.ds(r0, FR), :] = du.astype(bf16)
            return carry

        lax.fori_loop(0, S // FR, back, 0)

    ncb = D_FF // FBW
    us = lambda h: pl.BlockSpec((None, S, FBW), lambda b, cb: (h, b, cb))
    ws = lambda h: pl.BlockSpec((3, FBW), lambda b, cb: (0, h * ncb + cb))
    bs = lambda h: pl.BlockSpec((1, FBW), lambda b, cb: (0, h * ncb + cb))
    return pl.pallas_call(
        body, name="ffn_bwd", grid=(nb, ncb),
        in_specs=[us(0), us(1), pl.BlockSpec((S, FBW), lambda b, cb: (b, cb)), ws(0), ws(1), bs(0), bs(1)],
        out_specs=[pl.BlockSpec((2, S, FBW), lambda b, cb: (0, b, cb)),
                   pl.BlockSpec((None, 2, 32, FBW), lambda b, cb: (b, 0, 0, cb))],
        out_shape=[jax.ShapeDtypeStruct((2, T, D_FF), bf16), jax.ShapeDtypeStruct((nb, 2, 32, D_FF), f32)],
        scratch_shapes=[pltpu.VMEM((S + 8, FBW), f32), pltpu.VMEM((S + 8, FBW), f32),
                        pltpu.VMEM((4, FR, FBW), f32)],
        compiler_params=_cparams(("parallel", "parallel")))(u3, u3, df, ffn_w, ffn_w, ffn_b, ffn_b)


AB = ATTN_BLOCK


def _attn_bias_np():
    slopes = (np.float32(2.0) ** (np.float32(-8.0) * np.arange(1, N_HEADS + 1, dtype=np.float32)
                                  / np.float32(N_HEADS))).astype(np.float32)
    steps = (np.arange(AB)[:, None] + AB) - np.arange(2 * AB)[None, :]
    own = (np.arange(2 * AB) >= AB)[None, :]
    out = []
    for window, dil in GROUPS:
        valid = (steps >= 0) & (steps <= window // dil)
        dist = slopes[:, None, None] * (steps * dil).astype(np.float32)[None]
        kinds = [np.where(v[None], dist, np.float32(MASK_BIAS)) for v in (valid, valid & own)]
        out.append(np.stack(kinds, axis=1))
    return np.stack(out).astype(np.float32)


def _attn_bias():
    return jnp.asarray(_attn_bias_np())


def _head_masks():
    lane = lax.broadcasted_iota(jnp.int32, (1, 128), 1)
    return (lane < HEAD_DIM, lane >= HEAD_DIM)


def _perm_chunks(S, d):
    L = S // d
    ch = min(L, 256)
    out = []
    for r in range(d):
        for c in range(L // ch):
            start = r + d * ch * c
            out.append((pl.ds(start, ch, stride=d) if d > 1 else pl.ds(start, ch), r * L + c * ch, ch))
    return out


def _stack_heads(x, masks):
    return jnp.concatenate([jnp.where(masks[0], x, 0), jnp.where(masks[1], x, 0)], axis=0)


def _block_row(j):
    return j * AB if isinstance(j, int) else pl.multiple_of(j * AB, AB)


def _three_stages(n, stage_a, stage_b, stage_c, unroll):
    stage_a(0)
    stage_a(1)
    stage_b(0)

    def body(j, carry):
        stage_c(j - 1)
        stage_b(j)
        stage_a(j + 1)
        return carry

    lax.fori_loop(1, n - 1, body, 0, unroll=unroll)
    stage_c(n - 2)
    stage_b(n - 1)
    stage_c(n - 1)


_NT = (((1,), (1,)), ((), ()))
_TN = (((0,), (0,)), ((), ()))
SCH = 128


def _attn_fwd(qn, kn, z8, bias, S):
    T = qn.shape[0]
    nb = T // S
    nblk = S // AB

    def body(q_ref, k_ref, v_ref, bias_ref, o_ref, ob_ref, lse_ref, qs, ks, vs, s2, p2, ogp, lgp, *group_scratch):
        og, lg = group_scratch[:3], group_scratch[3:]
        masks = _head_masks()
        ks[0:AB, :] = jnp.zeros((AB, 128), bf16)
        vs[0:AB, :] = jnp.zeros((AB, 128), bf16)

        for g, (_, d) in enumerate(GROUPS):
            nsub = S // (d * AB)
            chunks = _perm_chunks(S, d)
            for src, dst, ch in chunks:
                qs[dst:dst + ch, :] = q_ref[src, :].astype(bf16)
                ks[AB + dst:AB + dst + ch, :] = k_ref[src, :].astype(bf16)
                vs[AB + dst:AB + dst + ch, :] = v_ref[src, :].astype(bf16)
            od, ld = (og[g], lg[g]) if d == 1 else (ogp, lgp)

            def scores(j):
                r0 = _block_row(j)
                q2 = _stack_heads(qs[pl.ds(r0, AB), :], masks)
                s2[j] = lax.dot_general(q2, ks[pl.ds(r0, 2 * AB), :], _NT, preferred_element_type=f32)

            def softmax(j, g=g, nsub=nsub, ld=ld):
                r0 = _block_row(j)
                kind = int(j % nsub == 0) if isinstance(j, int) else (j % nsub == 0).astype(jnp.int32)
                for cc in range(AB // SCH):
                    lses = []
                    for hh in range(2):
                        rows = pl.ds(hh * AB + cc * SCH, SCH)
                        sb = s2[j, rows, :] - bias_ref[g, hh, kind, cc * SCH:(cc + 1) * SCH, :]
                        m = jnp.max(sb, axis=-1, keepdims=True)
                        p = jnp.exp(sb - m)
                        den = jnp.sum(p, axis=-1, keepdims=True)
                        p2[j, rows, :] = (p * (1.0 / den)).astype(bf16)
                        lses.append(m + jnp.log(den))
                    ld[pl.ds(r0 + cc * SCH, SCH), :] = jnp.where(masks[0], lses[0], lses[1])

            def values(j, od=od):
                r0 = _block_row(j)
                pv2 = jnp.dot(p2[j], vs[pl.ds(r0, 2 * AB), :], preferred_element_type=f32)
                od[pl.ds(r0, AB), :] = jnp.where(masks[0], pv2[:AB], pv2[AB:])

            _three_stages(nblk, scores, softmax, values, nblk - 2)

            if d > 1:
                for src, dst, ch in chunks:
                    og[g][src, :] = ogp[dst:dst + ch, :]
                    lg[g][src, :] = lgp[dst:dst + ch, :]

        def combine(i, carry):
            rr = pl.ds(pl.multiple_of(i * 256, 256), 256)
            l0, l1, l2 = lg[0][rr, :], lg[1][rr, :], lg[2][rr, :]
            mx = jnp.maximum(jnp.maximum(l0, l1), l2)
            e0, e1, e2 = jnp.exp(l0 - mx), jnp.exp(l1 - mx), jnp.exp(l2 - mx)
            den = e0 + e1 + e2
            o = (e0 * og[0][rr, :] + e1 * og[1][rr, :] + e2 * og[2][rr, :]) / den
            o_ref[rr, :] = o
            ob_ref[rr, :] = o.astype(bf16)
            lse_ref[rr, :] = mx + jnp.log(den)
            return carry

        lax.fori_loop(0, S // 256, combine, 0, unroll=True)

    blk = pl.BlockSpec((S, 128), lambda b, hp: (b, hp))
    return pl.pallas_call(
        body, name="attn_fwd", grid=(nb, N_HEADS // 2),
        in_specs=[blk, blk, pl.BlockSpec((None, S, 128), lambda b, hp: (Z_V, b, hp)),
                  pl.BlockSpec((3, 2, 2, AB, 2 * AB), lambda b, hp: (0, hp, 0, 0, 0))],
        out_specs=[blk, blk, blk],
        out_shape=[jax.ShapeDtypeStruct((T, D), f32), jax.ShapeDtypeStruct((T, D), bf16),
                   jax.ShapeDtypeStruct((T, D), f32)],
        scratch_shapes=[pltpu.VMEM((S, 128), bf16), pltpu.VMEM((S + AB, 128), bf16), pltpu.VMEM((S + AB, 128), bf16),
                        pltpu.VMEM((nblk, 2 * AB, 2 * AB), f32), pltpu.VMEM((nblk, 2 * AB, 2 * AB), bf16),
                        pltpu.VMEM((S, 128), f32), pltpu.VMEM((S, 128), f32)] + [pltpu.VMEM((S, 128), f32)] * 6,
        compiler_params=_cparams(("parallel", "parallel")))(qn, kn, z8, bias)


def _attn_bwd(qn, kn, z8, do, o, lse, bias, bd, qg, kg, dz8, S):
    T = qn.shape[0]
    nb = T // S

    nblk = S // AB

    def body(q_ref, k_ref, v_ref, do_ref, o_ref, lse_ref, bias_ref, bd_ref, qraw_ref, kraw_ref, qg_ref, kg_ref,
             dz_in, dz_ref, dqg_ref, dkg_ref,
             dq_ref, dk_ref, dv_ref, delta, qs, ks, vs, dos, lsp, dlp, s2, dp2, p2, ds2, dqp, dkp, dvp):
        del dz_in
        masks = _head_masks()
        bdv = bd_ref[...]
        dq_ref[...] = jnp.zeros_like(dq_ref)
        dk_ref[...] = jnp.zeros_like(dk_ref)
        dv_ref[...] = jnp.zeros_like(dv_ref)
        ks[0:AB, :] = jnp.zeros((AB, 128), bf16)
        vs[0:AB, :] = jnp.zeros((AB, 128), bf16)

        def prep(i, carry):
            rr = pl.ds(pl.multiple_of(i * 256, 256), 256)
            delta[rr, :] = _head_sum(do_ref[rr, :] * o_ref[rr, :], bdv)
            return carry

        lax.fori_loop(0, S // 256, prep, 0, unroll=True)

        for g, (_, d) in enumerate(GROUPS):
            nsub = S // (d * AB)
            chunks = _perm_chunks(S, d)
            for src, dst, ch in chunks:
                qs[dst:dst + ch, :] = q_ref[src, :].astype(bf16)
                ks[AB + dst:AB + dst + ch, :] = k_ref[src, :].astype(bf16)
                vs[AB + dst:AB + dst + ch, :] = v_ref[src, :].astype(bf16)
                dos[dst:dst + ch, :] = do_ref[src, :].astype(bf16)
                lsp[dst:dst + ch, :] = lse_ref[src, :]
                dlp[dst:dst + ch, :] = delta[src, :]
            dkp[...] = jnp.zeros_like(dkp)
            dvp[...] = jnp.zeros_like(dvp)

            def scores(j):
                r0 = _block_row(j)
                q2 = _stack_heads(qs[pl.ds(r0, AB), :], masks)
                do2 = _stack_heads(dos[pl.ds(r0, AB), :], masks)
                s2[j] = lax.dot_general(q2, ks[pl.ds(r0, 2 * AB), :], _NT, preferred_element_type=f32)
                dp2[j] = lax.dot_general(do2, vs[pl.ds(r0, 2 * AB), :], _NT, preferred_element_type=f32)

            def probs(j, g=g, nsub=nsub):
                r0 = _block_row(j)
                kind = int(j % nsub == 0) if isinstance(j, int) else (j % nsub == 0).astype(jnp.int32)
                for cc in range(AB // SCH):
                    lse_c = lsp[pl.ds(r0 + cc * SCH, SCH), :]
                    del_c = dlp[pl.ds(r0 + cc * SCH, SCH), :]
                    for hh in range(2):
                        c0 = hh * HEAD_DIM
                        rows = pl.ds(hh * AB + cc * SCH, SCH)
                        sb = s2[j, rows, :] - bias_ref[g, hh, kind, cc * SCH:(cc + 1) * SCH, :]
                        p = jnp.exp(sb - lse_c[:, c0:c0 + 1])
                        p2[j, rows, :] = p.astype(bf16)
                        ds2[j, rows, :] = (p * (dp2[j, rows, :] - del_c[:, c0:c0 + 1])).astype(bf16)

            def grads(j):
                r0 = _block_row(j)
                q2 = _stack_heads(qs[pl.ds(r0, AB), :], masks)
                do2 = _stack_heads(dos[pl.ds(r0, AB), :], masks)
                dsb = ds2[j]
                t = jnp.dot(dsb, ks[pl.ds(r0, 2 * AB), :], preferred_element_type=f32)
                dqp[pl.ds(r0, AB), :] = jnp.where(masks[0], t[:AB], t[AB:])
                dkp[pl.ds(r0, 2 * AB), :] += lax.dot_general(dsb, q2, _TN, preferred_element_type=f32)
                dvp[pl.ds(r0, 2 * AB), :] += lax.dot_general(p2[j], do2, _TN, preferred_element_type=f32)

            _three_stages(nblk, scores, probs, grads, nblk - 2)

            for src, dst, ch in chunks:
                dq_ref[src, :] += dqp[dst:dst + ch, :]
                dk_ref[src, :] += dkp[AB + dst:AB + dst + ch, :]
                dv_ref[src, :] += dvp[AB + dst:AB + dst + ch, :]

        @pl.when(pl.program_id(1) == 0)
        def _():
            dqg_ref[...] = jnp.zeros_like(dqg_ref)
            dkg_ref[...] = jnp.zeros_like(dkg_ref)

        def norms(i, carry):
            rr = pl.ds(pl.multiple_of(i * 256, 256), 256)

            def one(raw, dn_scaled, g, dg_ref, sec):
                rstd = lax.rsqrt(_head_sum(raw * raw, bdv) * (1.0 / HEAD_DIM) + EPS)
                n = raw * rstd
                dg_ref[...] += _colsum8(dn_scaled * n)
                dn = dn_scaled * g
                draw = rstd * (dn - n * (_head_sum(dn * n, bdv) * (1.0 / HEAD_DIM)))
                dz_ref[sec, rr, :] = draw.astype(bf16)

            one(qraw_ref[rr, :], dq_ref[rr, :] * (HEAD_DIM ** -0.5), qg_ref[...], dqg_ref, 0)
            one(kraw_ref[rr, :], dk_ref[rr, :], kg_ref[...], dkg_ref, 1)
            dz_ref[2, rr, :] = dv_ref[rr, :].astype(bf16)
            dz_ref[3, rr, :] = jnp.zeros((256, 128), bf16)
            return carry

        lax.fori_loop(0, S // 256, norms, 0, unroll=True)

    blk = pl.BlockSpec((S, 128), lambda hp, b: (b, hp))
    sec = lambda s: pl.BlockSpec((None, S, 128), lambda hp, b: (s, b, hp))
    gain = pl.BlockSpec((1, 128), lambda hp, b: (0, hp))
    row = lambda dt, pad=0: pltpu.VMEM((S + pad, 128), dt)
    blocks = lambda dt: pltpu.VMEM((nblk, 2 * AB, 2 * AB), dt)
    return pl.pallas_call(
        body, name="attn_bwd", grid=(N_HEADS // 2, nb),
        in_specs=[blk, blk, sec(Z_V), blk, blk, blk,
                  pl.BlockSpec((3, 2, 2, AB, 2 * AB), lambda hp, b: (0, hp, 0, 0, 0)),
                  pl.BlockSpec((128, 128), lambda hp, b: (0, 0)), sec(Z_Q), sec(Z_K), gain, gain,
                  pl.BlockSpec(memory_space=pl.ANY)],
        out_specs=[pl.BlockSpec((4, S, 128), lambda hp, b: (1, b, hp)),
                   pl.BlockSpec((8, 128), lambda hp, b: (0, hp)), pl.BlockSpec((8, 128), lambda hp, b: (0, hp))],
        out_shape=[jax.ShapeDtypeStruct(dz8.shape, bf16), jax.ShapeDtypeStruct((8, D), f32),
                   jax.ShapeDtypeStruct((8, D), f32)],
        input_output_aliases={12: 0},
        scratch_shapes=[row(f32), row(f32), row(f32),
                        row(f32), row(bf16), row(bf16, AB), row(bf16, AB), row(bf16), row(f32), row(f32),
                        blocks(f32), blocks(f32), blocks(bf16), blocks(bf16), row(f32), row(f32, AB), row(f32, AB)],
        compiler_params=_cparams(("parallel", "arbitrary")))(qn, kn, z8, do, o, lse, bias, bd, z8, z8, qg, kg, dz8)


def _any_spec():
    return pl.BlockSpec(memory_space=pl.ANY)


def _allgather_rows(shards, n_full):
    n = len(shards)

    def body(*refs):
        ins, outs = refs[:n], refs[n:2 * n]
        send_sems, recv_sems, local_sems = refs[2 * n:]
        x, y, c, me = _my_pos()
        sibling = (x, y, 1 - c)
        chips = [(1 - x, y), (x, 1 - y), (1 - x, 1 - y)]

        def idx(px, py, pc):
            return 4 * px + 2 * py + pc

        def copy(a, k, blk, to, src=None):
            return pltpu.make_async_remote_copy(
                src_ref=outs[a].at[blk] if src is None else src, dst_ref=outs[a].at[blk],
                send_sem=send_sems.at[a, k], recv_sem=recv_sems.at[a, k], device_id=to, device_id_type=MESH)

        mine = [pltpu.make_async_copy(ins[a], outs[a].at[me], local_sems.at[a]) for a in range(n)]
        for cp in mine:
            cp.start()
        first = []
        for a in range(n_full):
            first.append(copy(a, 0, me, sibling, src=ins[a]))
            first += [copy(a, 1 + j, me, (*chip, c), src=ins[a]) for j, chip in enumerate(chips)]
        for cp in first:
            cp.start()
        passed = []
        for a in range(n_full):
            for j, chip in enumerate(chips):
                blk = idx(*chip, c)
                copy(a, 1 + j, blk, (x, y, c)).wait_recv()
                cp = copy(a, 4 + j, blk, sibling)
                cp.start()
                passed.append(cp)
        for a in range(n_full):
            copy(a, 0, idx(x, y, 1 - c), (x, y, c)).wait_recv()
            for j, chip in enumerate(chips):
                copy(a, 4 + j, idx(*chip, 1 - c), (x, y, c)).wait_recv()
        for cp in first + passed:
            cp.wait_send()
        for cp in mine:
            cp.wait()

    return pl.pallas_call(
        body, name="allgather_weights",
        in_specs=[_any_spec()] * n, out_specs=[_any_spec()] * n,
        out_shape=[jax.ShapeDtypeStruct((N_DEV,) + s.shape, s.dtype) for s in shards],
        scratch_shapes=[pltpu.SemaphoreType.DMA((n_full, 7)), pltpu.SemaphoreType.DMA((n_full, 7)),
                        pltpu.SemaphoreType.DMA((n,))],
    )(*shards)


def _peer(x, y, c, k):
    tx = 1 - x if (k >> 2) & 1 else x
    ty = 1 - y if (k >> 1) & 1 else y
    tc = 1 - c if k & 1 else c
    return (tx, ty, tc), 4 * tx + 2 * ty + tc


_PEER_ORDER = (2, 4, 6, 3, 5, 7, 1)


_HBM = pl.BlockSpec(memory_space=pltpu.HBM)
_SEM = pl.BlockSpec(memory_space=pltpu.SEMAPHORE)
_EFFECT = pltpu.SideEffectType.DATAFLOW_SIDE_EFFECTING


def _exchange_copies(srcs, lands, send_sems, recv_sems, gather):
    x, y, c, me = _my_pos()
    copies = []
    for k in _PEER_ORDER:
        tgt, tidx = _peer(x, y, c, k)
        for a in range(len(srcs)):
            copies.append(pltpu.make_async_remote_copy(
                src_ref=srcs[a] if gather else srcs[a].at[tidx], dst_ref=lands[a].at[me],
                send_sem=send_sems.at[7 * a + k - 1], recv_sem=recv_sems.at[7 * a + k - 1],
                device_id=tgt, device_id_type=MESH))
    return copies


def _exchange_start(name, srcs, lands=None, after=None):
    n = len(srcs)
    gather = lands is not None
    if lands is None:
        lands = [lax.empty(g.shape, g.dtype) for g in srcs]
    extra = [] if after is None else [after]

    def body(*refs):
        src_refs, land_refs = refs[:n], refs[n:2 * n]
        send_sems, recv_sems = refs[2 * n + len(extra)], refs[2 * n + len(extra) + 1]
        token = refs[-1]
        for cp in _exchange_copies(src_refs, land_refs, send_sems, recv_sems, gather):
            cp.start()
        token[...] = jnp.zeros_like(token)

    hbm = lambda a: pltpu.with_memory_space_constraint(a, pltpu.HBM)
    outs = pl.pallas_call(
        body, name=name,
        out_shape=(pltpu.SemaphoreType.DMA((7 * n,)), pltpu.SemaphoreType.DMA((7 * n,)),
                   *[pltpu.HBM(g.shape, g.dtype) for g in list(srcs) + list(lands)],
                   jax.ShapeDtypeStruct((8, 128), f32)),
        in_specs=[_HBM] * (2 * n) + [pl.BlockSpec(memory_space=pl.ANY)] * len(extra),
        out_specs=(_SEM, _SEM, *([_HBM] * (2 * n)), pl.BlockSpec(memory_space=pltpu.VMEM)),
        input_output_aliases={i: 2 + i for i in range(2 * n)},
        compiler_params=pltpu.CompilerParams(has_side_effects=_EFFECT),
    )(*[hbm(g) for g in srcs], *[hbm(g) for g in lands], *extra)
    return outs[0], outs[1], list(outs[2:2 + n]), list(outs[2 + n:2 + 2 * n]), outs[-1], gather


def _exchange_wait(name, started, after):
    send_sems, recv_sems, srcs, lands, _, gather = started
    n = len(srcs)
    after = list(after) if isinstance(after, (list, tuple)) else [after]

    def body(*refs):
        src_refs, land_refs = refs[:n], refs[n:2 * n]
        s_sems, r_sems = refs[2 * n], refs[2 * n + 1]
        for cp in _exchange_copies(src_refs, land_refs, s_sems, r_sems, gather):
            cp.wait_send()
            cp.wait_recv()

    outs = pl.pallas_call(
        body, name=name,
        out_shape=tuple(pltpu.HBM(a.shape, a.dtype) for a in list(srcs) + list(lands)),
        in_specs=[_HBM] * (2 * n) + [_SEM, _SEM] + [pl.BlockSpec(memory_space=pl.ANY)] * len(after),
        out_specs=tuple([_HBM] * (2 * n)),
        input_output_aliases={i: i for i in range(2 * n)},
        compiler_params=pltpu.CompilerParams(has_side_effects=_EFFECT),
    )(*srcs, *lands, send_sems, recv_sems, *after)
    return list(outs[:n]), list(outs[n:])


SMALL_ROWS = 128


def _small_start(name, sg, after=None):
    return _exchange_start(name, [sg], [lax.empty((N_DEV,) + sg.shape, f32)], after=after)


def _small_sum(name, me, started, after):
    (own,), (slots,) = _exchange_wait(name + "_wait", started, after)

    def body(me_ref, s_ref, own_ref, out_ref):
        acc = None
        for p in range(N_DEV):
            term = lax.cond(me_ref[0] == p, lambda: own_ref[...], lambda p=p: s_ref[p])
            acc = term if acc is None else acc + term
        out_ref[...] = acc

    return pl.pallas_call(
        body, name=name + "_sum",
        in_specs=[pl.BlockSpec(memory_space=pltpu.SMEM), pl.BlockSpec(memory_space=pltpu.VMEM),
                  pl.BlockSpec(memory_space=pltpu.VMEM)],
        out_specs=pl.BlockSpec(memory_space=pltpu.VMEM),
        out_shape=jax.ShapeDtypeStruct(own.shape, f32))(me, slots, own)


def _adam_math(g, w, m, v):
    m = ADAM_B1 * m + (1.0 - ADAM_B1) * g
    v = ADAM_B2 * v + (1.0 - ADAM_B2) * (g * g)
    m_hat = m / (1.0 - ADAM_B1 ** ADAM_STEP)
    v_hat = v / (1.0 - ADAM_B2 ** ADAM_STEP)
    delta = -ADAM_LR * (m_hat / (jnp.sqrt(v_hat) + ADAM_EPS) + ADAM_WD * w)
    return delta, m, v


def _adam_slots(name, me, slots, own, w, m, v, tr, transposed=False):
    rows = slots.shape[1]

    def body(me_ref, s_ref, own_ref, w_ref, m_ref, v_ref, g_ref, d_ref, nm_ref, nv_ref):
        mine = own_ref[...]
        g = None
        for p in range(N_DEV):
            term = lax.cond(me_ref[0] == p, lambda: mine, lambda p=p: s_ref[p]).astype(f32)
            g = term if g is None else g + term
        if transposed:
            g = g.T
        delta, nm, nv = _adam_math(g, w_ref[...], m_ref[...], v_ref[...])
        g_ref[...] = g
        d_ref[...] = delta
        nm_ref[...] = nm
        nv_ref[...] = nv

    mode = dict(pipeline_mode=pl.Buffered(1)) if rows == tr else {}
    if transposed:
        rs = pl.BlockSpec((D, tr), lambda i, me_ref: (0, i))
        rs_in = pl.BlockSpec((D, tr), lambda i, me_ref: (0, i), **mode)
    else:
        rs = pl.BlockSpec((tr, D), lambda i, me_ref: (i, 0))
        rs_in = pl.BlockSpec((tr, D), lambda i, me_ref: (i, 0), **mode)
    return pl.pallas_call(
        body, name=name,
        grid_spec=pltpu.PrefetchScalarGridSpec(
            num_scalar_prefetch=1, grid=(rows // tr,),
            in_specs=[pl.BlockSpec((N_DEV, tr, D), lambda i, me_ref: (0, i, 0), **mode),
                      pl.BlockSpec((None, tr, D), lambda i, me_ref: (me_ref[0], i, 0), **mode), rs_in, rs_in, rs_in],
            out_specs=[rs] * 4),
        out_shape=[jax.ShapeDtypeStruct(w.shape, f32)] * 4,
        compiler_params=_cparams(("parallel",)))(me, slots, own, w, m, v)


def _copy_cols(src, row0, dst, t, c0, n):
    done = 0
    while done < n:
        r, c = divmod(c0 + done, D)
        take = min(n - done, D - c)
        dst[t:t + 1, done:done + take] = src[row0 + r:row0 + r + 1, c:c + take]
        done += take


def _adam_small(me, g, w, m, v):
    o = _small_offsets()
    ffn_cols = 2 * D_FF // N_DEV
    shapes = dict(norm1_g=(1, D), gate_b=(1, 2 * D), conv_w=(CONV_WIDTH, D // N_DEV), conv_b=(1, D),
                  conv_norm_g=(1, D), q_norm_g=(1, HEAD_DIM), k_norm_g=(1, HEAD_DIM), norm2_g=(1, D),
                  ffn_conv_w=(3, ffn_cols), ffn_conv_b=(1, 2 * D_FF))
    names = tuple(shapes)

    def body(me_ref, g_ref, w_ref, m_ref, v_ref, *refs):
        parts, (d_ref, nm_ref, nv_ref) = refs[:-3], refs[-3:]
        delta, nm, nv = _adam_math(g_ref[...], w_ref[...], m_ref[...], v_ref[...])
        d_ref[...] = delta
        nm_ref[...] = nm
        nv_ref[...] = nv
        mine = pl.ds(pl.multiple_of(me_ref[0] * (D // N_DEV), D // N_DEV), D // N_DEV)
        for i, src in enumerate((g_ref, d_ref, nm_ref, nv_ref)):
            for k, name in enumerate(names):
                dst, r = parts[i * len(names) + k], o[name]
                if name == "gate_b":
                    dst[:, 0:D] = src[r:r + 1, :]
                    dst[:, D:2 * D] = src[r + 1:r + 2, :]
                elif name == "conv_w":
                    dst[...] = src[r:r + CONV_WIDTH, mine]
                elif name == "ffn_conv_b":
                    _copy_cols(src, r, dst, 0, 0, 2 * D_FF)
                elif name == "ffn_conv_w":
                    for p in range(N_DEV):
                        @pl.when(me_ref[0] == p)
                        def _(p=p, src=src, dst=dst, r=r):
                            for t in range(3):
                                _copy_cols(src, r + t * (FFN_PAD // D), dst, t, p * ffn_cols, ffn_cols)
                else:
                    dst[...] = src[r:r + 1, 0:shapes[name][1]]

    vmem = pl.BlockSpec(memory_space=pltpu.VMEM)
    outs = pl.pallas_call(
        body, name="adam_small", in_specs=[pl.BlockSpec(memory_space=pltpu.SMEM)] + [vmem] * 4,
        out_shape=[jax.ShapeDtypeStruct(shapes[n], f32) for _ in range(4) for n in names],
        scratch_shapes=[pltpu.VMEM(g.shape, f32)] * 3)(me, g, w, m, v)
    return {n: [outs[i * len(names) + k] for i in range(4)] for k, n in enumerate(names)}


FFN_PAD = 6 * D


_SMALL_PARTS = (("norm1_g", 1), ("gate_b", 2), ("conv_w", CONV_WIDTH), ("conv_b", 1), ("conv_norm_g", 1),
                ("q_norm_g", 1), ("k_norm_g", 1), ("norm2_g", 1), ("ffn_conv_w", 18), ("ffn_conv_b", 6), ("last", 1))


def _small_offsets():
    out, row = {}, 0
    for name, rows in _SMALL_PARTS:
        out[name] = row
        row += -(-rows // 8) * 8
    assert row == SMALL_ROWS
    return out


def _pack_small(norm1_g, gate_b, conv_w, conv_b, conv_norm_g, q_norm_g, k_norm_g, norm2_g, ffn_conv_w, ffn_conv_b,
                last_row=None):
    pad_h = lambda a: jnp.pad(a, ((0, 0), (0, D - HEAD_DIM)))
    pad_f = lambda a: jnp.pad(a, ((0, 0), (0, FFN_PAD - 2 * D_FF))).reshape(-1, D)
    parts = [norm1_g, gate_b.reshape(2, D), conv_w, conv_b, conv_norm_g, pad_h(q_norm_g), pad_h(k_norm_g), norm2_g,
             pad_f(ffn_conv_w), pad_f(ffn_conv_b), jnp.zeros((1, D), f32) if last_row is None else last_row]
    return jnp.concatenate([jnp.pad(p, ((0, -p.shape[0] % 8), (0, 0))) for p in parts], axis=0)


def _unpack_small(p):
    o = _small_offsets()
    rows = lambda name, n: p[o[name]:o[name] + n]
    ffn = lambda a: a.reshape(-1, FFN_PAD)[:, :2 * D_FF]
    return dict(
        norm1_g=rows("norm1_g", 1), gate_b=rows("gate_b", 2).reshape(1, 2 * D), conv_w=rows("conv_w", CONV_WIDTH),
        conv_b=rows("conv_b", 1), conv_norm_g=rows("conv_norm_g", 1), q_norm_g=rows("q_norm_g", 1)[:, :HEAD_DIM],
        k_norm_g=rows("k_norm_g", 1)[:, :HEAD_DIM], norm2_g=rows("norm2_g", 1),
        ffn_conv_w=ffn(rows("ffn_conv_w", 18)), ffn_conv_b=ffn(rows("ffn_conv_b", 6)))


_ADAM_TILE = {896: 128, 704: 704, 128: 128, 352: 176}


def kernel(x, norm1_g, w_in, gate_b, conv_w, conv_b, conv_norm_g, w_conv_out, q_norm_g, k_norm_g, w_attn_out, w_out, norm2_g, w_up, ffn_conv_w, ffn_conv_b, w_down, loss_target, m_norm1_g, m_w_in, m_gate_b, m_conv_w, m_conv_b, m_conv_norm_g, m_w_conv_out, m_q_norm_g, m_k_norm_g, m_w_attn_out, m_w_out, m_norm2_g, m_w_up, m_ffn_conv_w, m_ffn_conv_b, m_w_down, v_norm1_g, v_w_in, v_gate_b, v_conv_w, v_conv_b, v_conv_norm_g, v_w_conv_out, v_q_norm_g, v_k_norm_g, v_w_attn_out, v_w_out, v_norm2_g, v_w_up, v_ffn_conv_w, v_ffn_conv_b, v_w_down):
    BL, S, _ = x.shape
    T = BL * S
    me = 4 * lax.axis_index("x") + 2 * lax.axis_index("y") + lax.axis_index("c")
    xt = x.reshape(T, D)
    target = loss_target.reshape(T, D)

    big = dict(w_in=(w_in[0], m_w_in[0], v_w_in[0]), w_up=(w_up[0], m_w_up[0], v_w_up[0]),
               w_conv_out=(w_conv_out[0], m_w_conv_out[0], v_w_conv_out[0]),
               w_attn_out=(w_attn_out[0], m_w_attn_out[0], v_w_attn_out[0]),
               w_out=(w_out[0], m_w_out[0], v_w_out[0]), w_down=(w_down[0], m_w_down[0], v_w_down[0]))
    order = ["w_in", "w_conv_out", "w_attn_out", "w_out", "w_up", "w_down"]
    shards = [(big[n][0].T if n in ("w_in", "w_up") else big[n][0]).astype(bf16) for n in order]
    gathered = _allgather_rows(shards, 1)
    W = {"w_in": gathered[0].reshape(-1, D)}

    def place_cols(shard, full_cols):
        z = jnp.zeros((shard.shape[0], full_cols), f32)
        return lax.dynamic_update_slice(z, shard, (0, me * shard.shape[1]))

    zr = lambda a: jnp.zeros_like(a)
    conv_local = _pack_small(
        zr(norm1_g), zr(gate_b), place_cols(conv_w[0], D), zr(conv_b), zr(conv_norm_g), zr(q_norm_g), zr(k_norm_g),
        zr(norm2_g), place_cols(ffn_conv_w[0], 2 * D_FF), zr(ffn_conv_b))
    ga_conv = _small_start("gather_conv_start", conv_local, after=gathered[0])
    ga_proj = _exchange_start("gather_start_proj", shards[1:4], gathered[1:4], after=ga_conv[4])
    ga_ffn = _exchange_start("gather_start_ffn", shards[4:6], gathered[4:6], after=ga_proj[4])

    bd = (jnp.arange(128)[:, None] // HEAD_DIM == jnp.arange(128)[None, :] // HEAD_DIM).astype(bf16)
    bias = _attn_bias()
    qg = jnp.tile(q_norm_g, (1, N_HEADS))
    kg = jnp.tile(k_norm_g, (1, N_HEADS))

    z8, h, qn, kn = _in_proj_fwd(xt, norm1_g, W["w_in"], qg, kg, bd, ga_ffn[4])
    conv_all = _unpack_small(_small_sum("gather_conv", me.reshape(1), ga_conv, z8))
    conv_w_full, ffn_w_full = conv_all["conv_w"], conv_all["ffn_conv_w"]
    c = _conv_fwd(z8, conv_w_full, conv_b, S)
    o, ob, lse = _attn_fwd(qn, kn, z8, bias, S)
    for n, g in zip(order[1:4], _exchange_wait("gather_wait_proj", ga_proj, ob)[1]):
        W[n] = g.reshape(-1, D)
    s, ya, yb, mixed = _branches_fwd(c, ob, z8, conv_norm_g, gate_b, W["w_conv_out"], W["w_attn_out"])
    x1, h2 = _out_norm2_fwd(mixed, W["w_out"], xt, norm2_g)
    for n, g in zip(order[4:6], _exchange_wait("gather_wait_ffn", ga_ffn, x1)[1]):
        W[n] = g.reshape(-1, D)
    TNU = D_FF // 2
    u3 = _matmul_call(
        "mm_u", h2, W["w_up"],
        pl.BlockSpec((1024, D), lambda i, j, k: (i, 0)),
        pl.BlockSpec((TNU, D), lambda i, j, k: (j, 0)),
        pl.BlockSpec((None, 1024, TNU), lambda i, j, k: (j // 2, i, j % 2)),
        jax.ShapeDtypeStruct((2, T, D_FF), f32), (T // 1024, 4, 1), "nt", 1, 1024, TNU)
    f = _ffn_fwd(u3, ffn_w_full, ffn_conv_b, S)
    dy, dyb, lacc = _down_loss_fwd(f, W["w_down"], x1, target)
    loss_local = 0.5 / D * jnp.sum(lacc)

    df = _matmul("mm_df", dyb, W["w_down"], "nt", f32, tn=TNU)
    g_w_down = _matmul("mm_dwdn", f, dyb, "tn", bf16, tm=TNU)
    du3, dffn = _ffn_bwd(u3, df, ffn_w_full, ffn_conv_b, S)
    g_w_up = _matmul_call(
        "mm_dwup", du3, h2,
        pl.BlockSpec((None, T, TNU), lambda i, j, k: (i // 2, 0, i % 2)),
        pl.BlockSpec((T, D), lambda i, j, k: (0, 0)),
        pl.BlockSpec((TNU, D), lambda i, j, k: (i, 0)),
        jax.ShapeDtypeStruct((2 * D_FF, D), bf16), (4, 1, 1), "tn", 1, TNU, D)
    blocks8 = lambda a: a.reshape(N_DEV, -1, D)
    ex_ffn = _exchange_start("scatter_start_ffn", [blocks8(g_w_up), blocks8(g_w_down)])
    dx1, dx1b, dg_norm2 = _up_norm2_bwd(du3, W["w_up"], x1, dy, norm2_g, ex_ffn[4])
    g_w_out = _matmul("mm_dwo", mixed, dx1b, "tn", bf16, tm=512)
    dz8 = lax.empty((8, T, D), bf16)
    dya, dyb2, dz8, dg_gate = _out_gate_bwd(dx1b, W["w_out"], z8, gate_b, ya, yb, dz8)
    g_w_conv_out = _matmul("mm_dwco", s, dya, "tn", bf16, tm=512)
    g_w_attn_out = _matmul("mm_dwao", ob, dyb2, "tn", bf16, tm=512)
    ex_proj = _exchange_start("scatter_start_proj", [blocks8(g_w_conv_out), blocks8(g_w_attn_out), blocks8(g_w_out)])
    do = _matmul("mm_do", dyb2, W["w_attn_out"], "nt", f32, after=ex_proj[4])
    dc, dg_convnorm = _convnorm_bwd(dya, W["w_conv_out"], c, conv_norm_g)
    dz8a, dconv = _conv_bwd(dc, z8, conv_w_full, dz8, S)
    dz8b, dg_q, dg_k = _attn_bwd(qn, kn, z8, do, o, lse, bias, bd, qg, kg, dz8a, S)
    g_w_in = _matmul_call(
        "mm_dwin", dz8b, h,
        pl.BlockSpec((None, T, D), lambda i, j, k: (jnp.where(i < 2, i, jnp.where(i < 5, i + 2, i - 3)), 0, 0)),
        pl.BlockSpec((T, D), lambda i, j, k: (0, 0)), pl.BlockSpec((1024, D), lambda i, j, k: (i, 0)),
        jax.ShapeDtypeStruct((7 * D, D), bf16), (7, 1, 1), "tn", 1, D, D)
    ex_in = _exchange_start("scatter_start_in", [blocks8(g_w_in)])
    grad_x, dg_norm1 = _in_norm1_bwd(dz8b, W["w_in"], xt, dx1, norm1_g, ex_in[4])

    sum8 = lambda a: a.reshape(-1, 8, a.shape[-1]).sum(axis=1)
    dconv_s = sum8(dconv.sum(axis=0))
    dffn_s = dffn.sum(axis=0).reshape(2, 4, 8, D_FF).sum(axis=2)
    dffn_w = jnp.concatenate([dffn_s[0, :3], dffn_s[1, :3]], axis=1)
    dffn_b = jnp.concatenate([dffn_s[0, 3:4], dffn_s[1, 3:4]], axis=1)
    fold = lambda a: sum8(a).reshape(N_HEADS, HEAD_DIM).sum(axis=0)[None]
    small_g_local = _pack_small(
        sum8(dg_norm1), sum8(dg_gate), dconv_s[:CONV_WIDTH], dconv_s[CONV_WIDTH:], sum8(dg_convnorm),
        fold(dg_q), fold(dg_k), sum8(dg_norm2), dffn_w, dffn_b,
        last_row=jnp.pad(loss_local.reshape(1, 1), ((0, 0), (0, D - 1))))
    sg_start = _small_start("small_grads_start", small_g_local)

    place_m = lambda a, full: place_cols(a[0], full)
    small_w_true = _pack_small(norm1_g, gate_b, conv_w_full, conv_b, conv_norm_g, q_norm_g, k_norm_g, norm2_g,
                               ffn_w_full, ffn_conv_b)
    small_m = _pack_small(m_norm1_g, m_gate_b, place_m(m_conv_w, D), m_conv_b, m_conv_norm_g, m_q_norm_g, m_k_norm_g,
                          m_norm2_g, place_m(m_ffn_conv_w, 2 * D_FF), m_ffn_conv_b)
    small_v = _pack_small(v_norm1_g, v_gate_b, place_m(v_conv_w, D), v_conv_b, v_conv_norm_g, v_q_norm_g, v_k_norm_g,
                          v_norm2_g, place_m(v_ffn_conv_w, 2 * D_FF), v_ffn_conv_b)

    own, slots = {}, {}
    for tag, ex, names_ in (("ffn", ex_ffn, ("w_up", "w_down")),
                            ("proj", ex_proj, ("w_conv_out", "w_attn_out", "w_out")), ("in", ex_in, ("w_in",))):
        sent, landed = _exchange_wait("scatter_wait_" + tag, ex, [sg_start[4], small_w_true, small_m, small_v])
        for n, src, land in zip(names_, sent, landed):
            own[n], slots[n] = src, land

    res, adam_done = {}, []
    for n in order:
        w, m, v = big[n]
        outs = _adam_slots("adam_" + n, me.reshape(1), slots[n], own[n], w, m, v, _ADAM_TILE[slots[n].shape[1]],
                           transposed=n in ("w_in", "w_up"))
        adam_done.append(outs[0])
        res[n] = [a[None] for a in outs]
    small_g = _small_sum("small_grads", me.reshape(1), sg_start, adam_done)
    loss = small_g[_small_offsets()["last"], 0]

    for n, four in _adam_small(me.reshape(1), small_g, small_w_true, small_m, small_v).items():
        res[n] = [a[None] for a in four] if n in ("conv_w", "ffn_conv_w") else four

    names = ["norm1_g", "w_in", "gate_b", "conv_w", "conv_b", "conv_norm_g", "w_conv_out", "q_norm_g", "k_norm_g",
             "w_attn_out", "w_out", "norm2_g", "w_up", "ffn_conv_w", "ffn_conv_b", "w_down"]
    out = [loss, grad_x.reshape(BL, S, D)]
    for i in range(4):
        out += [res[n][i] for n in names]
    return tuple(out)
```
